```python
import math
import jax
import jax.numpy as jnp
from jax import lax
import numpy as np

D_MODEL = 1024
BATCH = 8
SEQ = 2048
DEPTH = 2

GRID_W = 64
CTX_LEN = 256
N_MIXERS = 2
N_MLA_LAYERS = (DEPTH + N_MIXERS - 1) // N_MIXERS
N_S5_LAYERS = DEPTH // N_MIXERS
EPS = 1e-6

MLA_HEADS = 16
QK_NOPE_DIM = 64
QK_ROPE_DIM = 32
V_HEAD_DIM = 64
Q_LORA_RANK = 256
KV_LORA_RANK = 128
MLA_WIDTH = MLA_HEADS * V_HEAD_DIM
QK_DIM = QK_NOPE_DIM + QK_ROPE_DIM
SOFTMAX_SCALE = QK_DIM ** -0.5
ROPE_THETA = 10000.0
Q_BLOCK = 128
MLA_IN_WIDTH = Q_LORA_RANK + KV_LORA_RANK + QK_ROPE_DIM + MLA_WIDTH

S5_WIDTH = D_MODEL
S5_GROUP = 16
S5_GROUPS = S5_WIDTH // S5_GROUP
S5_STATE = 64
DT_MIN = 0.001
DT_MAX = 0.1

kernel_name = "hybrid_mla_s5_context_prefix_dit"


def rmsnorm(x, g):
    xf = x.astype(jnp.float32)
    y = xf * lax.rsqrt(jnp.mean(xf * xf, axis=-1, keepdims=True) + EPS)
    return (y * g.astype(jnp.float32)).astype(x.dtype)


def grid_positions(L):
    rows = L // GRID_W
    row = jnp.repeat(jnp.arange(rows, dtype=jnp.int32), GRID_W)
    col = jnp.tile(jnp.arange(GRID_W, dtype=jnp.int32), rows)
    return row, col


def rope_1d(x, pos):
    d = x.shape[-1]
    inv = 1.0 / (ROPE_THETA ** (jnp.arange(0, d, 2, dtype=jnp.float32) / d))
    ang = pos.astype(jnp.float32)[:, None] * inv[None, :]
    cos = jnp.cos(ang)[:, None, :].astype(x.dtype)
    sin = jnp.sin(ang)[:, None, :].astype(x.dtype)
    x1, x2 = x[..., : d // 2], x[..., d // 2:]
    return jnp.concatenate([x1 * cos - x2 * sin, x1 * sin + x2 * cos], axis=-1)


def axial_rope(x, row, col):
    h = x.shape[-1] // 2
    return jnp.concatenate([rope_1d(x[..., :h], row), rope_1d(x[..., h:], col)], axis=-1)


def mla_project(h, w_in, q_norm, w_uq, kv_norm, w_ukv):
    B_, L, _ = h.shape
    p = h @ w_in
    o1 = Q_LORA_RANK
    o2 = o1 + KV_LORA_RANK
    o3 = o2 + QK_ROPE_DIM
    cq, ckv, kr, z = p[..., :o1], p[..., o1:o2], p[..., o2:o3], p[..., o3:]
    q = (rmsnorm(cq, q_norm) @ w_uq).reshape(B_, L, MLA_HEADS, QK_DIM)
    kv = (rmsnorm(ckv, kv_norm) @ w_ukv).reshape(B_, L, MLA_HEADS, QK_NOPE_DIM + V_HEAD_DIM)
    q_nope, q_rope = q[..., :QK_NOPE_DIM], q[..., QK_NOPE_DIM:]
    k_nope, v = kv[..., :QK_NOPE_DIM], kv[..., QK_NOPE_DIM:]
    return q_nope, q_rope, k_nope, kr[:, :, None, :], v, z


def mla_keys(k_nope, kr):
    kr_b = jnp.broadcast_to(kr, k_nope.shape[:-1] + (QK_ROPE_DIM,))
    return jnp.concatenate([k_nope, kr_b], axis=-1)


def attend(q, k, v):
    s = jnp.einsum('bqhd,bkhd->bhqk', q.astype(jnp.float32), k.astype(jnp.float32)) * SOFTMAX_SCALE
    p = jax.nn.softmax(s, axis=-1)
    return jnp.einsum('bhqk,bkhd->bqhd', p, v.astype(jnp.float32)).astype(v.dtype)


def mla_mixer(h_lat, h_ctx, need_ctx, w_in, q_norm, w_uq, kv_norm, w_ukv, w_out):
    B_, L, _ = h_lat.shape
    Lc = h_ctx.shape[1]
    row, col = grid_positions(L)
    qn_l, qr_l, kn_l, kr_l, v_l, z_l = mla_project(h_lat, w_in, q_norm, w_uq, kv_norm, w_ukv)
    qn_c, qr_c, kn_c, kr_c, v_c, z_c = mla_project(h_ctx, w_in, q_norm, w_uq, kv_norm, w_ukv)
    q_lat = jnp.concatenate([qn_l, axial_rope(qr_l, row, col)], axis=-1)
    k_lat = mla_keys(kn_l, axial_rope(kr_l, row, col))
    k_ctx = mla_keys(kn_c, kr_c)
    k_all = jnp.concatenate([k_ctx, k_lat], axis=1)
    v_all = jnp.concatenate([v_c, v_l], axis=1)
    nb = L // Q_BLOCK
    qb = jnp.transpose(q_lat.reshape(B_, nb, Q_BLOCK, MLA_HEADS, QK_DIM), (1, 0, 2, 3, 4))
    ob = lax.map(lambda qq: attend(qq, k_all, v_all), qb)
    o_lat = jnp.transpose(ob, (1, 0, 2, 3, 4)).reshape(B_, L, MLA_WIDTH)
    out_lat = (o_lat * jax.nn.silu(z_l)) @ w_out
    out_ctx = None
    if need_ctx:
        q_ctx = jnp.concatenate([qn_c, qr_c], axis=-1)
        o_ctx = attend(q_ctx, k_ctx, v_c).reshape(B_, Lc, MLA_WIDTH)
        out_ctx = (o_ctx * jax.nn.silu(z_c)) @ w_out
    return out_lat, out_ctx


def s5_discretise(a_re, a_im, log_step, b_re, b_im):
    dt = jnp.exp(log_step.astype(jnp.float32))[:, None]
    ar = a_re.astype(jnp.float32)
    ai = a_im.astype(jnp.float32)
    mag = jnp.exp(ar * dt)
    lb_re = mag * jnp.cos(ai * dt)
    lb_im = mag * jnp.sin(ai * dt)
    den = ar * ar + ai * ai
    nr = lb_re - 1.0
    f_re = ((nr * ar + lb_im * ai) / den)[..., None]
    f_im = ((lb_im * ar - nr * ai) / den)[..., None]
    br = b_re.astype(jnp.float32)
    bi = b_im.astype(jnp.float32)
    bb_re = f_re * br - f_im * bi
    bb_im = f_re * bi + f_im * br
    return lb_re, lb_im, bb_re, bb_im


def linear_recurrence_combine(e1, e2):
    a1r, a1i, b1r, b1i = e1
    a2r, a2i, b2r, b2i = e2
    return (a2r * a1r - a2i * a1i,
            a2r * a1i + a2i * a1r,
            a2r * b1r - a2i * b1i + b2r,
            a2r * b1i + a2i * b1r + b2i)


def s5_scan(u, disc, c_re, c_im, s0, reverse):
    lb_re, lb_im, bb_re, bb_im = disc
    L = u.shape[1]
    bu_re = jnp.einsum('blgc,gpc->blgp', u, bb_re)
    bu_im = jnp.einsum('blgc,gpc->blgp', u, bb_im)
    if s0 is not None:
        s0r, s0i = s0
        idx = L - 1 if reverse else 0
        bu_re = bu_re.at[:, idx].add(lb_re * s0r - lb_im * s0i)
        bu_im = bu_im.at[:, idx].add(lb_re * s0i + lb_im * s0r)
    a_re = jnp.broadcast_to(lb_re, (1, L) + lb_re.shape)
    a_im = jnp.broadcast_to(lb_im, (1, L) + lb_im.shape)
    _, _, s_re, s_im = lax.associative_scan(
        linear_recurrence_combine, (a_re, a_im, bu_re, bu_im), reverse=reverse, axis=1)
    y = (jnp.einsum('blgp,gcp->blgc', s_re, c_re.astype(jnp.float32))
         - jnp.einsum('blgp,gcp->blgc', s_im, c_im.astype(jnp.float32)))
    fin = (s_re[:, 0], s_im[:, 0]) if reverse else (s_re[:, -1], s_im[:, -1])
    return y, fin


def s5_finish(y_ssm, u, z, d, w_glu, b_glu, w_out):
    B_, L, _ = u.shape
    y = y_ssm.reshape(B_, L, S5_WIDTH) + d.astype(jnp.float32) * u.astype(jnp.float32)
    y = jax.nn.gelu(y).astype(u.dtype)
    y = y * jax.nn.sigmoid(y @ w_glu + b_glu)
    return (y * jax.nn.silu(z)) @ w_out


def s5_mixer(h_lat, h_ctx, need_ctx, w_in, a_re, a_im, log_step, b_re, b_im, c_re, c_im,
             d, w_glu, b_glu, w_out):
    B_, L, _ = h_lat.shape
    Lc = h_ctx.shape[1]
    p_l = h_lat @ w_in
    p_c = h_ctx @ w_in
    u_l, z_l = p_l[..., :S5_WIDTH], p_l[..., S5_WIDTH:]
    u_c, z_c = p_c[..., :S5_WIDTH], p_c[..., S5_WIDTH:]
    g_l = u_l.astype(jnp.float32).reshape(B_, L, S5_GROUPS, S5_GROUP)
    g_c = u_c.astype(jnp.float32).reshape(B_, Lc, S5_GROUPS, S5_GROUP)
    y_l = jnp.zeros_like(g_l)
    y_c = jnp.zeros_like(g_c)
    for k, rev in enumerate((False, True)):
        disc = s5_discretise(a_re[k], a_im[k], log_step[k], b_re[k], b_im[k])
        yc_k, s_fin = s5_scan(g_c, disc, c_re[k], c_im[k], None, rev)
        yl_k, _ = s5_scan(g_l, disc, c_re[k], c_im[k], s_fin, rev)
        y_l = y_l + yl_k
        y_c = y_c + yc_k
    out_lat = s5_finish(y_l, u_l, z_l, d, w_glu, b_glu, w_out)
    out_ctx = s5_finish(y_c, u_c, z_c, d, w_glu, b_glu, w_out) if need_ctx else None
    return out_lat, out_ctx


def _fwd_setup_inputs(seed: int = 0) -> dict:
    key = jax.random.key(seed)
    ks = jax.random.split(key, 32)

    def nrm(k, shape, scale):
        return jax.random.normal(k, shape, jnp.float32) * scale

    D, E = D_MODEL, MLA_WIDTH
    G, P, CH = S5_GROUPS, S5_STATE, S5_GROUP
    a_im_base = jnp.pi * jnp.arange(P, dtype=jnp.float32)
    return {
        'x': nrm(ks[0], (BATCH, SEQ, D), 1.0),
        'c': nrm(ks[1], (BATCH, D), 1.0),
        'ctx': nrm(ks[2], (BATCH, CTX_LEN, D), 1.0),
        'c_ctx': nrm(ks[3], (D,), 1.0),
        'ada_w': nrm(ks[4], (DEPTH, D, 3 * D), 0.5 * D ** -0.5),
        'ada_b': nrm(ks[5], (DEPTH, 3 * D), 0.01),
        'norm_g': 1.0 + nrm(ks[6], (DEPTH, D), 0.01),
        'mla_w_in': nrm(ks[7], (N_MLA_LAYERS, D, MLA_IN_WIDTH), D ** -0.5),
        'mla_q_norm': 1.0 + nrm(ks[8], (N_MLA_LAYERS, Q_LORA_RANK), 0.01),
        'mla_w_uq': nrm(ks[9], (N_MLA_LAYERS, Q_LORA_RANK, MLA_HEADS * QK_DIM), Q_LORA_RANK ** -0.5),
        'mla_kv_norm': 1.0 + nrm(ks[10], (N_MLA_LAYERS, KV_LORA_RANK), 0.01),
        'mla_w_ukv': nrm(ks[11], (N_MLA_LAYERS, KV_LORA_RANK, MLA_HEADS * (QK_NOPE_DIM + V_HEAD_DIM)),
                         KV_LORA_RANK ** -0.5),
        'mla_w_out': nrm(ks[12], (N_MLA_LAYERS, E, D), E ** -0.5),
        's5_w_in': nrm(ks[13], (N_S5_LAYERS, D, 2 * S5_WIDTH), D ** -0.5),
        's5_a_re': -0.5 + nrm(ks[14], (N_S5_LAYERS, 2, G, P), 0.01),
        's5_a_im': a_im_base + nrm(ks[15], (N_S5_LAYERS, 2, G, P), 0.01),
        's5_log_step': jax.random.uniform(ks[16], (N_S5_LAYERS, 2, G), jnp.float32,
                                          math.log(DT_MIN), math.log(DT_MAX)),
        's5_b_re': nrm(ks[17], (N_S5_LAYERS, 2, G, P, CH), (2 * CH) ** -0.5),
        's5_b_im': nrm(ks[18], (N_S5_LAYERS, 2, G, P, CH), (2 * CH) ** -0.5),
        's5_c_re': nrm(ks[19], (N_S5_LAYERS, 2, G, CH, P), P ** -0.5),
        's5_c_im': nrm(ks[20], (N_S5_LAYERS, 2, G, CH, P), P ** -0.5),
        's5_d': nrm(ks[21], (N_S5_LAYERS, S5_WIDTH), 1.0),
        's5_w_glu': nrm(ks[22], (N_S5_LAYERS, S5_WIDTH, S5_WIDTH), S5_WIDTH ** -0.5),
        's5_b_glu': nrm(ks[23], (N_S5_LAYERS, S5_WIDTH), 0.01),
        's5_w_out': nrm(ks[24], (N_S5_LAYERS, S5_WIDTH, D), S5_WIDTH ** -0.5),
        'final_g': 1.0 + nrm(ks[25], (D,), 0.01),
    }


def _fwd_reference(x, c, ctx, c_ctx, ada_w, ada_b, norm_g,
              mla_w_in, mla_q_norm, mla_w_uq, mla_kv_norm, mla_w_ukv, mla_w_out,
              s5_w_in, s5_a_re, s5_a_im, s5_log_step, s5_b_re, s5_b_im, s5_c_re, s5_c_im,
              s5_d, s5_w_glu, s5_b_glu, s5_w_out, final_g):
    silu_c = jax.nn.silu(c)
    silu_cc = jax.nn.silu(c_ctx)
    for i in range(DEPTH):
        need_ctx = i < DEPTH - 1
        mod_l = silu_c @ ada_w[i] + ada_b[i]
        mod_c = silu_cc @ ada_w[i] + ada_b[i]
        sh_l, sc_l, gt_l = jnp.split(mod_l, 3, axis=-1)
        sh_c, sc_c, gt_c = jnp.split(mod_c, 3, axis=-1)
        h_l = rmsnorm(x, norm_g[i]) * (1.0 + sc_l[:, None, :]) + sh_l[:, None, :]
        h_c = rmsnorm(ctx, norm_g[i]) * (1.0 + sc_c) + sh_c
        j = i // N_MIXERS
        if i % N_MIXERS == 0:
            o_l, o_c = mla_mixer(h_l, h_c, need_ctx, mla_w_in[j], mla_q_norm[j], mla_w_uq[j],
                                 mla_kv_norm[j], mla_w_ukv[j], mla_w_out[j])
        else:
            o_l, o_c = s5_mixer(h_l, h_c, need_ctx, s5_w_in[j], s5_a_re[j], s5_a_im[j],
                                s5_log_step[j], s5_b_re[j], s5_b_im[j], s5_c_re[j], s5_c_im[j],
                                s5_d[j], s5_w_glu[j], s5_b_glu[j], s5_w_out[j])
        x = x + gt_l[:, None, :] * o_l
        if need_ctx:
            ctx = ctx + gt_c * o_c
    return rmsnorm(x, final_g)


import jax as _jax
import jax.numpy as _jnp

TWIN_FORMAT = 'train_step'
FWD_PARAMS = ['x', 'c', 'ctx', 'c_ctx', 'ada_w', 'ada_b', 'norm_g', 'mla_w_in', 'mla_q_norm', 'mla_w_uq', 'mla_kv_norm', 'mla_w_ukv', 'mla_w_out', 's5_w_in', 's5_a_re', 's5_a_im', 's5_log_step', 's5_b_re', 's5_b_im', 's5_c_re', 's5_c_im', 's5_d', 's5_w_glu', 's5_b_glu', 's5_w_out', 'final_g']
TWIN_WEIGHTS = ['c_ctx', 'ada_w', 'ada_b', 'norm_g', 'mla_w_in', 'mla_q_norm', 'mla_w_uq', 'mla_kv_norm', 'mla_w_ukv', 'mla_w_out', 's5_w_in', 's5_a_re', 's5_a_im', 's5_log_step', 's5_b_re', 's5_b_im', 's5_c_re', 's5_c_im', 's5_d', 's5_w_glu', 's5_b_glu', 's5_w_out', 'final_g']
TWIN_DIFF_INPUT = 'x'
TWIN_INPUTS = ['x', 'c', 'ctx', 'c_ctx', 'ada_w', 'ada_b', 'norm_g', 'mla_w_in', 'mla_q_norm', 'mla_w_uq', 'mla_kv_norm', 'mla_w_ukv', 'mla_w_out', 's5_w_in', 's5_a_re', 's5_a_im', 's5_log_step', 's5_b_re', 's5_b_im', 's5_c_re', 's5_c_im', 's5_d', 's5_w_glu', 's5_b_glu', 's5_w_out', 'final_g', 'loss_target', 'm_c_ctx', 'm_ada_w', 'm_ada_b', 'm_norm_g', 'm_mla_w_in', 'm_mla_q_norm', 'm_mla_w_uq', 'm_mla_kv_norm', 'm_mla_w_ukv', 'm_mla_w_out', 'm_s5_w_in', 'm_s5_a_re', 'm_s5_a_im', 'm_s5_log_step', 'm_s5_b_re', 'm_s5_b_im', 'm_s5_c_re', 'm_s5_c_im', 'm_s5_d', 'm_s5_w_glu', 'm_s5_b_glu', 'm_s5_w_out', 'm_final_g', 'v_c_ctx', 'v_ada_w', 'v_ada_b', 'v_norm_g', 'v_mla_w_in', 'v_mla_q_norm', 'v_mla_w_uq', 'v_mla_kv_norm', 'v_mla_w_ukv', 'v_mla_w_out', 'v_s5_w_in', 'v_s5_a_re', 'v_s5_a_im', 'v_s5_log_step', 'v_s5_b_re', 'v_s5_b_im', 'v_s5_c_re', 'v_s5_c_im', 'v_s5_d', 'v_s5_w_glu', 'v_s5_b_glu', 'v_s5_w_out', 'v_final_g']
TWIN_OUTPUTS = ['loss', 'grad_x', 'grad_c_ctx', 'grad_ada_w', 'grad_ada_b', 'grad_norm_g', 'grad_mla_w_in', 'grad_mla_q_norm', 'grad_mla_w_uq', 'grad_mla_kv_norm', 'grad_mla_w_ukv', 'grad_mla_w_out', 'grad_s5_w_in', 'grad_s5_a_re', 'grad_s5_a_im', 'grad_s5_log_step', 'grad_s5_b_re', 'grad_s5_b_im', 'grad_s5_c_re', 'grad_s5_c_im', 'grad_s5_d', 'grad_s5_w_glu', 'grad_s5_b_glu', 'grad_s5_w_out', 'grad_final_g', 'delta_c_ctx', 'delta_ada_w', 'delta_ada_b', 'delta_norm_g', 'delta_mla_w_in', 'delta_mla_q_norm', 'delta_mla_w_uq', 'delta_mla_kv_norm', 'delta_mla_w_ukv', 'delta_mla_w_out', 'delta_s5_w_in', 'delta_s5_a_re', 'delta_s5_a_im', 'delta_s5_log_step', 'delta_s5_b_re', 'delta_s5_b_im', 'delta_s5_c_re', 'delta_s5_c_im', 'delta_s5_d', 'delta_s5_w_glu', 'delta_s5_b_glu', 'delta_s5_w_out', 'delta_final_g', 'new_m_c_ctx', 'new_m_ada_w', 'new_m_ada_b', 'new_m_norm_g', 'new_m_mla_w_in', 'new_m_mla_q_norm', 'new_m_mla_w_uq', 'new_m_mla_kv_norm', 'new_m_mla_w_ukv', 'new_m_mla_w_out', 'new_m_s5_w_in', 'new_m_s5_a_re', 'new_m_s5_a_im', 'new_m_s5_log_step', 'new_m_s5_b_re', 'new_m_s5_b_im', 'new_m_s5_c_re', 'new_m_s5_c_im', 'new_m_s5_d', 'new_m_s5_w_glu', 'new_m_s5_b_glu', 'new_m_s5_w_out', 'new_m_final_g', 'new_v_c_ctx', 'new_v_ada_w', 'new_v_ada_b', 'new_v_norm_g', 'new_v_mla_w_in', 'new_v_mla_q_norm', 'new_v_mla_w_uq', 'new_v_mla_kv_norm', 'new_v_mla_w_ukv', 'new_v_mla_w_out', 'new_v_s5_w_in', 'new_v_s5_a_re', 'new_v_s5_a_im', 'new_v_s5_log_step', 'new_v_s5_b_re', 'new_v_s5_b_im', 'new_v_s5_c_re', 'new_v_s5_c_im', 'new_v_s5_d', 'new_v_s5_w_glu', 'new_v_s5_b_glu', 'new_v_s5_w_out', 'new_v_final_g']
TWIN_LEAF_KINDS = {'loss': 'loss', 'grad_x': 'grad_x', 'grad_c_ctx': 'grad_w', 'grad_ada_w': 'grad_w', 'grad_ada_b': 'grad_w', 'grad_norm_g': 'grad_w', 'grad_mla_w_in': 'grad_w', 'grad_mla_q_norm': 'grad_w', 'grad_mla_w_uq': 'grad_w', 'grad_mla_kv_norm': 'grad_w', 'grad_mla_w_ukv': 'grad_w', 'grad_mla_w_out': 'grad_w', 'grad_s5_w_in': 'grad_w', 'grad_s5_a_re': 'grad_w', 'grad_s5_a_im': 'grad_w', 'grad_s5_log_step': 'grad_w', 'grad_s5_b_re': 'grad_w', 'grad_s5_b_im': 'grad_w', 'grad_s5_c_re': 'grad_w', 'grad_s5_c_im': 'grad_w', 'grad_s5_d': 'grad_w', 'grad_s5_w_glu': 'grad_w', 'grad_s5_b_glu': 'grad_w', 'grad_s5_w_out': 'grad_w', 'grad_final_g': 'grad_w', 'delta_c_ctx': 'delta_w', 'delta_ada_w': 'delta_w', 'delta_ada_b': 'delta_w', 'delta_norm_g': 'delta_w', 'delta_mla_w_in': 'delta_w', 'delta_mla_q_norm': 'delta_w', 'delta_mla_w_uq': 'delta_w', 'delta_mla_kv_norm': 'delta_w', 'delta_mla_w_ukv': 'delta_w', 'delta_mla_w_out': 'delta_w', 'delta_s5_w_in': 'delta_w', 'delta_s5_a_re': 'delta_w', 'delta_s5_a_im': 'delta_w', 'delta_s5_log_step': 'delta_w', 'delta_s5_b_re': 'delta_w', 'delta_s5_b_im': 'delta_w', 'delta_s5_c_re': 'delta_w', 'delta_s5_c_im': 'delta_w', 'delta_s5_d': 'delta_w', 'delta_s5_w_glu': 'delta_w', 'delta_s5_b_glu': 'delta_w', 'delta_s5_w_out': 'delta_w', 'delta_final_g': 'delta_w', 'new_m_c_ctx': 'new_m', 'new_m_ada_w': 'new_m', 'new_m_ada_b': 'new_m', 'new_m_norm_g': 'new_m', 'new_m_mla_w_in': 'new_m', 'new_m_mla_q_norm': 'new_m', 'new_m_mla_w_uq': 'new_m', 'new_m_mla_kv_norm': 'new_m', 'new_m_mla_w_ukv': 'new_m', 'new_m_mla_w_out': 'new_m', 'new_m_s5_w_in': 'new_m', 'new_m_s5_a_re': 'new_m', 'new_m_s5_a_im': 'new_m', 'new_m_s5_log_step': 'new_m', 'new_m_s5_b_re': 'new_m', 'new_m_s5_b_im': 'new_m', 'new_m_s5_c_re': 'new_m', 'new_m_s5_c_im': 'new_m', 'new_m_s5_d': 'new_m', 'new_m_s5_w_glu': 'new_m', 'new_m_s5_b_glu': 'new_m', 'new_m_s5_w_out': 'new_m', 'new_m_final_g': 'new_m', 'new_v_c_ctx': 'new_v', 'new_v_ada_w': 'new_v', 'new_v_ada_b': 'new_v', 'new_v_norm_g': 'new_v', 'new_v_mla_w_in': 'new_v', 'new_v_mla_q_norm': 'new_v', 'new_v_mla_w_uq': 'new_v', 'new_v_mla_kv_norm': 'new_v', 'new_v_mla_w_ukv': 'new_v', 'new_v_mla_w_out': 'new_v', 'new_v_s5_w_in': 'new_v', 'new_v_s5_a_re': 'new_v', 'new_v_s5_a_im': 'new_v', 'new_v_s5_log_step': 'new_v', 'new_v_s5_b_re': 'new_v', 'new_v_s5_b_im': 'new_v', 'new_v_s5_c_re': 'new_v', 'new_v_s5_c_im': 'new_v', 'new_v_s5_d': 'new_v', 'new_v_s5_w_glu': 'new_v', 'new_v_s5_b_glu': 'new_v', 'new_v_s5_w_out': 'new_v', 'new_v_final_g': 'new_v'}


def _forward(args):
    return _fwd_reference(*[args[k] for k in FWD_PARAMS])


def _output_shape():
    out = _jax.eval_shape(lambda: _forward(_fwd_setup_inputs(0)))
    return out.shape, out.dtype

N_MICROBATCH = 1
ADAM_LR = 0.001
ADAM_B1 = 0.9
ADAM_B2 = 0.999
ADAM_EPS = 1e-08
ADAM_WD = 0.01
ADAM_STEP = 10
PER_EXAMPLE_BATCH_AXIS = {'x': 0, 'c': 0, 'ctx': 0, 'loss_target': 0}
SHARED_INPUTS = []
_WEIGHT_DTYPES = {'c_ctx': _jnp.float32, 'ada_w': _jnp.float32, 'ada_b': _jnp.float32, 'norm_g': _jnp.float32, 'mla_w_in': _jnp.float32, 'mla_q_norm': _jnp.float32, 'mla_w_uq': _jnp.float32, 'mla_kv_norm': _jnp.float32, 'mla_w_ukv': _jnp.float32, 'mla_w_out': _jnp.float32, 's5_w_in': _jnp.float32, 's5_a_re': _jnp.float32, 's5_a_im': _jnp.float32, 's5_log_step': _jnp.float32, 's5_b_re': _jnp.float32, 's5_b_im': _jnp.float32, 's5_c_re': _jnp.float32, 's5_c_im': _jnp.float32, 's5_d': _jnp.float32, 's5_w_glu': _jnp.float32, 's5_b_glu': _jnp.float32, 's5_w_out': _jnp.float32, 'final_g': _jnp.float32}
MOMENT_SCALE = {'c_ctx': 3.732277e-03, 'ada_w': 1.257621e-02, 'ada_b': 2.058810e-02, 'norm_g': 1.173768e-02, 'mla_w_in': 8.385360e-03, 'mla_q_norm': 5.696501e-03, 'mla_w_uq': 2.272983e-03, 'mla_kv_norm': 3.169778e-02, 'mla_w_ukv': 5.024857e-03, 'mla_w_out': 7.026757e-03, 's5_w_in': 1.034752e-02, 's5_a_re': 9.233380e-04, 's5_a_im': 8.454713e-04, 's5_log_step': 5.331237e-01, 's5_b_re': 5.700341e-04, 's5_b_im': 5.346345e-04, 's5_c_re': 7.671336e-04, 's5_c_im': 7.648610e-04, 's5_d': 1.072586e-02, 's5_w_glu': 3.139839e-03, 's5_b_glu': 4.060695e-03, 's5_w_out': 9.881198e-03, 'final_g': 1.598979e+01}


def _to_microbatches(a, axis):
    t = _jnp.moveaxis(a, axis, 0)
    t = t.reshape((N_MICROBATCH, t.shape[0] // N_MICROBATCH) + t.shape[1:])
    return _jnp.moveaxis(t, 1, axis + 1)


def setup_inputs(seed: int = 0) -> dict:
    inp = _fwd_setup_inputs(seed)
    key = _jax.random.fold_in(_jax.random.key(seed), 7919)
    shape, _ = _output_shape()
    out = dict(inp)
    out["loss_target"] = _jax.random.normal(_jax.random.fold_in(key, 0), shape, _jnp.float32)
    for i, name in enumerate(TWIN_WEIGHTS):
        w = inp[name].astype(_jnp.float32)
        if MOMENT_SCALE is None:
            s = _jnp.sqrt(_jnp.mean(_jnp.square(w)) + 1e-30)
        else:
            s = MOMENT_SCALE[name]
        km, kv = _jax.random.split(_jax.random.fold_in(key, i + 1))
        out[name] = w
        out["m_" + name] = s * _jax.random.normal(km, w.shape, _jnp.float32)
        out["v_" + name] = (s * s) * _jax.random.uniform(kv, w.shape, _jnp.float32, 0.5, 1.5)
    if N_MICROBATCH > 1:
        for name, axis in PER_EXAMPLE_BATCH_AXIS.items():
            out[name] = _to_microbatches(out[name], axis)
    return {'x': out['x'], 'c': out['c'], 'ctx': out['ctx'], 'c_ctx': out['c_ctx'], 'ada_w': out['ada_w'], 'ada_b': out['ada_b'], 'norm_g': out['norm_g'], 'mla_w_in': out['mla_w_in'], 'mla_q_norm': out['mla_q_norm'], 'mla_w_uq': out['mla_w_uq'], 'mla_kv_norm': out['mla_kv_norm'], 'mla_w_ukv': out['mla_w_ukv'], 'mla_w_out': out['mla_w_out'], 's5_w_in': out['s5_w_in'], 's5_a_re': out['s5_a_re'], 's5_a_im': out['s5_a_im'], 's5_log_step': out['s5_log_step'], 's5_b_re': out['s5_b_re'], 's5_b_im': out['s5_b_im'], 's5_c_re': out['s5_c_re'], 's5_c_im': out['s5_c_im'], 's5_d': out['s5_d'], 's5_w_glu': out['s5_w_glu'], 's5_b_glu': out['s5_b_glu'], 's5_w_out': out['s5_w_out'], 'final_g': out['final_g'], 'loss_target': out['loss_target'], 'm_c_ctx': out['m_c_ctx'], 'm_ada_w': out['m_ada_w'], 'm_ada_b': out['m_ada_b'], 'm_norm_g': out['m_norm_g'], 'm_mla_w_in': out['m_mla_w_in'], 'm_mla_q_norm': out['m_mla_q_norm'], 'm_mla_w_uq': out['m_mla_w_uq'], 'm_mla_kv_norm': out['m_mla_kv_norm'], 'm_mla_w_ukv': out['m_mla_w_ukv'], 'm_mla_w_out': out['m_mla_w_out'], 'm_s5_w_in': out['m_s5_w_in'], 'm_s5_a_re': out['m_s5_a_re'], 'm_s5_a_im': out['m_s5_a_im'], 'm_s5_log_step': out['m_s5_log_step'], 'm_s5_b_re': out['m_s5_b_re'], 'm_s5_b_im': out['m_s5_b_im'], 'm_s5_c_re': out['m_s5_c_re'], 'm_s5_c_im': out['m_s5_c_im'], 'm_s5_d': out['m_s5_d'], 'm_s5_w_glu': out['m_s5_w_glu'], 'm_s5_b_glu': out['m_s5_b_glu'], 'm_s5_w_out': out['m_s5_w_out'], 'm_final_g': out['m_final_g'], 'v_c_ctx': out['v_c_ctx'], 'v_ada_w': out['v_ada_w'], 'v_ada_b': out['v_ada_b'], 'v_norm_g': out['v_norm_g'], 'v_mla_w_in': out['v_mla_w_in'], 'v_mla_q_norm': out['v_mla_q_norm'], 'v_mla_w_uq': out['v_mla_w_uq'], 'v_mla_kv_norm': out['v_mla_kv_norm'], 'v_mla_w_ukv': out['v_mla_w_ukv'], 'v_mla_w_out': out['v_mla_w_out'], 'v_s5_w_in': out['v_s5_w_in'], 'v_s5_a_re': out['v_s5_a_re'], 'v_s5_a_im': out['v_s5_a_im'], 'v_s5_log_step': out['v_s5_log_step'], 'v_s5_b_re': out['v_s5_b_re'], 'v_s5_b_im': out['v_s5_b_im'], 'v_s5_c_re': out['v_s5_c_re'], 'v_s5_c_im': out['v_s5_c_im'], 'v_s5_d': out['v_s5_d'], 'v_s5_w_glu': out['v_s5_w_glu'], 'v_s5_b_glu': out['v_s5_b_glu'], 'v_s5_w_out': out['v_s5_w_out'], 'v_final_g': out['v_final_g']}


def _loss(weights, diff, rest, loss_target):
    with _jax.named_scope("forward"):
        args = {**rest, TWIN_DIFF_INPUT: diff, **{k: w.astype(_WEIGHT_DTYPES[k]) for k, w in weights.items()}}
        y = _forward(args)
    with _jax.named_scope("loss_head"):
        err = _jnp.square(y.astype(_jnp.float32) - loss_target)
        return 0.5 * _jnp.sum(_jnp.mean(err, axis=-1)) if err.ndim else 0.5 * err


def _adamw(w, g, m, v):
    m = ADAM_B1 * m + (1.0 - ADAM_B1) * g
    v = ADAM_B2 * v + (1.0 - ADAM_B2) * _jnp.square(g)
    m_hat = m / (1.0 - ADAM_B1 ** ADAM_STEP)
    v_hat = v / (1.0 - ADAM_B2 ** ADAM_STEP)
    delta = -ADAM_LR * (m_hat / (_jnp.sqrt(v_hat) + ADAM_EPS) + ADAM_WD * w)
    return delta, m, v


def reference(x, c, ctx, c_ctx, ada_w, ada_b, norm_g, mla_w_in, mla_q_norm, mla_w_uq, mla_kv_norm, mla_w_ukv, mla_w_out, s5_w_in, s5_a_re, s5_a_im, s5_log_step, s5_b_re, s5_b_im, s5_c_re, s5_c_im, s5_d, s5_w_glu, s5_b_glu, s5_w_out, final_g, loss_target, m_c_ctx, m_ada_w, m_ada_b, m_norm_g, m_mla_w_in, m_mla_q_norm, m_mla_w_uq, m_mla_kv_norm, m_mla_w_ukv, m_mla_w_out, m_s5_w_in, m_s5_a_re, m_s5_a_im, m_s5_log_step, m_s5_b_re, m_s5_b_im, m_s5_c_re, m_s5_c_im, m_s5_d, m_s5_w_glu, m_s5_b_glu, m_s5_w_out, m_final_g, v_c_ctx, v_ada_w, v_ada_b, v_norm_g, v_mla_w_in, v_mla_q_norm, v_mla_w_uq, v_mla_kv_norm, v_mla_w_ukv, v_mla_w_out, v_s5_w_in, v_s5_a_re, v_s5_a_im, v_s5_log_step, v_s5_b_re, v_s5_b_im, v_s5_c_re, v_s5_c_im, v_s5_d, v_s5_w_glu, v_s5_b_glu, v_s5_w_out, v_final_g):
    given = dict(x=x, c=c, ctx=ctx, c_ctx=c_ctx, ada_w=ada_w, ada_b=ada_b, norm_g=norm_g, mla_w_in=mla_w_in, mla_q_norm=mla_q_norm, mla_w_uq=mla_w_uq, mla_kv_norm=mla_kv_norm, mla_w_ukv=mla_w_ukv, mla_w_out=mla_w_out, s5_w_in=s5_w_in, s5_a_re=s5_a_re, s5_a_im=s5_a_im, s5_log_step=s5_log_step, s5_b_re=s5_b_re, s5_b_im=s5_b_im, s5_c_re=s5_c_re, s5_c_im=s5_c_im, s5_d=s5_d, s5_w_glu=s5_w_glu, s5_b_glu=s5_b_glu, s5_w_out=s5_w_out, final_g=final_g, loss_target=loss_target, m_c_ctx=m_c_ctx, m_ada_w=m_ada_w, m_ada_b=m_ada_b, m_norm_g=m_norm_g, m_mla_w_in=m_mla_w_in, m_mla_q_norm=m_mla_q_norm, m_mla_w_uq=m_mla_w_uq, m_mla_kv_norm=m_mla_kv_norm, m_mla_w_ukv=m_mla_w_ukv, m_mla_w_out=m_mla_w_out, m_s5_w_in=m_s5_w_in, m_s5_a_re=m_s5_a_re, m_s5_a_im=m_s5_a_im, m_s5_log_step=m_s5_log_step, m_s5_b_re=m_s5_b_re, m_s5_b_im=m_s5_b_im, m_s5_c_re=m_s5_c_re, m_s5_c_im=m_s5_c_im, m_s5_d=m_s5_d, m_s5_w_glu=m_s5_w_glu, m_s5_b_glu=m_s5_b_glu, m_s5_w_out=m_s5_w_out, m_final_g=m_final_g, v_c_ctx=v_c_ctx, v_ada_w=v_ada_w, v_ada_b=v_ada_b, v_norm_g=v_norm_g, v_mla_w_in=v_mla_w_in, v_mla_q_norm=v_mla_q_norm, v_mla_w_uq=v_mla_w_uq, v_mla_kv_norm=v_mla_kv_norm, v_mla_w_ukv=v_mla_w_ukv, v_mla_w_out=v_mla_w_out, v_s5_w_in=v_s5_w_in, v_s5_a_re=v_s5_a_re, v_s5_a_im=v_s5_a_im, v_s5_log_step=v_s5_log_step, v_s5_b_re=v_s5_b_re, v_s5_b_im=v_s5_b_im, v_s5_c_re=v_s5_c_re, v_s5_c_im=v_s5_c_im, v_s5_d=v_s5_d, v_s5_w_glu=v_s5_w_glu, v_s5_b_glu=v_s5_b_glu, v_s5_w_out=v_s5_w_out, v_final_g=v_final_g)
    weights = {n: given[n] for n in TWIN_WEIGHTS}
    shared = {n: given[n] for n in SHARED_INPUTS}
    per_example = {n: given[n] for n in ['x', 'c', 'ctx']}
    grad_fn = _jax.value_and_grad(_loss, argnums=(0, 1))

    def one_microbatch(ex, loss_target):
        ex = dict(ex)
        diff = ex.pop(TWIN_DIFF_INPUT)
        return grad_fn(weights, diff, {**shared, **ex}, loss_target)

    if N_MICROBATCH == 1:
        loss, (grad_w, grad_x) = one_microbatch(per_example, given["loss_target"])
    else:
        def body(carry, xs):
            loss_sum, grad_sum = carry
            l_k, (gw_k, gx_k) = one_microbatch(xs[0], xs[1])
            with _jax.named_scope("update"):
                return (loss_sum + l_k, _jax.tree.map(_jnp.add, grad_sum, gw_k)), gx_k

        init = (_jnp.zeros((), _jnp.float32), _jax.tree.map(_jnp.zeros_like, weights))
        (loss, grad_w), grad_x = _jax.lax.scan(body, init, (per_example, given["loss_target"]))
    with _jax.named_scope("update"):
        delta_w, new_m, new_v = {}, {}, {}
        for n in TWIN_WEIGHTS:
            delta_w[n], new_m[n], new_v[n] = _adamw(weights[n], grad_w[n], given["m_" + n], given["v_" + n])
    return (loss, grad_x, *[grad_w[n] for n in TWIN_WEIGHTS], *[delta_w[n] for n in TWIN_WEIGHTS],
            *[new_m[n] for n in TWIN_WEIGHTS], *[new_v[n] for n in TWIN_WEIGHTS])
```

```python
import functools
import math

import jax
import jax.numpy as jnp
import numpy as np
from jax import lax
from jax.experimental import pallas as pl
from jax.experimental.pallas import tpu as pltpu

F32 = jnp.float32
BF16 = jnp.bfloat16

D_MODEL = 1024
GRID_W = 64
EPS = 1e-6
MLA_HEADS = 16
QK_NOPE_DIM = 64
QK_ROPE_DIM = 32
V_HEAD_DIM = 64
Q_LORA_RANK = 256
KV_LORA_RANK = 128
QK_DIM = QK_NOPE_DIM + QK_ROPE_DIM
SOFTMAX_SCALE = QK_DIM ** -0.5
ROPE_THETA = 10000.0
S5_GROUP = 16
S5_GROUPS = D_MODEL // S5_GROUP
S5_STATE = 64
S5_LANES = S5_GROUPS * S5_STATE
N_SEG = 8
GROUPS_PER_BLOCK = 8
N_BLOCKS = S5_GROUPS // GROUPS_PER_BLOCK
BLK_CH = GROUPS_PER_BLOCK * S5_GROUP
BLK_ST = GROUPS_PER_BLOCK * S5_STATE

ADAM_LR = 0.001
ADAM_B1 = 0.9
ADAM_B2 = 0.999
ADAM_EPS = 1e-08
ADAM_WD = 0.01
ADAM_STEP = 10

N_DEV = 8
N_CHIP = 4
MESH = pl.DeviceIdType.MESH
VMEM_LIMIT = 52 * 1024 * 1024
ROW_TILE = 256


def _params(sem=None, vmem=None):
    return pltpu.CompilerParams(dimension_semantics=sem, vmem_limit_bytes=vmem)


def mm_nn(a, b, out_dtype=F32, name="mm_nn"):
    M, K = a.shape
    N = b.shape[1]
    tm = math.gcd(ROW_TILE, M)

    def body(a_ref, b_ref, o_ref):
        o_ref[...] = jnp.dot(a_ref[...].astype(BF16), b_ref[...].astype(BF16),
                             preferred_element_type=F32).astype(o_ref.dtype)

    return pl.pallas_call(
        body, out_shape=jax.ShapeDtypeStruct((M, N), out_dtype), grid=(M // tm,),
        in_specs=[pl.BlockSpec((tm, K), lambda i: (i, 0)), pl.BlockSpec((K, N), lambda i: (0, 0))],
        out_specs=pl.BlockSpec((tm, N), lambda i: (i, 0)),
        compiler_params=_params(("parallel",), VMEM_LIMIT), name=name)(a, b)


def mm_nt(a, b, out_dtype=F32, name="mm_nt"):
    M, N = a.shape
    K = b.shape[0]
    tm = math.gcd(ROW_TILE, M)

    def body(a_ref, b_ref, o_ref):
        o_ref[...] = lax.dot_general(a_ref[...].astype(BF16), b_ref[...].astype(BF16),
                                     (((1,), (1,)), ((), ())),
                                     preferred_element_type=F32).astype(o_ref.dtype)

    return pl.pallas_call(
        body, out_shape=jax.ShapeDtypeStruct((M, K), out_dtype), grid=(M // tm,),
        in_specs=[pl.BlockSpec((tm, N), lambda i: (i, 0)), pl.BlockSpec((K, N), lambda i: (0, 0))],
        out_specs=pl.BlockSpec((tm, K), lambda i: (i, 0)),
        compiler_params=_params(("parallel",), VMEM_LIMIT), name=name)(a, b)


def mm_tn(a, b, name="mm_tn"):
    M, K = a.shape
    N = b.shape[1]
    tm = math.gcd(2 * ROW_TILE, M)

    def body(a_ref, b_ref, o_ref):
        @pl.when(pl.program_id(0) == 0)
        def _():
            o_ref[...] = jnp.zeros_like(o_ref)

        o_ref[...] += lax.dot_general(a_ref[...].astype(BF16), b_ref[...].astype(BF16),
                                      (((0,), (0,)), ((), ())), preferred_element_type=F32)

    return pl.pallas_call(
        body, out_shape=jax.ShapeDtypeStruct((K, N), F32), grid=(M // tm,),
        in_specs=[pl.BlockSpec((tm, K), lambda i: (i, 0)), pl.BlockSpec((tm, N), lambda i: (i, 0))],
        out_specs=pl.BlockSpec((K, N), lambda i: (0, 0)),
        compiler_params=_params(("arbitrary",), VMEM_LIMIT), name=name)(a, b)


class Rows:
    def __init__(self, arr, width=None, row_off=0, col_blk=0):
        self.arr = arr
        self.width = arr.shape[1] if width is None else width
        self.row_off = row_off
        self.col_blk = col_blk

    def spec(self, tm):
        ro, cb = self.row_off // tm, self.col_blk
        return pl.BlockSpec((tm, self.width), lambda i: (i + ro, cb))


def _as_rows(x):
    return x if isinstance(x, Rows) else Rows(x)


def _row_tile(n_rows, n_ctx_rows, rows):
    tm = math.gcd(ROW_TILE, n_rows, n_ctx_rows)
    for r in rows:
        tm = math.gcd(tm, r.row_off)
    return tm


def _bc_spec(arr, n_ctx_blocks):
    g, _, d = arr.shape
    if g == 1:
        return pl.BlockSpec((1, 1, d), lambda i: (0, 0, 0))
    return pl.BlockSpec((1, 1, d), lambda i: ((i >= n_ctx_blocks).astype(jnp.int32), 0, 0))


def rowwise_fwd(fn, rows, bcs, out_dims, out_dtypes, n_rows, n_ctx_rows, name):
    rows = [_as_rows(r) for r in rows]
    tm = _row_tile(n_rows, n_ctx_rows, rows)
    ncb = n_ctx_rows // tm
    nr, nb = len(rows), len(bcs)

    def body(*refs):
        vals = [r[...].astype(F32) for r in refs[:nr]] + [b[0].astype(F32) for b in refs[nr:nr + nb]]
        outs = fn(*vals)
        for o_ref, v in zip(refs[nr + nb:], outs):
            o_ref[...] = v.astype(o_ref.dtype)

    outs = pl.pallas_call(
        body,
        out_shape=[jax.ShapeDtypeStruct((n_rows, d), dt) for d, dt in zip(out_dims, out_dtypes)],
        grid=(n_rows // tm,),
        in_specs=[r.spec(tm) for r in rows] + [_bc_spec(b, ncb) for b in bcs],
        out_specs=[pl.BlockSpec((tm, d), lambda i: (i, 0)) for d in out_dims],
        compiler_params=_params(("parallel",), VMEM_LIMIT), name=name)(*[r.arr for r in rows], *bcs)
    return outs


def rowwise_bwd(fn, rows, bcs, cts, diff_rows, diff_bcs, n_rows, n_ctx_rows, name):
    rows = [_as_rows(r) for r in rows]
    cts = [_as_rows(c) for c in cts]
    tm = _row_tile(n_rows, n_ctx_rows, rows + cts)
    ncb = n_ctx_rows // tm
    nr, nb, nc = len(rows), len(bcs), len(cts)
    ndr, ndb = len(diff_rows), len(diff_bcs)

    def body(*refs):
        i = pl.program_id(0)
        rvals = [r[...].astype(F32) for r in refs[:nr]]
        bvals = [b[0].astype(F32) for b in refs[nr:nr + nb]]
        cvals = [c[...].astype(F32) for c in refs[nr + nb:nr + nb + nc]]
        outs = refs[nr + nb + nc:]

        def f(*d):
            rv, bv = list(rvals), list(bvals)
            for k, idx in enumerate(diff_rows):
                rv[idx] = d[k]
            for k, idx in enumerate(diff_bcs):
                bv[idx] = d[ndr + k]
            return tuple(fn(*rv, *bv))

        primals = [rvals[k] for k in diff_rows] + [bvals[k] for k in diff_bcs]
        _, vjp = jax.vjp(f, *primals)
        grads = vjp(tuple(cvals))
        for k in range(ndr):
            outs[k][...] = grads[k].astype(outs[k].dtype)
        for k, idx in enumerate(diff_bcs):
            o_ref = outs[ndr + k]
            first = (i == 0)
            if bcs[idx].shape[0] == 2:
                first = first | (i == ncb)

            @pl.when(first)
            def _(o_ref=o_ref):
                o_ref[...] = jnp.zeros_like(o_ref)

            o_ref[0] += grads[ndr + k]

    out_shape = [jax.ShapeDtypeStruct((n_rows, rows[k].width), F32) for k in diff_rows]
    out_shape += [jax.ShapeDtypeStruct(bcs[k].shape, F32) for k in diff_bcs]
    out_specs = [pl.BlockSpec((tm, rows[k].width), lambda i: (i, 0)) for k in diff_rows]
    out_specs += [_bc_spec(bcs[k], ncb) for k in diff_bcs]
    outs = pl.pallas_call(
        body, out_shape=out_shape, grid=(n_rows // tm,),
        in_specs=[r.spec(tm) for r in rows] + [_bc_spec(b, ncb) for b in bcs] + [c.spec(tm) for c in cts],
        out_specs=out_specs,
        compiler_params=_params(("arbitrary",), VMEM_LIMIT), name=name)(
            *[r.arr for r in rows], *bcs, *[c.arr for c in cts])
    return outs


def _rms(x):
    return x * lax.rsqrt(jnp.mean(x * x, axis=-1, keepdims=True) + EPS)


def _sigmoid(x):
    return 0.5 * (jnp.tanh(0.5 * x) + 1.0)


def _silu(x):
    return x * _sigmoid(x)


def _gelu_tanh(x):
    return 0.5 * x * (1.0 + jnp.tanh(math.sqrt(2.0 / math.pi) * (x + 0.044715 * (x * x * x))))


def f_norm_mod(x, g, sc, sh):
    return ((_rms(x) * g) * (1.0 + sc) + sh,)


def f_rms(x, g):
    return (_rms(x) * g,)


def f_gate(o, z):
    return (o * _silu(z),)


def f_res(x, o, gt):
    return (x + gt * o,)


def f_s5_act(y, u, d):
    return (_gelu_tanh(y + d * u),)


def f_s5_glu(ya, gl, z, b):
    return (ya * _sigmoid(gl + b) * _silu(z),)


def loss_and_grad(x2, final_g, target, name="loss_head"):
    n, d = x2.shape
    tm = math.gcd(ROW_TILE, n)

    def row_loss(x, g, t):
        y = _rms(x) * g
        e = y - t
        return 0.5 * (e * e) * (1.0 / d)

    def body(x_ref, g_ref, t_ref, l_ref, dx_ref, dg_ref):
        @pl.when(pl.program_id(0) == 0)
        def _():
            l_ref[...] = jnp.zeros_like(l_ref)
            dg_ref[...] = jnp.zeros_like(dg_ref)

        t = t_ref[...]
        lterm, vjp = jax.vjp(lambda x, g: row_loss(x, g, t), x_ref[...], g_ref[...])
        dx, dg = vjp(jnp.ones_like(lterm))
        l_ref[...] += jnp.sum(lterm, axis=0, keepdims=True)
        dx_ref[...] = dx
        dg_ref[...] += dg

    return pl.pallas_call(
        body,
        out_shape=[jax.ShapeDtypeStruct((1, d), F32), jax.ShapeDtypeStruct((n, d), F32),
                   jax.ShapeDtypeStruct((1, d), F32)],
        grid=(n // tm,),
        in_specs=[pl.BlockSpec((tm, d), lambda i: (i, 0)), pl.BlockSpec((1, d), lambda i: (0, 0)),
                  pl.BlockSpec((tm, d), lambda i: (i, 0))],
        out_specs=[pl.BlockSpec((1, d), lambda i: (0, 0)), pl.BlockSpec((tm, d), lambda i: (i, 0)),
                   pl.BlockSpec((1, d), lambda i: (0, 0))],
        compiler_params=_params(("arbitrary",), VMEM_LIMIT), name=name)(x2, final_g, target)


NEG_BIG = -1e30


def attn_fwd(qb, kb, vb, n_ctx):
    H, T, DK = qb.shape
    DV = vb.shape[-1]
    tq = math.gcd(ROW_TILE, n_ctx)
    nq, ncb = T // tq, n_ctx // tq

    def body(q_ref, k_ref, v_ref, o_ref, lse_ref):
        qi = pl.program_id(1)
        q = q_ref[0]
        nkv = jnp.where(qi < ncb, ncb, nq)

        def step(j, carry):
            m, l, acc = carry
            r0 = pl.multiple_of(j * tq, tq)
            k = k_ref[0, pl.ds(r0, tq), :]
            v = v_ref[0, pl.ds(r0, tq), :]
            s = lax.dot_general(q, k, (((1,), (1,)), ((), ())), preferred_element_type=F32) * SOFTMAX_SCALE
            m_new = jnp.maximum(m, jnp.max(s, axis=-1, keepdims=True))
            alpha = jnp.exp(m - m_new)
            p = jnp.exp(s - m_new)
            l = l * alpha + jnp.sum(p, axis=-1, keepdims=True)
            acc = acc * alpha + jnp.dot(p.astype(BF16), v, preferred_element_type=F32)
            return m_new, l, acc

        init = (jnp.full((tq, 1), NEG_BIG, F32), jnp.zeros((tq, 1), F32), jnp.zeros((tq, DV), F32))
        m, l, acc = lax.fori_loop(0, nkv, step, init)
        o_ref[0] = acc / l
        lse_ref[0] = m + jnp.log(l)

    return pl.pallas_call(
        body,
        out_shape=[jax.ShapeDtypeStruct((H, T, DV), F32), jax.ShapeDtypeStruct((H, T, 1), F32)],
        grid=(H, nq),
        in_specs=[pl.BlockSpec((1, tq, DK), lambda h, i: (h, i, 0)),
                  pl.BlockSpec((1, T, DK), lambda h, i: (h, 0, 0)),
                  pl.BlockSpec((1, T, DV), lambda h, i: (h, 0, 0))],
        out_specs=[pl.BlockSpec((1, tq, DV), lambda h, i: (h, i, 0)),
                   pl.BlockSpec((1, tq, 1), lambda h, i: (h, i, 0))],
        compiler_params=_params(("parallel", "parallel"), VMEM_LIMIT), name="attn_fwd")(qb, kb, vb)


def attn_bwd_dq(qb, kb, vb, o, do, lse, n_ctx):
    H, T, DK = qb.shape
    DV = vb.shape[-1]
    tq = math.gcd(ROW_TILE, n_ctx)
    nq, ncb = T // tq, n_ctx // tq

    def body(q_ref, k_ref, v_ref, o_ref, do_ref, lse_ref, dq_ref, delta_ref):
        qi = pl.program_id(1)
        q = q_ref[0]
        do = do_ref[0]
        delta = jnp.sum(do * o_ref[0], axis=-1, keepdims=True)
        lse_v = lse_ref[0]
        do16 = do.astype(BF16)
        nkv = jnp.where(qi < ncb, ncb, nq)

        def step(j, dq):
            r0 = pl.multiple_of(j * tq, tq)
            k = k_ref[0, pl.ds(r0, tq), :]
            v = v_ref[0, pl.ds(r0, tq), :]
            s = lax.dot_general(q, k, (((1,), (1,)), ((), ())), preferred_element_type=F32) * SOFTMAX_SCALE
            p = jnp.exp(s - lse_v)
            dp = lax.dot_general(do16, v, (((1,), (1,)), ((), ())), preferred_element_type=F32)
            ds = p * (dp - delta) * SOFTMAX_SCALE
            return dq + jnp.dot(ds.astype(BF16), k, preferred_element_type=F32)

        dq_ref[0] = lax.fori_loop(0, nkv, step, jnp.zeros((tq, DK), F32))
        delta_ref[0] = delta

    return pl.pallas_call(
        body,
        out_shape=[jax.ShapeDtypeStruct((H, T, DK), F32), jax.ShapeDtypeStruct((H, T, 1), F32)],
        grid=(H, nq),
        in_specs=[pl.BlockSpec((1, tq, DK), lambda h, i: (h, i, 0)),
                  pl.BlockSpec((1, T, DK), lambda h, i: (h, 0, 0)),
                  pl.BlockSpec((1, T, DV), lambda h, i: (h, 0, 0)),
                  pl.BlockSpec((1, tq, DV), lambda h, i: (h, i, 0)),
                  pl.BlockSpec((1, tq, DV), lambda h, i: (h, i, 0)),
                  pl.BlockSpec((1, tq, 1), lambda h, i: (h, i, 0))],
        out_specs=[pl.BlockSpec((1, tq, DK), lambda h, i: (h, i, 0)),
                   pl.BlockSpec((1, tq, 1), lambda h, i: (h, i, 0))],
        compiler_params=_params(("parallel", "parallel"), VMEM_LIMIT), name="attn_bwd_dq")(
            qb, kb, vb, o, do, lse)


def attn_bwd_dkv(qb, kb, vb, do, lse_rows, delta_rows, n_ctx):
    H, T, DK = qb.shape
    DV = vb.shape[-1]
    tq = math.gcd(ROW_TILE, n_ctx)
    nq, ncb = T // tq, n_ctx // tq

    def body(q_ref, do_ref, lse_ref, delta_ref, k_ref, v_ref, dk_ref, dv_ref):
        kj = pl.program_id(1)
        k = k_ref[0]
        v = v_ref[0]
        i0 = jnp.where(kj < ncb, 0, ncb)

        def step(i, carry):
            dk, dv = carry
            r0 = pl.multiple_of(i * tq, tq)
            q = q_ref[0, pl.ds(r0, tq), :]
            do16 = do_ref[0, pl.ds(r0, tq), :].astype(BF16)
            st = lax.dot_general(k, q, (((1,), (1,)), ((), ())), preferred_element_type=F32) * SOFTMAX_SCALE
            pt = jnp.exp(st - lse_ref[0, i])
            dv = dv + jnp.dot(pt.astype(BF16), do16, preferred_element_type=F32)
            dpt = lax.dot_general(v, do16, (((1,), (1,)), ((), ())), preferred_element_type=F32)
            dst = pt * (dpt - delta_ref[0, i]) * SOFTMAX_SCALE
            dk = dk + jnp.dot(dst.astype(BF16), q, preferred_element_type=F32)
            return dk, dv

        dk, dv = lax.fori_loop(i0, nq, step, (jnp.zeros((tq, DK), F32), jnp.zeros((tq, DV), F32)))
        dk_ref[0] = dk
        dv_ref[0] = dv

    return pl.pallas_call(
        body,
        out_shape=[jax.ShapeDtypeStruct((H, T, DK), F32), jax.ShapeDtypeStruct((H, T, DV), F32)],
        grid=(H, nq),
        in_specs=[pl.BlockSpec((1, T, DK), lambda h, j: (h, 0, 0)),
                  pl.BlockSpec((1, T, DV), lambda h, j: (h, 0, 0)),
                  pl.BlockSpec((1, nq, 1, tq), lambda h, j: (h, 0, 0, 0)),
                  pl.BlockSpec((1, nq, 1, tq), lambda h, j: (h, 0, 0, 0)),
                  pl.BlockSpec((1, tq, DK), lambda h, j: (h, j, 0)),
                  pl.BlockSpec((1, tq, DV), lambda h, j: (h, j, 0))],
        out_specs=[pl.BlockSpec((1, tq, DK), lambda h, j: (h, j, 0)),
                   pl.BlockSpec((1, tq, DV), lambda h, j: (h, j, 0))],
        compiler_params=_params(("parallel", "parallel"), VMEM_LIMIT), name="attn_bwd_dkv")(
            qb, do, lse_rows, delta_rows, kb, vb)


def _cmul(ar, ai, br, bi):
    return ar * br - ai * bi, ar * bi + ai * br


def s5_chain(finals, s0, a, n_steps, reverse, name):
    W = finals.shape[-1]
    first = N_SEG - 1 if reverse else 0

    def body(f_ref, s0_ref, a_ref, c_ref):
        pr, pi = jnp.ones((1, W), F32), jnp.zeros((1, W), F32)
        br, bi = a_ref[0], a_ref[1]
        n = n_steps
        while n:
            if n & 1:
                pr, pi = _cmul(pr, pi, br, bi)
            br, bi = _cmul(br, bi, br, bi)
            n >>= 1
        fr, fi = f_ref[0], f_ref[1]
        row = lax.broadcasted_iota(jnp.int32, (N_SEG, W), 0)
        s0r = jnp.broadcast_to(s0_ref[0], (N_SEG, W))
        s0i = jnp.broadcast_to(s0_ref[1], (N_SEG, W))
        cr = jnp.where(row == first, s0r, 0.0)
        ci = jnp.where(row == first, s0i, 0.0)
        shift = N_SEG - 1 if reverse else 1
        for _ in range(N_SEG - 1):
            mr, mi = _cmul(pr, pi, cr, ci)
            tr = pltpu.roll(fr + mr, shift, 0)
            ti = pltpu.roll(fi + mi, shift, 0)
            cr = jnp.where(row == first, s0r, tr)
            ci = jnp.where(row == first, s0i, ti)
        c_ref[0] = cr
        c_ref[1] = ci

    return pl.pallas_call(body, out_shape=jax.ShapeDtypeStruct((2, N_SEG, W), F32), name=name)(finals, s0, a)


def _scan_chunk(bur, bui, st_ref, a_ref, n_steps, reverse):
    for lc in range(S5_LANES // BLK_ST):
        sl = slice(lc * BLK_ST, (lc + 1) * BLK_ST)
        lr = jnp.broadcast_to(a_ref[0, :, sl], (N_SEG, BLK_ST))
        li = jnp.broadcast_to(a_ref[1, :, sl], (N_SEG, BLK_ST))

        def step(jj, carry, sl=sl, lr=lr, li=li):
            sr, si = carry
            j = (n_steps - 1 - jj) if reverse else jj
            r0 = pl.multiple_of(j * N_SEG, N_SEG)
            nr = lr * sr - li * si + bur[pl.ds(r0, N_SEG), sl]
            ni = lr * si + li * sr + bui[pl.ds(r0, N_SEG), sl]
            bur[pl.ds(r0, N_SEG), sl] = nr
            bui[pl.ds(r0, N_SEG), sl] = ni
            return nr, ni

        sr, si = lax.fori_loop(0, n_steps, step, (st_ref[0, :, sl], st_ref[1, :, sl]))
        st_ref[0, :, sl] = sr
        st_ref[1, :, sl] = si


def _project_in(x16, w_re, w_im, bur, bui, adjoint):
    for gb in range(N_BLOCKS):
        xb = x16[:, gb * BLK_CH:(gb + 1) * BLK_CH]
        sl = slice(gb * BLK_ST, (gb + 1) * BLK_ST)
        if adjoint:
            dn = (((1,), (1,)), ((), ()))
            bur[:, sl] = lax.dot_general(xb, w_re[gb], dn, preferred_element_type=F32)
            bui[:, sl] = -lax.dot_general(xb, w_im[gb], dn, preferred_element_type=F32)
        else:
            bur[:, sl] = jnp.dot(xb, w_re[gb], preferred_element_type=F32)
            bui[:, sl] = jnp.dot(xb, w_im[gb], preferred_element_type=F32)


def s5_scan(act, w_re, w_im, a, init, *, reverse, adjoint=False, c_re=None, c_im=None, add=None,
            want_ckpt=False, name):
    N = act.shape[0]
    R = math.gcd(ROW_TILE, N)
    nch, jc = N // R, R // N_SEG
    with_out = c_re is not None

    def chunk(i):
        return (nch - 1 - i) if reverse else i

    def body(*refs):
        act_ref, wre_ref, wim_ref, a_ref, init_ref = refs[:5]
        k = 5
        if with_out:
            cre_ref, cim_ref = refs[k:k + 2]
            k += 2
        if add is not None:
            add_ref = refs[k]
            k += 1
        if with_out:
            out_ref = refs[k]
            k += 1
        if want_ckpt:
            ck_ref = refs[k]
            k += 1
        fin_ref, bur, bui = refs[k:k + 3]

        @pl.when(pl.program_id(0) == 0)
        def _():
            fin_ref[...] = init_ref[...]

        if want_ckpt:
            ck_ref[0] = fin_ref[...]
        _project_in(act_ref[...].astype(BF16), wre_ref, wim_ref, bur, bui, adjoint)
        _scan_chunk(bur, bui, fin_ref, a_ref, jc, reverse)
        if with_out:
            for gb in range(N_BLOCKS):
                sl = slice(gb * BLK_ST, (gb + 1) * BLK_ST)
                y = (jnp.dot(bur[:, sl].astype(BF16), cre_ref[gb], preferred_element_type=F32)
                     - jnp.dot(bui[:, sl].astype(BF16), cim_ref[gb], preferred_element_type=F32))
                cs = slice(gb * BLK_CH, (gb + 1) * BLK_CH)
                if add is not None:
                    y = y + add_ref[:, cs]
                out_ref[:, cs] = y

    row_spec = pl.BlockSpec((R, D_MODEL), lambda i: (chunk(i), 0))
    w_spec = pl.BlockSpec(w_re.shape, lambda i: (0, 0, 0))
    st_spec = pl.BlockSpec((2, N_SEG, S5_LANES), lambda i: (0, 0, 0))
    ins = [act, w_re, w_im, a, init]
    in_specs = [row_spec, w_spec, w_spec, pl.BlockSpec((2, 1, S5_LANES), lambda i: (0, 0, 0)), st_spec]
    if with_out:
        ins += [c_re, c_im]
        in_specs += [pl.BlockSpec(c_re.shape, lambda i: (0, 0, 0))] * 2
    if add is not None:
        ins.append(add)
        in_specs.append(row_spec)
    out_shape, out_specs = [], []
    if with_out:
        out_shape.append(jax.ShapeDtypeStruct((N, D_MODEL), F32))
        out_specs.append(row_spec)
    if want_ckpt:
        out_shape.append(jax.ShapeDtypeStruct((nch, 2, N_SEG, S5_LANES), F32))
        out_specs.append(pl.BlockSpec((1, 2, N_SEG, S5_LANES), lambda i: (chunk(i), 0, 0, 0)))
    out_shape.append(jax.ShapeDtypeStruct((2, N_SEG, S5_LANES), F32))
    out_specs.append(st_spec)
    res = pl.pallas_call(
        body, out_shape=out_shape, grid=(nch,), in_specs=in_specs, out_specs=out_specs,
        scratch_shapes=[pltpu.VMEM((R, S5_LANES), F32), pltpu.VMEM((R, S5_LANES), F32)],
        compiler_params=_params(("arbitrary",), VMEM_LIMIT), name=name)(*ins)
    res = list(res)
    out = res.pop(0) if with_out else None
    ckpt = res.pop(0) if want_ckpt else None
    return out, ckpt, res[0]


def s5_grads(dy, u, ckpt, b_re, b_im, c_re, c_im, lam, init_adj, *, reverse, add=None, name):
    N = u.shape[0]
    R = math.gcd(ROW_TILE, N)
    nch, jc = N // R, R // N_SEG
    W = S5_LANES

    def chunk(i):
        return i if reverse else (nch - 1 - i)

    def body(*refs):
        dy_ref, u_ref, ck_ref, bre_ref, bim_ref, cre_ref, cim_ref, lam_ref, init_ref = refs[:9]
        k = 9
        if add is not None:
            add_ref = refs[k]
            k += 1
        du_ref, dlam_ref, dbre_ref, dbim_ref, dcre_ref, dcim_ref, fin_ref = refs[k:k + 7]
        sr_buf, si_buf, er_buf, ei_buf, st_buf = refs[k + 7:k + 12]

        @pl.when(pl.program_id(0) == 0)
        def _():
            fin_ref[...] = init_ref[...]
            dlam_ref[...] = jnp.zeros_like(dlam_ref)
            dbre_ref[...] = jnp.zeros_like(dbre_ref)
            dbim_ref[...] = jnp.zeros_like(dbim_ref)
            dcre_ref[...] = jnp.zeros_like(dcre_ref)
            dcim_ref[...] = jnp.zeros_like(dcim_ref)

        u16 = u_ref[...].astype(BF16)
        dy16 = dy_ref[...].astype(BF16)
        st_buf[...] = ck_ref[0]
        _project_in(u16, bre_ref, bim_ref, sr_buf, si_buf, False)
        _scan_chunk(sr_buf, si_buf, st_buf, lam_ref, jc, reverse)
        _project_in(dy16, cre_ref, cim_ref, er_buf, ei_buf, True)
        for lc in range(W // BLK_ST):
            sl = slice(lc * BLK_ST, (lc + 1) * BLK_ST)
            lr = jnp.broadcast_to(lam_ref[0, :, sl], (N_SEG, BLK_ST))
            li = jnp.broadcast_to(lam_ref[1, :, sl], (N_SEG, BLK_ST))

            def one(r0, spr, spi, carry, sl=sl, lr=lr, li=li):
                gr, gi, ar, ai = carry
                nr = er_buf[pl.ds(r0, N_SEG), sl] + lr * gr + li * gi
                ni = ei_buf[pl.ds(r0, N_SEG), sl] + lr * gi - li * gr
                er_buf[pl.ds(r0, N_SEG), sl] = nr
                ei_buf[pl.ds(r0, N_SEG), sl] = ni
                return nr, ni, ar + spr * nr + spi * ni, ai + spr * ni - spi * nr

            def step(ff, carry, sl=sl, one=one):
                f = jc - 1 - ff
                j = (jc - 1 - f) if reverse else f
                jp = (j + 1) if reverse else (j - 1)
                r0 = pl.multiple_of(j * N_SEG, N_SEG)
                p0 = pl.multiple_of(jp * N_SEG, N_SEG)
                return one(r0, sr_buf[pl.ds(p0, N_SEG), sl], si_buf[pl.ds(p0, N_SEG), sl], carry)

            carry = (fin_ref[0, :, sl], fin_ref[1, :, sl], dlam_ref[0, :, sl], dlam_ref[1, :, sl])
            carry = lax.fori_loop(0, jc - 1, step, carry)
            r_first = (jc - 1) * N_SEG if reverse else 0
            gr, gi, ar, ai = one(r_first, ck_ref[0, 0, :, sl], ck_ref[0, 1, :, sl], carry)
            fin_ref[0, :, sl] = gr
            fin_ref[1, :, sl] = gi
            dlam_ref[0, :, sl] = ar
            dlam_ref[1, :, sl] = ai
        tn = (((0,), (0,)), ((), ()))
        nt = (((1,), (1,)), ((), ()))
        for gb in range(N_BLOCKS):
            sl = slice(gb * BLK_ST, (gb + 1) * BLK_ST)
            cs = slice(gb * BLK_CH, (gb + 1) * BLK_CH)
            gr16 = er_buf[:, sl].astype(BF16)
            gi16 = ei_buf[:, sl].astype(BF16)
            du = (lax.dot_general(gr16, bre_ref[gb], nt, preferred_element_type=F32)
                  + lax.dot_general(gi16, bim_ref[gb], nt, preferred_element_type=F32))
            if add is not None:
                du = du + add_ref[:, cs]
            du_ref[:, cs] = du
            ub, dyb = u16[:, cs], dy16[:, cs]
            dbre_ref[gb] += lax.dot_general(ub, gr16, tn, preferred_element_type=F32)
            dbim_ref[gb] += lax.dot_general(ub, gi16, tn, preferred_element_type=F32)
            dcre_ref[gb] += lax.dot_general(sr_buf[:, sl].astype(BF16), dyb, tn, preferred_element_type=F32)
            dcim_ref[gb] -= lax.dot_general(si_buf[:, sl].astype(BF16), dyb, tn, preferred_element_type=F32)

    row_spec = pl.BlockSpec((R, D_MODEL), lambda i: (chunk(i), 0))
    st_spec = pl.BlockSpec((2, N_SEG, W), lambda i: (0, 0, 0))
    wb_spec = pl.BlockSpec(b_re.shape, lambda i: (0, 0, 0))
    wc_spec = pl.BlockSpec(c_re.shape, lambda i: (0, 0, 0))
    ins = [dy, u, ckpt, b_re, b_im, c_re, c_im, lam, init_adj]
    in_specs = [row_spec, row_spec, pl.BlockSpec((1, 2, N_SEG, W), lambda i: (chunk(i), 0, 0, 0)),
                wb_spec, wb_spec, wc_spec, wc_spec, pl.BlockSpec((2, 1, W), lambda i: (0, 0, 0)), st_spec]
    if add is not None:
        ins.append(add)
        in_specs.append(row_spec)
    out_shape = [jax.ShapeDtypeStruct((N, D_MODEL), F32), jax.ShapeDtypeStruct((2, N_SEG, W), F32),
                 jax.ShapeDtypeStruct(b_re.shape, F32), jax.ShapeDtypeStruct(b_re.shape, F32),
                 jax.ShapeDtypeStruct(c_re.shape, F32), jax.ShapeDtypeStruct(c_re.shape, F32),
                 jax.ShapeDtypeStruct((2, N_SEG, W), F32)]
    out_specs = [row_spec, st_spec, wb_spec, wb_spec, wc_spec, wc_spec, st_spec]
    return pl.pallas_call(
        body, out_shape=out_shape, grid=(nch,), in_specs=in_specs, out_specs=out_specs,
        scratch_shapes=[pltpu.VMEM((R, W), F32) for _ in range(4)] + [pltpu.VMEM((2, N_SEG, W), F32)],
        compiler_params=_params(("arbitrary",), VMEM_LIMIT), name=name)(*ins)


def adamw(w, g, m, v, name="adamw"):
    n, d = w.shape
    tm = n if n <= 512 else 256
    assert n % tm == 0
    c1 = 1.0 - ADAM_B1 ** ADAM_STEP
    c2 = 1.0 - ADAM_B2 ** ADAM_STEP

    def body(w_ref, g_ref, m_ref, v_ref, d_ref, nm_ref, nv_ref):
        g_ = g_ref[...]
        m_ = ADAM_B1 * m_ref[...] + (1.0 - ADAM_B1) * g_
        v_ = ADAM_B2 * v_ref[...] + (1.0 - ADAM_B2) * (g_ * g_)
        d_ref[...] = -ADAM_LR * ((m_ / c1) / (jnp.sqrt(v_ / c2) + ADAM_EPS) + ADAM_WD * w_ref[...])
        nm_ref[...] = m_
        nv_ref[...] = v_

    spec = pl.BlockSpec((tm, d), lambda i: (i, 0))
    return pl.pallas_call(
        body, out_shape=[jax.ShapeDtypeStruct((n, d), F32)] * 3, grid=(n // tm,),
        in_specs=[spec] * 4, out_specs=[spec] * 3,
        compiler_params=_params(("parallel",), VMEM_LIMIT), name=name)(w, g, m, v)


def _coords():
    return lax.axis_index("x"), lax.axis_index("y"), lax.axis_index("c")


def exchange(src, n_out_slots, relations, send_slot, recv_slot, local, name):
    nrel = len(relations)
    slot_shape = src.shape[1:]

    def body(src_ref, dst_ref, send_sems, recv_sems, local_sem):
        me = _coords()
        copies = []
        for k, (dx, dy, dc) in enumerate(relations):
            peer = (me[0] ^ dx, me[1] ^ dy, me[2] ^ dc)
            copies.append(pltpu.make_async_remote_copy(
                src_ref=src_ref.at[send_slot(me, peer)], dst_ref=dst_ref.at[recv_slot(me)],
                send_sem=send_sems.at[k], recv_sem=recv_sems.at[k], device_id=peer, device_id_type=MESH))
        if local is not None:
            mine = pltpu.make_async_copy(src_ref.at[local[0](me)], dst_ref.at[local[1](me)], local_sem)
            mine.start()
        for cp in copies:
            cp.start()
        for k, (dx, dy, dc) in enumerate(relations):
            peer = (me[0] ^ dx, me[1] ^ dy, me[2] ^ dc)
            pltpu.make_async_remote_copy(
                src_ref=src_ref.at[send_slot(me, peer)], dst_ref=dst_ref.at[recv_slot(peer)],
                send_sem=send_sems.at[k], recv_sem=recv_sems.at[k], device_id=peer, device_id_type=MESH).wait_recv()
        for cp in copies:
            cp.wait_send()
        if local is not None:
            mine.wait()

    return pl.pallas_call(
        body, out_shape=jax.ShapeDtypeStruct((n_out_slots,) + slot_shape, src.dtype),
        in_specs=[pl.BlockSpec(memory_space=pl.ANY)], out_specs=pl.BlockSpec(memory_space=pl.ANY),
        scratch_shapes=[pltpu.SemaphoreType.DMA((nrel,)), pltpu.SemaphoreType.DMA((nrel,)),
                        pltpu.SemaphoreType.DMA],
        name=name)(src)


ALL_FLIPS = [(dx, dy, dc) for dx in (0, 1) for dy in (0, 1) for dc in (0, 1)][1:]
CHIP_FLIPS = [(1, 0, 0), (0, 1, 0), (1, 1, 0)]


def _dev_index(p):
    return 4 * p[0] + 2 * p[1] + p[2]


def _chip_index(p):
    return 2 * p[0] + p[1]


def allgather_devices(x, name):
    return exchange(x[None], N_DEV, ALL_FLIPS, lambda me, peer: 0, _dev_index,
                    (lambda me: 0, _dev_index), name)


def allgather_chips(x, name):
    return exchange(x[None], N_CHIP, CHIP_FLIPS, lambda me, peer: 0, _chip_index,
                    (lambda me: 0, _chip_index), name)


def scatter_chips(x, name):
    return exchange(x, N_CHIP, CHIP_FLIPS, lambda me, peer: _chip_index(peer), _chip_index,
                    (_chip_index, _chip_index), name)


def swap_cores(x, name):
    return exchange(x[None], 1, [(0, 0, 1)], lambda me, peer: 0, lambda sender: 0, None, name)[0]


def add_pair(a, b, name):
    n, d = a.shape
    tm = math.gcd(2048, n)

    def body(a_ref, b_ref, o_ref):
        o_ref[...] = a_ref[...] + b_ref[...]

    spec = pl.BlockSpec((tm, d), lambda i: (i, 0))
    return pl.pallas_call(
        body, out_shape=jax.ShapeDtypeStruct((n, d), F32), grid=(n // tm,), in_specs=[spec, spec], out_specs=spec,
        compiler_params=_params(("parallel",), VMEM_LIMIT), name=name)(a, b)


def sum_slots(x, name):
    S, n, d = x.shape
    tm = math.gcd(2048, n)

    def body(x_ref, o_ref):
        acc = x_ref[0]
        for s in range(1, S):
            acc = acc + x_ref[s]
        o_ref[...] = acc

    return pl.pallas_call(
        body, out_shape=jax.ShapeDtypeStruct((n, d), F32), grid=(n // tm,),
        in_specs=[pl.BlockSpec((S, tm, d), lambda i: (0, i, 0))], out_specs=pl.BlockSpec((tm, d), lambda i: (i, 0)),
        compiler_params=_params(("parallel",), VMEM_LIMIT), name=name)(x)


def to_segments(a, n_ctx):
    def one(p):
        n = p.shape[0]
        return p.reshape(N_SEG, n // N_SEG, -1).transpose(1, 0, 2).reshape(n, -1)
    return jnp.concatenate([one(a[:n_ctx]), one(a[n_ctx:])], axis=0) if n_ctx else one(a)


def from_segments(a, n_ctx):
    def one(p):
        n = p.shape[0]
        return p.reshape(n // N_SEG, N_SEG, -1).transpose(1, 0, 2).reshape(n, -1)
    return jnp.concatenate([one(a[:n_ctx]), one(a[n_ctx:])], axis=0) if n_ctx else one(a)


def rope_tables(n_ctx, n_lat):
    rows = n_lat // GRID_W
    row = jnp.repeat(jnp.arange(rows, dtype=jnp.int32), GRID_W)
    col = jnp.tile(jnp.arange(GRID_W, dtype=jnp.int32), rows)
    d = QK_ROPE_DIM // 2
    inv = 1.0 / (ROPE_THETA ** (jnp.arange(0, d, 2, dtype=F32) / d))
    ang = jnp.concatenate([row.astype(F32)[:, None] * inv[None, :], col.astype(F32)[:, None] * inv[None, :]], axis=1)
    cos = jnp.concatenate([jnp.ones((n_ctx, d), F32), jnp.cos(ang)], axis=0)
    sin = jnp.concatenate([jnp.zeros((n_ctx, d), F32), jnp.sin(ang)], axis=0)
    return cos, sin


def rope(x, cos, sin):
    q = QK_ROPE_DIM // 4
    a, b, c, d = x[..., :q], x[..., q:2 * q], x[..., 2 * q:3 * q], x[..., 3 * q:]
    cr, cc = cos[:, :q], cos[:, q:]
    sr, sc = sin[:, :q], sin[:, q:]
    return jnp.concatenate([a * cr - b * sr, a * sr + b * cr, c * cc - d * sc, c * sc + d * cc], axis=-1)


def heads_layout(q, kv, kr, cos, sin):
    T = q.shape[0]
    qh = q.reshape(T, MLA_HEADS, QK_DIM).transpose(1, 0, 2)
    qh = jnp.concatenate([qh[..., :QK_NOPE_DIM], rope(qh[..., QK_NOPE_DIM:], cos, sin)], axis=-1)
    kvh = kv.reshape(T, MLA_HEADS, QK_NOPE_DIM + V_HEAD_DIM).transpose(1, 0, 2)
    krr = jnp.broadcast_to(rope(kr, cos, sin)[None], (MLA_HEADS, T, QK_ROPE_DIM))
    kh = jnp.concatenate([kvh[..., :QK_NOPE_DIM], krr], axis=-1)
    return qh, kh, kvh[..., QK_NOPE_DIM:]


def s5_discretise(a_re, a_im, log_step, b_re, b_im):
    dt = jnp.exp(log_step)[:, None]
    mag = jnp.exp(a_re * dt)
    lb_re = mag * jnp.cos(a_im * dt)
    lb_im = mag * jnp.sin(a_im * dt)
    den = a_re * a_re + a_im * a_im
    nr = lb_re - 1.0
    f_re = ((nr * a_re + lb_im * a_im) / den)[..., None]
    f_im = ((lb_im * a_re - nr * a_im) / den)[..., None]
    return lb_re, lb_im, f_re * b_re - f_im * b_im, f_re * b_im + f_im * b_re


def s5_block_weights(lb_re, lb_im, bb_re, bb_im, c_re, c_im):
    eye = jnp.eye(GROUPS_PER_BLOCK, dtype=F32)
    lam = jnp.stack([lb_re.reshape(1, S5_LANES), lb_im.reshape(1, S5_LANES)])

    def b_blocks(bb):
        t = bb.reshape(N_BLOCKS, GROUPS_PER_BLOCK, S5_STATE, S5_GROUP)
        return jnp.einsum("bgpc,gh->bgchp", t, eye).reshape(N_BLOCKS, BLK_CH, BLK_ST).astype(BF16)

    def c_blocks(cc):
        t = cc.reshape(N_BLOCKS, GROUPS_PER_BLOCK, S5_GROUP, S5_STATE)
        return jnp.einsum("bgcp,gh->bgphc", t, eye).reshape(N_BLOCKS, BLK_ST, BLK_CH).astype(BF16)

    return lam, b_blocks(bb_re), b_blocks(bb_im), c_blocks(c_re), c_blocks(c_im)


def b_block_diag(db):
    t = db.reshape(N_BLOCKS, GROUPS_PER_BLOCK, S5_GROUP, GROUPS_PER_BLOCK, S5_STATE)
    return jnp.einsum("bgchp,gh->bgpc", t, jnp.eye(GROUPS_PER_BLOCK, dtype=F32)).reshape(S5_GROUPS, S5_STATE, S5_GROUP)


def c_block_diag(dc):
    t = dc.reshape(N_BLOCKS, GROUPS_PER_BLOCK, S5_STATE, GROUPS_PER_BLOCK, S5_GROUP)
    return jnp.einsum("bgphc,gh->bgcp", t, jnp.eye(GROUPS_PER_BLOCK, dtype=F32)).reshape(S5_GROUPS, S5_GROUP, S5_STATE)


def conj(a):
    return jnp.stack([a[0], -a[1]])


PACK_TILE = 16 * 128
REDUCE_ROWS = 2048


def pack_flat(parts, dtype):
    flat = [p.reshape(-1).astype(dtype) for p in parts]
    sizes = [f.shape[0] for f in flat]
    total = sum(sizes)
    pad = (-total) % PACK_TILE
    if pad:
        flat.append(jnp.zeros((pad,), dtype))
    offs = np.cumsum([0] + sizes)[:-1].tolist()
    return jnp.concatenate(flat).reshape(-1, 128), offs


def unpack_flat(buf, offs, shapes):
    flat = buf.reshape(-1)
    return [flat[o:o + int(np.prod(s))].reshape(s) for o, s in zip(offs, shapes)]


def f32_as_bf16_pairs(x):
    return lax.bitcast_convert_type(x.astype(F32), BF16).reshape(-1)


def bf16_pairs_as_f32(x):
    return lax.bitcast_convert_type(x.reshape(-1, 2), F32)


def s5_forward(u_p, n_ctx, dirs):
    saved = []
    y = None
    u_c, u_l = u_p[:n_ctx], u_p[n_ctx:]
    zeros_tile = jnp.zeros((2, N_SEG, S5_LANES), F32)
    zeros_row = jnp.zeros((2, 1, S5_LANES), F32)
    for k, (lam, b_re, b_im, c_re, c_im) in enumerate(dirs):
        rev = k == 1
        last = 0 if rev else N_SEG - 1
        _, _, fin = s5_scan(u_c, b_re, b_im, lam, zeros_tile, reverse=rev, name=f"s5_ctx_finals{k}")
        carry_c = s5_chain(fin, zeros_row, lam, n_ctx // N_SEG, rev, name=f"s5_ctx_chain{k}")
        _, ck_c, fin_c = s5_scan(u_c, b_re, b_im, lam, carry_c, reverse=rev, want_ckpt=True, name=f"s5_ctx_scan{k}")
        s0 = fin_c[:, last:last + 1, :]
        _, _, fin = s5_scan(u_l, b_re, b_im, lam, zeros_tile, reverse=rev, name=f"s5_lat_finals{k}")
        carry_l = s5_chain(fin, s0, lam, u_l.shape[0] // N_SEG, rev, name=f"s5_lat_chain{k}")
        y, ck_l, _ = s5_scan(u_l, b_re, b_im, lam, carry_l, reverse=rev, c_re=c_re, c_im=c_im, add=y,
                             want_ckpt=True, name=f"s5_lat_scan{k}")
        saved.append((ck_c, ck_l))
    return y, saved


def s5_backward(dy_l, du_extra_l, u_p, n_ctx, dirs, saved):
    u_c, u_l = u_p[:n_ctx], u_p[n_ctx:]
    zeros_tile = jnp.zeros((2, N_SEG, S5_LANES), F32)
    zeros_row = jnp.zeros((2, 1, S5_LANES), F32)
    dy_c = jnp.zeros((n_ctx, D_MODEL), F32)
    du_l, du_c = du_extra_l, None
    grads = []
    for k, (lam, b_re, b_im, c_re, c_im) in enumerate(dirs):
        rev = k == 1
        lam_c = conj(lam)
        ck_c, ck_l = saved[k]
        first = N_SEG - 1 if rev else 0
        _, _, fin = s5_scan(dy_l, c_re, c_im, lam_c, zeros_tile, reverse=not rev, adjoint=True,
                            name=f"s5_lat_adj_finals{k}")
        carry = s5_chain(fin, zeros_row, lam_c, u_l.shape[0] // N_SEG, not rev, name=f"s5_lat_adj_chain{k}")
        du_l, dlam_l, dbr_l, dbi_l, dcr_l, dci_l, fin_a = s5_grads(
            dy_l, u_l, ck_l, b_re, b_im, c_re, c_im, lam, carry, reverse=rev, add=du_l, name=f"s5_lat_grads{k}")
        g0 = fin_a[:, first:first + 1, :]
        carry = s5_chain(zeros_tile, g0, lam_c, n_ctx // N_SEG, not rev, name=f"s5_ctx_adj_chain{k}")
        du_c, dlam_c, dbr_c, dbi_c, _, _, _ = s5_grads(
            dy_c, u_c, ck_c, b_re, b_im, c_re, c_im, lam, carry, reverse=rev, add=du_c, name=f"s5_ctx_grads{k}")
        dlam = jnp.sum(dlam_l + dlam_c, axis=1)
        grads.append((dlam, b_block_diag(dbr_l + dbr_c), b_block_diag(dbi_l + dbi_c),
                      c_block_diag(dcr_l), c_block_diag(dci_l)))
    return jnp.concatenate([du_c, du_l], axis=0), grads


def local_step(x, ctx, target, mod, w):
    L, Lc = x.shape[0], ctx.shape[0]
    T = L + Lc
    assert L % Lc == 0 and Lc % (2 * N_SEG) == 0 and L % GRID_W == 0
    D = D_MODEL
    X0 = jnp.concatenate([ctx, x], axis=0)

    def mod_of(i, j):
        return mod[i, :, j, :][:, None, :]

    def vec(v):
        return v.reshape(1, 1, -1).astype(F32)

    g0 = vec(w["norm_g"][0])
    (H0,) = rowwise_fwd(f_norm_mod, [X0], [g0, mod_of(0, 1), mod_of(0, 0)], [D], [BF16], T, Lc, "l0_norm")
    p0 = mm_nn(H0, w["mla_w_in"], name="l0_in")
    o1, o2 = Q_LORA_RANK, Q_LORA_RANK + KV_LORA_RANK
    o3 = o2 + QK_ROPE_DIM
    cq, ckv, kr, z0 = p0[:, :o1], p0[:, o1:o2], p0[:, o2:o3], p0[:, o3:]
    qng, kvng = vec(w["mla_q_norm"]), vec(w["mla_kv_norm"])
    (qn,) = rowwise_fwd(f_rms, [cq], [qng], [Q_LORA_RANK], [BF16], T, 0, "l0_qnorm")
    (kvn,) = rowwise_fwd(f_rms, [ckv], [kvng], [KV_LORA_RANK], [BF16], T, 0, "l0_kvnorm")
    q = mm_nn(qn, w["mla_w_uq"], name="l0_uq")
    kv = mm_nn(kvn, w["mla_w_ukv"], name="l0_ukv")
    cos, sin = rope_tables(Lc, L)
    (qh, kh, vh), heads_vjp = jax.vjp(lambda q_, kv_, kr_: heads_layout(q_, kv_, kr_, cos, sin), q, kv, kr)
    qb, kb, vb = qh.astype(BF16), kh.astype(BF16), vh.astype(BF16)
    oh, lse = attn_fwd(qb, kb, vb, Lc)
    o = oh.transpose(1, 0, 2).reshape(T, MLA_HEADS * V_HEAD_DIM)
    (og,) = rowwise_fwd(f_gate, [o, z0], [], [D], [BF16], T, 0, "l0_gate")
    out0 = mm_nn(og, w["mla_w_out"], name="l0_out")
    (X1,) = rowwise_fwd(f_res, [X0, out0], [mod_of(0, 2)], [D], [F32], T, Lc, "l0_res")

    X1p = to_segments(X1, Lc)
    tgt_p = to_segments(target, 0)
    g1 = vec(w["norm_g"][1])
    (H1,) = rowwise_fwd(f_norm_mod, [X1p], [g1, mod_of(1, 1), mod_of(1, 0)], [D], [BF16], T, Lc, "l1_norm")
    p1 = mm_nn(H1, w["s5_w_in"], name="l1_in")
    u_p = p1[:, :D]
    disc_fn = lambda *a: tuple(zip(*[s5_discretise(a[0][k], a[1][k], a[2][k], a[3][k], a[4][k]) for k in range(2)]))
    disc, disc_vjp = jax.vjp(disc_fn, w["s5_a_re"], w["s5_a_im"], w["s5_log_step"], w["s5_b_re"], w["s5_b_im"])
    dirs = [s5_block_weights(disc[0][k], disc[1][k], disc[2][k], disc[3][k], w["s5_c_re"][k], w["s5_c_im"][k])
            for k in range(2)]
    y_ssm, s5_saved = s5_forward(u_p, Lc, dirs)
    d_vec, bg_vec = vec(w["s5_d"]), vec(w["s5_b_glu"])
    u_lat = Rows(p1, D, row_off=Lc, col_blk=0)
    z_lat = Rows(p1, D, row_off=Lc, col_blk=1)
    (ya,) = rowwise_fwd(f_s5_act, [y_ssm, u_lat], [d_vec], [D], [F32], L, 0, "l1_act")
    gl = mm_nn(ya, w["s5_w_glu"], name="l1_glu")
    (y3,) = rowwise_fwd(f_s5_glu, [ya, gl, z_lat], [bg_vec], [D], [BF16], L, 0, "l1_gate")
    out1 = mm_nn(y3, w["s5_w_out"], name="l1_out")
    gt1 = mod[1, 1:2, 2, :][:, None, :]
    x1_lat = Rows(X1p, D, row_off=Lc)
    (X2,) = rowwise_fwd(f_res, [x1_lat, out1], [gt1], [D], [F32], L, 0, "l1_res")

    fg = w["final_g"].reshape(1, D).astype(F32)
    lvec, dX2, d_fg = loss_and_grad(X2, fg, tgt_p)
    loss = jnp.sum(lvec)
    gw = {"final_g": d_fg.reshape(D)}
    dmod = {}

    d_out1, d_gt1 = rowwise_bwd(f_res, [x1_lat, out1], [gt1], [dX2], [1], [0], L, 0, "l1_res_bwd")
    d_y3 = mm_nt(d_out1, w["s5_w_out"], name="l1_out_dx")
    gw["s5_w_out"] = mm_tn(y3, d_out1, name="l1_out_dw")
    d_ya_a, d_gl, d_z1, d_bg = rowwise_bwd(f_s5_glu, [ya, gl, z_lat], [bg_vec], [d_y3], [0, 1, 2], [0], L, 0,
                                           "l1_gate_bwd")
    gw["s5_b_glu"] = d_bg.reshape(-1)
    gw["s5_w_glu"] = mm_tn(ya, d_gl, name="l1_glu_dw")
    d_ya = d_ya_a + mm_nt(d_gl, w["s5_w_glu"], name="l1_glu_dx")
    d_yssm, d_u_act, d_d = rowwise_bwd(f_s5_act, [y_ssm, u_lat], [d_vec], [d_ya], [0, 1], [0], L, 0, "l1_act_bwd")
    gw["s5_d"] = d_d.reshape(-1)
    du_p, s5_g = s5_backward(d_yssm, d_u_act, u_p, Lc, dirs, s5_saved)
    d_disc = tuple(tuple(s5_g[k][j - 1].reshape(disc[j][k].shape) if j >= 2 else
                         s5_g[k][0][j].reshape(disc[j][k].shape) for k in range(2)) for j in range(4))
    gw["s5_a_re"], gw["s5_a_im"], gw["s5_log_step"], gw["s5_b_re"], gw["s5_b_im"] = disc_vjp(d_disc)
    gw["s5_c_re"] = jnp.stack([s5_g[0][3], s5_g[1][3]])
    gw["s5_c_im"] = jnp.stack([s5_g[0][4], s5_g[1][4]])
    d_p1 = jnp.concatenate([du_p, jnp.concatenate([jnp.zeros((Lc, D), F32), d_z1], axis=0)], axis=1)
    d_H1 = mm_nt(d_p1, w["s5_w_in"], name="l1_in_dx")
    gw["s5_w_in"] = mm_tn(H1, d_p1, name="l1_in_dw")
    d_X1p, d_g1, d_sc1, d_sh1 = rowwise_bwd(f_norm_mod, [X1p], [g1, mod_of(1, 1), mod_of(1, 0)], [d_H1],
                                            [0], [0, 1, 2], T, Lc, "l1_norm_bwd")
    d_X1p = d_X1p + jnp.concatenate([jnp.zeros((Lc, D), F32), dX2], axis=0)
    d_gt1_full = jnp.concatenate([jnp.zeros((1, 1, D), F32), d_gt1], axis=0)
    dmod[1] = (d_sh1, d_sc1, d_gt1_full)
    d_X1 = from_segments(d_X1p, Lc)

    d_out0, d_gt0 = rowwise_bwd(f_res, [X0, out0], [mod_of(0, 2)], [d_X1], [1], [0], T, Lc, "l0_res_bwd")
    d_og = mm_nt(d_out0, w["mla_w_out"], name="l0_out_dx")
    gw["mla_w_out"] = mm_tn(og, d_out0, name="l0_out_dw")
    d_o, d_z0 = rowwise_bwd(f_gate, [o, z0], [], [d_og], [0, 1], [], T, 0, "l0_gate_bwd")
    d_oh = d_o.reshape(T, MLA_HEADS, V_HEAD_DIM).transpose(1, 0, 2)
    tq = math.gcd(ROW_TILE, Lc)
    dqh, delta = attn_bwd_dq(qb, kb, vb, oh, d_oh, lse, Lc)
    dkh, dvh = attn_bwd_dkv(qb, kb, vb, d_oh, lse.reshape(MLA_HEADS, T // tq, 1, tq),
                            delta.reshape(MLA_HEADS, T // tq, 1, tq), Lc)
    d_q, d_kv, d_kr = heads_vjp((dqh, dkh, dvh))
    d_qn = mm_nt(d_q, w["mla_w_uq"], name="l0_uq_dx")
    gw["mla_w_uq"] = mm_tn(qn, d_q, name="l0_uq_dw")
    d_kvn = mm_nt(d_kv, w["mla_w_ukv"], name="l0_ukv_dx")
    gw["mla_w_ukv"] = mm_tn(kvn, d_kv, name="l0_ukv_dw")
    d_cq, d_qng = rowwise_bwd(f_rms, [cq], [qng], [d_qn], [0], [0], T, 0, "l0_qnorm_bwd")
    d_ckv, d_kvng = rowwise_bwd(f_rms, [ckv], [kvng], [d_kvn], [0], [0], T, 0, "l0_kvnorm_bwd")
    gw["mla_q_norm"] = d_qng.reshape(-1)
    gw["mla_kv_norm"] = d_kvng.reshape(-1)
    d_p0 = jnp.concatenate([d_cq, d_ckv, d_kr, d_z0], axis=1)
    d_H0 = mm_nt(d_p0, w["mla_w_in"], name="l0_in_dx")
    gw["mla_w_in"] = mm_tn(H0, d_p0, name="l0_in_dw")
    d_X0, d_g0, d_sc0, d_sh0 = rowwise_bwd(f_norm_mod, [X0], [g0, mod_of(0, 1), mod_of(0, 0)], [d_H0],
                                           [0], [0, 1, 2], T, Lc, "l0_norm_bwd")
    dmod[0] = (d_sh0, d_sc0, d_gt0)
    gw["norm_g"] = jnp.stack([d_g0.reshape(D), d_g1.reshape(D)])
    dx = (d_X0 + d_X1)[Lc:]
    dmod_arr = jnp.stack([jnp.stack([dmod[i][j][:, 0, :] for j in range(3)], axis=1) for i in range(2)])
    return loss, dx, dmod_arr, gw


SHARDED = {
    "mla_w_in": 1, "mla_w_uq": 1, "mla_w_ukv": 1, "mla_w_out": 0,
    "s5_w_in": 1, "s5_w_glu": 0, "s5_w_out": 0, "s5_d": 0, "s5_b_glu": 0,
}
SHARDED_MATS = ["mla_w_in", "mla_w_uq", "mla_w_ukv", "mla_w_out", "s5_w_in", "s5_w_glu", "s5_w_out"]
SHARDED_VECS = ["s5_d", "s5_b_glu"]
REPLICATED = ["norm_g", "mla_q_norm", "mla_kv_norm", "s5_a_re", "s5_a_im", "s5_log_step", "s5_b_re", "s5_b_im",
              "s5_c_re", "s5_c_im", "final_g"]
WEIGHT_ORDER = ["c_ctx", "ada_w", "ada_b", "norm_g", "mla_w_in", "mla_q_norm", "mla_w_uq", "mla_kv_norm", "mla_w_ukv",
                "mla_w_out", "s5_w_in", "s5_a_re", "s5_a_im", "s5_log_step", "s5_b_re", "s5_b_im", "s5_c_re", "s5_c_im",
                "s5_d", "s5_w_glu", "s5_b_glu", "s5_w_out", "final_g"]


def gather_weights(ws):
    parts = [ws[n] for n in SHARDED_MATS]
    vec_parts = [f32_as_bf16_pairs(ws[n]) for n in SHARDED_VECS]
    buf, offs = pack_flat([p.astype(BF16) for p in parts] + vec_parts, BF16)
    allb = allgather_chips(buf, "gather_weights")
    shapes = [p.shape for p in parts] + [v.shape for v in vec_parts]
    per_chip = [unpack_flat(allb[q], offs, shapes) for q in range(N_CHIP)]
    full = {}
    for k, n in enumerate(SHARDED_MATS):
        full[n] = jnp.concatenate([per_chip[q][k] for q in range(N_CHIP)], axis=SHARDED[n])
    for k, n in enumerate(SHARDED_VECS):
        full[n] = jnp.concatenate([bf16_pairs_as_f32(per_chip[q][len(SHARDED_MATS) + k]) for q in range(N_CHIP)])
    return full


def reduce_gradients(gw):
    c = lax.axis_index("c")
    per_chip = []
    rep_parts = [gw[n].astype(F32) for n in REPLICATED]
    rep_flat = jnp.concatenate([p.reshape(-1) for p in rep_parts])
    rep_pad = (-rep_flat.shape[0]) % (N_CHIP * 2 * PACK_TILE)
    rep_flat = jnp.pad(rep_flat, (0, rep_pad)).reshape(N_CHIP, -1)
    offs = shapes = None
    for q in range(N_CHIP):
        parts = []
        for n in SHARDED_MATS + SHARDED_VECS:
            g = gw[n]
            size = g.shape[SHARDED[n]] // N_CHIP
            parts.append(lax.slice_in_dim(g, q * size, (q + 1) * size, axis=SHARDED[n]))
        shapes = [p.shape for p in parts]
        buf, offs = pack_flat(parts + [rep_flat[q]], F32)
        rows = buf.shape[0]
        if rows % REDUCE_ROWS:
            buf = jnp.pad(buf, ((0, (-rows) % REDUCE_ROWS), (0, 0)))
        per_chip.append(buf)
    G = jnp.stack(per_chip)
    n = G.shape[1]
    halves = G.reshape(N_CHIP, 2, n // 2, 128)
    mine = lax.dynamic_index_in_dim(halves, c, axis=1, keepdims=False)
    theirs = lax.dynamic_index_in_dim(halves, 1 - c, axis=1, keepdims=False)
    got = swap_cores(theirs, "grads_swap_in")
    chip_sum = add_pair(mine.reshape(-1, 128), got.reshape(-1, 128), "grads_sum_cores").reshape(N_CHIP, n // 2, 128)
    parts4 = scatter_chips(chip_sum, "grads_scatter")
    my_half = sum_slots(parts4, "grads_sum_chips")
    other_half = swap_cores(my_half, "grads_swap_out")
    full = jnp.concatenate([jnp.where(c == 0, my_half, other_half), jnp.where(c == 0, other_half, my_half)], axis=0)
    rep_size = rep_flat.shape[1]
    vals = unpack_flat(full, offs, shapes + [(rep_size,)])
    out = dict(zip(SHARDED_MATS + SHARDED_VECS, vals[:-1]))
    rep_all = allgather_chips(vals[-1].reshape(-1, 128), "grads_gather_replicated").reshape(-1)
    o = 0
    for nme, p in zip(REPLICATED, rep_parts):
        out[nme] = rep_all[o:o + p.size].reshape(p.shape)
        o += p.size
    return out


def kernel(x, c, ctx, c_ctx, ada_w, ada_b, norm_g, mla_w_in, mla_q_norm, mla_w_uq, mla_kv_norm, mla_w_ukv, mla_w_out, s5_w_in, s5_a_re, s5_a_im, s5_log_step, s5_b_re, s5_b_im, s5_c_re, s5_c_im, s5_d, s5_w_glu, s5_b_glu, s5_w_out, final_g, loss_target, m_c_ctx, m_ada_w, m_ada_b, m_norm_g, m_mla_w_in, m_mla_q_norm, m_mla_w_uq, m_mla_kv_norm, m_mla_w_ukv, m_mla_w_out, m_s5_w_in, m_s5_a_re, m_s5_a_im, m_s5_log_step, m_s5_b_re, m_s5_b_im, m_s5_c_re, m_s5_c_im, m_s5_d, m_s5_w_glu, m_s5_b_glu, m_s5_w_out, m_final_g, v_c_ctx, v_ada_w, v_ada_b, v_norm_g, v_mla_w_in, v_mla_q_norm, v_mla_w_uq, v_mla_kv_norm, v_mla_w_ukv, v_mla_w_out, v_s5_w_in, v_s5_a_re, v_s5_a_im, v_s5_log_step, v_s5_b_re, v_s5_b_im, v_s5_c_re, v_s5_c_im, v_s5_d, v_s5_w_glu, v_s5_b_glu, v_s5_w_out, v_final_g):
    args = dict(locals())
    weights = {n: args[n] for n in WEIGHT_ORDER}
    D = D_MODEL
    xi, yi, ci = _coords()
    chip = 2 * xi + yi
    me = 4 * xi + 2 * yi + ci
    n_col = ada_w.shape[2]

    c_all = allgather_devices(jnp.pad(c, ((0, 7), (0, 0))), "gather_c")[:, 0, :]
    cond = jnp.concatenate([c_all, jnp.broadcast_to(c_ctx[None], (8, D))], axis=0)
    (s_cond,) = rowwise_fwd(lambda v: (_silu(v),), [cond], [], [D], [F32], 16, 0, "cond_silu")
    mod_cols = jnp.stack([mm_nn(s_cond, ada_w[i], name=f"mod_proj{i}") for i in range(2)])
    mod_all = allgather_chips(mod_cols, "gather_mod")
    mod_all = mod_all.transpose(1, 2, 0, 3).reshape(2, 16, 3 * D) + ada_b[:, None, :]
    mod_l = lax.dynamic_index_in_dim(mod_all, me, axis=1, keepdims=False)
    mod_c = mod_all[:, 8, :]
    mod = jnp.stack([mod_c.reshape(2, 3, D), mod_l.reshape(2, 3, D)], axis=1)

    shards = {n: weights[n][0] for n in SHARDED_MATS + SHARDED_VECS}
    w = gather_weights(shards)
    for n in ["norm_g", "final_g"]:
        w[n] = weights[n]
    for n in ["mla_q_norm", "mla_kv_norm", "s5_a_re", "s5_a_im", "s5_log_step", "s5_b_re", "s5_b_im",
              "s5_c_re", "s5_c_im"]:
        w[n] = weights[n][0]

    loss_me, dx, dmod, gw = local_step(x[0], ctx[0], loss_target[0], mod, w)
    loss = lax.psum(loss_me, ("x", "y", "c"))

    dmod_rows = allgather_devices(dmod.reshape(2, 2, 3 * D), "gather_dmod")
    dm = jnp.concatenate([dmod_rows[:, :, 1, :], dmod_rows[:, :, 0, :]], axis=0).transpose(1, 0, 2)
    g_ada_b = jnp.sum(dm, axis=1)
    dm_cols = lax.dynamic_slice_in_dim(dm, chip * n_col, n_col, axis=2)
    g_ada_w = jnp.stack([mm_tn(s_cond, dm_cols[i], name=f"mod_proj_dw{i}") for i in range(2)])
    dmc = jnp.sum(dm_cols[:, 8:, :], axis=1)
    dmc8 = jnp.broadcast_to(dmc[:, None, :], (2, 8, n_col))
    g_sc = mm_nt(dmc8[0], ada_w[0], name="mod_proj_dx0")[0] + mm_nt(dmc8[1], ada_w[1], name="mod_proj_dx1")[0]
    g_sc_all = allgather_devices(jnp.broadcast_to(g_sc[None], (8, D)), "gather_dcond")[:, 0, :]
    g_silu_cc = g_sc_all[0] + g_sc_all[2] + g_sc_all[4] + g_sc_all[6]
    (g_c_ctx,) = rowwise_bwd(lambda v: (_silu(v),), [jnp.broadcast_to(c_ctx[None], (8, D))], [],
                             [jnp.broadcast_to(g_silu_cc[None], (8, D))], [0], [], 8, 0, "cond_silu_bwd")
    g_c_ctx = g_c_ctx[0]

    gw_in = {}
    for n in SHARDED_MATS + SHARDED_VECS:
        gw_in[n] = gw[n]
    for n in REPLICATED:
        gw_in[n] = gw[n]
    red = reduce_gradients(gw_in)
    grads = {"c_ctx": g_c_ctx, "ada_w": g_ada_w, "ada_b": g_ada_b}
    for n in WEIGHT_ORDER[3:]:
        grads[n] = red[n].reshape(weights[n].shape)

    deltas, new_m, new_v = {}, {}, {}
    small = [n for n in WEIGHT_ORDER if weights[n].size < 50000]
    for n in WEIGHT_ORDER:
        if n in small:
            continue
        shp = weights[n].shape
        w2 = weights[n].reshape(-1, shp[-1] if shp[-1] >= 256 else 128)
        d_, m_, v_ = adamw(w2, grads[n].reshape(w2.shape), args["m_" + n].reshape(w2.shape),
                           args["v_" + n].reshape(w2.shape), name=f"adamw_{n}")
        deltas[n], new_m[n], new_v[n] = d_.reshape(shp), m_.reshape(shp), v_.reshape(shp)
    packs = []
    offs = None
    for src in (weights, grads, {n: args["m_" + n] for n in small}, {n: args["v_" + n] for n in small}):
        buf, offs = pack_flat([src[n] for n in small], F32)
        packs.append(buf)
    outs = adamw(*packs, name="adamw_small")
    for res, dst in zip(outs, (deltas, new_m, new_v)):
        for n, val in zip(small, unpack_flat(res, offs, [weights[n].shape for n in small])):
            dst[n] = val

    return (loss, dx[None], *[grads[n] for n in WEIGHT_ORDER], *[deltas[n] for n in WEIGHT_ORDER],
            *[new_m[n] for n in WEIGHT_ORDER], *[new_v[n] for n in WEIGHT_ORDER])
```

```python
import functools
import math

import jax
import jax.numpy as jnp
import numpy as np
from jax import lax
from jax.experimental import pallas as pl
from jax.experimental.pallas import tpu as pltpu

F32 = jnp.float32
BF16 = jnp.bfloat16

D_MODEL = 1024
GRID_W = 64
EPS = 1e-6
MLA_HEADS = 16
QK_NOPE_DIM = 64
QK_ROPE_DIM = 32
V_HEAD_DIM = 64
Q_LORA_RANK = 256
KV_LORA_RANK = 128
QK_DIM = QK_NOPE_DIM + QK_ROPE_DIM
SOFTMAX_SCALE = QK_DIM ** -0.5
ROPE_THETA = 10000.0
S5_GROUP = 16
S5_GROUPS = D_MODEL // S5_GROUP
S5_STATE = 64
S5_LANES = S5_GROUPS * S5_STATE
N_SEG = 8
GROUPS_PER_BLOCK = 8
N_BLOCKS = S5_GROUPS // GROUPS_PER_BLOCK
BLK_CH = GROUPS_PER_BLOCK * S5_GROUP
BLK_ST = GROUPS_PER_BLOCK * S5_STATE

ADAM_LR = 0.001
ADAM_B1 = 0.9
ADAM_B2 = 0.999
ADAM_EPS = 1e-08
ADAM_WD = 0.01
ADAM_STEP = 10

N_DEV = 8
N_CHIP = 4
MESH = pl.DeviceIdType.MESH
VMEM_LIMIT = 52 * 1024 * 1024
ROW_TILE = 256


def _params(sem=None, vmem=None):
    return pltpu.CompilerParams(dimension_semantics=sem, vmem_limit_bytes=vmem)


def mm_nn(a, b, out_dtype=F32, name="mm_nn"):
    M, K = a.shape
    N = b.shape[1]
    tm = math.gcd(ROW_TILE, M)

    def body(a_ref, b_ref, o_ref):
        o_ref[...] = jnp.dot(a_ref[...].astype(BF16), b_ref[...].astype(BF16),
                             preferred_element_type=F32).astype(o_ref.dtype)

    return pl.pallas_call(
        body, out_shape=jax.ShapeDtypeStruct((M, N), out_dtype), grid=(M // tm,),
        in_specs=[pl.BlockSpec((tm, K), lambda i: (i, 0)), pl.BlockSpec((K, N), lambda i: (0, 0))],
        out_specs=pl.BlockSpec((tm, N), lambda i: (i, 0)),
        compiler_params=_params(("parallel",), VMEM_LIMIT), name=name)(a, b)


def mm_nt(a, b, out_dtype=F32, name="mm_nt"):
    M, N = a.shape
    K = b.shape[0]
    tm = math.gcd(ROW_TILE, M)

    def body(a_ref, b_ref, o_ref):
        o_ref[...] = lax.dot_general(a_ref[...].astype(BF16), b_ref[...].astype(BF16),
                                     (((1,), (1,)), ((), ())),
                                     preferred_element_type=F32).astype(o_ref.dtype)

    return pl.pallas_call(
        body, out_shape=jax.ShapeDtypeStruct((M, K), out_dtype), grid=(M // tm,),
        in_specs=[pl.BlockSpec((tm, N), lambda i: (i, 0)), pl.BlockSpec((K, N), lambda i: (0, 0))],
        out_specs=pl.BlockSpec((tm, K), lambda i: (i, 0)),
        compiler_params=_params(("parallel",), VMEM_LIMIT), name=name)(a, b)


def mm_tn(a, b, name="mm_tn"):
    M, K = a.shape
    N = b.shape[1]
    tm = math.gcd(2 * ROW_TILE, M)

    def body(a_ref, b_ref, o_ref):
        @pl.when(pl.program_id(0) == 0)
        def _():
            o_ref[...] = jnp.zeros_like(o_ref)

        o_ref[...] += lax.dot_general(a_ref[...].astype(BF16), b_ref[...].astype(BF16),
                                      (((0,), (0,)), ((), ())), preferred_element_type=F32)

    return pl.pallas_call(
        body, out_shape=jax.ShapeDtypeStruct((K, N), F32), grid=(M // tm,),
        in_specs=[pl.BlockSpec((tm, K), lambda i: (i, 0)), pl.BlockSpec((tm, N), lambda i: (i, 0))],
        out_specs=pl.BlockSpec((K, N), lambda i: (0, 0)),
        compiler_params=_params(("arbitrary",), VMEM_LIMIT), name=name)(a, b)


class Rows:
    def __init__(self, arr, width=None, row_off=0, col_blk=0):
        self.arr = arr
        self.width = arr.shape[1] if width is None else width
        self.row_off = row_off
        self.col_blk = col_blk

    def spec(self, tm):
        ro, cb = self.row_off // tm, self.col_blk
        return pl.BlockSpec((tm, self.width), lambda i: (i + ro, cb))


def _as_rows(x):
    return x if isinstance(x, Rows) else Rows(x)


def _row_tile(n_rows, n_ctx_rows, rows):
    tm = math.gcd(ROW_TILE, n_rows, n_ctx_rows)
    for r in rows:
        tm = math.gcd(tm, r.row_off)
    return tm


def _bc_spec(arr, n_ctx_blocks):
    g, _, d = arr.shape
    if g == 1:
        return pl.BlockSpec((1, 1, d), lambda i: (0, 0, 0))
    return pl.BlockSpec((1, 1, d), lambda i: ((i >= n_ctx_blocks).astype(jnp.int32), 0, 0))


def rowwise_fwd(fn, rows, bcs, out_dims, out_dtypes, n_rows, n_ctx_rows, name):
    rows = [_as_rows(r) for r in rows]
    tm = _row_tile(n_rows, n_ctx_rows, rows)
    ncb = n_ctx_rows // tm
    nr, nb = len(rows), len(bcs)

    def body(*refs):
        vals = [r[...].astype(F32) for r in refs[:nr]] + [b[0].astype(F32) for b in refs[nr:nr + nb]]
        outs = fn(*vals)
        for o_ref, v in zip(refs[nr + nb:], outs):
            o_ref[...] = v.astype(o_ref.dtype)

    outs = pl.pallas_call(
        body,
        out_shape=[jax.ShapeDtypeStruct((n_rows, d), dt) for d, dt in zip(out_dims, out_dtypes)],
        grid=(n_rows // tm,),
        in_specs=[r.spec(tm) for r in rows] + [_bc_spec(b, ncb) for b in bcs],
        out_specs=[pl.BlockSpec((tm, d), lambda i: (i, 0)) for d in out_dims],
        compiler_params=_params(("parallel",), VMEM_LIMIT), name=name)(*[r.arr for r in rows], *bcs)
    return outs


def rowwise_bwd(fn, rows, bcs, cts, diff_rows, diff_bcs, n_rows, n_ctx_rows, name):
    rows = [_as_rows(r) for r in rows]
    cts = [_as_rows(c) for c in cts]
    tm = _row_tile(n_rows, n_ctx_rows, rows + cts)
    ncb = n_ctx_rows // tm
    nr, nb, nc = len(rows), len(bcs), len(cts)
    ndr, ndb = len(diff_rows), len(diff_bcs)

    def body(*refs):
        i = pl.program_id(0)
        rvals = [r[...].astype(F32) for r in refs[:nr]]
        bvals = [b[0].astype(F32) for b in refs[nr:nr + nb]]
        cvals = [c[...].astype(F32) for c in refs[nr + nb:nr + nb + nc]]
        outs = refs[nr + nb + nc:]

        def f(*d):
            rv, bv = list(rvals), list(bvals)
            for k, idx in enumerate(diff_rows):
                rv[idx] = d[k]
            for k, idx in enumerate(diff_bcs):
                bv[idx] = d[ndr + k]
            return tuple(fn(*rv, *bv))

        primals = [rvals[k] for k in diff_rows] + [bvals[k] for k in diff_bcs]
        _, vjp = jax.vjp(f, *primals)
        grads = vjp(tuple(cvals))
        for k in range(ndr):
            outs[k][...] = grads[k].astype(outs[k].dtype)
        for k, idx in enumerate(diff_bcs):
            o_ref = outs[ndr + k]
            first = (i == 0)
            if bcs[idx].shape[0] == 2:
                first = first | (i == ncb)

            @pl.when(first)
            def _(o_ref=o_ref):
                o_ref[...] = jnp.zeros_like(o_ref)

            o_ref[0] += grads[ndr + k]

    out_shape = [jax.ShapeDtypeStruct((n_rows, rows[k].width), F32) for k in diff_rows]
    out_shape += [jax.ShapeDtypeStruct(bcs[k].shape, F32) for k in diff_bcs]
    out_specs = [pl.BlockSpec((tm, rows[k].width), lambda i: (i, 0)) for k in diff_rows]
    out_specs += [_bc_spec(bcs[k], ncb) for k in diff_bcs]
    outs = pl.pallas_call(
        body, out_shape=out_shape, grid=(n_rows // tm,),
        in_specs=[r.spec(tm) for r in rows] + [_bc_spec(b, ncb) for b in bcs] + [c.spec(tm) for c in cts],
        out_specs=out_specs,
        compiler_params=_params(("arbitrary",), VMEM_LIMIT), name=name)(
            *[r.arr for r in rows], *bcs, *[c.arr for c in cts])
    return outs


def _rms(x):
    return x * lax.rsqrt(jnp.mean(x * x, axis=-1, keepdims=True) + EPS)


def _sigmoid(x):
    return 0.5 * (jnp.tanh(0.5 * x) + 1.0)


def _silu(x):
    return x * _sigmoid(x)


def _gelu_tanh(x):
    return 0.5 * x * (1.0 + jnp.tanh(math.sqrt(2.0 / math.pi) * (x + 0.044715 * (x * x * x))))


def f_norm_mod(x, g, sc, sh):
    return ((_rms(x) * g) * (1.0 + sc) + sh,)


def f_rms(x, g):
    return (_rms(x) * g,)


def f_gate(o, z):
    return (o * _silu(z),)


def f_res(x, o, gt):
    return (x + gt * o,)


def f_s5_act(y, u, d):
    return (_gelu_tanh(y + d * u),)


def f_s5_glu(ya, gl, z, b):
    return (ya * _sigmoid(gl + b) * _silu(z),)


def loss_and_grad(x2, final_g, target, name="loss_head"):
    n, d = x2.shape
    tm = math.gcd(ROW_TILE, n)

    def row_loss(x, g, t):
        y = _rms(x) * g
        e = y - t
        return 0.5 * (e * e) * (1.0 / d)

    def body(x_ref, g_ref, t_ref, l_ref, dx_ref, dg_ref):
        @pl.when(pl.program_id(0) == 0)
        def _():
            l_ref[...] = jnp.zeros_like(l_ref)
            dg_ref[...] = jnp.zeros_like(dg_ref)

        t = t_ref[...]
        lterm, vjp = jax.vjp(lambda x, g: row_loss(x, g, t), x_ref[...], g_ref[...])
        dx, dg = vjp(jnp.ones_like(lterm))
        l_ref[...] += jnp.sum(lterm, axis=0, keepdims=True)
        dx_ref[...] = dx
        dg_ref[...] += dg

    return pl.pallas_call(
        body,
        out_shape=[jax.ShapeDtypeStruct((1, d), F32), jax.ShapeDtypeStruct((n, d), F32),
                   jax.ShapeDtypeStruct((1, d), F32)],
        grid=(n // tm,),
        in_specs=[pl.BlockSpec((tm, d), lambda i: (i, 0)), pl.BlockSpec((1, d), lambda i: (0, 0)),
                  pl.BlockSpec((tm, d), lambda i: (i, 0))],
        out_specs=[pl.BlockSpec((1, d), lambda i: (0, 0)), pl.BlockSpec((tm, d), lambda i: (i, 0)),
                   pl.BlockSpec((1, d), lambda i: (0, 0))],
        compiler_params=_params(("arbitrary",), VMEM_LIMIT), name=name)(x2, final_g, target)


NT_DIMS = (((1,), (1,)), ((), ()))


def attn_fwd(qb, kb, vb, n_ctx):
    H, T, DK = qb.shape
    DV = vb.shape[-1]
    tq = math.gcd(ROW_TILE, n_ctx)
    nq, ncb = T // tq, n_ctx // tq

    def body(q_ref, k_ref, v_ref, o_ref, lse_ref):
        qi = pl.program_id(1)

        def rows(n_keys):
            k = k_ref[0, :n_keys, :]
            v = v_ref[0, :n_keys, :]
            s = lax.dot_general(q_ref[0], k, NT_DIMS, preferred_element_type=F32) * SOFTMAX_SCALE
            m = jnp.max(s, axis=-1, keepdims=True)
            p = jnp.exp(s - m)
            l = jnp.sum(p, axis=-1, keepdims=True)
            o_ref[0] = jnp.dot(p.astype(BF16), v, preferred_element_type=F32) / l
            lse_ref[0] = m + jnp.log(l)

        pl.when(qi < ncb)(lambda: rows(n_ctx))
        pl.when(qi >= ncb)(lambda: rows(T))

    return pl.pallas_call(
        body,
        out_shape=[jax.ShapeDtypeStruct((H, T, DV), F32), jax.ShapeDtypeStruct((H, T, 1), F32)],
        grid=(H, nq),
        in_specs=[pl.BlockSpec((1, tq, DK), lambda h, i: (h, i, 0)),
                  pl.BlockSpec((1, T, DK), lambda h, i: (h, 0, 0)),
                  pl.BlockSpec((1, T, DV), lambda h, i: (h, 0, 0))],
        out_specs=[pl.BlockSpec((1, tq, DV), lambda h, i: (h, i, 0)),
                   pl.BlockSpec((1, tq, 1), lambda h, i: (h, i, 0))],
        compiler_params=_params(("parallel", "parallel"), VMEM_LIMIT), name="attn_fwd")(qb, kb, vb)


def attn_bwd_dq(qb, kb, vb, o, do, lse, n_ctx):
    H, T, DK = qb.shape
    DV = vb.shape[-1]
    tq = math.gcd(ROW_TILE, n_ctx)
    nq, ncb = T // tq, n_ctx // tq

    def body(q_ref, k_ref, v_ref, o_ref, do_ref, lse_ref, dq_ref, delta_ref):
        qi = pl.program_id(1)

        def rows(n_keys):
            k = k_ref[0, :n_keys, :]
            v = v_ref[0, :n_keys, :]
            do = do_ref[0]
            delta = jnp.sum(do * o_ref[0], axis=-1, keepdims=True)
            s = lax.dot_general(q_ref[0], k, NT_DIMS, preferred_element_type=F32) * SOFTMAX_SCALE
            p = jnp.exp(s - lse_ref[0])
            dp = lax.dot_general(do.astype(BF16), v, NT_DIMS, preferred_element_type=F32)
            ds = p * (dp - delta) * SOFTMAX_SCALE
            dq_ref[0] = jnp.dot(ds.astype(BF16), k, preferred_element_type=F32)
            delta_ref[0] = delta

        pl.when(qi < ncb)(lambda: rows(n_ctx))
        pl.when(qi >= ncb)(lambda: rows(T))

    return pl.pallas_call(
        body,
        out_shape=[jax.ShapeDtypeStruct((H, T, DK), F32), jax.ShapeDtypeStruct((H, T, 1), F32)],
        grid=(H, nq),
        in_specs=[pl.BlockSpec((1, tq, DK), lambda h, i: (h, i, 0)),
                  pl.BlockSpec((1, T, DK), lambda h, i: (h, 0, 0)),
                  pl.BlockSpec((1, T, DV), lambda h, i: (h, 0, 0)),
                  pl.BlockSpec((1, tq, DV), lambda h, i: (h, i, 0)),
                  pl.BlockSpec((1, tq, DV), lambda h, i: (h, i, 0)),
                  pl.BlockSpec((1, tq, 1), lambda h, i: (h, i, 0))],
        out_specs=[pl.BlockSpec((1, tq, DK), lambda h, i: (h, i, 0)),
                   pl.BlockSpec((1, tq, 1), lambda h, i: (h, i, 0))],
        compiler_params=_params(("parallel", "parallel"), VMEM_LIMIT), name="attn_bwd_dq")(
            qb, kb, vb, o, do, lse)


def attn_bwd_dkv(qb, kb, vb, do, lse_rows, delta_rows, n_ctx):
    H, T, DK = qb.shape
    DV = vb.shape[-1]
    tq = math.gcd(ROW_TILE, n_ctx)
    nq, ncb = T // tq, n_ctx // tq

    def body(q_ref, do_ref, lse_ref, delta_ref, k_ref, v_ref, dk_ref, dv_ref):
        kj = pl.program_id(1)

        def cols(first):
            k = k_ref[0]
            v = v_ref[0]
            q = q_ref[0, first:, :]
            do16 = do_ref[0, first:, :].astype(BF16)
            st = lax.dot_general(k, q, NT_DIMS, preferred_element_type=F32) * SOFTMAX_SCALE
            pt = jnp.exp(st - lse_ref[0, :, first:])
            dv_ref[0] = jnp.dot(pt.astype(BF16), do16, preferred_element_type=F32)
            dpt = lax.dot_general(v, do16, NT_DIMS, preferred_element_type=F32)
            dst = pt * (dpt - delta_ref[0, :, first:]) * SOFTMAX_SCALE
            dk_ref[0] = jnp.dot(dst.astype(BF16), q, preferred_element_type=F32)

        pl.when(kj < ncb)(lambda: cols(0))
        pl.when(kj >= ncb)(lambda: cols(n_ctx))

    return pl.pallas_call(
        body,
        out_shape=[jax.ShapeDtypeStruct((H, T, DK), F32), jax.ShapeDtypeStruct((H, T, DV), F32)],
        grid=(H, nq),
        in_specs=[pl.BlockSpec((1, T, DK), lambda h, j: (h, 0, 0)),
                  pl.BlockSpec((1, T, DV), lambda h, j: (h, 0, 0)),
                  pl.BlockSpec((1, 1, T), lambda h, j: (h, 0, 0)),
                  pl.BlockSpec((1, 1, T), lambda h, j: (h, 0, 0)),
                  pl.BlockSpec((1, tq, DK), lambda h, j: (h, j, 0)),
                  pl.BlockSpec((1, tq, DV), lambda h, j: (h, j, 0))],
        out_specs=[pl.BlockSpec((1, tq, DK), lambda h, j: (h, j, 0)),
                   pl.BlockSpec((1, tq, DV), lambda h, j: (h, j, 0))],
        compiler_params=_params(("parallel", "parallel"), VMEM_LIMIT), name="attn_bwd_dkv")(
            qb, do, lse_rows, delta_rows, kb, vb)


def _cmul(ar, ai, br, bi):
    return ar * br - ai * bi, ar * bi + ai * br


def s5_chain(finals, s0, a, n_steps, reverse, name):
    W = finals.shape[-1]
    first = N_SEG - 1 if reverse else 0

    def body(f_ref, s0_ref, a_ref, c_ref):
        pr, pi = jnp.ones((1, W), F32), jnp.zeros((1, W), F32)
        br, bi = a_ref[0], a_ref[1]
        n = n_steps
        while n:
            if n & 1:
                pr, pi = _cmul(pr, pi, br, bi)
            br, bi = _cmul(br, bi, br, bi)
            n >>= 1
        fr, fi = f_ref[0], f_ref[1]
        row = lax.broadcasted_iota(jnp.int32, (N_SEG, W), 0)
        s0r = jnp.broadcast_to(s0_ref[0], (N_SEG, W))
        s0i = jnp.broadcast_to(s0_ref[1], (N_SEG, W))
        cr = jnp.where(row == first, s0r, 0.0)
        ci = jnp.where(row == first, s0i, 0.0)
        shift = N_SEG - 1 if reverse else 1
        for _ in range(N_SEG - 1):
            mr, mi = _cmul(pr, pi, cr, ci)
            tr = pltpu.roll(fr + mr, shift, 0)
            ti = pltpu.roll(fi + mi, shift, 0)
            cr = jnp.where(row == first, s0r, tr)
            ci = jnp.where(row == first, s0i, ti)
        c_ref[0] = cr
        c_ref[1] = ci

    return pl.pallas_call(body, out_shape=jax.ShapeDtypeStruct((2, N_SEG, W), F32), name=name)(finals, s0, a)


def _scan_chunk(bur, bui, st_ref, a_ref, n_steps, reverse):
    for lc in range(S5_LANES // BLK_ST):
        sl = slice(lc * BLK_ST, (lc + 1) * BLK_ST)
        lr = jnp.broadcast_to(a_ref[0, :, sl], (N_SEG, BLK_ST))
        li = jnp.broadcast_to(a_ref[1, :, sl], (N_SEG, BLK_ST))

        def step(jj, carry, sl=sl, lr=lr, li=li):
            sr, si = carry
            j = (n_steps - 1 - jj) if reverse else jj
            r0 = pl.multiple_of(j * N_SEG, N_SEG)
            nr = lr * sr - li * si + bur[pl.ds(r0, N_SEG), sl]
            ni = lr * si + li * sr + bui[pl.ds(r0, N_SEG), sl]
            bur[pl.ds(r0, N_SEG), sl] = nr
            bui[pl.ds(r0, N_SEG), sl] = ni
            return nr, ni

        sr, si = lax.fori_loop(0, n_steps, step, (st_ref[0, :, sl], st_ref[1, :, sl]))
        st_ref[0, :, sl] = sr
        st_ref[1, :, sl] = si


def _project_in(x16, w_re, w_im, bur, bui, adjoint):
    for gb in range(N_BLOCKS):
        xb = x16[:, gb * BLK_CH:(gb + 1) * BLK_CH]
        sl = slice(gb * BLK_ST, (gb + 1) * BLK_ST)
        if adjoint:
            dn = (((1,), (1,)), ((), ()))
            bur[:, sl] = lax.dot_general(xb, w_re[gb], dn, preferred_element_type=F32)
            bui[:, sl] = -lax.dot_general(xb, w_im[gb], dn, preferred_element_type=F32)
        else:
            bur[:, sl] = jnp.dot(xb, w_re[gb], preferred_element_type=F32)
            bui[:, sl] = jnp.dot(xb, w_im[gb], preferred_element_type=F32)


def s5_scan(act, w_re, w_im, a, init, *, reverse, adjoint=False, c_re=None, c_im=None, add=None,
            want_ckpt=False, name):
    N = act.shape[0]
    R = math.gcd(ROW_TILE, N)
    nch, jc = N // R, R // N_SEG
    with_out = c_re is not None

    def chunk(i):
        return (nch - 1 - i) if reverse else i

    def body(*refs):
        act_ref, wre_ref, wim_ref, a_ref, init_ref = refs[:5]
        k = 5
        if with_out:
            cre_ref, cim_ref = refs[k:k + 2]
            k += 2
        if add is not None:
            add_ref = refs[k]
            k += 1
        if with_out:
            out_ref = refs[k]
            k += 1
        if want_ckpt:
            ck_ref = refs[k]
            k += 1
        fin_ref, bur, bui = refs[k:k + 3]

        @pl.when(pl.program_id(0) == 0)
        def _():
            fin_ref[...] = init_ref[...]

        if want_ckpt:
            ck_ref[0] = fin_ref[...]
        _project_in(act_ref[...].astype(BF16), wre_ref, wim_ref, bur, bui, adjoint)
        _scan_chunk(bur, bui, fin_ref, a_ref, jc, reverse)
        if with_out:
            for gb in range(N_BLOCKS):
                sl = slice(gb * BLK_ST, (gb + 1) * BLK_ST)
                y = (jnp.dot(bur[:, sl].astype(BF16), cre_ref[gb], preferred_element_type=F32)
                     - jnp.dot(bui[:, sl].astype(BF16), cim_ref[gb], preferred_element_type=F32))
                cs = slice(gb * BLK_CH, (gb + 1) * BLK_CH)
                if add is not None:
                    y = y + add_ref[:, cs]
                out_ref[:, cs] = y

    row_spec = pl.BlockSpec((R, D_MODEL), lambda i: (chunk(i), 0))
    w_spec = pl.BlockSpec(w_re.shape, lambda i: (0, 0, 0))
    st_spec = pl.BlockSpec((2, N_SEG, S5_LANES), lambda i: (0, 0, 0))
    ins = [act, w_re, w_im, a, init]
    in_specs = [row_spec, w_spec, w_spec, pl.BlockSpec((2, 1, S5_LANES), lambda i: (0, 0, 0)), st_spec]
    if with_out:
        ins += [c_re, c_im]
        in_specs += [pl.BlockSpec(c_re.shape, lambda i: (0, 0, 0))] * 2
    if add is not None:
        ins.append(add)
        in_specs.append(row_spec)
    out_shape, out_specs = [], []
    if with_out:
        out_shape.append(jax.ShapeDtypeStruct((N, D_MODEL), F32))
        out_specs.append(row_spec)
    if want_ckpt:
        out_shape.append(jax.ShapeDtypeStruct((nch, 2, N_SEG, S5_LANES), F32))
        out_specs.append(pl.BlockSpec((1, 2, N_SEG, S5_LANES), lambda i: (chunk(i), 0, 0, 0)))
    out_shape.append(jax.ShapeDtypeStruct((2, N_SEG, S5_LANES), F32))
    out_specs.append(st_spec)
    res = pl.pallas_call(
        body, out_shape=out_shape, grid=(nch,), in_specs=in_specs, out_specs=out_specs,
        scratch_shapes=[pltpu.VMEM((R, S5_LANES), F32), pltpu.VMEM((R, S5_LANES), F32)],
        compiler_params=_params(("arbitrary",), VMEM_LIMIT), name=name)(*ins)
    res = list(res)
    out = res.pop(0) if with_out else None
    ckpt = res.pop(0) if want_ckpt else None
    return out, ckpt, res[0]


def s5_grads(dy, u, ckpt, b_re, b_im, c_re, c_im, lam, init_adj, *, reverse, add=None, name):
    N = u.shape[0]
    R = math.gcd(ROW_TILE, N)
    nch, jc = N // R, R // N_SEG
    W = S5_LANES

    def chunk(i):
        return i if reverse else (nch - 1 - i)

    def body(*refs):
        dy_ref, u_ref, ck_ref, bre_ref, bim_ref, cre_ref, cim_ref, lam_ref, init_ref = refs[:9]
        k = 9
        if add is not None:
            add_ref = refs[k]
            k += 1
        du_ref, dlam_ref, dbre_ref, dbim_ref, dcre_ref, dcim_ref, fin_ref = refs[k:k + 7]
        sr_buf, si_buf, er_buf, ei_buf, st_buf = refs[k + 7:k + 12]

        @pl.when(pl.program_id(0) == 0)
        def _():
            fin_ref[...] = init_ref[...]
            dlam_ref[...] = jnp.zeros_like(dlam_ref)
            dbre_ref[...] = jnp.zeros_like(dbre_ref)
            dbim_ref[...] = jnp.zeros_like(dbim_ref)
            dcre_ref[...] = jnp.zeros_like(dcre_ref)
            dcim_ref[...] = jnp.zeros_like(dcim_ref)

        u16 = u_ref[...].astype(BF16)
        dy16 = dy_ref[...].astype(BF16)
        st_buf[...] = ck_ref[0]
        _project_in(u16, bre_ref, bim_ref, sr_buf, si_buf, False)
        _scan_chunk(sr_buf, si_buf, st_buf, lam_ref, jc, reverse)
        _project_in(dy16, cre_ref, cim_ref, er_buf, ei_buf, True)
        for lc in range(W // BLK_ST):
            sl = slice(lc * BLK_ST, (lc + 1) * BLK_ST)
            lr = jnp.broadcast_to(lam_ref[0, :, sl], (N_SEG, BLK_ST))
            li = jnp.broadcast_to(lam_ref[1, :, sl], (N_SEG, BLK_ST))

            def one(r0, spr, spi, carry, sl=sl, lr=lr, li=li):
                gr, gi, ar, ai = carry
                nr = er_buf[pl.ds(r0, N_SEG), sl] + lr * gr + li * gi
                ni = ei_buf[pl.ds(r0, N_SEG), sl] + lr * gi - li * gr
                er_buf[pl.ds(r0, N_SEG), sl] = nr
                ei_buf[pl.ds(r0, N_SEG), sl] = ni
                return nr, ni, ar + spr * nr + spi * ni, ai + spr * ni - spi * nr

            def step(ff, carry, sl=sl, one=one):
                f = jc - 1 - ff
                j = (jc - 1 - f) if reverse else f
                jp = (j + 1) if reverse else (j - 1)
                r0 = pl.multiple_of(j * N_SEG, N_SEG)
                p0 = pl.multiple_of(jp * N_SEG, N_SEG)
                return one(r0, sr_buf[pl.ds(p0, N_SEG), sl], si_buf[pl.ds(p0, N_SEG), sl], carry)

            carry = (fin_ref[0, :, sl], fin_ref[1, :, sl], dlam_ref[0, :, sl], dlam_ref[1, :, sl])
            carry = lax.fori_loop(0, jc - 1, step, carry)
            r_first = (jc - 1) * N_SEG if reverse else 0
            gr, gi, ar, ai = one(r_first, ck_ref[0, 0, :, sl], ck_ref[0, 1, :, sl], carry)
            fin_ref[0, :, sl] = gr
            fin_ref[1, :, sl] = gi
            dlam_ref[0, :, sl] = ar
            dlam_ref[1, :, sl] = ai
        tn = (((0,), (0,)), ((), ()))
        nt = (((1,), (1,)), ((), ()))
        for gb in range(N_BLOCKS):
            sl = slice(gb * BLK_ST, (gb + 1) * BLK_ST)
            cs = slice(gb * BLK_CH, (gb + 1) * BLK_CH)
            gr16 = er_buf[:, sl].astype(BF16)
            gi16 = ei_buf[:, sl].astype(BF16)
            du = (lax.dot_general(gr16, bre_ref[gb], nt, preferred_element_type=F32)
                  + lax.dot_general(gi16, bim_ref[gb], nt, preferred_element_type=F32))
            if add is not None:
                du = du + add_ref[:, cs]
            du_ref[:, cs] = du
            ub, dyb = u16[:, cs], dy16[:, cs]
            dbre_ref[gb] += lax.dot_general(ub, gr16, tn, preferred_element_type=F32)
            dbim_ref[gb] += lax.dot_general(ub, gi16, tn, preferred_element_type=F32)
            dcre_ref[gb] += lax.dot_general(sr_buf[:, sl].astype(BF16), dyb, tn, preferred_element_type=F32)
            dcim_ref[gb] -= lax.dot_general(si_buf[:, sl].astype(BF16), dyb, tn, preferred_element_type=F32)

    row_spec = pl.BlockSpec((R, D_MODEL), lambda i: (chunk(i), 0))
    st_spec = pl.BlockSpec((2, N_SEG, W), lambda i: (0, 0, 0))
    wb_spec = pl.BlockSpec(b_re.shape, lambda i: (0, 0, 0))
    wc_spec = pl.BlockSpec(c_re.shape, lambda i: (0, 0, 0))
    ins = [dy, u, ckpt, b_re, b_im, c_re, c_im, lam, init_adj]
    in_specs = [row_spec, row_spec, pl.BlockSpec((1, 2, N_SEG, W), lambda i: (chunk(i), 0, 0, 0)),
                wb_spec, wb_spec, wc_spec, wc_spec, pl.BlockSpec((2, 1, W), lambda i: (0, 0, 0)), st_spec]
    if add is not None:
        ins.append(add)
        in_specs.append(row_spec)
    out_shape = [jax.ShapeDtypeStruct((N, D_MODEL), F32), jax.ShapeDtypeStruct((2, N_SEG, W), F32),
                 jax.ShapeDtypeStruct(b_re.shape, F32), jax.ShapeDtypeStruct(b_re.shape, F32),
                 jax.ShapeDtypeStruct(c_re.shape, F32), jax.ShapeDtypeStruct(c_re.shape, F32),
                 jax.ShapeDtypeStruct((2, N_SEG, W), F32)]
    out_specs = [row_spec, st_spec, wb_spec, wb_spec, wc_spec, wc_spec, st_spec]
    return pl.pallas_call(
        body, out_shape=out_shape, grid=(nch,), in_specs=in_specs, out_specs=out_specs,
        scratch_shapes=[pltpu.VMEM((R, W), F32) for _ in range(4)] + [pltpu.VMEM((2, N_SEG, W), F32)],
        compiler_params=_params(("arbitrary",), VMEM_LIMIT), name=name)(*ins)


def adamw(w, g, m, v, name="adamw"):
    n, d = w.shape
    tm = n if n <= 512 else 256
    assert n % tm == 0
    c1 = 1.0 - ADAM_B1 ** ADAM_STEP
    c2 = 1.0 - ADAM_B2 ** ADAM_STEP

    def body(w_ref, g_ref, m_ref, v_ref, d_ref, nm_ref, nv_ref):
        g_ = g_ref[...]
        m_ = ADAM_B1 * m_ref[...] + (1.0 - ADAM_B1) * g_
        v_ = ADAM_B2 * v_ref[...] + (1.0 - ADAM_B2) * (g_ * g_)
        d_ref[...] = -ADAM_LR * ((m_ / c1) / (jnp.sqrt(v_ / c2) + ADAM_EPS) + ADAM_WD * w_ref[...])
        nm_ref[...] = m_
        nv_ref[...] = v_

    spec = pl.BlockSpec((tm, d), lambda i: (i, 0))
    return pl.pallas_call(
        body, out_shape=[jax.ShapeDtypeStruct((n, d), F32)] * 3, grid=(n // tm,),
        in_specs=[spec] * 4, out_specs=[spec] * 3,
        compiler_params=_params(("parallel",), VMEM_LIMIT), name=name)(w, g, m, v)


def _coords():
    return lax.axis_index("x"), lax.axis_index("y"), lax.axis_index("c")


def exchange(src, n_out_slots, relations, send_slot, recv_slot, local, name):
    nrel = len(relations)
    slot_shape = src.shape[1:]

    def body(src_ref, dst_ref, send_sems, recv_sems, local_sem):
        me = _coords()
        copies = []
        for k, (dx, dy, dc) in enumerate(relations):
            peer = (me[0] ^ dx, me[1] ^ dy, me[2] ^ dc)
            copies.append(pltpu.make_async_remote_copy(
                src_ref=src_ref.at[send_slot(me, peer)], dst_ref=dst_ref.at[recv_slot(me)],
                send_sem=send_sems.at[k], recv_sem=recv_sems.at[k], device_id=peer, device_id_type=MESH))
        if local is not None:
            mine = pltpu.make_async_copy(src_ref.at[local[0](me)], dst_ref.at[local[1](me)], local_sem)
            mine.start()
        for cp in copies:
            cp.start()
        for k, (dx, dy, dc) in enumerate(relations):
            peer = (me[0] ^ dx, me[1] ^ dy, me[2] ^ dc)
            pltpu.make_async_remote_copy(
                src_ref=src_ref.at[send_slot(me, peer)], dst_ref=dst_ref.at[recv_slot(peer)],
                send_sem=send_sems.at[k], recv_sem=recv_sems.at[k], device_id=peer, device_id_type=MESH).wait_recv()
        for cp in copies:
            cp.wait_send()
        if local is not None:
            mine.wait()

    return pl.pallas_call(
        body, out_shape=jax.ShapeDtypeStruct((n_out_slots,) + slot_shape, src.dtype),
        in_specs=[pl.BlockSpec(memory_space=pl.ANY)], out_specs=pl.BlockSpec(memory_space=pl.ANY),
        scratch_shapes=[pltpu.SemaphoreType.DMA((nrel,)), pltpu.SemaphoreType.DMA((nrel,)),
                        pltpu.SemaphoreType.DMA],
        name=name)(src)


ALL_FLIPS = [(dx, dy, dc) for dx in (0, 1) for dy in (0, 1) for dc in (0, 1)][1:]
CHIP_FLIPS = [(1, 0, 0), (0, 1, 0), (1, 1, 0)]


def _dev_index(p):
    return 4 * p[0] + 2 * p[1] + p[2]


def _chip_index(p):
    return 2 * p[0] + p[1]


def allgather_devices(x, name):
    return exchange(x[None], N_DEV, ALL_FLIPS, lambda me, peer: 0, _dev_index,
                    (lambda me: 0, _dev_index), name)


def allgather_chips(x, name):
    return exchange(x[None], N_CHIP, CHIP_FLIPS, lambda me, peer: 0, _chip_index,
                    (lambda me: 0, _chip_index), name)


def scatter_chips(x, name):
    return exchange(x, N_CHIP, CHIP_FLIPS, lambda me, peer: _chip_index(peer), _chip_index,
                    (_chip_index, _chip_index), name)


def swap_cores(x, name):
    return exchange(x[None], 1, [(0, 0, 1)], lambda me, peer: 0, lambda sender: 0, None, name)[0]


def add_pair(a, b, name):
    n, d = a.shape
    tm = math.gcd(2048, n)

    def body(a_ref, b_ref, o_ref):
        o_ref[...] = a_ref[...] + b_ref[...]

    spec = pl.BlockSpec((tm, d), lambda i: (i, 0))
    return pl.pallas_call(
        body, out_shape=jax.ShapeDtypeStruct((n, d), F32), grid=(n // tm,), in_specs=[spec, spec], out_specs=spec,
        compiler_params=_params(("parallel",), VMEM_LIMIT), name=name)(a, b)


def sum_slots(x, name):
    S, n, d = x.shape
    tm = math.gcd(2048, n)

    def body(x_ref, o_ref):
        acc = x_ref[0]
        for s in range(1, S):
            acc = acc + x_ref[s]
        o_ref[...] = acc

    return pl.pallas_call(
        body, out_shape=jax.ShapeDtypeStruct((n, d), F32), grid=(n // tm,),
        in_specs=[pl.BlockSpec((S, tm, d), lambda i: (0, i, 0))], out_specs=pl.BlockSpec((tm, d), lambda i: (i, 0)),
        compiler_params=_params(("parallel",), VMEM_LIMIT), name=name)(x)


def to_segments(a, n_ctx):
    def one(p):
        n = p.shape[0]
        return p.reshape(N_SEG, n // N_SEG, -1).transpose(1, 0, 2).reshape(n, -1)
    return jnp.concatenate([one(a[:n_ctx]), one(a[n_ctx:])], axis=0) if n_ctx else one(a)


def from_segments(a, n_ctx):
    def one(p):
        n = p.shape[0]
        return p.reshape(n // N_SEG, N_SEG, -1).transpose(1, 0, 2).reshape(n, -1)
    return jnp.concatenate([one(a[:n_ctx]), one(a[n_ctx:])], axis=0) if n_ctx else one(a)


def rope_tables(n_ctx, n_lat):
    rows = n_lat // GRID_W
    row = jnp.repeat(jnp.arange(rows, dtype=jnp.int32), GRID_W)
    col = jnp.tile(jnp.arange(GRID_W, dtype=jnp.int32), rows)
    d = QK_ROPE_DIM // 2
    inv = 1.0 / (ROPE_THETA ** (jnp.arange(0, d, 2, dtype=F32) / d))
    ang = jnp.concatenate([row.astype(F32)[:, None] * inv[None, :], col.astype(F32)[:, None] * inv[None, :]], axis=1)
    cos = jnp.concatenate([jnp.ones((n_ctx, d), F32), jnp.cos(ang)], axis=0)
    sin = jnp.concatenate([jnp.zeros((n_ctx, d), F32), jnp.sin(ang)], axis=0)
    return cos, sin


def rope(x, cos, sin):
    q = QK_ROPE_DIM // 4
    a, b, c, d = x[..., :q], x[..., q:2 * q], x[..., 2 * q:3 * q], x[..., 3 * q:]
    cr, cc = cos[:, :q], cos[:, q:]
    sr, sc = sin[:, :q], sin[:, q:]
    return jnp.concatenate([a * cr - b * sr, a * sr + b * cr, c * cc - d * sc, c * sc + d * cc], axis=-1)


def heads_layout(q, kv, kr, cos, sin):
    T = q.shape[0]
    qh = q.reshape(T, MLA_HEADS, QK_DIM).transpose(1, 0, 2)
    qh = jnp.concatenate([qh[..., :QK_NOPE_DIM], rope(qh[..., QK_NOPE_DIM:], cos, sin)], axis=-1)
    kvh = kv.reshape(T, MLA_HEADS, QK_NOPE_DIM + V_HEAD_DIM).transpose(1, 0, 2)
    krr = jnp.broadcast_to(rope(kr, cos, sin)[None], (MLA_HEADS, T, QK_ROPE_DIM))
    kh = jnp.concatenate([kvh[..., :QK_NOPE_DIM], krr], axis=-1)
    return qh, kh, kvh[..., QK_NOPE_DIM:]


def s5_discretise(a_re, a_im, log_step, b_re, b_im):
    dt = jnp.exp(log_step)[:, None]
    mag = jnp.exp(a_re * dt)
    lb_re = mag * jnp.cos(a_im * dt)
    lb_im = mag * jnp.sin(a_im * dt)
    den = a_re * a_re + a_im * a_im
    nr = lb_re - 1.0
    f_re = ((nr * a_re + lb_im * a_im) / den)[..., None]
    f_im = ((lb_im * a_re - nr * a_im) / den)[..., None]
    return lb_re, lb_im, f_re * b_re - f_im * b_im, f_re * b_im + f_im * b_re


def s5_block_weights(lb_re, lb_im, bb_re, bb_im, c_re, c_im):
    eye = jnp.eye(GROUPS_PER_BLOCK, dtype=F32)
    lam = jnp.stack([lb_re.reshape(1, S5_LANES), lb_im.reshape(1, S5_LANES)])

    def b_blocks(bb):
        t = bb.reshape(N_BLOCKS, GROUPS_PER_BLOCK, S5_STATE, S5_GROUP)
        return jnp.einsum("bgpc,gh->bgchp", t, eye).reshape(N_BLOCKS, BLK_CH, BLK_ST).astype(BF16)

    def c_blocks(cc):
        t = cc.reshape(N_BLOCKS, GROUPS_PER_BLOCK, S5_GROUP, S5_STATE)
        return jnp.einsum("bgcp,gh->bgphc", t, eye).reshape(N_BLOCKS, BLK_ST, BLK_CH).astype(BF16)

    return lam, b_blocks(bb_re), b_blocks(bb_im), c_blocks(c_re), c_blocks(c_im)


def b_block_diag(db):
    t = db.reshape(N_BLOCKS, GROUPS_PER_BLOCK, S5_GROUP, GROUPS_PER_BLOCK, S5_STATE)
    return jnp.einsum("bgchp,gh->bgpc", t, jnp.eye(GROUPS_PER_BLOCK, dtype=F32)).reshape(S5_GROUPS, S5_STATE, S5_GROUP)


def c_block_diag(dc):
    t = dc.reshape(N_BLOCKS, GROUPS_PER_BLOCK, S5_STATE, GROUPS_PER_BLOCK, S5_GROUP)
    return jnp.einsum("bgphc,gh->bgcp", t, jnp.eye(GROUPS_PER_BLOCK, dtype=F32)).reshape(S5_GROUPS, S5_GROUP, S5_STATE)


def conj(a):
    return jnp.stack([a[0], -a[1]])


PACK_TILE = 16 * 128
REDUCE_ROWS = 2048


def pack_flat(parts, dtype):
    flat = [p.reshape(-1).astype(dtype) for p in parts]
    sizes = [f.shape[0] for f in flat]
    total = sum(sizes)
    pad = (-total) % PACK_TILE
    if pad:
        flat.append(jnp.zeros((pad,), dtype))
    offs = np.cumsum([0] + sizes)[:-1].tolist()
    return jnp.concatenate(flat).reshape(-1, 128), offs


def unpack_flat(buf, offs, shapes):
    flat = buf.reshape(-1)
    return [flat[o:o + int(np.prod(s))].reshape(s) for o, s in zip(offs, shapes)]


def f32_as_bf16_pairs(x):
    return lax.bitcast_convert_type(x.astype(F32), BF16).reshape(-1)


def bf16_pairs_as_f32(x):
    return lax.bitcast_convert_type(x.reshape(-1, 2), F32)


def s5_forward(u_p, n_ctx, dirs):
    saved = []
    y = None
    u_c, u_l = u_p[:n_ctx], u_p[n_ctx:]
    zeros_tile = jnp.zeros((2, N_SEG, S5_LANES), F32)
    zeros_row = jnp.zeros((2, 1, S5_LANES), F32)
    for k, (lam, b_re, b_im, c_re, c_im) in enumerate(dirs):
        rev = k == 1
        last = 0 if rev else N_SEG - 1
        _, _, fin = s5_scan(u_c, b_re, b_im, lam, zeros_tile, reverse=rev, name=f"s5_ctx_finals{k}")
        carry_c = s5_chain(fin, zeros_row, lam, n_ctx // N_SEG, rev, name=f"s5_ctx_chain{k}")
        _, ck_c, fin_c = s5_scan(u_c, b_re, b_im, lam, carry_c, reverse=rev, want_ckpt=True, name=f"s5_ctx_scan{k}")
        s0 = fin_c[:, last:last + 1, :]
        _, _, fin = s5_scan(u_l, b_re, b_im, lam, zeros_tile, reverse=rev, name=f"s5_lat_finals{k}")
        carry_l = s5_chain(fin, s0, lam, u_l.shape[0] // N_SEG, rev, name=f"s5_lat_chain{k}")
        y, ck_l, _ = s5_scan(u_l, b_re, b_im, lam, carry_l, reverse=rev, c_re=c_re, c_im=c_im, add=y,
                             want_ckpt=True, name=f"s5_lat_scan{k}")
        saved.append((ck_c, ck_l))
    return y, saved


def s5_backward(dy_l, du_extra_l, u_p, n_ctx, dirs, saved):
    u_c, u_l = u_p[:n_ctx], u_p[n_ctx:]
    zeros_tile = jnp.zeros((2, N_SEG, S5_LANES), F32)
    zeros_row = jnp.zeros((2, 1, S5_LANES), F32)
    dy_c = jnp.zeros((n_ctx, D_MODEL), F32)
    du_l, du_c = du_extra_l, None
    grads = []
    for k, (lam, b_re, b_im, c_re, c_im) in enumerate(dirs):
        rev = k == 1
        lam_c = conj(lam)
        ck_c, ck_l = saved[k]
        first = N_SEG - 1 if rev else 0
        _, _, fin = s5_scan(dy_l, c_re, c_im, lam_c, zeros_tile, reverse=not rev, adjoint=True,
                            name=f"s5_lat_adj_finals{k}")
        carry = s5_chain(fin, zeros_row, lam_c, u_l.shape[0] // N_SEG, not rev, name=f"s5_lat_adj_chain{k}")
        du_l, dlam_l, dbr_l, dbi_l, dcr_l, dci_l, fin_a = s5_grads(
            dy_l, u_l, ck_l, b_re, b_im, c_re, c_im, lam, carry, reverse=rev, add=du_l, name=f"s5_lat_grads{k}")
        g0 = fin_a[:, first:first + 1, :]
        carry = s5_chain(zeros_tile, g0, lam_c, n_ctx // N_SEG, not rev, name=f"s5_ctx_adj_chain{k}")
        du_c, dlam_c, dbr_c, dbi_c, _, _, _ = s5_grads(
            dy_c, u_c, ck_c, b_re, b_im, c_re, c_im, lam, carry, reverse=rev, add=du_c, name=f"s5_ctx_grads{k}")
        dlam = jnp.sum(dlam_l + dlam_c, axis=1)
        grads.append((dlam, b_block_diag(dbr_l + dbr_c), b_block_diag(dbi_l + dbi_c),
                      c_block_diag(dcr_l), c_block_diag(dci_l)))
    return jnp.concatenate([du_c, du_l], axis=0), grads


def local_step(x, ctx, target, mod, w):
    L, Lc = x.shape[0], ctx.shape[0]
    T = L + Lc
    assert L % Lc == 0 and Lc % (2 * N_SEG) == 0 and L % GRID_W == 0
    D = D_MODEL
    X0 = jnp.concatenate([ctx, x], axis=0)

    def mod_of(i, j):
        return mod[i, :, j, :][:, None, :]

    def vec(v):
        return v.reshape(1, 1, -1).astype(F32)

    g0 = vec(w["norm_g"][0])
    (H0,) = rowwise_fwd(f_norm_mod, [X0], [g0, mod_of(0, 1), mod_of(0, 0)], [D], [BF16], T, Lc, "l0_norm")
    p0 = mm_nn(H0, w["mla_w_in"], name="l0_in")
    o1, o2 = Q_LORA_RANK, Q_LORA_RANK + KV_LORA_RANK
    o3 = o2 + QK_ROPE_DIM
    cq, ckv, kr, z0 = p0[:, :o1], p0[:, o1:o2], p0[:, o2:o3], p0[:, o3:]
    qng, kvng = vec(w["mla_q_norm"]), vec(w["mla_kv_norm"])
    (qn,) = rowwise_fwd(f_rms, [cq], [qng], [Q_LORA_RANK], [BF16], T, 0, "l0_qnorm")
    (kvn,) = rowwise_fwd(f_rms, [ckv], [kvng], [KV_LORA_RANK], [BF16], T, 0, "l0_kvnorm")
    q = mm_nn(qn, w["mla_w_uq"], name="l0_uq")
    kv = mm_nn(kvn, w["mla_w_ukv"], name="l0_ukv")
    cos, sin = rope_tables(Lc, L)
    (qh, kh, vh), heads_vjp = jax.vjp(lambda q_, kv_, kr_: heads_layout(q_, kv_, kr_, cos, sin), q, kv, kr)
    qb, kb, vb = qh.astype(BF16), kh.astype(BF16), vh.astype(BF16)
    oh, lse = attn_fwd(qb, kb, vb, Lc)
    o = oh.transpose(1, 0, 2).reshape(T, MLA_HEADS * V_HEAD_DIM)
    (og,) = rowwise_fwd(f_gate, [o, z0], [], [D], [BF16], T, 0, "l0_gate")
    out0 = mm_nn(og, w["mla_w_out"], name="l0_out")
    (X1,) = rowwise_fwd(f_res, [X0, out0], [mod_of(0, 2)], [D], [F32], T, Lc, "l0_res")

    X1p = to_segments(X1, Lc)
    tgt_p = to_segments(target, 0)
    g1 = vec(w["norm_g"][1])
    (H1,) = rowwise_fwd(f_norm_mod, [X1p], [g1, mod_of(1, 1), mod_of(1, 0)], [D], [BF16], T, Lc, "l1_norm")
    p1 = mm_nn(H1, w["s5_w_in"], name="l1_in")
    u_p = p1[:, :D]
    disc_fn = lambda *a: tuple(zip(*[s5_discretise(a[0][k], a[1][k], a[2][k], a[3][k], a[4][k]) for k in range(2)]))
    disc, disc_vjp = jax.vjp(disc_fn, w["s5_a_re"], w["s5_a_im"], w["s5_log_step"], w["s5_b_re"], w["s5_b_im"])
    dirs = [s5_block_weights(disc[0][k], disc[1][k], disc[2][k], disc[3][k], w["s5_c_re"][k], w["s5_c_im"][k])
            for k in range(2)]
    y_ssm, s5_saved = s5_forward(u_p, Lc, dirs)
    d_vec, bg_vec = vec(w["s5_d"]), vec(w["s5_b_glu"])
    u_lat = Rows(p1, D, row_off=Lc, col_blk=0)
    z_lat = Rows(p1, D, row_off=Lc, col_blk=1)
    (ya,) = rowwise_fwd(f_s5_act, [y_ssm, u_lat], [d_vec], [D], [F32], L, 0, "l1_act")
    gl = mm_nn(ya, w["s5_w_glu"], name="l1_glu")
    (y3,) = rowwise_fwd(f_s5_glu, [ya, gl, z_lat], [bg_vec], [D], [BF16], L, 0, "l1_gate")
    out1 = mm_nn(y3, w["s5_w_out"], name="l1_out")
    gt1 = mod[1, 1:2, 2, :][:, None, :]
    x1_lat = Rows(X1p, D, row_off=Lc)
    (X2,) = rowwise_fwd(f_res, [x1_lat, out1], [gt1], [D], [F32], L, 0, "l1_res")

    fg = w["final_g"].reshape(1, D).astype(F32)
    lvec, dX2, d_fg = loss_and_grad(X2, fg, tgt_p)
    loss = jnp.sum(lvec)
    gw = {"final_g": d_fg.reshape(D)}
    dmod = {}

    d_out1, d_gt1 = rowwise_bwd(f_res, [x1_lat, out1], [gt1], [dX2], [1], [0], L, 0, "l1_res_bwd")
    d_y3 = mm_nt(d_out1, w["s5_w_out"], name="l1_out_dx")
    gw["s5_w_out"] = mm_tn(y3, d_out1, name="l1_out_dw")
    d_ya_a, d_gl, d_z1, d_bg = rowwise_bwd(f_s5_glu, [ya, gl, z_lat], [bg_vec], [d_y3], [0, 1, 2], [0], L, 0,
                                           "l1_gate_bwd")
    gw["s5_b_glu"] = d_bg.reshape(-1)
    gw["s5_w_glu"] = mm_tn(ya, d_gl, name="l1_glu_dw")
    d_ya = d_ya_a + mm_nt(d_gl, w["s5_w_glu"], name="l1_glu_dx")
    d_yssm, d_u_act, d_d = rowwise_bwd(f_s5_act, [y_ssm, u_lat], [d_vec], [d_ya], [0, 1], [0], L, 0, "l1_act_bwd")
    gw["s5_d"] = d_d.reshape(-1)
    du_p, s5_g = s5_backward(d_yssm, d_u_act, u_p, Lc, dirs, s5_saved)
    d_disc = tuple(tuple(s5_g[k][j - 1].reshape(disc[j][k].shape) if j >= 2 else
                         s5_g[k][0][j].reshape(disc[j][k].shape) for k in range(2)) for j in range(4))
    gw["s5_a_re"], gw["s5_a_im"], gw["s5_log_step"], gw["s5_b_re"], gw["s5_b_im"] = disc_vjp(d_disc)
    gw["s5_c_re"] = jnp.stack([s5_g[0][3], s5_g[1][3]])
    gw["s5_c_im"] = jnp.stack([s5_g[0][4], s5_g[1][4]])
    d_p1 = jnp.concatenate([du_p, jnp.concatenate([jnp.zeros((Lc, D), F32), d_z1], axis=0)], axis=1)
    d_H1 = mm_nt(d_p1, w["s5_w_in"], name="l1_in_dx")
    gw["s5_w_in"] = mm_tn(H1, d_p1, name="l1_in_dw")
    d_X1p, d_g1, d_sc1, d_sh1 = rowwise_bwd(f_norm_mod, [X1p], [g1, mod_of(1, 1), mod_of(1, 0)], [d_H1],
                                            [0], [0, 1, 2], T, Lc, "l1_norm_bwd")
    d_X1p = d_X1p + jnp.concatenate([jnp.zeros((Lc, D), F32), dX2], axis=0)
    d_gt1_full = jnp.concatenate([jnp.zeros((1, 1, D), F32), d_gt1], axis=0)
    dmod[1] = (d_sh1, d_sc1, d_gt1_full)
    d_X1 = from_segments(d_X1p, Lc)

    d_out0, d_gt0 = rowwise_bwd(f_res, [X0, out0], [mod_of(0, 2)], [d_X1], [1], [0], T, Lc, "l0_res_bwd")
    d_og = mm_nt(d_out0, w["mla_w_out"], name="l0_out_dx")
    gw["mla_w_out"] = mm_tn(og, d_out0, name="l0_out_dw")
    d_o, d_z0 = rowwise_bwd(f_gate, [o, z0], [], [d_og], [0, 1], [], T, 0, "l0_gate_bwd")
    d_oh = d_o.reshape(T, MLA_HEADS, V_HEAD_DIM).transpose(1, 0, 2)
    dqh, delta = attn_bwd_dq(qb, kb, vb, oh, d_oh, lse, Lc)
    dkh, dvh = attn_bwd_dkv(qb, kb, vb, d_oh, lse.reshape(MLA_HEADS, 1, T), delta.reshape(MLA_HEADS, 1, T), Lc)
    d_q, d_kv, d_kr = heads_vjp((dqh, dkh, dvh))
    d_qn = mm_nt(d_q, w["mla_w_uq"], name="l0_uq_dx")
    gw["mla_w_uq"] = mm_tn(qn, d_q, name="l0_uq_dw")
    d_kvn = mm_nt(d_kv, w["mla_w_ukv"], name="l0_ukv_dx")
    gw["mla_w_ukv"] = mm_tn(kvn, d_kv, name="l0_ukv_dw")
    d_cq, d_qng = rowwise_bwd(f_rms, [cq], [qng], [d_qn], [0], [0], T, 0, "l0_qnorm_bwd")
    d_ckv, d_kvng = rowwise_bwd(f_rms, [ckv], [kvng], [d_kvn], [0], [0], T, 0, "l0_kvnorm_bwd")
    gw["mla_q_norm"] = d_qng.reshape(-1)
    gw["mla_kv_norm"] = d_kvng.reshape(-1)
    d_p0 = jnp.concatenate([d_cq, d_ckv, d_kr, d_z0], axis=1)
    d_H0 = mm_nt(d_p0, w["mla_w_in"], name="l0_in_dx")
    gw["mla_w_in"] = mm_tn(H0, d_p0, name="l0_in_dw")
    d_X0, d_g0, d_sc0, d_sh0 = rowwise_bwd(f_norm_mod, [X0], [g0, mod_of(0, 1), mod_of(0, 0)], [d_H0],
                                           [0], [0, 1, 2], T, Lc, "l0_norm_bwd")
    dmod[0] = (d_sh0, d_sc0, d_gt0)
    gw["norm_g"] = jnp.stack([d_g0.reshape(D), d_g1.reshape(D)])
    dx = (d_X0 + d_X1)[Lc:]
    dmod_arr = jnp.stack([jnp.stack([dmod[i][j][:, 0, :] for j in range(3)], axis=1) for i in range(2)])
    return loss, dx, dmod_arr, gw


SHARDED = {
    "mla_w_in": 1, "mla_w_uq": 1, "mla_w_ukv": 1, "mla_w_out": 0,
    "s5_w_in": 1, "s5_w_glu": 0, "s5_w_out": 0, "s5_d": 0, "s5_b_glu": 0,
}
SHARDED_MATS = ["mla_w_in", "mla_w_uq", "mla_w_ukv", "mla_w_out", "s5_w_in", "s5_w_glu", "s5_w_out"]
SHARDED_VECS = ["s5_d", "s5_b_glu"]
REPLICATED = ["norm_g", "mla_q_norm", "mla_kv_norm", "s5_a_re", "s5_a_im", "s5_log_step", "s5_b_re", "s5_b_im",
              "s5_c_re", "s5_c_im", "final_g"]
WEIGHT_ORDER = ["c_ctx", "ada_w", "ada_b", "norm_g", "mla_w_in", "mla_q_norm", "mla_w_uq", "mla_kv_norm", "mla_w_ukv",
                "mla_w_out", "s5_w_in", "s5_a_re", "s5_a_im", "s5_log_step", "s5_b_re", "s5_b_im", "s5_c_re", "s5_c_im",
                "s5_d", "s5_w_glu", "s5_b_glu", "s5_w_out", "final_g"]


def gather_weights(ws):
    parts = [ws[n] for n in SHARDED_MATS]
    vec_parts = [f32_as_bf16_pairs(ws[n]) for n in SHARDED_VECS]
    buf, offs = pack_flat([p.astype(BF16) for p in parts] + vec_parts, BF16)
    allb = allgather_chips(buf, "gather_weights")
    shapes = [p.shape for p in parts] + [v.shape for v in vec_parts]
    per_chip = [unpack_flat(allb[q], offs, shapes) for q in range(N_CHIP)]
    full = {}
    for k, n in enumerate(SHARDED_MATS):
        full[n] = jnp.concatenate([per_chip[q][k] for q in range(N_CHIP)], axis=SHARDED[n])
    for k, n in enumerate(SHARDED_VECS):
        full[n] = jnp.concatenate([bf16_pairs_as_f32(per_chip[q][len(SHARDED_MATS) + k]) for q in range(N_CHIP)])
    return full


def reduce_gradients(gw):
    c = lax.axis_index("c")
    per_chip = []
    rep_parts = [gw[n].astype(F32) for n in REPLICATED]
    rep_flat = jnp.concatenate([p.reshape(-1) for p in rep_parts])
    rep_pad = (-rep_flat.shape[0]) % (N_CHIP * 2 * PACK_TILE)
    rep_flat = jnp.pad(rep_flat, (0, rep_pad)).reshape(N_CHIP, -1)
    offs = shapes = None
    for q in range(N_CHIP):
        parts = []
        for n in SHARDED_MATS + SHARDED_VECS:
            g = gw[n]
            size = g.shape[SHARDED[n]] // N_CHIP
            parts.append(lax.slice_in_dim(g, q * size, (q + 1) * size, axis=SHARDED[n]))
        shapes = [p.shape for p in parts]
        buf, offs = pack_flat(parts + [rep_flat[q]], F32)
        rows = buf.shape[0]
        if rows % REDUCE_ROWS:
            buf = jnp.pad(buf, ((0, (-rows) % REDUCE_ROWS), (0, 0)))
        per_chip.append(buf)
    G = jnp.stack(per_chip)
    n = G.shape[1]
    halves = G.reshape(N_CHIP, 2, n // 2, 128)
    mine = lax.dynamic_index_in_dim(halves, c, axis=1, keepdims=False)
    theirs = lax.dynamic_index_in_dim(halves, 1 - c, axis=1, keepdims=False)
    got = swap_cores(theirs, "grads_swap_in")
    chip_sum = add_pair(mine.reshape(-1, 128), got.reshape(-1, 128), "grads_sum_cores").reshape(N_CHIP, n // 2, 128)
    parts4 = scatter_chips(chip_sum, "grads_scatter")
    my_half = sum_slots(parts4, "grads_sum_chips")
    other_half = swap_cores(my_half, "grads_swap_out")
    full = jnp.concatenate([jnp.where(c == 0, my_half, other_half), jnp.where(c == 0, other_half, my_half)], axis=0)
    rep_size = rep_flat.shape[1]
    vals = unpack_flat(full, offs, shapes + [(rep_size,)])
    out = dict(zip(SHARDED_MATS + SHARDED_VECS, vals[:-1]))
    rep_all = allgather_chips(vals[-1].reshape(-1, 128), "grads_gather_replicated").reshape(-1)
    o = 0
    for nme, p in zip(REPLICATED, rep_parts):
        out[nme] = rep_all[o:o + p.size].reshape(p.shape)
        o += p.size
    return out


def kernel(x, c, ctx, c_ctx, ada_w, ada_b, norm_g, mla_w_in, mla_q_norm, mla_w_uq, mla_kv_norm, mla_w_ukv, mla_w_out, s5_w_in, s5_a_re, s5_a_im, s5_log_step, s5_b_re, s5_b_im, s5_c_re, s5_c_im, s5_d, s5_w_glu, s5_b_glu, s5_w_out, final_g, loss_target, m_c_ctx, m_ada_w, m_ada_b, m_norm_g, m_mla_w_in, m_mla_q_norm, m_mla_w_uq, m_mla_kv_norm, m_mla_w_ukv, m_mla_w_out, m_s5_w_in, m_s5_a_re, m_s5_a_im, m_s5_log_step, m_s5_b_re, m_s5_b_im, m_s5_c_re, m_s5_c_im, m_s5_d, m_s5_w_glu, m_s5_b_glu, m_s5_w_out, m_final_g, v_c_ctx, v_ada_w, v_ada_b, v_norm_g, v_mla_w_in, v_mla_q_norm, v_mla_w_uq, v_mla_kv_norm, v_mla_w_ukv, v_mla_w_out, v_s5_w_in, v_s5_a_re, v_s5_a_im, v_s5_log_step, v_s5_b_re, v_s5_b_im, v_s5_c_re, v_s5_c_im, v_s5_d, v_s5_w_glu, v_s5_b_glu, v_s5_w_out, v_final_g):
    args = dict(locals())
    weights = {n: args[n] for n in WEIGHT_ORDER}
    D = D_MODEL
    xi, yi, ci = _coords()
    chip = 2 * xi + yi
    me = 4 * xi + 2 * yi + ci
    n_col = ada_w.shape[2]

    c_all = allgather_devices(jnp.pad(c, ((0, 7), (0, 0))), "gather_c")[:, 0, :]
    cond = jnp.concatenate([c_all, jnp.broadcast_to(c_ctx[None], (8, D))], axis=0)
    (s_cond,) = rowwise_fwd(lambda v: (_silu(v),), [cond], [], [D], [F32], 16, 0, "cond_silu")
    mod_cols = jnp.stack([mm_nn(s_cond, ada_w[i], name=f"mod_proj{i}") for i in range(2)])
    mod_all = allgather_chips(mod_cols, "gather_mod")
    mod_all = mod_all.transpose(1, 2, 0, 3).reshape(2, 16, 3 * D) + ada_b[:, None, :]
    mod_l = lax.dynamic_index_in_dim(mod_all, me, axis=1, keepdims=False)
    mod_c = mod_all[:, 8, :]
    mod = jnp.stack([mod_c.reshape(2, 3, D), mod_l.reshape(2, 3, D)], axis=1)

    shards = {n: weights[n][0] for n in SHARDED_MATS + SHARDED_VECS}
    w = gather_weights(shards)
    for n in ["norm_g", "final_g"]:
        w[n] = weights[n]
    for n in ["mla_q_norm", "mla_kv_norm", "s5_a_re", "s5_a_im", "s5_log_step", "s5_b_re", "s5_b_im",
              "s5_c_re", "s5_c_im"]:
        w[n] = weights[n][0]

    loss_me, dx, dmod, gw = local_step(x[0], ctx[0], loss_target[0], mod, w)
    loss = lax.psum(loss_me, ("x", "y", "c"))

    dmod_rows = allgather_devices(dmod.reshape(2, 2, 3 * D), "gather_dmod")
    dm = jnp.concatenate([dmod_rows[:, :, 1, :], dmod_rows[:, :, 0, :]], axis=0).transpose(1, 0, 2)
    g_ada_b = jnp.sum(dm, axis=1)
    dm_cols = lax.dynamic_slice_in_dim(dm, chip * n_col, n_col, axis=2)
    g_ada_w = jnp.stack([mm_tn(s_cond, dm_cols[i], name=f"mod_proj_dw{i}") for i in range(2)])
    dmc = jnp.sum(dm_cols[:, 8:, :], axis=1)
    dmc8 = jnp.broadcast_to(dmc[:, None, :], (2, 8, n_col))
    g_sc = mm_nt(dmc8[0], ada_w[0], name="mod_proj_dx0")[0] + mm_nt(dmc8[1], ada_w[1], name="mod_proj_dx1")[0]
    g_sc_all = allgather_devices(jnp.broadcast_to(g_sc[None], (8, D)), "gather_dcond")[:, 0, :]
    g_silu_cc = g_sc_all[0] + g_sc_all[2] + g_sc_all[4] + g_sc_all[6]
    (g_c_ctx,) = rowwise_bwd(lambda v: (_silu(v),), [jnp.broadcast_to(c_ctx[None], (8, D))], [],
                             [jnp.broadcast_to(g_silu_cc[None], (8, D))], [0], [], 8, 0, "cond_silu_bwd")
    g_c_ctx = g_c_ctx[0]

    gw_in = {}
    for n in SHARDED_MATS + SHARDED_VECS:
        gw_in[n] = gw[n]
    for n in REPLICATED:
        gw_in[n] = gw[n]
    red = reduce_gradients(gw_in)
    grads = {"c_ctx": g_c_ctx, "ada_w": g_ada_w, "ada_b": g_ada_b}
    for n in WEIGHT_ORDER[3:]:
        grads[n] = red[n].reshape(weights[n].shape)

    deltas, new_m, new_v = {}, {}, {}
    small = [n for n in WEIGHT_ORDER if weights[n].size < 50000]
    for n in WEIGHT_ORDER:
        if n in small:
            continue
        shp = weights[n].shape
        w2 = weights[n].reshape(-1, shp[-1] if shp[-1] >= 256 else 128)
        d_, m_, v_ = adamw(w2, grads[n].reshape(w2.shape), args["m_" + n].reshape(w2.shape),
                           args["v_" + n].reshape(w2.shape), name=f"adamw_{n}")
        deltas[n], new_m[n], new_v[n] = d_.reshape(shp), m_.reshape(shp), v_.reshape(shp)
    packs = []
    offs = None
    for src in (weights, grads, {n: args["m_" + n] for n in small}, {n: args["v_" + n] for n in small}):
        buf, offs = pack_flat([src[n] for n in small], F32)
        packs.append(buf)
    outs = adamw(*packs, name="adamw_small")
    for res, dst in zip(outs, (deltas, new_m, new_v)):
        for n, val in zip(small, unpack_flat(res, offs, [weights[n].shape for n in small])):
            dst[n] = val

    return (loss, dx[None], *[grads[n] for n in WEIGHT_ORDER], *[deltas[n] for n in WEIGHT_ORDER],
            *[new_m[n] for n in WEIGHT_ORDER], *[new_v[n] for n in WEIGHT_ORDER])
```

```python
import functools
import math

import jax
import jax.numpy as jnp
import numpy as np
from jax import lax
from jax.experimental import pallas as pl
from jax.experimental.pallas import tpu as pltpu

F32 = jnp.float32
BF16 = jnp.bfloat16

D_MODEL = 1024
GRID_W = 64
EPS = 1e-6
MLA_HEADS = 16
QK_NOPE_DIM = 64
QK_ROPE_DIM = 32
V_HEAD_DIM = 64
Q_LORA_RANK = 256
KV_LORA_RANK = 128
QK_DIM = QK_NOPE_DIM + QK_ROPE_DIM
SOFTMAX_SCALE = QK_DIM ** -0.5
ROPE_THETA = 10000.0
S5_GROUP = 16
S5_GROUPS = D_MODEL // S5_GROUP
S5_STATE = 64
S5_LANES = S5_GROUPS * S5_STATE
N_SEG = 8
GROUPS_PER_BLOCK = 8
N_BLOCKS = S5_GROUPS // GROUPS_PER_BLOCK
BLK_CH = GROUPS_PER_BLOCK * S5_GROUP
BLK_ST = GROUPS_PER_BLOCK * S5_STATE

ADAM_LR = 0.001
ADAM_B1 = 0.9
ADAM_B2 = 0.999
ADAM_EPS = 1e-08
ADAM_WD = 0.01
ADAM_STEP = 10

N_DEV = 8
N_CHIP = 4
MESH = pl.DeviceIdType.MESH
VMEM_LIMIT = 52 * 1024 * 1024
ROW_TILE = 256


def _params(sem=None, vmem=None):
    return pltpu.CompilerParams(dimension_semantics=sem, vmem_limit_bytes=vmem)


def mm_nn(a, b, out_dtype=F32, name="mm_nn"):
    M, K = a.shape
    N = b.shape[1]
    tm = math.gcd(ROW_TILE, M)

    def body(a_ref, b_ref, o_ref):
        o_ref[...] = jnp.dot(a_ref[...].astype(BF16), b_ref[...].astype(BF16),
                             preferred_element_type=F32).astype(o_ref.dtype)

    return pl.pallas_call(
        body, out_shape=jax.ShapeDtypeStruct((M, N), out_dtype), grid=(M // tm,),
        in_specs=[pl.BlockSpec((tm, K), lambda i: (i, 0)), pl.BlockSpec((K, N), lambda i: (0, 0))],
        out_specs=pl.BlockSpec((tm, N), lambda i: (i, 0)),
        compiler_params=_params(("parallel",), VMEM_LIMIT), name=name)(a, b)


def mm_nt(a, b, out_dtype=F32, name="mm_nt"):
    M, N = a.shape
    K = b.shape[0]
    tm = math.gcd(ROW_TILE, M)

    def body(a_ref, b_ref, o_ref):
        o_ref[...] = lax.dot_general(a_ref[...].astype(BF16), b_ref[...].astype(BF16),
                                     (((1,), (1,)), ((), ())),
                                     preferred_element_type=F32).astype(o_ref.dtype)

    return pl.pallas_call(
        body, out_shape=jax.ShapeDtypeStruct((M, K), out_dtype), grid=(M // tm,),
        in_specs=[pl.BlockSpec((tm, N), lambda i: (i, 0)), pl.BlockSpec((K, N), lambda i: (0, 0))],
        out_specs=pl.BlockSpec((tm, K), lambda i: (i, 0)),
        compiler_params=_params(("parallel",), VMEM_LIMIT), name=name)(a, b)


def mm_tn(a, b, name="mm_tn"):
    M, K = a.shape
    N = b.shape[1]
    tm = math.gcd(2 * ROW_TILE, M)

    def body(a_ref, b_ref, o_ref):
        @pl.when(pl.program_id(0) == 0)
        def _():
            o_ref[...] = jnp.zeros_like(o_ref)

        o_ref[...] += lax.dot_general(a_ref[...].astype(BF16), b_ref[...].astype(BF16),
                                      (((0,), (0,)), ((), ())), preferred_element_type=F32)

    return pl.pallas_call(
        body, out_shape=jax.ShapeDtypeStruct((K, N), F32), grid=(M // tm,),
        in_specs=[pl.BlockSpec((tm, K), lambda i: (i, 0)), pl.BlockSpec((tm, N), lambda i: (i, 0))],
        out_specs=pl.BlockSpec((K, N), lambda i: (0, 0)),
        compiler_params=_params(("arbitrary",), VMEM_LIMIT), name=name)(a, b)


class Rows:
    def __init__(self, arr, width=None, row_off=0, col_blk=0):
        self.arr = arr
        self.width = arr.shape[1] if width is None else width
        self.row_off = row_off
        self.col_blk = col_blk

    def spec(self, tm):
        ro, cb = self.row_off // tm, self.col_blk
        return pl.BlockSpec((tm, self.width), lambda i: (i + ro, cb))


def _as_rows(x):
    return x if isinstance(x, Rows) else Rows(x)


def _row_tile(n_rows, n_ctx_rows, rows):
    tm = math.gcd(ROW_TILE, n_rows, n_ctx_rows)
    for r in rows:
        tm = math.gcd(tm, r.row_off)
    return tm


def _bc_spec(arr, n_ctx_blocks):
    g, _, d = arr.shape
    if g == 1:
        return pl.BlockSpec((1, 1, d), lambda i: (0, 0, 0))
    return pl.BlockSpec((1, 1, d), lambda i: ((i >= n_ctx_blocks).astype(jnp.int32), 0, 0))


def rowwise_fwd(fn, rows, bcs, out_dims, out_dtypes, n_rows, n_ctx_rows, name):
    rows = [_as_rows(r) for r in rows]
    tm = _row_tile(n_rows, n_ctx_rows, rows)
    ncb = n_ctx_rows // tm
    nr, nb = len(rows), len(bcs)

    def body(*refs):
        vals = [r[...].astype(F32) for r in refs[:nr]] + [b[0].astype(F32) for b in refs[nr:nr + nb]]
        outs = fn(*vals)
        for o_ref, v in zip(refs[nr + nb:], outs):
            o_ref[...] = v.astype(o_ref.dtype)

    outs = pl.pallas_call(
        body,
        out_shape=[jax.ShapeDtypeStruct((n_rows, d), dt) for d, dt in zip(out_dims, out_dtypes)],
        grid=(n_rows // tm,),
        in_specs=[r.spec(tm) for r in rows] + [_bc_spec(b, ncb) for b in bcs],
        out_specs=[pl.BlockSpec((tm, d), lambda i: (i, 0)) for d in out_dims],
        compiler_params=_params(("parallel",), VMEM_LIMIT), name=name)(*[r.arr for r in rows], *bcs)
    return outs


def rowwise_bwd(fn, rows, bcs, cts, diff_rows, diff_bcs, n_rows, n_ctx_rows, name):
    rows = [_as_rows(r) for r in rows]
    cts = [_as_rows(c) for c in cts]
    tm = _row_tile(n_rows, n_ctx_rows, rows + cts)
    ncb = n_ctx_rows // tm
    nr, nb, nc = len(rows), len(bcs), len(cts)
    ndr, ndb = len(diff_rows), len(diff_bcs)

    def body(*refs):
        i = pl.program_id(0)
        rvals = [r[...].astype(F32) for r in refs[:nr]]
        bvals = [b[0].astype(F32) for b in refs[nr:nr + nb]]
        cvals = [c[...].astype(F32) for c in refs[nr + nb:nr + nb + nc]]
        outs = refs[nr + nb + nc:]

        def f(*d):
            rv, bv = list(rvals), list(bvals)
            for k, idx in enumerate(diff_rows):
                rv[idx] = d[k]
            for k, idx in enumerate(diff_bcs):
                bv[idx] = d[ndr + k]
            return tuple(fn(*rv, *bv))

        primals = [rvals[k] for k in diff_rows] + [bvals[k] for k in diff_bcs]
        _, vjp = jax.vjp(f, *primals)
        grads = vjp(tuple(cvals))
        for k in range(ndr):
            outs[k][...] = grads[k].astype(outs[k].dtype)
        for k, idx in enumerate(diff_bcs):
            o_ref = outs[ndr + k]
            first = (i == 0)
            if bcs[idx].shape[0] == 2:
                first = first | (i == ncb)

            @pl.when(first)
            def _(o_ref=o_ref):
                o_ref[...] = jnp.zeros_like(o_ref)

            o_ref[0] += grads[ndr + k]

    out_shape = [jax.ShapeDtypeStruct((n_rows, rows[k].width), F32) for k in diff_rows]
    out_shape += [jax.ShapeDtypeStruct(bcs[k].shape, F32) for k in diff_bcs]
    out_specs = [pl.BlockSpec((tm, rows[k].width), lambda i: (i, 0)) for k in diff_rows]
    out_specs += [_bc_spec(bcs[k], ncb) for k in diff_bcs]
    outs = pl.pallas_call(
        body, out_shape=out_shape, grid=(n_rows // tm,),
        in_specs=[r.spec(tm) for r in rows] + [_bc_spec(b, ncb) for b in bcs] + [c.spec(tm) for c in cts],
        out_specs=out_specs,
        compiler_params=_params(("arbitrary",), VMEM_LIMIT), name=name)(
            *[r.arr for r in rows], *bcs, *[c.arr for c in cts])
    return outs


def _rms(x):
    return x * lax.rsqrt(jnp.mean(x * x, axis=-1, keepdims=True) + EPS)


def _sigmoid(x):
    return 0.5 * (jnp.tanh(0.5 * x) + 1.0)


def _silu(x):
    return x * _sigmoid(x)


def _gelu_tanh(x):
    return 0.5 * x * (1.0 + jnp.tanh(math.sqrt(2.0 / math.pi) * (x + 0.044715 * (x * x * x))))


def f_norm_mod(x, g, sc, sh):
    return ((_rms(x) * g) * (1.0 + sc) + sh,)


def f_rms(x, g):
    return (_rms(x) * g,)


def f_gate(o, z):
    return (o * _silu(z),)


def f_res(x, o, gt):
    return (x + gt * o,)


def f_s5_act(y, u, d):
    return (_gelu_tanh(y + d * u),)


def f_s5_glu(ya, gl, z, b):
    return (ya * _sigmoid(gl + b) * _silu(z),)


def loss_and_grad(x2, final_g, target, name="loss_head"):
    n, d = x2.shape
    tm = math.gcd(ROW_TILE, n)

    def row_loss(x, g, t):
        y = _rms(x) * g
        e = y - t
        return 0.5 * (e * e) * (1.0 / d)

    def body(x_ref, g_ref, t_ref, l_ref, dx_ref, dg_ref):
        @pl.when(pl.program_id(0) == 0)
        def _():
            l_ref[...] = jnp.zeros_like(l_ref)
            dg_ref[...] = jnp.zeros_like(dg_ref)

        t = t_ref[...]
        lterm, vjp = jax.vjp(lambda x, g: row_loss(x, g, t), x_ref[...], g_ref[...])
        dx, dg = vjp(jnp.ones_like(lterm))
        l_ref[...] += jnp.sum(lterm, axis=0, keepdims=True)
        dx_ref[...] = dx
        dg_ref[...] += dg

    return pl.pallas_call(
        body,
        out_shape=[jax.ShapeDtypeStruct((1, d), F32), jax.ShapeDtypeStruct((n, d), F32),
                   jax.ShapeDtypeStruct((1, d), F32)],
        grid=(n // tm,),
        in_specs=[pl.BlockSpec((tm, d), lambda i: (i, 0)), pl.BlockSpec((1, d), lambda i: (0, 0)),
                  pl.BlockSpec((tm, d), lambda i: (i, 0))],
        out_specs=[pl.BlockSpec((1, d), lambda i: (0, 0)), pl.BlockSpec((tm, d), lambda i: (i, 0)),
                   pl.BlockSpec((1, d), lambda i: (0, 0))],
        compiler_params=_params(("arbitrary",), VMEM_LIMIT), name=name)(x2, final_g, target)


NT_DIMS = (((1,), (1,)), ((), ()))


def attn_fwd(qb, kb, vb, n_ctx):
    H, T, DK = qb.shape
    DV = vb.shape[-1]
    tq = math.gcd(ROW_TILE, n_ctx)
    nq, ncb = T // tq, n_ctx // tq

    def body(q_ref, k_ref, v_ref, o_ref, lse_ref):
        qi = pl.program_id(1)

        def rows(n_keys):
            k = k_ref[0, :n_keys, :]
            v = v_ref[0, :n_keys, :]
            s = lax.dot_general(q_ref[0], k, NT_DIMS, preferred_element_type=F32) * SOFTMAX_SCALE
            m = jnp.max(s, axis=-1, keepdims=True)
            p = jnp.exp(s - m)
            l = jnp.sum(p, axis=-1, keepdims=True)
            o_ref[0] = jnp.dot(p.astype(BF16), v, preferred_element_type=F32) / l
            lse_ref[0] = m + jnp.log(l)

        pl.when(qi < ncb)(lambda: rows(n_ctx))
        pl.when(qi >= ncb)(lambda: rows(T))

    return pl.pallas_call(
        body,
        out_shape=[jax.ShapeDtypeStruct((H, T, DV), F32), jax.ShapeDtypeStruct((H, T, 1), F32)],
        grid=(H, nq),
        in_specs=[pl.BlockSpec((1, tq, DK), lambda h, i: (h, i, 0)),
                  pl.BlockSpec((1, T, DK), lambda h, i: (h, 0, 0)),
                  pl.BlockSpec((1, T, DV), lambda h, i: (h, 0, 0))],
        out_specs=[pl.BlockSpec((1, tq, DV), lambda h, i: (h, i, 0)),
                   pl.BlockSpec((1, tq, 1), lambda h, i: (h, i, 0))],
        compiler_params=_params(("parallel", "parallel"), VMEM_LIMIT), name="attn_fwd")(qb, kb, vb)


def attn_bwd_dq(qb, kb, vb, o, do, lse, n_ctx):
    H, T, DK = qb.shape
    DV = vb.shape[-1]
    tq = math.gcd(ROW_TILE, n_ctx)
    nq, ncb = T // tq, n_ctx // tq

    def body(q_ref, k_ref, v_ref, o_ref, do_ref, lse_ref, dq_ref, delta_ref):
        qi = pl.program_id(1)

        def rows(n_keys):
            k = k_ref[0, :n_keys, :]
            v = v_ref[0, :n_keys, :]
            do = do_ref[0]
            delta = jnp.sum(do * o_ref[0], axis=-1, keepdims=True)
            s = lax.dot_general(q_ref[0], k, NT_DIMS, preferred_element_type=F32) * SOFTMAX_SCALE
            p = jnp.exp(s - lse_ref[0])
            dp = lax.dot_general(do.astype(BF16), v, NT_DIMS, preferred_element_type=F32)
            ds = p * (dp - delta) * SOFTMAX_SCALE
            dq_ref[0] = jnp.dot(ds.astype(BF16), k, preferred_element_type=F32)
            delta_ref[0] = delta

        pl.when(qi < ncb)(lambda: rows(n_ctx))
        pl.when(qi >= ncb)(lambda: rows(T))

    return pl.pallas_call(
        body,
        out_shape=[jax.ShapeDtypeStruct((H, T, DK), F32), jax.ShapeDtypeStruct((H, T, 1), F32)],
        grid=(H, nq),
        in_specs=[pl.BlockSpec((1, tq, DK), lambda h, i: (h, i, 0)),
                  pl.BlockSpec((1, T, DK), lambda h, i: (h, 0, 0)),
                  pl.BlockSpec((1, T, DV), lambda h, i: (h, 0, 0)),
                  pl.BlockSpec((1, tq, DV), lambda h, i: (h, i, 0)),
                  pl.BlockSpec((1, tq, DV), lambda h, i: (h, i, 0)),
                  pl.BlockSpec((1, tq, 1), lambda h, i: (h, i, 0))],
        out_specs=[pl.BlockSpec((1, tq, DK), lambda h, i: (h, i, 0)),
                   pl.BlockSpec((1, tq, 1), lambda h, i: (h, i, 0))],
        compiler_params=_params(("parallel", "parallel"), VMEM_LIMIT), name="attn_bwd_dq")(
            qb, kb, vb, o, do, lse)


def attn_bwd_dkv(qb, kb, vb, do, lse_rows, delta_rows, n_ctx):
    H, T, DK = qb.shape
    DV = vb.shape[-1]
    tq = math.gcd(ROW_TILE, n_ctx)
    nq, ncb = T // tq, n_ctx // tq

    def body(q_ref, do_ref, lse_ref, delta_ref, k_ref, v_ref, dk_ref, dv_ref):
        kj = pl.program_id(1)

        def cols(first):
            k = k_ref[0]
            v = v_ref[0]
            q = q_ref[0, first:, :]
            do16 = do_ref[0, first:, :].astype(BF16)
            st = lax.dot_general(k, q, NT_DIMS, preferred_element_type=F32) * SOFTMAX_SCALE
            pt = jnp.exp(st - lse_ref[0, :, first:])
            dv_ref[0] = jnp.dot(pt.astype(BF16), do16, preferred_element_type=F32)
            dpt = lax.dot_general(v, do16, NT_DIMS, preferred_element_type=F32)
            dst = pt * (dpt - delta_ref[0, :, first:]) * SOFTMAX_SCALE
            dk_ref[0] = jnp.dot(dst.astype(BF16), q, preferred_element_type=F32)

        pl.when(kj < ncb)(lambda: cols(0))
        pl.when(kj >= ncb)(lambda: cols(n_ctx))

    return pl.pallas_call(
        body,
        out_shape=[jax.ShapeDtypeStruct((H, T, DK), F32), jax.ShapeDtypeStruct((H, T, DV), F32)],
        grid=(H, nq),
        in_specs=[pl.BlockSpec((1, T, DK), lambda h, j: (h, 0, 0)),
                  pl.BlockSpec((1, T, DV), lambda h, j: (h, 0, 0)),
                  pl.BlockSpec((1, 1, T), lambda h, j: (h, 0, 0)),
                  pl.BlockSpec((1, 1, T), lambda h, j: (h, 0, 0)),
                  pl.BlockSpec((1, tq, DK), lambda h, j: (h, j, 0)),
                  pl.BlockSpec((1, tq, DV), lambda h, j: (h, j, 0))],
        out_specs=[pl.BlockSpec((1, tq, DK), lambda h, j: (h, j, 0)),
                   pl.BlockSpec((1, tq, DV), lambda h, j: (h, j, 0))],
        compiler_params=_params(("parallel", "parallel"), VMEM_LIMIT), name="attn_bwd_dkv")(
            qb, do, lse_rows, delta_rows, kb, vb)


def _cmul(ar, ai, br, bi):
    return ar * br - ai * bi, ar * bi + ai * br


def s5_chain(finals, s0, a, n_steps, reverse, name):
    W = finals.shape[-1]
    first = N_SEG - 1 if reverse else 0

    def body(f_ref, s0_ref, a_ref, c_ref):
        pr, pi = jnp.ones((1, W), F32), jnp.zeros((1, W), F32)
        br, bi = a_ref[0], a_ref[1]
        n = n_steps
        while n:
            if n & 1:
                pr, pi = _cmul(pr, pi, br, bi)
            br, bi = _cmul(br, bi, br, bi)
            n >>= 1
        fr, fi = f_ref[0], f_ref[1]
        row = lax.broadcasted_iota(jnp.int32, (N_SEG, W), 0)
        s0r = jnp.broadcast_to(s0_ref[0], (N_SEG, W))
        s0i = jnp.broadcast_to(s0_ref[1], (N_SEG, W))
        cr = jnp.where(row == first, s0r, 0.0)
        ci = jnp.where(row == first, s0i, 0.0)
        shift = N_SEG - 1 if reverse else 1
        for _ in range(N_SEG - 1):
            mr, mi = _cmul(pr, pi, cr, ci)
            tr = pltpu.roll(fr + mr, shift, 0)
            ti = pltpu.roll(fi + mi, shift, 0)
            cr = jnp.where(row == first, s0r, tr)
            ci = jnp.where(row == first, s0i, ti)
        c_ref[0] = cr
        c_ref[1] = ci

    return pl.pallas_call(body, out_shape=jax.ShapeDtypeStruct((2, N_SEG, W), F32), name=name)(finals, s0, a)


def _scan_chunk(bur, bui, st_ref, a_ref, n_steps, reverse):
    for lc in range(S5_LANES // BLK_ST):
        sl = slice(lc * BLK_ST, (lc + 1) * BLK_ST)
        lr = jnp.broadcast_to(a_ref[0, :, sl], (N_SEG, BLK_ST))
        li = jnp.broadcast_to(a_ref[1, :, sl], (N_SEG, BLK_ST))

        def step(jj, carry, sl=sl, lr=lr, li=li):
            sr, si = carry
            j = (n_steps - 1 - jj) if reverse else jj
            r0 = pl.multiple_of(j * N_SEG, N_SEG)
            nr = lr * sr - li * si + bur[pl.ds(r0, N_SEG), sl]
            ni = lr * si + li * sr + bui[pl.ds(r0, N_SEG), sl]
            bur[pl.ds(r0, N_SEG), sl] = nr
            bui[pl.ds(r0, N_SEG), sl] = ni
            return nr, ni

        sr, si = lax.fori_loop(0, n_steps, step, (st_ref[0, :, sl], st_ref[1, :, sl]))
        st_ref[0, :, sl] = sr
        st_ref[1, :, sl] = si


def _project_in(x16, w_re, w_im, bur, bui, adjoint):
    for gb in range(N_BLOCKS):
        xb = x16[:, gb * BLK_CH:(gb + 1) * BLK_CH]
        sl = slice(gb * BLK_ST, (gb + 1) * BLK_ST)
        if adjoint:
            dn = (((1,), (1,)), ((), ()))
            bur[:, sl] = lax.dot_general(xb, w_re[gb], dn, preferred_element_type=F32)
            bui[:, sl] = -lax.dot_general(xb, w_im[gb], dn, preferred_element_type=F32)
        else:
            bur[:, sl] = jnp.dot(xb, w_re[gb], preferred_element_type=F32)
            bui[:, sl] = jnp.dot(xb, w_im[gb], preferred_element_type=F32)


def s5_scan(act, w_re, w_im, a, init, *, reverse, adjoint=False, c_re=None, c_im=None, add=None,
            want_ckpt=False, rows=None, name):
    act_off, N = rows if rows is not None else (0, act.shape[0])
    R = math.gcd(ROW_TILE, N, act_off)
    nch, jc = N // R, R // N_SEG
    with_out = c_re is not None

    def chunk(i):
        return (nch - 1 - i) if reverse else i

    def body(*refs):
        act_ref, wre_ref, wim_ref, a_ref, init_ref = refs[:5]
        k = 5
        if with_out:
            cre_ref, cim_ref = refs[k:k + 2]
            k += 2
        if add is not None:
            add_ref = refs[k]
            k += 1
        if with_out:
            out_ref = refs[k]
            k += 1
        if want_ckpt:
            ck_ref = refs[k]
            k += 1
        fin_ref, bur, bui = refs[k:k + 3]

        @pl.when(pl.program_id(0) == 0)
        def _():
            fin_ref[...] = init_ref[...]

        if want_ckpt:
            ck_ref[0] = fin_ref[...]
        _project_in(act_ref[...].astype(BF16), wre_ref, wim_ref, bur, bui, adjoint)
        _scan_chunk(bur, bui, fin_ref, a_ref, jc, reverse)
        if with_out:
            for gb in range(N_BLOCKS):
                sl = slice(gb * BLK_ST, (gb + 1) * BLK_ST)
                y = (jnp.dot(bur[:, sl].astype(BF16), cre_ref[gb], preferred_element_type=F32)
                     - jnp.dot(bui[:, sl].astype(BF16), cim_ref[gb], preferred_element_type=F32))
                cs = slice(gb * BLK_CH, (gb + 1) * BLK_CH)
                if add is not None:
                    y = y + add_ref[:, cs]
                out_ref[:, cs] = y

    row_spec = pl.BlockSpec((R, D_MODEL), lambda i: (chunk(i), 0))
    act_spec = pl.BlockSpec((R, D_MODEL), lambda i: (chunk(i) + act_off // R, 0))
    w_spec = pl.BlockSpec(w_re.shape, lambda i: (0, 0, 0))
    st_spec = pl.BlockSpec((2, N_SEG, S5_LANES), lambda i: (0, 0, 0))
    ins = [act, w_re, w_im, a, init]
    in_specs = [act_spec, w_spec, w_spec, pl.BlockSpec((2, 1, S5_LANES), lambda i: (0, 0, 0)), st_spec]
    if with_out:
        ins += [c_re, c_im]
        in_specs += [pl.BlockSpec(c_re.shape, lambda i: (0, 0, 0))] * 2
    if add is not None:
        ins.append(add)
        in_specs.append(row_spec)
    out_shape, out_specs = [], []
    if with_out:
        out_shape.append(jax.ShapeDtypeStruct((N, D_MODEL), F32))
        out_specs.append(row_spec)
    if want_ckpt:
        out_shape.append(jax.ShapeDtypeStruct((nch, 2, N_SEG, S5_LANES), F32))
        out_specs.append(pl.BlockSpec((1, 2, N_SEG, S5_LANES), lambda i: (chunk(i), 0, 0, 0)))
    out_shape.append(jax.ShapeDtypeStruct((2, N_SEG, S5_LANES), F32))
    out_specs.append(st_spec)
    res = pl.pallas_call(
        body, out_shape=out_shape, grid=(nch,), in_specs=in_specs, out_specs=out_specs,
        scratch_shapes=[pltpu.VMEM((R, S5_LANES), F32), pltpu.VMEM((R, S5_LANES), F32)],
        compiler_params=_params(("arbitrary",), VMEM_LIMIT), name=name)(*ins)
    res = list(res)
    out = res.pop(0) if with_out else None
    ckpt = res.pop(0) if want_ckpt else None
    return out, ckpt, res[0]


def s5_grads(dy, u, ckpt, b_re, b_im, c_re, c_im, lam, init_adj, *, reverse, add=None, u_off=0, name):
    N = dy.shape[0]
    R = math.gcd(ROW_TILE, N, u_off)
    nch, jc = N // R, R // N_SEG
    W = S5_LANES

    def chunk(i):
        return i if reverse else (nch - 1 - i)

    def body(*refs):
        dy_ref, u_ref, ck_ref, bre_ref, bim_ref, cre_ref, cim_ref, lam_ref, init_ref = refs[:9]
        k = 9
        if add is not None:
            add_ref = refs[k]
            k += 1
        du_ref, dlam_ref, dbre_ref, dbim_ref, dcre_ref, dcim_ref, fin_ref = refs[k:k + 7]
        sr_buf, si_buf, er_buf, ei_buf, st_buf = refs[k + 7:k + 12]

        @pl.when(pl.program_id(0) == 0)
        def _():
            fin_ref[...] = init_ref[...]
            dlam_ref[...] = jnp.zeros_like(dlam_ref)
            dbre_ref[...] = jnp.zeros_like(dbre_ref)
            dbim_ref[...] = jnp.zeros_like(dbim_ref)
            dcre_ref[...] = jnp.zeros_like(dcre_ref)
            dcim_ref[...] = jnp.zeros_like(dcim_ref)

        u16 = u_ref[...].astype(BF16)
        dy16 = dy_ref[...].astype(BF16)
        st_buf[...] = ck_ref[0]
        _project_in(u16, bre_ref, bim_ref, sr_buf, si_buf, False)
        _scan_chunk(sr_buf, si_buf, st_buf, lam_ref, jc, reverse)
        _project_in(dy16, cre_ref, cim_ref, er_buf, ei_buf, True)
        for lc in range(W // BLK_ST):
            sl = slice(lc * BLK_ST, (lc + 1) * BLK_ST)
            lr = jnp.broadcast_to(lam_ref[0, :, sl], (N_SEG, BLK_ST))
            li = jnp.broadcast_to(lam_ref[1, :, sl], (N_SEG, BLK_ST))

            def one(r0, spr, spi, carry, sl=sl, lr=lr, li=li):
                gr, gi, ar, ai = carry
                nr = er_buf[pl.ds(r0, N_SEG), sl] + lr * gr + li * gi
                ni = ei_buf[pl.ds(r0, N_SEG), sl] + lr * gi - li * gr
                er_buf[pl.ds(r0, N_SEG), sl] = nr
                ei_buf[pl.ds(r0, N_SEG), sl] = ni
                return nr, ni, ar + spr * nr + spi * ni, ai + spr * ni - spi * nr

            def step(ff, carry, sl=sl, one=one):
                f = jc - 1 - ff
                j = (jc - 1 - f) if reverse else f
                jp = (j + 1) if reverse else (j - 1)
                r0 = pl.multiple_of(j * N_SEG, N_SEG)
                p0 = pl.multiple_of(jp * N_SEG, N_SEG)
                return one(r0, sr_buf[pl.ds(p0, N_SEG), sl], si_buf[pl.ds(p0, N_SEG), sl], carry)

            carry = (fin_ref[0, :, sl], fin_ref[1, :, sl], dlam_ref[0, :, sl], dlam_ref[1, :, sl])
            carry = lax.fori_loop(0, jc - 1, step, carry)
            r_first = (jc - 1) * N_SEG if reverse else 0
            gr, gi, ar, ai = one(r_first, ck_ref[0, 0, :, sl], ck_ref[0, 1, :, sl], carry)
            fin_ref[0, :, sl] = gr
            fin_ref[1, :, sl] = gi
            dlam_ref[0, :, sl] = ar
            dlam_ref[1, :, sl] = ai
        tn = (((0,), (0,)), ((), ()))
        nt = (((1,), (1,)), ((), ()))
        for gb in range(N_BLOCKS):
            sl = slice(gb * BLK_ST, (gb + 1) * BLK_ST)
            cs = slice(gb * BLK_CH, (gb + 1) * BLK_CH)
            gr16 = er_buf[:, sl].astype(BF16)
            gi16 = ei_buf[:, sl].astype(BF16)
            du = (lax.dot_general(gr16, bre_ref[gb], nt, preferred_element_type=F32)
                  + lax.dot_general(gi16, bim_ref[gb], nt, preferred_element_type=F32))
            if add is not None:
                du = du + add_ref[:, cs]
            du_ref[:, cs] = du
            ub, dyb = u16[:, cs], dy16[:, cs]
            dbre_ref[gb] += lax.dot_general(ub, gr16, tn, preferred_element_type=F32)
            dbim_ref[gb] += lax.dot_general(ub, gi16, tn, preferred_element_type=F32)
            dcre_ref[gb] += lax.dot_general(sr_buf[:, sl].astype(BF16), dyb, tn, preferred_element_type=F32)
            dcim_ref[gb] -= lax.dot_general(si_buf[:, sl].astype(BF16), dyb, tn, preferred_element_type=F32)

    row_spec = pl.BlockSpec((R, D_MODEL), lambda i: (chunk(i), 0))
    st_spec = pl.BlockSpec((2, N_SEG, W), lambda i: (0, 0, 0))
    wb_spec = pl.BlockSpec(b_re.shape, lambda i: (0, 0, 0))
    wc_spec = pl.BlockSpec(c_re.shape, lambda i: (0, 0, 0))
    ins = [dy, u, ckpt, b_re, b_im, c_re, c_im, lam, init_adj]
    u_spec = pl.BlockSpec((R, D_MODEL), lambda i: (chunk(i) + u_off // R, 0))
    in_specs = [row_spec, u_spec, pl.BlockSpec((1, 2, N_SEG, W), lambda i: (chunk(i), 0, 0, 0)),
                wb_spec, wb_spec, wc_spec, wc_spec, pl.BlockSpec((2, 1, W), lambda i: (0, 0, 0)), st_spec]
    if add is not None:
        ins.append(add)
        in_specs.append(row_spec)
    out_shape = [jax.ShapeDtypeStruct((N, D_MODEL), F32), jax.ShapeDtypeStruct((2, N_SEG, W), F32),
                 jax.ShapeDtypeStruct(b_re.shape, F32), jax.ShapeDtypeStruct(b_re.shape, F32),
                 jax.ShapeDtypeStruct(c_re.shape, F32), jax.ShapeDtypeStruct(c_re.shape, F32),
                 jax.ShapeDtypeStruct((2, N_SEG, W), F32)]
    out_specs = [row_spec, st_spec, wb_spec, wb_spec, wc_spec, wc_spec, st_spec]
    return pl.pallas_call(
        body, out_shape=out_shape, grid=(nch,), in_specs=in_specs, out_specs=out_specs,
        scratch_shapes=[pltpu.VMEM((R, W), F32) for _ in range(4)] + [pltpu.VMEM((2, N_SEG, W), F32)],
        compiler_params=_params(("arbitrary",), VMEM_LIMIT), name=name)(*ins)


def adamw(w, g, m, v, name="adamw"):
    n, d = w.shape
    tm = n if n <= 512 else 256
    assert n % tm == 0
    c1 = 1.0 - ADAM_B1 ** ADAM_STEP
    c2 = 1.0 - ADAM_B2 ** ADAM_STEP

    def body(w_ref, g_ref, m_ref, v_ref, d_ref, nm_ref, nv_ref):
        g_ = g_ref[...]
        m_ = ADAM_B1 * m_ref[...] + (1.0 - ADAM_B1) * g_
        v_ = ADAM_B2 * v_ref[...] + (1.0 - ADAM_B2) * (g_ * g_)
        d_ref[...] = -ADAM_LR * ((m_ / c1) / (jnp.sqrt(v_ / c2) + ADAM_EPS) + ADAM_WD * w_ref[...])
        nm_ref[...] = m_
        nv_ref[...] = v_

    spec = pl.BlockSpec((tm, d), lambda i: (i, 0))
    return pl.pallas_call(
        body, out_shape=[jax.ShapeDtypeStruct((n, d), F32)] * 3, grid=(n // tm,),
        in_specs=[spec] * 4, out_specs=[spec] * 3,
        compiler_params=_params(("parallel",), VMEM_LIMIT), name=name)(w, g, m, v)


def _coords():
    return lax.axis_index("x"), lax.axis_index("y"), lax.axis_index("c")


def exchange(arrays, out_shapes, remote, local, name):
    n_in, n_out, n_rem, n_loc = len(arrays), len(out_shapes), len(remote), len(local)

    def at(ref, idx):
        return ref if idx is None else ref.at[idx]

    def body(*refs):
        ins, outs = refs[:n_in], refs[n_in:n_in + n_out]
        send_sems, recv_sems, local_sems = refs[n_in + n_out:]
        me = _coords()
        sends, recvs = [], []
        for k, (flip, ii, src_at, oi, dst_at) in enumerate(remote):
            peer = (me[0] ^ flip[0], me[1] ^ flip[1], me[2] ^ flip[2])
            src = at(ins[ii], src_at(me, peer))
            sends.append(pltpu.make_async_remote_copy(
                src_ref=src, dst_ref=at(outs[oi], dst_at(me)), send_sem=send_sems.at[k], recv_sem=recv_sems.at[k],
                device_id=peer, device_id_type=MESH))
            recvs.append(pltpu.make_async_remote_copy(
                src_ref=src, dst_ref=at(outs[oi], dst_at(peer)), send_sem=send_sems.at[k], recv_sem=recv_sems.at[k],
                device_id=peer, device_id_type=MESH))
        locs = [pltpu.make_async_copy(at(ins[ii], src_at(me)), at(outs[oi], dst_at(me)), local_sems.at[k])
                for k, (ii, src_at, oi, dst_at) in enumerate(local)]
        for cp in locs + sends:
            cp.start()
        for cp in recvs:
            cp.wait_recv()
        for cp in sends:
            cp.wait_send()
        for cp in locs:
            cp.wait()

    hbm = pl.BlockSpec(memory_space=pl.ANY)
    return pl.pallas_call(
        body, out_shape=list(out_shapes), in_specs=[hbm] * n_in, out_specs=[hbm] * n_out,
        scratch_shapes=[pltpu.SemaphoreType.DMA((n_rem,)), pltpu.SemaphoreType.DMA((n_rem,)),
                        pltpu.SemaphoreType.DMA((max(n_loc, 1),))],
        name=name)(*arrays)


ALL_FLIPS = [(dx, dy, dc) for dx in (0, 1) for dy in (0, 1) for dc in (0, 1)][1:]
CHIP_FLIPS = [(1, 0, 0), (0, 1, 0), (1, 1, 0)]
CORE_FLIP = (0, 0, 1)


def _dev_index(p):
    return 4 * p[0] + 2 * p[1] + p[2]


def _chip_index(p):
    return 2 * p[0] + p[1]


def _gather(xs, flips, index, n, name):
    arrays = [x[None] for x in xs]
    outs = [jax.ShapeDtypeStruct((n,) + x.shape, x.dtype) for x in xs]
    remote = [(f, a, lambda me, peer: (0,), a, lambda s: (index(s),)) for a in range(len(xs)) for f in flips]
    local = [(a, lambda me: (0,), a, lambda me: (index(me),)) for a in range(len(xs))]
    return exchange(arrays, outs, remote, local, name)


def allgather_devices(x, name):
    return _gather([x], ALL_FLIPS, _dev_index, N_DEV, name)[0]


def allgather_chips(xs, name):
    return _gather(xs, CHIP_FLIPS, _chip_index, N_CHIP, name)


def pair_add(g, got, core, out_dtype, name):
    _, _, h, cd = g.shape
    th = math.gcd(512, h)

    def body(c_ref, g_ref, got_ref, o_ref):
        o_ref[0] = (g_ref[0, 0] + got_ref[0]).astype(o_ref.dtype)

    return pl.pallas_call(
        body, out_shape=jax.ShapeDtypeStruct((N_CHIP, h, cd), out_dtype),
        grid_spec=pltpu.PrefetchScalarGridSpec(
            num_scalar_prefetch=1, grid=(N_CHIP, h // th),
            in_specs=[pl.BlockSpec((1, 1, th, cd), lambda q, i, c: (q, c[0], i, 0)),
                      pl.BlockSpec((1, th, cd), lambda q, i, c: (q, i, 0))],
            out_specs=pl.BlockSpec((1, th, cd), lambda q, i, c: (q, i, 0))),
        compiler_params=_params(("parallel", "parallel"), VMEM_LIMIT), name=name)(core, g, got)


def sum_chips(parts, name):
    _, h, cd = parts.shape
    th = math.gcd(512, h)

    def body(p_ref, o_ref):
        acc = p_ref[0].astype(F32)
        for q in range(1, N_CHIP):
            acc = acc + p_ref[q].astype(F32)
        o_ref[...] = acc

    return pl.pallas_call(
        body, out_shape=jax.ShapeDtypeStruct((h, cd), F32), grid=(h // th,),
        in_specs=[pl.BlockSpec((N_CHIP, th, cd), lambda i: (0, i, 0))], out_specs=pl.BlockSpec((th, cd), lambda i: (i, 0)),
        compiler_params=_params(("parallel",), VMEM_LIMIT), name=name)(parts)


def to_segments(a, n_ctx):
    def one(p):
        n = p.shape[0]
        return p.reshape(N_SEG, n // N_SEG, -1).transpose(1, 0, 2).reshape(n, -1)
    return jnp.concatenate([one(a[:n_ctx]), one(a[n_ctx:])], axis=0) if n_ctx else one(a)


def from_segments(a, n_ctx):
    def one(p):
        n = p.shape[0]
        return p.reshape(n // N_SEG, N_SEG, -1).transpose(1, 0, 2).reshape(n, -1)
    return jnp.concatenate([one(a[:n_ctx]), one(a[n_ctx:])], axis=0) if n_ctx else one(a)


def rope_tables(n_ctx, n_lat):
    rows = n_lat // GRID_W
    row = jnp.repeat(jnp.arange(rows, dtype=jnp.int32), GRID_W)
    col = jnp.tile(jnp.arange(GRID_W, dtype=jnp.int32), rows)
    d = QK_ROPE_DIM // 2
    inv = 1.0 / (ROPE_THETA ** (jnp.arange(0, d, 2, dtype=F32) / d))
    ang = jnp.concatenate([row.astype(F32)[:, None] * inv[None, :], col.astype(F32)[:, None] * inv[None, :]], axis=1)
    cos = jnp.concatenate([jnp.ones((n_ctx, d), F32), jnp.cos(ang)], axis=0)
    sin = jnp.concatenate([jnp.zeros((n_ctx, d), F32), jnp.sin(ang)], axis=0)
    return cos, sin


def rope(x, cos, sin):
    q = QK_ROPE_DIM // 4
    a, b, c, d = x[..., :q], x[..., q:2 * q], x[..., 2 * q:3 * q], x[..., 3 * q:]
    cr, cc = cos[:, :q], cos[:, q:]
    sr, sc = sin[:, :q], sin[:, q:]
    return jnp.concatenate([a * cr - b * sr, a * sr + b * cr, c * cc - d * sc, c * sc + d * cc], axis=-1)


def heads_layout(q, kv, kr, cos, sin):
    T = q.shape[0]
    qh = q.reshape(T, MLA_HEADS, QK_DIM).transpose(1, 0, 2)
    qh = jnp.concatenate([qh[..., :QK_NOPE_DIM], rope(qh[..., QK_NOPE_DIM:], cos, sin)], axis=-1)
    kvh = kv.reshape(T, MLA_HEADS, QK_NOPE_DIM + V_HEAD_DIM).transpose(1, 0, 2)
    krr = jnp.broadcast_to(rope(kr, cos, sin)[None], (MLA_HEADS, T, QK_ROPE_DIM))
    kh = jnp.concatenate([kvh[..., :QK_NOPE_DIM], krr], axis=-1)
    return qh, kh, kvh[..., QK_NOPE_DIM:]


def s5_discretise(a_re, a_im, log_step, b_re, b_im):
    dt = jnp.exp(log_step)[:, None]
    mag = jnp.exp(a_re * dt)
    lb_re = mag * jnp.cos(a_im * dt)
    lb_im = mag * jnp.sin(a_im * dt)
    den = a_re * a_re + a_im * a_im
    nr = lb_re - 1.0
    f_re = ((nr * a_re + lb_im * a_im) / den)[..., None]
    f_im = ((lb_im * a_re - nr * a_im) / den)[..., None]
    return lb_re, lb_im, f_re * b_re - f_im * b_im, f_re * b_im + f_im * b_re


def s5_block_weights(lb_re, lb_im, bb_re, bb_im, c_re, c_im):
    eye = jnp.eye(GROUPS_PER_BLOCK, dtype=F32)
    lam = jnp.stack([lb_re.reshape(1, S5_LANES), lb_im.reshape(1, S5_LANES)])

    def b_blocks(bb):
        t = bb.reshape(N_BLOCKS, GROUPS_PER_BLOCK, S5_STATE, S5_GROUP)
        return jnp.einsum("bgpc,gh->bgchp", t, eye).reshape(N_BLOCKS, BLK_CH, BLK_ST).astype(BF16)

    def c_blocks(cc):
        t = cc.reshape(N_BLOCKS, GROUPS_PER_BLOCK, S5_GROUP, S5_STATE)
        return jnp.einsum("bgcp,gh->bgphc", t, eye).reshape(N_BLOCKS, BLK_ST, BLK_CH).astype(BF16)

    return lam, b_blocks(bb_re), b_blocks(bb_im), c_blocks(c_re), c_blocks(c_im)


def b_block_diag(db):
    t = db.reshape(N_BLOCKS, GROUPS_PER_BLOCK, S5_GROUP, GROUPS_PER_BLOCK, S5_STATE)
    return jnp.einsum("bgchp,gh->bgpc", t, jnp.eye(GROUPS_PER_BLOCK, dtype=F32)).reshape(S5_GROUPS, S5_STATE, S5_GROUP)


def c_block_diag(dc):
    t = dc.reshape(N_BLOCKS, GROUPS_PER_BLOCK, S5_STATE, GROUPS_PER_BLOCK, S5_GROUP)
    return jnp.einsum("bgphc,gh->bgcp", t, jnp.eye(GROUPS_PER_BLOCK, dtype=F32)).reshape(S5_GROUPS, S5_GROUP, S5_STATE)


def conj(a):
    return jnp.stack([a[0], -a[1]])


PACK_TILE = 16 * 128


def pack_flat(parts, dtype):
    flat = [p.reshape(-1).astype(dtype) for p in parts]
    sizes = [f.shape[0] for f in flat]
    total = sum(sizes)
    pad = (-total) % PACK_TILE
    if pad:
        flat.append(jnp.zeros((pad,), dtype))
    offs = np.cumsum([0] + sizes)[:-1].tolist()
    return jnp.concatenate(flat).reshape(-1, 128), offs


def unpack_flat(buf, offs, shapes):
    flat = buf.reshape(-1)
    return [flat[o:o + int(np.prod(s))].reshape(s) for o, s in zip(offs, shapes)]


def s5_forward(p1, n_ctx, dirs):
    saved = []
    y = None
    ctx_rows, lat_rows = (0, n_ctx), (n_ctx, p1.shape[0] - n_ctx)
    zeros_tile = jnp.zeros((2, N_SEG, S5_LANES), F32)
    zeros_row = jnp.zeros((2, 1, S5_LANES), F32)
    for k, (lam, b_re, b_im, c_re, c_im) in enumerate(dirs):
        rev = k == 1
        last = 0 if rev else N_SEG - 1
        _, _, fin = s5_scan(p1, b_re, b_im, lam, zeros_tile, reverse=rev, rows=ctx_rows, name=f"s5_ctx_finals{k}")
        carry_c = s5_chain(fin, zeros_row, lam, n_ctx // N_SEG, rev, name=f"s5_ctx_chain{k}")
        _, ck_c, fin_c = s5_scan(p1, b_re, b_im, lam, carry_c, reverse=rev, want_ckpt=True, rows=ctx_rows,
                                 name=f"s5_ctx_scan{k}")
        s0 = fin_c[:, last:last + 1, :]
        _, _, fin = s5_scan(p1, b_re, b_im, lam, zeros_tile, reverse=rev, rows=lat_rows, name=f"s5_lat_finals{k}")
        carry_l = s5_chain(fin, s0, lam, lat_rows[1] // N_SEG, rev, name=f"s5_lat_chain{k}")
        y, ck_l, _ = s5_scan(p1, b_re, b_im, lam, carry_l, reverse=rev, c_re=c_re, c_im=c_im, add=y,
                             want_ckpt=True, rows=lat_rows, name=f"s5_lat_scan{k}")
        saved.append((ck_c, ck_l))
    return y, saved


def s5_backward(dy_l, du_extra_l, p1, n_ctx, dirs, saved):
    n_lat = p1.shape[0] - n_ctx
    zeros_tile = jnp.zeros((2, N_SEG, S5_LANES), F32)
    zeros_row = jnp.zeros((2, 1, S5_LANES), F32)
    dy_c = jnp.zeros((n_ctx, D_MODEL), F32)
    du_l, du_c = du_extra_l, None
    grads = []
    for k, (lam, b_re, b_im, c_re, c_im) in enumerate(dirs):
        rev = k == 1
        lam_c = conj(lam)
        ck_c, ck_l = saved[k]
        first = N_SEG - 1 if rev else 0
        _, _, fin = s5_scan(dy_l, c_re, c_im, lam_c, zeros_tile, reverse=not rev, adjoint=True,
                            name=f"s5_lat_adj_finals{k}")
        carry = s5_chain(fin, zeros_row, lam_c, n_lat // N_SEG, not rev, name=f"s5_lat_adj_chain{k}")
        du_l, dlam_l, dbr_l, dbi_l, dcr_l, dci_l, fin_a = s5_grads(
            dy_l, p1, ck_l, b_re, b_im, c_re, c_im, lam, carry, reverse=rev, add=du_l, u_off=n_ctx,
            name=f"s5_lat_grads{k}")
        g0 = fin_a[:, first:first + 1, :]
        carry = s5_chain(zeros_tile, g0, lam_c, n_ctx // N_SEG, not rev, name=f"s5_ctx_adj_chain{k}")
        du_c, dlam_c, dbr_c, dbi_c, _, _, _ = s5_grads(
            dy_c, p1, ck_c, b_re, b_im, c_re, c_im, lam, carry, reverse=rev, add=du_c, name=f"s5_ctx_grads{k}")
        dlam = jnp.sum(dlam_l + dlam_c, axis=1)
        grads.append((dlam, b_block_diag(dbr_l + dbr_c), b_block_diag(dbi_l + dbi_c),
                      c_block_diag(dcr_l), c_block_diag(dci_l)))
    return jnp.concatenate([du_c, du_l], axis=0), grads


def local_step(x, ctx, target, mod, w):
    L, Lc = x.shape[0], ctx.shape[0]
    T = L + Lc
    assert L % Lc == 0 and Lc % (2 * N_SEG) == 0 and L % GRID_W == 0
    D = D_MODEL
    X0 = jnp.concatenate([ctx, x], axis=0)

    def mod_of(i, j):
        return mod[i, :, j, :][:, None, :]

    def vec(v):
        return v.reshape(1, 1, -1).astype(F32)

    g0 = vec(w["norm_g"][0])
    (H0,) = rowwise_fwd(f_norm_mod, [X0], [g0, mod_of(0, 1), mod_of(0, 0)], [D], [BF16], T, Lc, "l0_norm")
    p0 = mm_nn(H0, w["mla_w_in"], name="l0_in")
    z0 = Rows(p0, D, col_blk=0)
    cq = Rows(p0, Q_LORA_RANK, col_blk=D // Q_LORA_RANK)
    ckv = Rows(p0, KV_LORA_RANK, col_blk=(D + Q_LORA_RANK) // KV_LORA_RANK)
    kr = p0[:, D + Q_LORA_RANK + KV_LORA_RANK:]
    qng, kvng = vec(w["mla_q_norm"]), vec(w["mla_kv_norm"])
    (qn,) = rowwise_fwd(f_rms, [cq], [qng], [Q_LORA_RANK], [BF16], T, 0, "l0_qnorm")
    (kvn,) = rowwise_fwd(f_rms, [ckv], [kvng], [KV_LORA_RANK], [BF16], T, 0, "l0_kvnorm")
    q = mm_nn(qn, w["mla_w_uq"], name="l0_uq")
    kv = mm_nn(kvn, w["mla_w_ukv"], name="l0_ukv")
    cos, sin = rope_tables(Lc, L)
    (qh, kh, vh), heads_vjp = jax.vjp(lambda q_, kv_, kr_: heads_layout(q_, kv_, kr_, cos, sin), q, kv, kr)
    qb, kb, vb = qh.astype(BF16), kh.astype(BF16), vh.astype(BF16)
    oh, lse = attn_fwd(qb, kb, vb, Lc)
    o = oh.transpose(1, 0, 2).reshape(T, MLA_HEADS * V_HEAD_DIM)
    (og,) = rowwise_fwd(f_gate, [o, z0], [], [D], [BF16], T, 0, "l0_gate")
    out0 = mm_nn(og, w["mla_w_out"], name="l0_out")
    (X1,) = rowwise_fwd(f_res, [X0, out0], [mod_of(0, 2)], [D], [F32], T, Lc, "l0_res")

    X1p = to_segments(X1, Lc)
    tgt_p = to_segments(target, 0)
    g1 = vec(w["norm_g"][1])
    (H1,) = rowwise_fwd(f_norm_mod, [X1p], [g1, mod_of(1, 1), mod_of(1, 0)], [D], [BF16], T, Lc, "l1_norm")
    p1 = mm_nn(H1, w["s5_w_in"], name="l1_in")
    disc_fn = lambda *a: tuple(zip(*[s5_discretise(a[0][k], a[1][k], a[2][k], a[3][k], a[4][k]) for k in range(2)]))
    disc, disc_vjp = jax.vjp(disc_fn, w["s5_a_re"], w["s5_a_im"], w["s5_log_step"], w["s5_b_re"], w["s5_b_im"])
    dirs = [s5_block_weights(disc[0][k], disc[1][k], disc[2][k], disc[3][k], w["s5_c_re"][k], w["s5_c_im"][k])
            for k in range(2)]
    y_ssm, s5_saved = s5_forward(p1, Lc, dirs)
    d_vec, bg_vec = vec(w["s5_d"]), vec(w["s5_b_glu"])
    u_lat = Rows(p1, D, row_off=Lc, col_blk=0)
    z_lat = Rows(p1, D, row_off=Lc, col_blk=1)
    (ya,) = rowwise_fwd(f_s5_act, [y_ssm, u_lat], [d_vec], [D], [F32], L, 0, "l1_act")
    gl = mm_nn(ya, w["s5_w_glu"], name="l1_glu")
    (y3,) = rowwise_fwd(f_s5_glu, [ya, gl, z_lat], [bg_vec], [D], [BF16], L, 0, "l1_gate")
    out1 = mm_nn(y3, w["s5_w_out"], name="l1_out")
    gt1 = mod[1, 1:2, 2, :][:, None, :]
    x1_lat = Rows(X1p, D, row_off=Lc)
    (X2,) = rowwise_fwd(f_res, [x1_lat, out1], [gt1], [D], [F32], L, 0, "l1_res")

    fg = w["final_g"].reshape(1, D).astype(F32)
    lvec, dX2, d_fg = loss_and_grad(X2, fg, tgt_p)
    loss = jnp.sum(lvec)
    gw = {"final_g": d_fg.reshape(D)}
    dmod = {}

    d_out1, d_gt1 = rowwise_bwd(f_res, [x1_lat, out1], [gt1], [dX2], [1], [0], L, 0, "l1_res_bwd")
    d_y3 = mm_nt(d_out1, w["s5_w_out"], name="l1_out_dx")
    gw["s5_w_out"] = mm_tn(y3, d_out1, name="l1_out_dw")
    d_ya_a, d_gl, d_z1, d_bg = rowwise_bwd(f_s5_glu, [ya, gl, z_lat], [bg_vec], [d_y3], [0, 1, 2], [0], L, 0,
                                           "l1_gate_bwd")
    gw["s5_b_glu"] = d_bg.reshape(-1)
    gw["s5_w_glu"] = mm_tn(ya, d_gl, name="l1_glu_dw")
    d_ya = d_ya_a + mm_nt(d_gl, w["s5_w_glu"], name="l1_glu_dx")
    d_yssm, d_u_act, d_d = rowwise_bwd(f_s5_act, [y_ssm, u_lat], [d_vec], [d_ya], [0, 1], [0], L, 0, "l1_act_bwd")
    gw["s5_d"] = d_d.reshape(-1)
    du_p, s5_g = s5_backward(d_yssm, d_u_act, p1, Lc, dirs, s5_saved)
    d_disc = tuple(tuple(s5_g[k][j - 1].reshape(disc[j][k].shape) if j >= 2 else
                         s5_g[k][0][j].reshape(disc[j][k].shape) for k in range(2)) for j in range(4))
    gw["s5_a_re"], gw["s5_a_im"], gw["s5_log_step"], gw["s5_b_re"], gw["s5_b_im"] = disc_vjp(d_disc)
    gw["s5_c_re"] = jnp.stack([s5_g[0][3], s5_g[1][3]])
    gw["s5_c_im"] = jnp.stack([s5_g[0][4], s5_g[1][4]])
    d_p1 = jnp.concatenate([du_p, jnp.concatenate([jnp.zeros((Lc, D), F32), d_z1], axis=0)], axis=1)
    d_H1 = mm_nt(d_p1, w["s5_w_in"], name="l1_in_dx")
    gw["s5_w_in"] = mm_tn(H1, d_p1, name="l1_in_dw")
    d_X1p, d_g1, d_sc1, d_sh1 = rowwise_bwd(f_norm_mod, [X1p], [g1, mod_of(1, 1), mod_of(1, 0)], [d_H1],
                                            [0], [0, 1, 2], T, Lc, "l1_norm_bwd")
    d_X1p = d_X1p + jnp.concatenate([jnp.zeros((Lc, D), F32), dX2], axis=0)
    d_gt1_full = jnp.concatenate([jnp.zeros((1, 1, D), F32), d_gt1], axis=0)
    dmod[1] = (d_sh1, d_sc1, d_gt1_full)
    d_X1 = from_segments(d_X1p, Lc)

    d_out0, d_gt0 = rowwise_bwd(f_res, [X0, out0], [mod_of(0, 2)], [d_X1], [1], [0], T, Lc, "l0_res_bwd")
    d_og = mm_nt(d_out0, w["mla_w_out"], name="l0_out_dx")
    gw["mla_w_out"] = mm_tn(og, d_out0, name="l0_out_dw")
    d_o, d_z0 = rowwise_bwd(f_gate, [o, z0], [], [d_og], [0, 1], [], T, 0, "l0_gate_bwd")
    d_oh = d_o.reshape(T, MLA_HEADS, V_HEAD_DIM).transpose(1, 0, 2)
    dqh, delta = attn_bwd_dq(qb, kb, vb, oh, d_oh, lse, Lc)
    dkh, dvh = attn_bwd_dkv(qb, kb, vb, d_oh, lse.reshape(MLA_HEADS, 1, T), delta.reshape(MLA_HEADS, 1, T), Lc)
    d_q, d_kv, d_kr = heads_vjp((dqh, dkh, dvh))
    d_qn = mm_nt(d_q, w["mla_w_uq"], name="l0_uq_dx")
    gw["mla_w_uq"] = mm_tn(qn, d_q, name="l0_uq_dw")
    d_kvn = mm_nt(d_kv, w["mla_w_ukv"], name="l0_ukv_dx")
    gw["mla_w_ukv"] = mm_tn(kvn, d_kv, name="l0_ukv_dw")
    d_cq, d_qng = rowwise_bwd(f_rms, [cq], [qng], [d_qn], [0], [0], T, 0, "l0_qnorm_bwd")
    d_ckv, d_kvng = rowwise_bwd(f_rms, [ckv], [kvng], [d_kvn], [0], [0], T, 0, "l0_kvnorm_bwd")
    gw["mla_q_norm"] = d_qng.reshape(-1)
    gw["mla_kv_norm"] = d_kvng.reshape(-1)
    d_p0 = jnp.concatenate([d_z0, d_cq, d_ckv, d_kr], axis=1)
    d_H0 = mm_nt(d_p0, w["mla_w_in"], name="l0_in_dx")
    gw["mla_w_in"] = mm_tn(H0, d_p0, name="l0_in_dw")
    d_X0, d_g0, d_sc0, d_sh0 = rowwise_bwd(f_norm_mod, [X0], [g0, mod_of(0, 1), mod_of(0, 0)], [d_H0],
                                           [0], [0, 1, 2], T, Lc, "l0_norm_bwd")
    dmod[0] = (d_sh0, d_sc0, d_gt0)
    gw["norm_g"] = jnp.stack([d_g0.reshape(D), d_g1.reshape(D)])
    dx = (d_X0 + d_X1)[Lc:]
    dmod_arr = jnp.stack([jnp.stack([dmod[i][j][:, 0, :] for j in range(3)], axis=1) for i in range(2)])
    return loss, dx, dmod_arr, gw


SHARDED = {
    "mla_w_in": 1, "mla_w_uq": 1, "mla_w_ukv": 1, "mla_w_out": 0,
    "s5_w_in": 1, "s5_w_glu": 0, "s5_w_out": 0, "s5_d": 0, "s5_b_glu": 0,
}
SHARDED_MATS = ["mla_w_in", "mla_w_uq", "mla_w_ukv", "mla_w_out", "s5_w_in", "s5_w_glu", "s5_w_out"]
SHARDED_VECS = ["s5_d", "s5_b_glu"]
REPLICATED = ["norm_g", "mla_q_norm", "mla_kv_norm", "s5_a_re", "s5_a_im", "s5_log_step", "s5_b_re", "s5_b_im",
              "s5_c_re", "s5_c_im", "final_g"]
WEIGHT_ORDER = ["c_ctx", "ada_w", "ada_b", "norm_g", "mla_w_in", "mla_q_norm", "mla_w_uq", "mla_kv_norm", "mla_w_ukv",
                "mla_w_out", "s5_w_in", "s5_a_re", "s5_a_im", "s5_log_step", "s5_b_re", "s5_b_im", "s5_c_re", "s5_c_im",
                "s5_d", "s5_w_glu", "s5_b_glu", "s5_w_out", "final_g"]


P0_HEAD = Q_LORA_RANK + KV_LORA_RANK + QK_ROPE_DIM


def w_in_to_kernel_order(w):
    return jnp.concatenate([w[:, P0_HEAD:], w[:, :P0_HEAD]], axis=1)


def w_in_from_kernel_order(w):
    return jnp.concatenate([w[:, D_MODEL:], w[:, :D_MODEL]], axis=1)


def gather_weights(ws):
    mats = [ws[n].astype(BF16) for n in SHARDED_MATS]
    vecs = [jnp.pad(ws[n].reshape(-1, 128), ((0, 6), (0, 0))) for n in SHARDED_VECS]
    outs = allgather_chips(mats + vecs, "gather_weights")
    full = {}
    for n, o in zip(SHARDED_MATS, outs):
        full[n] = o.reshape(-1, o.shape[-1]) if SHARDED[n] == 0 else o.transpose(1, 0, 2).reshape(o.shape[1], -1)
    full["mla_w_in"] = w_in_to_kernel_order(full["mla_w_in"])
    for n, o in zip(SHARDED_VECS, outs[len(mats):]):
        full[n] = o[:, :2, :].reshape(-1)
    return full


def reduce_gradients(gw):
    me = _coords()
    core = me[2].reshape(1).astype(jnp.int32)
    slots = {}
    for n in SHARDED_MATS:
        g = w_in_from_kernel_order(gw[n]) if n == "mla_w_in" else gw[n]
        if SHARDED[n] == 0:
            slots[n] = g.reshape(N_CHIP, 2, g.shape[0] // (2 * N_CHIP), g.shape[1])
        else:
            k, n4 = g.shape
            slots[n] = g.reshape(k, N_CHIP, n4 // N_CHIP).transpose(1, 0, 2).reshape(N_CHIP, 2, k // 2, n4 // N_CHIP)
    small_names = REPLICATED + SHARDED_VECS
    small, small_offs = pack_flat([gw[n].astype(F32) for n in small_names], F32)
    small = jnp.pad(small, ((0, (-small.shape[0]) % (N_CHIP * 32)), (0, 0)))
    slots["small"] = small.reshape(N_CHIP, 2, -1, 128)
    names = list(slots)
    idx = range(len(names))
    got = exchange(
        [slots[n] for n in names],
        [jax.ShapeDtypeStruct((N_CHIP,) + slots[n].shape[2:], F32) for n in names],
        [(CORE_FLIP, i, lambda me, peer: (slice(None), 1 - me[2]), i, lambda s: None) for i in idx], [],
        "grads_swap_in")
    sums = [pair_add(slots[n], g, core, F32 if n == "small" else BF16, f"grads_pair_{n}") for n, g in zip(names, got)]
    parts = exchange(
        sums, [jax.ShapeDtypeStruct(s.shape, s.dtype) for s in sums],
        [(f, i, lambda me, peer: (_chip_index(peer),), i, lambda s: (_chip_index(s),)) for i in idx for f in CHIP_FLIPS],
        [(i, lambda me: (_chip_index(me),), i, lambda me: (_chip_index(me),)) for i in idx],
        "grads_scatter")
    halves = [sum_chips(p, f"grads_sum_{n}") for n, p in zip(names, parts)]
    fulls = exchange(
        halves, [jax.ShapeDtypeStruct((2,) + h.shape, F32) for h in halves],
        [(CORE_FLIP, i, lambda me, peer: None, i, lambda s: (s[2],)) for i in idx],
        [(i, lambda me: None, i, lambda me: (me[2],)) for i in idx],
        "grads_swap_out")
    out = {n: f.reshape(-1, f.shape[-1]) for n, f in zip(names, fulls)}
    (small_all,) = allgather_chips([out.pop("small")], "grads_gather_small")
    vals = unpack_flat(small_all, small_offs, [gw[n].shape for n in small_names])
    for n, v in zip(small_names, vals):
        if n in SHARDED_VECS:
            size = v.shape[0] // N_CHIP
            v = lax.dynamic_slice_in_dim(v, _chip_index(me) * size, size)
        out[n] = v
    return out


def kernel(x, c, ctx, c_ctx, ada_w, ada_b, norm_g, mla_w_in, mla_q_norm, mla_w_uq, mla_kv_norm, mla_w_ukv, mla_w_out, s5_w_in, s5_a_re, s5_a_im, s5_log_step, s5_b_re, s5_b_im, s5_c_re, s5_c_im, s5_d, s5_w_glu, s5_b_glu, s5_w_out, final_g, loss_target, m_c_ctx, m_ada_w, m_ada_b, m_norm_g, m_mla_w_in, m_mla_q_norm, m_mla_w_uq, m_mla_kv_norm, m_mla_w_ukv, m_mla_w_out, m_s5_w_in, m_s5_a_re, m_s5_a_im, m_s5_log_step, m_s5_b_re, m_s5_b_im, m_s5_c_re, m_s5_c_im, m_s5_d, m_s5_w_glu, m_s5_b_glu, m_s5_w_out, m_final_g, v_c_ctx, v_ada_w, v_ada_b, v_norm_g, v_mla_w_in, v_mla_q_norm, v_mla_w_uq, v_mla_kv_norm, v_mla_w_ukv, v_mla_w_out, v_s5_w_in, v_s5_a_re, v_s5_a_im, v_s5_log_step, v_s5_b_re, v_s5_b_im, v_s5_c_re, v_s5_c_im, v_s5_d, v_s5_w_glu, v_s5_b_glu, v_s5_w_out, v_final_g):
    args = dict(locals())
    weights = {n: args[n] for n in WEIGHT_ORDER}
    D = D_MODEL
    xi, yi, ci = _coords()
    chip = 2 * xi + yi
    me = 4 * xi + 2 * yi + ci
    n_col = ada_w.shape[2]

    c_all = allgather_devices(jnp.pad(c, ((0, 7), (0, 0))), "gather_c")[:, 0, :]
    cond = jnp.concatenate([c_all, jnp.broadcast_to(c_ctx[None], (8, D))], axis=0)
    (s_cond,) = rowwise_fwd(lambda v: (_silu(v),), [cond], [], [D], [F32], 16, 0, "cond_silu")
    mod_cols = jnp.stack([mm_nn(s_cond, ada_w[i], name=f"mod_proj{i}") for i in range(2)])
    (mod_all,) = allgather_chips([mod_cols], "gather_mod")
    mod_all = mod_all.transpose(1, 2, 0, 3).reshape(2, 16, 3 * D) + ada_b[:, None, :]
    mod_l = lax.dynamic_index_in_dim(mod_all, me, axis=1, keepdims=False)
    mod_c = mod_all[:, 8, :]
    mod = jnp.stack([mod_c.reshape(2, 3, D), mod_l.reshape(2, 3, D)], axis=1)

    shards = {n: weights[n][0] for n in SHARDED_MATS + SHARDED_VECS}
    w = gather_weights(shards)
    for n in ["norm_g", "final_g"]:
        w[n] = weights[n]
    for n in ["mla_q_norm", "mla_kv_norm", "s5_a_re", "s5_a_im", "s5_log_step", "s5_b_re", "s5_b_im",
              "s5_c_re", "s5_c_im"]:
        w[n] = weights[n][0]

    loss_me, dx, dmod, gw = local_step(x[0], ctx[0], loss_target[0], mod, w)
    loss = lax.psum(loss_me, ("x", "y", "c"))

    dmod_rows = allgather_devices(dmod.reshape(2, 2, 3 * D), "gather_dmod")
    dm = jnp.concatenate([dmod_rows[:, :, 1, :], dmod_rows[:, :, 0, :]], axis=0).transpose(1, 0, 2)
    g_ada_b = jnp.sum(dm, axis=1)
    dm_cols = lax.dynamic_slice_in_dim(dm, chip * n_col, n_col, axis=2)
    g_ada_w = jnp.stack([mm_tn(s_cond, dm_cols[i], name=f"mod_proj_dw{i}") for i in range(2)])
    dmc = jnp.sum(dm_cols[:, 8:, :], axis=1)
    dmc8 = jnp.broadcast_to(dmc[:, None, :], (2, 8, n_col))
    g_sc = mm_nt(dmc8[0], ada_w[0], name="mod_proj_dx0")[0] + mm_nt(dmc8[1], ada_w[1], name="mod_proj_dx1")[0]
    g_sc_all = allgather_devices(jnp.broadcast_to(g_sc[None], (8, D)), "gather_dcond")[:, 0, :]
    g_silu_cc = g_sc_all[0] + g_sc_all[2] + g_sc_all[4] + g_sc_all[6]
    (g_c_ctx,) = rowwise_bwd(lambda v: (_silu(v),), [jnp.broadcast_to(c_ctx[None], (8, D))], [],
                             [jnp.broadcast_to(g_silu_cc[None], (8, D))], [0], [], 8, 0, "cond_silu_bwd")
    g_c_ctx = g_c_ctx[0]

    gw_in = {}
    for n in SHARDED_MATS + SHARDED_VECS:
        gw_in[n] = gw[n]
    for n in REPLICATED:
        gw_in[n] = gw[n]
    red = reduce_gradients(gw_in)
    grads = {"c_ctx": g_c_ctx, "ada_w": g_ada_w, "ada_b": g_ada_b}
    for n in WEIGHT_ORDER[3:]:
        grads[n] = red[n].reshape(weights[n].shape)

    deltas, new_m, new_v = {}, {}, {}
    small = [n for n in WEIGHT_ORDER if weights[n].size < 50000]
    for n in WEIGHT_ORDER:
        if n in small:
            continue
        shp = weights[n].shape
        w2 = weights[n].reshape(-1, shp[-1] if shp[-1] >= 256 else 128)
        d_, m_, v_ = adamw(w2, grads[n].reshape(w2.shape), args["m_" + n].reshape(w2.shape),
                           args["v_" + n].reshape(w2.shape), name=f"adamw_{n}")
        deltas[n], new_m[n], new_v[n] = d_.reshape(shp), m_.reshape(shp), v_.reshape(shp)
    packs = []
    offs = None
    for src in (weights, grads, {n: args["m_" + n] for n in small}, {n: args["v_" + n] for n in small}):
        buf, offs = pack_flat([src[n] for n in small], F32)
        packs.append(buf)
    outs = adamw(*packs, name="adamw_small")
    for res, dst in zip(outs, (deltas, new_m, new_v)):
        for n, val in zip(small, unpack_flat(res, offs, [weights[n].shape for n in small])):
            dst[n] = val

    return (loss, dx[None], *[grads[n] for n in WEIGHT_ORDER], *[deltas[n] for n in WEIGHT_ORDER],
            *[new_m[n] for n in WEIGHT_ORDER], *[new_v[n] for n in WEIGHT_ORDER])
```

```python
import functools
import math

import jax
import jax.numpy as jnp
import numpy as np
from jax import lax
from jax.experimental import pallas as pl
from jax.experimental.pallas import tpu as pltpu

F32 = jnp.float32
BF16 = jnp.bfloat16

D_MODEL = 1024
GRID_W = 64
EPS = 1e-6
MLA_HEADS = 16
QK_NOPE_DIM = 64
QK_ROPE_DIM = 32
V_HEAD_DIM = 64
Q_LORA_RANK = 256
KV_LORA_RANK = 128
QK_DIM = QK_NOPE_DIM + QK_ROPE_DIM
SOFTMAX_SCALE = QK_DIM ** -0.5
ROPE_THETA = 10000.0
S5_GROUP = 16
S5_GROUPS = D_MODEL // S5_GROUP
S5_STATE = 64
S5_LANES = S5_GROUPS * S5_STATE
N_SEG = 8
GROUPS_PER_BLOCK = 8
N_BLOCKS = S5_GROUPS // GROUPS_PER_BLOCK
BLK_CH = GROUPS_PER_BLOCK * S5_GROUP
BLK_ST = GROUPS_PER_BLOCK * S5_STATE

ADAM_LR = 0.001
ADAM_B1 = 0.9
ADAM_B2 = 0.999
ADAM_EPS = 1e-08
ADAM_WD = 0.01
ADAM_STEP = 10

N_DEV = 8
N_CHIP = 4
MESH = pl.DeviceIdType.MESH
VMEM_LIMIT = 52 * 1024 * 1024
ROW_TILE = 256


def _params(sem=None, vmem=None):
    return pltpu.CompilerParams(dimension_semantics=sem, vmem_limit_bytes=vmem)


def mm_nn(a, b, out_dtype=F32, name="mm_nn"):
    M, K = a.shape
    N = b.shape[1]
    tm = math.gcd(ROW_TILE, M)

    def body(a_ref, b_ref, o_ref):
        o_ref[...] = jnp.dot(a_ref[...].astype(BF16), b_ref[...].astype(BF16),
                             preferred_element_type=F32).astype(o_ref.dtype)

    return pl.pallas_call(
        body, out_shape=jax.ShapeDtypeStruct((M, N), out_dtype), grid=(M // tm,),
        in_specs=[pl.BlockSpec((tm, K), lambda i: (i, 0)), pl.BlockSpec((K, N), lambda i: (0, 0))],
        out_specs=pl.BlockSpec((tm, N), lambda i: (i, 0)),
        compiler_params=_params(("parallel",), VMEM_LIMIT), name=name)(a, b)


def mm_nt(a, b, out_dtype=F32, name="mm_nt"):
    M, N = a.shape
    K = b.shape[0]
    tm = math.gcd(ROW_TILE, M)

    def body(a_ref, b_ref, o_ref):
        o_ref[...] = lax.dot_general(a_ref[...].astype(BF16), b_ref[...].astype(BF16),
                                     (((1,), (1,)), ((), ())),
                                     preferred_element_type=F32).astype(o_ref.dtype)

    return pl.pallas_call(
        body, out_shape=jax.ShapeDtypeStruct((M, K), out_dtype), grid=(M // tm,),
        in_specs=[pl.BlockSpec((tm, N), lambda i: (i, 0)), pl.BlockSpec((K, N), lambda i: (0, 0))],
        out_specs=pl.BlockSpec((tm, K), lambda i: (i, 0)),
        compiler_params=_params(("parallel",), VMEM_LIMIT), name=name)(a, b)


def mm_tn(a, b, name="mm_tn"):
    M, K = a.shape
    N = b.shape[1]
    tm = math.gcd(2 * ROW_TILE, M)

    def body(a_ref, b_ref, o_ref):
        @pl.when(pl.program_id(0) == 0)
        def _():
            o_ref[...] = jnp.zeros_like(o_ref)

        o_ref[...] += lax.dot_general(a_ref[...].astype(BF16), b_ref[...].astype(BF16),
                                      (((0,), (0,)), ((), ())), preferred_element_type=F32)

    return pl.pallas_call(
        body, out_shape=jax.ShapeDtypeStruct((K, N), F32), grid=(M // tm,),
        in_specs=[pl.BlockSpec((tm, K), lambda i: (i, 0)), pl.BlockSpec((tm, N), lambda i: (i, 0))],
        out_specs=pl.BlockSpec((K, N), lambda i: (0, 0)),
        compiler_params=_params(("arbitrary",), VMEM_LIMIT), name=name)(a, b)


class Rows:
    def __init__(self, arr, width=None, row_off=0, col_blk=0):
        self.arr = arr
        self.width = arr.shape[1] if width is None else width
        self.row_off = row_off
        self.col_blk = col_blk

    def spec(self, tm):
        ro, cb = self.row_off // tm, self.col_blk
        return pl.BlockSpec((tm, self.width), lambda i: (i + ro, cb))


def _as_rows(x):
    return x if isinstance(x, Rows) else Rows(x)


def _row_tile(n_rows, n_ctx_rows, rows):
    tm = math.gcd(ROW_TILE, n_rows, n_ctx_rows)
    for r in rows:
        tm = math.gcd(tm, r.row_off)
    return tm


def _bc_spec(arr, n_ctx_blocks):
    g, _, d = arr.shape
    if g == 1:
        return pl.BlockSpec((1, 1, d), lambda i: (0, 0, 0))
    return pl.BlockSpec((1, 1, d), lambda i: ((i >= n_ctx_blocks).astype(jnp.int32), 0, 0))


def rowwise_fwd(fn, rows, bcs, out_dims, out_dtypes, n_rows, n_ctx_rows, name):
    rows = [_as_rows(r) for r in rows]
    tm = _row_tile(n_rows, n_ctx_rows, rows)
    ncb = n_ctx_rows // tm
    nr, nb = len(rows), len(bcs)

    def body(*refs):
        vals = [r[...].astype(F32) for r in refs[:nr]] + [b[0].astype(F32) for b in refs[nr:nr + nb]]
        outs = fn(*vals)
        for o_ref, v in zip(refs[nr + nb:], outs):
            o_ref[...] = v.astype(o_ref.dtype)

    outs = pl.pallas_call(
        body,
        out_shape=[jax.ShapeDtypeStruct((n_rows, d), dt) for d, dt in zip(out_dims, out_dtypes)],
        grid=(n_rows // tm,),
        in_specs=[r.spec(tm) for r in rows] + [_bc_spec(b, ncb) for b in bcs],
        out_specs=[pl.BlockSpec((tm, d), lambda i: (i, 0)) for d in out_dims],
        compiler_params=_params(("parallel",), VMEM_LIMIT), name=name)(*[r.arr for r in rows], *bcs)
    return outs


def rowwise_bwd(fn, rows, bcs, cts, diff_rows, diff_bcs, n_rows, n_ctx_rows, name):
    rows = [_as_rows(r) for r in rows]
    cts = [_as_rows(c) for c in cts]
    tm = _row_tile(n_rows, n_ctx_rows, rows + cts)
    ncb = n_ctx_rows // tm
    nr, nb, nc = len(rows), len(bcs), len(cts)
    ndr, ndb = len(diff_rows), len(diff_bcs)

    def body(*refs):
        i = pl.program_id(0)
        rvals = [r[...].astype(F32) for r in refs[:nr]]
        bvals = [b[0].astype(F32) for b in refs[nr:nr + nb]]
        cvals = [c[...].astype(F32) for c in refs[nr + nb:nr + nb + nc]]
        outs = refs[nr + nb + nc:]

        def f(*d):
            rv, bv = list(rvals), list(bvals)
            for k, idx in enumerate(diff_rows):
                rv[idx] = d[k]
            for k, idx in enumerate(diff_bcs):
                bv[idx] = d[ndr + k]
            return tuple(fn(*rv, *bv))

        primals = [rvals[k] for k in diff_rows] + [bvals[k] for k in diff_bcs]
        _, vjp = jax.vjp(f, *primals)
        grads = vjp(tuple(cvals))
        for k in range(ndr):
            outs[k][...] = grads[k].astype(outs[k].dtype)
        for k, idx in enumerate(diff_bcs):
            o_ref = outs[ndr + k]
            first = (i == 0)
            if bcs[idx].shape[0] == 2:
                first = first | (i == ncb)

            @pl.when(first)
            def _(o_ref=o_ref):
                o_ref[...] = jnp.zeros_like(o_ref)

            o_ref[0] += grads[ndr + k]

    out_shape = [jax.ShapeDtypeStruct((n_rows, rows[k].width), F32) for k in diff_rows]
    out_shape += [jax.ShapeDtypeStruct(bcs[k].shape, F32) for k in diff_bcs]
    out_specs = [pl.BlockSpec((tm, rows[k].width), lambda i: (i, 0)) for k in diff_rows]
    out_specs += [_bc_spec(bcs[k], ncb) for k in diff_bcs]
    outs = pl.pallas_call(
        body, out_shape=out_shape, grid=(n_rows // tm,),
        in_specs=[r.spec(tm) for r in rows] + [_bc_spec(b, ncb) for b in bcs] + [c.spec(tm) for c in cts],
        out_specs=out_specs,
        compiler_params=_params(("arbitrary",), VMEM_LIMIT), name=name)(
            *[r.arr for r in rows], *bcs, *[c.arr for c in cts])
    return outs


def _rms(x):
    return x * lax.rsqrt(jnp.mean(x * x, axis=-1, keepdims=True) + EPS)


def _sigmoid(x):
    return 0.5 * (jnp.tanh(0.5 * x) + 1.0)


def _silu(x):
    return x * _sigmoid(x)


def _gelu_tanh(x):
    return 0.5 * x * (1.0 + jnp.tanh(math.sqrt(2.0 / math.pi) * (x + 0.044715 * (x * x * x))))


def f_norm_mod(x, g, sc, sh):
    return ((_rms(x) * g) * (1.0 + sc) + sh,)


def f_rms(x, g):
    return (_rms(x) * g,)


def f_gate(o, z):
    return (o * _silu(z),)


def f_res(x, o, gt):
    return (x + gt * o,)


def f_s5_act(y, u, d):
    return (_gelu_tanh(y + d * u),)


def f_s5_glu(ya, gl, z, b):
    return (ya * _sigmoid(gl + b) * _silu(z),)


def loss_and_grad(x2, final_g, target, name="loss_head"):
    n, d = x2.shape
    tm = math.gcd(ROW_TILE, n)

    def row_loss(x, g, t):
        y = _rms(x) * g
        e = y - t
        return 0.5 * (e * e) * (1.0 / d)

    def body(x_ref, g_ref, t_ref, l_ref, dx_ref, dg_ref):
        @pl.when(pl.program_id(0) == 0)
        def _():
            l_ref[...] = jnp.zeros_like(l_ref)
            dg_ref[...] = jnp.zeros_like(dg_ref)

        t = t_ref[...]
        lterm, vjp = jax.vjp(lambda x, g: row_loss(x, g, t), x_ref[...], g_ref[...])
        dx, dg = vjp(jnp.ones_like(lterm))
        l_ref[...] += jnp.sum(lterm, axis=0, keepdims=True)
        dx_ref[...] = dx
        dg_ref[...] += dg

    return pl.pallas_call(
        body,
        out_shape=[jax.ShapeDtypeStruct((1, d), F32), jax.ShapeDtypeStruct((n, d), F32),
                   jax.ShapeDtypeStruct((1, d), F32)],
        grid=(n // tm,),
        in_specs=[pl.BlockSpec((tm, d), lambda i: (i, 0)), pl.BlockSpec((1, d), lambda i: (0, 0)),
                  pl.BlockSpec((tm, d), lambda i: (i, 0))],
        out_specs=[pl.BlockSpec((1, d), lambda i: (0, 0)), pl.BlockSpec((tm, d), lambda i: (i, 0)),
                   pl.BlockSpec((1, d), lambda i: (0, 0))],
        compiler_params=_params(("arbitrary",), VMEM_LIMIT), name=name)(x2, final_g, target)


NT_DIMS = (((1,), (1,)), ((), ()))
HEAD_LANES = 128
N_PAIRS = MLA_HEADS // 2


def _own_lanes(shape, hh):
    lane = lax.broadcasted_iota(jnp.int32, shape, len(shape) - 1)
    return (lane < V_HEAD_DIM) if hh == 0 else (lane >= V_HEAD_DIM)


def _rope_tiles(x, cos, sin_next, sin_prev, inverse):
    width = x.shape[-1]
    reps = width // HEAD_LANES
    c, sn, sp = (jnp.tile(t, (1, reps)) for t in (cos, sin_next, sin_prev))
    if inverse:
        return x * c + pltpu.roll(x * sn, 8, 1) + pltpu.roll(x * sp, width - 8, 1)
    return x * c + pltpu.roll(x, width - 8, 1) * sn + pltpu.roll(x, 8, 1) * sp


def attn_fwd(qb, kb, vb, n_ctx):
    T = qb.shape[0]
    tq = math.gcd(ROW_TILE, n_ctx)
    nq, ncb = T // tq, n_ctx // tq

    def body(q_ref, k_ref, v_ref, o_ref, lse_ref):
        qi = pl.program_id(1)

        def rows(n_keys):
            v = v_ref[:n_keys, :]
            outs = []
            for hh in range(2):
                hs = slice(hh * HEAD_LANES, (hh + 1) * HEAD_LANES)
                s = lax.dot_general(q_ref[:, hs], k_ref[:n_keys, hs], NT_DIMS,
                                    preferred_element_type=F32) * SOFTMAX_SCALE
                m = jnp.max(s, axis=-1, keepdims=True)
                p = jnp.exp(s - m)
                l = jnp.sum(p, axis=-1, keepdims=True)
                outs.append(jnp.dot(p.astype(BF16), v, preferred_element_type=F32) / l)
                lse_ref[hh] = m + jnp.log(l)
            o_ref[...] = jnp.where(_own_lanes(outs[0].shape, 0), outs[0], outs[1])

        pl.when(qi < ncb)(lambda: rows(n_ctx))
        pl.when(qi >= ncb)(lambda: rows(T))

    return pl.pallas_call(
        body,
        out_shape=[jax.ShapeDtypeStruct((T, MLA_HEADS * V_HEAD_DIM), F32),
                   jax.ShapeDtypeStruct((MLA_HEADS, T, 1), F32)],
        grid=(N_PAIRS, nq),
        in_specs=[pl.BlockSpec((tq, 2 * HEAD_LANES), lambda h, i: (i, h)),
                  pl.BlockSpec((T, 2 * HEAD_LANES), lambda h, i: (0, h)),
                  pl.BlockSpec((T, 2 * V_HEAD_DIM), lambda h, i: (0, h))],
        out_specs=[pl.BlockSpec((tq, 2 * V_HEAD_DIM), lambda h, i: (i, h)),
                   pl.BlockSpec((2, tq, 1), lambda h, i: (h, i, 0))],
        compiler_params=_params(("parallel", "parallel"), VMEM_LIMIT), name="attn_fwd")(qb, kb, vb)


def attn_bwd_dq(qb, kb, vb, o, do, lse, n_ctx):
    T = qb.shape[0]
    tq = math.gcd(ROW_TILE, n_ctx)
    nq, ncb = T // tq, n_ctx // tq

    def body(q_ref, k_ref, v_ref, o_ref, do_ref, lse_ref, dq_ref, delta_ref):
        qi = pl.program_id(1)

        def rows(n_keys):
            v = v_ref[:n_keys, :]
            for hh in range(2):
                hs = slice(hh * HEAD_LANES, (hh + 1) * HEAD_LANES)
                k = k_ref[:n_keys, hs]
                do = jnp.where(_own_lanes(do_ref.shape, hh), do_ref[...], 0.0)
                delta = jnp.sum(do * o_ref[...], axis=-1, keepdims=True)
                s = lax.dot_general(q_ref[:, hs], k, NT_DIMS, preferred_element_type=F32) * SOFTMAX_SCALE
                p = jnp.exp(s - lse_ref[hh])
                dp = lax.dot_general(do.astype(BF16), v, NT_DIMS, preferred_element_type=F32)
                ds = p * (dp - delta) * SOFTMAX_SCALE
                dq_ref[:, hs] = jnp.dot(ds.astype(BF16), k, preferred_element_type=F32)
                delta_ref[hh] = delta

        pl.when(qi < ncb)(lambda: rows(n_ctx))
        pl.when(qi >= ncb)(lambda: rows(T))

    return pl.pallas_call(
        body,
        out_shape=[jax.ShapeDtypeStruct((T, MLA_HEADS * HEAD_LANES), F32),
                   jax.ShapeDtypeStruct((MLA_HEADS, T, 1), F32)],
        grid=(N_PAIRS, nq),
        in_specs=[pl.BlockSpec((tq, 2 * HEAD_LANES), lambda h, i: (i, h)),
                  pl.BlockSpec((T, 2 * HEAD_LANES), lambda h, i: (0, h)),
                  pl.BlockSpec((T, 2 * V_HEAD_DIM), lambda h, i: (0, h)),
                  pl.BlockSpec((tq, 2 * V_HEAD_DIM), lambda h, i: (i, h)),
                  pl.BlockSpec((tq, 2 * V_HEAD_DIM), lambda h, i: (i, h)),
                  pl.BlockSpec((2, tq, 1), lambda h, i: (h, i, 0))],
        out_specs=[pl.BlockSpec((tq, 2 * HEAD_LANES), lambda h, i: (i, h)),
                   pl.BlockSpec((2, tq, 1), lambda h, i: (h, i, 0))],
        compiler_params=_params(("parallel", "parallel"), VMEM_LIMIT), name="attn_bwd_dq")(
            qb, kb, vb, o, do, lse)


def attn_bwd_dkv(qb, kb, vb, do, lse_rows, delta_rows, n_ctx):
    T = qb.shape[0]
    tq = math.gcd(ROW_TILE, n_ctx)
    nq, ncb = T // tq, n_ctx // tq

    def body(q_ref, do_ref, lse_ref, delta_ref, k_ref, v_ref, dk_ref, dv_ref):
        kj = pl.program_id(1)

        def cols(first):
            v = v_ref[...]
            do_all = do_ref[first:, :]
            dv = None
            for hh in range(2):
                hs = slice(hh * HEAD_LANES, (hh + 1) * HEAD_LANES)
                k = k_ref[:, hs]
                q = q_ref[first:, hs]
                do16 = jnp.where(_own_lanes(do_all.shape, hh), do_all, 0.0).astype(BF16)
                st = lax.dot_general(k, q, NT_DIMS, preferred_element_type=F32) * SOFTMAX_SCALE
                pt = jnp.exp(st - lse_ref[hh, :, first:])
                dv_h = jnp.dot(pt.astype(BF16), do16, preferred_element_type=F32)
                dv = dv_h if dv is None else dv + dv_h
                dpt = lax.dot_general(v, do16, NT_DIMS, preferred_element_type=F32)
                dst = pt * (dpt - delta_ref[hh, :, first:]) * SOFTMAX_SCALE
                dk_ref[:, hs] = jnp.dot(dst.astype(BF16), q, preferred_element_type=F32)
            dv_ref[...] = dv

        pl.when(kj < ncb)(lambda: cols(0))
        pl.when(kj >= ncb)(lambda: cols(n_ctx))

    return pl.pallas_call(
        body,
        out_shape=[jax.ShapeDtypeStruct((T, MLA_HEADS * HEAD_LANES), F32),
                   jax.ShapeDtypeStruct((T, MLA_HEADS * V_HEAD_DIM), F32)],
        grid=(N_PAIRS, nq),
        in_specs=[pl.BlockSpec((T, 2 * HEAD_LANES), lambda h, j: (0, h)),
                  pl.BlockSpec((T, 2 * V_HEAD_DIM), lambda h, j: (0, h)),
                  pl.BlockSpec((2, 1, T), lambda h, j: (h, 0, 0)),
                  pl.BlockSpec((2, 1, T), lambda h, j: (h, 0, 0)),
                  pl.BlockSpec((tq, 2 * HEAD_LANES), lambda h, j: (j, h)),
                  pl.BlockSpec((tq, 2 * V_HEAD_DIM), lambda h, j: (j, h))],
        out_specs=[pl.BlockSpec((tq, 2 * HEAD_LANES), lambda h, j: (j, h)),
                   pl.BlockSpec((tq, 2 * V_HEAD_DIM), lambda h, j: (j, h))],
        compiler_params=_params(("parallel", "parallel"), VMEM_LIMIT), name="attn_bwd_dkv")(
            qb, do, lse_rows, delta_rows, kb, vb)


def _split_bf16(x):
    hi = x.astype(BF16)
    return hi, (x - hi.astype(F32)).astype(BF16)


def q_heads(qn, w_uq_p, tabs, name="l0_uq"):
    T, K = qn.shape
    N = w_uq_p.shape[1]
    tm = math.gcd(ROW_TILE, T)

    def body(a_ref, w_ref, c_ref, sn_ref, sp_ref, o_ref):
        acc = jnp.dot(a_ref[...], w_ref[...], preferred_element_type=F32)
        o_ref[...] = _rope_tiles(acc, c_ref[...], sn_ref[...], sp_ref[...], False).astype(BF16)

    tab = pl.BlockSpec((tm, HEAD_LANES), lambda i: (i, 0))
    return pl.pallas_call(
        body, out_shape=jax.ShapeDtypeStruct((T, N), BF16), grid=(T // tm,),
        in_specs=[pl.BlockSpec((tm, K), lambda i: (i, 0)), pl.BlockSpec((K, N), lambda i: (0, 0)), tab, tab, tab],
        out_specs=pl.BlockSpec((tm, N), lambda i: (i, 0)),
        compiler_params=_params(("parallel",), VMEM_LIMIT), name=name)(qn, w_uq_p, *tabs)


def kv_heads(kvn, w_kn_p, w_v, kr, spread, tabs, name="l0_ukv"):
    T, K = kvn.shape
    N = w_kn_p.shape[1]
    NV = w_v.shape[1]
    tm = math.gcd(ROW_TILE, T)

    def body(a_ref, wk_ref, wv_ref, kr_ref, e_ref, c_ref, sn_ref, sp_ref, k_ref, v_ref):
        a = a_ref[...]
        hi, lo = _split_bf16(kr_ref[...])
        acc = (jnp.dot(a, wk_ref[...], preferred_element_type=F32)
               + jnp.dot(hi, e_ref[...], preferred_element_type=F32)
               + jnp.dot(lo, e_ref[...], preferred_element_type=F32))
        k_ref[...] = _rope_tiles(acc, c_ref[...], sn_ref[...], sp_ref[...], False).astype(BF16)
        v_ref[...] = jnp.dot(a, wv_ref[...], preferred_element_type=F32).astype(BF16)

    tab = pl.BlockSpec((tm, HEAD_LANES), lambda i: (i, 0))
    return pl.pallas_call(
        body, out_shape=[jax.ShapeDtypeStruct((T, N), BF16), jax.ShapeDtypeStruct((T, NV), BF16)], grid=(T // tm,),
        in_specs=[pl.BlockSpec((tm, K), lambda i: (i, 0)), pl.BlockSpec((K, N), lambda i: (0, 0)),
                  pl.BlockSpec((K, NV), lambda i: (0, 0)), pl.BlockSpec((tm, QK_ROPE_DIM), lambda i: (i, 0)),
                  pl.BlockSpec((QK_ROPE_DIM, N), lambda i: (0, 0)), tab, tab, tab],
        out_specs=[pl.BlockSpec((tm, N), lambda i: (i, 0)), pl.BlockSpec((tm, NV), lambda i: (i, 0))],
        compiler_params=_params(("parallel",), VMEM_LIMIT), name=name)(kvn, w_kn_p, w_v, kr, spread, *tabs)


def heads_unrope(d, tabs, spread=None, name="unrope"):
    T, N = d.shape
    tm = math.gcd(ROW_TILE, T)

    def body(*refs):
        if spread is None:
            d_ref, c_ref, sn_ref, sp_ref, o_ref = refs
        else:
            d_ref, c_ref, sn_ref, sp_ref, e_ref, o_ref, kr_ref = refs
        g = _rope_tiles(d_ref[...], c_ref[...], sn_ref[...], sp_ref[...], True)
        o_ref[...] = g.astype(BF16)
        if spread is not None:
            hi, lo = _split_bf16(g)
            kr_ref[...] = (lax.dot_general(hi, e_ref[...], NT_DIMS, preferred_element_type=F32)
                           + lax.dot_general(lo, e_ref[...], NT_DIMS, preferred_element_type=F32))

    tab = pl.BlockSpec((tm, HEAD_LANES), lambda i: (i, 0))
    row = pl.BlockSpec((tm, N), lambda i: (i, 0))
    ins, in_specs = [d, *tabs], [row, tab, tab, tab]
    out_shape, out_specs = [jax.ShapeDtypeStruct((T, N), BF16)], [row]
    if spread is not None:
        ins.append(spread)
        in_specs.append(pl.BlockSpec((QK_ROPE_DIM, N), lambda i: (0, 0)))
        out_shape.append(jax.ShapeDtypeStruct((T, QK_ROPE_DIM), F32))
        out_specs.append(pl.BlockSpec((tm, QK_ROPE_DIM), lambda i: (i, 0)))
    return pl.pallas_call(
        body, out_shape=out_shape, grid=(T // tm,), in_specs=in_specs, out_specs=out_specs,
        compiler_params=_params(("parallel",), VMEM_LIMIT), name=name)(*ins)


def _cmul(ar, ai, br, bi):
    return ar * br - ai * bi, ar * bi + ai * br


def s5_chain(finals, s0, a, n_steps, reverse, name):
    W = finals.shape[-1]
    first = N_SEG - 1 if reverse else 0

    def body(f_ref, s0_ref, a_ref, c_ref):
        pr, pi = jnp.ones((1, W), F32), jnp.zeros((1, W), F32)
        br, bi = a_ref[0], a_ref[1]
        n = n_steps
        while n:
            if n & 1:
                pr, pi = _cmul(pr, pi, br, bi)
            br, bi = _cmul(br, bi, br, bi)
            n >>= 1
        fr, fi = f_ref[0], f_ref[1]
        row = lax.broadcasted_iota(jnp.int32, (N_SEG, W), 0)
        s0r = jnp.broadcast_to(s0_ref[0], (N_SEG, W))
        s0i = jnp.broadcast_to(s0_ref[1], (N_SEG, W))
        cr = jnp.where(row == first, s0r, 0.0)
        ci = jnp.where(row == first, s0i, 0.0)
        shift = N_SEG - 1 if reverse else 1
        for _ in range(N_SEG - 1):
            mr, mi = _cmul(pr, pi, cr, ci)
            tr = pltpu.roll(fr + mr, shift, 0)
            ti = pltpu.roll(fi + mi, shift, 0)
            cr = jnp.where(row == first, s0r, tr)
            ci = jnp.where(row == first, s0i, ti)
        c_ref[0] = cr
        c_ref[1] = ci

    return pl.pallas_call(body, out_shape=jax.ShapeDtypeStruct((2, N_SEG, W), F32), name=name)(finals, s0, a)


def _scan_chunk(bur, bui, st_ref, a_ref, n_steps, reverse):
    for lc in range(S5_LANES // BLK_ST):
        sl = slice(lc * BLK_ST, (lc + 1) * BLK_ST)
        lr = jnp.broadcast_to(a_ref[0, :, sl], (N_SEG, BLK_ST))
        li = jnp.broadcast_to(a_ref[1, :, sl], (N_SEG, BLK_ST))

        def step(jj, carry, sl=sl, lr=lr, li=li):
            sr, si = carry
            j = (n_steps - 1 - jj) if reverse else jj
            r0 = pl.multiple_of(j * N_SEG, N_SEG)
            nr = lr * sr - li * si + bur[pl.ds(r0, N_SEG), sl]
            ni = lr * si + li * sr + bui[pl.ds(r0, N_SEG), sl]
            bur[pl.ds(r0, N_SEG), sl] = nr
            bui[pl.ds(r0, N_SEG), sl] = ni
            return nr, ni

        sr, si = lax.fori_loop(0, n_steps, step, (st_ref[0, :, sl], st_ref[1, :, sl]))
        st_ref[0, :, sl] = sr
        st_ref[1, :, sl] = si


def _project_in(x16, w_re, w_im, bur, bui, adjoint):
    for gb in range(N_BLOCKS):
        xb = x16[:, gb * BLK_CH:(gb + 1) * BLK_CH]
        sl = slice(gb * BLK_ST, (gb + 1) * BLK_ST)
        if adjoint:
            dn = (((1,), (1,)), ((), ()))
            bur[:, sl] = lax.dot_general(xb, w_re[gb], dn, preferred_element_type=F32)
            bui[:, sl] = -lax.dot_general(xb, w_im[gb], dn, preferred_element_type=F32)
        else:
            bur[:, sl] = jnp.dot(xb, w_re[gb], preferred_element_type=F32)
            bui[:, sl] = jnp.dot(xb, w_im[gb], preferred_element_type=F32)


def s5_scan(act, w_re, w_im, a, init, *, reverse, adjoint=False, c_re=None, c_im=None, add=None,
            want_ckpt=False, rows=None, name):
    act_off, N = rows if rows is not None else (0, act.shape[0])
    R = math.gcd(ROW_TILE, N, act_off)
    nch, jc = N // R, R // N_SEG
    with_out = c_re is not None

    def chunk(i):
        return (nch - 1 - i) if reverse else i

    def body(*refs):
        act_ref, wre_ref, wim_ref, a_ref, init_ref = refs[:5]
        k = 5
        if with_out:
            cre_ref, cim_ref = refs[k:k + 2]
            k += 2
        if add is not None:
            add_ref = refs[k]
            k += 1
        if with_out:
            out_ref = refs[k]
            k += 1
        if want_ckpt:
            ck_ref = refs[k]
            k += 1
        fin_ref, bur, bui = refs[k:k + 3]

        @pl.when(pl.program_id(0) == 0)
        def _():
            fin_ref[...] = init_ref[...]

        if want_ckpt:
            ck_ref[0] = fin_ref[...]
        _project_in(act_ref[...].astype(BF16), wre_ref, wim_ref, bur, bui, adjoint)
        _scan_chunk(bur, bui, fin_ref, a_ref, jc, reverse)
        if with_out:
            for gb in range(N_BLOCKS):
                sl = slice(gb * BLK_ST, (gb + 1) * BLK_ST)
                y = (jnp.dot(bur[:, sl].astype(BF16), cre_ref[gb], preferred_element_type=F32)
                     - jnp.dot(bui[:, sl].astype(BF16), cim_ref[gb], preferred_element_type=F32))
                cs = slice(gb * BLK_CH, (gb + 1) * BLK_CH)
                if add is not None:
                    y = y + add_ref[:, cs]
                out_ref[:, cs] = y

    row_spec = pl.BlockSpec((R, D_MODEL), lambda i: (chunk(i), 0))
    act_spec = pl.BlockSpec((R, D_MODEL), lambda i: (chunk(i) + act_off // R, 0))
    w_spec = pl.BlockSpec(w_re.shape, lambda i: (0, 0, 0))
    st_spec = pl.BlockSpec((2, N_SEG, S5_LANES), lambda i: (0, 0, 0))
    ins = [act, w_re, w_im, a, init]
    in_specs = [act_spec, w_spec, w_spec, pl.BlockSpec((2, 1, S5_LANES), lambda i: (0, 0, 0)), st_spec]
    if with_out:
        ins += [c_re, c_im]
        in_specs += [pl.BlockSpec(c_re.shape, lambda i: (0, 0, 0))] * 2
    if add is not None:
        ins.append(add)
        in_specs.append(row_spec)
    out_shape, out_specs = [], []
    if with_out:
        out_shape.append(jax.ShapeDtypeStruct((N, D_MODEL), F32))
        out_specs.append(row_spec)
    if want_ckpt:
        out_shape.append(jax.ShapeDtypeStruct((nch, 2, N_SEG, S5_LANES), F32))
        out_specs.append(pl.BlockSpec((1, 2, N_SEG, S5_LANES), lambda i: (chunk(i), 0, 0, 0)))
    out_shape.append(jax.ShapeDtypeStruct((2, N_SEG, S5_LANES), F32))
    out_specs.append(st_spec)
    res = pl.pallas_call(
        body, out_shape=out_shape, grid=(nch,), in_specs=in_specs, out_specs=out_specs,
        scratch_shapes=[pltpu.VMEM((R, S5_LANES), F32), pltpu.VMEM((R, S5_LANES), F32)],
        compiler_params=_params(("arbitrary",), VMEM_LIMIT), name=name)(*ins)
    res = list(res)
    out = res.pop(0) if with_out else None
    ckpt = res.pop(0) if want_ckpt else None
    return out, ckpt, res[0]


def s5_grads(dy, u, ckpt, b_re, b_im, c_re, c_im, lam, init_adj, *, reverse, add=None, u_off=0, name):
    N = dy.shape[0]
    R = math.gcd(ROW_TILE, N, u_off)
    nch, jc = N // R, R // N_SEG
    W = S5_LANES

    def chunk(i):
        return i if reverse else (nch - 1 - i)

    def body(*refs):
        dy_ref, u_ref, ck_ref, bre_ref, bim_ref, cre_ref, cim_ref, lam_ref, init_ref = refs[:9]
        k = 9
        if add is not None:
            add_ref = refs[k]
            k += 1
        du_ref, dlam_ref, dbre_ref, dbim_ref, dcre_ref, dcim_ref, fin_ref = refs[k:k + 7]
        sr_buf, si_buf, er_buf, ei_buf, st_buf = refs[k + 7:k + 12]

        @pl.when(pl.program_id(0) == 0)
        def _():
            fin_ref[...] = init_ref[...]
            dlam_ref[...] = jnp.zeros_like(dlam_ref)
            dbre_ref[...] = jnp.zeros_like(dbre_ref)
            dbim_ref[...] = jnp.zeros_like(dbim_ref)
            dcre_ref[...] = jnp.zeros_like(dcre_ref)
            dcim_ref[...] = jnp.zeros_like(dcim_ref)

        u16 = u_ref[...].astype(BF16)
        dy16 = dy_ref[...].astype(BF16)
        st_buf[...] = ck_ref[0]
        _project_in(u16, bre_ref, bim_ref, sr_buf, si_buf, False)
        _scan_chunk(sr_buf, si_buf, st_buf, lam_ref, jc, reverse)
        _project_in(dy16, cre_ref, cim_ref, er_buf, ei_buf, True)
        for lc in range(W // BLK_ST):
            sl = slice(lc * BLK_ST, (lc + 1) * BLK_ST)
            lr = jnp.broadcast_to(lam_ref[0, :, sl], (N_SEG, BLK_ST))
            li = jnp.broadcast_to(lam_ref[1, :, sl], (N_SEG, BLK_ST))

            def one(r0, spr, spi, carry, sl=sl, lr=lr, li=li):
                gr, gi, ar, ai = carry
                nr = er_buf[pl.ds(r0, N_SEG), sl] + lr * gr + li * gi
                ni = ei_buf[pl.ds(r0, N_SEG), sl] + lr * gi - li * gr
                er_buf[pl.ds(r0, N_SEG), sl] = nr
                ei_buf[pl.ds(r0, N_SEG), sl] = ni
                return nr, ni, ar + spr * nr + spi * ni, ai + spr * ni - spi * nr

            def step(ff, carry, sl=sl, one=one):
                f = jc - 1 - ff
                j = (jc - 1 - f) if reverse else f
                jp = (j + 1) if reverse else (j - 1)
                r0 = pl.multiple_of(j * N_SEG, N_SEG)
                p0 = pl.multiple_of(jp * N_SEG, N_SEG)
                return one(r0, sr_buf[pl.ds(p0, N_SEG), sl], si_buf[pl.ds(p0, N_SEG), sl], carry)

            carry = (fin_ref[0, :, sl], fin_ref[1, :, sl], dlam_ref[0, :, sl], dlam_ref[1, :, sl])
            carry = lax.fori_loop(0, jc - 1, step, carry)
            r_first = (jc - 1) * N_SEG if reverse else 0
            gr, gi, ar, ai = one(r_first, ck_ref[0, 0, :, sl], ck_ref[0, 1, :, sl], carry)
            fin_ref[0, :, sl] = gr
            fin_ref[1, :, sl] = gi
            dlam_ref[0, :, sl] = ar
            dlam_ref[1, :, sl] = ai
        tn = (((0,), (0,)), ((), ()))
        nt = (((1,), (1,)), ((), ()))
        for gb in range(N_BLOCKS):
            sl = slice(gb * BLK_ST, (gb + 1) * BLK_ST)
            cs = slice(gb * BLK_CH, (gb + 1) * BLK_CH)
            gr16 = er_buf[:, sl].astype(BF16)
            gi16 = ei_buf[:, sl].astype(BF16)
            du = (lax.dot_general(gr16, bre_ref[gb], nt, preferred_element_type=F32)
                  + lax.dot_general(gi16, bim_ref[gb], nt, preferred_element_type=F32))
            if add is not None:
                du = du + add_ref[:, cs]
            du_ref[:, cs] = du
            ub, dyb = u16[:, cs], dy16[:, cs]
            dbre_ref[gb] += lax.dot_general(ub, gr16, tn, preferred_element_type=F32)
            dbim_ref[gb] += lax.dot_general(ub, gi16, tn, preferred_element_type=F32)
            dcre_ref[gb] += lax.dot_general(sr_buf[:, sl].astype(BF16), dyb, tn, preferred_element_type=F32)
            dcim_ref[gb] -= lax.dot_general(si_buf[:, sl].astype(BF16), dyb, tn, preferred_element_type=F32)

    row_spec = pl.BlockSpec((R, D_MODEL), lambda i: (chunk(i), 0))
    st_spec = pl.BlockSpec((2, N_SEG, W), lambda i: (0, 0, 0))
    wb_spec = pl.BlockSpec(b_re.shape, lambda i: (0, 0, 0))
    wc_spec = pl.BlockSpec(c_re.shape, lambda i: (0, 0, 0))
    ins = [dy, u, ckpt, b_re, b_im, c_re, c_im, lam, init_adj]
    u_spec = pl.BlockSpec((R, D_MODEL), lambda i: (chunk(i) + u_off // R, 0))
    in_specs = [row_spec, u_spec, pl.BlockSpec((1, 2, N_SEG, W), lambda i: (chunk(i), 0, 0, 0)),
                wb_spec, wb_spec, wc_spec, wc_spec, pl.BlockSpec((2, 1, W), lambda i: (0, 0, 0)), st_spec]
    if add is not None:
        ins.append(add)
        in_specs.append(row_spec)
    out_shape = [jax.ShapeDtypeStruct((N, D_MODEL), F32), jax.ShapeDtypeStruct((2, N_SEG, W), F32),
                 jax.ShapeDtypeStruct(b_re.shape, F32), jax.ShapeDtypeStruct(b_re.shape, F32),
                 jax.ShapeDtypeStruct(c_re.shape, F32), jax.ShapeDtypeStruct(c_re.shape, F32),
                 jax.ShapeDtypeStruct((2, N_SEG, W), F32)]
    out_specs = [row_spec, st_spec, wb_spec, wb_spec, wc_spec, wc_spec, st_spec]
    return pl.pallas_call(
        body, out_shape=out_shape, grid=(nch,), in_specs=in_specs, out_specs=out_specs,
        scratch_shapes=[pltpu.VMEM((R, W), F32) for _ in range(4)] + [pltpu.VMEM((2, N_SEG, W), F32)],
        compiler_params=_params(("arbitrary",), VMEM_LIMIT), name=name)(*ins)


def adamw(w, g, m, v, name="adamw"):
    n, d = w.shape
    tm = n if n <= 512 else 256
    assert n % tm == 0
    c1 = 1.0 - ADAM_B1 ** ADAM_STEP
    c2 = 1.0 - ADAM_B2 ** ADAM_STEP

    def body(w_ref, g_ref, m_ref, v_ref, d_ref, nm_ref, nv_ref):
        g_ = g_ref[...]
        m_ = ADAM_B1 * m_ref[...] + (1.0 - ADAM_B1) * g_
        v_ = ADAM_B2 * v_ref[...] + (1.0 - ADAM_B2) * (g_ * g_)
        d_ref[...] = -ADAM_LR * ((m_ / c1) / (jnp.sqrt(v_ / c2) + ADAM_EPS) + ADAM_WD * w_ref[...])
        nm_ref[...] = m_
        nv_ref[...] = v_

    spec = pl.BlockSpec((tm, d), lambda i: (i, 0))
    return pl.pallas_call(
        body, out_shape=[jax.ShapeDtypeStruct((n, d), F32)] * 3, grid=(n // tm,),
        in_specs=[spec] * 4, out_specs=[spec] * 3,
        compiler_params=_params(("parallel",), VMEM_LIMIT), name=name)(w, g, m, v)


def _coords():
    return lax.axis_index("x"), lax.axis_index("y"), lax.axis_index("c")


def exchange(arrays, out_shapes, remote, local, name, aliases=None):
    n_in, n_out, n_rem, n_loc = len(arrays), len(out_shapes), len(remote), len(local)

    def at(ref, idx):
        return ref if idx is None else ref.at[idx]

    def body(*refs):
        ins, outs = refs[:n_in], refs[n_in:n_in + n_out]
        send_sems, recv_sems, local_sems = refs[n_in + n_out:]
        me = _coords()
        sends, recvs = [], []
        for k, (flip, ii, src_at, oi, dst_at) in enumerate(remote):
            peer = (me[0] ^ flip[0], me[1] ^ flip[1], me[2] ^ flip[2])
            src = at(ins[ii], src_at(me, peer))
            sends.append(pltpu.make_async_remote_copy(
                src_ref=src, dst_ref=at(outs[oi], dst_at(me)), send_sem=send_sems.at[k], recv_sem=recv_sems.at[k],
                device_id=peer, device_id_type=MESH))
            recvs.append(pltpu.make_async_remote_copy(
                src_ref=src, dst_ref=at(outs[oi], dst_at(peer)), send_sem=send_sems.at[k], recv_sem=recv_sems.at[k],
                device_id=peer, device_id_type=MESH))
        locs = [pltpu.make_async_copy(at(ins[ii], src_at(me)), at(outs[oi], dst_at(me)), local_sems.at[k])
                for k, (ii, src_at, oi, dst_at) in enumerate(local)]
        for cp in locs + sends:
            cp.start()
        for cp in recvs:
            cp.wait_recv()
        for cp in sends:
            cp.wait_send()
        for cp in locs:
            cp.wait()

    hbm = pl.BlockSpec(memory_space=pl.ANY)
    return pl.pallas_call(
        body, out_shape=list(out_shapes), in_specs=[hbm] * n_in, out_specs=[hbm] * n_out,
        scratch_shapes=[pltpu.SemaphoreType.DMA((n_rem,)), pltpu.SemaphoreType.DMA((n_rem,)),
                        pltpu.SemaphoreType.DMA((max(n_loc, 1),))],
        input_output_aliases=aliases or {}, name=name)(*arrays)


ALL_FLIPS = [(dx, dy, dc) for dx in (0, 1) for dy in (0, 1) for dc in (0, 1)][1:]
CHIP_FLIPS = [(1, 0, 0), (0, 1, 0), (1, 1, 0)]
CORE_FLIP = (0, 0, 1)


def _dev_index(p):
    return 4 * p[0] + 2 * p[1] + p[2]


def _chip_index(p):
    return 2 * p[0] + p[1]


def _gather(xs, flips, index, n, name):
    arrays = [x[None] for x in xs]
    outs = [jax.ShapeDtypeStruct((n,) + x.shape, x.dtype) for x in xs]
    remote = [(f, a, lambda me, peer: (0,), a, lambda s: (index(s),)) for a in range(len(xs)) for f in flips]
    local = [(a, lambda me: (0,), a, lambda me: (index(me),)) for a in range(len(xs))]
    return exchange(arrays, outs, remote, local, name)


def allgather_devices(x, name):
    return _gather([x], ALL_FLIPS, _dev_index, N_DEV, name)[0]


def allgather_chips(xs, name):
    return _gather(xs, CHIP_FLIPS, _chip_index, N_CHIP, name)


def _half_tile(h, cd):
    return h if h * cd * 4 <= (1 << 20) else math.gcd(512, h)


def pair_add(g, got, core, out_dtype, name):
    _, _, h, cd = g.shape
    th = _half_tile(h, cd)

    def body(c_ref, g_ref, got_ref, o_ref):
        o_ref[0] = (g_ref[0, 0] + got_ref[0]).astype(o_ref.dtype)

    return pl.pallas_call(
        body, out_shape=jax.ShapeDtypeStruct((N_CHIP, h, cd), out_dtype),
        grid_spec=pltpu.PrefetchScalarGridSpec(
            num_scalar_prefetch=1, grid=(N_CHIP, h // th),
            in_specs=[pl.BlockSpec((1, 1, th, cd), lambda q, i, c: (q, c[0], i, 0)),
                      pl.BlockSpec((1, th, cd), lambda q, i, c: (q, i, 0))],
            out_specs=pl.BlockSpec((1, th, cd), lambda q, i, c: (q, i, 0))),
        compiler_params=_params(("parallel", "parallel"), VMEM_LIMIT), name=name)(core, g, got)


def sum_chips(parts, core, name):
    _, h, cd = parts.shape
    th = _half_tile(h, cd)

    def body(c_ref, p_ref, o_ref):
        acc = p_ref[0].astype(F32)
        for q in range(1, N_CHIP):
            acc = acc + p_ref[q].astype(F32)
        o_ref[0] = acc

    return pl.pallas_call(
        body, out_shape=jax.ShapeDtypeStruct((2, h, cd), F32),
        grid_spec=pltpu.PrefetchScalarGridSpec(
            num_scalar_prefetch=1, grid=(h // th,),
            in_specs=[pl.BlockSpec((N_CHIP, th, cd), lambda i, c: (0, i, 0))],
            out_specs=pl.BlockSpec((1, th, cd), lambda i, c: (c[0], i, 0))),
        compiler_params=_params(("parallel",), VMEM_LIMIT), name=name)(core, parts)


def to_segments(a, n_ctx):
    def one(p):
        n = p.shape[0]
        return p.reshape(N_SEG, n // N_SEG, -1).transpose(1, 0, 2).reshape(n, -1)
    return jnp.concatenate([one(a[:n_ctx]), one(a[n_ctx:])], axis=0) if n_ctx else one(a)


def from_segments(a, n_ctx):
    def one(p):
        n = p.shape[0]
        return p.reshape(n // N_SEG, N_SEG, -1).transpose(1, 0, 2).reshape(n, -1)
    return jnp.concatenate([one(a[:n_ctx]), one(a[n_ctx:])], axis=0) if n_ctx else one(a)


def rope_tables(n_ctx, n_lat):
    rows = n_lat // GRID_W
    row = jnp.repeat(jnp.arange(rows, dtype=jnp.int32), GRID_W)
    col = jnp.tile(jnp.arange(GRID_W, dtype=jnp.int32), rows)
    d = QK_ROPE_DIM // 2
    inv = 1.0 / (ROPE_THETA ** (jnp.arange(0, d, 2, dtype=F32) / d))
    ang = jnp.concatenate([row.astype(F32)[:, None] * inv[None, :], col.astype(F32)[:, None] * inv[None, :]], axis=1)
    cos = jnp.concatenate([jnp.ones((n_ctx, d), F32), jnp.cos(ang)], axis=0)
    sin = jnp.concatenate([jnp.zeros((n_ctx, d), F32), jnp.sin(ang)], axis=0)
    q = QK_ROPE_DIM // 4
    T = n_ctx + n_lat
    ones, zeros = jnp.ones((T, QK_NOPE_DIM), F32), jnp.zeros((T, QK_NOPE_DIM), F32)
    tail, z8 = jnp.zeros((T, HEAD_LANES - QK_DIM), F32), jnp.zeros((T, q), F32)
    cr, cc, sr, sc = cos[:, :q], cos[:, q:], sin[:, :q], sin[:, q:]
    cos_t = jnp.concatenate([ones, cr, cr, cc, cc, tail], axis=1)
    sin_next = jnp.concatenate([zeros, -sr, z8, -sc, z8, tail], axis=1)
    sin_prev = jnp.concatenate([zeros, z8, sr, z8, sc, tail], axis=1)
    return cos_t, sin_next, sin_prev


def pad_heads(w, used):
    k = w.shape[0]
    return jnp.pad(w.reshape(k, MLA_HEADS, used), ((0, 0), (0, 0), (0, HEAD_LANES - used))).reshape(k, -1)


def unpad_heads(w, used):
    k = w.shape[0]
    return w.reshape(k, MLA_HEADS, HEAD_LANES)[:, :, :used].reshape(k, MLA_HEADS * used)


def rotary_spread():
    lane = np.arange(MLA_HEADS * HEAD_LANES) % HEAD_LANES
    return jnp.asarray(lane[None, :] == (QK_NOPE_DIM + np.arange(QK_ROPE_DIM))[:, None], BF16)


def s5_discretise(a_re, a_im, log_step, b_re, b_im):
    dt = jnp.exp(log_step)[:, None]
    mag = jnp.exp(a_re * dt)
    lb_re = mag * jnp.cos(a_im * dt)
    lb_im = mag * jnp.sin(a_im * dt)
    den = a_re * a_re + a_im * a_im
    nr = lb_re - 1.0
    f_re = ((nr * a_re + lb_im * a_im) / den)[..., None]
    f_im = ((lb_im * a_re - nr * a_im) / den)[..., None]
    return lb_re, lb_im, f_re * b_re - f_im * b_im, f_re * b_im + f_im * b_re


def s5_block_weights(lb_re, lb_im, bb_re, bb_im, c_re, c_im):
    eye = jnp.eye(GROUPS_PER_BLOCK, dtype=F32)
    lam = jnp.stack([lb_re.reshape(1, S5_LANES), lb_im.reshape(1, S5_LANES)])

    def b_blocks(bb):
        t = bb.reshape(N_BLOCKS, GROUPS_PER_BLOCK, S5_STATE, S5_GROUP)
        return jnp.einsum("bgpc,gh->bgchp", t, eye).reshape(N_BLOCKS, BLK_CH, BLK_ST).astype(BF16)

    def c_blocks(cc):
        t = cc.reshape(N_BLOCKS, GROUPS_PER_BLOCK, S5_GROUP, S5_STATE)
        return jnp.einsum("bgcp,gh->bgphc", t, eye).reshape(N_BLOCKS, BLK_ST, BLK_CH).astype(BF16)

    return lam, b_blocks(bb_re), b_blocks(bb_im), c_blocks(c_re), c_blocks(c_im)


def b_block_diag(db):
    t = db.reshape(N_BLOCKS, GROUPS_PER_BLOCK, S5_GROUP, GROUPS_PER_BLOCK, S5_STATE)
    return jnp.einsum("bgchp,gh->bgpc", t, jnp.eye(GROUPS_PER_BLOCK, dtype=F32)).reshape(S5_GROUPS, S5_STATE, S5_GROUP)


def c_block_diag(dc):
    t = dc.reshape(N_BLOCKS, GROUPS_PER_BLOCK, S5_STATE, GROUPS_PER_BLOCK, S5_GROUP)
    return jnp.einsum("bgphc,gh->bgcp", t, jnp.eye(GROUPS_PER_BLOCK, dtype=F32)).reshape(S5_GROUPS, S5_GROUP, S5_STATE)


def conj(a):
    return jnp.stack([a[0], -a[1]])


PACK_TILE = 16 * 128


def pack_flat(parts, dtype):
    flat = [p.reshape(-1).astype(dtype) for p in parts]
    sizes = [f.shape[0] for f in flat]
    total = sum(sizes)
    pad = (-total) % PACK_TILE
    if pad:
        flat.append(jnp.zeros((pad,), dtype))
    offs = np.cumsum([0] + sizes)[:-1].tolist()
    return jnp.concatenate(flat).reshape(-1, 128), offs


def unpack_flat(buf, offs, shapes):
    flat = buf.reshape(-1)
    return [flat[o:o + int(np.prod(s))].reshape(s) for o, s in zip(offs, shapes)]


def s5_forward(p1, n_ctx, dirs):
    saved = []
    y = None
    ctx_rows, lat_rows = (0, n_ctx), (n_ctx, p1.shape[0] - n_ctx)
    zeros_tile = jnp.zeros((2, N_SEG, S5_LANES), F32)
    zeros_row = jnp.zeros((2, 1, S5_LANES), F32)
    for k, (lam, b_re, b_im, c_re, c_im) in enumerate(dirs):
        rev = k == 1
        last = 0 if rev else N_SEG - 1
        _, _, fin = s5_scan(p1, b_re, b_im, lam, zeros_tile, reverse=rev, rows=ctx_rows, name=f"s5_ctx_finals{k}")
        carry_c = s5_chain(fin, zeros_row, lam, n_ctx // N_SEG, rev, name=f"s5_ctx_chain{k}")
        _, ck_c, fin_c = s5_scan(p1, b_re, b_im, lam, carry_c, reverse=rev, want_ckpt=True, rows=ctx_rows,
                                 name=f"s5_ctx_scan{k}")
        s0 = fin_c[:, last:last + 1, :]
        _, _, fin = s5_scan(p1, b_re, b_im, lam, zeros_tile, reverse=rev, rows=lat_rows, name=f"s5_lat_finals{k}")
        carry_l = s5_chain(fin, s0, lam, lat_rows[1] // N_SEG, rev, name=f"s5_lat_chain{k}")
        y, ck_l, _ = s5_scan(p1, b_re, b_im, lam, carry_l, reverse=rev, c_re=c_re, c_im=c_im, add=y,
                             want_ckpt=True, rows=lat_rows, name=f"s5_lat_scan{k}")
        saved.append((ck_c, ck_l))
    return y, saved


def s5_backward(dy_l, du_extra_l, p1, n_ctx, dirs, saved):
    n_lat = p1.shape[0] - n_ctx
    zeros_tile = jnp.zeros((2, N_SEG, S5_LANES), F32)
    zeros_row = jnp.zeros((2, 1, S5_LANES), F32)
    dy_c = jnp.zeros((n_ctx, D_MODEL), F32)
    du_l, du_c = du_extra_l, None
    grads = []
    for k, (lam, b_re, b_im, c_re, c_im) in enumerate(dirs):
        rev = k == 1
        lam_c = conj(lam)
        ck_c, ck_l = saved[k]
        first = N_SEG - 1 if rev else 0
        _, _, fin = s5_scan(dy_l, c_re, c_im, lam_c, zeros_tile, reverse=not rev, adjoint=True,
                            name=f"s5_lat_adj_finals{k}")
        carry = s5_chain(fin, zeros_row, lam_c, n_lat // N_SEG, not rev, name=f"s5_lat_adj_chain{k}")
        du_l, dlam_l, dbr_l, dbi_l, dcr_l, dci_l, fin_a = s5_grads(
            dy_l, p1, ck_l, b_re, b_im, c_re, c_im, lam, carry, reverse=rev, add=du_l, u_off=n_ctx,
            name=f"s5_lat_grads{k}")
        g0 = fin_a[:, first:first + 1, :]
        carry = s5_chain(zeros_tile, g0, lam_c, n_ctx // N_SEG, not rev, name=f"s5_ctx_adj_chain{k}")
        du_c, dlam_c, dbr_c, dbi_c, _, _, _ = s5_grads(
            dy_c, p1, ck_c, b_re, b_im, c_re, c_im, lam, carry, reverse=rev, add=du_c, name=f"s5_ctx_grads{k}")
        dlam = jnp.sum(dlam_l + dlam_c, axis=1)
        grads.append((dlam, b_block_diag(dbr_l + dbr_c), b_block_diag(dbi_l + dbi_c),
                      c_block_diag(dcr_l), c_block_diag(dci_l)))
    return jnp.concatenate([du_c, du_l], axis=0), grads


def local_step(x, ctx, target, mod, w):
    L, Lc = x.shape[0], ctx.shape[0]
    T = L + Lc
    assert L % Lc == 0 and Lc % (2 * N_SEG) == 0 and L % GRID_W == 0
    D = D_MODEL
    X0 = jnp.concatenate([ctx, x], axis=0)

    def mod_of(i, j):
        return mod[i, :, j, :][:, None, :]

    def vec(v):
        return v.reshape(1, 1, -1).astype(F32)

    g0 = vec(w["norm_g"][0])
    (H0,) = rowwise_fwd(f_norm_mod, [X0], [g0, mod_of(0, 1), mod_of(0, 0)], [D], [BF16], T, Lc, "l0_norm")
    p0 = mm_nn(H0, w["mla_w_in"], name="l0_in")
    z0 = Rows(p0, D, col_blk=0)
    cq = Rows(p0, Q_LORA_RANK, col_blk=D // Q_LORA_RANK)
    ckv = Rows(p0, KV_LORA_RANK, col_blk=(D + Q_LORA_RANK) // KV_LORA_RANK)
    kr = p0[:, D + Q_LORA_RANK + KV_LORA_RANK:]
    qng, kvng = vec(w["mla_q_norm"]), vec(w["mla_kv_norm"])
    (qn,) = rowwise_fwd(f_rms, [cq], [qng], [Q_LORA_RANK], [BF16], T, 0, "l0_qnorm")
    (kvn,) = rowwise_fwd(f_rms, [ckv], [kvng], [KV_LORA_RANK], [BF16], T, 0, "l0_kvnorm")
    tabs = rope_tables(Lc, L)
    spread = rotary_spread()
    w_uq_p = pad_heads(w["mla_w_uq"], QK_DIM)
    w_ukv3 = w["mla_w_ukv"].reshape(KV_LORA_RANK, MLA_HEADS, QK_NOPE_DIM + V_HEAD_DIM)
    w_kn_p = pad_heads(w_ukv3[:, :, :QK_NOPE_DIM].reshape(KV_LORA_RANK, -1), QK_NOPE_DIM)
    w_v = w_ukv3[:, :, QK_NOPE_DIM:].reshape(KV_LORA_RANK, -1)
    qb = q_heads(qn, w_uq_p, tabs)
    kb, vb = kv_heads(kvn, w_kn_p, w_v, kr, spread, tabs)
    o, lse = attn_fwd(qb, kb, vb, Lc)
    (og,) = rowwise_fwd(f_gate, [o, z0], [], [D], [BF16], T, 0, "l0_gate")
    out0 = mm_nn(og, w["mla_w_out"], name="l0_out")
    (X1,) = rowwise_fwd(f_res, [X0, out0], [mod_of(0, 2)], [D], [F32], T, Lc, "l0_res")

    X1p = to_segments(X1, Lc)
    tgt_p = to_segments(target, 0)
    g1 = vec(w["norm_g"][1])
    (H1,) = rowwise_fwd(f_norm_mod, [X1p], [g1, mod_of(1, 1), mod_of(1, 0)], [D], [BF16], T, Lc, "l1_norm")
    p1 = mm_nn(H1, w["s5_w_in"], name="l1_in")
    disc_fn = lambda *a: tuple(zip(*[s5_discretise(a[0][k], a[1][k], a[2][k], a[3][k], a[4][k]) for k in range(2)]))
    disc, disc_vjp = jax.vjp(disc_fn, w["s5_a_re"], w["s5_a_im"], w["s5_log_step"], w["s5_b_re"], w["s5_b_im"])
    dirs = [s5_block_weights(disc[0][k], disc[1][k], disc[2][k], disc[3][k], w["s5_c_re"][k], w["s5_c_im"][k])
            for k in range(2)]
    y_ssm, s5_saved = s5_forward(p1, Lc, dirs)
    d_vec, bg_vec = vec(w["s5_d"]), vec(w["s5_b_glu"])
    u_lat = Rows(p1, D, row_off=Lc, col_blk=0)
    z_lat = Rows(p1, D, row_off=Lc, col_blk=1)
    (ya,) = rowwise_fwd(f_s5_act, [y_ssm, u_lat], [d_vec], [D], [F32], L, 0, "l1_act")
    gl = mm_nn(ya, w["s5_w_glu"], name="l1_glu")
    (y3,) = rowwise_fwd(f_s5_glu, [ya, gl, z_lat], [bg_vec], [D], [BF16], L, 0, "l1_gate")
    out1 = mm_nn(y3, w["s5_w_out"], name="l1_out")
    gt1 = mod[1, 1:2, 2, :][:, None, :]
    x1_lat = Rows(X1p, D, row_off=Lc)
    (X2,) = rowwise_fwd(f_res, [x1_lat, out1], [gt1], [D], [F32], L, 0, "l1_res")

    fg = w["final_g"].reshape(1, D).astype(F32)
    lvec, dX2, d_fg = loss_and_grad(X2, fg, tgt_p)
    loss = jnp.sum(lvec)
    gw = {"final_g": d_fg.reshape(D)}
    dmod = {}

    d_out1, d_gt1 = rowwise_bwd(f_res, [x1_lat, out1], [gt1], [dX2], [1], [0], L, 0, "l1_res_bwd")
    d_y3 = mm_nt(d_out1, w["s5_w_out"], name="l1_out_dx")
    gw["s5_w_out"] = mm_tn(y3, d_out1, name="l1_out_dw")
    d_ya_a, d_gl, d_z1, d_bg = rowwise_bwd(f_s5_glu, [ya, gl, z_lat], [bg_vec], [d_y3], [0, 1, 2], [0], L, 0,
                                           "l1_gate_bwd")
    gw["s5_b_glu"] = d_bg.reshape(-1)
    gw["s5_w_glu"] = mm_tn(ya, d_gl, name="l1_glu_dw")
    d_ya = d_ya_a + mm_nt(d_gl, w["s5_w_glu"], name="l1_glu_dx")
    d_yssm, d_u_act, d_d = rowwise_bwd(f_s5_act, [y_ssm, u_lat], [d_vec], [d_ya], [0, 1], [0], L, 0, "l1_act_bwd")
    gw["s5_d"] = d_d.reshape(-1)
    du_p, s5_g = s5_backward(d_yssm, d_u_act, p1, Lc, dirs, s5_saved)
    d_disc = tuple(tuple(s5_g[k][j - 1].reshape(disc[j][k].shape) if j >= 2 else
                         s5_g[k][0][j].reshape(disc[j][k].shape) for k in range(2)) for j in range(4))
    gw["s5_a_re"], gw["s5_a_im"], gw["s5_log_step"], gw["s5_b_re"], gw["s5_b_im"] = disc_vjp(d_disc)
    gw["s5_c_re"] = jnp.stack([s5_g[0][3], s5_g[1][3]])
    gw["s5_c_im"] = jnp.stack([s5_g[0][4], s5_g[1][4]])
    d_p1 = jnp.concatenate([du_p, jnp.concatenate([jnp.zeros((Lc, D), F32), d_z1], axis=0)], axis=1)
    d_H1 = mm_nt(d_p1, w["s5_w_in"], name="l1_in_dx")
    gw["s5_w_in"] = mm_tn(H1, d_p1, name="l1_in_dw")
    d_X1p, d_g1, d_sc1, d_sh1 = rowwise_bwd(f_norm_mod, [X1p], [g1, mod_of(1, 1), mod_of(1, 0)], [d_H1],
                                            [0], [0, 1, 2], T, Lc, "l1_norm_bwd")
    d_X1p = d_X1p + jnp.concatenate([jnp.zeros((Lc, D), F32), dX2], axis=0)
    d_gt1_full = jnp.concatenate([jnp.zeros((1, 1, D), F32), d_gt1], axis=0)
    dmod[1] = (d_sh1, d_sc1, d_gt1_full)
    d_X1 = from_segments(d_X1p, Lc)

    d_out0, d_gt0 = rowwise_bwd(f_res, [X0, out0], [mod_of(0, 2)], [d_X1], [1], [0], T, Lc, "l0_res_bwd")
    d_og = mm_nt(d_out0, w["mla_w_out"], name="l0_out_dx")
    gw["mla_w_out"] = mm_tn(og, d_out0, name="l0_out_dw")
    d_o, d_z0 = rowwise_bwd(f_gate, [o, z0], [], [d_og], [0, 1], [], T, 0, "l0_gate_bwd")
    dq_p, delta = attn_bwd_dq(qb, kb, vb, o, d_o, lse, Lc)
    dk_p, d_v = attn_bwd_dkv(qb, kb, vb, d_o, lse.reshape(MLA_HEADS, 1, T), delta.reshape(MLA_HEADS, 1, T), Lc)
    (d_q,) = heads_unrope(dq_p, tabs, name="l0_q_unrope")
    d_k, d_kr = heads_unrope(dk_p, tabs, spread, name="l0_k_unrope")
    d_qn = mm_nt(d_q, w_uq_p, name="l0_uq_dx")
    gw["mla_w_uq"] = unpad_heads(mm_tn(qn, d_q, name="l0_uq_dw"), QK_DIM)
    d_kvn = mm_nt(d_k, w_kn_p, name="l0_ukn_dx") + mm_nt(d_v, w_v, name="l0_uv_dx")
    dw_kn = unpad_heads(mm_tn(kvn, d_k, name="l0_ukn_dw"), QK_NOPE_DIM).reshape(KV_LORA_RANK, MLA_HEADS, QK_NOPE_DIM)
    dw_v = mm_tn(kvn, d_v, name="l0_uv_dw").reshape(KV_LORA_RANK, MLA_HEADS, V_HEAD_DIM)
    gw["mla_w_ukv"] = jnp.concatenate([dw_kn, dw_v], axis=-1).reshape(KV_LORA_RANK, -1)
    d_cq, d_qng = rowwise_bwd(f_rms, [cq], [qng], [d_qn], [0], [0], T, 0, "l0_qnorm_bwd")
    d_ckv, d_kvng = rowwise_bwd(f_rms, [ckv], [kvng], [d_kvn], [0], [0], T, 0, "l0_kvnorm_bwd")
    gw["mla_q_norm"] = d_qng.reshape(-1)
    gw["mla_kv_norm"] = d_kvng.reshape(-1)
    d_p0 = jnp.concatenate([d_z0, d_cq, d_ckv, d_kr], axis=1)
    d_H0 = mm_nt(d_p0, w["mla_w_in"], name="l0_in_dx")
    gw["mla_w_in"] = mm_tn(H0, d_p0, name="l0_in_dw")
    d_X0, d_g0, d_sc0, d_sh0 = rowwise_bwd(f_norm_mod, [X0], [g0, mod_of(0, 1), mod_of(0, 0)], [d_H0],
                                           [0], [0, 1, 2], T, Lc, "l0_norm_bwd")
    dmod[0] = (d_sh0, d_sc0, d_gt0)
    gw["norm_g"] = jnp.stack([d_g0.reshape(D), d_g1.reshape(D)])
    dx = (d_X0 + d_X1)[Lc:]
    dmod_arr = jnp.stack([jnp.stack([dmod[i][j][:, 0, :] for j in range(3)], axis=1) for i in range(2)])
    return loss, dx, dmod_arr, gw


SHARDED = {
    "mla_w_in": 1, "mla_w_uq": 1, "mla_w_ukv": 1, "mla_w_out": 0,
    "s5_w_in": 1, "s5_w_glu": 0, "s5_w_out": 0, "s5_d": 0, "s5_b_glu": 0,
}
SHARDED_MATS = ["mla_w_in", "mla_w_uq", "mla_w_ukv", "mla_w_out", "s5_w_in", "s5_w_glu", "s5_w_out"]
SHARDED_VECS = ["s5_d", "s5_b_glu"]
REPLICATED = ["norm_g", "mla_q_norm", "mla_kv_norm", "s5_a_re", "s5_a_im", "s5_log_step", "s5_b_re", "s5_b_im",
              "s5_c_re", "s5_c_im", "final_g"]
WEIGHT_ORDER = ["c_ctx", "ada_w", "ada_b", "norm_g", "mla_w_in", "mla_q_norm", "mla_w_uq", "mla_kv_norm", "mla_w_ukv",
                "mla_w_out", "s5_w_in", "s5_a_re", "s5_a_im", "s5_log_step", "s5_b_re", "s5_b_im", "s5_c_re", "s5_c_im",
                "s5_d", "s5_w_glu", "s5_b_glu", "s5_w_out", "final_g"]


P0_HEAD = Q_LORA_RANK + KV_LORA_RANK + QK_ROPE_DIM


def w_in_to_kernel_order(w):
    return jnp.concatenate([w[:, P0_HEAD:], w[:, :P0_HEAD]], axis=1)


def w_in_from_kernel_order(w):
    return jnp.concatenate([w[:, D_MODEL:], w[:, :D_MODEL]], axis=1)


def gather_weights(ws):
    mats = [ws[n].astype(BF16) for n in SHARDED_MATS]
    vecs = [jnp.pad(ws[n].reshape(-1, 128), ((0, 6), (0, 0))) for n in SHARDED_VECS]
    outs = allgather_chips(mats + vecs, "gather_weights")
    full = {}
    for n, o in zip(SHARDED_MATS, outs):
        full[n] = o.reshape(-1, o.shape[-1]) if SHARDED[n] == 0 else o.transpose(1, 0, 2).reshape(o.shape[1], -1)
    full["mla_w_in"] = w_in_to_kernel_order(full["mla_w_in"])
    for n, o in zip(SHARDED_VECS, outs[len(mats):]):
        full[n] = o[:, :2, :].reshape(-1)
    return full


def reduce_gradients(gw):
    me = _coords()
    core = me[2].reshape(1).astype(jnp.int32)
    slots = {}
    for n in SHARDED_MATS:
        g = w_in_from_kernel_order(gw[n]) if n == "mla_w_in" else gw[n]
        if SHARDED[n] == 0:
            slots[n] = g.reshape(N_CHIP, 2, g.shape[0] // (2 * N_CHIP), g.shape[1])
        else:
            k, n4 = g.shape
            slots[n] = g.reshape(k, N_CHIP, n4 // N_CHIP).transpose(1, 0, 2).reshape(N_CHIP, 2, k // 2, n4 // N_CHIP)
    small_names = REPLICATED + SHARDED_VECS
    small, small_offs = pack_flat([gw[n].astype(F32) for n in small_names], F32)
    small = jnp.pad(small, ((0, (-small.shape[0]) % (N_CHIP * 32)), (0, 0)))
    slots["small"] = small.reshape(N_CHIP, 2, -1, 128)
    names = list(slots)
    idx = range(len(names))
    got = exchange(
        [slots[n] for n in names],
        [jax.ShapeDtypeStruct((N_CHIP,) + slots[n].shape[2:], F32) for n in names],
        [(CORE_FLIP, i, lambda me, peer: (slice(None), 1 - me[2]), i, lambda s: None) for i in idx], [],
        "grads_swap_in")
    sums = [pair_add(slots[n], g, core, F32 if n == "small" else BF16, f"grads_pair_{n}") for n, g in zip(names, got)]
    parts = exchange(
        sums, [jax.ShapeDtypeStruct(s.shape, s.dtype) for s in sums],
        [(f, i, lambda me, peer: (_chip_index(peer),), i, lambda s: (_chip_index(s),)) for i in idx for f in CHIP_FLIPS],
        [(i, lambda me: (_chip_index(me),), i, lambda me: (_chip_index(me),)) for i in idx],
        "grads_scatter")
    halves = [sum_chips(p, core, f"grads_sum_{n}") for n, p in zip(names, parts)]
    fulls = exchange(
        halves, [jax.ShapeDtypeStruct(h.shape, F32) for h in halves],
        [(CORE_FLIP, i, lambda me, peer: (me[2],), i, lambda s: (s[2],)) for i in idx], [],
        "grads_swap_out", aliases={i: i for i in idx})
    out = {n: f.reshape(-1, f.shape[-1]) for n, f in zip(names, fulls)}
    (small_all,) = allgather_chips([out.pop("small")], "grads_gather_small")
    vals = unpack_flat(small_all, small_offs, [gw[n].shape for n in small_names])
    for n, v in zip(small_names, vals):
        if n in SHARDED_VECS:
            size = v.shape[0] // N_CHIP
            v = lax.dynamic_slice_in_dim(v, _chip_index(me) * size, size)
        out[n] = v
    return out


def kernel(x, c, ctx, c_ctx, ada_w, ada_b, norm_g, mla_w_in, mla_q_norm, mla_w_uq, mla_kv_norm, mla_w_ukv, mla_w_out, s5_w_in, s5_a_re, s5_a_im, s5_log_step, s5_b_re, s5_b_im, s5_c_re, s5_c_im, s5_d, s5_w_glu, s5_b_glu, s5_w_out, final_g, loss_target, m_c_ctx, m_ada_w, m_ada_b, m_norm_g, m_mla_w_in, m_mla_q_norm, m_mla_w_uq, m_mla_kv_norm, m_mla_w_ukv, m_mla_w_out, m_s5_w_in, m_s5_a_re, m_s5_a_im, m_s5_log_step, m_s5_b_re, m_s5_b_im, m_s5_c_re, m_s5_c_im, m_s5_d, m_s5_w_glu, m_s5_b_glu, m_s5_w_out, m_final_g, v_c_ctx, v_ada_w, v_ada_b, v_norm_g, v_mla_w_in, v_mla_q_norm, v_mla_w_uq, v_mla_kv_norm, v_mla_w_ukv, v_mla_w_out, v_s5_w_in, v_s5_a_re, v_s5_a_im, v_s5_log_step, v_s5_b_re, v_s5_b_im, v_s5_c_re, v_s5_c_im, v_s5_d, v_s5_w_glu, v_s5_b_glu, v_s5_w_out, v_final_g):
    args = dict(locals())
    weights = {n: args[n] for n in WEIGHT_ORDER}
    D = D_MODEL
    xi, yi, ci = _coords()
    chip = 2 * xi + yi
    me = 4 * xi + 2 * yi + ci
    n_col = ada_w.shape[2]

    c_all = allgather_devices(jnp.pad(c, ((0, 7), (0, 0))), "gather_c")[:, 0, :]
    cond = jnp.concatenate([c_all, jnp.broadcast_to(c_ctx[None], (8, D))], axis=0)
    (s_cond,) = rowwise_fwd(lambda v: (_silu(v),), [cond], [], [D], [F32], 16, 0, "cond_silu")
    mod_cols = jnp.stack([mm_nn(s_cond, ada_w[i], name=f"mod_proj{i}") for i in range(2)])
    (mod_all,) = allgather_chips([mod_cols], "gather_mod")
    mod_all = mod_all.transpose(1, 2, 0, 3).reshape(2, 16, 3 * D) + ada_b[:, None, :]
    mod_l = lax.dynamic_index_in_dim(mod_all, me, axis=1, keepdims=False)
    mod_c = mod_all[:, 8, :]
    mod = jnp.stack([mod_c.reshape(2, 3, D), mod_l.reshape(2, 3, D)], axis=1)

    shards = {n: weights[n][0] for n in SHARDED_MATS + SHARDED_VECS}
    w = gather_weights(shards)
    for n in ["norm_g", "final_g"]:
        w[n] = weights[n]
    for n in ["mla_q_norm", "mla_kv_norm", "s5_a_re", "s5_a_im", "s5_log_step", "s5_b_re", "s5_b_im",
              "s5_c_re", "s5_c_im"]:
        w[n] = weights[n][0]

    loss_me, dx, dmod, gw = local_step(x[0], ctx[0], loss_target[0], mod, w)
    loss = lax.psum(loss_me, ("x", "y", "c"))

    dmod_rows = allgather_devices(dmod.reshape(2, 2, 3 * D), "gather_dmod")
    dm = jnp.concatenate([dmod_rows[:, :, 1, :], dmod_rows[:, :, 0, :]], axis=0).transpose(1, 0, 2)
    g_ada_b = jnp.sum(dm, axis=1)
    dm_cols = lax.dynamic_slice_in_dim(dm, chip * n_col, n_col, axis=2)
    g_ada_w = jnp.stack([mm_tn(s_cond, dm_cols[i], name=f"mod_proj_dw{i}") for i in range(2)])
    dmc = jnp.sum(dm_cols[:, 8:, :], axis=1)
    dmc8 = jnp.broadcast_to(dmc[:, None, :], (2, 8, n_col))
    g_sc = mm_nt(dmc8[0], ada_w[0], name="mod_proj_dx0")[0] + mm_nt(dmc8[1], ada_w[1], name="mod_proj_dx1")[0]
    g_sc_all = allgather_devices(jnp.broadcast_to(g_sc[None], (8, D)), "gather_dcond")[:, 0, :]
    g_silu_cc = g_sc_all[0] + g_sc_all[2] + g_sc_all[4] + g_sc_all[6]
    (g_c_ctx,) = rowwise_bwd(lambda v: (_silu(v),), [jnp.broadcast_to(c_ctx[None], (8, D))], [],
                             [jnp.broadcast_to(g_silu_cc[None], (8, D))], [0], [], 8, 0, "cond_silu_bwd")
    g_c_ctx = g_c_ctx[0]

    gw_in = {}
    for n in SHARDED_MATS + SHARDED_VECS:
        gw_in[n] = gw[n]
    for n in REPLICATED:
        gw_in[n] = gw[n]
    red = reduce_gradients(gw_in)
    grads = {"c_ctx": g_c_ctx, "ada_w": g_ada_w, "ada_b": g_ada_b}
    for n in WEIGHT_ORDER[3:]:
        grads[n] = red[n].reshape(weights[n].shape)

    deltas, new_m, new_v = {}, {}, {}
    small = [n for n in WEIGHT_ORDER if weights[n].size < 50000]
    for n in WEIGHT_ORDER:
        if n in small:
            continue
        shp = weights[n].shape
        w2 = weights[n].reshape(-1, shp[-1] if shp[-1] >= 256 else 128)
        d_, m_, v_ = adamw(w2, grads[n].reshape(w2.shape), args["m_" + n].reshape(w2.shape),
                           args["v_" + n].reshape(w2.shape), name=f"adamw_{n}")
        deltas[n], new_m[n], new_v[n] = d_.reshape(shp), m_.reshape(shp), v_.reshape(shp)
    packs = []
    offs = None
    for src in (weights, grads, {n: args["m_" + n] for n in small}, {n: args["v_" + n] for n in small}):
        buf, offs = pack_flat([src[n] for n in small], F32)
        packs.append(buf)
    outs = adamw(*packs, name="adamw_small")
    for res, dst in zip(outs, (deltas, new_m, new_v)):
        for n, val in zip(small, unpack_flat(res, offs, [weights[n].shape for n in small])):
            dst[n] = val

    return (loss, dx[None], *[grads[n] for n in WEIGHT_ORDER], *[deltas[n] for n in WEIGHT_ORDER],
            *[new_m[n] for n in WEIGHT_ORDER], *[new_v[n] for n in WEIGHT_ORDER])
```

```python
import functools
import math

import jax
import jax.numpy as jnp
import numpy as np
from jax import lax
from jax.experimental import pallas as pl
from jax.experimental.pallas import tpu as pltpu

F32 = jnp.float32
BF16 = jnp.bfloat16

D_MODEL = 1024
GRID_W = 64
EPS = 1e-6
MLA_HEADS = 16
QK_NOPE_DIM = 64
QK_ROPE_DIM = 32
V_HEAD_DIM = 64
Q_LORA_RANK = 256
KV_LORA_RANK = 128
QK_DIM = QK_NOPE_DIM + QK_ROPE_DIM
SOFTMAX_SCALE = QK_DIM ** -0.5
ROPE_THETA = 10000.0
S5_GROUP = 16
S5_GROUPS = D_MODEL // S5_GROUP
S5_STATE = 64
S5_LANES = S5_GROUPS * S5_STATE
N_SEG = 8
GROUPS_PER_BLOCK = 8
N_BLOCKS = S5_GROUPS // GROUPS_PER_BLOCK
BLK_CH = GROUPS_PER_BLOCK * S5_GROUP
BLK_ST = GROUPS_PER_BLOCK * S5_STATE

ADAM_LR = 0.001
ADAM_B1 = 0.9
ADAM_B2 = 0.999
ADAM_EPS = 1e-08
ADAM_WD = 0.01
ADAM_STEP = 10

N_DEV = 8
N_CHIP = 4
MESH = pl.DeviceIdType.MESH
VMEM_LIMIT = 52 * 1024 * 1024
ROW_TILE = 256


def _params(sem=None, vmem=None):
    return pltpu.CompilerParams(dimension_semantics=sem, vmem_limit_bytes=vmem)


def mm_nn(a, b, out_dtype=F32, name="mm_nn"):
    M, K = a.shape
    N = b.shape[1]
    tm = math.gcd(ROW_TILE, M)

    def body(a_ref, b_ref, o_ref):
        o_ref[...] = jnp.dot(a_ref[...].astype(BF16), b_ref[...].astype(BF16),
                             preferred_element_type=F32).astype(o_ref.dtype)

    return pl.pallas_call(
        body, out_shape=jax.ShapeDtypeStruct((M, N), out_dtype), grid=(M // tm,),
        in_specs=[pl.BlockSpec((tm, K), lambda i: (i, 0)), pl.BlockSpec((K, N), lambda i: (0, 0))],
        out_specs=pl.BlockSpec((tm, N), lambda i: (i, 0)),
        compiler_params=_params(("parallel",), VMEM_LIMIT), name=name)(a, b)


def mm_nt(a, b, out_dtype=F32, name="mm_nt"):
    M, N = a.shape
    K = b.shape[0]
    tm = math.gcd(ROW_TILE, M)

    def body(a_ref, b_ref, o_ref):
        o_ref[...] = lax.dot_general(a_ref[...].astype(BF16), b_ref[...].astype(BF16),
                                     (((1,), (1,)), ((), ())),
                                     preferred_element_type=F32).astype(o_ref.dtype)

    return pl.pallas_call(
        body, out_shape=jax.ShapeDtypeStruct((M, K), out_dtype), grid=(M // tm,),
        in_specs=[pl.BlockSpec((tm, N), lambda i: (i, 0)), pl.BlockSpec((K, N), lambda i: (0, 0))],
        out_specs=pl.BlockSpec((tm, K), lambda i: (i, 0)),
        compiler_params=_params(("parallel",), VMEM_LIMIT), name=name)(a, b)


def mm_tn(a, b, name="mm_tn"):
    M, K = a.shape
    N = b.shape[1]
    tm = math.gcd(2 * ROW_TILE, M)

    def body(a_ref, b_ref, o_ref):
        @pl.when(pl.program_id(0) == 0)
        def _():
            o_ref[...] = jnp.zeros_like(o_ref)

        o_ref[...] += lax.dot_general(a_ref[...].astype(BF16), b_ref[...].astype(BF16),
                                      (((0,), (0,)), ((), ())), preferred_element_type=F32)

    return pl.pallas_call(
        body, out_shape=jax.ShapeDtypeStruct((K, N), F32), grid=(M // tm,),
        in_specs=[pl.BlockSpec((tm, K), lambda i: (i, 0)), pl.BlockSpec((tm, N), lambda i: (i, 0))],
        out_specs=pl.BlockSpec((K, N), lambda i: (0, 0)),
        compiler_params=_params(("arbitrary",), VMEM_LIMIT), name=name)(a, b)


class Rows:
    def __init__(self, arr, width=None, row_off=0, col_blk=0):
        self.arr = arr
        self.width = arr.shape[1] if width is None else width
        self.row_off = row_off
        self.col_blk = col_blk

    def spec(self, tm):
        ro, cb = self.row_off // tm, self.col_blk
        return pl.BlockSpec((tm, self.width), lambda i: (i + ro, cb))


def _as_rows(x):
    return x if isinstance(x, Rows) else Rows(x)


def _row_tile(n_rows, n_ctx_rows, rows):
    tm = math.gcd(ROW_TILE, n_rows, n_ctx_rows)
    for r in rows:
        tm = math.gcd(tm, r.row_off)
    return tm


def _bc_spec(arr, n_ctx_blocks):
    g, _, d = arr.shape
    if g == 1:
        return pl.BlockSpec((1, 1, d), lambda i: (0, 0, 0))
    return pl.BlockSpec((1, 1, d), lambda i: ((i >= n_ctx_blocks).astype(jnp.int32), 0, 0))


def rowwise_fwd(fn, rows, bcs, out_dims, out_dtypes, n_rows, n_ctx_rows, name):
    rows = [_as_rows(r) for r in rows]
    tm = _row_tile(n_rows, n_ctx_rows, rows)
    ncb = n_ctx_rows // tm
    nr, nb = len(rows), len(bcs)

    def body(*refs):
        vals = [r[...].astype(F32) for r in refs[:nr]] + [b[0].astype(F32) for b in refs[nr:nr + nb]]
        outs = fn(*vals)
        for o_ref, v in zip(refs[nr + nb:], outs):
            o_ref[...] = v.astype(o_ref.dtype)

    outs = pl.pallas_call(
        body,
        out_shape=[jax.ShapeDtypeStruct((n_rows, d), dt) for d, dt in zip(out_dims, out_dtypes)],
        grid=(n_rows // tm,),
        in_specs=[r.spec(tm) for r in rows] + [_bc_spec(b, ncb) for b in bcs],
        out_specs=[pl.BlockSpec((tm, d), lambda i: (i, 0)) for d in out_dims],
        compiler_params=_params(("parallel",), VMEM_LIMIT), name=name)(*[r.arr for r in rows], *bcs)
    return outs


def rowwise_bwd(fn, rows, bcs, cts, diff_rows, diff_bcs, n_rows, n_ctx_rows, name, ct_extra=None, lat_add=None):
    rows = [_as_rows(r) for r in rows]
    cts = [_as_rows(c) for c in cts]
    extra = [_as_rows(ct_extra)] if ct_extra is not None else []
    tm = _row_tile(n_rows, n_ctx_rows, rows + cts + extra)
    ncb = n_ctx_rows // tm
    nr, nb, nc = len(rows), len(bcs), len(cts)
    ndr, ndb = len(diff_rows), len(diff_bcs)
    n_in = nr + nb + nc + len(extra) + (lat_add is not None)

    def body(*refs):
        i = pl.program_id(0)
        rvals = [r[...].astype(F32) for r in refs[:nr]]
        bvals = [b[0].astype(F32) for b in refs[nr:nr + nb]]
        cvals = [c[...].astype(F32) for c in refs[nr + nb:nr + nb + nc]]
        if extra:
            cvals[0] = cvals[0] + refs[nr + nb + nc][...].astype(F32)
        outs = refs[n_in:]

        def f(*d):
            rv, bv = list(rvals), list(bvals)
            for k, idx in enumerate(diff_rows):
                rv[idx] = d[k]
            for k, idx in enumerate(diff_bcs):
                bv[idx] = d[ndr + k]
            return tuple(fn(*rv, *bv))

        primals = [rvals[k] for k in diff_rows] + [bvals[k] for k in diff_bcs]
        _, vjp = jax.vjp(f, *primals)
        grads = list(vjp(tuple(cvals)))
        if lat_add is not None:
            add = refs[n_in - 1][...]
            grads[0] = grads[0] + (add if lat_add.shape[0] == n_rows else jnp.where(i >= ncb, add, 0.0))
        for k in range(ndr):
            outs[k][...] = grads[k].astype(outs[k].dtype)
        for k, idx in enumerate(diff_bcs):
            o_ref = outs[ndr + k]
            first = (i == 0)
            if bcs[idx].shape[0] == 2:
                first = first | (i == ncb)

            @pl.when(first)
            def _(o_ref=o_ref):
                o_ref[...] = jnp.zeros_like(o_ref)

            o_ref[0] += grads[ndr + k]

    out_shape = [jax.ShapeDtypeStruct((n_rows, rows[k].width), F32) for k in diff_rows]
    out_shape += [jax.ShapeDtypeStruct(bcs[k].shape, F32) for k in diff_bcs]
    out_specs = [pl.BlockSpec((tm, rows[k].width), lambda i: (i, 0)) for k in diff_rows]
    out_specs += [_bc_spec(bcs[k], ncb) for k in diff_bcs]
    ins = [r.arr for r in rows] + list(bcs) + [c.arr for c in cts + extra]
    in_specs = [r.spec(tm) for r in rows] + [_bc_spec(b, ncb) for b in bcs] + [c.spec(tm) for c in cts + extra]
    if lat_add is not None:
        ins.append(lat_add)
        skip = ncb if lat_add.shape[0] != n_rows else 0
        in_specs.append(pl.BlockSpec((tm, lat_add.shape[1]), lambda i: (jnp.maximum(i - skip, 0), 0)))
    outs = pl.pallas_call(
        body, out_shape=out_shape, grid=(n_rows // tm,), in_specs=in_specs, out_specs=out_specs,
        compiler_params=_params(("arbitrary",), VMEM_LIMIT), name=name)(*ins)
    return outs


def _rms(x):
    return x * lax.rsqrt(jnp.mean(x * x, axis=-1, keepdims=True) + EPS)


def _sigmoid(x):
    return 0.5 * (jnp.tanh(0.5 * x) + 1.0)


def _silu(x):
    return x * _sigmoid(x)


def _gelu_tanh(x):
    return 0.5 * x * (1.0 + jnp.tanh(math.sqrt(2.0 / math.pi) * (x + 0.044715 * (x * x * x))))


def f_norm_mod(x, g, sc, sh):
    return ((_rms(x) * g) * (1.0 + sc) + sh,)


def f_rms(x, g):
    return (_rms(x) * g,)


def f_gate(o, z):
    return (o * _silu(z),)


def f_res(x, o, gt):
    return (x + gt * o,)


def f_s5_act(y, u, d):
    return (_gelu_tanh(y + d * u),)


def f_s5_glu(ya, gl, z, b):
    return (ya * _sigmoid(gl + b) * _silu(z),)


def loss_and_grad(x2, final_g, target, name="loss_head"):
    n, d = x2.shape
    tm = math.gcd(ROW_TILE, n)

    def row_loss(x, g, t):
        y = _rms(x) * g
        e = y - t
        return 0.5 * (e * e) * (1.0 / d)

    def body(x_ref, g_ref, t_ref, l_ref, dx_ref, dg_ref):
        @pl.when(pl.program_id(0) == 0)
        def _():
            l_ref[...] = jnp.zeros_like(l_ref)
            dg_ref[...] = jnp.zeros_like(dg_ref)

        t = t_ref[...]
        lterm, vjp = jax.vjp(lambda x, g: row_loss(x, g, t), x_ref[...], g_ref[...])
        dx, dg = vjp(jnp.ones_like(lterm))
        l_ref[...] += jnp.sum(lterm, axis=0, keepdims=True)
        dx_ref[...] = dx
        dg_ref[...] += dg

    return pl.pallas_call(
        body,
        out_shape=[jax.ShapeDtypeStruct((1, d), F32), jax.ShapeDtypeStruct((n, d), F32),
                   jax.ShapeDtypeStruct((1, d), F32)],
        grid=(n // tm,),
        in_specs=[pl.BlockSpec((tm, d), lambda i: (i, 0)), pl.BlockSpec((1, d), lambda i: (0, 0)),
                  pl.BlockSpec((tm, d), lambda i: (i, 0))],
        out_specs=[pl.BlockSpec((1, d), lambda i: (0, 0)), pl.BlockSpec((tm, d), lambda i: (i, 0)),
                   pl.BlockSpec((1, d), lambda i: (0, 0))],
        compiler_params=_params(("arbitrary",), VMEM_LIMIT), name=name)(x2, final_g, target)


NT_DIMS = (((1,), (1,)), ((), ()))
HEAD_LANES = 128
N_PAIRS = MLA_HEADS // 2


def _own_lanes(shape, hh):
    lane = lax.broadcasted_iota(jnp.int32, shape, len(shape) - 1)
    return (lane < V_HEAD_DIM) if hh == 0 else (lane >= V_HEAD_DIM)


def _rope_tiles(x, cos, sin_next, sin_prev, inverse):
    width = x.shape[-1]
    reps = width // HEAD_LANES
    c, sn, sp = (jnp.tile(t, (1, reps)) for t in (cos, sin_next, sin_prev))
    if inverse:
        return x * c + pltpu.roll(x * sn, 8, 1) + pltpu.roll(x * sp, width - 8, 1)
    return x * c + pltpu.roll(x, width - 8, 1) * sn + pltpu.roll(x, 8, 1) * sp


def attn_fwd(qb, kb, vb, n_ctx):
    T = qb.shape[0]
    tq = math.gcd(ROW_TILE, n_ctx)
    nq, ncb = T // tq, n_ctx // tq

    def body(q_ref, k_ref, v_ref, o_ref, lse_ref):
        qi = pl.program_id(1)

        def rows(n_keys):
            v = v_ref[:n_keys, :]
            outs = []
            for hh in range(2):
                hs = slice(hh * HEAD_LANES, (hh + 1) * HEAD_LANES)
                s = lax.dot_general(q_ref[:, hs], k_ref[:n_keys, hs], NT_DIMS,
                                    preferred_element_type=F32) * SOFTMAX_SCALE
                m = jnp.max(s, axis=-1, keepdims=True)
                p = jnp.exp(s - m)
                l = jnp.sum(p, axis=-1, keepdims=True)
                outs.append(jnp.dot(p.astype(BF16), v, preferred_element_type=F32) / l)
                lse_ref[hh] = m + jnp.log(l)
            o_ref[...] = jnp.where(_own_lanes(outs[0].shape, 0), outs[0], outs[1])

        pl.when(qi < ncb)(lambda: rows(n_ctx))
        pl.when(qi >= ncb)(lambda: rows(T))

    return pl.pallas_call(
        body,
        out_shape=[jax.ShapeDtypeStruct((T, MLA_HEADS * V_HEAD_DIM), F32),
                   jax.ShapeDtypeStruct((MLA_HEADS, T, 1), F32)],
        grid=(N_PAIRS, nq),
        in_specs=[pl.BlockSpec((tq, 2 * HEAD_LANES), lambda h, i: (i, h)),
                  pl.BlockSpec((T, 2 * HEAD_LANES), lambda h, i: (0, h)),
                  pl.BlockSpec((T, 2 * V_HEAD_DIM), lambda h, i: (0, h))],
        out_specs=[pl.BlockSpec((tq, 2 * V_HEAD_DIM), lambda h, i: (i, h)),
                   pl.BlockSpec((2, tq, 1), lambda h, i: (h, i, 0))],
        compiler_params=_params(("parallel", "parallel"), VMEM_LIMIT), name="attn_fwd")(qb, kb, vb)


def attn_bwd_dq(qb, kb, vb, o, do, lse, n_ctx):
    T = qb.shape[0]
    tq = math.gcd(ROW_TILE, n_ctx)
    nq, ncb = T // tq, n_ctx // tq

    def body(q_ref, k_ref, v_ref, o_ref, do_ref, lse_ref, dq_ref, delta_ref):
        qi = pl.program_id(1)

        def rows(n_keys):
            v = v_ref[:n_keys, :]
            for hh in range(2):
                hs = slice(hh * HEAD_LANES, (hh + 1) * HEAD_LANES)
                k = k_ref[:n_keys, hs]
                do = jnp.where(_own_lanes(do_ref.shape, hh), do_ref[...], 0.0)
                delta = jnp.sum(do * o_ref[...], axis=-1, keepdims=True)
                s = lax.dot_general(q_ref[:, hs], k, NT_DIMS, preferred_element_type=F32) * SOFTMAX_SCALE
                p = jnp.exp(s - lse_ref[hh])
                dp = lax.dot_general(do.astype(BF16), v, NT_DIMS, preferred_element_type=F32)
                ds = p * (dp - delta) * SOFTMAX_SCALE
                dq_ref[:, hs] = jnp.dot(ds.astype(BF16), k, preferred_element_type=F32)
                delta_ref[hh] = delta

        pl.when(qi < ncb)(lambda: rows(n_ctx))
        pl.when(qi >= ncb)(lambda: rows(T))

    return pl.pallas_call(
        body,
        out_shape=[jax.ShapeDtypeStruct((T, MLA_HEADS * HEAD_LANES), F32),
                   jax.ShapeDtypeStruct((MLA_HEADS, T, 1), F32)],
        grid=(N_PAIRS, nq),
        in_specs=[pl.BlockSpec((tq, 2 * HEAD_LANES), lambda h, i: (i, h)),
                  pl.BlockSpec((T, 2 * HEAD_LANES), lambda h, i: (0, h)),
                  pl.BlockSpec((T, 2 * V_HEAD_DIM), lambda h, i: (0, h)),
                  pl.BlockSpec((tq, 2 * V_HEAD_DIM), lambda h, i: (i, h)),
                  pl.BlockSpec((tq, 2 * V_HEAD_DIM), lambda h, i: (i, h)),
                  pl.BlockSpec((2, tq, 1), lambda h, i: (h, i, 0))],
        out_specs=[pl.BlockSpec((tq, 2 * HEAD_LANES), lambda h, i: (i, h)),
                   pl.BlockSpec((2, tq, 1), lambda h, i: (h, i, 0))],
        compiler_params=_params(("parallel", "parallel"), VMEM_LIMIT), name="attn_bwd_dq")(
            qb, kb, vb, o, do, lse)


def attn_bwd_dkv(qb, kb, vb, do, lse_rows, delta_rows, n_ctx):
    T = qb.shape[0]
    tq = math.gcd(ROW_TILE, n_ctx)
    nq, ncb = T // tq, n_ctx // tq

    def body(q_ref, do_ref, lse_ref, delta_ref, k_ref, v_ref, dk_ref, dv_ref):
        kj = pl.program_id(1)

        def cols(first):
            v = v_ref[...]
            do_all = do_ref[first:, :]
            dv = None
            for hh in range(2):
                hs = slice(hh * HEAD_LANES, (hh + 1) * HEAD_LANES)
                k = k_ref[:, hs]
                q = q_ref[first:, hs]
                do16 = jnp.where(_own_lanes(do_all.shape, hh), do_all, 0.0).astype(BF16)
                st = lax.dot_general(k, q, NT_DIMS, preferred_element_type=F32) * SOFTMAX_SCALE
                pt = jnp.exp(st - lse_ref[hh, :, first:])
                dv_h = jnp.dot(pt.astype(BF16), do16, preferred_element_type=F32)
                dv = dv_h if dv is None else dv + dv_h
                dpt = lax.dot_general(v, do16, NT_DIMS, preferred_element_type=F32)
                dst = pt * (dpt - delta_ref[hh, :, first:]) * SOFTMAX_SCALE
                dk_ref[:, hs] = jnp.dot(dst.astype(BF16), q, preferred_element_type=F32)
            dv_ref[...] = dv

        pl.when(kj < ncb)(lambda: cols(0))
        pl.when(kj >= ncb)(lambda: cols(n_ctx))

    return pl.pallas_call(
        body,
        out_shape=[jax.ShapeDtypeStruct((T, MLA_HEADS * HEAD_LANES), F32),
                   jax.ShapeDtypeStruct((T, MLA_HEADS * V_HEAD_DIM), F32)],
        grid=(N_PAIRS, nq),
        in_specs=[pl.BlockSpec((T, 2 * HEAD_LANES), lambda h, j: (0, h)),
                  pl.BlockSpec((T, 2 * V_HEAD_DIM), lambda h, j: (0, h)),
                  pl.BlockSpec((2, 1, T), lambda h, j: (h, 0, 0)),
                  pl.BlockSpec((2, 1, T), lambda h, j: (h, 0, 0)),
                  pl.BlockSpec((tq, 2 * HEAD_LANES), lambda h, j: (j, h)),
                  pl.BlockSpec((tq, 2 * V_HEAD_DIM), lambda h, j: (j, h))],
        out_specs=[pl.BlockSpec((tq, 2 * HEAD_LANES), lambda h, j: (j, h)),
                   pl.BlockSpec((tq, 2 * V_HEAD_DIM), lambda h, j: (j, h))],
        compiler_params=_params(("parallel", "parallel"), VMEM_LIMIT), name="attn_bwd_dkv")(
            qb, do, lse_rows, delta_rows, kb, vb)


def _split_bf16(x):
    hi = x.astype(BF16)
    return hi, (x - hi.astype(F32)).astype(BF16)


def q_heads(qn, w_uq_p, tabs, name="l0_uq"):
    T, K = qn.shape
    N = w_uq_p.shape[1]
    tm = math.gcd(ROW_TILE, T)

    def body(a_ref, w_ref, c_ref, sn_ref, sp_ref, o_ref):
        acc = jnp.dot(a_ref[...], w_ref[...], preferred_element_type=F32)
        o_ref[...] = _rope_tiles(acc, c_ref[...], sn_ref[...], sp_ref[...], False).astype(BF16)

    tab = pl.BlockSpec((tm, HEAD_LANES), lambda i: (i, 0))
    return pl.pallas_call(
        body, out_shape=jax.ShapeDtypeStruct((T, N), BF16), grid=(T // tm,),
        in_specs=[pl.BlockSpec((tm, K), lambda i: (i, 0)), pl.BlockSpec((K, N), lambda i: (0, 0)), tab, tab, tab],
        out_specs=pl.BlockSpec((tm, N), lambda i: (i, 0)),
        compiler_params=_params(("parallel",), VMEM_LIMIT), name=name)(qn, w_uq_p, *tabs)


def kv_heads(kvn, w_kn_p, w_v, kr, spread, tabs, name="l0_ukv"):
    T, K = kvn.shape
    N = w_kn_p.shape[1]
    NV = w_v.shape[1]
    tm = math.gcd(ROW_TILE, T)

    def body(a_ref, wk_ref, wv_ref, kr_ref, e_ref, c_ref, sn_ref, sp_ref, k_ref, v_ref):
        a = a_ref[...]
        hi, lo = _split_bf16(kr_ref[...])
        acc = (jnp.dot(a, wk_ref[...], preferred_element_type=F32)
               + jnp.dot(hi, e_ref[...], preferred_element_type=F32)
               + jnp.dot(lo, e_ref[...], preferred_element_type=F32))
        k_ref[...] = _rope_tiles(acc, c_ref[...], sn_ref[...], sp_ref[...], False).astype(BF16)
        v_ref[...] = jnp.dot(a, wv_ref[...], preferred_element_type=F32).astype(BF16)

    tab = pl.BlockSpec((tm, HEAD_LANES), lambda i: (i, 0))
    return pl.pallas_call(
        body, out_shape=[jax.ShapeDtypeStruct((T, N), BF16), jax.ShapeDtypeStruct((T, NV), BF16)], grid=(T // tm,),
        in_specs=[pl.BlockSpec((tm, K), lambda i: (i, 0)), pl.BlockSpec((K, N), lambda i: (0, 0)),
                  pl.BlockSpec((K, NV), lambda i: (0, 0)), pl.BlockSpec((tm, QK_ROPE_DIM), lambda i: (i, 0)),
                  pl.BlockSpec((QK_ROPE_DIM, N), lambda i: (0, 0)), tab, tab, tab],
        out_specs=[pl.BlockSpec((tm, N), lambda i: (i, 0)), pl.BlockSpec((tm, NV), lambda i: (i, 0))],
        compiler_params=_params(("parallel",), VMEM_LIMIT), name=name)(kvn, w_kn_p, w_v, kr, spread, *tabs)


def heads_unrope(d, tabs, spread=None, name="unrope"):
    T, N = d.shape
    tm = math.gcd(ROW_TILE, T)

    def body(*refs):
        if spread is None:
            d_ref, c_ref, sn_ref, sp_ref, o_ref = refs
        else:
            d_ref, c_ref, sn_ref, sp_ref, e_ref, o_ref, kr_ref = refs
        g = _rope_tiles(d_ref[...], c_ref[...], sn_ref[...], sp_ref[...], True)
        o_ref[...] = g.astype(BF16)
        if spread is not None:
            hi, lo = _split_bf16(g)
            kr_ref[...] = (lax.dot_general(hi, e_ref[...], NT_DIMS, preferred_element_type=F32)
                           + lax.dot_general(lo, e_ref[...], NT_DIMS, preferred_element_type=F32))

    tab = pl.BlockSpec((tm, HEAD_LANES), lambda i: (i, 0))
    row = pl.BlockSpec((tm, N), lambda i: (i, 0))
    ins, in_specs = [d, *tabs], [row, tab, tab, tab]
    out_shape, out_specs = [jax.ShapeDtypeStruct((T, N), BF16)], [row]
    if spread is not None:
        ins.append(spread)
        in_specs.append(pl.BlockSpec((QK_ROPE_DIM, N), lambda i: (0, 0)))
        out_shape.append(jax.ShapeDtypeStruct((T, QK_ROPE_DIM), F32))
        out_specs.append(pl.BlockSpec((tm, QK_ROPE_DIM), lambda i: (i, 0)))
    return pl.pallas_call(
        body, out_shape=out_shape, grid=(T // tm,), in_specs=in_specs, out_specs=out_specs,
        compiler_params=_params(("parallel",), VMEM_LIMIT), name=name)(*ins)


def _cmul(ar, ai, br, bi):
    return ar * br - ai * bi, ar * bi + ai * br


def s5_chain(finals, s0, a, n_steps, reverse, name):
    W = finals.shape[-1]
    first = N_SEG - 1 if reverse else 0

    def body(f_ref, s0_ref, a_ref, c_ref):
        pr, pi = jnp.ones((1, W), F32), jnp.zeros((1, W), F32)
        br, bi = a_ref[0], a_ref[1]
        n = n_steps
        while n:
            if n & 1:
                pr, pi = _cmul(pr, pi, br, bi)
            br, bi = _cmul(br, bi, br, bi)
            n >>= 1
        fr, fi = f_ref[0], f_ref[1]
        row = lax.broadcasted_iota(jnp.int32, (N_SEG, W), 0)
        s0r = jnp.broadcast_to(s0_ref[0], (N_SEG, W))
        s0i = jnp.broadcast_to(s0_ref[1], (N_SEG, W))
        cr = jnp.where(row == first, s0r, 0.0)
        ci = jnp.where(row == first, s0i, 0.0)
        shift = N_SEG - 1 if reverse else 1
        for _ in range(N_SEG - 1):
            mr, mi = _cmul(pr, pi, cr, ci)
            tr = pltpu.roll(fr + mr, shift, 0)
            ti = pltpu.roll(fi + mi, shift, 0)
            cr = jnp.where(row == first, s0r, tr)
            ci = jnp.where(row == first, s0i, ti)
        c_ref[0] = cr
        c_ref[1] = ci

    return pl.pallas_call(body, out_shape=jax.ShapeDtypeStruct((2, N_SEG, W), F32), name=name)(finals, s0, a)


def _scan_chunk(bur, bui, st_ref, a_ref, n_steps, reverse):
    for lc in range(S5_LANES // BLK_ST):
        sl = slice(lc * BLK_ST, (lc + 1) * BLK_ST)
        lr = jnp.broadcast_to(a_ref[0, :, sl], (N_SEG, BLK_ST))
        li = jnp.broadcast_to(a_ref[1, :, sl], (N_SEG, BLK_ST))

        def step(jj, carry, sl=sl, lr=lr, li=li):
            sr, si = carry
            j = (n_steps - 1 - jj) if reverse else jj
            r0 = pl.multiple_of(j * N_SEG, N_SEG)
            nr = lr * sr - li * si + bur[pl.ds(r0, N_SEG), sl]
            ni = lr * si + li * sr + bui[pl.ds(r0, N_SEG), sl]
            bur[pl.ds(r0, N_SEG), sl] = nr
            bui[pl.ds(r0, N_SEG), sl] = ni
            return nr, ni

        sr, si = lax.fori_loop(0, n_steps, step, (st_ref[0, :, sl], st_ref[1, :, sl]))
        st_ref[0, :, sl] = sr
        st_ref[1, :, sl] = si


def _project_in(x16, w_re, w_im, bur, bui, adjoint):
    for gb in range(N_BLOCKS):
        xb = x16[:, gb * BLK_CH:(gb + 1) * BLK_CH]
        sl = slice(gb * BLK_ST, (gb + 1) * BLK_ST)
        if adjoint:
            dn = (((1,), (1,)), ((), ()))
            bur[:, sl] = lax.dot_general(xb, w_re[gb], dn, preferred_element_type=F32)
            bui[:, sl] = -lax.dot_general(xb, w_im[gb], dn, preferred_element_type=F32)
        else:
            bur[:, sl] = jnp.dot(xb, w_re[gb], preferred_element_type=F32)
            bui[:, sl] = jnp.dot(xb, w_im[gb], preferred_element_type=F32)


def s5_scan(act, w_re, w_im, a, init, *, reverse, adjoint=False, c_re=None, c_im=None, add=None,
            want_ckpt=False, rows=None, name):
    act_off, N = rows if rows is not None else (0, act.shape[0])
    R = math.gcd(ROW_TILE, N, act_off)
    nch, jc = N // R, R // N_SEG
    with_out = c_re is not None

    def chunk(i):
        return (nch - 1 - i) if reverse else i

    def body(*refs):
        act_ref, wre_ref, wim_ref, a_ref, init_ref = refs[:5]
        k = 5
        if with_out:
            cre_ref, cim_ref = refs[k:k + 2]
            k += 2
        if add is not None:
            add_ref = refs[k]
            k += 1
        if with_out:
            out_ref = refs[k]
            k += 1
        if want_ckpt:
            ck_ref = refs[k]
            k += 1
        fin_ref, bur, bui = refs[k:k + 3]

        @pl.when(pl.program_id(0) == 0)
        def _():
            fin_ref[...] = init_ref[...]

        if want_ckpt:
            ck_ref[0] = fin_ref[...]
        _project_in(act_ref[...].astype(BF16), wre_ref, wim_ref, bur, bui, adjoint)
        _scan_chunk(bur, bui, fin_ref, a_ref, jc, reverse)
        if with_out:
            for gb in range(N_BLOCKS):
                sl = slice(gb * BLK_ST, (gb + 1) * BLK_ST)
                y = (jnp.dot(bur[:, sl].astype(BF16), cre_ref[gb], preferred_element_type=F32)
                     - jnp.dot(bui[:, sl].astype(BF16), cim_ref[gb], preferred_element_type=F32))
                cs = slice(gb * BLK_CH, (gb + 1) * BLK_CH)
                if add is not None:
                    y = y + add_ref[:, cs]
                out_ref[:, cs] = y

    row_spec = pl.BlockSpec((R, D_MODEL), lambda i: (chunk(i), 0))
    act_spec = pl.BlockSpec((R, D_MODEL), lambda i: (chunk(i) + act_off // R, 0))
    w_spec = pl.BlockSpec(w_re.shape, lambda i: (0, 0, 0))
    st_spec = pl.BlockSpec((2, N_SEG, S5_LANES), lambda i: (0, 0, 0))
    ins = [act, w_re, w_im, a, init]
    in_specs = [act_spec, w_spec, w_spec, pl.BlockSpec((2, 1, S5_LANES), lambda i: (0, 0, 0)), st_spec]
    if with_out:
        ins += [c_re, c_im]
        in_specs += [pl.BlockSpec(c_re.shape, lambda i: (0, 0, 0))] * 2
    if add is not None:
        ins.append(add)
        in_specs.append(row_spec)
    out_shape, out_specs = [], []
    if with_out:
        out_shape.append(jax.ShapeDtypeStruct((N, D_MODEL), F32))
        out_specs.append(row_spec)
    if want_ckpt:
        out_shape.append(jax.ShapeDtypeStruct((nch, 2, N_SEG, S5_LANES), F32))
        out_specs.append(pl.BlockSpec((1, 2, N_SEG, S5_LANES), lambda i: (chunk(i), 0, 0, 0)))
    out_shape.append(jax.ShapeDtypeStruct((2, N_SEG, S5_LANES), F32))
    out_specs.append(st_spec)
    res = pl.pallas_call(
        body, out_shape=out_shape, grid=(nch,), in_specs=in_specs, out_specs=out_specs,
        scratch_shapes=[pltpu.VMEM((R, S5_LANES), F32), pltpu.VMEM((R, S5_LANES), F32)],
        compiler_params=_params(("arbitrary",), VMEM_LIMIT), name=name)(*ins)
    res = list(res)
    out = res.pop(0) if with_out else None
    ckpt = res.pop(0) if want_ckpt else None
    return out, ckpt, res[0]


def s5_grads(dy, u, ckpt, b_re, b_im, c_re, c_im, lam, init_adj, *, reverse, add=None, u_off=0, name):
    N = dy.shape[0]
    R = math.gcd(ROW_TILE, N, u_off)
    nch, jc = N // R, R // N_SEG
    W = S5_LANES

    def chunk(i):
        return i if reverse else (nch - 1 - i)

    def body(*refs):
        dy_ref, u_ref, ck_ref, bre_ref, bim_ref, cre_ref, cim_ref, lam_ref, init_ref = refs[:9]
        k = 9
        if add is not None:
            add_ref = refs[k]
            k += 1
        du_ref, dlam_ref, dbre_ref, dbim_ref, dcre_ref, dcim_ref, fin_ref = refs[k:k + 7]
        sr_buf, si_buf, er_buf, ei_buf, st_buf = refs[k + 7:k + 12]

        @pl.when(pl.program_id(0) == 0)
        def _():
            fin_ref[...] = init_ref[...]
            dlam_ref[...] = jnp.zeros_like(dlam_ref)
            dbre_ref[...] = jnp.zeros_like(dbre_ref)
            dbim_ref[...] = jnp.zeros_like(dbim_ref)
            dcre_ref[...] = jnp.zeros_like(dcre_ref)
            dcim_ref[...] = jnp.zeros_like(dcim_ref)

        u16 = u_ref[...].astype(BF16)
        dy16 = dy_ref[...].astype(BF16)
        st_buf[...] = ck_ref[0]
        _project_in(u16, bre_ref, bim_ref, sr_buf, si_buf, False)
        _scan_chunk(sr_buf, si_buf, st_buf, lam_ref, jc, reverse)
        _project_in(dy16, cre_ref, cim_ref, er_buf, ei_buf, True)
        for lc in range(W // BLK_ST):
            sl = slice(lc * BLK_ST, (lc + 1) * BLK_ST)
            lr = jnp.broadcast_to(lam_ref[0, :, sl], (N_SEG, BLK_ST))
            li = jnp.broadcast_to(lam_ref[1, :, sl], (N_SEG, BLK_ST))

            def one(r0, spr, spi, carry, sl=sl, lr=lr, li=li):
                gr, gi, ar, ai = carry
                nr = er_buf[pl.ds(r0, N_SEG), sl] + lr * gr + li * gi
                ni = ei_buf[pl.ds(r0, N_SEG), sl] + lr * gi - li * gr
                er_buf[pl.ds(r0, N_SEG), sl] = nr
                ei_buf[pl.ds(r0, N_SEG), sl] = ni
                return nr, ni, ar + spr * nr + spi * ni, ai + spr * ni - spi * nr

            def step(ff, carry, sl=sl, one=one):
                f = jc - 1 - ff
                j = (jc - 1 - f) if reverse else f
                jp = (j + 1) if reverse else (j - 1)
                r0 = pl.multiple_of(j * N_SEG, N_SEG)
                p0 = pl.multiple_of(jp * N_SEG, N_SEG)
                return one(r0, sr_buf[pl.ds(p0, N_SEG), sl], si_buf[pl.ds(p0, N_SEG), sl], carry)

            carry = (fin_ref[0, :, sl], fin_ref[1, :, sl], dlam_ref[0, :, sl], dlam_ref[1, :, sl])
            carry = lax.fori_loop(0, jc - 1, step, carry)
            r_first = (jc - 1) * N_SEG if reverse else 0
            gr, gi, ar, ai = one(r_first, ck_ref[0, 0, :, sl], ck_ref[0, 1, :, sl], carry)
            fin_ref[0, :, sl] = gr
            fin_ref[1, :, sl] = gi
            dlam_ref[0, :, sl] = ar
            dlam_ref[1, :, sl] = ai
        tn = (((0,), (0,)), ((), ()))
        nt = (((1,), (1,)), ((), ()))
        for gb in range(N_BLOCKS):
            sl = slice(gb * BLK_ST, (gb + 1) * BLK_ST)
            cs = slice(gb * BLK_CH, (gb + 1) * BLK_CH)
            gr16 = er_buf[:, sl].astype(BF16)
            gi16 = ei_buf[:, sl].astype(BF16)
            du = (lax.dot_general(gr16, bre_ref[gb], nt, preferred_element_type=F32)
                  + lax.dot_general(gi16, bim_ref[gb], nt, preferred_element_type=F32))
            if add is not None:
                du = du + add_ref[:, cs]
            du_ref[:, cs] = du
            ub, dyb = u16[:, cs], dy16[:, cs]
            dbre_ref[gb] += lax.dot_general(ub, gr16, tn, preferred_element_type=F32)
            dbim_ref[gb] += lax.dot_general(ub, gi16, tn, preferred_element_type=F32)
            dcre_ref[gb] += lax.dot_general(sr_buf[:, sl].astype(BF16), dyb, tn, preferred_element_type=F32)
            dcim_ref[gb] -= lax.dot_general(si_buf[:, sl].astype(BF16), dyb, tn, preferred_element_type=F32)

    row_spec = pl.BlockSpec((R, D_MODEL), lambda i: (chunk(i), 0))
    st_spec = pl.BlockSpec((2, N_SEG, W), lambda i: (0, 0, 0))
    wb_spec = pl.BlockSpec(b_re.shape, lambda i: (0, 0, 0))
    wc_spec = pl.BlockSpec(c_re.shape, lambda i: (0, 0, 0))
    ins = [dy, u, ckpt, b_re, b_im, c_re, c_im, lam, init_adj]
    u_spec = pl.BlockSpec((R, D_MODEL), lambda i: (chunk(i) + u_off // R, 0))
    in_specs = [row_spec, u_spec, pl.BlockSpec((1, 2, N_SEG, W), lambda i: (chunk(i), 0, 0, 0)),
                wb_spec, wb_spec, wc_spec, wc_spec, pl.BlockSpec((2, 1, W), lambda i: (0, 0, 0)), st_spec]
    if add is not None:
        ins.append(add)
        in_specs.append(row_spec)
    out_shape = [jax.ShapeDtypeStruct((N, D_MODEL), F32), jax.ShapeDtypeStruct((2, N_SEG, W), F32),
                 jax.ShapeDtypeStruct(b_re.shape, F32), jax.ShapeDtypeStruct(b_re.shape, F32),
                 jax.ShapeDtypeStruct(c_re.shape, F32), jax.ShapeDtypeStruct(c_re.shape, F32),
                 jax.ShapeDtypeStruct((2, N_SEG, W), F32)]
    out_specs = [row_spec, st_spec, wb_spec, wb_spec, wc_spec, wc_spec, st_spec]
    return pl.pallas_call(
        body, out_shape=out_shape, grid=(nch,), in_specs=in_specs, out_specs=out_specs,
        scratch_shapes=[pltpu.VMEM((R, W), F32) for _ in range(4)] + [pltpu.VMEM((2, N_SEG, W), F32)],
        compiler_params=_params(("arbitrary",), VMEM_LIMIT), name=name)(*ins)


def adamw(w, g, m, v, name="adamw"):
    n, d = w.shape
    lanes = -(-d // 128) * 128
    tm = n
    while tm * lanes * 4 > (1 << 20) and tm % 16 == 0:
        tm //= 2
    c1 = 1.0 - ADAM_B1 ** ADAM_STEP
    c2 = 1.0 - ADAM_B2 ** ADAM_STEP

    def body(w_ref, g_ref, m_ref, v_ref, d_ref, nm_ref, nv_ref):
        g_ = g_ref[...]
        m_ = ADAM_B1 * m_ref[...] + (1.0 - ADAM_B1) * g_
        v_ = ADAM_B2 * v_ref[...] + (1.0 - ADAM_B2) * (g_ * g_)
        d_ref[...] = -ADAM_LR * ((m_ / c1) / (jnp.sqrt(v_ / c2) + ADAM_EPS) + ADAM_WD * w_ref[...])
        nm_ref[...] = m_
        nv_ref[...] = v_

    spec = pl.BlockSpec((tm, d), lambda i: (i, 0))
    return pl.pallas_call(
        body, out_shape=[jax.ShapeDtypeStruct((n, d), F32)] * 3, grid=(n // tm,),
        in_specs=[spec] * 4, out_specs=[spec] * 3,
        compiler_params=_params(("parallel",), VMEM_LIMIT), name=name)(w, g, m, v)


def _coords():
    return lax.axis_index("x"), lax.axis_index("y"), lax.axis_index("c")


def exchange(arrays, out_shapes, remote, local, name, aliases=None):
    n_in, n_out, n_rem, n_loc = len(arrays), len(out_shapes), len(remote), len(local)

    def at(ref, idx):
        return ref if idx is None else ref.at[idx]

    def body(*refs):
        ins, outs = refs[:n_in], refs[n_in:n_in + n_out]
        send_sems, recv_sems, local_sems = refs[n_in + n_out:]
        me = _coords()
        sends, recvs = [], []
        for k, (flip, ii, src_at, oi, dst_at) in enumerate(remote):
            peer = (me[0] ^ flip[0], me[1] ^ flip[1], me[2] ^ flip[2])
            src = at(ins[ii], src_at(me, peer))
            sends.append(pltpu.make_async_remote_copy(
                src_ref=src, dst_ref=at(outs[oi], dst_at(me)), send_sem=send_sems.at[k], recv_sem=recv_sems.at[k],
                device_id=peer, device_id_type=MESH))
            recvs.append(pltpu.make_async_remote_copy(
                src_ref=src, dst_ref=at(outs[oi], dst_at(peer)), send_sem=send_sems.at[k], recv_sem=recv_sems.at[k],
                device_id=peer, device_id_type=MESH))
        locs = [pltpu.make_async_copy(at(ins[ii], src_at(me)), at(outs[oi], dst_at(me)), local_sems.at[k])
                for k, (ii, src_at, oi, dst_at) in enumerate(local)]
        for cp in locs + sends:
            cp.start()
        for cp in recvs:
            cp.wait_recv()
        for cp in sends:
            cp.wait_send()
        for cp in locs:
            cp.wait()

    hbm = pl.BlockSpec(memory_space=pl.ANY)
    return pl.pallas_call(
        body, out_shape=list(out_shapes), in_specs=[hbm] * n_in, out_specs=[hbm] * n_out,
        scratch_shapes=[pltpu.SemaphoreType.DMA((n_rem,)), pltpu.SemaphoreType.DMA((n_rem,)),
                        pltpu.SemaphoreType.DMA((max(n_loc, 1),))],
        input_output_aliases=aliases or {}, name=name)(*arrays)


ALL_FLIPS = [(dx, dy, dc) for dx in (0, 1) for dy in (0, 1) for dc in (0, 1)][1:]
CHIP_FLIPS = [(1, 0, 0), (0, 1, 0), (1, 1, 0)]
CORE_FLIP = (0, 0, 1)


def _dev_index(p):
    return 4 * p[0] + 2 * p[1] + p[2]


def _chip_index(p):
    return 2 * p[0] + p[1]


def _gather(xs, flips, index, n, name):
    arrays = [x[None] for x in xs]
    outs = [jax.ShapeDtypeStruct((n,) + x.shape, x.dtype) for x in xs]
    remote = [(f, a, lambda me, peer: (0,), a, lambda s: (index(s),)) for a in range(len(xs)) for f in flips]
    local = [(a, lambda me: (0,), a, lambda me: (index(me),)) for a in range(len(xs))]
    return exchange(arrays, outs, remote, local, name)


def allgather_devices(x, name):
    return _gather([x], ALL_FLIPS, _dev_index, N_DEV, name)[0]


def allgather_chips(xs, name):
    return _gather(xs, CHIP_FLIPS, _chip_index, N_CHIP, name)


def gather_halves(xs, name):
    n = len(xs)
    nk = n * len(CHIP_FLIPS)

    def body(*refs):
        ins, outs = refs[:n], refs[n:2 * n]
        ici_send, ici_recv, d2d_send, d2d_recv = refs[2 * n:]
        me = _coords()
        sibling = (me[0], me[1], 1 - me[2])
        first, passed, landed = [], [], []
        for a in range(n):
            half = ins[a].shape[0] // 2
            mine = ins[a].at[pl.ds(pl.multiple_of(me[2] * half, 16), half)]
            for j, flip in enumerate(CHIP_FLIPS):
                k = a * len(CHIP_FLIPS) + j
                peer = (me[0] ^ flip[0], me[1] ^ flip[1], me[2])
                first.append(pltpu.make_async_remote_copy(
                    src_ref=mine, dst_ref=outs[a].at[_chip_index(me), me[2]], send_sem=ici_send.at[k],
                    recv_sem=ici_recv.at[k], device_id=peer, device_id_type=MESH))
                arrived = outs[a].at[_chip_index(peer), me[2]]
                landed.append(pltpu.make_async_remote_copy(
                    src_ref=mine, dst_ref=arrived, send_sem=ici_send.at[k], recv_sem=ici_recv.at[k],
                    device_id=peer, device_id_type=MESH))
                passed.append(pltpu.make_async_remote_copy(
                    src_ref=arrived, dst_ref=arrived, send_sem=d2d_send.at[k], recv_sem=d2d_recv.at[k],
                    device_id=sibling, device_id_type=MESH))
        for cp in first:
            cp.start()
        for k in range(nk):
            landed[k].wait_recv()
            passed[k].start()
        for a in range(n):
            for j, flip in enumerate(CHIP_FLIPS):
                k = a * len(CHIP_FLIPS) + j
                peer_chip = _chip_index((me[0] ^ flip[0], me[1] ^ flip[1]))
                from_sibling = outs[a].at[peer_chip, 1 - me[2]]
                pltpu.make_async_remote_copy(
                    src_ref=from_sibling, dst_ref=from_sibling, send_sem=d2d_send.at[k], recv_sem=d2d_recv.at[k],
                    device_id=sibling, device_id_type=MESH).wait_recv()
        for cp in first + passed:
            cp.wait_send()

    hbm = pl.BlockSpec(memory_space=pl.ANY)
    return pl.pallas_call(
        body, out_shape=[jax.ShapeDtypeStruct((N_CHIP, 2, x.shape[0] // 2, x.shape[1]), x.dtype) for x in xs],
        in_specs=[hbm] * n, out_specs=[hbm] * n,
        scratch_shapes=[pltpu.SemaphoreType.DMA((nk,)) for _ in range(4)], name=name)(*xs)


def _half_tile(h, cd):
    return h if h * cd * 4 <= (1 << 20) else math.gcd(512, h)


def pair_add(g, got, core, out_dtype, name):
    _, _, h, cd = g.shape
    th = _half_tile(h, cd)

    def body(c_ref, g_ref, got_ref, o_ref):
        o_ref[0] = (g_ref[0, 0] + got_ref[0]).astype(o_ref.dtype)

    return pl.pallas_call(
        body, out_shape=jax.ShapeDtypeStruct((N_CHIP, h, cd), out_dtype),
        grid_spec=pltpu.PrefetchScalarGridSpec(
            num_scalar_prefetch=1, grid=(N_CHIP, h // th),
            in_specs=[pl.BlockSpec((1, 1, th, cd), lambda q, i, c: (q, c[0], i, 0)),
                      pl.BlockSpec((1, th, cd), lambda q, i, c: (q, i, 0))],
            out_specs=pl.BlockSpec((1, th, cd), lambda q, i, c: (q, i, 0))),
        compiler_params=_params(("parallel", "parallel"), VMEM_LIMIT), name=name)(core, g, got)


def sum_chips(parts, core, name):
    _, h, cd = parts.shape
    th = _half_tile(h, cd)

    def body(c_ref, p_ref, o_ref):
        acc = p_ref[0].astype(F32)
        for q in range(1, N_CHIP):
            acc = acc + p_ref[q].astype(F32)
        o_ref[0] = acc

    return pl.pallas_call(
        body, out_shape=jax.ShapeDtypeStruct((2, h, cd), F32),
        grid_spec=pltpu.PrefetchScalarGridSpec(
            num_scalar_prefetch=1, grid=(h // th,),
            in_specs=[pl.BlockSpec((N_CHIP, th, cd), lambda i, c: (0, i, 0))],
            out_specs=pl.BlockSpec((1, th, cd), lambda i, c: (c[0], i, 0))),
        compiler_params=_params(("parallel",), VMEM_LIMIT), name=name)(core, parts)


def to_segments(a, n_ctx):
    def one(p):
        n = p.shape[0]
        return p.reshape(N_SEG, n // N_SEG, -1).transpose(1, 0, 2).reshape(n, -1)
    return jnp.concatenate([one(a[:n_ctx]), one(a[n_ctx:])], axis=0) if n_ctx else one(a)


def from_segments(a, n_ctx):
    def one(p):
        n = p.shape[0]
        return p.reshape(n // N_SEG, N_SEG, -1).transpose(1, 0, 2).reshape(n, -1)
    return jnp.concatenate([one(a[:n_ctx]), one(a[n_ctx:])], axis=0) if n_ctx else one(a)


def rope_tables(n_ctx, n_lat):
    f32 = np.float32
    rows = n_lat // GRID_W
    row = np.repeat(np.arange(rows), GRID_W).astype(f32)
    col = np.tile(np.arange(GRID_W), rows).astype(f32)
    d = QK_ROPE_DIM // 2
    inv = (f32(1.0) / np.power(f32(ROPE_THETA), np.arange(0, d, 2, dtype=f32) / f32(d))).astype(f32)
    ang = np.concatenate([row[:, None] * inv[None, :], col[:, None] * inv[None, :]], axis=1).astype(f32)
    cos = np.concatenate([np.ones((n_ctx, d), f32), np.cos(ang)], axis=0)
    sin = np.concatenate([np.zeros((n_ctx, d), f32), np.sin(ang)], axis=0)
    q = QK_ROPE_DIM // 4
    T = n_ctx + n_lat
    ones, zeros = np.ones((T, QK_NOPE_DIM), f32), np.zeros((T, QK_NOPE_DIM), f32)
    tail, z8 = np.zeros((T, HEAD_LANES - QK_DIM), f32), np.zeros((T, q), f32)
    cr, cc, sr, sc = cos[:, :q], cos[:, q:], sin[:, :q], sin[:, q:]
    cos_t = np.concatenate([ones, cr, cr, cc, cc, tail], axis=1)
    sin_next = np.concatenate([zeros, -sr, z8, -sc, z8, tail], axis=1)
    sin_prev = np.concatenate([zeros, z8, sr, z8, sc, tail], axis=1)
    return tuple(jnp.asarray(t, F32) for t in (cos_t, sin_next, sin_prev))


def pad_heads(w, used):
    k = w.shape[0]
    return jnp.pad(w.reshape(k, MLA_HEADS, used), ((0, 0), (0, 0), (0, HEAD_LANES - used))).reshape(k, -1)


def unpad_heads(w, used):
    k = w.shape[0]
    return w.reshape(k, MLA_HEADS, HEAD_LANES)[:, :, :used].reshape(k, MLA_HEADS * used)


def rotary_spread():
    lane = np.arange(MLA_HEADS * HEAD_LANES) % HEAD_LANES
    return jnp.asarray(lane[None, :] == (QK_NOPE_DIM + np.arange(QK_ROPE_DIM))[:, None], BF16)


def s5_discretise(a_re, a_im, log_step, b_re, b_im):
    dt = jnp.exp(log_step)[:, None]
    mag = jnp.exp(a_re * dt)
    lb_re = mag * jnp.cos(a_im * dt)
    lb_im = mag * jnp.sin(a_im * dt)
    den = a_re * a_re + a_im * a_im
    nr = lb_re - 1.0
    f_re = ((nr * a_re + lb_im * a_im) / den)[..., None]
    f_im = ((lb_im * a_re - nr * a_im) / den)[..., None]
    return lb_re, lb_im, f_re * b_re - f_im * b_im, f_re * b_im + f_im * b_re


def s5_block_weights(lb_re, lb_im, bb_re, bb_im, c_re, c_im):
    eye = jnp.eye(GROUPS_PER_BLOCK, dtype=F32)
    lam = jnp.stack([lb_re.reshape(1, S5_LANES), lb_im.reshape(1, S5_LANES)])

    def b_blocks(bb):
        t = bb.reshape(N_BLOCKS, GROUPS_PER_BLOCK, S5_STATE, S5_GROUP)
        return jnp.einsum("bgpc,gh->bgchp", t, eye).reshape(N_BLOCKS, BLK_CH, BLK_ST).astype(BF16)

    def c_blocks(cc):
        t = cc.reshape(N_BLOCKS, GROUPS_PER_BLOCK, S5_GROUP, S5_STATE)
        return jnp.einsum("bgcp,gh->bgphc", t, eye).reshape(N_BLOCKS, BLK_ST, BLK_CH).astype(BF16)

    return lam, b_blocks(bb_re), b_blocks(bb_im), c_blocks(c_re), c_blocks(c_im)


def b_block_diag(db):
    t = db.reshape(N_BLOCKS, GROUPS_PER_BLOCK, S5_GROUP, GROUPS_PER_BLOCK, S5_STATE)
    return jnp.einsum("bgchp,gh->bgpc", t, jnp.eye(GROUPS_PER_BLOCK, dtype=F32)).reshape(S5_GROUPS, S5_STATE, S5_GROUP)


def c_block_diag(dc):
    t = dc.reshape(N_BLOCKS, GROUPS_PER_BLOCK, S5_STATE, GROUPS_PER_BLOCK, S5_GROUP)
    return jnp.einsum("bgphc,gh->bgcp", t, jnp.eye(GROUPS_PER_BLOCK, dtype=F32)).reshape(S5_GROUPS, S5_GROUP, S5_STATE)


def conj(a):
    return jnp.stack([a[0], -a[1]])


PACK_TILE = 16 * 128


def pack_flat(parts, dtype):
    flat = [p.reshape(-1).astype(dtype) for p in parts]
    sizes = [f.shape[0] for f in flat]
    total = sum(sizes)
    pad = (-total) % PACK_TILE
    if pad:
        flat.append(jnp.zeros((pad,), dtype))
    offs = np.cumsum([0] + sizes)[:-1].tolist()
    return jnp.concatenate(flat).reshape(-1, 128), offs


def unpack_flat(buf, offs, shapes):
    flat = buf.reshape(-1)
    return [flat[o:o + int(np.prod(s))].reshape(s) for o, s in zip(offs, shapes)]


def s5_forward(p1, n_ctx, dirs):
    saved = []
    y = None
    ctx_rows, lat_rows = (0, n_ctx), (n_ctx, p1.shape[0] - n_ctx)
    zeros_tile = jnp.zeros((2, N_SEG, S5_LANES), F32)
    zeros_row = jnp.zeros((2, 1, S5_LANES), F32)
    for k, (lam, b_re, b_im, c_re, c_im) in enumerate(dirs):
        rev = k == 1
        last = 0 if rev else N_SEG - 1
        _, _, fin = s5_scan(p1, b_re, b_im, lam, zeros_tile, reverse=rev, rows=ctx_rows, name=f"s5_ctx_finals{k}")
        carry_c = s5_chain(fin, zeros_row, lam, n_ctx // N_SEG, rev, name=f"s5_ctx_chain{k}")
        _, ck_c, fin_c = s5_scan(p1, b_re, b_im, lam, carry_c, reverse=rev, want_ckpt=True, rows=ctx_rows,
                                 name=f"s5_ctx_scan{k}")
        s0 = fin_c[:, last:last + 1, :]
        _, _, fin = s5_scan(p1, b_re, b_im, lam, zeros_tile, reverse=rev, rows=lat_rows, name=f"s5_lat_finals{k}")
        carry_l = s5_chain(fin, s0, lam, lat_rows[1] // N_SEG, rev, name=f"s5_lat_chain{k}")
        y, ck_l, _ = s5_scan(p1, b_re, b_im, lam, carry_l, reverse=rev, c_re=c_re, c_im=c_im, add=y,
                             want_ckpt=True, rows=lat_rows, name=f"s5_lat_scan{k}")
        saved.append((ck_c, ck_l))
    return y, saved


def s5_backward(dy_l, du_extra_l, p1, n_ctx, dirs, saved):
    n_lat = p1.shape[0] - n_ctx
    zeros_tile = jnp.zeros((2, N_SEG, S5_LANES), F32)
    zeros_row = jnp.zeros((2, 1, S5_LANES), F32)
    dy_c = jnp.zeros((n_ctx, D_MODEL), F32)
    du_l, du_c = du_extra_l, None
    grads = []
    for k, (lam, b_re, b_im, c_re, c_im) in enumerate(dirs):
        rev = k == 1
        lam_c = conj(lam)
        ck_c, ck_l = saved[k]
        first = N_SEG - 1 if rev else 0
        _, _, fin = s5_scan(dy_l, c_re, c_im, lam_c, zeros_tile, reverse=not rev, adjoint=True,
                            name=f"s5_lat_adj_finals{k}")
        carry = s5_chain(fin, zeros_row, lam_c, n_lat // N_SEG, not rev, name=f"s5_lat_adj_chain{k}")
        du_l, dlam_l, dbr_l, dbi_l, dcr_l, dci_l, fin_a = s5_grads(
            dy_l, p1, ck_l, b_re, b_im, c_re, c_im, lam, carry, reverse=rev, add=du_l, u_off=n_ctx,
            name=f"s5_lat_grads{k}")
        g0 = fin_a[:, first:first + 1, :]
        carry = s5_chain(zeros_tile, g0, lam_c, n_ctx // N_SEG, not rev, name=f"s5_ctx_adj_chain{k}")
        du_c, dlam_c, dbr_c, dbi_c, _, _, _ = s5_grads(
            dy_c, p1, ck_c, b_re, b_im, c_re, c_im, lam, carry, reverse=rev, add=du_c, name=f"s5_ctx_grads{k}")
        dlam = jnp.sum(dlam_l + dlam_c, axis=1)
        grads.append((dlam, b_block_diag(dbr_l + dbr_c), b_block_diag(dbi_l + dbi_c),
                      c_block_diag(dcr_l), c_block_diag(dci_l)))
    return jnp.concatenate([du_c, du_l], axis=0), grads


def local_step(x, ctx, target, mod, w):
    L, Lc = x.shape[0], ctx.shape[0]
    T = L + Lc
    assert L % Lc == 0 and Lc % (2 * N_SEG) == 0 and L % GRID_W == 0
    D = D_MODEL
    X0 = jnp.concatenate([ctx, x], axis=0)

    def mod_of(i, j):
        return mod[i, :, j, :][:, None, :]

    def vec(v):
        return v.reshape(1, 1, -1).astype(F32)

    g0 = vec(w["norm_g"][0])
    (H0,) = rowwise_fwd(f_norm_mod, [X0], [g0, mod_of(0, 1), mod_of(0, 0)], [D], [BF16], T, Lc, "l0_norm")
    p0 = mm_nn(H0, w["mla_w_in"], name="l0_in")
    z0 = Rows(p0, D, col_blk=0)
    cq = Rows(p0, Q_LORA_RANK, col_blk=D // Q_LORA_RANK)
    ckv = Rows(p0, KV_LORA_RANK, col_blk=(D + Q_LORA_RANK) // KV_LORA_RANK)
    kr = p0[:, D + Q_LORA_RANK + KV_LORA_RANK:]
    qng, kvng = vec(w["mla_q_norm"]), vec(w["mla_kv_norm"])
    (qn,) = rowwise_fwd(f_rms, [cq], [qng], [Q_LORA_RANK], [BF16], T, 0, "l0_qnorm")
    (kvn,) = rowwise_fwd(f_rms, [ckv], [kvng], [KV_LORA_RANK], [BF16], T, 0, "l0_kvnorm")
    tabs = rope_tables(Lc, L)
    spread = rotary_spread()
    w_uq_p = pad_heads(w["mla_w_uq"], QK_DIM)
    w_ukv3 = w["mla_w_ukv"].reshape(KV_LORA_RANK, MLA_HEADS, QK_NOPE_DIM + V_HEAD_DIM)
    w_kn_p = pad_heads(w_ukv3[:, :, :QK_NOPE_DIM].reshape(KV_LORA_RANK, -1), QK_NOPE_DIM)
    w_v = w_ukv3[:, :, QK_NOPE_DIM:].reshape(KV_LORA_RANK, -1)
    qb = q_heads(qn, w_uq_p, tabs)
    kb, vb = kv_heads(kvn, w_kn_p, w_v, kr, spread, tabs)
    o, lse = attn_fwd(qb, kb, vb, Lc)
    (og,) = rowwise_fwd(f_gate, [o, z0], [], [D], [BF16], T, 0, "l0_gate")
    out0 = mm_nn(og, w["mla_w_out"], name="l0_out")
    (X1,) = rowwise_fwd(f_res, [X0, out0], [mod_of(0, 2)], [D], [F32], T, Lc, "l0_res")

    X1p = to_segments(X1, Lc)
    tgt_p = to_segments(target, 0)
    g1 = vec(w["norm_g"][1])
    (H1,) = rowwise_fwd(f_norm_mod, [X1p], [g1, mod_of(1, 1), mod_of(1, 0)], [D], [BF16], T, Lc, "l1_norm")
    p1 = mm_nn(H1, w["s5_w_in"], name="l1_in")
    disc_fn = lambda *a: tuple(zip(*[s5_discretise(a[0][k], a[1][k], a[2][k], a[3][k], a[4][k]) for k in range(2)]))
    disc, disc_vjp = jax.vjp(disc_fn, w["s5_a_re"], w["s5_a_im"], w["s5_log_step"], w["s5_b_re"], w["s5_b_im"])
    dirs = [s5_block_weights(disc[0][k], disc[1][k], disc[2][k], disc[3][k], w["s5_c_re"][k], w["s5_c_im"][k])
            for k in range(2)]
    y_ssm, s5_saved = s5_forward(p1, Lc, dirs)
    d_vec, bg_vec = vec(w["s5_d"]), vec(w["s5_b_glu"])
    u_lat = Rows(p1, D, row_off=Lc, col_blk=0)
    z_lat = Rows(p1, D, row_off=Lc, col_blk=1)
    (ya,) = rowwise_fwd(f_s5_act, [y_ssm, u_lat], [d_vec], [D], [F32], L, 0, "l1_act")
    gl = mm_nn(ya, w["s5_w_glu"], name="l1_glu")
    (y3,) = rowwise_fwd(f_s5_glu, [ya, gl, z_lat], [bg_vec], [D], [BF16], L, 0, "l1_gate")
    out1 = mm_nn(y3, w["s5_w_out"], name="l1_out")
    gt1 = mod[1, 1:2, 2, :][:, None, :]
    x1_lat = Rows(X1p, D, row_off=Lc)
    (X2,) = rowwise_fwd(f_res, [x1_lat, out1], [gt1], [D], [F32], L, 0, "l1_res")

    fg = w["final_g"].reshape(1, D).astype(F32)
    lvec, dX2, d_fg = loss_and_grad(X2, fg, tgt_p)
    loss = jnp.sum(lvec)
    gw = {"final_g": d_fg.reshape(D)}
    dmod = {}

    d_out1, d_gt1 = rowwise_bwd(f_res, [x1_lat, out1], [gt1], [dX2], [1], [0], L, 0, "l1_res_bwd")
    d_y3 = mm_nt(d_out1, w["s5_w_out"], name="l1_out_dx")
    gw["s5_w_out"] = mm_tn(y3, d_out1, name="l1_out_dw")
    d_ya_a, d_gl, d_z1, d_bg = rowwise_bwd(f_s5_glu, [ya, gl, z_lat], [bg_vec], [d_y3], [0, 1, 2], [0], L, 0,
                                           "l1_gate_bwd")
    gw["s5_b_glu"] = d_bg.reshape(-1)
    gw["s5_w_glu"] = mm_tn(ya, d_gl, name="l1_glu_dw")
    d_ya_b = mm_nt(d_gl, w["s5_w_glu"], name="l1_glu_dx")
    d_yssm, d_u_act, d_d = rowwise_bwd(f_s5_act, [y_ssm, u_lat], [d_vec], [d_ya_a], [0, 1], [0], L, 0, "l1_act_bwd",
                                       ct_extra=d_ya_b)
    gw["s5_d"] = d_d.reshape(-1)
    du_p, s5_g = s5_backward(d_yssm, d_u_act, p1, Lc, dirs, s5_saved)
    d_disc = tuple(tuple(s5_g[k][j - 1].reshape(disc[j][k].shape) if j >= 2 else
                         s5_g[k][0][j].reshape(disc[j][k].shape) for k in range(2)) for j in range(4))
    gw["s5_a_re"], gw["s5_a_im"], gw["s5_log_step"], gw["s5_b_re"], gw["s5_b_im"] = disc_vjp(d_disc)
    gw["s5_c_re"] = jnp.stack([s5_g[0][3], s5_g[1][3]])
    gw["s5_c_im"] = jnp.stack([s5_g[0][4], s5_g[1][4]])
    d_p1 = jnp.concatenate([du_p, jnp.concatenate([jnp.zeros((Lc, D), F32), d_z1], axis=0)], axis=1)
    d_H1 = mm_nt(d_p1, w["s5_w_in"], name="l1_in_dx")
    gw["s5_w_in"] = mm_tn(H1, d_p1, name="l1_in_dw")
    d_X1p, d_g1, d_sc1, d_sh1 = rowwise_bwd(f_norm_mod, [X1p], [g1, mod_of(1, 1), mod_of(1, 0)], [d_H1],
                                            [0], [0, 1, 2], T, Lc, "l1_norm_bwd", lat_add=dX2)
    d_gt1_full = jnp.concatenate([jnp.zeros((1, 1, D), F32), d_gt1], axis=0)
    dmod[1] = (d_sh1, d_sc1, d_gt1_full)
    d_X1 = from_segments(d_X1p, Lc)

    d_out0, d_gt0 = rowwise_bwd(f_res, [X0, out0], [mod_of(0, 2)], [d_X1], [1], [0], T, Lc, "l0_res_bwd")
    d_og = mm_nt(d_out0, w["mla_w_out"], name="l0_out_dx")
    gw["mla_w_out"] = mm_tn(og, d_out0, name="l0_out_dw")
    d_o, d_z0 = rowwise_bwd(f_gate, [o, z0], [], [d_og], [0, 1], [], T, 0, "l0_gate_bwd")
    dq_p, delta = attn_bwd_dq(qb, kb, vb, o, d_o, lse, Lc)
    dk_p, d_v = attn_bwd_dkv(qb, kb, vb, d_o, lse.reshape(MLA_HEADS, 1, T), delta.reshape(MLA_HEADS, 1, T), Lc)
    (d_q,) = heads_unrope(dq_p, tabs, name="l0_q_unrope")
    d_k, d_kr = heads_unrope(dk_p, tabs, spread, name="l0_k_unrope")
    d_qn = mm_nt(d_q, w_uq_p, name="l0_uq_dx")
    gw["mla_w_uq"] = unpad_heads(mm_tn(qn, d_q, name="l0_uq_dw"), QK_DIM)
    d_kvn = mm_nt(d_k, w_kn_p, name="l0_ukn_dx") + mm_nt(d_v, w_v, name="l0_uv_dx")
    dw_kn = unpad_heads(mm_tn(kvn, d_k, name="l0_ukn_dw"), QK_NOPE_DIM).reshape(KV_LORA_RANK, MLA_HEADS, QK_NOPE_DIM)
    dw_v = mm_tn(kvn, d_v, name="l0_uv_dw").reshape(KV_LORA_RANK, MLA_HEADS, V_HEAD_DIM)
    gw["mla_w_ukv"] = jnp.concatenate([dw_kn, dw_v], axis=-1).reshape(KV_LORA_RANK, -1)
    d_cq, d_qng = rowwise_bwd(f_rms, [cq], [qng], [d_qn], [0], [0], T, 0, "l0_qnorm_bwd")
    d_ckv, d_kvng = rowwise_bwd(f_rms, [ckv], [kvng], [d_kvn], [0], [0], T, 0, "l0_kvnorm_bwd")
    gw["mla_q_norm"] = d_qng.reshape(-1)
    gw["mla_kv_norm"] = d_kvng.reshape(-1)
    d_p0 = jnp.concatenate([d_z0, d_cq, d_ckv, d_kr], axis=1)
    d_H0 = mm_nt(d_p0, w["mla_w_in"], name="l0_in_dx")
    gw["mla_w_in"] = mm_tn(H0, d_p0, name="l0_in_dw")
    d_X0, d_g0, d_sc0, d_sh0 = rowwise_bwd(f_norm_mod, [X0], [g0, mod_of(0, 1), mod_of(0, 0)], [d_H0],
                                           [0], [0, 1, 2], T, Lc, "l0_norm_bwd", lat_add=d_X1)
    dmod[0] = (d_sh0, d_sc0, d_gt0)
    gw["norm_g"] = jnp.stack([d_g0.reshape(D), d_g1.reshape(D)])
    dx = d_X0[Lc:]
    dmod_arr = jnp.stack([jnp.stack([dmod[i][j][:, 0, :] for j in range(3)], axis=1) for i in range(2)])
    return loss, dx, dmod_arr, gw


SHARDED = {
    "mla_w_in": 1, "mla_w_uq": 1, "mla_w_ukv": 1, "mla_w_out": 0,
    "s5_w_in": 1, "s5_w_glu": 0, "s5_w_out": 0, "s5_d": 0, "s5_b_glu": 0,
}
SHARDED_MATS = ["mla_w_in", "mla_w_uq", "mla_w_ukv", "mla_w_out", "s5_w_in", "s5_w_glu", "s5_w_out"]
SHARDED_VECS = ["s5_d", "s5_b_glu"]
REPLICATED = ["norm_g", "mla_q_norm", "mla_kv_norm", "s5_a_re", "s5_a_im", "s5_log_step", "s5_b_re", "s5_b_im",
              "s5_c_re", "s5_c_im", "final_g"]
WEIGHT_ORDER = ["c_ctx", "ada_w", "ada_b", "norm_g", "mla_w_in", "mla_q_norm", "mla_w_uq", "mla_kv_norm", "mla_w_ukv",
                "mla_w_out", "s5_w_in", "s5_a_re", "s5_a_im", "s5_log_step", "s5_b_re", "s5_b_im", "s5_c_re", "s5_c_im",
                "s5_d", "s5_w_glu", "s5_b_glu", "s5_w_out", "final_g"]


P0_HEAD = Q_LORA_RANK + KV_LORA_RANK + QK_ROPE_DIM


def w_in_to_kernel_order(w):
    return jnp.concatenate([w[:, P0_HEAD:], w[:, :P0_HEAD]], axis=1)


def w_in_from_kernel_order(w):
    return jnp.concatenate([w[:, D_MODEL:], w[:, :D_MODEL]], axis=1)


def gather_weights(ws):
    mats = [ws[n].astype(BF16) for n in SHARDED_MATS]
    outs = gather_halves(mats, "gather_weights")
    chip = _chip_index(_coords())
    full = {}
    for n, own, o in zip(SHARDED_MATS, mats, outs):
        o = lax.dynamic_update_index_in_dim(o.reshape((N_CHIP,) + own.shape), own, chip, 0)
        full[n] = o.reshape(-1, o.shape[-1]) if SHARDED[n] == 0 else o.transpose(1, 0, 2).reshape(o.shape[1], -1)
    full["mla_w_in"] = w_in_to_kernel_order(full["mla_w_in"])
    return full


def reduce_gradients(gw):
    me = _coords()
    core = me[2].reshape(1).astype(jnp.int32)
    slots = {}
    for n in SHARDED_MATS:
        g = w_in_from_kernel_order(gw[n]) if n == "mla_w_in" else gw[n]
        if SHARDED[n] == 0:
            slots[n] = g.reshape(N_CHIP, 2, g.shape[0] // (2 * N_CHIP), g.shape[1])
        else:
            k, n4 = g.shape
            slots[n] = g.reshape(k, N_CHIP, n4 // N_CHIP).transpose(1, 0, 2).reshape(N_CHIP, 2, k // 2, n4 // N_CHIP)
    small_names = REPLICATED + SHARDED_VECS
    small, small_offs = pack_flat([gw[n].astype(F32) for n in small_names], F32)
    small = jnp.pad(small, ((0, (-small.shape[0]) % (N_CHIP * 32)), (0, 0)))
    slots["small"] = small.reshape(N_CHIP, 2, -1, 128)
    names = list(slots)
    idx = range(len(names))
    got = exchange(
        [slots[n] for n in names],
        [jax.ShapeDtypeStruct((N_CHIP,) + slots[n].shape[2:], F32) for n in names],
        [(CORE_FLIP, i, lambda me, peer: (slice(None), 1 - me[2]), i, lambda s: None) for i in idx], [],
        "grads_swap_in")
    sums = [pair_add(slots[n], g, core, F32 if n == "small" else BF16, f"grads_pair_{n}") for n, g in zip(names, got)]
    parts = exchange(
        sums, [jax.ShapeDtypeStruct(s.shape, s.dtype) for s in sums],
        [(f, i, lambda me, peer: (_chip_index(peer),), i, lambda s: (_chip_index(s),)) for i in idx for f in CHIP_FLIPS],
        [(i, lambda me: (_chip_index(me),), i, lambda me: (_chip_index(me),)) for i in idx],
        "grads_scatter")
    halves = [sum_chips(p, core, f"grads_sum_{n}") for n, p in zip(names, parts)]
    fulls = exchange(
        halves, [jax.ShapeDtypeStruct(h.shape, F32) for h in halves],
        [(CORE_FLIP, i, lambda me, peer: (me[2],), i, lambda s: (s[2],)) for i in idx], [],
        "grads_swap_out", aliases={i: i for i in idx})
    out = {n: f.reshape(-1, f.shape[-1]) for n, f in zip(names, fulls)}
    (small_all,) = allgather_chips([out.pop("small")], "grads_gather_small")
    vals = unpack_flat(small_all, small_offs, [gw[n].shape for n in small_names])
    for n, v in zip(small_names, vals):
        if n in SHARDED_VECS:
            size = v.shape[0] // N_CHIP
            v = lax.dynamic_slice_in_dim(v, _chip_index(me) * size, size)
        out[n] = v
    return out


def kernel(x, c, ctx, c_ctx, ada_w, ada_b, norm_g, mla_w_in, mla_q_norm, mla_w_uq, mla_kv_norm, mla_w_ukv, mla_w_out, s5_w_in, s5_a_re, s5_a_im, s5_log_step, s5_b_re, s5_b_im, s5_c_re, s5_c_im, s5_d, s5_w_glu, s5_b_glu, s5_w_out, final_g, loss_target, m_c_ctx, m_ada_w, m_ada_b, m_norm_g, m_mla_w_in, m_mla_q_norm, m_mla_w_uq, m_mla_kv_norm, m_mla_w_ukv, m_mla_w_out, m_s5_w_in, m_s5_a_re, m_s5_a_im, m_s5_log_step, m_s5_b_re, m_s5_b_im, m_s5_c_re, m_s5_c_im, m_s5_d, m_s5_w_glu, m_s5_b_glu, m_s5_w_out, m_final_g, v_c_ctx, v_ada_w, v_ada_b, v_norm_g, v_mla_w_in, v_mla_q_norm, v_mla_w_uq, v_mla_kv_norm, v_mla_w_ukv, v_mla_w_out, v_s5_w_in, v_s5_a_re, v_s5_a_im, v_s5_log_step, v_s5_b_re, v_s5_b_im, v_s5_c_re, v_s5_c_im, v_s5_d, v_s5_w_glu, v_s5_b_glu, v_s5_w_out, v_final_g):
    args = dict(locals())
    weights = {n: args[n] for n in WEIGHT_ORDER}
    D = D_MODEL
    xi, yi, ci = _coords()
    chip = 2 * xi + yi
    me = 4 * xi + 2 * yi + ci
    n_col = ada_w.shape[2]

    c_all = allgather_devices(jnp.pad(c, ((0, 7), (0, 0))), "gather_c")[:, 0, :]
    cond = jnp.concatenate([c_all, jnp.broadcast_to(c_ctx[None], (8, D))], axis=0)
    (s_cond,) = rowwise_fwd(lambda v: (_silu(v),), [cond], [], [D], [F32], 16, 0, "cond_silu")
    mod_cols = jnp.stack([mm_nn(s_cond, ada_w[i], name=f"mod_proj{i}") for i in range(2)])
    vec_tiles = [jnp.pad(weights[n][0].reshape(-1, 128), ((0, 6), (0, 0))) for n in SHARDED_VECS]
    mod_all, *vec_all = allgather_chips([mod_cols] + vec_tiles, "gather_mod")
    mod_all = mod_all.transpose(1, 2, 0, 3).reshape(2, 16, 3 * D) + ada_b[:, None, :]
    mod_l = lax.dynamic_index_in_dim(mod_all, me, axis=1, keepdims=False)
    mod_c = mod_all[:, 8, :]
    mod = jnp.stack([mod_c.reshape(2, 3, D), mod_l.reshape(2, 3, D)], axis=1)

    w = gather_weights({n: weights[n][0] for n in SHARDED_MATS})
    for n, v in zip(SHARDED_VECS, vec_all):
        w[n] = v[:, :2, :].reshape(-1)
    for n in ["norm_g", "final_g"]:
        w[n] = weights[n]
    for n in ["mla_q_norm", "mla_kv_norm", "s5_a_re", "s5_a_im", "s5_log_step", "s5_b_re", "s5_b_im",
              "s5_c_re", "s5_c_im"]:
        w[n] = weights[n][0]

    loss_me, dx, dmod, gw = local_step(x[0], ctx[0], loss_target[0], mod, w)
    loss = lax.psum(loss_me, ("x", "y", "c"))

    dmod_rows = allgather_devices(dmod.reshape(2, 2, 3 * D), "gather_dmod")
    dm = jnp.concatenate([dmod_rows[:, :, 1, :], dmod_rows[:, :, 0, :]], axis=0).transpose(1, 0, 2)
    g_ada_b = jnp.sum(dm, axis=1)
    dm_cols = lax.dynamic_slice_in_dim(dm, chip * n_col, n_col, axis=2)
    g_ada_w = jnp.stack([mm_tn(s_cond, dm_cols[i], name=f"mod_proj_dw{i}") for i in range(2)])
    dmc = jnp.sum(dm_cols[:, 8:, :], axis=1)
    dmc8 = jnp.broadcast_to(dmc[:, None, :], (2, 8, n_col))
    g_sc = mm_nt(dmc8[0], ada_w[0], name="mod_proj_dx0")[0] + mm_nt(dmc8[1], ada_w[1], name="mod_proj_dx1")[0]
    g_sc_all = allgather_devices(jnp.broadcast_to(g_sc[None], (8, D)), "gather_dcond")[:, 0, :]
    g_silu_cc = g_sc_all[0] + g_sc_all[2] + g_sc_all[4] + g_sc_all[6]
    (g_c_ctx,) = rowwise_bwd(lambda v: (_silu(v),), [jnp.broadcast_to(c_ctx[None], (8, D))], [],
                             [jnp.broadcast_to(g_silu_cc[None], (8, D))], [0], [], 8, 0, "cond_silu_bwd")
    g_c_ctx = g_c_ctx[0]

    gw_in = {}
    for n in SHARDED_MATS + SHARDED_VECS:
        gw_in[n] = gw[n]
    for n in REPLICATED:
        gw_in[n] = gw[n]
    red = reduce_gradients(gw_in)
    grads = {"c_ctx": g_c_ctx, "ada_w": g_ada_w, "ada_b": g_ada_b}
    for n in WEIGHT_ORDER[3:]:
        grads[n] = red[n].reshape(weights[n].shape)

    deltas, new_m, new_v = {}, {}, {}
    small = [n for n in WEIGHT_ORDER if weights[n].size < 50000]
    for n in WEIGHT_ORDER:
        if n in small:
            continue
        shp = weights[n].shape
        w2 = weights[n].reshape(-1, shp[-1])
        d_, m_, v_ = adamw(w2, grads[n].reshape(w2.shape), args["m_" + n].reshape(w2.shape),
                           args["v_" + n].reshape(w2.shape), name=f"adamw_{n}")
        deltas[n], new_m[n], new_v[n] = d_.reshape(shp), m_.reshape(shp), v_.reshape(shp)
    packs = []
    offs = None
    for src in (weights, grads, {n: args["m_" + n] for n in small}, {n: args["v_" + n] for n in small}):
        buf, offs = pack_flat([src[n] for n in small], F32)
        packs.append(buf)
    outs = adamw(*packs, name="adamw_small")
    for res, dst in zip(outs, (deltas, new_m, new_v)):
        for n, val in zip(small, unpack_flat(res, offs, [weights[n].shape for n in small])):
            dst[n] = val

    return (loss, dx[None], *[grads[n] for n in WEIGHT_ORDER], *[deltas[n] for n in WEIGHT_ORDER],
            *[new_m[n] for n in WEIGHT_ORDER], *[new_v[n] for n in WEIGHT_ORDER])
```

```python
import functools
import math

import jax
import jax.numpy as jnp
import numpy as np
from jax import lax
from jax.experimental import pallas as pl
from jax.experimental.pallas import tpu as pltpu

F32 = jnp.float32
BF16 = jnp.bfloat16

D_MODEL = 1024
GRID_W = 64
EPS = 1e-6
MLA_HEADS = 16
QK_NOPE_DIM = 64
QK_ROPE_DIM = 32
V_HEAD_DIM = 64
Q_LORA_RANK = 256
KV_LORA_RANK = 128
QK_DIM = QK_NOPE_DIM + QK_ROPE_DIM
SOFTMAX_SCALE = QK_DIM ** -0.5
ROPE_THETA = 10000.0
S5_GROUP = 16
S5_GROUPS = D_MODEL // S5_GROUP
S5_STATE = 64
S5_LANES = S5_GROUPS * S5_STATE
N_SEG = 8
GROUPS_PER_BLOCK = 8
N_BLOCKS = S5_GROUPS // GROUPS_PER_BLOCK
BLK_CH = GROUPS_PER_BLOCK * S5_GROUP
BLK_ST = GROUPS_PER_BLOCK * S5_STATE

ADAM_LR = 0.001
ADAM_B1 = 0.9
ADAM_B2 = 0.999
ADAM_EPS = 1e-08
ADAM_WD = 0.01
ADAM_STEP = 10

N_DEV = 8
N_CHIP = 4
MESH = pl.DeviceIdType.MESH
VMEM_LIMIT = 52 * 1024 * 1024
ROW_TILE = 256


def _params(sem=None, vmem=None):
    return pltpu.CompilerParams(dimension_semantics=sem, vmem_limit_bytes=vmem)


def mm_nn(a, b, out_dtype=F32, name="mm_nn"):
    M, K = a.shape
    N = b.shape[1]
    tm = math.gcd(ROW_TILE, M)

    def body(a_ref, b_ref, o_ref):
        o_ref[...] = jnp.dot(a_ref[...].astype(BF16), b_ref[...].astype(BF16),
                             preferred_element_type=F32).astype(o_ref.dtype)

    return pl.pallas_call(
        body, out_shape=jax.ShapeDtypeStruct((M, N), out_dtype), grid=(M // tm,),
        in_specs=[pl.BlockSpec((tm, K), lambda i: (i, 0)), pl.BlockSpec((K, N), lambda i: (0, 0))],
        out_specs=pl.BlockSpec((tm, N), lambda i: (i, 0)),
        compiler_params=_params(("parallel",), VMEM_LIMIT), name=name)(a, b)


def mm_nt(a, b, out_dtype=F32, name="mm_nt"):
    M, N = a.shape
    K = b.shape[0]
    tm = math.gcd(ROW_TILE, M)

    def body(a_ref, b_ref, o_ref):
        o_ref[...] = lax.dot_general(a_ref[...].astype(BF16), b_ref[...].astype(BF16),
                                     (((1,), (1,)), ((), ())),
                                     preferred_element_type=F32).astype(o_ref.dtype)

    return pl.pallas_call(
        body, out_shape=jax.ShapeDtypeStruct((M, K), out_dtype), grid=(M // tm,),
        in_specs=[pl.BlockSpec((tm, N), lambda i: (i, 0)), pl.BlockSpec((K, N), lambda i: (0, 0))],
        out_specs=pl.BlockSpec((tm, K), lambda i: (i, 0)),
        compiler_params=_params(("parallel",), VMEM_LIMIT), name=name)(a, b)


def mm_tn(a, b, name="mm_tn"):
    M, K = a.shape
    N = b.shape[1]
    tn = math.gcd(512, N) if N % 128 == 0 and N > 512 else N

    def body(a_ref, b_ref, o_ref):
        o_ref[...] = lax.dot_general(a_ref[...].astype(BF16), b_ref[...].astype(BF16),
                                     (((0,), (0,)), ((), ())), preferred_element_type=F32)

    return pl.pallas_call(
        body, out_shape=jax.ShapeDtypeStruct((K, N), F32), grid=(N // tn,),
        in_specs=[pl.BlockSpec((M, K), lambda j: (0, 0)), pl.BlockSpec((M, tn), lambda j: (0, j))],
        out_specs=pl.BlockSpec((K, tn), lambda j: (0, j)),
        compiler_params=_params(("parallel",), VMEM_LIMIT), name=name)(a, b)


class Rows:
    def __init__(self, arr, width=None, row_off=0, col_blk=0):
        self.arr = arr
        self.width = arr.shape[1] if width is None else width
        self.row_off = row_off
        self.col_blk = col_blk

    def spec(self, tm):
        ro, cb = self.row_off // tm, self.col_blk
        return pl.BlockSpec((tm, self.width), lambda i: (i + ro, cb))


def _as_rows(x):
    return x if isinstance(x, Rows) else Rows(x)


def _row_tile(n_rows, n_ctx_rows, rows):
    tm = math.gcd(ROW_TILE, n_rows, n_ctx_rows)
    for r in rows:
        tm = math.gcd(tm, r.row_off)
    return tm


def _bc_spec(arr, n_ctx_blocks):
    g, _, d = arr.shape
    if g == 1:
        return pl.BlockSpec((1, 1, d), lambda i: (0, 0, 0))
    return pl.BlockSpec((1, 1, d), lambda i: ((i >= n_ctx_blocks).astype(jnp.int32), 0, 0))


def rowwise_fwd(fn, rows, bcs, out_dims, out_dtypes, n_rows, n_ctx_rows, name):
    rows = [_as_rows(r) for r in rows]
    tm = _row_tile(n_rows, n_ctx_rows, rows)
    ncb = n_ctx_rows // tm
    nr, nb = len(rows), len(bcs)

    def body(*refs):
        vals = [r[...].astype(F32) for r in refs[:nr]] + [b[0].astype(F32) for b in refs[nr:nr + nb]]
        outs = fn(*vals)
        for o_ref, v in zip(refs[nr + nb:], outs):
            o_ref[...] = v.astype(o_ref.dtype)

    outs = pl.pallas_call(
        body,
        out_shape=[jax.ShapeDtypeStruct((n_rows, d), dt) for d, dt in zip(out_dims, out_dtypes)],
        grid=(n_rows // tm,),
        in_specs=[r.spec(tm) for r in rows] + [_bc_spec(b, ncb) for b in bcs],
        out_specs=[pl.BlockSpec((tm, d), lambda i: (i, 0)) for d in out_dims],
        compiler_params=_params(("parallel",), VMEM_LIMIT), name=name)(*[r.arr for r in rows], *bcs)
    return outs


def rowwise_bwd(fn, rows, bcs, cts, diff_rows, diff_bcs, n_rows, n_ctx_rows, name, ct_extra=None, lat_add=None):
    rows = [_as_rows(r) for r in rows]
    cts = [_as_rows(c) for c in cts]
    extra = [_as_rows(ct_extra)] if ct_extra is not None else []
    tm = _row_tile(n_rows, n_ctx_rows, rows + cts + extra)
    ncb = n_ctx_rows // tm
    nr, nb, nc = len(rows), len(bcs), len(cts)
    ndr, ndb = len(diff_rows), len(diff_bcs)
    n_in = nr + nb + nc + len(extra) + (lat_add is not None)

    def body(*refs):
        i = pl.program_id(0)
        rvals = [r[...].astype(F32) for r in refs[:nr]]
        bvals = [b[0].astype(F32) for b in refs[nr:nr + nb]]
        cvals = [c[...].astype(F32) for c in refs[nr + nb:nr + nb + nc]]
        if extra:
            cvals[0] = cvals[0] + refs[nr + nb + nc][...].astype(F32)
        outs = refs[n_in:]

        def f(*d):
            rv, bv = list(rvals), list(bvals)
            for k, idx in enumerate(diff_rows):
                rv[idx] = d[k]
            for k, idx in enumerate(diff_bcs):
                bv[idx] = d[ndr + k]
            return tuple(fn(*rv, *bv))

        primals = [rvals[k] for k in diff_rows] + [bvals[k] for k in diff_bcs]
        _, vjp = jax.vjp(f, *primals)
        grads = list(vjp(tuple(cvals)))
        if lat_add is not None:
            add = refs[n_in - 1][...]
            grads[0] = grads[0] + (add if lat_add.shape[0] == n_rows else jnp.where(i >= ncb, add, 0.0))
        for k in range(ndr):
            outs[k][...] = grads[k].astype(outs[k].dtype)
        for k, idx in enumerate(diff_bcs):
            o_ref = outs[ndr + k]
            first = (i == 0)
            if bcs[idx].shape[0] == 2:
                first = first | (i == ncb)

            @pl.when(first)
            def _(o_ref=o_ref):
                o_ref[...] = jnp.zeros_like(o_ref)

            o_ref[0] += grads[ndr + k]

    out_shape = [jax.ShapeDtypeStruct((n_rows, rows[k].width), F32) for k in diff_rows]
    out_shape += [jax.ShapeDtypeStruct(bcs[k].shape, F32) for k in diff_bcs]
    out_specs = [pl.BlockSpec((tm, rows[k].width), lambda i: (i, 0)) for k in diff_rows]
    out_specs += [_bc_spec(bcs[k], ncb) for k in diff_bcs]
    ins = [r.arr for r in rows] + list(bcs) + [c.arr for c in cts + extra]
    in_specs = [r.spec(tm) for r in rows] + [_bc_spec(b, ncb) for b in bcs] + [c.spec(tm) for c in cts + extra]
    if lat_add is not None:
        ins.append(lat_add)
        skip = ncb if lat_add.shape[0] != n_rows else 0
        in_specs.append(pl.BlockSpec((tm, lat_add.shape[1]), lambda i: (jnp.maximum(i - skip, 0), 0)))
    outs = pl.pallas_call(
        body, out_shape=out_shape, grid=(n_rows // tm,), in_specs=in_specs, out_specs=out_specs,
        compiler_params=_params(("arbitrary",), VMEM_LIMIT), name=name)(*ins)
    return outs


def _rms(x):
    return x * lax.rsqrt(jnp.mean(x * x, axis=-1, keepdims=True) + EPS)


def _sigmoid(x):
    return 0.5 * (jnp.tanh(0.5 * x) + 1.0)


def _silu(x):
    return x * _sigmoid(x)


def _gelu_tanh(x):
    return 0.5 * x * (1.0 + jnp.tanh(math.sqrt(2.0 / math.pi) * (x + 0.044715 * (x * x * x))))


def f_norm_mod(x, g, sc, sh):
    return ((_rms(x) * g) * (1.0 + sc) + sh,)


def f_rms(x, g):
    return (_rms(x) * g,)


def f_gate(o, z):
    return (o * _silu(z),)


def f_res(x, o, gt):
    return (x + gt * o,)


def f_s5_act(y, u, d):
    return (_gelu_tanh(y + d * u),)


def f_s5_glu(ya, gl, z, b):
    return (ya * _sigmoid(gl + b) * _silu(z),)


def loss_and_grad(x2, final_g, target, name="loss_head"):
    n, d = x2.shape
    tm = math.gcd(ROW_TILE, n)

    def row_loss(x, g, t):
        y = _rms(x) * g
        e = y - t
        return 0.5 * (e * e) * (1.0 / d)

    def body(x_ref, g_ref, t_ref, l_ref, dx_ref, dg_ref):
        @pl.when(pl.program_id(0) == 0)
        def _():
            l_ref[...] = jnp.zeros_like(l_ref)
            dg_ref[...] = jnp.zeros_like(dg_ref)

        t = t_ref[...]
        lterm, vjp = jax.vjp(lambda x, g: row_loss(x, g, t), x_ref[...], g_ref[...])
        dx, dg = vjp(jnp.ones_like(lterm))
        l_ref[...] += jnp.sum(lterm, axis=0, keepdims=True)
        dx_ref[...] = dx
        dg_ref[...] += dg

    return pl.pallas_call(
        body,
        out_shape=[jax.ShapeDtypeStruct((1, d), F32), jax.ShapeDtypeStruct((n, d), F32),
                   jax.ShapeDtypeStruct((1, d), F32)],
        grid=(n // tm,),
        in_specs=[pl.BlockSpec((tm, d), lambda i: (i, 0)), pl.BlockSpec((1, d), lambda i: (0, 0)),
                  pl.BlockSpec((tm, d), lambda i: (i, 0))],
        out_specs=[pl.BlockSpec((1, d), lambda i: (0, 0)), pl.BlockSpec((tm, d), lambda i: (i, 0)),
                   pl.BlockSpec((1, d), lambda i: (0, 0))],
        compiler_params=_params(("arbitrary",), VMEM_LIMIT), name=name)(x2, final_g, target)


NT_DIMS = (((1,), (1,)), ((), ()))
HEAD_LANES = 128
N_PAIRS = MLA_HEADS // 2


def _own_lanes(shape, hh):
    lane = lax.broadcasted_iota(jnp.int32, shape, len(shape) - 1)
    return (lane < V_HEAD_DIM) if hh == 0 else (lane >= V_HEAD_DIM)


def _rope_tiles(x, cos, sin_next, sin_prev, inverse):
    width = x.shape[-1]
    reps = width // HEAD_LANES
    c, sn, sp = (jnp.tile(t, (1, reps)) for t in (cos, sin_next, sin_prev))
    if inverse:
        return x * c + pltpu.roll(x * sn, 8, 1) + pltpu.roll(x * sp, width - 8, 1)
    return x * c + pltpu.roll(x, width - 8, 1) * sn + pltpu.roll(x, 8, 1) * sp


def _col_to_row(col):
    n = col.shape[0]
    hi = col.astype(BF16)
    r1 = col - hi.astype(F32)
    mid = r1.astype(BF16)
    lo = (r1 - mid.astype(F32)).astype(BF16)
    lane = lax.broadcasted_iota(jnp.int32, (n, HEAD_LANES), 1)
    terms = jnp.where(lane == 0, hi, jnp.where(lane == 1, mid, jnp.where(lane == 2, lo, jnp.zeros_like(hi))))
    eye = (lax.broadcasted_iota(jnp.int32, (n, n), 0) == lax.broadcasted_iota(jnp.int32, (n, n), 1)).astype(BF16)
    rows = lax.dot_general(terms, eye, (((0,), (0,)), ((), ())), preferred_element_type=F32)
    return rows[0:1] + rows[1:2] + rows[2:3]


def attn_fwd(qb, kb, vb, n_ctx):
    T = qb.shape[0]
    tq = math.gcd(ROW_TILE, n_ctx)
    nq, ncb = T // tq, n_ctx // tq

    def body(q_ref, k_ref, v_ref, o_ref, lse_ref, lse_row_ref):
        qi = pl.program_id(1)

        def rows(n_keys):
            v = v_ref[:n_keys, :]
            outs = []
            for hh in range(2):
                hs = slice(hh * HEAD_LANES, (hh + 1) * HEAD_LANES)
                s = lax.dot_general(q_ref[:, hs], k_ref[:n_keys, hs], NT_DIMS,
                                    preferred_element_type=F32) * SOFTMAX_SCALE
                m = jnp.max(s, axis=-1, keepdims=True)
                p = jnp.exp(s - m)
                l = jnp.sum(p, axis=-1, keepdims=True)
                outs.append(jnp.dot(p.astype(BF16), v, preferred_element_type=F32) / l)
                lse = m + jnp.log(l)
                lse_ref[hh] = lse
                lse_row_ref[hh] = _col_to_row(lse)
            o_ref[...] = jnp.where(_own_lanes(outs[0].shape, 0), outs[0], outs[1])

        pl.when(qi < ncb)(lambda: rows(n_ctx))
        pl.when(qi >= ncb)(lambda: rows(T))

    return pl.pallas_call(
        body,
        out_shape=[jax.ShapeDtypeStruct((T, MLA_HEADS * V_HEAD_DIM), F32),
                   jax.ShapeDtypeStruct((MLA_HEADS, T, 1), F32), jax.ShapeDtypeStruct((MLA_HEADS, 1, T), F32)],
        grid=(N_PAIRS, nq),
        in_specs=[pl.BlockSpec((tq, 2 * HEAD_LANES), lambda h, i: (i, h)),
                  pl.BlockSpec((T, 2 * HEAD_LANES), lambda h, i: (0, h)),
                  pl.BlockSpec((T, 2 * V_HEAD_DIM), lambda h, i: (0, h))],
        out_specs=[pl.BlockSpec((tq, 2 * V_HEAD_DIM), lambda h, i: (i, h)),
                   pl.BlockSpec((2, tq, 1), lambda h, i: (h, i, 0)),
                   pl.BlockSpec((2, 1, tq), lambda h, i: (h, 0, i))],
        compiler_params=_params(("parallel", "parallel"), VMEM_LIMIT), name="attn_fwd")(qb, kb, vb)


def attn_bwd_dq(qb, kb, vb, o, do, lse, n_ctx):
    T = qb.shape[0]
    tq = math.gcd(ROW_TILE, n_ctx)
    nq, ncb = T // tq, n_ctx // tq

    def body(q_ref, k_ref, v_ref, o_ref, do_ref, lse_ref, dq_ref, delta_ref):
        qi = pl.program_id(1)

        def rows(n_keys):
            v = v_ref[:n_keys, :]
            for hh in range(2):
                hs = slice(hh * HEAD_LANES, (hh + 1) * HEAD_LANES)
                k = k_ref[:n_keys, hs]
                do = jnp.where(_own_lanes(do_ref.shape, hh), do_ref[...], 0.0)
                delta = jnp.sum(do * o_ref[...], axis=-1, keepdims=True)
                s = lax.dot_general(q_ref[:, hs], k, NT_DIMS, preferred_element_type=F32) * SOFTMAX_SCALE
                p = jnp.exp(s - lse_ref[hh])
                dp = lax.dot_general(do.astype(BF16), v, NT_DIMS, preferred_element_type=F32)
                ds = p * (dp - delta) * SOFTMAX_SCALE
                dq_ref[:, hs] = jnp.dot(ds.astype(BF16), k, preferred_element_type=F32)
                delta_ref[hh] = _col_to_row(delta)

        pl.when(qi < ncb)(lambda: rows(n_ctx))
        pl.when(qi >= ncb)(lambda: rows(T))

    return pl.pallas_call(
        body,
        out_shape=[jax.ShapeDtypeStruct((T, MLA_HEADS * HEAD_LANES), F32),
                   jax.ShapeDtypeStruct((MLA_HEADS, 1, T), F32)],
        grid=(N_PAIRS, nq),
        in_specs=[pl.BlockSpec((tq, 2 * HEAD_LANES), lambda h, i: (i, h)),
                  pl.BlockSpec((T, 2 * HEAD_LANES), lambda h, i: (0, h)),
                  pl.BlockSpec((T, 2 * V_HEAD_DIM), lambda h, i: (0, h)),
                  pl.BlockSpec((tq, 2 * V_HEAD_DIM), lambda h, i: (i, h)),
                  pl.BlockSpec((tq, 2 * V_HEAD_DIM), lambda h, i: (i, h)),
                  pl.BlockSpec((2, tq, 1), lambda h, i: (h, i, 0))],
        out_specs=[pl.BlockSpec((tq, 2 * HEAD_LANES), lambda h, i: (i, h)),
                   pl.BlockSpec((2, 1, tq), lambda h, i: (h, 0, i))],
        compiler_params=_params(("parallel", "parallel"), VMEM_LIMIT), name="attn_bwd_dq")(
            qb, kb, vb, o, do, lse)


def attn_bwd_dkv(qb, kb, vb, do, lse_rows, delta_rows, n_ctx):
    T = qb.shape[0]
    tq = math.gcd(ROW_TILE, n_ctx)
    nq, ncb = T // tq, n_ctx // tq

    def body(q_ref, do_ref, lse_ref, delta_ref, k_ref, v_ref, dk_ref, dv_ref):
        kj = pl.program_id(1)

        def cols(first):
            v = v_ref[...]
            do_all = do_ref[first:, :]
            dv = None
            for hh in range(2):
                hs = slice(hh * HEAD_LANES, (hh + 1) * HEAD_LANES)
                k = k_ref[:, hs]
                q = q_ref[first:, hs]
                do16 = jnp.where(_own_lanes(do_all.shape, hh), do_all, 0.0).astype(BF16)
                st = lax.dot_general(k, q, NT_DIMS, preferred_element_type=F32) * SOFTMAX_SCALE
                pt = jnp.exp(st - lse_ref[hh, :, first:])
                dv_h = jnp.dot(pt.astype(BF16), do16, preferred_element_type=F32)
                dv = dv_h if dv is None else dv + dv_h
                dpt = lax.dot_general(v, do16, NT_DIMS, preferred_element_type=F32)
                dst = pt * (dpt - delta_ref[hh, :, first:]) * SOFTMAX_SCALE
                dk_ref[:, hs] = jnp.dot(dst.astype(BF16), q, preferred_element_type=F32)
            dv_ref[...] = dv

        pl.when(kj < ncb)(lambda: cols(0))
        pl.when(kj >= ncb)(lambda: cols(n_ctx))

    return pl.pallas_call(
        body,
        out_shape=[jax.ShapeDtypeStruct((T, MLA_HEADS * HEAD_LANES), F32),
                   jax.ShapeDtypeStruct((T, MLA_HEADS * V_HEAD_DIM), F32)],
        grid=(N_PAIRS, nq),
        in_specs=[pl.BlockSpec((T, 2 * HEAD_LANES), lambda h, j: (0, h)),
                  pl.BlockSpec((T, 2 * V_HEAD_DIM), lambda h, j: (0, h)),
                  pl.BlockSpec((2, 1, T), lambda h, j: (h, 0, 0)),
                  pl.BlockSpec((2, 1, T), lambda h, j: (h, 0, 0)),
                  pl.BlockSpec((tq, 2 * HEAD_LANES), lambda h, j: (j, h)),
                  pl.BlockSpec((tq, 2 * V_HEAD_DIM), lambda h, j: (j, h))],
        out_specs=[pl.BlockSpec((tq, 2 * HEAD_LANES), lambda h, j: (j, h)),
                   pl.BlockSpec((tq, 2 * V_HEAD_DIM), lambda h, j: (j, h))],
        compiler_params=_params(("parallel", "parallel"), VMEM_LIMIT), name="attn_bwd_dkv")(
            qb, do, lse_rows, delta_rows, kb, vb)


def _split_bf16(x):
    hi = x.astype(BF16)
    return hi, (x - hi.astype(F32)).astype(BF16)


def q_heads(qn, w_uq_p, tabs, name="l0_uq"):
    T, K = qn.shape
    N = w_uq_p.shape[1]
    tm = math.gcd(ROW_TILE, T)

    def body(a_ref, w_ref, c_ref, sn_ref, sp_ref, o_ref):
        acc = jnp.dot(a_ref[...], w_ref[...], preferred_element_type=F32)
        o_ref[...] = _rope_tiles(acc, c_ref[...], sn_ref[...], sp_ref[...], False).astype(BF16)

    tab = pl.BlockSpec((tm, HEAD_LANES), lambda i: (i, 0))
    return pl.pallas_call(
        body, out_shape=jax.ShapeDtypeStruct((T, N), BF16), grid=(T // tm,),
        in_specs=[pl.BlockSpec((tm, K), lambda i: (i, 0)), pl.BlockSpec((K, N), lambda i: (0, 0)), tab, tab, tab],
        out_specs=pl.BlockSpec((tm, N), lambda i: (i, 0)),
        compiler_params=_params(("parallel",), VMEM_LIMIT), name=name)(qn, w_uq_p, *tabs)


def kv_heads(kvn, w_kn_p, w_v, kr, spread, tabs, name="l0_ukv"):
    T, K = kvn.shape
    N = w_kn_p.shape[1]
    NV = w_v.shape[1]
    tm = math.gcd(ROW_TILE, T)

    def body(a_ref, wk_ref, wv_ref, kr_ref, e_ref, c_ref, sn_ref, sp_ref, k_ref, v_ref):
        a = a_ref[...]
        hi, lo = _split_bf16(kr_ref[...])
        acc = (jnp.dot(a, wk_ref[...], preferred_element_type=F32)
               + jnp.dot(hi, e_ref[...], preferred_element_type=F32)
               + jnp.dot(lo, e_ref[...], preferred_element_type=F32))
        k_ref[...] = _rope_tiles(acc, c_ref[...], sn_ref[...], sp_ref[...], False).astype(BF16)
        v_ref[...] = jnp.dot(a, wv_ref[...], preferred_element_type=F32).astype(BF16)

    tab = pl.BlockSpec((tm, HEAD_LANES), lambda i: (i, 0))
    return pl.pallas_call(
        body, out_shape=[jax.ShapeDtypeStruct((T, N), BF16), jax.ShapeDtypeStruct((T, NV), BF16)], grid=(T // tm,),
        in_specs=[pl.BlockSpec((tm, K), lambda i: (i, 0)), pl.BlockSpec((K, N), lambda i: (0, 0)),
                  pl.BlockSpec((K, NV), lambda i: (0, 0)), pl.BlockSpec((tm, QK_ROPE_DIM), lambda i: (i, 0)),
                  pl.BlockSpec((QK_ROPE_DIM, N), lambda i: (0, 0)), tab, tab, tab],
        out_specs=[pl.BlockSpec((tm, N), lambda i: (i, 0)), pl.BlockSpec((tm, NV), lambda i: (i, 0))],
        compiler_params=_params(("parallel",), VMEM_LIMIT), name=name)(kvn, w_kn_p, w_v, kr, spread, *tabs)


def heads_unrope(d, tabs, spread=None, name="unrope"):
    T, N = d.shape
    tm = math.gcd(ROW_TILE, T)

    def body(*refs):
        if spread is None:
            d_ref, c_ref, sn_ref, sp_ref, o_ref = refs
        else:
            d_ref, c_ref, sn_ref, sp_ref, e_ref, o_ref, kr_ref = refs
        g = _rope_tiles(d_ref[...], c_ref[...], sn_ref[...], sp_ref[...], True)
        o_ref[...] = g.astype(BF16)
        if spread is not None:
            hi, lo = _split_bf16(g)
            kr_ref[...] = (lax.dot_general(hi, e_ref[...], NT_DIMS, preferred_element_type=F32)
                           + lax.dot_general(lo, e_ref[...], NT_DIMS, preferred_element_type=F32))

    tab = pl.BlockSpec((tm, HEAD_LANES), lambda i: (i, 0))
    row = pl.BlockSpec((tm, N), lambda i: (i, 0))
    ins, in_specs = [d, *tabs], [row, tab, tab, tab]
    out_shape, out_specs = [jax.ShapeDtypeStruct((T, N), BF16)], [row]
    if spread is not None:
        ins.append(spread)
        in_specs.append(pl.BlockSpec((QK_ROPE_DIM, N), lambda i: (0, 0)))
        out_shape.append(jax.ShapeDtypeStruct((T, QK_ROPE_DIM), F32))
        out_specs.append(pl.BlockSpec((tm, QK_ROPE_DIM), lambda i: (i, 0)))
    return pl.pallas_call(
        body, out_shape=out_shape, grid=(T // tm,), in_specs=in_specs, out_specs=out_specs,
        compiler_params=_params(("parallel",), VMEM_LIMIT), name=name)(*ins)


def _cmul(ar, ai, br, bi):
    return ar * br - ai * bi, ar * bi + ai * br


def s5_chain(finals, s0, a, n_steps, reverse, name):
    W = finals.shape[-1]
    first = N_SEG - 1 if reverse else 0

    def body(f_ref, s0_ref, a_ref, c_ref):
        pr, pi = jnp.ones((1, W), F32), jnp.zeros((1, W), F32)
        br, bi = a_ref[0], a_ref[1]
        n = n_steps
        while n:
            if n & 1:
                pr, pi = _cmul(pr, pi, br, bi)
            br, bi = _cmul(br, bi, br, bi)
            n >>= 1
        fr, fi = f_ref[0], f_ref[1]
        row = lax.broadcasted_iota(jnp.int32, (N_SEG, W), 0)
        s0r = jnp.broadcast_to(s0_ref[0], (N_SEG, W))
        s0i = jnp.broadcast_to(s0_ref[1], (N_SEG, W))
        cr = jnp.where(row == first, s0r, 0.0)
        ci = jnp.where(row == first, s0i, 0.0)
        shift = N_SEG - 1 if reverse else 1
        for _ in range(N_SEG - 1):
            mr, mi = _cmul(pr, pi, cr, ci)
            tr = pltpu.roll(fr + mr, shift, 0)
            ti = pltpu.roll(fi + mi, shift, 0)
            cr = jnp.where(row == first, s0r, tr)
            ci = jnp.where(row == first, s0i, ti)
        c_ref[0] = cr
        c_ref[1] = ci

    return pl.pallas_call(body, out_shape=jax.ShapeDtypeStruct((2, N_SEG, W), F32), name=name)(finals, s0, a)


def _scan_chunk(bur, bui, st_ref, a_ref, n_steps, reverse):
    for lc in range(S5_LANES // BLK_ST):
        sl = slice(lc * BLK_ST, (lc + 1) * BLK_ST)
        lr = jnp.broadcast_to(a_ref[0, :, sl], (N_SEG, BLK_ST))
        li = jnp.broadcast_to(a_ref[1, :, sl], (N_SEG, BLK_ST))

        def step(jj, carry, sl=sl, lr=lr, li=li):
            sr, si = carry
            j = (n_steps - 1 - jj) if reverse else jj
            r0 = pl.multiple_of(j * N_SEG, N_SEG)
            nr = lr * sr - li * si + bur[pl.ds(r0, N_SEG), sl]
            ni = lr * si + li * sr + bui[pl.ds(r0, N_SEG), sl]
            bur[pl.ds(r0, N_SEG), sl] = nr
            bui[pl.ds(r0, N_SEG), sl] = ni
            return nr, ni

        sr, si = lax.fori_loop(0, n_steps, step, (st_ref[0, :, sl], st_ref[1, :, sl]))
        st_ref[0, :, sl] = sr
        st_ref[1, :, sl] = si


def _project_in(x16, w_re, w_im, bur, bui, adjoint):
    for gb in range(N_BLOCKS):
        xb = x16[:, gb * BLK_CH:(gb + 1) * BLK_CH]
        sl = slice(gb * BLK_ST, (gb + 1) * BLK_ST)
        if adjoint:
            dn = (((1,), (1,)), ((), ()))
            bur[:, sl] = lax.dot_general(xb, w_re[gb], dn, preferred_element_type=F32)
            bui[:, sl] = -lax.dot_general(xb, w_im[gb], dn, preferred_element_type=F32)
        else:
            bur[:, sl] = jnp.dot(xb, w_re[gb], preferred_element_type=F32)
            bui[:, sl] = jnp.dot(xb, w_im[gb], preferred_element_type=F32)


def s5_scan(act, w_re, w_im, a, init, *, reverse, adjoint=False, c_re=None, c_im=None, add=None,
            want_ckpt=False, rows=None, name):
    act_off, N = rows if rows is not None else (0, act.shape[0])
    R = math.gcd(ROW_TILE, N, act_off)
    nch, jc = N // R, R // N_SEG
    with_out = c_re is not None

    def chunk(i):
        return (nch - 1 - i) if reverse else i

    def body(*refs):
        act_ref, wre_ref, wim_ref, a_ref, init_ref = refs[:5]
        k = 5
        if with_out:
            cre_ref, cim_ref = refs[k:k + 2]
            k += 2
        if add is not None:
            add_ref = refs[k]
            k += 1
        if with_out:
            out_ref = refs[k]
            k += 1
        if want_ckpt:
            ck_ref = refs[k]
            k += 1
        fin_ref, bur, bui = refs[k:k + 3]

        @pl.when(pl.program_id(0) == 0)
        def _():
            fin_ref[...] = init_ref[...]

        if want_ckpt:
            ck_ref[0] = fin_ref[...]
        _project_in(act_ref[...].astype(BF16), wre_ref, wim_ref, bur, bui, adjoint)
        _scan_chunk(bur, bui, fin_ref, a_ref, jc, reverse)
        if with_out:
            for gb in range(N_BLOCKS):
                sl = slice(gb * BLK_ST, (gb + 1) * BLK_ST)
                y = (jnp.dot(bur[:, sl].astype(BF16), cre_ref[gb], preferred_element_type=F32)
                     - jnp.dot(bui[:, sl].astype(BF16), cim_ref[gb], preferred_element_type=F32))
                cs = slice(gb * BLK_CH, (gb + 1) * BLK_CH)
                if add is not None:
                    y = y + add_ref[:, cs]
                out_ref[:, cs] = y

    row_spec = pl.BlockSpec((R, D_MODEL), lambda i: (chunk(i), 0))
    act_spec = pl.BlockSpec((R, D_MODEL), lambda i: (chunk(i) + act_off // R, 0))
    w_spec = pl.BlockSpec(w_re.shape, lambda i: (0, 0, 0))
    st_spec = pl.BlockSpec((2, N_SEG, S5_LANES), lambda i: (0, 0, 0))
    ins = [act, w_re, w_im, a, init]
    in_specs = [act_spec, w_spec, w_spec, pl.BlockSpec((2, 1, S5_LANES), lambda i: (0, 0, 0)), st_spec]
    if with_out:
        ins += [c_re, c_im]
        in_specs += [pl.BlockSpec(c_re.shape, lambda i: (0, 0, 0))] * 2
    if add is not None:
        ins.append(add)
        in_specs.append(row_spec)
    out_shape, out_specs = [], []
    if with_out:
        out_shape.append(jax.ShapeDtypeStruct((N, D_MODEL), F32))
        out_specs.append(row_spec)
    if want_ckpt:
        out_shape.append(jax.ShapeDtypeStruct((nch, 2, N_SEG, S5_LANES), F32))
        out_specs.append(pl.BlockSpec((1, 2, N_SEG, S5_LANES), lambda i: (chunk(i), 0, 0, 0)))
    out_shape.append(jax.ShapeDtypeStruct((2, N_SEG, S5_LANES), F32))
    out_specs.append(st_spec)
    res = pl.pallas_call(
        body, out_shape=out_shape, grid=(nch,), in_specs=in_specs, out_specs=out_specs,
        scratch_shapes=[pltpu.VMEM((R, S5_LANES), F32), pltpu.VMEM((R, S5_LANES), F32)],
        compiler_params=_params(("arbitrary",), VMEM_LIMIT), name=name)(*ins)
    res = list(res)
    out = res.pop(0) if with_out else None
    ckpt = res.pop(0) if want_ckpt else None
    return out, ckpt, res[0]


def s5_grads(dy, u, ckpt, b_re, b_im, c_re, c_im, lam, init_adj, *, reverse, add=None, u_off=0, name):
    N = dy.shape[0]
    R = math.gcd(ROW_TILE, N, u_off)
    nch, jc = N // R, R // N_SEG
    W = S5_LANES

    def chunk(i):
        return i if reverse else (nch - 1 - i)

    def body(*refs):
        dy_ref, u_ref, ck_ref, bre_ref, bim_ref, cre_ref, cim_ref, lam_ref, init_ref = refs[:9]
        k = 9
        if add is not None:
            add_ref = refs[k]
            k += 1
        du_ref, dlam_ref, dbre_ref, dbim_ref, dcre_ref, dcim_ref, fin_ref = refs[k:k + 7]
        sr_buf, si_buf, er_buf, ei_buf, st_buf = refs[k + 7:k + 12]

        @pl.when(pl.program_id(0) == 0)
        def _():
            fin_ref[...] = init_ref[...]
            dlam_ref[...] = jnp.zeros_like(dlam_ref)
            dbre_ref[...] = jnp.zeros_like(dbre_ref)
            dbim_ref[...] = jnp.zeros_like(dbim_ref)
            dcre_ref[...] = jnp.zeros_like(dcre_ref)
            dcim_ref[...] = jnp.zeros_like(dcim_ref)

        u16 = u_ref[...].astype(BF16)
        dy16 = dy_ref[...].astype(BF16)
        st_buf[...] = ck_ref[0]
        _project_in(u16, bre_ref, bim_ref, sr_buf, si_buf, False)
        _scan_chunk(sr_buf, si_buf, st_buf, lam_ref, jc, reverse)
        _project_in(dy16, cre_ref, cim_ref, er_buf, ei_buf, True)
        for lc in range(W // BLK_ST):
            sl = slice(lc * BLK_ST, (lc + 1) * BLK_ST)
            lr = jnp.broadcast_to(lam_ref[0, :, sl], (N_SEG, BLK_ST))
            li = jnp.broadcast_to(lam_ref[1, :, sl], (N_SEG, BLK_ST))

            def one(r0, spr, spi, carry, sl=sl, lr=lr, li=li):
                gr, gi, ar, ai = carry
                nr = er_buf[pl.ds(r0, N_SEG), sl] + lr * gr + li * gi
                ni = ei_buf[pl.ds(r0, N_SEG), sl] + lr * gi - li * gr
                er_buf[pl.ds(r0, N_SEG), sl] = nr
                ei_buf[pl.ds(r0, N_SEG), sl] = ni
                return nr, ni, ar + spr * nr + spi * ni, ai + spr * ni - spi * nr

            def step(ff, carry, sl=sl, one=one):
                f = jc - 1 - ff
                j = (jc - 1 - f) if reverse else f
                jp = (j + 1) if reverse else (j - 1)
                r0 = pl.multiple_of(j * N_SEG, N_SEG)
                p0 = pl.multiple_of(jp * N_SEG, N_SEG)
                return one(r0, sr_buf[pl.ds(p0, N_SEG), sl], si_buf[pl.ds(p0, N_SEG), sl], carry)

            carry = (fin_ref[0, :, sl], fin_ref[1, :, sl], dlam_ref[0, :, sl], dlam_ref[1, :, sl])
            carry = lax.fori_loop(0, jc - 1, step, carry)
            r_first = (jc - 1) * N_SEG if reverse else 0
            gr, gi, ar, ai = one(r_first, ck_ref[0, 0, :, sl], ck_ref[0, 1, :, sl], carry)
            fin_ref[0, :, sl] = gr
            fin_ref[1, :, sl] = gi
            dlam_ref[0, :, sl] = ar
            dlam_ref[1, :, sl] = ai
        tn = (((0,), (0,)), ((), ()))
        nt = (((1,), (1,)), ((), ()))
        for gb in range(N_BLOCKS):
            sl = slice(gb * BLK_ST, (gb + 1) * BLK_ST)
            cs = slice(gb * BLK_CH, (gb + 1) * BLK_CH)
            gr16 = er_buf[:, sl].astype(BF16)
            gi16 = ei_buf[:, sl].astype(BF16)
            du = (lax.dot_general(gr16, bre_ref[gb], nt, preferred_element_type=F32)
                  + lax.dot_general(gi16, bim_ref[gb], nt, preferred_element_type=F32))
            if add is not None:
                du = du + add_ref[:, cs]
            du_ref[:, cs] = du
            ub, dyb = u16[:, cs], dy16[:, cs]
            dbre_ref[gb] += lax.dot_general(ub, gr16, tn, preferred_element_type=F32)
            dbim_ref[gb] += lax.dot_general(ub, gi16, tn, preferred_element_type=F32)
            dcre_ref[gb] += lax.dot_general(sr_buf[:, sl].astype(BF16), dyb, tn, preferred_element_type=F32)
            dcim_ref[gb] -= lax.dot_general(si_buf[:, sl].astype(BF16), dyb, tn, preferred_element_type=F32)

    row_spec = pl.BlockSpec((R, D_MODEL), lambda i: (chunk(i), 0))
    st_spec = pl.BlockSpec((2, N_SEG, W), lambda i: (0, 0, 0))
    wb_spec = pl.BlockSpec(b_re.shape, lambda i: (0, 0, 0))
    wc_spec = pl.BlockSpec(c_re.shape, lambda i: (0, 0, 0))
    ins = [dy, u, ckpt, b_re, b_im, c_re, c_im, lam, init_adj]
    u_spec = pl.BlockSpec((R, D_MODEL), lambda i: (chunk(i) + u_off // R, 0))
    in_specs = [row_spec, u_spec, pl.BlockSpec((1, 2, N_SEG, W), lambda i: (chunk(i), 0, 0, 0)),
                wb_spec, wb_spec, wc_spec, wc_spec, pl.BlockSpec((2, 1, W), lambda i: (0, 0, 0)), st_spec]
    if add is not None:
        ins.append(add)
        in_specs.append(row_spec)
    out_shape = [jax.ShapeDtypeStruct((N, D_MODEL), F32), jax.ShapeDtypeStruct((2, N_SEG, W), F32),
                 jax.ShapeDtypeStruct(b_re.shape, F32), jax.ShapeDtypeStruct(b_re.shape, F32),
                 jax.ShapeDtypeStruct(c_re.shape, F32), jax.ShapeDtypeStruct(c_re.shape, F32),
                 jax.ShapeDtypeStruct((2, N_SEG, W), F32)]
    out_specs = [row_spec, st_spec, wb_spec, wb_spec, wc_spec, wc_spec, st_spec]
    return pl.pallas_call(
        body, out_shape=out_shape, grid=(nch,), in_specs=in_specs, out_specs=out_specs,
        scratch_shapes=[pltpu.VMEM((R, W), F32) for _ in range(4)] + [pltpu.VMEM((2, N_SEG, W), F32)],
        compiler_params=_params(("arbitrary",), VMEM_LIMIT), name=name)(*ins)


def adamw(w, g, m, v, name="adamw"):
    n, d = w.shape
    lanes = -(-d // 128) * 128
    tm = n
    while tm * lanes * 4 > (1 << 20) and tm % 16 == 0:
        tm //= 2
    c1 = 1.0 - ADAM_B1 ** ADAM_STEP
    c2 = 1.0 - ADAM_B2 ** ADAM_STEP

    def body(w_ref, g_ref, m_ref, v_ref, d_ref, nm_ref, nv_ref):
        g_ = g_ref[...]
        m_ = ADAM_B1 * m_ref[...] + (1.0 - ADAM_B1) * g_
        v_ = ADAM_B2 * v_ref[...] + (1.0 - ADAM_B2) * (g_ * g_)
        d_ref[...] = -ADAM_LR * ((m_ / c1) / (jnp.sqrt(v_ / c2) + ADAM_EPS) + ADAM_WD * w_ref[...])
        nm_ref[...] = m_
        nv_ref[...] = v_

    spec = pl.BlockSpec((tm, d), lambda i: (i, 0))
    return pl.pallas_call(
        body, out_shape=[jax.ShapeDtypeStruct((n, d), F32)] * 3, grid=(n // tm,),
        in_specs=[spec] * 4, out_specs=[spec] * 3,
        compiler_params=_params(("parallel",), VMEM_LIMIT), name=name)(w, g, m, v)


def _coords():
    return lax.axis_index("x"), lax.axis_index("y"), lax.axis_index("c")


def exchange(arrays, out_shapes, remote, local, name, aliases=None):
    n_in, n_out, n_rem, n_loc = len(arrays), len(out_shapes), len(remote), len(local)

    def at(ref, idx):
        return ref if idx is None else ref.at[idx]

    def body(*refs):
        ins, outs = refs[:n_in], refs[n_in:n_in + n_out]
        send_sems, recv_sems, local_sems = refs[n_in + n_out:]
        me = _coords()
        sends, recvs = [], []
        for k, (flip, ii, src_at, oi, dst_at) in enumerate(remote):
            peer = (me[0] ^ flip[0], me[1] ^ flip[1], me[2] ^ flip[2])
            src = at(ins[ii], src_at(me, peer))
            sends.append(pltpu.make_async_remote_copy(
                src_ref=src, dst_ref=at(outs[oi], dst_at(me)), send_sem=send_sems.at[k], recv_sem=recv_sems.at[k],
                device_id=peer, device_id_type=MESH))
            recvs.append(pltpu.make_async_remote_copy(
                src_ref=src, dst_ref=at(outs[oi], dst_at(peer)), send_sem=send_sems.at[k], recv_sem=recv_sems.at[k],
                device_id=peer, device_id_type=MESH))
        locs = [pltpu.make_async_copy(at(ins[ii], src_at(me)), at(outs[oi], dst_at(me)), local_sems.at[k])
                for k, (ii, src_at, oi, dst_at) in enumerate(local)]
        for cp in locs + sends:
            cp.start()
        for cp in recvs:
            cp.wait_recv()
        for cp in sends:
            cp.wait_send()
        for cp in locs:
            cp.wait()

    hbm = pl.BlockSpec(memory_space=pl.ANY)
    return pl.pallas_call(
        body, out_shape=list(out_shapes), in_specs=[hbm] * n_in, out_specs=[hbm] * n_out,
        scratch_shapes=[pltpu.SemaphoreType.DMA((n_rem,)), pltpu.SemaphoreType.DMA((n_rem,)),
                        pltpu.SemaphoreType.DMA((max(n_loc, 1),))],
        input_output_aliases=aliases or {}, name=name)(*arrays)


ALL_FLIPS = [(dx, dy, dc) for dx in (0, 1) for dy in (0, 1) for dc in (0, 1)][1:]
CHIP_FLIPS = [(1, 0, 0), (0, 1, 0), (1, 1, 0)]
CORE_FLIP = (0, 0, 1)


def _dev_index(p):
    return 4 * p[0] + 2 * p[1] + p[2]


def _chip_index(p):
    return 2 * p[0] + p[1]


def _gather(xs, flips, index, n, name):
    arrays = [x[None] for x in xs]
    outs = [jax.ShapeDtypeStruct((n,) + x.shape, x.dtype) for x in xs]
    remote = [(f, a, lambda me, peer: (0,), a, lambda s: (index(s),)) for a in range(len(xs)) for f in flips]
    local = [(a, lambda me: (0,), a, lambda me: (index(me),)) for a in range(len(xs))]
    return exchange(arrays, outs, remote, local, name)


def allgather_devices(x, name):
    return _gather([x], ALL_FLIPS, _dev_index, N_DEV, name)[0]


def allgather_chips(xs, name):
    return _gather(xs, CHIP_FLIPS, _chip_index, N_CHIP, name)


def gather_halves(xs, name):
    n = len(xs)
    nk = n * len(CHIP_FLIPS)

    def body(*refs):
        ins, outs = refs[:n], refs[n:2 * n]
        ici_send, ici_recv, d2d_send, d2d_recv = refs[2 * n:]
        me = _coords()
        sibling = (me[0], me[1], 1 - me[2])
        first, passed, landed = [], [], []
        for a in range(n):
            half = ins[a].shape[0] // 2
            mine = ins[a].at[pl.ds(pl.multiple_of(me[2] * half, 16), half)]
            for j, flip in enumerate(CHIP_FLIPS):
                k = a * len(CHIP_FLIPS) + j
                peer = (me[0] ^ flip[0], me[1] ^ flip[1], me[2])
                first.append(pltpu.make_async_remote_copy(
                    src_ref=mine, dst_ref=outs[a].at[_chip_index(me), me[2]], send_sem=ici_send.at[k],
                    recv_sem=ici_recv.at[k], device_id=peer, device_id_type=MESH))
                arrived = outs[a].at[_chip_index(peer), me[2]]
                landed.append(pltpu.make_async_remote_copy(
                    src_ref=mine, dst_ref=arrived, send_sem=ici_send.at[k], recv_sem=ici_recv.at[k],
                    device_id=peer, device_id_type=MESH))
                passed.append(pltpu.make_async_remote_copy(
                    src_ref=arrived, dst_ref=arrived, send_sem=d2d_send.at[k], recv_sem=d2d_recv.at[k],
                    device_id=sibling, device_id_type=MESH))
        for cp in first:
            cp.start()
        for k in range(nk):
            landed[k].wait_recv()
            passed[k].start()
        for a in range(n):
            for j, flip in enumerate(CHIP_FLIPS):
                k = a * len(CHIP_FLIPS) + j
                peer_chip = _chip_index((me[0] ^ flip[0], me[1] ^ flip[1]))
                from_sibling = outs[a].at[peer_chip, 1 - me[2]]
                pltpu.make_async_remote_copy(
                    src_ref=from_sibling, dst_ref=from_sibling, send_sem=d2d_send.at[k], recv_sem=d2d_recv.at[k],
                    device_id=sibling, device_id_type=MESH).wait_recv()
        for cp in first + passed:
            cp.wait_send()

    hbm = pl.BlockSpec(memory_space=pl.ANY)
    return pl.pallas_call(
        body, out_shape=[jax.ShapeDtypeStruct((N_CHIP, 2, x.shape[0] // 2, x.shape[1]), x.dtype) for x in xs],
        in_specs=[hbm] * n, out_specs=[hbm] * n,
        scratch_shapes=[pltpu.SemaphoreType.DMA((nk,)) for _ in range(4)], name=name)(*xs)


def _half_tile(h, cd):
    return h if h * cd * 4 <= (1 << 20) else math.gcd(512, h)


def pair_add(g, got, core, out_dtype, name):
    _, _, h, cd = g.shape
    th = _half_tile(h, cd)

    def body(c_ref, g_ref, got_ref, o_ref):
        o_ref[0] = (g_ref[0, 0] + got_ref[0]).astype(o_ref.dtype)

    return pl.pallas_call(
        body, out_shape=jax.ShapeDtypeStruct((N_CHIP, h, cd), out_dtype),
        grid_spec=pltpu.PrefetchScalarGridSpec(
            num_scalar_prefetch=1, grid=(N_CHIP, h // th),
            in_specs=[pl.BlockSpec((1, 1, th, cd), lambda q, i, c: (q, c[0], i, 0)),
                      pl.BlockSpec((1, th, cd), lambda q, i, c: (q, i, 0))],
            out_specs=pl.BlockSpec((1, th, cd), lambda q, i, c: (q, i, 0))),
        compiler_params=_params(("parallel", "parallel"), VMEM_LIMIT), name=name)(core, g, got)


def sum_chips(parts, core, name):
    _, h, cd = parts.shape
    th = _half_tile(h, cd)

    def body(c_ref, p_ref, o_ref):
        acc = p_ref[0].astype(F32)
        for q in range(1, N_CHIP):
            acc = acc + p_ref[q].astype(F32)
        o_ref[0] = acc

    return pl.pallas_call(
        body, out_shape=jax.ShapeDtypeStruct((2, h, cd), F32),
        grid_spec=pltpu.PrefetchScalarGridSpec(
            num_scalar_prefetch=1, grid=(h // th,),
            in_specs=[pl.BlockSpec((N_CHIP, th, cd), lambda i, c: (0, i, 0))],
            out_specs=pl.BlockSpec((1, th, cd), lambda i, c: (c[0], i, 0))),
        compiler_params=_params(("parallel",), VMEM_LIMIT), name=name)(core, parts)


def to_segments(a, n_ctx):
    def one(p):
        n = p.shape[0]
        return p.reshape(N_SEG, n // N_SEG, -1).transpose(1, 0, 2).reshape(n, -1)
    return jnp.concatenate([one(a[:n_ctx]), one(a[n_ctx:])], axis=0) if n_ctx else one(a)


def from_segments(a, n_ctx):
    def one(p):
        n = p.shape[0]
        return p.reshape(n // N_SEG, N_SEG, -1).transpose(1, 0, 2).reshape(n, -1)
    return jnp.concatenate([one(a[:n_ctx]), one(a[n_ctx:])], axis=0) if n_ctx else one(a)


def rope_tables(n_ctx, n_lat):
    f32 = np.float32
    rows = n_lat // GRID_W
    row = np.repeat(np.arange(rows), GRID_W).astype(f32)
    col = np.tile(np.arange(GRID_W), rows).astype(f32)
    d = QK_ROPE_DIM // 2
    inv = (f32(1.0) / np.power(f32(ROPE_THETA), np.arange(0, d, 2, dtype=f32) / f32(d))).astype(f32)
    ang = np.concatenate([row[:, None] * inv[None, :], col[:, None] * inv[None, :]], axis=1).astype(f32)
    cos = np.concatenate([np.ones((n_ctx, d), f32), np.cos(ang)], axis=0)
    sin = np.concatenate([np.zeros((n_ctx, d), f32), np.sin(ang)], axis=0)
    q = QK_ROPE_DIM // 4
    T = n_ctx + n_lat
    ones, zeros = np.ones((T, QK_NOPE_DIM), f32), np.zeros((T, QK_NOPE_DIM), f32)
    tail, z8 = np.zeros((T, HEAD_LANES - QK_DIM), f32), np.zeros((T, q), f32)
    cr, cc, sr, sc = cos[:, :q], cos[:, q:], sin[:, :q], sin[:, q:]
    cos_t = np.concatenate([ones, cr, cr, cc, cc, tail], axis=1)
    sin_next = np.concatenate([zeros, -sr, z8, -sc, z8, tail], axis=1)
    sin_prev = np.concatenate([zeros, z8, sr, z8, sc, tail], axis=1)
    return tuple(jnp.asarray(t, F32) for t in (cos_t, sin_next, sin_prev))


def pad_heads(w, used):
    k = w.shape[0]
    return jnp.pad(w.reshape(k, MLA_HEADS, used), ((0, 0), (0, 0), (0, HEAD_LANES - used))).reshape(k, -1)


def unpad_heads(w, used):
    k = w.shape[0]
    return w.reshape(k, MLA_HEADS, HEAD_LANES)[:, :, :used].reshape(k, MLA_HEADS * used)


def rotary_spread():
    lane = np.arange(MLA_HEADS * HEAD_LANES) % HEAD_LANES
    return jnp.asarray(lane[None, :] == (QK_NOPE_DIM + np.arange(QK_ROPE_DIM))[:, None], BF16)


def s5_discretise(a_re, a_im, log_step, b_re, b_im):
    dt = jnp.exp(log_step)[:, None]
    mag = jnp.exp(a_re * dt)
    lb_re = mag * jnp.cos(a_im * dt)
    lb_im = mag * jnp.sin(a_im * dt)
    den = a_re * a_re + a_im * a_im
    nr = lb_re - 1.0
    f_re = ((nr * a_re + lb_im * a_im) / den)[..., None]
    f_im = ((lb_im * a_re - nr * a_im) / den)[..., None]
    return lb_re, lb_im, f_re * b_re - f_im * b_im, f_re * b_im + f_im * b_re


def s5_block_weights(lb_re, lb_im, bb_re, bb_im, c_re, c_im):
    eye = jnp.eye(GROUPS_PER_BLOCK, dtype=F32)
    lam = jnp.stack([lb_re.reshape(1, S5_LANES), lb_im.reshape(1, S5_LANES)])

    def b_blocks(bb):
        t = bb.reshape(N_BLOCKS, GROUPS_PER_BLOCK, S5_STATE, S5_GROUP)
        return jnp.einsum("bgpc,gh->bgchp", t, eye).reshape(N_BLOCKS, BLK_CH, BLK_ST).astype(BF16)

    def c_blocks(cc):
        t = cc.reshape(N_BLOCKS, GROUPS_PER_BLOCK, S5_GROUP, S5_STATE)
        return jnp.einsum("bgcp,gh->bgphc", t, eye).reshape(N_BLOCKS, BLK_ST, BLK_CH).astype(BF16)

    return lam, b_blocks(bb_re), b_blocks(bb_im), c_blocks(c_re), c_blocks(c_im)


def b_block_diag(db):
    t = db.reshape(N_BLOCKS, GROUPS_PER_BLOCK, S5_GROUP, GROUPS_PER_BLOCK, S5_STATE)
    return jnp.einsum("bgchp,gh->bgpc", t, jnp.eye(GROUPS_PER_BLOCK, dtype=F32)).reshape(S5_GROUPS, S5_STATE, S5_GROUP)


def c_block_diag(dc):
    t = dc.reshape(N_BLOCKS, GROUPS_PER_BLOCK, S5_STATE, GROUPS_PER_BLOCK, S5_GROUP)
    return jnp.einsum("bgphc,gh->bgcp", t, jnp.eye(GROUPS_PER_BLOCK, dtype=F32)).reshape(S5_GROUPS, S5_GROUP, S5_STATE)


def conj(a):
    return jnp.stack([a[0], -a[1]])


PACK_TILE = 16 * 128


def pack_flat(parts, dtype):
    flat = [p.reshape(-1).astype(dtype) for p in parts]
    sizes = [f.shape[0] for f in flat]
    total = sum(sizes)
    pad = (-total) % PACK_TILE
    if pad:
        flat.append(jnp.zeros((pad,), dtype))
    offs = np.cumsum([0] + sizes)[:-1].tolist()
    return jnp.concatenate(flat).reshape(-1, 128), offs


def unpack_flat(buf, offs, shapes):
    flat = buf.reshape(-1)
    return [flat[o:o + int(np.prod(s))].reshape(s) for o, s in zip(offs, shapes)]


def s5_forward(p1, n_ctx, dirs):
    saved = []
    y = None
    ctx_rows, lat_rows = (0, n_ctx), (n_ctx, p1.shape[0] - n_ctx)
    zeros_tile = jnp.zeros((2, N_SEG, S5_LANES), F32)
    zeros_row = jnp.zeros((2, 1, S5_LANES), F32)
    for k, (lam, b_re, b_im, c_re, c_im) in enumerate(dirs):
        rev = k == 1
        last = 0 if rev else N_SEG - 1
        _, _, fin = s5_scan(p1, b_re, b_im, lam, zeros_tile, reverse=rev, rows=ctx_rows, name=f"s5_ctx_finals{k}")
        carry_c = s5_chain(fin, zeros_row, lam, n_ctx // N_SEG, rev, name=f"s5_ctx_chain{k}")
        _, ck_c, fin_c = s5_scan(p1, b_re, b_im, lam, carry_c, reverse=rev, want_ckpt=True, rows=ctx_rows,
                                 name=f"s5_ctx_scan{k}")
        s0 = fin_c[:, last:last + 1, :]
        _, _, fin = s5_scan(p1, b_re, b_im, lam, zeros_tile, reverse=rev, rows=lat_rows, name=f"s5_lat_finals{k}")
        carry_l = s5_chain(fin, s0, lam, lat_rows[1] // N_SEG, rev, name=f"s5_lat_chain{k}")
        y, ck_l, _ = s5_scan(p1, b_re, b_im, lam, carry_l, reverse=rev, c_re=c_re, c_im=c_im, add=y,
                             want_ckpt=True, rows=lat_rows, name=f"s5_lat_scan{k}")
        saved.append((ck_c, ck_l))
    return y, saved


def s5_backward(dy_l, du_extra_l, p1, n_ctx, dirs, saved):
    n_lat = p1.shape[0] - n_ctx
    zeros_tile = jnp.zeros((2, N_SEG, S5_LANES), F32)
    zeros_row = jnp.zeros((2, 1, S5_LANES), F32)
    dy_c = jnp.zeros((n_ctx, D_MODEL), F32)
    du_l, du_c = du_extra_l, None
    grads = []
    for k, (lam, b_re, b_im, c_re, c_im) in enumerate(dirs):
        rev = k == 1
        lam_c = conj(lam)
        ck_c, ck_l = saved[k]
        first = N_SEG - 1 if rev else 0
        _, _, fin = s5_scan(dy_l, c_re, c_im, lam_c, zeros_tile, reverse=not rev, adjoint=True,
                            name=f"s5_lat_adj_finals{k}")
        carry = s5_chain(fin, zeros_row, lam_c, n_lat // N_SEG, not rev, name=f"s5_lat_adj_chain{k}")
        du_l, dlam_l, dbr_l, dbi_l, dcr_l, dci_l, fin_a = s5_grads(
            dy_l, p1, ck_l, b_re, b_im, c_re, c_im, lam, carry, reverse=rev, add=du_l, u_off=n_ctx,
            name=f"s5_lat_grads{k}")
        g0 = fin_a[:, first:first + 1, :]
        carry = s5_chain(zeros_tile, g0, lam_c, n_ctx // N_SEG, not rev, name=f"s5_ctx_adj_chain{k}")
        du_c, dlam_c, dbr_c, dbi_c, _, _, _ = s5_grads(
            dy_c, p1, ck_c, b_re, b_im, c_re, c_im, lam, carry, reverse=rev, add=du_c, name=f"s5_ctx_grads{k}")
        dlam = jnp.sum(dlam_l + dlam_c, axis=1)
        grads.append((dlam, b_block_diag(dbr_l + dbr_c), b_block_diag(dbi_l + dbi_c),
                      c_block_diag(dcr_l), c_block_diag(dci_l)))
    return jnp.concatenate([du_c, du_l], axis=0), grads


def local_step(x, ctx, target, mod, w):
    L, Lc = x.shape[0], ctx.shape[0]
    T = L + Lc
    assert L % Lc == 0 and Lc % (2 * N_SEG) == 0 and L % GRID_W == 0
    D = D_MODEL
    X0 = jnp.concatenate([ctx, x], axis=0)

    def mod_of(i, j):
        return mod[i, :, j, :][:, None, :]

    def vec(v):
        return v.reshape(1, 1, -1).astype(F32)

    g0 = vec(w["norm_g"][0])
    (H0,) = rowwise_fwd(f_norm_mod, [X0], [g0, mod_of(0, 1), mod_of(0, 0)], [D], [BF16], T, Lc, "l0_norm")
    p0 = mm_nn(H0, w["mla_w_in"], name="l0_in")
    z0 = Rows(p0, D, col_blk=0)
    cq = Rows(p0, Q_LORA_RANK, col_blk=D // Q_LORA_RANK)
    ckv = Rows(p0, KV_LORA_RANK, col_blk=(D + Q_LORA_RANK) // KV_LORA_RANK)
    kr = p0[:, D + Q_LORA_RANK + KV_LORA_RANK:D + P0_HEAD]
    qng, kvng = vec(w["mla_q_norm"]), vec(w["mla_kv_norm"])
    (qn,) = rowwise_fwd(f_rms, [cq], [qng], [Q_LORA_RANK], [BF16], T, 0, "l0_qnorm")
    (kvn,) = rowwise_fwd(f_rms, [ckv], [kvng], [KV_LORA_RANK], [BF16], T, 0, "l0_kvnorm")
    tabs = rope_tables(Lc, L)
    spread = rotary_spread()
    w_uq_p = pad_heads(w["mla_w_uq"], QK_DIM)
    w_ukv3 = w["mla_w_ukv"].reshape(KV_LORA_RANK, MLA_HEADS, QK_NOPE_DIM + V_HEAD_DIM)
    w_kn_p = pad_heads(w_ukv3[:, :, :QK_NOPE_DIM].reshape(KV_LORA_RANK, -1), QK_NOPE_DIM)
    w_v = w_ukv3[:, :, QK_NOPE_DIM:].reshape(KV_LORA_RANK, -1)
    qb = q_heads(qn, w_uq_p, tabs)
    kb, vb = kv_heads(kvn, w_kn_p, w_v, kr, spread, tabs)
    o, lse, lse_rows = attn_fwd(qb, kb, vb, Lc)
    (og,) = rowwise_fwd(f_gate, [o, z0], [], [D], [BF16], T, 0, "l0_gate")
    out0 = mm_nn(og, w["mla_w_out"], name="l0_out")
    (X1,) = rowwise_fwd(f_res, [X0, out0], [mod_of(0, 2)], [D], [F32], T, Lc, "l0_res")

    X1p = to_segments(X1, Lc)
    tgt_p = to_segments(target, 0)
    g1 = vec(w["norm_g"][1])
    (H1,) = rowwise_fwd(f_norm_mod, [X1p], [g1, mod_of(1, 1), mod_of(1, 0)], [D], [BF16], T, Lc, "l1_norm")
    p1 = mm_nn(H1, w["s5_w_in"], name="l1_in")
    disc_fn = lambda *a: tuple(zip(*[s5_discretise(a[0][k], a[1][k], a[2][k], a[3][k], a[4][k]) for k in range(2)]))
    disc, disc_vjp = jax.vjp(disc_fn, w["s5_a_re"], w["s5_a_im"], w["s5_log_step"], w["s5_b_re"], w["s5_b_im"])
    dirs = [s5_block_weights(disc[0][k], disc[1][k], disc[2][k], disc[3][k], w["s5_c_re"][k], w["s5_c_im"][k])
            for k in range(2)]
    y_ssm, s5_saved = s5_forward(p1, Lc, dirs)
    d_vec, bg_vec = vec(w["s5_d"]), vec(w["s5_b_glu"])
    u_lat = Rows(p1, D, row_off=Lc, col_blk=0)
    z_lat = Rows(p1, D, row_off=Lc, col_blk=1)
    (ya,) = rowwise_fwd(f_s5_act, [y_ssm, u_lat], [d_vec], [D], [F32], L, 0, "l1_act")
    gl = mm_nn(ya, w["s5_w_glu"], name="l1_glu")
    (y3,) = rowwise_fwd(f_s5_glu, [ya, gl, z_lat], [bg_vec], [D], [BF16], L, 0, "l1_gate")
    out1 = mm_nn(y3, w["s5_w_out"], name="l1_out")
    gt1 = mod[1, 1:2, 2, :][:, None, :]
    x1_lat = Rows(X1p, D, row_off=Lc)
    (X2,) = rowwise_fwd(f_res, [x1_lat, out1], [gt1], [D], [F32], L, 0, "l1_res")

    fg = w["final_g"].reshape(1, D).astype(F32)
    lvec, dX2, d_fg = loss_and_grad(X2, fg, tgt_p)
    loss = jnp.sum(lvec)
    gw = {"final_g": d_fg.reshape(D)}
    dmod = {}

    d_out1, d_gt1 = rowwise_bwd(f_res, [x1_lat, out1], [gt1], [dX2], [1], [0], L, 0, "l1_res_bwd")
    d_y3 = mm_nt(d_out1, w["s5_w_out"], name="l1_out_dx")
    gw["s5_w_out"] = mm_tn(y3, d_out1, name="l1_out_dw")
    d_ya_a, d_gl, d_z1, d_bg = rowwise_bwd(f_s5_glu, [ya, gl, z_lat], [bg_vec], [d_y3], [0, 1, 2], [0], L, 0,
                                           "l1_gate_bwd")
    gw["s5_b_glu"] = d_bg.reshape(-1)
    gw["s5_w_glu"] = mm_tn(ya, d_gl, name="l1_glu_dw")
    d_ya_b = mm_nt(d_gl, w["s5_w_glu"], name="l1_glu_dx")
    d_yssm, d_u_act, d_d = rowwise_bwd(f_s5_act, [y_ssm, u_lat], [d_vec], [d_ya_a], [0, 1], [0], L, 0, "l1_act_bwd",
                                       ct_extra=d_ya_b)
    gw["s5_d"] = d_d.reshape(-1)
    du_p, s5_g = s5_backward(d_yssm, d_u_act, p1, Lc, dirs, s5_saved)
    d_disc = tuple(tuple(s5_g[k][j - 1].reshape(disc[j][k].shape) if j >= 2 else
                         s5_g[k][0][j].reshape(disc[j][k].shape) for k in range(2)) for j in range(4))
    gw["s5_a_re"], gw["s5_a_im"], gw["s5_log_step"], gw["s5_b_re"], gw["s5_b_im"] = disc_vjp(d_disc)
    gw["s5_c_re"] = jnp.stack([s5_g[0][3], s5_g[1][3]])
    gw["s5_c_im"] = jnp.stack([s5_g[0][4], s5_g[1][4]])
    d_p1 = jnp.concatenate([du_p, jnp.concatenate([jnp.zeros((Lc, D), F32), d_z1], axis=0)], axis=1)
    d_H1 = mm_nt(d_p1, w["s5_w_in"], name="l1_in_dx")
    gw["s5_w_in"] = mm_tn(H1, d_p1, name="l1_in_dw")
    d_X1p, d_g1, d_sc1, d_sh1 = rowwise_bwd(f_norm_mod, [X1p], [g1, mod_of(1, 1), mod_of(1, 0)], [d_H1],
                                            [0], [0, 1, 2], T, Lc, "l1_norm_bwd", lat_add=dX2)
    d_gt1_full = jnp.concatenate([jnp.zeros((1, 1, D), F32), d_gt1], axis=0)
    dmod[1] = (d_sh1, d_sc1, d_gt1_full)
    d_X1 = from_segments(d_X1p, Lc)

    d_out0, d_gt0 = rowwise_bwd(f_res, [X0, out0], [mod_of(0, 2)], [d_X1], [1], [0], T, Lc, "l0_res_bwd")
    d_og = mm_nt(d_out0, w["mla_w_out"], name="l0_out_dx")
    gw["mla_w_out"] = mm_tn(og, d_out0, name="l0_out_dw")
    d_o, d_z0 = rowwise_bwd(f_gate, [o, z0], [], [d_og], [0, 1], [], T, 0, "l0_gate_bwd")
    dq_p, delta_rows = attn_bwd_dq(qb, kb, vb, o, d_o, lse, Lc)
    dk_p, d_v = attn_bwd_dkv(qb, kb, vb, d_o, lse_rows, delta_rows, Lc)
    (d_q,) = heads_unrope(dq_p, tabs, name="l0_q_unrope")
    d_k, d_kr = heads_unrope(dk_p, tabs, spread, name="l0_k_unrope")
    d_qn = mm_nt(d_q, w_uq_p, name="l0_uq_dx")
    gw["mla_w_uq"] = unpad_heads(mm_tn(qn, d_q, name="l0_uq_dw"), QK_DIM)
    d_kvn = mm_nt(d_k, w_kn_p, name="l0_ukn_dx") + mm_nt(d_v, w_v, name="l0_uv_dx")
    dw_kn = unpad_heads(mm_tn(kvn, d_k, name="l0_ukn_dw"), QK_NOPE_DIM).reshape(KV_LORA_RANK, MLA_HEADS, QK_NOPE_DIM)
    dw_v = mm_tn(kvn, d_v, name="l0_uv_dw").reshape(KV_LORA_RANK, MLA_HEADS, V_HEAD_DIM)
    gw["mla_w_ukv"] = jnp.concatenate([dw_kn, dw_v], axis=-1).reshape(KV_LORA_RANK, -1)
    d_cq, d_qng = rowwise_bwd(f_rms, [cq], [qng], [d_qn], [0], [0], T, 0, "l0_qnorm_bwd")
    d_ckv, d_kvng = rowwise_bwd(f_rms, [ckv], [kvng], [d_kvn], [0], [0], T, 0, "l0_kvnorm_bwd")
    gw["mla_q_norm"] = d_qng.reshape(-1)
    gw["mla_kv_norm"] = d_kvng.reshape(-1)
    d_p0 = jnp.concatenate([d_z0, d_cq, d_ckv, d_kr, jnp.zeros((T, P0_WIDTH - D - P0_HEAD), F32)], axis=1)
    d_H0 = mm_nt(d_p0, w["mla_w_in"], name="l0_in_dx")
    gw["mla_w_in"] = mm_tn(H0, d_p0, name="l0_in_dw")
    d_X0, d_g0, d_sc0, d_sh0 = rowwise_bwd(f_norm_mod, [X0], [g0, mod_of(0, 1), mod_of(0, 0)], [d_H0],
                                           [0], [0, 1, 2], T, Lc, "l0_norm_bwd", lat_add=d_X1)
    dmod[0] = (d_sh0, d_sc0, d_gt0)
    gw["norm_g"] = jnp.stack([d_g0.reshape(D), d_g1.reshape(D)])
    dx = d_X0[Lc:]
    dmod_arr = jnp.stack([jnp.stack([dmod[i][j][:, 0, :] for j in range(3)], axis=1) for i in range(2)])
    return loss, dx, dmod_arr, gw


SHARDED = {
    "mla_w_in": 1, "mla_w_uq": 1, "mla_w_ukv": 1, "mla_w_out": 0,
    "s5_w_in": 1, "s5_w_glu": 0, "s5_w_out": 0, "s5_d": 0, "s5_b_glu": 0,
}
SHARDED_MATS = ["mla_w_in", "mla_w_uq", "mla_w_ukv", "mla_w_out", "s5_w_in", "s5_w_glu", "s5_w_out"]
SHARDED_VECS = ["s5_d", "s5_b_glu"]
REPLICATED = ["norm_g", "mla_q_norm", "mla_kv_norm", "s5_a_re", "s5_a_im", "s5_log_step", "s5_b_re", "s5_b_im",
              "s5_c_re", "s5_c_im", "final_g"]
WEIGHT_ORDER = ["c_ctx", "ada_w", "ada_b", "norm_g", "mla_w_in", "mla_q_norm", "mla_w_uq", "mla_kv_norm", "mla_w_ukv",
                "mla_w_out", "s5_w_in", "s5_a_re", "s5_a_im", "s5_log_step", "s5_b_re", "s5_b_im", "s5_c_re", "s5_c_im",
                "s5_d", "s5_w_glu", "s5_b_glu", "s5_w_out", "final_g"]


P0_HEAD = Q_LORA_RANK + KV_LORA_RANK + QK_ROPE_DIM


P0_WIDTH = 1536


def w_in_to_kernel_order(w):
    pad = jnp.zeros((w.shape[0], P0_WIDTH - w.shape[1]), w.dtype)
    return jnp.concatenate([w[:, P0_HEAD:], w[:, :P0_HEAD], pad], axis=1)


def w_in_from_kernel_order(w):
    return jnp.concatenate([w[:, D_MODEL:D_MODEL + P0_HEAD], w[:, :D_MODEL]], axis=1)


def gather_weights(ws):
    mats = [ws[n].astype(BF16) for n in SHARDED_MATS]
    outs = gather_halves(mats, "gather_weights")
    chip = _chip_index(_coords())
    full = {}
    for n, own, o in zip(SHARDED_MATS, mats, outs):
        slot = lax.broadcasted_iota(jnp.int32, (N_CHIP, 1, 1), 0)
        o = jnp.where(slot == chip, own[None], o.reshape((N_CHIP,) + own.shape))
        full[n] = o.reshape(-1, o.shape[-1]) if SHARDED[n] == 0 else o.transpose(1, 0, 2).reshape(o.shape[1], -1)
    full["mla_w_in"] = w_in_to_kernel_order(full["mla_w_in"])
    return full


def reduce_gradients(gw):
    me = _coords()
    core = me[2].reshape(1).astype(jnp.int32)
    slots = {}
    for n in SHARDED_MATS:
        g = w_in_from_kernel_order(gw[n]) if n == "mla_w_in" else gw[n]
        if SHARDED[n] == 0:
            slots[n] = g.reshape(N_CHIP, 2, g.shape[0] // (2 * N_CHIP), g.shape[1])
        else:
            k, n4 = g.shape
            slots[n] = g.reshape(k, N_CHIP, n4 // N_CHIP).transpose(1, 0, 2).reshape(N_CHIP, 2, k // 2, n4 // N_CHIP)
    small_names = REPLICATED + SHARDED_VECS
    small, small_offs = pack_flat([gw[n].astype(F32) for n in small_names], F32)
    small = jnp.pad(small, ((0, (-small.shape[0]) % (N_CHIP * 32)), (0, 0)))
    slots["small"] = small.reshape(N_CHIP, 2, -1, 128)
    names = list(slots)
    idx = range(len(names))
    got = exchange(
        [slots[n] for n in names],
        [jax.ShapeDtypeStruct((N_CHIP,) + slots[n].shape[2:], F32) for n in names],
        [(CORE_FLIP, i, lambda me, peer: (slice(None), 1 - me[2]), i, lambda s: None) for i in idx], [],
        "grads_swap_in")
    sums = [pair_add(slots[n], g, core, F32 if n == "small" else BF16, f"grads_pair_{n}") for n, g in zip(names, got)]
    parts = exchange(
        sums, [jax.ShapeDtypeStruct(s.shape, s.dtype) for s in sums],
        [(f, i, lambda me, peer: (_chip_index(peer),), i, lambda s: (_chip_index(s),)) for i in idx for f in CHIP_FLIPS],
        [(i, lambda me: (_chip_index(me),), i, lambda me: (_chip_index(me),)) for i in idx],
        "grads_scatter")
    halves = [sum_chips(p, core, f"grads_sum_{n}") for n, p in zip(names, parts)]
    fulls = exchange(
        halves, [jax.ShapeDtypeStruct(h.shape, F32) for h in halves],
        [(CORE_FLIP, i, lambda me, peer: (me[2],), i, lambda s: (s[2],)) for i in idx], [],
        "grads_swap_out", aliases={i: i for i in idx})
    out = {n: f.reshape(-1, f.shape[-1]) for n, f in zip(names, fulls)}
    (small_all,) = allgather_chips([out.pop("small")], "grads_gather_small")
    vals = unpack_flat(small_all, small_offs, [gw[n].shape for n in small_names])
    for n, v in zip(small_names, vals):
        if n in SHARDED_VECS:
            size = v.shape[0] // N_CHIP
            v = lax.dynamic_slice_in_dim(v, _chip_index(me) * size, size)
        out[n] = v
    return out


def kernel(x, c, ctx, c_ctx, ada_w, ada_b, norm_g, mla_w_in, mla_q_norm, mla_w_uq, mla_kv_norm, mla_w_ukv, mla_w_out, s5_w_in, s5_a_re, s5_a_im, s5_log_step, s5_b_re, s5_b_im, s5_c_re, s5_c_im, s5_d, s5_w_glu, s5_b_glu, s5_w_out, final_g, loss_target, m_c_ctx, m_ada_w, m_ada_b, m_norm_g, m_mla_w_in, m_mla_q_norm, m_mla_w_uq, m_mla_kv_norm, m_mla_w_ukv, m_mla_w_out, m_s5_w_in, m_s5_a_re, m_s5_a_im, m_s5_log_step, m_s5_b_re, m_s5_b_im, m_s5_c_re, m_s5_c_im, m_s5_d, m_s5_w_glu, m_s5_b_glu, m_s5_w_out, m_final_g, v_c_ctx, v_ada_w, v_ada_b, v_norm_g, v_mla_w_in, v_mla_q_norm, v_mla_w_uq, v_mla_kv_norm, v_mla_w_ukv, v_mla_w_out, v_s5_w_in, v_s5_a_re, v_s5_a_im, v_s5_log_step, v_s5_b_re, v_s5_b_im, v_s5_c_re, v_s5_c_im, v_s5_d, v_s5_w_glu, v_s5_b_glu, v_s5_w_out, v_final_g):
    args = dict(locals())
    weights = {n: args[n] for n in WEIGHT_ORDER}
    D = D_MODEL
    xi, yi, ci = _coords()
    chip = 2 * xi + yi
    me = 4 * xi + 2 * yi + ci
    n_col = ada_w.shape[2]

    c_all = allgather_devices(jnp.pad(c, ((0, 7), (0, 0))), "gather_c")[:, 0, :]
    cond = jnp.concatenate([c_all, jnp.broadcast_to(c_ctx[None], (8, D))], axis=0)
    (s_cond,) = rowwise_fwd(lambda v: (_silu(v),), [cond], [], [D], [F32], 16, 0, "cond_silu")
    mod_cols = jnp.stack([mm_nn(s_cond, ada_w[i], name=f"mod_proj{i}") for i in range(2)])
    vec_tiles = [jnp.pad(weights[n][0].reshape(-1, 128), ((0, 6), (0, 0))) for n in SHARDED_VECS]
    mod_all, *vec_all = allgather_chips([mod_cols] + vec_tiles, "gather_mod")
    mod_all = mod_all.transpose(1, 2, 0, 3).reshape(2, 16, 3 * D) + ada_b[:, None, :]
    mod_l = lax.dynamic_index_in_dim(mod_all, me, axis=1, keepdims=False)
    mod_c = mod_all[:, 8, :]
    mod = jnp.stack([mod_c.reshape(2, 3, D), mod_l.reshape(2, 3, D)], axis=1)

    w = gather_weights({n: weights[n][0] for n in SHARDED_MATS})
    for n, v in zip(SHARDED_VECS, vec_all):
        w[n] = v[:, :2, :].reshape(-1)
    for n in ["norm_g", "final_g"]:
        w[n] = weights[n]
    for n in ["mla_q_norm", "mla_kv_norm", "s5_a_re", "s5_a_im", "s5_log_step", "s5_b_re", "s5_b_im",
              "s5_c_re", "s5_c_im"]:
        w[n] = weights[n][0]

    loss_me, dx, dmod, gw = local_step(x[0], ctx[0], loss_target[0], mod, w)

    dmod_rows, loss_all = _gather([dmod.reshape(2, 2, 3 * D), jnp.broadcast_to(loss_me, (8, 128))],
                                  ALL_FLIPS, _dev_index, N_DEV, "gather_dmod")
    loss = functools.reduce(lambda s, d: s + loss_all[d, 0, 0], range(1, N_DEV), loss_all[0, 0, 0])
    dm = jnp.concatenate([dmod_rows[:, :, 1, :], dmod_rows[:, :, 0, :]], axis=0).transpose(1, 0, 2)
    g_ada_b = jnp.sum(dm, axis=1)
    dm_cols = lax.dynamic_slice_in_dim(dm, chip * n_col, n_col, axis=2)
    g_ada_w = jnp.stack([mm_tn(s_cond, dm_cols[i], name=f"mod_proj_dw{i}") for i in range(2)])
    dmc = jnp.sum(dm_cols[:, 8:, :], axis=1)
    dmc8 = jnp.broadcast_to(dmc[:, None, :], (2, 8, n_col))
    g_sc = mm_nt(dmc8[0], ada_w[0], name="mod_proj_dx0")[0] + mm_nt(dmc8[1], ada_w[1], name="mod_proj_dx1")[0]
    g_sc_all = allgather_devices(jnp.broadcast_to(g_sc[None], (8, D)), "gather_dcond")[:, 0, :]
    g_silu_cc = g_sc_all[0] + g_sc_all[2] + g_sc_all[4] + g_sc_all[6]
    (g_c_ctx,) = rowwise_bwd(lambda v: (_silu(v),), [jnp.broadcast_to(c_ctx[None], (8, D))], [],
                             [jnp.broadcast_to(g_silu_cc[None], (8, D))], [0], [], 8, 0, "cond_silu_bwd")
    g_c_ctx = g_c_ctx[0]

    gw_in = {}
    for n in SHARDED_MATS + SHARDED_VECS:
        gw_in[n] = gw[n]
    for n in REPLICATED:
        gw_in[n] = gw[n]
    red = reduce_gradients(gw_in)
    grads = {"c_ctx": g_c_ctx, "ada_w": g_ada_w, "ada_b": g_ada_b}
    for n in WEIGHT_ORDER[3:]:
        grads[n] = red[n].reshape(weights[n].shape)

    deltas, new_m, new_v = {}, {}, {}
    small = [n for n in WEIGHT_ORDER if weights[n].size < 50000]
    for n in WEIGHT_ORDER:
        if n in small:
            continue
        shp = weights[n].shape
        w2 = weights[n].reshape(-1, shp[-1])
        d_, m_, v_ = adamw(w2, grads[n].reshape(w2.shape), args["m_" + n].reshape(w2.shape),
                           args["v_" + n].reshape(w2.shape), name=f"adamw_{n}")
        deltas[n], new_m[n], new_v[n] = d_.reshape(shp), m_.reshape(shp), v_.reshape(shp)
    packs = []
    offs = None
    for src in (weights, grads, {n: args["m_" + n] for n in small}, {n: args["v_" + n] for n in small}):
        buf, offs = pack_flat([src[n] for n in small], F32)
        packs.append(buf)
    outs = adamw(*packs, name="adamw_small")
    for res, dst in zip(outs, (deltas, new_m, new_v)):
        for n, val in zip(small, unpack_flat(res, offs, [weights[n].shape for n in small])):
            dst[n] = val

    return (loss, dx[None], *[grads[n] for n in WEIGHT_ORDER], *[deltas[n] for n in WEIGHT_ORDER],
            *[new_m[n] for n in WEIGHT_ORDER], *[new_v[n] for n in WEIGHT_ORDER])
```

```python
import functools
import math

import jax
import jax.numpy as jnp
import numpy as np
from jax import lax
from jax.experimental import pallas as pl
from jax.experimental.pallas import tpu as pltpu

F32 = jnp.float32
BF16 = jnp.bfloat16

D_MODEL = 1024
GRID_W = 64
EPS = 1e-6
MLA_HEADS = 16
QK_NOPE_DIM = 64
QK_ROPE_DIM = 32
V_HEAD_DIM = 64
Q_LORA_RANK = 256
KV_LORA_RANK = 128
QK_DIM = QK_NOPE_DIM + QK_ROPE_DIM
SOFTMAX_SCALE = QK_DIM ** -0.5
ROPE_THETA = 10000.0
S5_GROUP = 16
S5_GROUPS = D_MODEL // S5_GROUP
S5_STATE = 64
S5_LANES = S5_GROUPS * S5_STATE
N_SEG = 8
GROUPS_PER_BLOCK = 8
N_BLOCKS = S5_GROUPS // GROUPS_PER_BLOCK
BLK_CH = GROUPS_PER_BLOCK * S5_GROUP
BLK_ST = GROUPS_PER_BLOCK * S5_STATE

ADAM_LR = 0.001
ADAM_B1 = 0.9
ADAM_B2 = 0.999
ADAM_EPS = 1e-08
ADAM_WD = 0.01
ADAM_STEP = 10

N_DEV = 8
N_CHIP = 4
MESH = pl.DeviceIdType.MESH
VMEM_LIMIT = 52 * 1024 * 1024
ROW_TILE = 256


def _params(sem=None, vmem=None):
    return pltpu.CompilerParams(dimension_semantics=sem, vmem_limit_bytes=vmem)


def mm_nn(a, b, out_dtype=F32, name="mm_nn"):
    M, K = a.shape
    N = b.shape[1]
    tm = math.gcd(ROW_TILE, M)

    def body(a_ref, b_ref, o_ref):
        o_ref[...] = jnp.dot(a_ref[...].astype(BF16), b_ref[...].astype(BF16),
                             preferred_element_type=F32).astype(o_ref.dtype)

    return pl.pallas_call(
        body, out_shape=jax.ShapeDtypeStruct((M, N), out_dtype), grid=(M // tm,),
        in_specs=[pl.BlockSpec((tm, K), lambda i: (i, 0)), pl.BlockSpec((K, N), lambda i: (0, 0))],
        out_specs=pl.BlockSpec((tm, N), lambda i: (i, 0)),
        compiler_params=_params(("parallel",), VMEM_LIMIT), name=name)(a, b)


def mm_nt(a, b, out_dtype=F32, name="mm_nt"):
    M, N = a.shape
    K = b.shape[0]
    tm = math.gcd(ROW_TILE, M)

    def body(a_ref, b_ref, o_ref):
        o_ref[...] = lax.dot_general(a_ref[...].astype(BF16), b_ref[...].astype(BF16),
                                     (((1,), (1,)), ((), ())),
                                     preferred_element_type=F32).astype(o_ref.dtype)

    return pl.pallas_call(
        body, out_shape=jax.ShapeDtypeStruct((M, K), out_dtype), grid=(M // tm,),
        in_specs=[pl.BlockSpec((tm, N), lambda i: (i, 0)), pl.BlockSpec((K, N), lambda i: (0, 0))],
        out_specs=pl.BlockSpec((tm, K), lambda i: (i, 0)),
        compiler_params=_params(("parallel",), VMEM_LIMIT), name=name)(a, b)


def mm_nt_sum(terms, w, n_rows, name):
    K = w.shape[0]
    tm = math.gcd(ROW_TILE, n_rows, *[t[2] for t in terms])

    def body(*refs):
        i = pl.program_id(0)
        w_ref, o_ref = refs[len(terms)], refs[len(terms) + 1]
        acc = None
        for a_ref, (a, off, first) in zip(refs, terms):
            part = lax.dot_general(a_ref[...].astype(BF16), w_ref[:, off:off + a.shape[1]].astype(BF16), NT_DIMS,
                                   preferred_element_type=F32)
            if first:
                part = jnp.where(i >= first // tm, part, 0.0)
            acc = part if acc is None else acc + part
        o_ref[...] = acc

    def a_spec(a, first):
        skip = first // tm
        return pl.BlockSpec((tm, a.shape[1]), lambda i: (jnp.maximum(i - skip, 0), 0))

    return pl.pallas_call(
        body, out_shape=jax.ShapeDtypeStruct((n_rows, K), F32), grid=(n_rows // tm,),
        in_specs=[a_spec(a, first) for a, _, first in terms] + [pl.BlockSpec(w.shape, lambda i: (0, 0))],
        out_specs=pl.BlockSpec((tm, K), lambda i: (i, 0)),
        compiler_params=_params(("parallel",), VMEM_LIMIT), name=name)(*[t[0] for t in terms], w)


def mm_tn(a, b, name="mm_tn"):
    M, K = a.shape
    N = b.shape[1]
    tn = math.gcd(512, N) if N % 128 == 0 and N > 512 else N

    def body(a_ref, b_ref, o_ref):
        o_ref[...] = lax.dot_general(a_ref[...].astype(BF16), b_ref[...].astype(BF16),
                                     (((0,), (0,)), ((), ())), preferred_element_type=F32)

    return pl.pallas_call(
        body, out_shape=jax.ShapeDtypeStruct((K, N), F32), grid=(N // tn,),
        in_specs=[pl.BlockSpec((M, K), lambda j: (0, 0)), pl.BlockSpec((M, tn), lambda j: (0, j))],
        out_specs=pl.BlockSpec((K, tn), lambda j: (0, j)),
        compiler_params=_params(("parallel",), VMEM_LIMIT), name=name)(a, b)


class Rows:
    def __init__(self, arr, width=None, row_off=0, col_blk=0):
        self.arr = arr
        self.width = arr.shape[1] if width is None else width
        self.row_off = row_off
        self.col_blk = col_blk

    def spec(self, tm):
        ro, cb = self.row_off // tm, self.col_blk
        return pl.BlockSpec((tm, self.width), lambda i: (i + ro, cb))


def _as_rows(x):
    return x if isinstance(x, Rows) else Rows(x)


def _row_tile(n_rows, n_ctx_rows, rows):
    tm = math.gcd(ROW_TILE, n_rows, n_ctx_rows)
    for r in rows:
        tm = math.gcd(tm, r.row_off)
    return tm


def _bc_spec(arr, n_ctx_blocks):
    g, _, d = arr.shape
    if g == 1:
        return pl.BlockSpec((1, 1, d), lambda i: (0, 0, 0))
    return pl.BlockSpec((1, 1, d), lambda i: ((i >= n_ctx_blocks).astype(jnp.int32), 0, 0))


def rowwise_fwd(fn, rows, bcs, out_dims, out_dtypes, n_rows, n_ctx_rows, name):
    rows = [_as_rows(r) for r in rows]
    tm = _row_tile(n_rows, n_ctx_rows, rows)
    ncb = n_ctx_rows // tm
    nr, nb = len(rows), len(bcs)

    def body(*refs):
        vals = [r[...].astype(F32) for r in refs[:nr]] + [b[0].astype(F32) for b in refs[nr:nr + nb]]
        outs = fn(*vals)
        for o_ref, v in zip(refs[nr + nb:], outs):
            o_ref[...] = v.astype(o_ref.dtype)

    outs = pl.pallas_call(
        body,
        out_shape=[jax.ShapeDtypeStruct((n_rows, d), dt) for d, dt in zip(out_dims, out_dtypes)],
        grid=(n_rows // tm,),
        in_specs=[r.spec(tm) for r in rows] + [_bc_spec(b, ncb) for b in bcs],
        out_specs=[pl.BlockSpec((tm, d), lambda i: (i, 0)) for d in out_dims],
        compiler_params=_params(("parallel",), VMEM_LIMIT), name=name)(*[r.arr for r in rows], *bcs)
    return outs


def rowwise_bwd(fn, rows, bcs, cts, diff_rows, diff_bcs, n_rows, n_ctx_rows, name, ct_extra=None, lat_add=None):
    rows = [_as_rows(r) for r in rows]
    cts = [_as_rows(c) for c in cts]
    extra = [_as_rows(ct_extra)] if ct_extra is not None else []
    tm = _row_tile(n_rows, n_ctx_rows, rows + cts + extra)
    ncb = n_ctx_rows // tm
    nr, nb, nc = len(rows), len(bcs), len(cts)
    ndr, ndb = len(diff_rows), len(diff_bcs)
    n_in = nr + nb + nc + len(extra) + (lat_add is not None)

    def body(*refs):
        i = pl.program_id(0)
        rvals = [r[...].astype(F32) for r in refs[:nr]]
        bvals = [b[0].astype(F32) for b in refs[nr:nr + nb]]
        cvals = [c[...].astype(F32) for c in refs[nr + nb:nr + nb + nc]]
        if extra:
            cvals[0] = cvals[0] + refs[nr + nb + nc][...].astype(F32)
        outs = refs[n_in:]

        def f(*d):
            rv, bv = list(rvals), list(bvals)
            for k, idx in enumerate(diff_rows):
                rv[idx] = d[k]
            for k, idx in enumerate(diff_bcs):
                bv[idx] = d[ndr + k]
            return tuple(fn(*rv, *bv))

        primals = [rvals[k] for k in diff_rows] + [bvals[k] for k in diff_bcs]
        _, vjp = jax.vjp(f, *primals)
        grads = list(vjp(tuple(cvals)))
        if lat_add is not None:
            add = refs[n_in - 1][...]
            grads[0] = grads[0] + (add if lat_add.shape[0] == n_rows else jnp.where(i >= ncb, add, 0.0))
        for k in range(ndr):
            outs[k][...] = grads[k].astype(outs[k].dtype)
        for k, idx in enumerate(diff_bcs):
            o_ref = outs[ndr + k]
            first = (i == 0)
            if bcs[idx].shape[0] == 2:
                first = first | (i == ncb)

            @pl.when(first)
            def _(o_ref=o_ref):
                o_ref[...] = jnp.zeros_like(o_ref)

            o_ref[0] += grads[ndr + k]

    out_shape = [jax.ShapeDtypeStruct((n_rows, rows[k].width), F32) for k in diff_rows]
    out_shape += [jax.ShapeDtypeStruct(bcs[k].shape, F32) for k in diff_bcs]
    out_specs = [pl.BlockSpec((tm, rows[k].width), lambda i: (i, 0)) for k in diff_rows]
    out_specs += [_bc_spec(bcs[k], ncb) for k in diff_bcs]
    ins = [r.arr for r in rows] + list(bcs) + [c.arr for c in cts + extra]
    in_specs = [r.spec(tm) for r in rows] + [_bc_spec(b, ncb) for b in bcs] + [c.spec(tm) for c in cts + extra]
    if lat_add is not None:
        ins.append(lat_add)
        skip = ncb if lat_add.shape[0] != n_rows else 0
        in_specs.append(pl.BlockSpec((tm, lat_add.shape[1]), lambda i: (jnp.maximum(i - skip, 0), 0)))
    outs = pl.pallas_call(
        body, out_shape=out_shape, grid=(n_rows // tm,), in_specs=in_specs, out_specs=out_specs,
        compiler_params=_params(("arbitrary",), VMEM_LIMIT), name=name)(*ins)
    return outs


def _rms(x):
    return x * lax.rsqrt(jnp.mean(x * x, axis=-1, keepdims=True) + EPS)


def _sigmoid(x):
    return 0.5 * (jnp.tanh(0.5 * x) + 1.0)


def _silu(x):
    return x * _sigmoid(x)


def _gelu_tanh(x):
    return 0.5 * x * (1.0 + jnp.tanh(math.sqrt(2.0 / math.pi) * (x + 0.044715 * (x * x * x))))


def f_norm_mod(x, g, sc, sh):
    return ((_rms(x) * g) * (1.0 + sc) + sh,)


def f_rms(x, g):
    return (_rms(x) * g,)


def f_gate(o, z):
    return (o * _silu(z),)


def f_res(x, o, gt):
    return (x + gt * o,)


def f_s5_act(y, u, d):
    return (_gelu_tanh(y + d * u),)


def f_s5_glu(ya, gl, z, b):
    return (ya * _sigmoid(gl + b) * _silu(z),)


def s5_tail(y_ssm, p1, x1p, target, n_ctx, d_vec, b_glu, gate, final_g, w_glu, w_out, name="l1_tail"):
    n, d = y_ssm.shape
    tm = math.gcd(ROW_TILE, n, n_ctx)
    off = n_ctx // tm
    tn_dims = (((0,), (0,)), ((), ()))

    def row_loss(x, g, t):
        e = _rms(x) * g - t
        return 0.5 * (e * e) * (1.0 / d)

    def body(y_ref, u_ref, z_ref, x1_ref, t_ref, d_ref, b_ref, gt_ref, fg_ref, wg_ref, wo_ref,
             l_ref, dx_ref, dy_ref, du_ref, dz_ref, dfg_ref, dgt_ref, db_ref, dd_ref, dwg_ref, dwo_ref):
        @pl.when(pl.program_id(0) == 0)
        def _():
            for r in (l_ref, dfg_ref, dgt_ref, db_ref, dd_ref, dwg_ref, dwo_ref):
                r[...] = jnp.zeros_like(r)

        u, z, tgt, gt = u_ref[...], z_ref[...], t_ref[...], gt_ref[...]
        (ya,), act_vjp = jax.vjp(lambda y_, u_, d_: f_s5_act(y_, u_, d_), y_ref[...], u, d_ref[...])
        ya16 = ya.astype(BF16)
        gl = jnp.dot(ya16, wg_ref[...], preferred_element_type=F32)
        (y3,), glu_vjp = jax.vjp(lambda a_, g_, z_, b_: f_s5_glu(a_, g_, z_, b_), ya, gl, z, b_ref[...])
        y3_16 = y3.astype(BF16)
        out1 = jnp.dot(y3_16, wo_ref[...], preferred_element_type=F32)
        lterm, loss_vjp = jax.vjp(lambda x_, g_: row_loss(x_, g_, tgt), x1_ref[...] + gt * out1, fg_ref[...])
        dx2, dfg = loss_vjp(jnp.ones_like(lterm))
        l_ref[...] += jnp.sum(lterm, axis=0, keepdims=True)
        dfg_ref[...] += dfg
        dx_ref[...] = dx2
        dgt_ref[...] += jnp.sum(dx2 * out1, axis=0, keepdims=True)
        d_out16 = (gt * dx2).astype(BF16)
        dwo_ref[...] += lax.dot_general(y3_16, d_out16, tn_dims, preferred_element_type=F32)
        d_y3 = lax.dot_general(d_out16, wo_ref[...], NT_DIMS, preferred_element_type=F32)
        d_ya, d_gl, d_z, d_b = glu_vjp((d_y3,))
        dz_ref[...] = d_z
        db_ref[...] += d_b
        d_gl16 = d_gl.astype(BF16)
        dwg_ref[...] += lax.dot_general(ya16, d_gl16, tn_dims, preferred_element_type=F32)
        d_ya = d_ya + lax.dot_general(d_gl16, wg_ref[...], NT_DIMS, preferred_element_type=F32)
        d_y, d_u, d_d = act_vjp((d_ya,))
        dy_ref[...] = d_y
        du_ref[...] = d_u
        dd_ref[...] += d_d

    row = pl.BlockSpec((tm, d), lambda i: (i, 0))
    vecs = pl.BlockSpec((1, d), lambda i: (0, 0))
    mat = pl.BlockSpec((d, d), lambda i: (0, 0))
    return pl.pallas_call(
        body,
        out_shape=[jax.ShapeDtypeStruct((1, d), F32)] + [jax.ShapeDtypeStruct((n, d), F32)] * 4
        + [jax.ShapeDtypeStruct((1, d), F32)] * 4 + [jax.ShapeDtypeStruct((d, d), F32)] * 2,
        grid=(n // tm,),
        in_specs=[row, pl.BlockSpec((tm, d), lambda i: (i + off, 0)), pl.BlockSpec((tm, d), lambda i: (i + off, 1)),
                  pl.BlockSpec((tm, d), lambda i: (i + off, 0)), row, vecs, vecs, vecs, vecs, mat, mat],
        out_specs=[vecs, row, row, row, row, vecs, vecs, vecs, vecs, mat, mat],
        compiler_params=_params(("arbitrary",), VMEM_LIMIT), name=name)(
            y_ssm, p1, p1, x1p, target, d_vec, b_glu, gate, final_g, w_glu, w_out)


NT_DIMS = (((1,), (1,)), ((), ()))
HEAD_LANES = 128
N_PAIRS = MLA_HEADS // 2


def _own_lanes(shape, hh):
    lane = lax.broadcasted_iota(jnp.int32, shape, len(shape) - 1)
    return (lane < V_HEAD_DIM) if hh == 0 else (lane >= V_HEAD_DIM)


def _rope_tiles(x, cos, sin_next, sin_prev, inverse):
    width = x.shape[-1]
    reps = width // HEAD_LANES
    c, sn, sp = (jnp.tile(t, (1, reps)) for t in (cos, sin_next, sin_prev))
    if inverse:
        return x * c + pltpu.roll(x * sn, 8, 1) + pltpu.roll(x * sp, width - 8, 1)
    return x * c + pltpu.roll(x, width - 8, 1) * sn + pltpu.roll(x, 8, 1) * sp


def _col_to_row(col):
    n = col.shape[0]
    hi = col.astype(BF16)
    r1 = col - hi.astype(F32)
    mid = r1.astype(BF16)
    lo = (r1 - mid.astype(F32)).astype(BF16)
    lane = lax.broadcasted_iota(jnp.int32, (n, HEAD_LANES), 1)
    terms = jnp.where(lane == 0, hi, jnp.where(lane == 1, mid, jnp.where(lane == 2, lo, jnp.zeros_like(hi))))
    eye = (lax.broadcasted_iota(jnp.int32, (n, n), 0) == lax.broadcasted_iota(jnp.int32, (n, n), 1)).astype(BF16)
    rows = lax.dot_general(terms, eye, (((0,), (0,)), ((), ())), preferred_element_type=F32)
    return rows[0:1] + rows[1:2] + rows[2:3]


def attn_fwd(qb, kb, vb, n_ctx):
    T = qb.shape[0]
    tq = math.gcd(ROW_TILE, n_ctx)
    nq, ncb = T // tq, n_ctx // tq

    def body(q_ref, k_ref, v_ref, o_ref, lse_ref, lse_row_ref):
        qi = pl.program_id(1)

        def rows(n_keys):
            v = v_ref[:n_keys, :]
            outs = []
            for hh in range(2):
                hs = slice(hh * HEAD_LANES, (hh + 1) * HEAD_LANES)
                s = lax.dot_general(q_ref[:, hs], k_ref[:n_keys, hs], NT_DIMS,
                                    preferred_element_type=F32) * SOFTMAX_SCALE
                m = jnp.max(s, axis=-1, keepdims=True)
                p = jnp.exp(s - m)
                l = jnp.sum(p, axis=-1, keepdims=True)
                outs.append(jnp.dot(p.astype(BF16), v, preferred_element_type=F32) / l)
                lse = m + jnp.log(l)
                lse_ref[hh] = lse
                lse_row_ref[hh] = _col_to_row(lse)
            o_ref[...] = jnp.where(_own_lanes(outs[0].shape, 0), outs[0], outs[1])

        pl.when(qi < ncb)(lambda: rows(n_ctx))
        pl.when(qi >= ncb)(lambda: rows(T))

    return pl.pallas_call(
        body,
        out_shape=[jax.ShapeDtypeStruct((T, MLA_HEADS * V_HEAD_DIM), F32),
                   jax.ShapeDtypeStruct((MLA_HEADS, T, 1), F32), jax.ShapeDtypeStruct((MLA_HEADS, 1, T), F32)],
        grid=(N_PAIRS, nq),
        in_specs=[pl.BlockSpec((tq, 2 * HEAD_LANES), lambda h, i: (i, h)),
                  pl.BlockSpec((T, 2 * HEAD_LANES), lambda h, i: (0, h)),
                  pl.BlockSpec((T, 2 * V_HEAD_DIM), lambda h, i: (0, h))],
        out_specs=[pl.BlockSpec((tq, 2 * V_HEAD_DIM), lambda h, i: (i, h)),
                   pl.BlockSpec((2, tq, 1), lambda h, i: (h, i, 0)),
                   pl.BlockSpec((2, 1, tq), lambda h, i: (h, 0, i))],
        compiler_params=_params(("parallel", "parallel"), VMEM_LIMIT), name="attn_fwd")(qb, kb, vb)


def attn_bwd_dq(qb, kb, vb, o, do, lse, n_ctx):
    T = qb.shape[0]
    tq = math.gcd(ROW_TILE, n_ctx)
    nq, ncb = T // tq, n_ctx // tq

    def body(q_ref, k_ref, v_ref, o_ref, do_ref, lse_ref, dq_ref, delta_ref):
        qi = pl.program_id(1)

        def rows(n_keys):
            v = v_ref[:n_keys, :]
            for hh in range(2):
                hs = slice(hh * HEAD_LANES, (hh + 1) * HEAD_LANES)
                k = k_ref[:n_keys, hs]
                do = jnp.where(_own_lanes(do_ref.shape, hh), do_ref[...], 0.0)
                delta = jnp.sum(do * o_ref[...], axis=-1, keepdims=True)
                s = lax.dot_general(q_ref[:, hs], k, NT_DIMS, preferred_element_type=F32) * SOFTMAX_SCALE
                p = jnp.exp(s - lse_ref[hh])
                dp = lax.dot_general(do.astype(BF16), v, NT_DIMS, preferred_element_type=F32)
                ds = p * (dp - delta) * SOFTMAX_SCALE
                dq_ref[:, hs] = jnp.dot(ds.astype(BF16), k, preferred_element_type=F32)
                delta_ref[hh] = _col_to_row(delta)

        pl.when(qi < ncb)(lambda: rows(n_ctx))
        pl.when(qi >= ncb)(lambda: rows(T))

    return pl.pallas_call(
        body,
        out_shape=[jax.ShapeDtypeStruct((T, MLA_HEADS * HEAD_LANES), F32),
                   jax.ShapeDtypeStruct((MLA_HEADS, 1, T), F32)],
        grid=(N_PAIRS, nq),
        in_specs=[pl.BlockSpec((tq, 2 * HEAD_LANES), lambda h, i: (i, h)),
                  pl.BlockSpec((T, 2 * HEAD_LANES), lambda h, i: (0, h)),
                  pl.BlockSpec((T, 2 * V_HEAD_DIM), lambda h, i: (0, h)),
                  pl.BlockSpec((tq, 2 * V_HEAD_DIM), lambda h, i: (i, h)),
                  pl.BlockSpec((tq, 2 * V_HEAD_DIM), lambda h, i: (i, h)),
                  pl.BlockSpec((2, tq, 1), lambda h, i: (h, i, 0))],
        out_specs=[pl.BlockSpec((tq, 2 * HEAD_LANES), lambda h, i: (i, h)),
                   pl.BlockSpec((2, 1, tq), lambda h, i: (h, 0, i))],
        compiler_params=_params(("parallel", "parallel"), VMEM_LIMIT), name="attn_bwd_dq")(
            qb, kb, vb, o, do, lse)


def attn_bwd_dkv(qb, kb, vb, do, lse_rows, delta_rows, n_ctx):
    T = qb.shape[0]
    tq = math.gcd(ROW_TILE, n_ctx)
    nq, ncb = T // tq, n_ctx // tq

    def body(q_ref, do_ref, lse_ref, delta_ref, k_ref, v_ref, dk_ref, dv_ref):
        kj = pl.program_id(1)

        def cols(first):
            v = v_ref[...]
            do_all = do_ref[first:, :]
            dv = None
            for hh in range(2):
                hs = slice(hh * HEAD_LANES, (hh + 1) * HEAD_LANES)
                k = k_ref[:, hs]
                q = q_ref[first:, hs]
                do16 = jnp.where(_own_lanes(do_all.shape, hh), do_all, 0.0).astype(BF16)
                st = lax.dot_general(k, q, NT_DIMS, preferred_element_type=F32) * SOFTMAX_SCALE
                pt = jnp.exp(st - lse_ref[hh, :, first:])
                dv_h = jnp.dot(pt.astype(BF16), do16, preferred_element_type=F32)
                dv = dv_h if dv is None else dv + dv_h
                dpt = lax.dot_general(v, do16, NT_DIMS, preferred_element_type=F32)
                dst = pt * (dpt - delta_ref[hh, :, first:]) * SOFTMAX_SCALE
                dk_ref[:, hs] = jnp.dot(dst.astype(BF16), q, preferred_element_type=F32)
            dv_ref[...] = dv

        pl.when(kj < ncb)(lambda: cols(0))
        pl.when(kj >= ncb)(lambda: cols(n_ctx))

    return pl.pallas_call(
        body,
        out_shape=[jax.ShapeDtypeStruct((T, MLA_HEADS * HEAD_LANES), F32),
                   jax.ShapeDtypeStruct((T, MLA_HEADS * V_HEAD_DIM), F32)],
        grid=(N_PAIRS, nq),
        in_specs=[pl.BlockSpec((T, 2 * HEAD_LANES), lambda h, j: (0, h)),
                  pl.BlockSpec((T, 2 * V_HEAD_DIM), lambda h, j: (0, h)),
                  pl.BlockSpec((2, 1, T), lambda h, j: (h, 0, 0)),
                  pl.BlockSpec((2, 1, T), lambda h, j: (h, 0, 0)),
                  pl.BlockSpec((tq, 2 * HEAD_LANES), lambda h, j: (j, h)),
                  pl.BlockSpec((tq, 2 * V_HEAD_DIM), lambda h, j: (j, h))],
        out_specs=[pl.BlockSpec((tq, 2 * HEAD_LANES), lambda h, j: (j, h)),
                   pl.BlockSpec((tq, 2 * V_HEAD_DIM), lambda h, j: (j, h))],
        compiler_params=_params(("parallel", "parallel"), VMEM_LIMIT), name="attn_bwd_dkv")(
            qb, do, lse_rows, delta_rows, kb, vb)


def _split_bf16(x):
    hi = x.astype(BF16)
    return hi, (x - hi.astype(F32)).astype(BF16)


def q_heads(qn, w_uq_p, tabs, name="l0_uq"):
    T, K = qn.shape
    N = w_uq_p.shape[1]
    tm = math.gcd(ROW_TILE, T)

    def body(a_ref, w_ref, c_ref, sn_ref, sp_ref, o_ref):
        acc = jnp.dot(a_ref[...], w_ref[...], preferred_element_type=F32)
        o_ref[...] = _rope_tiles(acc, c_ref[...], sn_ref[...], sp_ref[...], False).astype(BF16)

    tab = pl.BlockSpec((tm, HEAD_LANES), lambda i: (i, 0))
    return pl.pallas_call(
        body, out_shape=jax.ShapeDtypeStruct((T, N), BF16), grid=(T // tm,),
        in_specs=[pl.BlockSpec((tm, K), lambda i: (i, 0)), pl.BlockSpec((K, N), lambda i: (0, 0)), tab, tab, tab],
        out_specs=pl.BlockSpec((tm, N), lambda i: (i, 0)),
        compiler_params=_params(("parallel",), VMEM_LIMIT), name=name)(qn, w_uq_p, *tabs)


def kv_heads(kvn, w_kn_p, w_v, kr, spread, tabs, name="l0_ukv"):
    T, K = kvn.shape
    N = w_kn_p.shape[1]
    NV = w_v.shape[1]
    tm = math.gcd(ROW_TILE, T)

    def body(a_ref, wk_ref, wv_ref, kr_ref, e_ref, c_ref, sn_ref, sp_ref, k_ref, v_ref):
        a = a_ref[...]
        hi, lo = _split_bf16(kr_ref[...])
        acc = (jnp.dot(a, wk_ref[...], preferred_element_type=F32)
               + jnp.dot(hi, e_ref[...], preferred_element_type=F32)
               + jnp.dot(lo, e_ref[...], preferred_element_type=F32))
        k_ref[...] = _rope_tiles(acc, c_ref[...], sn_ref[...], sp_ref[...], False).astype(BF16)
        v_ref[...] = jnp.dot(a, wv_ref[...], preferred_element_type=F32).astype(BF16)

    tab = pl.BlockSpec((tm, HEAD_LANES), lambda i: (i, 0))
    return pl.pallas_call(
        body, out_shape=[jax.ShapeDtypeStruct((T, N), BF16), jax.ShapeDtypeStruct((T, NV), BF16)], grid=(T // tm,),
        in_specs=[pl.BlockSpec((tm, K), lambda i: (i, 0)), pl.BlockSpec((K, N), lambda i: (0, 0)),
                  pl.BlockSpec((K, NV), lambda i: (0, 0)), pl.BlockSpec((tm, QK_ROPE_DIM), lambda i: (i, 0)),
                  pl.BlockSpec((QK_ROPE_DIM, N), lambda i: (0, 0)), tab, tab, tab],
        out_specs=[pl.BlockSpec((tm, N), lambda i: (i, 0)), pl.BlockSpec((tm, NV), lambda i: (i, 0))],
        compiler_params=_params(("parallel",), VMEM_LIMIT), name=name)(kvn, w_kn_p, w_v, kr, spread, *tabs)


def heads_unrope(d, tabs, spread=None, name="unrope"):
    T, N = d.shape
    tm = math.gcd(ROW_TILE, T)

    def body(*refs):
        if spread is None:
            d_ref, c_ref, sn_ref, sp_ref, o_ref = refs
        else:
            d_ref, c_ref, sn_ref, sp_ref, e_ref, o_ref, kr_ref = refs
        g = _rope_tiles(d_ref[...], c_ref[...], sn_ref[...], sp_ref[...], True)
        o_ref[...] = g.astype(BF16)
        if spread is not None:
            hi, lo = _split_bf16(g)
            kr_ref[...] = (lax.dot_general(hi, e_ref[...], NT_DIMS, preferred_element_type=F32)
                           + lax.dot_general(lo, e_ref[...], NT_DIMS, preferred_element_type=F32))

    tab = pl.BlockSpec((tm, HEAD_LANES), lambda i: (i, 0))
    row = pl.BlockSpec((tm, N), lambda i: (i, 0))
    ins, in_specs = [d, *tabs], [row, tab, tab, tab]
    out_shape, out_specs = [jax.ShapeDtypeStruct((T, N), BF16)], [row]
    if spread is not None:
        ins.append(spread)
        in_specs.append(pl.BlockSpec(spread.shape, lambda i: (0, 0)))
        out_shape.append(jax.ShapeDtypeStruct((T, spread.shape[0]), F32))
        out_specs.append(pl.BlockSpec((tm, spread.shape[0]), lambda i: (i, 0)))
    return pl.pallas_call(
        body, out_shape=out_shape, grid=(T // tm,), in_specs=in_specs, out_specs=out_specs,
        compiler_params=_params(("parallel",), VMEM_LIMIT), name=name)(*ins)


def _cmul(ar, ai, br, bi):
    return ar * br - ai * bi, ar * bi + ai * br


def s5_chain(finals, s0, a, n_steps, reverse, name):
    W = finals.shape[-1]
    first = N_SEG - 1 if reverse else 0

    def body(f_ref, s0_ref, a_ref, c_ref):
        pr, pi = jnp.ones((1, W), F32), jnp.zeros((1, W), F32)
        br, bi = a_ref[0], a_ref[1]
        n = n_steps
        while n:
            if n & 1:
                pr, pi = _cmul(pr, pi, br, bi)
            br, bi = _cmul(br, bi, br, bi)
            n >>= 1
        fr, fi = f_ref[0], f_ref[1]
        row = lax.broadcasted_iota(jnp.int32, (N_SEG, W), 0)
        s0r = jnp.broadcast_to(s0_ref[0], (N_SEG, W))
        s0i = jnp.broadcast_to(s0_ref[1], (N_SEG, W))
        cr = jnp.where(row == first, s0r, 0.0)
        ci = jnp.where(row == first, s0i, 0.0)
        shift = N_SEG - 1 if reverse else 1
        for _ in range(N_SEG - 1):
            mr, mi = _cmul(pr, pi, cr, ci)
            tr = pltpu.roll(fr + mr, shift, 0)
            ti = pltpu.roll(fi + mi, shift, 0)
            cr = jnp.where(row == first, s0r, tr)
            ci = jnp.where(row == first, s0i, ti)
        c_ref[0] = cr
        c_ref[1] = ci

    return pl.pallas_call(body, out_shape=jax.ShapeDtypeStruct((2, N_SEG, W), F32), name=name)(finals, s0, a)


def _scan_chunk(bur, bui, st_ref, a_ref, n_steps, reverse):
    for lc in range(S5_LANES // BLK_ST):
        sl = slice(lc * BLK_ST, (lc + 1) * BLK_ST)
        lr = jnp.broadcast_to(a_ref[0, :, sl], (N_SEG, BLK_ST))
        li = jnp.broadcast_to(a_ref[1, :, sl], (N_SEG, BLK_ST))

        def step(jj, carry, sl=sl, lr=lr, li=li):
            sr, si = carry
            j = (n_steps - 1 - jj) if reverse else jj
            r0 = pl.multiple_of(j * N_SEG, N_SEG)
            nr = lr * sr - li * si + bur[pl.ds(r0, N_SEG), sl]
            ni = lr * si + li * sr + bui[pl.ds(r0, N_SEG), sl]
            bur[pl.ds(r0, N_SEG), sl] = nr
            bui[pl.ds(r0, N_SEG), sl] = ni
            return nr, ni

        sr, si = lax.fori_loop(0, n_steps, step, (st_ref[0, :, sl], st_ref[1, :, sl]))
        st_ref[0, :, sl] = sr
        st_ref[1, :, sl] = si


def _project_in(x16, w_re, w_im, bur, bui, adjoint):
    for gb in range(N_BLOCKS):
        xb = x16[:, gb * BLK_CH:(gb + 1) * BLK_CH]
        sl = slice(gb * BLK_ST, (gb + 1) * BLK_ST)
        if adjoint:
            dn = (((1,), (1,)), ((), ()))
            bur[:, sl] = lax.dot_general(xb, w_re[gb], dn, preferred_element_type=F32)
            bui[:, sl] = -lax.dot_general(xb, w_im[gb], dn, preferred_element_type=F32)
        else:
            bur[:, sl] = jnp.dot(xb, w_re[gb], preferred_element_type=F32)
            bui[:, sl] = jnp.dot(xb, w_im[gb], preferred_element_type=F32)


def s5_scan(act, w_re, w_im, a, init, *, reverse, adjoint=False, c_re=None, c_im=None, add=None,
            want_ckpt=False, rows=None, name):
    act_off, N = rows if rows is not None else (0, act.shape[0])
    R = math.gcd(ROW_TILE, N, act_off)
    nch, jc = N // R, R // N_SEG
    with_out = c_re is not None

    def chunk(i):
        return (nch - 1 - i) if reverse else i

    def body(*refs):
        act_ref, wre_ref, wim_ref, a_ref, init_ref = refs[:5]
        k = 5
        if with_out:
            cre_ref, cim_ref = refs[k:k + 2]
            k += 2
        if add is not None:
            add_ref = refs[k]
            k += 1
        if with_out:
            out_ref = refs[k]
            k += 1
        if want_ckpt:
            ck_ref = refs[k]
            k += 1
        fin_ref, bur, bui = refs[k:k + 3]

        @pl.when(pl.program_id(0) == 0)
        def _():
            fin_ref[...] = init_ref[...]

        if want_ckpt:
            ck_ref[0] = fin_ref[...]
        _project_in(act_ref[...].astype(BF16), wre_ref, wim_ref, bur, bui, adjoint)
        _scan_chunk(bur, bui, fin_ref, a_ref, jc, reverse)
        if with_out:
            for gb in range(N_BLOCKS):
                sl = slice(gb * BLK_ST, (gb + 1) * BLK_ST)
                y = (jnp.dot(bur[:, sl].astype(BF16), cre_ref[gb], preferred_element_type=F32)
                     - jnp.dot(bui[:, sl].astype(BF16), cim_ref[gb], preferred_element_type=F32))
                cs = slice(gb * BLK_CH, (gb + 1) * BLK_CH)
                if add is not None:
                    y = y + add_ref[:, cs]
                out_ref[:, cs] = y

    row_spec = pl.BlockSpec((R, D_MODEL), lambda i: (chunk(i), 0))
    act_spec = pl.BlockSpec((R, D_MODEL), lambda i: (chunk(i) + act_off // R, 0))
    w_spec = pl.BlockSpec(w_re.shape, lambda i: (0, 0, 0))
    st_spec = pl.BlockSpec((2, N_SEG, S5_LANES), lambda i: (0, 0, 0))
    ins = [act, w_re, w_im, a, init]
    in_specs = [act_spec, w_spec, w_spec, pl.BlockSpec((2, 1, S5_LANES), lambda i: (0, 0, 0)), st_spec]
    if with_out:
        ins += [c_re, c_im]
        in_specs += [pl.BlockSpec(c_re.shape, lambda i: (0, 0, 0))] * 2
    if add is not None:
        ins.append(add)
        in_specs.append(row_spec)
    out_shape, out_specs = [], []
    if with_out:
        out_shape.append(jax.ShapeDtypeStruct((N, D_MODEL), F32))
        out_specs.append(row_spec)
    if want_ckpt:
        out_shape.append(jax.ShapeDtypeStruct((nch, 2, N_SEG, S5_LANES), F32))
        out_specs.append(pl.BlockSpec((1, 2, N_SEG, S5_LANES), lambda i: (chunk(i), 0, 0, 0)))
    out_shape.append(jax.ShapeDtypeStruct((2, N_SEG, S5_LANES), F32))
    out_specs.append(st_spec)
    res = pl.pallas_call(
        body, out_shape=out_shape, grid=(nch,), in_specs=in_specs, out_specs=out_specs,
        scratch_shapes=[pltpu.VMEM((R, S5_LANES), F32), pltpu.VMEM((R, S5_LANES), F32)],
        compiler_params=_params(("arbitrary",), VMEM_LIMIT), name=name)(*ins)
    res = list(res)
    out = res.pop(0) if with_out else None
    ckpt = res.pop(0) if want_ckpt else None
    return out, ckpt, res[0]


def s5_grads(dy, u, ckpt, b_re, b_im, c_re, c_im, lam, init_adj, *, reverse, add=None, u_off=0, name):
    N = dy.shape[0]
    R = math.gcd(ROW_TILE, N, u_off)
    nch, jc = N // R, R // N_SEG
    W = S5_LANES

    def chunk(i):
        return i if reverse else (nch - 1 - i)

    def body(*refs):
        dy_ref, u_ref, ck_ref, bre_ref, bim_ref, cre_ref, cim_ref, lam_ref, init_ref = refs[:9]
        k = 9
        if add is not None:
            add_ref = refs[k]
            k += 1
        du_ref, dlam_ref, dbre_ref, dbim_ref, dcre_ref, dcim_ref, fin_ref = refs[k:k + 7]
        sr_buf, si_buf, er_buf, ei_buf, st_buf = refs[k + 7:k + 12]

        @pl.when(pl.program_id(0) == 0)
        def _():
            fin_ref[...] = init_ref[...]
            dlam_ref[...] = jnp.zeros_like(dlam_ref)
            dbre_ref[...] = jnp.zeros_like(dbre_ref)
            dbim_ref[...] = jnp.zeros_like(dbim_ref)
            dcre_ref[...] = jnp.zeros_like(dcre_ref)
            dcim_ref[...] = jnp.zeros_like(dcim_ref)

        u16 = u_ref[...].astype(BF16)
        dy16 = dy_ref[...].astype(BF16)
        st_buf[...] = ck_ref[0]
        _project_in(u16, bre_ref, bim_ref, sr_buf, si_buf, False)
        _scan_chunk(sr_buf, si_buf, st_buf, lam_ref, jc, reverse)
        _project_in(dy16, cre_ref, cim_ref, er_buf, ei_buf, True)
        for lc in range(W // BLK_ST):
            sl = slice(lc * BLK_ST, (lc + 1) * BLK_ST)
            lr = jnp.broadcast_to(lam_ref[0, :, sl], (N_SEG, BLK_ST))
            li = jnp.broadcast_to(lam_ref[1, :, sl], (N_SEG, BLK_ST))

            def one(r0, spr, spi, carry, sl=sl, lr=lr, li=li):
                gr, gi, ar, ai = carry
                nr = er_buf[pl.ds(r0, N_SEG), sl] + lr * gr + li * gi
                ni = ei_buf[pl.ds(r0, N_SEG), sl] + lr * gi - li * gr
                er_buf[pl.ds(r0, N_SEG), sl] = nr
                ei_buf[pl.ds(r0, N_SEG), sl] = ni
                return nr, ni, ar + spr * nr + spi * ni, ai + spr * ni - spi * nr

            def step(ff, carry, sl=sl, one=one):
                f = jc - 1 - ff
                j = (jc - 1 - f) if reverse else f
                jp = (j + 1) if reverse else (j - 1)
                r0 = pl.multiple_of(j * N_SEG, N_SEG)
                p0 = pl.multiple_of(jp * N_SEG, N_SEG)
                return one(r0, sr_buf[pl.ds(p0, N_SEG), sl], si_buf[pl.ds(p0, N_SEG), sl], carry)

            carry = (fin_ref[0, :, sl], fin_ref[1, :, sl], dlam_ref[0, :, sl], dlam_ref[1, :, sl])
            carry = lax.fori_loop(0, jc - 1, step, carry)
            r_first = (jc - 1) * N_SEG if reverse else 0
            gr, gi, ar, ai = one(r_first, ck_ref[0, 0, :, sl], ck_ref[0, 1, :, sl], carry)
            fin_ref[0, :, sl] = gr
            fin_ref[1, :, sl] = gi
            dlam_ref[0, :, sl] = ar
            dlam_ref[1, :, sl] = ai
        tn = (((0,), (0,)), ((), ()))
        nt = (((1,), (1,)), ((), ()))
        for gb in range(N_BLOCKS):
            sl = slice(gb * BLK_ST, (gb + 1) * BLK_ST)
            cs = slice(gb * BLK_CH, (gb + 1) * BLK_CH)
            gr16 = er_buf[:, sl].astype(BF16)
            gi16 = ei_buf[:, sl].astype(BF16)
            du = (lax.dot_general(gr16, bre_ref[gb], nt, preferred_element_type=F32)
                  + lax.dot_general(gi16, bim_ref[gb], nt, preferred_element_type=F32))
            if add is not None:
                du = du + add_ref[:, cs]
            du_ref[:, cs] = du
            ub, dyb = u16[:, cs], dy16[:, cs]
            dbre_ref[gb] += lax.dot_general(ub, gr16, tn, preferred_element_type=F32)
            dbim_ref[gb] += lax.dot_general(ub, gi16, tn, preferred_element_type=F32)
            dcre_ref[gb] += lax.dot_general(sr_buf[:, sl].astype(BF16), dyb, tn, preferred_element_type=F32)
            dcim_ref[gb] -= lax.dot_general(si_buf[:, sl].astype(BF16), dyb, tn, preferred_element_type=F32)

    row_spec = pl.BlockSpec((R, D_MODEL), lambda i: (chunk(i), 0))
    st_spec = pl.BlockSpec((2, N_SEG, W), lambda i: (0, 0, 0))
    wb_spec = pl.BlockSpec(b_re.shape, lambda i: (0, 0, 0))
    wc_spec = pl.BlockSpec(c_re.shape, lambda i: (0, 0, 0))
    ins = [dy, u, ckpt, b_re, b_im, c_re, c_im, lam, init_adj]
    u_spec = pl.BlockSpec((R, D_MODEL), lambda i: (chunk(i) + u_off // R, 0))
    in_specs = [row_spec, u_spec, pl.BlockSpec((1, 2, N_SEG, W), lambda i: (chunk(i), 0, 0, 0)),
                wb_spec, wb_spec, wc_spec, wc_spec, pl.BlockSpec((2, 1, W), lambda i: (0, 0, 0)), st_spec]
    if add is not None:
        ins.append(add)
        in_specs.append(row_spec)
    out_shape = [jax.ShapeDtypeStruct((N, D_MODEL), F32), jax.ShapeDtypeStruct((2, N_SEG, W), F32),
                 jax.ShapeDtypeStruct(b_re.shape, F32), jax.ShapeDtypeStruct(b_re.shape, F32),
                 jax.ShapeDtypeStruct(c_re.shape, F32), jax.ShapeDtypeStruct(c_re.shape, F32),
                 jax.ShapeDtypeStruct((2, N_SEG, W), F32)]
    out_specs = [row_spec, st_spec, wb_spec, wb_spec, wc_spec, wc_spec, st_spec]
    return pl.pallas_call(
        body, out_shape=out_shape, grid=(nch,), in_specs=in_specs, out_specs=out_specs,
        scratch_shapes=[pltpu.VMEM((R, W), F32) for _ in range(4)] + [pltpu.VMEM((2, N_SEG, W), F32)],
        compiler_params=_params(("arbitrary",), VMEM_LIMIT), name=name)(*ins)


def adamw(w, g, m, v, name="adamw"):
    n, d = w.shape
    lanes = -(-d // 128) * 128
    tm = n
    while tm * lanes * 4 > (1 << 20) and tm % 16 == 0:
        tm //= 2
    c1 = 1.0 - ADAM_B1 ** ADAM_STEP
    c2 = 1.0 - ADAM_B2 ** ADAM_STEP

    def body(w_ref, g_ref, m_ref, v_ref, d_ref, nm_ref, nv_ref):
        g_ = g_ref[...]
        m_ = ADAM_B1 * m_ref[...] + (1.0 - ADAM_B1) * g_
        v_ = ADAM_B2 * v_ref[...] + (1.0 - ADAM_B2) * (g_ * g_)
        d_ref[...] = -ADAM_LR * ((m_ / c1) / (jnp.sqrt(v_ / c2) + ADAM_EPS) + ADAM_WD * w_ref[...])
        nm_ref[...] = m_
        nv_ref[...] = v_

    spec = pl.BlockSpec((tm, d), lambda i: (i, 0))
    return pl.pallas_call(
        body, out_shape=[jax.ShapeDtypeStruct((n, d), F32)] * 3, grid=(n // tm,),
        in_specs=[spec] * 4, out_specs=[spec] * 3,
        compiler_params=_params(("parallel",), VMEM_LIMIT), name=name)(w, g, m, v)


def _coords():
    return lax.axis_index("x"), lax.axis_index("y"), lax.axis_index("c")


def exchange(arrays, out_shapes, remote, local, name, aliases=None):
    n_in, n_out, n_rem, n_loc = len(arrays), len(out_shapes), len(remote), len(local)

    def at(ref, idx):
        return ref if idx is None else ref.at[idx]

    def body(*refs):
        ins, outs = refs[:n_in], refs[n_in:n_in + n_out]
        send_sems, recv_sems, local_sems = refs[n_in + n_out:]
        me = _coords()
        sends, recvs = [], []
        for k, (flip, ii, src_at, oi, dst_at) in enumerate(remote):
            peer = (me[0] ^ flip[0], me[1] ^ flip[1], me[2] ^ flip[2])
            src = at(ins[ii], src_at(me, peer))
            sends.append(pltpu.make_async_remote_copy(
                src_ref=src, dst_ref=at(outs[oi], dst_at(me)), send_sem=send_sems.at[k], recv_sem=recv_sems.at[k],
                device_id=peer, device_id_type=MESH))
            recvs.append(pltpu.make_async_remote_copy(
                src_ref=src, dst_ref=at(outs[oi], dst_at(peer)), send_sem=send_sems.at[k], recv_sem=recv_sems.at[k],
                device_id=peer, device_id_type=MESH))
        locs = [pltpu.make_async_copy(at(ins[ii], src_at(me)), at(outs[oi], dst_at(me)), local_sems.at[k])
                for k, (ii, src_at, oi, dst_at) in enumerate(local)]
        for cp in locs + sends:
            cp.start()
        for cp in recvs:
            cp.wait_recv()
        for cp in sends:
            cp.wait_send()
        for cp in locs:
            cp.wait()

    hbm = pl.BlockSpec(memory_space=pl.ANY)
    return pl.pallas_call(
        body, out_shape=list(out_shapes), in_specs=[hbm] * n_in, out_specs=[hbm] * n_out,
        scratch_shapes=[pltpu.SemaphoreType.DMA((n_rem,)), pltpu.SemaphoreType.DMA((n_rem,)),
                        pltpu.SemaphoreType.DMA((max(n_loc, 1),))],
        input_output_aliases=aliases or {}, name=name)(*arrays)


ALL_FLIPS = [(dx, dy, dc) for dx in (0, 1) for dy in (0, 1) for dc in (0, 1)][1:]
CHIP_FLIPS = [(1, 0, 0), (0, 1, 0), (1, 1, 0)]
CORE_FLIP = (0, 0, 1)


def _dev_index(p):
    return 4 * p[0] + 2 * p[1] + p[2]


def _chip_index(p):
    return 2 * p[0] + p[1]


def _gather(xs, flips, index, n, name):
    arrays = [x[None] for x in xs]
    outs = [jax.ShapeDtypeStruct((n,) + x.shape, x.dtype) for x in xs]
    remote = [(f, a, lambda me, peer: (0,), a, lambda s: (index(s),)) for a in range(len(xs)) for f in flips]
    local = [(a, lambda me: (0,), a, lambda me: (index(me),)) for a in range(len(xs))]
    return exchange(arrays, outs, remote, local, name)


def allgather_devices(x, name):
    return _gather([x], ALL_FLIPS, _dev_index, N_DEV, name)[0]


def allgather_chips(xs, name):
    return _gather(xs, CHIP_FLIPS, _chip_index, N_CHIP, name)


def gather_halves(xs, name):
    n = len(xs)
    nk = n * len(CHIP_FLIPS)

    def body(*refs):
        ins, outs = refs[:n], refs[n:2 * n]
        ici_send, ici_recv, d2d_send, d2d_recv = refs[2 * n:]
        me = _coords()
        sibling = (me[0], me[1], 1 - me[2])
        first, passed, landed = [], [], []
        for a in range(n):
            half = ins[a].shape[0] // 2
            mine = ins[a].at[pl.ds(pl.multiple_of(me[2] * half, 16), half)]
            for j, flip in enumerate(CHIP_FLIPS):
                k = a * len(CHIP_FLIPS) + j
                peer = (me[0] ^ flip[0], me[1] ^ flip[1], me[2])
                first.append(pltpu.make_async_remote_copy(
                    src_ref=mine, dst_ref=outs[a].at[_chip_index(me), me[2]], send_sem=ici_send.at[k],
                    recv_sem=ici_recv.at[k], device_id=peer, device_id_type=MESH))
                arrived = outs[a].at[_chip_index(peer), me[2]]
                landed.append(pltpu.make_async_remote_copy(
                    src_ref=mine, dst_ref=arrived, send_sem=ici_send.at[k], recv_sem=ici_recv.at[k],
                    device_id=peer, device_id_type=MESH))
                passed.append(pltpu.make_async_remote_copy(
                    src_ref=arrived, dst_ref=arrived, send_sem=d2d_send.at[k], recv_sem=d2d_recv.at[k],
                    device_id=sibling, device_id_type=MESH))
        for cp in first:
            cp.start()
        for k in range(nk):
            landed[k].wait_recv()
            passed[k].start()
        for a in range(n):
            for j, flip in enumerate(CHIP_FLIPS):
                k = a * len(CHIP_FLIPS) + j
                peer_chip = _chip_index((me[0] ^ flip[0], me[1] ^ flip[1]))
                from_sibling = outs[a].at[peer_chip, 1 - me[2]]
                pltpu.make_async_remote_copy(
                    src_ref=from_sibling, dst_ref=from_sibling, send_sem=d2d_send.at[k], recv_sem=d2d_recv.at[k],
                    device_id=sibling, device_id_type=MESH).wait_recv()
        for cp in first + passed:
            cp.wait_send()

    hbm = pl.BlockSpec(memory_space=pl.ANY)
    return pl.pallas_call(
        body, out_shape=[jax.ShapeDtypeStruct((N_CHIP, 2, x.shape[0] // 2, x.shape[1]), x.dtype) for x in xs],
        in_specs=[hbm] * n, out_specs=[hbm] * n,
        scratch_shapes=[pltpu.SemaphoreType.DMA((nk,)) for _ in range(4)], name=name)(*xs)


def _half_tile(h, cd):
    return h if h * cd * 4 <= (1 << 20) else math.gcd(512, h)


def pair_add(g, got, core, out_dtype, name):
    _, _, h, cd = g.shape
    th = _half_tile(h, cd)

    def body(c_ref, g_ref, got_ref, o_ref):
        o_ref[0] = (g_ref[0, 0] + got_ref[0]).astype(o_ref.dtype)

    return pl.pallas_call(
        body, out_shape=jax.ShapeDtypeStruct((N_CHIP, h, cd), out_dtype),
        grid_spec=pltpu.PrefetchScalarGridSpec(
            num_scalar_prefetch=1, grid=(N_CHIP, h // th),
            in_specs=[pl.BlockSpec((1, 1, th, cd), lambda q, i, c: (q, c[0], i, 0)),
                      pl.BlockSpec((1, th, cd), lambda q, i, c: (q, i, 0))],
            out_specs=pl.BlockSpec((1, th, cd), lambda q, i, c: (q, i, 0))),
        compiler_params=_params(("parallel", "parallel"), VMEM_LIMIT), name=name)(core, g, got)


def sum_chips(parts, core, name):
    _, h, cd = parts.shape
    th = _half_tile(h, cd)

    def body(c_ref, p_ref, o_ref):
        acc = p_ref[0].astype(F32)
        for q in range(1, N_CHIP):
            acc = acc + p_ref[q].astype(F32)
        o_ref[0] = acc

    return pl.pallas_call(
        body, out_shape=jax.ShapeDtypeStruct((2, h, cd), F32),
        grid_spec=pltpu.PrefetchScalarGridSpec(
            num_scalar_prefetch=1, grid=(h // th,),
            in_specs=[pl.BlockSpec((N_CHIP, th, cd), lambda i, c: (0, i, 0))],
            out_specs=pl.BlockSpec((1, th, cd), lambda i, c: (c[0], i, 0))),
        compiler_params=_params(("parallel",), VMEM_LIMIT), name=name)(core, parts)


def to_segments(a, n_ctx):
    def one(p):
        n = p.shape[0]
        return p.reshape(N_SEG, n // N_SEG, -1).transpose(1, 0, 2).reshape(n, -1)
    return jnp.concatenate([one(a[:n_ctx]), one(a[n_ctx:])], axis=0) if n_ctx else one(a)


def from_segments(a, n_ctx):
    def one(p):
        n = p.shape[0]
        return p.reshape(n // N_SEG, N_SEG, -1).transpose(1, 0, 2).reshape(n, -1)
    return jnp.concatenate([one(a[:n_ctx]), one(a[n_ctx:])], axis=0) if n_ctx else one(a)


def rope_tables(n_ctx, n_lat):
    f32 = np.float32
    rows = n_lat // GRID_W
    row = np.repeat(np.arange(rows), GRID_W).astype(f32)
    col = np.tile(np.arange(GRID_W), rows).astype(f32)
    d = QK_ROPE_DIM // 2
    inv = (f32(1.0) / np.power(f32(ROPE_THETA), np.arange(0, d, 2, dtype=f32) / f32(d))).astype(f32)
    ang = np.concatenate([row[:, None] * inv[None, :], col[:, None] * inv[None, :]], axis=1).astype(f32)
    cos = np.concatenate([np.ones((n_ctx, d), f32), np.cos(ang)], axis=0)
    sin = np.concatenate([np.zeros((n_ctx, d), f32), np.sin(ang)], axis=0)
    q = QK_ROPE_DIM // 4
    T = n_ctx + n_lat
    ones, zeros = np.ones((T, QK_NOPE_DIM), f32), np.zeros((T, QK_NOPE_DIM), f32)
    tail, z8 = np.zeros((T, HEAD_LANES - QK_DIM), f32), np.zeros((T, q), f32)
    cr, cc, sr, sc = cos[:, :q], cos[:, q:], sin[:, :q], sin[:, q:]
    cos_t = np.concatenate([ones, cr, cr, cc, cc, tail], axis=1)
    sin_next = np.concatenate([zeros, -sr, z8, -sc, z8, tail], axis=1)
    sin_prev = np.concatenate([zeros, z8, sr, z8, sc, tail], axis=1)
    return tuple(jnp.asarray(t, F32) for t in (cos_t, sin_next, sin_prev))


def pad_heads(w, used):
    k = w.shape[0]
    return jnp.pad(w.reshape(k, MLA_HEADS, used), ((0, 0), (0, 0), (0, HEAD_LANES - used))).reshape(k, -1)


def unpad_heads(w, used):
    k = w.shape[0]
    return w.reshape(k, MLA_HEADS, HEAD_LANES)[:, :, :used].reshape(k, MLA_HEADS * used)


def rotary_spread():
    lane = np.arange(MLA_HEADS * HEAD_LANES) % HEAD_LANES
    return jnp.asarray(lane[None, :] == (QK_NOPE_DIM + np.arange(QK_ROPE_DIM))[:, None], BF16)


def s5_discretise(a_re, a_im, log_step, b_re, b_im):
    dt = jnp.exp(log_step)[:, None]
    mag = jnp.exp(a_re * dt)
    lb_re = mag * jnp.cos(a_im * dt)
    lb_im = mag * jnp.sin(a_im * dt)
    den = a_re * a_re + a_im * a_im
    nr = lb_re - 1.0
    f_re = ((nr * a_re + lb_im * a_im) / den)[..., None]
    f_im = ((lb_im * a_re - nr * a_im) / den)[..., None]
    return lb_re, lb_im, f_re * b_re - f_im * b_im, f_re * b_im + f_im * b_re


def s5_block_weights(lb_re, lb_im, bb_re, bb_im, c_re, c_im):
    eye = jnp.eye(GROUPS_PER_BLOCK, dtype=F32)
    lam = jnp.stack([lb_re.reshape(1, S5_LANES), lb_im.reshape(1, S5_LANES)])

    def b_blocks(bb):
        t = bb.reshape(N_BLOCKS, GROUPS_PER_BLOCK, S5_STATE, S5_GROUP)
        return jnp.einsum("bgpc,gh->bgchp", t, eye).reshape(N_BLOCKS, BLK_CH, BLK_ST).astype(BF16)

    def c_blocks(cc):
        t = cc.reshape(N_BLOCKS, GROUPS_PER_BLOCK, S5_GROUP, S5_STATE)
        return jnp.einsum("bgcp,gh->bgphc", t, eye).reshape(N_BLOCKS, BLK_ST, BLK_CH).astype(BF16)

    return lam, b_blocks(bb_re), b_blocks(bb_im), c_blocks(c_re), c_blocks(c_im)


def b_block_diag(db):
    t = db.reshape(N_BLOCKS, GROUPS_PER_BLOCK, S5_GROUP, GROUPS_PER_BLOCK, S5_STATE)
    return jnp.einsum("bgchp,gh->bgpc", t, jnp.eye(GROUPS_PER_BLOCK, dtype=F32)).reshape(S5_GROUPS, S5_STATE, S5_GROUP)


def c_block_diag(dc):
    t = dc.reshape(N_BLOCKS, GROUPS_PER_BLOCK, S5_STATE, GROUPS_PER_BLOCK, S5_GROUP)
    return jnp.einsum("bgphc,gh->bgcp", t, jnp.eye(GROUPS_PER_BLOCK, dtype=F32)).reshape(S5_GROUPS, S5_GROUP, S5_STATE)


def conj(a):
    return jnp.stack([a[0], -a[1]])


PACK_TILE = 16 * 128


def pack_flat(parts, dtype):
    flat = [p.reshape(-1).astype(dtype) for p in parts]
    sizes = [f.shape[0] for f in flat]
    total = sum(sizes)
    pad = (-total) % PACK_TILE
    if pad:
        flat.append(jnp.zeros((pad,), dtype))
    offs = np.cumsum([0] + sizes)[:-1].tolist()
    return jnp.concatenate(flat).reshape(-1, 128), offs


def unpack_flat(buf, offs, shapes):
    flat = buf.reshape(-1)
    return [flat[o:o + int(np.prod(s))].reshape(s) for o, s in zip(offs, shapes)]


def s5_forward(p1, n_ctx, dirs):
    saved = []
    y = None
    ctx_rows, lat_rows = (0, n_ctx), (n_ctx, p1.shape[0] - n_ctx)
    zeros_tile = jnp.zeros((2, N_SEG, S5_LANES), F32)
    zeros_row = jnp.zeros((2, 1, S5_LANES), F32)
    for k, (lam, b_re, b_im, c_re, c_im) in enumerate(dirs):
        rev = k == 1
        last = 0 if rev else N_SEG - 1
        _, _, fin = s5_scan(p1, b_re, b_im, lam, zeros_tile, reverse=rev, rows=ctx_rows, name=f"s5_ctx_finals{k}")
        carry_c = s5_chain(fin, zeros_row, lam, n_ctx // N_SEG, rev, name=f"s5_ctx_chain{k}")
        _, ck_c, fin_c = s5_scan(p1, b_re, b_im, lam, carry_c, reverse=rev, want_ckpt=True, rows=ctx_rows,
                                 name=f"s5_ctx_scan{k}")
        s0 = fin_c[:, last:last + 1, :]
        _, _, fin = s5_scan(p1, b_re, b_im, lam, zeros_tile, reverse=rev, rows=lat_rows, name=f"s5_lat_finals{k}")
        carry_l = s5_chain(fin, s0, lam, lat_rows[1] // N_SEG, rev, name=f"s5_lat_chain{k}")
        y, ck_l, _ = s5_scan(p1, b_re, b_im, lam, carry_l, reverse=rev, c_re=c_re, c_im=c_im, add=y,
                             want_ckpt=True, rows=lat_rows, name=f"s5_lat_scan{k}")
        saved.append((ck_c, ck_l))
    return y, saved


def s5_backward(dy_l, du_extra_l, p1, n_ctx, dirs, saved):
    n_lat = p1.shape[0] - n_ctx
    zeros_tile = jnp.zeros((2, N_SEG, S5_LANES), F32)
    zeros_row = jnp.zeros((2, 1, S5_LANES), F32)
    dy_c = jnp.zeros((n_ctx, D_MODEL), F32)
    du_l, du_c = du_extra_l, None
    grads = []
    for k, (lam, b_re, b_im, c_re, c_im) in enumerate(dirs):
        rev = k == 1
        lam_c = conj(lam)
        ck_c, ck_l = saved[k]
        first = N_SEG - 1 if rev else 0
        _, _, fin = s5_scan(dy_l, c_re, c_im, lam_c, zeros_tile, reverse=not rev, adjoint=True,
                            name=f"s5_lat_adj_finals{k}")
        carry = s5_chain(fin, zeros_row, lam_c, n_lat // N_SEG, not rev, name=f"s5_lat_adj_chain{k}")
        du_l, dlam_l, dbr_l, dbi_l, dcr_l, dci_l, fin_a = s5_grads(
            dy_l, p1, ck_l, b_re, b_im, c_re, c_im, lam, carry, reverse=rev, add=du_l, u_off=n_ctx,
            name=f"s5_lat_grads{k}")
        g0 = fin_a[:, first:first + 1, :]
        carry = s5_chain(zeros_tile, g0, lam_c, n_ctx // N_SEG, not rev, name=f"s5_ctx_adj_chain{k}")
        du_c, dlam_c, dbr_c, dbi_c, _, _, _ = s5_grads(
            dy_c, p1, ck_c, b_re, b_im, c_re, c_im, lam, carry, reverse=rev, add=du_c, name=f"s5_ctx_grads{k}")
        dlam = jnp.sum(dlam_l + dlam_c, axis=1)
        grads.append((dlam, b_block_diag(dbr_l + dbr_c), b_block_diag(dbi_l + dbi_c),
                      c_block_diag(dcr_l), c_block_diag(dci_l)))
    return jnp.concatenate([du_c, du_l], axis=0), grads


def local_step(x, ctx, target, mod, w):
    L, Lc = x.shape[0], ctx.shape[0]
    T = L + Lc
    assert L % Lc == 0 and Lc % (2 * N_SEG) == 0 and L % GRID_W == 0
    D = D_MODEL
    X0 = jnp.concatenate([ctx, x], axis=0)

    def mod_of(i, j):
        return mod[i, :, j, :][:, None, :]

    def vec(v):
        return v.reshape(1, 1, -1).astype(F32)

    g0 = vec(w["norm_g"][0])
    (H0,) = rowwise_fwd(f_norm_mod, [X0], [g0, mod_of(0, 1), mod_of(0, 0)], [D], [BF16], T, Lc, "l0_norm")
    p0 = mm_nn(H0, w["mla_w_in"], name="l0_in")
    z0 = Rows(p0, D, col_blk=0)
    cq = Rows(p0, Q_LORA_RANK, col_blk=D // Q_LORA_RANK)
    ckv = Rows(p0, KV_LORA_RANK, col_blk=(D + Q_LORA_RANK) // KV_LORA_RANK)
    kr = p0[:, D + Q_LORA_RANK + KV_LORA_RANK:D + P0_HEAD]
    qng, kvng = vec(w["mla_q_norm"]), vec(w["mla_kv_norm"])
    (qn,) = rowwise_fwd(f_rms, [cq], [qng], [Q_LORA_RANK], [BF16], T, 0, "l0_qnorm")
    (kvn,) = rowwise_fwd(f_rms, [ckv], [kvng], [KV_LORA_RANK], [BF16], T, 0, "l0_kvnorm")
    tabs = rope_tables(Lc, L)
    spread = rotary_spread()
    w_uq_p = pad_heads(w["mla_w_uq"], QK_DIM)
    w_ukv3 = w["mla_w_ukv"].reshape(KV_LORA_RANK, MLA_HEADS, QK_NOPE_DIM + V_HEAD_DIM)
    w_kn_p = pad_heads(w_ukv3[:, :, :QK_NOPE_DIM].reshape(KV_LORA_RANK, -1), QK_NOPE_DIM)
    w_v = w_ukv3[:, :, QK_NOPE_DIM:].reshape(KV_LORA_RANK, -1)
    qb = q_heads(qn, w_uq_p, tabs)
    kb, vb = kv_heads(kvn, w_kn_p, w_v, kr, spread, tabs)
    o, lse, lse_rows = attn_fwd(qb, kb, vb, Lc)
    (og,) = rowwise_fwd(f_gate, [o, z0], [], [D], [BF16], T, 0, "l0_gate")
    out0 = mm_nn(og, w["mla_w_out"], name="l0_out")
    (X1,) = rowwise_fwd(f_res, [X0, out0], [mod_of(0, 2)], [D], [F32], T, Lc, "l0_res")

    X1p = to_segments(X1, Lc)
    tgt_p = to_segments(target, 0)
    g1 = vec(w["norm_g"][1])
    (H1,) = rowwise_fwd(f_norm_mod, [X1p], [g1, mod_of(1, 1), mod_of(1, 0)], [D], [BF16], T, Lc, "l1_norm")
    p1 = mm_nn(H1, w["s5_w_in"], name="l1_in")
    disc_fn = lambda *a: tuple(zip(*[s5_discretise(a[0][k], a[1][k], a[2][k], a[3][k], a[4][k]) for k in range(2)]))
    disc, disc_vjp = jax.vjp(disc_fn, w["s5_a_re"], w["s5_a_im"], w["s5_log_step"], w["s5_b_re"], w["s5_b_im"])
    dirs = [s5_block_weights(disc[0][k], disc[1][k], disc[2][k], disc[3][k], w["s5_c_re"][k], w["s5_c_im"][k])
            for k in range(2)]
    y_ssm, s5_saved = s5_forward(p1, Lc, dirs)

    row = lambda v: v.reshape(1, D).astype(F32)
    (lvec, dX2, d_yssm, d_u_act, d_z1, d_fg, d_gt1, d_bg, d_d, gw_glu, gw_out) = s5_tail(
        y_ssm, p1, X1p, tgt_p, Lc, row(w["s5_d"]), row(w["s5_b_glu"]), mod[1, 1:2, 2, :], row(w["final_g"]),
        w["s5_w_glu"], w["s5_w_out"])
    loss = jnp.sum(lvec)
    gw = {"final_g": d_fg.reshape(D), "s5_b_glu": d_bg.reshape(D), "s5_d": d_d.reshape(D),
          "s5_w_glu": gw_glu, "s5_w_out": gw_out}
    dmod = {}

    du_p, s5_g = s5_backward(d_yssm, d_u_act, p1, Lc, dirs, s5_saved)
    d_disc = tuple(tuple(s5_g[k][j - 1].reshape(disc[j][k].shape) if j >= 2 else
                         s5_g[k][0][j].reshape(disc[j][k].shape) for k in range(2)) for j in range(4))
    gw["s5_a_re"], gw["s5_a_im"], gw["s5_log_step"], gw["s5_b_re"], gw["s5_b_im"] = disc_vjp(d_disc)
    gw["s5_c_re"] = jnp.stack([s5_g[0][3], s5_g[1][3]])
    gw["s5_c_im"] = jnp.stack([s5_g[0][4], s5_g[1][4]])
    d_H1 = mm_nt_sum([(du_p, 0, 0), (d_z1, D, Lc)], w["s5_w_in"], T, "l1_in_dx")
    gw["s5_w_in"] = jnp.concatenate([mm_tn(H1, du_p, name="l1_in_dw_u"), mm_tn(H1[Lc:], d_z1, name="l1_in_dw_z")],
                                    axis=1)
    d_X1p, d_g1, d_sc1, d_sh1 = rowwise_bwd(f_norm_mod, [X1p], [g1, mod_of(1, 1), mod_of(1, 0)], [d_H1],
                                            [0], [0, 1, 2], T, Lc, "l1_norm_bwd", lat_add=dX2)
    d_gt1_full = jnp.concatenate([jnp.zeros((1, 1, D), F32), d_gt1[None]], axis=0)
    dmod[1] = (d_sh1, d_sc1, d_gt1_full)
    d_X1 = from_segments(d_X1p, Lc)

    d_out0, d_gt0 = rowwise_bwd(f_res, [X0, out0], [mod_of(0, 2)], [d_X1], [1], [0], T, Lc, "l0_res_bwd")
    d_og = mm_nt(d_out0, w["mla_w_out"], name="l0_out_dx")
    gw["mla_w_out"] = mm_tn(og, d_out0, name="l0_out_dw")
    d_o, d_z0 = rowwise_bwd(f_gate, [o, z0], [], [d_og], [0, 1], [], T, 0, "l0_gate_bwd")
    dq_p, delta_rows = attn_bwd_dq(qb, kb, vb, o, d_o, lse, Lc)
    dk_p, d_v = attn_bwd_dkv(qb, kb, vb, d_o, lse_rows, delta_rows, Lc)
    (d_q,) = heads_unrope(dq_p, tabs, name="l0_q_unrope")
    d_k, d_kr = heads_unrope(dk_p, tabs, jnp.pad(spread, ((0, HEAD_LANES - QK_ROPE_DIM), (0, 0))),
                             name="l0_k_unrope")
    d_qn = mm_nt(d_q, w_uq_p, name="l0_uq_dx")
    gw["mla_w_uq"] = unpad_heads(mm_tn(qn, d_q, name="l0_uq_dw"), QK_DIM)
    d_kvn = mm_nt(d_k, w_kn_p, name="l0_ukn_dx") + mm_nt(d_v, w_v, name="l0_uv_dx")
    dw_kn = unpad_heads(mm_tn(kvn, d_k, name="l0_ukn_dw"), QK_NOPE_DIM).reshape(KV_LORA_RANK, MLA_HEADS, QK_NOPE_DIM)
    dw_v = mm_tn(kvn, d_v, name="l0_uv_dw").reshape(KV_LORA_RANK, MLA_HEADS, V_HEAD_DIM)
    gw["mla_w_ukv"] = jnp.concatenate([dw_kn, dw_v], axis=-1).reshape(KV_LORA_RANK, -1)
    d_cq, d_qng = rowwise_bwd(f_rms, [cq], [qng], [d_qn], [0], [0], T, 0, "l0_qnorm_bwd")
    d_ckv, d_kvng = rowwise_bwd(f_rms, [ckv], [kvng], [d_kvn], [0], [0], T, 0, "l0_kvnorm_bwd")
    gw["mla_q_norm"] = d_qng.reshape(-1)
    gw["mla_kv_norm"] = d_kvng.reshape(-1)
    o_cq, o_ckv = D, D + Q_LORA_RANK
    o_kr = o_ckv + KV_LORA_RANK
    d_H0 = mm_nt_sum([(d_z0, 0, 0), (d_cq, o_cq, 0), (d_ckv, o_ckv, 0), (d_kr, o_kr, 0)], w["mla_w_in"], T, "l0_in_dx")
    d_head = jnp.concatenate([d_cq, d_ckv, d_kr], axis=1)
    gw["mla_w_in"] = jnp.concatenate([mm_tn(H0, d_head, name="l0_in_dw_head")[:, :P0_HEAD],
                                      mm_tn(H0, d_z0, name="l0_in_dw_z")], axis=1)
    d_X0, d_g0, d_sc0, d_sh0 = rowwise_bwd(f_norm_mod, [X0], [g0, mod_of(0, 1), mod_of(0, 0)], [d_H0],
                                           [0], [0, 1, 2], T, Lc, "l0_norm_bwd", lat_add=d_X1)
    dmod[0] = (d_sh0, d_sc0, d_gt0)
    gw["norm_g"] = jnp.stack([d_g0.reshape(D), d_g1.reshape(D)])
    dx = d_X0[Lc:]
    dmod_arr = jnp.stack([jnp.stack([dmod[i][j][:, 0, :] for j in range(3)], axis=1) for i in range(2)])
    return loss, dx, dmod_arr, gw


SHARDED = {
    "mla_w_in": 1, "mla_w_uq": 1, "mla_w_ukv": 1, "mla_w_out": 0,
    "s5_w_in": 1, "s5_w_glu": 0, "s5_w_out": 0, "s5_d": 0, "s5_b_glu": 0,
}
SHARDED_MATS = ["mla_w_in", "mla_w_uq", "mla_w_ukv", "mla_w_out", "s5_w_in", "s5_w_glu", "s5_w_out"]
SHARDED_VECS = ["s5_d", "s5_b_glu"]
REPLICATED = ["norm_g", "mla_q_norm", "mla_kv_norm", "s5_a_re", "s5_a_im", "s5_log_step", "s5_b_re", "s5_b_im",
              "s5_c_re", "s5_c_im", "final_g"]
WEIGHT_ORDER = ["c_ctx", "ada_w", "ada_b", "norm_g", "mla_w_in", "mla_q_norm", "mla_w_uq", "mla_kv_norm", "mla_w_ukv",
                "mla_w_out", "s5_w_in", "s5_a_re", "s5_a_im", "s5_log_step", "s5_b_re", "s5_b_im", "s5_c_re", "s5_c_im",
                "s5_d", "s5_w_glu", "s5_b_glu", "s5_w_out", "final_g"]


P0_HEAD = Q_LORA_RANK + KV_LORA_RANK + QK_ROPE_DIM


P0_WIDTH = 1536


def w_in_to_kernel_order(w):
    pad = jnp.zeros((w.shape[0], P0_WIDTH - w.shape[1]), w.dtype)
    return jnp.concatenate([w[:, P0_HEAD:], w[:, :P0_HEAD], pad], axis=1)


def gather_weights(ws):
    mats = [ws[n].astype(BF16) for n in SHARDED_MATS]
    outs = gather_halves(mats, "gather_weights")
    chip = _chip_index(_coords())
    full = {}
    for n, own, o in zip(SHARDED_MATS, mats, outs):
        slot = lax.broadcasted_iota(jnp.int32, (N_CHIP, 1, 1), 0)
        o = jnp.where(slot == chip, own[None], o.reshape((N_CHIP,) + own.shape))
        full[n] = o.reshape(-1, o.shape[-1]) if SHARDED[n] == 0 else o.transpose(1, 0, 2).reshape(o.shape[1], -1)
    full["mla_w_in"] = w_in_to_kernel_order(full["mla_w_in"])
    return full


def reduce_gradients(gw):
    me = _coords()
    core = me[2].reshape(1).astype(jnp.int32)
    slots = {}
    for n in SHARDED_MATS:
        g = gw[n]
        if SHARDED[n] == 0:
            slots[n] = g.reshape(N_CHIP, 2, g.shape[0] // (2 * N_CHIP), g.shape[1])
        else:
            k, n4 = g.shape
            slots[n] = g.reshape(k, N_CHIP, n4 // N_CHIP).transpose(1, 0, 2).reshape(N_CHIP, 2, k // 2, n4 // N_CHIP)
    small_names = REPLICATED + SHARDED_VECS
    small, small_offs = pack_flat([gw[n].astype(F32) for n in small_names], F32)
    small = jnp.pad(small, ((0, (-small.shape[0]) % (N_CHIP * 32)), (0, 0)))
    slots["small"] = small.reshape(N_CHIP, 2, -1, 128)
    names = list(slots)
    idx = range(len(names))
    got = exchange(
        [slots[n] for n in names],
        [jax.ShapeDtypeStruct((N_CHIP,) + slots[n].shape[2:], F32) for n in names],
        [(CORE_FLIP, i, lambda me, peer: (slice(None), 1 - me[2]), i, lambda s: None) for i in idx], [],
        "grads_swap_in")
    sums = [pair_add(slots[n], g, core, F32 if n == "small" else BF16, f"grads_pair_{n}") for n, g in zip(names, got)]
    parts = exchange(
        sums, [jax.ShapeDtypeStruct(s.shape, s.dtype) for s in sums],
        [(f, i, lambda me, peer: (_chip_index(peer),), i, lambda s: (_chip_index(s),)) for i in idx for f in CHIP_FLIPS],
        [(i, lambda me: (_chip_index(me),), i, lambda me: (_chip_index(me),)) for i in idx],
        "grads_scatter")
    halves = [sum_chips(p, core, f"grads_sum_{n}") for n, p in zip(names, parts)]
    fulls = exchange(
        halves, [jax.ShapeDtypeStruct(h.shape, F32) for h in halves],
        [(CORE_FLIP, i, lambda me, peer: (me[2],), i, lambda s: (s[2],)) for i in idx], [],
        "grads_swap_out", aliases={i: i for i in idx})
    out = {n: f.reshape(-1, f.shape[-1]) for n, f in zip(names, fulls)}
    (small_all,) = allgather_chips([out.pop("small")], "grads_gather_small")
    vals = unpack_flat(small_all, small_offs, [gw[n].shape for n in small_names])
    for n, v in zip(small_names, vals):
        if n in SHARDED_VECS:
            size = v.shape[0] // N_CHIP
            v = lax.dynamic_slice_in_dim(v, _chip_index(me) * size, size)
        out[n] = v
    return out


def kernel(x, c, ctx, c_ctx, ada_w, ada_b, norm_g, mla_w_in, mla_q_norm, mla_w_uq, mla_kv_norm, mla_w_ukv, mla_w_out, s5_w_in, s5_a_re, s5_a_im, s5_log_step, s5_b_re, s5_b_im, s5_c_re, s5_c_im, s5_d, s5_w_glu, s5_b_glu, s5_w_out, final_g, loss_target, m_c_ctx, m_ada_w, m_ada_b, m_norm_g, m_mla_w_in, m_mla_q_norm, m_mla_w_uq, m_mla_kv_norm, m_mla_w_ukv, m_mla_w_out, m_s5_w_in, m_s5_a_re, m_s5_a_im, m_s5_log_step, m_s5_b_re, m_s5_b_im, m_s5_c_re, m_s5_c_im, m_s5_d, m_s5_w_glu, m_s5_b_glu, m_s5_w_out, m_final_g, v_c_ctx, v_ada_w, v_ada_b, v_norm_g, v_mla_w_in, v_mla_q_norm, v_mla_w_uq, v_mla_kv_norm, v_mla_w_ukv, v_mla_w_out, v_s5_w_in, v_s5_a_re, v_s5_a_im, v_s5_log_step, v_s5_b_re, v_s5_b_im, v_s5_c_re, v_s5_c_im, v_s5_d, v_s5_w_glu, v_s5_b_glu, v_s5_w_out, v_final_g):
    args = dict(locals())
    weights = {n: args[n] for n in WEIGHT_ORDER}
    D = D_MODEL
    xi, yi, ci = _coords()
    chip = 2 * xi + yi
    me = 4 * xi + 2 * yi + ci
    n_col = ada_w.shape[2]

    c_all = allgather_devices(jnp.pad(c, ((0, 7), (0, 0))), "gather_c")[:, 0, :]
    cond = jnp.concatenate([c_all, jnp.broadcast_to(c_ctx[None], (8, D))], axis=0)
    (s_cond,) = rowwise_fwd(lambda v: (_silu(v),), [cond], [], [D], [F32], 16, 0, "cond_silu")
    mod_cols = jnp.stack([mm_nn(s_cond, ada_w[i], name=f"mod_proj{i}") for i in range(2)])
    vec_tiles = [jnp.pad(weights[n][0].reshape(-1, 128), ((0, 6), (0, 0))) for n in SHARDED_VECS]
    mod_all, *vec_all = allgather_chips([mod_cols] + vec_tiles, "gather_mod")
    mod_all = mod_all.transpose(1, 2, 0, 3).reshape(2, 16, 3 * D) + ada_b[:, None, :]
    mod_l = lax.dynamic_index_in_dim(mod_all, me, axis=1, keepdims=False)
    mod_c = mod_all[:, 8, :]
    mod = jnp.stack([mod_c.reshape(2, 3, D), mod_l.reshape(2, 3, D)], axis=1)

    w = gather_weights({n: weights[n][0] for n in SHARDED_MATS})
    for n, v in zip(SHARDED_VECS, vec_all):
        w[n] = v[:, :2, :].reshape(-1)
    for n in ["norm_g", "final_g"]:
        w[n] = weights[n]
    for n in ["mla_q_norm", "mla_kv_norm", "s5_a_re", "s5_a_im", "s5_log_step", "s5_b_re", "s5_b_im",
              "s5_c_re", "s5_c_im"]:
        w[n] = weights[n][0]

    loss_me, dx, dmod, gw = local_step(x[0], ctx[0], loss_target[0], mod, w)

    dmod_rows, loss_all = _gather([dmod.reshape(2, 2, 3 * D), jnp.broadcast_to(loss_me, (8, 128))],
                                  ALL_FLIPS, _dev_index, N_DEV, "gather_dmod")
    loss = functools.reduce(lambda s, d: s + loss_all[d, 0, 0], range(1, N_DEV), loss_all[0, 0, 0])
    dm = jnp.concatenate([dmod_rows[:, :, 1, :], dmod_rows[:, :, 0, :]], axis=0).transpose(1, 0, 2)
    g_ada_b = jnp.sum(dm, axis=1)
    dm_cols = lax.dynamic_slice_in_dim(dm, chip * n_col, n_col, axis=2)
    g_ada_w = jnp.stack([mm_tn(s_cond, dm_cols[i], name=f"mod_proj_dw{i}") for i in range(2)])
    dmc = jnp.sum(dm_cols[:, 8:, :], axis=1)
    dmc8 = jnp.broadcast_to(dmc[:, None, :], (2, 8, n_col))
    g_sc = mm_nt(dmc8[0], ada_w[0], name="mod_proj_dx0")[0] + mm_nt(dmc8[1], ada_w[1], name="mod_proj_dx1")[0]
    g_sc_all = allgather_devices(jnp.broadcast_to(g_sc[None], (8, D)), "gather_dcond")[:, 0, :]
    g_silu_cc = g_sc_all[0] + g_sc_all[2] + g_sc_all[4] + g_sc_all[6]
    (g_c_ctx,) = rowwise_bwd(lambda v: (_silu(v),), [jnp.broadcast_to(c_ctx[None], (8, D))], [],
                             [jnp.broadcast_to(g_silu_cc[None], (8, D))], [0], [], 8, 0, "cond_silu_bwd")
    g_c_ctx = g_c_ctx[0]

    gw_in = {}
    for n in SHARDED_MATS + SHARDED_VECS:
        gw_in[n] = gw[n]
    for n in REPLICATED:
        gw_in[n] = gw[n]
    red = reduce_gradients(gw_in)
    grads = {"c_ctx": g_c_ctx, "ada_w": g_ada_w, "ada_b": g_ada_b}
    for n in WEIGHT_ORDER[3:]:
        grads[n] = red[n].reshape(weights[n].shape)

    deltas, new_m, new_v = {}, {}, {}
    small = [n for n in WEIGHT_ORDER if weights[n].size < 50000]
    for n in WEIGHT_ORDER:
        if n in small:
            continue
        shp = weights[n].shape
        w2 = weights[n].reshape(-1, shp[-1])
        d_, m_, v_ = adamw(w2, grads[n].reshape(w2.shape), args["m_" + n].reshape(w2.shape),
                           args["v_" + n].reshape(w2.shape), name=f"adamw_{n}")
        deltas[n], new_m[n], new_v[n] = d_.reshape(shp), m_.reshape(shp), v_.reshape(shp)
    packs = []
    offs = None
    for src in (weights, grads, {n: args["m_" + n] for n in small}, {n: args["v_" + n] for n in small}):
        buf, offs = pack_flat([src[n] for n in small], F32)
        packs.append(buf)
    outs = adamw(*packs, name="adamw_small")
    for res, dst in zip(outs, (deltas, new_m, new_v)):
        for n, val in zip(small, unpack_flat(res, offs, [weights[n].shape for n in small])):
            dst[n] = val

    return (loss, dx[None], *[grads[n] for n in WEIGHT_ORDER], *[deltas[n] for n in WEIGHT_ORDER],
            *[new_m[n] for n in WEIGHT_ORDER], *[new_v[n] for n in WEIGHT_ORDER])
```

```python
import functools
import math

import jax
import jax.numpy as jnp
import numpy as np
from jax import lax
from jax.experimental import pallas as pl
from jax.experimental.pallas import tpu as pltpu

F32 = jnp.float32
BF16 = jnp.bfloat16

D_MODEL = 1024
GRID_W = 64
EPS = 1e-6
MLA_HEADS = 16
QK_NOPE_DIM = 64
QK_ROPE_DIM = 32
V_HEAD_DIM = 64
Q_LORA_RANK = 256
KV_LORA_RANK = 128
QK_DIM = QK_NOPE_DIM + QK_ROPE_DIM
SOFTMAX_SCALE = QK_DIM ** -0.5
ROPE_THETA = 10000.0
S5_GROUP = 16
S5_GROUPS = D_MODEL // S5_GROUP
S5_STATE = 64
S5_LANES = S5_GROUPS * S5_STATE
N_SEG = 8
GROUPS_PER_BLOCK = 8
N_BLOCKS = S5_GROUPS // GROUPS_PER_BLOCK
BLK_CH = GROUPS_PER_BLOCK * S5_GROUP
BLK_ST = GROUPS_PER_BLOCK * S5_STATE

ADAM_LR = 0.001
ADAM_B1 = 0.9
ADAM_B2 = 0.999
ADAM_EPS = 1e-08
ADAM_WD = 0.01
ADAM_STEP = 10

N_DEV = 8
N_CHIP = 4
MESH = pl.DeviceIdType.MESH
VMEM_LIMIT = 52 * 1024 * 1024
ROW_TILE = 256


def _params(sem=None, vmem=None):
    return pltpu.CompilerParams(dimension_semantics=sem, vmem_limit_bytes=vmem)


def mm_nn(a, b, out_dtype=F32, name="mm_nn"):
    M, K = a.shape
    N = b.shape[1]
    tm = math.gcd(ROW_TILE, M)

    def body(a_ref, b_ref, o_ref):
        o_ref[...] = jnp.dot(a_ref[...].astype(BF16), b_ref[...].astype(BF16),
                             preferred_element_type=F32).astype(o_ref.dtype)

    return pl.pallas_call(
        body, out_shape=jax.ShapeDtypeStruct((M, N), out_dtype), grid=(M // tm,),
        in_specs=[pl.BlockSpec((tm, K), lambda i: (i, 0)), pl.BlockSpec((K, N), lambda i: (0, 0))],
        out_specs=pl.BlockSpec((tm, N), lambda i: (i, 0)),
        compiler_params=_params(("parallel",), VMEM_LIMIT), name=name)(a, b)


def mm_nt(a, b, out_dtype=F32, name="mm_nt"):
    M, N = a.shape
    K = b.shape[0]
    tm = math.gcd(ROW_TILE, M)

    def body(a_ref, b_ref, o_ref):
        o_ref[...] = lax.dot_general(a_ref[...].astype(BF16), b_ref[...].astype(BF16),
                                     (((1,), (1,)), ((), ())),
                                     preferred_element_type=F32).astype(o_ref.dtype)

    return pl.pallas_call(
        body, out_shape=jax.ShapeDtypeStruct((M, K), out_dtype), grid=(M // tm,),
        in_specs=[pl.BlockSpec((tm, N), lambda i: (i, 0)), pl.BlockSpec((K, N), lambda i: (0, 0))],
        out_specs=pl.BlockSpec((tm, K), lambda i: (i, 0)),
        compiler_params=_params(("parallel",), VMEM_LIMIT), name=name)(a, b)


def mm_nt_sum(terms, w, n_rows, name):
    K = w.shape[0]
    tm = math.gcd(ROW_TILE, n_rows, *[t[2] for t in terms])

    def body(*refs):
        i = pl.program_id(0)
        w_ref, o_ref = refs[len(terms)], refs[len(terms) + 1]
        acc = None
        for a_ref, (a, off, first) in zip(refs, terms):
            part = lax.dot_general(a_ref[...].astype(BF16), w_ref[:, off:off + a.shape[1]].astype(BF16), NT_DIMS,
                                   preferred_element_type=F32)
            if first:
                part = jnp.where(i >= first // tm, part, 0.0)
            acc = part if acc is None else acc + part
        o_ref[...] = acc

    def a_spec(a, first):
        skip = first // tm
        return pl.BlockSpec((tm, a.shape[1]), lambda i: (jnp.maximum(i - skip, 0), 0))

    return pl.pallas_call(
        body, out_shape=jax.ShapeDtypeStruct((n_rows, K), F32), grid=(n_rows // tm,),
        in_specs=[a_spec(a, first) for a, _, first in terms] + [pl.BlockSpec(w.shape, lambda i: (0, 0))],
        out_specs=pl.BlockSpec((tm, K), lambda i: (i, 0)),
        compiler_params=_params(("parallel",), VMEM_LIMIT), name=name)(*[t[0] for t in terms], w)


def mm_tn(a, b, name="mm_tn"):
    M, K = a.shape
    N = b.shape[1]
    tn = math.gcd(512, N) if N % 128 == 0 and N > 512 else N

    def body(a_ref, b_ref, o_ref):
        o_ref[...] = lax.dot_general(a_ref[...].astype(BF16), b_ref[...].astype(BF16),
                                     (((0,), (0,)), ((), ())), preferred_element_type=F32)

    return pl.pallas_call(
        body, out_shape=jax.ShapeDtypeStruct((K, N), F32), grid=(N // tn,),
        in_specs=[pl.BlockSpec((M, K), lambda j: (0, 0)), pl.BlockSpec((M, tn), lambda j: (0, j))],
        out_specs=pl.BlockSpec((K, tn), lambda j: (0, j)),
        compiler_params=_params(("parallel",), VMEM_LIMIT), name=name)(a, b)


class Rows:
    def __init__(self, arr, width=None, row_off=0, col_blk=0):
        self.arr = arr
        self.width = arr.shape[1] if width is None else width
        self.row_off = row_off
        self.col_blk = col_blk

    def spec(self, tm):
        ro, cb = self.row_off // tm, self.col_blk
        return pl.BlockSpec((tm, self.width), lambda i: (i + ro, cb))


def _as_rows(x):
    return x if isinstance(x, Rows) else Rows(x)


def _row_tile(n_rows, n_ctx_rows, rows):
    tm = math.gcd(ROW_TILE, n_rows, n_ctx_rows)
    for r in rows:
        tm = math.gcd(tm, r.row_off)
    return tm


def _bc_spec(arr, n_ctx_blocks):
    g, _, d = arr.shape
    if g == 1:
        return pl.BlockSpec((1, 1, d), lambda i: (0, 0, 0))
    return pl.BlockSpec((1, 1, d), lambda i: ((i >= n_ctx_blocks).astype(jnp.int32), 0, 0))


def rowwise_fwd(fn, rows, bcs, out_dims, out_dtypes, n_rows, n_ctx_rows, name):
    rows = [_as_rows(r) for r in rows]
    tm = _row_tile(n_rows, n_ctx_rows, rows)
    ncb = n_ctx_rows // tm
    nr, nb = len(rows), len(bcs)

    def body(*refs):
        vals = [r[...].astype(F32) for r in refs[:nr]] + [b[0].astype(F32) for b in refs[nr:nr + nb]]
        outs = fn(*vals)
        for o_ref, v in zip(refs[nr + nb:], outs):
            o_ref[...] = v.astype(o_ref.dtype)

    outs = pl.pallas_call(
        body,
        out_shape=[jax.ShapeDtypeStruct((n_rows, d), dt) for d, dt in zip(out_dims, out_dtypes)],
        grid=(n_rows // tm,),
        in_specs=[r.spec(tm) for r in rows] + [_bc_spec(b, ncb) for b in bcs],
        out_specs=[pl.BlockSpec((tm, d), lambda i: (i, 0)) for d in out_dims],
        compiler_params=_params(("parallel",), VMEM_LIMIT), name=name)(*[r.arr for r in rows], *bcs)
    return outs


def rowwise_bwd(fn, rows, bcs, cts, diff_rows, diff_bcs, n_rows, n_ctx_rows, name, ct_extra=None, lat_add=None):
    rows = [_as_rows(r) for r in rows]
    cts = [_as_rows(c) for c in cts]
    extra = [_as_rows(ct_extra)] if ct_extra is not None else []
    tm = _row_tile(n_rows, n_ctx_rows, rows + cts + extra)
    ncb = n_ctx_rows // tm
    nr, nb, nc = len(rows), len(bcs), len(cts)
    ndr, ndb = len(diff_rows), len(diff_bcs)
    n_in = nr + nb + nc + len(extra) + (lat_add is not None)

    def body(*refs):
        i = pl.program_id(0)
        rvals = [r[...].astype(F32) for r in refs[:nr]]
        bvals = [b[0].astype(F32) for b in refs[nr:nr + nb]]
        cvals = [c[...].astype(F32) for c in refs[nr + nb:nr + nb + nc]]
        if extra:
            cvals[0] = cvals[0] + refs[nr + nb + nc][...].astype(F32)
        outs = refs[n_in:]

        def f(*d):
            rv, bv = list(rvals), list(bvals)
            for k, idx in enumerate(diff_rows):
                rv[idx] = d[k]
            for k, idx in enumerate(diff_bcs):
                bv[idx] = d[ndr + k]
            return tuple(fn(*rv, *bv))

        primals = [rvals[k] for k in diff_rows] + [bvals[k] for k in diff_bcs]
        _, vjp = jax.vjp(f, *primals)
        grads = list(vjp(tuple(cvals)))
        if lat_add is not None:
            add = refs[n_in - 1][...]
            grads[0] = grads[0] + (add if lat_add.shape[0] == n_rows else jnp.where(i >= ncb, add, 0.0))
        for k in range(ndr):
            outs[k][...] = grads[k].astype(outs[k].dtype)
        for k, idx in enumerate(diff_bcs):
            o_ref = outs[ndr + k]
            first = (i == 0)
            if bcs[idx].shape[0] == 2:
                first = first | (i == ncb)

            @pl.when(first)
            def _(o_ref=o_ref):
                o_ref[...] = jnp.zeros_like(o_ref)

            o_ref[0] += grads[ndr + k]

    out_shape = [jax.ShapeDtypeStruct((n_rows, rows[k].width), F32) for k in diff_rows]
    out_shape += [jax.ShapeDtypeStruct(bcs[k].shape, F32) for k in diff_bcs]
    out_specs = [pl.BlockSpec((tm, rows[k].width), lambda i: (i, 0)) for k in diff_rows]
    out_specs += [_bc_spec(bcs[k], ncb) for k in diff_bcs]
    ins = [r.arr for r in rows] + list(bcs) + [c.arr for c in cts + extra]
    in_specs = [r.spec(tm) for r in rows] + [_bc_spec(b, ncb) for b in bcs] + [c.spec(tm) for c in cts + extra]
    if lat_add is not None:
        ins.append(lat_add)
        skip = ncb if lat_add.shape[0] != n_rows else 0
        in_specs.append(pl.BlockSpec((tm, lat_add.shape[1]), lambda i: (jnp.maximum(i - skip, 0), 0)))
    outs = pl.pallas_call(
        body, out_shape=out_shape, grid=(n_rows // tm,), in_specs=in_specs, out_specs=out_specs,
        compiler_params=_params(("arbitrary",), VMEM_LIMIT), name=name)(*ins)
    return outs


def _rms(x):
    return x * lax.rsqrt(jnp.mean(x * x, axis=-1, keepdims=True) + EPS)


def _sigmoid(x):
    return 0.5 * (jnp.tanh(0.5 * x) + 1.0)


def _silu(x):
    return x * _sigmoid(x)


def _gelu_tanh(x):
    return 0.5 * x * (1.0 + jnp.tanh(math.sqrt(2.0 / math.pi) * (x + 0.044715 * (x * x * x))))


def f_norm_mod(x, g, sc, sh):
    return ((_rms(x) * g) * (1.0 + sc) + sh,)


def f_rms(x, g):
    return (_rms(x) * g,)


def f_gate(o, z):
    return (o * _silu(z),)


def f_s5_act(y, u, d):
    return (_gelu_tanh(y + d * u),)


def f_s5_glu(ya, gl, z, b):
    return (ya * _sigmoid(gl + b) * _silu(z),)


def mla_post_fwd(o, p0, x0, gate, w_out, n_ctx, name="l0_post"):
    n, d = o.shape
    tm = math.gcd(ROW_TILE, n, n_ctx)
    ncb = n_ctx // tm

    def body(o_ref, z_ref, x_ref, gt_ref, w_ref, x1_ref, og_ref, out_ref):
        og = f_gate(o_ref[...], z_ref[...])[0].astype(BF16)
        out = jnp.dot(og, w_ref[...], preferred_element_type=F32)
        og_ref[...] = og
        out_ref[...] = out
        x1_ref[...] = x_ref[...] + gt_ref[0] * out

    row = pl.BlockSpec((tm, d), lambda i: (i, 0))
    return pl.pallas_call(
        body, out_shape=[jax.ShapeDtypeStruct((n, d), F32), jax.ShapeDtypeStruct((n, d), BF16),
                         jax.ShapeDtypeStruct((n, d), F32)],
        grid=(n // tm,),
        in_specs=[row, row, row, _bc_spec(gate, ncb), pl.BlockSpec((d, d), lambda i: (0, 0))],
        out_specs=[row, row, row],
        compiler_params=_params(("parallel",), VMEM_LIMIT), name=name)(o, p0, x0, gate, w_out)


def mla_post_bwd(dx1, out, og, o, p0, gate, w_out, n_ctx, name="l0_post_bwd"):
    n, d = o.shape
    tm = math.gcd(ROW_TILE, n, n_ctx)
    ncb = n_ctx // tm

    def body(dx_ref, out_ref, og_ref, o_ref, z_ref, gt_ref, w_ref, do_ref, dz_ref, dgt_ref, dw_ref):
        i = pl.program_id(0)

        @pl.when(i == 0)
        def _():
            dw_ref[...] = jnp.zeros_like(dw_ref)

        @pl.when((i == 0) | (i == ncb))
        def _():
            dgt_ref[...] = jnp.zeros_like(dgt_ref)

        dx = dx_ref[...]
        dgt_ref[0] += jnp.sum(dx * out_ref[...], axis=0, keepdims=True)
        d_out16 = (gt_ref[0] * dx).astype(BF16)
        dw_ref[...] += lax.dot_general(og_ref[...], d_out16, (((0,), (0,)), ((), ())), preferred_element_type=F32)
        d_og = lax.dot_general(d_out16, w_ref[...], NT_DIMS, preferred_element_type=F32)
        _, gate_vjp = jax.vjp(lambda o_, z_: f_gate(o_, z_), o_ref[...], z_ref[...])
        d_o, d_z = gate_vjp((d_og,))
        do_ref[...] = d_o
        dz_ref[...] = d_z

    row = pl.BlockSpec((tm, d), lambda i: (i, 0))
    mat = pl.BlockSpec((d, d), lambda i: (0, 0))
    return pl.pallas_call(
        body, out_shape=[jax.ShapeDtypeStruct((n, d), F32), jax.ShapeDtypeStruct((n, d), F32),
                         jax.ShapeDtypeStruct(gate.shape, F32), jax.ShapeDtypeStruct((d, d), F32)],
        grid=(n // tm,),
        in_specs=[row, row, row, row, row, _bc_spec(gate, ncb), mat],
        out_specs=[row, row, _bc_spec(gate, ncb), mat],
        compiler_params=_params(("arbitrary",), VMEM_LIMIT), name=name)(dx1, out, og, o, p0, gate, w_out)


def s5_tail(y_ssm, p1, x1p, target, n_ctx, d_vec, b_glu, gate, final_g, w_glu, w_out, name="l1_tail"):
    n, d = y_ssm.shape
    tm = math.gcd(ROW_TILE, n, n_ctx)
    off = n_ctx // tm
    tn_dims = (((0,), (0,)), ((), ()))

    def row_loss(x, g, t):
        e = _rms(x) * g - t
        return 0.5 * (e * e) * (1.0 / d)

    def body(y_ref, u_ref, z_ref, x1_ref, t_ref, d_ref, b_ref, gt_ref, fg_ref, wg_ref, wo_ref,
             l_ref, dx_ref, dy_ref, du_ref, dz_ref, dfg_ref, dgt_ref, db_ref, dd_ref, dwg_ref, dwo_ref):
        @pl.when(pl.program_id(0) == 0)
        def _():
            for r in (l_ref, dfg_ref, dgt_ref, db_ref, dd_ref, dwg_ref, dwo_ref):
                r[...] = jnp.zeros_like(r)

        u, z, tgt, gt = u_ref[...], z_ref[...], t_ref[...], gt_ref[...]
        (ya,), act_vjp = jax.vjp(lambda y_, u_, d_: f_s5_act(y_, u_, d_), y_ref[...], u, d_ref[...])
        ya16 = ya.astype(BF16)
        gl = jnp.dot(ya16, wg_ref[...], preferred_element_type=F32)
        (y3,), glu_vjp = jax.vjp(lambda a_, g_, z_, b_: f_s5_glu(a_, g_, z_, b_), ya, gl, z, b_ref[...])
        y3_16 = y3.astype(BF16)
        out1 = jnp.dot(y3_16, wo_ref[...], preferred_element_type=F32)
        lterm, loss_vjp = jax.vjp(lambda x_, g_: row_loss(x_, g_, tgt), x1_ref[...] + gt * out1, fg_ref[...])
        dx2, dfg = loss_vjp(jnp.ones_like(lterm))
        l_ref[...] += jnp.sum(lterm, axis=0, keepdims=True)
        dfg_ref[...] += dfg
        dx_ref[...] = dx2
        dgt_ref[...] += jnp.sum(dx2 * out1, axis=0, keepdims=True)
        d_out16 = (gt * dx2).astype(BF16)
        dwo_ref[...] += lax.dot_general(y3_16, d_out16, tn_dims, preferred_element_type=F32)
        d_y3 = lax.dot_general(d_out16, wo_ref[...], NT_DIMS, preferred_element_type=F32)
        d_ya, d_gl, d_z, d_b = glu_vjp((d_y3,))
        dz_ref[...] = d_z
        db_ref[...] += d_b
        d_gl16 = d_gl.astype(BF16)
        dwg_ref[...] += lax.dot_general(ya16, d_gl16, tn_dims, preferred_element_type=F32)
        d_ya = d_ya + lax.dot_general(d_gl16, wg_ref[...], NT_DIMS, preferred_element_type=F32)
        d_y, d_u, d_d = act_vjp((d_ya,))
        dy_ref[...] = d_y
        du_ref[...] = d_u
        dd_ref[...] += d_d

    row = pl.BlockSpec((tm, d), lambda i: (i, 0))
    vecs = pl.BlockSpec((1, d), lambda i: (0, 0))
    mat = pl.BlockSpec((d, d), lambda i: (0, 0))
    return pl.pallas_call(
        body,
        out_shape=[jax.ShapeDtypeStruct((1, d), F32)] + [jax.ShapeDtypeStruct((n, d), F32)] * 4
        + [jax.ShapeDtypeStruct((1, d), F32)] * 4 + [jax.ShapeDtypeStruct((d, d), F32)] * 2,
        grid=(n // tm,),
        in_specs=[row, pl.BlockSpec((tm, d), lambda i: (i + off, 0)), pl.BlockSpec((tm, d), lambda i: (i + off, 1)),
                  pl.BlockSpec((tm, d), lambda i: (i + off, 0)), row, vecs, vecs, vecs, vecs, mat, mat],
        out_specs=[vecs, row, row, row, row, vecs, vecs, vecs, vecs, mat, mat],
        compiler_params=_params(("arbitrary",), VMEM_LIMIT), name=name)(
            y_ssm, p1, p1, x1p, target, d_vec, b_glu, gate, final_g, w_glu, w_out)


NT_DIMS = (((1,), (1,)), ((), ()))
HEAD_LANES = 128
N_PAIRS = MLA_HEADS // 2


def _own_lanes(shape, hh):
    lane = lax.broadcasted_iota(jnp.int32, shape, len(shape) - 1)
    return (lane < V_HEAD_DIM) if hh == 0 else (lane >= V_HEAD_DIM)


def _rope_tiles(x, cos, sin_next, sin_prev, inverse):
    width = x.shape[-1]
    reps = width // HEAD_LANES
    c, sn, sp = (jnp.tile(t, (1, reps)) for t in (cos, sin_next, sin_prev))
    if inverse:
        return x * c + pltpu.roll(x * sn, 8, 1) + pltpu.roll(x * sp, width - 8, 1)
    return x * c + pltpu.roll(x, width - 8, 1) * sn + pltpu.roll(x, 8, 1) * sp


def _col_to_row(col):
    n = col.shape[0]
    hi = col.astype(BF16)
    r1 = col - hi.astype(F32)
    mid = r1.astype(BF16)
    lo = (r1 - mid.astype(F32)).astype(BF16)
    lane = lax.broadcasted_iota(jnp.int32, (n, HEAD_LANES), 1)
    terms = jnp.where(lane == 0, hi, jnp.where(lane == 1, mid, jnp.where(lane == 2, lo, jnp.zeros_like(hi))))
    eye = (lax.broadcasted_iota(jnp.int32, (n, n), 0) == lax.broadcasted_iota(jnp.int32, (n, n), 1)).astype(BF16)
    rows = lax.dot_general(terms, eye, (((0,), (0,)), ((), ())), preferred_element_type=F32)
    return rows[0:1] + rows[1:2] + rows[2:3]


def attn_fwd(qb, kb, vb, n_ctx):
    T = qb.shape[0]
    tq = math.gcd(ROW_TILE, n_ctx)
    nq, ncb = T // tq, n_ctx // tq

    def body(q_ref, k_ref, v_ref, o_ref, lse_ref, lse_row_ref):
        qi = pl.program_id(1)

        def rows(n_keys):
            v = v_ref[:n_keys, :]
            outs = []
            for hh in range(2):
                hs = slice(hh * HEAD_LANES, (hh + 1) * HEAD_LANES)
                s = lax.dot_general(q_ref[:, hs], k_ref[:n_keys, hs], NT_DIMS,
                                    preferred_element_type=F32) * SOFTMAX_SCALE
                m = jnp.max(s, axis=-1, keepdims=True)
                p = jnp.exp(s - m)
                l = jnp.sum(p, axis=-1, keepdims=True)
                outs.append(jnp.dot(p.astype(BF16), v, preferred_element_type=F32) / l)
                lse = m + jnp.log(l)
                lse_ref[hh] = lse
                lse_row_ref[hh] = _col_to_row(lse)
            o_ref[...] = jnp.where(_own_lanes(outs[0].shape, 0), outs[0], outs[1])

        pl.when(qi < ncb)(lambda: rows(n_ctx))
        pl.when(qi >= ncb)(lambda: rows(T))

    return pl.pallas_call(
        body,
        out_shape=[jax.ShapeDtypeStruct((T, MLA_HEADS * V_HEAD_DIM), F32),
                   jax.ShapeDtypeStruct((MLA_HEADS, T, 1), F32), jax.ShapeDtypeStruct((MLA_HEADS, 1, T), F32)],
        grid=(N_PAIRS, nq),
        in_specs=[pl.BlockSpec((tq, 2 * HEAD_LANES), lambda h, i: (i, h)),
                  pl.BlockSpec((T, 2 * HEAD_LANES), lambda h, i: (0, h)),
                  pl.BlockSpec((T, 2 * V_HEAD_DIM), lambda h, i: (0, h))],
        out_specs=[pl.BlockSpec((tq, 2 * V_HEAD_DIM), lambda h, i: (i, h)),
                   pl.BlockSpec((2, tq, 1), lambda h, i: (h, i, 0)),
                   pl.BlockSpec((2, 1, tq), lambda h, i: (h, 0, i))],
        compiler_params=_params(("parallel", "parallel"), VMEM_LIMIT), name="attn_fwd")(qb, kb, vb)


def attn_bwd_dq(qb, kb, vb, o, do, lse, tabs, n_ctx):
    T = qb.shape[0]
    tq = math.gcd(ROW_TILE, n_ctx)
    nq, ncb = T // tq, n_ctx // tq

    def body(q_ref, k_ref, v_ref, o_ref, do_ref, lse_ref, c_ref, sn_ref, sp_ref, dq_ref, delta_ref):
        qi = pl.program_id(1)

        def rows(n_keys):
            v = v_ref[:n_keys, :]
            dqs = []
            for hh in range(2):
                hs = slice(hh * HEAD_LANES, (hh + 1) * HEAD_LANES)
                k = k_ref[:n_keys, hs]
                do = jnp.where(_own_lanes(do_ref.shape, hh), do_ref[...], 0.0)
                delta = jnp.sum(do * o_ref[...], axis=-1, keepdims=True)
                s = lax.dot_general(q_ref[:, hs], k, NT_DIMS, preferred_element_type=F32) * SOFTMAX_SCALE
                p = jnp.exp(s - lse_ref[hh])
                dp = lax.dot_general(do.astype(BF16), v, NT_DIMS, preferred_element_type=F32)
                ds = p * (dp - delta) * SOFTMAX_SCALE
                dqs.append(jnp.dot(ds.astype(BF16), k, preferred_element_type=F32))
                delta_ref[hh] = _col_to_row(delta)
            dq = jnp.concatenate(dqs, axis=1)
            dq_ref[...] = _rope_tiles(dq, c_ref[...], sn_ref[...], sp_ref[...], True).astype(BF16)

        pl.when(qi < ncb)(lambda: rows(n_ctx))
        pl.when(qi >= ncb)(lambda: rows(T))

    tab = pl.BlockSpec((tq, HEAD_LANES), lambda h, i: (i, 0))
    return pl.pallas_call(
        body,
        out_shape=[jax.ShapeDtypeStruct((T, MLA_HEADS * HEAD_LANES), BF16),
                   jax.ShapeDtypeStruct((MLA_HEADS, 1, T), F32)],
        grid=(N_PAIRS, nq),
        in_specs=[pl.BlockSpec((tq, 2 * HEAD_LANES), lambda h, i: (i, h)),
                  pl.BlockSpec((T, 2 * HEAD_LANES), lambda h, i: (0, h)),
                  pl.BlockSpec((T, 2 * V_HEAD_DIM), lambda h, i: (0, h)),
                  pl.BlockSpec((tq, 2 * V_HEAD_DIM), lambda h, i: (i, h)),
                  pl.BlockSpec((tq, 2 * V_HEAD_DIM), lambda h, i: (i, h)),
                  pl.BlockSpec((2, tq, 1), lambda h, i: (h, i, 0)), tab, tab, tab],
        out_specs=[pl.BlockSpec((tq, 2 * HEAD_LANES), lambda h, i: (i, h)),
                   pl.BlockSpec((2, 1, tq), lambda h, i: (h, 0, i))],
        compiler_params=_params(("parallel", "parallel"), VMEM_LIMIT), name="attn_bwd_dq")(
            qb, kb, vb, o, do, lse, *tabs)


def attn_bwd_dkv(qb, kb, vb, do, lse_rows, delta_rows, n_ctx):
    T = qb.shape[0]
    tq = math.gcd(ROW_TILE, n_ctx)
    nq, ncb = T // tq, n_ctx // tq

    def body(q_ref, do_ref, lse_ref, delta_ref, k_ref, v_ref, dk_ref, dv_ref):
        kj = pl.program_id(1)

        def cols(first):
            v = v_ref[...]
            do_all = do_ref[first:, :]
            dv = None
            for hh in range(2):
                hs = slice(hh * HEAD_LANES, (hh + 1) * HEAD_LANES)
                k = k_ref[:, hs]
                q = q_ref[first:, hs]
                do16 = jnp.where(_own_lanes(do_all.shape, hh), do_all, 0.0).astype(BF16)
                st = lax.dot_general(k, q, NT_DIMS, preferred_element_type=F32) * SOFTMAX_SCALE
                pt = jnp.exp(st - lse_ref[hh, :, first:])
                dv_h = jnp.dot(pt.astype(BF16), do16, preferred_element_type=F32)
                dv = dv_h if dv is None else dv + dv_h
                dpt = lax.dot_general(v, do16, NT_DIMS, preferred_element_type=F32)
                dst = pt * (dpt - delta_ref[hh, :, first:]) * SOFTMAX_SCALE
                dk_ref[:, hs] = jnp.dot(dst.astype(BF16), q, preferred_element_type=F32)
            dv_ref[...] = dv

        pl.when(kj < ncb)(lambda: cols(0))
        pl.when(kj >= ncb)(lambda: cols(n_ctx))

    return pl.pallas_call(
        body,
        out_shape=[jax.ShapeDtypeStruct((T, MLA_HEADS * HEAD_LANES), F32),
                   jax.ShapeDtypeStruct((T, MLA_HEADS * V_HEAD_DIM), F32)],
        grid=(N_PAIRS, nq),
        in_specs=[pl.BlockSpec((T, 2 * HEAD_LANES), lambda h, j: (0, h)),
                  pl.BlockSpec((T, 2 * V_HEAD_DIM), lambda h, j: (0, h)),
                  pl.BlockSpec((2, 1, T), lambda h, j: (h, 0, 0)),
                  pl.BlockSpec((2, 1, T), lambda h, j: (h, 0, 0)),
                  pl.BlockSpec((tq, 2 * HEAD_LANES), lambda h, j: (j, h)),
                  pl.BlockSpec((tq, 2 * V_HEAD_DIM), lambda h, j: (j, h))],
        out_specs=[pl.BlockSpec((tq, 2 * HEAD_LANES), lambda h, j: (j, h)),
                   pl.BlockSpec((tq, 2 * V_HEAD_DIM), lambda h, j: (j, h))],
        compiler_params=_params(("parallel", "parallel"), VMEM_LIMIT), name="attn_bwd_dkv")(
            qb, do, lse_rows, delta_rows, kb, vb)


def _split_bf16(x):
    hi = x.astype(BF16)
    return hi, (x - hi.astype(F32)).astype(BF16)


def q_heads(qn, w_uq_p, tabs, name="l0_uq"):
    T, K = qn.shape
    N = w_uq_p.shape[1]
    tm = math.gcd(ROW_TILE, T)

    def body(a_ref, w_ref, c_ref, sn_ref, sp_ref, o_ref):
        acc = jnp.dot(a_ref[...], w_ref[...], preferred_element_type=F32)
        o_ref[...] = _rope_tiles(acc, c_ref[...], sn_ref[...], sp_ref[...], False).astype(BF16)

    tab = pl.BlockSpec((tm, HEAD_LANES), lambda i: (i, 0))
    return pl.pallas_call(
        body, out_shape=jax.ShapeDtypeStruct((T, N), BF16), grid=(T // tm,),
        in_specs=[pl.BlockSpec((tm, K), lambda i: (i, 0)), pl.BlockSpec((K, N), lambda i: (0, 0)), tab, tab, tab],
        out_specs=pl.BlockSpec((tm, N), lambda i: (i, 0)),
        compiler_params=_params(("parallel",), VMEM_LIMIT), name=name)(qn, w_uq_p, *tabs)


def kv_heads(kvn, w_kn_p, w_v, kr, spread, tabs, name="l0_ukv"):
    T, K = kvn.shape
    N = w_kn_p.shape[1]
    NV = w_v.shape[1]
    tm = math.gcd(ROW_TILE, T)

    def body(a_ref, wk_ref, wv_ref, kr_ref, e_ref, c_ref, sn_ref, sp_ref, k_ref, v_ref):
        a = a_ref[...]
        hi, lo = _split_bf16(kr_ref[...])
        acc = (jnp.dot(a, wk_ref[...], preferred_element_type=F32)
               + jnp.dot(hi, e_ref[...], preferred_element_type=F32)
               + jnp.dot(lo, e_ref[...], preferred_element_type=F32))
        k_ref[...] = _rope_tiles(acc, c_ref[...], sn_ref[...], sp_ref[...], False).astype(BF16)
        v_ref[...] = jnp.dot(a, wv_ref[...], preferred_element_type=F32).astype(BF16)

    tab = pl.BlockSpec((tm, HEAD_LANES), lambda i: (i, 0))
    return pl.pallas_call(
        body, out_shape=[jax.ShapeDtypeStruct((T, N), BF16), jax.ShapeDtypeStruct((T, NV), BF16)], grid=(T // tm,),
        in_specs=[pl.BlockSpec((tm, K), lambda i: (i, 0)), pl.BlockSpec((K, N), lambda i: (0, 0)),
                  pl.BlockSpec((K, NV), lambda i: (0, 0)), pl.BlockSpec((tm, QK_ROPE_DIM), lambda i: (i, 0)),
                  pl.BlockSpec((QK_ROPE_DIM, N), lambda i: (0, 0)), tab, tab, tab],
        out_specs=[pl.BlockSpec((tm, N), lambda i: (i, 0)), pl.BlockSpec((tm, NV), lambda i: (i, 0))],
        compiler_params=_params(("parallel",), VMEM_LIMIT), name=name)(kvn, w_kn_p, w_v, kr, spread, *tabs)


def heads_unrope(d, tabs, spread=None, name="unrope"):
    T, N = d.shape
    tm = math.gcd(ROW_TILE, T)

    def body(*refs):
        if spread is None:
            d_ref, c_ref, sn_ref, sp_ref, o_ref = refs
        else:
            d_ref, c_ref, sn_ref, sp_ref, e_ref, o_ref, kr_ref = refs
        g = _rope_tiles(d_ref[...], c_ref[...], sn_ref[...], sp_ref[...], True)
        o_ref[...] = g.astype(BF16)
        if spread is not None:
            hi, lo = _split_bf16(g)
            kr_ref[...] = (lax.dot_general(hi, e_ref[...], NT_DIMS, preferred_element_type=F32)
                           + lax.dot_general(lo, e_ref[...], NT_DIMS, preferred_element_type=F32))

    tab = pl.BlockSpec((tm, HEAD_LANES), lambda i: (i, 0))
    row = pl.BlockSpec((tm, N), lambda i: (i, 0))
    ins, in_specs = [d, *tabs], [row, tab, tab, tab]
    out_shape, out_specs = [jax.ShapeDtypeStruct((T, N), BF16)], [row]
    if spread is not None:
        ins.append(spread)
        in_specs.append(pl.BlockSpec(spread.shape, lambda i: (0, 0)))
        out_shape.append(jax.ShapeDtypeStruct((T, spread.shape[0]), F32))
        out_specs.append(pl.BlockSpec((tm, spread.shape[0]), lambda i: (i, 0)))
    return pl.pallas_call(
        body, out_shape=out_shape, grid=(T // tm,), in_specs=in_specs, out_specs=out_specs,
        compiler_params=_params(("parallel",), VMEM_LIMIT), name=name)(*ins)


def _cmul(ar, ai, br, bi):
    return ar * br - ai * bi, ar * bi + ai * br


def s5_chain(finals, s0, a, n_steps, reverse, name):
    W = finals.shape[-1]
    first = N_SEG - 1 if reverse else 0

    def body(f_ref, s0_ref, a_ref, c_ref):
        pr, pi = jnp.ones((1, W), F32), jnp.zeros((1, W), F32)
        br, bi = a_ref[0], a_ref[1]
        n = n_steps
        while n:
            if n & 1:
                pr, pi = _cmul(pr, pi, br, bi)
            br, bi = _cmul(br, bi, br, bi)
            n >>= 1
        fr, fi = f_ref[0], f_ref[1]
        row = lax.broadcasted_iota(jnp.int32, (N_SEG, W), 0)
        s0r = jnp.broadcast_to(s0_ref[0], (N_SEG, W))
        s0i = jnp.broadcast_to(s0_ref[1], (N_SEG, W))
        cr = jnp.where(row == first, s0r, 0.0)
        ci = jnp.where(row == first, s0i, 0.0)
        shift = N_SEG - 1 if reverse else 1
        for _ in range(N_SEG - 1):
            mr, mi = _cmul(pr, pi, cr, ci)
            tr = pltpu.roll(fr + mr, shift, 0)
            ti = pltpu.roll(fi + mi, shift, 0)
            cr = jnp.where(row == first, s0r, tr)
            ci = jnp.where(row == first, s0i, ti)
        c_ref[0] = cr
        c_ref[1] = ci

    return pl.pallas_call(body, out_shape=jax.ShapeDtypeStruct((2, N_SEG, W), F32), name=name)(finals, s0, a)


def _scan_chunk(bur, bui, st_ref, a_ref, n_steps, reverse):
    for lc in range(S5_LANES // BLK_ST):
        sl = slice(lc * BLK_ST, (lc + 1) * BLK_ST)
        lr = jnp.broadcast_to(a_ref[0, :, sl], (N_SEG, BLK_ST))
        li = jnp.broadcast_to(a_ref[1, :, sl], (N_SEG, BLK_ST))

        def step(jj, carry, sl=sl, lr=lr, li=li):
            sr, si = carry
            j = (n_steps - 1 - jj) if reverse else jj
            r0 = pl.multiple_of(j * N_SEG, N_SEG)
            nr = lr * sr - li * si + bur[pl.ds(r0, N_SEG), sl]
            ni = lr * si + li * sr + bui[pl.ds(r0, N_SEG), sl]
            bur[pl.ds(r0, N_SEG), sl] = nr
            bui[pl.ds(r0, N_SEG), sl] = ni
            return nr, ni

        sr, si = lax.fori_loop(0, n_steps, step, (st_ref[0, :, sl], st_ref[1, :, sl]))
        st_ref[0, :, sl] = sr
        st_ref[1, :, sl] = si


def _project_in(x16, w_re, w_im, bur, bui, adjoint):
    for gb in range(N_BLOCKS):
        xb = x16[:, gb * BLK_CH:(gb + 1) * BLK_CH]
        sl = slice(gb * BLK_ST, (gb + 1) * BLK_ST)
        if adjoint:
            dn = (((1,), (1,)), ((), ()))
            bur[:, sl] = lax.dot_general(xb, w_re[gb], dn, preferred_element_type=F32)
            bui[:, sl] = -lax.dot_general(xb, w_im[gb], dn, preferred_element_type=F32)
        else:
            bur[:, sl] = jnp.dot(xb, w_re[gb], preferred_element_type=F32)
            bui[:, sl] = jnp.dot(xb, w_im[gb], preferred_element_type=F32)


def s5_scan(act, w_re, w_im, a, init, *, reverse, adjoint=False, c_re=None, c_im=None, add=None,
            want_ckpt=False, rows=None, name):
    act_off, N = rows if rows is not None else (0, act.shape[0])
    R = math.gcd(ROW_TILE, N, act_off)
    nch, jc = N // R, R // N_SEG
    with_out = c_re is not None

    def chunk(i):
        return (nch - 1 - i) if reverse else i

    def body(*refs):
        act_ref, wre_ref, wim_ref, a_ref, init_ref = refs[:5]
        k = 5
        if with_out:
            cre_ref, cim_ref = refs[k:k + 2]
            k += 2
        if add is not None:
            add_ref = refs[k]
            k += 1
        if with_out:
            out_ref = refs[k]
            k += 1
        if want_ckpt:
            ck_ref = refs[k]
            k += 1
        fin_ref, bur, bui = refs[k:k + 3]

        @pl.when(pl.program_id(0) == 0)
        def _():
            fin_ref[...] = init_ref[...]

        if want_ckpt:
            ck_ref[0] = fin_ref[...]
        _project_in(act_ref[...].astype(BF16), wre_ref, wim_ref, bur, bui, adjoint)
        _scan_chunk(bur, bui, fin_ref, a_ref, jc, reverse)
        if with_out:
            for gb in range(N_BLOCKS):
                sl = slice(gb * BLK_ST, (gb + 1) * BLK_ST)
                y = (jnp.dot(bur[:, sl].astype(BF16), cre_ref[gb], preferred_element_type=F32)
                     - jnp.dot(bui[:, sl].astype(BF16), cim_ref[gb], preferred_element_type=F32))
                cs = slice(gb * BLK_CH, (gb + 1) * BLK_CH)
                if add is not None:
                    y = y + add_ref[:, cs]
                out_ref[:, cs] = y

    row_spec = pl.BlockSpec((R, D_MODEL), lambda i: (chunk(i), 0))
    act_spec = pl.BlockSpec((R, D_MODEL), lambda i: (chunk(i) + act_off // R, 0))
    w_spec = pl.BlockSpec(w_re.shape, lambda i: (0, 0, 0))
    st_spec = pl.BlockSpec((2, N_SEG, S5_LANES), lambda i: (0, 0, 0))
    ins = [act, w_re, w_im, a, init]
    in_specs = [act_spec, w_spec, w_spec, pl.BlockSpec((2, 1, S5_LANES), lambda i: (0, 0, 0)), st_spec]
    if with_out:
        ins += [c_re, c_im]
        in_specs += [pl.BlockSpec(c_re.shape, lambda i: (0, 0, 0))] * 2
    if add is not None:
        ins.append(add)
        in_specs.append(row_spec)
    out_shape, out_specs = [], []
    if with_out:
        out_shape.append(jax.ShapeDtypeStruct((N, D_MODEL), F32))
        out_specs.append(row_spec)
    if want_ckpt:
        out_shape.append(jax.ShapeDtypeStruct((nch, 2, N_SEG, S5_LANES), F32))
        out_specs.append(pl.BlockSpec((1, 2, N_SEG, S5_LANES), lambda i: (chunk(i), 0, 0, 0)))
    out_shape.append(jax.ShapeDtypeStruct((2, N_SEG, S5_LANES), F32))
    out_specs.append(st_spec)
    res = pl.pallas_call(
        body, out_shape=out_shape, grid=(nch,), in_specs=in_specs, out_specs=out_specs,
        scratch_shapes=[pltpu.VMEM((R, S5_LANES), F32), pltpu.VMEM((R, S5_LANES), F32)],
        compiler_params=_params(("arbitrary",), VMEM_LIMIT), name=name)(*ins)
    res = list(res)
    out = res.pop(0) if with_out else None
    ckpt = res.pop(0) if want_ckpt else None
    return out, ckpt, res[0]


def s5_grads(dy, u, ckpt, b_re, b_im, c_re, c_im, lam, init_adj, *, reverse, add=None, u_off=0, name):
    N = dy.shape[0]
    R = math.gcd(ROW_TILE, N, u_off)
    nch, jc = N // R, R // N_SEG
    W = S5_LANES

    def chunk(i):
        return i if reverse else (nch - 1 - i)

    def body(*refs):
        dy_ref, u_ref, ck_ref, bre_ref, bim_ref, cre_ref, cim_ref, lam_ref, init_ref = refs[:9]
        k = 9
        if add is not None:
            add_ref = refs[k]
            k += 1
        du_ref, dlam_ref, dbre_ref, dbim_ref, dcre_ref, dcim_ref, fin_ref = refs[k:k + 7]
        sr_buf, si_buf, er_buf, ei_buf, st_buf = refs[k + 7:k + 12]

        @pl.when(pl.program_id(0) == 0)
        def _():
            fin_ref[...] = init_ref[...]
            dlam_ref[...] = jnp.zeros_like(dlam_ref)
            dbre_ref[...] = jnp.zeros_like(dbre_ref)
            dbim_ref[...] = jnp.zeros_like(dbim_ref)
            dcre_ref[...] = jnp.zeros_like(dcre_ref)
            dcim_ref[...] = jnp.zeros_like(dcim_ref)

        u16 = u_ref[...].astype(BF16)
        dy16 = dy_ref[...].astype(BF16)
        st_buf[...] = ck_ref[0]
        _project_in(u16, bre_ref, bim_ref, sr_buf, si_buf, False)
        _scan_chunk(sr_buf, si_buf, st_buf, lam_ref, jc, reverse)
        _project_in(dy16, cre_ref, cim_ref, er_buf, ei_buf, True)
        for lc in range(W // BLK_ST):
            sl = slice(lc * BLK_ST, (lc + 1) * BLK_ST)
            lr = jnp.broadcast_to(lam_ref[0, :, sl], (N_SEG, BLK_ST))
            li = jnp.broadcast_to(lam_ref[1, :, sl], (N_SEG, BLK_ST))

            def one(r0, spr, spi, carry, sl=sl, lr=lr, li=li):
                gr, gi, ar, ai = carry
                nr = er_buf[pl.ds(r0, N_SEG), sl] + lr * gr + li * gi
                ni = ei_buf[pl.ds(r0, N_SEG), sl] + lr * gi - li * gr
                er_buf[pl.ds(r0, N_SEG), sl] = nr
                ei_buf[pl.ds(r0, N_SEG), sl] = ni
                return nr, ni, ar + spr * nr + spi * ni, ai + spr * ni - spi * nr

            def step(ff, carry, sl=sl, one=one):
                f = jc - 1 - ff
                j = (jc - 1 - f) if reverse else f
                jp = (j + 1) if reverse else (j - 1)
                r0 = pl.multiple_of(j * N_SEG, N_SEG)
                p0 = pl.multiple_of(jp * N_SEG, N_SEG)
                return one(r0, sr_buf[pl.ds(p0, N_SEG), sl], si_buf[pl.ds(p0, N_SEG), sl], carry)

            carry = (fin_ref[0, :, sl], fin_ref[1, :, sl], dlam_ref[0, :, sl], dlam_ref[1, :, sl])
            carry = lax.fori_loop(0, jc - 1, step, carry)
            r_first = (jc - 1) * N_SEG if reverse else 0
            gr, gi, ar, ai = one(r_first, ck_ref[0, 0, :, sl], ck_ref[0, 1, :, sl], carry)
            fin_ref[0, :, sl] = gr
            fin_ref[1, :, sl] = gi
            dlam_ref[0, :, sl] = ar
            dlam_ref[1, :, sl] = ai
        tn = (((0,), (0,)), ((), ()))
        nt = (((1,), (1,)), ((), ()))
        for gb in range(N_BLOCKS):
            sl = slice(gb * BLK_ST, (gb + 1) * BLK_ST)
            cs = slice(gb * BLK_CH, (gb + 1) * BLK_CH)
            gr16 = er_buf[:, sl].astype(BF16)
            gi16 = ei_buf[:, sl].astype(BF16)
            du = (lax.dot_general(gr16, bre_ref[gb], nt, preferred_element_type=F32)
                  + lax.dot_general(gi16, bim_ref[gb], nt, preferred_element_type=F32))
            if add is not None:
                du = du + add_ref[:, cs]
            du_ref[:, cs] = du
            ub, dyb = u16[:, cs], dy16[:, cs]
            dbre_ref[gb] += lax.dot_general(ub, gr16, tn, preferred_element_type=F32)
            dbim_ref[gb] += lax.dot_general(ub, gi16, tn, preferred_element_type=F32)
            dcre_ref[gb] += lax.dot_general(sr_buf[:, sl].astype(BF16), dyb, tn, preferred_element_type=F32)
            dcim_ref[gb] -= lax.dot_general(si_buf[:, sl].astype(BF16), dyb, tn, preferred_element_type=F32)

    row_spec = pl.BlockSpec((R, D_MODEL), lambda i: (chunk(i), 0))
    st_spec = pl.BlockSpec((2, N_SEG, W), lambda i: (0, 0, 0))
    wb_spec = pl.BlockSpec(b_re.shape, lambda i: (0, 0, 0))
    wc_spec = pl.BlockSpec(c_re.shape, lambda i: (0, 0, 0))
    ins = [dy, u, ckpt, b_re, b_im, c_re, c_im, lam, init_adj]
    u_spec = pl.BlockSpec((R, D_MODEL), lambda i: (chunk(i) + u_off // R, 0))
    in_specs = [row_spec, u_spec, pl.BlockSpec((1, 2, N_SEG, W), lambda i: (chunk(i), 0, 0, 0)),
                wb_spec, wb_spec, wc_spec, wc_spec, pl.BlockSpec((2, 1, W), lambda i: (0, 0, 0)), st_spec]
    if add is not None:
        ins.append(add)
        in_specs.append(row_spec)
    out_shape = [jax.ShapeDtypeStruct((N, D_MODEL), F32), jax.ShapeDtypeStruct((2, N_SEG, W), F32),
                 jax.ShapeDtypeStruct(b_re.shape, F32), jax.ShapeDtypeStruct(b_re.shape, F32),
                 jax.ShapeDtypeStruct(c_re.shape, F32), jax.ShapeDtypeStruct(c_re.shape, F32),
                 jax.ShapeDtypeStruct((2, N_SEG, W), F32)]
    out_specs = [row_spec, st_spec, wb_spec, wb_spec, wc_spec, wc_spec, st_spec]
    return pl.pallas_call(
        body, out_shape=out_shape, grid=(nch,), in_specs=in_specs, out_specs=out_specs,
        scratch_shapes=[pltpu.VMEM((R, W), F32) for _ in range(4)] + [pltpu.VMEM((2, N_SEG, W), F32)],
        compiler_params=_params(("arbitrary",), VMEM_LIMIT), name=name)(*ins)


def adamw(w, g, m, v, name="adamw"):
    n, d = w.shape
    lanes = -(-d // 128) * 128
    tm = n
    while tm * lanes * 4 > (1 << 20) and tm % 16 == 0:
        tm //= 2
    c1 = 1.0 - ADAM_B1 ** ADAM_STEP
    c2 = 1.0 - ADAM_B2 ** ADAM_STEP

    def body(w_ref, g_ref, m_ref, v_ref, d_ref, nm_ref, nv_ref):
        g_ = g_ref[...]
        m_ = ADAM_B1 * m_ref[...] + (1.0 - ADAM_B1) * g_
        v_ = ADAM_B2 * v_ref[...] + (1.0 - ADAM_B2) * (g_ * g_)
        d_ref[...] = -ADAM_LR * ((m_ / c1) / (jnp.sqrt(v_ / c2) + ADAM_EPS) + ADAM_WD * w_ref[...])
        nm_ref[...] = m_
        nv_ref[...] = v_

    spec = pl.BlockSpec((tm, d), lambda i: (i, 0))
    return pl.pallas_call(
        body, out_shape=[jax.ShapeDtypeStruct((n, d), F32)] * 3, grid=(n // tm,),
        in_specs=[spec] * 4, out_specs=[spec] * 3,
        compiler_params=_params(("parallel",), VMEM_LIMIT), name=name)(w, g, m, v)


def _coords():
    return lax.axis_index("x"), lax.axis_index("y"), lax.axis_index("c")


def exchange(arrays, out_shapes, remote, local, name, aliases=None):
    n_in, n_out, n_rem, n_loc = len(arrays), len(out_shapes), len(remote), len(local)

    def at(ref, idx):
        return ref if idx is None else ref.at[idx]

    def body(*refs):
        ins, outs = refs[:n_in], refs[n_in:n_in + n_out]
        send_sems, recv_sems, local_sems = refs[n_in + n_out:]
        me = _coords()
        sends, recvs = [], []
        for k, (flip, ii, src_at, oi, dst_at) in enumerate(remote):
            peer = (me[0] ^ flip[0], me[1] ^ flip[1], me[2] ^ flip[2])
            src = at(ins[ii], src_at(me, peer))
            sends.append(pltpu.make_async_remote_copy(
                src_ref=src, dst_ref=at(outs[oi], dst_at(me)), send_sem=send_sems.at[k], recv_sem=recv_sems.at[k],
                device_id=peer, device_id_type=MESH))
            recvs.append(pltpu.make_async_remote_copy(
                src_ref=src, dst_ref=at(outs[oi], dst_at(peer)), send_sem=send_sems.at[k], recv_sem=recv_sems.at[k],
                device_id=peer, device_id_type=MESH))
        locs = [pltpu.make_async_copy(at(ins[ii], src_at(me)), at(outs[oi], dst_at(me)), local_sems.at[k])
                for k, (ii, src_at, oi, dst_at) in enumerate(local)]
        for cp in locs + sends:
            cp.start()
        for cp in recvs:
            cp.wait_recv()
        for cp in sends:
            cp.wait_send()
        for cp in locs:
            cp.wait()

    hbm = pl.BlockSpec(memory_space=pl.ANY)
    return pl.pallas_call(
        body, out_shape=list(out_shapes), in_specs=[hbm] * n_in, out_specs=[hbm] * n_out,
        scratch_shapes=[pltpu.SemaphoreType.DMA((n_rem,)), pltpu.SemaphoreType.DMA((n_rem,)),
                        pltpu.SemaphoreType.DMA((max(n_loc, 1),))],
        input_output_aliases=aliases or {}, name=name)(*arrays)


ALL_FLIPS = [(dx, dy, dc) for dx in (0, 1) for dy in (0, 1) for dc in (0, 1)][1:]
CHIP_FLIPS = [(1, 0, 0), (0, 1, 0), (1, 1, 0)]
CORE_FLIP = (0, 0, 1)


def _dev_index(p):
    return 4 * p[0] + 2 * p[1] + p[2]


def _chip_index(p):
    return 2 * p[0] + p[1]


def _gather(xs, flips, index, n, name):
    arrays = [x[None] for x in xs]
    outs = [jax.ShapeDtypeStruct((n,) + x.shape, x.dtype) for x in xs]
    remote = [(f, a, lambda me, peer: (0,), a, lambda s: (index(s),)) for a in range(len(xs)) for f in flips]
    local = [(a, lambda me: (0,), a, lambda me: (index(me),)) for a in range(len(xs))]
    return exchange(arrays, outs, remote, local, name)


def allgather_devices(x, name):
    return _gather([x], ALL_FLIPS, _dev_index, N_DEV, name)[0]


def allgather_chips(xs, name):
    return _gather(xs, CHIP_FLIPS, _chip_index, N_CHIP, name)


def gather_halves(xs, name):
    n = len(xs)
    nk = n * len(CHIP_FLIPS)

    def body(*refs):
        ins, outs = refs[:n], refs[n:2 * n]
        ici_send, ici_recv, d2d_send, d2d_recv = refs[2 * n:]
        me = _coords()
        sibling = (me[0], me[1], 1 - me[2])
        first, passed, landed = [], [], []
        for a in range(n):
            half = ins[a].shape[0] // 2
            mine = ins[a].at[pl.ds(pl.multiple_of(me[2] * half, 16), half)]
            for j, flip in enumerate(CHIP_FLIPS):
                k = a * len(CHIP_FLIPS) + j
                peer = (me[0] ^ flip[0], me[1] ^ flip[1], me[2])
                first.append(pltpu.make_async_remote_copy(
                    src_ref=mine, dst_ref=outs[a].at[_chip_index(me), me[2]], send_sem=ici_send.at[k],
                    recv_sem=ici_recv.at[k], device_id=peer, device_id_type=MESH))
                arrived = outs[a].at[_chip_index(peer), me[2]]
                landed.append(pltpu.make_async_remote_copy(
                    src_ref=mine, dst_ref=arrived, send_sem=ici_send.at[k], recv_sem=ici_recv.at[k],
                    device_id=peer, device_id_type=MESH))
                passed.append(pltpu.make_async_remote_copy(
                    src_ref=arrived, dst_ref=arrived, send_sem=d2d_send.at[k], recv_sem=d2d_recv.at[k],
                    device_id=sibling, device_id_type=MESH))
        for cp in first:
            cp.start()
        for k in range(nk):
            landed[k].wait_recv()
            passed[k].start()
        for a in range(n):
            for j, flip in enumerate(CHIP_FLIPS):
                k = a * len(CHIP_FLIPS) + j
                peer_chip = _chip_index((me[0] ^ flip[0], me[1] ^ flip[1]))
                from_sibling = outs[a].at[peer_chip, 1 - me[2]]
                pltpu.make_async_remote_copy(
                    src_ref=from_sibling, dst_ref=from_sibling, send_sem=d2d_send.at[k], recv_sem=d2d_recv.at[k],
                    device_id=sibling, device_id_type=MESH).wait_recv()
        for cp in first + passed:
            cp.wait_send()

    hbm = pl.BlockSpec(memory_space=pl.ANY)
    return pl.pallas_call(
        body, out_shape=[jax.ShapeDtypeStruct((N_CHIP, 2, x.shape[0] // 2, x.shape[1]), x.dtype) for x in xs],
        in_specs=[hbm] * n, out_specs=[hbm] * n,
        scratch_shapes=[pltpu.SemaphoreType.DMA((nk,)) for _ in range(4)], name=name)(*xs)


def _half_tile(h, cd):
    return h if h * cd * 4 <= (1 << 20) else math.gcd(512, h)


def pair_add(g, got, core, out_dtype, name):
    _, _, h, cd = g.shape
    th = _half_tile(h, cd)

    def body(c_ref, g_ref, got_ref, o_ref):
        o_ref[0] = (g_ref[0, 0] + got_ref[0]).astype(o_ref.dtype)

    return pl.pallas_call(
        body, out_shape=jax.ShapeDtypeStruct((N_CHIP, h, cd), out_dtype),
        grid_spec=pltpu.PrefetchScalarGridSpec(
            num_scalar_prefetch=1, grid=(N_CHIP, h // th),
            in_specs=[pl.BlockSpec((1, 1, th, cd), lambda q, i, c: (q, c[0], i, 0)),
                      pl.BlockSpec((1, th, cd), lambda q, i, c: (q, i, 0))],
            out_specs=pl.BlockSpec((1, th, cd), lambda q, i, c: (q, i, 0))),
        compiler_params=_params(("parallel", "parallel"), VMEM_LIMIT), name=name)(core, g, got)


def sum_chips(parts, core, name):
    _, h, cd = parts.shape
    th = _half_tile(h, cd)

    def body(c_ref, p_ref, o_ref):
        acc = p_ref[0].astype(F32)
        for q in range(1, N_CHIP):
            acc = acc + p_ref[q].astype(F32)
        o_ref[0] = acc

    return pl.pallas_call(
        body, out_shape=jax.ShapeDtypeStruct((2, h, cd), F32),
        grid_spec=pltpu.PrefetchScalarGridSpec(
            num_scalar_prefetch=1, grid=(h // th,),
            in_specs=[pl.BlockSpec((N_CHIP, th, cd), lambda i, c: (0, i, 0))],
            out_specs=pl.BlockSpec((1, th, cd), lambda i, c: (c[0], i, 0))),
        compiler_params=_params(("parallel",), VMEM_LIMIT), name=name)(core, parts)


def to_segments(a, n_ctx):
    def one(p):
        n = p.shape[0]
        return p.reshape(N_SEG, n // N_SEG, -1).transpose(1, 0, 2).reshape(n, -1)
    return jnp.concatenate([one(a[:n_ctx]), one(a[n_ctx:])], axis=0) if n_ctx else one(a)


def from_segments(a, n_ctx):
    def one(p):
        n = p.shape[0]
        return p.reshape(n // N_SEG, N_SEG, -1).transpose(1, 0, 2).reshape(n, -1)
    return jnp.concatenate([one(a[:n_ctx]), one(a[n_ctx:])], axis=0) if n_ctx else one(a)


def rope_tables(n_ctx, n_lat):
    f32 = np.float32
    rows = n_lat // GRID_W
    row = np.repeat(np.arange(rows), GRID_W).astype(f32)
    col = np.tile(np.arange(GRID_W), rows).astype(f32)
    d = QK_ROPE_DIM // 2
    inv = (f32(1.0) / np.power(f32(ROPE_THETA), np.arange(0, d, 2, dtype=f32) / f32(d))).astype(f32)
    ang = np.concatenate([row[:, None] * inv[None, :], col[:, None] * inv[None, :]], axis=1).astype(f32)
    cos = np.concatenate([np.ones((n_ctx, d), f32), np.cos(ang)], axis=0)
    sin = np.concatenate([np.zeros((n_ctx, d), f32), np.sin(ang)], axis=0)
    q = QK_ROPE_DIM // 4
    T = n_ctx + n_lat
    ones, zeros = np.ones((T, QK_NOPE_DIM), f32), np.zeros((T, QK_NOPE_DIM), f32)
    tail, z8 = np.zeros((T, HEAD_LANES - QK_DIM), f32), np.zeros((T, q), f32)
    cr, cc, sr, sc = cos[:, :q], cos[:, q:], sin[:, :q], sin[:, q:]
    cos_t = np.concatenate([ones, cr, cr, cc, cc, tail], axis=1)
    sin_next = np.concatenate([zeros, -sr, z8, -sc, z8, tail], axis=1)
    sin_prev = np.concatenate([zeros, z8, sr, z8, sc, tail], axis=1)
    return tuple(jnp.asarray(t, F32) for t in (cos_t, sin_next, sin_prev))


def pad_heads(w, used):
    k = w.shape[0]
    return jnp.pad(w.reshape(k, MLA_HEADS, used), ((0, 0), (0, 0), (0, HEAD_LANES - used))).reshape(k, -1)


def unpad_heads(w, used):
    k = w.shape[0]
    return w.reshape(k, MLA_HEADS, HEAD_LANES)[:, :, :used].reshape(k, MLA_HEADS * used)


def rotary_spread():
    lane = np.arange(MLA_HEADS * HEAD_LANES) % HEAD_LANES
    return jnp.asarray(lane[None, :] == (QK_NOPE_DIM + np.arange(QK_ROPE_DIM))[:, None], BF16)


def s5_discretise(a_re, a_im, log_step, b_re, b_im):
    dt = jnp.exp(log_step)[:, None]
    mag = jnp.exp(a_re * dt)
    lb_re = mag * jnp.cos(a_im * dt)
    lb_im = mag * jnp.sin(a_im * dt)
    den = a_re * a_re + a_im * a_im
    nr = lb_re - 1.0
    f_re = ((nr * a_re + lb_im * a_im) / den)[..., None]
    f_im = ((lb_im * a_re - nr * a_im) / den)[..., None]
    return lb_re, lb_im, f_re * b_re - f_im * b_im, f_re * b_im + f_im * b_re


def s5_block_weights(lb_re, lb_im, bb_re, bb_im, c_re, c_im):
    eye = jnp.eye(GROUPS_PER_BLOCK, dtype=F32)
    lam = jnp.stack([lb_re.reshape(1, S5_LANES), lb_im.reshape(1, S5_LANES)])

    def b_blocks(bb):
        t = bb.reshape(N_BLOCKS, GROUPS_PER_BLOCK, S5_STATE, S5_GROUP)
        return jnp.einsum("bgpc,gh->bgchp", t, eye).reshape(N_BLOCKS, BLK_CH, BLK_ST).astype(BF16)

    def c_blocks(cc):
        t = cc.reshape(N_BLOCKS, GROUPS_PER_BLOCK, S5_GROUP, S5_STATE)
        return jnp.einsum("bgcp,gh->bgphc", t, eye).reshape(N_BLOCKS, BLK_ST, BLK_CH).astype(BF16)

    return lam, b_blocks(bb_re), b_blocks(bb_im), c_blocks(c_re), c_blocks(c_im)


def b_block_diag(db):
    t = db.reshape(N_BLOCKS, GROUPS_PER_BLOCK, S5_GROUP, GROUPS_PER_BLOCK, S5_STATE)
    return jnp.einsum("bgchp,gh->bgpc", t, jnp.eye(GROUPS_PER_BLOCK, dtype=F32)).reshape(S5_GROUPS, S5_STATE, S5_GROUP)


def c_block_diag(dc):
    t = dc.reshape(N_BLOCKS, GROUPS_PER_BLOCK, S5_STATE, GROUPS_PER_BLOCK, S5_GROUP)
    return jnp.einsum("bgphc,gh->bgcp", t, jnp.eye(GROUPS_PER_BLOCK, dtype=F32)).reshape(S5_GROUPS, S5_GROUP, S5_STATE)


def conj(a):
    return jnp.stack([a[0], -a[1]])


PACK_TILE = 16 * 128


def pack_flat(parts, dtype):
    flat = [p.reshape(-1).astype(dtype) for p in parts]
    sizes = [f.shape[0] for f in flat]
    total = sum(sizes)
    pad = (-total) % PACK_TILE
    if pad:
        flat.append(jnp.zeros((pad,), dtype))
    offs = np.cumsum([0] + sizes)[:-1].tolist()
    return jnp.concatenate(flat).reshape(-1, 128), offs


def unpack_flat(buf, offs, shapes):
    flat = buf.reshape(-1)
    return [flat[o:o + int(np.prod(s))].reshape(s) for o, s in zip(offs, shapes)]


def s5_forward(p1, n_ctx, dirs):
    saved = []
    y = None
    ctx_rows, lat_rows = (0, n_ctx), (n_ctx, p1.shape[0] - n_ctx)
    zeros_tile = jnp.zeros((2, N_SEG, S5_LANES), F32)
    zeros_row = jnp.zeros((2, 1, S5_LANES), F32)
    for k, (lam, b_re, b_im, c_re, c_im) in enumerate(dirs):
        rev = k == 1
        last = 0 if rev else N_SEG - 1
        _, _, fin = s5_scan(p1, b_re, b_im, lam, zeros_tile, reverse=rev, rows=ctx_rows, name=f"s5_ctx_finals{k}")
        carry_c = s5_chain(fin, zeros_row, lam, n_ctx // N_SEG, rev, name=f"s5_ctx_chain{k}")
        _, ck_c, fin_c = s5_scan(p1, b_re, b_im, lam, carry_c, reverse=rev, want_ckpt=True, rows=ctx_rows,
                                 name=f"s5_ctx_scan{k}")
        s0 = fin_c[:, last:last + 1, :]
        _, _, fin = s5_scan(p1, b_re, b_im, lam, zeros_tile, reverse=rev, rows=lat_rows, name=f"s5_lat_finals{k}")
        carry_l = s5_chain(fin, s0, lam, lat_rows[1] // N_SEG, rev, name=f"s5_lat_chain{k}")
        y, ck_l, _ = s5_scan(p1, b_re, b_im, lam, carry_l, reverse=rev, c_re=c_re, c_im=c_im, add=y,
                             want_ckpt=True, rows=lat_rows, name=f"s5_lat_scan{k}")
        saved.append((ck_c, ck_l))
    return y, saved


def s5_backward(dy_l, du_extra_l, p1, n_ctx, dirs, saved):
    n_lat = p1.shape[0] - n_ctx
    zeros_tile = jnp.zeros((2, N_SEG, S5_LANES), F32)
    zeros_row = jnp.zeros((2, 1, S5_LANES), F32)
    dy_c = jnp.zeros((n_ctx, D_MODEL), F32)
    du_l, du_c = du_extra_l, None
    grads = []
    for k, (lam, b_re, b_im, c_re, c_im) in enumerate(dirs):
        rev = k == 1
        lam_c = conj(lam)
        ck_c, ck_l = saved[k]
        first = N_SEG - 1 if rev else 0
        _, _, fin = s5_scan(dy_l, c_re, c_im, lam_c, zeros_tile, reverse=not rev, adjoint=True,
                            name=f"s5_lat_adj_finals{k}")
        carry = s5_chain(fin, zeros_row, lam_c, n_lat // N_SEG, not rev, name=f"s5_lat_adj_chain{k}")
        du_l, dlam_l, dbr_l, dbi_l, dcr_l, dci_l, fin_a = s5_grads(
            dy_l, p1, ck_l, b_re, b_im, c_re, c_im, lam, carry, reverse=rev, add=du_l, u_off=n_ctx,
            name=f"s5_lat_grads{k}")
        g0 = fin_a[:, first:first + 1, :]
        carry = s5_chain(zeros_tile, g0, lam_c, n_ctx // N_SEG, not rev, name=f"s5_ctx_adj_chain{k}")
        du_c, dlam_c, dbr_c, dbi_c, _, _, _ = s5_grads(
            dy_c, p1, ck_c, b_re, b_im, c_re, c_im, lam, carry, reverse=rev, add=du_c, name=f"s5_ctx_grads{k}")
        dlam = jnp.sum(dlam_l + dlam_c, axis=1)
        grads.append((dlam, b_block_diag(dbr_l + dbr_c), b_block_diag(dbi_l + dbi_c),
                      c_block_diag(dcr_l), c_block_diag(dci_l)))
    return jnp.concatenate([du_c, du_l], axis=0), grads


def local_step(x, ctx, target, mod, w):
    L, Lc = x.shape[0], ctx.shape[0]
    T = L + Lc
    assert L % Lc == 0 and Lc % (2 * N_SEG) == 0 and L % GRID_W == 0
    D = D_MODEL
    X0 = jnp.concatenate([ctx, x], axis=0)

    def mod_of(i, j):
        return mod[i, :, j, :][:, None, :]

    def vec(v):
        return v.reshape(1, 1, -1).astype(F32)

    g0 = vec(w["norm_g"][0])
    (H0,) = rowwise_fwd(f_norm_mod, [X0], [g0, mod_of(0, 1), mod_of(0, 0)], [D], [BF16], T, Lc, "l0_norm")
    p0 = mm_nn(H0, w["mla_w_in"], name="l0_in")
    cq = Rows(p0, Q_LORA_RANK, col_blk=D // Q_LORA_RANK)
    ckv = Rows(p0, KV_LORA_RANK, col_blk=(D + Q_LORA_RANK) // KV_LORA_RANK)
    kr = p0[:, D + Q_LORA_RANK + KV_LORA_RANK:D + P0_HEAD]
    qng, kvng = vec(w["mla_q_norm"]), vec(w["mla_kv_norm"])
    (qn,) = rowwise_fwd(f_rms, [cq], [qng], [Q_LORA_RANK], [BF16], T, 0, "l0_qnorm")
    (kvn,) = rowwise_fwd(f_rms, [ckv], [kvng], [KV_LORA_RANK], [BF16], T, 0, "l0_kvnorm")
    tabs = rope_tables(Lc, L)
    spread = rotary_spread()
    w_uq_p = pad_heads(w["mla_w_uq"], QK_DIM)
    w_ukv3 = w["mla_w_ukv"].reshape(KV_LORA_RANK, MLA_HEADS, QK_NOPE_DIM + V_HEAD_DIM)
    w_kn_p = pad_heads(w_ukv3[:, :, :QK_NOPE_DIM].reshape(KV_LORA_RANK, -1), QK_NOPE_DIM)
    w_v = w_ukv3[:, :, QK_NOPE_DIM:].reshape(KV_LORA_RANK, -1)
    qb = q_heads(qn, w_uq_p, tabs)
    kb, vb = kv_heads(kvn, w_kn_p, w_v, kr, spread, tabs)
    o, lse, lse_rows = attn_fwd(qb, kb, vb, Lc)
    X1, og, out0 = mla_post_fwd(o, p0, X0, mod_of(0, 2), w["mla_w_out"], Lc)

    X1p = to_segments(X1, Lc)
    tgt_p = to_segments(target, 0)
    g1 = vec(w["norm_g"][1])
    (H1,) = rowwise_fwd(f_norm_mod, [X1p], [g1, mod_of(1, 1), mod_of(1, 0)], [D], [BF16], T, Lc, "l1_norm")
    p1 = mm_nn(H1, w["s5_w_in"], name="l1_in")
    disc_fn = lambda *a: tuple(zip(*[s5_discretise(a[0][k], a[1][k], a[2][k], a[3][k], a[4][k]) for k in range(2)]))
    disc, disc_vjp = jax.vjp(disc_fn, w["s5_a_re"], w["s5_a_im"], w["s5_log_step"], w["s5_b_re"], w["s5_b_im"])
    dirs = [s5_block_weights(disc[0][k], disc[1][k], disc[2][k], disc[3][k], w["s5_c_re"][k], w["s5_c_im"][k])
            for k in range(2)]
    y_ssm, s5_saved = s5_forward(p1, Lc, dirs)

    row = lambda v: v.reshape(1, D).astype(F32)
    (lvec, dX2, d_yssm, d_u_act, d_z1, d_fg, d_gt1, d_bg, d_d, gw_glu, gw_out) = s5_tail(
        y_ssm, p1, X1p, tgt_p, Lc, row(w["s5_d"]), row(w["s5_b_glu"]), mod[1, 1:2, 2, :], row(w["final_g"]),
        w["s5_w_glu"], w["s5_w_out"])
    loss = jnp.sum(lvec)
    gw = {"final_g": d_fg.reshape(D), "s5_b_glu": d_bg.reshape(D), "s5_d": d_d.reshape(D),
          "s5_w_glu": gw_glu, "s5_w_out": gw_out}
    dmod = {}

    du_p, s5_g = s5_backward(d_yssm, d_u_act, p1, Lc, dirs, s5_saved)
    d_disc = tuple(tuple(s5_g[k][j - 1].reshape(disc[j][k].shape) if j >= 2 else
                         s5_g[k][0][j].reshape(disc[j][k].shape) for k in range(2)) for j in range(4))
    gw["s5_a_re"], gw["s5_a_im"], gw["s5_log_step"], gw["s5_b_re"], gw["s5_b_im"] = disc_vjp(d_disc)
    gw["s5_c_re"] = jnp.stack([s5_g[0][3], s5_g[1][3]])
    gw["s5_c_im"] = jnp.stack([s5_g[0][4], s5_g[1][4]])
    d_H1 = mm_nt_sum([(du_p, 0, 0), (d_z1, D, Lc)], w["s5_w_in"], T, "l1_in_dx")
    gw["s5_w_in"] = jnp.concatenate([mm_tn(H1, du_p, name="l1_in_dw_u"), mm_tn(H1[Lc:], d_z1, name="l1_in_dw_z")],
                                    axis=1)
    d_X1p, d_g1, d_sc1, d_sh1 = rowwise_bwd(f_norm_mod, [X1p], [g1, mod_of(1, 1), mod_of(1, 0)], [d_H1],
                                            [0], [0, 1, 2], T, Lc, "l1_norm_bwd", lat_add=dX2)
    d_gt1_full = jnp.concatenate([jnp.zeros((1, 1, D), F32), d_gt1[None]], axis=0)
    dmod[1] = (d_sh1, d_sc1, d_gt1_full)
    d_X1 = from_segments(d_X1p, Lc)

    d_o, d_z0, d_gt0, gw["mla_w_out"] = mla_post_bwd(d_X1, out0, og, o, p0, mod_of(0, 2), w["mla_w_out"], Lc)
    d_q, delta_rows = attn_bwd_dq(qb, kb, vb, o, d_o, lse, tabs, Lc)
    dk_p, d_v = attn_bwd_dkv(qb, kb, vb, d_o, lse_rows, delta_rows, Lc)
    d_k, d_kr = heads_unrope(dk_p, tabs, jnp.pad(spread, ((0, HEAD_LANES - QK_ROPE_DIM), (0, 0))),
                             name="l0_k_unrope")
    d_qn = mm_nt(d_q, w_uq_p, name="l0_uq_dx")
    gw["mla_w_uq"] = unpad_heads(mm_tn(qn, d_q, name="l0_uq_dw"), QK_DIM)
    d_kvn = mm_nt(d_k, w_kn_p, name="l0_ukn_dx") + mm_nt(d_v, w_v, name="l0_uv_dx")
    dw_kn = unpad_heads(mm_tn(kvn, d_k, name="l0_ukn_dw"), QK_NOPE_DIM).reshape(KV_LORA_RANK, MLA_HEADS, QK_NOPE_DIM)
    dw_v = mm_tn(kvn, d_v, name="l0_uv_dw").reshape(KV_LORA_RANK, MLA_HEADS, V_HEAD_DIM)
    gw["mla_w_ukv"] = jnp.concatenate([dw_kn, dw_v], axis=-1).reshape(KV_LORA_RANK, -1)
    d_cq, d_qng = rowwise_bwd(f_rms, [cq], [qng], [d_qn], [0], [0], T, 0, "l0_qnorm_bwd")
    d_ckv, d_kvng = rowwise_bwd(f_rms, [ckv], [kvng], [d_kvn], [0], [0], T, 0, "l0_kvnorm_bwd")
    gw["mla_q_norm"] = d_qng.reshape(-1)
    gw["mla_kv_norm"] = d_kvng.reshape(-1)
    o_cq, o_ckv = D, D + Q_LORA_RANK
    o_kr = o_ckv + KV_LORA_RANK
    d_H0 = mm_nt_sum([(d_z0, 0, 0), (d_cq, o_cq, 0), (d_ckv, o_ckv, 0), (d_kr, o_kr, 0)], w["mla_w_in"], T, "l0_in_dx")
    d_head = jnp.concatenate([d_cq, d_ckv, d_kr], axis=1)
    gw["mla_w_in"] = jnp.concatenate([mm_tn(H0, d_head, name="l0_in_dw_head")[:, :P0_HEAD],
                                      mm_tn(H0, d_z0, name="l0_in_dw_z")], axis=1)
    d_X0, d_g0, d_sc0, d_sh0 = rowwise_bwd(f_norm_mod, [X0], [g0, mod_of(0, 1), mod_of(0, 0)], [d_H0],
                                           [0], [0, 1, 2], T, Lc, "l0_norm_bwd", lat_add=d_X1)
    dmod[0] = (d_sh0, d_sc0, d_gt0)
    gw["norm_g"] = jnp.stack([d_g0.reshape(D), d_g1.reshape(D)])
    dx = d_X0[Lc:]
    dmod_arr = jnp.stack([jnp.stack([dmod[i][j][:, 0, :] for j in range(3)], axis=1) for i in range(2)])
    return loss, dx, dmod_arr, gw


SHARDED = {
    "mla_w_in": 1, "mla_w_uq": 1, "mla_w_ukv": 1, "mla_w_out": 0,
    "s5_w_in": 1, "s5_w_glu": 0, "s5_w_out": 0, "s5_d": 0, "s5_b_glu": 0,
}
SHARDED_MATS = ["mla_w_in", "mla_w_uq", "mla_w_ukv", "mla_w_out", "s5_w_in", "s5_w_glu", "s5_w_out"]
SHARDED_VECS = ["s5_d", "s5_b_glu"]
REPLICATED = ["norm_g", "mla_q_norm", "mla_kv_norm", "s5_a_re", "s5_a_im", "s5_log_step", "s5_b_re", "s5_b_im",
              "s5_c_re", "s5_c_im", "final_g"]
WEIGHT_ORDER = ["c_ctx", "ada_w", "ada_b", "norm_g", "mla_w_in", "mla_q_norm", "mla_w_uq", "mla_kv_norm", "mla_w_ukv",
                "mla_w_out", "s5_w_in", "s5_a_re", "s5_a_im", "s5_log_step", "s5_b_re", "s5_b_im", "s5_c_re", "s5_c_im",
                "s5_d", "s5_w_glu", "s5_b_glu", "s5_w_out", "final_g"]


P0_HEAD = Q_LORA_RANK + KV_LORA_RANK + QK_ROPE_DIM


P0_WIDTH = 1536


def w_in_to_kernel_order(w):
    pad = jnp.zeros((w.shape[0], P0_WIDTH - w.shape[1]), w.dtype)
    return jnp.concatenate([w[:, P0_HEAD:], w[:, :P0_HEAD], pad], axis=1)


def gather_weights(ws):
    mats = [ws[n].astype(BF16) for n in SHARDED_MATS]
    outs = gather_halves(mats, "gather_weights")
    chip = _chip_index(_coords())
    full = {}
    for n, own, o in zip(SHARDED_MATS, mats, outs):
        slot = lax.broadcasted_iota(jnp.int32, (N_CHIP, 1, 1), 0)
        o = jnp.where(slot == chip, own[None], o.reshape((N_CHIP,) + own.shape))
        full[n] = o.reshape(-1, o.shape[-1]) if SHARDED[n] == 0 else o.transpose(1, 0, 2).reshape(o.shape[1], -1)
    full["mla_w_in"] = w_in_to_kernel_order(full["mla_w_in"])
    return full


def reduce_gradients(gw):
    me = _coords()
    core = me[2].reshape(1).astype(jnp.int32)
    slots = {}
    for n in SHARDED_MATS:
        g = gw[n]
        if SHARDED[n] == 0:
            slots[n] = g.reshape(N_CHIP, 2, g.shape[0] // (2 * N_CHIP), g.shape[1])
        else:
            k, n4 = g.shape
            slots[n] = g.reshape(k, N_CHIP, n4 // N_CHIP).transpose(1, 0, 2).reshape(N_CHIP, 2, k // 2, n4 // N_CHIP)
    small_names = REPLICATED + SHARDED_VECS
    small, small_offs = pack_flat([gw[n].astype(F32) for n in small_names], F32)
    small = jnp.pad(small, ((0, (-small.shape[0]) % (N_CHIP * 32)), (0, 0)))
    slots["small"] = small.reshape(N_CHIP, 2, -1, 128)
    names = list(slots)
    idx = range(len(names))
    got = exchange(
        [slots[n] for n in names],
        [jax.ShapeDtypeStruct((N_CHIP,) + slots[n].shape[2:], F32) for n in names],
        [(CORE_FLIP, i, lambda me, peer: (slice(None), 1 - me[2]), i, lambda s: None) for i in idx], [],
        "grads_swap_in")
    sums = [pair_add(slots[n], g, core, F32 if n == "small" else BF16, f"grads_pair_{n}") for n, g in zip(names, got)]
    parts = exchange(
        sums, [jax.ShapeDtypeStruct(s.shape, s.dtype) for s in sums],
        [(f, i, lambda me, peer: (_chip_index(peer),), i, lambda s: (_chip_index(s),)) for i in idx for f in CHIP_FLIPS],
        [(i, lambda me: (_chip_index(me),), i, lambda me: (_chip_index(me),)) for i in idx],
        "grads_scatter")
    halves = [sum_chips(p, core, f"grads_sum_{n}") for n, p in zip(names, parts)]
    fulls = exchange(
        halves, [jax.ShapeDtypeStruct(h.shape, F32) for h in halves],
        [(CORE_FLIP, i, lambda me, peer: (me[2],), i, lambda s: (s[2],)) for i in idx], [],
        "grads_swap_out", aliases={i: i for i in idx})
    out = {n: f.reshape(-1, f.shape[-1]) for n, f in zip(names, fulls)}
    (small_all,) = allgather_chips([out.pop("small")], "grads_gather_small")
    vals = unpack_flat(small_all, small_offs, [gw[n].shape for n in small_names])
    for n, v in zip(small_names, vals):
        if n in SHARDED_VECS:
            size = v.shape[0] // N_CHIP
            v = lax.dynamic_slice_in_dim(v, _chip_index(me) * size, size)
        out[n] = v
    return out


def kernel(x, c, ctx, c_ctx, ada_w, ada_b, norm_g, mla_w_in, mla_q_norm, mla_w_uq, mla_kv_norm, mla_w_ukv, mla_w_out, s5_w_in, s5_a_re, s5_a_im, s5_log_step, s5_b_re, s5_b_im, s5_c_re, s5_c_im, s5_d, s5_w_glu, s5_b_glu, s5_w_out, final_g, loss_target, m_c_ctx, m_ada_w, m_ada_b, m_norm_g, m_mla_w_in, m_mla_q_norm, m_mla_w_uq, m_mla_kv_norm, m_mla_w_ukv, m_mla_w_out, m_s5_w_in, m_s5_a_re, m_s5_a_im, m_s5_log_step, m_s5_b_re, m_s5_b_im, m_s5_c_re, m_s5_c_im, m_s5_d, m_s5_w_glu, m_s5_b_glu, m_s5_w_out, m_final_g, v_c_ctx, v_ada_w, v_ada_b, v_norm_g, v_mla_w_in, v_mla_q_norm, v_mla_w_uq, v_mla_kv_norm, v_mla_w_ukv, v_mla_w_out, v_s5_w_in, v_s5_a_re, v_s5_a_im, v_s5_log_step, v_s5_b_re, v_s5_b_im, v_s5_c_re, v_s5_c_im, v_s5_d, v_s5_w_glu, v_s5_b_glu, v_s5_w_out, v_final_g):
    args = dict(locals())
    weights = {n: args[n] for n in WEIGHT_ORDER}
    D = D_MODEL
    xi, yi, ci = _coords()
    chip = 2 * xi + yi
    me = 4 * xi + 2 * yi + ci
    n_col = ada_w.shape[2]

    c_all = allgather_devices(jnp.pad(c, ((0, 7), (0, 0))), "gather_c")[:, 0, :]
    cond = jnp.concatenate([c_all, jnp.broadcast_to(c_ctx[None], (8, D))], axis=0)
    (s_cond,) = rowwise_fwd(lambda v: (_silu(v),), [cond], [], [D], [F32], 16, 0, "cond_silu")
    mod_cols = jnp.stack([mm_nn(s_cond, ada_w[i], name=f"mod_proj{i}") for i in range(2)])
    vec_tiles = [jnp.pad(weights[n][0].reshape(-1, 128), ((0, 6), (0, 0))) for n in SHARDED_VECS]
    mod_all, *vec_all = allgather_chips([mod_cols] + vec_tiles, "gather_mod")
    mod_all = mod_all.transpose(1, 2, 0, 3).reshape(2, 16, 3 * D) + ada_b[:, None, :]
    mod_l = lax.dynamic_index_in_dim(mod_all, me, axis=1, keepdims=False)
    mod_c = mod_all[:, 8, :]
    mod = jnp.stack([mod_c.reshape(2, 3, D), mod_l.reshape(2, 3, D)], axis=1)

    w = gather_weights({n: weights[n][0] for n in SHARDED_MATS})
    for n, v in zip(SHARDED_VECS, vec_all):
        w[n] = v[:, :2, :].reshape(-1)
    for n in ["norm_g", "final_g"]:
        w[n] = weights[n]
    for n in ["mla_q_norm", "mla_kv_norm", "s5_a_re", "s5_a_im", "s5_log_step", "s5_b_re", "s5_b_im",
              "s5_c_re", "s5_c_im"]:
        w[n] = weights[n][0]

    loss_me, dx, dmod, gw = local_step(x[0], ctx[0], loss_target[0], mod, w)

    dmod_rows, loss_all = _gather([dmod.reshape(2, 2, 3 * D), jnp.broadcast_to(loss_me, (8, 128))],
                                  ALL_FLIPS, _dev_index, N_DEV, "gather_dmod")
    loss = functools.reduce(lambda s, d: s + loss_all[d, 0, 0], range(1, N_DEV), loss_all[0, 0, 0])
    dm = jnp.concatenate([dmod_rows[:, :, 1, :], dmod_rows[:, :, 0, :]], axis=0).transpose(1, 0, 2)
    g_ada_b = jnp.sum(dm, axis=1)
    dm_cols = lax.dynamic_slice_in_dim(dm, chip * n_col, n_col, axis=2)
    g_ada_w = jnp.stack([mm_tn(s_cond, dm_cols[i], name=f"mod_proj_dw{i}") for i in range(2)])
    dmc = jnp.sum(dm_cols[:, 8:, :], axis=1)
    dmc8 = jnp.broadcast_to(dmc[:, None, :], (2, 8, n_col))
    g_sc = mm_nt(dmc8[0], ada_w[0], name="mod_proj_dx0")[0] + mm_nt(dmc8[1], ada_w[1], name="mod_proj_dx1")[0]
    g_sc_all = allgather_devices(jnp.broadcast_to(g_sc[None], (8, D)), "gather_dcond")[:, 0, :]
    g_silu_cc = g_sc_all[0] + g_sc_all[2] + g_sc_all[4] + g_sc_all[6]
    (g_c_ctx,) = rowwise_bwd(lambda v: (_silu(v),), [jnp.broadcast_to(c_ctx[None], (8, D))], [],
                             [jnp.broadcast_to(g_silu_cc[None], (8, D))], [0], [], 8, 0, "cond_silu_bwd")
    g_c_ctx = g_c_ctx[0]

    gw_in = {}
    for n in SHARDED_MATS + SHARDED_VECS:
        gw_in[n] = gw[n]
    for n in REPLICATED:
        gw_in[n] = gw[n]
    red = reduce_gradients(gw_in)
    grads = {"c_ctx": g_c_ctx, "ada_w": g_ada_w, "ada_b": g_ada_b}
    for n in WEIGHT_ORDER[3:]:
        grads[n] = red[n].reshape(weights[n].shape)

    deltas, new_m, new_v = {}, {}, {}
    small = [n for n in WEIGHT_ORDER if weights[n].size < 50000]
    for n in WEIGHT_ORDER:
        if n in small:
            continue
        shp = weights[n].shape
        w2 = weights[n].reshape(-1, shp[-1])
        d_, m_, v_ = adamw(w2, grads[n].reshape(w2.shape), args["m_" + n].reshape(w2.shape),
                           args["v_" + n].reshape(w2.shape), name=f"adamw_{n}")
        deltas[n], new_m[n], new_v[n] = d_.reshape(shp), m_.reshape(shp), v_.reshape(shp)
    packs = []
    offs = None
    for src in (weights, grads, {n: args["m_" + n] for n in small}, {n: args["v_" + n] for n in small}):
        buf, offs = pack_flat([src[n] for n in small], F32)
        packs.append(buf)
    outs = adamw(*packs, name="adamw_small")
    for res, dst in zip(outs, (deltas, new_m, new_v)):
        for n, val in zip(small, unpack_flat(res, offs, [weights[n].shape for n in small])):
            dst[n] = val

    return (loss, dx[None], *[grads[n] for n in WEIGHT_ORDER], *[deltas[n] for n in WEIGHT_ORDER],
            *[new_m[n] for n in WEIGHT_ORDER], *[new_v[n] for n in WEIGHT_ORDER])
```

```python
import functools
import math

import jax
import jax.numpy as jnp
import numpy as np
from jax import lax
from jax.experimental import pallas as pl
from jax.experimental.pallas import tpu as pltpu

F32 = jnp.float32
BF16 = jnp.bfloat16

D_MODEL = 1024
GRID_W = 64
EPS = 1e-6
MLA_HEADS = 16
QK_NOPE_DIM = 64
QK_ROPE_DIM = 32
V_HEAD_DIM = 64
Q_LORA_RANK = 256
KV_LORA_RANK = 128
QK_DIM = QK_NOPE_DIM + QK_ROPE_DIM
SOFTMAX_SCALE = QK_DIM ** -0.5
ROPE_THETA = 10000.0
S5_GROUP = 16
S5_GROUPS = D_MODEL // S5_GROUP
S5_STATE = 64
S5_LANES = S5_GROUPS * S5_STATE
N_SEG = 8
GROUPS_PER_BLOCK = 8
N_BLOCKS = S5_GROUPS // GROUPS_PER_BLOCK
BLK_CH = GROUPS_PER_BLOCK * S5_GROUP
BLK_ST = GROUPS_PER_BLOCK * S5_STATE

ADAM_LR = 0.001
ADAM_B1 = 0.9
ADAM_B2 = 0.999
ADAM_EPS = 1e-08
ADAM_WD = 0.01
ADAM_STEP = 10

N_DEV = 8
N_CHIP = 4
MESH = pl.DeviceIdType.MESH
VMEM_LIMIT = 52 * 1024 * 1024
ROW_TILE = 256


def _params(sem=None, vmem=None):
    return pltpu.CompilerParams(dimension_semantics=sem, vmem_limit_bytes=vmem)


def mm_nn(a, b, out_dtype=F32, name="mm_nn"):
    M, K = a.shape
    N = b.shape[1]
    tm = math.gcd(ROW_TILE, M)

    def body(a_ref, b_ref, o_ref):
        o_ref[...] = jnp.dot(a_ref[...].astype(BF16), b_ref[...].astype(BF16),
                             preferred_element_type=F32).astype(o_ref.dtype)

    return pl.pallas_call(
        body, out_shape=jax.ShapeDtypeStruct((M, N), out_dtype), grid=(M // tm,),
        in_specs=[pl.BlockSpec((tm, K), lambda i: (i, 0)), pl.BlockSpec((K, N), lambda i: (0, 0))],
        out_specs=pl.BlockSpec((tm, N), lambda i: (i, 0)),
        compiler_params=_params(("parallel",), VMEM_LIMIT), name=name)(a, b)


def mm_nt(a, b, out_dtype=F32, name="mm_nt"):
    M, N = a.shape
    K = b.shape[0]
    tm = math.gcd(ROW_TILE, M)

    def body(a_ref, b_ref, o_ref):
        o_ref[...] = lax.dot_general(a_ref[...].astype(BF16), b_ref[...].astype(BF16),
                                     (((1,), (1,)), ((), ())),
                                     preferred_element_type=F32).astype(o_ref.dtype)

    return pl.pallas_call(
        body, out_shape=jax.ShapeDtypeStruct((M, K), out_dtype), grid=(M // tm,),
        in_specs=[pl.BlockSpec((tm, N), lambda i: (i, 0)), pl.BlockSpec((K, N), lambda i: (0, 0))],
        out_specs=pl.BlockSpec((tm, K), lambda i: (i, 0)),
        compiler_params=_params(("parallel",), VMEM_LIMIT), name=name)(a, b)


def mm_nt_sum(terms, w, n_rows, name):
    K = w.shape[0]
    tm = math.gcd(ROW_TILE, n_rows, *[t[2] for t in terms])

    def body(*refs):
        i = pl.program_id(0)
        w_ref, o_ref = refs[len(terms)], refs[len(terms) + 1]
        acc = None
        for a_ref, (a, off, first) in zip(refs, terms):
            part = lax.dot_general(a_ref[...].astype(BF16), w_ref[:, off:off + a.shape[1]].astype(BF16), NT_DIMS,
                                   preferred_element_type=F32)
            if first:
                part = jnp.where(i >= first // tm, part, 0.0)
            acc = part if acc is None else acc + part
        o_ref[...] = acc

    def a_spec(a, first):
        skip = first // tm
        return pl.BlockSpec((tm, a.shape[1]), lambda i: (jnp.maximum(i - skip, 0), 0))

    return pl.pallas_call(
        body, out_shape=jax.ShapeDtypeStruct((n_rows, K), F32), grid=(n_rows // tm,),
        in_specs=[a_spec(a, first) for a, _, first in terms] + [pl.BlockSpec(w.shape, lambda i: (0, 0))],
        out_specs=pl.BlockSpec((tm, K), lambda i: (i, 0)),
        compiler_params=_params(("parallel",), VMEM_LIMIT), name=name)(*[t[0] for t in terms], w)


def mm_tn(a, b, name="mm_tn"):
    M, K = a.shape
    N = b.shape[1]
    tn = math.gcd(512, N) if N % 128 == 0 and N > 512 else N

    def body(a_ref, b_ref, o_ref):
        o_ref[...] = lax.dot_general(a_ref[...].astype(BF16), b_ref[...].astype(BF16),
                                     (((0,), (0,)), ((), ())), preferred_element_type=F32)

    return pl.pallas_call(
        body, out_shape=jax.ShapeDtypeStruct((K, N), F32), grid=(N // tn,),
        in_specs=[pl.BlockSpec((M, K), lambda j: (0, 0)), pl.BlockSpec((M, tn), lambda j: (0, j))],
        out_specs=pl.BlockSpec((K, tn), lambda j: (0, j)),
        compiler_params=_params(("parallel",), VMEM_LIMIT), name=name)(a, b)


class Rows:
    def __init__(self, arr, width=None, row_off=0, col_blk=0):
        self.arr = arr
        self.width = arr.shape[1] if width is None else width
        self.row_off = row_off
        self.col_blk = col_blk

    def spec(self, tm):
        ro, cb = self.row_off // tm, self.col_blk
        return pl.BlockSpec((tm, self.width), lambda i: (i + ro, cb))


def _as_rows(x):
    return x if isinstance(x, Rows) else Rows(x)


def _row_tile(n_rows, n_ctx_rows, rows):
    tm = math.gcd(ROW_TILE, n_rows, n_ctx_rows)
    for r in rows:
        tm = math.gcd(tm, r.row_off)
    return tm


def _bc_spec(arr, n_ctx_blocks):
    g, _, d = arr.shape
    if g == 1:
        return pl.BlockSpec((1, 1, d), lambda i: (0, 0, 0))
    return pl.BlockSpec((1, 1, d), lambda i: ((i >= n_ctx_blocks).astype(jnp.int32), 0, 0))


def rowwise_fwd(fn, rows, bcs, out_dims, out_dtypes, n_rows, n_ctx_rows, name):
    rows = [_as_rows(r) for r in rows]
    tm = _row_tile(n_rows, n_ctx_rows, rows)
    ncb = n_ctx_rows // tm
    nr, nb = len(rows), len(bcs)

    def body(*refs):
        vals = [r[...].astype(F32) for r in refs[:nr]] + [b[0].astype(F32) for b in refs[nr:nr + nb]]
        outs = fn(*vals)
        for o_ref, v in zip(refs[nr + nb:], outs):
            o_ref[...] = v.astype(o_ref.dtype)

    outs = pl.pallas_call(
        body,
        out_shape=[jax.ShapeDtypeStruct((n_rows, d), dt) for d, dt in zip(out_dims, out_dtypes)],
        grid=(n_rows // tm,),
        in_specs=[r.spec(tm) for r in rows] + [_bc_spec(b, ncb) for b in bcs],
        out_specs=[pl.BlockSpec((tm, d), lambda i: (i, 0)) for d in out_dims],
        compiler_params=_params(("parallel",), VMEM_LIMIT), name=name)(*[r.arr for r in rows], *bcs)
    return outs


def rowwise_bwd(fn, rows, bcs, cts, diff_rows, diff_bcs, n_rows, n_ctx_rows, name, ct_extra=None, lat_add=None):
    rows = [_as_rows(r) for r in rows]
    cts = [_as_rows(c) for c in cts]
    extra = [_as_rows(ct_extra)] if ct_extra is not None else []
    tm = _row_tile(n_rows, n_ctx_rows, rows + cts + extra)
    ncb = n_ctx_rows // tm
    nr, nb, nc = len(rows), len(bcs), len(cts)
    ndr, ndb = len(diff_rows), len(diff_bcs)
    n_in = nr + nb + nc + len(extra) + (lat_add is not None)

    def body(*refs):
        i = pl.program_id(0)
        rvals = [r[...].astype(F32) for r in refs[:nr]]
        bvals = [b[0].astype(F32) for b in refs[nr:nr + nb]]
        cvals = [c[...].astype(F32) for c in refs[nr + nb:nr + nb + nc]]
        if extra:
            cvals[0] = cvals[0] + refs[nr + nb + nc][...].astype(F32)
        outs = refs[n_in:]

        def f(*d):
            rv, bv = list(rvals), list(bvals)
            for k, idx in enumerate(diff_rows):
                rv[idx] = d[k]
            for k, idx in enumerate(diff_bcs):
                bv[idx] = d[ndr + k]
            return tuple(fn(*rv, *bv))

        primals = [rvals[k] for k in diff_rows] + [bvals[k] for k in diff_bcs]
        _, vjp = jax.vjp(f, *primals)
        grads = list(vjp(tuple(cvals)))
        if lat_add is not None:
            add = refs[n_in - 1][...]
            grads[0] = grads[0] + (add if lat_add.shape[0] == n_rows else jnp.where(i >= ncb, add, 0.0))
        for k in range(ndr):
            outs[k][...] = grads[k].astype(outs[k].dtype)
        for k, idx in enumerate(diff_bcs):
            o_ref = outs[ndr + k]
            first = (i == 0)
            if bcs[idx].shape[0] == 2:
                first = first | (i == ncb)

            @pl.when(first)
            def _(o_ref=o_ref):
                o_ref[...] = jnp.zeros_like(o_ref)

            o_ref[0] += grads[ndr + k]

    out_shape = [jax.ShapeDtypeStruct((n_rows, rows[k].width), F32) for k in diff_rows]
    out_shape += [jax.ShapeDtypeStruct(bcs[k].shape, F32) for k in diff_bcs]
    out_specs = [pl.BlockSpec((tm, rows[k].width), lambda i: (i, 0)) for k in diff_rows]
    out_specs += [_bc_spec(bcs[k], ncb) for k in diff_bcs]
    ins = [r.arr for r in rows] + list(bcs) + [c.arr for c in cts + extra]
    in_specs = [r.spec(tm) for r in rows] + [_bc_spec(b, ncb) for b in bcs] + [c.spec(tm) for c in cts + extra]
    if lat_add is not None:
        ins.append(lat_add)
        skip = ncb if lat_add.shape[0] != n_rows else 0
        in_specs.append(pl.BlockSpec((tm, lat_add.shape[1]), lambda i: (jnp.maximum(i - skip, 0), 0)))
    outs = pl.pallas_call(
        body, out_shape=out_shape, grid=(n_rows // tm,), in_specs=in_specs, out_specs=out_specs,
        compiler_params=_params(("arbitrary",), VMEM_LIMIT), name=name)(*ins)
    return outs


def _rms(x):
    return x * lax.rsqrt(jnp.mean(x * x, axis=-1, keepdims=True) + EPS)


def _sigmoid(x):
    return 0.5 * (jnp.tanh(0.5 * x) + 1.0)


def _silu(x):
    return x * _sigmoid(x)


def _gelu_tanh(x):
    return 0.5 * x * (1.0 + jnp.tanh(math.sqrt(2.0 / math.pi) * (x + 0.044715 * (x * x * x))))


def f_norm_mod(x, g, sc, sh):
    return ((_rms(x) * g) * (1.0 + sc) + sh,)


def f_rms(x, g):
    return (_rms(x) * g,)


def f_gate(o, z):
    return (o * _silu(z),)


def f_s5_act(y, u, d):
    return (_gelu_tanh(y + d * u),)


def f_s5_glu(ya, gl, z, b):
    return (ya * _sigmoid(gl + b) * _silu(z),)


def mla_post_fwd(o, p0, x0, gate, w_out, n_ctx, name="l0_post"):
    n, d = o.shape
    tm = math.gcd(ROW_TILE, n, n_ctx)
    ncb = n_ctx // tm

    def body(o_ref, z_ref, x_ref, gt_ref, w_ref, x1_ref, og_ref, out_ref):
        og = f_gate(o_ref[...], z_ref[...])[0].astype(BF16)
        out = jnp.dot(og, w_ref[...], preferred_element_type=F32)
        og_ref[...] = og
        out_ref[...] = out
        x1_ref[...] = x_ref[...] + gt_ref[0] * out

    row = pl.BlockSpec((tm, d), lambda i: (i, 0))
    return pl.pallas_call(
        body, out_shape=[jax.ShapeDtypeStruct((n, d), F32), jax.ShapeDtypeStruct((n, d), BF16),
                         jax.ShapeDtypeStruct((n, d), F32)],
        grid=(n // tm,),
        in_specs=[row, row, row, _bc_spec(gate, ncb), pl.BlockSpec((d, d), lambda i: (0, 0))],
        out_specs=[row, row, row],
        compiler_params=_params(("parallel",), VMEM_LIMIT), name=name)(o, p0, x0, gate, w_out)


def mla_post_bwd(dx1, out, og, o, p0, gate, w_out, n_ctx, name="l0_post_bwd"):
    n, d = o.shape
    tm = math.gcd(ROW_TILE, n, n_ctx)
    ncb = n_ctx // tm

    def body(dx_ref, out_ref, og_ref, o_ref, z_ref, gt_ref, w_ref, do_ref, dz_ref, dgt_ref, dw_ref):
        i = pl.program_id(0)

        @pl.when(i == 0)
        def _():
            dw_ref[...] = jnp.zeros_like(dw_ref)

        @pl.when((i == 0) | (i == ncb))
        def _():
            dgt_ref[...] = jnp.zeros_like(dgt_ref)

        dx = dx_ref[...]
        dgt_ref[0] += jnp.sum(dx * out_ref[...], axis=0, keepdims=True)
        d_out16 = (gt_ref[0] * dx).astype(BF16)
        dw_ref[...] += lax.dot_general(og_ref[...], d_out16, (((0,), (0,)), ((), ())), preferred_element_type=F32)
        d_og = lax.dot_general(d_out16, w_ref[...], NT_DIMS, preferred_element_type=F32)
        _, gate_vjp = jax.vjp(lambda o_, z_: f_gate(o_, z_), o_ref[...], z_ref[...])
        d_o, d_z = gate_vjp((d_og,))
        do_ref[...] = d_o
        dz_ref[...] = d_z

    row = pl.BlockSpec((tm, d), lambda i: (i, 0))
    mat = pl.BlockSpec((d, d), lambda i: (0, 0))
    return pl.pallas_call(
        body, out_shape=[jax.ShapeDtypeStruct((n, d), F32), jax.ShapeDtypeStruct((n, d), F32),
                         jax.ShapeDtypeStruct(gate.shape, F32), jax.ShapeDtypeStruct((d, d), F32)],
        grid=(n // tm,),
        in_specs=[row, row, row, row, row, _bc_spec(gate, ncb), mat],
        out_specs=[row, row, _bc_spec(gate, ncb), mat],
        compiler_params=_params(("arbitrary",), VMEM_LIMIT), name=name)(dx1, out, og, o, p0, gate, w_out)


def s5_tail(y_ssm, p1, x1p, target, n_ctx, d_vec, b_glu, gate, final_g, w_glu, w_out, name="l1_tail"):
    n, d = y_ssm.shape
    tm = math.gcd(ROW_TILE, n, n_ctx)
    off = n_ctx // tm
    tn_dims = (((0,), (0,)), ((), ()))

    def row_loss(x, g, t):
        e = _rms(x) * g - t
        return 0.5 * (e * e) * (1.0 / d)

    def body(y_ref, u_ref, z_ref, x1_ref, t_ref, d_ref, b_ref, gt_ref, fg_ref, wg_ref, wo_ref,
             l_ref, dx_ref, dy_ref, du_ref, dz_ref, dfg_ref, dgt_ref, db_ref, dd_ref, dwg_ref, dwo_ref):
        @pl.when(pl.program_id(0) == 0)
        def _():
            for r in (l_ref, dfg_ref, dgt_ref, db_ref, dd_ref, dwg_ref, dwo_ref):
                r[...] = jnp.zeros_like(r)

        u, z, tgt, gt = u_ref[...], z_ref[...], t_ref[...], gt_ref[...]
        (ya,), act_vjp = jax.vjp(lambda y_, u_, d_: f_s5_act(y_, u_, d_), y_ref[...], u, d_ref[...])
        ya16 = ya.astype(BF16)
        gl = jnp.dot(ya16, wg_ref[...], preferred_element_type=F32)
        (y3,), glu_vjp = jax.vjp(lambda a_, g_, z_, b_: f_s5_glu(a_, g_, z_, b_), ya, gl, z, b_ref[...])
        y3_16 = y3.astype(BF16)
        out1 = jnp.dot(y3_16, wo_ref[...], preferred_element_type=F32)
        lterm, loss_vjp = jax.vjp(lambda x_, g_: row_loss(x_, g_, tgt), x1_ref[...] + gt * out1, fg_ref[...])
        dx2, dfg = loss_vjp(jnp.ones_like(lterm))
        l_ref[...] += jnp.sum(lterm, axis=0, keepdims=True)
        dfg_ref[...] += dfg
        dx_ref[...] = dx2
        dgt_ref[...] += jnp.sum(dx2 * out1, axis=0, keepdims=True)
        d_out16 = (gt * dx2).astype(BF16)
        dwo_ref[...] += lax.dot_general(y3_16, d_out16, tn_dims, preferred_element_type=F32)
        d_y3 = lax.dot_general(d_out16, wo_ref[...], NT_DIMS, preferred_element_type=F32)
        d_ya, d_gl, d_z, d_b = glu_vjp((d_y3,))
        dz_ref[...] = d_z
        db_ref[...] += d_b
        d_gl16 = d_gl.astype(BF16)
        dwg_ref[...] += lax.dot_general(ya16, d_gl16, tn_dims, preferred_element_type=F32)
        d_ya = d_ya + lax.dot_general(d_gl16, wg_ref[...], NT_DIMS, preferred_element_type=F32)
        d_y, d_u, d_d = act_vjp((d_ya,))
        dy_ref[...] = d_y
        du_ref[...] = d_u
        dd_ref[...] += d_d

    row = pl.BlockSpec((tm, d), lambda i: (i, 0))
    vecs = pl.BlockSpec((1, d), lambda i: (0, 0))
    mat = pl.BlockSpec((d, d), lambda i: (0, 0))
    return pl.pallas_call(
        body,
        out_shape=[jax.ShapeDtypeStruct((1, d), F32)] + [jax.ShapeDtypeStruct((n, d), F32)] * 4
        + [jax.ShapeDtypeStruct((1, d), F32)] * 4 + [jax.ShapeDtypeStruct((d, d), F32)] * 2,
        grid=(n // tm,),
        in_specs=[row, pl.BlockSpec((tm, d), lambda i: (i + off, 0)), pl.BlockSpec((tm, d), lambda i: (i + off, 1)),
                  pl.BlockSpec((tm, d), lambda i: (i + off, 0)), row, vecs, vecs, vecs, vecs, mat, mat],
        out_specs=[vecs, row, row, row, row, vecs, vecs, vecs, vecs, mat, mat],
        compiler_params=_params(("arbitrary",), VMEM_LIMIT), name=name)(
            y_ssm, p1, p1, x1p, target, d_vec, b_glu, gate, final_g, w_glu, w_out)


NT_DIMS = (((1,), (1,)), ((), ()))
HEAD_LANES = 128
N_PAIRS = MLA_HEADS // 2


def _own_lanes(shape, hh):
    lane = lax.broadcasted_iota(jnp.int32, shape, len(shape) - 1)
    return (lane < V_HEAD_DIM) if hh == 0 else (lane >= V_HEAD_DIM)


def _rope_tiles(x, cos, sin_next, sin_prev, inverse):
    width = x.shape[-1]
    reps = width // HEAD_LANES
    c, sn, sp = (jnp.tile(t, (1, reps)) for t in (cos, sin_next, sin_prev))
    if inverse:
        return x * c + pltpu.roll(x * sn, 8, 1) + pltpu.roll(x * sp, width - 8, 1)
    return x * c + pltpu.roll(x, width - 8, 1) * sn + pltpu.roll(x, 8, 1) * sp


def _col_to_row(col):
    n = col.shape[0]
    hi = col.astype(BF16)
    r1 = col - hi.astype(F32)
    mid = r1.astype(BF16)
    lo = (r1 - mid.astype(F32)).astype(BF16)
    lane = lax.broadcasted_iota(jnp.int32, (n, HEAD_LANES), 1)
    terms = jnp.where(lane == 0, hi, jnp.where(lane == 1, mid, jnp.where(lane == 2, lo, jnp.zeros_like(hi))))
    eye = (lax.broadcasted_iota(jnp.int32, (n, n), 0) == lax.broadcasted_iota(jnp.int32, (n, n), 1)).astype(BF16)
    rows = lax.dot_general(terms, eye, (((0,), (0,)), ((), ())), preferred_element_type=F32)
    return rows[0:1] + rows[1:2] + rows[2:3]


def attn_fwd(qb, kb, vb, n_ctx):
    T = qb.shape[0]
    tq = math.gcd(ROW_TILE, n_ctx)
    nq, ncb = T // tq, n_ctx // tq

    def body(q_ref, k_ref, v_ref, o_ref, lse_ref, lse_row_ref):
        qi = pl.program_id(1)

        def rows(n_keys):
            v = v_ref[:n_keys, :]
            outs = []
            for hh in range(2):
                hs = slice(hh * HEAD_LANES, (hh + 1) * HEAD_LANES)
                s = lax.dot_general(q_ref[:, hs], k_ref[:n_keys, hs], NT_DIMS,
                                    preferred_element_type=F32) * SOFTMAX_SCALE
                m = jnp.max(s, axis=-1, keepdims=True)
                p = jnp.exp(s - m)
                l = jnp.sum(p, axis=-1, keepdims=True)
                outs.append(jnp.dot(p.astype(BF16), v, preferred_element_type=F32) / l)
                lse = m + jnp.log(l)
                lse_ref[hh] = lse
                lse_row_ref[hh] = _col_to_row(lse)
            o_ref[...] = jnp.where(_own_lanes(outs[0].shape, 0), outs[0], outs[1])

        pl.when(qi < ncb)(lambda: rows(n_ctx))
        pl.when(qi >= ncb)(lambda: rows(T))

    return pl.pallas_call(
        body,
        out_shape=[jax.ShapeDtypeStruct((T, MLA_HEADS * V_HEAD_DIM), F32),
                   jax.ShapeDtypeStruct((MLA_HEADS, T, 1), F32), jax.ShapeDtypeStruct((MLA_HEADS, 1, T), F32)],
        grid=(N_PAIRS, nq),
        in_specs=[pl.BlockSpec((tq, 2 * HEAD_LANES), lambda h, i: (i, h)),
                  pl.BlockSpec((T, 2 * HEAD_LANES), lambda h, i: (0, h)),
                  pl.BlockSpec((T, 2 * V_HEAD_DIM), lambda h, i: (0, h))],
        out_specs=[pl.BlockSpec((tq, 2 * V_HEAD_DIM), lambda h, i: (i, h)),
                   pl.BlockSpec((2, tq, 1), lambda h, i: (h, i, 0)),
                   pl.BlockSpec((2, 1, tq), lambda h, i: (h, 0, i))],
        compiler_params=_params(("parallel", "parallel"), VMEM_LIMIT), name="attn_fwd")(qb, kb, vb)


def attn_bwd_dq(qb, kb, vb, o, do, lse, tabs, n_ctx):
    T = qb.shape[0]
    tq = math.gcd(ROW_TILE, n_ctx)
    nq, ncb = T // tq, n_ctx // tq

    def body(q_ref, k_ref, v_ref, o_ref, do_ref, lse_ref, c_ref, sn_ref, sp_ref, dq_ref, delta_ref):
        qi = pl.program_id(1)

        def rows(n_keys):
            v = v_ref[:n_keys, :]
            dqs = []
            for hh in range(2):
                hs = slice(hh * HEAD_LANES, (hh + 1) * HEAD_LANES)
                k = k_ref[:n_keys, hs]
                do = jnp.where(_own_lanes(do_ref.shape, hh), do_ref[...], 0.0)
                delta = jnp.sum(do * o_ref[...], axis=-1, keepdims=True)
                s = lax.dot_general(q_ref[:, hs], k, NT_DIMS, preferred_element_type=F32) * SOFTMAX_SCALE
                p = jnp.exp(s - lse_ref[hh])
                dp = lax.dot_general(do.astype(BF16), v, NT_DIMS, preferred_element_type=F32)
                ds = p * (dp - delta) * SOFTMAX_SCALE
                dqs.append(jnp.dot(ds.astype(BF16), k, preferred_element_type=F32))
                delta_ref[hh] = _col_to_row(delta)
            dq = jnp.concatenate(dqs, axis=1)
            dq_ref[...] = _rope_tiles(dq, c_ref[...], sn_ref[...], sp_ref[...], True).astype(BF16)

        pl.when(qi < ncb)(lambda: rows(n_ctx))
        pl.when(qi >= ncb)(lambda: rows(T))

    tab = pl.BlockSpec((tq, HEAD_LANES), lambda h, i: (i, 0))
    return pl.pallas_call(
        body,
        out_shape=[jax.ShapeDtypeStruct((T, MLA_HEADS * HEAD_LANES), BF16),
                   jax.ShapeDtypeStruct((MLA_HEADS, 1, T), F32)],
        grid=(N_PAIRS, nq),
        in_specs=[pl.BlockSpec((tq, 2 * HEAD_LANES), lambda h, i: (i, h)),
                  pl.BlockSpec((T, 2 * HEAD_LANES), lambda h, i: (0, h)),
                  pl.BlockSpec((T, 2 * V_HEAD_DIM), lambda h, i: (0, h)),
                  pl.BlockSpec((tq, 2 * V_HEAD_DIM), lambda h, i: (i, h)),
                  pl.BlockSpec((tq, 2 * V_HEAD_DIM), lambda h, i: (i, h)),
                  pl.BlockSpec((2, tq, 1), lambda h, i: (h, i, 0)), tab, tab, tab],
        out_specs=[pl.BlockSpec((tq, 2 * HEAD_LANES), lambda h, i: (i, h)),
                   pl.BlockSpec((2, 1, tq), lambda h, i: (h, 0, i))],
        compiler_params=_params(("parallel", "parallel"), VMEM_LIMIT), name="attn_bwd_dq")(
            qb, kb, vb, o, do, lse, *tabs)


def attn_bwd_dkv(qb, kb, vb, do, lse_rows, delta_rows, n_ctx):
    T = qb.shape[0]
    tq = math.gcd(ROW_TILE, n_ctx)
    nq, ncb = T // tq, n_ctx // tq

    def body(q_ref, do_ref, lse_ref, delta_ref, k_ref, v_ref, dk_ref, dv_ref):
        kj = pl.program_id(1)

        def cols(first):
            v = v_ref[...]
            do_all = do_ref[first:, :]
            dv = None
            for hh in range(2):
                hs = slice(hh * HEAD_LANES, (hh + 1) * HEAD_LANES)
                k = k_ref[:, hs]
                q = q_ref[first:, hs]
                do16 = jnp.where(_own_lanes(do_all.shape, hh), do_all, 0.0).astype(BF16)
                st = lax.dot_general(k, q, NT_DIMS, preferred_element_type=F32) * SOFTMAX_SCALE
                pt = jnp.exp(st - lse_ref[hh, :, first:])
                dv_h = jnp.dot(pt.astype(BF16), do16, preferred_element_type=F32)
                dv = dv_h if dv is None else dv + dv_h
                dpt = lax.dot_general(v, do16, NT_DIMS, preferred_element_type=F32)
                dst = pt * (dpt - delta_ref[hh, :, first:]) * SOFTMAX_SCALE
                dk_ref[:, hs] = jnp.dot(dst.astype(BF16), q, preferred_element_type=F32)
            dv_ref[...] = dv

        pl.when(kj < ncb)(lambda: cols(0))
        pl.when(kj >= ncb)(lambda: cols(n_ctx))

    return pl.pallas_call(
        body,
        out_shape=[jax.ShapeDtypeStruct((T, MLA_HEADS * HEAD_LANES), F32),
                   jax.ShapeDtypeStruct((T, MLA_HEADS * V_HEAD_DIM), F32)],
        grid=(N_PAIRS, nq),
        in_specs=[pl.BlockSpec((T, 2 * HEAD_LANES), lambda h, j: (0, h)),
                  pl.BlockSpec((T, 2 * V_HEAD_DIM), lambda h, j: (0, h)),
                  pl.BlockSpec((2, 1, T), lambda h, j: (h, 0, 0)),
                  pl.BlockSpec((2, 1, T), lambda h, j: (h, 0, 0)),
                  pl.BlockSpec((tq, 2 * HEAD_LANES), lambda h, j: (j, h)),
                  pl.BlockSpec((tq, 2 * V_HEAD_DIM), lambda h, j: (j, h))],
        out_specs=[pl.BlockSpec((tq, 2 * HEAD_LANES), lambda h, j: (j, h)),
                   pl.BlockSpec((tq, 2 * V_HEAD_DIM), lambda h, j: (j, h))],
        compiler_params=_params(("parallel", "parallel"), VMEM_LIMIT), name="attn_bwd_dkv")(
            qb, do, lse_rows, delta_rows, kb, vb)


def _split_bf16(x):
    hi = x.astype(BF16)
    return hi, (x - hi.astype(F32)).astype(BF16)


def q_heads(qn, w_uq_p, tabs, name="l0_uq"):
    T, K = qn.shape
    N = w_uq_p.shape[1]
    tm = math.gcd(ROW_TILE, T)

    def body(a_ref, w_ref, c_ref, sn_ref, sp_ref, o_ref):
        acc = jnp.dot(a_ref[...], w_ref[...], preferred_element_type=F32)
        o_ref[...] = _rope_tiles(acc, c_ref[...], sn_ref[...], sp_ref[...], False).astype(BF16)

    tab = pl.BlockSpec((tm, HEAD_LANES), lambda i: (i, 0))
    return pl.pallas_call(
        body, out_shape=jax.ShapeDtypeStruct((T, N), BF16), grid=(T // tm,),
        in_specs=[pl.BlockSpec((tm, K), lambda i: (i, 0)), pl.BlockSpec((K, N), lambda i: (0, 0)), tab, tab, tab],
        out_specs=pl.BlockSpec((tm, N), lambda i: (i, 0)),
        compiler_params=_params(("parallel",), VMEM_LIMIT), name=name)(qn, w_uq_p, *tabs)


def kv_heads(kvn, w_kn_p, w_v, kr, spread, tabs, name="l0_ukv"):
    T, K = kvn.shape
    N = w_kn_p.shape[1]
    NV = w_v.shape[1]
    tm = math.gcd(ROW_TILE, T)

    def body(a_ref, wk_ref, wv_ref, kr_ref, e_ref, c_ref, sn_ref, sp_ref, k_ref, v_ref):
        a = a_ref[...]
        hi, lo = _split_bf16(kr_ref[...])
        acc = (jnp.dot(a, wk_ref[...], preferred_element_type=F32)
               + jnp.dot(hi, e_ref[...], preferred_element_type=F32)
               + jnp.dot(lo, e_ref[...], preferred_element_type=F32))
        k_ref[...] = _rope_tiles(acc, c_ref[...], sn_ref[...], sp_ref[...], False).astype(BF16)
        v_ref[...] = jnp.dot(a, wv_ref[...], preferred_element_type=F32).astype(BF16)

    tab = pl.BlockSpec((tm, HEAD_LANES), lambda i: (i, 0))
    return pl.pallas_call(
        body, out_shape=[jax.ShapeDtypeStruct((T, N), BF16), jax.ShapeDtypeStruct((T, NV), BF16)], grid=(T // tm,),
        in_specs=[pl.BlockSpec((tm, K), lambda i: (i, 0)), pl.BlockSpec((K, N), lambda i: (0, 0)),
                  pl.BlockSpec((K, NV), lambda i: (0, 0)), pl.BlockSpec((tm, QK_ROPE_DIM), lambda i: (i, 0)),
                  pl.BlockSpec((QK_ROPE_DIM, N), lambda i: (0, 0)), tab, tab, tab],
        out_specs=[pl.BlockSpec((tm, N), lambda i: (i, 0)), pl.BlockSpec((tm, NV), lambda i: (i, 0))],
        compiler_params=_params(("parallel",), VMEM_LIMIT), name=name)(kvn, w_kn_p, w_v, kr, spread, *tabs)


def heads_unrope(d, tabs, spread=None, name="unrope"):
    T, N = d.shape
    tm = math.gcd(ROW_TILE, T)

    def body(*refs):
        if spread is None:
            d_ref, c_ref, sn_ref, sp_ref, o_ref = refs
        else:
            d_ref, c_ref, sn_ref, sp_ref, e_ref, o_ref, kr_ref = refs
        g = _rope_tiles(d_ref[...], c_ref[...], sn_ref[...], sp_ref[...], True)
        o_ref[...] = g.astype(BF16)
        if spread is not None:
            hi, lo = _split_bf16(g)
            kr_ref[...] = (lax.dot_general(hi, e_ref[...], NT_DIMS, preferred_element_type=F32)
                           + lax.dot_general(lo, e_ref[...], NT_DIMS, preferred_element_type=F32))

    tab = pl.BlockSpec((tm, HEAD_LANES), lambda i: (i, 0))
    row = pl.BlockSpec((tm, N), lambda i: (i, 0))
    ins, in_specs = [d, *tabs], [row, tab, tab, tab]
    out_shape, out_specs = [jax.ShapeDtypeStruct((T, N), BF16)], [row]
    if spread is not None:
        ins.append(spread)
        in_specs.append(pl.BlockSpec(spread.shape, lambda i: (0, 0)))
        out_shape.append(jax.ShapeDtypeStruct((T, spread.shape[0]), F32))
        out_specs.append(pl.BlockSpec((tm, spread.shape[0]), lambda i: (i, 0)))
    return pl.pallas_call(
        body, out_shape=out_shape, grid=(T // tm,), in_specs=in_specs, out_specs=out_specs,
        compiler_params=_params(("parallel",), VMEM_LIMIT), name=name)(*ins)


def _cmul(ar, ai, br, bi):
    return ar * br - ai * bi, ar * bi + ai * br


def s5_chain(finals, s0, a, n_steps, reverse, name):
    W = finals.shape[-1]
    first = N_SEG - 1 if reverse else 0

    def body(f_ref, s0_ref, a_ref, c_ref):
        pr, pi = jnp.ones((1, W), F32), jnp.zeros((1, W), F32)
        br, bi = a_ref[0], a_ref[1]
        n = n_steps
        while n:
            if n & 1:
                pr, pi = _cmul(pr, pi, br, bi)
            br, bi = _cmul(br, bi, br, bi)
            n >>= 1
        fr, fi = f_ref[0], f_ref[1]
        row = lax.broadcasted_iota(jnp.int32, (N_SEG, W), 0)
        s0r = jnp.broadcast_to(s0_ref[0], (N_SEG, W))
        s0i = jnp.broadcast_to(s0_ref[1], (N_SEG, W))
        cr = jnp.where(row == first, s0r, 0.0)
        ci = jnp.where(row == first, s0i, 0.0)
        shift = N_SEG - 1 if reverse else 1
        for _ in range(N_SEG - 1):
            mr, mi = _cmul(pr, pi, cr, ci)
            tr = pltpu.roll(fr + mr, shift, 0)
            ti = pltpu.roll(fi + mi, shift, 0)
            cr = jnp.where(row == first, s0r, tr)
            ci = jnp.where(row == first, s0i, ti)
        c_ref[0] = cr
        c_ref[1] = ci

    return pl.pallas_call(body, out_shape=jax.ShapeDtypeStruct((2, N_SEG, W), F32), name=name)(finals, s0, a)


def _scan_chunk(bur, bui, st_ref, a_ref, n_steps, reverse):
    for lc in range(S5_LANES // BLK_ST):
        sl = slice(lc * BLK_ST, (lc + 1) * BLK_ST)
        lr = jnp.broadcast_to(a_ref[0, :, sl], (N_SEG, BLK_ST))
        li = jnp.broadcast_to(a_ref[1, :, sl], (N_SEG, BLK_ST))

        def step(jj, carry, sl=sl, lr=lr, li=li):
            sr, si = carry
            j = (n_steps - 1 - jj) if reverse else jj
            r0 = pl.multiple_of(j * N_SEG, N_SEG)
            nr = lr * sr - li * si + bur[pl.ds(r0, N_SEG), sl]
            ni = lr * si + li * sr + bui[pl.ds(r0, N_SEG), sl]
            bur[pl.ds(r0, N_SEG), sl] = nr
            bui[pl.ds(r0, N_SEG), sl] = ni
            return nr, ni

        sr, si = lax.fori_loop(0, n_steps, step, (st_ref[0, :, sl], st_ref[1, :, sl]))
        st_ref[0, :, sl] = sr
        st_ref[1, :, sl] = si


def _project_in(x16, w_re, w_im, bur, bui, adjoint):
    for gb in range(N_BLOCKS):
        xb = x16[:, gb * BLK_CH:(gb + 1) * BLK_CH]
        sl = slice(gb * BLK_ST, (gb + 1) * BLK_ST)
        if adjoint:
            dn = (((1,), (1,)), ((), ()))
            bur[:, sl] = lax.dot_general(xb, w_re[gb], dn, preferred_element_type=F32)
            bui[:, sl] = -lax.dot_general(xb, w_im[gb], dn, preferred_element_type=F32)
        else:
            bur[:, sl] = jnp.dot(xb, w_re[gb], preferred_element_type=F32)
            bui[:, sl] = jnp.dot(xb, w_im[gb], preferred_element_type=F32)


def s5_scan(act, w_re, w_im, a, init, *, reverse, adjoint=False, c_re=None, c_im=None, add=None,
            want_ckpt=False, rows=None, name):
    act_off, N = rows if rows is not None else (0, act.shape[0])
    R = math.gcd(ROW_TILE, N, act_off)
    nch, jc = N // R, R // N_SEG
    with_out = c_re is not None

    def chunk(i):
        return (nch - 1 - i) if reverse else i

    def body(*refs):
        act_ref, wre_ref, wim_ref, a_ref, init_ref = refs[:5]
        k = 5
        if with_out:
            cre_ref, cim_ref = refs[k:k + 2]
            k += 2
        if add is not None:
            add_ref = refs[k]
            k += 1
        if with_out:
            out_ref = refs[k]
            k += 1
        if want_ckpt:
            ck_ref = refs[k]
            k += 1
        fin_ref, bur, bui = refs[k:k + 3]

        @pl.when(pl.program_id(0) == 0)
        def _():
            fin_ref[...] = init_ref[...]

        if want_ckpt:
            ck_ref[0] = fin_ref[...]
        _project_in(act_ref[...].astype(BF16), wre_ref, wim_ref, bur, bui, adjoint)
        _scan_chunk(bur, bui, fin_ref, a_ref, jc, reverse)
        if with_out:
            for gb in range(N_BLOCKS):
                sl = slice(gb * BLK_ST, (gb + 1) * BLK_ST)
                y = (jnp.dot(bur[:, sl].astype(BF16), cre_ref[gb], preferred_element_type=F32)
                     - jnp.dot(bui[:, sl].astype(BF16), cim_ref[gb], preferred_element_type=F32))
                cs = slice(gb * BLK_CH, (gb + 1) * BLK_CH)
                if add is not None:
                    y = y + add_ref[:, cs]
                out_ref[:, cs] = y

    row_spec = pl.BlockSpec((R, D_MODEL), lambda i: (chunk(i), 0))
    act_spec = pl.BlockSpec((R, D_MODEL), lambda i: (chunk(i) + act_off // R, 0))
    w_spec = pl.BlockSpec(w_re.shape, lambda i: (0, 0, 0))
    st_spec = pl.BlockSpec((2, N_SEG, S5_LANES), lambda i: (0, 0, 0))
    ins = [act, w_re, w_im, a, init]
    in_specs = [act_spec, w_spec, w_spec, pl.BlockSpec((2, 1, S5_LANES), lambda i: (0, 0, 0)), st_spec]
    if with_out:
        ins += [c_re, c_im]
        in_specs += [pl.BlockSpec(c_re.shape, lambda i: (0, 0, 0))] * 2
    if add is not None:
        ins.append(add)
        in_specs.append(row_spec)
    out_shape, out_specs = [], []
    if with_out:
        out_shape.append(jax.ShapeDtypeStruct((N, D_MODEL), F32))
        out_specs.append(row_spec)
    if want_ckpt:
        out_shape.append(jax.ShapeDtypeStruct((nch, 2, N_SEG, S5_LANES), F32))
        out_specs.append(pl.BlockSpec((1, 2, N_SEG, S5_LANES), lambda i: (chunk(i), 0, 0, 0)))
    out_shape.append(jax.ShapeDtypeStruct((2, N_SEG, S5_LANES), F32))
    out_specs.append(st_spec)
    res = pl.pallas_call(
        body, out_shape=out_shape, grid=(nch,), in_specs=in_specs, out_specs=out_specs,
        scratch_shapes=[pltpu.VMEM((R, S5_LANES), F32), pltpu.VMEM((R, S5_LANES), F32)],
        compiler_params=_params(("arbitrary",), VMEM_LIMIT), name=name)(*ins)
    res = list(res)
    out = res.pop(0) if with_out else None
    ckpt = res.pop(0) if want_ckpt else None
    return out, ckpt, res[0]


def s5_grads(dy, u, ckpt, b_re, b_im, c_re, c_im, lam, init_adj, *, reverse, add=None, u_off=0, name):
    N = dy.shape[0]
    R = math.gcd(ROW_TILE, N, u_off)
    nch, jc = N // R, R // N_SEG
    W = S5_LANES

    def chunk(i):
        return i if reverse else (nch - 1 - i)

    def body(*refs):
        dy_ref, u_ref, ck_ref, bre_ref, bim_ref, cre_ref, cim_ref, lam_ref, init_ref = refs[:9]
        k = 9
        if add is not None:
            add_ref = refs[k]
            k += 1
        du_ref, dlam_ref, dbre_ref, dbim_ref, dcre_ref, dcim_ref, fin_ref = refs[k:k + 7]
        sr_buf, si_buf, er_buf, ei_buf, st_buf = refs[k + 7:k + 12]

        @pl.when(pl.program_id(0) == 0)
        def _():
            fin_ref[...] = init_ref[...]
            dlam_ref[...] = jnp.zeros_like(dlam_ref)
            dbre_ref[...] = jnp.zeros_like(dbre_ref)
            dbim_ref[...] = jnp.zeros_like(dbim_ref)
            dcre_ref[...] = jnp.zeros_like(dcre_ref)
            dcim_ref[...] = jnp.zeros_like(dcim_ref)

        u16 = u_ref[...].astype(BF16)
        dy16 = dy_ref[...].astype(BF16)
        st_buf[...] = ck_ref[0]
        _project_in(u16, bre_ref, bim_ref, sr_buf, si_buf, False)
        _scan_chunk(sr_buf, si_buf, st_buf, lam_ref, jc, reverse)
        _project_in(dy16, cre_ref, cim_ref, er_buf, ei_buf, True)
        for lc in range(W // BLK_ST):
            sl = slice(lc * BLK_ST, (lc + 1) * BLK_ST)
            lr = jnp.broadcast_to(lam_ref[0, :, sl], (N_SEG, BLK_ST))
            li = jnp.broadcast_to(lam_ref[1, :, sl], (N_SEG, BLK_ST))

            def one(r0, spr, spi, carry, sl=sl, lr=lr, li=li):
                gr, gi, ar, ai = carry
                nr = er_buf[pl.ds(r0, N_SEG), sl] + lr * gr + li * gi
                ni = ei_buf[pl.ds(r0, N_SEG), sl] + lr * gi - li * gr
                er_buf[pl.ds(r0, N_SEG), sl] = nr
                ei_buf[pl.ds(r0, N_SEG), sl] = ni
                return nr, ni, ar + spr * nr + spi * ni, ai + spr * ni - spi * nr

            def step(ff, carry, sl=sl, one=one):
                f = jc - 1 - ff
                j = (jc - 1 - f) if reverse else f
                jp = (j + 1) if reverse else (j - 1)
                r0 = pl.multiple_of(j * N_SEG, N_SEG)
                p0 = pl.multiple_of(jp * N_SEG, N_SEG)
                return one(r0, sr_buf[pl.ds(p0, N_SEG), sl], si_buf[pl.ds(p0, N_SEG), sl], carry)

            carry = (fin_ref[0, :, sl], fin_ref[1, :, sl], dlam_ref[0, :, sl], dlam_ref[1, :, sl])
            carry = lax.fori_loop(0, jc - 1, step, carry)
            r_first = (jc - 1) * N_SEG if reverse else 0
            gr, gi, ar, ai = one(r_first, ck_ref[0, 0, :, sl], ck_ref[0, 1, :, sl], carry)
            fin_ref[0, :, sl] = gr
            fin_ref[1, :, sl] = gi
            dlam_ref[0, :, sl] = ar
            dlam_ref[1, :, sl] = ai
        tn = (((0,), (0,)), ((), ()))
        nt = (((1,), (1,)), ((), ()))
        for gb in range(N_BLOCKS):
            sl = slice(gb * BLK_ST, (gb + 1) * BLK_ST)
            cs = slice(gb * BLK_CH, (gb + 1) * BLK_CH)
            gr16 = er_buf[:, sl].astype(BF16)
            gi16 = ei_buf[:, sl].astype(BF16)
            du = (lax.dot_general(gr16, bre_ref[gb], nt, preferred_element_type=F32)
                  + lax.dot_general(gi16, bim_ref[gb], nt, preferred_element_type=F32))
            if add is not None:
                du = du + add_ref[:, cs]
            du_ref[:, cs] = du
            ub, dyb = u16[:, cs], dy16[:, cs]
            dbre_ref[gb] += lax.dot_general(ub, gr16, tn, preferred_element_type=F32)
            dbim_ref[gb] += lax.dot_general(ub, gi16, tn, preferred_element_type=F32)
            dcre_ref[gb] += lax.dot_general(sr_buf[:, sl].astype(BF16), dyb, tn, preferred_element_type=F32)
            dcim_ref[gb] -= lax.dot_general(si_buf[:, sl].astype(BF16), dyb, tn, preferred_element_type=F32)

    row_spec = pl.BlockSpec((R, D_MODEL), lambda i: (chunk(i), 0))
    st_spec = pl.BlockSpec((2, N_SEG, W), lambda i: (0, 0, 0))
    wb_spec = pl.BlockSpec(b_re.shape, lambda i: (0, 0, 0))
    wc_spec = pl.BlockSpec(c_re.shape, lambda i: (0, 0, 0))
    ins = [dy, u, ckpt, b_re, b_im, c_re, c_im, lam, init_adj]
    u_spec = pl.BlockSpec((R, D_MODEL), lambda i: (chunk(i) + u_off // R, 0))
    in_specs = [row_spec, u_spec, pl.BlockSpec((1, 2, N_SEG, W), lambda i: (chunk(i), 0, 0, 0)),
                wb_spec, wb_spec, wc_spec, wc_spec, pl.BlockSpec((2, 1, W), lambda i: (0, 0, 0)), st_spec]
    if add is not None:
        ins.append(add)
        in_specs.append(row_spec)
    out_shape = [jax.ShapeDtypeStruct((N, D_MODEL), F32), jax.ShapeDtypeStruct((2, N_SEG, W), F32),
                 jax.ShapeDtypeStruct(b_re.shape, F32), jax.ShapeDtypeStruct(b_re.shape, F32),
                 jax.ShapeDtypeStruct(c_re.shape, F32), jax.ShapeDtypeStruct(c_re.shape, F32),
                 jax.ShapeDtypeStruct((2, N_SEG, W), F32)]
    out_specs = [row_spec, st_spec, wb_spec, wb_spec, wc_spec, wc_spec, st_spec]
    return pl.pallas_call(
        body, out_shape=out_shape, grid=(nch,), in_specs=in_specs, out_specs=out_specs,
        scratch_shapes=[pltpu.VMEM((R, W), F32) for _ in range(4)] + [pltpu.VMEM((2, N_SEG, W), F32)],
        compiler_params=_params(("arbitrary",), VMEM_LIMIT), name=name)(*ins)


def adamw(w, g, m, v, name="adamw"):
    n, d = w.shape
    lanes = -(-d // 128) * 128
    tm = n
    while tm * lanes * 4 > (1 << 20) and tm % 16 == 0:
        tm //= 2
    c1 = 1.0 - ADAM_B1 ** ADAM_STEP
    c2 = 1.0 - ADAM_B2 ** ADAM_STEP

    def body(w_ref, g_ref, m_ref, v_ref, d_ref, nm_ref, nv_ref):
        g_ = g_ref[...]
        m_ = ADAM_B1 * m_ref[...] + (1.0 - ADAM_B1) * g_
        v_ = ADAM_B2 * v_ref[...] + (1.0 - ADAM_B2) * (g_ * g_)
        d_ref[...] = -ADAM_LR * ((m_ / c1) / (jnp.sqrt(v_ / c2) + ADAM_EPS) + ADAM_WD * w_ref[...])
        nm_ref[...] = m_
        nv_ref[...] = v_

    spec = pl.BlockSpec((tm, d), lambda i: (i, 0))
    return pl.pallas_call(
        body, out_shape=[jax.ShapeDtypeStruct((n, d), F32)] * 3, grid=(n // tm,),
        in_specs=[spec] * 4, out_specs=[spec] * 3,
        compiler_params=_params(("parallel",), VMEM_LIMIT), name=name)(w, g, m, v)


def _coords():
    return lax.axis_index("x"), lax.axis_index("y"), lax.axis_index("c")


def exchange(arrays, out_shapes, remote, local, name, aliases=None):
    n_in, n_out, n_rem, n_loc = len(arrays), len(out_shapes), len(remote), len(local)

    def at(ref, idx):
        return ref if idx is None else ref.at[idx]

    def body(*refs):
        ins, outs = refs[:n_in], refs[n_in:n_in + n_out]
        send_sems, recv_sems, local_sems = refs[n_in + n_out:]
        me = _coords()
        sends, recvs = [], []
        for k, (flip, ii, src_at, oi, dst_at) in enumerate(remote):
            peer = (me[0] ^ flip[0], me[1] ^ flip[1], me[2] ^ flip[2])
            src = at(ins[ii], src_at(me, peer))
            sends.append(pltpu.make_async_remote_copy(
                src_ref=src, dst_ref=at(outs[oi], dst_at(me)), send_sem=send_sems.at[k], recv_sem=recv_sems.at[k],
                device_id=peer, device_id_type=MESH))
            recvs.append(pltpu.make_async_remote_copy(
                src_ref=src, dst_ref=at(outs[oi], dst_at(peer)), send_sem=send_sems.at[k], recv_sem=recv_sems.at[k],
                device_id=peer, device_id_type=MESH))
        locs = [pltpu.make_async_copy(at(ins[ii], src_at(me)), at(outs[oi], dst_at(me)), local_sems.at[k])
                for k, (ii, src_at, oi, dst_at) in enumerate(local)]
        for cp in locs + sends:
            cp.start()
        for cp in recvs:
            cp.wait_recv()
        for cp in sends:
            cp.wait_send()
        for cp in locs:
            cp.wait()

    hbm = pl.BlockSpec(memory_space=pl.ANY)
    return pl.pallas_call(
        body, out_shape=list(out_shapes), in_specs=[hbm] * n_in, out_specs=[hbm] * n_out,
        scratch_shapes=[pltpu.SemaphoreType.DMA((n_rem,)), pltpu.SemaphoreType.DMA((n_rem,)),
                        pltpu.SemaphoreType.DMA((max(n_loc, 1),))],
        input_output_aliases=aliases or {}, name=name)(*arrays)


ALL_FLIPS = [(dx, dy, dc) for dx in (0, 1) for dy in (0, 1) for dc in (0, 1)][1:]
CHIP_FLIPS = [(1, 0, 0), (0, 1, 0), (1, 1, 0)]
CORE_FLIP = (0, 0, 1)


def _dev_index(p):
    return 4 * p[0] + 2 * p[1] + p[2]


def _chip_index(p):
    return 2 * p[0] + p[1]


def _gather(xs, flips, index, n, name):
    arrays = [x[None] for x in xs]
    outs = [jax.ShapeDtypeStruct((n,) + x.shape, x.dtype) for x in xs]
    remote = [(f, a, lambda me, peer: (0,), a, lambda s: (index(s),)) for a in range(len(xs)) for f in flips]
    local = [(a, lambda me: (0,), a, lambda me: (index(me),)) for a in range(len(xs))]
    return exchange(arrays, outs, remote, local, name)


def allgather_devices(x, name):
    return _gather([x], ALL_FLIPS, _dev_index, N_DEV, name)[0]


def allgather_chips(xs, name):
    return _gather(xs, CHIP_FLIPS, _chip_index, N_CHIP, name)


def gather_halves(xs, name):
    n = len(xs)
    nk = n * len(CHIP_FLIPS)

    def body(*refs):
        ins, outs = refs[:n], refs[n:2 * n]
        ici_send, ici_recv, d2d_send, d2d_recv = refs[2 * n:]
        me = _coords()
        sibling = (me[0], me[1], 1 - me[2])
        first, passed, landed = [], [], []
        for a in range(n):
            half = ins[a].shape[0] // 2
            mine = ins[a].at[pl.ds(pl.multiple_of(me[2] * half, 16), half)]
            for j, flip in enumerate(CHIP_FLIPS):
                k = a * len(CHIP_FLIPS) + j
                peer = (me[0] ^ flip[0], me[1] ^ flip[1], me[2])
                first.append(pltpu.make_async_remote_copy(
                    src_ref=mine, dst_ref=outs[a].at[_chip_index(me), me[2]], send_sem=ici_send.at[k],
                    recv_sem=ici_recv.at[k], device_id=peer, device_id_type=MESH))
                arrived = outs[a].at[_chip_index(peer), me[2]]
                landed.append(pltpu.make_async_remote_copy(
                    src_ref=mine, dst_ref=arrived, send_sem=ici_send.at[k], recv_sem=ici_recv.at[k],
                    device_id=peer, device_id_type=MESH))
                passed.append(pltpu.make_async_remote_copy(
                    src_ref=arrived, dst_ref=arrived, send_sem=d2d_send.at[k], recv_sem=d2d_recv.at[k],
                    device_id=sibling, device_id_type=MESH))
        for cp in first:
            cp.start()
        for k in range(nk):
            landed[k].wait_recv()
            passed[k].start()
        for a in range(n):
            for j, flip in enumerate(CHIP_FLIPS):
                k = a * len(CHIP_FLIPS) + j
                peer_chip = _chip_index((me[0] ^ flip[0], me[1] ^ flip[1]))
                from_sibling = outs[a].at[peer_chip, 1 - me[2]]
                pltpu.make_async_remote_copy(
                    src_ref=from_sibling, dst_ref=from_sibling, send_sem=d2d_send.at[k], recv_sem=d2d_recv.at[k],
                    device_id=sibling, device_id_type=MESH).wait_recv()
        for cp in first + passed:
            cp.wait_send()

    hbm = pl.BlockSpec(memory_space=pl.ANY)
    return pl.pallas_call(
        body, out_shape=[jax.ShapeDtypeStruct((N_CHIP, 2, x.shape[0] // 2, x.shape[1]), x.dtype) for x in xs],
        in_specs=[hbm] * n, out_specs=[hbm] * n,
        scratch_shapes=[pltpu.SemaphoreType.DMA((nk,)) for _ in range(4)], name=name)(*xs)


HBM_SPEC = pl.BlockSpec(memory_space=pltpu.HBM)
SEM_SPEC = pl.BlockSpec(memory_space=pltpu.SEMAPHORE)
DATAFLOW = pltpu.SideEffectType.DATAFLOW_SIDE_EFFECTING


def gather_chips_start(xs, name):
    n = len(xs)
    nk = n * len(CHIP_FLIPS)

    def body(*refs):
        srcs, lands = refs[:n], refs[n:2 * n]
        send_sems, recv_sems = refs[2 * n], refs[2 * n + 1]
        token = refs[-1]
        me = _coords()
        for a in range(n):
            for j, flip in enumerate(CHIP_FLIPS):
                peer = (me[0] ^ flip[0], me[1] ^ flip[1], me[2])
                pltpu.make_async_remote_copy(
                    src_ref=srcs[a], dst_ref=lands[a].at[_chip_index(me)], send_sem=send_sems.at[a * 3 + j],
                    recv_sem=recv_sems.at[a * 3 + j], device_id=peer, device_id_type=MESH).start()
        token[...] = jnp.zeros_like(token)

    lands = [lax.empty((N_CHIP,) + x.shape, x.dtype) for x in xs]
    out = pl.pallas_call(
        body, name=name,
        out_shape=(pltpu.SemaphoreType.DMA((nk,)), pltpu.SemaphoreType.DMA((nk,)),
                   *[pltpu.HBM(x.shape, x.dtype) for x in xs], *[pltpu.HBM(l.shape, l.dtype) for l in lands],
                   jax.ShapeDtypeStruct((8, 128), F32)),
        in_specs=[HBM_SPEC] * (2 * n),
        out_specs=(SEM_SPEC, SEM_SPEC, *[HBM_SPEC] * (2 * n), pl.BlockSpec(memory_space=pltpu.VMEM)),
        input_output_aliases={a: 2 + a for a in range(2 * n)},
        compiler_params=pltpu.CompilerParams(has_side_effects=DATAFLOW),
    )(*[pltpu.with_memory_space_constraint(x, pltpu.HBM) for x in xs],
      *[pltpu.with_memory_space_constraint(l, pltpu.HBM) for l in lands])
    return out[0], out[1], list(out[2:2 + n]), list(out[2 + n:2 + 2 * n]), out[-1]


def gather_chips_wait(send_sems, recv_sems, xs, lands, after, name):
    n = len(xs)

    def body(*refs):
        srcs, lnds = refs[:n], refs[n:2 * n]
        s_sems, r_sems = refs[2 * n], refs[2 * n + 1]
        me = _coords()
        for a in range(n):
            for j, flip in enumerate(CHIP_FLIPS):
                peer = (me[0] ^ flip[0], me[1] ^ flip[1], me[2])
                copy = pltpu.make_async_remote_copy(
                    src_ref=srcs[a], dst_ref=lnds[a].at[_chip_index(peer)], send_sem=s_sems.at[a * 3 + j],
                    recv_sem=r_sems.at[a * 3 + j], device_id=peer, device_id_type=MESH)
                copy.wait_send()
                copy.wait_recv()

    out = pl.pallas_call(
        body, name=name,
        out_shape=(*[pltpu.HBM(x.shape, x.dtype) for x in xs], *[pltpu.HBM(l.shape, l.dtype) for l in lands]),
        in_specs=[HBM_SPEC] * (2 * n) + [SEM_SPEC, SEM_SPEC, pl.BlockSpec(memory_space=pl.ANY)],
        out_specs=tuple([HBM_SPEC] * (2 * n)),
        input_output_aliases={a: a for a in range(2 * n)},
        compiler_params=pltpu.CompilerParams(has_side_effects=DATAFLOW),
    )(*xs, *lands, send_sems, recv_sems, after)
    return list(out[:n]), list(out[n:])


def _half_tile(h, cd):
    return h if h * cd * 4 <= (1 << 20) else math.gcd(512, h)


def pair_add(g, got, core, out_dtype, name):
    _, _, h, cd = g.shape
    th = _half_tile(h, cd)

    def body(c_ref, g_ref, got_ref, o_ref):
        o_ref[0] = (g_ref[0, 0] + got_ref[0]).astype(o_ref.dtype)

    return pl.pallas_call(
        body, out_shape=jax.ShapeDtypeStruct((N_CHIP, h, cd), out_dtype),
        grid_spec=pltpu.PrefetchScalarGridSpec(
            num_scalar_prefetch=1, grid=(N_CHIP, h // th),
            in_specs=[pl.BlockSpec((1, 1, th, cd), lambda q, i, c: (q, c[0], i, 0)),
                      pl.BlockSpec((1, th, cd), lambda q, i, c: (q, i, 0))],
            out_specs=pl.BlockSpec((1, th, cd), lambda q, i, c: (q, i, 0))),
        compiler_params=_params(("parallel", "parallel"), VMEM_LIMIT), name=name)(core, g, got)


def sum_chips(parts, core, name):
    _, h, cd = parts.shape
    th = _half_tile(h, cd)

    def body(c_ref, p_ref, o_ref):
        acc = p_ref[0].astype(F32)
        for q in range(1, N_CHIP):
            acc = acc + p_ref[q].astype(F32)
        o_ref[0] = acc

    return pl.pallas_call(
        body, out_shape=jax.ShapeDtypeStruct((2, h, cd), F32),
        grid_spec=pltpu.PrefetchScalarGridSpec(
            num_scalar_prefetch=1, grid=(h // th,),
            in_specs=[pl.BlockSpec((N_CHIP, th, cd), lambda i, c: (0, i, 0))],
            out_specs=pl.BlockSpec((1, th, cd), lambda i, c: (c[0], i, 0))),
        compiler_params=_params(("parallel",), VMEM_LIMIT), name=name)(core, parts)


def to_segments(a, n_ctx):
    def one(p):
        n = p.shape[0]
        return p.reshape(N_SEG, n // N_SEG, -1).transpose(1, 0, 2).reshape(n, -1)
    return jnp.concatenate([one(a[:n_ctx]), one(a[n_ctx:])], axis=0) if n_ctx else one(a)


def from_segments(a, n_ctx):
    def one(p):
        n = p.shape[0]
        return p.reshape(n // N_SEG, N_SEG, -1).transpose(1, 0, 2).reshape(n, -1)
    return jnp.concatenate([one(a[:n_ctx]), one(a[n_ctx:])], axis=0) if n_ctx else one(a)


def rope_tables(n_ctx, n_lat):
    f32 = np.float32
    rows = n_lat // GRID_W
    row = np.repeat(np.arange(rows), GRID_W).astype(f32)
    col = np.tile(np.arange(GRID_W), rows).astype(f32)
    d = QK_ROPE_DIM // 2
    inv = (f32(1.0) / np.power(f32(ROPE_THETA), np.arange(0, d, 2, dtype=f32) / f32(d))).astype(f32)
    ang = np.concatenate([row[:, None] * inv[None, :], col[:, None] * inv[None, :]], axis=1).astype(f32)
    cos = np.concatenate([np.ones((n_ctx, d), f32), np.cos(ang)], axis=0)
    sin = np.concatenate([np.zeros((n_ctx, d), f32), np.sin(ang)], axis=0)
    q = QK_ROPE_DIM // 4
    T = n_ctx + n_lat
    ones, zeros = np.ones((T, QK_NOPE_DIM), f32), np.zeros((T, QK_NOPE_DIM), f32)
    tail, z8 = np.zeros((T, HEAD_LANES - QK_DIM), f32), np.zeros((T, q), f32)
    cr, cc, sr, sc = cos[:, :q], cos[:, q:], sin[:, :q], sin[:, q:]
    cos_t = np.concatenate([ones, cr, cr, cc, cc, tail], axis=1)
    sin_next = np.concatenate([zeros, -sr, z8, -sc, z8, tail], axis=1)
    sin_prev = np.concatenate([zeros, z8, sr, z8, sc, tail], axis=1)
    return tuple(jnp.asarray(t, F32) for t in (cos_t, sin_next, sin_prev))


def pad_heads(w, used):
    k = w.shape[0]
    return jnp.pad(w.reshape(k, MLA_HEADS, used), ((0, 0), (0, 0), (0, HEAD_LANES - used))).reshape(k, -1)


def unpad_heads(w, used):
    k = w.shape[0]
    return w.reshape(k, MLA_HEADS, HEAD_LANES)[:, :, :used].reshape(k, MLA_HEADS * used)


def rotary_spread():
    lane = np.arange(MLA_HEADS * HEAD_LANES) % HEAD_LANES
    return jnp.asarray(lane[None, :] == (QK_NOPE_DIM + np.arange(QK_ROPE_DIM))[:, None], BF16)


def s5_discretise(a_re, a_im, log_step, b_re, b_im):
    dt = jnp.exp(log_step)[:, None]
    mag = jnp.exp(a_re * dt)
    lb_re = mag * jnp.cos(a_im * dt)
    lb_im = mag * jnp.sin(a_im * dt)
    den = a_re * a_re + a_im * a_im
    nr = lb_re - 1.0
    f_re = ((nr * a_re + lb_im * a_im) / den)[..., None]
    f_im = ((lb_im * a_re - nr * a_im) / den)[..., None]
    return lb_re, lb_im, f_re * b_re - f_im * b_im, f_re * b_im + f_im * b_re


def s5_block_weights(lb_re, lb_im, bb_re, bb_im, c_re, c_im):
    eye = jnp.eye(GROUPS_PER_BLOCK, dtype=F32)
    lam = jnp.stack([lb_re.reshape(1, S5_LANES), lb_im.reshape(1, S5_LANES)])

    def b_blocks(bb):
        t = bb.reshape(N_BLOCKS, GROUPS_PER_BLOCK, S5_STATE, S5_GROUP)
        return jnp.einsum("bgpc,gh->bgchp", t, eye).reshape(N_BLOCKS, BLK_CH, BLK_ST).astype(BF16)

    def c_blocks(cc):
        t = cc.reshape(N_BLOCKS, GROUPS_PER_BLOCK, S5_GROUP, S5_STATE)
        return jnp.einsum("bgcp,gh->bgphc", t, eye).reshape(N_BLOCKS, BLK_ST, BLK_CH).astype(BF16)

    return lam, b_blocks(bb_re), b_blocks(bb_im), c_blocks(c_re), c_blocks(c_im)


def b_block_diag(db):
    t = db.reshape(N_BLOCKS, GROUPS_PER_BLOCK, S5_GROUP, GROUPS_PER_BLOCK, S5_STATE)
    return jnp.einsum("bgchp,gh->bgpc", t, jnp.eye(GROUPS_PER_BLOCK, dtype=F32)).reshape(S5_GROUPS, S5_STATE, S5_GROUP)


def c_block_diag(dc):
    t = dc.reshape(N_BLOCKS, GROUPS_PER_BLOCK, S5_STATE, GROUPS_PER_BLOCK, S5_GROUP)
    return jnp.einsum("bgphc,gh->bgcp", t, jnp.eye(GROUPS_PER_BLOCK, dtype=F32)).reshape(S5_GROUPS, S5_GROUP, S5_STATE)


def conj(a):
    return jnp.stack([a[0], -a[1]])


PACK_TILE = 16 * 128


def pack_flat(parts, dtype):
    flat = [p.reshape(-1).astype(dtype) for p in parts]
    sizes = [f.shape[0] for f in flat]
    total = sum(sizes)
    pad = (-total) % PACK_TILE
    if pad:
        flat.append(jnp.zeros((pad,), dtype))
    offs = np.cumsum([0] + sizes)[:-1].tolist()
    return jnp.concatenate(flat).reshape(-1, 128), offs


def unpack_flat(buf, offs, shapes):
    flat = buf.reshape(-1)
    return [flat[o:o + int(np.prod(s))].reshape(s) for o, s in zip(offs, shapes)]


def s5_forward(p1, n_ctx, dirs):
    saved = []
    y = None
    ctx_rows, lat_rows = (0, n_ctx), (n_ctx, p1.shape[0] - n_ctx)
    zeros_tile = jnp.zeros((2, N_SEG, S5_LANES), F32)
    zeros_row = jnp.zeros((2, 1, S5_LANES), F32)
    for k, (lam, b_re, b_im, c_re, c_im) in enumerate(dirs):
        rev = k == 1
        last = 0 if rev else N_SEG - 1
        _, _, fin = s5_scan(p1, b_re, b_im, lam, zeros_tile, reverse=rev, rows=ctx_rows, name=f"s5_ctx_finals{k}")
        carry_c = s5_chain(fin, zeros_row, lam, n_ctx // N_SEG, rev, name=f"s5_ctx_chain{k}")
        _, ck_c, fin_c = s5_scan(p1, b_re, b_im, lam, carry_c, reverse=rev, want_ckpt=True, rows=ctx_rows,
                                 name=f"s5_ctx_scan{k}")
        s0 = fin_c[:, last:last + 1, :]
        _, _, fin = s5_scan(p1, b_re, b_im, lam, zeros_tile, reverse=rev, rows=lat_rows, name=f"s5_lat_finals{k}")
        carry_l = s5_chain(fin, s0, lam, lat_rows[1] // N_SEG, rev, name=f"s5_lat_chain{k}")
        y, ck_l, _ = s5_scan(p1, b_re, b_im, lam, carry_l, reverse=rev, c_re=c_re, c_im=c_im, add=y,
                             want_ckpt=True, rows=lat_rows, name=f"s5_lat_scan{k}")
        saved.append((ck_c, ck_l))
    return y, saved


def s5_backward(dy_l, du_extra_l, p1, n_ctx, dirs, saved):
    n_lat = p1.shape[0] - n_ctx
    zeros_tile = jnp.zeros((2, N_SEG, S5_LANES), F32)
    zeros_row = jnp.zeros((2, 1, S5_LANES), F32)
    dy_c = jnp.zeros((n_ctx, D_MODEL), F32)
    du_l, du_c = du_extra_l, None
    grads = []
    for k, (lam, b_re, b_im, c_re, c_im) in enumerate(dirs):
        rev = k == 1
        lam_c = conj(lam)
        ck_c, ck_l = saved[k]
        first = N_SEG - 1 if rev else 0
        _, _, fin = s5_scan(dy_l, c_re, c_im, lam_c, zeros_tile, reverse=not rev, adjoint=True,
                            name=f"s5_lat_adj_finals{k}")
        carry = s5_chain(fin, zeros_row, lam_c, n_lat // N_SEG, not rev, name=f"s5_lat_adj_chain{k}")
        du_l, dlam_l, dbr_l, dbi_l, dcr_l, dci_l, fin_a = s5_grads(
            dy_l, p1, ck_l, b_re, b_im, c_re, c_im, lam, carry, reverse=rev, add=du_l, u_off=n_ctx,
            name=f"s5_lat_grads{k}")
        g0 = fin_a[:, first:first + 1, :]
        carry = s5_chain(zeros_tile, g0, lam_c, n_ctx // N_SEG, not rev, name=f"s5_ctx_adj_chain{k}")
        du_c, dlam_c, dbr_c, dbi_c, _, _, _ = s5_grads(
            dy_c, p1, ck_c, b_re, b_im, c_re, c_im, lam, carry, reverse=rev, add=du_c, name=f"s5_ctx_grads{k}")
        dlam = jnp.sum(dlam_l + dlam_c, axis=1)
        grads.append((dlam, b_block_diag(dbr_l + dbr_c), b_block_diag(dbi_l + dbi_c),
                      c_block_diag(dcr_l), c_block_diag(dci_l)))
    return jnp.concatenate([du_c, du_l], axis=0), grads


def local_step(x, ctx, target, mod, w, late=None):
    L, Lc = x.shape[0], ctx.shape[0]
    T = L + Lc
    assert L % Lc == 0 and Lc % (2 * N_SEG) == 0 and L % GRID_W == 0
    D = D_MODEL
    X0 = jnp.concatenate([ctx, x], axis=0)
    if late is not None:
        X0 = X0 + late[0][0, 0]

    def mod_of(i, j):
        return mod[i, :, j, :][:, None, :]

    def vec(v):
        return v.reshape(1, 1, -1).astype(F32)

    g0 = vec(w["norm_g"][0])
    (H0,) = rowwise_fwd(f_norm_mod, [X0], [g0, mod_of(0, 1), mod_of(0, 0)], [D], [BF16], T, Lc, "l0_norm")
    p0 = mm_nn(H0, w["mla_w_in"], name="l0_in")
    cq = Rows(p0, Q_LORA_RANK, col_blk=D // Q_LORA_RANK)
    ckv = Rows(p0, KV_LORA_RANK, col_blk=(D + Q_LORA_RANK) // KV_LORA_RANK)
    kr = p0[:, D + Q_LORA_RANK + KV_LORA_RANK:D + P0_HEAD]
    qng, kvng = vec(w["mla_q_norm"]), vec(w["mla_kv_norm"])
    (qn,) = rowwise_fwd(f_rms, [cq], [qng], [Q_LORA_RANK], [BF16], T, 0, "l0_qnorm")
    (kvn,) = rowwise_fwd(f_rms, [ckv], [kvng], [KV_LORA_RANK], [BF16], T, 0, "l0_kvnorm")
    tabs = rope_tables(Lc, L)
    spread = rotary_spread()
    w_uq_p = pad_heads(w["mla_w_uq"], QK_DIM)
    w_ukv3 = w["mla_w_ukv"].reshape(KV_LORA_RANK, MLA_HEADS, QK_NOPE_DIM + V_HEAD_DIM)
    w_kn_p = pad_heads(w_ukv3[:, :, :QK_NOPE_DIM].reshape(KV_LORA_RANK, -1), QK_NOPE_DIM)
    w_v = w_ukv3[:, :, QK_NOPE_DIM:].reshape(KV_LORA_RANK, -1)
    qb = q_heads(qn, w_uq_p, tabs)
    kb, vb = kv_heads(kvn, w_kn_p, w_v, kr, spread, tabs)
    o, lse, lse_rows = attn_fwd(qb, kb, vb, Lc)
    X1, og, out0 = mla_post_fwd(o, p0, X0, mod_of(0, 2), w["mla_w_out"], Lc)

    if late is not None:
        w = {**w, **late[1](X1)}
    X1p = to_segments(X1, Lc)
    tgt_p = to_segments(target, 0)
    g1 = vec(w["norm_g"][1])
    (H1,) = rowwise_fwd(f_norm_mod, [X1p], [g1, mod_of(1, 1), mod_of(1, 0)], [D], [BF16], T, Lc, "l1_norm")
    p1 = mm_nn(H1, w["s5_w_in"], name="l1_in")
    disc_fn = lambda *a: tuple(zip(*[s5_discretise(a[0][k], a[1][k], a[2][k], a[3][k], a[4][k]) for k in range(2)]))
    disc, disc_vjp = jax.vjp(disc_fn, w["s5_a_re"], w["s5_a_im"], w["s5_log_step"], w["s5_b_re"], w["s5_b_im"])
    dirs = [s5_block_weights(disc[0][k], disc[1][k], disc[2][k], disc[3][k], w["s5_c_re"][k], w["s5_c_im"][k])
            for k in range(2)]
    y_ssm, s5_saved = s5_forward(p1, Lc, dirs)

    row = lambda v: v.reshape(1, D).astype(F32)
    (lvec, dX2, d_yssm, d_u_act, d_z1, d_fg, d_gt1, d_bg, d_d, gw_glu, gw_out) = s5_tail(
        y_ssm, p1, X1p, tgt_p, Lc, row(w["s5_d"]), row(w["s5_b_glu"]), mod[1, 1:2, 2, :], row(w["final_g"]),
        w["s5_w_glu"], w["s5_w_out"])
    loss = jnp.sum(lvec)
    gw = {"final_g": d_fg.reshape(D), "s5_b_glu": d_bg.reshape(D), "s5_d": d_d.reshape(D),
          "s5_w_glu": gw_glu, "s5_w_out": gw_out}
    dmod = {}

    du_p, s5_g = s5_backward(d_yssm, d_u_act, p1, Lc, dirs, s5_saved)
    d_disc = tuple(tuple(s5_g[k][j - 1].reshape(disc[j][k].shape) if j >= 2 else
                         s5_g[k][0][j].reshape(disc[j][k].shape) for k in range(2)) for j in range(4))
    gw["s5_a_re"], gw["s5_a_im"], gw["s5_log_step"], gw["s5_b_re"], gw["s5_b_im"] = disc_vjp(d_disc)
    gw["s5_c_re"] = jnp.stack([s5_g[0][3], s5_g[1][3]])
    gw["s5_c_im"] = jnp.stack([s5_g[0][4], s5_g[1][4]])
    d_H1 = mm_nt_sum([(du_p, 0, 0), (d_z1, D, Lc)], w["s5_w_in"], T, "l1_in_dx")
    gw["s5_w_in"] = jnp.concatenate([mm_tn(H1, du_p, name="l1_in_dw_u"), mm_tn(H1[Lc:], d_z1, name="l1_in_dw_z")],
                                    axis=1)
    d_X1p, d_g1, d_sc1, d_sh1 = rowwise_bwd(f_norm_mod, [X1p], [g1, mod_of(1, 1), mod_of(1, 0)], [d_H1],
                                            [0], [0, 1, 2], T, Lc, "l1_norm_bwd", lat_add=dX2)
    d_gt1_full = jnp.concatenate([jnp.zeros((1, 1, D), F32), d_gt1[None]], axis=0)
    dmod[1] = (d_sh1, d_sc1, d_gt1_full)
    d_X1 = from_segments(d_X1p, Lc)

    d_o, d_z0, d_gt0, gw["mla_w_out"] = mla_post_bwd(d_X1, out0, og, o, p0, mod_of(0, 2), w["mla_w_out"], Lc)
    d_q, delta_rows = attn_bwd_dq(qb, kb, vb, o, d_o, lse, tabs, Lc)
    dk_p, d_v = attn_bwd_dkv(qb, kb, vb, d_o, lse_rows, delta_rows, Lc)
    d_k, d_kr = heads_unrope(dk_p, tabs, jnp.pad(spread, ((0, HEAD_LANES - QK_ROPE_DIM), (0, 0))),
                             name="l0_k_unrope")
    d_qn = mm_nt(d_q, w_uq_p, name="l0_uq_dx")
    gw["mla_w_uq"] = unpad_heads(mm_tn(qn, d_q, name="l0_uq_dw"), QK_DIM)
    d_kvn = mm_nt(d_k, w_kn_p, name="l0_ukn_dx") + mm_nt(d_v, w_v, name="l0_uv_dx")
    dw_kn = unpad_heads(mm_tn(kvn, d_k, name="l0_ukn_dw"), QK_NOPE_DIM).reshape(KV_LORA_RANK, MLA_HEADS, QK_NOPE_DIM)
    dw_v = mm_tn(kvn, d_v, name="l0_uv_dw").reshape(KV_LORA_RANK, MLA_HEADS, V_HEAD_DIM)
    gw["mla_w_ukv"] = jnp.concatenate([dw_kn, dw_v], axis=-1).reshape(KV_LORA_RANK, -1)
    d_cq, d_qng = rowwise_bwd(f_rms, [cq], [qng], [d_qn], [0], [0], T, 0, "l0_qnorm_bwd")
    d_ckv, d_kvng = rowwise_bwd(f_rms, [ckv], [kvng], [d_kvn], [0], [0], T, 0, "l0_kvnorm_bwd")
    gw["mla_q_norm"] = d_qng.reshape(-1)
    gw["mla_kv_norm"] = d_kvng.reshape(-1)
    o_cq, o_ckv = D, D + Q_LORA_RANK
    o_kr = o_ckv + KV_LORA_RANK
    d_H0 = mm_nt_sum([(d_z0, 0, 0), (d_cq, o_cq, 0), (d_ckv, o_ckv, 0), (d_kr, o_kr, 0)], w["mla_w_in"], T, "l0_in_dx")
    d_head = jnp.concatenate([d_cq, d_ckv, d_kr], axis=1)
    gw["mla_w_in"] = jnp.concatenate([mm_tn(H0, d_head, name="l0_in_dw_head")[:, :P0_HEAD],
                                      mm_tn(H0, d_z0, name="l0_in_dw_z")], axis=1)
    d_X0, d_g0, d_sc0, d_sh0 = rowwise_bwd(f_norm_mod, [X0], [g0, mod_of(0, 1), mod_of(0, 0)], [d_H0],
                                           [0], [0, 1, 2], T, Lc, "l0_norm_bwd", lat_add=d_X1)
    dmod[0] = (d_sh0, d_sc0, d_gt0)
    gw["norm_g"] = jnp.stack([d_g0.reshape(D), d_g1.reshape(D)])
    dx = d_X0[Lc:]
    dmod_arr = jnp.stack([jnp.stack([dmod[i][j][:, 0, :] for j in range(3)], axis=1) for i in range(2)])
    return loss, dx, dmod_arr, gw


SHARDED = {
    "mla_w_in": 1, "mla_w_uq": 1, "mla_w_ukv": 1, "mla_w_out": 0,
    "s5_w_in": 1, "s5_w_glu": 0, "s5_w_out": 0, "s5_d": 0, "s5_b_glu": 0,
}
SHARDED_MATS = ["mla_w_in", "mla_w_uq", "mla_w_ukv", "mla_w_out", "s5_w_in", "s5_w_glu", "s5_w_out"]
SHARDED_VECS = ["s5_d", "s5_b_glu"]
REPLICATED = ["norm_g", "mla_q_norm", "mla_kv_norm", "s5_a_re", "s5_a_im", "s5_log_step", "s5_b_re", "s5_b_im",
              "s5_c_re", "s5_c_im", "final_g"]
WEIGHT_ORDER = ["c_ctx", "ada_w", "ada_b", "norm_g", "mla_w_in", "mla_q_norm", "mla_w_uq", "mla_kv_norm", "mla_w_ukv",
                "mla_w_out", "s5_w_in", "s5_a_re", "s5_a_im", "s5_log_step", "s5_b_re", "s5_b_im", "s5_c_re", "s5_c_im",
                "s5_d", "s5_w_glu", "s5_b_glu", "s5_w_out", "final_g"]


P0_HEAD = Q_LORA_RANK + KV_LORA_RANK + QK_ROPE_DIM


P0_WIDTH = 1536


def w_in_to_kernel_order(w):
    pad = jnp.zeros((w.shape[0], P0_WIDTH - w.shape[1]), w.dtype)
    return jnp.concatenate([w[:, P0_HEAD:], w[:, :P0_HEAD], pad], axis=1)


LAYER0_MATS = ["mla_w_in", "mla_w_uq", "mla_w_ukv", "mla_w_out"]
LAYER1_MATS = ["s5_w_in", "s5_w_glu", "s5_w_out"]


def _whole_matrices(names, own_blocks, gathered):
    chip = _chip_index(_coords())
    full = {}
    for n, own, o in zip(names, own_blocks, gathered):
        slot = lax.broadcasted_iota(jnp.int32, (N_CHIP, 1, 1), 0)
        o = jnp.where(slot == chip, own[None], o.reshape((N_CHIP,) + own.shape))
        full[n] = o.reshape(-1, o.shape[-1]) if SHARDED[n] == 0 else o.transpose(1, 0, 2).reshape(o.shape[1], -1)
    return full


def gather_weights(ws):
    mats = [ws[n].astype(BF16) for n in LAYER0_MATS]
    full = _whole_matrices(LAYER0_MATS, mats, gather_halves(mats, "gather_weights"))
    full["mla_w_in"] = w_in_to_kernel_order(full["mla_w_in"])
    return full


def gather_weights_behind(ws):
    mats = [ws[n].astype(BF16) for n in LAYER1_MATS]
    send_sems, recv_sems, srcs, lands, token = gather_chips_start(mats, "gather_l1_start")

    def finish(after):
        own, got = gather_chips_wait(send_sems, recv_sems, srcs, lands, after, "gather_l1_wait")
        return _whole_matrices(LAYER1_MATS, own, got)

    return token, finish


def reduce_gradients(gw):
    me = _coords()
    core = me[2].reshape(1).astype(jnp.int32)
    slots = {}
    for n in SHARDED_MATS:
        g = gw[n]
        if SHARDED[n] == 0:
            slots[n] = g.reshape(N_CHIP, 2, g.shape[0] // (2 * N_CHIP), g.shape[1])
        else:
            k, n4 = g.shape
            slots[n] = g.reshape(k, N_CHIP, n4 // N_CHIP).transpose(1, 0, 2).reshape(N_CHIP, 2, k // 2, n4 // N_CHIP)
    small_names = REPLICATED + SHARDED_VECS
    small, small_offs = pack_flat([gw[n].astype(F32) for n in small_names], F32)
    small = jnp.pad(small, ((0, (-small.shape[0]) % (N_CHIP * 32)), (0, 0)))
    slots["small"] = small.reshape(N_CHIP, 2, -1, 128)
    names = list(slots)
    idx = range(len(names))
    got = exchange(
        [slots[n] for n in names],
        [jax.ShapeDtypeStruct((N_CHIP,) + slots[n].shape[2:], F32) for n in names],
        [(CORE_FLIP, i, lambda me, peer: (slice(None), 1 - me[2]), i, lambda s: None) for i in idx], [],
        "grads_swap_in")
    sums = [pair_add(slots[n], g, core, F32 if n == "small" else BF16, f"grads_pair_{n}") for n, g in zip(names, got)]
    parts = exchange(
        sums, [jax.ShapeDtypeStruct(s.shape, s.dtype) for s in sums],
        [(f, i, lambda me, peer: (_chip_index(peer),), i, lambda s: (_chip_index(s),)) for i in idx for f in CHIP_FLIPS],
        [(i, lambda me: (_chip_index(me),), i, lambda me: (_chip_index(me),)) for i in idx],
        "grads_scatter")
    halves = [sum_chips(p, core, f"grads_sum_{n}") for n, p in zip(names, parts)]
    fulls = exchange(
        halves, [jax.ShapeDtypeStruct(h.shape, F32) for h in halves],
        [(CORE_FLIP, i, lambda me, peer: (me[2],), i, lambda s: (s[2],)) for i in idx], [],
        "grads_swap_out", aliases={i: i for i in idx})
    out = {n: f.reshape(-1, f.shape[-1]) for n, f in zip(names, fulls)}
    (small_all,) = allgather_chips([out.pop("small")], "grads_gather_small")
    vals = unpack_flat(small_all, small_offs, [gw[n].shape for n in small_names])
    for n, v in zip(small_names, vals):
        if n in SHARDED_VECS:
            size = v.shape[0] // N_CHIP
            v = lax.dynamic_slice_in_dim(v, _chip_index(me) * size, size)
        out[n] = v
    return out


def kernel(x, c, ctx, c_ctx, ada_w, ada_b, norm_g, mla_w_in, mla_q_norm, mla_w_uq, mla_kv_norm, mla_w_ukv, mla_w_out, s5_w_in, s5_a_re, s5_a_im, s5_log_step, s5_b_re, s5_b_im, s5_c_re, s5_c_im, s5_d, s5_w_glu, s5_b_glu, s5_w_out, final_g, loss_target, m_c_ctx, m_ada_w, m_ada_b, m_norm_g, m_mla_w_in, m_mla_q_norm, m_mla_w_uq, m_mla_kv_norm, m_mla_w_ukv, m_mla_w_out, m_s5_w_in, m_s5_a_re, m_s5_a_im, m_s5_log_step, m_s5_b_re, m_s5_b_im, m_s5_c_re, m_s5_c_im, m_s5_d, m_s5_w_glu, m_s5_b_glu, m_s5_w_out, m_final_g, v_c_ctx, v_ada_w, v_ada_b, v_norm_g, v_mla_w_in, v_mla_q_norm, v_mla_w_uq, v_mla_kv_norm, v_mla_w_ukv, v_mla_w_out, v_s5_w_in, v_s5_a_re, v_s5_a_im, v_s5_log_step, v_s5_b_re, v_s5_b_im, v_s5_c_re, v_s5_c_im, v_s5_d, v_s5_w_glu, v_s5_b_glu, v_s5_w_out, v_final_g):
    args = dict(locals())
    weights = {n: args[n] for n in WEIGHT_ORDER}
    D = D_MODEL
    xi, yi, ci = _coords()
    chip = 2 * xi + yi
    me = 4 * xi + 2 * yi + ci
    n_col = ada_w.shape[2]

    late = gather_weights_behind({n: weights[n][0] for n in LAYER1_MATS})

    c_all = allgather_devices(jnp.pad(c, ((0, 7), (0, 0))), "gather_c")[:, 0, :]
    cond = jnp.concatenate([c_all, jnp.broadcast_to(c_ctx[None], (8, D))], axis=0)
    (s_cond,) = rowwise_fwd(lambda v: (_silu(v),), [cond], [], [D], [F32], 16, 0, "cond_silu")
    mod_cols = jnp.stack([mm_nn(s_cond, ada_w[i], name=f"mod_proj{i}") for i in range(2)])
    vec_tiles = [jnp.pad(weights[n][0].reshape(-1, 128), ((0, 6), (0, 0))) for n in SHARDED_VECS]
    mod_all, *vec_all = allgather_chips([mod_cols] + vec_tiles, "gather_mod")
    mod_all = mod_all.transpose(1, 2, 0, 3).reshape(2, 16, 3 * D) + ada_b[:, None, :]
    mod_l = lax.dynamic_index_in_dim(mod_all, me, axis=1, keepdims=False)
    mod_c = mod_all[:, 8, :]
    mod = jnp.stack([mod_c.reshape(2, 3, D), mod_l.reshape(2, 3, D)], axis=1)

    w = gather_weights({n: weights[n][0] for n in LAYER0_MATS})
    for n, v in zip(SHARDED_VECS, vec_all):
        w[n] = v[:, :2, :].reshape(-1)
    for n in ["norm_g", "final_g"]:
        w[n] = weights[n]
    for n in ["mla_q_norm", "mla_kv_norm", "s5_a_re", "s5_a_im", "s5_log_step", "s5_b_re", "s5_b_im",
              "s5_c_re", "s5_c_im"]:
        w[n] = weights[n][0]

    loss_me, dx, dmod, gw = local_step(x[0], ctx[0], loss_target[0], mod, w, late)

    dmod_rows, loss_all = _gather([dmod.reshape(2, 2, 3 * D), jnp.broadcast_to(loss_me, (8, 128))],
                                  ALL_FLIPS, _dev_index, N_DEV, "gather_dmod")
    loss = functools.reduce(lambda s, d: s + loss_all[d, 0, 0], range(1, N_DEV), loss_all[0, 0, 0])
    dm = jnp.concatenate([dmod_rows[:, :, 1, :], dmod_rows[:, :, 0, :]], axis=0).transpose(1, 0, 2)
    g_ada_b = jnp.sum(dm, axis=1)
    dm_cols = lax.dynamic_slice_in_dim(dm, chip * n_col, n_col, axis=2)
    g_ada_w = jnp.stack([mm_tn(s_cond, dm_cols[i], name=f"mod_proj_dw{i}") for i in range(2)])
    dmc = jnp.sum(dm_cols[:, 8:, :], axis=1)
    dmc8 = jnp.broadcast_to(dmc[:, None, :], (2, 8, n_col))
    g_sc = mm_nt(dmc8[0], ada_w[0], name="mod_proj_dx0")[0] + mm_nt(dmc8[1], ada_w[1], name="mod_proj_dx1")[0]
    g_sc_all = allgather_devices(jnp.broadcast_to(g_sc[None], (8, D)), "gather_dcond")[:, 0, :]
    g_silu_cc = g_sc_all[0] + g_sc_all[2] + g_sc_all[4] + g_sc_all[6]
    (g_c_ctx,) = rowwise_bwd(lambda v: (_silu(v),), [jnp.broadcast_to(c_ctx[None], (8, D))], [],
                             [jnp.broadcast_to(g_silu_cc[None], (8, D))], [0], [], 8, 0, "cond_silu_bwd")
    g_c_ctx = g_c_ctx[0]

    gw_in = {}
    for n in SHARDED_MATS + SHARDED_VECS:
        gw_in[n] = gw[n]
    for n in REPLICATED:
        gw_in[n] = gw[n]
    red = reduce_gradients(gw_in)
    grads = {"c_ctx": g_c_ctx, "ada_w": g_ada_w, "ada_b": g_ada_b}
    for n in WEIGHT_ORDER[3:]:
        grads[n] = red[n].reshape(weights[n].shape)

    deltas, new_m, new_v = {}, {}, {}
    small = [n for n in WEIGHT_ORDER if weights[n].size < 50000]
    for n in WEIGHT_ORDER:
        if n in small:
            continue
        shp = weights[n].shape
        w2 = weights[n].reshape(-1, shp[-1])
        d_, m_, v_ = adamw(w2, grads[n].reshape(w2.shape), args["m_" + n].reshape(w2.shape),
                           args["v_" + n].reshape(w2.shape), name=f"adamw_{n}")
        deltas[n], new_m[n], new_v[n] = d_.reshape(shp), m_.reshape(shp), v_.reshape(shp)
    packs = []
    offs = None
    for src in (weights, grads, {n: args["m_" + n] for n in small}, {n: args["v_" + n] for n in small}):
        buf, offs = pack_flat([src[n] for n in small], F32)
        packs.append(buf)
    outs = adamw(*packs, name="adamw_small")
    for res, dst in zip(outs, (deltas, new_m, new_v)):
        for n, val in zip(small, unpack_flat(res, offs, [weights[n].shape for n in small])):
            dst[n] = val

    return (loss, dx[None], *[grads[n] for n in WEIGHT_ORDER], *[deltas[n] for n in WEIGHT_ORDER],
            *[new_m[n] for n in WEIGHT_ORDER], *[new_v[n] for n in WEIGHT_ORDER])
```

```python
import functools
import math

import jax
import jax.numpy as jnp
import numpy as np
from jax import lax
from jax.experimental import pallas as pl
from jax.experimental.pallas import tpu as pltpu

F32 = jnp.float32
BF16 = jnp.bfloat16

D_MODEL = 1024
GRID_W = 64
EPS = 1e-6
MLA_HEADS = 16
QK_NOPE_DIM = 64
QK_ROPE_DIM = 32
V_HEAD_DIM = 64
Q_LORA_RANK = 256
KV_LORA_RANK = 128
QK_DIM = QK_NOPE_DIM + QK_ROPE_DIM
SOFTMAX_SCALE = QK_DIM ** -0.5
ROPE_THETA = 10000.0
S5_GROUP = 16
S5_GROUPS = D_MODEL // S5_GROUP
S5_STATE = 64
S5_LANES = S5_GROUPS * S5_STATE
N_SEG = 8
GROUPS_PER_BLOCK = 8
N_BLOCKS = S5_GROUPS // GROUPS_PER_BLOCK
BLK_CH = GROUPS_PER_BLOCK * S5_GROUP
BLK_ST = GROUPS_PER_BLOCK * S5_STATE

ADAM_LR = 0.001
ADAM_B1 = 0.9
ADAM_B2 = 0.999
ADAM_EPS = 1e-08
ADAM_WD = 0.01
ADAM_STEP = 10

N_DEV = 8
N_CHIP = 4
MESH = pl.DeviceIdType.MESH
VMEM_LIMIT = 52 * 1024 * 1024
ROW_TILE = 256


def _params(sem=None, vmem=None):
    return pltpu.CompilerParams(dimension_semantics=sem, vmem_limit_bytes=vmem)


def mm_nn(a, b, out_dtype=F32, name="mm_nn"):
    M, K = a.shape
    N = b.shape[1]
    tm = math.gcd(ROW_TILE, M)

    def body(a_ref, b_ref, o_ref):
        o_ref[...] = jnp.dot(a_ref[...].astype(BF16), b_ref[...].astype(BF16),
                             preferred_element_type=F32).astype(o_ref.dtype)

    return pl.pallas_call(
        body, out_shape=jax.ShapeDtypeStruct((M, N), out_dtype), grid=(M // tm,),
        in_specs=[pl.BlockSpec((tm, K), lambda i: (i, 0)), pl.BlockSpec((K, N), lambda i: (0, 0))],
        out_specs=pl.BlockSpec((tm, N), lambda i: (i, 0)),
        compiler_params=_params(("parallel",), VMEM_LIMIT), name=name)(a, b)


def mm_nt(a, b, out_dtype=F32, name="mm_nt"):
    M, N = a.shape
    K = b.shape[0]
    tm = math.gcd(ROW_TILE, M)

    def body(a_ref, b_ref, o_ref):
        o_ref[...] = lax.dot_general(a_ref[...].astype(BF16), b_ref[...].astype(BF16),
                                     (((1,), (1,)), ((), ())),
                                     preferred_element_type=F32).astype(o_ref.dtype)

    return pl.pallas_call(
        body, out_shape=jax.ShapeDtypeStruct((M, K), out_dtype), grid=(M // tm,),
        in_specs=[pl.BlockSpec((tm, N), lambda i: (i, 0)), pl.BlockSpec((K, N), lambda i: (0, 0))],
        out_specs=pl.BlockSpec((tm, K), lambda i: (i, 0)),
        compiler_params=_params(("parallel",), VMEM_LIMIT), name=name)(a, b)


def mm_nt_sum(terms, w, n_rows, name):
    K = w.shape[0]
    tm = math.gcd(ROW_TILE, n_rows, *[t[2] for t in terms])

    def body(*refs):
        i = pl.program_id(0)
        w_ref, o_ref = refs[len(terms)], refs[len(terms) + 1]
        acc = None
        for a_ref, (a, off, first) in zip(refs, terms):
            part = lax.dot_general(a_ref[...].astype(BF16), w_ref[:, off:off + a.shape[1]].astype(BF16), NT_DIMS,
                                   preferred_element_type=F32)
            if first:
                part = jnp.where(i >= first // tm, part, 0.0)
            acc = part if acc is None else acc + part
        o_ref[...] = acc

    def a_spec(a, first):
        skip = first // tm
        return pl.BlockSpec((tm, a.shape[1]), lambda i: (jnp.maximum(i - skip, 0), 0))

    return pl.pallas_call(
        body, out_shape=jax.ShapeDtypeStruct((n_rows, K), F32), grid=(n_rows // tm,),
        in_specs=[a_spec(a, first) for a, _, first in terms] + [pl.BlockSpec(w.shape, lambda i: (0, 0))],
        out_specs=pl.BlockSpec((tm, K), lambda i: (i, 0)),
        compiler_params=_params(("parallel",), VMEM_LIMIT), name=name)(*[t[0] for t in terms], w)


def mm_tn(a, b, name="mm_tn"):
    M, K = a.shape
    N = b.shape[1]
    tn = math.gcd(512, N) if N % 128 == 0 and N > 512 else N

    def body(a_ref, b_ref, o_ref):
        o_ref[...] = lax.dot_general(a_ref[...].astype(BF16), b_ref[...].astype(BF16),
                                     (((0,), (0,)), ((), ())), preferred_element_type=F32)

    return pl.pallas_call(
        body, out_shape=jax.ShapeDtypeStruct((K, N), F32), grid=(N // tn,),
        in_specs=[pl.BlockSpec((M, K), lambda j: (0, 0)), pl.BlockSpec((M, tn), lambda j: (0, j))],
        out_specs=pl.BlockSpec((K, tn), lambda j: (0, j)),
        compiler_params=_params(("parallel",), VMEM_LIMIT), name=name)(a, b)


class Rows:
    def __init__(self, arr, width=None, row_off=0, col_blk=0):
        self.arr = arr
        self.width = arr.shape[1] if width is None else width
        self.row_off = row_off
        self.col_blk = col_blk

    def spec(self, tm):
        ro, cb = self.row_off // tm, self.col_blk
        return pl.BlockSpec((tm, self.width), lambda i: (i + ro, cb))


def _as_rows(x):
    return x if isinstance(x, Rows) else Rows(x)


def _row_tile(n_rows, n_ctx_rows, rows):
    tm = math.gcd(ROW_TILE, n_rows, n_ctx_rows)
    for r in rows:
        tm = math.gcd(tm, r.row_off)
    return tm


def _bc_spec(arr, n_ctx_blocks):
    g, _, d = arr.shape
    if g == 1:
        return pl.BlockSpec((1, 1, d), lambda i: (0, 0, 0))
    return pl.BlockSpec((1, 1, d), lambda i: ((i >= n_ctx_blocks).astype(jnp.int32), 0, 0))


def rowwise_fwd(fn, rows, bcs, out_dims, out_dtypes, n_rows, n_ctx_rows, name):
    rows = [_as_rows(r) for r in rows]
    tm = _row_tile(n_rows, n_ctx_rows, rows)
    ncb = n_ctx_rows // tm
    nr, nb = len(rows), len(bcs)

    def body(*refs):
        vals = [r[...].astype(F32) for r in refs[:nr]] + [b[0].astype(F32) for b in refs[nr:nr + nb]]
        outs = fn(*vals)
        for o_ref, v in zip(refs[nr + nb:], outs):
            o_ref[...] = v.astype(o_ref.dtype)

    outs = pl.pallas_call(
        body,
        out_shape=[jax.ShapeDtypeStruct((n_rows, d), dt) for d, dt in zip(out_dims, out_dtypes)],
        grid=(n_rows // tm,),
        in_specs=[r.spec(tm) for r in rows] + [_bc_spec(b, ncb) for b in bcs],
        out_specs=[pl.BlockSpec((tm, d), lambda i: (i, 0)) for d in out_dims],
        compiler_params=_params(("parallel",), VMEM_LIMIT), name=name)(*[r.arr for r in rows], *bcs)
    return outs


def rowwise_bwd(fn, rows, bcs, cts, diff_rows, diff_bcs, n_rows, n_ctx_rows, name, ct_extra=None, lat_add=None):
    rows = [_as_rows(r) for r in rows]
    cts = [_as_rows(c) for c in cts]
    extra = [_as_rows(ct_extra)] if ct_extra is not None else []
    tm = _row_tile(n_rows, n_ctx_rows, rows + cts + extra)
    ncb = n_ctx_rows // tm
    nr, nb, nc = len(rows), len(bcs), len(cts)
    ndr, ndb = len(diff_rows), len(diff_bcs)
    n_in = nr + nb + nc + len(extra) + (lat_add is not None)

    def body(*refs):
        i = pl.program_id(0)
        rvals = [r[...].astype(F32) for r in refs[:nr]]
        bvals = [b[0].astype(F32) for b in refs[nr:nr + nb]]
        cvals = [c[...].astype(F32) for c in refs[nr + nb:nr + nb + nc]]
        if extra:
            cvals[0] = cvals[0] + refs[nr + nb + nc][...].astype(F32)
        outs = refs[n_in:]

        def f(*d):
            rv, bv = list(rvals), list(bvals)
            for k, idx in enumerate(diff_rows):
                rv[idx] = d[k]
            for k, idx in enumerate(diff_bcs):
                bv[idx] = d[ndr + k]
            return tuple(fn(*rv, *bv))

        primals = [rvals[k] for k in diff_rows] + [bvals[k] for k in diff_bcs]
        _, vjp = jax.vjp(f, *primals)
        grads = list(vjp(tuple(cvals)))
        if lat_add is not None:
            add = refs[n_in - 1][...]
            grads[0] = grads[0] + (add if lat_add.shape[0] == n_rows else jnp.where(i >= ncb, add, 0.0))
        for k in range(ndr):
            outs[k][...] = grads[k].astype(outs[k].dtype)
        for k, idx in enumerate(diff_bcs):
            o_ref = outs[ndr + k]
            first = (i == 0)
            if bcs[idx].shape[0] == 2:
                first = first | (i == ncb)

            @pl.when(first)
            def _(o_ref=o_ref):
                o_ref[...] = jnp.zeros_like(o_ref)

            o_ref[0] += grads[ndr + k]

    out_shape = [jax.ShapeDtypeStruct((n_rows, rows[k].width), F32) for k in diff_rows]
    out_shape += [jax.ShapeDtypeStruct(bcs[k].shape, F32) for k in diff_bcs]
    out_specs = [pl.BlockSpec((tm, rows[k].width), lambda i: (i, 0)) for k in diff_rows]
    out_specs += [_bc_spec(bcs[k], ncb) for k in diff_bcs]
    ins = [r.arr for r in rows] + list(bcs) + [c.arr for c in cts + extra]
    in_specs = [r.spec(tm) for r in rows] + [_bc_spec(b, ncb) for b in bcs] + [c.spec(tm) for c in cts + extra]
    if lat_add is not None:
        ins.append(lat_add)
        skip = ncb if lat_add.shape[0] != n_rows else 0
        in_specs.append(pl.BlockSpec((tm, lat_add.shape[1]), lambda i: (jnp.maximum(i - skip, 0), 0)))
    outs = pl.pallas_call(
        body, out_shape=out_shape, grid=(n_rows // tm,), in_specs=in_specs, out_specs=out_specs,
        compiler_params=_params(("arbitrary",), VMEM_LIMIT), name=name)(*ins)
    return outs


def _rms(x):
    return x * lax.rsqrt(jnp.mean(x * x, axis=-1, keepdims=True) + EPS)


def _sigmoid(x):
    return 0.5 * (jnp.tanh(0.5 * x) + 1.0)


def _silu(x):
    return x * _sigmoid(x)


def _gelu_tanh(x):
    return 0.5 * x * (1.0 + jnp.tanh(math.sqrt(2.0 / math.pi) * (x + 0.044715 * (x * x * x))))


def f_norm_mod(x, g, sc, sh):
    return ((_rms(x) * g) * (1.0 + sc) + sh,)


def f_rms(x, g):
    return (_rms(x) * g,)


def f_gate(o, z):
    return (o * _silu(z),)


def f_s5_act(y, u, d):
    return (_gelu_tanh(y + d * u),)


def f_s5_glu(ya, gl, z, b):
    return (ya * _sigmoid(gl + b) * _silu(z),)


def mla_post_fwd(o, p0, x0, gate, w_out, n_ctx, name="l0_post"):
    n, d = o.shape
    tm = math.gcd(ROW_TILE, n, n_ctx)
    ncb = n_ctx // tm

    def body(o_ref, z_ref, x_ref, gt_ref, w_ref, x1_ref, og_ref, out_ref):
        og = f_gate(o_ref[...], z_ref[...])[0].astype(BF16)
        out = jnp.dot(og, w_ref[...], preferred_element_type=F32)
        og_ref[...] = og
        out_ref[...] = out
        x1_ref[...] = x_ref[...] + gt_ref[0] * out

    row = pl.BlockSpec((tm, d), lambda i: (i, 0))
    return pl.pallas_call(
        body, out_shape=[jax.ShapeDtypeStruct((n, d), F32), jax.ShapeDtypeStruct((n, d), BF16),
                         jax.ShapeDtypeStruct((n, d), F32)],
        grid=(n // tm,),
        in_specs=[row, row, row, _bc_spec(gate, ncb), pl.BlockSpec((d, d), lambda i: (0, 0))],
        out_specs=[row, row, row],
        compiler_params=_params(("parallel",), VMEM_LIMIT), name=name)(o, p0, x0, gate, w_out)


def mla_post_bwd(dx1, out, og, o, p0, gate, w_out, n_ctx, name="l0_post_bwd"):
    n, d = o.shape
    tm = math.gcd(ROW_TILE, n, n_ctx)
    ncb = n_ctx // tm

    def body(dx_ref, out_ref, og_ref, o_ref, z_ref, gt_ref, w_ref, do_ref, dz_ref, dgt_ref, dw_ref):
        i = pl.program_id(0)

        @pl.when(i == 0)
        def _():
            dw_ref[...] = jnp.zeros_like(dw_ref)

        @pl.when((i == 0) | (i == ncb))
        def _():
            dgt_ref[...] = jnp.zeros_like(dgt_ref)

        dx = dx_ref[...]
        dgt_ref[0] += jnp.sum(dx * out_ref[...], axis=0, keepdims=True)
        d_out16 = (gt_ref[0] * dx).astype(BF16)
        dw_ref[...] += lax.dot_general(og_ref[...], d_out16, (((0,), (0,)), ((), ())), preferred_element_type=F32)
        d_og = lax.dot_general(d_out16, w_ref[...], NT_DIMS, preferred_element_type=F32)
        _, gate_vjp = jax.vjp(lambda o_, z_: f_gate(o_, z_), o_ref[...], z_ref[...])
        d_o, d_z = gate_vjp((d_og,))
        do_ref[...] = d_o
        dz_ref[...] = d_z

    row = pl.BlockSpec((tm, d), lambda i: (i, 0))
    mat = pl.BlockSpec((d, d), lambda i: (0, 0))
    return pl.pallas_call(
        body, out_shape=[jax.ShapeDtypeStruct((n, d), F32), jax.ShapeDtypeStruct((n, d), F32),
                         jax.ShapeDtypeStruct(gate.shape, F32), jax.ShapeDtypeStruct((d, d), F32)],
        grid=(n // tm,),
        in_specs=[row, row, row, row, row, _bc_spec(gate, ncb), mat],
        out_specs=[row, row, _bc_spec(gate, ncb), mat],
        compiler_params=_params(("arbitrary",), VMEM_LIMIT), name=name)(dx1, out, og, o, p0, gate, w_out)


def s5_tail(y_ssm, p1, x1p, target, n_ctx, d_vec, b_glu, gate, final_g, w_glu, w_out, name="l1_tail"):
    n, d = y_ssm.shape
    tm = math.gcd(ROW_TILE, n, n_ctx)
    off = n_ctx // tm
    tn_dims = (((0,), (0,)), ((), ()))

    def row_loss(x, g, t):
        e = _rms(x) * g - t
        return 0.5 * (e * e) * (1.0 / d)

    def body(y_ref, u_ref, z_ref, x1_ref, t_ref, d_ref, b_ref, gt_ref, fg_ref, wg_ref, wo_ref,
             l_ref, dx_ref, dy_ref, du_ref, dz_ref, dfg_ref, dgt_ref, db_ref, dd_ref, dwg_ref, dwo_ref):
        @pl.when(pl.program_id(0) == 0)
        def _():
            for r in (l_ref, dfg_ref, dgt_ref, db_ref, dd_ref, dwg_ref, dwo_ref):
                r[...] = jnp.zeros_like(r)

        u, z, tgt, gt = u_ref[...], z_ref[...], t_ref[...], gt_ref[...]
        (ya,), act_vjp = jax.vjp(lambda y_, u_, d_: f_s5_act(y_, u_, d_), y_ref[...], u, d_ref[...])
        ya16 = ya.astype(BF16)
        gl = jnp.dot(ya16, wg_ref[...], preferred_element_type=F32)
        (y3,), glu_vjp = jax.vjp(lambda a_, g_, z_, b_: f_s5_glu(a_, g_, z_, b_), ya, gl, z, b_ref[...])
        y3_16 = y3.astype(BF16)
        out1 = jnp.dot(y3_16, wo_ref[...], preferred_element_type=F32)
        lterm, loss_vjp = jax.vjp(lambda x_, g_: row_loss(x_, g_, tgt), x1_ref[...] + gt * out1, fg_ref[...])
        dx2, dfg = loss_vjp(jnp.ones_like(lterm))
        l_ref[...] += jnp.sum(lterm, axis=0, keepdims=True)
        dfg_ref[...] += dfg
        dx_ref[...] = dx2
        dgt_ref[...] += jnp.sum(dx2 * out1, axis=0, keepdims=True)
        d_out16 = (gt * dx2).astype(BF16)
        dwo_ref[...] += lax.dot_general(y3_16, d_out16, tn_dims, preferred_element_type=F32)
        d_y3 = lax.dot_general(d_out16, wo_ref[...], NT_DIMS, preferred_element_type=F32)
        d_ya, d_gl, d_z, d_b = glu_vjp((d_y3,))
        dz_ref[...] = d_z
        db_ref[...] += d_b
        d_gl16 = d_gl.astype(BF16)
        dwg_ref[...] += lax.dot_general(ya16, d_gl16, tn_dims, preferred_element_type=F32)
        d_ya = d_ya + lax.dot_general(d_gl16, wg_ref[...], NT_DIMS, preferred_element_type=F32)
        d_y, d_u, d_d = act_vjp((d_ya,))
        dy_ref[...] = d_y
        du_ref[...] = d_u
        dd_ref[...] += d_d

    row = pl.BlockSpec((tm, d), lambda i: (i, 0))
    vecs = pl.BlockSpec((1, d), lambda i: (0, 0))
    mat = pl.BlockSpec((d, d), lambda i: (0, 0))
    return pl.pallas_call(
        body,
        out_shape=[jax.ShapeDtypeStruct((1, d), F32)] + [jax.ShapeDtypeStruct((n, d), F32)] * 4
        + [jax.ShapeDtypeStruct((1, d), F32)] * 4 + [jax.ShapeDtypeStruct((d, d), F32)] * 2,
        grid=(n // tm,),
        in_specs=[row, pl.BlockSpec((tm, d), lambda i: (i + off, 0)), pl.BlockSpec((tm, d), lambda i: (i + off, 1)),
                  pl.BlockSpec((tm, d), lambda i: (i + off, 0)), row, vecs, vecs, vecs, vecs, mat, mat],
        out_specs=[vecs, row, row, row, row, vecs, vecs, vecs, vecs, mat, mat],
        compiler_params=_params(("arbitrary",), VMEM_LIMIT), name=name)(
            y_ssm, p1, p1, x1p, target, d_vec, b_glu, gate, final_g, w_glu, w_out)


NT_DIMS = (((1,), (1,)), ((), ()))
HEAD_LANES = 128
N_PAIRS = MLA_HEADS // 2


def _own_lanes(shape, hh):
    lane = lax.broadcasted_iota(jnp.int32, shape, len(shape) - 1)
    return (lane < V_HEAD_DIM) if hh == 0 else (lane >= V_HEAD_DIM)


def _rope_tiles(x, cos, sin_next, sin_prev, inverse):
    width = x.shape[-1]
    reps = width // HEAD_LANES
    c, sn, sp = (jnp.tile(t, (1, reps)) for t in (cos, sin_next, sin_prev))
    if inverse:
        return x * c + pltpu.roll(x * sn, 8, 1) + pltpu.roll(x * sp, width - 8, 1)
    return x * c + pltpu.roll(x, width - 8, 1) * sn + pltpu.roll(x, 8, 1) * sp


def _col_to_row(col):
    n = col.shape[0]
    hi = col.astype(BF16)
    r1 = col - hi.astype(F32)
    mid = r1.astype(BF16)
    lo = (r1 - mid.astype(F32)).astype(BF16)
    lane = lax.broadcasted_iota(jnp.int32, (n, HEAD_LANES), 1)
    terms = jnp.where(lane == 0, hi, jnp.where(lane == 1, mid, jnp.where(lane == 2, lo, jnp.zeros_like(hi))))
    eye = (lax.broadcasted_iota(jnp.int32, (n, n), 0) == lax.broadcasted_iota(jnp.int32, (n, n), 1)).astype(BF16)
    rows = lax.dot_general(terms, eye, (((0,), (0,)), ((), ())), preferred_element_type=F32)
    return rows[0:1] + rows[1:2] + rows[2:3]


def attn_fwd(qb, kb, vb, n_ctx):
    T = qb.shape[0]
    tq = math.gcd(ROW_TILE, n_ctx)
    nq, ncb = T // tq, n_ctx // tq

    def body(q_ref, k_ref, v_ref, o_ref, lse_ref, lse_row_ref):
        qi = pl.program_id(1)

        def rows(n_keys):
            v = v_ref[:n_keys, :]
            outs = []
            for hh in range(2):
                hs = slice(hh * HEAD_LANES, (hh + 1) * HEAD_LANES)
                s = lax.dot_general(q_ref[:, hs], k_ref[:n_keys, hs], NT_DIMS,
                                    preferred_element_type=F32) * SOFTMAX_SCALE
                m = jnp.max(s, axis=-1, keepdims=True)
                p = jnp.exp(s - m)
                l = jnp.sum(p, axis=-1, keepdims=True)
                outs.append(jnp.dot(p.astype(BF16), v, preferred_element_type=F32) / l)
                lse = m + jnp.log(l)
                lse_ref[hh] = lse
                lse_row_ref[hh] = _col_to_row(lse)
            o_ref[...] = jnp.where(_own_lanes(outs[0].shape, 0), outs[0], outs[1])

        pl.when(qi < ncb)(lambda: rows(n_ctx))
        pl.when(qi >= ncb)(lambda: rows(T))

    return pl.pallas_call(
        body,
        out_shape=[jax.ShapeDtypeStruct((T, MLA_HEADS * V_HEAD_DIM), F32),
                   jax.ShapeDtypeStruct((MLA_HEADS, T, 1), F32), jax.ShapeDtypeStruct((MLA_HEADS, 1, T), F32)],
        grid=(N_PAIRS, nq),
        in_specs=[pl.BlockSpec((tq, 2 * HEAD_LANES), lambda h, i: (i, h)),
                  pl.BlockSpec((T, 2 * HEAD_LANES), lambda h, i: (0, h)),
                  pl.BlockSpec((T, 2 * V_HEAD_DIM), lambda h, i: (0, h))],
        out_specs=[pl.BlockSpec((tq, 2 * V_HEAD_DIM), lambda h, i: (i, h)),
                   pl.BlockSpec((2, tq, 1), lambda h, i: (h, i, 0)),
                   pl.BlockSpec((2, 1, tq), lambda h, i: (h, 0, i))],
        compiler_params=_params(("parallel", "parallel"), VMEM_LIMIT), name="attn_fwd")(qb, kb, vb)


def attn_bwd_dq(qb, kb, vb, o, do, lse, tabs, n_ctx):
    T = qb.shape[0]
    tq = math.gcd(ROW_TILE, n_ctx)
    nq, ncb = T // tq, n_ctx // tq

    def body(q_ref, k_ref, v_ref, o_ref, do_ref, lse_ref, c_ref, sn_ref, sp_ref, dq_ref, delta_ref):
        qi = pl.program_id(1)

        def rows(n_keys):
            v = v_ref[:n_keys, :]
            dqs = []
            for hh in range(2):
                hs = slice(hh * HEAD_LANES, (hh + 1) * HEAD_LANES)
                k = k_ref[:n_keys, hs]
                do = jnp.where(_own_lanes(do_ref.shape, hh), do_ref[...], 0.0)
                delta = jnp.sum(do * o_ref[...], axis=-1, keepdims=True)
                s = lax.dot_general(q_ref[:, hs], k, NT_DIMS, preferred_element_type=F32) * SOFTMAX_SCALE
                p = jnp.exp(s - lse_ref[hh])
                dp = lax.dot_general(do.astype(BF16), v, NT_DIMS, preferred_element_type=F32)
                ds = p * (dp - delta) * SOFTMAX_SCALE
                dqs.append(jnp.dot(ds.astype(BF16), k, preferred_element_type=F32))
                delta_ref[hh] = _col_to_row(delta)
            dq = jnp.concatenate(dqs, axis=1)
            dq_ref[...] = _rope_tiles(dq, c_ref[...], sn_ref[...], sp_ref[...], True).astype(BF16)

        pl.when(qi < ncb)(lambda: rows(n_ctx))
        pl.when(qi >= ncb)(lambda: rows(T))

    tab = pl.BlockSpec((tq, HEAD_LANES), lambda h, i: (i, 0))
    return pl.pallas_call(
        body,
        out_shape=[jax.ShapeDtypeStruct((T, MLA_HEADS * HEAD_LANES), BF16),
                   jax.ShapeDtypeStruct((MLA_HEADS, 1, T), F32)],
        grid=(N_PAIRS, nq),
        in_specs=[pl.BlockSpec((tq, 2 * HEAD_LANES), lambda h, i: (i, h)),
                  pl.BlockSpec((T, 2 * HEAD_LANES), lambda h, i: (0, h)),
                  pl.BlockSpec((T, 2 * V_HEAD_DIM), lambda h, i: (0, h)),
                  pl.BlockSpec((tq, 2 * V_HEAD_DIM), lambda h, i: (i, h)),
                  pl.BlockSpec((tq, 2 * V_HEAD_DIM), lambda h, i: (i, h)),
                  pl.BlockSpec((2, tq, 1), lambda h, i: (h, i, 0)), tab, tab, tab],
        out_specs=[pl.BlockSpec((tq, 2 * HEAD_LANES), lambda h, i: (i, h)),
                   pl.BlockSpec((2, 1, tq), lambda h, i: (h, 0, i))],
        compiler_params=_params(("parallel", "parallel"), VMEM_LIMIT), name="attn_bwd_dq")(
            qb, kb, vb, o, do, lse, *tabs)


def attn_bwd_dkv(qb, kb, vb, do, lse_rows, delta_rows, n_ctx):
    T = qb.shape[0]
    tq = math.gcd(ROW_TILE, n_ctx)
    nq, ncb = T // tq, n_ctx // tq

    def body(q_ref, do_ref, lse_ref, delta_ref, k_ref, v_ref, dk_ref, dv_ref):
        kj = pl.program_id(1)

        def cols(first):
            v = v_ref[...]
            do_all = do_ref[first:, :]
            dv = None
            for hh in range(2):
                hs = slice(hh * HEAD_LANES, (hh + 1) * HEAD_LANES)
                k = k_ref[:, hs]
                q = q_ref[first:, hs]
                do16 = jnp.where(_own_lanes(do_all.shape, hh), do_all, 0.0).astype(BF16)
                st = lax.dot_general(k, q, NT_DIMS, preferred_element_type=F32) * SOFTMAX_SCALE
                pt = jnp.exp(st - lse_ref[hh, :, first:])
                dv_h = jnp.dot(pt.astype(BF16), do16, preferred_element_type=F32)
                dv = dv_h if dv is None else dv + dv_h
                dpt = lax.dot_general(v, do16, NT_DIMS, preferred_element_type=F32)
                dst = pt * (dpt - delta_ref[hh, :, first:]) * SOFTMAX_SCALE
                dk_ref[:, hs] = jnp.dot(dst.astype(BF16), q, preferred_element_type=F32)
            dv_ref[...] = dv

        pl.when(kj < ncb)(lambda: cols(0))
        pl.when(kj >= ncb)(lambda: cols(n_ctx))

    return pl.pallas_call(
        body,
        out_shape=[jax.ShapeDtypeStruct((T, MLA_HEADS * HEAD_LANES), F32),
                   jax.ShapeDtypeStruct((T, MLA_HEADS * V_HEAD_DIM), F32)],
        grid=(N_PAIRS, nq),
        in_specs=[pl.BlockSpec((T, 2 * HEAD_LANES), lambda h, j: (0, h)),
                  pl.BlockSpec((T, 2 * V_HEAD_DIM), lambda h, j: (0, h)),
                  pl.BlockSpec((2, 1, T), lambda h, j: (h, 0, 0)),
                  pl.BlockSpec((2, 1, T), lambda h, j: (h, 0, 0)),
                  pl.BlockSpec((tq, 2 * HEAD_LANES), lambda h, j: (j, h)),
                  pl.BlockSpec((tq, 2 * V_HEAD_DIM), lambda h, j: (j, h))],
        out_specs=[pl.BlockSpec((tq, 2 * HEAD_LANES), lambda h, j: (j, h)),
                   pl.BlockSpec((tq, 2 * V_HEAD_DIM), lambda h, j: (j, h))],
        compiler_params=_params(("parallel", "parallel"), VMEM_LIMIT), name="attn_bwd_dkv")(
            qb, do, lse_rows, delta_rows, kb, vb)


def _split_bf16(x):
    hi = x.astype(BF16)
    return hi, (x - hi.astype(F32)).astype(BF16)


def q_heads(qn, w_uq_p, tabs, name="l0_uq"):
    T, K = qn.shape
    N = w_uq_p.shape[1]
    tm = math.gcd(ROW_TILE, T)

    def body(a_ref, w_ref, c_ref, sn_ref, sp_ref, o_ref):
        acc = jnp.dot(a_ref[...], w_ref[...], preferred_element_type=F32)
        o_ref[...] = _rope_tiles(acc, c_ref[...], sn_ref[...], sp_ref[...], False).astype(BF16)

    tab = pl.BlockSpec((tm, HEAD_LANES), lambda i: (i, 0))
    return pl.pallas_call(
        body, out_shape=jax.ShapeDtypeStruct((T, N), BF16), grid=(T // tm,),
        in_specs=[pl.BlockSpec((tm, K), lambda i: (i, 0)), pl.BlockSpec((K, N), lambda i: (0, 0)), tab, tab, tab],
        out_specs=pl.BlockSpec((tm, N), lambda i: (i, 0)),
        compiler_params=_params(("parallel",), VMEM_LIMIT), name=name)(qn, w_uq_p, *tabs)


def kv_heads(kvn, w_kn_p, w_v, kr, spread, tabs, name="l0_ukv"):
    T, K = kvn.shape
    N = w_kn_p.shape[1]
    NV = w_v.shape[1]
    tm = math.gcd(ROW_TILE, T)

    def body(a_ref, wk_ref, wv_ref, kr_ref, e_ref, c_ref, sn_ref, sp_ref, k_ref, v_ref):
        a = a_ref[...]
        hi, lo = _split_bf16(kr_ref[...])
        acc = (jnp.dot(a, wk_ref[...], preferred_element_type=F32)
               + jnp.dot(hi, e_ref[...], preferred_element_type=F32)
               + jnp.dot(lo, e_ref[...], preferred_element_type=F32))
        k_ref[...] = _rope_tiles(acc, c_ref[...], sn_ref[...], sp_ref[...], False).astype(BF16)
        v_ref[...] = jnp.dot(a, wv_ref[...], preferred_element_type=F32).astype(BF16)

    tab = pl.BlockSpec((tm, HEAD_LANES), lambda i: (i, 0))
    return pl.pallas_call(
        body, out_shape=[jax.ShapeDtypeStruct((T, N), BF16), jax.ShapeDtypeStruct((T, NV), BF16)], grid=(T // tm,),
        in_specs=[pl.BlockSpec((tm, K), lambda i: (i, 0)), pl.BlockSpec((K, N), lambda i: (0, 0)),
                  pl.BlockSpec((K, NV), lambda i: (0, 0)), pl.BlockSpec((tm, QK_ROPE_DIM), lambda i: (i, 0)),
                  pl.BlockSpec((QK_ROPE_DIM, N), lambda i: (0, 0)), tab, tab, tab],
        out_specs=[pl.BlockSpec((tm, N), lambda i: (i, 0)), pl.BlockSpec((tm, NV), lambda i: (i, 0))],
        compiler_params=_params(("parallel",), VMEM_LIMIT), name=name)(kvn, w_kn_p, w_v, kr, spread, *tabs)


def heads_unrope(d, tabs, spread=None, name="unrope"):
    T, N = d.shape
    tm = math.gcd(ROW_TILE, T)

    def body(*refs):
        if spread is None:
            d_ref, c_ref, sn_ref, sp_ref, o_ref = refs
        else:
            d_ref, c_ref, sn_ref, sp_ref, e_ref, o_ref, kr_ref = refs
        g = _rope_tiles(d_ref[...], c_ref[...], sn_ref[...], sp_ref[...], True)
        o_ref[...] = g.astype(BF16)
        if spread is not None:
            hi, lo = _split_bf16(g)
            kr_ref[...] = (lax.dot_general(hi, e_ref[...], NT_DIMS, preferred_element_type=F32)
                           + lax.dot_general(lo, e_ref[...], NT_DIMS, preferred_element_type=F32))

    tab = pl.BlockSpec((tm, HEAD_LANES), lambda i: (i, 0))
    row = pl.BlockSpec((tm, N), lambda i: (i, 0))
    ins, in_specs = [d, *tabs], [row, tab, tab, tab]
    out_shape, out_specs = [jax.ShapeDtypeStruct((T, N), BF16)], [row]
    if spread is not None:
        ins.append(spread)
        in_specs.append(pl.BlockSpec(spread.shape, lambda i: (0, 0)))
        out_shape.append(jax.ShapeDtypeStruct((T, spread.shape[0]), F32))
        out_specs.append(pl.BlockSpec((tm, spread.shape[0]), lambda i: (i, 0)))
    return pl.pallas_call(
        body, out_shape=out_shape, grid=(T // tm,), in_specs=in_specs, out_specs=out_specs,
        compiler_params=_params(("parallel",), VMEM_LIMIT), name=name)(*ins)


def _cmul(ar, ai, br, bi):
    return ar * br - ai * bi, ar * bi + ai * br


def s5_chain(finals, s0, a, n_steps, reverse, name):
    W = finals.shape[-1]
    first = N_SEG - 1 if reverse else 0

    def body(f_ref, s0_ref, a_ref, c_ref):
        pr, pi = jnp.ones((1, W), F32), jnp.zeros((1, W), F32)
        br, bi = a_ref[0], a_ref[1]
        n = n_steps
        while n:
            if n & 1:
                pr, pi = _cmul(pr, pi, br, bi)
            br, bi = _cmul(br, bi, br, bi)
            n >>= 1
        fr, fi = f_ref[0], f_ref[1]
        row = lax.broadcasted_iota(jnp.int32, (N_SEG, W), 0)
        s0r = jnp.broadcast_to(s0_ref[0], (N_SEG, W))
        s0i = jnp.broadcast_to(s0_ref[1], (N_SEG, W))
        cr = jnp.where(row == first, s0r, 0.0)
        ci = jnp.where(row == first, s0i, 0.0)
        shift = N_SEG - 1 if reverse else 1
        for _ in range(N_SEG - 1):
            mr, mi = _cmul(pr, pi, cr, ci)
            tr = pltpu.roll(fr + mr, shift, 0)
            ti = pltpu.roll(fi + mi, shift, 0)
            cr = jnp.where(row == first, s0r, tr)
            ci = jnp.where(row == first, s0i, ti)
        c_ref[0] = cr
        c_ref[1] = ci

    return pl.pallas_call(body, out_shape=jax.ShapeDtypeStruct((2, N_SEG, W), F32), name=name)(finals, s0, a)


def _scan_chunk(bur, bui, st_ref, a_ref, n_steps, reverse):
    for lc in range(S5_LANES // BLK_ST):
        sl = slice(lc * BLK_ST, (lc + 1) * BLK_ST)
        lr = jnp.broadcast_to(a_ref[0, :, sl], (N_SEG, BLK_ST))
        li = jnp.broadcast_to(a_ref[1, :, sl], (N_SEG, BLK_ST))

        def step(jj, carry, sl=sl, lr=lr, li=li):
            sr, si = carry
            j = (n_steps - 1 - jj) if reverse else jj
            r0 = pl.multiple_of(j * N_SEG, N_SEG)
            nr = lr * sr - li * si + bur[pl.ds(r0, N_SEG), sl]
            ni = lr * si + li * sr + bui[pl.ds(r0, N_SEG), sl]
            bur[pl.ds(r0, N_SEG), sl] = nr
            bui[pl.ds(r0, N_SEG), sl] = ni
            return nr, ni

        sr, si = lax.fori_loop(0, n_steps, step, (st_ref[0, :, sl], st_ref[1, :, sl]))
        st_ref[0, :, sl] = sr
        st_ref[1, :, sl] = si


def _project_in(x16, w_re, w_im, bur, bui, adjoint):
    for gb in range(N_BLOCKS):
        xb = x16[:, gb * BLK_CH:(gb + 1) * BLK_CH]
        sl = slice(gb * BLK_ST, (gb + 1) * BLK_ST)
        if adjoint:
            dn = (((1,), (1,)), ((), ()))
            bur[:, sl] = lax.dot_general(xb, w_re[gb], dn, preferred_element_type=F32)
            bui[:, sl] = -lax.dot_general(xb, w_im[gb], dn, preferred_element_type=F32)
        else:
            bur[:, sl] = jnp.dot(xb, w_re[gb], preferred_element_type=F32)
            bui[:, sl] = jnp.dot(xb, w_im[gb], preferred_element_type=F32)


def s5_scan(act, w_re, w_im, a, init, *, reverse, adjoint=False, c_re=None, c_im=None, add=None,
            want_ckpt=False, rows=None, name):
    act_off, N = rows if rows is not None else (0, act.shape[0])
    R = math.gcd(ROW_TILE, N, act_off)
    nch, jc = N // R, R // N_SEG
    with_out = c_re is not None

    def chunk(i):
        return (nch - 1 - i) if reverse else i

    def body(*refs):
        act_ref, wre_ref, wim_ref, a_ref, init_ref = refs[:5]
        k = 5
        if with_out:
            cre_ref, cim_ref = refs[k:k + 2]
            k += 2
        if add is not None:
            add_ref = refs[k]
            k += 1
        if with_out:
            out_ref = refs[k]
            k += 1
        if want_ckpt:
            ck_ref = refs[k]
            k += 1
        fin_ref, bur, bui = refs[k:k + 3]

        @pl.when(pl.program_id(0) == 0)
        def _():
            fin_ref[...] = init_ref[...]

        if want_ckpt:
            ck_ref[0] = fin_ref[...]
        _project_in(act_ref[...].astype(BF16), wre_ref, wim_ref, bur, bui, adjoint)
        _scan_chunk(bur, bui, fin_ref, a_ref, jc, reverse)
        if with_out:
            for gb in range(N_BLOCKS):
                sl = slice(gb * BLK_ST, (gb + 1) * BLK_ST)
                y = (jnp.dot(bur[:, sl].astype(BF16), cre_ref[gb], preferred_element_type=F32)
                     - jnp.dot(bui[:, sl].astype(BF16), cim_ref[gb], preferred_element_type=F32))
                cs = slice(gb * BLK_CH, (gb + 1) * BLK_CH)
                if add is not None:
                    y = y + add_ref[:, cs]
                out_ref[:, cs] = y

    row_spec = pl.BlockSpec((R, D_MODEL), lambda i: (chunk(i), 0))
    act_spec = pl.BlockSpec((R, D_MODEL), lambda i: (chunk(i) + act_off // R, 0))
    w_spec = pl.BlockSpec(w_re.shape, lambda i: (0, 0, 0))
    st_spec = pl.BlockSpec((2, N_SEG, S5_LANES), lambda i: (0, 0, 0))
    ins = [act, w_re, w_im, a, init]
    in_specs = [act_spec, w_spec, w_spec, pl.BlockSpec((2, 1, S5_LANES), lambda i: (0, 0, 0)), st_spec]
    if with_out:
        ins += [c_re, c_im]
        in_specs += [pl.BlockSpec(c_re.shape, lambda i: (0, 0, 0))] * 2
    if add is not None:
        ins.append(add)
        in_specs.append(row_spec)
    out_shape, out_specs = [], []
    if with_out:
        out_shape.append(jax.ShapeDtypeStruct((N, D_MODEL), F32))
        out_specs.append(row_spec)
    if want_ckpt:
        out_shape.append(jax.ShapeDtypeStruct((nch, 2, N_SEG, S5_LANES), F32))
        out_specs.append(pl.BlockSpec((1, 2, N_SEG, S5_LANES), lambda i: (chunk(i), 0, 0, 0)))
    out_shape.append(jax.ShapeDtypeStruct((2, N_SEG, S5_LANES), F32))
    out_specs.append(st_spec)
    res = pl.pallas_call(
        body, out_shape=out_shape, grid=(nch,), in_specs=in_specs, out_specs=out_specs,
        scratch_shapes=[pltpu.VMEM((R, S5_LANES), F32), pltpu.VMEM((R, S5_LANES), F32)],
        compiler_params=_params(("arbitrary",), VMEM_LIMIT), name=name)(*ins)
    res = list(res)
    out = res.pop(0) if with_out else None
    ckpt = res.pop(0) if want_ckpt else None
    return out, ckpt, res[0]


def s5_grads(dy, u, ckpt, b_re, b_im, c_re, c_im, lam, init_adj, *, reverse, add=None, u_off=0, name):
    N = dy.shape[0]
    R = math.gcd(ROW_TILE, N, u_off)
    nch, jc = N // R, R // N_SEG
    W = S5_LANES

    def chunk(i):
        return i if reverse else (nch - 1 - i)

    def body(*refs):
        dy_ref, u_ref, ck_ref, bre_ref, bim_ref, cre_ref, cim_ref, lam_ref, init_ref = refs[:9]
        k = 9
        if add is not None:
            add_ref = refs[k]
            k += 1
        du_ref, dlam_ref, dbre_ref, dbim_ref, dcre_ref, dcim_ref, fin_ref = refs[k:k + 7]
        sr_buf, si_buf, er_buf, ei_buf, st_buf = refs[k + 7:k + 12]

        @pl.when(pl.program_id(0) == 0)
        def _():
            fin_ref[...] = init_ref[...]
            dlam_ref[...] = jnp.zeros_like(dlam_ref)
            dbre_ref[...] = jnp.zeros_like(dbre_ref)
            dbim_ref[...] = jnp.zeros_like(dbim_ref)
            dcre_ref[...] = jnp.zeros_like(dcre_ref)
            dcim_ref[...] = jnp.zeros_like(dcim_ref)

        u16 = u_ref[...].astype(BF16)
        dy16 = dy_ref[...].astype(BF16)
        st_buf[...] = ck_ref[0]
        _project_in(u16, bre_ref, bim_ref, sr_buf, si_buf, False)
        _scan_chunk(sr_buf, si_buf, st_buf, lam_ref, jc, reverse)
        _project_in(dy16, cre_ref, cim_ref, er_buf, ei_buf, True)
        for lc in range(W // BLK_ST):
            sl = slice(lc * BLK_ST, (lc + 1) * BLK_ST)
            lr = jnp.broadcast_to(lam_ref[0, :, sl], (N_SEG, BLK_ST))
            li = jnp.broadcast_to(lam_ref[1, :, sl], (N_SEG, BLK_ST))

            def one(r0, spr, spi, carry, sl=sl, lr=lr, li=li):
                gr, gi, ar, ai = carry
                nr = er_buf[pl.ds(r0, N_SEG), sl] + lr * gr + li * gi
                ni = ei_buf[pl.ds(r0, N_SEG), sl] + lr * gi - li * gr
                er_buf[pl.ds(r0, N_SEG), sl] = nr
                ei_buf[pl.ds(r0, N_SEG), sl] = ni
                return nr, ni, ar + spr * nr + spi * ni, ai + spr * ni - spi * nr

            def step(ff, carry, sl=sl, one=one):
                f = jc - 1 - ff
                j = (jc - 1 - f) if reverse else f
                jp = (j + 1) if reverse else (j - 1)
                r0 = pl.multiple_of(j * N_SEG, N_SEG)
                p0 = pl.multiple_of(jp * N_SEG, N_SEG)
                return one(r0, sr_buf[pl.ds(p0, N_SEG), sl], si_buf[pl.ds(p0, N_SEG), sl], carry)

            carry = (fin_ref[0, :, sl], fin_ref[1, :, sl], dlam_ref[0, :, sl], dlam_ref[1, :, sl])
            carry = lax.fori_loop(0, jc - 1, step, carry)
            r_first = (jc - 1) * N_SEG if reverse else 0
            gr, gi, ar, ai = one(r_first, ck_ref[0, 0, :, sl], ck_ref[0, 1, :, sl], carry)
            fin_ref[0, :, sl] = gr
            fin_ref[1, :, sl] = gi
            dlam_ref[0, :, sl] = ar
            dlam_ref[1, :, sl] = ai
        tn = (((0,), (0,)), ((), ()))
        nt = (((1,), (1,)), ((), ()))
        for gb in range(N_BLOCKS):
            sl = slice(gb * BLK_ST, (gb + 1) * BLK_ST)
            cs = slice(gb * BLK_CH, (gb + 1) * BLK_CH)
            gr16 = er_buf[:, sl].astype(BF16)
            gi16 = ei_buf[:, sl].astype(BF16)
            du = (lax.dot_general(gr16, bre_ref[gb], nt, preferred_element_type=F32)
                  + lax.dot_general(gi16, bim_ref[gb], nt, preferred_element_type=F32))
            if add is not None:
                du = du + add_ref[:, cs]
            du_ref[:, cs] = du
            ub, dyb = u16[:, cs], dy16[:, cs]
            dbre_ref[gb] += lax.dot_general(ub, gr16, tn, preferred_element_type=F32)
            dbim_ref[gb] += lax.dot_general(ub, gi16, tn, preferred_element_type=F32)
            dcre_ref[gb] += lax.dot_general(sr_buf[:, sl].astype(BF16), dyb, tn, preferred_element_type=F32)
            dcim_ref[gb] -= lax.dot_general(si_buf[:, sl].astype(BF16), dyb, tn, preferred_element_type=F32)

    row_spec = pl.BlockSpec((R, D_MODEL), lambda i: (chunk(i), 0))
    st_spec = pl.BlockSpec((2, N_SEG, W), lambda i: (0, 0, 0))
    wb_spec = pl.BlockSpec(b_re.shape, lambda i: (0, 0, 0))
    wc_spec = pl.BlockSpec(c_re.shape, lambda i: (0, 0, 0))
    ins = [dy, u, ckpt, b_re, b_im, c_re, c_im, lam, init_adj]
    u_spec = pl.BlockSpec((R, D_MODEL), lambda i: (chunk(i) + u_off // R, 0))
    in_specs = [row_spec, u_spec, pl.BlockSpec((1, 2, N_SEG, W), lambda i: (chunk(i), 0, 0, 0)),
                wb_spec, wb_spec, wc_spec, wc_spec, pl.BlockSpec((2, 1, W), lambda i: (0, 0, 0)), st_spec]
    if add is not None:
        ins.append(add)
        in_specs.append(row_spec)
    out_shape = [jax.ShapeDtypeStruct((N, D_MODEL), F32), jax.ShapeDtypeStruct((2, N_SEG, W), F32),
                 jax.ShapeDtypeStruct(b_re.shape, F32), jax.ShapeDtypeStruct(b_re.shape, F32),
                 jax.ShapeDtypeStruct(c_re.shape, F32), jax.ShapeDtypeStruct(c_re.shape, F32),
                 jax.ShapeDtypeStruct((2, N_SEG, W), F32)]
    out_specs = [row_spec, st_spec, wb_spec, wb_spec, wc_spec, wc_spec, st_spec]
    return pl.pallas_call(
        body, out_shape=out_shape, grid=(nch,), in_specs=in_specs, out_specs=out_specs,
        scratch_shapes=[pltpu.VMEM((R, W), F32) for _ in range(4)] + [pltpu.VMEM((2, N_SEG, W), F32)],
        compiler_params=_params(("arbitrary",), VMEM_LIMIT), name=name)(*ins)


def adamw(w, g, m, v, name="adamw"):
    n, d = w.shape
    lanes = -(-d // 128) * 128
    tm = n
    while tm * lanes * 4 > (1 << 20) and tm % 16 == 0:
        tm //= 2
    c1 = 1.0 - ADAM_B1 ** ADAM_STEP
    c2 = 1.0 - ADAM_B2 ** ADAM_STEP

    def body(w_ref, g_ref, m_ref, v_ref, d_ref, nm_ref, nv_ref):
        g_ = g_ref[...]
        m_ = ADAM_B1 * m_ref[...] + (1.0 - ADAM_B1) * g_
        v_ = ADAM_B2 * v_ref[...] + (1.0 - ADAM_B2) * (g_ * g_)
        d_ref[...] = -ADAM_LR * ((m_ / c1) / (jnp.sqrt(v_ / c2) + ADAM_EPS) + ADAM_WD * w_ref[...])
        nm_ref[...] = m_
        nv_ref[...] = v_

    spec = pl.BlockSpec((tm, d), lambda i: (i, 0))
    return pl.pallas_call(
        body, out_shape=[jax.ShapeDtypeStruct((n, d), F32)] * 3, grid=(n // tm,),
        in_specs=[spec] * 4, out_specs=[spec] * 3,
        compiler_params=_params(("parallel",), VMEM_LIMIT), name=name)(w, g, m, v)


def _coords():
    return lax.axis_index("x"), lax.axis_index("y"), lax.axis_index("c")


def exchange(arrays, out_shapes, remote, local, name, aliases=None):
    n_in, n_out, n_rem, n_loc = len(arrays), len(out_shapes), len(remote), len(local)

    def at(ref, idx):
        return ref if idx is None else ref.at[idx]

    def body(*refs):
        ins, outs = refs[:n_in], refs[n_in:n_in + n_out]
        send_sems, recv_sems, local_sems = refs[n_in + n_out:]
        me = _coords()
        sends, recvs = [], []
        for k, (flip, ii, src_at, oi, dst_at) in enumerate(remote):
            peer = (me[0] ^ flip[0], me[1] ^ flip[1], me[2] ^ flip[2])
            src = at(ins[ii], src_at(me, peer))
            sends.append(pltpu.make_async_remote_copy(
                src_ref=src, dst_ref=at(outs[oi], dst_at(me)), send_sem=send_sems.at[k], recv_sem=recv_sems.at[k],
                device_id=peer, device_id_type=MESH))
            recvs.append(pltpu.make_async_remote_copy(
                src_ref=src, dst_ref=at(outs[oi], dst_at(peer)), send_sem=send_sems.at[k], recv_sem=recv_sems.at[k],
                device_id=peer, device_id_type=MESH))
        locs = [pltpu.make_async_copy(at(ins[ii], src_at(me)), at(outs[oi], dst_at(me)), local_sems.at[k])
                for k, (ii, src_at, oi, dst_at) in enumerate(local)]
        for cp in locs + sends:
            cp.start()
        for cp in recvs:
            cp.wait_recv()
        for cp in sends:
            cp.wait_send()
        for cp in locs:
            cp.wait()

    hbm = pl.BlockSpec(memory_space=pl.ANY)
    return pl.pallas_call(
        body, out_shape=list(out_shapes), in_specs=[hbm] * n_in, out_specs=[hbm] * n_out,
        scratch_shapes=[pltpu.SemaphoreType.DMA((n_rem,)), pltpu.SemaphoreType.DMA((n_rem,)),
                        pltpu.SemaphoreType.DMA((max(n_loc, 1),))],
        input_output_aliases=aliases or {}, name=name)(*arrays)


ALL_FLIPS = [(dx, dy, dc) for dx in (0, 1) for dy in (0, 1) for dc in (0, 1)][1:]
CHIP_FLIPS = [(1, 0, 0), (0, 1, 0), (1, 1, 0)]
CORE_FLIP = (0, 0, 1)


def _dev_index(p):
    return 4 * p[0] + 2 * p[1] + p[2]


def _chip_index(p):
    return 2 * p[0] + p[1]


def _gather(xs, flips, index, n, name):
    arrays = [x[None] for x in xs]
    outs = [jax.ShapeDtypeStruct((n,) + x.shape, x.dtype) for x in xs]
    remote = [(f, a, lambda me, peer: (0,), a, lambda s: (index(s),)) for a in range(len(xs)) for f in flips]
    local = [(a, lambda me: (0,), a, lambda me: (index(me),)) for a in range(len(xs))]
    return exchange(arrays, outs, remote, local, name)


def allgather_devices(x, name):
    return _gather([x], ALL_FLIPS, _dev_index, N_DEV, name)[0]


def allgather_chips(xs, name):
    return _gather(xs, CHIP_FLIPS, _chip_index, N_CHIP, name)


def gather_halves(xs, name):
    n = len(xs)
    nk = n * len(CHIP_FLIPS)

    def body(*refs):
        ins, outs = refs[:n], refs[n:2 * n]
        ici_send, ici_recv, d2d_send, d2d_recv = refs[2 * n:]
        me = _coords()
        sibling = (me[0], me[1], 1 - me[2])
        first, passed, landed = [], [], []
        for a in range(n):
            half = ins[a].shape[0] // 2
            mine = ins[a].at[pl.ds(pl.multiple_of(me[2] * half, 16), half)]
            for j, flip in enumerate(CHIP_FLIPS):
                k = a * len(CHIP_FLIPS) + j
                peer = (me[0] ^ flip[0], me[1] ^ flip[1], me[2])
                first.append(pltpu.make_async_remote_copy(
                    src_ref=mine, dst_ref=outs[a].at[_chip_index(me), me[2]], send_sem=ici_send.at[k],
                    recv_sem=ici_recv.at[k], device_id=peer, device_id_type=MESH))
                arrived = outs[a].at[_chip_index(peer), me[2]]
                landed.append(pltpu.make_async_remote_copy(
                    src_ref=mine, dst_ref=arrived, send_sem=ici_send.at[k], recv_sem=ici_recv.at[k],
                    device_id=peer, device_id_type=MESH))
                passed.append(pltpu.make_async_remote_copy(
                    src_ref=arrived, dst_ref=arrived, send_sem=d2d_send.at[k], recv_sem=d2d_recv.at[k],
                    device_id=sibling, device_id_type=MESH))
        for cp in first:
            cp.start()
        for k in range(nk):
            landed[k].wait_recv()
            passed[k].start()
        for a in range(n):
            for j, flip in enumerate(CHIP_FLIPS):
                k = a * len(CHIP_FLIPS) + j
                peer_chip = _chip_index((me[0] ^ flip[0], me[1] ^ flip[1]))
                from_sibling = outs[a].at[peer_chip, 1 - me[2]]
                pltpu.make_async_remote_copy(
                    src_ref=from_sibling, dst_ref=from_sibling, send_sem=d2d_send.at[k], recv_sem=d2d_recv.at[k],
                    device_id=sibling, device_id_type=MESH).wait_recv()
        for cp in first + passed:
            cp.wait_send()

    hbm = pl.BlockSpec(memory_space=pl.ANY)
    return pl.pallas_call(
        body, out_shape=[jax.ShapeDtypeStruct((N_CHIP, 2, x.shape[0] // 2, x.shape[1]), x.dtype) for x in xs],
        in_specs=[hbm] * n, out_specs=[hbm] * n,
        scratch_shapes=[pltpu.SemaphoreType.DMA((nk,)) for _ in range(4)], name=name)(*xs)


HBM_SPEC = pl.BlockSpec(memory_space=pltpu.HBM)
SEM_SPEC = pl.BlockSpec(memory_space=pltpu.SEMAPHORE)
DATAFLOW = pltpu.SideEffectType.DATAFLOW_SIDE_EFFECTING


def _at(ref, idx):
    return ref if idx is None else ref.at[idx]


def _peer(me, flip):
    return (me[0] ^ flip[0], me[1] ^ flip[1], me[2] ^ flip[2])


def exchange_start(arrays, land_shapes, remote, name):
    n_in, n_out, nk = len(arrays), len(land_shapes), len(remote)

    def body(*refs):
        srcs, lands = refs[:n_in], refs[n_in:n_in + n_out]
        send_sems, recv_sems, token = refs[n_in + n_out], refs[n_in + n_out + 1], refs[-1]
        me = _coords()
        for k, (flip, ii, src_at, oi, dst_at) in enumerate(remote):
            peer = _peer(me, flip)
            pltpu.make_async_remote_copy(
                src_ref=_at(srcs[ii], src_at(me, peer)), dst_ref=_at(lands[oi], dst_at(me)), send_sem=send_sems.at[k],
                recv_sem=recv_sems.at[k], device_id=peer, device_id_type=MESH).start()
        token[...] = jnp.zeros_like(token)

    lands = [lax.empty(s.shape, s.dtype) for s in land_shapes]
    bufs = list(arrays) + lands
    out = pl.pallas_call(
        body, name=name,
        out_shape=(pltpu.SemaphoreType.DMA((nk,)), pltpu.SemaphoreType.DMA((nk,)),
                   *[pltpu.HBM(b.shape, b.dtype) for b in bufs], jax.ShapeDtypeStruct((8, 128), F32)),
        in_specs=[HBM_SPEC] * len(bufs),
        out_specs=(SEM_SPEC, SEM_SPEC, *[HBM_SPEC] * len(bufs), pl.BlockSpec(memory_space=pltpu.VMEM)),
        input_output_aliases={a: 2 + a for a in range(len(bufs))},
        compiler_params=pltpu.CompilerParams(has_side_effects=DATAFLOW),
    )(*[pltpu.with_memory_space_constraint(b, pltpu.HBM) for b in bufs])
    flight = (out[0], out[1], list(out[2:2 + n_in]), list(out[2 + n_in:2 + n_in + n_out]), remote)
    return flight, out[-1]


def exchange_wait(flight, after, name):
    send_sems, recv_sems, arrays, lands, remote = flight
    n_in, n_out = len(arrays), len(lands)

    def body(*refs):
        srcs, lnds = refs[:n_in], refs[n_in:n_in + n_out]
        s_sems, r_sems = refs[n_in + n_out], refs[n_in + n_out + 1]
        me = _coords()
        for k, (flip, ii, src_at, oi, dst_at) in enumerate(remote):
            peer = _peer(me, flip)
            copy = pltpu.make_async_remote_copy(
                src_ref=_at(srcs[ii], src_at(me, peer)), dst_ref=_at(lnds[oi], dst_at(peer)), send_sem=s_sems.at[k],
                recv_sem=r_sems.at[k], device_id=peer, device_id_type=MESH)
            copy.wait_send()
            copy.wait_recv()

    bufs = list(arrays) + list(lands)
    out = pl.pallas_call(
        body, name=name,
        out_shape=tuple(pltpu.HBM(b.shape, b.dtype) for b in bufs),
        in_specs=[HBM_SPEC] * len(bufs) + [SEM_SPEC, SEM_SPEC, pl.BlockSpec(memory_space=pl.ANY)],
        out_specs=tuple([HBM_SPEC] * len(bufs)),
        input_output_aliases={a: a for a in range(len(bufs))},
        compiler_params=pltpu.CompilerParams(has_side_effects=DATAFLOW),
    )(*bufs, send_sems, recv_sems, after)
    return list(out[:n_in]), list(out[n_in:])


def _half_tile(h, cd):
    return h if h * cd * 4 <= (1 << 20) else math.gcd(512, h)


def pair_add(g, got, core, out_dtype, name):
    _, _, h, cd = g.shape
    th = _half_tile(h, cd)

    def body(c_ref, g_ref, got_ref, o_ref):
        o_ref[0] = (g_ref[0, 0] + got_ref[0]).astype(o_ref.dtype)

    return pl.pallas_call(
        body, out_shape=jax.ShapeDtypeStruct((N_CHIP, h, cd), out_dtype),
        grid_spec=pltpu.PrefetchScalarGridSpec(
            num_scalar_prefetch=1, grid=(N_CHIP, h // th),
            in_specs=[pl.BlockSpec((1, 1, th, cd), lambda q, i, c: (q, c[0], i, 0)),
                      pl.BlockSpec((1, th, cd), lambda q, i, c: (q, i, 0))],
            out_specs=pl.BlockSpec((1, th, cd), lambda q, i, c: (q, i, 0))),
        compiler_params=_params(("parallel", "parallel"), VMEM_LIMIT), name=name)(core, g, got)


def sum_chips(parts, sums, place, name):
    _, h, cd = parts.shape
    th = _half_tile(h, cd)

    def body(pc_ref, p_ref, own_ref, o_ref):
        acc = None
        for q in range(N_CHIP):
            term = jnp.where(pc_ref[1] == q, own_ref[0], p_ref[q]).astype(F32)
            acc = term if acc is None else acc + term
        o_ref[0] = acc

    return pl.pallas_call(
        body, out_shape=jax.ShapeDtypeStruct((2, h, cd), F32),
        grid_spec=pltpu.PrefetchScalarGridSpec(
            num_scalar_prefetch=1, grid=(h // th,),
            in_specs=[pl.BlockSpec((N_CHIP, th, cd), lambda i, pc: (0, i, 0)),
                      pl.BlockSpec((1, th, cd), lambda i, pc: (pc[1], i, 0))],
            out_specs=pl.BlockSpec((1, th, cd), lambda i, pc: (pc[0], i, 0))),
        compiler_params=_params(("parallel",), VMEM_LIMIT), name=name)(place, parts, sums)


def to_segments(a, n_ctx):
    def one(p):
        n = p.shape[0]
        return p.reshape(N_SEG, n // N_SEG, -1).transpose(1, 0, 2).reshape(n, -1)
    return jnp.concatenate([one(a[:n_ctx]), one(a[n_ctx:])], axis=0) if n_ctx else one(a)


def from_segments(a, n_ctx):
    def one(p):
        n = p.shape[0]
        return p.reshape(n // N_SEG, N_SEG, -1).transpose(1, 0, 2).reshape(n, -1)
    return jnp.concatenate([one(a[:n_ctx]), one(a[n_ctx:])], axis=0) if n_ctx else one(a)


def rope_tables(n_ctx, n_lat):
    f32 = np.float32
    rows = n_lat // GRID_W
    row = np.repeat(np.arange(rows), GRID_W).astype(f32)
    col = np.tile(np.arange(GRID_W), rows).astype(f32)
    d = QK_ROPE_DIM // 2
    inv = (f32(1.0) / np.power(f32(ROPE_THETA), np.arange(0, d, 2, dtype=f32) / f32(d))).astype(f32)
    ang = np.concatenate([row[:, None] * inv[None, :], col[:, None] * inv[None, :]], axis=1).astype(f32)
    cos = np.concatenate([np.ones((n_ctx, d), f32), np.cos(ang)], axis=0)
    sin = np.concatenate([np.zeros((n_ctx, d), f32), np.sin(ang)], axis=0)
    q = QK_ROPE_DIM // 4
    T = n_ctx + n_lat
    ones, zeros = np.ones((T, QK_NOPE_DIM), f32), np.zeros((T, QK_NOPE_DIM), f32)
    tail, z8 = np.zeros((T, HEAD_LANES - QK_DIM), f32), np.zeros((T, q), f32)
    cr, cc, sr, sc = cos[:, :q], cos[:, q:], sin[:, :q], sin[:, q:]
    cos_t = np.concatenate([ones, cr, cr, cc, cc, tail], axis=1)
    sin_next = np.concatenate([zeros, -sr, z8, -sc, z8, tail], axis=1)
    sin_prev = np.concatenate([zeros, z8, sr, z8, sc, tail], axis=1)
    return tuple(jnp.asarray(t, F32) for t in (cos_t, sin_next, sin_prev))


def pad_heads(w, used):
    k = w.shape[0]
    return jnp.pad(w.reshape(k, MLA_HEADS, used), ((0, 0), (0, 0), (0, HEAD_LANES - used))).reshape(k, -1)


def unpad_heads(w, used):
    k = w.shape[0]
    return w.reshape(k, MLA_HEADS, HEAD_LANES)[:, :, :used].reshape(k, MLA_HEADS * used)


def rotary_spread():
    lane = np.arange(MLA_HEADS * HEAD_LANES) % HEAD_LANES
    return jnp.asarray(lane[None, :] == (QK_NOPE_DIM + np.arange(QK_ROPE_DIM))[:, None], BF16)


def s5_discretise(a_re, a_im, log_step, b_re, b_im):
    dt = jnp.exp(log_step)[:, None]
    mag = jnp.exp(a_re * dt)
    lb_re = mag * jnp.cos(a_im * dt)
    lb_im = mag * jnp.sin(a_im * dt)
    den = a_re * a_re + a_im * a_im
    nr = lb_re - 1.0
    f_re = ((nr * a_re + lb_im * a_im) / den)[..., None]
    f_im = ((lb_im * a_re - nr * a_im) / den)[..., None]
    return lb_re, lb_im, f_re * b_re - f_im * b_im, f_re * b_im + f_im * b_re


def s5_block_weights(lb_re, lb_im, bb_re, bb_im, c_re, c_im):
    eye = jnp.eye(GROUPS_PER_BLOCK, dtype=F32)
    lam = jnp.stack([lb_re.reshape(1, S5_LANES), lb_im.reshape(1, S5_LANES)])

    def b_blocks(bb):
        t = bb.reshape(N_BLOCKS, GROUPS_PER_BLOCK, S5_STATE, S5_GROUP)
        return jnp.einsum("bgpc,gh->bgchp", t, eye).reshape(N_BLOCKS, BLK_CH, BLK_ST).astype(BF16)

    def c_blocks(cc):
        t = cc.reshape(N_BLOCKS, GROUPS_PER_BLOCK, S5_GROUP, S5_STATE)
        return jnp.einsum("bgcp,gh->bgphc", t, eye).reshape(N_BLOCKS, BLK_ST, BLK_CH).astype(BF16)

    return lam, b_blocks(bb_re), b_blocks(bb_im), c_blocks(c_re), c_blocks(c_im)


def b_block_diag(db):
    t = db.reshape(N_BLOCKS, GROUPS_PER_BLOCK, S5_GROUP, GROUPS_PER_BLOCK, S5_STATE)
    return jnp.einsum("bgchp,gh->bgpc", t, jnp.eye(GROUPS_PER_BLOCK, dtype=F32)).reshape(S5_GROUPS, S5_STATE, S5_GROUP)


def c_block_diag(dc):
    t = dc.reshape(N_BLOCKS, GROUPS_PER_BLOCK, S5_STATE, GROUPS_PER_BLOCK, S5_GROUP)
    return jnp.einsum("bgphc,gh->bgcp", t, jnp.eye(GROUPS_PER_BLOCK, dtype=F32)).reshape(S5_GROUPS, S5_GROUP, S5_STATE)


def conj(a):
    return jnp.stack([a[0], -a[1]])


PACK_TILE = 16 * 128


def pack_flat(parts, dtype):
    flat = [p.reshape(-1).astype(dtype) for p in parts]
    sizes = [f.shape[0] for f in flat]
    total = sum(sizes)
    pad = (-total) % PACK_TILE
    if pad:
        flat.append(jnp.zeros((pad,), dtype))
    offs = np.cumsum([0] + sizes)[:-1].tolist()
    return jnp.concatenate(flat).reshape(-1, 128), offs


def unpack_flat(buf, offs, shapes):
    flat = buf.reshape(-1)
    return [flat[o:o + int(np.prod(s))].reshape(s) for o, s in zip(offs, shapes)]


def s5_forward(p1, n_ctx, dirs):
    saved = []
    y = None
    ctx_rows, lat_rows = (0, n_ctx), (n_ctx, p1.shape[0] - n_ctx)
    zeros_tile = jnp.zeros((2, N_SEG, S5_LANES), F32)
    zeros_row = jnp.zeros((2, 1, S5_LANES), F32)
    for k, (lam, b_re, b_im, c_re, c_im) in enumerate(dirs):
        rev = k == 1
        last = 0 if rev else N_SEG - 1
        _, _, fin = s5_scan(p1, b_re, b_im, lam, zeros_tile, reverse=rev, rows=ctx_rows, name=f"s5_ctx_finals{k}")
        carry_c = s5_chain(fin, zeros_row, lam, n_ctx // N_SEG, rev, name=f"s5_ctx_chain{k}")
        _, ck_c, fin_c = s5_scan(p1, b_re, b_im, lam, carry_c, reverse=rev, want_ckpt=True, rows=ctx_rows,
                                 name=f"s5_ctx_scan{k}")
        s0 = fin_c[:, last:last + 1, :]
        _, _, fin = s5_scan(p1, b_re, b_im, lam, zeros_tile, reverse=rev, rows=lat_rows, name=f"s5_lat_finals{k}")
        carry_l = s5_chain(fin, s0, lam, lat_rows[1] // N_SEG, rev, name=f"s5_lat_chain{k}")
        y, ck_l, _ = s5_scan(p1, b_re, b_im, lam, carry_l, reverse=rev, c_re=c_re, c_im=c_im, add=y,
                             want_ckpt=True, rows=lat_rows, name=f"s5_lat_scan{k}")
        saved.append((ck_c, ck_l))
    return y, saved


def s5_backward(dy_l, du_extra_l, p1, n_ctx, dirs, saved):
    n_lat = p1.shape[0] - n_ctx
    zeros_tile = jnp.zeros((2, N_SEG, S5_LANES), F32)
    zeros_row = jnp.zeros((2, 1, S5_LANES), F32)
    dy_c = jnp.zeros((n_ctx, D_MODEL), F32)
    du_l, du_c = du_extra_l, None
    grads = []
    for k, (lam, b_re, b_im, c_re, c_im) in enumerate(dirs):
        rev = k == 1
        lam_c = conj(lam)
        ck_c, ck_l = saved[k]
        first = N_SEG - 1 if rev else 0
        _, _, fin = s5_scan(dy_l, c_re, c_im, lam_c, zeros_tile, reverse=not rev, adjoint=True,
                            name=f"s5_lat_adj_finals{k}")
        carry = s5_chain(fin, zeros_row, lam_c, n_lat // N_SEG, not rev, name=f"s5_lat_adj_chain{k}")
        du_l, dlam_l, dbr_l, dbi_l, dcr_l, dci_l, fin_a = s5_grads(
            dy_l, p1, ck_l, b_re, b_im, c_re, c_im, lam, carry, reverse=rev, add=du_l, u_off=n_ctx,
            name=f"s5_lat_grads{k}")
        g0 = fin_a[:, first:first + 1, :]
        carry = s5_chain(zeros_tile, g0, lam_c, n_ctx // N_SEG, not rev, name=f"s5_ctx_adj_chain{k}")
        du_c, dlam_c, dbr_c, dbi_c, _, _, _ = s5_grads(
            dy_c, p1, ck_c, b_re, b_im, c_re, c_im, lam, carry, reverse=rev, add=du_c, name=f"s5_ctx_grads{k}")
        dlam = jnp.sum(dlam_l + dlam_c, axis=1)
        grads.append((dlam, b_block_diag(dbr_l + dbr_c), b_block_diag(dbi_l + dbi_c),
                      c_block_diag(dcr_l), c_block_diag(dci_l)))
    return jnp.concatenate([du_c, du_l], axis=0), grads


def local_step(x, ctx, target, mod, w, late=None, reducer=None):
    L, Lc = x.shape[0], ctx.shape[0]
    T = L + Lc
    assert L % Lc == 0 and Lc % (2 * N_SEG) == 0 and L % GRID_W == 0
    D = D_MODEL
    X0 = jnp.concatenate([ctx, x], axis=0)
    if late is not None:
        X0 = X0 + late[0][0, 0]

    def mod_of(i, j):
        return mod[i, :, j, :][:, None, :]

    def vec(v):
        return v.reshape(1, 1, -1).astype(F32)

    g0 = vec(w["norm_g"][0])
    (H0,) = rowwise_fwd(f_norm_mod, [X0], [g0, mod_of(0, 1), mod_of(0, 0)], [D], [BF16], T, Lc, "l0_norm")
    p0 = mm_nn(H0, w["mla_w_in"], name="l0_in")
    cq = Rows(p0, Q_LORA_RANK, col_blk=D // Q_LORA_RANK)
    ckv = Rows(p0, KV_LORA_RANK, col_blk=(D + Q_LORA_RANK) // KV_LORA_RANK)
    kr = p0[:, D + Q_LORA_RANK + KV_LORA_RANK:D + P0_HEAD]
    qng, kvng = vec(w["mla_q_norm"]), vec(w["mla_kv_norm"])
    (qn,) = rowwise_fwd(f_rms, [cq], [qng], [Q_LORA_RANK], [BF16], T, 0, "l0_qnorm")
    (kvn,) = rowwise_fwd(f_rms, [ckv], [kvng], [KV_LORA_RANK], [BF16], T, 0, "l0_kvnorm")
    tabs = rope_tables(Lc, L)
    spread = rotary_spread()
    w_uq_p = pad_heads(w["mla_w_uq"], QK_DIM)
    w_ukv3 = w["mla_w_ukv"].reshape(KV_LORA_RANK, MLA_HEADS, QK_NOPE_DIM + V_HEAD_DIM)
    w_kn_p = pad_heads(w_ukv3[:, :, :QK_NOPE_DIM].reshape(KV_LORA_RANK, -1), QK_NOPE_DIM)
    w_v = w_ukv3[:, :, QK_NOPE_DIM:].reshape(KV_LORA_RANK, -1)
    qb = q_heads(qn, w_uq_p, tabs)
    kb, vb = kv_heads(kvn, w_kn_p, w_v, kr, spread, tabs)
    o, lse, lse_rows = attn_fwd(qb, kb, vb, Lc)
    X1, og, out0 = mla_post_fwd(o, p0, X0, mod_of(0, 2), w["mla_w_out"], Lc)

    if late is not None:
        w = {**w, **late[1](X1)}
    X1p = to_segments(X1, Lc)
    tgt_p = to_segments(target, 0)
    g1 = vec(w["norm_g"][1])
    (H1,) = rowwise_fwd(f_norm_mod, [X1p], [g1, mod_of(1, 1), mod_of(1, 0)], [D], [BF16], T, Lc, "l1_norm")
    p1 = mm_nn(H1, w["s5_w_in"], name="l1_in")
    disc_fn = lambda *a: tuple(zip(*[s5_discretise(a[0][k], a[1][k], a[2][k], a[3][k], a[4][k]) for k in range(2)]))
    disc, disc_vjp = jax.vjp(disc_fn, w["s5_a_re"], w["s5_a_im"], w["s5_log_step"], w["s5_b_re"], w["s5_b_im"])
    dirs = [s5_block_weights(disc[0][k], disc[1][k], disc[2][k], disc[3][k], w["s5_c_re"][k], w["s5_c_im"][k])
            for k in range(2)]
    y_ssm, s5_saved = s5_forward(p1, Lc, dirs)

    row = lambda v: v.reshape(1, D).astype(F32)
    (lvec, dX2, d_yssm, d_u_act, d_z1, d_fg, d_gt1, d_bg, d_d, gw_glu, gw_out) = s5_tail(
        y_ssm, p1, X1p, tgt_p, Lc, row(w["s5_d"]), row(w["s5_b_glu"]), mod[1, 1:2, 2, :], row(w["final_g"]),
        w["s5_w_glu"], w["s5_w_out"])
    loss = jnp.sum(lvec)
    gw = {"final_g": d_fg.reshape(D), "s5_b_glu": d_bg.reshape(D), "s5_d": d_d.reshape(D),
          "s5_w_glu": gw_glu, "s5_w_out": gw_out}
    dmod = {}

    du_p, s5_g = s5_backward(d_yssm, d_u_act, p1, Lc, dirs, s5_saved)
    d_disc = tuple(tuple(s5_g[k][j - 1].reshape(disc[j][k].shape) if j >= 2 else
                         s5_g[k][0][j].reshape(disc[j][k].shape) for k in range(2)) for j in range(4))
    gw["s5_a_re"], gw["s5_a_im"], gw["s5_log_step"], gw["s5_b_re"], gw["s5_b_im"] = disc_vjp(d_disc)
    gw["s5_c_re"] = jnp.stack([s5_g[0][3], s5_g[1][3]])
    gw["s5_c_im"] = jnp.stack([s5_g[0][4], s5_g[1][4]])
    d_H1 = mm_nt_sum([(du_p, 0, 0), (d_z1, D, Lc)], w["s5_w_in"], T, "l1_in_dx")
    gw["s5_w_in"] = jnp.concatenate([mm_tn(H1, du_p, name="l1_in_dw_u"), mm_tn(H1[Lc:], d_z1, name="l1_in_dw_z")],
                                    axis=1)
    if reducer is not None:
        g1 = g1 + reducer[0]({n: gw.pop(n) for n in LAYER1_MATS})[0, 0]
    d_X1p, d_g1, d_sc1, d_sh1 = rowwise_bwd(f_norm_mod, [X1p], [g1, mod_of(1, 1), mod_of(1, 0)], [d_H1],
                                            [0], [0, 1, 2], T, Lc, "l1_norm_bwd", lat_add=dX2)
    d_gt1_full = jnp.concatenate([jnp.zeros((1, 1, D), F32), d_gt1[None]], axis=0)
    dmod[1] = (d_sh1, d_sc1, d_gt1_full)
    d_X1 = from_segments(d_X1p, Lc)

    d_o, d_z0, d_gt0, gw["mla_w_out"] = mla_post_bwd(d_X1, out0, og, o, p0, mod_of(0, 2), w["mla_w_out"], Lc)
    if reducer is not None:
        tabs = (tabs[0] + reducer[1](d_o)[0, 0],) + tabs[1:]
    d_q, delta_rows = attn_bwd_dq(qb, kb, vb, o, d_o, lse, tabs, Lc)
    dk_p, d_v = attn_bwd_dkv(qb, kb, vb, d_o, lse_rows, delta_rows, Lc)
    d_k, d_kr = heads_unrope(dk_p, tabs, jnp.pad(spread, ((0, HEAD_LANES - QK_ROPE_DIM), (0, 0))),
                             name="l0_k_unrope")
    d_qn = mm_nt(d_q, w_uq_p, name="l0_uq_dx")
    gw["mla_w_uq"] = unpad_heads(mm_tn(qn, d_q, name="l0_uq_dw"), QK_DIM)
    d_kvn = mm_nt(d_k, w_kn_p, name="l0_ukn_dx") + mm_nt(d_v, w_v, name="l0_uv_dx")
    dw_kn = unpad_heads(mm_tn(kvn, d_k, name="l0_ukn_dw"), QK_NOPE_DIM).reshape(KV_LORA_RANK, MLA_HEADS, QK_NOPE_DIM)
    dw_v = mm_tn(kvn, d_v, name="l0_uv_dw").reshape(KV_LORA_RANK, MLA_HEADS, V_HEAD_DIM)
    gw["mla_w_ukv"] = jnp.concatenate([dw_kn, dw_v], axis=-1).reshape(KV_LORA_RANK, -1)
    d_cq, d_qng = rowwise_bwd(f_rms, [cq], [qng], [d_qn], [0], [0], T, 0, "l0_qnorm_bwd")
    d_ckv, d_kvng = rowwise_bwd(f_rms, [ckv], [kvng], [d_kvn], [0], [0], T, 0, "l0_kvnorm_bwd")
    gw["mla_q_norm"] = d_qng.reshape(-1)
    gw["mla_kv_norm"] = d_kvng.reshape(-1)
    o_cq, o_ckv = D, D + Q_LORA_RANK
    o_kr = o_ckv + KV_LORA_RANK
    d_H0 = mm_nt_sum([(d_z0, 0, 0), (d_cq, o_cq, 0), (d_ckv, o_ckv, 0), (d_kr, o_kr, 0)], w["mla_w_in"], T, "l0_in_dx")
    d_head = jnp.concatenate([d_cq, d_ckv, d_kr], axis=1)
    gw["mla_w_in"] = jnp.concatenate([mm_tn(H0, d_head, name="l0_in_dw_head")[:, :P0_HEAD],
                                      mm_tn(H0, d_z0, name="l0_in_dw_z")], axis=1)
    d_X0, d_g0, d_sc0, d_sh0 = rowwise_bwd(f_norm_mod, [X0], [g0, mod_of(0, 1), mod_of(0, 0)], [d_H0],
                                           [0], [0, 1, 2], T, Lc, "l0_norm_bwd", lat_add=d_X1)
    dmod[0] = (d_sh0, d_sc0, d_gt0)
    gw["norm_g"] = jnp.stack([d_g0.reshape(D), d_g1.reshape(D)])
    dx = d_X0[Lc:]
    dmod_arr = jnp.stack([jnp.stack([dmod[i][j][:, 0, :] for j in range(3)], axis=1) for i in range(2)])
    ready = reducer[2](d_X0) if reducer is not None else {}
    return loss, dx, dmod_arr, gw, ready


SHARDED = {
    "mla_w_in": 1, "mla_w_uq": 1, "mla_w_ukv": 1, "mla_w_out": 0,
    "s5_w_in": 1, "s5_w_glu": 0, "s5_w_out": 0, "s5_d": 0, "s5_b_glu": 0,
}
SHARDED_MATS = ["mla_w_in", "mla_w_uq", "mla_w_ukv", "mla_w_out", "s5_w_in", "s5_w_glu", "s5_w_out"]
SHARDED_VECS = ["s5_d", "s5_b_glu"]
REPLICATED = ["norm_g", "mla_q_norm", "mla_kv_norm", "s5_a_re", "s5_a_im", "s5_log_step", "s5_b_re", "s5_b_im",
              "s5_c_re", "s5_c_im", "final_g"]
WEIGHT_ORDER = ["c_ctx", "ada_w", "ada_b", "norm_g", "mla_w_in", "mla_q_norm", "mla_w_uq", "mla_kv_norm", "mla_w_ukv",
                "mla_w_out", "s5_w_in", "s5_a_re", "s5_a_im", "s5_log_step", "s5_b_re", "s5_b_im", "s5_c_re", "s5_c_im",
                "s5_d", "s5_w_glu", "s5_b_glu", "s5_w_out", "final_g"]


P0_HEAD = Q_LORA_RANK + KV_LORA_RANK + QK_ROPE_DIM


P0_WIDTH = 1536


def w_in_to_kernel_order(w):
    pad = jnp.zeros((w.shape[0], P0_WIDTH - w.shape[1]), w.dtype)
    return jnp.concatenate([w[:, P0_HEAD:], w[:, :P0_HEAD], pad], axis=1)


LAYER0_MATS = ["mla_w_in", "mla_w_uq", "mla_w_ukv", "mla_w_out"]
LAYER1_MATS = ["s5_w_in", "s5_w_glu", "s5_w_out"]


def _whole_matrices(names, own_blocks, gathered):
    chip = _chip_index(_coords())
    full = {}
    for n, own, o in zip(names, own_blocks, gathered):
        slot = lax.broadcasted_iota(jnp.int32, (N_CHIP, 1, 1), 0)
        o = jnp.where(slot == chip, own[None], o.reshape((N_CHIP,) + own.shape))
        full[n] = o.reshape(-1, o.shape[-1]) if SHARDED[n] == 0 else o.transpose(1, 0, 2).reshape(o.shape[1], -1)
    return full


def gather_weights(ws):
    mats = [ws[n].astype(BF16) for n in LAYER0_MATS]
    full = _whole_matrices(LAYER0_MATS, mats, gather_halves(mats, "gather_weights"))
    full["mla_w_in"] = w_in_to_kernel_order(full["mla_w_in"])
    return full


def gather_weights_behind(ws):
    mats = [ws[n].astype(BF16) for n in LAYER1_MATS]
    flight, token = exchange_start(
        mats, [jax.ShapeDtypeStruct((N_CHIP,) + m.shape, m.dtype) for m in mats],
        [(f, a, lambda me, peer: None, a, lambda s: (_chip_index(s),)) for a in range(len(mats)) for f in CHIP_FLIPS],
        "gather_l1_start")

    def finish(after):
        own, got = exchange_wait(flight, after, "gather_l1_wait")
        return _whole_matrices(LAYER1_MATS, own, got)

    return token, finish


def _grad_slots(gw, names):
    slots = []
    for n in names:
        g = gw[n]
        if SHARDED[n] == 0:
            slots.append(g.reshape(N_CHIP, 2, g.shape[0] // (2 * N_CHIP), g.shape[1]))
        else:
            k, n4 = g.shape
            slots.append(g.reshape(k, N_CHIP, n4 // N_CHIP).transpose(1, 0, 2)
                         .reshape(N_CHIP, 2, k // 2, n4 // N_CHIP))
    return slots


def _to_sibling_half(count):
    return [(CORE_FLIP, i, lambda me, peer: (slice(None), 1 - me[2]), i, lambda s: None) for i in range(count)]


def _to_chips(count):
    return [(f, i, lambda me, peer: (_chip_index(peer),), i, lambda s: (_chip_index(s),))
            for i in range(count) for f in CHIP_FLIPS]


def _place():
    me = _coords()
    return jnp.stack([me[2], _chip_index(me)]).astype(jnp.int32)


def reduce_behind(names):
    state = {}
    count = len(names)

    def begin(gw):
        slots = _grad_slots(gw, names)
        lands = [jax.ShapeDtypeStruct((N_CHIP,) + s.shape[2:], F32) for s in slots]
        state["in"], token = exchange_start(slots, lands, _to_sibling_half(count), "grads_l1_swap_in_start")
        return token

    def middle(after):
        slots, got = exchange_wait(state["in"], after, "grads_l1_swap_in_wait")
        place = _place()
        sums = [pair_add(s, g, place[:1], BF16, f"grads_pair_{n}") for n, s, g in zip(names, slots, got)]
        lands = [jax.ShapeDtypeStruct(s.shape, s.dtype) for s in sums]
        state["out"], token = exchange_start(sums, lands, _to_chips(count), "grads_l1_scatter_start")
        return token

    def end(after):
        sums, parts = exchange_wait(state["out"], after, "grads_l1_scatter_wait")
        place = _place()
        return {n: sum_chips(p, s, place, f"grads_sum_{n}") for n, p, s in zip(names, parts, sums)}

    return begin, middle, end


def reduce_gradients(gw, ready_halves):
    me = _coords()
    place = _place()
    mat_names = [n for n in SHARDED_MATS if n not in ready_halves]
    slots = dict(zip(mat_names, _grad_slots(gw, mat_names)))
    small_names = REPLICATED + SHARDED_VECS
    small, small_offs = pack_flat([gw[n].astype(F32) for n in small_names], F32)
    small = jnp.pad(small, ((0, (-small.shape[0]) % (N_CHIP * 32)), (0, 0)))
    slots["small"] = small.reshape(N_CHIP, 2, -1, 128)
    names = list(slots)
    count = len(names)
    got = exchange([slots[n] for n in names],
                   [jax.ShapeDtypeStruct((N_CHIP,) + slots[n].shape[2:], F32) for n in names],
                   _to_sibling_half(count), [], "grads_swap_in")
    sums = [pair_add(slots[n], g, place[:1], F32 if n == "small" else BF16, f"grads_pair_{n}")
            for n, g in zip(names, got)]
    parts = exchange(sums, [jax.ShapeDtypeStruct(s.shape, s.dtype) for s in sums], _to_chips(count), [],
                     "grads_scatter")
    halves = {n: sum_chips(p, s, place, f"grads_sum_{n}") for n, p, s in zip(names, parts, sums)}
    halves.update(ready_halves)
    names = list(halves)
    fulls = exchange(
        [halves[n] for n in names], [jax.ShapeDtypeStruct(halves[n].shape, F32) for n in names],
        [(CORE_FLIP, i, lambda me, peer: (me[2],), i, lambda s: (s[2],)) for i in range(len(names))], [],
        "grads_swap_out", aliases={i: i for i in range(len(names))})
    out = {n: f.reshape(-1, f.shape[-1]) for n, f in zip(names, fulls)}
    (small_all,) = allgather_chips([out.pop("small")], "grads_gather_small")
    vals = unpack_flat(small_all, small_offs, [gw[n].shape for n in small_names])
    for n, v in zip(small_names, vals):
        if n in SHARDED_VECS:
            size = v.shape[0] // N_CHIP
            v = lax.dynamic_slice_in_dim(v, _chip_index(me) * size, size)
        out[n] = v
    return out


def kernel(x, c, ctx, c_ctx, ada_w, ada_b, norm_g, mla_w_in, mla_q_norm, mla_w_uq, mla_kv_norm, mla_w_ukv, mla_w_out, s5_w_in, s5_a_re, s5_a_im, s5_log_step, s5_b_re, s5_b_im, s5_c_re, s5_c_im, s5_d, s5_w_glu, s5_b_glu, s5_w_out, final_g, loss_target, m_c_ctx, m_ada_w, m_ada_b, m_norm_g, m_mla_w_in, m_mla_q_norm, m_mla_w_uq, m_mla_kv_norm, m_mla_w_ukv, m_mla_w_out, m_s5_w_in, m_s5_a_re, m_s5_a_im, m_s5_log_step, m_s5_b_re, m_s5_b_im, m_s5_c_re, m_s5_c_im, m_s5_d, m_s5_w_glu, m_s5_b_glu, m_s5_w_out, m_final_g, v_c_ctx, v_ada_w, v_ada_b, v_norm_g, v_mla_w_in, v_mla_q_norm, v_mla_w_uq, v_mla_kv_norm, v_mla_w_ukv, v_mla_w_out, v_s5_w_in, v_s5_a_re, v_s5_a_im, v_s5_log_step, v_s5_b_re, v_s5_b_im, v_s5_c_re, v_s5_c_im, v_s5_d, v_s5_w_glu, v_s5_b_glu, v_s5_w_out, v_final_g):
    args = dict(locals())
    weights = {n: args[n] for n in WEIGHT_ORDER}
    D = D_MODEL
    xi, yi, ci = _coords()
    chip = 2 * xi + yi
    me = 4 * xi + 2 * yi + ci
    n_col = ada_w.shape[2]

    late = gather_weights_behind({n: weights[n][0] for n in LAYER1_MATS})

    c_all = allgather_devices(jnp.pad(c, ((0, 7), (0, 0))), "gather_c")[:, 0, :]
    cond = jnp.concatenate([c_all, jnp.broadcast_to(c_ctx[None], (8, D))], axis=0)
    (s_cond,) = rowwise_fwd(lambda v: (_silu(v),), [cond], [], [D], [F32], 16, 0, "cond_silu")
    mod_cols = jnp.stack([mm_nn(s_cond, ada_w[i], name=f"mod_proj{i}") for i in range(2)])
    vec_tiles = [jnp.pad(weights[n][0].reshape(-1, 128), ((0, 6), (0, 0))) for n in SHARDED_VECS]
    mod_all, *vec_all = allgather_chips([mod_cols] + vec_tiles, "gather_mod")
    mod_all = mod_all.transpose(1, 2, 0, 3).reshape(2, 16, 3 * D) + ada_b[:, None, :]
    mod_l = lax.dynamic_index_in_dim(mod_all, me, axis=1, keepdims=False)
    mod_c = mod_all[:, 8, :]
    mod = jnp.stack([mod_c.reshape(2, 3, D), mod_l.reshape(2, 3, D)], axis=1)

    w = gather_weights({n: weights[n][0] for n in LAYER0_MATS})
    for n, v in zip(SHARDED_VECS, vec_all):
        w[n] = v[:, :2, :].reshape(-1)
    for n in ["norm_g", "final_g"]:
        w[n] = weights[n]
    for n in ["mla_q_norm", "mla_kv_norm", "s5_a_re", "s5_a_im", "s5_log_step", "s5_b_re", "s5_b_im",
              "s5_c_re", "s5_c_im"]:
        w[n] = weights[n][0]

    loss_me, dx, dmod, gw, ready = local_step(x[0], ctx[0], loss_target[0], mod, w, late, reduce_behind(LAYER1_MATS))

    dmod_rows, loss_all = _gather([dmod.reshape(2, 2, 3 * D), jnp.broadcast_to(loss_me, (8, 128))],
                                  ALL_FLIPS, _dev_index, N_DEV, "gather_dmod")
    loss = functools.reduce(lambda s, d: s + loss_all[d, 0, 0], range(1, N_DEV), loss_all[0, 0, 0])
    dm = jnp.concatenate([dmod_rows[:, :, 1, :], dmod_rows[:, :, 0, :]], axis=0).transpose(1, 0, 2)
    g_ada_b = jnp.sum(dm, axis=1)
    dm_cols = lax.dynamic_slice_in_dim(dm, chip * n_col, n_col, axis=2)
    g_ada_w = jnp.stack([mm_tn(s_cond, dm_cols[i], name=f"mod_proj_dw{i}") for i in range(2)])
    dmc = jnp.sum(dm_cols[:, 8:, :], axis=1)
    dmc8 = jnp.broadcast_to(dmc[:, None, :], (2, 8, n_col))
    g_sc = mm_nt(dmc8[0], ada_w[0], name="mod_proj_dx0")[0] + mm_nt(dmc8[1], ada_w[1], name="mod_proj_dx1")[0]
    g_sc_all = allgather_devices(jnp.broadcast_to(g_sc[None], (8, D)), "gather_dcond")[:, 0, :]
    g_silu_cc = g_sc_all[0] + g_sc_all[2] + g_sc_all[4] + g_sc_all[6]
    (g_c_ctx,) = rowwise_bwd(lambda v: (_silu(v),), [jnp.broadcast_to(c_ctx[None], (8, D))], [],
                             [jnp.broadcast_to(g_silu_cc[None], (8, D))], [0], [], 8, 0, "cond_silu_bwd")
    g_c_ctx = g_c_ctx[0]

    red = reduce_gradients(gw, ready)
    grads = {"c_ctx": g_c_ctx, "ada_w": g_ada_w, "ada_b": g_ada_b}
    for n in WEIGHT_ORDER[3:]:
        grads[n] = red[n].reshape(weights[n].shape)

    deltas, new_m, new_v = {}, {}, {}
    small = [n for n in WEIGHT_ORDER if weights[n].size < 50000]
    for n in WEIGHT_ORDER:
        if n in small:
            continue
        shp = weights[n].shape
        w2 = weights[n].reshape(-1, shp[-1])
        d_, m_, v_ = adamw(w2, grads[n].reshape(w2.shape), args["m_" + n].reshape(w2.shape),
                           args["v_" + n].reshape(w2.shape), name=f"adamw_{n}")
        deltas[n], new_m[n], new_v[n] = d_.reshape(shp), m_.reshape(shp), v_.reshape(shp)
    packs = []
    offs = None
    for src in (weights, grads, {n: args["m_" + n] for n in small}, {n: args["v_" + n] for n in small}):
        buf, offs = pack_flat([src[n] for n in small], F32)
        packs.append(buf)
    outs = adamw(*packs, name="adamw_small")
    for res, dst in zip(outs, (deltas, new_m, new_v)):
        for n, val in zip(small, unpack_flat(res, offs, [weights[n].shape for n in small])):
            dst[n] = val

    return (loss, dx[None], *[grads[n] for n in WEIGHT_ORDER], *[deltas[n] for n in WEIGHT_ORDER],
            *[new_m[n] for n in WEIGHT_ORDER], *[new_v[n] for n in WEIGHT_ORDER])
```

```python
import functools
import math

import jax
import jax.numpy as jnp
import numpy as np
from jax import lax
from jax.experimental import pallas as pl
from jax.experimental.pallas import tpu as pltpu

F32 = jnp.float32
BF16 = jnp.bfloat16

D_MODEL = 1024
GRID_W = 64
EPS = 1e-6
MLA_HEADS = 16
QK_NOPE_DIM = 64
QK_ROPE_DIM = 32
V_HEAD_DIM = 64
Q_LORA_RANK = 256
KV_LORA_RANK = 128
QK_DIM = QK_NOPE_DIM + QK_ROPE_DIM
SOFTMAX_SCALE = QK_DIM ** -0.5
ROPE_THETA = 10000.0
S5_GROUP = 16
S5_GROUPS = D_MODEL // S5_GROUP
S5_STATE = 64
S5_LANES = S5_GROUPS * S5_STATE
N_SEG = 8
GROUPS_PER_BLOCK = 8
N_BLOCKS = S5_GROUPS // GROUPS_PER_BLOCK
BLK_CH = GROUPS_PER_BLOCK * S5_GROUP
BLK_ST = GROUPS_PER_BLOCK * S5_STATE

ADAM_LR = 0.001
ADAM_B1 = 0.9
ADAM_B2 = 0.999
ADAM_EPS = 1e-08
ADAM_WD = 0.01
ADAM_STEP = 10

N_DEV = 8
N_CHIP = 4
MESH = pl.DeviceIdType.MESH
VMEM_LIMIT = 52 * 1024 * 1024
ROW_TILE = 256


def _params(sem=None, vmem=None):
    return pltpu.CompilerParams(dimension_semantics=sem, vmem_limit_bytes=vmem)


def mm_nn(a, b, out_dtype=F32, name="mm_nn"):
    M, K = a.shape
    N = b.shape[1]
    tm = math.gcd(ROW_TILE, M)

    def body(a_ref, b_ref, o_ref):
        o_ref[...] = jnp.dot(a_ref[...].astype(BF16), b_ref[...].astype(BF16),
                             preferred_element_type=F32).astype(o_ref.dtype)

    return pl.pallas_call(
        body, out_shape=jax.ShapeDtypeStruct((M, N), out_dtype), grid=(M // tm,),
        in_specs=[pl.BlockSpec((tm, K), lambda i: (i, 0)), pl.BlockSpec((K, N), lambda i: (0, 0))],
        out_specs=pl.BlockSpec((tm, N), lambda i: (i, 0)),
        compiler_params=_params(("parallel",), VMEM_LIMIT), name=name)(a, b)


def mm_nt(a, b, out_dtype=F32, name="mm_nt"):
    M, N = a.shape
    K = b.shape[0]
    tm = math.gcd(ROW_TILE, M)

    def body(a_ref, b_ref, o_ref):
        o_ref[...] = lax.dot_general(a_ref[...].astype(BF16), b_ref[...].astype(BF16),
                                     (((1,), (1,)), ((), ())),
                                     preferred_element_type=F32).astype(o_ref.dtype)

    return pl.pallas_call(
        body, out_shape=jax.ShapeDtypeStruct((M, K), out_dtype), grid=(M // tm,),
        in_specs=[pl.BlockSpec((tm, N), lambda i: (i, 0)), pl.BlockSpec((K, N), lambda i: (0, 0))],
        out_specs=pl.BlockSpec((tm, K), lambda i: (i, 0)),
        compiler_params=_params(("parallel",), VMEM_LIMIT), name=name)(a, b)


def mm_nt_sum(terms, w, n_rows, name):
    K = w.shape[0]
    tm = math.gcd(ROW_TILE, n_rows, *[t[2] for t in terms])

    def body(*refs):
        i = pl.program_id(0)
        w_ref, o_ref = refs[len(terms)], refs[len(terms) + 1]
        acc = None
        for a_ref, (a, off, first) in zip(refs, terms):
            part = lax.dot_general(a_ref[...].astype(BF16), w_ref[:, off:off + a.shape[1]].astype(BF16), NT_DIMS,
                                   preferred_element_type=F32)
            if first:
                part = jnp.where(i >= first // tm, part, 0.0)
            acc = part if acc is None else acc + part
        o_ref[...] = acc

    def a_spec(a, first):
        skip = first // tm
        return pl.BlockSpec((tm, a.shape[1]), lambda i: (jnp.maximum(i - skip, 0), 0))

    return pl.pallas_call(
        body, out_shape=jax.ShapeDtypeStruct((n_rows, K), F32), grid=(n_rows // tm,),
        in_specs=[a_spec(a, first) for a, _, first in terms] + [pl.BlockSpec(w.shape, lambda i: (0, 0))],
        out_specs=pl.BlockSpec((tm, K), lambda i: (i, 0)),
        compiler_params=_params(("parallel",), VMEM_LIMIT), name=name)(*[t[0] for t in terms], w)


def mm_tn(a, b, name="mm_tn"):
    M, K = a.shape
    N = b.shape[1]
    tn = math.gcd(512, N) if N % 128 == 0 and N > 512 else N

    def body(a_ref, b_ref, o_ref):
        o_ref[...] = lax.dot_general(a_ref[...].astype(BF16), b_ref[...].astype(BF16),
                                     (((0,), (0,)), ((), ())), preferred_element_type=F32)

    return pl.pallas_call(
        body, out_shape=jax.ShapeDtypeStruct((K, N), F32), grid=(N // tn,),
        in_specs=[pl.BlockSpec((M, K), lambda j: (0, 0)), pl.BlockSpec((M, tn), lambda j: (0, j))],
        out_specs=pl.BlockSpec((K, tn), lambda j: (0, j)),
        compiler_params=_params(("parallel",), VMEM_LIMIT), name=name)(a, b)


class Rows:
    def __init__(self, arr, width=None, row_off=0, col_blk=0):
        self.arr = arr
        self.width = arr.shape[1] if width is None else width
        self.row_off = row_off
        self.col_blk = col_blk

    def spec(self, tm):
        ro, cb = self.row_off // tm, self.col_blk
        return pl.BlockSpec((tm, self.width), lambda i: (i + ro, cb))


def _as_rows(x):
    return x if isinstance(x, Rows) else Rows(x)


def _row_tile(n_rows, n_ctx_rows, rows):
    tm = math.gcd(ROW_TILE, n_rows, n_ctx_rows)
    for r in rows:
        tm = math.gcd(tm, r.row_off)
    return tm


def _bc_spec(arr, n_ctx_blocks):
    g, _, d = arr.shape
    if g == 1:
        return pl.BlockSpec((1, 1, d), lambda i: (0, 0, 0))
    return pl.BlockSpec((1, 1, d), lambda i: ((i >= n_ctx_blocks).astype(jnp.int32), 0, 0))


def rowwise_fwd(fn, rows, bcs, out_dims, out_dtypes, n_rows, n_ctx_rows, name):
    rows = [_as_rows(r) for r in rows]
    tm = _row_tile(n_rows, n_ctx_rows, rows)
    ncb = n_ctx_rows // tm
    nr, nb = len(rows), len(bcs)

    def body(*refs):
        vals = [r[...].astype(F32) for r in refs[:nr]] + [b[0].astype(F32) for b in refs[nr:nr + nb]]
        outs = fn(*vals)
        for o_ref, v in zip(refs[nr + nb:], outs):
            o_ref[...] = v.astype(o_ref.dtype)

    outs = pl.pallas_call(
        body,
        out_shape=[jax.ShapeDtypeStruct((n_rows, d), dt) for d, dt in zip(out_dims, out_dtypes)],
        grid=(n_rows // tm,),
        in_specs=[r.spec(tm) for r in rows] + [_bc_spec(b, ncb) for b in bcs],
        out_specs=[pl.BlockSpec((tm, d), lambda i: (i, 0)) for d in out_dims],
        compiler_params=_params(("parallel",), VMEM_LIMIT), name=name)(*[r.arr for r in rows], *bcs)
    return outs


def rowwise_bwd(fn, rows, bcs, cts, diff_rows, diff_bcs, n_rows, n_ctx_rows, name, ct_extra=None, lat_add=None):
    rows = [_as_rows(r) for r in rows]
    cts = [_as_rows(c) for c in cts]
    extra = [_as_rows(ct_extra)] if ct_extra is not None else []
    tm = _row_tile(n_rows, n_ctx_rows, rows + cts + extra)
    ncb = n_ctx_rows // tm
    nr, nb, nc = len(rows), len(bcs), len(cts)
    ndr, ndb = len(diff_rows), len(diff_bcs)
    n_in = nr + nb + nc + len(extra) + (lat_add is not None)

    def body(*refs):
        i = pl.program_id(0)
        rvals = [r[...].astype(F32) for r in refs[:nr]]
        bvals = [b[0].astype(F32) for b in refs[nr:nr + nb]]
        cvals = [c[...].astype(F32) for c in refs[nr + nb:nr + nb + nc]]
        if extra:
            cvals[0] = cvals[0] + refs[nr + nb + nc][...].astype(F32)
        outs = refs[n_in:]

        def f(*d):
            rv, bv = list(rvals), list(bvals)
            for k, idx in enumerate(diff_rows):
                rv[idx] = d[k]
            for k, idx in enumerate(diff_bcs):
                bv[idx] = d[ndr + k]
            return tuple(fn(*rv, *bv))

        primals = [rvals[k] for k in diff_rows] + [bvals[k] for k in diff_bcs]
        _, vjp = jax.vjp(f, *primals)
        grads = list(vjp(tuple(cvals)))
        if lat_add is not None:
            add = refs[n_in - 1][...]
            grads[0] = grads[0] + (add if lat_add.shape[0] == n_rows else jnp.where(i >= ncb, add, 0.0))
        for k in range(ndr):
            outs[k][...] = grads[k].astype(outs[k].dtype)
        for k, idx in enumerate(diff_bcs):
            o_ref = outs[ndr + k]
            first = (i == 0)
            if bcs[idx].shape[0] == 2:
                first = first | (i == ncb)

            @pl.when(first)
            def _(o_ref=o_ref):
                o_ref[...] = jnp.zeros_like(o_ref)

            o_ref[0] += grads[ndr + k]

    out_shape = [jax.ShapeDtypeStruct((n_rows, rows[k].width), F32) for k in diff_rows]
    out_shape += [jax.ShapeDtypeStruct(bcs[k].shape, F32) for k in diff_bcs]
    out_specs = [pl.BlockSpec((tm, rows[k].width), lambda i: (i, 0)) for k in diff_rows]
    out_specs += [_bc_spec(bcs[k], ncb) for k in diff_bcs]
    ins = [r.arr for r in rows] + list(bcs) + [c.arr for c in cts + extra]
    in_specs = [r.spec(tm) for r in rows] + [_bc_spec(b, ncb) for b in bcs] + [c.spec(tm) for c in cts + extra]
    if lat_add is not None:
        ins.append(lat_add)
        skip = ncb if lat_add.shape[0] != n_rows else 0
        in_specs.append(pl.BlockSpec((tm, lat_add.shape[1]), lambda i: (jnp.maximum(i - skip, 0), 0)))
    outs = pl.pallas_call(
        body, out_shape=out_shape, grid=(n_rows // tm,), in_specs=in_specs, out_specs=out_specs,
        compiler_params=_params(("arbitrary",), VMEM_LIMIT), name=name)(*ins)
    return outs


def _rms(x):
    return x * lax.rsqrt(jnp.mean(x * x, axis=-1, keepdims=True) + EPS)


def _sigmoid(x):
    return 0.5 * (jnp.tanh(0.5 * x) + 1.0)


def _silu(x):
    return x * _sigmoid(x)


def _gelu_tanh(x):
    return 0.5 * x * (1.0 + jnp.tanh(math.sqrt(2.0 / math.pi) * (x + 0.044715 * (x * x * x))))


def f_norm_mod(x, g, sc, sh):
    return ((_rms(x) * g) * (1.0 + sc) + sh,)


def f_rms(x, g):
    return (_rms(x) * g,)


def f_gate(o, z):
    return (o * _silu(z),)


def f_s5_act(y, u, d):
    return (_gelu_tanh(y + d * u),)


def f_s5_glu(ya, gl, z, b):
    return (ya * _sigmoid(gl + b) * _silu(z),)


def mla_post_fwd(o, p0, x0, gate, w_out, n_ctx, name="l0_post"):
    n, d = o.shape
    tm = math.gcd(ROW_TILE, n, n_ctx)
    ncb = n_ctx // tm

    def body(o_ref, z_ref, x_ref, gt_ref, w_ref, x1_ref, og_ref, out_ref):
        og = f_gate(o_ref[...], z_ref[...])[0].astype(BF16)
        out = jnp.dot(og, w_ref[...], preferred_element_type=F32)
        og_ref[...] = og
        out_ref[...] = out
        x1_ref[...] = x_ref[...] + gt_ref[0] * out

    row = pl.BlockSpec((tm, d), lambda i: (i, 0))
    return pl.pallas_call(
        body, out_shape=[jax.ShapeDtypeStruct((n, d), F32), jax.ShapeDtypeStruct((n, d), BF16),
                         jax.ShapeDtypeStruct((n, d), F32)],
        grid=(n // tm,),
        in_specs=[row, row, row, _bc_spec(gate, ncb), pl.BlockSpec((d, d), lambda i: (0, 0))],
        out_specs=[row, row, row],
        compiler_params=_params(("parallel",), VMEM_LIMIT), name=name)(o, p0, x0, gate, w_out)


def mla_post_bwd(dx1, out, og, o, p0, gate, w_out, n_ctx, name="l0_post_bwd"):
    n, d = o.shape
    tm = math.gcd(ROW_TILE, n, n_ctx)
    ncb = n_ctx // tm

    def body(dx_ref, out_ref, og_ref, o_ref, z_ref, gt_ref, w_ref, do_ref, dz_ref, dgt_ref, dw_ref):
        i = pl.program_id(0)

        @pl.when(i == 0)
        def _():
            dw_ref[...] = jnp.zeros_like(dw_ref)

        @pl.when((i == 0) | (i == ncb))
        def _():
            dgt_ref[...] = jnp.zeros_like(dgt_ref)

        dx = dx_ref[...]
        dgt_ref[0] += jnp.sum(dx * out_ref[...], axis=0, keepdims=True)
        d_out16 = (gt_ref[0] * dx).astype(BF16)
        dw_ref[...] += lax.dot_general(og_ref[...], d_out16, (((0,), (0,)), ((), ())), preferred_element_type=F32)
        d_og = lax.dot_general(d_out16, w_ref[...], NT_DIMS, preferred_element_type=F32)
        _, gate_vjp = jax.vjp(lambda o_, z_: f_gate(o_, z_), o_ref[...], z_ref[...])
        d_o, d_z = gate_vjp((d_og,))
        do_ref[...] = d_o
        dz_ref[...] = d_z

    row = pl.BlockSpec((tm, d), lambda i: (i, 0))
    mat = pl.BlockSpec((d, d), lambda i: (0, 0))
    return pl.pallas_call(
        body, out_shape=[jax.ShapeDtypeStruct((n, d), F32), jax.ShapeDtypeStruct((n, d), F32),
                         jax.ShapeDtypeStruct(gate.shape, F32), jax.ShapeDtypeStruct((d, d), F32)],
        grid=(n // tm,),
        in_specs=[row, row, row, row, row, _bc_spec(gate, ncb), mat],
        out_specs=[row, row, _bc_spec(gate, ncb), mat],
        compiler_params=_params(("arbitrary",), VMEM_LIMIT), name=name)(dx1, out, og, o, p0, gate, w_out)


def s5_tail(y_ssm, p1, x1p, target, n_ctx, d_vec, b_glu, gate, final_g, w_glu, w_out, name="l1_tail"):
    n, d = y_ssm.shape
    tm = math.gcd(ROW_TILE, n, n_ctx)
    off = n_ctx // tm
    tn_dims = (((0,), (0,)), ((), ()))

    def row_loss(x, g, t):
        e = _rms(x) * g - t
        return 0.5 * (e * e) * (1.0 / d)

    def body(y_ref, u_ref, z_ref, x1_ref, t_ref, d_ref, b_ref, gt_ref, fg_ref, wg_ref, wo_ref,
             l_ref, dx_ref, dy_ref, du_ref, dz_ref, dfg_ref, dgt_ref, db_ref, dd_ref, dwg_ref, dwo_ref):
        @pl.when(pl.program_id(0) == 0)
        def _():
            for r in (l_ref, dfg_ref, dgt_ref, db_ref, dd_ref, dwg_ref, dwo_ref):
                r[...] = jnp.zeros_like(r)

        u, z, tgt, gt = u_ref[...], z_ref[...], t_ref[...], gt_ref[...]
        (ya,), act_vjp = jax.vjp(lambda y_, u_, d_: f_s5_act(y_, u_, d_), y_ref[...], u, d_ref[...])
        ya16 = ya.astype(BF16)
        gl = jnp.dot(ya16, wg_ref[...], preferred_element_type=F32)
        (y3,), glu_vjp = jax.vjp(lambda a_, g_, z_, b_: f_s5_glu(a_, g_, z_, b_), ya, gl, z, b_ref[...])
        y3_16 = y3.astype(BF16)
        out1 = jnp.dot(y3_16, wo_ref[...], preferred_element_type=F32)
        lterm, loss_vjp = jax.vjp(lambda x_, g_: row_loss(x_, g_, tgt), x1_ref[...] + gt * out1, fg_ref[...])
        dx2, dfg = loss_vjp(jnp.ones_like(lterm))
        l_ref[...] += jnp.sum(lterm, axis=0, keepdims=True)
        dfg_ref[...] += dfg
        dx_ref[...] = dx2
        dgt_ref[...] += jnp.sum(dx2 * out1, axis=0, keepdims=True)
        d_out16 = (gt * dx2).astype(BF16)
        dwo_ref[...] += lax.dot_general(y3_16, d_out16, tn_dims, preferred_element_type=F32)
        d_y3 = lax.dot_general(d_out16, wo_ref[...], NT_DIMS, preferred_element_type=F32)
        d_ya, d_gl, d_z, d_b = glu_vjp((d_y3,))
        dz_ref[...] = d_z
        db_ref[...] += d_b
        d_gl16 = d_gl.astype(BF16)
        dwg_ref[...] += lax.dot_general(ya16, d_gl16, tn_dims, preferred_element_type=F32)
        d_ya = d_ya + lax.dot_general(d_gl16, wg_ref[...], NT_DIMS, preferred_element_type=F32)
        d_y, d_u, d_d = act_vjp((d_ya,))
        dy_ref[...] = d_y
        du_ref[...] = d_u
        dd_ref[...] += d_d

    row = pl.BlockSpec((tm, d), lambda i: (i, 0))
    vecs = pl.BlockSpec((1, d), lambda i: (0, 0))
    mat = pl.BlockSpec((d, d), lambda i: (0, 0))
    return pl.pallas_call(
        body,
        out_shape=[jax.ShapeDtypeStruct((1, d), F32)] + [jax.ShapeDtypeStruct((n, d), F32)] * 4
        + [jax.ShapeDtypeStruct((1, d), F32)] * 4 + [jax.ShapeDtypeStruct((d, d), F32)] * 2,
        grid=(n // tm,),
        in_specs=[row, pl.BlockSpec((tm, d), lambda i: (i + off, 0)), pl.BlockSpec((tm, d), lambda i: (i + off, 1)),
                  pl.BlockSpec((tm, d), lambda i: (i + off, 0)), row, vecs, vecs, vecs, vecs, mat, mat],
        out_specs=[vecs, row, row, row, row, vecs, vecs, vecs, vecs, mat, mat],
        compiler_params=_params(("arbitrary",), VMEM_LIMIT), name=name)(
            y_ssm, p1, p1, x1p, target, d_vec, b_glu, gate, final_g, w_glu, w_out)


NT_DIMS = (((1,), (1,)), ((), ()))
HEAD_LANES = 128
N_PAIRS = MLA_HEADS // 2


def _own_lanes(shape, hh):
    lane = lax.broadcasted_iota(jnp.int32, shape, len(shape) - 1)
    return (lane < V_HEAD_DIM) if hh == 0 else (lane >= V_HEAD_DIM)


def _rope_tiles(x, cos, sin_next, sin_prev, inverse):
    width = x.shape[-1]
    reps = width // HEAD_LANES
    c, sn, sp = (jnp.tile(t, (1, reps)) for t in (cos, sin_next, sin_prev))
    if inverse:
        return x * c + pltpu.roll(x * sn, 8, 1) + pltpu.roll(x * sp, width - 8, 1)
    return x * c + pltpu.roll(x, width - 8, 1) * sn + pltpu.roll(x, 8, 1) * sp


def _col_to_row(col):
    n = col.shape[0]
    hi = col.astype(BF16)
    r1 = col - hi.astype(F32)
    mid = r1.astype(BF16)
    lo = (r1 - mid.astype(F32)).astype(BF16)
    lane = lax.broadcasted_iota(jnp.int32, (n, HEAD_LANES), 1)
    terms = jnp.where(lane == 0, hi, jnp.where(lane == 1, mid, jnp.where(lane == 2, lo, jnp.zeros_like(hi))))
    eye = (lax.broadcasted_iota(jnp.int32, (n, n), 0) == lax.broadcasted_iota(jnp.int32, (n, n), 1)).astype(BF16)
    rows = lax.dot_general(terms, eye, (((0,), (0,)), ((), ())), preferred_element_type=F32)
    return rows[0:1] + rows[1:2] + rows[2:3]


def attn_fwd(qb, kb, vb, n_ctx):
    T = qb.shape[0]
    tq = math.gcd(ROW_TILE, n_ctx)
    nq, ncb = T // tq, n_ctx // tq

    def body(q_ref, k_ref, v_ref, o_ref, lse_ref, lse_row_ref):
        qi = pl.program_id(1)

        def rows(n_keys):
            v = v_ref[:n_keys, :]
            outs = []
            for hh in range(2):
                hs = slice(hh * HEAD_LANES, (hh + 1) * HEAD_LANES)
                s = lax.dot_general(q_ref[:, hs], k_ref[:n_keys, hs], NT_DIMS,
                                    preferred_element_type=F32) * SOFTMAX_SCALE
                m = jnp.max(s, axis=-1, keepdims=True)
                p = jnp.exp(s - m)
                l = jnp.sum(p, axis=-1, keepdims=True)
                outs.append(jnp.dot(p.astype(BF16), v, preferred_element_type=F32) / l)
                lse = m + jnp.log(l)
                lse_ref[hh] = lse
                lse_row_ref[hh] = _col_to_row(lse)
            o_ref[...] = jnp.where(_own_lanes(outs[0].shape, 0), outs[0], outs[1])

        pl.when(qi < ncb)(lambda: rows(n_ctx))
        pl.when(qi >= ncb)(lambda: rows(T))

    return pl.pallas_call(
        body,
        out_shape=[jax.ShapeDtypeStruct((T, MLA_HEADS * V_HEAD_DIM), F32),
                   jax.ShapeDtypeStruct((MLA_HEADS, T, 1), F32), jax.ShapeDtypeStruct((MLA_HEADS, 1, T), F32)],
        grid=(N_PAIRS, nq),
        in_specs=[pl.BlockSpec((tq, 2 * HEAD_LANES), lambda h, i: (i, h)),
                  pl.BlockSpec((T, 2 * HEAD_LANES), lambda h, i: (0, h)),
                  pl.BlockSpec((T, 2 * V_HEAD_DIM), lambda h, i: (0, h))],
        out_specs=[pl.BlockSpec((tq, 2 * V_HEAD_DIM), lambda h, i: (i, h)),
                   pl.BlockSpec((2, tq, 1), lambda h, i: (h, i, 0)),
                   pl.BlockSpec((2, 1, tq), lambda h, i: (h, 0, i))],
        compiler_params=_params(("parallel", "parallel"), VMEM_LIMIT), name="attn_fwd")(qb, kb, vb)


def attn_bwd_dq(qb, kb, vb, o, do, lse, tabs, n_ctx):
    T = qb.shape[0]
    tq = math.gcd(ROW_TILE, n_ctx)
    nq, ncb = T // tq, n_ctx // tq

    def body(q_ref, k_ref, v_ref, o_ref, do_ref, lse_ref, c_ref, sn_ref, sp_ref, dq_ref, delta_ref):
        qi = pl.program_id(1)

        def rows(n_keys):
            v = v_ref[:n_keys, :]
            dqs = []
            for hh in range(2):
                hs = slice(hh * HEAD_LANES, (hh + 1) * HEAD_LANES)
                k = k_ref[:n_keys, hs]
                do = jnp.where(_own_lanes(do_ref.shape, hh), do_ref[...], 0.0)
                delta = jnp.sum(do * o_ref[...], axis=-1, keepdims=True)
                s = lax.dot_general(q_ref[:, hs], k, NT_DIMS, preferred_element_type=F32) * SOFTMAX_SCALE
                p = jnp.exp(s - lse_ref[hh])
                dp = lax.dot_general(do.astype(BF16), v, NT_DIMS, preferred_element_type=F32)
                ds = p * (dp - delta) * SOFTMAX_SCALE
                dqs.append(jnp.dot(ds.astype(BF16), k, preferred_element_type=F32))
                delta_ref[hh] = _col_to_row(delta)
            dq = jnp.concatenate(dqs, axis=1)
            dq_ref[...] = _rope_tiles(dq, c_ref[...], sn_ref[...], sp_ref[...], True).astype(BF16)

        pl.when(qi < ncb)(lambda: rows(n_ctx))
        pl.when(qi >= ncb)(lambda: rows(T))

    tab = pl.BlockSpec((tq, HEAD_LANES), lambda h, i: (i, 0))
    return pl.pallas_call(
        body,
        out_shape=[jax.ShapeDtypeStruct((T, MLA_HEADS * HEAD_LANES), BF16),
                   jax.ShapeDtypeStruct((MLA_HEADS, 1, T), F32)],
        grid=(N_PAIRS, nq),
        in_specs=[pl.BlockSpec((tq, 2 * HEAD_LANES), lambda h, i: (i, h)),
                  pl.BlockSpec((T, 2 * HEAD_LANES), lambda h, i: (0, h)),
                  pl.BlockSpec((T, 2 * V_HEAD_DIM), lambda h, i: (0, h)),
                  pl.BlockSpec((tq, 2 * V_HEAD_DIM), lambda h, i: (i, h)),
                  pl.BlockSpec((tq, 2 * V_HEAD_DIM), lambda h, i: (i, h)),
                  pl.BlockSpec((2, tq, 1), lambda h, i: (h, i, 0)), tab, tab, tab],
        out_specs=[pl.BlockSpec((tq, 2 * HEAD_LANES), lambda h, i: (i, h)),
                   pl.BlockSpec((2, 1, tq), lambda h, i: (h, 0, i))],
        compiler_params=_params(("parallel", "parallel"), VMEM_LIMIT), name="attn_bwd_dq")(
            qb, kb, vb, o, do, lse, *tabs)


def attn_bwd_dkv(qb, kb, vb, do, lse_rows, delta_rows, n_ctx):
    T = qb.shape[0]
    tq = math.gcd(ROW_TILE, n_ctx)
    nq, ncb = T // tq, n_ctx // tq

    def body(q_ref, do_ref, lse_ref, delta_ref, k_ref, v_ref, dk_ref, dv_ref):
        kj = pl.program_id(1)

        def cols(first):
            v = v_ref[...]
            do_all = do_ref[first:, :]
            dv = None
            for hh in range(2):
                hs = slice(hh * HEAD_LANES, (hh + 1) * HEAD_LANES)
                k = k_ref[:, hs]
                q = q_ref[first:, hs]
                do16 = jnp.where(_own_lanes(do_all.shape, hh), do_all, 0.0).astype(BF16)
                st = lax.dot_general(k, q, NT_DIMS, preferred_element_type=F32) * SOFTMAX_SCALE
                pt = jnp.exp(st - lse_ref[hh, :, first:])
                dv_h = jnp.dot(pt.astype(BF16), do16, preferred_element_type=F32)
                dv = dv_h if dv is None else dv + dv_h
                dpt = lax.dot_general(v, do16, NT_DIMS, preferred_element_type=F32)
                dst = pt * (dpt - delta_ref[hh, :, first:]) * SOFTMAX_SCALE
                dk_ref[:, hs] = jnp.dot(dst.astype(BF16), q, preferred_element_type=F32)
            dv_ref[...] = dv

        pl.when(kj < ncb)(lambda: cols(0))
        pl.when(kj >= ncb)(lambda: cols(n_ctx))

    return pl.pallas_call(
        body,
        out_shape=[jax.ShapeDtypeStruct((T, MLA_HEADS * HEAD_LANES), F32),
                   jax.ShapeDtypeStruct((T, MLA_HEADS * V_HEAD_DIM), F32)],
        grid=(N_PAIRS, nq),
        in_specs=[pl.BlockSpec((T, 2 * HEAD_LANES), lambda h, j: (0, h)),
                  pl.BlockSpec((T, 2 * V_HEAD_DIM), lambda h, j: (0, h)),
                  pl.BlockSpec((2, 1, T), lambda h, j: (h, 0, 0)),
                  pl.BlockSpec((2, 1, T), lambda h, j: (h, 0, 0)),
                  pl.BlockSpec((tq, 2 * HEAD_LANES), lambda h, j: (j, h)),
                  pl.BlockSpec((tq, 2 * V_HEAD_DIM), lambda h, j: (j, h))],
        out_specs=[pl.BlockSpec((tq, 2 * HEAD_LANES), lambda h, j: (j, h)),
                   pl.BlockSpec((tq, 2 * V_HEAD_DIM), lambda h, j: (j, h))],
        compiler_params=_params(("parallel", "parallel"), VMEM_LIMIT), name="attn_bwd_dkv")(
            qb, do, lse_rows, delta_rows, kb, vb)


def _split_bf16(x):
    hi = x.astype(BF16)
    return hi, (x - hi.astype(F32)).astype(BF16)


def q_heads(qn, w_uq_p, tabs, name="l0_uq"):
    T, K = qn.shape
    N = w_uq_p.shape[1]
    tm = math.gcd(ROW_TILE, T)

    def body(a_ref, w_ref, c_ref, sn_ref, sp_ref, o_ref):
        acc = jnp.dot(a_ref[...], w_ref[...], preferred_element_type=F32)
        o_ref[...] = _rope_tiles(acc, c_ref[...], sn_ref[...], sp_ref[...], False).astype(BF16)

    tab = pl.BlockSpec((tm, HEAD_LANES), lambda i: (i, 0))
    return pl.pallas_call(
        body, out_shape=jax.ShapeDtypeStruct((T, N), BF16), grid=(T // tm,),
        in_specs=[pl.BlockSpec((tm, K), lambda i: (i, 0)), pl.BlockSpec((K, N), lambda i: (0, 0)), tab, tab, tab],
        out_specs=pl.BlockSpec((tm, N), lambda i: (i, 0)),
        compiler_params=_params(("parallel",), VMEM_LIMIT), name=name)(qn, w_uq_p, *tabs)


def kv_heads(kvn, w_kn_p, w_v, kr, spread, tabs, name="l0_ukv"):
    T, K = kvn.shape
    N = w_kn_p.shape[1]
    NV = w_v.shape[1]
    tm = math.gcd(ROW_TILE, T)

    def body(a_ref, wk_ref, wv_ref, kr_ref, e_ref, c_ref, sn_ref, sp_ref, k_ref, v_ref):
        a = a_ref[...]
        hi, lo = _split_bf16(kr_ref[...])
        acc = (jnp.dot(a, wk_ref[...], preferred_element_type=F32)
               + jnp.dot(hi, e_ref[...], preferred_element_type=F32)
               + jnp.dot(lo, e_ref[...], preferred_element_type=F32))
        k_ref[...] = _rope_tiles(acc, c_ref[...], sn_ref[...], sp_ref[...], False).astype(BF16)
        v_ref[...] = jnp.dot(a, wv_ref[...], preferred_element_type=F32).astype(BF16)

    tab = pl.BlockSpec((tm, HEAD_LANES), lambda i: (i, 0))
    return pl.pallas_call(
        body, out_shape=[jax.ShapeDtypeStruct((T, N), BF16), jax.ShapeDtypeStruct((T, NV), BF16)], grid=(T // tm,),
        in_specs=[pl.BlockSpec((tm, K), lambda i: (i, 0)), pl.BlockSpec((K, N), lambda i: (0, 0)),
                  pl.BlockSpec((K, NV), lambda i: (0, 0)), pl.BlockSpec((tm, QK_ROPE_DIM), lambda i: (i, 0)),
                  pl.BlockSpec((QK_ROPE_DIM, N), lambda i: (0, 0)), tab, tab, tab],
        out_specs=[pl.BlockSpec((tm, N), lambda i: (i, 0)), pl.BlockSpec((tm, NV), lambda i: (i, 0))],
        compiler_params=_params(("parallel",), VMEM_LIMIT), name=name)(kvn, w_kn_p, w_v, kr, spread, *tabs)


def heads_unrope(d, tabs, spread=None, name="unrope"):
    T, N = d.shape
    tm = math.gcd(ROW_TILE, T)

    def body(*refs):
        if spread is None:
            d_ref, c_ref, sn_ref, sp_ref, o_ref = refs
        else:
            d_ref, c_ref, sn_ref, sp_ref, e_ref, o_ref, kr_ref = refs
        g = _rope_tiles(d_ref[...], c_ref[...], sn_ref[...], sp_ref[...], True)
        o_ref[...] = g.astype(BF16)
        if spread is not None:
            hi, lo = _split_bf16(g)
            kr_ref[...] = (lax.dot_general(hi, e_ref[...], NT_DIMS, preferred_element_type=F32)
                           + lax.dot_general(lo, e_ref[...], NT_DIMS, preferred_element_type=F32))

    tab = pl.BlockSpec((tm, HEAD_LANES), lambda i: (i, 0))
    row = pl.BlockSpec((tm, N), lambda i: (i, 0))
    ins, in_specs = [d, *tabs], [row, tab, tab, tab]
    out_shape, out_specs = [jax.ShapeDtypeStruct((T, N), BF16)], [row]
    if spread is not None:
        ins.append(spread)
        in_specs.append(pl.BlockSpec(spread.shape, lambda i: (0, 0)))
        out_shape.append(jax.ShapeDtypeStruct((T, spread.shape[0]), F32))
        out_specs.append(pl.BlockSpec((tm, spread.shape[0]), lambda i: (i, 0)))
    return pl.pallas_call(
        body, out_shape=out_shape, grid=(T // tm,), in_specs=in_specs, out_specs=out_specs,
        compiler_params=_params(("parallel",), VMEM_LIMIT), name=name)(*ins)


def _cmul(ar, ai, br, bi):
    return ar * br - ai * bi, ar * bi + ai * br


def s5_chain(finals, s0, a, n_steps, reverse, name):
    W = finals.shape[-1]
    first = N_SEG - 1 if reverse else 0

    def body(f_ref, s0_ref, a_ref, c_ref):
        pr, pi = jnp.ones((1, W), F32), jnp.zeros((1, W), F32)
        br, bi = a_ref[0], a_ref[1]
        n = n_steps
        while n:
            if n & 1:
                pr, pi = _cmul(pr, pi, br, bi)
            br, bi = _cmul(br, bi, br, bi)
            n >>= 1
        fr, fi = f_ref[0], f_ref[1]
        row = lax.broadcasted_iota(jnp.int32, (N_SEG, W), 0)
        s0r = jnp.broadcast_to(s0_ref[0], (N_SEG, W))
        s0i = jnp.broadcast_to(s0_ref[1], (N_SEG, W))
        cr = jnp.where(row == first, s0r, 0.0)
        ci = jnp.where(row == first, s0i, 0.0)
        shift = N_SEG - 1 if reverse else 1
        for _ in range(N_SEG - 1):
            mr, mi = _cmul(pr, pi, cr, ci)
            tr = pltpu.roll(fr + mr, shift, 0)
            ti = pltpu.roll(fi + mi, shift, 0)
            cr = jnp.where(row == first, s0r, tr)
            ci = jnp.where(row == first, s0i, ti)
        c_ref[0] = cr
        c_ref[1] = ci

    return pl.pallas_call(body, out_shape=jax.ShapeDtypeStruct((2, N_SEG, W), F32), name=name)(finals, s0, a)


def _scan_chunk(bur, bui, st_ref, a_ref, n_steps, reverse):
    for lc in range(S5_LANES // BLK_ST):
        sl = slice(lc * BLK_ST, (lc + 1) * BLK_ST)
        lr = jnp.broadcast_to(a_ref[0, :, sl], (N_SEG, BLK_ST))
        li = jnp.broadcast_to(a_ref[1, :, sl], (N_SEG, BLK_ST))

        def step(jj, carry, sl=sl, lr=lr, li=li):
            sr, si = carry
            j = (n_steps - 1 - jj) if reverse else jj
            r0 = pl.multiple_of(j * N_SEG, N_SEG)
            nr = lr * sr - li * si + bur[pl.ds(r0, N_SEG), sl]
            ni = lr * si + li * sr + bui[pl.ds(r0, N_SEG), sl]
            bur[pl.ds(r0, N_SEG), sl] = nr
            bui[pl.ds(r0, N_SEG), sl] = ni
            return nr, ni

        sr, si = lax.fori_loop(0, n_steps, step, (st_ref[0, :, sl], st_ref[1, :, sl]))
        st_ref[0, :, sl] = sr
        st_ref[1, :, sl] = si


def _project_in(x16, w_re, w_im, bur, bui, adjoint):
    for gb in range(N_BLOCKS):
        xb = x16[:, gb * BLK_CH:(gb + 1) * BLK_CH]
        sl = slice(gb * BLK_ST, (gb + 1) * BLK_ST)
        if adjoint:
            dn = (((1,), (1,)), ((), ()))
            bur[:, sl] = lax.dot_general(xb, w_re[gb], dn, preferred_element_type=F32)
            bui[:, sl] = -lax.dot_general(xb, w_im[gb], dn, preferred_element_type=F32)
        else:
            bur[:, sl] = jnp.dot(xb, w_re[gb], preferred_element_type=F32)
            bui[:, sl] = jnp.dot(xb, w_im[gb], preferred_element_type=F32)


def s5_scan(act, w_re, w_im, a, init, *, reverse, adjoint=False, c_re=None, c_im=None, add=None,
            want_ckpt=False, rows=None, name):
    act_off, N = rows if rows is not None else (0, act.shape[0])
    R = math.gcd(ROW_TILE, N, act_off)
    nch, jc = N // R, R // N_SEG
    with_out = c_re is not None

    def chunk(i):
        return (nch - 1 - i) if reverse else i

    def body(*refs):
        act_ref, wre_ref, wim_ref, a_ref, init_ref = refs[:5]
        k = 5
        if with_out:
            cre_ref, cim_ref = refs[k:k + 2]
            k += 2
        if add is not None:
            add_ref = refs[k]
            k += 1
        if with_out:
            out_ref = refs[k]
            k += 1
        if want_ckpt:
            ck_ref = refs[k]
            k += 1
        fin_ref, bur, bui = refs[k:k + 3]

        @pl.when(pl.program_id(0) == 0)
        def _():
            fin_ref[...] = init_ref[...]

        if want_ckpt:
            ck_ref[0] = fin_ref[...]
        _project_in(act_ref[...].astype(BF16), wre_ref, wim_ref, bur, bui, adjoint)
        _scan_chunk(bur, bui, fin_ref, a_ref, jc, reverse)
        if with_out:
            for gb in range(N_BLOCKS):
                sl = slice(gb * BLK_ST, (gb + 1) * BLK_ST)
                y = (jnp.dot(bur[:, sl].astype(BF16), cre_ref[gb], preferred_element_type=F32)
                     - jnp.dot(bui[:, sl].astype(BF16), cim_ref[gb], preferred_element_type=F32))
                cs = slice(gb * BLK_CH, (gb + 1) * BLK_CH)
                if add is not None:
                    y = y + add_ref[:, cs]
                out_ref[:, cs] = y

    row_spec = pl.BlockSpec((R, D_MODEL), lambda i: (chunk(i), 0))
    act_spec = pl.BlockSpec((R, D_MODEL), lambda i: (chunk(i) + act_off // R, 0))
    w_spec = pl.BlockSpec(w_re.shape, lambda i: (0, 0, 0))
    st_spec = pl.BlockSpec((2, N_SEG, S5_LANES), lambda i: (0, 0, 0))
    ins = [act, w_re, w_im, a, init]
    in_specs = [act_spec, w_spec, w_spec, pl.BlockSpec((2, 1, S5_LANES), lambda i: (0, 0, 0)), st_spec]
    if with_out:
        ins += [c_re, c_im]
        in_specs += [pl.BlockSpec(c_re.shape, lambda i: (0, 0, 0))] * 2
    if add is not None:
        ins.append(add)
        in_specs.append(row_spec)
    out_shape, out_specs = [], []
    if with_out:
        out_shape.append(jax.ShapeDtypeStruct((N, D_MODEL), F32))
        out_specs.append(row_spec)
    if want_ckpt:
        out_shape.append(jax.ShapeDtypeStruct((nch, 2, N_SEG, S5_LANES), F32))
        out_specs.append(pl.BlockSpec((1, 2, N_SEG, S5_LANES), lambda i: (chunk(i), 0, 0, 0)))
    out_shape.append(jax.ShapeDtypeStruct((2, N_SEG, S5_LANES), F32))
    out_specs.append(st_spec)
    res = pl.pallas_call(
        body, out_shape=out_shape, grid=(nch,), in_specs=in_specs, out_specs=out_specs,
        scratch_shapes=[pltpu.VMEM((R, S5_LANES), F32), pltpu.VMEM((R, S5_LANES), F32)],
        compiler_params=_params(("arbitrary",), VMEM_LIMIT), name=name)(*ins)
    res = list(res)
    out = res.pop(0) if with_out else None
    ckpt = res.pop(0) if want_ckpt else None
    return out, ckpt, res[0]


def s5_grads(dy, u, ckpt, b_re, b_im, c_re, c_im, lam, init_adj, *, reverse, add=None, u_off=0, name):
    N = dy.shape[0]
    R = math.gcd(ROW_TILE, N, u_off)
    nch, jc = N // R, R // N_SEG
    W = S5_LANES

    def chunk(i):
        return i if reverse else (nch - 1 - i)

    def body(*refs):
        dy_ref, u_ref, ck_ref, bre_ref, bim_ref, cre_ref, cim_ref, lam_ref, init_ref = refs[:9]
        k = 9
        if add is not None:
            add_ref = refs[k]
            k += 1
        du_ref, dlam_ref, dbre_ref, dbim_ref, dcre_ref, dcim_ref, fin_ref = refs[k:k + 7]
        sr_buf, si_buf, er_buf, ei_buf, st_buf = refs[k + 7:k + 12]

        @pl.when(pl.program_id(0) == 0)
        def _():
            fin_ref[...] = init_ref[...]
            dlam_ref[...] = jnp.zeros_like(dlam_ref)
            dbre_ref[...] = jnp.zeros_like(dbre_ref)
            dbim_ref[...] = jnp.zeros_like(dbim_ref)
            dcre_ref[...] = jnp.zeros_like(dcre_ref)
            dcim_ref[...] = jnp.zeros_like(dcim_ref)

        u16 = u_ref[...].astype(BF16)
        dy16 = dy_ref[...].astype(BF16)
        st_buf[...] = ck_ref[0]
        _project_in(u16, bre_ref, bim_ref, sr_buf, si_buf, False)
        _scan_chunk(sr_buf, si_buf, st_buf, lam_ref, jc, reverse)
        _project_in(dy16, cre_ref, cim_ref, er_buf, ei_buf, True)
        for lc in range(W // BLK_ST):
            sl = slice(lc * BLK_ST, (lc + 1) * BLK_ST)
            lr = jnp.broadcast_to(lam_ref[0, :, sl], (N_SEG, BLK_ST))
            li = jnp.broadcast_to(lam_ref[1, :, sl], (N_SEG, BLK_ST))

            def one(r0, spr, spi, carry, sl=sl, lr=lr, li=li):
                gr, gi, ar, ai = carry
                nr = er_buf[pl.ds(r0, N_SEG), sl] + lr * gr + li * gi
                ni = ei_buf[pl.ds(r0, N_SEG), sl] + lr * gi - li * gr
                er_buf[pl.ds(r0, N_SEG), sl] = nr
                ei_buf[pl.ds(r0, N_SEG), sl] = ni
                return nr, ni, ar + spr * nr + spi * ni, ai + spr * ni - spi * nr

            def step(ff, carry, sl=sl, one=one):
                f = jc - 1 - ff
                j = (jc - 1 - f) if reverse else f
                jp = (j + 1) if reverse else (j - 1)
                r0 = pl.multiple_of(j * N_SEG, N_SEG)
                p0 = pl.multiple_of(jp * N_SEG, N_SEG)
                return one(r0, sr_buf[pl.ds(p0, N_SEG), sl], si_buf[pl.ds(p0, N_SEG), sl], carry)

            carry = (fin_ref[0, :, sl], fin_ref[1, :, sl], dlam_ref[0, :, sl], dlam_ref[1, :, sl])
            carry = lax.fori_loop(0, jc - 1, step, carry)
            r_first = (jc - 1) * N_SEG if reverse else 0
            gr, gi, ar, ai = one(r_first, ck_ref[0, 0, :, sl], ck_ref[0, 1, :, sl], carry)
            fin_ref[0, :, sl] = gr
            fin_ref[1, :, sl] = gi
            dlam_ref[0, :, sl] = ar
            dlam_ref[1, :, sl] = ai
        tn = (((0,), (0,)), ((), ()))
        nt = (((1,), (1,)), ((), ()))
        for gb in range(N_BLOCKS):
            sl = slice(gb * BLK_ST, (gb + 1) * BLK_ST)
            cs = slice(gb * BLK_CH, (gb + 1) * BLK_CH)
            gr16 = er_buf[:, sl].astype(BF16)
            gi16 = ei_buf[:, sl].astype(BF16)
            du = (lax.dot_general(gr16, bre_ref[gb], nt, preferred_element_type=F32)
                  + lax.dot_general(gi16, bim_ref[gb], nt, preferred_element_type=F32))
            if add is not None:
                du = du + add_ref[:, cs]
            du_ref[:, cs] = du
            ub, dyb = u16[:, cs], dy16[:, cs]
            dbre_ref[gb] += lax.dot_general(ub, gr16, tn, preferred_element_type=F32)
            dbim_ref[gb] += lax.dot_general(ub, gi16, tn, preferred_element_type=F32)
            dcre_ref[gb] += lax.dot_general(sr_buf[:, sl].astype(BF16), dyb, tn, preferred_element_type=F32)
            dcim_ref[gb] -= lax.dot_general(si_buf[:, sl].astype(BF16), dyb, tn, preferred_element_type=F32)

    row_spec = pl.BlockSpec((R, D_MODEL), lambda i: (chunk(i), 0))
    st_spec = pl.BlockSpec((2, N_SEG, W), lambda i: (0, 0, 0))
    wb_spec = pl.BlockSpec(b_re.shape, lambda i: (0, 0, 0))
    wc_spec = pl.BlockSpec(c_re.shape, lambda i: (0, 0, 0))
    ins = [dy, u, ckpt, b_re, b_im, c_re, c_im, lam, init_adj]
    u_spec = pl.BlockSpec((R, D_MODEL), lambda i: (chunk(i) + u_off // R, 0))
    in_specs = [row_spec, u_spec, pl.BlockSpec((1, 2, N_SEG, W), lambda i: (chunk(i), 0, 0, 0)),
                wb_spec, wb_spec, wc_spec, wc_spec, pl.BlockSpec((2, 1, W), lambda i: (0, 0, 0)), st_spec]
    if add is not None:
        ins.append(add)
        in_specs.append(row_spec)
    out_shape = [jax.ShapeDtypeStruct((N, D_MODEL), F32), jax.ShapeDtypeStruct((2, N_SEG, W), F32),
                 jax.ShapeDtypeStruct(b_re.shape, F32), jax.ShapeDtypeStruct(b_re.shape, F32),
                 jax.ShapeDtypeStruct(c_re.shape, F32), jax.ShapeDtypeStruct(c_re.shape, F32),
                 jax.ShapeDtypeStruct((2, N_SEG, W), F32)]
    out_specs = [row_spec, st_spec, wb_spec, wb_spec, wc_spec, wc_spec, st_spec]
    return pl.pallas_call(
        body, out_shape=out_shape, grid=(nch,), in_specs=in_specs, out_specs=out_specs,
        scratch_shapes=[pltpu.VMEM((R, W), F32) for _ in range(4)] + [pltpu.VMEM((2, N_SEG, W), F32)],
        compiler_params=_params(("arbitrary",), VMEM_LIMIT), name=name)(*ins)


def adamw(w, g, m, v, name="adamw"):
    n, d = w.shape
    lanes = -(-d // 128) * 128
    tm = n
    while tm * lanes * 4 > (1 << 20) and tm % 16 == 0:
        tm //= 2
    c1 = 1.0 - ADAM_B1 ** ADAM_STEP
    c2 = 1.0 - ADAM_B2 ** ADAM_STEP

    def body(w_ref, g_ref, m_ref, v_ref, d_ref, nm_ref, nv_ref):
        g_ = g_ref[...]
        m_ = ADAM_B1 * m_ref[...] + (1.0 - ADAM_B1) * g_
        v_ = ADAM_B2 * v_ref[...] + (1.0 - ADAM_B2) * (g_ * g_)
        d_ref[...] = -ADAM_LR * ((m_ / c1) / (jnp.sqrt(v_ / c2) + ADAM_EPS) + ADAM_WD * w_ref[...])
        nm_ref[...] = m_
        nv_ref[...] = v_

    spec = pl.BlockSpec((tm, d), lambda i: (i, 0))
    return pl.pallas_call(
        body, out_shape=[jax.ShapeDtypeStruct((n, d), F32)] * 3, grid=(n // tm,),
        in_specs=[spec] * 4, out_specs=[spec] * 3,
        compiler_params=_params(("parallel",), VMEM_LIMIT), name=name)(w, g, m, v)


def _coords():
    return lax.axis_index("x"), lax.axis_index("y"), lax.axis_index("c")


def exchange(arrays, out_shapes, remote, local, name, aliases=None):
    n_in, n_out, n_rem, n_loc = len(arrays), len(out_shapes), len(remote), len(local)

    def at(ref, idx):
        return ref if idx is None else ref.at[idx]

    def body(*refs):
        ins, outs = refs[:n_in], refs[n_in:n_in + n_out]
        send_sems, recv_sems, local_sems = refs[n_in + n_out:]
        me = _coords()
        sends, recvs = [], []
        for k, (flip, ii, src_at, oi, dst_at) in enumerate(remote):
            peer = (me[0] ^ flip[0], me[1] ^ flip[1], me[2] ^ flip[2])
            src = at(ins[ii], src_at(me, peer))
            sends.append(pltpu.make_async_remote_copy(
                src_ref=src, dst_ref=at(outs[oi], dst_at(me)), send_sem=send_sems.at[k], recv_sem=recv_sems.at[k],
                device_id=peer, device_id_type=MESH))
            recvs.append(pltpu.make_async_remote_copy(
                src_ref=src, dst_ref=at(outs[oi], dst_at(peer)), send_sem=send_sems.at[k], recv_sem=recv_sems.at[k],
                device_id=peer, device_id_type=MESH))
        locs = [pltpu.make_async_copy(at(ins[ii], src_at(me)), at(outs[oi], dst_at(me)), local_sems.at[k])
                for k, (ii, src_at, oi, dst_at) in enumerate(local)]
        for cp in locs + sends:
            cp.start()
        for cp in recvs:
            cp.wait_recv()
        for cp in sends:
            cp.wait_send()
        for cp in locs:
            cp.wait()

    hbm = pl.BlockSpec(memory_space=pl.ANY)
    return pl.pallas_call(
        body, out_shape=list(out_shapes), in_specs=[hbm] * n_in, out_specs=[hbm] * n_out,
        scratch_shapes=[pltpu.SemaphoreType.DMA((n_rem,)), pltpu.SemaphoreType.DMA((n_rem,)),
                        pltpu.SemaphoreType.DMA((max(n_loc, 1),))],
        input_output_aliases=aliases or {}, name=name)(*arrays)


ALL_FLIPS = [(dx, dy, dc) for dx in (0, 1) for dy in (0, 1) for dc in (0, 1)][1:]
CHIP_FLIPS = [(1, 0, 0), (0, 1, 0), (1, 1, 0)]
CORE_FLIP = (0, 0, 1)


def _dev_index(p):
    return 4 * p[0] + 2 * p[1] + p[2]


def _chip_index(p):
    return 2 * p[0] + p[1]


def _gather(xs, flips, index, n, name):
    arrays = [x[None] for x in xs]
    outs = [jax.ShapeDtypeStruct((n,) + x.shape, x.dtype) for x in xs]
    remote = [(f, a, lambda me, peer: (0,), a, lambda s: (index(s),)) for a in range(len(xs)) for f in flips]
    local = [(a, lambda me: (0,), a, lambda me: (index(me),)) for a in range(len(xs))]
    return exchange(arrays, outs, remote, local, name)


def allgather_devices(x, name):
    return _gather([x], ALL_FLIPS, _dev_index, N_DEV, name)[0]


def allgather_chips(xs, name):
    return _gather(xs, CHIP_FLIPS, _chip_index, N_CHIP, name)


HBM_SPEC = pl.BlockSpec(memory_space=pltpu.HBM)
SEM_SPEC = pl.BlockSpec(memory_space=pltpu.SEMAPHORE)
DATAFLOW = pltpu.SideEffectType.DATAFLOW_SIDE_EFFECTING


def _at(ref, idx):
    return ref if idx is None else ref.at[idx]


def _peer(me, flip):
    return (me[0] ^ flip[0], me[1] ^ flip[1], me[2] ^ flip[2])


def exchange_start(arrays, land_shapes, remote, name):
    n_in, n_out, nk = len(arrays), len(land_shapes), len(remote)

    def body(*refs):
        srcs, lands = refs[:n_in], refs[n_in:n_in + n_out]
        send_sems, recv_sems, token = refs[n_in + n_out], refs[n_in + n_out + 1], refs[-1]
        me = _coords()
        for k, (flip, ii, src_at, oi, dst_at) in enumerate(remote):
            peer = _peer(me, flip)
            pltpu.make_async_remote_copy(
                src_ref=_at(srcs[ii], src_at(me, peer)), dst_ref=_at(lands[oi], dst_at(me)), send_sem=send_sems.at[k],
                recv_sem=recv_sems.at[k], device_id=peer, device_id_type=MESH).start()
        token[...] = jnp.zeros_like(token)

    lands = [lax.empty(s.shape, s.dtype) for s in land_shapes]
    bufs = list(arrays) + lands
    out = pl.pallas_call(
        body, name=name,
        out_shape=(pltpu.SemaphoreType.DMA((nk,)), pltpu.SemaphoreType.DMA((nk,)),
                   *[pltpu.HBM(b.shape, b.dtype) for b in bufs], jax.ShapeDtypeStruct((8, 128), F32)),
        in_specs=[HBM_SPEC] * len(bufs),
        out_specs=(SEM_SPEC, SEM_SPEC, *[HBM_SPEC] * len(bufs), pl.BlockSpec(memory_space=pltpu.VMEM)),
        input_output_aliases={a: 2 + a for a in range(len(bufs))},
        compiler_params=pltpu.CompilerParams(has_side_effects=DATAFLOW),
    )(*[pltpu.with_memory_space_constraint(b, pltpu.HBM) for b in bufs])
    flight = (out[0], out[1], list(out[2:2 + n_in]), list(out[2 + n_in:2 + n_in + n_out]), remote)
    return flight, out[-1]


def exchange_wait(flight, after, name):
    send_sems, recv_sems, arrays, lands, remote = flight
    n_in, n_out = len(arrays), len(lands)

    def body(*refs):
        srcs, lnds = refs[:n_in], refs[n_in:n_in + n_out]
        s_sems, r_sems = refs[n_in + n_out], refs[n_in + n_out + 1]
        me = _coords()
        for k, (flip, ii, src_at, oi, dst_at) in enumerate(remote):
            peer = _peer(me, flip)
            copy = pltpu.make_async_remote_copy(
                src_ref=_at(srcs[ii], src_at(me, peer)), dst_ref=_at(lnds[oi], dst_at(peer)), send_sem=s_sems.at[k],
                recv_sem=r_sems.at[k], device_id=peer, device_id_type=MESH)
            copy.wait_send()
            copy.wait_recv()

    bufs = list(arrays) + list(lands)
    out = pl.pallas_call(
        body, name=name,
        out_shape=tuple(pltpu.HBM(b.shape, b.dtype) for b in bufs),
        in_specs=[HBM_SPEC] * len(bufs) + [SEM_SPEC, SEM_SPEC, pl.BlockSpec(memory_space=pl.ANY)],
        out_specs=tuple([HBM_SPEC] * len(bufs)),
        input_output_aliases={a: a for a in range(len(bufs))},
        compiler_params=pltpu.CompilerParams(has_side_effects=DATAFLOW),
    )(*bufs, send_sems, recv_sems, after)
    return list(out[:n_in]), list(out[n_in:])


def _half_tile(h, cd):
    return h if h * cd * 4 <= (1 << 20) else math.gcd(512, h)


def pair_add(g, got, core, out_dtype, name):
    _, _, h, cd = g.shape
    th = _half_tile(h, cd)

    def body(c_ref, g_ref, got_ref, o_ref):
        o_ref[0] = (g_ref[0, 0] + got_ref[0]).astype(o_ref.dtype)

    return pl.pallas_call(
        body, out_shape=jax.ShapeDtypeStruct((N_CHIP, h, cd), out_dtype),
        grid_spec=pltpu.PrefetchScalarGridSpec(
            num_scalar_prefetch=1, grid=(N_CHIP, h // th),
            in_specs=[pl.BlockSpec((1, 1, th, cd), lambda q, i, c: (q, c[0], i, 0)),
                      pl.BlockSpec((1, th, cd), lambda q, i, c: (q, i, 0))],
            out_specs=pl.BlockSpec((1, th, cd), lambda q, i, c: (q, i, 0))),
        compiler_params=_params(("parallel", "parallel"), VMEM_LIMIT), name=name)(core, g, got)


def sum_chips(parts, sums, place, name):
    _, h, cd = parts.shape
    th = _half_tile(h, cd)

    def body(pc_ref, p_ref, own_ref, o_ref):
        acc = None
        for q in range(N_CHIP):
            term = jnp.where(pc_ref[1] == q, own_ref[0], p_ref[q]).astype(F32)
            acc = term if acc is None else acc + term
        o_ref[0] = acc

    return pl.pallas_call(
        body, out_shape=jax.ShapeDtypeStruct((2, h, cd), F32),
        grid_spec=pltpu.PrefetchScalarGridSpec(
            num_scalar_prefetch=1, grid=(h // th,),
            in_specs=[pl.BlockSpec((N_CHIP, th, cd), lambda i, pc: (0, i, 0)),
                      pl.BlockSpec((1, th, cd), lambda i, pc: (pc[1], i, 0))],
            out_specs=pl.BlockSpec((1, th, cd), lambda i, pc: (pc[0], i, 0))),
        compiler_params=_params(("parallel",), VMEM_LIMIT), name=name)(place, parts, sums)


def to_segments(a, n_ctx):
    def one(p):
        n = p.shape[0]
        return p.reshape(N_SEG, n // N_SEG, -1).transpose(1, 0, 2).reshape(n, -1)
    return jnp.concatenate([one(a[:n_ctx]), one(a[n_ctx:])], axis=0) if n_ctx else one(a)


def from_segments(a, n_ctx):
    def one(p):
        n = p.shape[0]
        return p.reshape(n // N_SEG, N_SEG, -1).transpose(1, 0, 2).reshape(n, -1)
    return jnp.concatenate([one(a[:n_ctx]), one(a[n_ctx:])], axis=0) if n_ctx else one(a)


def rope_tables(n_ctx, n_lat):
    f32 = np.float32
    rows = n_lat // GRID_W
    row = np.repeat(np.arange(rows), GRID_W).astype(f32)
    col = np.tile(np.arange(GRID_W), rows).astype(f32)
    d = QK_ROPE_DIM // 2
    inv = (f32(1.0) / np.power(f32(ROPE_THETA), np.arange(0, d, 2, dtype=f32) / f32(d))).astype(f32)
    ang = np.concatenate([row[:, None] * inv[None, :], col[:, None] * inv[None, :]], axis=1).astype(f32)
    cos = np.concatenate([np.ones((n_ctx, d), f32), np.cos(ang)], axis=0)
    sin = np.concatenate([np.zeros((n_ctx, d), f32), np.sin(ang)], axis=0)
    q = QK_ROPE_DIM // 4
    T = n_ctx + n_lat
    ones, zeros = np.ones((T, QK_NOPE_DIM), f32), np.zeros((T, QK_NOPE_DIM), f32)
    tail, z8 = np.zeros((T, HEAD_LANES - QK_DIM), f32), np.zeros((T, q), f32)
    cr, cc, sr, sc = cos[:, :q], cos[:, q:], sin[:, :q], sin[:, q:]
    cos_t = np.concatenate([ones, cr, cr, cc, cc, tail], axis=1)
    sin_next = np.concatenate([zeros, -sr, z8, -sc, z8, tail], axis=1)
    sin_prev = np.concatenate([zeros, z8, sr, z8, sc, tail], axis=1)
    return tuple(jnp.asarray(t, F32) for t in (cos_t, sin_next, sin_prev))


def pad_heads(w, used):
    k = w.shape[0]
    return jnp.pad(w.reshape(k, MLA_HEADS, used), ((0, 0), (0, 0), (0, HEAD_LANES - used))).reshape(k, -1)


def unpad_heads(w, used):
    k = w.shape[0]
    return w.reshape(k, MLA_HEADS, HEAD_LANES)[:, :, :used].reshape(k, MLA_HEADS * used)


def rotary_spread():
    lane = np.arange(MLA_HEADS * HEAD_LANES) % HEAD_LANES
    return jnp.asarray(lane[None, :] == (QK_NOPE_DIM + np.arange(QK_ROPE_DIM))[:, None], BF16)


def s5_discretise(a_re, a_im, log_step, b_re, b_im):
    dt = jnp.exp(log_step)[:, None]
    mag = jnp.exp(a_re * dt)
    lb_re = mag * jnp.cos(a_im * dt)
    lb_im = mag * jnp.sin(a_im * dt)
    den = a_re * a_re + a_im * a_im
    nr = lb_re - 1.0
    f_re = ((nr * a_re + lb_im * a_im) / den)[..., None]
    f_im = ((lb_im * a_re - nr * a_im) / den)[..., None]
    return lb_re, lb_im, f_re * b_re - f_im * b_im, f_re * b_im + f_im * b_re


def s5_block_weights(lb_re, lb_im, bb_re, bb_im, c_re, c_im):
    eye = jnp.eye(GROUPS_PER_BLOCK, dtype=F32)
    lam = jnp.stack([lb_re.reshape(1, S5_LANES), lb_im.reshape(1, S5_LANES)])

    def b_blocks(bb):
        t = bb.reshape(N_BLOCKS, GROUPS_PER_BLOCK, S5_STATE, S5_GROUP)
        return jnp.einsum("bgpc,gh->bgchp", t, eye).reshape(N_BLOCKS, BLK_CH, BLK_ST).astype(BF16)

    def c_blocks(cc):
        t = cc.reshape(N_BLOCKS, GROUPS_PER_BLOCK, S5_GROUP, S5_STATE)
        return jnp.einsum("bgcp,gh->bgphc", t, eye).reshape(N_BLOCKS, BLK_ST, BLK_CH).astype(BF16)

    return lam, b_blocks(bb_re), b_blocks(bb_im), c_blocks(c_re), c_blocks(c_im)


def b_block_diag(db):
    t = db.reshape(N_BLOCKS, GROUPS_PER_BLOCK, S5_GROUP, GROUPS_PER_BLOCK, S5_STATE)
    return jnp.einsum("bgchp,gh->bgpc", t, jnp.eye(GROUPS_PER_BLOCK, dtype=F32)).reshape(S5_GROUPS, S5_STATE, S5_GROUP)


def c_block_diag(dc):
    t = dc.reshape(N_BLOCKS, GROUPS_PER_BLOCK, S5_STATE, GROUPS_PER_BLOCK, S5_GROUP)
    return jnp.einsum("bgphc,gh->bgcp", t, jnp.eye(GROUPS_PER_BLOCK, dtype=F32)).reshape(S5_GROUPS, S5_GROUP, S5_STATE)


def conj(a):
    return jnp.stack([a[0], -a[1]])


PACK_TILE = 16 * 128


def pack_flat(parts, dtype):
    flat = [p.reshape(-1).astype(dtype) for p in parts]
    sizes = [f.shape[0] for f in flat]
    total = sum(sizes)
    pad = (-total) % PACK_TILE
    if pad:
        flat.append(jnp.zeros((pad,), dtype))
    offs = np.cumsum([0] + sizes)[:-1].tolist()
    return jnp.concatenate(flat).reshape(-1, 128), offs


def unpack_flat(buf, offs, shapes):
    flat = buf.reshape(-1)
    return [flat[o:o + int(np.prod(s))].reshape(s) for o, s in zip(offs, shapes)]


def s5_forward(p1, n_ctx, dirs):
    saved = []
    y = None
    ctx_rows, lat_rows = (0, n_ctx), (n_ctx, p1.shape[0] - n_ctx)
    zeros_tile = jnp.zeros((2, N_SEG, S5_LANES), F32)
    zeros_row = jnp.zeros((2, 1, S5_LANES), F32)
    for k, (lam, b_re, b_im, c_re, c_im) in enumerate(dirs):
        rev = k == 1
        last = 0 if rev else N_SEG - 1
        _, _, fin = s5_scan(p1, b_re, b_im, lam, zeros_tile, reverse=rev, rows=ctx_rows, name=f"s5_ctx_finals{k}")
        carry_c = s5_chain(fin, zeros_row, lam, n_ctx // N_SEG, rev, name=f"s5_ctx_chain{k}")
        _, ck_c, fin_c = s5_scan(p1, b_re, b_im, lam, carry_c, reverse=rev, want_ckpt=True, rows=ctx_rows,
                                 name=f"s5_ctx_scan{k}")
        s0 = fin_c[:, last:last + 1, :]
        _, _, fin = s5_scan(p1, b_re, b_im, lam, zeros_tile, reverse=rev, rows=lat_rows, name=f"s5_lat_finals{k}")
        carry_l = s5_chain(fin, s0, lam, lat_rows[1] // N_SEG, rev, name=f"s5_lat_chain{k}")
        y, ck_l, _ = s5_scan(p1, b_re, b_im, lam, carry_l, reverse=rev, c_re=c_re, c_im=c_im, add=y,
                             want_ckpt=True, rows=lat_rows, name=f"s5_lat_scan{k}")
        saved.append((ck_c, ck_l))
    return y, saved


def s5_backward(dy_l, du_extra_l, p1, n_ctx, dirs, saved):
    n_lat = p1.shape[0] - n_ctx
    zeros_tile = jnp.zeros((2, N_SEG, S5_LANES), F32)
    zeros_row = jnp.zeros((2, 1, S5_LANES), F32)
    dy_c = jnp.zeros((n_ctx, D_MODEL), F32)
    du_l, du_c = du_extra_l, None
    grads = []
    for k, (lam, b_re, b_im, c_re, c_im) in enumerate(dirs):
        rev = k == 1
        lam_c = conj(lam)
        ck_c, ck_l = saved[k]
        first = N_SEG - 1 if rev else 0
        _, _, fin = s5_scan(dy_l, c_re, c_im, lam_c, zeros_tile, reverse=not rev, adjoint=True,
                            name=f"s5_lat_adj_finals{k}")
        carry = s5_chain(fin, zeros_row, lam_c, n_lat // N_SEG, not rev, name=f"s5_lat_adj_chain{k}")
        du_l, dlam_l, dbr_l, dbi_l, dcr_l, dci_l, fin_a = s5_grads(
            dy_l, p1, ck_l, b_re, b_im, c_re, c_im, lam, carry, reverse=rev, add=du_l, u_off=n_ctx,
            name=f"s5_lat_grads{k}")
        g0 = fin_a[:, first:first + 1, :]
        carry = s5_chain(zeros_tile, g0, lam_c, n_ctx // N_SEG, not rev, name=f"s5_ctx_adj_chain{k}")
        du_c, dlam_c, dbr_c, dbi_c, _, _, _ = s5_grads(
            dy_c, p1, ck_c, b_re, b_im, c_re, c_im, lam, carry, reverse=rev, add=du_c, name=f"s5_ctx_grads{k}")
        dlam = jnp.sum(dlam_l + dlam_c, axis=1)
        grads.append((dlam, b_block_diag(dbr_l + dbr_c), b_block_diag(dbi_l + dbi_c),
                      c_block_diag(dcr_l), c_block_diag(dci_l)))
    return jnp.concatenate([du_c, du_l], axis=0), grads


def local_step(x, ctx, target, mod, w, late=None, reducer=None):
    L, Lc = x.shape[0], ctx.shape[0]
    T = L + Lc
    assert L % Lc == 0 and Lc % (2 * N_SEG) == 0 and L % GRID_W == 0
    D = D_MODEL
    X0 = jnp.concatenate([ctx, x], axis=0)
    w = dict(w)

    def arrive(group, after):
        if late is not None:
            w.update(late[group](after))

    def mod_of(i, j):
        return mod[i, :, j, :][:, None, :]

    def vec(v):
        return v.reshape(1, 1, -1).astype(F32)

    g0 = vec(w["norm_g"][0])
    (H0,) = rowwise_fwd(f_norm_mod, [X0], [g0, mod_of(0, 1), mod_of(0, 0)], [D], [BF16], T, Lc, "l0_norm")
    arrive("in", H0)
    p0 = mm_nn(H0, w["mla_w_in"], name="l0_in")
    cq = Rows(p0, Q_LORA_RANK, col_blk=D // Q_LORA_RANK)
    ckv = Rows(p0, KV_LORA_RANK, col_blk=(D + Q_LORA_RANK) // KV_LORA_RANK)
    kr = p0[:, D + Q_LORA_RANK + KV_LORA_RANK:D + P0_HEAD]
    qng, kvng = vec(w["mla_q_norm"]), vec(w["mla_kv_norm"])
    (qn,) = rowwise_fwd(f_rms, [cq], [qng], [Q_LORA_RANK], [BF16], T, 0, "l0_qnorm")
    (kvn,) = rowwise_fwd(f_rms, [ckv], [kvng], [KV_LORA_RANK], [BF16], T, 0, "l0_kvnorm")
    tabs = rope_tables(Lc, L)
    spread = rotary_spread()
    arrive("qkv", p0)
    w_uq_p = pad_heads(w["mla_w_uq"], QK_DIM)
    w_ukv3 = w["mla_w_ukv"].reshape(KV_LORA_RANK, MLA_HEADS, QK_NOPE_DIM + V_HEAD_DIM)
    w_kn_p = pad_heads(w_ukv3[:, :, :QK_NOPE_DIM].reshape(KV_LORA_RANK, -1), QK_NOPE_DIM)
    w_v = w_ukv3[:, :, QK_NOPE_DIM:].reshape(KV_LORA_RANK, -1)
    qb = q_heads(qn, w_uq_p, tabs)
    kb, vb = kv_heads(kvn, w_kn_p, w_v, kr, spread, tabs)
    o, lse, lse_rows = attn_fwd(qb, kb, vb, Lc)
    arrive("out", o)
    X1, og, out0 = mla_post_fwd(o, p0, X0, mod_of(0, 2), w["mla_w_out"], Lc)

    arrive("l1", X1)
    X1p = to_segments(X1, Lc)
    tgt_p = to_segments(target, 0)
    g1 = vec(w["norm_g"][1])
    (H1,) = rowwise_fwd(f_norm_mod, [X1p], [g1, mod_of(1, 1), mod_of(1, 0)], [D], [BF16], T, Lc, "l1_norm")
    p1 = mm_nn(H1, w["s5_w_in"], name="l1_in")
    disc_fn = lambda *a: tuple(zip(*[s5_discretise(a[0][k], a[1][k], a[2][k], a[3][k], a[4][k]) for k in range(2)]))
    disc, disc_vjp = jax.vjp(disc_fn, w["s5_a_re"], w["s5_a_im"], w["s5_log_step"], w["s5_b_re"], w["s5_b_im"])
    dirs = [s5_block_weights(disc[0][k], disc[1][k], disc[2][k], disc[3][k], w["s5_c_re"][k], w["s5_c_im"][k])
            for k in range(2)]
    y_ssm, s5_saved = s5_forward(p1, Lc, dirs)

    row = lambda v: v.reshape(1, D).astype(F32)
    (lvec, dX2, d_yssm, d_u_act, d_z1, d_fg, d_gt1, d_bg, d_d, gw_glu, gw_out) = s5_tail(
        y_ssm, p1, X1p, tgt_p, Lc, row(w["s5_d"]), row(w["s5_b_glu"]), mod[1, 1:2, 2, :], row(w["final_g"]),
        w["s5_w_glu"], w["s5_w_out"])
    loss = jnp.sum(lvec)
    gw = {"final_g": d_fg.reshape(D), "s5_b_glu": d_bg.reshape(D), "s5_d": d_d.reshape(D),
          "s5_w_glu": gw_glu, "s5_w_out": gw_out}
    dmod = {}

    du_p, s5_g = s5_backward(d_yssm, d_u_act, p1, Lc, dirs, s5_saved)
    d_disc = tuple(tuple(s5_g[k][j - 1].reshape(disc[j][k].shape) if j >= 2 else
                         s5_g[k][0][j].reshape(disc[j][k].shape) for k in range(2)) for j in range(4))
    gw["s5_a_re"], gw["s5_a_im"], gw["s5_log_step"], gw["s5_b_re"], gw["s5_b_im"] = disc_vjp(d_disc)
    gw["s5_c_re"] = jnp.stack([s5_g[0][3], s5_g[1][3]])
    gw["s5_c_im"] = jnp.stack([s5_g[0][4], s5_g[1][4]])
    d_H1 = mm_nt_sum([(du_p, 0, 0), (d_z1, D, Lc)], w["s5_w_in"], T, "l1_in_dx")
    gw["s5_w_in"] = jnp.concatenate([mm_tn(H1, du_p, name="l1_in_dw_u"), mm_tn(H1[Lc:], d_z1, name="l1_in_dw_z")],
                                    axis=1)
    if reducer is not None:
        g1 = g1 + reducer[0]({n: gw.pop(n) for n in LAYER1_MATS})[0, 0]
    d_X1p, d_g1, d_sc1, d_sh1 = rowwise_bwd(f_norm_mod, [X1p], [g1, mod_of(1, 1), mod_of(1, 0)], [d_H1],
                                            [0], [0, 1, 2], T, Lc, "l1_norm_bwd", lat_add=dX2)
    d_gt1_full = jnp.concatenate([jnp.zeros((1, 1, D), F32), d_gt1[None]], axis=0)
    dmod[1] = (d_sh1, d_sc1, d_gt1_full)
    d_X1 = from_segments(d_X1p, Lc)

    d_o, d_z0, d_gt0, gw["mla_w_out"] = mla_post_bwd(d_X1, out0, og, o, p0, mod_of(0, 2), w["mla_w_out"], Lc)
    if reducer is not None:
        tabs = (tabs[0] + reducer[1](d_o)[0, 0],) + tabs[1:]
    d_q, delta_rows = attn_bwd_dq(qb, kb, vb, o, d_o, lse, tabs, Lc)
    dk_p, d_v = attn_bwd_dkv(qb, kb, vb, d_o, lse_rows, delta_rows, Lc)
    d_k, d_kr = heads_unrope(dk_p, tabs, jnp.pad(spread, ((0, HEAD_LANES - QK_ROPE_DIM), (0, 0))),
                             name="l0_k_unrope")
    d_qn = mm_nt(d_q, w_uq_p, name="l0_uq_dx")
    gw["mla_w_uq"] = unpad_heads(mm_tn(qn, d_q, name="l0_uq_dw"), QK_DIM)
    d_kvn = mm_nt(d_k, w_kn_p, name="l0_ukn_dx") + mm_nt(d_v, w_v, name="l0_uv_dx")
    dw_kn = unpad_heads(mm_tn(kvn, d_k, name="l0_ukn_dw"), QK_NOPE_DIM).reshape(KV_LORA_RANK, MLA_HEADS, QK_NOPE_DIM)
    dw_v = mm_tn(kvn, d_v, name="l0_uv_dw").reshape(KV_LORA_RANK, MLA_HEADS, V_HEAD_DIM)
    gw["mla_w_ukv"] = jnp.concatenate([dw_kn, dw_v], axis=-1).reshape(KV_LORA_RANK, -1)
    d_cq, d_qng = rowwise_bwd(f_rms, [cq], [qng], [d_qn], [0], [0], T, 0, "l0_qnorm_bwd")
    d_ckv, d_kvng = rowwise_bwd(f_rms, [ckv], [kvng], [d_kvn], [0], [0], T, 0, "l0_kvnorm_bwd")
    gw["mla_q_norm"] = d_qng.reshape(-1)
    gw["mla_kv_norm"] = d_kvng.reshape(-1)
    o_cq, o_ckv = D, D + Q_LORA_RANK
    o_kr = o_ckv + KV_LORA_RANK
    d_H0 = mm_nt_sum([(d_z0, 0, 0), (d_cq, o_cq, 0), (d_ckv, o_ckv, 0), (d_kr, o_kr, 0)], w["mla_w_in"], T, "l0_in_dx")
    d_head = jnp.concatenate([d_cq, d_ckv, d_kr], axis=1)
    gw["mla_w_in"] = jnp.concatenate([mm_tn(H0, d_head, name="l0_in_dw_head")[:, :P0_HEAD],
                                      mm_tn(H0, d_z0, name="l0_in_dw_z")], axis=1)
    d_X0, d_g0, d_sc0, d_sh0 = rowwise_bwd(f_norm_mod, [X0], [g0, mod_of(0, 1), mod_of(0, 0)], [d_H0],
                                           [0], [0, 1, 2], T, Lc, "l0_norm_bwd", lat_add=d_X1)
    dmod[0] = (d_sh0, d_sc0, d_gt0)
    gw["norm_g"] = jnp.stack([d_g0.reshape(D), d_g1.reshape(D)])
    dx = d_X0[Lc:]
    dmod_arr = jnp.stack([jnp.stack([dmod[i][j][:, 0, :] for j in range(3)], axis=1) for i in range(2)])
    ready = reducer[2](d_X0) if reducer is not None else {}
    return loss, dx, dmod_arr, gw, ready


SHARDED = {
    "mla_w_in": 1, "mla_w_uq": 1, "mla_w_ukv": 1, "mla_w_out": 0,
    "s5_w_in": 1, "s5_w_glu": 0, "s5_w_out": 0, "s5_d": 0, "s5_b_glu": 0,
}
SHARDED_MATS = ["mla_w_in", "mla_w_uq", "mla_w_ukv", "mla_w_out", "s5_w_in", "s5_w_glu", "s5_w_out"]
SHARDED_VECS = ["s5_d", "s5_b_glu"]
REPLICATED = ["norm_g", "mla_q_norm", "mla_kv_norm", "s5_a_re", "s5_a_im", "s5_log_step", "s5_b_re", "s5_b_im",
              "s5_c_re", "s5_c_im", "final_g"]
WEIGHT_ORDER = ["c_ctx", "ada_w", "ada_b", "norm_g", "mla_w_in", "mla_q_norm", "mla_w_uq", "mla_kv_norm", "mla_w_ukv",
                "mla_w_out", "s5_w_in", "s5_a_re", "s5_a_im", "s5_log_step", "s5_b_re", "s5_b_im", "s5_c_re", "s5_c_im",
                "s5_d", "s5_w_glu", "s5_b_glu", "s5_w_out", "final_g"]


P0_HEAD = Q_LORA_RANK + KV_LORA_RANK + QK_ROPE_DIM


P0_WIDTH = 1536


def w_in_to_kernel_order(w):
    pad = jnp.zeros((w.shape[0], P0_WIDTH - w.shape[1]), w.dtype)
    return jnp.concatenate([w[:, P0_HEAD:], w[:, :P0_HEAD], pad], axis=1)


LAYER0_MATS = ["mla_w_in", "mla_w_uq", "mla_w_ukv", "mla_w_out"]
LAYER1_MATS = ["s5_w_in", "s5_w_glu", "s5_w_out"]


def _whole_matrices(names, own_blocks, gathered):
    chip = _chip_index(_coords())
    full = {}
    for n, own, o in zip(names, own_blocks, gathered):
        slot = lax.broadcasted_iota(jnp.int32, (N_CHIP, 1, 1), 0)
        o = jnp.where(slot == chip, own[None], o.reshape((N_CHIP,) + own.shape))
        full[n] = o.reshape(-1, o.shape[-1]) if SHARDED[n] == 0 else o.transpose(1, 0, 2).reshape(o.shape[1], -1)
    return full


WEIGHT_GROUPS = {
    "in": ["mla_w_in"], "qkv": ["mla_w_uq", "mla_w_ukv"], "out": ["mla_w_out"], "l1": LAYER1_MATS}


def gather_weights_behind(ws):
    token, finish = 0.0, {}
    for group, names in WEIGHT_GROUPS.items():
        mats = [ws[n].astype(BF16) for n in names]
        flight, tok = exchange_start(
            mats, [jax.ShapeDtypeStruct((N_CHIP,) + m.shape, m.dtype) for m in mats],
            [(f, a, lambda me, peer: None, a, lambda s: (_chip_index(s),))
             for a in range(len(mats)) for f in CHIP_FLIPS], f"gather_{group}_start")
        token = token + tok[0, 0]

        def finish_group(after, group=group, names=names, flight=flight):
            own, got = exchange_wait(flight, after, f"gather_{group}_wait")
            full = _whole_matrices(names, own, got)
            if "mla_w_in" in full:
                full["mla_w_in"] = w_in_to_kernel_order(full["mla_w_in"])
            return full

        finish[group] = finish_group
    return token, finish


def _grad_slots(gw, names):
    slots = []
    for n in names:
        g = gw[n]
        if SHARDED[n] == 0:
            slots.append(g.reshape(N_CHIP, 2, g.shape[0] // (2 * N_CHIP), g.shape[1]))
        else:
            k, n4 = g.shape
            slots.append(g.reshape(k, N_CHIP, n4 // N_CHIP).transpose(1, 0, 2)
                         .reshape(N_CHIP, 2, k // 2, n4 // N_CHIP))
    return slots


def _to_sibling_half(count):
    return [(CORE_FLIP, i, lambda me, peer: (slice(None), 1 - me[2]), i, lambda s: None) for i in range(count)]


def _to_chips(count):
    return [(f, i, lambda me, peer: (_chip_index(peer),), i, lambda s: (_chip_index(s),))
            for i in range(count) for f in CHIP_FLIPS]


def _place():
    me = _coords()
    return jnp.stack([me[2], _chip_index(me)]).astype(jnp.int32)


def reduce_behind(names):
    state = {}
    count = len(names)

    def begin(gw):
        slots = _grad_slots(gw, names)
        lands = [jax.ShapeDtypeStruct((N_CHIP,) + s.shape[2:], F32) for s in slots]
        state["in"], token = exchange_start(slots, lands, _to_sibling_half(count), "grads_l1_swap_in_start")
        return token

    def middle(after):
        slots, got = exchange_wait(state["in"], after, "grads_l1_swap_in_wait")
        place = _place()
        sums = [pair_add(s, g, place[:1], BF16, f"grads_pair_{n}") for n, s, g in zip(names, slots, got)]
        lands = [jax.ShapeDtypeStruct(s.shape, s.dtype) for s in sums]
        state["out"], token = exchange_start(sums, lands, _to_chips(count), "grads_l1_scatter_start")
        return token

    def end(after):
        sums, parts = exchange_wait(state["out"], after, "grads_l1_scatter_wait")
        place = _place()
        return {n: sum_chips(p, s, place, f"grads_sum_{n}") for n, p, s in zip(names, parts, sums)}

    return begin, middle, end


def reduce_gradients(gw, ready_halves):
    me = _coords()
    place = _place()
    mat_names = [n for n in SHARDED_MATS if n not in ready_halves]
    slots = dict(zip(mat_names, _grad_slots(gw, mat_names)))
    small_names = REPLICATED + SHARDED_VECS
    small, small_offs = pack_flat([gw[n].astype(F32) for n in small_names], F32)
    small = jnp.pad(small, ((0, (-small.shape[0]) % (N_CHIP * 32)), (0, 0)))
    slots["small"] = small.reshape(N_CHIP, 2, -1, 128)
    names = list(slots)
    count = len(names)
    got = exchange([slots[n] for n in names],
                   [jax.ShapeDtypeStruct((N_CHIP,) + slots[n].shape[2:], F32) for n in names],
                   _to_sibling_half(count), [], "grads_swap_in")
    sums = [pair_add(slots[n], g, place[:1], F32 if n == "small" else BF16, f"grads_pair_{n}")
            for n, g in zip(names, got)]
    scatter, token = exchange_start(sums, [jax.ShapeDtypeStruct(s.shape, s.dtype) for s in sums], _to_chips(count),
                                    "grads_scatter_start")

    def finish_mats(after):
        own_sums, parts = exchange_wait(scatter, after, "grads_scatter_wait")
        halves = {n: sum_chips(p, s, place, f"grads_sum_{n}") for n, p, s in zip(names, parts, own_sums)}
        halves.update(ready_halves)
        all_names = list(halves)
        fulls = exchange(
            [halves[n] for n in all_names], [jax.ShapeDtypeStruct(halves[n].shape, F32) for n in all_names],
            [(CORE_FLIP, i, lambda me, peer: (me[2],), i, lambda s: (s[2],)) for i in range(len(all_names))], [],
            "grads_swap_out", aliases={i: i for i in range(len(all_names))})
        out = {n: f.reshape(-1, f.shape[-1]) for n, f in zip(all_names, fulls)}
        quarter = out.pop("small")
        gather, _ = exchange_start(
            [quarter], [jax.ShapeDtypeStruct((N_CHIP,) + quarter.shape, F32)],
            [(f, 0, lambda me, peer: None, 0, lambda s: (_chip_index(s),)) for f in CHIP_FLIPS],
            "grads_gather_small_start")

        def finish_small(after_mats):
            (own,), (got_small,) = exchange_wait(gather, after_mats, "grads_gather_small_wait")
            slot = lax.broadcasted_iota(jnp.int32, (N_CHIP, 1, 1), 0)
            small_all = jnp.where(slot == _chip_index(me), own[None], got_small)
            vals = unpack_flat(small_all, small_offs, [gw[n].shape for n in small_names])
            res = {}
            for n, v in zip(small_names, vals):
                if n in SHARDED_VECS:
                    size = v.shape[0] // N_CHIP
                    v = lax.dynamic_slice_in_dim(v, _chip_index(me) * size, size)
                res[n] = v
            return res

        return out, finish_small

    return token, finish_mats


def kernel(x, c, ctx, c_ctx, ada_w, ada_b, norm_g, mla_w_in, mla_q_norm, mla_w_uq, mla_kv_norm, mla_w_ukv, mla_w_out, s5_w_in, s5_a_re, s5_a_im, s5_log_step, s5_b_re, s5_b_im, s5_c_re, s5_c_im, s5_d, s5_w_glu, s5_b_glu, s5_w_out, final_g, loss_target, m_c_ctx, m_ada_w, m_ada_b, m_norm_g, m_mla_w_in, m_mla_q_norm, m_mla_w_uq, m_mla_kv_norm, m_mla_w_ukv, m_mla_w_out, m_s5_w_in, m_s5_a_re, m_s5_a_im, m_s5_log_step, m_s5_b_re, m_s5_b_im, m_s5_c_re, m_s5_c_im, m_s5_d, m_s5_w_glu, m_s5_b_glu, m_s5_w_out, m_final_g, v_c_ctx, v_ada_w, v_ada_b, v_norm_g, v_mla_w_in, v_mla_q_norm, v_mla_w_uq, v_mla_kv_norm, v_mla_w_ukv, v_mla_w_out, v_s5_w_in, v_s5_a_re, v_s5_a_im, v_s5_log_step, v_s5_b_re, v_s5_b_im, v_s5_c_re, v_s5_c_im, v_s5_d, v_s5_w_glu, v_s5_b_glu, v_s5_w_out, v_final_g):
    args = dict(locals())
    weights = {n: args[n] for n in WEIGHT_ORDER}
    D = D_MODEL
    xi, yi, ci = _coords()
    chip = 2 * xi + yi
    me = 4 * xi + 2 * yi + ci
    n_col = ada_w.shape[2]

    token, late = gather_weights_behind({n: weights[n][0] for n in SHARDED_MATS})

    c_all = allgather_devices(jnp.pad(c + token, ((0, 7), (0, 0))), "gather_c")[:, 0, :]
    cond = jnp.concatenate([c_all, jnp.broadcast_to(c_ctx[None], (8, D))], axis=0)
    (s_cond,) = rowwise_fwd(lambda v: (_silu(v),), [cond], [], [D], [F32], 16, 0, "cond_silu")
    mod_cols = jnp.stack([mm_nn(s_cond, ada_w[i], name=f"mod_proj{i}") for i in range(2)])
    vec_tiles = [jnp.pad(weights[n][0].reshape(-1, 128), ((0, 6), (0, 0))) for n in SHARDED_VECS]
    mod_all, *vec_all = allgather_chips([mod_cols] + vec_tiles, "gather_mod")
    mod_all = mod_all.transpose(1, 2, 0, 3).reshape(2, 16, 3 * D) + ada_b[:, None, :]
    mod_l = lax.dynamic_index_in_dim(mod_all, me, axis=1, keepdims=False)
    mod_c = mod_all[:, 8, :]
    mod = jnp.stack([mod_c.reshape(2, 3, D), mod_l.reshape(2, 3, D)], axis=1)

    w = {}
    for n, v in zip(SHARDED_VECS, vec_all):
        w[n] = v[:, :2, :].reshape(-1)
    for n in ["norm_g", "final_g"]:
        w[n] = weights[n]
    for n in ["mla_q_norm", "mla_kv_norm", "s5_a_re", "s5_a_im", "s5_log_step", "s5_b_re", "s5_b_im",
              "s5_c_re", "s5_c_im"]:
        w[n] = weights[n][0]

    loss_me, dx, dmod, gw, ready = local_step(x[0], ctx[0], loss_target[0], mod, w, late, reduce_behind(LAYER1_MATS))

    reduce_token, finish_mats = reduce_gradients(gw, ready)

    dmod_rows, loss_all = _gather([dmod.reshape(2, 2, 3 * D) + reduce_token[0, 0], jnp.broadcast_to(loss_me, (8, 128))],
                                  ALL_FLIPS, _dev_index, N_DEV, "gather_dmod")
    loss = functools.reduce(lambda s, d: s + loss_all[d, 0, 0], range(1, N_DEV), loss_all[0, 0, 0])
    dm = jnp.concatenate([dmod_rows[:, :, 1, :], dmod_rows[:, :, 0, :]], axis=0).transpose(1, 0, 2)
    g_ada_b = jnp.sum(dm, axis=1)
    dm_cols = lax.dynamic_slice_in_dim(dm, chip * n_col, n_col, axis=2)
    g_ada_w = jnp.stack([mm_tn(s_cond, dm_cols[i], name=f"mod_proj_dw{i}") for i in range(2)])
    dmc = jnp.sum(dm_cols[:, 8:, :], axis=1)
    dmc8 = jnp.broadcast_to(dmc[:, None, :], (2, 8, n_col))
    g_sc = mm_nt(dmc8[0], ada_w[0], name="mod_proj_dx0")[0] + mm_nt(dmc8[1], ada_w[1], name="mod_proj_dx1")[0]
    g_sc_all = allgather_devices(jnp.broadcast_to(g_sc[None], (8, D)), "gather_dcond")[:, 0, :]
    g_silu_cc = g_sc_all[0] + g_sc_all[2] + g_sc_all[4] + g_sc_all[6]
    (g_c_ctx,) = rowwise_bwd(lambda v: (_silu(v),), [jnp.broadcast_to(c_ctx[None], (8, D))], [],
                             [jnp.broadcast_to(g_silu_cc[None], (8, D))], [0], [], 8, 0, "cond_silu_bwd")
    g_c_ctx = g_c_ctx[0]

    grads = {"c_ctx": g_c_ctx, "ada_w": g_ada_w, "ada_b": g_ada_b}
    deltas, new_m, new_v = {}, {}, {}
    small = [n for n in WEIGHT_ORDER if weights[n].size < 50000]

    def update(n):
        shp = weights[n].shape
        w2 = weights[n].reshape(-1, shp[-1])
        d_, m_, v_ = adamw(w2, grads[n].reshape(w2.shape), args["m_" + n].reshape(w2.shape),
                           args["v_" + n].reshape(w2.shape), name=f"adamw_{n}")
        deltas[n], new_m[n], new_v[n] = d_.reshape(shp), m_.reshape(shp), v_.reshape(shp)

    update("ada_w")
    red, finish_small = finish_mats(deltas["ada_w"])
    for n in SHARDED_MATS:
        grads[n] = red[n].reshape(weights[n].shape)
        update(n)
    red_small = finish_small(deltas[SHARDED_MATS[-1]])
    for n in REPLICATED + SHARDED_VECS:
        grads[n] = red_small[n].reshape(weights[n].shape)
    for n in WEIGHT_ORDER:
        if n not in small and n not in deltas:
            update(n)
    packs = []
    offs = None
    for src in (weights, grads, {n: args["m_" + n] for n in small}, {n: args["v_" + n] for n in small}):
        buf, offs = pack_flat([src[n] for n in small], F32)
        packs.append(buf)
    outs = adamw(*packs, name="adamw_small")
    for res, dst in zip(outs, (deltas, new_m, new_v)):
        for n, val in zip(small, unpack_flat(res, offs, [weights[n].shape for n in small])):
            dst[n] = val

    return (loss, dx[None], *[grads[n] for n in WEIGHT_ORDER], *[deltas[n] for n in WEIGHT_ORDER],
            *[new_m[n] for n in WEIGHT_ORDER], *[new_v[n] for n in WEIGHT_ORDER])
```

```python
import functools
import math

import jax
import jax.numpy as jnp
import numpy as np
from jax import lax
from jax.experimental import pallas as pl
from jax.experimental.pallas import tpu as pltpu

F32 = jnp.float32
BF16 = jnp.bfloat16

D_MODEL = 1024
GRID_W = 64
EPS = 1e-6
MLA_HEADS = 16
QK_NOPE_DIM = 64
QK_ROPE_DIM = 32
V_HEAD_DIM = 64
Q_LORA_RANK = 256
KV_LORA_RANK = 128
QK_DIM = QK_NOPE_DIM + QK_ROPE_DIM
SOFTMAX_SCALE = QK_DIM ** -0.5
ROPE_THETA = 10000.0
S5_GROUP = 16
S5_GROUPS = D_MODEL // S5_GROUP
S5_STATE = 64
S5_LANES = S5_GROUPS * S5_STATE
N_SEG = 8
GROUPS_PER_BLOCK = 8
N_BLOCKS = S5_GROUPS // GROUPS_PER_BLOCK
BLK_CH = GROUPS_PER_BLOCK * S5_GROUP
BLK_ST = GROUPS_PER_BLOCK * S5_STATE

ADAM_LR = 0.001
ADAM_B1 = 0.9
ADAM_B2 = 0.999
ADAM_EPS = 1e-08
ADAM_WD = 0.01
ADAM_STEP = 10

N_DEV = 8
N_CHIP = 4
MESH = pl.DeviceIdType.MESH
VMEM_LIMIT = 52 * 1024 * 1024
ROW_TILE = 256


def _params(sem=None, vmem=None):
    return pltpu.CompilerParams(dimension_semantics=sem, vmem_limit_bytes=vmem)


def mm_nn(a, b, out_dtype=F32, name="mm_nn", b_blk=0):
    M, K = a.shape
    N = b.shape[1]
    tm = math.gcd(ROW_TILE, M)

    def body(a_ref, b_ref, o_ref):
        o_ref[...] = jnp.dot(a_ref[...].astype(BF16), b_ref[...].astype(BF16),
                             preferred_element_type=F32).astype(o_ref.dtype)

    return pl.pallas_call(
        body, out_shape=jax.ShapeDtypeStruct((M, N), out_dtype), grid=(M // tm,),
        in_specs=[pl.BlockSpec((tm, K), lambda i: (i, 0)), pl.BlockSpec((K, N), lambda i: (b_blk, 0))],
        out_specs=pl.BlockSpec((tm, N), lambda i: (i, 0)),
        compiler_params=_params(("parallel",), VMEM_LIMIT), name=name)(a, b)


def mm_nt(a, b, out_dtype=F32, name="mm_nt", b_rows=None, b_blk=0):
    M, N = a.shape
    K = b.shape[0] if b_rows is None else b_rows
    tm = math.gcd(ROW_TILE, M)

    def body(a_ref, b_ref, o_ref):
        o_ref[...] = lax.dot_general(a_ref[...].astype(BF16), b_ref[...].astype(BF16),
                                     (((1,), (1,)), ((), ())),
                                     preferred_element_type=F32).astype(o_ref.dtype)

    return pl.pallas_call(
        body, out_shape=jax.ShapeDtypeStruct((M, K), out_dtype), grid=(M // tm,),
        in_specs=[pl.BlockSpec((tm, N), lambda i: (i, 0)), pl.BlockSpec((K, N), lambda i: (b_blk, 0))],
        out_specs=pl.BlockSpec((tm, K), lambda i: (i, 0)),
        compiler_params=_params(("parallel",), VMEM_LIMIT), name=name)(a, b)


def mm_nt_sum(terms, w, n_rows, name):
    K = w.shape[0]
    tm = math.gcd(ROW_TILE, n_rows, *[t[2] for t in terms])

    def body(*refs):
        i = pl.program_id(0)
        w_ref, o_ref = refs[len(terms)], refs[len(terms) + 1]
        acc = None
        for a_ref, (a, off, first) in zip(refs, terms):
            part = lax.dot_general(a_ref[...].astype(BF16), w_ref[:, off:off + a.shape[1]].astype(BF16), NT_DIMS,
                                   preferred_element_type=F32)
            if first:
                part = jnp.where(i >= first // tm, part, 0.0)
            acc = part if acc is None else acc + part
        o_ref[...] = acc

    def a_spec(a, first):
        skip = first // tm
        return pl.BlockSpec((tm, a.shape[1]), lambda i: (jnp.maximum(i - skip, 0), 0))

    return pl.pallas_call(
        body, out_shape=jax.ShapeDtypeStruct((n_rows, K), F32), grid=(n_rows // tm,),
        in_specs=[a_spec(a, first) for a, _, first in terms] + [pl.BlockSpec(w.shape, lambda i: (0, 0))],
        out_specs=pl.BlockSpec((tm, K), lambda i: (i, 0)),
        compiler_params=_params(("parallel",), VMEM_LIMIT), name=name)(*[t[0] for t in terms], w)


def mm_tn(a, b, name="mm_tn"):
    M, K = a.shape
    N = b.shape[1]
    tn = math.gcd(512, N) if N % 128 == 0 and N > 512 else N

    def body(a_ref, b_ref, o_ref):
        o_ref[...] = lax.dot_general(a_ref[...].astype(BF16), b_ref[...].astype(BF16),
                                     (((0,), (0,)), ((), ())), preferred_element_type=F32)

    return pl.pallas_call(
        body, out_shape=jax.ShapeDtypeStruct((K, N), F32), grid=(N // tn,),
        in_specs=[pl.BlockSpec((M, K), lambda j: (0, 0)), pl.BlockSpec((M, tn), lambda j: (0, j))],
        out_specs=pl.BlockSpec((K, tn), lambda j: (0, j)),
        compiler_params=_params(("parallel",), VMEM_LIMIT), name=name)(a, b)


class Rows:
    def __init__(self, arr, width=None, row_off=0, col_blk=0):
        self.arr = arr
        self.width = arr.shape[1] if width is None else width
        self.row_off = row_off
        self.col_blk = col_blk

    def spec(self, tm):
        ro, cb = self.row_off // tm, self.col_blk
        return pl.BlockSpec((tm, self.width), lambda i: (i + ro, cb))


def _as_rows(x):
    return x if isinstance(x, Rows) else Rows(x)


def _row_tile(n_rows, n_ctx_rows, rows):
    tm = math.gcd(ROW_TILE, n_rows, n_ctx_rows)
    for r in rows:
        tm = math.gcd(tm, r.row_off)
    return tm


def _bc_spec(arr, n_ctx_blocks):
    g, _, d = arr.shape
    if g == 1:
        return pl.BlockSpec((1, 1, d), lambda i: (0, 0, 0))
    return pl.BlockSpec((1, 1, d), lambda i: ((i >= n_ctx_blocks).astype(jnp.int32), 0, 0))


def rowwise_fwd(fn, rows, bcs, out_dims, out_dtypes, n_rows, n_ctx_rows, name):
    rows = [_as_rows(r) for r in rows]
    tm = _row_tile(n_rows, n_ctx_rows, rows)
    ncb = n_ctx_rows // tm
    nr, nb = len(rows), len(bcs)

    def body(*refs):
        vals = [r[...].astype(F32) for r in refs[:nr]] + [b[0].astype(F32) for b in refs[nr:nr + nb]]
        outs = fn(*vals)
        for o_ref, v in zip(refs[nr + nb:], outs):
            o_ref[...] = v.astype(o_ref.dtype)

    outs = pl.pallas_call(
        body,
        out_shape=[jax.ShapeDtypeStruct((n_rows, d), dt) for d, dt in zip(out_dims, out_dtypes)],
        grid=(n_rows // tm,),
        in_specs=[r.spec(tm) for r in rows] + [_bc_spec(b, ncb) for b in bcs],
        out_specs=[pl.BlockSpec((tm, d), lambda i: (i, 0)) for d in out_dims],
        compiler_params=_params(("parallel",), VMEM_LIMIT), name=name)(*[r.arr for r in rows], *bcs)
    return outs


def rowwise_bwd(fn, rows, bcs, cts, diff_rows, diff_bcs, n_rows, n_ctx_rows, name, ct_extra=None, lat_add=None):
    rows = [_as_rows(r) for r in rows]
    cts = [_as_rows(c) for c in cts]
    extra = [_as_rows(ct_extra)] if ct_extra is not None else []
    tm = _row_tile(n_rows, n_ctx_rows, rows + cts + extra)
    ncb = n_ctx_rows // tm
    nr, nb, nc = len(rows), len(bcs), len(cts)
    ndr, ndb = len(diff_rows), len(diff_bcs)
    n_in = nr + nb + nc + len(extra) + (lat_add is not None)

    def body(*refs):
        i = pl.program_id(0)
        rvals = [r[...].astype(F32) for r in refs[:nr]]
        bvals = [b[0].astype(F32) for b in refs[nr:nr + nb]]
        cvals = [c[...].astype(F32) for c in refs[nr + nb:nr + nb + nc]]
        if extra:
            cvals[0] = cvals[0] + refs[nr + nb + nc][...].astype(F32)
        outs = refs[n_in:]

        def f(*d):
            rv, bv = list(rvals), list(bvals)
            for k, idx in enumerate(diff_rows):
                rv[idx] = d[k]
            for k, idx in enumerate(diff_bcs):
                bv[idx] = d[ndr + k]
            return tuple(fn(*rv, *bv))

        primals = [rvals[k] for k in diff_rows] + [bvals[k] for k in diff_bcs]
        _, vjp = jax.vjp(f, *primals)
        grads = list(vjp(tuple(cvals)))
        if lat_add is not None:
            add = refs[n_in - 1][...]
            grads[0] = grads[0] + (add if lat_add.shape[0] == n_rows else jnp.where(i >= ncb, add, 0.0))
        for k in range(ndr):
            outs[k][...] = grads[k].astype(outs[k].dtype)
        for k, idx in enumerate(diff_bcs):
            o_ref = outs[ndr + k]
            first = (i == 0)
            if bcs[idx].shape[0] == 2:
                first = first | (i == ncb)

            @pl.when(first)
            def _(o_ref=o_ref):
                o_ref[...] = jnp.zeros_like(o_ref)

            o_ref[0] += grads[ndr + k]

    out_shape = [jax.ShapeDtypeStruct((n_rows, rows[k].width), F32) for k in diff_rows]
    out_shape += [jax.ShapeDtypeStruct(bcs[k].shape, F32) for k in diff_bcs]
    out_specs = [pl.BlockSpec((tm, rows[k].width), lambda i: (i, 0)) for k in diff_rows]
    out_specs += [_bc_spec(bcs[k], ncb) for k in diff_bcs]
    ins = [r.arr for r in rows] + list(bcs) + [c.arr for c in cts + extra]
    in_specs = [r.spec(tm) for r in rows] + [_bc_spec(b, ncb) for b in bcs] + [c.spec(tm) for c in cts + extra]
    if lat_add is not None:
        ins.append(lat_add)
        skip = ncb if lat_add.shape[0] != n_rows else 0
        in_specs.append(pl.BlockSpec((tm, lat_add.shape[1]), lambda i: (jnp.maximum(i - skip, 0), 0)))
    outs = pl.pallas_call(
        body, out_shape=out_shape, grid=(n_rows // tm,), in_specs=in_specs, out_specs=out_specs,
        compiler_params=_params(("arbitrary",), VMEM_LIMIT), name=name)(*ins)
    return outs


def _rms(x):
    return x * lax.rsqrt(jnp.mean(x * x, axis=-1, keepdims=True) + EPS)


def _sigmoid(x):
    return 0.5 * (jnp.tanh(0.5 * x) + 1.0)


def _silu(x):
    return x * _sigmoid(x)


def _gelu_tanh(x):
    return 0.5 * x * (1.0 + jnp.tanh(math.sqrt(2.0 / math.pi) * (x + 0.044715 * (x * x * x))))


def f_norm_mod(x, g, sc, sh):
    return ((_rms(x) * g) * (1.0 + sc) + sh,)


def f_rms(x, g):
    return (_rms(x) * g,)


def f_gate(o, z):
    return (o * _silu(z),)


def f_s5_act(y, u, d):
    return (_gelu_tanh(y + d * u),)


def f_s5_glu(ya, gl, z, b):
    return (ya * _sigmoid(gl + b) * _silu(z),)


def mla_post_fwd(o, p0, x0, gate, w_out, n_ctx, name="l0_post"):
    n, d = o.shape
    tm = math.gcd(ROW_TILE, n, n_ctx)
    ncb = n_ctx // tm

    def body(o_ref, z_ref, x_ref, gt_ref, w_ref, x1_ref, og_ref, out_ref):
        og = f_gate(o_ref[...], z_ref[...])[0].astype(BF16)
        out = jnp.dot(og, w_ref[...], preferred_element_type=F32)
        og_ref[...] = og
        out_ref[...] = out
        x1_ref[...] = x_ref[...] + gt_ref[0] * out

    row = pl.BlockSpec((tm, d), lambda i: (i, 0))
    return pl.pallas_call(
        body, out_shape=[jax.ShapeDtypeStruct((n, d), F32), jax.ShapeDtypeStruct((n, d), BF16),
                         jax.ShapeDtypeStruct((n, d), F32)],
        grid=(n // tm,),
        in_specs=[row, row, row, _bc_spec(gate, ncb), pl.BlockSpec((d, d), lambda i: (0, 0))],
        out_specs=[row, row, row],
        compiler_params=_params(("parallel",), VMEM_LIMIT), name=name)(o, p0, x0, gate, w_out)


def mla_post_bwd(dx1, out, og, o, p0, gate, w_out, n_ctx, name="l0_post_bwd"):
    n, d = o.shape
    tm = math.gcd(ROW_TILE, n, n_ctx)
    ncb = n_ctx // tm

    def body(dx_ref, out_ref, og_ref, o_ref, z_ref, gt_ref, w_ref, do_ref, dz_ref, dgt_ref, dw_ref):
        i = pl.program_id(0)

        @pl.when(i == 0)
        def _():
            dw_ref[...] = jnp.zeros_like(dw_ref)

        @pl.when((i == 0) | (i == ncb))
        def _():
            dgt_ref[...] = jnp.zeros_like(dgt_ref)

        dx = dx_ref[...]
        dgt_ref[0] += jnp.sum(dx * out_ref[...], axis=0, keepdims=True)
        d_out16 = (gt_ref[0] * dx).astype(BF16)
        dw_ref[...] += lax.dot_general(og_ref[...], d_out16, (((0,), (0,)), ((), ())), preferred_element_type=F32)
        d_og = lax.dot_general(d_out16, w_ref[...], NT_DIMS, preferred_element_type=F32)
        _, gate_vjp = jax.vjp(lambda o_, z_: f_gate(o_, z_), o_ref[...], z_ref[...])
        d_o, d_z = gate_vjp((d_og,))
        do_ref[...] = d_o
        dz_ref[...] = d_z

    row = pl.BlockSpec((tm, d), lambda i: (i, 0))
    mat = pl.BlockSpec((d, d), lambda i: (0, 0))
    return pl.pallas_call(
        body, out_shape=[jax.ShapeDtypeStruct((n, d), F32), jax.ShapeDtypeStruct((n, d), F32),
                         jax.ShapeDtypeStruct(gate.shape, F32), jax.ShapeDtypeStruct((d, d), F32)],
        grid=(n // tm,),
        in_specs=[row, row, row, row, row, _bc_spec(gate, ncb), mat],
        out_specs=[row, row, _bc_spec(gate, ncb), mat],
        compiler_params=_params(("arbitrary",), VMEM_LIMIT), name=name)(dx1, out, og, o, p0, gate, w_out)


def s5_tail(y_ssm, p1, x1p, target, n_ctx, d_vec, b_glu, gate, final_g, w_glu, w_out, name="l1_tail"):
    n, d = y_ssm.shape
    tm = math.gcd(ROW_TILE, n, n_ctx)
    off = n_ctx // tm
    tn_dims = (((0,), (0,)), ((), ()))

    def row_loss(x, g, t):
        e = _rms(x) * g - t
        return 0.5 * (e * e) * (1.0 / d)

    def body(y_ref, u_ref, z_ref, x1_ref, t_ref, d_ref, b_ref, gt_ref, fg_ref, wg_ref, wo_ref,
             l_ref, dx_ref, dy_ref, du_ref, dz_ref, dfg_ref, dgt_ref, db_ref, dd_ref, dwg_ref, dwo_ref):
        @pl.when(pl.program_id(0) == 0)
        def _():
            for r in (l_ref, dfg_ref, dgt_ref, db_ref, dd_ref, dwg_ref, dwo_ref):
                r[...] = jnp.zeros_like(r)

        u, z, tgt, gt = u_ref[...], z_ref[...], t_ref[...], gt_ref[...]
        (ya,), act_vjp = jax.vjp(lambda y_, u_, d_: f_s5_act(y_, u_, d_), y_ref[...], u, d_ref[...])
        ya16 = ya.astype(BF16)
        gl = jnp.dot(ya16, wg_ref[...], preferred_element_type=F32)
        (y3,), glu_vjp = jax.vjp(lambda a_, g_, z_, b_: f_s5_glu(a_, g_, z_, b_), ya, gl, z, b_ref[...])
        y3_16 = y3.astype(BF16)
        out1 = jnp.dot(y3_16, wo_ref[...], preferred_element_type=F32)
        lterm, loss_vjp = jax.vjp(lambda x_, g_: row_loss(x_, g_, tgt), x1_ref[...] + gt * out1, fg_ref[...])
        dx2, dfg = loss_vjp(jnp.ones_like(lterm))
        l_ref[...] += jnp.sum(lterm, axis=0, keepdims=True)
        dfg_ref[...] += dfg
        dx_ref[...] = dx2
        dgt_ref[...] += jnp.sum(dx2 * out1, axis=0, keepdims=True)
        d_out16 = (gt * dx2).astype(BF16)
        dwo_ref[...] += lax.dot_general(y3_16, d_out16, tn_dims, preferred_element_type=F32)
        d_y3 = lax.dot_general(d_out16, wo_ref[...], NT_DIMS, preferred_element_type=F32)
        d_ya, d_gl, d_z, d_b = glu_vjp((d_y3,))
        dz_ref[...] = d_z
        db_ref[...] += d_b
        d_gl16 = d_gl.astype(BF16)
        dwg_ref[...] += lax.dot_general(ya16, d_gl16, tn_dims, preferred_element_type=F32)
        d_ya = d_ya + lax.dot_general(d_gl16, wg_ref[...], NT_DIMS, preferred_element_type=F32)
        d_y, d_u, d_d = act_vjp((d_ya,))
        dy_ref[...] = d_y
        du_ref[...] = d_u
        dd_ref[...] += d_d

    row = pl.BlockSpec((tm, d), lambda i: (i, 0))
    vecs = pl.BlockSpec((1, d), lambda i: (0, 0))
    mat = pl.BlockSpec((d, d), lambda i: (0, 0))
    return pl.pallas_call(
        body,
        out_shape=[jax.ShapeDtypeStruct((1, d), F32)] + [jax.ShapeDtypeStruct((n, d), F32)] * 4
        + [jax.ShapeDtypeStruct((1, d), F32)] * 4 + [jax.ShapeDtypeStruct((d, d), F32)] * 2,
        grid=(n // tm,),
        in_specs=[row, pl.BlockSpec((tm, d), lambda i: (i + off, 0)), pl.BlockSpec((tm, d), lambda i: (i + off, 1)),
                  pl.BlockSpec((tm, d), lambda i: (i + off, 0)), row, vecs, vecs, vecs, vecs, mat, mat],
        out_specs=[vecs, row, row, row, row, vecs, vecs, vecs, vecs, mat, mat],
        compiler_params=_params(("arbitrary",), VMEM_LIMIT), name=name)(
            y_ssm, p1, p1, x1p, target, d_vec, b_glu, gate, final_g, w_glu, w_out)


NT_DIMS = (((1,), (1,)), ((), ()))
HEAD_LANES = 128
N_PAIRS = MLA_HEADS // 2


def _own_lanes(shape, hh):
    lane = lax.broadcasted_iota(jnp.int32, shape, len(shape) - 1)
    return (lane < V_HEAD_DIM) if hh == 0 else (lane >= V_HEAD_DIM)


def _rope_tiles(x, cos, sin_next, sin_prev, inverse):
    width = x.shape[-1]
    reps = width // HEAD_LANES
    c, sn, sp = (jnp.tile(t, (1, reps)) for t in (cos, sin_next, sin_prev))
    if inverse:
        return x * c + pltpu.roll(x * sn, 8, 1) + pltpu.roll(x * sp, width - 8, 1)
    return x * c + pltpu.roll(x, width - 8, 1) * sn + pltpu.roll(x, 8, 1) * sp


def _col_to_row(col):
    n = col.shape[0]
    hi = col.astype(BF16)
    r1 = col - hi.astype(F32)
    mid = r1.astype(BF16)
    lo = (r1 - mid.astype(F32)).astype(BF16)
    lane = lax.broadcasted_iota(jnp.int32, (n, HEAD_LANES), 1)
    terms = jnp.where(lane == 0, hi, jnp.where(lane == 1, mid, jnp.where(lane == 2, lo, jnp.zeros_like(hi))))
    eye = (lax.broadcasted_iota(jnp.int32, (n, n), 0) == lax.broadcasted_iota(jnp.int32, (n, n), 1)).astype(BF16)
    rows = lax.dot_general(terms, eye, (((0,), (0,)), ((), ())), preferred_element_type=F32)
    return rows[0:1] + rows[1:2] + rows[2:3]


def attn_fwd(qb, kb, vb, n_ctx):
    T = qb.shape[0]
    tq = math.gcd(ROW_TILE, n_ctx)
    nq, ncb = T // tq, n_ctx // tq

    def body(q_ref, k_ref, v_ref, o_ref, lse_ref, lse_row_ref):
        qi = pl.program_id(1)

        def rows(n_keys):
            v = v_ref[:n_keys, :]
            outs = []
            for hh in range(2):
                hs = slice(hh * HEAD_LANES, (hh + 1) * HEAD_LANES)
                s = lax.dot_general(q_ref[:, hs], k_ref[:n_keys, hs], NT_DIMS,
                                    preferred_element_type=F32) * SOFTMAX_SCALE
                m = jnp.max(s, axis=-1, keepdims=True)
                p = jnp.exp(s - m)
                l = jnp.sum(p, axis=-1, keepdims=True)
                outs.append(jnp.dot(p.astype(BF16), v, preferred_element_type=F32) / l)
                lse = m + jnp.log(l)
                lse_ref[hh] = lse
                lse_row_ref[hh] = _col_to_row(lse)
            o_ref[...] = jnp.where(_own_lanes(outs[0].shape, 0), outs[0], outs[1])

        pl.when(qi < ncb)(lambda: rows(n_ctx))
        pl.when(qi >= ncb)(lambda: rows(T))

    return pl.pallas_call(
        body,
        out_shape=[jax.ShapeDtypeStruct((T, MLA_HEADS * V_HEAD_DIM), F32),
                   jax.ShapeDtypeStruct((MLA_HEADS, T, 1), F32), jax.ShapeDtypeStruct((MLA_HEADS, 1, T), F32)],
        grid=(N_PAIRS, nq),
        in_specs=[pl.BlockSpec((tq, 2 * HEAD_LANES), lambda h, i: (i, h)),
                  pl.BlockSpec((T, 2 * HEAD_LANES), lambda h, i: (0, h)),
                  pl.BlockSpec((T, 2 * V_HEAD_DIM), lambda h, i: (0, h))],
        out_specs=[pl.BlockSpec((tq, 2 * V_HEAD_DIM), lambda h, i: (i, h)),
                   pl.BlockSpec((2, tq, 1), lambda h, i: (h, i, 0)),
                   pl.BlockSpec((2, 1, tq), lambda h, i: (h, 0, i))],
        compiler_params=_params(("parallel", "parallel"), VMEM_LIMIT), name="attn_fwd")(qb, kb, vb)


def attn_bwd_dq(qb, kb, vb, o, do, lse, tabs, n_ctx):
    T = qb.shape[0]
    tq = math.gcd(ROW_TILE, n_ctx)
    nq, ncb = T // tq, n_ctx // tq

    def body(q_ref, k_ref, v_ref, o_ref, do_ref, lse_ref, c_ref, sn_ref, sp_ref, dq_ref, delta_ref):
        qi = pl.program_id(1)

        def rows(n_keys):
            v = v_ref[:n_keys, :]
            dqs = []
            for hh in range(2):
                hs = slice(hh * HEAD_LANES, (hh + 1) * HEAD_LANES)
                k = k_ref[:n_keys, hs]
                do = jnp.where(_own_lanes(do_ref.shape, hh), do_ref[...], 0.0)
                delta = jnp.sum(do * o_ref[...], axis=-1, keepdims=True)
                s = lax.dot_general(q_ref[:, hs], k, NT_DIMS, preferred_element_type=F32) * SOFTMAX_SCALE
                p = jnp.exp(s - lse_ref[hh])
                dp = lax.dot_general(do.astype(BF16), v, NT_DIMS, preferred_element_type=F32)
                ds = p * (dp - delta) * SOFTMAX_SCALE
                dqs.append(jnp.dot(ds.astype(BF16), k, preferred_element_type=F32))
                delta_ref[hh] = _col_to_row(delta)
            dq = jnp.concatenate(dqs, axis=1)
            dq_ref[...] = _rope_tiles(dq, c_ref[...], sn_ref[...], sp_ref[...], True).astype(BF16)

        pl.when(qi < ncb)(lambda: rows(n_ctx))
        pl.when(qi >= ncb)(lambda: rows(T))

    tab = pl.BlockSpec((tq, HEAD_LANES), lambda h, i: (i, 0))
    return pl.pallas_call(
        body,
        out_shape=[jax.ShapeDtypeStruct((T, MLA_HEADS * HEAD_LANES), BF16),
                   jax.ShapeDtypeStruct((MLA_HEADS, 1, T), F32)],
        grid=(N_PAIRS, nq),
        in_specs=[pl.BlockSpec((tq, 2 * HEAD_LANES), lambda h, i: (i, h)),
                  pl.BlockSpec((T, 2 * HEAD_LANES), lambda h, i: (0, h)),
                  pl.BlockSpec((T, 2 * V_HEAD_DIM), lambda h, i: (0, h)),
                  pl.BlockSpec((tq, 2 * V_HEAD_DIM), lambda h, i: (i, h)),
                  pl.BlockSpec((tq, 2 * V_HEAD_DIM), lambda h, i: (i, h)),
                  pl.BlockSpec((2, tq, 1), lambda h, i: (h, i, 0)), tab, tab, tab],
        out_specs=[pl.BlockSpec((tq, 2 * HEAD_LANES), lambda h, i: (i, h)),
                   pl.BlockSpec((2, 1, tq), lambda h, i: (h, 0, i))],
        compiler_params=_params(("parallel", "parallel"), VMEM_LIMIT), name="attn_bwd_dq")(
            qb, kb, vb, o, do, lse, *tabs)


def attn_bwd_dkv(qb, kb, vb, do, lse_rows, delta_rows, n_ctx):
    T = qb.shape[0]
    tq = math.gcd(ROW_TILE, n_ctx)
    nq, ncb = T // tq, n_ctx // tq

    def body(q_ref, do_ref, lse_ref, delta_ref, k_ref, v_ref, dk_ref, dv_ref):
        kj = pl.program_id(1)

        def cols(first):
            v = v_ref[...]
            do_all = do_ref[first:, :]
            dv = None
            for hh in range(2):
                hs = slice(hh * HEAD_LANES, (hh + 1) * HEAD_LANES)
                k = k_ref[:, hs]
                q = q_ref[first:, hs]
                do16 = jnp.where(_own_lanes(do_all.shape, hh), do_all, 0.0).astype(BF16)
                st = lax.dot_general(k, q, NT_DIMS, preferred_element_type=F32) * SOFTMAX_SCALE
                pt = jnp.exp(st - lse_ref[hh, :, first:])
                dv_h = jnp.dot(pt.astype(BF16), do16, preferred_element_type=F32)
                dv = dv_h if dv is None else dv + dv_h
                dpt = lax.dot_general(v, do16, NT_DIMS, preferred_element_type=F32)
                dst = pt * (dpt - delta_ref[hh, :, first:]) * SOFTMAX_SCALE
                dk_ref[:, hs] = jnp.dot(dst.astype(BF16), q, preferred_element_type=F32)
            dv_ref[...] = dv

        pl.when(kj < ncb)(lambda: cols(0))
        pl.when(kj >= ncb)(lambda: cols(n_ctx))

    return pl.pallas_call(
        body,
        out_shape=[jax.ShapeDtypeStruct((T, MLA_HEADS * HEAD_LANES), F32),
                   jax.ShapeDtypeStruct((T, MLA_HEADS * V_HEAD_DIM), F32)],
        grid=(N_PAIRS, nq),
        in_specs=[pl.BlockSpec((T, 2 * HEAD_LANES), lambda h, j: (0, h)),
                  pl.BlockSpec((T, 2 * V_HEAD_DIM), lambda h, j: (0, h)),
                  pl.BlockSpec((2, 1, T), lambda h, j: (h, 0, 0)),
                  pl.BlockSpec((2, 1, T), lambda h, j: (h, 0, 0)),
                  pl.BlockSpec((tq, 2 * HEAD_LANES), lambda h, j: (j, h)),
                  pl.BlockSpec((tq, 2 * V_HEAD_DIM), lambda h, j: (j, h))],
        out_specs=[pl.BlockSpec((tq, 2 * HEAD_LANES), lambda h, j: (j, h)),
                   pl.BlockSpec((tq, 2 * V_HEAD_DIM), lambda h, j: (j, h))],
        compiler_params=_params(("parallel", "parallel"), VMEM_LIMIT), name="attn_bwd_dkv")(
            qb, do, lse_rows, delta_rows, kb, vb)


def _split_bf16(x):
    hi = x.astype(BF16)
    return hi, (x - hi.astype(F32)).astype(BF16)


def q_heads(qn, w_uq_p, tabs, name="l0_uq"):
    T, K = qn.shape
    N = w_uq_p.shape[1]
    tm = math.gcd(ROW_TILE, T)

    def body(a_ref, w_ref, c_ref, sn_ref, sp_ref, o_ref):
        acc = jnp.dot(a_ref[...], w_ref[...], preferred_element_type=F32)
        o_ref[...] = _rope_tiles(acc, c_ref[...], sn_ref[...], sp_ref[...], False).astype(BF16)

    tab = pl.BlockSpec((tm, HEAD_LANES), lambda i: (i, 0))
    return pl.pallas_call(
        body, out_shape=jax.ShapeDtypeStruct((T, N), BF16), grid=(T // tm,),
        in_specs=[pl.BlockSpec((tm, K), lambda i: (i, 0)), pl.BlockSpec((K, N), lambda i: (0, 0)), tab, tab, tab],
        out_specs=pl.BlockSpec((tm, N), lambda i: (i, 0)),
        compiler_params=_params(("parallel",), VMEM_LIMIT), name=name)(qn, w_uq_p, *tabs)


def kv_heads(kvn, w_kn_p, w_v, kr, spread, tabs, name="l0_ukv"):
    T, K = kvn.shape
    N = w_kn_p.shape[1]
    NV = w_v.shape[1]
    tm = math.gcd(ROW_TILE, T)

    def body(a_ref, wk_ref, wv_ref, kr_ref, e_ref, c_ref, sn_ref, sp_ref, k_ref, v_ref):
        a = a_ref[...]
        hi, lo = _split_bf16(kr_ref[...])
        acc = (jnp.dot(a, wk_ref[...], preferred_element_type=F32)
               + jnp.dot(hi, e_ref[...], preferred_element_type=F32)
               + jnp.dot(lo, e_ref[...], preferred_element_type=F32))
        k_ref[...] = _rope_tiles(acc, c_ref[...], sn_ref[...], sp_ref[...], False).astype(BF16)
        v_ref[...] = jnp.dot(a, wv_ref[...], preferred_element_type=F32).astype(BF16)

    tab = pl.BlockSpec((tm, HEAD_LANES), lambda i: (i, 0))
    return pl.pallas_call(
        body, out_shape=[jax.ShapeDtypeStruct((T, N), BF16), jax.ShapeDtypeStruct((T, NV), BF16)], grid=(T // tm,),
        in_specs=[pl.BlockSpec((tm, K), lambda i: (i, 0)), pl.BlockSpec((K, N), lambda i: (0, 0)),
                  pl.BlockSpec((K, NV), lambda i: (0, 0)), pl.BlockSpec((tm, QK_ROPE_DIM), lambda i: (i, 0)),
                  pl.BlockSpec((QK_ROPE_DIM, N), lambda i: (0, 0)), tab, tab, tab],
        out_specs=[pl.BlockSpec((tm, N), lambda i: (i, 0)), pl.BlockSpec((tm, NV), lambda i: (i, 0))],
        compiler_params=_params(("parallel",), VMEM_LIMIT), name=name)(kvn, w_kn_p, w_v, kr, spread, *tabs)


def heads_unrope(d, tabs, spread=None, name="unrope"):
    T, N = d.shape
    tm = math.gcd(ROW_TILE, T)

    def body(*refs):
        if spread is None:
            d_ref, c_ref, sn_ref, sp_ref, o_ref = refs
        else:
            d_ref, c_ref, sn_ref, sp_ref, e_ref, o_ref, kr_ref = refs
        g = _rope_tiles(d_ref[...], c_ref[...], sn_ref[...], sp_ref[...], True)
        o_ref[...] = g.astype(BF16)
        if spread is not None:
            hi, lo = _split_bf16(g)
            kr_ref[...] = (lax.dot_general(hi, e_ref[...], NT_DIMS, preferred_element_type=F32)
                           + lax.dot_general(lo, e_ref[...], NT_DIMS, preferred_element_type=F32))

    tab = pl.BlockSpec((tm, HEAD_LANES), lambda i: (i, 0))
    row = pl.BlockSpec((tm, N), lambda i: (i, 0))
    ins, in_specs = [d, *tabs], [row, tab, tab, tab]
    out_shape, out_specs = [jax.ShapeDtypeStruct((T, N), BF16)], [row]
    if spread is not None:
        ins.append(spread)
        in_specs.append(pl.BlockSpec(spread.shape, lambda i: (0, 0)))
        out_shape.append(jax.ShapeDtypeStruct((T, spread.shape[0]), F32))
        out_specs.append(pl.BlockSpec((tm, spread.shape[0]), lambda i: (i, 0)))
    return pl.pallas_call(
        body, out_shape=out_shape, grid=(T // tm,), in_specs=in_specs, out_specs=out_specs,
        compiler_params=_params(("parallel",), VMEM_LIMIT), name=name)(*ins)


def _cmul(ar, ai, br, bi):
    return ar * br - ai * bi, ar * bi + ai * br


def s5_chain(finals, s0, a, n_steps, reverse, name):
    W = finals.shape[-1]
    first = N_SEG - 1 if reverse else 0

    def body(f_ref, s0_ref, a_ref, c_ref):
        pr, pi = jnp.ones((1, W), F32), jnp.zeros((1, W), F32)
        br, bi = a_ref[0], a_ref[1]
        n = n_steps
        while n:
            if n & 1:
                pr, pi = _cmul(pr, pi, br, bi)
            br, bi = _cmul(br, bi, br, bi)
            n >>= 1
        fr, fi = f_ref[0], f_ref[1]
        row = lax.broadcasted_iota(jnp.int32, (N_SEG, W), 0)
        s0r = jnp.broadcast_to(s0_ref[0], (N_SEG, W))
        s0i = jnp.broadcast_to(s0_ref[1], (N_SEG, W))
        cr = jnp.where(row == first, s0r, 0.0)
        ci = jnp.where(row == first, s0i, 0.0)
        shift = N_SEG - 1 if reverse else 1
        for _ in range(N_SEG - 1):
            mr, mi = _cmul(pr, pi, cr, ci)
            tr = pltpu.roll(fr + mr, shift, 0)
            ti = pltpu.roll(fi + mi, shift, 0)
            cr = jnp.where(row == first, s0r, tr)
            ci = jnp.where(row == first, s0i, ti)
        c_ref[0] = cr
        c_ref[1] = ci

    return pl.pallas_call(body, out_shape=jax.ShapeDtypeStruct((2, N_SEG, W), F32), name=name)(finals, s0, a)


def _scan_chunk(bur, bui, st_ref, a_ref, n_steps, reverse):
    for lc in range(S5_LANES // BLK_ST):
        sl = slice(lc * BLK_ST, (lc + 1) * BLK_ST)
        lr = jnp.broadcast_to(a_ref[0, :, sl], (N_SEG, BLK_ST))
        li = jnp.broadcast_to(a_ref[1, :, sl], (N_SEG, BLK_ST))

        def step(jj, carry, sl=sl, lr=lr, li=li):
            sr, si = carry
            j = (n_steps - 1 - jj) if reverse else jj
            r0 = pl.multiple_of(j * N_SEG, N_SEG)
            nr = lr * sr - li * si + bur[pl.ds(r0, N_SEG), sl]
            ni = lr * si + li * sr + bui[pl.ds(r0, N_SEG), sl]
            bur[pl.ds(r0, N_SEG), sl] = nr
            bui[pl.ds(r0, N_SEG), sl] = ni
            return nr, ni

        sr, si = lax.fori_loop(0, n_steps, step, (st_ref[0, :, sl], st_ref[1, :, sl]))
        st_ref[0, :, sl] = sr
        st_ref[1, :, sl] = si


def _project_in(x16, w_re, w_im, bur, bui, adjoint):
    for gb in range(N_BLOCKS):
        xb = x16[:, gb * BLK_CH:(gb + 1) * BLK_CH]
        sl = slice(gb * BLK_ST, (gb + 1) * BLK_ST)
        if adjoint:
            dn = (((1,), (1,)), ((), ()))
            bur[:, sl] = lax.dot_general(xb, w_re[gb], dn, preferred_element_type=F32)
            bui[:, sl] = -lax.dot_general(xb, w_im[gb], dn, preferred_element_type=F32)
        else:
            bur[:, sl] = jnp.dot(xb, w_re[gb], preferred_element_type=F32)
            bui[:, sl] = jnp.dot(xb, w_im[gb], preferred_element_type=F32)


def s5_scan(act, w_re, w_im, a, init, *, reverse, adjoint=False, c_re=None, c_im=None, add=None,
            want_ckpt=False, rows=None, name):
    act_off, N = rows if rows is not None else (0, act.shape[0])
    R = math.gcd(ROW_TILE, N, act_off)
    nch, jc = N // R, R // N_SEG
    with_out = c_re is not None

    def chunk(i):
        return (nch - 1 - i) if reverse else i

    def body(*refs):
        act_ref, wre_ref, wim_ref, a_ref, init_ref = refs[:5]
        k = 5
        if with_out:
            cre_ref, cim_ref = refs[k:k + 2]
            k += 2
        if add is not None:
            add_ref = refs[k]
            k += 1
        if with_out:
            out_ref = refs[k]
            k += 1
        if want_ckpt:
            ck_ref = refs[k]
            k += 1
        fin_ref, bur, bui = refs[k:k + 3]

        @pl.when(pl.program_id(0) == 0)
        def _():
            fin_ref[...] = init_ref[...]

        if want_ckpt:
            ck_ref[0] = fin_ref[...]
        _project_in(act_ref[...].astype(BF16), wre_ref, wim_ref, bur, bui, adjoint)
        _scan_chunk(bur, bui, fin_ref, a_ref, jc, reverse)
        if with_out:
            for gb in range(N_BLOCKS):
                sl = slice(gb * BLK_ST, (gb + 1) * BLK_ST)
                y = (jnp.dot(bur[:, sl].astype(BF16), cre_ref[gb], preferred_element_type=F32)
                     - jnp.dot(bui[:, sl].astype(BF16), cim_ref[gb], preferred_element_type=F32))
                cs = slice(gb * BLK_CH, (gb + 1) * BLK_CH)
                if add is not None:
                    y = y + add_ref[:, cs]
                out_ref[:, cs] = y

    row_spec = pl.BlockSpec((R, D_MODEL), lambda i: (chunk(i), 0))
    act_spec = pl.BlockSpec((R, D_MODEL), lambda i: (chunk(i) + act_off // R, 0))
    w_spec = pl.BlockSpec(w_re.shape, lambda i: (0, 0, 0))
    st_spec = pl.BlockSpec((2, N_SEG, S5_LANES), lambda i: (0, 0, 0))
    ins = [act, w_re, w_im, a, init]
    in_specs = [act_spec, w_spec, w_spec, pl.BlockSpec((2, 1, S5_LANES), lambda i: (0, 0, 0)), st_spec]
    if with_out:
        ins += [c_re, c_im]
        in_specs += [pl.BlockSpec(c_re.shape, lambda i: (0, 0, 0))] * 2
    if add is not None:
        ins.append(add)
        in_specs.append(row_spec)
    out_shape, out_specs = [], []
    if with_out:
        out_shape.append(jax.ShapeDtypeStruct((N, D_MODEL), F32))
        out_specs.append(row_spec)
    if want_ckpt:
        out_shape.append(jax.ShapeDtypeStruct((nch, 2, N_SEG, S5_LANES), F32))
        out_specs.append(pl.BlockSpec((1, 2, N_SEG, S5_LANES), lambda i: (chunk(i), 0, 0, 0)))
    out_shape.append(jax.ShapeDtypeStruct((2, N_SEG, S5_LANES), F32))
    out_specs.append(st_spec)
    res = pl.pallas_call(
        body, out_shape=out_shape, grid=(nch,), in_specs=in_specs, out_specs=out_specs,
        scratch_shapes=[pltpu.VMEM((R, S5_LANES), F32), pltpu.VMEM((R, S5_LANES), F32)],
        compiler_params=_params(("arbitrary",), VMEM_LIMIT), name=name)(*ins)
    res = list(res)
    out = res.pop(0) if with_out else None
    ckpt = res.pop(0) if want_ckpt else None
    return out, ckpt, res[0]


def s5_grads(dy, u, ckpt, b_re, b_im, c_re, c_im, lam, init_adj, *, reverse, add=None, u_off=0, name):
    N = dy.shape[0]
    R = math.gcd(ROW_TILE, N, u_off)
    nch, jc = N // R, R // N_SEG
    W = S5_LANES

    def chunk(i):
        return i if reverse else (nch - 1 - i)

    def body(*refs):
        dy_ref, u_ref, ck_ref, bre_ref, bim_ref, cre_ref, cim_ref, lam_ref, init_ref = refs[:9]
        k = 9
        if add is not None:
            add_ref = refs[k]
            k += 1
        du_ref, dlam_ref, dbre_ref, dbim_ref, dcre_ref, dcim_ref, fin_ref = refs[k:k + 7]
        sr_buf, si_buf, er_buf, ei_buf, st_buf = refs[k + 7:k + 12]

        @pl.when(pl.program_id(0) == 0)
        def _():
            fin_ref[...] = init_ref[...]
            dlam_ref[...] = jnp.zeros_like(dlam_ref)
            dbre_ref[...] = jnp.zeros_like(dbre_ref)
            dbim_ref[...] = jnp.zeros_like(dbim_ref)
            dcre_ref[...] = jnp.zeros_like(dcre_ref)
            dcim_ref[...] = jnp.zeros_like(dcim_ref)

        u16 = u_ref[...].astype(BF16)
        dy16 = dy_ref[...].astype(BF16)
        st_buf[...] = ck_ref[0]
        _project_in(u16, bre_ref, bim_ref, sr_buf, si_buf, False)
        _scan_chunk(sr_buf, si_buf, st_buf, lam_ref, jc, reverse)
        _project_in(dy16, cre_ref, cim_ref, er_buf, ei_buf, True)
        for lc in range(W // BLK_ST):
            sl = slice(lc * BLK_ST, (lc + 1) * BLK_ST)
            lr = jnp.broadcast_to(lam_ref[0, :, sl], (N_SEG, BLK_ST))
            li = jnp.broadcast_to(lam_ref[1, :, sl], (N_SEG, BLK_ST))

            def one(r0, spr, spi, carry, sl=sl, lr=lr, li=li):
                gr, gi, ar, ai = carry
                nr = er_buf[pl.ds(r0, N_SEG), sl] + lr * gr + li * gi
                ni = ei_buf[pl.ds(r0, N_SEG), sl] + lr * gi - li * gr
                er_buf[pl.ds(r0, N_SEG), sl] = nr
                ei_buf[pl.ds(r0, N_SEG), sl] = ni
                return nr, ni, ar + spr * nr + spi * ni, ai + spr * ni - spi * nr

            def step(ff, carry, sl=sl, one=one):
                f = jc - 1 - ff
                j = (jc - 1 - f) if reverse else f
                jp = (j + 1) if reverse else (j - 1)
                r0 = pl.multiple_of(j * N_SEG, N_SEG)
                p0 = pl.multiple_of(jp * N_SEG, N_SEG)
                return one(r0, sr_buf[pl.ds(p0, N_SEG), sl], si_buf[pl.ds(p0, N_SEG), sl], carry)

            carry = (fin_ref[0, :, sl], fin_ref[1, :, sl], dlam_ref[0, :, sl], dlam_ref[1, :, sl])
            carry = lax.fori_loop(0, jc - 1, step, carry)
            r_first = (jc - 1) * N_SEG if reverse else 0
            gr, gi, ar, ai = one(r_first, ck_ref[0, 0, :, sl], ck_ref[0, 1, :, sl], carry)
            fin_ref[0, :, sl] = gr
            fin_ref[1, :, sl] = gi
            dlam_ref[0, :, sl] = ar
            dlam_ref[1, :, sl] = ai
        tn = (((0,), (0,)), ((), ()))
        nt = (((1,), (1,)), ((), ()))
        for gb in range(N_BLOCKS):
            sl = slice(gb * BLK_ST, (gb + 1) * BLK_ST)
            cs = slice(gb * BLK_CH, (gb + 1) * BLK_CH)
            gr16 = er_buf[:, sl].astype(BF16)
            gi16 = ei_buf[:, sl].astype(BF16)
            du = (lax.dot_general(gr16, bre_ref[gb], nt, preferred_element_type=F32)
                  + lax.dot_general(gi16, bim_ref[gb], nt, preferred_element_type=F32))
            if add is not None:
                du = du + add_ref[:, cs]
            du_ref[:, cs] = du
            ub, dyb = u16[:, cs], dy16[:, cs]
            dbre_ref[gb] += lax.dot_general(ub, gr16, tn, preferred_element_type=F32)
            dbim_ref[gb] += lax.dot_general(ub, gi16, tn, preferred_element_type=F32)
            dcre_ref[gb] += lax.dot_general(sr_buf[:, sl].astype(BF16), dyb, tn, preferred_element_type=F32)
            dcim_ref[gb] -= lax.dot_general(si_buf[:, sl].astype(BF16), dyb, tn, preferred_element_type=F32)

    row_spec = pl.BlockSpec((R, D_MODEL), lambda i: (chunk(i), 0))
    st_spec = pl.BlockSpec((2, N_SEG, W), lambda i: (0, 0, 0))
    wb_spec = pl.BlockSpec(b_re.shape, lambda i: (0, 0, 0))
    wc_spec = pl.BlockSpec(c_re.shape, lambda i: (0, 0, 0))
    ins = [dy, u, ckpt, b_re, b_im, c_re, c_im, lam, init_adj]
    u_spec = pl.BlockSpec((R, D_MODEL), lambda i: (chunk(i) + u_off // R, 0))
    in_specs = [row_spec, u_spec, pl.BlockSpec((1, 2, N_SEG, W), lambda i: (chunk(i), 0, 0, 0)),
                wb_spec, wb_spec, wc_spec, wc_spec, pl.BlockSpec((2, 1, W), lambda i: (0, 0, 0)), st_spec]
    if add is not None:
        ins.append(add)
        in_specs.append(row_spec)
    out_shape = [jax.ShapeDtypeStruct((N, D_MODEL), F32), jax.ShapeDtypeStruct((2, N_SEG, W), F32),
                 jax.ShapeDtypeStruct(b_re.shape, F32), jax.ShapeDtypeStruct(b_re.shape, F32),
                 jax.ShapeDtypeStruct(c_re.shape, F32), jax.ShapeDtypeStruct(c_re.shape, F32),
                 jax.ShapeDtypeStruct((2, N_SEG, W), F32)]
    out_specs = [row_spec, st_spec, wb_spec, wb_spec, wc_spec, wc_spec, st_spec]
    return pl.pallas_call(
        body, out_shape=out_shape, grid=(nch,), in_specs=in_specs, out_specs=out_specs,
        scratch_shapes=[pltpu.VMEM((R, W), F32) for _ in range(4)] + [pltpu.VMEM((2, N_SEG, W), F32)],
        compiler_params=_params(("arbitrary",), VMEM_LIMIT), name=name)(*ins)


def adamw(w, g, m, v, name="adamw"):
    n, d = w.shape
    lanes = -(-d // 128) * 128
    tm = n
    while tm * lanes * 4 > (1 << 20) and tm % 16 == 0:
        tm //= 2
    c1 = 1.0 - ADAM_B1 ** ADAM_STEP
    c2 = 1.0 - ADAM_B2 ** ADAM_STEP

    def body(w_ref, g_ref, m_ref, v_ref, d_ref, nm_ref, nv_ref):
        g_ = g_ref[...]
        m_ = ADAM_B1 * m_ref[...] + (1.0 - ADAM_B1) * g_
        v_ = ADAM_B2 * v_ref[...] + (1.0 - ADAM_B2) * (g_ * g_)
        d_ref[...] = -ADAM_LR * ((m_ / c1) / (jnp.sqrt(v_ / c2) + ADAM_EPS) + ADAM_WD * w_ref[...])
        nm_ref[...] = m_
        nv_ref[...] = v_

    spec = pl.BlockSpec((tm, d), lambda i: (i, 0))
    return pl.pallas_call(
        body, out_shape=[jax.ShapeDtypeStruct((n, d), F32)] * 3, grid=(n // tm,),
        in_specs=[spec] * 4, out_specs=[spec] * 3,
        compiler_params=_params(("parallel",), VMEM_LIMIT), name=name)(w, g, m, v)


def _coords():
    return lax.axis_index("x"), lax.axis_index("y"), lax.axis_index("c")


def exchange(arrays, out_shapes, remote, local, name, aliases=None):
    n_in, n_out, n_rem, n_loc = len(arrays), len(out_shapes), len(remote), len(local)

    def at(ref, idx):
        return ref if idx is None else ref.at[idx]

    def body(*refs):
        ins, outs = refs[:n_in], refs[n_in:n_in + n_out]
        send_sems, recv_sems, local_sems = refs[n_in + n_out:]
        me = _coords()
        sends, recvs = [], []
        for k, (flip, ii, src_at, oi, dst_at) in enumerate(remote):
            peer = (me[0] ^ flip[0], me[1] ^ flip[1], me[2] ^ flip[2])
            src = at(ins[ii], src_at(me, peer))
            sends.append(pltpu.make_async_remote_copy(
                src_ref=src, dst_ref=at(outs[oi], dst_at(me)), send_sem=send_sems.at[k], recv_sem=recv_sems.at[k],
                device_id=peer, device_id_type=MESH))
            recvs.append(pltpu.make_async_remote_copy(
                src_ref=src, dst_ref=at(outs[oi], dst_at(peer)), send_sem=send_sems.at[k], recv_sem=recv_sems.at[k],
                device_id=peer, device_id_type=MESH))
        locs = [pltpu.make_async_copy(at(ins[ii], src_at(me)), at(outs[oi], dst_at(me)), local_sems.at[k])
                for k, (ii, src_at, oi, dst_at) in enumerate(local)]
        for cp in locs + sends:
            cp.start()
        for cp in recvs:
            cp.wait_recv()
        for cp in sends:
            cp.wait_send()
        for cp in locs:
            cp.wait()

    hbm = pl.BlockSpec(memory_space=pl.ANY)
    return pl.pallas_call(
        body, out_shape=list(out_shapes), in_specs=[hbm] * n_in, out_specs=[hbm] * n_out,
        scratch_shapes=[pltpu.SemaphoreType.DMA((n_rem,)), pltpu.SemaphoreType.DMA((n_rem,)),
                        pltpu.SemaphoreType.DMA((max(n_loc, 1),))],
        input_output_aliases=aliases or {}, name=name)(*arrays)


ALL_FLIPS = [(dx, dy, dc) for dx in (0, 1) for dy in (0, 1) for dc in (0, 1)][1:]
CHIP_FLIPS = [(1, 0, 0), (0, 1, 0), (1, 1, 0)]
CORE_FLIP = (0, 0, 1)


def _dev_index(p):
    return 4 * p[0] + 2 * p[1] + p[2]


def _chip_index(p):
    return 2 * p[0] + p[1]


def _gather(xs, flips, index, n, name):
    arrays = [x[None] for x in xs]
    outs = [jax.ShapeDtypeStruct((n,) + x.shape, x.dtype) for x in xs]
    remote = [(f, a, lambda me, peer: (0,), a, lambda s: (index(s),)) for a in range(len(xs)) for f in flips]
    local = [(a, lambda me: (0,), a, lambda me: (index(me),)) for a in range(len(xs))]
    return exchange(arrays, outs, remote, local, name)


def allgather_devices(x, name):
    return _gather([x], ALL_FLIPS, _dev_index, N_DEV, name)[0]


def allgather_chips(xs, name):
    return _gather(xs, CHIP_FLIPS, _chip_index, N_CHIP, name)


def gather_halves(xs, name):
    n = len(xs)
    nk = n * len(CHIP_FLIPS)

    def body(*refs):
        ins, outs = refs[:n], refs[n:2 * n]
        ici_send, ici_recv, d2d_send, d2d_recv = refs[2 * n:]
        me = _coords()
        sibling = (me[0], me[1], 1 - me[2])
        first, passed, landed = [], [], []
        for a in range(n):
            half = ins[a].shape[0] // 2
            mine = ins[a].at[pl.ds(pl.multiple_of(me[2] * half, 16), half)]
            for j, flip in enumerate(CHIP_FLIPS):
                k = a * len(CHIP_FLIPS) + j
                peer = (me[0] ^ flip[0], me[1] ^ flip[1], me[2])
                first.append(pltpu.make_async_remote_copy(
                    src_ref=mine, dst_ref=outs[a].at[_chip_index(me), me[2]], send_sem=ici_send.at[k],
                    recv_sem=ici_recv.at[k], device_id=peer, device_id_type=MESH))
                arrived = outs[a].at[_chip_index(peer), me[2]]
                landed.append(pltpu.make_async_remote_copy(
                    src_ref=mine, dst_ref=arrived, send_sem=ici_send.at[k], recv_sem=ici_recv.at[k],
                    device_id=peer, device_id_type=MESH))
                passed.append(pltpu.make_async_remote_copy(
                    src_ref=arrived, dst_ref=arrived, send_sem=d2d_send.at[k], recv_sem=d2d_recv.at[k],
                    device_id=sibling, device_id_type=MESH))
        for cp in first:
            cp.start()
        for k in range(nk):
            landed[k].wait_recv()
            passed[k].start()
        for a in range(n):
            for j, flip in enumerate(CHIP_FLIPS):
                k = a * len(CHIP_FLIPS) + j
                peer_chip = _chip_index((me[0] ^ flip[0], me[1] ^ flip[1]))
                from_sibling = outs[a].at[peer_chip, 1 - me[2]]
                pltpu.make_async_remote_copy(
                    src_ref=from_sibling, dst_ref=from_sibling, send_sem=d2d_send.at[k], recv_sem=d2d_recv.at[k],
                    device_id=sibling, device_id_type=MESH).wait_recv()
        for cp in first + passed:
            cp.wait_send()

    hbm = pl.BlockSpec(memory_space=pl.ANY)
    return pl.pallas_call(
        body, out_shape=[jax.ShapeDtypeStruct((N_CHIP, 2, x.shape[0] // 2, x.shape[1]), x.dtype) for x in xs],
        in_specs=[hbm] * n, out_specs=[hbm] * n,
        scratch_shapes=[pltpu.SemaphoreType.DMA((nk,)) for _ in range(4)], name=name)(*xs)


HBM_SPEC = pl.BlockSpec(memory_space=pltpu.HBM)
SEM_SPEC = pl.BlockSpec(memory_space=pltpu.SEMAPHORE)
DATAFLOW = pltpu.SideEffectType.DATAFLOW_SIDE_EFFECTING


def _at(ref, idx):
    return ref if idx is None else ref.at[idx]


def _peer(me, flip):
    return (me[0] ^ flip[0], me[1] ^ flip[1], me[2] ^ flip[2])


def exchange_start(arrays, land_shapes, remote, name):
    n_in, n_out, nk = len(arrays), len(land_shapes), len(remote)

    def body(*refs):
        srcs, lands = refs[:n_in], refs[n_in:n_in + n_out]
        send_sems, recv_sems, token = refs[n_in + n_out], refs[n_in + n_out + 1], refs[-1]
        me = _coords()
        for k, (flip, ii, src_at, oi, dst_at) in enumerate(remote):
            peer = _peer(me, flip)
            pltpu.make_async_remote_copy(
                src_ref=_at(srcs[ii], src_at(me, peer)), dst_ref=_at(lands[oi], dst_at(me)), send_sem=send_sems.at[k],
                recv_sem=recv_sems.at[k], device_id=peer, device_id_type=MESH).start()
        token[...] = jnp.zeros_like(token)

    lands = [lax.empty(s.shape, s.dtype) for s in land_shapes]
    bufs = list(arrays) + lands
    out = pl.pallas_call(
        body, name=name,
        out_shape=(pltpu.SemaphoreType.DMA((nk,)), pltpu.SemaphoreType.DMA((nk,)),
                   *[pltpu.HBM(b.shape, b.dtype) for b in bufs], jax.ShapeDtypeStruct((8, 128), F32)),
        in_specs=[HBM_SPEC] * len(bufs),
        out_specs=(SEM_SPEC, SEM_SPEC, *[HBM_SPEC] * len(bufs), pl.BlockSpec(memory_space=pltpu.VMEM)),
        input_output_aliases={a: 2 + a for a in range(len(bufs))},
        compiler_params=pltpu.CompilerParams(has_side_effects=DATAFLOW),
    )(*[pltpu.with_memory_space_constraint(b, pltpu.HBM) for b in bufs])
    flight = (out[0], out[1], list(out[2:2 + n_in]), list(out[2 + n_in:2 + n_in + n_out]), remote)
    return flight, out[-1]


def exchange_wait(flight, after, name):
    send_sems, recv_sems, arrays, lands, remote = flight
    n_in, n_out = len(arrays), len(lands)

    def body(*refs):
        srcs, lnds = refs[:n_in], refs[n_in:n_in + n_out]
        s_sems, r_sems = refs[n_in + n_out], refs[n_in + n_out + 1]
        me = _coords()
        for k, (flip, ii, src_at, oi, dst_at) in enumerate(remote):
            peer = _peer(me, flip)
            copy = pltpu.make_async_remote_copy(
                src_ref=_at(srcs[ii], src_at(me, peer)), dst_ref=_at(lnds[oi], dst_at(peer)), send_sem=s_sems.at[k],
                recv_sem=r_sems.at[k], device_id=peer, device_id_type=MESH)
            copy.wait_send()
            copy.wait_recv()

    bufs = list(arrays) + list(lands)
    out = pl.pallas_call(
        body, name=name,
        out_shape=tuple(pltpu.HBM(b.shape, b.dtype) for b in bufs),
        in_specs=[HBM_SPEC] * len(bufs) + [SEM_SPEC, SEM_SPEC, pl.BlockSpec(memory_space=pl.ANY)],
        out_specs=tuple([HBM_SPEC] * len(bufs)),
        input_output_aliases={a: a for a in range(len(bufs))},
        compiler_params=pltpu.CompilerParams(has_side_effects=DATAFLOW),
    )(*bufs, send_sems, recv_sems, after)
    return list(out[:n_in]), list(out[n_in:])


def _half_tile(h, cd):
    return h if h * cd * 4 <= (1 << 20) else math.gcd(512, h)


def pair_add(g, got, core, out_dtype, name):
    _, _, h, cd = g.shape
    th = _half_tile(h, cd)

    def body(c_ref, g_ref, got_ref, o_ref):
        o_ref[0] = (g_ref[0, 0] + got_ref[0]).astype(o_ref.dtype)

    return pl.pallas_call(
        body, out_shape=jax.ShapeDtypeStruct((N_CHIP, h, cd), out_dtype),
        grid_spec=pltpu.PrefetchScalarGridSpec(
            num_scalar_prefetch=1, grid=(N_CHIP, h // th),
            in_specs=[pl.BlockSpec((1, 1, th, cd), lambda q, i, c: (q, c[0], i, 0)),
                      pl.BlockSpec((1, th, cd), lambda q, i, c: (q, i, 0))],
            out_specs=pl.BlockSpec((1, th, cd), lambda q, i, c: (q, i, 0))),
        compiler_params=_params(("parallel", "parallel"), VMEM_LIMIT), name=name)(core, g, got)


def sum_chips(parts, sums, place, name):
    _, h, cd = parts.shape
    th = _half_tile(h, cd)

    def body(pc_ref, p_ref, own_ref, o_ref):
        acc = None
        for q in range(N_CHIP):
            term = jnp.where(pc_ref[1] == q, own_ref[0], p_ref[q]).astype(F32)
            acc = term if acc is None else acc + term
        o_ref[0] = acc

    return pl.pallas_call(
        body, out_shape=jax.ShapeDtypeStruct((2, h, cd), F32),
        grid_spec=pltpu.PrefetchScalarGridSpec(
            num_scalar_prefetch=1, grid=(h // th,),
            in_specs=[pl.BlockSpec((N_CHIP, th, cd), lambda i, pc: (0, i, 0)),
                      pl.BlockSpec((1, th, cd), lambda i, pc: (pc[1], i, 0))],
            out_specs=pl.BlockSpec((1, th, cd), lambda i, pc: (pc[0], i, 0))),
        compiler_params=_params(("parallel",), VMEM_LIMIT), name=name)(place, parts, sums)


def to_segments(a, n_ctx):
    def one(p):
        n = p.shape[0]
        return p.reshape(N_SEG, n // N_SEG, -1).transpose(1, 0, 2).reshape(n, -1)
    return jnp.concatenate([one(a[:n_ctx]), one(a[n_ctx:])], axis=0) if n_ctx else one(a)


def from_segments(a, n_ctx):
    def one(p):
        n = p.shape[0]
        return p.reshape(n // N_SEG, N_SEG, -1).transpose(1, 0, 2).reshape(n, -1)
    return jnp.concatenate([one(a[:n_ctx]), one(a[n_ctx:])], axis=0) if n_ctx else one(a)


def rope_tables(n_ctx, n_lat):
    f32 = np.float32
    rows = n_lat // GRID_W
    row = np.repeat(np.arange(rows), GRID_W).astype(f32)
    col = np.tile(np.arange(GRID_W), rows).astype(f32)
    d = QK_ROPE_DIM // 2
    inv = (f32(1.0) / np.power(f32(ROPE_THETA), np.arange(0, d, 2, dtype=f32) / f32(d))).astype(f32)
    ang = np.concatenate([row[:, None] * inv[None, :], col[:, None] * inv[None, :]], axis=1).astype(f32)
    cos = np.concatenate([np.ones((n_ctx, d), f32), np.cos(ang)], axis=0)
    sin = np.concatenate([np.zeros((n_ctx, d), f32), np.sin(ang)], axis=0)
    q = QK_ROPE_DIM // 4
    T = n_ctx + n_lat
    ones, zeros = np.ones((T, QK_NOPE_DIM), f32), np.zeros((T, QK_NOPE_DIM), f32)
    tail, z8 = np.zeros((T, HEAD_LANES - QK_DIM), f32), np.zeros((T, q), f32)
    cr, cc, sr, sc = cos[:, :q], cos[:, q:], sin[:, :q], sin[:, q:]
    cos_t = np.concatenate([ones, cr, cr, cc, cc, tail], axis=1)
    sin_next = np.concatenate([zeros, -sr, z8, -sc, z8, tail], axis=1)
    sin_prev = np.concatenate([zeros, z8, sr, z8, sc, tail], axis=1)
    return tuple(jnp.asarray(t, F32) for t in (cos_t, sin_next, sin_prev))


def pad_heads(w, used):
    k = w.shape[0]
    return jnp.pad(w.reshape(k, MLA_HEADS, used), ((0, 0), (0, 0), (0, HEAD_LANES - used))).reshape(k, -1)


def unpad_heads(w, used):
    k = w.shape[0]
    return w.reshape(k, MLA_HEADS, HEAD_LANES)[:, :, :used].reshape(k, MLA_HEADS * used)


def rotary_spread():
    lane = np.arange(MLA_HEADS * HEAD_LANES) % HEAD_LANES
    return jnp.asarray(lane[None, :] == (QK_NOPE_DIM + np.arange(QK_ROPE_DIM))[:, None], BF16)


def s5_discretise(a_re, a_im, log_step, b_re, b_im):
    dt = jnp.exp(log_step)[:, None]
    mag = jnp.exp(a_re * dt)
    lb_re = mag * jnp.cos(a_im * dt)
    lb_im = mag * jnp.sin(a_im * dt)
    den = a_re * a_re + a_im * a_im
    nr = lb_re - 1.0
    f_re = ((nr * a_re + lb_im * a_im) / den)[..., None]
    f_im = ((lb_im * a_re - nr * a_im) / den)[..., None]
    return lb_re, lb_im, f_re * b_re - f_im * b_im, f_re * b_im + f_im * b_re


def s5_block_weights(lb_re, lb_im, bb_re, bb_im, c_re, c_im):
    eye = jnp.eye(GROUPS_PER_BLOCK, dtype=F32)
    lam = jnp.stack([lb_re.reshape(1, S5_LANES), lb_im.reshape(1, S5_LANES)])

    def b_blocks(bb):
        t = bb.reshape(N_BLOCKS, GROUPS_PER_BLOCK, S5_STATE, S5_GROUP)
        return jnp.einsum("bgpc,gh->bgchp", t, eye).reshape(N_BLOCKS, BLK_CH, BLK_ST).astype(BF16)

    def c_blocks(cc):
        t = cc.reshape(N_BLOCKS, GROUPS_PER_BLOCK, S5_GROUP, S5_STATE)
        return jnp.einsum("bgcp,gh->bgphc", t, eye).reshape(N_BLOCKS, BLK_ST, BLK_CH).astype(BF16)

    return lam, b_blocks(bb_re), b_blocks(bb_im), c_blocks(c_re), c_blocks(c_im)


def b_block_diag(db):
    t = db.reshape(N_BLOCKS, GROUPS_PER_BLOCK, S5_GROUP, GROUPS_PER_BLOCK, S5_STATE)
    return jnp.einsum("bgchp,gh->bgpc", t, jnp.eye(GROUPS_PER_BLOCK, dtype=F32)).reshape(S5_GROUPS, S5_STATE, S5_GROUP)


def c_block_diag(dc):
    t = dc.reshape(N_BLOCKS, GROUPS_PER_BLOCK, S5_STATE, GROUPS_PER_BLOCK, S5_GROUP)
    return jnp.einsum("bgphc,gh->bgcp", t, jnp.eye(GROUPS_PER_BLOCK, dtype=F32)).reshape(S5_GROUPS, S5_GROUP, S5_STATE)


def conj(a):
    return jnp.stack([a[0], -a[1]])


PACK_TILE = 16 * 128


def pack_flat(parts, dtype):
    flat = [p.reshape(-1).astype(dtype) for p in parts]
    sizes = [f.shape[0] for f in flat]
    total = sum(sizes)
    pad = (-total) % PACK_TILE
    if pad:
        flat.append(jnp.zeros((pad,), dtype))
    offs = np.cumsum([0] + sizes)[:-1].tolist()
    return jnp.concatenate(flat).reshape(-1, 128), offs


def unpack_flat(buf, offs, shapes):
    flat = buf.reshape(-1)
    return [flat[o:o + int(np.prod(s))].reshape(s) for o, s in zip(offs, shapes)]


def s5_forward(p1, n_ctx, dirs):
    saved = []
    y = None
    ctx_rows, lat_rows = (0, n_ctx), (n_ctx, p1.shape[0] - n_ctx)
    zeros_tile = jnp.zeros((2, N_SEG, S5_LANES), F32)
    zeros_row = jnp.zeros((2, 1, S5_LANES), F32)
    for k, (lam, b_re, b_im, c_re, c_im) in enumerate(dirs):
        rev = k == 1
        last = 0 if rev else N_SEG - 1
        _, _, fin = s5_scan(p1, b_re, b_im, lam, zeros_tile, reverse=rev, rows=ctx_rows, name=f"s5_ctx_finals{k}")
        carry_c = s5_chain(fin, zeros_row, lam, n_ctx // N_SEG, rev, name=f"s5_ctx_chain{k}")
        _, ck_c, fin_c = s5_scan(p1, b_re, b_im, lam, carry_c, reverse=rev, want_ckpt=True, rows=ctx_rows,
                                 name=f"s5_ctx_scan{k}")
        s0 = fin_c[:, last:last + 1, :]
        _, _, fin = s5_scan(p1, b_re, b_im, lam, zeros_tile, reverse=rev, rows=lat_rows, name=f"s5_lat_finals{k}")
        carry_l = s5_chain(fin, s0, lam, lat_rows[1] // N_SEG, rev, name=f"s5_lat_chain{k}")
        y, ck_l, _ = s5_scan(p1, b_re, b_im, lam, carry_l, reverse=rev, c_re=c_re, c_im=c_im, add=y,
                             want_ckpt=True, rows=lat_rows, name=f"s5_lat_scan{k}")
        saved.append((ck_c, ck_l))
    return y, saved


def s5_backward(dy_l, du_extra_l, p1, n_ctx, dirs, saved):
    n_lat = p1.shape[0] - n_ctx
    zeros_tile = jnp.zeros((2, N_SEG, S5_LANES), F32)
    zeros_row = jnp.zeros((2, 1, S5_LANES), F32)
    dy_c = jnp.zeros((n_ctx, D_MODEL), F32)
    du_l, du_c = du_extra_l, None
    grads = []
    for k, (lam, b_re, b_im, c_re, c_im) in enumerate(dirs):
        rev = k == 1
        lam_c = conj(lam)
        ck_c, ck_l = saved[k]
        first = N_SEG - 1 if rev else 0
        _, _, fin = s5_scan(dy_l, c_re, c_im, lam_c, zeros_tile, reverse=not rev, adjoint=True,
                            name=f"s5_lat_adj_finals{k}")
        carry = s5_chain(fin, zeros_row, lam_c, n_lat // N_SEG, not rev, name=f"s5_lat_adj_chain{k}")
        du_l, dlam_l, dbr_l, dbi_l, dcr_l, dci_l, fin_a = s5_grads(
            dy_l, p1, ck_l, b_re, b_im, c_re, c_im, lam, carry, reverse=rev, add=du_l, u_off=n_ctx,
            name=f"s5_lat_grads{k}")
        g0 = fin_a[:, first:first + 1, :]
        carry = s5_chain(zeros_tile, g0, lam_c, n_ctx // N_SEG, not rev, name=f"s5_ctx_adj_chain{k}")
        du_c, dlam_c, dbr_c, dbi_c, _, _, _ = s5_grads(
            dy_c, p1, ck_c, b_re, b_im, c_re, c_im, lam, carry, reverse=rev, add=du_c, name=f"s5_ctx_grads{k}")
        dlam = jnp.sum(dlam_l + dlam_c, axis=1)
        grads.append((dlam, b_block_diag(dbr_l + dbr_c), b_block_diag(dbi_l + dbi_c),
                      c_block_diag(dcr_l), c_block_diag(dci_l)))
    return jnp.concatenate([du_c, du_l], axis=0), grads


def local_step(x, ctx, target, mod, w, late=None, reducer=None):
    L, Lc = x.shape[0], ctx.shape[0]
    T = L + Lc
    assert L % Lc == 0 and Lc % (2 * N_SEG) == 0 and L % GRID_W == 0
    D = D_MODEL
    X0 = jnp.concatenate([ctx, x], axis=0)

    def mod_of(i, j):
        return mod[i, :, j, :][:, None, :]

    def vec(v):
        return v.reshape(1, 1, -1).astype(F32)

    g0 = vec(w["norm_g"][0])
    (H0,) = rowwise_fwd(f_norm_mod, [X0], [g0, mod_of(0, 1), mod_of(0, 0)], [D], [BF16], T, Lc, "l0_norm")
    p0 = mm_nn(H0, w["mla_w_in"], name="l0_in")
    cq = Rows(p0, Q_LORA_RANK, col_blk=D // Q_LORA_RANK)
    ckv = Rows(p0, KV_LORA_RANK, col_blk=(D + Q_LORA_RANK) // KV_LORA_RANK)
    kr = p0[:, D + Q_LORA_RANK + KV_LORA_RANK:D + P0_HEAD]
    qng, kvng = vec(w["mla_q_norm"]), vec(w["mla_kv_norm"])
    (qn,) = rowwise_fwd(f_rms, [cq], [qng], [Q_LORA_RANK], [BF16], T, 0, "l0_qnorm")
    (kvn,) = rowwise_fwd(f_rms, [ckv], [kvng], [KV_LORA_RANK], [BF16], T, 0, "l0_kvnorm")
    tabs = rope_tables(Lc, L)
    spread = rotary_spread()
    w_uq_p = pad_heads(w["mla_w_uq"], QK_DIM)
    w_ukv3 = w["mla_w_ukv"].reshape(KV_LORA_RANK, MLA_HEADS, QK_NOPE_DIM + V_HEAD_DIM)
    w_kn_p = pad_heads(w_ukv3[:, :, :QK_NOPE_DIM].reshape(KV_LORA_RANK, -1), QK_NOPE_DIM)
    w_v = w_ukv3[:, :, QK_NOPE_DIM:].reshape(KV_LORA_RANK, -1)
    qb = q_heads(qn, w_uq_p, tabs)
    kb, vb = kv_heads(kvn, w_kn_p, w_v, kr, spread, tabs)
    o, lse, lse_rows = attn_fwd(qb, kb, vb, Lc)
    X1, og, out0 = mla_post_fwd(o, p0, X0, mod_of(0, 2), w["mla_w_out"], Lc)

    if late is not None:
        w = {**w, **late(X1)}
    X1p = to_segments(X1, Lc)
    tgt_p = to_segments(target, 0)
    g1 = vec(w["norm_g"][1])
    (H1,) = rowwise_fwd(f_norm_mod, [X1p], [g1, mod_of(1, 1), mod_of(1, 0)], [D], [BF16], T, Lc, "l1_norm")
    p1 = mm_nn(H1, w["s5_w_in"], name="l1_in")
    disc_fn = lambda *a: tuple(zip(*[s5_discretise(a[0][k], a[1][k], a[2][k], a[3][k], a[4][k]) for k in range(2)]))
    disc, disc_vjp = jax.vjp(disc_fn, w["s5_a_re"], w["s5_a_im"], w["s5_log_step"], w["s5_b_re"], w["s5_b_im"])
    dirs = [s5_block_weights(disc[0][k], disc[1][k], disc[2][k], disc[3][k], w["s5_c_re"][k], w["s5_c_im"][k])
            for k in range(2)]
    y_ssm, s5_saved = s5_forward(p1, Lc, dirs)

    row = lambda v: v.reshape(1, D).astype(F32)
    (lvec, dX2, d_yssm, d_u_act, d_z1, d_fg, d_gt1, d_bg, d_d, gw_glu, gw_out) = s5_tail(
        y_ssm, p1, X1p, tgt_p, Lc, row(w["s5_d"]), row(w["s5_b_glu"]), mod[1, 1:2, 2, :], row(w["final_g"]),
        w["s5_w_glu"], w["s5_w_out"])
    loss = jnp.sum(lvec)
    gw = {"final_g": d_fg.reshape(D), "s5_b_glu": d_bg.reshape(D), "s5_d": d_d.reshape(D),
          "s5_w_glu": gw_glu, "s5_w_out": gw_out}
    dmod = {}

    du_p, s5_g = s5_backward(d_yssm, d_u_act, p1, Lc, dirs, s5_saved)
    d_disc = tuple(tuple(s5_g[k][j - 1].reshape(disc[j][k].shape) if j >= 2 else
                         s5_g[k][0][j].reshape(disc[j][k].shape) for k in range(2)) for j in range(4))
    gw["s5_a_re"], gw["s5_a_im"], gw["s5_log_step"], gw["s5_b_re"], gw["s5_b_im"] = disc_vjp(d_disc)
    gw["s5_c_re"] = jnp.stack([s5_g[0][3], s5_g[1][3]])
    gw["s5_c_im"] = jnp.stack([s5_g[0][4], s5_g[1][4]])
    d_H1 = mm_nt_sum([(du_p, 0, 0), (d_z1, D, Lc)], w["s5_w_in"], T, "l1_in_dx")
    gw["s5_w_in"] = jnp.concatenate([mm_tn(H1, du_p, name="l1_in_dw_u"), mm_tn(H1[Lc:], d_z1, name="l1_in_dw_z")],
                                    axis=1)
    if reducer is not None:
        g1 = g1 + reducer[0]({n: gw.pop(n) for n in LAYER1_MATS})[0, 0]
    d_X1p, d_g1, d_sc1, d_sh1 = rowwise_bwd(f_norm_mod, [X1p], [g1, mod_of(1, 1), mod_of(1, 0)], [d_H1],
                                            [0], [0, 1, 2], T, Lc, "l1_norm_bwd", lat_add=dX2)
    d_gt1_full = jnp.concatenate([jnp.zeros((1, 1, D), F32), d_gt1[None]], axis=0)
    dmod[1] = (d_sh1, d_sc1, d_gt1_full)
    d_X1 = from_segments(d_X1p, Lc)

    d_o, d_z0, d_gt0, gw["mla_w_out"] = mla_post_bwd(d_X1, out0, og, o, p0, mod_of(0, 2), w["mla_w_out"], Lc)
    if reducer is not None:
        tabs = (tabs[0] + reducer[1](d_o)[0, 0],) + tabs[1:]
    d_q, delta_rows = attn_bwd_dq(qb, kb, vb, o, d_o, lse, tabs, Lc)
    dk_p, d_v = attn_bwd_dkv(qb, kb, vb, d_o, lse_rows, delta_rows, Lc)
    d_k, d_kr = heads_unrope(dk_p, tabs, jnp.pad(spread, ((0, HEAD_LANES - QK_ROPE_DIM), (0, 0))),
                             name="l0_k_unrope")
    d_qn = mm_nt(d_q, w_uq_p, name="l0_uq_dx")
    gw["mla_w_uq"] = unpad_heads(mm_tn(qn, d_q, name="l0_uq_dw"), QK_DIM)
    d_kvn = mm_nt(d_k, w_kn_p, name="l0_ukn_dx") + mm_nt(d_v, w_v, name="l0_uv_dx")
    dw_kn = unpad_heads(mm_tn(kvn, d_k, name="l0_ukn_dw"), QK_NOPE_DIM).reshape(KV_LORA_RANK, MLA_HEADS, QK_NOPE_DIM)
    dw_v = mm_tn(kvn, d_v, name="l0_uv_dw").reshape(KV_LORA_RANK, MLA_HEADS, V_HEAD_DIM)
    gw["mla_w_ukv"] = jnp.concatenate([dw_kn, dw_v], axis=-1).reshape(KV_LORA_RANK, -1)
    d_cq, d_qng = rowwise_bwd(f_rms, [cq], [qng], [d_qn], [0], [0], T, 0, "l0_qnorm_bwd")
    d_ckv, d_kvng = rowwise_bwd(f_rms, [ckv], [kvng], [d_kvn], [0], [0], T, 0, "l0_kvnorm_bwd")
    gw["mla_q_norm"] = d_qng.reshape(-1)
    gw["mla_kv_norm"] = d_kvng.reshape(-1)
    o_cq, o_ckv = D, D + Q_LORA_RANK
    o_kr = o_ckv + KV_LORA_RANK
    d_H0 = mm_nt_sum([(d_z0, 0, 0), (d_cq, o_cq, 0), (d_ckv, o_ckv, 0), (d_kr, o_kr, 0)], w["mla_w_in"], T, "l0_in_dx")
    d_head = jnp.concatenate([d_cq, d_ckv, d_kr], axis=1)
    gw["mla_w_in"] = jnp.concatenate([mm_tn(H0, d_head, name="l0_in_dw_head")[:, :P0_HEAD],
                                      mm_tn(H0, d_z0, name="l0_in_dw_z")], axis=1)
    d_X0, d_g0, d_sc0, d_sh0 = rowwise_bwd(f_norm_mod, [X0], [g0, mod_of(0, 1), mod_of(0, 0)], [d_H0],
                                           [0], [0, 1, 2], T, Lc, "l0_norm_bwd", lat_add=d_X1)
    dmod[0] = (d_sh0, d_sc0, d_gt0)
    gw["norm_g"] = jnp.stack([d_g0.reshape(D), d_g1.reshape(D)])
    dx = d_X0[Lc:]
    dmod_arr = jnp.stack([jnp.stack([dmod[i][j][:, 0, :] for j in range(3)], axis=1) for i in range(2)])
    ready = reducer[2](d_X0) if reducer is not None else {}
    return loss, dx, dmod_arr, gw, ready


SHARDED = {
    "mla_w_in": 1, "mla_w_uq": 1, "mla_w_ukv": 1, "mla_w_out": 0,
    "s5_w_in": 1, "s5_w_glu": 0, "s5_w_out": 0, "s5_d": 0, "s5_b_glu": 0,
}
SHARDED_MATS = ["mla_w_in", "mla_w_uq", "mla_w_ukv", "mla_w_out", "s5_w_in", "s5_w_glu", "s5_w_out"]
SHARDED_VECS = ["s5_d", "s5_b_glu"]
REPLICATED = ["norm_g", "mla_q_norm", "mla_kv_norm", "s5_a_re", "s5_a_im", "s5_log_step", "s5_b_re", "s5_b_im",
              "s5_c_re", "s5_c_im", "final_g"]
WEIGHT_ORDER = ["c_ctx", "ada_w", "ada_b", "norm_g", "mla_w_in", "mla_q_norm", "mla_w_uq", "mla_kv_norm", "mla_w_ukv",
                "mla_w_out", "s5_w_in", "s5_a_re", "s5_a_im", "s5_log_step", "s5_b_re", "s5_b_im", "s5_c_re", "s5_c_im",
                "s5_d", "s5_w_glu", "s5_b_glu", "s5_w_out", "final_g"]


P0_HEAD = Q_LORA_RANK + KV_LORA_RANK + QK_ROPE_DIM


P0_WIDTH = 1536


def w_in_to_kernel_order(w):
    pad = jnp.zeros((w.shape[0], P0_WIDTH - w.shape[1]), w.dtype)
    return jnp.concatenate([w[:, P0_HEAD:], w[:, :P0_HEAD], pad], axis=1)


LAYER0_MATS = ["mla_w_in", "mla_w_uq", "mla_w_ukv", "mla_w_out"]
LAYER1_MATS = ["s5_w_in", "s5_w_glu", "s5_w_out"]


def _whole_matrices(names, own_blocks, gathered):
    chip = _chip_index(_coords())
    full = {}
    for n, own, o in zip(names, own_blocks, gathered):
        slot = lax.broadcasted_iota(jnp.int32, (N_CHIP, 1, 1), 0)
        o = jnp.where(slot == chip, own[None], o.reshape((N_CHIP,) + own.shape))
        full[n] = o.reshape(-1, o.shape[-1]) if SHARDED[n] == 0 else o.transpose(1, 0, 2).reshape(o.shape[1], -1)
    return full


def gather_weights(ws):
    mats = [ws[n].astype(BF16) for n in LAYER0_MATS]
    full = _whole_matrices(LAYER0_MATS, mats, gather_halves(mats, "gather_weights"))
    full["mla_w_in"] = w_in_to_kernel_order(full["mla_w_in"])
    return full


def gather_weights_behind(ws):
    mats = [ws[n].astype(BF16) for n in LAYER1_MATS]
    flight, token = exchange_start(
        mats, [jax.ShapeDtypeStruct((N_CHIP,) + m.shape, m.dtype) for m in mats],
        [(f, a, lambda me, peer: None, a, lambda s: (_chip_index(s),)) for a in range(len(mats)) for f in CHIP_FLIPS],
        "gather_l1_start")

    def finish(after):
        own, got = exchange_wait(flight, after, "gather_l1_wait")
        return _whole_matrices(LAYER1_MATS, own, got)

    return token[0, 0], finish


def _grad_slots(gw, names):
    slots = []
    for n in names:
        g = gw[n]
        if SHARDED[n] == 0:
            slots.append(g.reshape(N_CHIP, 2, g.shape[0] // (2 * N_CHIP), g.shape[1]))
        else:
            k, n4 = g.shape
            slots.append(g.reshape(k, N_CHIP, n4 // N_CHIP).transpose(1, 0, 2)
                         .reshape(N_CHIP, 2, k // 2, n4 // N_CHIP))
    return slots


def _to_sibling_half(count):
    return [(CORE_FLIP, i, lambda me, peer: (slice(None), 1 - me[2]), i, lambda s: None) for i in range(count)]


def _to_chips(count):
    return [(f, i, lambda me, peer: (_chip_index(peer),), i, lambda s: (_chip_index(s),))
            for i in range(count) for f in CHIP_FLIPS]


def _place():
    me = _coords()
    return jnp.stack([me[2], _chip_index(me)]).astype(jnp.int32)


def reduce_behind(names):
    state = {}
    count = len(names)

    def begin(gw):
        slots = _grad_slots(gw, names)
        lands = [jax.ShapeDtypeStruct((N_CHIP,) + s.shape[2:], F32) for s in slots]
        state["in"], token = exchange_start(slots, lands, _to_sibling_half(count), "grads_l1_swap_in_start")
        return token

    def middle(after):
        slots, got = exchange_wait(state["in"], after, "grads_l1_swap_in_wait")
        place = _place()
        sums = [pair_add(s, g, place[:1], BF16, f"grads_pair_{n}") for n, s, g in zip(names, slots, got)]
        lands = [jax.ShapeDtypeStruct(s.shape, s.dtype) for s in sums]
        state["out"], token = exchange_start(sums, lands, _to_chips(count), "grads_l1_scatter_start")
        return token

    def end(after):
        sums, parts = exchange_wait(state["out"], after, "grads_l1_scatter_wait")
        place = _place()
        return {n: sum_chips(p, s, place, f"grads_sum_{n}") for n, p, s in zip(names, parts, sums)}

    return begin, middle, end


def reduce_gradients(gw, ready_halves):
    me = _coords()
    place = _place()
    mat_names = [n for n in SHARDED_MATS if n not in ready_halves]
    slots = dict(zip(mat_names, _grad_slots(gw, mat_names)))
    small_names = REPLICATED + SHARDED_VECS
    small, small_offs = pack_flat([gw[n].astype(F32) for n in small_names], F32)
    small = jnp.pad(small, ((0, (-small.shape[0]) % (N_CHIP * 32)), (0, 0)))
    slots["small"] = small.reshape(N_CHIP, 2, -1, 128)
    names = list(slots)
    count = len(names)
    got = exchange([slots[n] for n in names],
                   [jax.ShapeDtypeStruct((N_CHIP,) + slots[n].shape[2:], F32) for n in names],
                   _to_sibling_half(count), [], "grads_swap_in")
    sums = [pair_add(slots[n], g, place[:1], F32 if n == "small" else BF16, f"grads_pair_{n}")
            for n, g in zip(names, got)]
    parts = exchange(sums, [jax.ShapeDtypeStruct(s.shape, s.dtype) for s in sums], _to_chips(count), [],
                     "grads_scatter")
    halves = {n: sum_chips(p, s, place, f"grads_sum_{n}") for n, p, s in zip(names, parts, sums)}
    halves.update(ready_halves)
    all_names = list(halves)
    fulls = exchange(
        [halves[n] for n in all_names], [jax.ShapeDtypeStruct(halves[n].shape, F32) for n in all_names],
        [(CORE_FLIP, i, lambda me, peer: (me[2],), i, lambda s: (s[2],)) for i in range(len(all_names))], [],
        "grads_swap_out", aliases={i: i for i in range(len(all_names))})
    out = {n: f.reshape(-1, f.shape[-1]) for n, f in zip(all_names, fulls)}
    quarter = out.pop("small")
    gather, _ = exchange_start(
        [quarter], [jax.ShapeDtypeStruct((N_CHIP,) + quarter.shape, F32)],
        [(f, 0, lambda me, peer: None, 0, lambda s: (_chip_index(s),)) for f in CHIP_FLIPS],
        "grads_gather_small_start")

    def finish_small(after):
        (own,), (got_small,) = exchange_wait(gather, after, "grads_gather_small_wait")
        slot = lax.broadcasted_iota(jnp.int32, (N_CHIP, 1, 1), 0)
        small_all = jnp.where(slot == _chip_index(me), own[None], got_small)
        vals = unpack_flat(small_all, small_offs, [gw[n].shape for n in small_names])
        res = {}
        for n, v in zip(small_names, vals):
            if n in SHARDED_VECS:
                size = v.shape[0] // N_CHIP
                v = lax.dynamic_slice_in_dim(v, _chip_index(me) * size, size)
            res[n] = v
        return res

    return out, finish_small


def kernel(x, c, ctx, c_ctx, ada_w, ada_b, norm_g, mla_w_in, mla_q_norm, mla_w_uq, mla_kv_norm, mla_w_ukv, mla_w_out, s5_w_in, s5_a_re, s5_a_im, s5_log_step, s5_b_re, s5_b_im, s5_c_re, s5_c_im, s5_d, s5_w_glu, s5_b_glu, s5_w_out, final_g, loss_target, m_c_ctx, m_ada_w, m_ada_b, m_norm_g, m_mla_w_in, m_mla_q_norm, m_mla_w_uq, m_mla_kv_norm, m_mla_w_ukv, m_mla_w_out, m_s5_w_in, m_s5_a_re, m_s5_a_im, m_s5_log_step, m_s5_b_re, m_s5_b_im, m_s5_c_re, m_s5_c_im, m_s5_d, m_s5_w_glu, m_s5_b_glu, m_s5_w_out, m_final_g, v_c_ctx, v_ada_w, v_ada_b, v_norm_g, v_mla_w_in, v_mla_q_norm, v_mla_w_uq, v_mla_kv_norm, v_mla_w_ukv, v_mla_w_out, v_s5_w_in, v_s5_a_re, v_s5_a_im, v_s5_log_step, v_s5_b_re, v_s5_b_im, v_s5_c_re, v_s5_c_im, v_s5_d, v_s5_w_glu, v_s5_b_glu, v_s5_w_out, v_final_g):
    args = dict(locals())
    weights = {n: args[n] for n in WEIGHT_ORDER}
    D = D_MODEL
    xi, yi, ci = _coords()
    chip = 2 * xi + yi
    me = 4 * xi + 2 * yi + ci
    n_col = ada_w.shape[2]

    token, late = gather_weights_behind({n: weights[n][0] for n in LAYER1_MATS})

    c_all = allgather_devices(jnp.pad(c + token, ((0, 7), (0, 0))), "gather_c")[:, 0, :]
    cond = jnp.concatenate([c_all, jnp.broadcast_to(c_ctx[None], (8, D))], axis=0)
    (s_cond,) = rowwise_fwd(lambda v: (_silu(v),), [cond], [], [D], [F32], 16, 0, "cond_silu")
    ada_rows = ada_w.reshape(2 * D, n_col)
    mod_cols = jnp.stack([mm_nn(s_cond, ada_rows, name=f"mod_proj{i}", b_blk=i) for i in range(2)])
    vec_tiles = [jnp.pad(weights[n][0].reshape(-1, 128), ((0, 6), (0, 0))) for n in SHARDED_VECS]
    mod_all, *vec_all = allgather_chips([mod_cols] + vec_tiles, "gather_mod")
    mod_all = mod_all.transpose(1, 2, 0, 3).reshape(2, 16, 3 * D) + ada_b[:, None, :]
    mod_l = lax.dynamic_index_in_dim(mod_all, me, axis=1, keepdims=False)
    mod_c = mod_all[:, 8, :]
    mod = jnp.stack([mod_c.reshape(2, 3, D), mod_l.reshape(2, 3, D)], axis=1)

    w = gather_weights({n: weights[n][0] for n in LAYER0_MATS})
    for n, v in zip(SHARDED_VECS, vec_all):
        w[n] = v[:, :2, :].reshape(-1)
    for n in ["norm_g", "final_g"]:
        w[n] = weights[n]
    for n in ["mla_q_norm", "mla_kv_norm", "s5_a_re", "s5_a_im", "s5_log_step", "s5_b_re", "s5_b_im",
              "s5_c_re", "s5_c_im"]:
        w[n] = weights[n][0]

    loss_me, dx, dmod, gw, ready = local_step(x[0], ctx[0], loss_target[0], mod, w, late, reduce_behind(LAYER1_MATS))

    dmod_rows, loss_all = _gather([dmod.reshape(2, 2, 3 * D), jnp.broadcast_to(loss_me, (8, 128))],
                                  ALL_FLIPS, _dev_index, N_DEV, "gather_dmod")
    loss = functools.reduce(lambda s, d: s + loss_all[d, 0, 0], range(1, N_DEV), loss_all[0, 0, 0])
    dm = jnp.concatenate([dmod_rows[:, :, 1, :], dmod_rows[:, :, 0, :]], axis=0).transpose(1, 0, 2)
    g_ada_b = jnp.sum(dm, axis=1)
    dm_cols = lax.dynamic_slice_in_dim(dm, chip * n_col, n_col, axis=2)
    g_ada_w = jnp.stack([mm_tn(s_cond, dm_cols[i], name=f"mod_proj_dw{i}") for i in range(2)])
    dmc = jnp.sum(dm_cols[:, 8:, :], axis=1)
    dmc8 = jnp.broadcast_to(dmc[:, None, :], (2, 8, n_col))
    g_sc = (mm_nt(dmc8[0], ada_rows, name="mod_proj_dx0", b_rows=D, b_blk=0)[0]
            + mm_nt(dmc8[1], ada_rows, name="mod_proj_dx1", b_rows=D, b_blk=1)[0])
    g_sc_all = allgather_devices(jnp.broadcast_to(g_sc[None], (8, D)), "gather_dcond")[:, 0, :]
    g_silu_cc = g_sc_all[0] + g_sc_all[2] + g_sc_all[4] + g_sc_all[6]
    (g_c_ctx,) = rowwise_bwd(lambda v: (_silu(v),), [jnp.broadcast_to(c_ctx[None], (8, D))], [],
                             [jnp.broadcast_to(g_silu_cc[None], (8, D))], [0], [], 8, 0, "cond_silu_bwd")
    g_c_ctx = g_c_ctx[0]

    grads = {"c_ctx": g_c_ctx, "ada_w": g_ada_w, "ada_b": g_ada_b}
    deltas, new_m, new_v = {}, {}, {}
    small = [n for n in WEIGHT_ORDER if weights[n].size < 50000]

    def update(n):
        shp = weights[n].shape
        w2 = weights[n].reshape(-1, shp[-1])
        d_, m_, v_ = adamw(w2, grads[n].reshape(w2.shape), args["m_" + n].reshape(w2.shape),
                           args["v_" + n].reshape(w2.shape), name=f"adamw_{n}")
        deltas[n], new_m[n], new_v[n] = d_.reshape(shp), m_.reshape(shp), v_.reshape(shp)

    red, finish_small = reduce_gradients(gw, ready)
    update("ada_w")
    for n in SHARDED_MATS:
        grads[n] = red[n].reshape(weights[n].shape)
        update(n)
    red_small = finish_small(deltas[SHARDED_MATS[-1]])
    for n in REPLICATED + SHARDED_VECS:
        grads[n] = red_small[n].reshape(weights[n].shape)
    for n in WEIGHT_ORDER:
        if n not in small and n not in deltas:
            update(n)
    packs = []
    offs = None
    for src in (weights, grads, {n: args["m_" + n] for n in small}, {n: args["v_" + n] for n in small}):
        buf, offs = pack_flat([src[n] for n in small], F32)
        packs.append(buf)
    outs = adamw(*packs, name="adamw_small")
    for res, dst in zip(outs, (deltas, new_m, new_v)):
        for n, val in zip(small, unpack_flat(res, offs, [weights[n].shape for n in small])):
            dst[n] = val

    return (loss, dx[None], *[grads[n] for n in WEIGHT_ORDER], *[deltas[n] for n in WEIGHT_ORDER],
            *[new_m[n] for n in WEIGHT_ORDER], *[new_v[n] for n in WEIGHT_ORDER])
```

```python
import functools
import math

import jax
import jax.numpy as jnp
import numpy as np
from jax import lax
from jax.experimental import pallas as pl
from jax.experimental.pallas import tpu as pltpu

F32 = jnp.float32
BF16 = jnp.bfloat16

D_MODEL = 1024
GRID_W = 64
EPS = 1e-6
MLA_HEADS = 16
QK_NOPE_DIM = 64
QK_ROPE_DIM = 32
V_HEAD_DIM = 64
Q_LORA_RANK = 256
KV_LORA_RANK = 128
QK_DIM = QK_NOPE_DIM + QK_ROPE_DIM
SOFTMAX_SCALE = QK_DIM ** -0.5
ROPE_THETA = 10000.0
S5_GROUP = 16
S5_GROUPS = D_MODEL // S5_GROUP
S5_STATE = 64
S5_LANES = S5_GROUPS * S5_STATE
N_SEG = 8
GROUPS_PER_BLOCK = 8
N_BLOCKS = S5_GROUPS // GROUPS_PER_BLOCK
BLK_CH = GROUPS_PER_BLOCK * S5_GROUP
BLK_ST = GROUPS_PER_BLOCK * S5_STATE

ADAM_LR = 0.001
ADAM_B1 = 0.9
ADAM_B2 = 0.999
ADAM_EPS = 1e-08
ADAM_WD = 0.01
ADAM_STEP = 10

N_DEV = 8
N_CHIP = 4
MESH = pl.DeviceIdType.MESH
VMEM_LIMIT = 52 * 1024 * 1024
ROW_TILE = 256


def _params(sem=None, vmem=None):
    return pltpu.CompilerParams(dimension_semantics=sem, vmem_limit_bytes=vmem)


def mm_nn(a, b, out_dtype=F32, name="mm_nn", b_blk=0):
    M, K = a.shape
    N = b.shape[1]
    tm = math.gcd(ROW_TILE, M)

    def body(a_ref, b_ref, o_ref):
        o_ref[...] = jnp.dot(a_ref[...].astype(BF16), b_ref[...].astype(BF16),
                             preferred_element_type=F32).astype(o_ref.dtype)

    return pl.pallas_call(
        body, out_shape=jax.ShapeDtypeStruct((M, N), out_dtype), grid=(M // tm,),
        in_specs=[pl.BlockSpec((tm, K), lambda i: (i, 0)), pl.BlockSpec((K, N), lambda i: (b_blk, 0))],
        out_specs=pl.BlockSpec((tm, N), lambda i: (i, 0)),
        compiler_params=_params(("parallel",), VMEM_LIMIT), name=name)(a, b)


def mm_nt(a, b, out_dtype=F32, name="mm_nt", b_rows=None, b_blk=0):
    M, N = a.shape
    K = b.shape[0] if b_rows is None else b_rows
    tm = math.gcd(ROW_TILE, M)

    def body(a_ref, b_ref, o_ref):
        o_ref[...] = lax.dot_general(a_ref[...].astype(BF16), b_ref[...].astype(BF16),
                                     (((1,), (1,)), ((), ())),
                                     preferred_element_type=F32).astype(o_ref.dtype)

    return pl.pallas_call(
        body, out_shape=jax.ShapeDtypeStruct((M, K), out_dtype), grid=(M // tm,),
        in_specs=[pl.BlockSpec((tm, N), lambda i: (i, 0)), pl.BlockSpec((K, N), lambda i: (b_blk, 0))],
        out_specs=pl.BlockSpec((tm, K), lambda i: (i, 0)),
        compiler_params=_params(("parallel",), VMEM_LIMIT), name=name)(a, b)


def mm_nt_sum(terms, w, n_rows, name):
    K = w.shape[0]
    tm = math.gcd(ROW_TILE, n_rows, *[t[2] for t in terms])

    def body(*refs):
        i = pl.program_id(0)
        w_ref, o_ref = refs[len(terms)], refs[len(terms) + 1]
        acc = None
        for a_ref, (a, off, first) in zip(refs, terms):
            part = lax.dot_general(a_ref[...].astype(BF16), w_ref[:, off:off + a.shape[1]].astype(BF16), NT_DIMS,
                                   preferred_element_type=F32)
            if first:
                part = jnp.where(i >= first // tm, part, 0.0)
            acc = part if acc is None else acc + part
        o_ref[...] = acc

    def a_spec(a, first):
        skip = first // tm
        return pl.BlockSpec((tm, a.shape[1]), lambda i: (jnp.maximum(i - skip, 0), 0))

    return pl.pallas_call(
        body, out_shape=jax.ShapeDtypeStruct((n_rows, K), F32), grid=(n_rows // tm,),
        in_specs=[a_spec(a, first) for a, _, first in terms] + [pl.BlockSpec(w.shape, lambda i: (0, 0))],
        out_specs=pl.BlockSpec((tm, K), lambda i: (i, 0)),
        compiler_params=_params(("parallel",), VMEM_LIMIT), name=name)(*[t[0] for t in terms], w)


def mm_tn(a, b, name="mm_tn"):
    M, K = a.shape
    N = b.shape[1]
    tn = math.gcd(512, N) if N % 128 == 0 and N > 512 else N

    def body(a_ref, b_ref, o_ref):
        o_ref[...] = lax.dot_general(a_ref[...].astype(BF16), b_ref[...].astype(BF16),
                                     (((0,), (0,)), ((), ())), preferred_element_type=F32)

    return pl.pallas_call(
        body, out_shape=jax.ShapeDtypeStruct((K, N), F32), grid=(N // tn,),
        in_specs=[pl.BlockSpec((M, K), lambda j: (0, 0)), pl.BlockSpec((M, tn), lambda j: (0, j))],
        out_specs=pl.BlockSpec((K, tn), lambda j: (0, j)),
        compiler_params=_params(("parallel",), VMEM_LIMIT), name=name)(a, b)


class Rows:
    def __init__(self, arr, width=None, row_off=0, col_blk=0):
        self.arr = arr
        self.width = arr.shape[1] if width is None else width
        self.row_off = row_off
        self.col_blk = col_blk

    def spec(self, tm):
        ro, cb = self.row_off // tm, self.col_blk
        return pl.BlockSpec((tm, self.width), lambda i: (i + ro, cb))


def _as_rows(x):
    return x if isinstance(x, Rows) else Rows(x)


def _row_tile(n_rows, n_ctx_rows, rows):
    tm = math.gcd(ROW_TILE, n_rows, n_ctx_rows)
    for r in rows:
        tm = math.gcd(tm, r.row_off)
    return tm


def _bc_spec(arr, n_ctx_blocks):
    g, _, d = arr.shape
    if g == 1:
        return pl.BlockSpec((1, 1, d), lambda i: (0, 0, 0))
    return pl.BlockSpec((1, 1, d), lambda i: ((i >= n_ctx_blocks).astype(jnp.int32), 0, 0))


def rowwise_fwd(fn, rows, bcs, out_dims, out_dtypes, n_rows, n_ctx_rows, name):
    rows = [_as_rows(r) for r in rows]
    tm = _row_tile(n_rows, n_ctx_rows, rows)
    ncb = n_ctx_rows // tm
    nr, nb = len(rows), len(bcs)

    def body(*refs):
        vals = [r[...].astype(F32) for r in refs[:nr]] + [b[0].astype(F32) for b in refs[nr:nr + nb]]
        outs = fn(*vals)
        for o_ref, v in zip(refs[nr + nb:], outs):
            o_ref[...] = v.astype(o_ref.dtype)

    outs = pl.pallas_call(
        body,
        out_shape=[jax.ShapeDtypeStruct((n_rows, d), dt) for d, dt in zip(out_dims, out_dtypes)],
        grid=(n_rows // tm,),
        in_specs=[r.spec(tm) for r in rows] + [_bc_spec(b, ncb) for b in bcs],
        out_specs=[pl.BlockSpec((tm, d), lambda i: (i, 0)) for d in out_dims],
        compiler_params=_params(("parallel",), VMEM_LIMIT), name=name)(*[r.arr for r in rows], *bcs)
    return outs


def rowwise_bwd(fn, rows, bcs, cts, diff_rows, diff_bcs, n_rows, n_ctx_rows, name, ct_extra=None, lat_add=None):
    rows = [_as_rows(r) for r in rows]
    cts = [_as_rows(c) for c in cts]
    extra = [_as_rows(ct_extra)] if ct_extra is not None else []
    tm = _row_tile(n_rows, n_ctx_rows, rows + cts + extra)
    ncb = n_ctx_rows // tm
    nr, nb, nc = len(rows), len(bcs), len(cts)
    ndr, ndb = len(diff_rows), len(diff_bcs)
    n_in = nr + nb + nc + len(extra) + (lat_add is not None)

    def body(*refs):
        i = pl.program_id(0)
        rvals = [r[...].astype(F32) for r in refs[:nr]]
        bvals = [b[0].astype(F32) for b in refs[nr:nr + nb]]
        cvals = [c[...].astype(F32) for c in refs[nr + nb:nr + nb + nc]]
        if extra:
            cvals[0] = cvals[0] + refs[nr + nb + nc][...].astype(F32)
        outs = refs[n_in:]

        def f(*d):
            rv, bv = list(rvals), list(bvals)
            for k, idx in enumerate(diff_rows):
                rv[idx] = d[k]
            for k, idx in enumerate(diff_bcs):
                bv[idx] = d[ndr + k]
            return tuple(fn(*rv, *bv))

        primals = [rvals[k] for k in diff_rows] + [bvals[k] for k in diff_bcs]
        _, vjp = jax.vjp(f, *primals)
        grads = list(vjp(tuple(cvals)))
        if lat_add is not None:
            add = refs[n_in - 1][...]
            grads[0] = grads[0] + (add if lat_add.shape[0] == n_rows else jnp.where(i >= ncb, add, 0.0))
        for k in range(ndr):
            outs[k][...] = grads[k].astype(outs[k].dtype)
        for k, idx in enumerate(diff_bcs):
            o_ref = outs[ndr + k]
            first = (i == 0)
            if bcs[idx].shape[0] == 2:
                first = first | (i == ncb)

            @pl.when(first)
            def _(o_ref=o_ref):
                o_ref[...] = jnp.zeros_like(o_ref)

            o_ref[0] += grads[ndr + k]

    out_shape = [jax.ShapeDtypeStruct((n_rows, rows[k].width), F32) for k in diff_rows]
    out_shape += [jax.ShapeDtypeStruct(bcs[k].shape, F32) for k in diff_bcs]
    out_specs = [pl.BlockSpec((tm, rows[k].width), lambda i: (i, 0)) for k in diff_rows]
    out_specs += [_bc_spec(bcs[k], ncb) for k in diff_bcs]
    ins = [r.arr for r in rows] + list(bcs) + [c.arr for c in cts + extra]
    in_specs = [r.spec(tm) for r in rows] + [_bc_spec(b, ncb) for b in bcs] + [c.spec(tm) for c in cts + extra]
    if lat_add is not None:
        ins.append(lat_add)
        skip = ncb if lat_add.shape[0] != n_rows else 0
        in_specs.append(pl.BlockSpec((tm, lat_add.shape[1]), lambda i: (jnp.maximum(i - skip, 0), 0)))
    outs = pl.pallas_call(
        body, out_shape=out_shape, grid=(n_rows // tm,), in_specs=in_specs, out_specs=out_specs,
        compiler_params=_params(("arbitrary",), VMEM_LIMIT), name=name)(*ins)
    return outs


def _rms(x):
    return x * lax.rsqrt(jnp.mean(x * x, axis=-1, keepdims=True) + EPS)


def _sigmoid(x):
    return 0.5 * (jnp.tanh(0.5 * x) + 1.0)


def _silu(x):
    return x * _sigmoid(x)


def _gelu_tanh(x):
    return 0.5 * x * (1.0 + jnp.tanh(math.sqrt(2.0 / math.pi) * (x + 0.044715 * (x * x * x))))


def f_norm_mod(x, g, sc, sh):
    return ((_rms(x) * g) * (1.0 + sc) + sh,)


def f_rms(x, g):
    return (_rms(x) * g,)


def f_gate(o, z):
    return (o * _silu(z),)


def f_s5_act(y, u, d):
    return (_gelu_tanh(y + d * u),)


def f_s5_glu(ya, gl, z, b):
    return (ya * _sigmoid(gl + b) * _silu(z),)


def mla_post_fwd(o, p0, x0, gate, w_out, n_ctx, name="l0_post"):
    n, d = o.shape
    tm = math.gcd(ROW_TILE, n, n_ctx)
    ncb = n_ctx // tm

    def body(o_ref, z_ref, x_ref, gt_ref, w_ref, x1_ref, og_ref, out_ref):
        og = f_gate(o_ref[...], z_ref[...])[0].astype(BF16)
        out = jnp.dot(og, w_ref[...], preferred_element_type=F32)
        og_ref[...] = og
        out_ref[...] = out
        x1_ref[...] = x_ref[...] + gt_ref[0] * out

    row = pl.BlockSpec((tm, d), lambda i: (i, 0))
    return pl.pallas_call(
        body, out_shape=[jax.ShapeDtypeStruct((n, d), F32), jax.ShapeDtypeStruct((n, d), BF16),
                         jax.ShapeDtypeStruct((n, d), F32)],
        grid=(n // tm,),
        in_specs=[row, row, row, _bc_spec(gate, ncb), pl.BlockSpec((d, d), lambda i: (0, 0))],
        out_specs=[row, row, row],
        compiler_params=_params(("parallel",), VMEM_LIMIT), name=name)(o, p0, x0, gate, w_out)


def mla_post_bwd(dx1, out, og, o, p0, gate, w_out, n_ctx, name="l0_post_bwd"):
    n, d = o.shape
    tm = math.gcd(ROW_TILE, n, n_ctx)
    ncb = n_ctx // tm

    def body(dx_ref, out_ref, og_ref, o_ref, z_ref, gt_ref, w_ref, do_ref, dz_ref, dgt_ref, dw_ref):
        i = pl.program_id(0)

        @pl.when(i == 0)
        def _():
            dw_ref[...] = jnp.zeros_like(dw_ref)

        @pl.when((i == 0) | (i == ncb))
        def _():
            dgt_ref[...] = jnp.zeros_like(dgt_ref)

        dx = dx_ref[...]
        dgt_ref[0] += jnp.sum(dx * out_ref[...], axis=0, keepdims=True)
        d_out16 = (gt_ref[0] * dx).astype(BF16)
        dw_ref[...] += lax.dot_general(og_ref[...], d_out16, (((0,), (0,)), ((), ())), preferred_element_type=F32)
        d_og = lax.dot_general(d_out16, w_ref[...], NT_DIMS, preferred_element_type=F32)
        _, gate_vjp = jax.vjp(lambda o_, z_: f_gate(o_, z_), o_ref[...], z_ref[...])
        d_o, d_z = gate_vjp((d_og,))
        do_ref[...] = d_o
        dz_ref[...] = d_z

    row = pl.BlockSpec((tm, d), lambda i: (i, 0))
    mat = pl.BlockSpec((d, d), lambda i: (0, 0))
    return pl.pallas_call(
        body, out_shape=[jax.ShapeDtypeStruct((n, d), F32), jax.ShapeDtypeStruct((n, d), F32),
                         jax.ShapeDtypeStruct(gate.shape, F32), jax.ShapeDtypeStruct((d, d), F32)],
        grid=(n // tm,),
        in_specs=[row, row, row, row, row, _bc_spec(gate, ncb), mat],
        out_specs=[row, row, _bc_spec(gate, ncb), mat],
        compiler_params=_params(("arbitrary",), VMEM_LIMIT), name=name)(dx1, out, og, o, p0, gate, w_out)


def s5_tail(y_ssm, p1, x1p, target, n_ctx, d_vec, b_glu, gate, final_g, w_glu, w_out, name="l1_tail"):
    n, d = y_ssm.shape
    tm = math.gcd(ROW_TILE, n, n_ctx)
    off = n_ctx // tm
    tn_dims = (((0,), (0,)), ((), ()))

    def row_loss(x, g, t):
        e = _rms(x) * g - t
        return 0.5 * (e * e) * (1.0 / d)

    def body(y_ref, u_ref, z_ref, x1_ref, t_ref, d_ref, b_ref, gt_ref, fg_ref, wg_ref, wo_ref,
             l_ref, dx_ref, dy_ref, du_ref, dz_ref, dfg_ref, dgt_ref, db_ref, dd_ref, dwg_ref, dwo_ref):
        @pl.when(pl.program_id(0) == 0)
        def _():
            for r in (l_ref, dfg_ref, dgt_ref, db_ref, dd_ref, dwg_ref, dwo_ref):
                r[...] = jnp.zeros_like(r)

        u, z, tgt, gt = u_ref[...], z_ref[...], t_ref[...], gt_ref[...]
        (ya,), act_vjp = jax.vjp(lambda y_, u_, d_: f_s5_act(y_, u_, d_), y_ref[...], u, d_ref[...])
        ya16 = ya.astype(BF16)
        gl = jnp.dot(ya16, wg_ref[...], preferred_element_type=F32)
        (y3,), glu_vjp = jax.vjp(lambda a_, g_, z_, b_: f_s5_glu(a_, g_, z_, b_), ya, gl, z, b_ref[...])
        y3_16 = y3.astype(BF16)
        out1 = jnp.dot(y3_16, wo_ref[...], preferred_element_type=F32)
        lterm, loss_vjp = jax.vjp(lambda x_, g_: row_loss(x_, g_, tgt), x1_ref[...] + gt * out1, fg_ref[...])
        dx2, dfg = loss_vjp(jnp.ones_like(lterm))
        l_ref[...] += jnp.sum(lterm, axis=0, keepdims=True)
        dfg_ref[...] += dfg
        dx_ref[...] = dx2
        dgt_ref[...] += jnp.sum(dx2 * out1, axis=0, keepdims=True)
        d_out16 = (gt * dx2).astype(BF16)
        dwo_ref[...] += lax.dot_general(y3_16, d_out16, tn_dims, preferred_element_type=F32)
        d_y3 = lax.dot_general(d_out16, wo_ref[...], NT_DIMS, preferred_element_type=F32)
        d_ya, d_gl, d_z, d_b = glu_vjp((d_y3,))
        dz_ref[...] = d_z
        db_ref[...] += d_b
        d_gl16 = d_gl.astype(BF16)
        dwg_ref[...] += lax.dot_general(ya16, d_gl16, tn_dims, preferred_element_type=F32)
        d_ya = d_ya + lax.dot_general(d_gl16, wg_ref[...], NT_DIMS, preferred_element_type=F32)
        d_y, d_u, d_d = act_vjp((d_ya,))
        dy_ref[...] = d_y
        du_ref[...] = d_u
        dd_ref[...] += d_d

    row = pl.BlockSpec((tm, d), lambda i: (i, 0))
    vecs = pl.BlockSpec((1, d), lambda i: (0, 0))
    mat = pl.BlockSpec((d, d), lambda i: (0, 0))
    return pl.pallas_call(
        body,
        out_shape=[jax.ShapeDtypeStruct((1, d), F32)] + [jax.ShapeDtypeStruct((n, d), F32)] * 4
        + [jax.ShapeDtypeStruct((1, d), F32)] * 4 + [jax.ShapeDtypeStruct((d, d), F32)] * 2,
        grid=(n // tm,),
        in_specs=[row, pl.BlockSpec((tm, d), lambda i: (i + off, 0)), pl.BlockSpec((tm, d), lambda i: (i + off, 1)),
                  pl.BlockSpec((tm, d), lambda i: (i + off, 0)), row, vecs, vecs, vecs, vecs, mat, mat],
        out_specs=[vecs, row, row, row, row, vecs, vecs, vecs, vecs, mat, mat],
        compiler_params=_params(("arbitrary",), VMEM_LIMIT), name=name)(
            y_ssm, p1, p1, x1p, target, d_vec, b_glu, gate, final_g, w_glu, w_out)


NT_DIMS = (((1,), (1,)), ((), ()))
HEAD_LANES = 128
N_PAIRS = MLA_HEADS // 2


def _own_lanes(shape, hh):
    lane = lax.broadcasted_iota(jnp.int32, shape, len(shape) - 1)
    return (lane < V_HEAD_DIM) if hh == 0 else (lane >= V_HEAD_DIM)


def _rope_tiles(x, cos, sin_next, sin_prev, inverse):
    width = x.shape[-1]
    reps = width // HEAD_LANES
    c, sn, sp = (jnp.tile(t, (1, reps)) for t in (cos, sin_next, sin_prev))
    if inverse:
        return x * c + pltpu.roll(x * sn, 8, 1) + pltpu.roll(x * sp, width - 8, 1)
    return x * c + pltpu.roll(x, width - 8, 1) * sn + pltpu.roll(x, 8, 1) * sp


def _col_to_row(col):
    n = col.shape[0]
    hi = col.astype(BF16)
    r1 = col - hi.astype(F32)
    mid = r1.astype(BF16)
    lo = (r1 - mid.astype(F32)).astype(BF16)
    lane = lax.broadcasted_iota(jnp.int32, (n, HEAD_LANES), 1)
    terms = jnp.where(lane == 0, hi, jnp.where(lane == 1, mid, jnp.where(lane == 2, lo, jnp.zeros_like(hi))))
    eye = (lax.broadcasted_iota(jnp.int32, (n, n), 0) == lax.broadcasted_iota(jnp.int32, (n, n), 1)).astype(BF16)
    rows = lax.dot_general(terms, eye, (((0,), (0,)), ((), ())), preferred_element_type=F32)
    return rows[0:1] + rows[1:2] + rows[2:3]


def attn_fwd(qb, kb, vb, n_ctx):
    T = qb.shape[0]
    tq = math.gcd(ROW_TILE, n_ctx)
    nq, ncb = T // tq, n_ctx // tq

    def body(q_ref, k_ref, v_ref, o_ref, lse_ref, lse_row_ref):
        qi = pl.program_id(1)

        def rows(n_keys):
            v = v_ref[:n_keys, :]
            outs = []
            for hh in range(2):
                hs = slice(hh * HEAD_LANES, (hh + 1) * HEAD_LANES)
                s = lax.dot_general(q_ref[:, hs], k_ref[:n_keys, hs], NT_DIMS,
                                    preferred_element_type=F32) * SOFTMAX_SCALE
                m = jnp.max(s, axis=-1, keepdims=True)
                p = jnp.exp(s - m)
                l = jnp.sum(p, axis=-1, keepdims=True)
                outs.append(jnp.dot(p.astype(BF16), v, preferred_element_type=F32) / l)
                lse = m + jnp.log(l)
                lse_ref[hh] = lse
                lse_row_ref[hh] = _col_to_row(lse)
            o_ref[...] = jnp.where(_own_lanes(outs[0].shape, 0), outs[0], outs[1])

        pl.when(qi < ncb)(lambda: rows(n_ctx))
        pl.when(qi >= ncb)(lambda: rows(T))

    return pl.pallas_call(
        body,
        out_shape=[jax.ShapeDtypeStruct((T, MLA_HEADS * V_HEAD_DIM), F32),
                   jax.ShapeDtypeStruct((MLA_HEADS, T, 1), F32), jax.ShapeDtypeStruct((MLA_HEADS, 1, T), F32)],
        grid=(N_PAIRS, nq),
        in_specs=[pl.BlockSpec((tq, 2 * HEAD_LANES), lambda h, i: (i, h)),
                  pl.BlockSpec((T, 2 * HEAD_LANES), lambda h, i: (0, h)),
                  pl.BlockSpec((T, 2 * V_HEAD_DIM), lambda h, i: (0, h))],
        out_specs=[pl.BlockSpec((tq, 2 * V_HEAD_DIM), lambda h, i: (i, h)),
                   pl.BlockSpec((2, tq, 1), lambda h, i: (h, i, 0)),
                   pl.BlockSpec((2, 1, tq), lambda h, i: (h, 0, i))],
        compiler_params=_params(("parallel", "parallel"), VMEM_LIMIT), name="attn_fwd")(qb, kb, vb)


def attn_bwd_dq(qb, kb, vb, o, do, lse, tabs, n_ctx):
    T = qb.shape[0]
    tq = math.gcd(ROW_TILE, n_ctx)
    nq, ncb = T // tq, n_ctx // tq

    def body(q_ref, k_ref, v_ref, o_ref, do_ref, lse_ref, c_ref, sn_ref, sp_ref, dq_ref, delta_ref):
        qi = pl.program_id(1)

        def rows(n_keys):
            v = v_ref[:n_keys, :]
            dqs = []
            for hh in range(2):
                hs = slice(hh * HEAD_LANES, (hh + 1) * HEAD_LANES)
                k = k_ref[:n_keys, hs]
                do = jnp.where(_own_lanes(do_ref.shape, hh), do_ref[...], 0.0)
                delta = jnp.sum(do * o_ref[...], axis=-1, keepdims=True)
                s = lax.dot_general(q_ref[:, hs], k, NT_DIMS, preferred_element_type=F32) * SOFTMAX_SCALE
                p = jnp.exp(s - lse_ref[hh])
                dp = lax.dot_general(do.astype(BF16), v, NT_DIMS, preferred_element_type=F32)
                ds = p * (dp - delta) * SOFTMAX_SCALE
                dqs.append(jnp.dot(ds.astype(BF16), k, preferred_element_type=F32))
                delta_ref[hh] = _col_to_row(delta)
            dq = jnp.concatenate(dqs, axis=1)
            dq_ref[...] = _rope_tiles(dq, c_ref[...], sn_ref[...], sp_ref[...], True).astype(BF16)

        pl.when(qi < ncb)(lambda: rows(n_ctx))
        pl.when(qi >= ncb)(lambda: rows(T))

    tab = pl.BlockSpec((tq, HEAD_LANES), lambda h, i: (i, 0))
    return pl.pallas_call(
        body,
        out_shape=[jax.ShapeDtypeStruct((T, MLA_HEADS * HEAD_LANES), BF16),
                   jax.ShapeDtypeStruct((MLA_HEADS, 1, T), F32)],
        grid=(N_PAIRS, nq),
        in_specs=[pl.BlockSpec((tq, 2 * HEAD_LANES), lambda h, i: (i, h)),
                  pl.BlockSpec((T, 2 * HEAD_LANES), lambda h, i: (0, h)),
                  pl.BlockSpec((T, 2 * V_HEAD_DIM), lambda h, i: (0, h)),
                  pl.BlockSpec((tq, 2 * V_HEAD_DIM), lambda h, i: (i, h)),
                  pl.BlockSpec((tq, 2 * V_HEAD_DIM), lambda h, i: (i, h)),
                  pl.BlockSpec((2, tq, 1), lambda h, i: (h, i, 0)), tab, tab, tab],
        out_specs=[pl.BlockSpec((tq, 2 * HEAD_LANES), lambda h, i: (i, h)),
                   pl.BlockSpec((2, 1, tq), lambda h, i: (h, 0, i))],
        compiler_params=_params(("parallel", "parallel"), VMEM_LIMIT), name="attn_bwd_dq")(
            qb, kb, vb, o, do, lse, *tabs)


def attn_bwd_dkv(qb, kb, vb, do, lse_rows, delta_rows, n_ctx):
    T = qb.shape[0]
    tq = math.gcd(ROW_TILE, n_ctx)
    nq, ncb = T // tq, n_ctx // tq

    def body(q_ref, do_ref, lse_ref, delta_ref, k_ref, v_ref, dk_ref, dv_ref):
        kj = pl.program_id(1)

        def cols(first):
            v = v_ref[...]
            do_all = do_ref[first:, :]
            dv = None
            for hh in range(2):
                hs = slice(hh * HEAD_LANES, (hh + 1) * HEAD_LANES)
                k = k_ref[:, hs]
                q = q_ref[first:, hs]
                do16 = jnp.where(_own_lanes(do_all.shape, hh), do_all, 0.0).astype(BF16)
                st = lax.dot_general(k, q, NT_DIMS, preferred_element_type=F32) * SOFTMAX_SCALE
                pt = jnp.exp(st - lse_ref[hh, :, first:])
                dv_h = jnp.dot(pt.astype(BF16), do16, preferred_element_type=F32)
                dv = dv_h if dv is None else dv + dv_h
                dpt = lax.dot_general(v, do16, NT_DIMS, preferred_element_type=F32)
                dst = pt * (dpt - delta_ref[hh, :, first:]) * SOFTMAX_SCALE
                dk_ref[:, hs] = jnp.dot(dst.astype(BF16), q, preferred_element_type=F32)
            dv_ref[...] = dv

        pl.when(kj < ncb)(lambda: cols(0))
        pl.when(kj >= ncb)(lambda: cols(n_ctx))

    return pl.pallas_call(
        body,
        out_shape=[jax.ShapeDtypeStruct((T, MLA_HEADS * HEAD_LANES), F32),
                   jax.ShapeDtypeStruct((T, MLA_HEADS * V_HEAD_DIM), F32)],
        grid=(N_PAIRS, nq),
        in_specs=[pl.BlockSpec((T, 2 * HEAD_LANES), lambda h, j: (0, h)),
                  pl.BlockSpec((T, 2 * V_HEAD_DIM), lambda h, j: (0, h)),
                  pl.BlockSpec((2, 1, T), lambda h, j: (h, 0, 0)),
                  pl.BlockSpec((2, 1, T), lambda h, j: (h, 0, 0)),
                  pl.BlockSpec((tq, 2 * HEAD_LANES), lambda h, j: (j, h)),
                  pl.BlockSpec((tq, 2 * V_HEAD_DIM), lambda h, j: (j, h))],
        out_specs=[pl.BlockSpec((tq, 2 * HEAD_LANES), lambda h, j: (j, h)),
                   pl.BlockSpec((tq, 2 * V_HEAD_DIM), lambda h, j: (j, h))],
        compiler_params=_params(("parallel", "parallel"), VMEM_LIMIT), name="attn_bwd_dkv")(
            qb, do, lse_rows, delta_rows, kb, vb)


def _split_bf16(x):
    hi = x.astype(BF16)
    return hi, (x - hi.astype(F32)).astype(BF16)


def q_heads(qn, w_uq_p, tabs, name="l0_uq"):
    T, K = qn.shape
    N = w_uq_p.shape[1]
    tm = math.gcd(ROW_TILE, T)

    def body(a_ref, w_ref, c_ref, sn_ref, sp_ref, o_ref):
        acc = jnp.dot(a_ref[...], w_ref[...], preferred_element_type=F32)
        o_ref[...] = _rope_tiles(acc, c_ref[...], sn_ref[...], sp_ref[...], False).astype(BF16)

    tab = pl.BlockSpec((tm, HEAD_LANES), lambda i: (i, 0))
    return pl.pallas_call(
        body, out_shape=jax.ShapeDtypeStruct((T, N), BF16), grid=(T // tm,),
        in_specs=[pl.BlockSpec((tm, K), lambda i: (i, 0)), pl.BlockSpec((K, N), lambda i: (0, 0)), tab, tab, tab],
        out_specs=pl.BlockSpec((tm, N), lambda i: (i, 0)),
        compiler_params=_params(("parallel",), VMEM_LIMIT), name=name)(qn, w_uq_p, *tabs)


def kv_heads(kvn, w_kn_p, w_v, kr, spread, tabs, name="l0_ukv"):
    T, K = kvn.shape
    N = w_kn_p.shape[1]
    NV = w_v.shape[1]
    tm = math.gcd(ROW_TILE, T)

    def body(a_ref, wk_ref, wv_ref, kr_ref, e_ref, c_ref, sn_ref, sp_ref, k_ref, v_ref):
        a = a_ref[...]
        hi, lo = _split_bf16(kr_ref[...])
        acc = (jnp.dot(a, wk_ref[...], preferred_element_type=F32)
               + jnp.dot(hi, e_ref[...], preferred_element_type=F32)
               + jnp.dot(lo, e_ref[...], preferred_element_type=F32))
        k_ref[...] = _rope_tiles(acc, c_ref[...], sn_ref[...], sp_ref[...], False).astype(BF16)
        v_ref[...] = jnp.dot(a, wv_ref[...], preferred_element_type=F32).astype(BF16)

    tab = pl.BlockSpec((tm, HEAD_LANES), lambda i: (i, 0))
    return pl.pallas_call(
        body, out_shape=[jax.ShapeDtypeStruct((T, N), BF16), jax.ShapeDtypeStruct((T, NV), BF16)], grid=(T // tm,),
        in_specs=[pl.BlockSpec((tm, K), lambda i: (i, 0)), pl.BlockSpec((K, N), lambda i: (0, 0)),
                  pl.BlockSpec((K, NV), lambda i: (0, 0)), pl.BlockSpec((tm, QK_ROPE_DIM), lambda i: (i, 0)),
                  pl.BlockSpec((QK_ROPE_DIM, N), lambda i: (0, 0)), tab, tab, tab],
        out_specs=[pl.BlockSpec((tm, N), lambda i: (i, 0)), pl.BlockSpec((tm, NV), lambda i: (i, 0))],
        compiler_params=_params(("parallel",), VMEM_LIMIT), name=name)(kvn, w_kn_p, w_v, kr, spread, *tabs)


def heads_unrope(d, tabs, spread=None, name="unrope"):
    T, N = d.shape
    tm = math.gcd(ROW_TILE, T)

    def body(*refs):
        if spread is None:
            d_ref, c_ref, sn_ref, sp_ref, o_ref = refs
        else:
            d_ref, c_ref, sn_ref, sp_ref, e_ref, o_ref, kr_ref = refs
        g = _rope_tiles(d_ref[...], c_ref[...], sn_ref[...], sp_ref[...], True)
        o_ref[...] = g.astype(BF16)
        if spread is not None:
            hi, lo = _split_bf16(g)
            kr_ref[...] = (lax.dot_general(hi, e_ref[...], NT_DIMS, preferred_element_type=F32)
                           + lax.dot_general(lo, e_ref[...], NT_DIMS, preferred_element_type=F32))

    tab = pl.BlockSpec((tm, HEAD_LANES), lambda i: (i, 0))
    row = pl.BlockSpec((tm, N), lambda i: (i, 0))
    ins, in_specs = [d, *tabs], [row, tab, tab, tab]
    out_shape, out_specs = [jax.ShapeDtypeStruct((T, N), BF16)], [row]
    if spread is not None:
        ins.append(spread)
        in_specs.append(pl.BlockSpec(spread.shape, lambda i: (0, 0)))
        out_shape.append(jax.ShapeDtypeStruct((T, spread.shape[0]), F32))
        out_specs.append(pl.BlockSpec((tm, spread.shape[0]), lambda i: (i, 0)))
    return pl.pallas_call(
        body, out_shape=out_shape, grid=(T // tm,), in_specs=in_specs, out_specs=out_specs,
        compiler_params=_params(("parallel",), VMEM_LIMIT), name=name)(*ins)


def _cmul(ar, ai, br, bi):
    return ar * br - ai * bi, ar * bi + ai * br


def s5_chain(finals, s0, a, n_steps, reverse, name):
    W = finals.shape[-1]
    first = N_SEG - 1 if reverse else 0

    def body(f_ref, s0_ref, a_ref, c_ref):
        pr, pi = jnp.ones((1, W), F32), jnp.zeros((1, W), F32)
        br, bi = a_ref[0], a_ref[1]
        n = n_steps
        while n:
            if n & 1:
                pr, pi = _cmul(pr, pi, br, bi)
            br, bi = _cmul(br, bi, br, bi)
            n >>= 1
        fr, fi = f_ref[0], f_ref[1]
        row = lax.broadcasted_iota(jnp.int32, (N_SEG, W), 0)
        s0r = jnp.broadcast_to(s0_ref[0], (N_SEG, W))
        s0i = jnp.broadcast_to(s0_ref[1], (N_SEG, W))
        cr = jnp.where(row == first, s0r, 0.0)
        ci = jnp.where(row == first, s0i, 0.0)
        shift = N_SEG - 1 if reverse else 1
        for _ in range(N_SEG - 1):
            mr, mi = _cmul(pr, pi, cr, ci)
            tr = pltpu.roll(fr + mr, shift, 0)
            ti = pltpu.roll(fi + mi, shift, 0)
            cr = jnp.where(row == first, s0r, tr)
            ci = jnp.where(row == first, s0i, ti)
        c_ref[0] = cr
        c_ref[1] = ci

    return pl.pallas_call(body, out_shape=jax.ShapeDtypeStruct((2, N_SEG, W), F32), name=name)(finals, s0, a)


def _scan_chunk(bur, bui, st_ref, a_ref, n_steps, reverse):
    for lc in range(S5_LANES // BLK_ST):
        sl = slice(lc * BLK_ST, (lc + 1) * BLK_ST)
        lr = jnp.broadcast_to(a_ref[0, :, sl], (N_SEG, BLK_ST))
        li = jnp.broadcast_to(a_ref[1, :, sl], (N_SEG, BLK_ST))

        def step(jj, carry, sl=sl, lr=lr, li=li):
            sr, si = carry
            j = (n_steps - 1 - jj) if reverse else jj
            r0 = pl.multiple_of(j * N_SEG, N_SEG)
            nr = lr * sr - li * si + bur[pl.ds(r0, N_SEG), sl]
            ni = lr * si + li * sr + bui[pl.ds(r0, N_SEG), sl]
            bur[pl.ds(r0, N_SEG), sl] = nr
            bui[pl.ds(r0, N_SEG), sl] = ni
            return nr, ni

        sr, si = lax.fori_loop(0, n_steps, step, (st_ref[0, :, sl], st_ref[1, :, sl]))
        st_ref[0, :, sl] = sr
        st_ref[1, :, sl] = si


def _project_in(x16, w_re, w_im, bur, bui, adjoint):
    for gb in range(N_BLOCKS):
        xb = x16[:, gb * BLK_CH:(gb + 1) * BLK_CH]
        sl = slice(gb * BLK_ST, (gb + 1) * BLK_ST)
        if adjoint:
            dn = (((1,), (1,)), ((), ()))
            bur[:, sl] = lax.dot_general(xb, w_re[gb], dn, preferred_element_type=F32)
            bui[:, sl] = -lax.dot_general(xb, w_im[gb], dn, preferred_element_type=F32)
        else:
            bur[:, sl] = jnp.dot(xb, w_re[gb], preferred_element_type=F32)
            bui[:, sl] = jnp.dot(xb, w_im[gb], preferred_element_type=F32)


def s5_scan(act, w_re, w_im, a, init, *, reverse, adjoint=False, c_re=None, c_im=None, add=None,
            want_ckpt=False, rows=None, name):
    act_off, N = rows if rows is not None else (0, act.shape[0])
    R = math.gcd(ROW_TILE, N, act_off)
    nch, jc = N // R, R // N_SEG
    with_out = c_re is not None

    def chunk(i):
        return (nch - 1 - i) if reverse else i

    def body(*refs):
        act_ref, wre_ref, wim_ref, a_ref, init_ref = refs[:5]
        k = 5
        if with_out:
            cre_ref, cim_ref = refs[k:k + 2]
            k += 2
        if add is not None:
            add_ref = refs[k]
            k += 1
        if with_out:
            out_ref = refs[k]
            k += 1
        if want_ckpt:
            ck_ref = refs[k]
            k += 1
        fin_ref, bur, bui = refs[k:k + 3]

        @pl.when(pl.program_id(0) == 0)
        def _():
            fin_ref[...] = init_ref[...]

        if want_ckpt:
            ck_ref[0] = fin_ref[...]
        _project_in(act_ref[...].astype(BF16), wre_ref, wim_ref, bur, bui, adjoint)
        _scan_chunk(bur, bui, fin_ref, a_ref, jc, reverse)
        if with_out:
            for gb in range(N_BLOCKS):
                sl = slice(gb * BLK_ST, (gb + 1) * BLK_ST)
                y = (jnp.dot(bur[:, sl].astype(BF16), cre_ref[gb], preferred_element_type=F32)
                     - jnp.dot(bui[:, sl].astype(BF16), cim_ref[gb], preferred_element_type=F32))
                cs = slice(gb * BLK_CH, (gb + 1) * BLK_CH)
                if add is not None:
                    y = y + add_ref[:, cs]
                out_ref[:, cs] = y

    row_spec = pl.BlockSpec((R, D_MODEL), lambda i: (chunk(i), 0))
    act_spec = pl.BlockSpec((R, D_MODEL), lambda i: (chunk(i) + act_off // R, 0))
    w_spec = pl.BlockSpec(w_re.shape, lambda i: (0, 0, 0))
    st_spec = pl.BlockSpec((2, N_SEG, S5_LANES), lambda i: (0, 0, 0))
    ins = [act, w_re, w_im, a, init]
    in_specs = [act_spec, w_spec, w_spec, pl.BlockSpec((2, 1, S5_LANES), lambda i: (0, 0, 0)), st_spec]
    if with_out:
        ins += [c_re, c_im]
        in_specs += [pl.BlockSpec(c_re.shape, lambda i: (0, 0, 0))] * 2
    if add is not None:
        ins.append(add)
        in_specs.append(row_spec)
    out_shape, out_specs = [], []
    if with_out:
        out_shape.append(jax.ShapeDtypeStruct((N, D_MODEL), F32))
        out_specs.append(row_spec)
    if want_ckpt:
        out_shape.append(jax.ShapeDtypeStruct((nch, 2, N_SEG, S5_LANES), F32))
        out_specs.append(pl.BlockSpec((1, 2, N_SEG, S5_LANES), lambda i: (chunk(i), 0, 0, 0)))
    out_shape.append(jax.ShapeDtypeStruct((2, N_SEG, S5_LANES), F32))
    out_specs.append(st_spec)
    res = pl.pallas_call(
        body, out_shape=out_shape, grid=(nch,), in_specs=in_specs, out_specs=out_specs,
        scratch_shapes=[pltpu.VMEM((R, S5_LANES), F32), pltpu.VMEM((R, S5_LANES), F32)],
        compiler_params=_params(("arbitrary",), VMEM_LIMIT), name=name)(*ins)
    res = list(res)
    out = res.pop(0) if with_out else None
    ckpt = res.pop(0) if want_ckpt else None
    return out, ckpt, res[0]


def s5_grads(dy, u, ckpt, b_re, b_im, c_re, c_im, lam, init_adj, *, reverse, add=None, u_off=0, name):
    N = dy.shape[0]
    R = math.gcd(ROW_TILE, N, u_off)
    nch, jc = N // R, R // N_SEG
    W = S5_LANES

    def chunk(i):
        return i if reverse else (nch - 1 - i)

    def body(*refs):
        dy_ref, u_ref, ck_ref, bre_ref, bim_ref, cre_ref, cim_ref, lam_ref, init_ref = refs[:9]
        k = 9
        if add is not None:
            add_ref = refs[k]
            k += 1
        du_ref, dlam_ref, dbre_ref, dbim_ref, dcre_ref, dcim_ref, fin_ref = refs[k:k + 7]
        sr_buf, si_buf, er_buf, ei_buf, st_buf = refs[k + 7:k + 12]

        @pl.when(pl.program_id(0) == 0)
        def _():
            fin_ref[...] = init_ref[...]
            dlam_ref[...] = jnp.zeros_like(dlam_ref)
            dbre_ref[...] = jnp.zeros_like(dbre_ref)
            dbim_ref[...] = jnp.zeros_like(dbim_ref)
            dcre_ref[...] = jnp.zeros_like(dcre_ref)
            dcim_ref[...] = jnp.zeros_like(dcim_ref)

        u16 = u_ref[...].astype(BF16)
        dy16 = dy_ref[...].astype(BF16)
        st_buf[...] = ck_ref[0]
        _project_in(u16, bre_ref, bim_ref, sr_buf, si_buf, False)
        _scan_chunk(sr_buf, si_buf, st_buf, lam_ref, jc, reverse)
        _project_in(dy16, cre_ref, cim_ref, er_buf, ei_buf, True)
        for lc in range(W // BLK_ST):
            sl = slice(lc * BLK_ST, (lc + 1) * BLK_ST)
            lr = jnp.broadcast_to(lam_ref[0, :, sl], (N_SEG, BLK_ST))
            li = jnp.broadcast_to(lam_ref[1, :, sl], (N_SEG, BLK_ST))

            def one(r0, spr, spi, carry, sl=sl, lr=lr, li=li):
                gr, gi, ar, ai = carry
                nr = er_buf[pl.ds(r0, N_SEG), sl] + lr * gr + li * gi
                ni = ei_buf[pl.ds(r0, N_SEG), sl] + lr * gi - li * gr
                er_buf[pl.ds(r0, N_SEG), sl] = nr
                ei_buf[pl.ds(r0, N_SEG), sl] = ni
                return nr, ni, ar + spr * nr + spi * ni, ai + spr * ni - spi * nr

            def step(ff, carry, sl=sl, one=one):
                f = jc - 1 - ff
                j = (jc - 1 - f) if reverse else f
                jp = (j + 1) if reverse else (j - 1)
                r0 = pl.multiple_of(j * N_SEG, N_SEG)
                p0 = pl.multiple_of(jp * N_SEG, N_SEG)
                return one(r0, sr_buf[pl.ds(p0, N_SEG), sl], si_buf[pl.ds(p0, N_SEG), sl], carry)

            carry = (fin_ref[0, :, sl], fin_ref[1, :, sl], dlam_ref[0, :, sl], dlam_ref[1, :, sl])
            carry = lax.fori_loop(0, jc - 1, step, carry)
            r_first = (jc - 1) * N_SEG if reverse else 0
            gr, gi, ar, ai = one(r_first, ck_ref[0, 0, :, sl], ck_ref[0, 1, :, sl], carry)
            fin_ref[0, :, sl] = gr
            fin_ref[1, :, sl] = gi
            dlam_ref[0, :, sl] = ar
            dlam_ref[1, :, sl] = ai
        tn = (((0,), (0,)), ((), ()))
        nt = (((1,), (1,)), ((), ()))
        for gb in range(N_BLOCKS):
            sl = slice(gb * BLK_ST, (gb + 1) * BLK_ST)
            cs = slice(gb * BLK_CH, (gb + 1) * BLK_CH)
            gr16 = er_buf[:, sl].astype(BF16)
            gi16 = ei_buf[:, sl].astype(BF16)
            du = (lax.dot_general(gr16, bre_ref[gb], nt, preferred_element_type=F32)
                  + lax.dot_general(gi16, bim_ref[gb], nt, preferred_element_type=F32))
            if add is not None:
                du = du + add_ref[:, cs]
            du_ref[:, cs] = du
            ub, dyb = u16[:, cs], dy16[:, cs]
            dbre_ref[gb] += lax.dot_general(ub, gr16, tn, preferred_element_type=F32)
            dbim_ref[gb] += lax.dot_general(ub, gi16, tn, preferred_element_type=F32)
            dcre_ref[gb] += lax.dot_general(sr_buf[:, sl].astype(BF16), dyb, tn, preferred_element_type=F32)
            dcim_ref[gb] -= lax.dot_general(si_buf[:, sl].astype(BF16), dyb, tn, preferred_element_type=F32)

    row_spec = pl.BlockSpec((R, D_MODEL), lambda i: (chunk(i), 0))
    st_spec = pl.BlockSpec((2, N_SEG, W), lambda i: (0, 0, 0))
    wb_spec = pl.BlockSpec(b_re.shape, lambda i: (0, 0, 0))
    wc_spec = pl.BlockSpec(c_re.shape, lambda i: (0, 0, 0))
    ins = [dy, u, ckpt, b_re, b_im, c_re, c_im, lam, init_adj]
    u_spec = pl.BlockSpec((R, D_MODEL), lambda i: (chunk(i) + u_off // R, 0))
    in_specs = [row_spec, u_spec, pl.BlockSpec((1, 2, N_SEG, W), lambda i: (chunk(i), 0, 0, 0)),
                wb_spec, wb_spec, wc_spec, wc_spec, pl.BlockSpec((2, 1, W), lambda i: (0, 0, 0)), st_spec]
    if add is not None:
        ins.append(add)
        in_specs.append(row_spec)
    out_shape = [jax.ShapeDtypeStruct((N, D_MODEL), F32), jax.ShapeDtypeStruct((2, N_SEG, W), F32),
                 jax.ShapeDtypeStruct(b_re.shape, F32), jax.ShapeDtypeStruct(b_re.shape, F32),
                 jax.ShapeDtypeStruct(c_re.shape, F32), jax.ShapeDtypeStruct(c_re.shape, F32),
                 jax.ShapeDtypeStruct((2, N_SEG, W), F32)]
    out_specs = [row_spec, st_spec, wb_spec, wb_spec, wc_spec, wc_spec, st_spec]
    return pl.pallas_call(
        body, out_shape=out_shape, grid=(nch,), in_specs=in_specs, out_specs=out_specs,
        scratch_shapes=[pltpu.VMEM((R, W), F32) for _ in range(4)] + [pltpu.VMEM((2, N_SEG, W), F32)],
        compiler_params=_params(("arbitrary",), VMEM_LIMIT), name=name)(*ins)


def adamw(w, g, m, v, name="adamw"):
    n, d = w.shape
    lanes = -(-d // 128) * 128
    tm = n
    while tm * lanes * 4 > (1 << 20) and tm % 16 == 0:
        tm //= 2
    c1 = 1.0 - ADAM_B1 ** ADAM_STEP
    c2 = 1.0 - ADAM_B2 ** ADAM_STEP

    def body(w_ref, g_ref, m_ref, v_ref, d_ref, nm_ref, nv_ref):
        g_ = g_ref[...]
        m_ = ADAM_B1 * m_ref[...] + (1.0 - ADAM_B1) * g_
        v_ = ADAM_B2 * v_ref[...] + (1.0 - ADAM_B2) * (g_ * g_)
        d_ref[...] = -ADAM_LR * ((m_ / c1) / (jnp.sqrt(v_ / c2) + ADAM_EPS) + ADAM_WD * w_ref[...])
        nm_ref[...] = m_
        nv_ref[...] = v_

    spec = pl.BlockSpec((tm, d), lambda i: (i, 0))
    return pl.pallas_call(
        body, out_shape=[jax.ShapeDtypeStruct((n, d), F32)] * 3, grid=(n // tm,),
        in_specs=[spec] * 4, out_specs=[spec] * 3,
        compiler_params=_params(("parallel",), VMEM_LIMIT), name=name)(w, g, m, v)


def _coords():
    return lax.axis_index("x"), lax.axis_index("y"), lax.axis_index("c")


def exchange(arrays, out_shapes, remote, local, name, aliases=None):
    n_in, n_out, n_rem, n_loc = len(arrays), len(out_shapes), len(remote), len(local)

    def at(ref, idx):
        return ref if idx is None else ref.at[idx]

    def body(*refs):
        ins, outs = refs[:n_in], refs[n_in:n_in + n_out]
        send_sems, recv_sems, local_sems = refs[n_in + n_out:]
        me = _coords()
        sends, recvs = [], []
        for k, (flip, ii, src_at, oi, dst_at) in enumerate(remote):
            peer = (me[0] ^ flip[0], me[1] ^ flip[1], me[2] ^ flip[2])
            src = at(ins[ii], src_at(me, peer))
            sends.append(pltpu.make_async_remote_copy(
                src_ref=src, dst_ref=at(outs[oi], dst_at(me)), send_sem=send_sems.at[k], recv_sem=recv_sems.at[k],
                device_id=peer, device_id_type=MESH))
            recvs.append(pltpu.make_async_remote_copy(
                src_ref=src, dst_ref=at(outs[oi], dst_at(peer)), send_sem=send_sems.at[k], recv_sem=recv_sems.at[k],
                device_id=peer, device_id_type=MESH))
        locs = [pltpu.make_async_copy(at(ins[ii], src_at(me)), at(outs[oi], dst_at(me)), local_sems.at[k])
                for k, (ii, src_at, oi, dst_at) in enumerate(local)]
        for cp in locs + sends:
            cp.start()
        for cp in recvs:
            cp.wait_recv()
        for cp in sends:
            cp.wait_send()
        for cp in locs:
            cp.wait()

    hbm = pl.BlockSpec(memory_space=pl.ANY)
    return pl.pallas_call(
        body, out_shape=list(out_shapes), in_specs=[hbm] * n_in, out_specs=[hbm] * n_out,
        scratch_shapes=[pltpu.SemaphoreType.DMA((n_rem,)), pltpu.SemaphoreType.DMA((n_rem,)),
                        pltpu.SemaphoreType.DMA((max(n_loc, 1),))],
        input_output_aliases=aliases or {}, name=name)(*arrays)


ALL_FLIPS = [(dx, dy, dc) for dx in (0, 1) for dy in (0, 1) for dc in (0, 1)][1:]
CHIP_FLIPS = [(1, 0, 0), (0, 1, 0), (1, 1, 0)]
CORE_FLIP = (0, 0, 1)


def _dev_index(p):
    return 4 * p[0] + 2 * p[1] + p[2]


def _chip_index(p):
    return 2 * p[0] + p[1]


def _gather(xs, flips, index, n, name):
    arrays = [x[None] for x in xs]
    outs = [jax.ShapeDtypeStruct((n,) + x.shape, x.dtype) for x in xs]
    remote = [(f, a, lambda me, peer: (0,), a, lambda s: (index(s),)) for a in range(len(xs)) for f in flips]
    local = [(a, lambda me: (0,), a, lambda me: (index(me),)) for a in range(len(xs))]
    return exchange(arrays, outs, remote, local, name)


def allgather_devices(x, name):
    return _gather([x], ALL_FLIPS, _dev_index, N_DEV, name)[0]


def allgather_chips(xs, name):
    return _gather(xs, CHIP_FLIPS, _chip_index, N_CHIP, name)


def gather_halves(xs, name):
    n = len(xs)
    nk = n * len(CHIP_FLIPS)

    def body(*refs):
        ins, outs = refs[:n], refs[n:2 * n]
        ici_send, ici_recv, d2d_send, d2d_recv = refs[2 * n:]
        me = _coords()
        sibling = (me[0], me[1], 1 - me[2])
        first, passed, landed = [], [], []
        for a in range(n):
            half = ins[a].shape[0] // 2
            mine = ins[a].at[pl.ds(pl.multiple_of(me[2] * half, 16), half)]
            for j, flip in enumerate(CHIP_FLIPS):
                k = a * len(CHIP_FLIPS) + j
                peer = (me[0] ^ flip[0], me[1] ^ flip[1], me[2])
                first.append(pltpu.make_async_remote_copy(
                    src_ref=mine, dst_ref=outs[a].at[_chip_index(me), me[2]], send_sem=ici_send.at[k],
                    recv_sem=ici_recv.at[k], device_id=peer, device_id_type=MESH))
                arrived = outs[a].at[_chip_index(peer), me[2]]
                landed.append(pltpu.make_async_remote_copy(
                    src_ref=mine, dst_ref=arrived, send_sem=ici_send.at[k], recv_sem=ici_recv.at[k],
                    device_id=peer, device_id_type=MESH))
                passed.append(pltpu.make_async_remote_copy(
                    src_ref=arrived, dst_ref=arrived, send_sem=d2d_send.at[k], recv_sem=d2d_recv.at[k],
                    device_id=sibling, device_id_type=MESH))
        for cp in first:
            cp.start()
        for k in range(nk):
            landed[k].wait_recv()
            passed[k].start()
        for a in range(n):
            for j, flip in enumerate(CHIP_FLIPS):
                k = a * len(CHIP_FLIPS) + j
                peer_chip = _chip_index((me[0] ^ flip[0], me[1] ^ flip[1]))
                from_sibling = outs[a].at[peer_chip, 1 - me[2]]
                pltpu.make_async_remote_copy(
                    src_ref=from_sibling, dst_ref=from_sibling, send_sem=d2d_send.at[k], recv_sem=d2d_recv.at[k],
                    device_id=sibling, device_id_type=MESH).wait_recv()
        for cp in first + passed:
            cp.wait_send()

    hbm = pl.BlockSpec(memory_space=pl.ANY)
    return pl.pallas_call(
        body, out_shape=[jax.ShapeDtypeStruct((N_CHIP, 2, x.shape[0] // 2, x.shape[1]), x.dtype) for x in xs],
        in_specs=[hbm] * n, out_specs=[hbm] * n,
        scratch_shapes=[pltpu.SemaphoreType.DMA((nk,)) for _ in range(4)], name=name)(*xs)


HBM_SPEC = pl.BlockSpec(memory_space=pltpu.HBM)
SEM_SPEC = pl.BlockSpec(memory_space=pltpu.SEMAPHORE)
DATAFLOW = pltpu.SideEffectType.DATAFLOW_SIDE_EFFECTING


def _at(ref, idx):
    return ref if idx is None else ref.at[idx]


def _peer(me, flip):
    return (me[0] ^ flip[0], me[1] ^ flip[1], me[2] ^ flip[2])


def exchange_start(arrays, land_shapes, remote, name):
    n_in, n_out, nk = len(arrays), len(land_shapes), len(remote)

    def body(*refs):
        srcs, lands = refs[:n_in], refs[n_in:n_in + n_out]
        send_sems, recv_sems, token = refs[n_in + n_out], refs[n_in + n_out + 1], refs[-1]
        me = _coords()
        for k, (flip, ii, src_at, oi, dst_at) in enumerate(remote):
            peer = _peer(me, flip)
            pltpu.make_async_remote_copy(
                src_ref=_at(srcs[ii], src_at(me, peer)), dst_ref=_at(lands[oi], dst_at(me)), send_sem=send_sems.at[k],
                recv_sem=recv_sems.at[k], device_id=peer, device_id_type=MESH).start()
        token[...] = jnp.zeros_like(token)

    lands = [lax.empty(s.shape, s.dtype) for s in land_shapes]
    bufs = list(arrays) + lands
    out = pl.pallas_call(
        body, name=name,
        out_shape=(pltpu.SemaphoreType.DMA((nk,)), pltpu.SemaphoreType.DMA((nk,)),
                   *[pltpu.HBM(b.shape, b.dtype) for b in bufs], jax.ShapeDtypeStruct((8, 128), F32)),
        in_specs=[HBM_SPEC] * len(bufs),
        out_specs=(SEM_SPEC, SEM_SPEC, *[HBM_SPEC] * len(bufs), pl.BlockSpec(memory_space=pltpu.VMEM)),
        input_output_aliases={a: 2 + a for a in range(len(bufs))},
        compiler_params=pltpu.CompilerParams(has_side_effects=DATAFLOW),
    )(*[pltpu.with_memory_space_constraint(b, pltpu.HBM) for b in bufs])
    flight = (out[0], out[1], list(out[2:2 + n_in]), list(out[2 + n_in:2 + n_in + n_out]), remote)
    return flight, out[-1]


def exchange_wait(flight, after, name):
    send_sems, recv_sems, arrays, lands, remote = flight
    n_in, n_out = len(arrays), len(lands)

    def body(*refs):
        srcs, lnds = refs[:n_in], refs[n_in:n_in + n_out]
        s_sems, r_sems = refs[n_in + n_out], refs[n_in + n_out + 1]
        me = _coords()
        for k, (flip, ii, src_at, oi, dst_at) in enumerate(remote):
            peer = _peer(me, flip)
            copy = pltpu.make_async_remote_copy(
                src_ref=_at(srcs[ii], src_at(me, peer)), dst_ref=_at(lnds[oi], dst_at(peer)), send_sem=s_sems.at[k],
                recv_sem=r_sems.at[k], device_id=peer, device_id_type=MESH)
            copy.wait_send()
            copy.wait_recv()

    bufs = list(arrays) + list(lands)
    out = pl.pallas_call(
        body, name=name,
        out_shape=tuple(pltpu.HBM(b.shape, b.dtype) for b in bufs),
        in_specs=[HBM_SPEC] * len(bufs) + [SEM_SPEC, SEM_SPEC, pl.BlockSpec(memory_space=pl.ANY)],
        out_specs=tuple([HBM_SPEC] * len(bufs)),
        input_output_aliases={a: a for a in range(len(bufs))},
        compiler_params=pltpu.CompilerParams(has_side_effects=DATAFLOW),
    )(*bufs, send_sems, recv_sems, after)
    return list(out[:n_in]), list(out[n_in:])


def _half_tile(h, cd):
    return h if h * cd * 4 <= (1 << 20) else math.gcd(512, h)


def pair_add(g, got, core, out_dtype, name):
    _, _, h, cd = g.shape
    th = _half_tile(h, cd)

    def body(c_ref, g_ref, got_ref, o_ref):
        o_ref[0] = (g_ref[0, 0] + got_ref[0]).astype(o_ref.dtype)

    return pl.pallas_call(
        body, out_shape=jax.ShapeDtypeStruct((N_CHIP, h, cd), out_dtype),
        grid_spec=pltpu.PrefetchScalarGridSpec(
            num_scalar_prefetch=1, grid=(N_CHIP, h // th),
            in_specs=[pl.BlockSpec((1, 1, th, cd), lambda q, i, c: (q, c[0], i, 0)),
                      pl.BlockSpec((1, th, cd), lambda q, i, c: (q, i, 0))],
            out_specs=pl.BlockSpec((1, th, cd), lambda q, i, c: (q, i, 0))),
        compiler_params=_params(("parallel", "parallel"), VMEM_LIMIT), name=name)(core, g, got)


def sum_chips(parts, sums, place, name):
    _, h, cd = parts.shape
    th = _half_tile(h, cd)

    def body(pc_ref, p_ref, own_ref, o_ref):
        acc = None
        for q in range(N_CHIP):
            term = jnp.where(pc_ref[1] == q, own_ref[0], p_ref[q]).astype(F32)
            acc = term if acc is None else acc + term
        o_ref[0] = acc

    return pl.pallas_call(
        body, out_shape=jax.ShapeDtypeStruct((2, h, cd), F32),
        grid_spec=pltpu.PrefetchScalarGridSpec(
            num_scalar_prefetch=1, grid=(h // th,),
            in_specs=[pl.BlockSpec((N_CHIP, th, cd), lambda i, pc: (0, i, 0)),
                      pl.BlockSpec((1, th, cd), lambda i, pc: (pc[1], i, 0))],
            out_specs=pl.BlockSpec((1, th, cd), lambda i, pc: (pc[0], i, 0))),
        compiler_params=_params(("parallel",), VMEM_LIMIT), name=name)(place, parts, sums)


def to_segments(a, n_ctx):
    def one(p):
        n = p.shape[0]
        return p.reshape(N_SEG, n // N_SEG, -1).transpose(1, 0, 2).reshape(n, -1)
    return jnp.concatenate([one(a[:n_ctx]), one(a[n_ctx:])], axis=0) if n_ctx else one(a)


def from_segments(a, n_ctx):
    def one(p):
        n = p.shape[0]
        return p.reshape(n // N_SEG, N_SEG, -1).transpose(1, 0, 2).reshape(n, -1)
    return jnp.concatenate([one(a[:n_ctx]), one(a[n_ctx:])], axis=0) if n_ctx else one(a)


def rope_tables(n_ctx, n_lat):
    f32 = np.float32
    rows = n_lat // GRID_W
    row = np.repeat(np.arange(rows), GRID_W).astype(f32)
    col = np.tile(np.arange(GRID_W), rows).astype(f32)
    d = QK_ROPE_DIM // 2
    inv = (f32(1.0) / np.power(f32(ROPE_THETA), np.arange(0, d, 2, dtype=f32) / f32(d))).astype(f32)
    ang = np.concatenate([row[:, None] * inv[None, :], col[:, None] * inv[None, :]], axis=1).astype(f32)
    cos = np.concatenate([np.ones((n_ctx, d), f32), np.cos(ang)], axis=0)
    sin = np.concatenate([np.zeros((n_ctx, d), f32), np.sin(ang)], axis=0)
    q = QK_ROPE_DIM // 4
    T = n_ctx + n_lat
    ones, zeros = np.ones((T, QK_NOPE_DIM), f32), np.zeros((T, QK_NOPE_DIM), f32)
    tail, z8 = np.zeros((T, HEAD_LANES - QK_DIM), f32), np.zeros((T, q), f32)
    cr, cc, sr, sc = cos[:, :q], cos[:, q:], sin[:, :q], sin[:, q:]
    cos_t = np.concatenate([ones, cr, cr, cc, cc, tail], axis=1)
    sin_next = np.concatenate([zeros, -sr, z8, -sc, z8, tail], axis=1)
    sin_prev = np.concatenate([zeros, z8, sr, z8, sc, tail], axis=1)
    return tuple(jnp.asarray(t, F32) for t in (cos_t, sin_next, sin_prev))


def pad_heads(w, used):
    k = w.shape[0]
    return jnp.pad(w.reshape(k, MLA_HEADS, used), ((0, 0), (0, 0), (0, HEAD_LANES - used))).reshape(k, -1)


def unpad_heads(w, used):
    k = w.shape[0]
    return w.reshape(k, MLA_HEADS, HEAD_LANES)[:, :, :used].reshape(k, MLA_HEADS * used)


def rotary_spread():
    lane = np.arange(MLA_HEADS * HEAD_LANES) % HEAD_LANES
    return jnp.asarray(lane[None, :] == (QK_NOPE_DIM + np.arange(QK_ROPE_DIM))[:, None], BF16)


def s5_discretise(a_re, a_im, log_step, b_re, b_im):
    dt = jnp.exp(log_step)[:, None]
    mag = jnp.exp(a_re * dt)
    lb_re = mag * jnp.cos(a_im * dt)
    lb_im = mag * jnp.sin(a_im * dt)
    den = a_re * a_re + a_im * a_im
    nr = lb_re - 1.0
    f_re = ((nr * a_re + lb_im * a_im) / den)[..., None]
    f_im = ((lb_im * a_re - nr * a_im) / den)[..., None]
    return lb_re, lb_im, f_re * b_re - f_im * b_im, f_re * b_im + f_im * b_re


def s5_block_weights(lb_re, lb_im, bb_re, bb_im, c_re, c_im):
    eye = jnp.eye(GROUPS_PER_BLOCK, dtype=F32)
    lam = jnp.stack([lb_re.reshape(1, S5_LANES), lb_im.reshape(1, S5_LANES)])

    def b_blocks(bb):
        t = bb.reshape(N_BLOCKS, GROUPS_PER_BLOCK, S5_STATE, S5_GROUP)
        return jnp.einsum("bgpc,gh->bgchp", t, eye).reshape(N_BLOCKS, BLK_CH, BLK_ST).astype(BF16)

    def c_blocks(cc):
        t = cc.reshape(N_BLOCKS, GROUPS_PER_BLOCK, S5_GROUP, S5_STATE)
        return jnp.einsum("bgcp,gh->bgphc", t, eye).reshape(N_BLOCKS, BLK_ST, BLK_CH).astype(BF16)

    return lam, b_blocks(bb_re), b_blocks(bb_im), c_blocks(c_re), c_blocks(c_im)


def b_block_diag(db):
    t = db.reshape(N_BLOCKS, GROUPS_PER_BLOCK, S5_GROUP, GROUPS_PER_BLOCK, S5_STATE)
    return jnp.einsum("bgchp,gh->bgpc", t, jnp.eye(GROUPS_PER_BLOCK, dtype=F32)).reshape(S5_GROUPS, S5_STATE, S5_GROUP)


def c_block_diag(dc):
    t = dc.reshape(N_BLOCKS, GROUPS_PER_BLOCK, S5_STATE, GROUPS_PER_BLOCK, S5_GROUP)
    return jnp.einsum("bgphc,gh->bgcp", t, jnp.eye(GROUPS_PER_BLOCK, dtype=F32)).reshape(S5_GROUPS, S5_GROUP, S5_STATE)


def conj(a):
    return jnp.stack([a[0], -a[1]])


PACK_TILE = 16 * 128


def pack_flat(parts, dtype):
    flat = [p.reshape(-1).astype(dtype) for p in parts]
    sizes = [f.shape[0] for f in flat]
    total = sum(sizes)
    pad = (-total) % PACK_TILE
    if pad:
        flat.append(jnp.zeros((pad,), dtype))
    offs = np.cumsum([0] + sizes)[:-1].tolist()
    return jnp.concatenate(flat).reshape(-1, 128), offs


def unpack_flat(buf, offs, shapes):
    flat = buf.reshape(-1)
    return [flat[o:o + int(np.prod(s))].reshape(s) for o, s in zip(offs, shapes)]


def s5_forward(p1, n_ctx, dirs):
    saved = []
    y = None
    ctx_rows, lat_rows = (0, n_ctx), (n_ctx, p1.shape[0] - n_ctx)
    zeros_tile = jnp.zeros((2, N_SEG, S5_LANES), F32)
    zeros_row = jnp.zeros((2, 1, S5_LANES), F32)
    for k, (lam, b_re, b_im, c_re, c_im) in enumerate(dirs):
        rev = k == 1
        last = 0 if rev else N_SEG - 1
        _, _, fin = s5_scan(p1, b_re, b_im, lam, zeros_tile, reverse=rev, rows=ctx_rows, name=f"s5_ctx_finals{k}")
        carry_c = s5_chain(fin, zeros_row, lam, n_ctx // N_SEG, rev, name=f"s5_ctx_chain{k}")
        _, ck_c, fin_c = s5_scan(p1, b_re, b_im, lam, carry_c, reverse=rev, want_ckpt=True, rows=ctx_rows,
                                 name=f"s5_ctx_scan{k}")
        s0 = fin_c[:, last:last + 1, :]
        _, _, fin = s5_scan(p1, b_re, b_im, lam, zeros_tile, reverse=rev, rows=lat_rows, name=f"s5_lat_finals{k}")
        carry_l = s5_chain(fin, s0, lam, lat_rows[1] // N_SEG, rev, name=f"s5_lat_chain{k}")
        y, ck_l, _ = s5_scan(p1, b_re, b_im, lam, carry_l, reverse=rev, c_re=c_re, c_im=c_im, add=y,
                             want_ckpt=True, rows=lat_rows, name=f"s5_lat_scan{k}")
        saved.append((ck_c, ck_l))
    return y, saved


def s5_backward(dy_l, du_extra_l, p1, n_ctx, dirs, saved):
    n_lat = p1.shape[0] - n_ctx
    zeros_tile = jnp.zeros((2, N_SEG, S5_LANES), F32)
    zeros_row = jnp.zeros((2, 1, S5_LANES), F32)
    dy_c = jnp.zeros((n_ctx, D_MODEL), F32)
    du_l, du_c = du_extra_l, None
    grads = []
    for k, (lam, b_re, b_im, c_re, c_im) in enumerate(dirs):
        rev = k == 1
        lam_c = conj(lam)
        ck_c, ck_l = saved[k]
        first = N_SEG - 1 if rev else 0
        _, _, fin = s5_scan(dy_l, c_re, c_im, lam_c, zeros_tile, reverse=not rev, adjoint=True,
                            name=f"s5_lat_adj_finals{k}")
        carry = s5_chain(fin, zeros_row, lam_c, n_lat // N_SEG, not rev, name=f"s5_lat_adj_chain{k}")
        du_l, dlam_l, dbr_l, dbi_l, dcr_l, dci_l, fin_a = s5_grads(
            dy_l, p1, ck_l, b_re, b_im, c_re, c_im, lam, carry, reverse=rev, add=du_l, u_off=n_ctx,
            name=f"s5_lat_grads{k}")
        g0 = fin_a[:, first:first + 1, :]
        carry = s5_chain(zeros_tile, g0, lam_c, n_ctx // N_SEG, not rev, name=f"s5_ctx_adj_chain{k}")
        du_c, dlam_c, dbr_c, dbi_c, _, _, _ = s5_grads(
            dy_c, p1, ck_c, b_re, b_im, c_re, c_im, lam, carry, reverse=rev, add=du_c, name=f"s5_ctx_grads{k}")
        dlam = jnp.sum(dlam_l + dlam_c, axis=1)
        grads.append((dlam, b_block_diag(dbr_l + dbr_c), b_block_diag(dbi_l + dbi_c),
                      c_block_diag(dcr_l), c_block_diag(dci_l)))
    return jnp.concatenate([du_c, du_l], axis=0), grads


def local_step(x, ctx, target, mod, w, late=None, reducer=None):
    L, Lc = x.shape[0], ctx.shape[0]
    T = L + Lc
    assert L % Lc == 0 and Lc % (2 * N_SEG) == 0 and L % GRID_W == 0
    D = D_MODEL
    X0 = jnp.concatenate([ctx, x], axis=0)

    def mod_of(i, j):
        return mod[i, :, j, :][:, None, :]

    def vec(v):
        return v.reshape(1, 1, -1).astype(F32)

    g0 = vec(w["norm_g"][0])
    (H0,) = rowwise_fwd(f_norm_mod, [X0], [g0, mod_of(0, 1), mod_of(0, 0)], [D], [BF16], T, Lc, "l0_norm")
    p0 = mm_nn(H0, w["mla_w_in"], name="l0_in")
    cq = Rows(p0, Q_LORA_RANK, col_blk=D // Q_LORA_RANK)
    ckv = Rows(p0, KV_LORA_RANK, col_blk=(D + Q_LORA_RANK) // KV_LORA_RANK)
    kr = p0[:, D + Q_LORA_RANK + KV_LORA_RANK:D + P0_HEAD]
    qng, kvng = vec(w["mla_q_norm"]), vec(w["mla_kv_norm"])
    (qn,) = rowwise_fwd(f_rms, [cq], [qng], [Q_LORA_RANK], [BF16], T, 0, "l0_qnorm")
    (kvn,) = rowwise_fwd(f_rms, [ckv], [kvng], [KV_LORA_RANK], [BF16], T, 0, "l0_kvnorm")
    tabs = rope_tables(Lc, L)
    spread = rotary_spread()
    w_uq_p = pad_heads(w["mla_w_uq"], QK_DIM)
    w_ukv3 = w["mla_w_ukv"].reshape(KV_LORA_RANK, MLA_HEADS, QK_NOPE_DIM + V_HEAD_DIM)
    w_kn_p = pad_heads(w_ukv3[:, :, :QK_NOPE_DIM].reshape(KV_LORA_RANK, -1), QK_NOPE_DIM)
    w_v = w_ukv3[:, :, QK_NOPE_DIM:].reshape(KV_LORA_RANK, -1)
    qb = q_heads(qn, w_uq_p, tabs)
    kb, vb = kv_heads(kvn, w_kn_p, w_v, kr, spread, tabs)
    o, lse, lse_rows = attn_fwd(qb, kb, vb, Lc)
    X1, og, out0 = mla_post_fwd(o, p0, X0, mod_of(0, 2), w["mla_w_out"], Lc)

    if late is not None:
        w = {**w, **late(X1)}
    X1p = to_segments(X1, Lc)
    tgt_p = to_segments(target, 0)
    g1 = vec(w["norm_g"][1])
    (H1,) = rowwise_fwd(f_norm_mod, [X1p], [g1, mod_of(1, 1), mod_of(1, 0)], [D], [BF16], T, Lc, "l1_norm")
    p1 = mm_nn(H1, w["s5_w_in"], name="l1_in")
    disc_fn = lambda *a: tuple(zip(*[s5_discretise(a[0][k], a[1][k], a[2][k], a[3][k], a[4][k]) for k in range(2)]))
    disc, disc_vjp = jax.vjp(disc_fn, w["s5_a_re"], w["s5_a_im"], w["s5_log_step"], w["s5_b_re"], w["s5_b_im"])
    dirs = [s5_block_weights(disc[0][k], disc[1][k], disc[2][k], disc[3][k], w["s5_c_re"][k], w["s5_c_im"][k])
            for k in range(2)]
    y_ssm, s5_saved = s5_forward(p1, Lc, dirs)

    row = lambda v: v.reshape(1, D).astype(F32)
    (lvec, dX2, d_yssm, d_u_act, d_z1, d_fg, d_gt1, d_bg, d_d, gw_glu, gw_out) = s5_tail(
        y_ssm, p1, X1p, tgt_p, Lc, row(w["s5_d"]), row(w["s5_b_glu"]), mod[1, 1:2, 2, :], row(w["final_g"]),
        w["s5_w_glu"], w["s5_w_out"])
    loss = jnp.sum(lvec)
    gw = {"final_g": d_fg.reshape(D), "s5_b_glu": d_bg.reshape(D), "s5_d": d_d.reshape(D),
          "s5_w_glu": gw_glu, "s5_w_out": gw_out}
    dmod = {}

    du_p, s5_g = s5_backward(d_yssm, d_u_act, p1, Lc, dirs, s5_saved)
    d_disc = tuple(tuple(s5_g[k][j - 1].reshape(disc[j][k].shape) if j >= 2 else
                         s5_g[k][0][j].reshape(disc[j][k].shape) for k in range(2)) for j in range(4))
    gw["s5_a_re"], gw["s5_a_im"], gw["s5_log_step"], gw["s5_b_re"], gw["s5_b_im"] = disc_vjp(d_disc)
    gw["s5_c_re"] = jnp.stack([s5_g[0][3], s5_g[1][3]])
    gw["s5_c_im"] = jnp.stack([s5_g[0][4], s5_g[1][4]])
    d_H1 = mm_nt_sum([(du_p, 0, 0), (d_z1, D, Lc)], w["s5_w_in"], T, "l1_in_dx")
    gw["s5_w_in"] = jnp.concatenate([mm_tn(H1, du_p, name="l1_in_dw_u"), mm_tn(H1[Lc:], d_z1, name="l1_in_dw_z")],
                                    axis=1)
    if reducer is not None:
        g1 = g1 + reducer[0]({n: gw.pop(n) for n in LAYER1_MATS})[0, 0]
    d_X1p, d_g1, d_sc1, d_sh1 = rowwise_bwd(f_norm_mod, [X1p], [g1, mod_of(1, 1), mod_of(1, 0)], [d_H1],
                                            [0], [0, 1, 2], T, Lc, "l1_norm_bwd", lat_add=dX2)
    d_gt1_full = jnp.concatenate([jnp.zeros((1, 1, D), F32), d_gt1[None]], axis=0)
    dmod[1] = (d_sh1, d_sc1, d_gt1_full)
    d_X1 = from_segments(d_X1p, Lc)

    d_o, d_z0, d_gt0, gw["mla_w_out"] = mla_post_bwd(d_X1, out0, og, o, p0, mod_of(0, 2), w["mla_w_out"], Lc)
    if reducer is not None:
        tabs = (tabs[0] + reducer[1](d_o)[0, 0],) + tabs[1:]
    d_q, delta_rows = attn_bwd_dq(qb, kb, vb, o, d_o, lse, tabs, Lc)
    dk_p, d_v = attn_bwd_dkv(qb, kb, vb, d_o, lse_rows, delta_rows, Lc)
    d_k, d_kr = heads_unrope(dk_p, tabs, jnp.pad(spread, ((0, HEAD_LANES - QK_ROPE_DIM), (0, 0))),
                             name="l0_k_unrope")
    d_qn = mm_nt(d_q, w_uq_p, name="l0_uq_dx")
    gw["mla_w_uq"] = unpad_heads(mm_tn(qn, d_q, name="l0_uq_dw"), QK_DIM)
    d_kvn = mm_nt(d_k, w_kn_p, name="l0_ukn_dx") + mm_nt(d_v, w_v, name="l0_uv_dx")
    dw_kn = unpad_heads(mm_tn(kvn, d_k, name="l0_ukn_dw"), QK_NOPE_DIM).reshape(KV_LORA_RANK, MLA_HEADS, QK_NOPE_DIM)
    dw_v = mm_tn(kvn, d_v, name="l0_uv_dw").reshape(KV_LORA_RANK, MLA_HEADS, V_HEAD_DIM)
    gw["mla_w_ukv"] = jnp.concatenate([dw_kn, dw_v], axis=-1).reshape(KV_LORA_RANK, -1)
    d_cq, d_qng = rowwise_bwd(f_rms, [cq], [qng], [d_qn], [0], [0], T, 0, "l0_qnorm_bwd")
    d_ckv, d_kvng = rowwise_bwd(f_rms, [ckv], [kvng], [d_kvn], [0], [0], T, 0, "l0_kvnorm_bwd")
    gw["mla_q_norm"] = d_qng.reshape(-1)
    gw["mla_kv_norm"] = d_kvng.reshape(-1)
    o_cq, o_ckv = D, D + Q_LORA_RANK
    o_kr = o_ckv + KV_LORA_RANK
    d_H0 = mm_nt_sum([(d_z0, 0, 0), (d_cq, o_cq, 0), (d_ckv, o_ckv, 0), (d_kr, o_kr, 0)], w["mla_w_in"], T, "l0_in_dx")
    d_head = jnp.concatenate([d_cq, d_ckv, d_kr], axis=1)
    gw["mla_w_in"] = jnp.concatenate([mm_tn(H0, d_head, name="l0_in_dw_head")[:, :P0_HEAD],
                                      mm_tn(H0, d_z0, name="l0_in_dw_z")], axis=1)
    d_X0, d_g0, d_sc0, d_sh0 = rowwise_bwd(f_norm_mod, [X0], [g0, mod_of(0, 1), mod_of(0, 0)], [d_H0],
                                           [0], [0, 1, 2], T, Lc, "l0_norm_bwd", lat_add=d_X1)
    dmod[0] = (d_sh0, d_sc0, d_gt0)
    gw["norm_g"] = jnp.stack([d_g0.reshape(D), d_g1.reshape(D)])
    dx = d_X0[Lc:]
    dmod_arr = jnp.stack([jnp.stack([dmod[i][j][:, 0, :] for j in range(3)], axis=1) for i in range(2)])
    ready = reducer[2](d_X0) if reducer is not None else {}
    return loss, dx, dmod_arr, gw, ready


SHARDED = {
    "mla_w_in": 1, "mla_w_uq": 1, "mla_w_ukv": 1, "mla_w_out": 0,
    "s5_w_in": 1, "s5_w_glu": 0, "s5_w_out": 0, "s5_d": 0, "s5_b_glu": 0,
}
SHARDED_MATS = ["mla_w_in", "mla_w_uq", "mla_w_ukv", "mla_w_out", "s5_w_in", "s5_w_glu", "s5_w_out"]
SHARDED_VECS = ["s5_d", "s5_b_glu"]
REPLICATED = ["norm_g", "mla_q_norm", "mla_kv_norm", "s5_a_re", "s5_a_im", "s5_log_step", "s5_b_re", "s5_b_im",
              "s5_c_re", "s5_c_im", "final_g"]
WEIGHT_ORDER = ["c_ctx", "ada_w", "ada_b", "norm_g", "mla_w_in", "mla_q_norm", "mla_w_uq", "mla_kv_norm", "mla_w_ukv",
                "mla_w_out", "s5_w_in", "s5_a_re", "s5_a_im", "s5_log_step", "s5_b_re", "s5_b_im", "s5_c_re", "s5_c_im",
                "s5_d", "s5_w_glu", "s5_b_glu", "s5_w_out", "final_g"]


P0_HEAD = Q_LORA_RANK + KV_LORA_RANK + QK_ROPE_DIM


P0_WIDTH = 1536


def w_in_to_kernel_order(w):
    pad = jnp.zeros((w.shape[0], P0_WIDTH - w.shape[1]), w.dtype)
    return jnp.concatenate([w[:, P0_HEAD:], w[:, :P0_HEAD], pad], axis=1)


LAYER0_MATS = ["mla_w_in", "mla_w_uq", "mla_w_ukv", "mla_w_out"]
LAYER1_MATS = ["s5_w_in", "s5_w_glu", "s5_w_out"]


def _whole_matrices(names, own_blocks, gathered):
    chip = _chip_index(_coords())
    full = {}
    for n, own, o in zip(names, own_blocks, gathered):
        slot = lax.broadcasted_iota(jnp.int32, (N_CHIP, 1, 1), 0)
        o = jnp.where(slot == chip, own[None], o.reshape((N_CHIP,) + own.shape))
        full[n] = o.reshape(-1, o.shape[-1]) if SHARDED[n] == 0 else o.transpose(1, 0, 2).reshape(o.shape[1], -1)
    return full


def gather_weights(ws):
    mats = [ws[n].astype(BF16) for n in LAYER0_MATS]
    full = _whole_matrices(LAYER0_MATS, mats, gather_halves(mats, "gather_weights"))
    full["mla_w_in"] = w_in_to_kernel_order(full["mla_w_in"])
    return full


def gather_weights_behind(ws):
    mats = [ws[n].astype(BF16) for n in LAYER1_MATS]
    flight, token = exchange_start(
        mats, [jax.ShapeDtypeStruct((N_CHIP,) + m.shape, m.dtype) for m in mats],
        [(f, a, lambda me, peer: None, a, lambda s: (_chip_index(s),)) for a in range(len(mats)) for f in CHIP_FLIPS],
        "gather_l1_start")

    def finish(after):
        own, got = exchange_wait(flight, after, "gather_l1_wait")
        return _whole_matrices(LAYER1_MATS, own, got)

    return token[0, 0], finish


def _grad_slots(gw, names):
    slots = []
    for n in names:
        g = gw[n]
        if SHARDED[n] == 0:
            slots.append(g.reshape(N_CHIP, 2, g.shape[0] // (2 * N_CHIP), g.shape[1]))
        else:
            k, n4 = g.shape
            slots.append(g.reshape(k, N_CHIP, n4 // N_CHIP).transpose(1, 0, 2)
                         .reshape(N_CHIP, 2, k // 2, n4 // N_CHIP))
    return slots


def _to_sibling_half(count):
    return [(CORE_FLIP, i, lambda me, peer: (slice(None), 1 - me[2]), i, lambda s: None) for i in range(count)]


def _to_chips(count):
    return [(f, i, lambda me, peer: (_chip_index(peer),), i, lambda s: (_chip_index(s),))
            for i in range(count) for f in CHIP_FLIPS]


def _place():
    me = _coords()
    return jnp.stack([me[2], _chip_index(me)]).astype(jnp.int32)


def reduce_behind(names):
    state = {}
    count = len(names)

    def begin(gw):
        slots = _grad_slots(gw, names)
        lands = [jax.ShapeDtypeStruct((N_CHIP,) + s.shape[2:], F32) for s in slots]
        state["in"], token = exchange_start(slots, lands, _to_sibling_half(count), "grads_l1_swap_in_start")
        return token

    def middle(after):
        slots, got = exchange_wait(state["in"], after, "grads_l1_swap_in_wait")
        place = _place()
        sums = [pair_add(s, g, place[:1], BF16, f"grads_pair_{n}") for n, s, g in zip(names, slots, got)]
        lands = [jax.ShapeDtypeStruct(s.shape, s.dtype) for s in sums]
        state["out"], token = exchange_start(sums, lands, _to_chips(count), "grads_l1_scatter_start")
        return token

    def end(after):
        sums, parts = exchange_wait(state["out"], after, "grads_l1_scatter_wait")
        place = _place()
        return {n: sum_chips(p, s, place, f"grads_sum_{n}") for n, p, s in zip(names, parts, sums)}

    return begin, middle, end


def reduce_gradients(gw, ready_halves):
    me = _coords()
    place = _place()
    mat_names = [n for n in SHARDED_MATS if n not in ready_halves]
    slots = dict(zip(mat_names, _grad_slots(gw, mat_names)))
    small_names = REPLICATED + SHARDED_VECS
    small, small_offs = pack_flat([gw[n].astype(F32) for n in small_names], F32)
    small = jnp.pad(small, ((0, (-small.shape[0]) % (N_CHIP * 32)), (0, 0)))
    slots["small"] = small.reshape(N_CHIP, 2, -1, 128)
    names = list(slots)
    count = len(names)
    got = exchange([slots[n] for n in names],
                   [jax.ShapeDtypeStruct((N_CHIP,) + slots[n].shape[2:], F32) for n in names],
                   _to_sibling_half(count), [], "grads_swap_in")
    sums = [pair_add(slots[n], g, place[:1], F32 if n == "small" else BF16, f"grads_pair_{n}")
            for n, g in zip(names, got)]
    parts = exchange(sums, [jax.ShapeDtypeStruct(s.shape, s.dtype) for s in sums], _to_chips(count), [],
                     "grads_scatter")
    halves = {n: sum_chips(p, s, place, f"grads_sum_{n}") for n, p, s in zip(names, parts, sums)}
    halves.update(ready_halves)
    all_names = list(halves)
    fulls = exchange(
        [halves[n] for n in all_names], [jax.ShapeDtypeStruct(halves[n].shape, F32) for n in all_names],
        [(CORE_FLIP, i, lambda me, peer: (me[2],), i, lambda s: (s[2],)) for i in range(len(all_names))], [],
        "grads_swap_out", aliases={i: i for i in range(len(all_names))})
    out = {n: f.reshape(-1, f.shape[-1]) for n, f in zip(all_names, fulls)}
    quarter = out.pop("small")
    gather, _ = exchange_start(
        [quarter], [jax.ShapeDtypeStruct((N_CHIP,) + quarter.shape, F32)],
        [(f, 0, lambda me, peer: None, 0, lambda s: (_chip_index(s),)) for f in CHIP_FLIPS],
        "grads_gather_small_start")

    def finish_small(after):
        (own,), (got_small,) = exchange_wait(gather, after, "grads_gather_small_wait")
        slot = lax.broadcasted_iota(jnp.int32, (N_CHIP, 1, 1), 0)
        small_all = jnp.where(slot == _chip_index(me), own[None], got_small)
        vals = unpack_flat(small_all, small_offs, [gw[n].shape for n in small_names])
        res = {}
        for n, v in zip(small_names, vals):
            if n in SHARDED_VECS:
                size = v.shape[0] // N_CHIP
                v = lax.dynamic_slice_in_dim(v, _chip_index(me) * size, size)
            res[n] = v
        return res

    return out, finish_small


def kernel(x, c, ctx, c_ctx, ada_w, ada_b, norm_g, mla_w_in, mla_q_norm, mla_w_uq, mla_kv_norm, mla_w_ukv, mla_w_out, s5_w_in, s5_a_re, s5_a_im, s5_log_step, s5_b_re, s5_b_im, s5_c_re, s5_c_im, s5_d, s5_w_glu, s5_b_glu, s5_w_out, final_g, loss_target, m_c_ctx, m_ada_w, m_ada_b, m_norm_g, m_mla_w_in, m_mla_q_norm, m_mla_w_uq, m_mla_kv_norm, m_mla_w_ukv, m_mla_w_out, m_s5_w_in, m_s5_a_re, m_s5_a_im, m_s5_log_step, m_s5_b_re, m_s5_b_im, m_s5_c_re, m_s5_c_im, m_s5_d, m_s5_w_glu, m_s5_b_glu, m_s5_w_out, m_final_g, v_c_ctx, v_ada_w, v_ada_b, v_norm_g, v_mla_w_in, v_mla_q_norm, v_mla_w_uq, v_mla_kv_norm, v_mla_w_ukv, v_mla_w_out, v_s5_w_in, v_s5_a_re, v_s5_a_im, v_s5_log_step, v_s5_b_re, v_s5_b_im, v_s5_c_re, v_s5_c_im, v_s5_d, v_s5_w_glu, v_s5_b_glu, v_s5_w_out, v_final_g):
    args = dict(locals())
    weights = {n: args[n] for n in WEIGHT_ORDER}
    D = D_MODEL
    xi, yi, ci = _coords()
    chip = 2 * xi + yi
    me = 4 * xi + 2 * yi + ci
    n_col = ada_w.shape[2]

    c_all = allgather_devices(jnp.pad(c, ((0, 7), (0, 0))), "gather_c")[:, 0, :]
    cond = jnp.concatenate([c_all, jnp.broadcast_to(c_ctx[None], (8, D))], axis=0)
    (s_cond,) = rowwise_fwd(lambda v: (_silu(v),), [cond], [], [D], [F32], 16, 0, "cond_silu")
    ada_rows = ada_w.reshape(2 * D, n_col)
    mod_cols = jnp.stack([mm_nn(s_cond, ada_rows, name=f"mod_proj{i}", b_blk=i) for i in range(2)])
    vec_tiles = [jnp.pad(weights[n][0].reshape(-1, 128), ((0, 6), (0, 0))) for n in SHARDED_VECS]
    mod_all, *vec_all = allgather_chips([mod_cols] + vec_tiles, "gather_mod")
    mod_all = mod_all.transpose(1, 2, 0, 3).reshape(2, 16, 3 * D) + ada_b[:, None, :]
    mod_l = lax.dynamic_index_in_dim(mod_all, me, axis=1, keepdims=False)
    mod_c = mod_all[:, 8, :]
    mod = jnp.stack([mod_c.reshape(2, 3, D), mod_l.reshape(2, 3, D)], axis=1)

    w = gather_weights({n: weights[n][0] for n in LAYER0_MATS})
    landed = w["mla_w_out"][0, 0].astype(F32) * 0.0
    token, late = gather_weights_behind({n: weights[n][0] + landed for n in LAYER1_MATS})
    for n, v in zip(SHARDED_VECS, vec_all):
        w[n] = v[:, :2, :].reshape(-1)
    for n in ["norm_g", "final_g"]:
        w[n] = weights[n]
    for n in ["mla_q_norm", "mla_kv_norm", "s5_a_re", "s5_a_im", "s5_log_step", "s5_b_re", "s5_b_im",
              "s5_c_re", "s5_c_im"]:
        w[n] = weights[n][0]

    loss_me, dx, dmod, gw, ready = local_step(x[0] + token, ctx[0], loss_target[0], mod, w, late,
                                              reduce_behind(LAYER1_MATS))

    dmod_rows, loss_all = _gather([dmod.reshape(2, 2, 3 * D), jnp.broadcast_to(loss_me, (8, 128))],
                                  ALL_FLIPS, _dev_index, N_DEV, "gather_dmod")
    loss = functools.reduce(lambda s, d: s + loss_all[d, 0, 0], range(1, N_DEV), loss_all[0, 0, 0])
    dm = jnp.concatenate([dmod_rows[:, :, 1, :], dmod_rows[:, :, 0, :]], axis=0).transpose(1, 0, 2)
    g_ada_b = jnp.sum(dm, axis=1)
    dm_cols = lax.dynamic_slice_in_dim(dm, chip * n_col, n_col, axis=2)
    g_ada_w = jnp.stack([mm_tn(s_cond, dm_cols[i], name=f"mod_proj_dw{i}") for i in range(2)])
    dmc = jnp.sum(dm_cols[:, 8:, :], axis=1)
    dmc8 = jnp.broadcast_to(dmc[:, None, :], (2, 8, n_col))
    g_sc = (mm_nt(dmc8[0], ada_rows, name="mod_proj_dx0", b_rows=D, b_blk=0)[0]
            + mm_nt(dmc8[1], ada_rows, name="mod_proj_dx1", b_rows=D, b_blk=1)[0])
    g_sc_all = allgather_devices(jnp.broadcast_to(g_sc[None], (8, D)), "gather_dcond")[:, 0, :]
    g_silu_cc = g_sc_all[0] + g_sc_all[2] + g_sc_all[4] + g_sc_all[6]
    (g_c_ctx,) = rowwise_bwd(lambda v: (_silu(v),), [jnp.broadcast_to(c_ctx[None], (8, D))], [],
                             [jnp.broadcast_to(g_silu_cc[None], (8, D))], [0], [], 8, 0, "cond_silu_bwd")
    g_c_ctx = g_c_ctx[0]

    grads = {"c_ctx": g_c_ctx, "ada_w": g_ada_w, "ada_b": g_ada_b}
    deltas, new_m, new_v = {}, {}, {}
    small = [n for n in WEIGHT_ORDER if weights[n].size < 50000]

    def update(n):
        shp = weights[n].shape
        w2 = weights[n].reshape(-1, shp[-1])
        d_, m_, v_ = adamw(w2, grads[n].reshape(w2.shape), args["m_" + n].reshape(w2.shape),
                           args["v_" + n].reshape(w2.shape), name=f"adamw_{n}")
        deltas[n], new_m[n], new_v[n] = d_.reshape(shp), m_.reshape(shp), v_.reshape(shp)

    red, finish_small = reduce_gradients(gw, ready)
    update("ada_w")
    for n in SHARDED_MATS:
        grads[n] = red[n].reshape(weights[n].shape)
        update(n)
    red_small = finish_small(jnp.stack([deltas[n].reshape(-1)[0] for n in ["ada_w"] + SHARDED_MATS]))
    for n in REPLICATED + SHARDED_VECS:
        grads[n] = red_small[n].reshape(weights[n].shape)
    for n in WEIGHT_ORDER:
        if n not in small and n not in deltas:
            update(n)
    packs = []
    offs = None
    for src in (weights, grads, {n: args["m_" + n] for n in small}, {n: args["v_" + n] for n in small}):
        buf, offs = pack_flat([src[n] for n in small], F32)
        packs.append(buf)
    outs = adamw(*packs, name="adamw_small")
    for res, dst in zip(outs, (deltas, new_m, new_v)):
        for n, val in zip(small, unpack_flat(res, offs, [weights[n].shape for n in small])):
            dst[n] = val

    return (loss, dx[None], *[grads[n] for n in WEIGHT_ORDER], *[deltas[n] for n in WEIGHT_ORDER],
            *[new_m[n] for n in WEIGHT_ORDER], *[new_v[n] for n in WEIGHT_ORDER])
```

```python
import functools
import math

import jax
import jax.numpy as jnp
import numpy as np
from jax import lax
from jax.experimental import pallas as pl
from jax.experimental.pallas import tpu as pltpu

F32 = jnp.float32
BF16 = jnp.bfloat16

D_MODEL = 1024
GRID_W = 64
EPS = 1e-6
MLA_HEADS = 16
QK_NOPE_DIM = 64
QK_ROPE_DIM = 32
V_HEAD_DIM = 64
Q_LORA_RANK = 256
KV_LORA_RANK = 128
QK_DIM = QK_NOPE_DIM + QK_ROPE_DIM
SOFTMAX_SCALE = QK_DIM ** -0.5
ROPE_THETA = 10000.0
S5_GROUP = 16
S5_GROUPS = D_MODEL // S5_GROUP
S5_STATE = 64
S5_LANES = S5_GROUPS * S5_STATE
N_SEG = 8
GROUPS_PER_BLOCK = 8
N_BLOCKS = S5_GROUPS // GROUPS_PER_BLOCK
BLK_CH = GROUPS_PER_BLOCK * S5_GROUP
BLK_ST = GROUPS_PER_BLOCK * S5_STATE

ADAM_LR = 0.001
ADAM_B1 = 0.9
ADAM_B2 = 0.999
ADAM_EPS = 1e-08
ADAM_WD = 0.01
ADAM_STEP = 10

N_DEV = 8
N_CHIP = 4
MESH = pl.DeviceIdType.MESH
VMEM_LIMIT = 52 * 1024 * 1024
ROW_TILE = 256


def _params(sem=None, vmem=None):
    return pltpu.CompilerParams(dimension_semantics=sem, vmem_limit_bytes=vmem)


def mm_nn(a, b, out_dtype=F32, name="mm_nn", b_blk=0):
    M, K = a.shape
    N = b.shape[1]
    tm = math.gcd(ROW_TILE, M)

    def body(a_ref, b_ref, o_ref):
        o_ref[...] = jnp.dot(a_ref[...].astype(BF16), b_ref[...].astype(BF16),
                             preferred_element_type=F32).astype(o_ref.dtype)

    return pl.pallas_call(
        body, out_shape=jax.ShapeDtypeStruct((M, N), out_dtype), grid=(M // tm,),
        in_specs=[pl.BlockSpec((tm, K), lambda i: (i, 0)), pl.BlockSpec((K, N), lambda i: (b_blk, 0))],
        out_specs=pl.BlockSpec((tm, N), lambda i: (i, 0)),
        compiler_params=_params(("parallel",), VMEM_LIMIT), name=name)(a, b)


def mm_nt(a, b, out_dtype=F32, name="mm_nt", b_rows=None, b_blk=0):
    M, N = a.shape
    K = b.shape[0] if b_rows is None else b_rows
    tm = math.gcd(ROW_TILE, M)

    def body(a_ref, b_ref, o_ref):
        o_ref[...] = lax.dot_general(a_ref[...].astype(BF16), b_ref[...].astype(BF16),
                                     (((1,), (1,)), ((), ())),
                                     preferred_element_type=F32).astype(o_ref.dtype)

    return pl.pallas_call(
        body, out_shape=jax.ShapeDtypeStruct((M, K), out_dtype), grid=(M // tm,),
        in_specs=[pl.BlockSpec((tm, N), lambda i: (i, 0)), pl.BlockSpec((K, N), lambda i: (b_blk, 0))],
        out_specs=pl.BlockSpec((tm, K), lambda i: (i, 0)),
        compiler_params=_params(("parallel",), VMEM_LIMIT), name=name)(a, b)


def mm_nt_sum(terms, w, n_rows, name):
    K = w.shape[0]
    tm = math.gcd(ROW_TILE, n_rows, *[t[2] for t in terms])

    def body(*refs):
        i = pl.program_id(0)
        w_ref, o_ref = refs[len(terms)], refs[len(terms) + 1]
        acc = None
        for a_ref, (a, off, first) in zip(refs, terms):
            part = lax.dot_general(a_ref[...].astype(BF16), w_ref[:, off:off + a.shape[1]].astype(BF16), NT_DIMS,
                                   preferred_element_type=F32)
            if first:
                part = jnp.where(i >= first // tm, part, 0.0)
            acc = part if acc is None else acc + part
        o_ref[...] = acc

    def a_spec(a, first):
        skip = first // tm
        return pl.BlockSpec((tm, a.shape[1]), lambda i: (jnp.maximum(i - skip, 0), 0))

    return pl.pallas_call(
        body, out_shape=jax.ShapeDtypeStruct((n_rows, K), F32), grid=(n_rows // tm,),
        in_specs=[a_spec(a, first) for a, _, first in terms] + [pl.BlockSpec(w.shape, lambda i: (0, 0))],
        out_specs=pl.BlockSpec((tm, K), lambda i: (i, 0)),
        compiler_params=_params(("parallel",), VMEM_LIMIT), name=name)(*[t[0] for t in terms], w)


def mm_tn(a, b, name="mm_tn"):
    M, K = a.shape
    N = b.shape[1]
    tn = math.gcd(512, N) if N % 128 == 0 and N > 512 else N

    def body(a_ref, b_ref, o_ref):
        o_ref[...] = lax.dot_general(a_ref[...].astype(BF16), b_ref[...].astype(BF16),
                                     (((0,), (0,)), ((), ())), preferred_element_type=F32)

    return pl.pallas_call(
        body, out_shape=jax.ShapeDtypeStruct((K, N), F32), grid=(N // tn,),
        in_specs=[pl.BlockSpec((M, K), lambda j: (0, 0)), pl.BlockSpec((M, tn), lambda j: (0, j))],
        out_specs=pl.BlockSpec((K, tn), lambda j: (0, j)),
        compiler_params=_params(("parallel",), VMEM_LIMIT), name=name)(a, b)


class Rows:
    def __init__(self, arr, width=None, row_off=0, col_blk=0):
        self.arr = arr
        self.width = arr.shape[1] if width is None else width
        self.row_off = row_off
        self.col_blk = col_blk

    def spec(self, tm):
        ro, cb = self.row_off // tm, self.col_blk
        return pl.BlockSpec((tm, self.width), lambda i: (i + ro, cb))


def _as_rows(x):
    return x if isinstance(x, Rows) else Rows(x)


def _row_tile(n_rows, n_ctx_rows, rows):
    tm = math.gcd(ROW_TILE, n_rows, n_ctx_rows)
    for r in rows:
        tm = math.gcd(tm, r.row_off)
    return tm


def _bc_spec(arr, n_ctx_blocks):
    g, _, d = arr.shape
    if g == 1:
        return pl.BlockSpec((1, 1, d), lambda i: (0, 0, 0))
    return pl.BlockSpec((1, 1, d), lambda i: ((i >= n_ctx_blocks).astype(jnp.int32), 0, 0))


def rowwise_fwd(fn, rows, bcs, out_dims, out_dtypes, n_rows, n_ctx_rows, name):
    rows = [_as_rows(r) for r in rows]
    tm = _row_tile(n_rows, n_ctx_rows, rows)
    ncb = n_ctx_rows // tm
    nr, nb = len(rows), len(bcs)

    def body(*refs):
        vals = [r[...].astype(F32) for r in refs[:nr]] + [b[0].astype(F32) for b in refs[nr:nr + nb]]
        outs = fn(*vals)
        for o_ref, v in zip(refs[nr + nb:], outs):
            o_ref[...] = v.astype(o_ref.dtype)

    outs = pl.pallas_call(
        body,
        out_shape=[jax.ShapeDtypeStruct((n_rows, d), dt) for d, dt in zip(out_dims, out_dtypes)],
        grid=(n_rows // tm,),
        in_specs=[r.spec(tm) for r in rows] + [_bc_spec(b, ncb) for b in bcs],
        out_specs=[pl.BlockSpec((tm, d), lambda i: (i, 0)) for d in out_dims],
        compiler_params=_params(("parallel",), VMEM_LIMIT), name=name)(*[r.arr for r in rows], *bcs)
    return outs


def rowwise_bwd(fn, rows, bcs, cts, diff_rows, diff_bcs, n_rows, n_ctx_rows, name, ct_extra=None, lat_add=None):
    rows = [_as_rows(r) for r in rows]
    cts = [_as_rows(c) for c in cts]
    extra = [_as_rows(ct_extra)] if ct_extra is not None else []
    tm = _row_tile(n_rows, n_ctx_rows, rows + cts + extra)
    ncb = n_ctx_rows // tm
    nr, nb, nc = len(rows), len(bcs), len(cts)
    ndr, ndb = len(diff_rows), len(diff_bcs)
    n_in = nr + nb + nc + len(extra) + (lat_add is not None)

    def body(*refs):
        i = pl.program_id(0)
        rvals = [r[...].astype(F32) for r in refs[:nr]]
        bvals = [b[0].astype(F32) for b in refs[nr:nr + nb]]
        cvals = [c[...].astype(F32) for c in refs[nr + nb:nr + nb + nc]]
        if extra:
            cvals[0] = cvals[0] + refs[nr + nb + nc][...].astype(F32)
        outs = refs[n_in:]

        def f(*d):
            rv, bv = list(rvals), list(bvals)
            for k, idx in enumerate(diff_rows):
                rv[idx] = d[k]
            for k, idx in enumerate(diff_bcs):
                bv[idx] = d[ndr + k]
            return tuple(fn(*rv, *bv))

        primals = [rvals[k] for k in diff_rows] + [bvals[k] for k in diff_bcs]
        _, vjp = jax.vjp(f, *primals)
        grads = list(vjp(tuple(cvals)))
        if lat_add is not None:
            add = refs[n_in - 1][...]
            grads[0] = grads[0] + (add if lat_add.shape[0] == n_rows else jnp.where(i >= ncb, add, 0.0))
        for k in range(ndr):
            outs[k][...] = grads[k].astype(outs[k].dtype)
        for k, idx in enumerate(diff_bcs):
            o_ref = outs[ndr + k]
            first = (i == 0)
            if bcs[idx].shape[0] == 2:
                first = first | (i == ncb)

            @pl.when(first)
            def _(o_ref=o_ref):
                o_ref[...] = jnp.zeros_like(o_ref)

            o_ref[0] += grads[ndr + k]

    out_shape = [jax.ShapeDtypeStruct((n_rows, rows[k].width), F32) for k in diff_rows]
    out_shape += [jax.ShapeDtypeStruct(bcs[k].shape, F32) for k in diff_bcs]
    out_specs = [pl.BlockSpec((tm, rows[k].width), lambda i: (i, 0)) for k in diff_rows]
    out_specs += [_bc_spec(bcs[k], ncb) for k in diff_bcs]
    ins = [r.arr for r in rows] + list(bcs) + [c.arr for c in cts + extra]
    in_specs = [r.spec(tm) for r in rows] + [_bc_spec(b, ncb) for b in bcs] + [c.spec(tm) for c in cts + extra]
    if lat_add is not None:
        ins.append(lat_add)
        skip = ncb if lat_add.shape[0] != n_rows else 0
        in_specs.append(pl.BlockSpec((tm, lat_add.shape[1]), lambda i: (jnp.maximum(i - skip, 0), 0)))
    outs = pl.pallas_call(
        body, out_shape=out_shape, grid=(n_rows // tm,), in_specs=in_specs, out_specs=out_specs,
        compiler_params=_params(("arbitrary",), VMEM_LIMIT), name=name)(*ins)
    return outs


def _rms(x):
    return x * lax.rsqrt(jnp.mean(x * x, axis=-1, keepdims=True) + EPS)


def _sigmoid(x):
    return 0.5 * (jnp.tanh(0.5 * x) + 1.0)


def _silu(x):
    return x * _sigmoid(x)


def _gelu_tanh(x):
    return 0.5 * x * (1.0 + jnp.tanh(math.sqrt(2.0 / math.pi) * (x + 0.044715 * (x * x * x))))


def f_norm_mod(x, g, sc, sh):
    return ((_rms(x) * g) * (1.0 + sc) + sh,)


def f_rms(x, g):
    return (_rms(x) * g,)


def f_gate(o, z):
    return (o * _silu(z),)


def f_s5_act(y, u, d):
    return (_gelu_tanh(y + d * u),)


def f_s5_glu(ya, gl, z, b):
    return (ya * _sigmoid(gl + b) * _silu(z),)


def mla_post_fwd(o, p0, x0, gate, w_out, n_ctx, name="l0_post"):
    n, d = o.shape
    tm = math.gcd(ROW_TILE, n, n_ctx)
    ncb = n_ctx // tm

    def body(o_ref, z_ref, x_ref, gt_ref, w_ref, x1_ref, og_ref, out_ref):
        og = f_gate(o_ref[...], z_ref[...])[0].astype(BF16)
        out = jnp.dot(og, w_ref[...], preferred_element_type=F32)
        og_ref[...] = og
        out_ref[...] = out
        x1_ref[...] = x_ref[...] + gt_ref[0] * out

    row = pl.BlockSpec((tm, d), lambda i: (i, 0))
    return pl.pallas_call(
        body, out_shape=[jax.ShapeDtypeStruct((n, d), F32), jax.ShapeDtypeStruct((n, d), BF16),
                         jax.ShapeDtypeStruct((n, d), F32)],
        grid=(n // tm,),
        in_specs=[row, row, row, _bc_spec(gate, ncb), pl.BlockSpec((d, d), lambda i: (0, 0))],
        out_specs=[row, row, row],
        compiler_params=_params(("parallel",), VMEM_LIMIT), name=name)(o, p0, x0, gate, w_out)


def mla_post_bwd(dx1, out, og, o, p0, gate, w_out, n_ctx, name="l0_post_bwd"):
    n, d = o.shape
    tm = math.gcd(ROW_TILE, n, n_ctx)
    ncb = n_ctx // tm

    def body(dx_ref, out_ref, og_ref, o_ref, z_ref, gt_ref, w_ref, do_ref, dz_ref, dgt_ref, dw_ref):
        i = pl.program_id(0)

        @pl.when(i == 0)
        def _():
            dw_ref[...] = jnp.zeros_like(dw_ref)

        @pl.when((i == 0) | (i == ncb))
        def _():
            dgt_ref[...] = jnp.zeros_like(dgt_ref)

        dx = dx_ref[...]
        dgt_ref[0] += jnp.sum(dx * out_ref[...], axis=0, keepdims=True)
        d_out16 = (gt_ref[0] * dx).astype(BF16)
        dw_ref[...] += lax.dot_general(og_ref[...], d_out16, (((0,), (0,)), ((), ())), preferred_element_type=F32)
        d_og = lax.dot_general(d_out16, w_ref[...], NT_DIMS, preferred_element_type=F32)
        _, gate_vjp = jax.vjp(lambda o_, z_: f_gate(o_, z_), o_ref[...], z_ref[...])
        d_o, d_z = gate_vjp((d_og,))
        do_ref[...] = d_o
        dz_ref[...] = d_z

    row = pl.BlockSpec((tm, d), lambda i: (i, 0))
    mat = pl.BlockSpec((d, d), lambda i: (0, 0))
    return pl.pallas_call(
        body, out_shape=[jax.ShapeDtypeStruct((n, d), F32), jax.ShapeDtypeStruct((n, d), F32),
                         jax.ShapeDtypeStruct(gate.shape, F32), jax.ShapeDtypeStruct((d, d), F32)],
        grid=(n // tm,),
        in_specs=[row, row, row, row, row, _bc_spec(gate, ncb), mat],
        out_specs=[row, row, _bc_spec(gate, ncb), mat],
        compiler_params=_params(("arbitrary",), VMEM_LIMIT), name=name)(dx1, out, og, o, p0, gate, w_out)


def s5_tail(y_ssm, p1, x1p, target, n_ctx, d_vec, b_glu, gate, final_g, w_glu, w_out, name="l1_tail"):
    n, d = y_ssm.shape
    tm = math.gcd(ROW_TILE, n, n_ctx)
    off = n_ctx // tm
    tn_dims = (((0,), (0,)), ((), ()))

    def row_loss(x, g, t):
        e = _rms(x) * g - t
        return 0.5 * (e * e) * (1.0 / d)

    def body(y_ref, u_ref, z_ref, x1_ref, t_ref, d_ref, b_ref, gt_ref, fg_ref, wg_ref, wo_ref,
             l_ref, dx_ref, dy_ref, du_ref, dz_ref, dfg_ref, dgt_ref, db_ref, dd_ref, dwg_ref, dwo_ref):
        @pl.when(pl.program_id(0) == 0)
        def _():
            for r in (l_ref, dfg_ref, dgt_ref, db_ref, dd_ref, dwg_ref, dwo_ref):
                r[...] = jnp.zeros_like(r)

        u, z, tgt, gt = u_ref[...], z_ref[...], t_ref[...], gt_ref[...]
        (ya,), act_vjp = jax.vjp(lambda y_, u_, d_: f_s5_act(y_, u_, d_), y_ref[...], u, d_ref[...])
        ya16 = ya.astype(BF16)
        gl = jnp.dot(ya16, wg_ref[...], preferred_element_type=F32)
        (y3,), glu_vjp = jax.vjp(lambda a_, g_, z_, b_: f_s5_glu(a_, g_, z_, b_), ya, gl, z, b_ref[...])
        y3_16 = y3.astype(BF16)
        out1 = jnp.dot(y3_16, wo_ref[...], preferred_element_type=F32)
        lterm, loss_vjp = jax.vjp(lambda x_, g_: row_loss(x_, g_, tgt), x1_ref[...] + gt * out1, fg_ref[...])
        dx2, dfg = loss_vjp(jnp.ones_like(lterm))
        l_ref[...] += jnp.sum(lterm, axis=0, keepdims=True)
        dfg_ref[...] += dfg
        dx_ref[...] = dx2
        dgt_ref[...] += jnp.sum(dx2 * out1, axis=0, keepdims=True)
        d_out16 = (gt * dx2).astype(BF16)
        dwo_ref[...] += lax.dot_general(y3_16, d_out16, tn_dims, preferred_element_type=F32)
        d_y3 = lax.dot_general(d_out16, wo_ref[...], NT_DIMS, preferred_element_type=F32)
        d_ya, d_gl, d_z, d_b = glu_vjp((d_y3,))
        dz_ref[...] = d_z
        db_ref[...] += d_b
        d_gl16 = d_gl.astype(BF16)
        dwg_ref[...] += lax.dot_general(ya16, d_gl16, tn_dims, preferred_element_type=F32)
        d_ya = d_ya + lax.dot_general(d_gl16, wg_ref[...], NT_DIMS, preferred_element_type=F32)
        d_y, d_u, d_d = act_vjp((d_ya,))
        dy_ref[...] = d_y
        du_ref[...] = d_u
        dd_ref[...] += d_d

    row = pl.BlockSpec((tm, d), lambda i: (i, 0))
    vecs = pl.BlockSpec((1, d), lambda i: (0, 0))
    mat = pl.BlockSpec((d, d), lambda i: (0, 0))
    return pl.pallas_call(
        body,
        out_shape=[jax.ShapeDtypeStruct((1, d), F32)] + [jax.ShapeDtypeStruct((n, d), F32)] * 4
        + [jax.ShapeDtypeStruct((1, d), F32)] * 4 + [jax.ShapeDtypeStruct((d, d), F32)] * 2,
        grid=(n // tm,),
        in_specs=[row, pl.BlockSpec((tm, d), lambda i: (i + off, 0)), pl.BlockSpec((tm, d), lambda i: (i + off, 1)),
                  pl.BlockSpec((tm, d), lambda i: (i + off, 0)), row, vecs, vecs, vecs, vecs, mat, mat],
        out_specs=[vecs, row, row, row, row, vecs, vecs, vecs, vecs, mat, mat],
        compiler_params=_params(("arbitrary",), VMEM_LIMIT), name=name)(
            y_ssm, p1, p1, x1p, target, d_vec, b_glu, gate, final_g, w_glu, w_out)


NT_DIMS = (((1,), (1,)), ((), ()))
HEAD_LANES = 128
N_PAIRS = MLA_HEADS // 2


def _own_lanes(shape, hh):
    lane = lax.broadcasted_iota(jnp.int32, shape, len(shape) - 1)
    return (lane < V_HEAD_DIM) if hh == 0 else (lane >= V_HEAD_DIM)


def _rope_tiles(x, cos, sin_next, sin_prev, inverse):
    width = x.shape[-1]
    reps = width // HEAD_LANES
    c, sn, sp = (jnp.tile(t, (1, reps)) for t in (cos, sin_next, sin_prev))
    if inverse:
        return x * c + pltpu.roll(x * sn, 8, 1) + pltpu.roll(x * sp, width - 8, 1)
    return x * c + pltpu.roll(x, width - 8, 1) * sn + pltpu.roll(x, 8, 1) * sp


def _col_to_row(col):
    n = col.shape[0]
    hi = col.astype(BF16)
    r1 = col - hi.astype(F32)
    mid = r1.astype(BF16)
    lo = (r1 - mid.astype(F32)).astype(BF16)
    lane = lax.broadcasted_iota(jnp.int32, (n, HEAD_LANES), 1)
    terms = jnp.where(lane == 0, hi, jnp.where(lane == 1, mid, jnp.where(lane == 2, lo, jnp.zeros_like(hi))))
    eye = (lax.broadcasted_iota(jnp.int32, (n, n), 0) == lax.broadcasted_iota(jnp.int32, (n, n), 1)).astype(BF16)
    rows = lax.dot_general(terms, eye, (((0,), (0,)), ((), ())), preferred_element_type=F32)
    return rows[0:1] + rows[1:2] + rows[2:3]


def attn_fwd(qb, kb, vb, n_ctx):
    T = qb.shape[0]
    tq = math.gcd(ROW_TILE, n_ctx)
    nq, ncb = T // tq, n_ctx // tq

    def body(q_ref, k_ref, v_ref, o_ref, lse_ref, lse_row_ref):
        qi = pl.program_id(1)

        def rows(n_keys):
            v = v_ref[:n_keys, :]
            outs = []
            for hh in range(2):
                hs = slice(hh * HEAD_LANES, (hh + 1) * HEAD_LANES)
                s = lax.dot_general(q_ref[:, hs], k_ref[:n_keys, hs], NT_DIMS,
                                    preferred_element_type=F32) * SOFTMAX_SCALE
                m = jnp.max(s, axis=-1, keepdims=True)
                p = jnp.exp(s - m)
                l = jnp.sum(p, axis=-1, keepdims=True)
                outs.append(jnp.dot(p.astype(BF16), v, preferred_element_type=F32) / l)
                lse = m + jnp.log(l)
                lse_ref[hh] = lse
                lse_row_ref[hh] = _col_to_row(lse)
            o_ref[...] = jnp.where(_own_lanes(outs[0].shape, 0), outs[0], outs[1])

        pl.when(qi < ncb)(lambda: rows(n_ctx))
        pl.when(qi >= ncb)(lambda: rows(T))

    return pl.pallas_call(
        body,
        out_shape=[jax.ShapeDtypeStruct((T, MLA_HEADS * V_HEAD_DIM), F32),
                   jax.ShapeDtypeStruct((MLA_HEADS, T, 1), F32), jax.ShapeDtypeStruct((MLA_HEADS, 1, T), F32)],
        grid=(N_PAIRS, nq),
        in_specs=[pl.BlockSpec((tq, 2 * HEAD_LANES), lambda h, i: (i, h)),
                  pl.BlockSpec((T, 2 * HEAD_LANES), lambda h, i: (0, h)),
                  pl.BlockSpec((T, 2 * V_HEAD_DIM), lambda h, i: (0, h))],
        out_specs=[pl.BlockSpec((tq, 2 * V_HEAD_DIM), lambda h, i: (i, h)),
                   pl.BlockSpec((2, tq, 1), lambda h, i: (h, i, 0)),
                   pl.BlockSpec((2, 1, tq), lambda h, i: (h, 0, i))],
        compiler_params=_params(("parallel", "parallel"), VMEM_LIMIT), name="attn_fwd")(qb, kb, vb)


def attn_bwd_dq(qb, kb, vb, o, do, lse, tabs, n_ctx):
    T = qb.shape[0]
    tq = math.gcd(ROW_TILE, n_ctx)
    nq, ncb = T // tq, n_ctx // tq

    def body(q_ref, k_ref, v_ref, o_ref, do_ref, lse_ref, c_ref, sn_ref, sp_ref, dq_ref, delta_ref):
        qi = pl.program_id(1)

        def rows(n_keys):
            v = v_ref[:n_keys, :]
            dqs = []
            for hh in range(2):
                hs = slice(hh * HEAD_LANES, (hh + 1) * HEAD_LANES)
                k = k_ref[:n_keys, hs]
                do = jnp.where(_own_lanes(do_ref.shape, hh), do_ref[...], 0.0)
                delta = jnp.sum(do * o_ref[...], axis=-1, keepdims=True)
                s = lax.dot_general(q_ref[:, hs], k, NT_DIMS, preferred_element_type=F32) * SOFTMAX_SCALE
                p = jnp.exp(s - lse_ref[hh])
                dp = lax.dot_general(do.astype(BF16), v, NT_DIMS, preferred_element_type=F32)
                ds = p * (dp - delta) * SOFTMAX_SCALE
                dqs.append(jnp.dot(ds.astype(BF16), k, preferred_element_type=F32))
                delta_ref[hh] = _col_to_row(delta)
            dq = jnp.concatenate(dqs, axis=1)
            dq_ref[...] = _rope_tiles(dq, c_ref[...], sn_ref[...], sp_ref[...], True).astype(BF16)

        pl.when(qi < ncb)(lambda: rows(n_ctx))
        pl.when(qi >= ncb)(lambda: rows(T))

    tab = pl.BlockSpec((tq, HEAD_LANES), lambda h, i: (i, 0))
    return pl.pallas_call(
        body,
        out_shape=[jax.ShapeDtypeStruct((T, MLA_HEADS * HEAD_LANES), BF16),
                   jax.ShapeDtypeStruct((MLA_HEADS, 1, T), F32)],
        grid=(N_PAIRS, nq),
        in_specs=[pl.BlockSpec((tq, 2 * HEAD_LANES), lambda h, i: (i, h)),
                  pl.BlockSpec((T, 2 * HEAD_LANES), lambda h, i: (0, h)),
                  pl.BlockSpec((T, 2 * V_HEAD_DIM), lambda h, i: (0, h)),
                  pl.BlockSpec((tq, 2 * V_HEAD_DIM), lambda h, i: (i, h)),
                  pl.BlockSpec((tq, 2 * V_HEAD_DIM), lambda h, i: (i, h)),
                  pl.BlockSpec((2, tq, 1), lambda h, i: (h, i, 0)), tab, tab, tab],
        out_specs=[pl.BlockSpec((tq, 2 * HEAD_LANES), lambda h, i: (i, h)),
                   pl.BlockSpec((2, 1, tq), lambda h, i: (h, 0, i))],
        compiler_params=_params(("parallel", "parallel"), VMEM_LIMIT), name="attn_bwd_dq")(
            qb, kb, vb, o, do, lse, *tabs)


def attn_bwd_dkv(qb, kb, vb, do, lse_rows, delta_rows, n_ctx):
    T = qb.shape[0]
    tq = math.gcd(ROW_TILE, n_ctx)
    nq, ncb = T // tq, n_ctx // tq

    def body(q_ref, do_ref, lse_ref, delta_ref, k_ref, v_ref, dk_ref, dv_ref):
        kj = pl.program_id(1)

        def cols(first):
            v = v_ref[...]
            do_all = do_ref[first:, :]
            dv = None
            for hh in range(2):
                hs = slice(hh * HEAD_LANES, (hh + 1) * HEAD_LANES)
                k = k_ref[:, hs]
                q = q_ref[first:, hs]
                do16 = jnp.where(_own_lanes(do_all.shape, hh), do_all, 0.0).astype(BF16)
                st = lax.dot_general(k, q, NT_DIMS, preferred_element_type=F32) * SOFTMAX_SCALE
                pt = jnp.exp(st - lse_ref[hh, :, first:])
                dv_h = jnp.dot(pt.astype(BF16), do16, preferred_element_type=F32)
                dv = dv_h if dv is None else dv + dv_h
                dpt = lax.dot_general(v, do16, NT_DIMS, preferred_element_type=F32)
                dst = pt * (dpt - delta_ref[hh, :, first:]) * SOFTMAX_SCALE
                dk_ref[:, hs] = jnp.dot(dst.astype(BF16), q, preferred_element_type=F32)
            dv_ref[...] = dv

        pl.when(kj < ncb)(lambda: cols(0))
        pl.when(kj >= ncb)(lambda: cols(n_ctx))

    return pl.pallas_call(
        body,
        out_shape=[jax.ShapeDtypeStruct((T, MLA_HEADS * HEAD_LANES), F32),
                   jax.ShapeDtypeStruct((T, MLA_HEADS * V_HEAD_DIM), F32)],
        grid=(N_PAIRS, nq),
        in_specs=[pl.BlockSpec((T, 2 * HEAD_LANES), lambda h, j: (0, h)),
                  pl.BlockSpec((T, 2 * V_HEAD_DIM), lambda h, j: (0, h)),
                  pl.BlockSpec((2, 1, T), lambda h, j: (h, 0, 0)),
                  pl.BlockSpec((2, 1, T), lambda h, j: (h, 0, 0)),
                  pl.BlockSpec((tq, 2 * HEAD_LANES), lambda h, j: (j, h)),
                  pl.BlockSpec((tq, 2 * V_HEAD_DIM), lambda h, j: (j, h))],
        out_specs=[pl.BlockSpec((tq, 2 * HEAD_LANES), lambda h, j: (j, h)),
                   pl.BlockSpec((tq, 2 * V_HEAD_DIM), lambda h, j: (j, h))],
        compiler_params=_params(("parallel", "parallel"), VMEM_LIMIT), name="attn_bwd_dkv")(
            qb, do, lse_rows, delta_rows, kb, vb)


def _split_bf16(x):
    hi = x.astype(BF16)
    return hi, (x - hi.astype(F32)).astype(BF16)


def q_heads(qn, w_uq_p, tabs, name="l0_uq"):
    T, K = qn.shape
    N = w_uq_p.shape[1]
    tm = math.gcd(ROW_TILE, T)

    def body(a_ref, w_ref, c_ref, sn_ref, sp_ref, o_ref):
        acc = jnp.dot(a_ref[...], w_ref[...], preferred_element_type=F32)
        o_ref[...] = _rope_tiles(acc, c_ref[...], sn_ref[...], sp_ref[...], False).astype(BF16)

    tab = pl.BlockSpec((tm, HEAD_LANES), lambda i: (i, 0))
    return pl.pallas_call(
        body, out_shape=jax.ShapeDtypeStruct((T, N), BF16), grid=(T // tm,),
        in_specs=[pl.BlockSpec((tm, K), lambda i: (i, 0)), pl.BlockSpec((K, N), lambda i: (0, 0)), tab, tab, tab],
        out_specs=pl.BlockSpec((tm, N), lambda i: (i, 0)),
        compiler_params=_params(("parallel",), VMEM_LIMIT), name=name)(qn, w_uq_p, *tabs)


def kv_heads(kvn, w_kn_p, w_v, kr, spread, tabs, name="l0_ukv"):
    T, K = kvn.shape
    N = w_kn_p.shape[1]
    NV = w_v.shape[1]
    tm = math.gcd(ROW_TILE, T)

    def body(a_ref, wk_ref, wv_ref, kr_ref, e_ref, c_ref, sn_ref, sp_ref, k_ref, v_ref):
        a = a_ref[...]
        hi, lo = _split_bf16(kr_ref[...])
        acc = (jnp.dot(a, wk_ref[...], preferred_element_type=F32)
               + jnp.dot(hi, e_ref[...], preferred_element_type=F32)
               + jnp.dot(lo, e_ref[...], preferred_element_type=F32))
        k_ref[...] = _rope_tiles(acc, c_ref[...], sn_ref[...], sp_ref[...], False).astype(BF16)
        v_ref[...] = jnp.dot(a, wv_ref[...], preferred_element_type=F32).astype(BF16)

    tab = pl.BlockSpec((tm, HEAD_LANES), lambda i: (i, 0))
    return pl.pallas_call(
        body, out_shape=[jax.ShapeDtypeStruct((T, N), BF16), jax.ShapeDtypeStruct((T, NV), BF16)], grid=(T // tm,),
        in_specs=[pl.BlockSpec((tm, K), lambda i: (i, 0)), pl.BlockSpec((K, N), lambda i: (0, 0)),
                  pl.BlockSpec((K, NV), lambda i: (0, 0)), pl.BlockSpec((tm, QK_ROPE_DIM), lambda i: (i, 0)),
                  pl.BlockSpec((QK_ROPE_DIM, N), lambda i: (0, 0)), tab, tab, tab],
        out_specs=[pl.BlockSpec((tm, N), lambda i: (i, 0)), pl.BlockSpec((tm, NV), lambda i: (i, 0))],
        compiler_params=_params(("parallel",), VMEM_LIMIT), name=name)(kvn, w_kn_p, w_v, kr, spread, *tabs)


def heads_unrope(d, tabs, spread=None, name="unrope"):
    T, N = d.shape
    tm = math.gcd(ROW_TILE, T)

    def body(*refs):
        if spread is None:
            d_ref, c_ref, sn_ref, sp_ref, o_ref = refs
        else:
            d_ref, c_ref, sn_ref, sp_ref, e_ref, o_ref, kr_ref = refs
        g = _rope_tiles(d_ref[...], c_ref[...], sn_ref[...], sp_ref[...], True)
        o_ref[...] = g.astype(BF16)
        if spread is not None:
            hi, lo = _split_bf16(g)
            kr_ref[...] = (lax.dot_general(hi, e_ref[...], NT_DIMS, preferred_element_type=F32)
                           + lax.dot_general(lo, e_ref[...], NT_DIMS, preferred_element_type=F32))

    tab = pl.BlockSpec((tm, HEAD_LANES), lambda i: (i, 0))
    row = pl.BlockSpec((tm, N), lambda i: (i, 0))
    ins, in_specs = [d, *tabs], [row, tab, tab, tab]
    out_shape, out_specs = [jax.ShapeDtypeStruct((T, N), BF16)], [row]
    if spread is not None:
        ins.append(spread)
        in_specs.append(pl.BlockSpec(spread.shape, lambda i: (0, 0)))
        out_shape.append(jax.ShapeDtypeStruct((T, spread.shape[0]), F32))
        out_specs.append(pl.BlockSpec((tm, spread.shape[0]), lambda i: (i, 0)))
    return pl.pallas_call(
        body, out_shape=out_shape, grid=(T // tm,), in_specs=in_specs, out_specs=out_specs,
        compiler_params=_params(("parallel",), VMEM_LIMIT), name=name)(*ins)


def _cmul(ar, ai, br, bi):
    return ar * br - ai * bi, ar * bi + ai * br


def s5_chain(finals, s0, a, n_steps, reverse, name):
    W = finals.shape[-1]
    first = N_SEG - 1 if reverse else 0

    def body(f_ref, s0_ref, a_ref, c_ref):
        pr, pi = jnp.ones((1, W), F32), jnp.zeros((1, W), F32)
        br, bi = a_ref[0], a_ref[1]
        n = n_steps
        while n:
            if n & 1:
                pr, pi = _cmul(pr, pi, br, bi)
            br, bi = _cmul(br, bi, br, bi)
            n >>= 1
        fr, fi = f_ref[0], f_ref[1]
        row = lax.broadcasted_iota(jnp.int32, (N_SEG, W), 0)
        s0r = jnp.broadcast_to(s0_ref[0], (N_SEG, W))
        s0i = jnp.broadcast_to(s0_ref[1], (N_SEG, W))
        cr = jnp.where(row == first, s0r, 0.0)
        ci = jnp.where(row == first, s0i, 0.0)
        shift = N_SEG - 1 if reverse else 1
        for _ in range(N_SEG - 1):
            mr, mi = _cmul(pr, pi, cr, ci)
            tr = pltpu.roll(fr + mr, shift, 0)
            ti = pltpu.roll(fi + mi, shift, 0)
            cr = jnp.where(row == first, s0r, tr)
            ci = jnp.where(row == first, s0i, ti)
        c_ref[0] = cr
        c_ref[1] = ci

    return pl.pallas_call(body, out_shape=jax.ShapeDtypeStruct((2, N_SEG, W), F32), name=name)(finals, s0, a)


def _scan_chunk(bur, bui, st_ref, a_ref, n_steps, reverse):
    for lc in range(S5_LANES // BLK_ST):
        sl = slice(lc * BLK_ST, (lc + 1) * BLK_ST)
        lr = jnp.broadcast_to(a_ref[0, :, sl], (N_SEG, BLK_ST))
        li = jnp.broadcast_to(a_ref[1, :, sl], (N_SEG, BLK_ST))

        def step(jj, carry, sl=sl, lr=lr, li=li):
            sr, si = carry
            j = (n_steps - 1 - jj) if reverse else jj
            r0 = pl.multiple_of(j * N_SEG, N_SEG)
            nr = lr * sr - li * si + bur[pl.ds(r0, N_SEG), sl]
            ni = lr * si + li * sr + bui[pl.ds(r0, N_SEG), sl]
            bur[pl.ds(r0, N_SEG), sl] = nr
            bui[pl.ds(r0, N_SEG), sl] = ni
            return nr, ni

        sr, si = lax.fori_loop(0, n_steps, step, (st_ref[0, :, sl], st_ref[1, :, sl]))
        st_ref[0, :, sl] = sr
        st_ref[1, :, sl] = si


def _project_in(x16, w_re, w_im, bur, bui, adjoint):
    for gb in range(N_BLOCKS):
        xb = x16[:, gb * BLK_CH:(gb + 1) * BLK_CH]
        sl = slice(gb * BLK_ST, (gb + 1) * BLK_ST)
        if adjoint:
            dn = (((1,), (1,)), ((), ()))
            bur[:, sl] = lax.dot_general(xb, w_re[gb], dn, preferred_element_type=F32)
            bui[:, sl] = -lax.dot_general(xb, w_im[gb], dn, preferred_element_type=F32)
        else:
            bur[:, sl] = jnp.dot(xb, w_re[gb], preferred_element_type=F32)
            bui[:, sl] = jnp.dot(xb, w_im[gb], preferred_element_type=F32)


def s5_scan(act, w_re, w_im, a, init, *, reverse, adjoint=False, c_re=None, c_im=None, add=None,
            want_ckpt=False, rows=None, name):
    act_off, N = rows if rows is not None else (0, act.shape[0])
    R = math.gcd(ROW_TILE, N, act_off)
    nch, jc = N // R, R // N_SEG
    with_out = c_re is not None

    def chunk(i):
        return (nch - 1 - i) if reverse else i

    def body(*refs):
        act_ref, wre_ref, wim_ref, a_ref, init_ref = refs[:5]
        k = 5
        if with_out:
            cre_ref, cim_ref = refs[k:k + 2]
            k += 2
        if add is not None:
            add_ref = refs[k]
            k += 1
        if with_out:
            out_ref = refs[k]
            k += 1
        if want_ckpt:
            ck_ref = refs[k]
            k += 1
        fin_ref, bur, bui = refs[k:k + 3]

        @pl.when(pl.program_id(0) == 0)
        def _():
            fin_ref[...] = init_ref[...]

        if want_ckpt:
            ck_ref[0] = fin_ref[...]
        _project_in(act_ref[...].astype(BF16), wre_ref, wim_ref, bur, bui, adjoint)
        _scan_chunk(bur, bui, fin_ref, a_ref, jc, reverse)
        if with_out:
            for gb in range(N_BLOCKS):
                sl = slice(gb * BLK_ST, (gb + 1) * BLK_ST)
                y = (jnp.dot(bur[:, sl].astype(BF16), cre_ref[gb], preferred_element_type=F32)
                     - jnp.dot(bui[:, sl].astype(BF16), cim_ref[gb], preferred_element_type=F32))
                cs = slice(gb * BLK_CH, (gb + 1) * BLK_CH)
                if add is not None:
                    y = y + add_ref[:, cs]
                out_ref[:, cs] = y

    row_spec = pl.BlockSpec((R, D_MODEL), lambda i: (chunk(i), 0))
    act_spec = pl.BlockSpec((R, D_MODEL), lambda i: (chunk(i) + act_off // R, 0))
    w_spec = pl.BlockSpec(w_re.shape, lambda i: (0, 0, 0))
    st_spec = pl.BlockSpec((2, N_SEG, S5_LANES), lambda i: (0, 0, 0))
    ins = [act, w_re, w_im, a, init]
    in_specs = [act_spec, w_spec, w_spec, pl.BlockSpec((2, 1, S5_LANES), lambda i: (0, 0, 0)), st_spec]
    if with_out:
        ins += [c_re, c_im]
        in_specs += [pl.BlockSpec(c_re.shape, lambda i: (0, 0, 0))] * 2
    if add is not None:
        ins.append(add)
        in_specs.append(row_spec)
    out_shape, out_specs = [], []
    if with_out:
        out_shape.append(jax.ShapeDtypeStruct((N, D_MODEL), F32))
        out_specs.append(row_spec)
    if want_ckpt:
        out_shape.append(jax.ShapeDtypeStruct((nch, 2, N_SEG, S5_LANES), F32))
        out_specs.append(pl.BlockSpec((1, 2, N_SEG, S5_LANES), lambda i: (chunk(i), 0, 0, 0)))
    out_shape.append(jax.ShapeDtypeStruct((2, N_SEG, S5_LANES), F32))
    out_specs.append(st_spec)
    res = pl.pallas_call(
        body, out_shape=out_shape, grid=(nch,), in_specs=in_specs, out_specs=out_specs,
        scratch_shapes=[pltpu.VMEM((R, S5_LANES), F32), pltpu.VMEM((R, S5_LANES), F32)],
        compiler_params=_params(("arbitrary",), VMEM_LIMIT), name=name)(*ins)
    res = list(res)
    out = res.pop(0) if with_out else None
    ckpt = res.pop(0) if want_ckpt else None
    return out, ckpt, res[0]


def s5_grads(dy, u, ckpt, b_re, b_im, c_re, c_im, lam, init_adj, *, reverse, add=None, u_off=0, name):
    N = dy.shape[0]
    R = math.gcd(ROW_TILE, N, u_off)
    nch, jc = N // R, R // N_SEG
    W = S5_LANES

    def chunk(i):
        return i if reverse else (nch - 1 - i)

    def body(*refs):
        dy_ref, u_ref, ck_ref, bre_ref, bim_ref, cre_ref, cim_ref, lam_ref, init_ref = refs[:9]
        k = 9
        if add is not None:
            add_ref = refs[k]
            k += 1
        du_ref, dlam_ref, dbre_ref, dbim_ref, dcre_ref, dcim_ref, fin_ref = refs[k:k + 7]
        sr_buf, si_buf, er_buf, ei_buf, st_buf = refs[k + 7:k + 12]

        @pl.when(pl.program_id(0) == 0)
        def _():
            fin_ref[...] = init_ref[...]
            dlam_ref[...] = jnp.zeros_like(dlam_ref)
            dbre_ref[...] = jnp.zeros_like(dbre_ref)
            dbim_ref[...] = jnp.zeros_like(dbim_ref)
            dcre_ref[...] = jnp.zeros_like(dcre_ref)
            dcim_ref[...] = jnp.zeros_like(dcim_ref)

        u16 = u_ref[...].astype(BF16)
        dy16 = dy_ref[...].astype(BF16)
        st_buf[...] = ck_ref[0]
        _project_in(u16, bre_ref, bim_ref, sr_buf, si_buf, False)
        _scan_chunk(sr_buf, si_buf, st_buf, lam_ref, jc, reverse)
        _project_in(dy16, cre_ref, cim_ref, er_buf, ei_buf, True)
        for lc in range(W // BLK_ST):
            sl = slice(lc * BLK_ST, (lc + 1) * BLK_ST)
            lr = jnp.broadcast_to(lam_ref[0, :, sl], (N_SEG, BLK_ST))
            li = jnp.broadcast_to(lam_ref[1, :, sl], (N_SEG, BLK_ST))

            def one(r0, spr, spi, carry, sl=sl, lr=lr, li=li):
                gr, gi, ar, ai = carry
                nr = er_buf[pl.ds(r0, N_SEG), sl] + lr * gr + li * gi
                ni = ei_buf[pl.ds(r0, N_SEG), sl] + lr * gi - li * gr
                er_buf[pl.ds(r0, N_SEG), sl] = nr
                ei_buf[pl.ds(r0, N_SEG), sl] = ni
                return nr, ni, ar + spr * nr + spi * ni, ai + spr * ni - spi * nr

            def step(ff, carry, sl=sl, one=one):
                f = jc - 1 - ff
                j = (jc - 1 - f) if reverse else f
                jp = (j + 1) if reverse else (j - 1)
                r0 = pl.multiple_of(j * N_SEG, N_SEG)
                p0 = pl.multiple_of(jp * N_SEG, N_SEG)
                return one(r0, sr_buf[pl.ds(p0, N_SEG), sl], si_buf[pl.ds(p0, N_SEG), sl], carry)

            carry = (fin_ref[0, :, sl], fin_ref[1, :, sl], dlam_ref[0, :, sl], dlam_ref[1, :, sl])
            carry = lax.fori_loop(0, jc - 1, step, carry)
            r_first = (jc - 1) * N_SEG if reverse else 0
            gr, gi, ar, ai = one(r_first, ck_ref[0, 0, :, sl], ck_ref[0, 1, :, sl], carry)
            fin_ref[0, :, sl] = gr
            fin_ref[1, :, sl] = gi
            dlam_ref[0, :, sl] = ar
            dlam_ref[1, :, sl] = ai
        tn = (((0,), (0,)), ((), ()))
        nt = (((1,), (1,)), ((), ()))
        for gb in range(N_BLOCKS):
            sl = slice(gb * BLK_ST, (gb + 1) * BLK_ST)
            cs = slice(gb * BLK_CH, (gb + 1) * BLK_CH)
            gr16 = er_buf[:, sl].astype(BF16)
            gi16 = ei_buf[:, sl].astype(BF16)
            du = (lax.dot_general(gr16, bre_ref[gb], nt, preferred_element_type=F32)
                  + lax.dot_general(gi16, bim_ref[gb], nt, preferred_element_type=F32))
            if add is not None:
                du = du + add_ref[:, cs]
            du_ref[:, cs] = du
            ub, dyb = u16[:, cs], dy16[:, cs]
            dbre_ref[gb] += lax.dot_general(ub, gr16, tn, preferred_element_type=F32)
            dbim_ref[gb] += lax.dot_general(ub, gi16, tn, preferred_element_type=F32)
            dcre_ref[gb] += lax.dot_general(sr_buf[:, sl].astype(BF16), dyb, tn, preferred_element_type=F32)
            dcim_ref[gb] -= lax.dot_general(si_buf[:, sl].astype(BF16), dyb, tn, preferred_element_type=F32)

    row_spec = pl.BlockSpec((R, D_MODEL), lambda i: (chunk(i), 0))
    st_spec = pl.BlockSpec((2, N_SEG, W), lambda i: (0, 0, 0))
    wb_spec = pl.BlockSpec(b_re.shape, lambda i: (0, 0, 0))
    wc_spec = pl.BlockSpec(c_re.shape, lambda i: (0, 0, 0))
    ins = [dy, u, ckpt, b_re, b_im, c_re, c_im, lam, init_adj]
    u_spec = pl.BlockSpec((R, D_MODEL), lambda i: (chunk(i) + u_off // R, 0))
    in_specs = [row_spec, u_spec, pl.BlockSpec((1, 2, N_SEG, W), lambda i: (chunk(i), 0, 0, 0)),
                wb_spec, wb_spec, wc_spec, wc_spec, pl.BlockSpec((2, 1, W), lambda i: (0, 0, 0)), st_spec]
    if add is not None:
        ins.append(add)
        in_specs.append(row_spec)
    out_shape = [jax.ShapeDtypeStruct((N, D_MODEL), F32), jax.ShapeDtypeStruct((2, N_SEG, W), F32),
                 jax.ShapeDtypeStruct(b_re.shape, F32), jax.ShapeDtypeStruct(b_re.shape, F32),
                 jax.ShapeDtypeStruct(c_re.shape, F32), jax.ShapeDtypeStruct(c_re.shape, F32),
                 jax.ShapeDtypeStruct((2, N_SEG, W), F32)]
    out_specs = [row_spec, st_spec, wb_spec, wb_spec, wc_spec, wc_spec, st_spec]
    return pl.pallas_call(
        body, out_shape=out_shape, grid=(nch,), in_specs=in_specs, out_specs=out_specs,
        scratch_shapes=[pltpu.VMEM((R, W), F32) for _ in range(4)] + [pltpu.VMEM((2, N_SEG, W), F32)],
        compiler_params=_params(("arbitrary",), VMEM_LIMIT), name=name)(*ins)


def adamw(w, g, m, v, name="adamw", after=None):
    n, d = w.shape
    lanes = -(-d // 128) * 128
    tm = n
    while tm * lanes * 4 > (1 << 20) and tm % 16 == 0:
        tm //= 2
    c1 = 1.0 - ADAM_B1 ** ADAM_STEP
    c2 = 1.0 - ADAM_B2 ** ADAM_STEP

    def body(w_ref, g_ref, m_ref, v_ref, *rest):
        d_ref, nm_ref, nv_ref = rest[-3:]
        g_ = g_ref[...]
        m_ = ADAM_B1 * m_ref[...] + (1.0 - ADAM_B1) * g_
        v_ = ADAM_B2 * v_ref[...] + (1.0 - ADAM_B2) * (g_ * g_)
        d_ref[...] = -ADAM_LR * ((m_ / c1) / (jnp.sqrt(v_ / c2) + ADAM_EPS) + ADAM_WD * w_ref[...])
        nm_ref[...] = m_
        nv_ref[...] = v_

    spec = pl.BlockSpec((tm, d), lambda i: (i, 0))
    extra = [] if after is None else [after]
    return pl.pallas_call(
        body, out_shape=[jax.ShapeDtypeStruct((n, d), F32)] * 3, grid=(n // tm,),
        in_specs=[spec] * 4 + [pl.BlockSpec(memory_space=pl.ANY)] * len(extra), out_specs=[spec] * 3,
        compiler_params=_params(("parallel",), VMEM_LIMIT), name=name)(w, g, m, v, *extra)


def _coords():
    return lax.axis_index("x"), lax.axis_index("y"), lax.axis_index("c")


def exchange(arrays, out_shapes, remote, local, name, aliases=None):
    n_in, n_out, n_rem, n_loc = len(arrays), len(out_shapes), len(remote), len(local)

    def at(ref, idx):
        return ref if idx is None else ref.at[idx]

    def body(*refs):
        ins, outs = refs[:n_in], refs[n_in:n_in + n_out]
        send_sems, recv_sems, local_sems = refs[n_in + n_out:]
        me = _coords()
        sends, recvs = [], []
        for k, (flip, ii, src_at, oi, dst_at) in enumerate(remote):
            peer = (me[0] ^ flip[0], me[1] ^ flip[1], me[2] ^ flip[2])
            src = at(ins[ii], src_at(me, peer))
            sends.append(pltpu.make_async_remote_copy(
                src_ref=src, dst_ref=at(outs[oi], dst_at(me)), send_sem=send_sems.at[k], recv_sem=recv_sems.at[k],
                device_id=peer, device_id_type=MESH))
            recvs.append(pltpu.make_async_remote_copy(
                src_ref=src, dst_ref=at(outs[oi], dst_at(peer)), send_sem=send_sems.at[k], recv_sem=recv_sems.at[k],
                device_id=peer, device_id_type=MESH))
        locs = [pltpu.make_async_copy(at(ins[ii], src_at(me)), at(outs[oi], dst_at(me)), local_sems.at[k])
                for k, (ii, src_at, oi, dst_at) in enumerate(local)]
        for cp in locs + sends:
            cp.start()
        for cp in recvs:
            cp.wait_recv()
        for cp in sends:
            cp.wait_send()
        for cp in locs:
            cp.wait()

    hbm = pl.BlockSpec(memory_space=pl.ANY)
    return pl.pallas_call(
        body, out_shape=list(out_shapes), in_specs=[hbm] * n_in, out_specs=[hbm] * n_out,
        scratch_shapes=[pltpu.SemaphoreType.DMA((n_rem,)), pltpu.SemaphoreType.DMA((n_rem,)),
                        pltpu.SemaphoreType.DMA((max(n_loc, 1),))],
        input_output_aliases=aliases or {}, name=name)(*arrays)


ALL_FLIPS = [(dx, dy, dc) for dx in (0, 1) for dy in (0, 1) for dc in (0, 1)][1:]
CHIP_FLIPS = [(1, 0, 0), (0, 1, 0), (1, 1, 0)]
CORE_FLIP = (0, 0, 1)


def _dev_index(p):
    return 4 * p[0] + 2 * p[1] + p[2]


def _chip_index(p):
    return 2 * p[0] + p[1]


def _gather(xs, flips, index, n, name):
    arrays = [x[None] for x in xs]
    outs = [jax.ShapeDtypeStruct((n,) + x.shape, x.dtype) for x in xs]
    remote = [(f, a, lambda me, peer: (0,), a, lambda s: (index(s),)) for a in range(len(xs)) for f in flips]
    local = [(a, lambda me: (0,), a, lambda me: (index(me),)) for a in range(len(xs))]
    return exchange(arrays, outs, remote, local, name)


def allgather_devices(x, name):
    return _gather([x], ALL_FLIPS, _dev_index, N_DEV, name)[0]


def allgather_chips(xs, name):
    return _gather(xs, CHIP_FLIPS, _chip_index, N_CHIP, name)


def gather_halves(xs, name):
    n = len(xs)
    nk = n * len(CHIP_FLIPS)

    def body(*refs):
        ins, outs = refs[:n], refs[n:2 * n]
        ici_send, ici_recv, d2d_send, d2d_recv = refs[2 * n:]
        me = _coords()
        sibling = (me[0], me[1], 1 - me[2])
        first, passed, landed = [], [], []
        for a in range(n):
            half = ins[a].shape[0] // 2
            mine = ins[a].at[pl.ds(pl.multiple_of(me[2] * half, 16), half)]
            for j, flip in enumerate(CHIP_FLIPS):
                k = a * len(CHIP_FLIPS) + j
                peer = (me[0] ^ flip[0], me[1] ^ flip[1], me[2])
                first.append(pltpu.make_async_remote_copy(
                    src_ref=mine, dst_ref=outs[a].at[_chip_index(me), me[2]], send_sem=ici_send.at[k],
                    recv_sem=ici_recv.at[k], device_id=peer, device_id_type=MESH))
                arrived = outs[a].at[_chip_index(peer), me[2]]
                landed.append(pltpu.make_async_remote_copy(
                    src_ref=mine, dst_ref=arrived, send_sem=ici_send.at[k], recv_sem=ici_recv.at[k],
                    device_id=peer, device_id_type=MESH))
                passed.append(pltpu.make_async_remote_copy(
                    src_ref=arrived, dst_ref=arrived, send_sem=d2d_send.at[k], recv_sem=d2d_recv.at[k],
                    device_id=sibling, device_id_type=MESH))
        for cp in first:
            cp.start()
        for k in range(nk):
            landed[k].wait_recv()
            passed[k].start()
        for a in range(n):
            for j, flip in enumerate(CHIP_FLIPS):
                k = a * len(CHIP_FLIPS) + j
                peer_chip = _chip_index((me[0] ^ flip[0], me[1] ^ flip[1]))
                from_sibling = outs[a].at[peer_chip, 1 - me[2]]
                pltpu.make_async_remote_copy(
                    src_ref=from_sibling, dst_ref=from_sibling, send_sem=d2d_send.at[k], recv_sem=d2d_recv.at[k],
                    device_id=sibling, device_id_type=MESH).wait_recv()
        for cp in first + passed:
            cp.wait_send()

    hbm = pl.BlockSpec(memory_space=pl.ANY)
    return pl.pallas_call(
        body, out_shape=[jax.ShapeDtypeStruct((N_CHIP, 2, x.shape[0] // 2, x.shape[1]), x.dtype) for x in xs],
        in_specs=[hbm] * n, out_specs=[hbm] * n,
        scratch_shapes=[pltpu.SemaphoreType.DMA((nk,)) for _ in range(4)], name=name)(*xs)


HBM_SPEC = pl.BlockSpec(memory_space=pltpu.HBM)
SEM_SPEC = pl.BlockSpec(memory_space=pltpu.SEMAPHORE)
DATAFLOW = pltpu.SideEffectType.DATAFLOW_SIDE_EFFECTING


def _at(ref, idx):
    return ref if idx is None else ref.at[idx]


def _peer(me, flip):
    return (me[0] ^ flip[0], me[1] ^ flip[1], me[2] ^ flip[2])


def exchange_start(arrays, land_shapes, remote, name, after=None):
    n_in, n_out, nk = len(arrays), len(land_shapes), len(remote)
    n_after = 0 if after is None else 1

    def body(*refs):
        srcs, lands = refs[:n_in], refs[n_in:n_in + n_out]
        first_out = n_in + n_out + n_after
        send_sems, recv_sems, token = refs[first_out], refs[first_out + 1], refs[-1]
        me = _coords()
        for k, (flip, ii, src_at, oi, dst_at) in enumerate(remote):
            peer = _peer(me, flip)
            pltpu.make_async_remote_copy(
                src_ref=_at(srcs[ii], src_at(me, peer)), dst_ref=_at(lands[oi], dst_at(me)), send_sem=send_sems.at[k],
                recv_sem=recv_sems.at[k], device_id=peer, device_id_type=MESH).start()
        token[...] = jnp.zeros_like(token)

    lands = [lax.empty(s.shape, s.dtype) for s in land_shapes]
    bufs = list(arrays) + lands
    out = pl.pallas_call(
        body, name=name,
        out_shape=(pltpu.SemaphoreType.DMA((nk,)), pltpu.SemaphoreType.DMA((nk,)),
                   *[pltpu.HBM(b.shape, b.dtype) for b in bufs], jax.ShapeDtypeStruct((8, 128), F32)),
        in_specs=[HBM_SPEC] * len(bufs) + [pl.BlockSpec(memory_space=pl.ANY)] * n_after,
        out_specs=(SEM_SPEC, SEM_SPEC, *[HBM_SPEC] * len(bufs), pl.BlockSpec(memory_space=pltpu.VMEM)),
        input_output_aliases={a: 2 + a for a in range(len(bufs))},
        compiler_params=pltpu.CompilerParams(has_side_effects=DATAFLOW),
    )(*[pltpu.with_memory_space_constraint(b, pltpu.HBM) for b in bufs], *([after] if n_after else []))
    flight = (out[0], out[1], list(out[2:2 + n_in]), list(out[2 + n_in:2 + n_in + n_out]), remote)
    return flight, out[-1]


def exchange_wait(flight, after, name):
    send_sems, recv_sems, arrays, lands, remote = flight
    n_in, n_out = len(arrays), len(lands)

    def body(*refs):
        srcs, lnds = refs[:n_in], refs[n_in:n_in + n_out]
        s_sems, r_sems = refs[n_in + n_out], refs[n_in + n_out + 1]
        me = _coords()
        for k, (flip, ii, src_at, oi, dst_at) in enumerate(remote):
            peer = _peer(me, flip)
            copy = pltpu.make_async_remote_copy(
                src_ref=_at(srcs[ii], src_at(me, peer)), dst_ref=_at(lnds[oi], dst_at(peer)), send_sem=s_sems.at[k],
                recv_sem=r_sems.at[k], device_id=peer, device_id_type=MESH)
            copy.wait_send()
            copy.wait_recv()

    bufs = list(arrays) + list(lands)
    out = pl.pallas_call(
        body, name=name,
        out_shape=tuple(pltpu.HBM(b.shape, b.dtype) for b in bufs),
        in_specs=[HBM_SPEC] * len(bufs) + [SEM_SPEC, SEM_SPEC, pl.BlockSpec(memory_space=pl.ANY)],
        out_specs=tuple([HBM_SPEC] * len(bufs)),
        input_output_aliases={a: a for a in range(len(bufs))},
        compiler_params=pltpu.CompilerParams(has_side_effects=DATAFLOW),
    )(*bufs, send_sems, recv_sems, after)
    return list(out[:n_in]), list(out[n_in:])


def _half_tile(h, cd):
    return h if h * cd * 4 <= (1 << 20) else math.gcd(512, h)


def pair_add(g, got, core, out_dtype, name):
    _, _, h, cd = g.shape
    th = _half_tile(h, cd)

    def body(c_ref, g_ref, got_ref, o_ref):
        o_ref[0] = (g_ref[0, 0] + got_ref[0]).astype(o_ref.dtype)

    return pl.pallas_call(
        body, out_shape=jax.ShapeDtypeStruct((N_CHIP, h, cd), out_dtype),
        grid_spec=pltpu.PrefetchScalarGridSpec(
            num_scalar_prefetch=1, grid=(N_CHIP, h // th),
            in_specs=[pl.BlockSpec((1, 1, th, cd), lambda q, i, c: (q, c[0], i, 0)),
                      pl.BlockSpec((1, th, cd), lambda q, i, c: (q, i, 0))],
            out_specs=pl.BlockSpec((1, th, cd), lambda q, i, c: (q, i, 0))),
        compiler_params=_params(("parallel", "parallel"), VMEM_LIMIT), name=name)(core, g, got)


def sum_chips(parts, sums, place, name):
    _, h, cd = parts.shape
    th = _half_tile(h, cd)

    def body(pc_ref, p_ref, own_ref, o_ref):
        acc = None
        for q in range(N_CHIP):
            term = jnp.where(pc_ref[1] == q, own_ref[0], p_ref[q]).astype(F32)
            acc = term if acc is None else acc + term
        o_ref[0] = acc

    return pl.pallas_call(
        body, out_shape=jax.ShapeDtypeStruct((2, h, cd), F32),
        grid_spec=pltpu.PrefetchScalarGridSpec(
            num_scalar_prefetch=1, grid=(h // th,),
            in_specs=[pl.BlockSpec((N_CHIP, th, cd), lambda i, pc: (0, i, 0)),
                      pl.BlockSpec((1, th, cd), lambda i, pc: (pc[1], i, 0))],
            out_specs=pl.BlockSpec((1, th, cd), lambda i, pc: (pc[0], i, 0))),
        compiler_params=_params(("parallel",), VMEM_LIMIT), name=name)(place, parts, sums)


def to_segments(a, n_ctx):
    def one(p):
        n = p.shape[0]
        return p.reshape(N_SEG, n // N_SEG, -1).transpose(1, 0, 2).reshape(n, -1)
    return jnp.concatenate([one(a[:n_ctx]), one(a[n_ctx:])], axis=0) if n_ctx else one(a)


def from_segments(a, n_ctx):
    def one(p):
        n = p.shape[0]
        return p.reshape(n // N_SEG, N_SEG, -1).transpose(1, 0, 2).reshape(n, -1)
    return jnp.concatenate([one(a[:n_ctx]), one(a[n_ctx:])], axis=0) if n_ctx else one(a)


def rope_tables(n_ctx, n_lat):
    f32 = np.float32
    rows = n_lat // GRID_W
    row = np.repeat(np.arange(rows), GRID_W).astype(f32)
    col = np.tile(np.arange(GRID_W), rows).astype(f32)
    d = QK_ROPE_DIM // 2
    inv = (f32(1.0) / np.power(f32(ROPE_THETA), np.arange(0, d, 2, dtype=f32) / f32(d))).astype(f32)
    ang = np.concatenate([row[:, None] * inv[None, :], col[:, None] * inv[None, :]], axis=1).astype(f32)
    cos = np.concatenate([np.ones((n_ctx, d), f32), np.cos(ang)], axis=0)
    sin = np.concatenate([np.zeros((n_ctx, d), f32), np.sin(ang)], axis=0)
    q = QK_ROPE_DIM // 4
    T = n_ctx + n_lat
    ones, zeros = np.ones((T, QK_NOPE_DIM), f32), np.zeros((T, QK_NOPE_DIM), f32)
    tail, z8 = np.zeros((T, HEAD_LANES - QK_DIM), f32), np.zeros((T, q), f32)
    cr, cc, sr, sc = cos[:, :q], cos[:, q:], sin[:, :q], sin[:, q:]
    cos_t = np.concatenate([ones, cr, cr, cc, cc, tail], axis=1)
    sin_next = np.concatenate([zeros, -sr, z8, -sc, z8, tail], axis=1)
    sin_prev = np.concatenate([zeros, z8, sr, z8, sc, tail], axis=1)
    return tuple(jnp.asarray(t, F32) for t in (cos_t, sin_next, sin_prev))


def pad_heads(w, used):
    k = w.shape[0]
    return jnp.pad(w.reshape(k, MLA_HEADS, used), ((0, 0), (0, 0), (0, HEAD_LANES - used))).reshape(k, -1)


def unpad_heads(w, used):
    k = w.shape[0]
    return w.reshape(k, MLA_HEADS, HEAD_LANES)[:, :, :used].reshape(k, MLA_HEADS * used)


def rotary_spread():
    lane = np.arange(MLA_HEADS * HEAD_LANES) % HEAD_LANES
    return jnp.asarray(lane[None, :] == (QK_NOPE_DIM + np.arange(QK_ROPE_DIM))[:, None], BF16)


def s5_discretise(a_re, a_im, log_step, b_re, b_im):
    dt = jnp.exp(log_step)[:, None]
    mag = jnp.exp(a_re * dt)
    lb_re = mag * jnp.cos(a_im * dt)
    lb_im = mag * jnp.sin(a_im * dt)
    den = a_re * a_re + a_im * a_im
    nr = lb_re - 1.0
    f_re = ((nr * a_re + lb_im * a_im) / den)[..., None]
    f_im = ((lb_im * a_re - nr * a_im) / den)[..., None]
    return lb_re, lb_im, f_re * b_re - f_im * b_im, f_re * b_im + f_im * b_re


def s5_block_weights(lb_re, lb_im, bb_re, bb_im, c_re, c_im):
    eye = jnp.eye(GROUPS_PER_BLOCK, dtype=F32)
    lam = jnp.stack([lb_re.reshape(1, S5_LANES), lb_im.reshape(1, S5_LANES)])

    def b_blocks(bb):
        t = bb.reshape(N_BLOCKS, GROUPS_PER_BLOCK, S5_STATE, S5_GROUP)
        return jnp.einsum("bgpc,gh->bgchp", t, eye).reshape(N_BLOCKS, BLK_CH, BLK_ST).astype(BF16)

    def c_blocks(cc):
        t = cc.reshape(N_BLOCKS, GROUPS_PER_BLOCK, S5_GROUP, S5_STATE)
        return jnp.einsum("bgcp,gh->bgphc", t, eye).reshape(N_BLOCKS, BLK_ST, BLK_CH).astype(BF16)

    return lam, b_blocks(bb_re), b_blocks(bb_im), c_blocks(c_re), c_blocks(c_im)


def b_block_diag(db):
    t = db.reshape(N_BLOCKS, GROUPS_PER_BLOCK, S5_GROUP, GROUPS_PER_BLOCK, S5_STATE)
    return jnp.einsum("bgchp,gh->bgpc", t, jnp.eye(GROUPS_PER_BLOCK, dtype=F32)).reshape(S5_GROUPS, S5_STATE, S5_GROUP)


def c_block_diag(dc):
    t = dc.reshape(N_BLOCKS, GROUPS_PER_BLOCK, S5_STATE, GROUPS_PER_BLOCK, S5_GROUP)
    return jnp.einsum("bgphc,gh->bgcp", t, jnp.eye(GROUPS_PER_BLOCK, dtype=F32)).reshape(S5_GROUPS, S5_GROUP, S5_STATE)


def conj(a):
    return jnp.stack([a[0], -a[1]])


PACK_TILE = 16 * 128


def pack_flat(parts, dtype):
    flat = [p.reshape(-1).astype(dtype) for p in parts]
    sizes = [f.shape[0] for f in flat]
    total = sum(sizes)
    pad = (-total) % PACK_TILE
    if pad:
        flat.append(jnp.zeros((pad,), dtype))
    offs = np.cumsum([0] + sizes)[:-1].tolist()
    return jnp.concatenate(flat).reshape(-1, 128), offs


def unpack_flat(buf, offs, shapes):
    flat = buf.reshape(-1)
    return [flat[o:o + int(np.prod(s))].reshape(s) for o, s in zip(offs, shapes)]


def s5_forward(p1, n_ctx, dirs):
    saved = []
    y = None
    ctx_rows, lat_rows = (0, n_ctx), (n_ctx, p1.shape[0] - n_ctx)
    zeros_tile = jnp.zeros((2, N_SEG, S5_LANES), F32)
    zeros_row = jnp.zeros((2, 1, S5_LANES), F32)
    for k, (lam, b_re, b_im, c_re, c_im) in enumerate(dirs):
        rev = k == 1
        last = 0 if rev else N_SEG - 1
        _, _, fin = s5_scan(p1, b_re, b_im, lam, zeros_tile, reverse=rev, rows=ctx_rows, name=f"s5_ctx_finals{k}")
        carry_c = s5_chain(fin, zeros_row, lam, n_ctx // N_SEG, rev, name=f"s5_ctx_chain{k}")
        _, ck_c, fin_c = s5_scan(p1, b_re, b_im, lam, carry_c, reverse=rev, want_ckpt=True, rows=ctx_rows,
                                 name=f"s5_ctx_scan{k}")
        s0 = fin_c[:, last:last + 1, :]
        _, _, fin = s5_scan(p1, b_re, b_im, lam, zeros_tile, reverse=rev, rows=lat_rows, name=f"s5_lat_finals{k}")
        carry_l = s5_chain(fin, s0, lam, lat_rows[1] // N_SEG, rev, name=f"s5_lat_chain{k}")
        y, ck_l, _ = s5_scan(p1, b_re, b_im, lam, carry_l, reverse=rev, c_re=c_re, c_im=c_im, add=y,
                             want_ckpt=True, rows=lat_rows, name=f"s5_lat_scan{k}")
        saved.append((ck_c, ck_l))
    return y, saved


def s5_backward(dy_l, du_extra_l, p1, n_ctx, dirs, saved):
    n_lat = p1.shape[0] - n_ctx
    zeros_tile = jnp.zeros((2, N_SEG, S5_LANES), F32)
    zeros_row = jnp.zeros((2, 1, S5_LANES), F32)
    dy_c = jnp.zeros((n_ctx, D_MODEL), F32)
    du_l, du_c = du_extra_l, None
    grads = []
    for k, (lam, b_re, b_im, c_re, c_im) in enumerate(dirs):
        rev = k == 1
        lam_c = conj(lam)
        ck_c, ck_l = saved[k]
        first = N_SEG - 1 if rev else 0
        _, _, fin = s5_scan(dy_l, c_re, c_im, lam_c, zeros_tile, reverse=not rev, adjoint=True,
                            name=f"s5_lat_adj_finals{k}")
        carry = s5_chain(fin, zeros_row, lam_c, n_lat // N_SEG, not rev, name=f"s5_lat_adj_chain{k}")
        du_l, dlam_l, dbr_l, dbi_l, dcr_l, dci_l, fin_a = s5_grads(
            dy_l, p1, ck_l, b_re, b_im, c_re, c_im, lam, carry, reverse=rev, add=du_l, u_off=n_ctx,
            name=f"s5_lat_grads{k}")
        g0 = fin_a[:, first:first + 1, :]
        carry = s5_chain(zeros_tile, g0, lam_c, n_ctx // N_SEG, not rev, name=f"s5_ctx_adj_chain{k}")
        du_c, dlam_c, dbr_c, dbi_c, _, _, _ = s5_grads(
            dy_c, p1, ck_c, b_re, b_im, c_re, c_im, lam, carry, reverse=rev, add=du_c, name=f"s5_ctx_grads{k}")
        dlam = jnp.sum(dlam_l + dlam_c, axis=1)
        grads.append((dlam, b_block_diag(dbr_l + dbr_c), b_block_diag(dbi_l + dbi_c),
                      c_block_diag(dcr_l), c_block_diag(dci_l)))
    return jnp.concatenate([du_c, du_l], axis=0), grads


def local_step(x, ctx, target, mod, w, late=None, reducer=None):
    L, Lc = x.shape[0], ctx.shape[0]
    T = L + Lc
    assert L % Lc == 0 and Lc % (2 * N_SEG) == 0 and L % GRID_W == 0
    D = D_MODEL
    X0 = jnp.concatenate([ctx, x], axis=0)

    def mod_of(i, j):
        return mod[i, :, j, :][:, None, :]

    def vec(v):
        return v.reshape(1, 1, -1).astype(F32)

    g0 = vec(w["norm_g"][0])
    (H0,) = rowwise_fwd(f_norm_mod, [X0], [g0, mod_of(0, 1), mod_of(0, 0)], [D], [BF16], T, Lc, "l0_norm")
    p0 = mm_nn(H0, w["mla_w_in"], name="l0_in")
    cq = Rows(p0, Q_LORA_RANK, col_blk=D // Q_LORA_RANK)
    ckv = Rows(p0, KV_LORA_RANK, col_blk=(D + Q_LORA_RANK) // KV_LORA_RANK)
    kr = p0[:, D + Q_LORA_RANK + KV_LORA_RANK:D + P0_HEAD]
    qng, kvng = vec(w["mla_q_norm"]), vec(w["mla_kv_norm"])
    (qn,) = rowwise_fwd(f_rms, [cq], [qng], [Q_LORA_RANK], [BF16], T, 0, "l0_qnorm")
    (kvn,) = rowwise_fwd(f_rms, [ckv], [kvng], [KV_LORA_RANK], [BF16], T, 0, "l0_kvnorm")
    tabs = rope_tables(Lc, L)
    spread = rotary_spread()
    w_uq_p = pad_heads(w["mla_w_uq"], QK_DIM)
    w_ukv3 = w["mla_w_ukv"].reshape(KV_LORA_RANK, MLA_HEADS, QK_NOPE_DIM + V_HEAD_DIM)
    w_kn_p = pad_heads(w_ukv3[:, :, :QK_NOPE_DIM].reshape(KV_LORA_RANK, -1), QK_NOPE_DIM)
    w_v = w_ukv3[:, :, QK_NOPE_DIM:].reshape(KV_LORA_RANK, -1)
    qb = q_heads(qn, w_uq_p, tabs)
    kb, vb = kv_heads(kvn, w_kn_p, w_v, kr, spread, tabs)
    o, lse, lse_rows = attn_fwd(qb, kb, vb, Lc)
    X1, og, out0 = mla_post_fwd(o, p0, X0, mod_of(0, 2), w["mla_w_out"], Lc)

    if late is not None:
        w = {**w, **late(X1)}
    X1p = to_segments(X1, Lc)
    tgt_p = to_segments(target, 0)
    g1 = vec(w["norm_g"][1])
    (H1,) = rowwise_fwd(f_norm_mod, [X1p], [g1, mod_of(1, 1), mod_of(1, 0)], [D], [BF16], T, Lc, "l1_norm")
    p1 = mm_nn(H1, w["s5_w_in"], name="l1_in")
    disc_fn = lambda *a: tuple(zip(*[s5_discretise(a[0][k], a[1][k], a[2][k], a[3][k], a[4][k]) for k in range(2)]))
    disc, disc_vjp = jax.vjp(disc_fn, w["s5_a_re"], w["s5_a_im"], w["s5_log_step"], w["s5_b_re"], w["s5_b_im"])
    dirs = [s5_block_weights(disc[0][k], disc[1][k], disc[2][k], disc[3][k], w["s5_c_re"][k], w["s5_c_im"][k])
            for k in range(2)]
    y_ssm, s5_saved = s5_forward(p1, Lc, dirs)

    row = lambda v: v.reshape(1, D).astype(F32)
    (lvec, dX2, d_yssm, d_u_act, d_z1, d_fg, d_gt1, d_bg, d_d, gw_glu, gw_out) = s5_tail(
        y_ssm, p1, X1p, tgt_p, Lc, row(w["s5_d"]), row(w["s5_b_glu"]), mod[1, 1:2, 2, :], row(w["final_g"]),
        w["s5_w_glu"], w["s5_w_out"])
    loss = jnp.sum(lvec)
    gw = {"final_g": d_fg.reshape(D), "s5_b_glu": d_bg.reshape(D), "s5_d": d_d.reshape(D),
          "s5_w_glu": gw_glu, "s5_w_out": gw_out}
    dmod = {}

    du_p, s5_g = s5_backward(d_yssm, d_u_act, p1, Lc, dirs, s5_saved)
    d_disc = tuple(tuple(s5_g[k][j - 1].reshape(disc[j][k].shape) if j >= 2 else
                         s5_g[k][0][j].reshape(disc[j][k].shape) for k in range(2)) for j in range(4))
    gw["s5_a_re"], gw["s5_a_im"], gw["s5_log_step"], gw["s5_b_re"], gw["s5_b_im"] = disc_vjp(d_disc)
    gw["s5_c_re"] = jnp.stack([s5_g[0][3], s5_g[1][3]])
    gw["s5_c_im"] = jnp.stack([s5_g[0][4], s5_g[1][4]])
    d_H1 = mm_nt_sum([(du_p, 0, 0), (d_z1, D, Lc)], w["s5_w_in"], T, "l1_in_dx")
    gw["s5_w_in"] = jnp.concatenate([mm_tn(H1, du_p, name="l1_in_dw_u"), mm_tn(H1[Lc:], d_z1, name="l1_in_dw_z")],
                                    axis=1)
    if reducer is not None:
        g1 = g1 + reducer[0]({n: gw.pop(n) for n in LAYER1_MATS})[0, 0]
    d_X1p, d_g1, d_sc1, d_sh1 = rowwise_bwd(f_norm_mod, [X1p], [g1, mod_of(1, 1), mod_of(1, 0)], [d_H1],
                                            [0], [0, 1, 2], T, Lc, "l1_norm_bwd", lat_add=dX2)
    d_gt1_full = jnp.concatenate([jnp.zeros((1, 1, D), F32), d_gt1[None]], axis=0)
    dmod[1] = (d_sh1, d_sc1, d_gt1_full)
    d_X1 = from_segments(d_X1p, Lc)

    d_o, d_z0, d_gt0, gw["mla_w_out"] = mla_post_bwd(d_X1, out0, og, o, p0, mod_of(0, 2), w["mla_w_out"], Lc)
    if reducer is not None:
        tabs = (tabs[0] + reducer[1](d_o)[0, 0],) + tabs[1:]
    d_q, delta_rows = attn_bwd_dq(qb, kb, vb, o, d_o, lse, tabs, Lc)
    dk_p, d_v = attn_bwd_dkv(qb, kb, vb, d_o, lse_rows, delta_rows, Lc)
    d_k, d_kr = heads_unrope(dk_p, tabs, jnp.pad(spread, ((0, HEAD_LANES - QK_ROPE_DIM), (0, 0))),
                             name="l0_k_unrope")
    d_qn = mm_nt(d_q, w_uq_p, name="l0_uq_dx")
    gw["mla_w_uq"] = unpad_heads(mm_tn(qn, d_q, name="l0_uq_dw"), QK_DIM)
    d_kvn = mm_nt(d_k, w_kn_p, name="l0_ukn_dx") + mm_nt(d_v, w_v, name="l0_uv_dx")
    dw_kn = unpad_heads(mm_tn(kvn, d_k, name="l0_ukn_dw"), QK_NOPE_DIM).reshape(KV_LORA_RANK, MLA_HEADS, QK_NOPE_DIM)
    dw_v = mm_tn(kvn, d_v, name="l0_uv_dw").reshape(KV_LORA_RANK, MLA_HEADS, V_HEAD_DIM)
    gw["mla_w_ukv"] = jnp.concatenate([dw_kn, dw_v], axis=-1).reshape(KV_LORA_RANK, -1)
    d_cq, d_qng = rowwise_bwd(f_rms, [cq], [qng], [d_qn], [0], [0], T, 0, "l0_qnorm_bwd")
    d_ckv, d_kvng = rowwise_bwd(f_rms, [ckv], [kvng], [d_kvn], [0], [0], T, 0, "l0_kvnorm_bwd")
    gw["mla_q_norm"] = d_qng.reshape(-1)
    gw["mla_kv_norm"] = d_kvng.reshape(-1)
    o_cq, o_ckv = D, D + Q_LORA_RANK
    o_kr = o_ckv + KV_LORA_RANK
    d_H0 = mm_nt_sum([(d_z0, 0, 0), (d_cq, o_cq, 0), (d_ckv, o_ckv, 0), (d_kr, o_kr, 0)], w["mla_w_in"], T, "l0_in_dx")
    d_head = jnp.concatenate([d_cq, d_ckv, d_kr], axis=1)
    gw["mla_w_in"] = jnp.concatenate([mm_tn(H0, d_head, name="l0_in_dw_head")[:, :P0_HEAD],
                                      mm_tn(H0, d_z0, name="l0_in_dw_z")], axis=1)
    d_X0, d_g0, d_sc0, d_sh0 = rowwise_bwd(f_norm_mod, [X0], [g0, mod_of(0, 1), mod_of(0, 0)], [d_H0],
                                           [0], [0, 1, 2], T, Lc, "l0_norm_bwd", lat_add=d_X1)
    dmod[0] = (d_sh0, d_sc0, d_gt0)
    gw["norm_g"] = jnp.stack([d_g0.reshape(D), d_g1.reshape(D)])
    dx = d_X0[Lc:]
    dmod_arr = jnp.stack([jnp.stack([dmod[i][j][:, 0, :] for j in range(3)], axis=1) for i in range(2)])
    ready = reducer[2](d_X0) if reducer is not None else {}
    return loss, dx, dmod_arr, gw, ready


SHARDED = {
    "mla_w_in": 1, "mla_w_uq": 1, "mla_w_ukv": 1, "mla_w_out": 0,
    "s5_w_in": 1, "s5_w_glu": 0, "s5_w_out": 0, "s5_d": 0, "s5_b_glu": 0,
}
SHARDED_MATS = ["mla_w_in", "mla_w_uq", "mla_w_ukv", "mla_w_out", "s5_w_in", "s5_w_glu", "s5_w_out"]
SHARDED_VECS = ["s5_d", "s5_b_glu"]
REPLICATED = ["norm_g", "mla_q_norm", "mla_kv_norm", "s5_a_re", "s5_a_im", "s5_log_step", "s5_b_re", "s5_b_im",
              "s5_c_re", "s5_c_im", "final_g"]
WEIGHT_ORDER = ["c_ctx", "ada_w", "ada_b", "norm_g", "mla_w_in", "mla_q_norm", "mla_w_uq", "mla_kv_norm", "mla_w_ukv",
                "mla_w_out", "s5_w_in", "s5_a_re", "s5_a_im", "s5_log_step", "s5_b_re", "s5_b_im", "s5_c_re", "s5_c_im",
                "s5_d", "s5_w_glu", "s5_b_glu", "s5_w_out", "final_g"]


P0_HEAD = Q_LORA_RANK + KV_LORA_RANK + QK_ROPE_DIM


P0_WIDTH = 1536


def w_in_to_kernel_order(w):
    pad = jnp.zeros((w.shape[0], P0_WIDTH - w.shape[1]), w.dtype)
    return jnp.concatenate([w[:, P0_HEAD:], w[:, :P0_HEAD], pad], axis=1)


LAYER0_MATS = ["mla_w_in", "mla_w_uq", "mla_w_ukv", "mla_w_out"]
LAYER1_MATS = ["s5_w_in", "s5_w_glu", "s5_w_out"]


def _whole_matrices(names, own_blocks, gathered):
    chip = _chip_index(_coords())
    full = {}
    for n, own, o in zip(names, own_blocks, gathered):
        slot = lax.broadcasted_iota(jnp.int32, (N_CHIP, 1, 1), 0)
        o = jnp.where(slot == chip, own[None], o.reshape((N_CHIP,) + own.shape))
        full[n] = o.reshape(-1, o.shape[-1]) if SHARDED[n] == 0 else o.transpose(1, 0, 2).reshape(o.shape[1], -1)
    return full


def gather_weights(ws):
    mats = [ws[n].astype(BF16) for n in LAYER0_MATS]
    full = _whole_matrices(LAYER0_MATS, mats, gather_halves(mats, "gather_weights"))
    full["mla_w_in"] = w_in_to_kernel_order(full["mla_w_in"])
    return full


def gather_weights_behind(ws, after):
    mats = [ws[n].astype(BF16) for n in LAYER1_MATS]
    flight, token = exchange_start(
        mats, [jax.ShapeDtypeStruct((N_CHIP,) + m.shape, m.dtype) for m in mats],
        [(f, a, lambda me, peer: None, a, lambda s: (_chip_index(s),)) for a in range(len(mats)) for f in CHIP_FLIPS],
        "gather_l1_start", after=after)

    def finish(after):
        own, got = exchange_wait(flight, after, "gather_l1_wait")
        return _whole_matrices(LAYER1_MATS, own, got)

    return token[0, 0], finish


def _grad_slots(gw, names):
    slots = []
    for n in names:
        g = gw[n]
        if SHARDED[n] == 0:
            slots.append(g.reshape(N_CHIP, 2, g.shape[0] // (2 * N_CHIP), g.shape[1]))
        else:
            k, n4 = g.shape
            slots.append(g.reshape(k, N_CHIP, n4 // N_CHIP).transpose(1, 0, 2)
                         .reshape(N_CHIP, 2, k // 2, n4 // N_CHIP))
    return slots


def _to_sibling_half(count):
    return [(CORE_FLIP, i, lambda me, peer: (slice(None), 1 - me[2]), i, lambda s: None) for i in range(count)]


def _to_chips(count):
    return [(f, i, lambda me, peer: (_chip_index(peer),), i, lambda s: (_chip_index(s),))
            for i in range(count) for f in CHIP_FLIPS]


def _place():
    me = _coords()
    return jnp.stack([me[2], _chip_index(me)]).astype(jnp.int32)


def reduce_behind(names):
    state = {}
    count = len(names)

    def begin(gw):
        slots = _grad_slots(gw, names)
        lands = [jax.ShapeDtypeStruct((N_CHIP,) + s.shape[2:], F32) for s in slots]
        state["in"], token = exchange_start(slots, lands, _to_sibling_half(count), "grads_l1_swap_in_start")
        return token

    def middle(after):
        slots, got = exchange_wait(state["in"], after, "grads_l1_swap_in_wait")
        place = _place()
        sums = [pair_add(s, g, place[:1], BF16, f"grads_pair_{n}") for n, s, g in zip(names, slots, got)]
        lands = [jax.ShapeDtypeStruct(s.shape, s.dtype) for s in sums]
        state["out"], token = exchange_start(sums, lands, _to_chips(count), "grads_l1_scatter_start")
        return token

    def end(after):
        sums, parts = exchange_wait(state["out"], after, "grads_l1_scatter_wait")
        place = _place()
        return {n: sum_chips(p, s, place, f"grads_sum_{n}") for n, p, s in zip(names, parts, sums)}

    return begin, middle, end


def reduce_gradients(gw, ready_halves):
    me = _coords()
    place = _place()
    mat_names = [n for n in SHARDED_MATS if n not in ready_halves]
    slots = dict(zip(mat_names, _grad_slots(gw, mat_names)))
    small_names = REPLICATED + SHARDED_VECS
    small, small_offs = pack_flat([gw[n].astype(F32) for n in small_names], F32)
    small = jnp.pad(small, ((0, (-small.shape[0]) % (N_CHIP * 32)), (0, 0)))
    slots["small"] = small.reshape(N_CHIP, 2, -1, 128)
    names = list(slots)
    count = len(names)
    got = exchange([slots[n] for n in names],
                   [jax.ShapeDtypeStruct((N_CHIP,) + slots[n].shape[2:], F32) for n in names],
                   _to_sibling_half(count), [], "grads_swap_in")
    sums = [pair_add(slots[n], g, place[:1], F32 if n == "small" else BF16, f"grads_pair_{n}")
            for n, g in zip(names, got)]
    parts = exchange(sums, [jax.ShapeDtypeStruct(s.shape, s.dtype) for s in sums], _to_chips(count), [],
                     "grads_scatter")
    halves = {n: sum_chips(p, s, place, f"grads_sum_{n}") for n, p, s in zip(names, parts, sums)}
    halves.update(ready_halves)
    all_names = list(halves)
    fulls = exchange(
        [halves[n] for n in all_names], [jax.ShapeDtypeStruct(halves[n].shape, F32) for n in all_names],
        [(CORE_FLIP, i, lambda me, peer: (me[2],), i, lambda s: (s[2],)) for i in range(len(all_names))], [],
        "grads_swap_out", aliases={i: i for i in range(len(all_names))})
    out = {n: f.reshape(-1, f.shape[-1]) for n, f in zip(all_names, fulls)}
    quarter = out.pop("small")
    gather, token = exchange_start(
        [quarter], [jax.ShapeDtypeStruct((N_CHIP,) + quarter.shape, F32)],
        [(f, 0, lambda me, peer: None, 0, lambda s: (_chip_index(s),)) for f in CHIP_FLIPS],
        "grads_gather_small_start")

    def finish_small(after):
        (own,), (got_small,) = exchange_wait(gather, after, "grads_gather_small_wait")
        slot = lax.broadcasted_iota(jnp.int32, (N_CHIP, 1, 1), 0)
        small_all = jnp.where(slot == _chip_index(me), own[None], got_small)
        vals = unpack_flat(small_all, small_offs, [gw[n].shape for n in small_names])
        res = {}
        for n, v in zip(small_names, vals):
            if n in SHARDED_VECS:
                size = v.shape[0] // N_CHIP
                v = lax.dynamic_slice_in_dim(v, _chip_index(me) * size, size)
            res[n] = v
        return res

    return out, finish_small, token


def kernel(x, c, ctx, c_ctx, ada_w, ada_b, norm_g, mla_w_in, mla_q_norm, mla_w_uq, mla_kv_norm, mla_w_ukv, mla_w_out, s5_w_in, s5_a_re, s5_a_im, s5_log_step, s5_b_re, s5_b_im, s5_c_re, s5_c_im, s5_d, s5_w_glu, s5_b_glu, s5_w_out, final_g, loss_target, m_c_ctx, m_ada_w, m_ada_b, m_norm_g, m_mla_w_in, m_mla_q_norm, m_mla_w_uq, m_mla_kv_norm, m_mla_w_ukv, m_mla_w_out, m_s5_w_in, m_s5_a_re, m_s5_a_im, m_s5_log_step, m_s5_b_re, m_s5_b_im, m_s5_c_re, m_s5_c_im, m_s5_d, m_s5_w_glu, m_s5_b_glu, m_s5_w_out, m_final_g, v_c_ctx, v_ada_w, v_ada_b, v_norm_g, v_mla_w_in, v_mla_q_norm, v_mla_w_uq, v_mla_kv_norm, v_mla_w_ukv, v_mla_w_out, v_s5_w_in, v_s5_a_re, v_s5_a_im, v_s5_log_step, v_s5_b_re, v_s5_b_im, v_s5_c_re, v_s5_c_im, v_s5_d, v_s5_w_glu, v_s5_b_glu, v_s5_w_out, v_final_g):
    args = dict(locals())
    weights = {n: args[n] for n in WEIGHT_ORDER}
    D = D_MODEL
    xi, yi, ci = _coords()
    chip = 2 * xi + yi
    me = 4 * xi + 2 * yi + ci
    n_col = ada_w.shape[2]

    c_all = allgather_devices(jnp.pad(c, ((0, 7), (0, 0))), "gather_c")[:, 0, :]
    cond = jnp.concatenate([c_all, jnp.broadcast_to(c_ctx[None], (8, D))], axis=0)
    (s_cond,) = rowwise_fwd(lambda v: (_silu(v),), [cond], [], [D], [F32], 16, 0, "cond_silu")
    ada_rows = ada_w.reshape(2 * D, n_col)
    mod_cols = jnp.stack([mm_nn(s_cond, ada_rows, name=f"mod_proj{i}", b_blk=i) for i in range(2)])
    vec_tiles = [jnp.pad(weights[n][0].reshape(-1, 128), ((0, 6), (0, 0))) for n in SHARDED_VECS]
    mod_all, *vec_all = allgather_chips([mod_cols] + vec_tiles, "gather_mod")
    mod_all = mod_all.transpose(1, 2, 0, 3).reshape(2, 16, 3 * D) + ada_b[:, None, :]
    mod_l = lax.dynamic_index_in_dim(mod_all, me, axis=1, keepdims=False)
    mod_c = mod_all[:, 8, :]
    mod = jnp.stack([mod_c.reshape(2, 3, D), mod_l.reshape(2, 3, D)], axis=1)

    w = gather_weights({n: weights[n][0] for n in LAYER0_MATS})
    token, late = gather_weights_behind({n: weights[n][0] for n in LAYER1_MATS}, w["mla_w_out"])
    for n, v in zip(SHARDED_VECS, vec_all):
        w[n] = v[:, :2, :].reshape(-1)
    for n in ["norm_g", "final_g"]:
        w[n] = weights[n]
    for n in ["mla_q_norm", "mla_kv_norm", "s5_a_re", "s5_a_im", "s5_log_step", "s5_b_re", "s5_b_im",
              "s5_c_re", "s5_c_im"]:
        w[n] = weights[n][0]

    loss_me, dx, dmod, gw, ready = local_step(x[0] + token, ctx[0], loss_target[0], mod, w, late,
                                              reduce_behind(LAYER1_MATS))

    dmod_rows, loss_all = _gather([dmod.reshape(2, 2, 3 * D), jnp.broadcast_to(loss_me, (8, 128))],
                                  ALL_FLIPS, _dev_index, N_DEV, "gather_dmod")
    loss = functools.reduce(lambda s, d: s + loss_all[d, 0, 0], range(1, N_DEV), loss_all[0, 0, 0])
    dm = jnp.concatenate([dmod_rows[:, :, 1, :], dmod_rows[:, :, 0, :]], axis=0).transpose(1, 0, 2)
    g_ada_b = jnp.sum(dm, axis=1)
    dm_cols = lax.dynamic_slice_in_dim(dm, chip * n_col, n_col, axis=2)
    g_ada_w = jnp.stack([mm_tn(s_cond, dm_cols[i], name=f"mod_proj_dw{i}") for i in range(2)])
    dmc = jnp.sum(dm_cols[:, 8:, :], axis=1)
    dmc8 = jnp.broadcast_to(dmc[:, None, :], (2, 8, n_col))
    g_sc = (mm_nt(dmc8[0], ada_rows, name="mod_proj_dx0", b_rows=D, b_blk=0)[0]
            + mm_nt(dmc8[1], ada_rows, name="mod_proj_dx1", b_rows=D, b_blk=1)[0])
    g_sc_all = allgather_devices(jnp.broadcast_to(g_sc[None], (8, D)), "gather_dcond")[:, 0, :]
    g_silu_cc = g_sc_all[0] + g_sc_all[2] + g_sc_all[4] + g_sc_all[6]
    (g_c_ctx,) = rowwise_bwd(lambda v: (_silu(v),), [jnp.broadcast_to(c_ctx[None], (8, D))], [],
                             [jnp.broadcast_to(g_silu_cc[None], (8, D))], [0], [], 8, 0, "cond_silu_bwd")
    g_c_ctx = g_c_ctx[0]

    grads = {"c_ctx": g_c_ctx, "ada_w": g_ada_w, "ada_b": g_ada_b}
    deltas, new_m, new_v = {}, {}, {}
    small = [n for n in WEIGHT_ORDER if weights[n].size < 50000]

    def update(n, after=None):
        shp = weights[n].shape
        w2 = weights[n].reshape(-1, shp[-1])
        d_, m_, v_ = adamw(w2, grads[n].reshape(w2.shape), args["m_" + n].reshape(w2.shape),
                           args["v_" + n].reshape(w2.shape), name=f"adamw_{n}", after=after)
        deltas[n], new_m[n], new_v[n] = d_.reshape(shp), m_.reshape(shp), v_.reshape(shp)

    red, finish_small, small_started = reduce_gradients(gw, ready)
    update("ada_w", small_started)
    for n in SHARDED_MATS:
        grads[n] = red[n].reshape(weights[n].shape)
        update(n, small_started)
    red_small = finish_small(jnp.stack([deltas[n].reshape(-1)[0] for n in ["ada_w"] + SHARDED_MATS]))
    for n in REPLICATED + SHARDED_VECS:
        grads[n] = red_small[n].reshape(weights[n].shape)
    for n in WEIGHT_ORDER:
        if n not in small and n not in deltas:
            update(n)
    packs = []
    offs = None
    for src in (weights, grads, {n: args["m_" + n] for n in small}, {n: args["v_" + n] for n in small}):
        buf, offs = pack_flat([src[n] for n in small], F32)
        packs.append(buf)
    outs = adamw(*packs, name="adamw_small")
    for res, dst in zip(outs, (deltas, new_m, new_v)):
        for n, val in zip(small, unpack_flat(res, offs, [weights[n].shape for n in small])):
            dst[n] = val

    return (loss, dx[None], *[grads[n] for n in WEIGHT_ORDER], *[deltas[n] for n in WEIGHT_ORDER],
            *[new_m[n] for n in WEIGHT_ORDER], *[new_v[n] for n in WEIGHT_ORDER])
```

```python
import functools
import math

import jax
import jax.numpy as jnp
import numpy as np
from jax import lax
from jax.experimental import pallas as pl
from jax.experimental.pallas import tpu as pltpu

F32 = jnp.float32
BF16 = jnp.bfloat16

D_MODEL = 1024
GRID_W = 64
EPS = 1e-6
MLA_HEADS = 16
QK_NOPE_DIM = 64
QK_ROPE_DIM = 32
V_HEAD_DIM = 64
Q_LORA_RANK = 256
KV_LORA_RANK = 128
QK_DIM = QK_NOPE_DIM + QK_ROPE_DIM
SOFTMAX_SCALE = QK_DIM ** -0.5
ROPE_THETA = 10000.0
S5_GROUP = 16
S5_GROUPS = D_MODEL // S5_GROUP
S5_STATE = 64
S5_LANES = S5_GROUPS * S5_STATE
N_SEG = 8
GROUPS_PER_BLOCK = 8
N_BLOCKS = S5_GROUPS // GROUPS_PER_BLOCK
BLK_CH = GROUPS_PER_BLOCK * S5_GROUP
BLK_ST = GROUPS_PER_BLOCK * S5_STATE

ADAM_LR = 0.001
ADAM_B1 = 0.9
ADAM_B2 = 0.999
ADAM_EPS = 1e-08
ADAM_WD = 0.01
ADAM_STEP = 10

N_DEV = 8
N_CHIP = 4
MESH = pl.DeviceIdType.MESH
VMEM_LIMIT = 52 * 1024 * 1024
ROW_TILE = 256


def _params(sem=None, vmem=None):
    return pltpu.CompilerParams(dimension_semantics=sem, vmem_limit_bytes=vmem)


def mm_nn(a, b, out_dtype=F32, name="mm_nn", b_blk=0):
    M, K = a.shape
    N = b.shape[1]
    tm = math.gcd(ROW_TILE, M)

    def body(a_ref, b_ref, o_ref):
        o_ref[...] = jnp.dot(a_ref[...].astype(BF16), b_ref[...].astype(BF16),
                             preferred_element_type=F32).astype(o_ref.dtype)

    return pl.pallas_call(
        body, out_shape=jax.ShapeDtypeStruct((M, N), out_dtype), grid=(M // tm,),
        in_specs=[pl.BlockSpec((tm, K), lambda i: (i, 0)), pl.BlockSpec((K, N), lambda i: (b_blk, 0))],
        out_specs=pl.BlockSpec((tm, N), lambda i: (i, 0)),
        compiler_params=_params(("parallel",), VMEM_LIMIT), name=name)(a, b)


def mm_nt(a, b, out_dtype=F32, name="mm_nt", b_rows=None, b_blk=0):
    M, N = a.shape
    K = b.shape[0] if b_rows is None else b_rows
    tm = math.gcd(ROW_TILE, M)

    def body(a_ref, b_ref, o_ref):
        o_ref[...] = lax.dot_general(a_ref[...].astype(BF16), b_ref[...].astype(BF16),
                                     (((1,), (1,)), ((), ())),
                                     preferred_element_type=F32).astype(o_ref.dtype)

    return pl.pallas_call(
        body, out_shape=jax.ShapeDtypeStruct((M, K), out_dtype), grid=(M // tm,),
        in_specs=[pl.BlockSpec((tm, N), lambda i: (i, 0)), pl.BlockSpec((K, N), lambda i: (b_blk, 0))],
        out_specs=pl.BlockSpec((tm, K), lambda i: (i, 0)),
        compiler_params=_params(("parallel",), VMEM_LIMIT), name=name)(a, b)


def mm_nt_sum(terms, w, n_rows, name):
    K = w.shape[0]
    tm = math.gcd(ROW_TILE, n_rows, *[t[2] for t in terms])

    def body(*refs):
        i = pl.program_id(0)
        w_ref, o_ref = refs[len(terms)], refs[len(terms) + 1]
        acc = None
        for a_ref, (a, off, first) in zip(refs, terms):
            part = lax.dot_general(a_ref[...].astype(BF16), w_ref[:, off:off + a.shape[1]].astype(BF16), NT_DIMS,
                                   preferred_element_type=F32)
            if first:
                part = jnp.where(i >= first // tm, part, 0.0)
            acc = part if acc is None else acc + part
        o_ref[...] = acc

    def a_spec(a, first):
        skip = first // tm
        return pl.BlockSpec((tm, a.shape[1]), lambda i: (jnp.maximum(i - skip, 0), 0))

    return pl.pallas_call(
        body, out_shape=jax.ShapeDtypeStruct((n_rows, K), F32), grid=(n_rows // tm,),
        in_specs=[a_spec(a, first) for a, _, first in terms] + [pl.BlockSpec(w.shape, lambda i: (0, 0))],
        out_specs=pl.BlockSpec((tm, K), lambda i: (i, 0)),
        compiler_params=_params(("parallel",), VMEM_LIMIT), name=name)(*[t[0] for t in terms], w)


def mm_tn(a, b, name="mm_tn"):
    M, K = a.shape
    N = b.shape[1]
    tn = math.gcd(512, N) if N % 128 == 0 and N > 512 else N

    def body(a_ref, b_ref, o_ref):
        o_ref[...] = lax.dot_general(a_ref[...].astype(BF16), b_ref[...].astype(BF16),
                                     (((0,), (0,)), ((), ())), preferred_element_type=F32)

    return pl.pallas_call(
        body, out_shape=jax.ShapeDtypeStruct((K, N), F32), grid=(N // tn,),
        in_specs=[pl.BlockSpec((M, K), lambda j: (0, 0)), pl.BlockSpec((M, tn), lambda j: (0, j))],
        out_specs=pl.BlockSpec((K, tn), lambda j: (0, j)),
        compiler_params=_params(("parallel",), VMEM_LIMIT), name=name)(a, b)


class Rows:
    def __init__(self, arr, width=None, row_off=0, col_blk=0):
        self.arr = arr
        self.width = arr.shape[1] if width is None else width
        self.row_off = row_off
        self.col_blk = col_blk

    def spec(self, tm):
        ro, cb = self.row_off // tm, self.col_blk
        return pl.BlockSpec((tm, self.width), lambda i: (i + ro, cb))


def _as_rows(x):
    return x if isinstance(x, Rows) else Rows(x)


def _row_tile(n_rows, n_ctx_rows, rows):
    tm = math.gcd(ROW_TILE, n_rows, n_ctx_rows)
    for r in rows:
        tm = math.gcd(tm, r.row_off)
    return tm


def _bc_spec(arr, n_ctx_blocks):
    g, _, d = arr.shape
    if g == 1:
        return pl.BlockSpec((1, 1, d), lambda i: (0, 0, 0))
    return pl.BlockSpec((1, 1, d), lambda i: ((i >= n_ctx_blocks).astype(jnp.int32), 0, 0))


def rowwise_fwd(fn, rows, bcs, out_dims, out_dtypes, n_rows, n_ctx_rows, name):
    rows = [_as_rows(r) for r in rows]
    tm = _row_tile(n_rows, n_ctx_rows, rows)
    ncb = n_ctx_rows // tm
    nr, nb = len(rows), len(bcs)

    def body(*refs):
        vals = [r[...].astype(F32) for r in refs[:nr]] + [b[0].astype(F32) for b in refs[nr:nr + nb]]
        outs = fn(*vals)
        for o_ref, v in zip(refs[nr + nb:], outs):
            o_ref[...] = v.astype(o_ref.dtype)

    outs = pl.pallas_call(
        body,
        out_shape=[jax.ShapeDtypeStruct((n_rows, d), dt) for d, dt in zip(out_dims, out_dtypes)],
        grid=(n_rows // tm,),
        in_specs=[r.spec(tm) for r in rows] + [_bc_spec(b, ncb) for b in bcs],
        out_specs=[pl.BlockSpec((tm, d), lambda i: (i, 0)) for d in out_dims],
        compiler_params=_params(("parallel",), VMEM_LIMIT), name=name)(*[r.arr for r in rows], *bcs)
    return outs


def rowwise_bwd(fn, rows, bcs, cts, diff_rows, diff_bcs, n_rows, n_ctx_rows, name, ct_extra=None, lat_add=None):
    rows = [_as_rows(r) for r in rows]
    cts = [_as_rows(c) for c in cts]
    extra = [_as_rows(ct_extra)] if ct_extra is not None else []
    tm = _row_tile(n_rows, n_ctx_rows, rows + cts + extra)
    ncb = n_ctx_rows // tm
    nr, nb, nc = len(rows), len(bcs), len(cts)
    ndr, ndb = len(diff_rows), len(diff_bcs)
    n_in = nr + nb + nc + len(extra) + (lat_add is not None)

    def body(*refs):
        i = pl.program_id(0)
        rvals = [r[...].astype(F32) for r in refs[:nr]]
        bvals = [b[0].astype(F32) for b in refs[nr:nr + nb]]
        cvals = [c[...].astype(F32) for c in refs[nr + nb:nr + nb + nc]]
        if extra:
            cvals[0] = cvals[0] + refs[nr + nb + nc][...].astype(F32)
        outs = refs[n_in:]

        def f(*d):
            rv, bv = list(rvals), list(bvals)
            for k, idx in enumerate(diff_rows):
                rv[idx] = d[k]
            for k, idx in enumerate(diff_bcs):
                bv[idx] = d[ndr + k]
            return tuple(fn(*rv, *bv))

        primals = [rvals[k] for k in diff_rows] + [bvals[k] for k in diff_bcs]
        _, vjp = jax.vjp(f, *primals)
        grads = list(vjp(tuple(cvals)))
        if lat_add is not None:
            add = refs[n_in - 1][...]
            grads[0] = grads[0] + (add if lat_add.shape[0] == n_rows else jnp.where(i >= ncb, add, 0.0))
        for k in range(ndr):
            outs[k][...] = grads[k].astype(outs[k].dtype)
        for k, idx in enumerate(diff_bcs):
            o_ref = outs[ndr + k]
            first = (i == 0)
            if bcs[idx].shape[0] == 2:
                first = first | (i == ncb)

            @pl.when(first)
            def _(o_ref=o_ref):
                o_ref[...] = jnp.zeros_like(o_ref)

            o_ref[0] += grads[ndr + k]

    out_shape = [jax.ShapeDtypeStruct((n_rows, rows[k].width), F32) for k in diff_rows]
    out_shape += [jax.ShapeDtypeStruct(bcs[k].shape, F32) for k in diff_bcs]
    out_specs = [pl.BlockSpec((tm, rows[k].width), lambda i: (i, 0)) for k in diff_rows]
    out_specs += [_bc_spec(bcs[k], ncb) for k in diff_bcs]
    ins = [r.arr for r in rows] + list(bcs) + [c.arr for c in cts + extra]
    in_specs = [r.spec(tm) for r in rows] + [_bc_spec(b, ncb) for b in bcs] + [c.spec(tm) for c in cts + extra]
    if lat_add is not None:
        ins.append(lat_add)
        skip = ncb if lat_add.shape[0] != n_rows else 0
        in_specs.append(pl.BlockSpec((tm, lat_add.shape[1]), lambda i: (jnp.maximum(i - skip, 0), 0)))
    outs = pl.pallas_call(
        body, out_shape=out_shape, grid=(n_rows // tm,), in_specs=in_specs, out_specs=out_specs,
        compiler_params=_params(("arbitrary",), VMEM_LIMIT), name=name)(*ins)
    return outs


def _rms(x):
    return x * lax.rsqrt(jnp.mean(x * x, axis=-1, keepdims=True) + EPS)


def _sigmoid(x):
    return 0.5 * (jnp.tanh(0.5 * x) + 1.0)


def _silu(x):
    return x * _sigmoid(x)


def _gelu_tanh(x):
    return 0.5 * x * (1.0 + jnp.tanh(math.sqrt(2.0 / math.pi) * (x + 0.044715 * (x * x * x))))


def f_norm_mod(x, g, sc, sh):
    return ((_rms(x) * g) * (1.0 + sc) + sh,)


def f_rms(x, g):
    return (_rms(x) * g,)


def f_gate(o, z):
    return (o * _silu(z),)


def f_s5_act(y, u, d):
    return (_gelu_tanh(y + d * u),)


def f_s5_glu(ya, gl, z, b):
    return (ya * _sigmoid(gl + b) * _silu(z),)


def mla_post_fwd(o, p0, x0, gate, w_out, n_ctx, name="l0_post"):
    n, d = o.shape
    tm = math.gcd(ROW_TILE, n, n_ctx)
    ncb = n_ctx // tm

    def body(o_ref, z_ref, x_ref, gt_ref, w_ref, x1_ref, og_ref, out_ref):
        og = f_gate(o_ref[...], z_ref[...])[0].astype(BF16)
        out = jnp.dot(og, w_ref[...], preferred_element_type=F32)
        og_ref[...] = og
        out_ref[...] = out
        x1_ref[...] = x_ref[...] + gt_ref[0] * out

    row = pl.BlockSpec((tm, d), lambda i: (i, 0))
    return pl.pallas_call(
        body, out_shape=[jax.ShapeDtypeStruct((n, d), F32), jax.ShapeDtypeStruct((n, d), BF16),
                         jax.ShapeDtypeStruct((n, d), F32)],
        grid=(n // tm,),
        in_specs=[row, row, row, _bc_spec(gate, ncb), pl.BlockSpec((d, d), lambda i: (0, 0))],
        out_specs=[row, row, row],
        compiler_params=_params(("parallel",), VMEM_LIMIT), name=name)(o, p0, x0, gate, w_out)


def mla_post_bwd(dx1, out, og, o, p0, gate, w_out, n_ctx, name="l0_post_bwd"):
    n, d = o.shape
    tm = math.gcd(ROW_TILE, n, n_ctx)
    ncb = n_ctx // tm

    def body(dx_ref, out_ref, og_ref, o_ref, z_ref, gt_ref, w_ref, do_ref, dz_ref, dgt_ref, dw_ref):
        i = pl.program_id(0)

        @pl.when(i == 0)
        def _():
            dw_ref[...] = jnp.zeros_like(dw_ref)

        @pl.when((i == 0) | (i == ncb))
        def _():
            dgt_ref[...] = jnp.zeros_like(dgt_ref)

        dx = dx_ref[...]
        dgt_ref[0] += jnp.sum(dx * out_ref[...], axis=0, keepdims=True)
        d_out16 = (gt_ref[0] * dx).astype(BF16)
        dw_ref[...] += lax.dot_general(og_ref[...], d_out16, (((0,), (0,)), ((), ())), preferred_element_type=F32)
        d_og = lax.dot_general(d_out16, w_ref[...], NT_DIMS, preferred_element_type=F32)
        _, gate_vjp = jax.vjp(lambda o_, z_: f_gate(o_, z_), o_ref[...], z_ref[...])
        d_o, d_z = gate_vjp((d_og,))
        do_ref[...] = d_o
        dz_ref[...] = d_z

    row = pl.BlockSpec((tm, d), lambda i: (i, 0))
    mat = pl.BlockSpec((d, d), lambda i: (0, 0))
    return pl.pallas_call(
        body, out_shape=[jax.ShapeDtypeStruct((n, d), F32), jax.ShapeDtypeStruct((n, d), F32),
                         jax.ShapeDtypeStruct(gate.shape, F32), jax.ShapeDtypeStruct((d, d), F32)],
        grid=(n // tm,),
        in_specs=[row, row, row, row, row, _bc_spec(gate, ncb), mat],
        out_specs=[row, row, _bc_spec(gate, ncb), mat],
        compiler_params=_params(("arbitrary",), VMEM_LIMIT), name=name)(dx1, out, og, o, p0, gate, w_out)


def s5_tail(y_ssm, p1, x1p, target, n_ctx, d_vec, b_glu, gate, final_g, w_glu, w_out, name="l1_tail"):
    n, d = y_ssm.shape
    tm = math.gcd(ROW_TILE, n, n_ctx)
    off = n_ctx // tm
    tn_dims = (((0,), (0,)), ((), ()))

    def row_loss(x, g, t):
        e = _rms(x) * g - t
        return 0.5 * (e * e) * (1.0 / d)

    def body(y_ref, u_ref, z_ref, x1_ref, t_ref, d_ref, b_ref, gt_ref, fg_ref, wg_ref, wo_ref,
             l_ref, dx_ref, dy_ref, du_ref, dz_ref, dfg_ref, dgt_ref, db_ref, dd_ref, dwg_ref, dwo_ref):
        @pl.when(pl.program_id(0) == 0)
        def _():
            for r in (l_ref, dfg_ref, dgt_ref, db_ref, dd_ref, dwg_ref, dwo_ref):
                r[...] = jnp.zeros_like(r)

        u, z, tgt, gt = u_ref[...], z_ref[...], t_ref[...], gt_ref[...]
        (ya,), act_vjp = jax.vjp(lambda y_, u_, d_: f_s5_act(y_, u_, d_), y_ref[...], u, d_ref[...])
        ya16 = ya.astype(BF16)
        gl = jnp.dot(ya16, wg_ref[...], preferred_element_type=F32)
        (y3,), glu_vjp = jax.vjp(lambda a_, g_, z_, b_: f_s5_glu(a_, g_, z_, b_), ya, gl, z, b_ref[...])
        y3_16 = y3.astype(BF16)
        out1 = jnp.dot(y3_16, wo_ref[...], preferred_element_type=F32)
        lterm, loss_vjp = jax.vjp(lambda x_, g_: row_loss(x_, g_, tgt), x1_ref[...] + gt * out1, fg_ref[...])
        dx2, dfg = loss_vjp(jnp.ones_like(lterm))
        l_ref[...] += jnp.sum(lterm, axis=0, keepdims=True)
        dfg_ref[...] += dfg
        dx_ref[...] = dx2
        dgt_ref[...] += jnp.sum(dx2 * out1, axis=0, keepdims=True)
        d_out16 = (gt * dx2).astype(BF16)
        dwo_ref[...] += lax.dot_general(y3_16, d_out16, tn_dims, preferred_element_type=F32)
        d_y3 = lax.dot_general(d_out16, wo_ref[...], NT_DIMS, preferred_element_type=F32)
        d_ya, d_gl, d_z, d_b = glu_vjp((d_y3,))
        dz_ref[...] = d_z
        db_ref[...] += d_b
        d_gl16 = d_gl.astype(BF16)
        dwg_ref[...] += lax.dot_general(ya16, d_gl16, tn_dims, preferred_element_type=F32)
        d_ya = d_ya + lax.dot_general(d_gl16, wg_ref[...], NT_DIMS, preferred_element_type=F32)
        d_y, d_u, d_d = act_vjp((d_ya,))
        dy_ref[...] = d_y
        du_ref[...] = d_u
        dd_ref[...] += d_d

    row = pl.BlockSpec((tm, d), lambda i: (i, 0))
    vecs = pl.BlockSpec((1, d), lambda i: (0, 0))
    mat = pl.BlockSpec((d, d), lambda i: (0, 0))
    return pl.pallas_call(
        body,
        out_shape=[jax.ShapeDtypeStruct((1, d), F32)] + [jax.ShapeDtypeStruct((n, d), F32)] * 4
        + [jax.ShapeDtypeStruct((1, d), F32)] * 4 + [jax.ShapeDtypeStruct((d, d), F32)] * 2,
        grid=(n // tm,),
        in_specs=[row, pl.BlockSpec((tm, d), lambda i: (i + off, 0)), pl.BlockSpec((tm, d), lambda i: (i + off, 1)),
                  pl.BlockSpec((tm, d), lambda i: (i + off, 0)), row, vecs, vecs, vecs, vecs, mat, mat],
        out_specs=[vecs, row, row, row, row, vecs, vecs, vecs, vecs, mat, mat],
        compiler_params=_params(("arbitrary",), VMEM_LIMIT), name=name)(
            y_ssm, p1, p1, x1p, target, d_vec, b_glu, gate, final_g, w_glu, w_out)


NT_DIMS = (((1,), (1,)), ((), ()))
HEAD_LANES = 128
N_PAIRS = MLA_HEADS // 2


def _own_lanes(shape, hh):
    lane = lax.broadcasted_iota(jnp.int32, shape, len(shape) - 1)
    return (lane < V_HEAD_DIM) if hh == 0 else (lane >= V_HEAD_DIM)


def _delta_lane(hh):
    return V_HEAD_DIM if hh == 0 else 0


def _rope_tiles(x, cos, sin_next, sin_prev, inverse):
    width = x.shape[-1]
    reps = width // HEAD_LANES
    c, sn, sp = (jnp.tile(t, (1, reps)) for t in (cos, sin_next, sin_prev))
    if inverse:
        return x * c + pltpu.roll(x * sn, 8, 1) + pltpu.roll(x * sp, width - 8, 1)
    return x * c + pltpu.roll(x, width - 8, 1) * sn + pltpu.roll(x, 8, 1) * sp


STAT_LANE = QK_DIM


def _with_stat(x16, col, lane0):
    hi = col.astype(BF16)
    r1 = col - hi.astype(F32)
    mid = r1.astype(BF16)
    lo = (r1 - mid.astype(F32)).astype(BF16)
    lane = lax.broadcasted_iota(jnp.int32, x16.shape, 1)
    return jnp.where(lane == lane0, hi, jnp.where(lane == lane0 + 1, mid, jnp.where(lane == lane0 + 2, lo, x16)))


def attn_fwd(qb, kb, vb, n_ctx):
    T = qb.shape[0]
    tq = math.gcd(ROW_TILE, n_ctx)
    nq, ncb = T // tq, n_ctx // tq

    def body(q_ref, k_ref, v_ref, o_ref, lse_ref, qs_ref):
        qi = pl.program_id(1)

        def rows(n_keys):
            v = v_ref[:n_keys, :]
            outs = []
            for hh in range(2):
                hs = slice(hh * HEAD_LANES, (hh + 1) * HEAD_LANES)
                s = lax.dot_general(q_ref[:, hs], k_ref[:n_keys, hs], NT_DIMS,
                                    preferred_element_type=F32) * SOFTMAX_SCALE
                m = jnp.max(s, axis=-1, keepdims=True)
                p = jnp.exp(s - m)
                l = jnp.sum(p, axis=-1, keepdims=True)
                outs.append(jnp.dot(p.astype(BF16), v, preferred_element_type=F32) / l)
                lse = m + jnp.log(l)
                lse_ref[hh] = lse
                qs_ref[:, hs] = _with_stat(q_ref[:, hs], lse * (-1.0 / SOFTMAX_SCALE), STAT_LANE)
            o_ref[...] = jnp.where(_own_lanes(outs[0].shape, 0), outs[0], outs[1])

        pl.when(qi < ncb)(lambda: rows(n_ctx))
        pl.when(qi >= ncb)(lambda: rows(T))

    return pl.pallas_call(
        body,
        out_shape=[jax.ShapeDtypeStruct((T, MLA_HEADS * V_HEAD_DIM), F32),
                   jax.ShapeDtypeStruct((MLA_HEADS, T, 1), F32), jax.ShapeDtypeStruct(qb.shape, BF16)],
        grid=(N_PAIRS, nq),
        in_specs=[pl.BlockSpec((tq, 2 * HEAD_LANES), lambda h, i: (i, h)),
                  pl.BlockSpec((T, 2 * HEAD_LANES), lambda h, i: (0, h)),
                  pl.BlockSpec((T, 2 * V_HEAD_DIM), lambda h, i: (0, h))],
        out_specs=[pl.BlockSpec((tq, 2 * V_HEAD_DIM), lambda h, i: (i, h)),
                   pl.BlockSpec((2, tq, 1), lambda h, i: (h, i, 0)),
                   pl.BlockSpec((tq, 2 * HEAD_LANES), lambda h, i: (i, h))],
        compiler_params=_params(("parallel", "parallel"), VMEM_LIMIT), name="attn_fwd")(qb, kb, vb)


def attn_bwd_dq(qb, kb, vb, o, do, lse, tabs, n_ctx):
    T = qb.shape[0]
    tq = math.gcd(ROW_TILE, n_ctx)
    nq, ncb = T // tq, n_ctx // tq

    def body(q_ref, k_ref, v_ref, o_ref, do_ref, lse_ref, c_ref, sn_ref, sp_ref, dq_ref, dos_ref):
        qi = pl.program_id(1)

        def rows(n_keys):
            v = v_ref[:n_keys, :]
            dqs = []
            for hh in range(2):
                hs = slice(hh * HEAD_LANES, (hh + 1) * HEAD_LANES)
                k = k_ref[:n_keys, hs]
                do = jnp.where(_own_lanes(do_ref.shape, hh), do_ref[...], 0.0)
                delta = jnp.sum(do * o_ref[...], axis=-1, keepdims=True)
                s = lax.dot_general(q_ref[:, hs], k, NT_DIMS, preferred_element_type=F32) * SOFTMAX_SCALE
                p = jnp.exp(s - lse_ref[hh])
                do16 = do.astype(BF16)
                dp = lax.dot_general(do16, v, NT_DIMS, preferred_element_type=F32)
                ds = p * (dp - delta) * SOFTMAX_SCALE
                dqs.append(jnp.dot(ds.astype(BF16), k, preferred_element_type=F32))
                dos_ref[:, hs] = _with_stat(do16, delta, _delta_lane(hh))
            dq = jnp.concatenate(dqs, axis=1)
            dq_ref[...] = _rope_tiles(dq, c_ref[...], sn_ref[...], sp_ref[...], True).astype(BF16)

        pl.when(qi < ncb)(lambda: rows(n_ctx))
        pl.when(qi >= ncb)(lambda: rows(T))

    tab = pl.BlockSpec((tq, HEAD_LANES), lambda h, i: (i, 0))
    return pl.pallas_call(
        body,
        out_shape=[jax.ShapeDtypeStruct((T, MLA_HEADS * HEAD_LANES), BF16)] * 2,
        grid=(N_PAIRS, nq),
        in_specs=[pl.BlockSpec((tq, 2 * HEAD_LANES), lambda h, i: (i, h)),
                  pl.BlockSpec((T, 2 * HEAD_LANES), lambda h, i: (0, h)),
                  pl.BlockSpec((T, 2 * V_HEAD_DIM), lambda h, i: (0, h)),
                  pl.BlockSpec((tq, 2 * V_HEAD_DIM), lambda h, i: (i, h)),
                  pl.BlockSpec((tq, 2 * V_HEAD_DIM), lambda h, i: (i, h)),
                  pl.BlockSpec((2, tq, 1), lambda h, i: (h, i, 0)), tab, tab, tab],
        out_specs=[pl.BlockSpec((tq, 2 * HEAD_LANES), lambda h, i: (i, h))] * 2,
        compiler_params=_params(("parallel", "parallel"), VMEM_LIMIT), name="attn_bwd_dq")(
            qb, kb, vb, o, do, lse, *tabs)


def attn_bwd_dkv(qs, kb, vb, dos, n_ctx):
    T = qs.shape[0]
    tq = math.gcd(ROW_TILE, n_ctx)
    nq, ncb = T // tq, n_ctx // tq

    def body(q_ref, do_ref, k_ref, v_ref, dk_ref, dv_ref):
        kj = pl.program_id(1)

        def cols(first):
            v = v_ref[...]
            lane = lax.broadcasted_iota(jnp.int32, v.shape, 1)
            dvs = []
            for hh in range(2):
                hs = slice(hh * HEAD_LANES, (hh + 1) * HEAD_LANES)
                q = q_ref[first:, hs]
                do16 = do_ref[first:, hs]
                in_delta = (lane >= _delta_lane(hh)) & (lane < _delta_lane(hh) + 3)
                v_minus = jnp.where(in_delta, -jnp.ones_like(v), v)
                pt = jnp.exp(lax.dot_general(k_ref[:, hs], q, NT_DIMS, preferred_element_type=F32) * SOFTMAX_SCALE)
                dvs.append(jnp.dot(pt.astype(BF16), do16, preferred_element_type=F32))
                dst = pt * lax.dot_general(v_minus, do16, NT_DIMS, preferred_element_type=F32) * SOFTMAX_SCALE
                dk_ref[:, hs] = jnp.dot(dst.astype(BF16), q, preferred_element_type=F32)
            dv_ref[...] = jnp.where(_own_lanes(dvs[0].shape, 0), dvs[0], dvs[1])

        pl.when(kj < ncb)(lambda: cols(0))
        pl.when(kj >= ncb)(lambda: cols(n_ctx))

    return pl.pallas_call(
        body,
        out_shape=[jax.ShapeDtypeStruct((T, MLA_HEADS * HEAD_LANES), F32),
                   jax.ShapeDtypeStruct((T, MLA_HEADS * V_HEAD_DIM), F32)],
        grid=(N_PAIRS, nq),
        in_specs=[pl.BlockSpec((T, 2 * HEAD_LANES), lambda h, j: (0, h)),
                  pl.BlockSpec((T, 2 * HEAD_LANES), lambda h, j: (0, h)),
                  pl.BlockSpec((tq, 2 * HEAD_LANES), lambda h, j: (j, h)),
                  pl.BlockSpec((tq, 2 * V_HEAD_DIM), lambda h, j: (j, h))],
        out_specs=[pl.BlockSpec((tq, 2 * HEAD_LANES), lambda h, j: (j, h)),
                   pl.BlockSpec((tq, 2 * V_HEAD_DIM), lambda h, j: (j, h))],
        compiler_params=_params(("parallel", "parallel"), VMEM_LIMIT), name="attn_bwd_dkv")(
            qs, dos, kb, vb)


def _split_bf16(x):
    hi = x.astype(BF16)
    return hi, (x - hi.astype(F32)).astype(BF16)


def q_heads(qn, w_uq_p, tabs, name="l0_uq"):
    T, K = qn.shape
    N = w_uq_p.shape[1]
    tm = math.gcd(ROW_TILE, T)

    def body(a_ref, w_ref, c_ref, sn_ref, sp_ref, o_ref):
        acc = jnp.dot(a_ref[...], w_ref[...], preferred_element_type=F32)
        o_ref[...] = _rope_tiles(acc, c_ref[...], sn_ref[...], sp_ref[...], False).astype(BF16)

    tab = pl.BlockSpec((tm, HEAD_LANES), lambda i: (i, 0))
    return pl.pallas_call(
        body, out_shape=jax.ShapeDtypeStruct((T, N), BF16), grid=(T // tm,),
        in_specs=[pl.BlockSpec((tm, K), lambda i: (i, 0)), pl.BlockSpec((K, N), lambda i: (0, 0)), tab, tab, tab],
        out_specs=pl.BlockSpec((tm, N), lambda i: (i, 0)),
        compiler_params=_params(("parallel",), VMEM_LIMIT), name=name)(qn, w_uq_p, *tabs)


def kv_heads(kvn, w_kn_p, w_v, kr, spread, tabs, name="l0_ukv"):
    T, K = kvn.shape
    N = w_kn_p.shape[1]
    NV = w_v.shape[1]
    tm = math.gcd(ROW_TILE, T)

    def body(a_ref, wk_ref, wv_ref, kr_ref, e_ref, c_ref, sn_ref, sp_ref, k_ref, v_ref):
        a = a_ref[...]
        hi, lo = _split_bf16(kr_ref[...])
        acc = (jnp.dot(a, wk_ref[...], preferred_element_type=F32)
               + jnp.dot(hi, e_ref[...], preferred_element_type=F32)
               + jnp.dot(lo, e_ref[...], preferred_element_type=F32))
        roped = _rope_tiles(acc, c_ref[...], sn_ref[...], sp_ref[...], False)
        lane = lax.broadcasted_iota(jnp.int32, roped.shape, 1) % HEAD_LANES
        k_ref[...] = jnp.where((lane >= STAT_LANE) & (lane < STAT_LANE + 3), 1.0, roped).astype(BF16)
        v_ref[...] = jnp.dot(a, wv_ref[...], preferred_element_type=F32).astype(BF16)

    tab = pl.BlockSpec((tm, HEAD_LANES), lambda i: (i, 0))
    return pl.pallas_call(
        body, out_shape=[jax.ShapeDtypeStruct((T, N), BF16), jax.ShapeDtypeStruct((T, NV), BF16)], grid=(T // tm,),
        in_specs=[pl.BlockSpec((tm, K), lambda i: (i, 0)), pl.BlockSpec((K, N), lambda i: (0, 0)),
                  pl.BlockSpec((K, NV), lambda i: (0, 0)), pl.BlockSpec((tm, QK_ROPE_DIM), lambda i: (i, 0)),
                  pl.BlockSpec((QK_ROPE_DIM, N), lambda i: (0, 0)), tab, tab, tab],
        out_specs=[pl.BlockSpec((tm, N), lambda i: (i, 0)), pl.BlockSpec((tm, NV), lambda i: (i, 0))],
        compiler_params=_params(("parallel",), VMEM_LIMIT), name=name)(kvn, w_kn_p, w_v, kr, spread, *tabs)


def heads_unrope(d, tabs, spread=None, name="unrope"):
    T, N = d.shape
    tm = math.gcd(ROW_TILE, T)

    def body(*refs):
        if spread is None:
            d_ref, c_ref, sn_ref, sp_ref, o_ref = refs
        else:
            d_ref, c_ref, sn_ref, sp_ref, e_ref, o_ref, kr_ref = refs
        g = _rope_tiles(d_ref[...], c_ref[...], sn_ref[...], sp_ref[...], True)
        o_ref[...] = g.astype(BF16)
        if spread is not None:
            hi, lo = _split_bf16(g)
            kr_ref[...] = (lax.dot_general(hi, e_ref[...], NT_DIMS, preferred_element_type=F32)
                           + lax.dot_general(lo, e_ref[...], NT_DIMS, preferred_element_type=F32))

    tab = pl.BlockSpec((tm, HEAD_LANES), lambda i: (i, 0))
    row = pl.BlockSpec((tm, N), lambda i: (i, 0))
    ins, in_specs = [d, *tabs], [row, tab, tab, tab]
    out_shape, out_specs = [jax.ShapeDtypeStruct((T, N), BF16)], [row]
    if spread is not None:
        ins.append(spread)
        in_specs.append(pl.BlockSpec(spread.shape, lambda i: (0, 0)))
        out_shape.append(jax.ShapeDtypeStruct((T, spread.shape[0]), F32))
        out_specs.append(pl.BlockSpec((tm, spread.shape[0]), lambda i: (i, 0)))
    return pl.pallas_call(
        body, out_shape=out_shape, grid=(T // tm,), in_specs=in_specs, out_specs=out_specs,
        compiler_params=_params(("parallel",), VMEM_LIMIT), name=name)(*ins)


def _cmul(ar, ai, br, bi):
    return ar * br - ai * bi, ar * bi + ai * br


def s5_chain(finals, s0, a, n_steps, reverse, name):
    W = finals.shape[-1]
    first = N_SEG - 1 if reverse else 0

    def body(f_ref, s0_ref, a_ref, c_ref):
        pr, pi = jnp.ones((1, W), F32), jnp.zeros((1, W), F32)
        br, bi = a_ref[0], a_ref[1]
        n = n_steps
        while n:
            if n & 1:
                pr, pi = _cmul(pr, pi, br, bi)
            br, bi = _cmul(br, bi, br, bi)
            n >>= 1
        fr, fi = f_ref[0], f_ref[1]
        row = lax.broadcasted_iota(jnp.int32, (N_SEG, W), 0)
        s0r = jnp.broadcast_to(s0_ref[0], (N_SEG, W))
        s0i = jnp.broadcast_to(s0_ref[1], (N_SEG, W))
        cr = jnp.where(row == first, s0r, 0.0)
        ci = jnp.where(row == first, s0i, 0.0)
        shift = N_SEG - 1 if reverse else 1
        for _ in range(N_SEG - 1):
            mr, mi = _cmul(pr, pi, cr, ci)
            tr = pltpu.roll(fr + mr, shift, 0)
            ti = pltpu.roll(fi + mi, shift, 0)
            cr = jnp.where(row == first, s0r, tr)
            ci = jnp.where(row == first, s0i, ti)
        c_ref[0] = cr
        c_ref[1] = ci

    return pl.pallas_call(body, out_shape=jax.ShapeDtypeStruct((2, N_SEG, W), F32), name=name)(finals, s0, a)


def _scan_chunk(bur, bui, st_ref, a_ref, n_steps, reverse):
    for lc in range(S5_LANES // BLK_ST):
        sl = slice(lc * BLK_ST, (lc + 1) * BLK_ST)
        lr = jnp.broadcast_to(a_ref[0, :, sl], (N_SEG, BLK_ST))
        li = jnp.broadcast_to(a_ref[1, :, sl], (N_SEG, BLK_ST))

        def step(jj, carry, sl=sl, lr=lr, li=li):
            sr, si = carry
            j = (n_steps - 1 - jj) if reverse else jj
            r0 = pl.multiple_of(j * N_SEG, N_SEG)
            nr = lr * sr - li * si + bur[pl.ds(r0, N_SEG), sl]
            ni = lr * si + li * sr + bui[pl.ds(r0, N_SEG), sl]
            bur[pl.ds(r0, N_SEG), sl] = nr
            bui[pl.ds(r0, N_SEG), sl] = ni
            return nr, ni

        sr, si = lax.fori_loop(0, n_steps, step, (st_ref[0, :, sl], st_ref[1, :, sl]))
        st_ref[0, :, sl] = sr
        st_ref[1, :, sl] = si


def _project_in(x16, w_re, w_im, bur, bui, adjoint):
    for gb in range(N_BLOCKS):
        xb = x16[:, gb * BLK_CH:(gb + 1) * BLK_CH]
        sl = slice(gb * BLK_ST, (gb + 1) * BLK_ST)
        if adjoint:
            dn = (((1,), (1,)), ((), ()))
            bur[:, sl] = lax.dot_general(xb, w_re[gb], dn, preferred_element_type=F32)
            bui[:, sl] = -lax.dot_general(xb, w_im[gb], dn, preferred_element_type=F32)
        else:
            bur[:, sl] = jnp.dot(xb, w_re[gb], preferred_element_type=F32)
            bui[:, sl] = jnp.dot(xb, w_im[gb], preferred_element_type=F32)


def s5_scan(act, w_re, w_im, a, init, *, reverse, adjoint=False, c_re=None, c_im=None, add=None,
            want_ckpt=False, rows=None, name):
    act_off, N = rows if rows is not None else (0, act.shape[0])
    R = math.gcd(ROW_TILE, N, act_off)
    nch, jc = N // R, R // N_SEG
    with_out = c_re is not None

    def chunk(i):
        return (nch - 1 - i) if reverse else i

    def body(*refs):
        act_ref, wre_ref, wim_ref, a_ref, init_ref = refs[:5]
        k = 5
        if with_out:
            cre_ref, cim_ref = refs[k:k + 2]
            k += 2
        if add is not None:
            add_ref = refs[k]
            k += 1
        if with_out:
            out_ref = refs[k]
            k += 1
        if want_ckpt:
            ck_ref = refs[k]
            k += 1
        fin_ref, bur, bui = refs[k:k + 3]

        @pl.when(pl.program_id(0) == 0)
        def _():
            fin_ref[...] = init_ref[...]

        if want_ckpt:
            ck_ref[0] = fin_ref[...]
        _project_in(act_ref[...].astype(BF16), wre_ref, wim_ref, bur, bui, adjoint)
        _scan_chunk(bur, bui, fin_ref, a_ref, jc, reverse)
        if with_out:
            for gb in range(N_BLOCKS):
                sl = slice(gb * BLK_ST, (gb + 1) * BLK_ST)
                y = (jnp.dot(bur[:, sl].astype(BF16), cre_ref[gb], preferred_element_type=F32)
                     - jnp.dot(bui[:, sl].astype(BF16), cim_ref[gb], preferred_element_type=F32))
                cs = slice(gb * BLK_CH, (gb + 1) * BLK_CH)
                if add is not None:
                    y = y + add_ref[:, cs]
                out_ref[:, cs] = y

    row_spec = pl.BlockSpec((R, D_MODEL), lambda i: (chunk(i), 0))
    act_spec = pl.BlockSpec((R, D_MODEL), lambda i: (chunk(i) + act_off // R, 0))
    w_spec = pl.BlockSpec(w_re.shape, lambda i: (0, 0, 0))
    st_spec = pl.BlockSpec((2, N_SEG, S5_LANES), lambda i: (0, 0, 0))
    ins = [act, w_re, w_im, a, init]
    in_specs = [act_spec, w_spec, w_spec, pl.BlockSpec((2, 1, S5_LANES), lambda i: (0, 0, 0)), st_spec]
    if with_out:
        ins += [c_re, c_im]
        in_specs += [pl.BlockSpec(c_re.shape, lambda i: (0, 0, 0))] * 2
    if add is not None:
        ins.append(add)
        in_specs.append(row_spec)
    out_shape, out_specs = [], []
    if with_out:
        out_shape.append(jax.ShapeDtypeStruct((N, D_MODEL), F32))
        out_specs.append(row_spec)
    if want_ckpt:
        out_shape.append(jax.ShapeDtypeStruct((nch, 2, N_SEG, S5_LANES), F32))
        out_specs.append(pl.BlockSpec((1, 2, N_SEG, S5_LANES), lambda i: (chunk(i), 0, 0, 0)))
    out_shape.append(jax.ShapeDtypeStruct((2, N_SEG, S5_LANES), F32))
    out_specs.append(st_spec)
    res = pl.pallas_call(
        body, out_shape=out_shape, grid=(nch,), in_specs=in_specs, out_specs=out_specs,
        scratch_shapes=[pltpu.VMEM((R, S5_LANES), F32), pltpu.VMEM((R, S5_LANES), F32)],
        compiler_params=_params(("arbitrary",), VMEM_LIMIT), name=name)(*ins)
    res = list(res)
    out = res.pop(0) if with_out else None
    ckpt = res.pop(0) if want_ckpt else None
    return out, ckpt, res[0]


def s5_grads(dy, u, ckpt, b_re, b_im, c_re, c_im, lam, init_adj, *, reverse, add=None, u_off=0, name):
    N = dy.shape[0]
    R = math.gcd(ROW_TILE, N, u_off)
    nch, jc = N // R, R // N_SEG
    W = S5_LANES

    def chunk(i):
        return i if reverse else (nch - 1 - i)

    def body(*refs):
        dy_ref, u_ref, ck_ref, bre_ref, bim_ref, cre_ref, cim_ref, lam_ref, init_ref = refs[:9]
        k = 9
        if add is not None:
            add_ref = refs[k]
            k += 1
        du_ref, dlam_ref, dbre_ref, dbim_ref, dcre_ref, dcim_ref, fin_ref = refs[k:k + 7]
        sr_buf, si_buf, er_buf, ei_buf, st_buf = refs[k + 7:k + 12]

        @pl.when(pl.program_id(0) == 0)
        def _():
            fin_ref[...] = init_ref[...]
            dlam_ref[...] = jnp.zeros_like(dlam_ref)
            dbre_ref[...] = jnp.zeros_like(dbre_ref)
            dbim_ref[...] = jnp.zeros_like(dbim_ref)
            dcre_ref[...] = jnp.zeros_like(dcre_ref)
            dcim_ref[...] = jnp.zeros_like(dcim_ref)

        u16 = u_ref[...].astype(BF16)
        dy16 = dy_ref[...].astype(BF16)
        st_buf[...] = ck_ref[0]
        _project_in(u16, bre_ref, bim_ref, sr_buf, si_buf, False)
        _scan_chunk(sr_buf, si_buf, st_buf, lam_ref, jc, reverse)
        _project_in(dy16, cre_ref, cim_ref, er_buf, ei_buf, True)
        for lc in range(W // BLK_ST):
            sl = slice(lc * BLK_ST, (lc + 1) * BLK_ST)
            lr = jnp.broadcast_to(lam_ref[0, :, sl], (N_SEG, BLK_ST))
            li = jnp.broadcast_to(lam_ref[1, :, sl], (N_SEG, BLK_ST))

            def one(r0, spr, spi, carry, sl=sl, lr=lr, li=li):
                gr, gi, ar, ai = carry
                nr = er_buf[pl.ds(r0, N_SEG), sl] + lr * gr + li * gi
                ni = ei_buf[pl.ds(r0, N_SEG), sl] + lr * gi - li * gr
                er_buf[pl.ds(r0, N_SEG), sl] = nr
                ei_buf[pl.ds(r0, N_SEG), sl] = ni
                return nr, ni, ar + spr * nr + spi * ni, ai + spr * ni - spi * nr

            def step(ff, carry, sl=sl, one=one):
                f = jc - 1 - ff
                j = (jc - 1 - f) if reverse else f
                jp = (j + 1) if reverse else (j - 1)
                r0 = pl.multiple_of(j * N_SEG, N_SEG)
                p0 = pl.multiple_of(jp * N_SEG, N_SEG)
                return one(r0, sr_buf[pl.ds(p0, N_SEG), sl], si_buf[pl.ds(p0, N_SEG), sl], carry)

            carry = (fin_ref[0, :, sl], fin_ref[1, :, sl], dlam_ref[0, :, sl], dlam_ref[1, :, sl])
            carry = lax.fori_loop(0, jc - 1, step, carry)
            r_first = (jc - 1) * N_SEG if reverse else 0
            gr, gi, ar, ai = one(r_first, ck_ref[0, 0, :, sl], ck_ref[0, 1, :, sl], carry)
            fin_ref[0, :, sl] = gr
            fin_ref[1, :, sl] = gi
            dlam_ref[0, :, sl] = ar
            dlam_ref[1, :, sl] = ai
        tn = (((0,), (0,)), ((), ()))
        nt = (((1,), (1,)), ((), ()))
        for gb in range(N_BLOCKS):
            sl = slice(gb * BLK_ST, (gb + 1) * BLK_ST)
            cs = slice(gb * BLK_CH, (gb + 1) * BLK_CH)
            gr16 = er_buf[:, sl].astype(BF16)
            gi16 = ei_buf[:, sl].astype(BF16)
            du = (lax.dot_general(gr16, bre_ref[gb], nt, preferred_element_type=F32)
                  + lax.dot_general(gi16, bim_ref[gb], nt, preferred_element_type=F32))
            if add is not None:
                du = du + add_ref[:, cs]
            du_ref[:, cs] = du
            ub, dyb = u16[:, cs], dy16[:, cs]
            dbre_ref[gb] += lax.dot_general(ub, gr16, tn, preferred_element_type=F32)
            dbim_ref[gb] += lax.dot_general(ub, gi16, tn, preferred_element_type=F32)
            dcre_ref[gb] += lax.dot_general(sr_buf[:, sl].astype(BF16), dyb, tn, preferred_element_type=F32)
            dcim_ref[gb] -= lax.dot_general(si_buf[:, sl].astype(BF16), dyb, tn, preferred_element_type=F32)

    row_spec = pl.BlockSpec((R, D_MODEL), lambda i: (chunk(i), 0))
    st_spec = pl.BlockSpec((2, N_SEG, W), lambda i: (0, 0, 0))
    wb_spec = pl.BlockSpec(b_re.shape, lambda i: (0, 0, 0))
    wc_spec = pl.BlockSpec(c_re.shape, lambda i: (0, 0, 0))
    ins = [dy, u, ckpt, b_re, b_im, c_re, c_im, lam, init_adj]
    u_spec = pl.BlockSpec((R, D_MODEL), lambda i: (chunk(i) + u_off // R, 0))
    in_specs = [row_spec, u_spec, pl.BlockSpec((1, 2, N_SEG, W), lambda i: (chunk(i), 0, 0, 0)),
                wb_spec, wb_spec, wc_spec, wc_spec, pl.BlockSpec((2, 1, W), lambda i: (0, 0, 0)), st_spec]
    if add is not None:
        ins.append(add)
        in_specs.append(row_spec)
    out_shape = [jax.ShapeDtypeStruct((N, D_MODEL), F32), jax.ShapeDtypeStruct((2, N_SEG, W), F32),
                 jax.ShapeDtypeStruct(b_re.shape, F32), jax.ShapeDtypeStruct(b_re.shape, F32),
                 jax.ShapeDtypeStruct(c_re.shape, F32), jax.ShapeDtypeStruct(c_re.shape, F32),
                 jax.ShapeDtypeStruct((2, N_SEG, W), F32)]
    out_specs = [row_spec, st_spec, wb_spec, wb_spec, wc_spec, wc_spec, st_spec]
    return pl.pallas_call(
        body, out_shape=out_shape, grid=(nch,), in_specs=in_specs, out_specs=out_specs,
        scratch_shapes=[pltpu.VMEM((R, W), F32) for _ in range(4)] + [pltpu.VMEM((2, N_SEG, W), F32)],
        compiler_params=_params(("arbitrary",), VMEM_LIMIT), name=name)(*ins)


def adamw(w, g, m, v, name="adamw", after=None):
    n, d = w.shape
    lanes = -(-d // 128) * 128
    tm = n
    while tm * lanes * 4 > (1 << 20) and tm % 16 == 0:
        tm //= 2
    c1 = 1.0 - ADAM_B1 ** ADAM_STEP
    c2 = 1.0 - ADAM_B2 ** ADAM_STEP

    def body(w_ref, g_ref, m_ref, v_ref, *rest):
        d_ref, nm_ref, nv_ref = rest[-3:]
        g_ = g_ref[...]
        m_ = ADAM_B1 * m_ref[...] + (1.0 - ADAM_B1) * g_
        v_ = ADAM_B2 * v_ref[...] + (1.0 - ADAM_B2) * (g_ * g_)
        d_ref[...] = -ADAM_LR * ((m_ / c1) / (jnp.sqrt(v_ / c2) + ADAM_EPS) + ADAM_WD * w_ref[...])
        nm_ref[...] = m_
        nv_ref[...] = v_

    spec = pl.BlockSpec((tm, d), lambda i: (i, 0))
    extra = [] if after is None else [after]
    return pl.pallas_call(
        body, out_shape=[jax.ShapeDtypeStruct((n, d), F32)] * 3, grid=(n // tm,),
        in_specs=[spec] * 4 + [pl.BlockSpec(memory_space=pl.ANY)] * len(extra), out_specs=[spec] * 3,
        compiler_params=_params(("parallel",), VMEM_LIMIT), name=name)(w, g, m, v, *extra)


def _coords():
    return lax.axis_index("x"), lax.axis_index("y"), lax.axis_index("c")


def exchange(arrays, out_shapes, remote, local, name, aliases=None):
    n_in, n_out, n_rem, n_loc = len(arrays), len(out_shapes), len(remote), len(local)

    def at(ref, idx):
        return ref if idx is None else ref.at[idx]

    def body(*refs):
        ins, outs = refs[:n_in], refs[n_in:n_in + n_out]
        send_sems, recv_sems, local_sems = refs[n_in + n_out:]
        me = _coords()
        sends, recvs = [], []
        for k, (flip, ii, src_at, oi, dst_at) in enumerate(remote):
            peer = (me[0] ^ flip[0], me[1] ^ flip[1], me[2] ^ flip[2])
            src = at(ins[ii], src_at(me, peer))
            sends.append(pltpu.make_async_remote_copy(
                src_ref=src, dst_ref=at(outs[oi], dst_at(me)), send_sem=send_sems.at[k], recv_sem=recv_sems.at[k],
                device_id=peer, device_id_type=MESH))
            recvs.append(pltpu.make_async_remote_copy(
                src_ref=src, dst_ref=at(outs[oi], dst_at(peer)), send_sem=send_sems.at[k], recv_sem=recv_sems.at[k],
                device_id=peer, device_id_type=MESH))
        locs = [pltpu.make_async_copy(at(ins[ii], src_at(me)), at(outs[oi], dst_at(me)), local_sems.at[k])
                for k, (ii, src_at, oi, dst_at) in enumerate(local)]
        for cp in locs + sends:
            cp.start()
        for cp in recvs:
            cp.wait_recv()
        for cp in sends:
            cp.wait_send()
        for cp in locs:
            cp.wait()

    hbm = pl.BlockSpec(memory_space=pl.ANY)
    return pl.pallas_call(
        body, out_shape=list(out_shapes), in_specs=[hbm] * n_in, out_specs=[hbm] * n_out,
        scratch_shapes=[pltpu.SemaphoreType.DMA((n_rem,)), pltpu.SemaphoreType.DMA((n_rem,)),
                        pltpu.SemaphoreType.DMA((max(n_loc, 1),))],
        input_output_aliases=aliases or {}, name=name)(*arrays)


ALL_FLIPS = [(dx, dy, dc) for dx in (0, 1) for dy in (0, 1) for dc in (0, 1)][1:]
CHIP_FLIPS = [(1, 0, 0), (0, 1, 0), (1, 1, 0)]
CORE_FLIP = (0, 0, 1)


def _dev_index(p):
    return 4 * p[0] + 2 * p[1] + p[2]


def _chip_index(p):
    return 2 * p[0] + p[1]


def _gather(xs, flips, index, n, name):
    arrays = [x[None] for x in xs]
    outs = [jax.ShapeDtypeStruct((n,) + x.shape, x.dtype) for x in xs]
    remote = [(f, a, lambda me, peer: (0,), a, lambda s: (index(s),)) for a in range(len(xs)) for f in flips]
    local = [(a, lambda me: (0,), a, lambda me: (index(me),)) for a in range(len(xs))]
    return exchange(arrays, outs, remote, local, name)


def allgather_devices(x, name):
    return _gather([x], ALL_FLIPS, _dev_index, N_DEV, name)[0]


def allgather_chips(xs, name):
    return _gather(xs, CHIP_FLIPS, _chip_index, N_CHIP, name)


def gather_halves(xs, name):
    n = len(xs)
    nk = n * len(CHIP_FLIPS)

    def body(*refs):
        ins, outs = refs[:n], refs[n:2 * n]
        ici_send, ici_recv, d2d_send, d2d_recv = refs[2 * n:]
        me = _coords()
        sibling = (me[0], me[1], 1 - me[2])
        first, passed, landed = [], [], []
        for a in range(n):
            half = ins[a].shape[0] // 2
            mine = ins[a].at[pl.ds(pl.multiple_of(me[2] * half, 16), half)]
            for j, flip in enumerate(CHIP_FLIPS):
                k = a * len(CHIP_FLIPS) + j
                peer = (me[0] ^ flip[0], me[1] ^ flip[1], me[2])
                first.append(pltpu.make_async_remote_copy(
                    src_ref=mine, dst_ref=outs[a].at[_chip_index(me), me[2]], send_sem=ici_send.at[k],
                    recv_sem=ici_recv.at[k], device_id=peer, device_id_type=MESH))
                arrived = outs[a].at[_chip_index(peer), me[2]]
                landed.append(pltpu.make_async_remote_copy(
                    src_ref=mine, dst_ref=arrived, send_sem=ici_send.at[k], recv_sem=ici_recv.at[k],
                    device_id=peer, device_id_type=MESH))
                passed.append(pltpu.make_async_remote_copy(
                    src_ref=arrived, dst_ref=arrived, send_sem=d2d_send.at[k], recv_sem=d2d_recv.at[k],
                    device_id=sibling, device_id_type=MESH))
        for cp in first:
            cp.start()
        for k in range(nk):
            landed[k].wait_recv()
            passed[k].start()
        for a in range(n):
            for j, flip in enumerate(CHIP_FLIPS):
                k = a * len(CHIP_FLIPS) + j
                peer_chip = _chip_index((me[0] ^ flip[0], me[1] ^ flip[1]))
                from_sibling = outs[a].at[peer_chip, 1 - me[2]]
                pltpu.make_async_remote_copy(
                    src_ref=from_sibling, dst_ref=from_sibling, send_sem=d2d_send.at[k], recv_sem=d2d_recv.at[k],
                    device_id=sibling, device_id_type=MESH).wait_recv()
        for cp in first + passed:
            cp.wait_send()

    hbm = pl.BlockSpec(memory_space=pl.ANY)
    return pl.pallas_call(
        body, out_shape=[jax.ShapeDtypeStruct((N_CHIP, 2, x.shape[0] // 2, x.shape[1]), x.dtype) for x in xs],
        in_specs=[hbm] * n, out_specs=[hbm] * n,
        scratch_shapes=[pltpu.SemaphoreType.DMA((nk,)) for _ in range(4)], name=name)(*xs)


HBM_SPEC = pl.BlockSpec(memory_space=pltpu.HBM)
SEM_SPEC = pl.BlockSpec(memory_space=pltpu.SEMAPHORE)
DATAFLOW = pltpu.SideEffectType.DATAFLOW_SIDE_EFFECTING


def _at(ref, idx):
    return ref if idx is None else ref.at[idx]


def _peer(me, flip):
    return (me[0] ^ flip[0], me[1] ^ flip[1], me[2] ^ flip[2])


def exchange_start(arrays, land_shapes, remote, name, after=None):
    n_in, n_out, nk = len(arrays), len(land_shapes), len(remote)
    n_after = 0 if after is None else 1

    def body(*refs):
        srcs, lands = refs[:n_in], refs[n_in:n_in + n_out]
        first_out = n_in + n_out + n_after
        send_sems, recv_sems, token = refs[first_out], refs[first_out + 1], refs[-1]
        me = _coords()
        for k, (flip, ii, src_at, oi, dst_at) in enumerate(remote):
            peer = _peer(me, flip)
            pltpu.make_async_remote_copy(
                src_ref=_at(srcs[ii], src_at(me, peer)), dst_ref=_at(lands[oi], dst_at(me)), send_sem=send_sems.at[k],
                recv_sem=recv_sems.at[k], device_id=peer, device_id_type=MESH).start()
        token[...] = jnp.zeros_like(token)

    lands = [lax.empty(s.shape, s.dtype) for s in land_shapes]
    bufs = list(arrays) + lands
    out = pl.pallas_call(
        body, name=name,
        out_shape=(pltpu.SemaphoreType.DMA((nk,)), pltpu.SemaphoreType.DMA((nk,)),
                   *[pltpu.HBM(b.shape, b.dtype) for b in bufs], jax.ShapeDtypeStruct((8, 128), F32)),
        in_specs=[HBM_SPEC] * len(bufs) + [pl.BlockSpec(memory_space=pl.ANY)] * n_after,
        out_specs=(SEM_SPEC, SEM_SPEC, *[HBM_SPEC] * len(bufs), pl.BlockSpec(memory_space=pltpu.VMEM)),
        input_output_aliases={a: 2 + a for a in range(len(bufs))},
        compiler_params=pltpu.CompilerParams(has_side_effects=DATAFLOW),
    )(*[pltpu.with_memory_space_constraint(b, pltpu.HBM) for b in bufs], *([after] if n_after else []))
    flight = (out[0], out[1], list(out[2:2 + n_in]), list(out[2 + n_in:2 + n_in + n_out]), remote)
    return flight, out[-1]


def exchange_wait(flight, after, name):
    send_sems, recv_sems, arrays, lands, remote = flight
    n_in, n_out = len(arrays), len(lands)

    def body(*refs):
        srcs, lnds = refs[:n_in], refs[n_in:n_in + n_out]
        s_sems, r_sems = refs[n_in + n_out], refs[n_in + n_out + 1]
        me = _coords()
        for k, (flip, ii, src_at, oi, dst_at) in enumerate(remote):
            peer = _peer(me, flip)
            copy = pltpu.make_async_remote_copy(
                src_ref=_at(srcs[ii], src_at(me, peer)), dst_ref=_at(lnds[oi], dst_at(peer)), send_sem=s_sems.at[k],
                recv_sem=r_sems.at[k], device_id=peer, device_id_type=MESH)
            copy.wait_send()
            copy.wait_recv()

    bufs = list(arrays) + list(lands)
    out = pl.pallas_call(
        body, name=name,
        out_shape=tuple(pltpu.HBM(b.shape, b.dtype) for b in bufs),
        in_specs=[HBM_SPEC] * len(bufs) + [SEM_SPEC, SEM_SPEC, pl.BlockSpec(memory_space=pl.ANY)],
        out_specs=tuple([HBM_SPEC] * len(bufs)),
        input_output_aliases={a: a for a in range(len(bufs))},
        compiler_params=pltpu.CompilerParams(has_side_effects=DATAFLOW),
    )(*bufs, send_sems, recv_sems, after)
    return list(out[:n_in]), list(out[n_in:])


def _half_tile(h, cd):
    return h if h * cd * 4 <= (1 << 20) else math.gcd(512, h)


def pair_add(g, got, core, out_dtype, name):
    _, _, h, cd = g.shape
    th = _half_tile(h, cd)

    def body(c_ref, g_ref, got_ref, o_ref):
        o_ref[0] = (g_ref[0, 0] + got_ref[0]).astype(o_ref.dtype)

    return pl.pallas_call(
        body, out_shape=jax.ShapeDtypeStruct((N_CHIP, h, cd), out_dtype),
        grid_spec=pltpu.PrefetchScalarGridSpec(
            num_scalar_prefetch=1, grid=(N_CHIP, h // th),
            in_specs=[pl.BlockSpec((1, 1, th, cd), lambda q, i, c: (q, c[0], i, 0)),
                      pl.BlockSpec((1, th, cd), lambda q, i, c: (q, i, 0))],
            out_specs=pl.BlockSpec((1, th, cd), lambda q, i, c: (q, i, 0))),
        compiler_params=_params(("parallel", "parallel"), VMEM_LIMIT), name=name)(core, g, got)


def sum_chips(parts, sums, place, name):
    _, h, cd = parts.shape
    th = _half_tile(h, cd)

    def body(pc_ref, p_ref, own_ref, o_ref):
        acc = None
        for q in range(N_CHIP):
            term = jnp.where(pc_ref[1] == q, own_ref[0], p_ref[q]).astype(F32)
            acc = term if acc is None else acc + term
        o_ref[0] = acc

    return pl.pallas_call(
        body, out_shape=jax.ShapeDtypeStruct((2, h, cd), F32),
        grid_spec=pltpu.PrefetchScalarGridSpec(
            num_scalar_prefetch=1, grid=(h // th,),
            in_specs=[pl.BlockSpec((N_CHIP, th, cd), lambda i, pc: (0, i, 0)),
                      pl.BlockSpec((1, th, cd), lambda i, pc: (pc[1], i, 0))],
            out_specs=pl.BlockSpec((1, th, cd), lambda i, pc: (pc[0], i, 0))),
        compiler_params=_params(("parallel",), VMEM_LIMIT), name=name)(place, parts, sums)


def to_segments(a, n_ctx):
    def one(p):
        n = p.shape[0]
        return p.reshape(N_SEG, n // N_SEG, -1).transpose(1, 0, 2).reshape(n, -1)
    return jnp.concatenate([one(a[:n_ctx]), one(a[n_ctx:])], axis=0) if n_ctx else one(a)


def from_segments(a, n_ctx):
    def one(p):
        n = p.shape[0]
        return p.reshape(n // N_SEG, N_SEG, -1).transpose(1, 0, 2).reshape(n, -1)
    return jnp.concatenate([one(a[:n_ctx]), one(a[n_ctx:])], axis=0) if n_ctx else one(a)


def rope_tables(n_ctx, n_lat):
    f32 = np.float32
    rows = n_lat // GRID_W
    row = np.repeat(np.arange(rows), GRID_W).astype(f32)
    col = np.tile(np.arange(GRID_W), rows).astype(f32)
    d = QK_ROPE_DIM // 2
    inv = (f32(1.0) / np.power(f32(ROPE_THETA), np.arange(0, d, 2, dtype=f32) / f32(d))).astype(f32)
    ang = np.concatenate([row[:, None] * inv[None, :], col[:, None] * inv[None, :]], axis=1).astype(f32)
    cos = np.concatenate([np.ones((n_ctx, d), f32), np.cos(ang)], axis=0)
    sin = np.concatenate([np.zeros((n_ctx, d), f32), np.sin(ang)], axis=0)
    q = QK_ROPE_DIM // 4
    T = n_ctx + n_lat
    ones, zeros = np.ones((T, QK_NOPE_DIM), f32), np.zeros((T, QK_NOPE_DIM), f32)
    tail, z8 = np.zeros((T, HEAD_LANES - QK_DIM), f32), np.zeros((T, q), f32)
    cr, cc, sr, sc = cos[:, :q], cos[:, q:], sin[:, :q], sin[:, q:]
    cos_t = np.concatenate([ones, cr, cr, cc, cc, tail], axis=1)
    sin_next = np.concatenate([zeros, -sr, z8, -sc, z8, tail], axis=1)
    sin_prev = np.concatenate([zeros, z8, sr, z8, sc, tail], axis=1)
    return tuple(jnp.asarray(t, F32) for t in (cos_t, sin_next, sin_prev))


def pad_heads(w, used):
    k = w.shape[0]
    return jnp.pad(w.reshape(k, MLA_HEADS, used), ((0, 0), (0, 0), (0, HEAD_LANES - used))).reshape(k, -1)


def unpad_heads(w, used):
    k = w.shape[0]
    return w.reshape(k, MLA_HEADS, HEAD_LANES)[:, :, :used].reshape(k, MLA_HEADS * used)


def rotary_spread():
    lane = np.arange(MLA_HEADS * HEAD_LANES) % HEAD_LANES
    return jnp.asarray(lane[None, :] == (QK_NOPE_DIM + np.arange(QK_ROPE_DIM))[:, None], BF16)


def s5_discretise(a_re, a_im, log_step, b_re, b_im):
    dt = jnp.exp(log_step)[:, None]
    mag = jnp.exp(a_re * dt)
    lb_re = mag * jnp.cos(a_im * dt)
    lb_im = mag * jnp.sin(a_im * dt)
    den = a_re * a_re + a_im * a_im
    nr = lb_re - 1.0
    f_re = ((nr * a_re + lb_im * a_im) / den)[..., None]
    f_im = ((lb_im * a_re - nr * a_im) / den)[..., None]
    return lb_re, lb_im, f_re * b_re - f_im * b_im, f_re * b_im + f_im * b_re


def s5_block_weights(lb_re, lb_im, bb_re, bb_im, c_re, c_im):
    eye = jnp.eye(GROUPS_PER_BLOCK, dtype=F32)
    lam = jnp.stack([lb_re.reshape(1, S5_LANES), lb_im.reshape(1, S5_LANES)])

    def b_blocks(bb):
        t = bb.reshape(N_BLOCKS, GROUPS_PER_BLOCK, S5_STATE, S5_GROUP)
        return jnp.einsum("bgpc,gh->bgchp", t, eye).reshape(N_BLOCKS, BLK_CH, BLK_ST).astype(BF16)

    def c_blocks(cc):
        t = cc.reshape(N_BLOCKS, GROUPS_PER_BLOCK, S5_GROUP, S5_STATE)
        return jnp.einsum("bgcp,gh->bgphc", t, eye).reshape(N_BLOCKS, BLK_ST, BLK_CH).astype(BF16)

    return lam, b_blocks(bb_re), b_blocks(bb_im), c_blocks(c_re), c_blocks(c_im)


def b_block_diag(db):
    t = db.reshape(N_BLOCKS, GROUPS_PER_BLOCK, S5_GROUP, GROUPS_PER_BLOCK, S5_STATE)
    return jnp.einsum("bgchp,gh->bgpc", t, jnp.eye(GROUPS_PER_BLOCK, dtype=F32)).reshape(S5_GROUPS, S5_STATE, S5_GROUP)


def c_block_diag(dc):
    t = dc.reshape(N_BLOCKS, GROUPS_PER_BLOCK, S5_STATE, GROUPS_PER_BLOCK, S5_GROUP)
    return jnp.einsum("bgphc,gh->bgcp", t, jnp.eye(GROUPS_PER_BLOCK, dtype=F32)).reshape(S5_GROUPS, S5_GROUP, S5_STATE)


def conj(a):
    return jnp.stack([a[0], -a[1]])


PACK_TILE = 16 * 128


def pack_flat(parts, dtype):
    flat = [p.reshape(-1).astype(dtype) for p in parts]
    sizes = [f.shape[0] for f in flat]
    total = sum(sizes)
    pad = (-total) % PACK_TILE
    if pad:
        flat.append(jnp.zeros((pad,), dtype))
    offs = np.cumsum([0] + sizes)[:-1].tolist()
    return jnp.concatenate(flat).reshape(-1, 128), offs


def unpack_flat(buf, offs, shapes):
    flat = buf.reshape(-1)
    return [flat[o:o + int(np.prod(s))].reshape(s) for o, s in zip(offs, shapes)]


def s5_forward(p1, n_ctx, dirs):
    saved = []
    y = None
    ctx_rows, lat_rows = (0, n_ctx), (n_ctx, p1.shape[0] - n_ctx)
    zeros_tile = jnp.zeros((2, N_SEG, S5_LANES), F32)
    zeros_row = jnp.zeros((2, 1, S5_LANES), F32)
    for k, (lam, b_re, b_im, c_re, c_im) in enumerate(dirs):
        rev = k == 1
        last = 0 if rev else N_SEG - 1
        _, _, fin = s5_scan(p1, b_re, b_im, lam, zeros_tile, reverse=rev, rows=ctx_rows, name=f"s5_ctx_finals{k}")
        carry_c = s5_chain(fin, zeros_row, lam, n_ctx // N_SEG, rev, name=f"s5_ctx_chain{k}")
        _, ck_c, fin_c = s5_scan(p1, b_re, b_im, lam, carry_c, reverse=rev, want_ckpt=True, rows=ctx_rows,
                                 name=f"s5_ctx_scan{k}")
        s0 = fin_c[:, last:last + 1, :]
        _, _, fin = s5_scan(p1, b_re, b_im, lam, zeros_tile, reverse=rev, rows=lat_rows, name=f"s5_lat_finals{k}")
        carry_l = s5_chain(fin, s0, lam, lat_rows[1] // N_SEG, rev, name=f"s5_lat_chain{k}")
        y, ck_l, _ = s5_scan(p1, b_re, b_im, lam, carry_l, reverse=rev, c_re=c_re, c_im=c_im, add=y,
                             want_ckpt=True, rows=lat_rows, name=f"s5_lat_scan{k}")
        saved.append((ck_c, ck_l))
    return y, saved


def s5_backward(dy_l, du_extra_l, p1, n_ctx, dirs, saved):
    n_lat = p1.shape[0] - n_ctx
    zeros_tile = jnp.zeros((2, N_SEG, S5_LANES), F32)
    zeros_row = jnp.zeros((2, 1, S5_LANES), F32)
    dy_c = jnp.zeros((n_ctx, D_MODEL), F32)
    du_l, du_c = du_extra_l, None
    grads = []
    for k, (lam, b_re, b_im, c_re, c_im) in enumerate(dirs):
        rev = k == 1
        lam_c = conj(lam)
        ck_c, ck_l = saved[k]
        first = N_SEG - 1 if rev else 0
        _, _, fin = s5_scan(dy_l, c_re, c_im, lam_c, zeros_tile, reverse=not rev, adjoint=True,
                            name=f"s5_lat_adj_finals{k}")
        carry = s5_chain(fin, zeros_row, lam_c, n_lat // N_SEG, not rev, name=f"s5_lat_adj_chain{k}")
        du_l, dlam_l, dbr_l, dbi_l, dcr_l, dci_l, fin_a = s5_grads(
            dy_l, p1, ck_l, b_re, b_im, c_re, c_im, lam, carry, reverse=rev, add=du_l, u_off=n_ctx,
            name=f"s5_lat_grads{k}")
        g0 = fin_a[:, first:first + 1, :]
        carry = s5_chain(zeros_tile, g0, lam_c, n_ctx // N_SEG, not rev, name=f"s5_ctx_adj_chain{k}")
        du_c, dlam_c, dbr_c, dbi_c, _, _, _ = s5_grads(
            dy_c, p1, ck_c, b_re, b_im, c_re, c_im, lam, carry, reverse=rev, add=du_c, name=f"s5_ctx_grads{k}")
        dlam = jnp.sum(dlam_l + dlam_c, axis=1)
        grads.append((dlam, b_block_diag(dbr_l + dbr_c), b_block_diag(dbi_l + dbi_c),
                      c_block_diag(dcr_l), c_block_diag(dci_l)))
    return jnp.concatenate([du_c, du_l], axis=0), grads


def local_step(x, ctx, target, mod, w, late=None, reducer=None):
    L, Lc = x.shape[0], ctx.shape[0]
    T = L + Lc
    assert L % Lc == 0 and Lc % (2 * N_SEG) == 0 and L % GRID_W == 0
    D = D_MODEL
    X0 = jnp.concatenate([ctx, x], axis=0)

    def mod_of(i, j):
        return mod[i, :, j, :][:, None, :]

    def vec(v):
        return v.reshape(1, 1, -1).astype(F32)

    g0 = vec(w["norm_g"][0])
    (H0,) = rowwise_fwd(f_norm_mod, [X0], [g0, mod_of(0, 1), mod_of(0, 0)], [D], [BF16], T, Lc, "l0_norm")
    p0 = mm_nn(H0, w["mla_w_in"], name="l0_in")
    cq = Rows(p0, Q_LORA_RANK, col_blk=D // Q_LORA_RANK)
    ckv = Rows(p0, KV_LORA_RANK, col_blk=(D + Q_LORA_RANK) // KV_LORA_RANK)
    kr = p0[:, D + Q_LORA_RANK + KV_LORA_RANK:D + P0_HEAD]
    qng, kvng = vec(w["mla_q_norm"]), vec(w["mla_kv_norm"])
    (qn,) = rowwise_fwd(f_rms, [cq], [qng], [Q_LORA_RANK], [BF16], T, 0, "l0_qnorm")
    (kvn,) = rowwise_fwd(f_rms, [ckv], [kvng], [KV_LORA_RANK], [BF16], T, 0, "l0_kvnorm")
    tabs = rope_tables(Lc, L)
    spread = rotary_spread()
    w_uq_p = pad_heads(w["mla_w_uq"], QK_DIM)
    w_ukv3 = w["mla_w_ukv"].reshape(KV_LORA_RANK, MLA_HEADS, QK_NOPE_DIM + V_HEAD_DIM)
    w_kn_p = pad_heads(w_ukv3[:, :, :QK_NOPE_DIM].reshape(KV_LORA_RANK, -1), QK_NOPE_DIM)
    w_v = w_ukv3[:, :, QK_NOPE_DIM:].reshape(KV_LORA_RANK, -1)
    qb = q_heads(qn, w_uq_p, tabs)
    kb, vb = kv_heads(kvn, w_kn_p, w_v, kr, spread, tabs)
    o, lse, qs = attn_fwd(qb, kb, vb, Lc)
    X1, og, out0 = mla_post_fwd(o, p0, X0, mod_of(0, 2), w["mla_w_out"], Lc)

    if late is not None:
        w = {**w, **late(X1)}
    X1p = to_segments(X1, Lc)
    tgt_p = to_segments(target, 0)
    g1 = vec(w["norm_g"][1])
    (H1,) = rowwise_fwd(f_norm_mod, [X1p], [g1, mod_of(1, 1), mod_of(1, 0)], [D], [BF16], T, Lc, "l1_norm")
    p1 = mm_nn(H1, w["s5_w_in"], name="l1_in")
    disc_fn = lambda *a: tuple(zip(*[s5_discretise(a[0][k], a[1][k], a[2][k], a[3][k], a[4][k]) for k in range(2)]))
    disc, disc_vjp = jax.vjp(disc_fn, w["s5_a_re"], w["s5_a_im"], w["s5_log_step"], w["s5_b_re"], w["s5_b_im"])
    dirs = [s5_block_weights(disc[0][k], disc[1][k], disc[2][k], disc[3][k], w["s5_c_re"][k], w["s5_c_im"][k])
            for k in range(2)]
    y_ssm, s5_saved = s5_forward(p1, Lc, dirs)

    row = lambda v: v.reshape(1, D).astype(F32)
    (lvec, dX2, d_yssm, d_u_act, d_z1, d_fg, d_gt1, d_bg, d_d, gw_glu, gw_out) = s5_tail(
        y_ssm, p1, X1p, tgt_p, Lc, row(w["s5_d"]), row(w["s5_b_glu"]), mod[1, 1:2, 2, :], row(w["final_g"]),
        w["s5_w_glu"], w["s5_w_out"])
    loss = jnp.sum(lvec)
    gw = {"final_g": d_fg.reshape(D), "s5_b_glu": d_bg.reshape(D), "s5_d": d_d.reshape(D),
          "s5_w_glu": gw_glu, "s5_w_out": gw_out}
    dmod = {}

    du_p, s5_g = s5_backward(d_yssm, d_u_act, p1, Lc, dirs, s5_saved)
    d_disc = tuple(tuple(s5_g[k][j - 1].reshape(disc[j][k].shape) if j >= 2 else
                         s5_g[k][0][j].reshape(disc[j][k].shape) for k in range(2)) for j in range(4))
    gw["s5_a_re"], gw["s5_a_im"], gw["s5_log_step"], gw["s5_b_re"], gw["s5_b_im"] = disc_vjp(d_disc)
    gw["s5_c_re"] = jnp.stack([s5_g[0][3], s5_g[1][3]])
    gw["s5_c_im"] = jnp.stack([s5_g[0][4], s5_g[1][4]])
    d_H1 = mm_nt_sum([(du_p, 0, 0), (d_z1, D, Lc)], w["s5_w_in"], T, "l1_in_dx")
    gw["s5_w_in"] = jnp.concatenate([mm_tn(H1, du_p, name="l1_in_dw_u"), mm_tn(H1[Lc:], d_z1, name="l1_in_dw_z")],
                                    axis=1)
    if reducer is not None:
        g1 = g1 + reducer[0]({n: gw.pop(n) for n in LAYER1_MATS})[0, 0]
    d_X1p, d_g1, d_sc1, d_sh1 = rowwise_bwd(f_norm_mod, [X1p], [g1, mod_of(1, 1), mod_of(1, 0)], [d_H1],
                                            [0], [0, 1, 2], T, Lc, "l1_norm_bwd", lat_add=dX2)
    d_gt1_full = jnp.concatenate([jnp.zeros((1, 1, D), F32), d_gt1[None]], axis=0)
    dmod[1] = (d_sh1, d_sc1, d_gt1_full)
    d_X1 = from_segments(d_X1p, Lc)

    d_o, d_z0, d_gt0, gw["mla_w_out"] = mla_post_bwd(d_X1, out0, og, o, p0, mod_of(0, 2), w["mla_w_out"], Lc)
    if reducer is not None:
        tabs = (tabs[0] + reducer[1](d_o)[0, 0],) + tabs[1:]
    d_q, dos = attn_bwd_dq(qb, kb, vb, o, d_o, lse, tabs, Lc)
    dk_p, d_v = attn_bwd_dkv(qs, kb, vb, dos, Lc)
    d_k, d_kr = heads_unrope(dk_p, tabs, jnp.pad(spread, ((0, HEAD_LANES - QK_ROPE_DIM), (0, 0))),
                             name="l0_k_unrope")
    d_qn = mm_nt(d_q, w_uq_p, name="l0_uq_dx")
    gw["mla_w_uq"] = unpad_heads(mm_tn(qn, d_q, name="l0_uq_dw"), QK_DIM)
    d_kvn = mm_nt(d_k, w_kn_p, name="l0_ukn_dx") + mm_nt(d_v, w_v, name="l0_uv_dx")
    dw_kn = unpad_heads(mm_tn(kvn, d_k, name="l0_ukn_dw"), QK_NOPE_DIM).reshape(KV_LORA_RANK, MLA_HEADS, QK_NOPE_DIM)
    dw_v = mm_tn(kvn, d_v, name="l0_uv_dw").reshape(KV_LORA_RANK, MLA_HEADS, V_HEAD_DIM)
    gw["mla_w_ukv"] = jnp.concatenate([dw_kn, dw_v], axis=-1).reshape(KV_LORA_RANK, -1)
    d_cq, d_qng = rowwise_bwd(f_rms, [cq], [qng], [d_qn], [0], [0], T, 0, "l0_qnorm_bwd")
    d_ckv, d_kvng = rowwise_bwd(f_rms, [ckv], [kvng], [d_kvn], [0], [0], T, 0, "l0_kvnorm_bwd")
    gw["mla_q_norm"] = d_qng.reshape(-1)
    gw["mla_kv_norm"] = d_kvng.reshape(-1)
    o_cq, o_ckv = D, D + Q_LORA_RANK
    o_kr = o_ckv + KV_LORA_RANK
    d_H0 = mm_nt_sum([(d_z0, 0, 0), (d_cq, o_cq, 0), (d_ckv, o_ckv, 0), (d_kr, o_kr, 0)], w["mla_w_in"], T, "l0_in_dx")
    d_head = jnp.concatenate([d_cq, d_ckv, d_kr], axis=1)
    gw["mla_w_in"] = jnp.concatenate([mm_tn(H0, d_head, name="l0_in_dw_head")[:, :P0_HEAD],
                                      mm_tn(H0, d_z0, name="l0_in_dw_z")], axis=1)
    d_X0, d_g0, d_sc0, d_sh0 = rowwise_bwd(f_norm_mod, [X0], [g0, mod_of(0, 1), mod_of(0, 0)], [d_H0],
                                           [0], [0, 1, 2], T, Lc, "l0_norm_bwd", lat_add=d_X1)
    dmod[0] = (d_sh0, d_sc0, d_gt0)
    gw["norm_g"] = jnp.stack([d_g0.reshape(D), d_g1.reshape(D)])
    dx = d_X0[Lc:]
    dmod_arr = jnp.stack([jnp.stack([dmod[i][j][:, 0, :] for j in range(3)], axis=1) for i in range(2)])
    ready = reducer[2](d_X0) if reducer is not None else {}
    return loss, dx, dmod_arr, gw, ready


SHARDED = {
    "mla_w_in": 1, "mla_w_uq": 1, "mla_w_ukv": 1, "mla_w_out": 0,
    "s5_w_in": 1, "s5_w_glu": 0, "s5_w_out": 0, "s5_d": 0, "s5_b_glu": 0,
}
SHARDED_MATS = ["mla_w_in", "mla_w_uq", "mla_w_ukv", "mla_w_out", "s5_w_in", "s5_w_glu", "s5_w_out"]
SHARDED_VECS = ["s5_d", "s5_b_glu"]
REPLICATED = ["norm_g", "mla_q_norm", "mla_kv_norm", "s5_a_re", "s5_a_im", "s5_log_step", "s5_b_re", "s5_b_im",
              "s5_c_re", "s5_c_im", "final_g"]
WEIGHT_ORDER = ["c_ctx", "ada_w", "ada_b", "norm_g", "mla_w_in", "mla_q_norm", "mla_w_uq", "mla_kv_norm", "mla_w_ukv",
                "mla_w_out", "s5_w_in", "s5_a_re", "s5_a_im", "s5_log_step", "s5_b_re", "s5_b_im", "s5_c_re", "s5_c_im",
                "s5_d", "s5_w_glu", "s5_b_glu", "s5_w_out", "final_g"]


P0_HEAD = Q_LORA_RANK + KV_LORA_RANK + QK_ROPE_DIM


P0_WIDTH = 1536


def w_in_to_kernel_order(w):
    pad = jnp.zeros((w.shape[0], P0_WIDTH - w.shape[1]), w.dtype)
    return jnp.concatenate([w[:, P0_HEAD:], w[:, :P0_HEAD], pad], axis=1)


LAYER0_MATS = ["mla_w_in", "mla_w_uq", "mla_w_ukv", "mla_w_out"]
LAYER1_MATS = ["s5_w_in", "s5_w_glu", "s5_w_out"]


def _whole_matrices(names, own_blocks, gathered):
    chip = _chip_index(_coords())
    full = {}
    for n, own, o in zip(names, own_blocks, gathered):
        slot = lax.broadcasted_iota(jnp.int32, (N_CHIP, 1, 1), 0)
        o = jnp.where(slot == chip, own[None], o.reshape((N_CHIP,) + own.shape))
        full[n] = o.reshape(-1, o.shape[-1]) if SHARDED[n] == 0 else o.transpose(1, 0, 2).reshape(o.shape[1], -1)
    return full


def gather_weights(ws):
    mats = [ws[n].astype(BF16) for n in LAYER0_MATS]
    full = _whole_matrices(LAYER0_MATS, mats, gather_halves(mats, "gather_weights"))
    full["mla_w_in"] = w_in_to_kernel_order(full["mla_w_in"])
    return full


def gather_weights_behind(ws, after):
    mats = [ws[n].astype(BF16) for n in LAYER1_MATS]
    flight, token = exchange_start(
        mats, [jax.ShapeDtypeStruct((N_CHIP,) + m.shape, m.dtype) for m in mats],
        [(f, a, lambda me, peer: None, a, lambda s: (_chip_index(s),)) for a in range(len(mats)) for f in CHIP_FLIPS],
        "gather_l1_start", after=after)

    def finish(after):
        own, got = exchange_wait(flight, after, "gather_l1_wait")
        return _whole_matrices(LAYER1_MATS, own, got)

    return token[0, 0], finish


def _grad_slots(gw, names):
    slots = []
    for n in names:
        g = gw[n]
        if SHARDED[n] == 0:
            slots.append(g.reshape(N_CHIP, 2, g.shape[0] // (2 * N_CHIP), g.shape[1]))
        else:
            k, n4 = g.shape
            slots.append(g.reshape(k, N_CHIP, n4 // N_CHIP).transpose(1, 0, 2)
                         .reshape(N_CHIP, 2, k // 2, n4 // N_CHIP))
    return slots


def _to_sibling_half(count):
    return [(CORE_FLIP, i, lambda me, peer: (slice(None), 1 - me[2]), i, lambda s: None) for i in range(count)]


def _to_chips(count):
    return [(f, i, lambda me, peer: (_chip_index(peer),), i, lambda s: (_chip_index(s),))
            for i in range(count) for f in CHIP_FLIPS]


def _place():
    me = _coords()
    return jnp.stack([me[2], _chip_index(me)]).astype(jnp.int32)


def reduce_behind(names):
    state = {}
    count = len(names)

    def begin(gw):
        slots = _grad_slots(gw, names)
        lands = [jax.ShapeDtypeStruct((N_CHIP,) + s.shape[2:], F32) for s in slots]
        state["in"], token = exchange_start(slots, lands, _to_sibling_half(count), "grads_l1_swap_in_start")
        return token

    def middle(after):
        slots, got = exchange_wait(state["in"], after, "grads_l1_swap_in_wait")
        place = _place()
        sums = [pair_add(s, g, place[:1], BF16, f"grads_pair_{n}") for n, s, g in zip(names, slots, got)]
        lands = [jax.ShapeDtypeStruct(s.shape, s.dtype) for s in sums]
        state["out"], token = exchange_start(sums, lands, _to_chips(count), "grads_l1_scatter_start")
        return token

    def end(after):
        sums, parts = exchange_wait(state["out"], after, "grads_l1_scatter_wait")
        place = _place()
        return {n: sum_chips(p, s, place, f"grads_sum_{n}") for n, p, s in zip(names, parts, sums)}

    return begin, middle, end


def reduce_gradients(gw, ready_halves):
    me = _coords()
    place = _place()
    mat_names = [n for n in SHARDED_MATS if n not in ready_halves]
    slots = dict(zip(mat_names, _grad_slots(gw, mat_names)))
    small_names = REPLICATED + SHARDED_VECS
    small, small_offs = pack_flat([gw[n].astype(F32) for n in small_names], F32)
    small = jnp.pad(small, ((0, (-small.shape[0]) % (N_CHIP * 32)), (0, 0)))
    slots["small"] = small.reshape(N_CHIP, 2, -1, 128)
    names = list(slots)
    count = len(names)
    got = exchange([slots[n] for n in names],
                   [jax.ShapeDtypeStruct((N_CHIP,) + slots[n].shape[2:], F32) for n in names],
                   _to_sibling_half(count), [], "grads_swap_in")
    sums = [pair_add(slots[n], g, place[:1], F32 if n == "small" else BF16, f"grads_pair_{n}")
            for n, g in zip(names, got)]
    parts = exchange(sums, [jax.ShapeDtypeStruct(s.shape, s.dtype) for s in sums], _to_chips(count), [],
                     "grads_scatter")
    halves = {n: sum_chips(p, s, place, f"grads_sum_{n}") for n, p, s in zip(names, parts, sums)}
    halves.update(ready_halves)
    all_names = list(halves)
    fulls = exchange(
        [halves[n] for n in all_names], [jax.ShapeDtypeStruct(halves[n].shape, F32) for n in all_names],
        [(CORE_FLIP, i, lambda me, peer: (me[2],), i, lambda s: (s[2],)) for i in range(len(all_names))], [],
        "grads_swap_out", aliases={i: i for i in range(len(all_names))})
    out = {n: f.reshape(-1, f.shape[-1]) for n, f in zip(all_names, fulls)}
    quarter = out.pop("small")
    gather, token = exchange_start(
        [quarter], [jax.ShapeDtypeStruct((N_CHIP,) + quarter.shape, F32)],
        [(f, 0, lambda me, peer: None, 0, lambda s: (_chip_index(s),)) for f in CHIP_FLIPS],
        "grads_gather_small_start")

    def finish_small(after):
        (own,), (got_small,) = exchange_wait(gather, after, "grads_gather_small_wait")
        slot = lax.broadcasted_iota(jnp.int32, (N_CHIP, 1, 1), 0)
        small_all = jnp.where(slot == _chip_index(me), own[None], got_small)
        vals = unpack_flat(small_all, small_offs, [gw[n].shape for n in small_names])
        res = {}
        for n, v in zip(small_names, vals):
            if n in SHARDED_VECS:
                size = v.shape[0] // N_CHIP
                v = lax.dynamic_slice_in_dim(v, _chip_index(me) * size, size)
            res[n] = v
        return res

    return out, finish_small, token


def kernel(x, c, ctx, c_ctx, ada_w, ada_b, norm_g, mla_w_in, mla_q_norm, mla_w_uq, mla_kv_norm, mla_w_ukv, mla_w_out, s5_w_in, s5_a_re, s5_a_im, s5_log_step, s5_b_re, s5_b_im, s5_c_re, s5_c_im, s5_d, s5_w_glu, s5_b_glu, s5_w_out, final_g, loss_target, m_c_ctx, m_ada_w, m_ada_b, m_norm_g, m_mla_w_in, m_mla_q_norm, m_mla_w_uq, m_mla_kv_norm, m_mla_w_ukv, m_mla_w_out, m_s5_w_in, m_s5_a_re, m_s5_a_im, m_s5_log_step, m_s5_b_re, m_s5_b_im, m_s5_c_re, m_s5_c_im, m_s5_d, m_s5_w_glu, m_s5_b_glu, m_s5_w_out, m_final_g, v_c_ctx, v_ada_w, v_ada_b, v_norm_g, v_mla_w_in, v_mla_q_norm, v_mla_w_uq, v_mla_kv_norm, v_mla_w_ukv, v_mla_w_out, v_s5_w_in, v_s5_a_re, v_s5_a_im, v_s5_log_step, v_s5_b_re, v_s5_b_im, v_s5_c_re, v_s5_c_im, v_s5_d, v_s5_w_glu, v_s5_b_glu, v_s5_w_out, v_final_g):
    args = dict(locals())
    weights = {n: args[n] for n in WEIGHT_ORDER}
    D = D_MODEL
    xi, yi, ci = _coords()
    chip = 2 * xi + yi
    me = 4 * xi + 2 * yi + ci
    n_col = ada_w.shape[2]

    c_all = allgather_devices(jnp.pad(c, ((0, 7), (0, 0))), "gather_c")[:, 0, :]
    cond = jnp.concatenate([c_all, jnp.broadcast_to(c_ctx[None], (8, D))], axis=0)
    (s_cond,) = rowwise_fwd(lambda v: (_silu(v),), [cond], [], [D], [F32], 16, 0, "cond_silu")
    ada_rows = ada_w.reshape(2 * D, n_col)
    mod_cols = jnp.stack([mm_nn(s_cond, ada_rows, name=f"mod_proj{i}", b_blk=i) for i in range(2)])
    vec_tiles = [jnp.pad(weights[n][0].reshape(-1, 128), ((0, 6), (0, 0))) for n in SHARDED_VECS]
    mod_all, *vec_all = allgather_chips([mod_cols] + vec_tiles, "gather_mod")
    mod_all = mod_all.transpose(1, 2, 0, 3).reshape(2, 16, 3 * D) + ada_b[:, None, :]
    mod_l = lax.dynamic_index_in_dim(mod_all, me, axis=1, keepdims=False)
    mod_c = mod_all[:, 8, :]
    mod = jnp.stack([mod_c.reshape(2, 3, D), mod_l.reshape(2, 3, D)], axis=1)

    w = gather_weights({n: weights[n][0] for n in LAYER0_MATS})
    token, late = gather_weights_behind({n: weights[n][0] for n in LAYER1_MATS}, w["mla_w_out"])
    for n, v in zip(SHARDED_VECS, vec_all):
        w[n] = v[:, :2, :].reshape(-1)
    for n in ["norm_g", "final_g"]:
        w[n] = weights[n]
    for n in ["mla_q_norm", "mla_kv_norm", "s5_a_re", "s5_a_im", "s5_log_step", "s5_b_re", "s5_b_im",
              "s5_c_re", "s5_c_im"]:
        w[n] = weights[n][0]

    loss_me, dx, dmod, gw, ready = local_step(x[0] + token, ctx[0], loss_target[0], mod, w, late,
                                              reduce_behind(LAYER1_MATS))

    dmod_rows, loss_all = _gather([dmod.reshape(2, 2, 3 * D), jnp.broadcast_to(loss_me, (8, 128))],
                                  ALL_FLIPS, _dev_index, N_DEV, "gather_dmod")
    loss = functools.reduce(lambda s, d: s + loss_all[d, 0, 0], range(1, N_DEV), loss_all[0, 0, 0])
    dm = jnp.concatenate([dmod_rows[:, :, 1, :], dmod_rows[:, :, 0, :]], axis=0).transpose(1, 0, 2)
    g_ada_b = jnp.sum(dm, axis=1)
    dm_cols = lax.dynamic_slice_in_dim(dm, chip * n_col, n_col, axis=2)
    g_ada_w = jnp.stack([mm_tn(s_cond, dm_cols[i], name=f"mod_proj_dw{i}") for i in range(2)])
    dmc = jnp.sum(dm_cols[:, 8:, :], axis=1)
    dmc8 = jnp.broadcast_to(dmc[:, None, :], (2, 8, n_col))
    g_sc = (mm_nt(dmc8[0], ada_rows, name="mod_proj_dx0", b_rows=D, b_blk=0)[0]
            + mm_nt(dmc8[1], ada_rows, name="mod_proj_dx1", b_rows=D, b_blk=1)[0])
    g_sc_all = allgather_devices(jnp.broadcast_to(g_sc[None], (8, D)), "gather_dcond")[:, 0, :]
    g_silu_cc = g_sc_all[0] + g_sc_all[2] + g_sc_all[4] + g_sc_all[6]
    (g_c_ctx,) = rowwise_bwd(lambda v: (_silu(v),), [jnp.broadcast_to(c_ctx[None], (8, D))], [],
                             [jnp.broadcast_to(g_silu_cc[None], (8, D))], [0], [], 8, 0, "cond_silu_bwd")
    g_c_ctx = g_c_ctx[0]

    grads = {"c_ctx": g_c_ctx, "ada_w": g_ada_w, "ada_b": g_ada_b}
    deltas, new_m, new_v = {}, {}, {}
    small = [n for n in WEIGHT_ORDER if weights[n].size < 50000]

    def update(n, after=None):
        shp = weights[n].shape
        w2 = weights[n].reshape(-1, shp[-1])
        d_, m_, v_ = adamw(w2, grads[n].reshape(w2.shape), args["m_" + n].reshape(w2.shape),
                           args["v_" + n].reshape(w2.shape), name=f"adamw_{n}", after=after)
        deltas[n], new_m[n], new_v[n] = d_.reshape(shp), m_.reshape(shp), v_.reshape(shp)

    red, finish_small, small_started = reduce_gradients(gw, ready)
    update("ada_w", small_started)
    for n in SHARDED_MATS:
        grads[n] = red[n].reshape(weights[n].shape)
        update(n, small_started)
    red_small = finish_small(jnp.stack([deltas[n][(0,) * deltas[n].ndim]
                                        for n in ["ada_w"] + SHARDED_MATS]))
    for n in REPLICATED + SHARDED_VECS:
        grads[n] = red_small[n].reshape(weights[n].shape)
    for n in WEIGHT_ORDER:
        if n not in small and n not in deltas:
            update(n)
    packs = []
    offs = None
    for src in (weights, grads, {n: args["m_" + n] for n in small}, {n: args["v_" + n] for n in small}):
        buf, offs = pack_flat([src[n] for n in small], F32)
        packs.append(buf)
    outs = adamw(*packs, name="adamw_small")
    for res, dst in zip(outs, (deltas, new_m, new_v)):
        for n, val in zip(small, unpack_flat(res, offs, [weights[n].shape for n in small])):
            dst[n] = val

    return (loss, dx[None], *[grads[n] for n in WEIGHT_ORDER], *[deltas[n] for n in WEIGHT_ORDER],
            *[new_m[n] for n in WEIGHT_ORDER], *[new_v[n] for n in WEIGHT_ORDER])
```

```python
import functools
import math

import jax
import jax.numpy as jnp
import numpy as np
from jax import lax
from jax.experimental import pallas as pl
from jax.experimental.pallas import tpu as pltpu

F32 = jnp.float32
BF16 = jnp.bfloat16

D_MODEL = 1024
GRID_W = 64
EPS = 1e-6
MLA_HEADS = 16
QK_NOPE_DIM = 64
QK_ROPE_DIM = 32
V_HEAD_DIM = 64
Q_LORA_RANK = 256
KV_LORA_RANK = 128
QK_DIM = QK_NOPE_DIM + QK_ROPE_DIM
SOFTMAX_SCALE = QK_DIM ** -0.5
ROPE_THETA = 10000.0
S5_GROUP = 16
S5_GROUPS = D_MODEL // S5_GROUP
S5_STATE = 64
S5_LANES = S5_GROUPS * S5_STATE
N_SEG = 8
GROUPS_PER_BLOCK = 8
N_BLOCKS = S5_GROUPS // GROUPS_PER_BLOCK
BLK_CH = GROUPS_PER_BLOCK * S5_GROUP
BLK_ST = GROUPS_PER_BLOCK * S5_STATE

ADAM_LR = 0.001
ADAM_B1 = 0.9
ADAM_B2 = 0.999
ADAM_EPS = 1e-08
ADAM_WD = 0.01
ADAM_STEP = 10

N_DEV = 8
N_CHIP = 4
MESH = pl.DeviceIdType.MESH
VMEM_LIMIT = 52 * 1024 * 1024
ROW_TILE = 256


def _params(sem=None, vmem=None):
    return pltpu.CompilerParams(dimension_semantics=sem, vmem_limit_bytes=vmem)


def mm_nn(a, b, out_dtype=F32, name="mm_nn", b_blk=0):
    M, K = a.shape
    N = b.shape[1]
    tm = math.gcd(ROW_TILE, M)

    def body(a_ref, b_ref, o_ref):
        o_ref[...] = jnp.dot(a_ref[...].astype(BF16), b_ref[...].astype(BF16),
                             preferred_element_type=F32).astype(o_ref.dtype)

    return pl.pallas_call(
        body, out_shape=jax.ShapeDtypeStruct((M, N), out_dtype), grid=(M // tm,),
        in_specs=[pl.BlockSpec((tm, K), lambda i: (i, 0)), pl.BlockSpec((K, N), lambda i: (b_blk, 0))],
        out_specs=pl.BlockSpec((tm, N), lambda i: (i, 0)),
        compiler_params=_params(("parallel",), VMEM_LIMIT), name=name)(a, b)


def mm_nt(a, b, out_dtype=F32, name="mm_nt", b_rows=None, b_blk=0):
    M, N = a.shape
    K = b.shape[0] if b_rows is None else b_rows
    tm = math.gcd(ROW_TILE, M)

    def body(a_ref, b_ref, o_ref):
        o_ref[...] = lax.dot_general(a_ref[...].astype(BF16), b_ref[...].astype(BF16),
                                     (((1,), (1,)), ((), ())),
                                     preferred_element_type=F32).astype(o_ref.dtype)

    return pl.pallas_call(
        body, out_shape=jax.ShapeDtypeStruct((M, K), out_dtype), grid=(M // tm,),
        in_specs=[pl.BlockSpec((tm, N), lambda i: (i, 0)), pl.BlockSpec((K, N), lambda i: (b_blk, 0))],
        out_specs=pl.BlockSpec((tm, K), lambda i: (i, 0)),
        compiler_params=_params(("parallel",), VMEM_LIMIT), name=name)(a, b)


def mm_tn(a, b, name="mm_tn"):
    M, K = a.shape
    N = b.shape[1]
    tn = math.gcd(512, N) if N % 128 == 0 and N > 512 else N

    def body(a_ref, b_ref, o_ref):
        o_ref[...] = lax.dot_general(a_ref[...].astype(BF16), b_ref[...].astype(BF16),
                                     (((0,), (0,)), ((), ())), preferred_element_type=F32)

    return pl.pallas_call(
        body, out_shape=jax.ShapeDtypeStruct((K, N), F32), grid=(N // tn,),
        in_specs=[pl.BlockSpec((M, K), lambda j: (0, 0)), pl.BlockSpec((M, tn), lambda j: (0, j))],
        out_specs=pl.BlockSpec((K, tn), lambda j: (0, j)),
        compiler_params=_params(("parallel",), VMEM_LIMIT), name=name)(a, b)


class Rows:
    def __init__(self, arr, width=None, row_off=0, col_blk=0):
        self.arr = arr
        self.width = arr.shape[1] if width is None else width
        self.row_off = row_off
        self.col_blk = col_blk

    def spec(self, tm):
        ro, cb = self.row_off // tm, self.col_blk
        return pl.BlockSpec((tm, self.width), lambda i: (i + ro, cb))


def _as_rows(x):
    return x if isinstance(x, Rows) else Rows(x)


def _row_tile(n_rows, n_ctx_rows, rows):
    tm = math.gcd(ROW_TILE, n_rows, n_ctx_rows)
    for r in rows:
        tm = math.gcd(tm, r.row_off)
    return tm


def _bc_spec(arr, n_ctx_blocks):
    g, _, d = arr.shape
    if g == 1:
        return pl.BlockSpec((1, 1, d), lambda i: (0, 0, 0))
    return pl.BlockSpec((1, 1, d), lambda i: ((i >= n_ctx_blocks).astype(jnp.int32), 0, 0))


def rowwise_fwd(fn, rows, bcs, out_dims, out_dtypes, n_rows, n_ctx_rows, name):
    rows = [_as_rows(r) for r in rows]
    tm = _row_tile(n_rows, n_ctx_rows, rows)
    ncb = n_ctx_rows // tm
    nr, nb = len(rows), len(bcs)

    def body(*refs):
        vals = [r[...].astype(F32) for r in refs[:nr]] + [b[0].astype(F32) for b in refs[nr:nr + nb]]
        outs = fn(*vals)
        for o_ref, v in zip(refs[nr + nb:], outs):
            o_ref[...] = v.astype(o_ref.dtype)

    outs = pl.pallas_call(
        body,
        out_shape=[jax.ShapeDtypeStruct((n_rows, d), dt) for d, dt in zip(out_dims, out_dtypes)],
        grid=(n_rows // tm,),
        in_specs=[r.spec(tm) for r in rows] + [_bc_spec(b, ncb) for b in bcs],
        out_specs=[pl.BlockSpec((tm, d), lambda i: (i, 0)) for d in out_dims],
        compiler_params=_params(("parallel",), VMEM_LIMIT), name=name)(*[r.arr for r in rows], *bcs)
    return outs


def rowwise_bwd(fn, rows, bcs, cts, diff_rows, diff_bcs, n_rows, n_ctx_rows, name, ct_extra=None, lat_add=None):
    rows = [_as_rows(r) for r in rows]
    cts = [_as_rows(c) for c in cts]
    extra = [_as_rows(ct_extra)] if ct_extra is not None else []
    tm = _row_tile(n_rows, n_ctx_rows, rows + cts + extra)
    ncb = n_ctx_rows // tm
    nr, nb, nc = len(rows), len(bcs), len(cts)
    ndr, ndb = len(diff_rows), len(diff_bcs)
    n_in = nr + nb + nc + len(extra) + (lat_add is not None)

    def body(*refs):
        i = pl.program_id(0)
        rvals = [r[...].astype(F32) for r in refs[:nr]]
        bvals = [b[0].astype(F32) for b in refs[nr:nr + nb]]
        cvals = [c[...].astype(F32) for c in refs[nr + nb:nr + nb + nc]]
        if extra:
            cvals[0] = cvals[0] + refs[nr + nb + nc][...].astype(F32)
        outs = refs[n_in:]

        def f(*d):
            rv, bv = list(rvals), list(bvals)
            for k, idx in enumerate(diff_rows):
                rv[idx] = d[k]
            for k, idx in enumerate(diff_bcs):
                bv[idx] = d[ndr + k]
            return tuple(fn(*rv, *bv))

        primals = [rvals[k] for k in diff_rows] + [bvals[k] for k in diff_bcs]
        _, vjp = jax.vjp(f, *primals)
        grads = list(vjp(tuple(cvals)))
        if lat_add is not None:
            add = refs[n_in - 1][...]
            grads[0] = grads[0] + (add if lat_add.shape[0] == n_rows else jnp.where(i >= ncb, add, 0.0))
        for k in range(ndr):
            outs[k][...] = grads[k].astype(outs[k].dtype)
        for k, idx in enumerate(diff_bcs):
            o_ref = outs[ndr + k]
            first = (i == 0)
            if bcs[idx].shape[0] == 2:
                first = first | (i == ncb)

            @pl.when(first)
            def _(o_ref=o_ref):
                o_ref[...] = jnp.zeros_like(o_ref)

            o_ref[0] += grads[ndr + k]

    out_shape = [jax.ShapeDtypeStruct((n_rows, rows[k].width), F32) for k in diff_rows]
    out_shape += [jax.ShapeDtypeStruct(bcs[k].shape, F32) for k in diff_bcs]
    out_specs = [pl.BlockSpec((tm, rows[k].width), lambda i: (i, 0)) for k in diff_rows]
    out_specs += [_bc_spec(bcs[k], ncb) for k in diff_bcs]
    ins = [r.arr for r in rows] + list(bcs) + [c.arr for c in cts + extra]
    in_specs = [r.spec(tm) for r in rows] + [_bc_spec(b, ncb) for b in bcs] + [c.spec(tm) for c in cts + extra]
    if lat_add is not None:
        ins.append(lat_add)
        skip = ncb if lat_add.shape[0] != n_rows else 0
        in_specs.append(pl.BlockSpec((tm, lat_add.shape[1]), lambda i: (jnp.maximum(i - skip, 0), 0)))
    outs = pl.pallas_call(
        body, out_shape=out_shape, grid=(n_rows // tm,), in_specs=in_specs, out_specs=out_specs,
        compiler_params=_params(("arbitrary",), VMEM_LIMIT), name=name)(*ins)
    return outs


def _rms(x):
    return x * lax.rsqrt(jnp.mean(x * x, axis=-1, keepdims=True) + EPS)


def _sigmoid(x):
    return 0.5 * (jnp.tanh(0.5 * x) + 1.0)


def _silu(x):
    return x * _sigmoid(x)


def _gelu_tanh(x):
    return 0.5 * x * (1.0 + jnp.tanh(math.sqrt(2.0 / math.pi) * (x + 0.044715 * (x * x * x))))


def f_norm_mod(x, g, sc, sh):
    return ((_rms(x) * g) * (1.0 + sc) + sh,)


def f_rms(x, g):
    return (_rms(x) * g,)


def f_gate(o, z):
    return (o * _silu(z),)


def f_s5_act(y, u, d):
    return (_gelu_tanh(y + d * u),)


def f_s5_glu(ya, gl, z, b):
    return (ya * _sigmoid(gl + b) * _silu(z),)


def norm_proj(x, g, sc, sh, w, n_ctx, name):
    n, d = x.shape
    nw = w.shape[1]
    tm = math.gcd(ROW_TILE, n, n_ctx)
    ncb = n_ctx // tm

    def body(x_ref, g_ref, sc_ref, sh_ref, w_ref, h_ref, p_ref):
        h = f_norm_mod(x_ref[...], g_ref[0], sc_ref[0], sh_ref[0])[0].astype(BF16)
        h_ref[...] = h
        p_ref[...] = jnp.dot(h, w_ref[...], preferred_element_type=F32)

    row = pl.BlockSpec((tm, d), lambda i: (i, 0))
    return pl.pallas_call(
        body, out_shape=[jax.ShapeDtypeStruct((n, d), BF16), jax.ShapeDtypeStruct((n, nw), F32)], grid=(n // tm,),
        in_specs=[row, _bc_spec(g, ncb), _bc_spec(sc, ncb), _bc_spec(sh, ncb), pl.BlockSpec(w.shape, lambda i: (0, 0))],
        out_specs=[row, pl.BlockSpec((tm, nw), lambda i: (i, 0))],
        compiler_params=_params(("parallel",), VMEM_LIMIT), name=name)(x, g, sc, sh, w)


def norm_proj_bwd(terms, w, x, g, sc, sh, add, n_ctx, name):
    n, d = x.shape
    tm = math.gcd(ROW_TILE, n, n_ctx, *[t[2] for t in terms])
    ncb = n_ctx // tm
    nt = len(terms)
    add_skip = ncb if add.shape[0] != n else 0

    def body(*refs):
        i = pl.program_id(0)
        a_refs = refs[:nt]
        w_ref, x_ref, g_ref, sc_ref, sh_ref, add_ref = refs[nt:nt + 6]
        dx_ref, dg_ref, dsc_ref, dsh_ref = refs[nt + 6:]
        d_h = None
        for a_ref, (a, off, first) in zip(a_refs, terms):
            part = lax.dot_general(a_ref[...].astype(BF16), w_ref[:, off:off + a.shape[1]], NT_DIMS,
                                   preferred_element_type=F32)
            if first:
                part = jnp.where(i >= first // tm, part, 0.0)
            d_h = part if d_h is None else d_h + part
        _, vjp = jax.vjp(lambda x_, g_, sc_, sh_: f_norm_mod(x_, g_, sc_, sh_), x_ref[...], g_ref[0], sc_ref[0],
                         sh_ref[0])
        d_x, d_g, d_sc, d_sh = vjp((d_h,))
        extra = add_ref[...]
        dx_ref[...] = d_x + (extra if add_skip == 0 else jnp.where(i >= ncb, extra, 0.0))

        @pl.when(i == 0)
        def _():
            dg_ref[...] = jnp.zeros_like(dg_ref)

        @pl.when((i == 0) | (i == ncb))
        def _():
            dsc_ref[...] = jnp.zeros_like(dsc_ref)
            dsh_ref[...] = jnp.zeros_like(dsh_ref)

        dg_ref[0] += d_g
        dsc_ref[0] += d_sc
        dsh_ref[0] += d_sh

    def a_spec(a, first):
        skip = first // tm
        return pl.BlockSpec((tm, a.shape[1]), lambda i: (jnp.maximum(i - skip, 0), 0))

    row = pl.BlockSpec((tm, d), lambda i: (i, 0))
    return pl.pallas_call(
        body,
        out_shape=[jax.ShapeDtypeStruct((n, d), F32), jax.ShapeDtypeStruct(g.shape, F32),
                   jax.ShapeDtypeStruct(sc.shape, F32), jax.ShapeDtypeStruct(sh.shape, F32)],
        grid=(n // tm,),
        in_specs=[a_spec(a, first) for a, _, first in terms]
        + [pl.BlockSpec(w.shape, lambda i: (0, 0)), row, _bc_spec(g, ncb), _bc_spec(sc, ncb), _bc_spec(sh, ncb),
           pl.BlockSpec((tm, d), lambda i: (jnp.maximum(i - add_skip, 0), 0))],
        out_specs=[row, _bc_spec(g, ncb), _bc_spec(sc, ncb), _bc_spec(sh, ncb)],
        compiler_params=_params(("arbitrary",), VMEM_LIMIT), name=name)(*[t[0] for t in terms], w, x, g, sc, sh, add)


def mla_post_fwd(o, p0, x0, gate, w_out, n_ctx, name="l0_post"):
    n, d = o.shape
    tm = math.gcd(ROW_TILE, n, n_ctx)
    ncb = n_ctx // tm

    def body(o_ref, z_ref, x_ref, gt_ref, w_ref, x1_ref, og_ref, out_ref):
        og = f_gate(o_ref[...], z_ref[...])[0].astype(BF16)
        out = jnp.dot(og, w_ref[...], preferred_element_type=F32)
        og_ref[...] = og
        out_ref[...] = out
        x1_ref[...] = x_ref[...] + gt_ref[0] * out

    row = pl.BlockSpec((tm, d), lambda i: (i, 0))
    return pl.pallas_call(
        body, out_shape=[jax.ShapeDtypeStruct((n, d), F32), jax.ShapeDtypeStruct((n, d), BF16),
                         jax.ShapeDtypeStruct((n, d), F32)],
        grid=(n // tm,),
        in_specs=[row, row, row, _bc_spec(gate, ncb), pl.BlockSpec((d, d), lambda i: (0, 0))],
        out_specs=[row, row, row],
        compiler_params=_params(("parallel",), VMEM_LIMIT), name=name)(o, p0, x0, gate, w_out)


def mla_post_bwd(dx1, out, og, o, p0, gate, w_out, n_ctx, name="l0_post_bwd"):
    n, d = o.shape
    tm = math.gcd(ROW_TILE, n, n_ctx)
    ncb = n_ctx // tm

    def body(dx_ref, out_ref, og_ref, o_ref, z_ref, gt_ref, w_ref, do_ref, dz_ref, dgt_ref, dw_ref):
        i = pl.program_id(0)

        @pl.when(i == 0)
        def _():
            dw_ref[...] = jnp.zeros_like(dw_ref)

        @pl.when((i == 0) | (i == ncb))
        def _():
            dgt_ref[...] = jnp.zeros_like(dgt_ref)

        dx = dx_ref[...]
        dgt_ref[0] += jnp.sum(dx * out_ref[...], axis=0, keepdims=True)
        d_out16 = (gt_ref[0] * dx).astype(BF16)
        dw_ref[...] += lax.dot_general(og_ref[...], d_out16, (((0,), (0,)), ((), ())), preferred_element_type=F32)
        d_og = lax.dot_general(d_out16, w_ref[...], NT_DIMS, preferred_element_type=F32)
        _, gate_vjp = jax.vjp(lambda o_, z_: f_gate(o_, z_), o_ref[...], z_ref[...])
        d_o, d_z = gate_vjp((d_og,))
        do_ref[...] = d_o
        dz_ref[...] = d_z

    row = pl.BlockSpec((tm, d), lambda i: (i, 0))
    mat = pl.BlockSpec((d, d), lambda i: (0, 0))
    return pl.pallas_call(
        body, out_shape=[jax.ShapeDtypeStruct((n, d), F32), jax.ShapeDtypeStruct((n, d), F32),
                         jax.ShapeDtypeStruct(gate.shape, F32), jax.ShapeDtypeStruct((d, d), F32)],
        grid=(n // tm,),
        in_specs=[row, row, row, row, row, _bc_spec(gate, ncb), mat],
        out_specs=[row, row, _bc_spec(gate, ncb), mat],
        compiler_params=_params(("arbitrary",), VMEM_LIMIT), name=name)(dx1, out, og, o, p0, gate, w_out)


def s5_tail(y_ssm, p1, x1p, target, n_ctx, d_vec, b_glu, gate, final_g, w_glu, w_out, name="l1_tail"):
    n, d = y_ssm.shape
    tm = math.gcd(ROW_TILE, n, n_ctx)
    off = n_ctx // tm
    tn_dims = (((0,), (0,)), ((), ()))

    def row_loss(x, g, t):
        e = _rms(x) * g - t
        return 0.5 * (e * e) * (1.0 / d)

    def body(y_ref, u_ref, z_ref, x1_ref, t_ref, d_ref, b_ref, gt_ref, fg_ref, wg_ref, wo_ref,
             l_ref, dx_ref, dy_ref, du_ref, dz_ref, dfg_ref, dgt_ref, db_ref, dd_ref, dwg_ref, dwo_ref):
        @pl.when(pl.program_id(0) == 0)
        def _():
            for r in (l_ref, dfg_ref, dgt_ref, db_ref, dd_ref, dwg_ref, dwo_ref):
                r[...] = jnp.zeros_like(r)

        u, z, tgt, gt = u_ref[...], z_ref[...], t_ref[...], gt_ref[...]
        (ya,), act_vjp = jax.vjp(lambda y_, u_, d_: f_s5_act(y_, u_, d_), y_ref[...], u, d_ref[...])
        ya16 = ya.astype(BF16)
        gl = jnp.dot(ya16, wg_ref[...], preferred_element_type=F32)
        (y3,), glu_vjp = jax.vjp(lambda a_, g_, z_, b_: f_s5_glu(a_, g_, z_, b_), ya, gl, z, b_ref[...])
        y3_16 = y3.astype(BF16)
        out1 = jnp.dot(y3_16, wo_ref[...], preferred_element_type=F32)
        lterm, loss_vjp = jax.vjp(lambda x_, g_: row_loss(x_, g_, tgt), x1_ref[...] + gt * out1, fg_ref[...])
        dx2, dfg = loss_vjp(jnp.ones_like(lterm))
        l_ref[...] += jnp.sum(lterm, axis=0, keepdims=True)
        dfg_ref[...] += dfg
        dx_ref[...] = dx2
        dgt_ref[...] += jnp.sum(dx2 * out1, axis=0, keepdims=True)
        d_out16 = (gt * dx2).astype(BF16)
        dwo_ref[...] += lax.dot_general(y3_16, d_out16, tn_dims, preferred_element_type=F32)
        d_y3 = lax.dot_general(d_out16, wo_ref[...], NT_DIMS, preferred_element_type=F32)
        d_ya, d_gl, d_z, d_b = glu_vjp((d_y3,))
        dz_ref[...] = d_z
        db_ref[...] += d_b
        d_gl16 = d_gl.astype(BF16)
        dwg_ref[...] += lax.dot_general(ya16, d_gl16, tn_dims, preferred_element_type=F32)
        d_ya = d_ya + lax.dot_general(d_gl16, wg_ref[...], NT_DIMS, preferred_element_type=F32)
        d_y, d_u, d_d = act_vjp((d_ya,))
        dy_ref[...] = d_y
        du_ref[...] = d_u
        dd_ref[...] += d_d

    row = pl.BlockSpec((tm, d), lambda i: (i, 0))
    vecs = pl.BlockSpec((1, d), lambda i: (0, 0))
    mat = pl.BlockSpec((d, d), lambda i: (0, 0))
    return pl.pallas_call(
        body,
        out_shape=[jax.ShapeDtypeStruct((1, d), F32)] + [jax.ShapeDtypeStruct((n, d), F32)] * 4
        + [jax.ShapeDtypeStruct((1, d), F32)] * 4 + [jax.ShapeDtypeStruct((d, d), F32)] * 2,
        grid=(n // tm,),
        in_specs=[row, pl.BlockSpec((tm, d), lambda i: (i + off, 0)), pl.BlockSpec((tm, d), lambda i: (i + off, 1)),
                  pl.BlockSpec((tm, d), lambda i: (i + off, 0)), row, vecs, vecs, vecs, vecs, mat, mat],
        out_specs=[vecs, row, row, row, row, vecs, vecs, vecs, vecs, mat, mat],
        compiler_params=_params(("arbitrary",), VMEM_LIMIT), name=name)(
            y_ssm, p1, p1, x1p, target, d_vec, b_glu, gate, final_g, w_glu, w_out)


NT_DIMS = (((1,), (1,)), ((), ()))
HEAD_LANES = 128
N_PAIRS = MLA_HEADS // 2


def _own_lanes(shape, hh):
    lane = lax.broadcasted_iota(jnp.int32, shape, len(shape) - 1)
    return (lane < V_HEAD_DIM) if hh == 0 else (lane >= V_HEAD_DIM)


def _delta_lane(hh):
    return V_HEAD_DIM if hh == 0 else 0


def _rope_tiles(x, cos, sin_next, sin_prev, inverse):
    width = x.shape[-1]
    reps = width // HEAD_LANES
    c, sn, sp = (jnp.tile(t, (1, reps)) for t in (cos, sin_next, sin_prev))
    if inverse:
        return x * c + pltpu.roll(x * sn, 8, 1) + pltpu.roll(x * sp, width - 8, 1)
    return x * c + pltpu.roll(x, width - 8, 1) * sn + pltpu.roll(x, 8, 1) * sp


STAT_LANE = QK_DIM


def _with_stat(x16, col, lane0):
    hi = col.astype(BF16)
    r1 = col - hi.astype(F32)
    mid = r1.astype(BF16)
    lo = (r1 - mid.astype(F32)).astype(BF16)
    lane = lax.broadcasted_iota(jnp.int32, x16.shape, 1)
    return jnp.where(lane == lane0, hi, jnp.where(lane == lane0 + 1, mid, jnp.where(lane == lane0 + 2, lo, x16)))


def attn_fwd(qb, kb, vb, n_ctx):
    T = qb.shape[0]
    tq = math.gcd(ROW_TILE, n_ctx)
    nq, ncb = T // tq, n_ctx // tq

    def body(q_ref, k_ref, v_ref, o_ref, lse_ref, qs_ref):
        qi = pl.program_id(1)

        def rows(n_keys):
            v = v_ref[:n_keys, :]
            outs = []
            for hh in range(2):
                hs = slice(hh * HEAD_LANES, (hh + 1) * HEAD_LANES)
                s = lax.dot_general(q_ref[:, hs], k_ref[:n_keys, hs], NT_DIMS,
                                    preferred_element_type=F32) * SOFTMAX_SCALE
                m = jnp.max(s, axis=-1, keepdims=True)
                p = jnp.exp(s - m)
                l = jnp.sum(p, axis=-1, keepdims=True)
                outs.append(jnp.dot(p.astype(BF16), v, preferred_element_type=F32) / l)
                lse = m + jnp.log(l)
                lse_ref[hh] = lse
                qs_ref[:, hs] = _with_stat(q_ref[:, hs], lse * (-1.0 / SOFTMAX_SCALE), STAT_LANE)
            o_ref[...] = jnp.where(_own_lanes(outs[0].shape, 0), outs[0], outs[1])

        pl.when(qi < ncb)(lambda: rows(n_ctx))
        pl.when(qi >= ncb)(lambda: rows(T))

    return pl.pallas_call(
        body,
        out_shape=[jax.ShapeDtypeStruct((T, MLA_HEADS * V_HEAD_DIM), F32),
                   jax.ShapeDtypeStruct((MLA_HEADS, T, 1), F32), jax.ShapeDtypeStruct(qb.shape, BF16)],
        grid=(N_PAIRS, nq),
        in_specs=[pl.BlockSpec((tq, 2 * HEAD_LANES), lambda h, i: (i, h)),
                  pl.BlockSpec((T, 2 * HEAD_LANES), lambda h, i: (0, h)),
                  pl.BlockSpec((T, 2 * V_HEAD_DIM), lambda h, i: (0, h))],
        out_specs=[pl.BlockSpec((tq, 2 * V_HEAD_DIM), lambda h, i: (i, h)),
                   pl.BlockSpec((2, tq, 1), lambda h, i: (h, i, 0)),
                   pl.BlockSpec((tq, 2 * HEAD_LANES), lambda h, i: (i, h))],
        compiler_params=_params(("parallel", "parallel"), VMEM_LIMIT), name="attn_fwd")(qb, kb, vb)


def attn_bwd_dq(qb, kb, vb, o, do, lse, tabs, n_ctx):
    T = qb.shape[0]
    tq = math.gcd(ROW_TILE, n_ctx)
    nq, ncb = T // tq, n_ctx // tq

    def body(q_ref, k_ref, v_ref, o_ref, do_ref, lse_ref, c_ref, sn_ref, sp_ref, dq_ref, dos_ref):
        qi = pl.program_id(1)

        def rows(n_keys):
            v = v_ref[:n_keys, :]
            dqs = []
            for hh in range(2):
                hs = slice(hh * HEAD_LANES, (hh + 1) * HEAD_LANES)
                k = k_ref[:n_keys, hs]
                do = jnp.where(_own_lanes(do_ref.shape, hh), do_ref[...], 0.0)
                delta = jnp.sum(do * o_ref[...], axis=-1, keepdims=True)
                s = lax.dot_general(q_ref[:, hs], k, NT_DIMS, preferred_element_type=F32) * SOFTMAX_SCALE
                p = jnp.exp(s - lse_ref[hh])
                do16 = do.astype(BF16)
                dp = lax.dot_general(do16, v, NT_DIMS, preferred_element_type=F32)
                ds = p * (dp - delta) * SOFTMAX_SCALE
                dqs.append(jnp.dot(ds.astype(BF16), k, preferred_element_type=F32))
                dos_ref[:, hs] = _with_stat(do16, delta, _delta_lane(hh))
            dq = jnp.concatenate(dqs, axis=1)
            dq_ref[...] = _rope_tiles(dq, c_ref[...], sn_ref[...], sp_ref[...], True).astype(BF16)

        pl.when(qi < ncb)(lambda: rows(n_ctx))
        pl.when(qi >= ncb)(lambda: rows(T))

    tab = pl.BlockSpec((tq, HEAD_LANES), lambda h, i: (i, 0))
    return pl.pallas_call(
        body,
        out_shape=[jax.ShapeDtypeStruct((T, MLA_HEADS * HEAD_LANES), BF16)] * 2,
        grid=(N_PAIRS, nq),
        in_specs=[pl.BlockSpec((tq, 2 * HEAD_LANES), lambda h, i: (i, h)),
                  pl.BlockSpec((T, 2 * HEAD_LANES), lambda h, i: (0, h)),
                  pl.BlockSpec((T, 2 * V_HEAD_DIM), lambda h, i: (0, h)),
                  pl.BlockSpec((tq, 2 * V_HEAD_DIM), lambda h, i: (i, h)),
                  pl.BlockSpec((tq, 2 * V_HEAD_DIM), lambda h, i: (i, h)),
                  pl.BlockSpec((2, tq, 1), lambda h, i: (h, i, 0)), tab, tab, tab],
        out_specs=[pl.BlockSpec((tq, 2 * HEAD_LANES), lambda h, i: (i, h))] * 2,
        compiler_params=_params(("parallel", "parallel"), VMEM_LIMIT), name="attn_bwd_dq")(
            qb, kb, vb, o, do, lse, *tabs)


def attn_bwd_dkv(qs, kb, vb, dos, n_ctx):
    T = qs.shape[0]
    tq = math.gcd(ROW_TILE, n_ctx)
    nq, ncb = T // tq, n_ctx // tq

    def body(q_ref, do_ref, k_ref, v_ref, dk_ref, dv_ref):
        kj = pl.program_id(1)

        def cols(first):
            v = v_ref[...]
            lane = lax.broadcasted_iota(jnp.int32, v.shape, 1)
            dvs = []
            for hh in range(2):
                hs = slice(hh * HEAD_LANES, (hh + 1) * HEAD_LANES)
                q = q_ref[first:, hs]
                do16 = do_ref[first:, hs]
                in_delta = (lane >= _delta_lane(hh)) & (lane < _delta_lane(hh) + 3)
                v_minus = jnp.where(in_delta, -jnp.ones_like(v), v)
                pt = jnp.exp(lax.dot_general(k_ref[:, hs], q, NT_DIMS, preferred_element_type=F32) * SOFTMAX_SCALE)
                dvs.append(jnp.dot(pt.astype(BF16), do16, preferred_element_type=F32))
                dst = pt * lax.dot_general(v_minus, do16, NT_DIMS, preferred_element_type=F32) * SOFTMAX_SCALE
                dk_ref[:, hs] = jnp.dot(dst.astype(BF16), q, preferred_element_type=F32)
            dv_ref[...] = jnp.where(_own_lanes(dvs[0].shape, 0), dvs[0], dvs[1])

        pl.when(kj < ncb)(lambda: cols(0))
        pl.when(kj >= ncb)(lambda: cols(n_ctx))

    return pl.pallas_call(
        body,
        out_shape=[jax.ShapeDtypeStruct((T, MLA_HEADS * HEAD_LANES), F32),
                   jax.ShapeDtypeStruct((T, MLA_HEADS * V_HEAD_DIM), F32)],
        grid=(N_PAIRS, nq),
        in_specs=[pl.BlockSpec((T, 2 * HEAD_LANES), lambda h, j: (0, h)),
                  pl.BlockSpec((T, 2 * HEAD_LANES), lambda h, j: (0, h)),
                  pl.BlockSpec((tq, 2 * HEAD_LANES), lambda h, j: (j, h)),
                  pl.BlockSpec((tq, 2 * V_HEAD_DIM), lambda h, j: (j, h))],
        out_specs=[pl.BlockSpec((tq, 2 * HEAD_LANES), lambda h, j: (j, h)),
                   pl.BlockSpec((tq, 2 * V_HEAD_DIM), lambda h, j: (j, h))],
        compiler_params=_params(("parallel", "parallel"), VMEM_LIMIT), name="attn_bwd_dkv")(
            qs, dos, kb, vb)


def _split_bf16(x):
    hi = x.astype(BF16)
    return hi, (x - hi.astype(F32)).astype(BF16)


def q_heads(cq, gain, w_uq_p, tabs, name="l0_uq"):
    T, K = cq.arr.shape[0], cq.width
    N = w_uq_p.shape[1]
    tm = math.gcd(ROW_TILE, T)

    def body(a_ref, g_ref, w_ref, c_ref, sn_ref, sp_ref, o_ref, n_ref):
        qn = f_rms(a_ref[...], g_ref[0])[0].astype(BF16)
        n_ref[...] = qn
        acc = jnp.dot(qn, w_ref[...], preferred_element_type=F32)
        o_ref[...] = _rope_tiles(acc, c_ref[...], sn_ref[...], sp_ref[...], False).astype(BF16)

    tab = pl.BlockSpec((tm, HEAD_LANES), lambda i: (i, 0))
    return pl.pallas_call(
        body, out_shape=[jax.ShapeDtypeStruct((T, N), BF16), jax.ShapeDtypeStruct((T, K), BF16)], grid=(T // tm,),
        in_specs=[cq.spec(tm), pl.BlockSpec((1, 1, K), lambda i: (0, 0, 0)), pl.BlockSpec((K, N), lambda i: (0, 0)),
                  tab, tab, tab],
        out_specs=[pl.BlockSpec((tm, N), lambda i: (i, 0)), pl.BlockSpec((tm, K), lambda i: (i, 0))],
        compiler_params=_params(("parallel",), VMEM_LIMIT), name=name)(cq.arr, gain, w_uq_p, *tabs)


def kv_heads(ckv, gain, w_kn_p, w_v, kr, spread, tabs, name="l0_ukv"):
    T, K = ckv.arr.shape[0], ckv.width
    N = w_kn_p.shape[1]
    NV = w_v.shape[1]
    tm = math.gcd(ROW_TILE, T)

    def body(a_ref, g_ref, wk_ref, wv_ref, kr_ref, e_ref, c_ref, sn_ref, sp_ref, k_ref, v_ref, n_ref):
        a = f_rms(a_ref[...], g_ref[0])[0].astype(BF16)
        n_ref[...] = a
        hi, lo = _split_bf16(kr_ref[...])
        acc = (jnp.dot(a, wk_ref[...], preferred_element_type=F32)
               + jnp.dot(hi, e_ref[...], preferred_element_type=F32)
               + jnp.dot(lo, e_ref[...], preferred_element_type=F32))
        roped = _rope_tiles(acc, c_ref[...], sn_ref[...], sp_ref[...], False)
        lane = lax.broadcasted_iota(jnp.int32, roped.shape, 1) % HEAD_LANES
        k_ref[...] = jnp.where((lane >= STAT_LANE) & (lane < STAT_LANE + 3), 1.0, roped).astype(BF16)
        v_ref[...] = jnp.dot(a, wv_ref[...], preferred_element_type=F32).astype(BF16)

    tab = pl.BlockSpec((tm, HEAD_LANES), lambda i: (i, 0))
    return pl.pallas_call(
        body, out_shape=[jax.ShapeDtypeStruct((T, N), BF16), jax.ShapeDtypeStruct((T, NV), BF16),
                         jax.ShapeDtypeStruct((T, K), BF16)], grid=(T // tm,),
        in_specs=[ckv.spec(tm), pl.BlockSpec((1, 1, K), lambda i: (0, 0, 0)), pl.BlockSpec((K, N), lambda i: (0, 0)),
                  pl.BlockSpec((K, NV), lambda i: (0, 0)), pl.BlockSpec((tm, QK_ROPE_DIM), lambda i: (i, 0)),
                  pl.BlockSpec((QK_ROPE_DIM, N), lambda i: (0, 0)), tab, tab, tab],
        out_specs=[pl.BlockSpec((tm, N), lambda i: (i, 0)), pl.BlockSpec((tm, NV), lambda i: (i, 0)),
                   pl.BlockSpec((tm, K), lambda i: (i, 0))],
        compiler_params=_params(("parallel",), VMEM_LIMIT), name=name)(ckv.arr, gain, w_kn_p, w_v, kr, spread, *tabs)


def heads_unrope(d, tabs, spread=None, name="unrope"):
    T, N = d.shape
    tm = math.gcd(ROW_TILE, T)

    def body(*refs):
        if spread is None:
            d_ref, c_ref, sn_ref, sp_ref, o_ref = refs
        else:
            d_ref, c_ref, sn_ref, sp_ref, e_ref, o_ref, kr_ref = refs
        g = _rope_tiles(d_ref[...], c_ref[...], sn_ref[...], sp_ref[...], True)
        o_ref[...] = g.astype(BF16)
        if spread is not None:
            hi, lo = _split_bf16(g)
            kr_ref[...] = (lax.dot_general(hi, e_ref[...], NT_DIMS, preferred_element_type=F32)
                           + lax.dot_general(lo, e_ref[...], NT_DIMS, preferred_element_type=F32))

    tab = pl.BlockSpec((tm, HEAD_LANES), lambda i: (i, 0))
    row = pl.BlockSpec((tm, N), lambda i: (i, 0))
    ins, in_specs = [d, *tabs], [row, tab, tab, tab]
    out_shape, out_specs = [jax.ShapeDtypeStruct((T, N), BF16)], [row]
    if spread is not None:
        ins.append(spread)
        in_specs.append(pl.BlockSpec(spread.shape, lambda i: (0, 0)))
        out_shape.append(jax.ShapeDtypeStruct((T, spread.shape[0]), F32))
        out_specs.append(pl.BlockSpec((tm, spread.shape[0]), lambda i: (i, 0)))
    return pl.pallas_call(
        body, out_shape=out_shape, grid=(T // tm,), in_specs=in_specs, out_specs=out_specs,
        compiler_params=_params(("parallel",), VMEM_LIMIT), name=name)(*ins)


def _cmul(ar, ai, br, bi):
    return ar * br - ai * bi, ar * bi + ai * br


def s5_chain(finals, s0, a, n_steps, reverse, name):
    W = finals.shape[-1]
    first = N_SEG - 1 if reverse else 0

    def body(f_ref, s0_ref, a_ref, c_ref):
        pr, pi = jnp.ones((1, W), F32), jnp.zeros((1, W), F32)
        br, bi = a_ref[0], a_ref[1]
        n = n_steps
        while n:
            if n & 1:
                pr, pi = _cmul(pr, pi, br, bi)
            br, bi = _cmul(br, bi, br, bi)
            n >>= 1
        fr, fi = f_ref[0], f_ref[1]
        row = lax.broadcasted_iota(jnp.int32, (N_SEG, W), 0)
        s0r = jnp.broadcast_to(s0_ref[0], (N_SEG, W))
        s0i = jnp.broadcast_to(s0_ref[1], (N_SEG, W))
        cr = jnp.where(row == first, s0r, 0.0)
        ci = jnp.where(row == first, s0i, 0.0)
        shift = N_SEG - 1 if reverse else 1
        for _ in range(N_SEG - 1):
            mr, mi = _cmul(pr, pi, cr, ci)
            tr = pltpu.roll(fr + mr, shift, 0)
            ti = pltpu.roll(fi + mi, shift, 0)
            cr = jnp.where(row == first, s0r, tr)
            ci = jnp.where(row == first, s0i, ti)
        c_ref[0] = cr
        c_ref[1] = ci

    return pl.pallas_call(body, out_shape=jax.ShapeDtypeStruct((2, N_SEG, W), F32), name=name)(finals, s0, a)


def _scan_chunk(bur, bui, st_ref, a_ref, n_steps, reverse):
    for lc in range(S5_LANES // BLK_ST):
        sl = slice(lc * BLK_ST, (lc + 1) * BLK_ST)
        lr = jnp.broadcast_to(a_ref[0, :, sl], (N_SEG, BLK_ST))
        li = jnp.broadcast_to(a_ref[1, :, sl], (N_SEG, BLK_ST))

        def step(jj, carry, sl=sl, lr=lr, li=li):
            sr, si = carry
            j = (n_steps - 1 - jj) if reverse else jj
            r0 = pl.multiple_of(j * N_SEG, N_SEG)
            nr = lr * sr - li * si + bur[pl.ds(r0, N_SEG), sl]
            ni = lr * si + li * sr + bui[pl.ds(r0, N_SEG), sl]
            bur[pl.ds(r0, N_SEG), sl] = nr
            bui[pl.ds(r0, N_SEG), sl] = ni
            return nr, ni

        sr, si = lax.fori_loop(0, n_steps, step, (st_ref[0, :, sl], st_ref[1, :, sl]))
        st_ref[0, :, sl] = sr
        st_ref[1, :, sl] = si


def _project_in(x16, w_re, w_im, bur, bui, adjoint):
    for gb in range(N_BLOCKS):
        xb = x16[:, gb * BLK_CH:(gb + 1) * BLK_CH]
        sl = slice(gb * BLK_ST, (gb + 1) * BLK_ST)
        if adjoint:
            dn = (((1,), (1,)), ((), ()))
            bur[:, sl] = lax.dot_general(xb, w_re[gb], dn, preferred_element_type=F32)
            bui[:, sl] = -lax.dot_general(xb, w_im[gb], dn, preferred_element_type=F32)
        else:
            bur[:, sl] = jnp.dot(xb, w_re[gb], preferred_element_type=F32)
            bui[:, sl] = jnp.dot(xb, w_im[gb], preferred_element_type=F32)


def s5_scan(act, w_re, w_im, a, init, *, reverse, adjoint=False, c_re=None, c_im=None, add=None,
            want_ckpt=False, rows=None, name):
    act_off, N = rows if rows is not None else (0, act.shape[0])
    R = math.gcd(ROW_TILE, N, act_off)
    nch, jc = N // R, R // N_SEG
    with_out = c_re is not None

    def chunk(i):
        return (nch - 1 - i) if reverse else i

    def body(*refs):
        act_ref, wre_ref, wim_ref, a_ref, init_ref = refs[:5]
        k = 5
        if with_out:
            cre_ref, cim_ref = refs[k:k + 2]
            k += 2
        if add is not None:
            add_ref = refs[k]
            k += 1
        if with_out:
            out_ref = refs[k]
            k += 1
        if want_ckpt:
            ck_ref = refs[k]
            k += 1
        fin_ref, bur, bui = refs[k:k + 3]

        @pl.when(pl.program_id(0) == 0)
        def _():
            fin_ref[...] = init_ref[...]

        if want_ckpt:
            ck_ref[0] = fin_ref[...]
        _project_in(act_ref[...].astype(BF16), wre_ref, wim_ref, bur, bui, adjoint)
        _scan_chunk(bur, bui, fin_ref, a_ref, jc, reverse)
        if with_out:
            for gb in range(N_BLOCKS):
                sl = slice(gb * BLK_ST, (gb + 1) * BLK_ST)
                y = (jnp.dot(bur[:, sl].astype(BF16), cre_ref[gb], preferred_element_type=F32)
                     - jnp.dot(bui[:, sl].astype(BF16), cim_ref[gb], preferred_element_type=F32))
                cs = slice(gb * BLK_CH, (gb + 1) * BLK_CH)
                if add is not None:
                    y = y + add_ref[:, cs]
                out_ref[:, cs] = y

    row_spec = pl.BlockSpec((R, D_MODEL), lambda i: (chunk(i), 0))
    act_spec = pl.BlockSpec((R, D_MODEL), lambda i: (chunk(i) + act_off // R, 0))
    w_spec = pl.BlockSpec(w_re.shape, lambda i: (0, 0, 0))
    st_spec = pl.BlockSpec((2, N_SEG, S5_LANES), lambda i: (0, 0, 0))
    ins = [act, w_re, w_im, a, init]
    in_specs = [act_spec, w_spec, w_spec, pl.BlockSpec((2, 1, S5_LANES), lambda i: (0, 0, 0)), st_spec]
    if with_out:
        ins += [c_re, c_im]
        in_specs += [pl.BlockSpec(c_re.shape, lambda i: (0, 0, 0))] * 2
    if add is not None:
        ins.append(add)
        in_specs.append(row_spec)
    out_shape, out_specs = [], []
    if with_out:
        out_shape.append(jax.ShapeDtypeStruct((N, D_MODEL), F32))
        out_specs.append(row_spec)
    if want_ckpt:
        out_shape.append(jax.ShapeDtypeStruct((nch, 2, N_SEG, S5_LANES), F32))
        out_specs.append(pl.BlockSpec((1, 2, N_SEG, S5_LANES), lambda i: (chunk(i), 0, 0, 0)))
    out_shape.append(jax.ShapeDtypeStruct((2, N_SEG, S5_LANES), F32))
    out_specs.append(st_spec)
    res = pl.pallas_call(
        body, out_shape=out_shape, grid=(nch,), in_specs=in_specs, out_specs=out_specs,
        scratch_shapes=[pltpu.VMEM((R, S5_LANES), F32), pltpu.VMEM((R, S5_LANES), F32)],
        compiler_params=_params(("arbitrary",), VMEM_LIMIT), name=name)(*ins)
    res = list(res)
    out = res.pop(0) if with_out else None
    ckpt = res.pop(0) if want_ckpt else None
    return out, ckpt, res[0]


def s5_grads(dy, u, ckpt, b_re, b_im, c_re, c_im, lam, init_adj, *, reverse, add=None, u_off=0, name):
    N = dy.shape[0]
    R = math.gcd(ROW_TILE, N, u_off)
    nch, jc = N // R, R // N_SEG
    W = S5_LANES

    def chunk(i):
        return i if reverse else (nch - 1 - i)

    def body(*refs):
        dy_ref, u_ref, ck_ref, bre_ref, bim_ref, cre_ref, cim_ref, lam_ref, init_ref = refs[:9]
        k = 9
        if add is not None:
            add_ref = refs[k]
            k += 1
        du_ref, dlam_ref, dbre_ref, dbim_ref, dcre_ref, dcim_ref, fin_ref = refs[k:k + 7]
        sr_buf, si_buf, er_buf, ei_buf, st_buf = refs[k + 7:k + 12]

        @pl.when(pl.program_id(0) == 0)
        def _():
            fin_ref[...] = init_ref[...]
            dlam_ref[...] = jnp.zeros_like(dlam_ref)
            dbre_ref[...] = jnp.zeros_like(dbre_ref)
            dbim_ref[...] = jnp.zeros_like(dbim_ref)
            dcre_ref[...] = jnp.zeros_like(dcre_ref)
            dcim_ref[...] = jnp.zeros_like(dcim_ref)

        u16 = u_ref[...].astype(BF16)
        dy16 = dy_ref[...].astype(BF16)
        st_buf[...] = ck_ref[0]
        _project_in(u16, bre_ref, bim_ref, sr_buf, si_buf, False)
        _scan_chunk(sr_buf, si_buf, st_buf, lam_ref, jc, reverse)
        _project_in(dy16, cre_ref, cim_ref, er_buf, ei_buf, True)
        for lc in range(W // BLK_ST):
            sl = slice(lc * BLK_ST, (lc + 1) * BLK_ST)
            lr = jnp.broadcast_to(lam_ref[0, :, sl], (N_SEG, BLK_ST))
            li = jnp.broadcast_to(lam_ref[1, :, sl], (N_SEG, BLK_ST))

            def one(r0, spr, spi, carry, sl=sl, lr=lr, li=li):
                gr, gi, ar, ai = carry
                nr = er_buf[pl.ds(r0, N_SEG), sl] + lr * gr + li * gi
                ni = ei_buf[pl.ds(r0, N_SEG), sl] + lr * gi - li * gr
                er_buf[pl.ds(r0, N_SEG), sl] = nr
                ei_buf[pl.ds(r0, N_SEG), sl] = ni
                return nr, ni, ar + spr * nr + spi * ni, ai + spr * ni - spi * nr

            def step(ff, carry, sl=sl, one=one):
                f = jc - 1 - ff
                j = (jc - 1 - f) if reverse else f
                jp = (j + 1) if reverse else (j - 1)
                r0 = pl.multiple_of(j * N_SEG, N_SEG)
                p0 = pl.multiple_of(jp * N_SEG, N_SEG)
                return one(r0, sr_buf[pl.ds(p0, N_SEG), sl], si_buf[pl.ds(p0, N_SEG), sl], carry)

            carry = (fin_ref[0, :, sl], fin_ref[1, :, sl], dlam_ref[0, :, sl], dlam_ref[1, :, sl])
            carry = lax.fori_loop(0, jc - 1, step, carry)
            r_first = (jc - 1) * N_SEG if reverse else 0
            gr, gi, ar, ai = one(r_first, ck_ref[0, 0, :, sl], ck_ref[0, 1, :, sl], carry)
            fin_ref[0, :, sl] = gr
            fin_ref[1, :, sl] = gi
            dlam_ref[0, :, sl] = ar
            dlam_ref[1, :, sl] = ai
        tn = (((0,), (0,)), ((), ()))
        nt = (((1,), (1,)), ((), ()))
        for gb in range(N_BLOCKS):
            sl = slice(gb * BLK_ST, (gb + 1) * BLK_ST)
            cs = slice(gb * BLK_CH, (gb + 1) * BLK_CH)
            gr16 = er_buf[:, sl].astype(BF16)
            gi16 = ei_buf[:, sl].astype(BF16)
            du = (lax.dot_general(gr16, bre_ref[gb], nt, preferred_element_type=F32)
                  + lax.dot_general(gi16, bim_ref[gb], nt, preferred_element_type=F32))
            if add is not None:
                du = du + add_ref[:, cs]
            du_ref[:, cs] = du
            ub, dyb = u16[:, cs], dy16[:, cs]
            dbre_ref[gb] += lax.dot_general(ub, gr16, tn, preferred_element_type=F32)
            dbim_ref[gb] += lax.dot_general(ub, gi16, tn, preferred_element_type=F32)
            dcre_ref[gb] += lax.dot_general(sr_buf[:, sl].astype(BF16), dyb, tn, preferred_element_type=F32)
            dcim_ref[gb] -= lax.dot_general(si_buf[:, sl].astype(BF16), dyb, tn, preferred_element_type=F32)

    row_spec = pl.BlockSpec((R, D_MODEL), lambda i: (chunk(i), 0))
    st_spec = pl.BlockSpec((2, N_SEG, W), lambda i: (0, 0, 0))
    wb_spec = pl.BlockSpec(b_re.shape, lambda i: (0, 0, 0))
    wc_spec = pl.BlockSpec(c_re.shape, lambda i: (0, 0, 0))
    ins = [dy, u, ckpt, b_re, b_im, c_re, c_im, lam, init_adj]
    u_spec = pl.BlockSpec((R, D_MODEL), lambda i: (chunk(i) + u_off // R, 0))
    in_specs = [row_spec, u_spec, pl.BlockSpec((1, 2, N_SEG, W), lambda i: (chunk(i), 0, 0, 0)),
                wb_spec, wb_spec, wc_spec, wc_spec, pl.BlockSpec((2, 1, W), lambda i: (0, 0, 0)), st_spec]
    if add is not None:
        ins.append(add)
        in_specs.append(row_spec)
    out_shape = [jax.ShapeDtypeStruct((N, D_MODEL), F32), jax.ShapeDtypeStruct((2, N_SEG, W), F32),
                 jax.ShapeDtypeStruct(b_re.shape, F32), jax.ShapeDtypeStruct(b_re.shape, F32),
                 jax.ShapeDtypeStruct(c_re.shape, F32), jax.ShapeDtypeStruct(c_re.shape, F32),
                 jax.ShapeDtypeStruct((2, N_SEG, W), F32)]
    out_specs = [row_spec, st_spec, wb_spec, wb_spec, wc_spec, wc_spec, st_spec]
    return pl.pallas_call(
        body, out_shape=out_shape, grid=(nch,), in_specs=in_specs, out_specs=out_specs,
        scratch_shapes=[pltpu.VMEM((R, W), F32) for _ in range(4)] + [pltpu.VMEM((2, N_SEG, W), F32)],
        compiler_params=_params(("arbitrary",), VMEM_LIMIT), name=name)(*ins)


def adamw(w, g, m, v, name="adamw", after=None):
    n, d = w.shape
    lanes = -(-d // 128) * 128
    tm = n
    while tm * lanes * 4 > (1 << 20) and tm % 16 == 0:
        tm //= 2
    c1 = 1.0 - ADAM_B1 ** ADAM_STEP
    c2 = 1.0 - ADAM_B2 ** ADAM_STEP

    def body(w_ref, g_ref, m_ref, v_ref, *rest):
        d_ref, nm_ref, nv_ref = rest[-3:]
        g_ = g_ref[...]
        m_ = ADAM_B1 * m_ref[...] + (1.0 - ADAM_B1) * g_
        v_ = ADAM_B2 * v_ref[...] + (1.0 - ADAM_B2) * (g_ * g_)
        d_ref[...] = -ADAM_LR * ((m_ / c1) / (jnp.sqrt(v_ / c2) + ADAM_EPS) + ADAM_WD * w_ref[...])
        nm_ref[...] = m_
        nv_ref[...] = v_

    spec = pl.BlockSpec((tm, d), lambda i: (i, 0))
    extra = [] if after is None else [after]
    return pl.pallas_call(
        body, out_shape=[jax.ShapeDtypeStruct((n, d), F32)] * 3, grid=(n // tm,),
        in_specs=[spec] * 4 + [pl.BlockSpec(memory_space=pl.ANY)] * len(extra), out_specs=[spec] * 3,
        compiler_params=_params(("parallel",), VMEM_LIMIT), name=name)(w, g, m, v, *extra)


def _coords():
    return lax.axis_index("x"), lax.axis_index("y"), lax.axis_index("c")


def exchange(arrays, out_shapes, remote, local, name, aliases=None):
    n_in, n_out, n_rem, n_loc = len(arrays), len(out_shapes), len(remote), len(local)

    def at(ref, idx):
        return ref if idx is None else ref.at[idx]

    def body(*refs):
        ins, outs = refs[:n_in], refs[n_in:n_in + n_out]
        send_sems, recv_sems, local_sems = refs[n_in + n_out:]
        me = _coords()
        sends, recvs = [], []
        for k, (flip, ii, src_at, oi, dst_at) in enumerate(remote):
            peer = (me[0] ^ flip[0], me[1] ^ flip[1], me[2] ^ flip[2])
            src = at(ins[ii], src_at(me, peer))
            sends.append(pltpu.make_async_remote_copy(
                src_ref=src, dst_ref=at(outs[oi], dst_at(me)), send_sem=send_sems.at[k], recv_sem=recv_sems.at[k],
                device_id=peer, device_id_type=MESH))
            recvs.append(pltpu.make_async_remote_copy(
                src_ref=src, dst_ref=at(outs[oi], dst_at(peer)), send_sem=send_sems.at[k], recv_sem=recv_sems.at[k],
                device_id=peer, device_id_type=MESH))
        locs = [pltpu.make_async_copy(at(ins[ii], src_at(me)), at(outs[oi], dst_at(me)), local_sems.at[k])
                for k, (ii, src_at, oi, dst_at) in enumerate(local)]
        for cp in locs + sends:
            cp.start()
        for cp in recvs:
            cp.wait_recv()
        for cp in sends:
            cp.wait_send()
        for cp in locs:
            cp.wait()

    hbm = pl.BlockSpec(memory_space=pl.ANY)
    return pl.pallas_call(
        body, out_shape=list(out_shapes), in_specs=[hbm] * n_in, out_specs=[hbm] * n_out,
        scratch_shapes=[pltpu.SemaphoreType.DMA((n_rem,)), pltpu.SemaphoreType.DMA((n_rem,)),
                        pltpu.SemaphoreType.DMA((max(n_loc, 1),))],
        input_output_aliases=aliases or {}, name=name)(*arrays)


ALL_FLIPS = [(dx, dy, dc) for dx in (0, 1) for dy in (0, 1) for dc in (0, 1)][1:]
CHIP_FLIPS = [(1, 0, 0), (0, 1, 0), (1, 1, 0)]
CORE_FLIP = (0, 0, 1)


def _dev_index(p):
    return 4 * p[0] + 2 * p[1] + p[2]


def _chip_index(p):
    return 2 * p[0] + p[1]


def _gather(xs, flips, index, n, name):
    arrays = [x[None] for x in xs]
    outs = [jax.ShapeDtypeStruct((n,) + x.shape, x.dtype) for x in xs]
    remote = [(f, a, lambda me, peer: (0,), a, lambda s: (index(s),)) for a in range(len(xs)) for f in flips]
    local = [(a, lambda me: (0,), a, lambda me: (index(me),)) for a in range(len(xs))]
    return exchange(arrays, outs, remote, local, name)


def allgather_devices(x, name):
    return _gather([x], ALL_FLIPS, _dev_index, N_DEV, name)[0]


def allgather_chips(xs, name):
    return _gather(xs, CHIP_FLIPS, _chip_index, N_CHIP, name)


def gather_halves(xs, name):
    n = len(xs)
    nk = n * len(CHIP_FLIPS)

    def body(*refs):
        ins, outs = refs[:n], refs[n:2 * n]
        ici_send, ici_recv, d2d_send, d2d_recv = refs[2 * n:]
        me = _coords()
        sibling = (me[0], me[1], 1 - me[2])
        first, passed, landed = [], [], []
        for a in range(n):
            half = ins[a].shape[0] // 2
            mine = ins[a].at[pl.ds(pl.multiple_of(me[2] * half, 16), half)]
            for j, flip in enumerate(CHIP_FLIPS):
                k = a * len(CHIP_FLIPS) + j
                peer = (me[0] ^ flip[0], me[1] ^ flip[1], me[2])
                first.append(pltpu.make_async_remote_copy(
                    src_ref=mine, dst_ref=outs[a].at[_chip_index(me), me[2]], send_sem=ici_send.at[k],
                    recv_sem=ici_recv.at[k], device_id=peer, device_id_type=MESH))
                arrived = outs[a].at[_chip_index(peer), me[2]]
                landed.append(pltpu.make_async_remote_copy(
                    src_ref=mine, dst_ref=arrived, send_sem=ici_send.at[k], recv_sem=ici_recv.at[k],
                    device_id=peer, device_id_type=MESH))
                passed.append(pltpu.make_async_remote_copy(
                    src_ref=arrived, dst_ref=arrived, send_sem=d2d_send.at[k], recv_sem=d2d_recv.at[k],
                    device_id=sibling, device_id_type=MESH))
        for cp in first:
            cp.start()
        for k in range(nk):
            landed[k].wait_recv()
            passed[k].start()
        for a in range(n):
            for j, flip in enumerate(CHIP_FLIPS):
                k = a * len(CHIP_FLIPS) + j
                peer_chip = _chip_index((me[0] ^ flip[0], me[1] ^ flip[1]))
                from_sibling = outs[a].at[peer_chip, 1 - me[2]]
                pltpu.make_async_remote_copy(
                    src_ref=from_sibling, dst_ref=from_sibling, send_sem=d2d_send.at[k], recv_sem=d2d_recv.at[k],
                    device_id=sibling, device_id_type=MESH).wait_recv()
        for cp in first + passed:
            cp.wait_send()

    hbm = pl.BlockSpec(memory_space=pl.ANY)
    return pl.pallas_call(
        body, out_shape=[jax.ShapeDtypeStruct((N_CHIP, 2, x.shape[0] // 2, x.shape[1]), x.dtype) for x in xs],
        in_specs=[hbm] * n, out_specs=[hbm] * n,
        scratch_shapes=[pltpu.SemaphoreType.DMA((nk,)) for _ in range(4)], name=name)(*xs)


HBM_SPEC = pl.BlockSpec(memory_space=pltpu.HBM)
SEM_SPEC = pl.BlockSpec(memory_space=pltpu.SEMAPHORE)
DATAFLOW = pltpu.SideEffectType.DATAFLOW_SIDE_EFFECTING


def _at(ref, idx):
    return ref if idx is None else ref.at[idx]


def _peer(me, flip):
    return (me[0] ^ flip[0], me[1] ^ flip[1], me[2] ^ flip[2])


def exchange_start(arrays, land_shapes, remote, name, after=None):
    n_in, n_out, nk = len(arrays), len(land_shapes), len(remote)
    n_after = 0 if after is None else 1

    def body(*refs):
        srcs, lands = refs[:n_in], refs[n_in:n_in + n_out]
        first_out = n_in + n_out + n_after
        send_sems, recv_sems, token = refs[first_out], refs[first_out + 1], refs[-1]
        me = _coords()
        for k, (flip, ii, src_at, oi, dst_at) in enumerate(remote):
            peer = _peer(me, flip)
            pltpu.make_async_remote_copy(
                src_ref=_at(srcs[ii], src_at(me, peer)), dst_ref=_at(lands[oi], dst_at(me)), send_sem=send_sems.at[k],
                recv_sem=recv_sems.at[k], device_id=peer, device_id_type=MESH).start()
        token[...] = jnp.zeros_like(token)

    lands = [lax.empty(s.shape, s.dtype) for s in land_shapes]
    bufs = list(arrays) + lands
    out = pl.pallas_call(
        body, name=name,
        out_shape=(pltpu.SemaphoreType.DMA((nk,)), pltpu.SemaphoreType.DMA((nk,)),
                   *[pltpu.HBM(b.shape, b.dtype) for b in bufs], jax.ShapeDtypeStruct((8, 128), F32)),
        in_specs=[HBM_SPEC] * len(bufs) + [pl.BlockSpec(memory_space=pl.ANY)] * n_after,
        out_specs=(SEM_SPEC, SEM_SPEC, *[HBM_SPEC] * len(bufs), pl.BlockSpec(memory_space=pltpu.VMEM)),
        input_output_aliases={a: 2 + a for a in range(len(bufs))},
        compiler_params=pltpu.CompilerParams(has_side_effects=DATAFLOW),
    )(*[pltpu.with_memory_space_constraint(b, pltpu.HBM) for b in bufs], *([after] if n_after else []))
    flight = (out[0], out[1], list(out[2:2 + n_in]), list(out[2 + n_in:2 + n_in + n_out]), remote)
    return flight, out[-1]


def exchange_wait(flight, after, name):
    send_sems, recv_sems, arrays, lands, remote = flight
    n_in, n_out = len(arrays), len(lands)

    def body(*refs):
        srcs, lnds = refs[:n_in], refs[n_in:n_in + n_out]
        s_sems, r_sems = refs[n_in + n_out], refs[n_in + n_out + 1]
        me = _coords()
        for k, (flip, ii, src_at, oi, dst_at) in enumerate(remote):
            peer = _peer(me, flip)
            copy = pltpu.make_async_remote_copy(
                src_ref=_at(srcs[ii], src_at(me, peer)), dst_ref=_at(lnds[oi], dst_at(peer)), send_sem=s_sems.at[k],
                recv_sem=r_sems.at[k], device_id=peer, device_id_type=MESH)
            copy.wait_send()
            copy.wait_recv()

    bufs = list(arrays) + list(lands)
    out = pl.pallas_call(
        body, name=name,
        out_shape=tuple(pltpu.HBM(b.shape, b.dtype) for b in bufs),
        in_specs=[HBM_SPEC] * len(bufs) + [SEM_SPEC, SEM_SPEC, pl.BlockSpec(memory_space=pl.ANY)],
        out_specs=tuple([HBM_SPEC] * len(bufs)),
        input_output_aliases={a: a for a in range(len(bufs))},
        compiler_params=pltpu.CompilerParams(has_side_effects=DATAFLOW),
    )(*bufs, send_sems, recv_sems, after)
    return list(out[:n_in]), list(out[n_in:])


def _half_tile(h, cd):
    return h if h * cd * 4 <= (1 << 20) else math.gcd(512, h)


def pair_add(g, got, core, out_dtype, name):
    _, _, h, cd = g.shape
    th = _half_tile(h, cd)

    def body(c_ref, g_ref, got_ref, o_ref):
        o_ref[0] = (g_ref[0, 0] + got_ref[0]).astype(o_ref.dtype)

    return pl.pallas_call(
        body, out_shape=jax.ShapeDtypeStruct((N_CHIP, h, cd), out_dtype),
        grid_spec=pltpu.PrefetchScalarGridSpec(
            num_scalar_prefetch=1, grid=(N_CHIP, h // th),
            in_specs=[pl.BlockSpec((1, 1, th, cd), lambda q, i, c: (q, c[0], i, 0)),
                      pl.BlockSpec((1, th, cd), lambda q, i, c: (q, i, 0))],
            out_specs=pl.BlockSpec((1, th, cd), lambda q, i, c: (q, i, 0))),
        compiler_params=_params(("parallel", "parallel"), VMEM_LIMIT), name=name)(core, g, got)


def sum_chips(parts, sums, place, name):
    _, h, cd = parts.shape
    th = _half_tile(h, cd)

    def body(pc_ref, p_ref, own_ref, o_ref):
        acc = None
        for q in range(N_CHIP):
            term = jnp.where(pc_ref[1] == q, own_ref[0], p_ref[q]).astype(F32)
            acc = term if acc is None else acc + term
        o_ref[0] = acc

    return pl.pallas_call(
        body, out_shape=jax.ShapeDtypeStruct((2, h, cd), F32),
        grid_spec=pltpu.PrefetchScalarGridSpec(
            num_scalar_prefetch=1, grid=(h // th,),
            in_specs=[pl.BlockSpec((N_CHIP, th, cd), lambda i, pc: (0, i, 0)),
                      pl.BlockSpec((1, th, cd), lambda i, pc: (pc[1], i, 0))],
            out_specs=pl.BlockSpec((1, th, cd), lambda i, pc: (pc[0], i, 0))),
        compiler_params=_params(("parallel",), VMEM_LIMIT), name=name)(place, parts, sums)


def to_segments(a, n_ctx):
    def one(p):
        n = p.shape[0]
        return p.reshape(N_SEG, n // N_SEG, -1).transpose(1, 0, 2).reshape(n, -1)
    return jnp.concatenate([one(a[:n_ctx]), one(a[n_ctx:])], axis=0) if n_ctx else one(a)


def from_segments(a, n_ctx):
    def one(p):
        n = p.shape[0]
        return p.reshape(n // N_SEG, N_SEG, -1).transpose(1, 0, 2).reshape(n, -1)
    return jnp.concatenate([one(a[:n_ctx]), one(a[n_ctx:])], axis=0) if n_ctx else one(a)


def rope_tables(n_ctx, n_lat):
    f32 = np.float32
    rows = n_lat // GRID_W
    row = np.repeat(np.arange(rows), GRID_W).astype(f32)
    col = np.tile(np.arange(GRID_W), rows).astype(f32)
    d = QK_ROPE_DIM // 2
    inv = (f32(1.0) / np.power(f32(ROPE_THETA), np.arange(0, d, 2, dtype=f32) / f32(d))).astype(f32)
    ang = np.concatenate([row[:, None] * inv[None, :], col[:, None] * inv[None, :]], axis=1).astype(f32)
    cos = np.concatenate([np.ones((n_ctx, d), f32), np.cos(ang)], axis=0)
    sin = np.concatenate([np.zeros((n_ctx, d), f32), np.sin(ang)], axis=0)
    q = QK_ROPE_DIM // 4
    T = n_ctx + n_lat
    ones, zeros = np.ones((T, QK_NOPE_DIM), f32), np.zeros((T, QK_NOPE_DIM), f32)
    tail, z8 = np.zeros((T, HEAD_LANES - QK_DIM), f32), np.zeros((T, q), f32)
    cr, cc, sr, sc = cos[:, :q], cos[:, q:], sin[:, :q], sin[:, q:]
    cos_t = np.concatenate([ones, cr, cr, cc, cc, tail], axis=1)
    sin_next = np.concatenate([zeros, -sr, z8, -sc, z8, tail], axis=1)
    sin_prev = np.concatenate([zeros, z8, sr, z8, sc, tail], axis=1)
    return tuple(jnp.asarray(t, F32) for t in (cos_t, sin_next, sin_prev))


def pad_heads(w, used):
    k = w.shape[0]
    return jnp.pad(w.reshape(k, MLA_HEADS, used), ((0, 0), (0, 0), (0, HEAD_LANES - used))).reshape(k, -1)


def unpad_heads(w, used):
    k = w.shape[0]
    return w.reshape(k, MLA_HEADS, HEAD_LANES)[:, :, :used].reshape(k, MLA_HEADS * used)


def rotary_spread():
    lane = np.arange(MLA_HEADS * HEAD_LANES) % HEAD_LANES
    return jnp.asarray(lane[None, :] == (QK_NOPE_DIM + np.arange(QK_ROPE_DIM))[:, None], BF16)


def s5_discretise(a_re, a_im, log_step, b_re, b_im):
    dt = jnp.exp(log_step)[:, None]
    mag = jnp.exp(a_re * dt)
    lb_re = mag * jnp.cos(a_im * dt)
    lb_im = mag * jnp.sin(a_im * dt)
    den = a_re * a_re + a_im * a_im
    nr = lb_re - 1.0
    f_re = ((nr * a_re + lb_im * a_im) / den)[..., None]
    f_im = ((lb_im * a_re - nr * a_im) / den)[..., None]
    return lb_re, lb_im, f_re * b_re - f_im * b_im, f_re * b_im + f_im * b_re


def s5_block_weights(lb_re, lb_im, bb_re, bb_im, c_re, c_im):
    eye = jnp.eye(GROUPS_PER_BLOCK, dtype=F32)
    lam = jnp.stack([lb_re.reshape(1, S5_LANES), lb_im.reshape(1, S5_LANES)])

    def b_blocks(bb):
        t = bb.reshape(N_BLOCKS, GROUPS_PER_BLOCK, S5_STATE, S5_GROUP)
        return jnp.einsum("bgpc,gh->bgchp", t, eye).reshape(N_BLOCKS, BLK_CH, BLK_ST).astype(BF16)

    def c_blocks(cc):
        t = cc.reshape(N_BLOCKS, GROUPS_PER_BLOCK, S5_GROUP, S5_STATE)
        return jnp.einsum("bgcp,gh->bgphc", t, eye).reshape(N_BLOCKS, BLK_ST, BLK_CH).astype(BF16)

    return lam, b_blocks(bb_re), b_blocks(bb_im), c_blocks(c_re), c_blocks(c_im)


def b_block_diag(db):
    t = db.reshape(N_BLOCKS, GROUPS_PER_BLOCK, S5_GROUP, GROUPS_PER_BLOCK, S5_STATE)
    return jnp.einsum("bgchp,gh->bgpc", t, jnp.eye(GROUPS_PER_BLOCK, dtype=F32)).reshape(S5_GROUPS, S5_STATE, S5_GROUP)


def c_block_diag(dc):
    t = dc.reshape(N_BLOCKS, GROUPS_PER_BLOCK, S5_STATE, GROUPS_PER_BLOCK, S5_GROUP)
    return jnp.einsum("bgphc,gh->bgcp", t, jnp.eye(GROUPS_PER_BLOCK, dtype=F32)).reshape(S5_GROUPS, S5_GROUP, S5_STATE)


def conj(a):
    return jnp.stack([a[0], -a[1]])


PACK_TILE = 16 * 128


def pack_flat(parts, dtype):
    flat = [p.reshape(-1).astype(dtype) for p in parts]
    sizes = [f.shape[0] for f in flat]
    total = sum(sizes)
    pad = (-total) % PACK_TILE
    if pad:
        flat.append(jnp.zeros((pad,), dtype))
    offs = np.cumsum([0] + sizes)[:-1].tolist()
    return jnp.concatenate(flat).reshape(-1, 128), offs


def unpack_flat(buf, offs, shapes):
    flat = buf.reshape(-1)
    return [flat[o:o + int(np.prod(s))].reshape(s) for o, s in zip(offs, shapes)]


def s5_forward(p1, n_ctx, dirs):
    saved = []
    y = None
    ctx_rows, lat_rows = (0, n_ctx), (n_ctx, p1.shape[0] - n_ctx)
    zeros_tile = jnp.zeros((2, N_SEG, S5_LANES), F32)
    zeros_row = jnp.zeros((2, 1, S5_LANES), F32)
    for k, (lam, b_re, b_im, c_re, c_im) in enumerate(dirs):
        rev = k == 1
        last = 0 if rev else N_SEG - 1
        _, _, fin = s5_scan(p1, b_re, b_im, lam, zeros_tile, reverse=rev, rows=ctx_rows, name=f"s5_ctx_finals{k}")
        carry_c = s5_chain(fin, zeros_row, lam, n_ctx // N_SEG, rev, name=f"s5_ctx_chain{k}")
        _, ck_c, fin_c = s5_scan(p1, b_re, b_im, lam, carry_c, reverse=rev, want_ckpt=True, rows=ctx_rows,
                                 name=f"s5_ctx_scan{k}")
        s0 = fin_c[:, last:last + 1, :]
        _, _, fin = s5_scan(p1, b_re, b_im, lam, zeros_tile, reverse=rev, rows=lat_rows, name=f"s5_lat_finals{k}")
        carry_l = s5_chain(fin, s0, lam, lat_rows[1] // N_SEG, rev, name=f"s5_lat_chain{k}")
        y, ck_l, _ = s5_scan(p1, b_re, b_im, lam, carry_l, reverse=rev, c_re=c_re, c_im=c_im, add=y,
                             want_ckpt=True, rows=lat_rows, name=f"s5_lat_scan{k}")
        saved.append((ck_c, ck_l))
    return y, saved


def s5_backward(dy_l, du_extra_l, p1, n_ctx, dirs, saved):
    n_lat = p1.shape[0] - n_ctx
    zeros_tile = jnp.zeros((2, N_SEG, S5_LANES), F32)
    zeros_row = jnp.zeros((2, 1, S5_LANES), F32)
    dy_c = jnp.zeros((n_ctx, D_MODEL), F32)
    du_l, du_c = du_extra_l, None
    grads = []
    for k, (lam, b_re, b_im, c_re, c_im) in enumerate(dirs):
        rev = k == 1
        lam_c = conj(lam)
        ck_c, ck_l = saved[k]
        first = N_SEG - 1 if rev else 0
        _, _, fin = s5_scan(dy_l, c_re, c_im, lam_c, zeros_tile, reverse=not rev, adjoint=True,
                            name=f"s5_lat_adj_finals{k}")
        carry = s5_chain(fin, zeros_row, lam_c, n_lat // N_SEG, not rev, name=f"s5_lat_adj_chain{k}")
        du_l, dlam_l, dbr_l, dbi_l, dcr_l, dci_l, fin_a = s5_grads(
            dy_l, p1, ck_l, b_re, b_im, c_re, c_im, lam, carry, reverse=rev, add=du_l, u_off=n_ctx,
            name=f"s5_lat_grads{k}")
        g0 = fin_a[:, first:first + 1, :]
        carry = s5_chain(zeros_tile, g0, lam_c, n_ctx // N_SEG, not rev, name=f"s5_ctx_adj_chain{k}")
        du_c, dlam_c, dbr_c, dbi_c, _, _, _ = s5_grads(
            dy_c, p1, ck_c, b_re, b_im, c_re, c_im, lam, carry, reverse=rev, add=du_c, name=f"s5_ctx_grads{k}")
        dlam = jnp.sum(dlam_l + dlam_c, axis=1)
        grads.append((dlam, b_block_diag(dbr_l + dbr_c), b_block_diag(dbi_l + dbi_c),
                      c_block_diag(dcr_l), c_block_diag(dci_l)))
    return jnp.concatenate([du_c, du_l], axis=0), grads


def local_step(x, ctx, target, mod, w, late=None, reducer=None):
    L, Lc = x.shape[0], ctx.shape[0]
    T = L + Lc
    assert L % Lc == 0 and Lc % (2 * N_SEG) == 0 and L % GRID_W == 0
    D = D_MODEL
    X0 = jnp.concatenate([ctx, x], axis=0)

    def mod_of(i, j):
        return mod[i, :, j, :][:, None, :]

    def vec(v):
        return v.reshape(1, 1, -1).astype(F32)

    g0 = vec(w["norm_g"][0])
    H0, p0 = norm_proj(X0, g0, mod_of(0, 1), mod_of(0, 0), w["mla_w_in"], Lc, "l0_norm_in")
    cq = Rows(p0, Q_LORA_RANK, col_blk=D // Q_LORA_RANK)
    ckv = Rows(p0, KV_LORA_RANK, col_blk=(D + Q_LORA_RANK) // KV_LORA_RANK)
    kr = p0[:, D + Q_LORA_RANK + KV_LORA_RANK:D + P0_HEAD]
    qng, kvng = vec(w["mla_q_norm"]), vec(w["mla_kv_norm"])
    tabs = rope_tables(Lc, L)
    spread = rotary_spread()
    w_uq_p = pad_heads(w["mla_w_uq"], QK_DIM)
    w_ukv3 = w["mla_w_ukv"].reshape(KV_LORA_RANK, MLA_HEADS, QK_NOPE_DIM + V_HEAD_DIM)
    w_kn_p = pad_heads(w_ukv3[:, :, :QK_NOPE_DIM].reshape(KV_LORA_RANK, -1), QK_NOPE_DIM)
    w_v = w_ukv3[:, :, QK_NOPE_DIM:].reshape(KV_LORA_RANK, -1)
    qb, qn = q_heads(cq, qng, w_uq_p, tabs)
    kb, vb, kvn = kv_heads(ckv, kvng, w_kn_p, w_v, kr, spread, tabs)
    o, lse, qs = attn_fwd(qb, kb, vb, Lc)
    X1, og, out0 = mla_post_fwd(o, p0, X0, mod_of(0, 2), w["mla_w_out"], Lc)

    if late is not None:
        w = {**w, **late(X1)}
    X1p = to_segments(X1, Lc)
    tgt_p = to_segments(target, 0)
    g1 = vec(w["norm_g"][1])
    H1, p1 = norm_proj(X1p, g1, mod_of(1, 1), mod_of(1, 0), w["s5_w_in"], Lc, "l1_norm_in")
    disc_fn = lambda *a: tuple(zip(*[s5_discretise(a[0][k], a[1][k], a[2][k], a[3][k], a[4][k]) for k in range(2)]))
    disc, disc_vjp = jax.vjp(disc_fn, w["s5_a_re"], w["s5_a_im"], w["s5_log_step"], w["s5_b_re"], w["s5_b_im"])
    dirs = [s5_block_weights(disc[0][k], disc[1][k], disc[2][k], disc[3][k], w["s5_c_re"][k], w["s5_c_im"][k])
            for k in range(2)]
    y_ssm, s5_saved = s5_forward(p1, Lc, dirs)

    row = lambda v: v.reshape(1, D).astype(F32)
    (lvec, dX2, d_yssm, d_u_act, d_z1, d_fg, d_gt1, d_bg, d_d, gw_glu, gw_out) = s5_tail(
        y_ssm, p1, X1p, tgt_p, Lc, row(w["s5_d"]), row(w["s5_b_glu"]), mod[1, 1:2, 2, :], row(w["final_g"]),
        w["s5_w_glu"], w["s5_w_out"])
    loss = jnp.sum(lvec)
    gw = {"final_g": d_fg.reshape(D), "s5_b_glu": d_bg.reshape(D), "s5_d": d_d.reshape(D),
          "s5_w_glu": gw_glu, "s5_w_out": gw_out}
    dmod = {}

    du_p, s5_g = s5_backward(d_yssm, d_u_act, p1, Lc, dirs, s5_saved)
    d_disc = tuple(tuple(s5_g[k][j - 1].reshape(disc[j][k].shape) if j >= 2 else
                         s5_g[k][0][j].reshape(disc[j][k].shape) for k in range(2)) for j in range(4))
    gw["s5_a_re"], gw["s5_a_im"], gw["s5_log_step"], gw["s5_b_re"], gw["s5_b_im"] = disc_vjp(d_disc)
    gw["s5_c_re"] = jnp.stack([s5_g[0][3], s5_g[1][3]])
    gw["s5_c_im"] = jnp.stack([s5_g[0][4], s5_g[1][4]])
    gw["s5_w_in"] = jnp.concatenate([mm_tn(H1, du_p, name="l1_in_dw_u"), mm_tn(H1[Lc:], d_z1, name="l1_in_dw_z")],
                                    axis=1)
    if reducer is not None:
        g1 = g1 + reducer[0]({n: gw.pop(n) for n in LAYER1_MATS})[0, 0]
    d_X1p, d_g1, d_sc1, d_sh1 = norm_proj_bwd([(du_p, 0, 0), (d_z1, D, Lc)], w["s5_w_in"], X1p, g1, mod_of(1, 1),
                                              mod_of(1, 0), dX2, Lc, "l1_norm_in_bwd")
    d_gt1_full = jnp.concatenate([jnp.zeros((1, 1, D), F32), d_gt1[None]], axis=0)
    dmod[1] = (d_sh1, d_sc1, d_gt1_full)
    d_X1 = from_segments(d_X1p, Lc)

    d_o, d_z0, d_gt0, gw["mla_w_out"] = mla_post_bwd(d_X1, out0, og, o, p0, mod_of(0, 2), w["mla_w_out"], Lc)
    if reducer is not None:
        tabs = (tabs[0] + reducer[1](d_o)[0, 0],) + tabs[1:]
    d_q, dos = attn_bwd_dq(qb, kb, vb, o, d_o, lse, tabs, Lc)
    dk_p, d_v = attn_bwd_dkv(qs, kb, vb, dos, Lc)
    d_k, d_kr = heads_unrope(dk_p, tabs, jnp.pad(spread, ((0, HEAD_LANES - QK_ROPE_DIM), (0, 0))),
                             name="l0_k_unrope")
    d_qn = mm_nt(d_q, w_uq_p, name="l0_uq_dx")
    gw["mla_w_uq"] = unpad_heads(mm_tn(qn, d_q, name="l0_uq_dw"), QK_DIM)
    d_kvn = mm_nt(d_k, w_kn_p, name="l0_ukn_dx") + mm_nt(d_v, w_v, name="l0_uv_dx")
    dw_kn = unpad_heads(mm_tn(kvn, d_k, name="l0_ukn_dw"), QK_NOPE_DIM).reshape(KV_LORA_RANK, MLA_HEADS, QK_NOPE_DIM)
    dw_v = mm_tn(kvn, d_v, name="l0_uv_dw").reshape(KV_LORA_RANK, MLA_HEADS, V_HEAD_DIM)
    gw["mla_w_ukv"] = jnp.concatenate([dw_kn, dw_v], axis=-1).reshape(KV_LORA_RANK, -1)
    d_cq, d_qng = rowwise_bwd(f_rms, [cq], [qng], [d_qn], [0], [0], T, 0, "l0_qnorm_bwd")
    d_ckv, d_kvng = rowwise_bwd(f_rms, [ckv], [kvng], [d_kvn], [0], [0], T, 0, "l0_kvnorm_bwd")
    gw["mla_q_norm"] = d_qng.reshape(-1)
    gw["mla_kv_norm"] = d_kvng.reshape(-1)
    o_cq, o_ckv = D, D + Q_LORA_RANK
    o_kr = o_ckv + KV_LORA_RANK
    d_head = jnp.concatenate([d_cq, d_ckv, d_kr], axis=1)
    gw["mla_w_in"] = jnp.concatenate([mm_tn(H0, d_head, name="l0_in_dw_head")[:, :P0_HEAD],
                                      mm_tn(H0, d_z0, name="l0_in_dw_z")], axis=1)
    d_X0, d_g0, d_sc0, d_sh0 = norm_proj_bwd(
        [(d_z0, 0, 0), (d_cq, o_cq, 0), (d_ckv, o_ckv, 0), (d_kr, o_kr, 0)], w["mla_w_in"], X0, g0, mod_of(0, 1),
        mod_of(0, 0), d_X1, Lc, "l0_norm_in_bwd")
    dmod[0] = (d_sh0, d_sc0, d_gt0)
    gw["norm_g"] = jnp.stack([d_g0.reshape(D), d_g1.reshape(D)])
    dx = d_X0[Lc:]
    dmod_arr = jnp.stack([jnp.stack([dmod[i][j][:, 0, :] for j in range(3)], axis=1) for i in range(2)])
    ready = reducer[2](d_X0) if reducer is not None else {}
    return loss, dx, dmod_arr, gw, ready


SHARDED = {
    "mla_w_in": 1, "mla_w_uq": 1, "mla_w_ukv": 1, "mla_w_out": 0,
    "s5_w_in": 1, "s5_w_glu": 0, "s5_w_out": 0, "s5_d": 0, "s5_b_glu": 0,
}
SHARDED_MATS = ["mla_w_in", "mla_w_uq", "mla_w_ukv", "mla_w_out", "s5_w_in", "s5_w_glu", "s5_w_out"]
SHARDED_VECS = ["s5_d", "s5_b_glu"]
REPLICATED = ["norm_g", "mla_q_norm", "mla_kv_norm", "s5_a_re", "s5_a_im", "s5_log_step", "s5_b_re", "s5_b_im",
              "s5_c_re", "s5_c_im", "final_g"]
WEIGHT_ORDER = ["c_ctx", "ada_w", "ada_b", "norm_g", "mla_w_in", "mla_q_norm", "mla_w_uq", "mla_kv_norm", "mla_w_ukv",
                "mla_w_out", "s5_w_in", "s5_a_re", "s5_a_im", "s5_log_step", "s5_b_re", "s5_b_im", "s5_c_re", "s5_c_im",
                "s5_d", "s5_w_glu", "s5_b_glu", "s5_w_out", "final_g"]


P0_HEAD = Q_LORA_RANK + KV_LORA_RANK + QK_ROPE_DIM


P0_WIDTH = 1536


def w_in_to_kernel_order(w):
    pad = jnp.zeros((w.shape[0], P0_WIDTH - w.shape[1]), w.dtype)
    return jnp.concatenate([w[:, P0_HEAD:], w[:, :P0_HEAD], pad], axis=1)


LAYER0_MATS = ["mla_w_in", "mla_w_uq", "mla_w_ukv", "mla_w_out"]
LAYER1_MATS = ["s5_w_in", "s5_w_glu", "s5_w_out"]


def _whole_matrices(names, own_blocks, gathered):
    chip = _chip_index(_coords())
    full = {}
    for n, own, o in zip(names, own_blocks, gathered):
        slot = lax.broadcasted_iota(jnp.int32, (N_CHIP, 1, 1), 0)
        o = jnp.where(slot == chip, own[None], o.reshape((N_CHIP,) + own.shape))
        full[n] = o.reshape(-1, o.shape[-1]) if SHARDED[n] == 0 else o.transpose(1, 0, 2).reshape(o.shape[1], -1)
    return full


def gather_weights(ws):
    mats = [ws[n].astype(BF16) for n in LAYER0_MATS]
    full = _whole_matrices(LAYER0_MATS, mats, gather_halves(mats, "gather_weights"))
    full["mla_w_in"] = w_in_to_kernel_order(full["mla_w_in"])
    return full


def gather_weights_behind(ws, after):
    mats = [ws[n].astype(BF16) for n in LAYER1_MATS]
    flight, token = exchange_start(
        mats, [jax.ShapeDtypeStruct((N_CHIP,) + m.shape, m.dtype) for m in mats],
        [(f, a, lambda me, peer: None, a, lambda s: (_chip_index(s),)) for a in range(len(mats)) for f in CHIP_FLIPS],
        "gather_l1_start", after=after)

    def finish(after):
        own, got = exchange_wait(flight, after, "gather_l1_wait")
        return _whole_matrices(LAYER1_MATS, own, got)

    return token[0, 0], finish


def _grad_slots(gw, names):
    slots = []
    for n in names:
        g = gw[n]
        if SHARDED[n] == 0:
            slots.append(g.reshape(N_CHIP, 2, g.shape[0] // (2 * N_CHIP), g.shape[1]))
        else:
            k, n4 = g.shape
            slots.append(g.reshape(k, N_CHIP, n4 // N_CHIP).transpose(1, 0, 2)
                         .reshape(N_CHIP, 2, k // 2, n4 // N_CHIP))
    return slots


def _to_sibling_half(count):
    return [(CORE_FLIP, i, lambda me, peer: (slice(None), 1 - me[2]), i, lambda s: None) for i in range(count)]


def _to_chips(count):
    return [(f, i, lambda me, peer: (_chip_index(peer),), i, lambda s: (_chip_index(s),))
            for i in range(count) for f in CHIP_FLIPS]


def _place():
    me = _coords()
    return jnp.stack([me[2], _chip_index(me)]).astype(jnp.int32)


def reduce_behind(names):
    state = {}
    count = len(names)

    def begin(gw):
        slots = _grad_slots(gw, names)
        lands = [jax.ShapeDtypeStruct((N_CHIP,) + s.shape[2:], F32) for s in slots]
        state["in"], token = exchange_start(slots, lands, _to_sibling_half(count), "grads_l1_swap_in_start")
        return token

    def middle(after):
        slots, got = exchange_wait(state["in"], after, "grads_l1_swap_in_wait")
        place = _place()
        sums = [pair_add(s, g, place[:1], BF16, f"grads_pair_{n}") for n, s, g in zip(names, slots, got)]
        lands = [jax.ShapeDtypeStruct(s.shape, s.dtype) for s in sums]
        state["out"], token = exchange_start(sums, lands, _to_chips(count), "grads_l1_scatter_start")
        return token

    def end(after):
        sums, parts = exchange_wait(state["out"], after, "grads_l1_scatter_wait")
        place = _place()
        return {n: sum_chips(p, s, place, f"grads_sum_{n}") for n, p, s in zip(names, parts, sums)}

    return begin, middle, end


def reduce_gradients(gw, ready_halves):
    me = _coords()
    place = _place()
    mat_names = [n for n in SHARDED_MATS if n not in ready_halves]
    slots = dict(zip(mat_names, _grad_slots(gw, mat_names)))
    small_names = REPLICATED + SHARDED_VECS
    small, small_offs = pack_flat([gw[n].astype(F32) for n in small_names], F32)
    small = jnp.pad(small, ((0, (-small.shape[0]) % (N_CHIP * 32)), (0, 0)))
    slots["small"] = small.reshape(N_CHIP, 2, -1, 128)
    names = list(slots)
    count = len(names)
    got = exchange([slots[n] for n in names],
                   [jax.ShapeDtypeStruct((N_CHIP,) + slots[n].shape[2:], F32) for n in names],
                   _to_sibling_half(count), [], "grads_swap_in")
    sums = [pair_add(slots[n], g, place[:1], F32 if n == "small" else BF16, f"grads_pair_{n}")
            for n, g in zip(names, got)]
    parts = exchange(sums, [jax.ShapeDtypeStruct(s.shape, s.dtype) for s in sums], _to_chips(count), [],
                     "grads_scatter")
    halves = {n: sum_chips(p, s, place, f"grads_sum_{n}") for n, p, s in zip(names, parts, sums)}
    halves.update(ready_halves)
    all_names = list(halves)
    fulls = exchange(
        [halves[n] for n in all_names], [jax.ShapeDtypeStruct(halves[n].shape, F32) for n in all_names],
        [(CORE_FLIP, i, lambda me, peer: (me[2],), i, lambda s: (s[2],)) for i in range(len(all_names))], [],
        "grads_swap_out", aliases={i: i for i in range(len(all_names))})
    out = {n: f.reshape(-1, f.shape[-1]) for n, f in zip(all_names, fulls)}
    quarter = out.pop("small")
    gather, token = exchange_start(
        [quarter], [jax.ShapeDtypeStruct((N_CHIP,) + quarter.shape, F32)],
        [(f, 0, lambda me, peer: None, 0, lambda s: (_chip_index(s),)) for f in CHIP_FLIPS],
        "grads_gather_small_start")

    def finish_small(after):
        (own,), (got_small,) = exchange_wait(gather, after, "grads_gather_small_wait")
        slot = lax.broadcasted_iota(jnp.int32, (N_CHIP, 1, 1), 0)
        small_all = jnp.where(slot == _chip_index(me), own[None], got_small)
        vals = unpack_flat(small_all, small_offs, [gw[n].shape for n in small_names])
        res = {}
        for n, v in zip(small_names, vals):
            if n in SHARDED_VECS:
                size = v.shape[0] // N_CHIP
                v = lax.dynamic_slice_in_dim(v, _chip_index(me) * size, size)
            res[n] = v
        return res

    return out, finish_small, token


def kernel(x, c, ctx, c_ctx, ada_w, ada_b, norm_g, mla_w_in, mla_q_norm, mla_w_uq, mla_kv_norm, mla_w_ukv, mla_w_out, s5_w_in, s5_a_re, s5_a_im, s5_log_step, s5_b_re, s5_b_im, s5_c_re, s5_c_im, s5_d, s5_w_glu, s5_b_glu, s5_w_out, final_g, loss_target, m_c_ctx, m_ada_w, m_ada_b, m_norm_g, m_mla_w_in, m_mla_q_norm, m_mla_w_uq, m_mla_kv_norm, m_mla_w_ukv, m_mla_w_out, m_s5_w_in, m_s5_a_re, m_s5_a_im, m_s5_log_step, m_s5_b_re, m_s5_b_im, m_s5_c_re, m_s5_c_im, m_s5_d, m_s5_w_glu, m_s5_b_glu, m_s5_w_out, m_final_g, v_c_ctx, v_ada_w, v_ada_b, v_norm_g, v_mla_w_in, v_mla_q_norm, v_mla_w_uq, v_mla_kv_norm, v_mla_w_ukv, v_mla_w_out, v_s5_w_in, v_s5_a_re, v_s5_a_im, v_s5_log_step, v_s5_b_re, v_s5_b_im, v_s5_c_re, v_s5_c_im, v_s5_d, v_s5_w_glu, v_s5_b_glu, v_s5_w_out, v_final_g):
    args = dict(locals())
    weights = {n: args[n] for n in WEIGHT_ORDER}
    D = D_MODEL
    xi, yi, ci = _coords()
    chip = 2 * xi + yi
    me = 4 * xi + 2 * yi + ci
    n_col = ada_w.shape[2]

    c_all = allgather_devices(jnp.pad(c, ((0, 7), (0, 0))), "gather_c")[:, 0, :]
    cond = jnp.concatenate([c_all, jnp.broadcast_to(c_ctx[None], (8, D))], axis=0)
    (s_cond,) = rowwise_fwd(lambda v: (_silu(v),), [cond], [], [D], [F32], 16, 0, "cond_silu")
    ada_rows = ada_w.reshape(2 * D, n_col)
    mod_cols = jnp.stack([mm_nn(s_cond, ada_rows, name=f"mod_proj{i}", b_blk=i) for i in range(2)])
    vec_tiles = [jnp.pad(weights[n][0].reshape(-1, 128), ((0, 6), (0, 0))) for n in SHARDED_VECS]
    mod_all, *vec_all = allgather_chips([mod_cols] + vec_tiles, "gather_mod")
    mod_all = mod_all.transpose(1, 2, 0, 3).reshape(2, 16, 3 * D) + ada_b[:, None, :]
    mod_l = lax.dynamic_index_in_dim(mod_all, me, axis=1, keepdims=False)
    mod_c = mod_all[:, 8, :]
    mod = jnp.stack([mod_c.reshape(2, 3, D), mod_l.reshape(2, 3, D)], axis=1)

    w = gather_weights({n: weights[n][0] for n in LAYER0_MATS})
    token, late = gather_weights_behind({n: weights[n][0] for n in LAYER1_MATS}, w["mla_w_out"])
    for n, v in zip(SHARDED_VECS, vec_all):
        w[n] = v[:, :2, :].reshape(-1)
    for n in ["norm_g", "final_g"]:
        w[n] = weights[n]
    for n in ["mla_q_norm", "mla_kv_norm", "s5_a_re", "s5_a_im", "s5_log_step", "s5_b_re", "s5_b_im",
              "s5_c_re", "s5_c_im"]:
        w[n] = weights[n][0]

    loss_me, dx, dmod, gw, ready = local_step(x[0] + token, ctx[0], loss_target[0], mod, w, late,
                                              reduce_behind(LAYER1_MATS))

    dmod_rows, loss_all = _gather([dmod.reshape(2, 2, 3 * D), jnp.broadcast_to(loss_me, (8, 128))],
                                  ALL_FLIPS, _dev_index, N_DEV, "gather_dmod")
    loss = functools.reduce(lambda s, d: s + loss_all[d, 0, 0], range(1, N_DEV), loss_all[0, 0, 0])
    dm = jnp.concatenate([dmod_rows[:, :, 1, :], dmod_rows[:, :, 0, :]], axis=0).transpose(1, 0, 2)
    g_ada_b = jnp.sum(dm, axis=1)
    dm_cols = lax.dynamic_slice_in_dim(dm, chip * n_col, n_col, axis=2)
    g_ada_w = jnp.stack([mm_tn(s_cond, dm_cols[i], name=f"mod_proj_dw{i}") for i in range(2)])
    dmc = jnp.sum(dm_cols[:, 8:, :], axis=1)
    dmc8 = jnp.broadcast_to(dmc[:, None, :], (2, 8, n_col))
    g_sc = (mm_nt(dmc8[0], ada_rows, name="mod_proj_dx0", b_rows=D, b_blk=0)[0]
            + mm_nt(dmc8[1], ada_rows, name="mod_proj_dx1", b_rows=D, b_blk=1)[0])
    g_sc_all = allgather_devices(jnp.broadcast_to(g_sc[None], (8, D)), "gather_dcond")[:, 0, :]
    g_silu_cc = g_sc_all[0] + g_sc_all[2] + g_sc_all[4] + g_sc_all[6]
    (g_c_ctx,) = rowwise_bwd(lambda v: (_silu(v),), [jnp.broadcast_to(c_ctx[None], (8, D))], [],
                             [jnp.broadcast_to(g_silu_cc[None], (8, D))], [0], [], 8, 0, "cond_silu_bwd")
    g_c_ctx = g_c_ctx[0]

    grads = {"c_ctx": g_c_ctx, "ada_w": g_ada_w, "ada_b": g_ada_b}
    deltas, new_m, new_v = {}, {}, {}
    small = [n for n in WEIGHT_ORDER if weights[n].size < 50000]

    def update(n, after=None):
        shp = weights[n].shape
        w2 = weights[n].reshape(-1, shp[-1])
        d_, m_, v_ = adamw(w2, grads[n].reshape(w2.shape), args["m_" + n].reshape(w2.shape),
                           args["v_" + n].reshape(w2.shape), name=f"adamw_{n}", after=after)
        deltas[n], new_m[n], new_v[n] = d_.reshape(shp), m_.reshape(shp), v_.reshape(shp)

    red, finish_small, small_started = reduce_gradients(gw, ready)
    update("ada_w", small_started)
    for n in SHARDED_MATS:
        grads[n] = red[n].reshape(weights[n].shape)
        update(n, small_started)
    red_small = finish_small(jnp.stack([deltas[n][(0,) * deltas[n].ndim]
                                        for n in ["ada_w"] + SHARDED_MATS]))
    for n in REPLICATED + SHARDED_VECS:
        grads[n] = red_small[n].reshape(weights[n].shape)
    for n in WEIGHT_ORDER:
        if n not in small and n not in deltas:
            update(n)
    packs = []
    offs = None
    for src in (weights, grads, {n: args["m_" + n] for n in small}, {n: args["v_" + n] for n in small}):
        buf, offs = pack_flat([src[n] for n in small], F32)
        packs.append(buf)
    outs = adamw(*packs, name="adamw_small")
    for res, dst in zip(outs, (deltas, new_m, new_v)):
        for n, val in zip(small, unpack_flat(res, offs, [weights[n].shape for n in small])):
            dst[n] = val

    return (loss, dx[None], *[grads[n] for n in WEIGHT_ORDER], *[deltas[n] for n in WEIGHT_ORDER],
            *[new_m[n] for n in WEIGHT_ORDER], *[new_v[n] for n in WEIGHT_ORDER])
```

```python
import functools
import math

import jax
import jax.numpy as jnp
import numpy as np
from jax import lax
from jax.experimental import pallas as pl
from jax.experimental.pallas import tpu as pltpu

F32 = jnp.float32
BF16 = jnp.bfloat16

D_MODEL = 1024
GRID_W = 64
EPS = 1e-6
MLA_HEADS = 16
QK_NOPE_DIM = 64
QK_ROPE_DIM = 32
V_HEAD_DIM = 64
Q_LORA_RANK = 256
KV_LORA_RANK = 128
QK_DIM = QK_NOPE_DIM + QK_ROPE_DIM
SOFTMAX_SCALE = QK_DIM ** -0.5
ROPE_THETA = 10000.0
S5_GROUP = 16
S5_GROUPS = D_MODEL // S5_GROUP
S5_STATE = 64
S5_LANES = S5_GROUPS * S5_STATE
N_SEG = 8
GROUPS_PER_BLOCK = 8
N_BLOCKS = S5_GROUPS // GROUPS_PER_BLOCK
BLK_CH = GROUPS_PER_BLOCK * S5_GROUP
BLK_ST = GROUPS_PER_BLOCK * S5_STATE

ADAM_LR = 0.001
ADAM_B1 = 0.9
ADAM_B2 = 0.999
ADAM_EPS = 1e-08
ADAM_WD = 0.01
ADAM_STEP = 10

N_DEV = 8
N_CHIP = 4
MESH = pl.DeviceIdType.MESH
VMEM_LIMIT = 52 * 1024 * 1024
ROW_TILE = 256


def _params(sem=None, vmem=None):
    return pltpu.CompilerParams(dimension_semantics=sem, vmem_limit_bytes=vmem)


def mm_nn(a, b, out_dtype=F32, name="mm_nn", b_blk=0):
    M, K = a.shape
    N = b.shape[1]
    tm = math.gcd(ROW_TILE, M)

    def body(a_ref, b_ref, o_ref):
        o_ref[...] = jnp.dot(a_ref[...].astype(BF16), b_ref[...].astype(BF16),
                             preferred_element_type=F32).astype(o_ref.dtype)

    return pl.pallas_call(
        body, out_shape=jax.ShapeDtypeStruct((M, N), out_dtype), grid=(M // tm,),
        in_specs=[pl.BlockSpec((tm, K), lambda i: (i, 0)), pl.BlockSpec((K, N), lambda i: (b_blk, 0))],
        out_specs=pl.BlockSpec((tm, N), lambda i: (i, 0)),
        compiler_params=_params(("parallel",), VMEM_LIMIT), name=name)(a, b)


def mm_nt(a, b, out_dtype=F32, name="mm_nt", b_rows=None, b_blk=0):
    M, N = a.shape
    K = b.shape[0] if b_rows is None else b_rows
    tm = math.gcd(ROW_TILE, M)

    def body(a_ref, b_ref, o_ref):
        o_ref[...] = lax.dot_general(a_ref[...].astype(BF16), b_ref[...].astype(BF16),
                                     (((1,), (1,)), ((), ())),
                                     preferred_element_type=F32).astype(o_ref.dtype)

    return pl.pallas_call(
        body, out_shape=jax.ShapeDtypeStruct((M, K), out_dtype), grid=(M // tm,),
        in_specs=[pl.BlockSpec((tm, N), lambda i: (i, 0)), pl.BlockSpec((K, N), lambda i: (b_blk, 0))],
        out_specs=pl.BlockSpec((tm, K), lambda i: (i, 0)),
        compiler_params=_params(("parallel",), VMEM_LIMIT), name=name)(a, b)


def mm_tn(a, b, name="mm_tn"):
    M, K = a.shape
    N = b.shape[1]
    tn = math.gcd(512, N) if N % 128 == 0 and N > 512 else N

    def body(a_ref, b_ref, o_ref):
        o_ref[...] = lax.dot_general(a_ref[...].astype(BF16), b_ref[...].astype(BF16),
                                     (((0,), (0,)), ((), ())), preferred_element_type=F32)

    return pl.pallas_call(
        body, out_shape=jax.ShapeDtypeStruct((K, N), F32), grid=(N // tn,),
        in_specs=[pl.BlockSpec((M, K), lambda j: (0, 0)), pl.BlockSpec((M, tn), lambda j: (0, j))],
        out_specs=pl.BlockSpec((K, tn), lambda j: (0, j)),
        compiler_params=_params(("parallel",), VMEM_LIMIT), name=name)(a, b)


class Rows:
    def __init__(self, arr, width=None, row_off=0, col_blk=0):
        self.arr = arr
        self.width = arr.shape[1] if width is None else width
        self.row_off = row_off
        self.col_blk = col_blk

    def spec(self, tm):
        ro, cb = self.row_off // tm, self.col_blk
        return pl.BlockSpec((tm, self.width), lambda i: (i + ro, cb))


def _as_rows(x):
    return x if isinstance(x, Rows) else Rows(x)


def _row_tile(n_rows, n_ctx_rows, rows):
    tm = math.gcd(ROW_TILE, n_rows, n_ctx_rows)
    for r in rows:
        tm = math.gcd(tm, r.row_off)
    return tm


def _bc_spec(arr, n_ctx_blocks):
    g, _, d = arr.shape
    if g == 1:
        return pl.BlockSpec((1, 1, d), lambda i: (0, 0, 0))
    return pl.BlockSpec((1, 1, d), lambda i: ((i >= n_ctx_blocks).astype(jnp.int32), 0, 0))


def rowwise_fwd(fn, rows, bcs, out_dims, out_dtypes, n_rows, n_ctx_rows, name):
    rows = [_as_rows(r) for r in rows]
    tm = _row_tile(n_rows, n_ctx_rows, rows)
    ncb = n_ctx_rows // tm
    nr, nb = len(rows), len(bcs)

    def body(*refs):
        vals = [r[...].astype(F32) for r in refs[:nr]] + [b[0].astype(F32) for b in refs[nr:nr + nb]]
        outs = fn(*vals)
        for o_ref, v in zip(refs[nr + nb:], outs):
            o_ref[...] = v.astype(o_ref.dtype)

    outs = pl.pallas_call(
        body,
        out_shape=[jax.ShapeDtypeStruct((n_rows, d), dt) for d, dt in zip(out_dims, out_dtypes)],
        grid=(n_rows // tm,),
        in_specs=[r.spec(tm) for r in rows] + [_bc_spec(b, ncb) for b in bcs],
        out_specs=[pl.BlockSpec((tm, d), lambda i: (i, 0)) for d in out_dims],
        compiler_params=_params(("parallel",), VMEM_LIMIT), name=name)(*[r.arr for r in rows], *bcs)
    return outs


def rowwise_bwd(fn, rows, bcs, cts, diff_rows, diff_bcs, n_rows, n_ctx_rows, name, ct_extra=None, lat_add=None):
    rows = [_as_rows(r) for r in rows]
    cts = [_as_rows(c) for c in cts]
    extra = [_as_rows(ct_extra)] if ct_extra is not None else []
    tm = _row_tile(n_rows, n_ctx_rows, rows + cts + extra)
    ncb = n_ctx_rows // tm
    nr, nb, nc = len(rows), len(bcs), len(cts)
    ndr, ndb = len(diff_rows), len(diff_bcs)
    n_in = nr + nb + nc + len(extra) + (lat_add is not None)

    def body(*refs):
        i = pl.program_id(0)
        rvals = [r[...].astype(F32) for r in refs[:nr]]
        bvals = [b[0].astype(F32) for b in refs[nr:nr + nb]]
        cvals = [c[...].astype(F32) for c in refs[nr + nb:nr + nb + nc]]
        if extra:
            cvals[0] = cvals[0] + refs[nr + nb + nc][...].astype(F32)
        outs = refs[n_in:]

        def f(*d):
            rv, bv = list(rvals), list(bvals)
            for k, idx in enumerate(diff_rows):
                rv[idx] = d[k]
            for k, idx in enumerate(diff_bcs):
                bv[idx] = d[ndr + k]
            return tuple(fn(*rv, *bv))

        primals = [rvals[k] for k in diff_rows] + [bvals[k] for k in diff_bcs]
        _, vjp = jax.vjp(f, *primals)
        grads = list(vjp(tuple(cvals)))
        if lat_add is not None:
            add = refs[n_in - 1][...]
            grads[0] = grads[0] + (add if lat_add.shape[0] == n_rows else jnp.where(i >= ncb, add, 0.0))
        for k in range(ndr):
            outs[k][...] = grads[k].astype(outs[k].dtype)
        for k, idx in enumerate(diff_bcs):
            o_ref = outs[ndr + k]
            first = (i == 0)
            if bcs[idx].shape[0] == 2:
                first = first | (i == ncb)

            @pl.when(first)
            def _(o_ref=o_ref):
                o_ref[...] = jnp.zeros_like(o_ref)

            o_ref[0] += grads[ndr + k]

    out_shape = [jax.ShapeDtypeStruct((n_rows, rows[k].width), F32) for k in diff_rows]
    out_shape += [jax.ShapeDtypeStruct(bcs[k].shape, F32) for k in diff_bcs]
    out_specs = [pl.BlockSpec((tm, rows[k].width), lambda i: (i, 0)) for k in diff_rows]
    out_specs += [_bc_spec(bcs[k], ncb) for k in diff_bcs]
    ins = [r.arr for r in rows] + list(bcs) + [c.arr for c in cts + extra]
    in_specs = [r.spec(tm) for r in rows] + [_bc_spec(b, ncb) for b in bcs] + [c.spec(tm) for c in cts + extra]
    if lat_add is not None:
        ins.append(lat_add)
        skip = ncb if lat_add.shape[0] != n_rows else 0
        in_specs.append(pl.BlockSpec((tm, lat_add.shape[1]), lambda i: (jnp.maximum(i - skip, 0), 0)))
    outs = pl.pallas_call(
        body, out_shape=out_shape, grid=(n_rows // tm,), in_specs=in_specs, out_specs=out_specs,
        compiler_params=_params(("arbitrary",), VMEM_LIMIT), name=name)(*ins)
    return outs


def _rms(x):
    return x * lax.rsqrt(jnp.mean(x * x, axis=-1, keepdims=True) + EPS)


def _sigmoid(x):
    return 0.5 * (jnp.tanh(0.5 * x) + 1.0)


def _silu(x):
    return x * _sigmoid(x)


def _gelu_tanh(x):
    return 0.5 * x * (1.0 + jnp.tanh(math.sqrt(2.0 / math.pi) * (x + 0.044715 * (x * x * x))))


def f_norm_mod(x, g, sc, sh):
    return ((_rms(x) * g) * (1.0 + sc) + sh,)


def f_rms(x, g):
    return (_rms(x) * g,)


def f_gate(o, z):
    return (o * _silu(z),)


def f_s5_act(y, u, d):
    return (_gelu_tanh(y + d * u),)


def f_s5_glu(ya, gl, z, b):
    return (ya * _sigmoid(gl + b) * _silu(z),)


def norm_proj(x, g, sc, sh, w, n_ctx, name):
    n, d = x.shape
    nw = w.shape[1]
    tm = math.gcd(ROW_TILE, n, n_ctx)
    ncb = n_ctx // tm

    def body(x_ref, g_ref, sc_ref, sh_ref, w_ref, h_ref, p_ref):
        h = f_norm_mod(x_ref[...], g_ref[0], sc_ref[0], sh_ref[0])[0].astype(BF16)
        h_ref[...] = h
        p_ref[...] = jnp.dot(h, w_ref[...], preferred_element_type=F32)

    row = pl.BlockSpec((tm, d), lambda i: (i, 0))
    return pl.pallas_call(
        body, out_shape=[jax.ShapeDtypeStruct((n, d), BF16), jax.ShapeDtypeStruct((n, nw), F32)], grid=(n // tm,),
        in_specs=[row, _bc_spec(g, ncb), _bc_spec(sc, ncb), _bc_spec(sh, ncb), pl.BlockSpec(w.shape, lambda i: (0, 0))],
        out_specs=[row, pl.BlockSpec((tm, nw), lambda i: (i, 0))],
        compiler_params=_params(("parallel",), VMEM_LIMIT), name=name)(x, g, sc, sh, w)


def norm_proj_bwd(terms, w, x, g, sc, sh, add, n_ctx, name):
    n, d = x.shape
    tm = math.gcd(ROW_TILE, n, n_ctx, *[t[2] for t in terms])
    ncb = n_ctx // tm
    nt = len(terms)
    add_skip = ncb if add.shape[0] != n else 0

    def body(*refs):
        i = pl.program_id(0)
        a_refs = refs[:nt]
        w_ref, x_ref, g_ref, sc_ref, sh_ref, add_ref = refs[nt:nt + 6]
        dx_ref, dg_ref, dsc_ref, dsh_ref = refs[nt + 6:]
        d_h = None
        for a_ref, (a, off, first) in zip(a_refs, terms):
            part = lax.dot_general(a_ref[...].astype(BF16), w_ref[:, off:off + a.shape[1]], NT_DIMS,
                                   preferred_element_type=F32)
            if first:
                part = jnp.where(i >= first // tm, part, 0.0)
            d_h = part if d_h is None else d_h + part
        _, vjp = jax.vjp(lambda x_, g_, sc_, sh_: f_norm_mod(x_, g_, sc_, sh_), x_ref[...], g_ref[0], sc_ref[0],
                         sh_ref[0])
        d_x, d_g, d_sc, d_sh = vjp((d_h,))
        extra = add_ref[...]
        dx_ref[...] = d_x + (extra if add_skip == 0 else jnp.where(i >= ncb, extra, 0.0))

        @pl.when(i == 0)
        def _():
            dg_ref[...] = jnp.zeros_like(dg_ref)

        @pl.when((i == 0) | (i == ncb))
        def _():
            dsc_ref[...] = jnp.zeros_like(dsc_ref)
            dsh_ref[...] = jnp.zeros_like(dsh_ref)

        dg_ref[0] += d_g
        dsc_ref[0] += d_sc
        dsh_ref[0] += d_sh

    def a_spec(a, first):
        skip = first // tm
        return pl.BlockSpec((tm, a.shape[1]), lambda i: (jnp.maximum(i - skip, 0), 0))

    row = pl.BlockSpec((tm, d), lambda i: (i, 0))
    return pl.pallas_call(
        body,
        out_shape=[jax.ShapeDtypeStruct((n, d), F32), jax.ShapeDtypeStruct(g.shape, F32),
                   jax.ShapeDtypeStruct(sc.shape, F32), jax.ShapeDtypeStruct(sh.shape, F32)],
        grid=(n // tm,),
        in_specs=[a_spec(a, first) for a, _, first in terms]
        + [pl.BlockSpec(w.shape, lambda i: (0, 0)), row, _bc_spec(g, ncb), _bc_spec(sc, ncb), _bc_spec(sh, ncb),
           pl.BlockSpec((tm, d), lambda i: (jnp.maximum(i - add_skip, 0), 0))],
        out_specs=[row, _bc_spec(g, ncb), _bc_spec(sc, ncb), _bc_spec(sh, ncb)],
        compiler_params=_params(("arbitrary",), VMEM_LIMIT), name=name)(*[t[0] for t in terms], w, x, g, sc, sh, add)


def mla_post_fwd(o, p0, x0, gate, w_out, n_ctx, name="l0_post"):
    n, d = o.shape
    tm = math.gcd(ROW_TILE, n, n_ctx)
    ncb = n_ctx // tm

    def body(o_ref, z_ref, x_ref, gt_ref, w_ref, x1_ref, og_ref, out_ref):
        og = f_gate(o_ref[...], z_ref[...])[0].astype(BF16)
        out = jnp.dot(og, w_ref[...], preferred_element_type=F32)
        og_ref[...] = og
        out_ref[...] = out
        x1_ref[...] = x_ref[...] + gt_ref[0] * out

    row = pl.BlockSpec((tm, d), lambda i: (i, 0))
    return pl.pallas_call(
        body, out_shape=[jax.ShapeDtypeStruct((n, d), F32), jax.ShapeDtypeStruct((n, d), BF16),
                         jax.ShapeDtypeStruct((n, d), F32)],
        grid=(n // tm,),
        in_specs=[row, row, row, _bc_spec(gate, ncb), pl.BlockSpec((d, d), lambda i: (0, 0))],
        out_specs=[row, row, row],
        compiler_params=_params(("parallel",), VMEM_LIMIT), name=name)(o, p0, x0, gate, w_out)


def mla_post_bwd(dx1, out, og, o, p0, gate, w_out, n_ctx, name="l0_post_bwd"):
    n, d = o.shape
    tm = math.gcd(ROW_TILE, n, n_ctx)
    ncb = n_ctx // tm

    def body(dx_ref, out_ref, og_ref, o_ref, z_ref, gt_ref, w_ref, do_ref, dz_ref, dgt_ref, dw_ref):
        i = pl.program_id(0)

        @pl.when(i == 0)
        def _():
            dw_ref[...] = jnp.zeros_like(dw_ref)

        @pl.when((i == 0) | (i == ncb))
        def _():
            dgt_ref[...] = jnp.zeros_like(dgt_ref)

        dx = dx_ref[...]
        dgt_ref[0] += jnp.sum(dx * out_ref[...], axis=0, keepdims=True)
        d_out16 = (gt_ref[0] * dx).astype(BF16)
        dw_ref[...] += lax.dot_general(og_ref[...], d_out16, (((0,), (0,)), ((), ())), preferred_element_type=F32)
        d_og = lax.dot_general(d_out16, w_ref[...], NT_DIMS, preferred_element_type=F32)
        _, gate_vjp = jax.vjp(lambda o_, z_: f_gate(o_, z_), o_ref[...], z_ref[...])
        d_o, d_z = gate_vjp((d_og,))
        do_ref[...] = d_o
        dz_ref[...] = d_z

    row = pl.BlockSpec((tm, d), lambda i: (i, 0))
    mat = pl.BlockSpec((d, d), lambda i: (0, 0))
    return pl.pallas_call(
        body, out_shape=[jax.ShapeDtypeStruct((n, d), F32), jax.ShapeDtypeStruct((n, d), F32),
                         jax.ShapeDtypeStruct(gate.shape, F32), jax.ShapeDtypeStruct((d, d), F32)],
        grid=(n // tm,),
        in_specs=[row, row, row, row, row, _bc_spec(gate, ncb), mat],
        out_specs=[row, row, _bc_spec(gate, ncb), mat],
        compiler_params=_params(("arbitrary",), VMEM_LIMIT), name=name)(dx1, out, og, o, p0, gate, w_out)


def s5_tail(y_ssm, p1, x1p, target, n_ctx, d_vec, b_glu, gate, final_g, w_glu, w_out, name="l1_tail"):
    n, d = y_ssm.shape
    tm = math.gcd(ROW_TILE, n, n_ctx)
    off = n_ctx // tm
    tn_dims = (((0,), (0,)), ((), ()))

    def row_loss(x, g, t):
        e = _rms(x) * g - t
        return 0.5 * (e * e) * (1.0 / d)

    def body(y_ref, u_ref, z_ref, x1_ref, t_ref, d_ref, b_ref, gt_ref, fg_ref, wg_ref, wo_ref,
             l_ref, dx_ref, dy_ref, du_ref, dz_ref, dfg_ref, dgt_ref, db_ref, dd_ref, dwg_ref, dwo_ref):
        @pl.when(pl.program_id(0) == 0)
        def _():
            for r in (l_ref, dfg_ref, dgt_ref, db_ref, dd_ref, dwg_ref, dwo_ref):
                r[...] = jnp.zeros_like(r)

        u, z, tgt, gt = u_ref[...], z_ref[...], t_ref[...], gt_ref[...]
        (ya,), act_vjp = jax.vjp(lambda y_, u_, d_: f_s5_act(y_, u_, d_), y_ref[...], u, d_ref[...])
        ya16 = ya.astype(BF16)
        gl = jnp.dot(ya16, wg_ref[...], preferred_element_type=F32)
        (y3,), glu_vjp = jax.vjp(lambda a_, g_, z_, b_: f_s5_glu(a_, g_, z_, b_), ya, gl, z, b_ref[...])
        y3_16 = y3.astype(BF16)
        out1 = jnp.dot(y3_16, wo_ref[...], preferred_element_type=F32)
        lterm, loss_vjp = jax.vjp(lambda x_, g_: row_loss(x_, g_, tgt), x1_ref[...] + gt * out1, fg_ref[...])
        dx2, dfg = loss_vjp(jnp.ones_like(lterm))
        l_ref[...] += jnp.sum(lterm, axis=0, keepdims=True)
        dfg_ref[...] += dfg
        dx_ref[...] = dx2
        dgt_ref[...] += jnp.sum(dx2 * out1, axis=0, keepdims=True)
        d_out16 = (gt * dx2).astype(BF16)
        dwo_ref[...] += lax.dot_general(y3_16, d_out16, tn_dims, preferred_element_type=F32)
        d_y3 = lax.dot_general(d_out16, wo_ref[...], NT_DIMS, preferred_element_type=F32)
        d_ya, d_gl, d_z, d_b = glu_vjp((d_y3,))
        dz_ref[...] = d_z
        db_ref[...] += d_b
        d_gl16 = d_gl.astype(BF16)
        dwg_ref[...] += lax.dot_general(ya16, d_gl16, tn_dims, preferred_element_type=F32)
        d_ya = d_ya + lax.dot_general(d_gl16, wg_ref[...], NT_DIMS, preferred_element_type=F32)
        d_y, d_u, d_d = act_vjp((d_ya,))
        dy_ref[...] = d_y
        du_ref[...] = d_u
        dd_ref[...] += d_d

    row = pl.BlockSpec((tm, d), lambda i: (i, 0))
    vecs = pl.BlockSpec((1, d), lambda i: (0, 0))
    mat = pl.BlockSpec((d, d), lambda i: (0, 0))
    return pl.pallas_call(
        body,
        out_shape=[jax.ShapeDtypeStruct((1, d), F32)] + [jax.ShapeDtypeStruct((n, d), F32)] * 4
        + [jax.ShapeDtypeStruct((1, d), F32)] * 4 + [jax.ShapeDtypeStruct((d, d), F32)] * 2,
        grid=(n // tm,),
        in_specs=[row, pl.BlockSpec((tm, d), lambda i: (i + off, 0)), pl.BlockSpec((tm, d), lambda i: (i + off, 1)),
                  pl.BlockSpec((tm, d), lambda i: (i + off, 0)), row, vecs, vecs, vecs, vecs, mat, mat],
        out_specs=[vecs, row, row, row, row, vecs, vecs, vecs, vecs, mat, mat],
        compiler_params=_params(("arbitrary",), VMEM_LIMIT), name=name)(
            y_ssm, p1, p1, x1p, target, d_vec, b_glu, gate, final_g, w_glu, w_out)


NT_DIMS = (((1,), (1,)), ((), ()))
HEAD_LANES = 128
N_PAIRS = MLA_HEADS // 2


def _own_lanes(shape, hh):
    lane = lax.broadcasted_iota(jnp.int32, shape, len(shape) - 1)
    return (lane < V_HEAD_DIM) if hh == 0 else (lane >= V_HEAD_DIM)


def _delta_lane(hh):
    return V_HEAD_DIM if hh == 0 else 0


def _rope_tiles(x, cos, sin_next, sin_prev, inverse):
    width = x.shape[-1]
    reps = width // HEAD_LANES
    c, sn, sp = (jnp.tile(t, (1, reps)) for t in (cos, sin_next, sin_prev))
    if inverse:
        return x * c + pltpu.roll(x * sn, 8, 1) + pltpu.roll(x * sp, width - 8, 1)
    return x * c + pltpu.roll(x, width - 8, 1) * sn + pltpu.roll(x, 8, 1) * sp


STAT_LANE = QK_DIM


def _with_stat(x16, col, lane0):
    hi = col.astype(BF16)
    r1 = col - hi.astype(F32)
    mid = r1.astype(BF16)
    lo = (r1 - mid.astype(F32)).astype(BF16)
    lane = lax.broadcasted_iota(jnp.int32, x16.shape, 1)
    return jnp.where(lane == lane0, hi, jnp.where(lane == lane0 + 1, mid, jnp.where(lane == lane0 + 2, lo, x16)))


def attn_fwd(qb, kb, vb, n_ctx):
    T = qb.shape[0]
    tq = math.gcd(ROW_TILE, n_ctx)
    nq, ncb = T // tq, n_ctx // tq

    def body(q_ref, k_ref, v_ref, o_ref, lse_ref, qs_ref):
        qi = pl.program_id(1)

        def rows(n_keys):
            v = v_ref[:n_keys, :]
            outs = []
            for hh in range(2):
                hs = slice(hh * HEAD_LANES, (hh + 1) * HEAD_LANES)
                s = lax.dot_general(q_ref[:, hs], k_ref[:n_keys, hs], NT_DIMS,
                                    preferred_element_type=F32) * SOFTMAX_SCALE
                m = jnp.max(s, axis=-1, keepdims=True)
                p = jnp.exp(s - m)
                l = jnp.sum(p, axis=-1, keepdims=True)
                outs.append(jnp.dot(p.astype(BF16), v, preferred_element_type=F32) / l)
                lse = m + jnp.log(l)
                lse_ref[hh] = lse
                qs_ref[:, hs] = _with_stat(q_ref[:, hs], lse * (-1.0 / SOFTMAX_SCALE), STAT_LANE)
            o_ref[...] = jnp.where(_own_lanes(outs[0].shape, 0), outs[0], outs[1])

        pl.when(qi < ncb)(lambda: rows(n_ctx))
        pl.when(qi >= ncb)(lambda: rows(T))

    return pl.pallas_call(
        body,
        out_shape=[jax.ShapeDtypeStruct((T, MLA_HEADS * V_HEAD_DIM), F32),
                   jax.ShapeDtypeStruct((MLA_HEADS, T, 1), F32), jax.ShapeDtypeStruct(qb.shape, BF16)],
        grid=(N_PAIRS, nq),
        in_specs=[pl.BlockSpec((tq, 2 * HEAD_LANES), lambda h, i: (i, h)),
                  pl.BlockSpec((T, 2 * HEAD_LANES), lambda h, i: (0, h)),
                  pl.BlockSpec((T, 2 * V_HEAD_DIM), lambda h, i: (0, h))],
        out_specs=[pl.BlockSpec((tq, 2 * V_HEAD_DIM), lambda h, i: (i, h)),
                   pl.BlockSpec((2, tq, 1), lambda h, i: (h, i, 0)),
                   pl.BlockSpec((tq, 2 * HEAD_LANES), lambda h, i: (i, h))],
        compiler_params=_params(("parallel", "parallel"), VMEM_LIMIT), name="attn_fwd")(qb, kb, vb)


def attn_bwd_dq(qb, kb, vb, o, do, lse, tabs, n_ctx):
    T = qb.shape[0]
    tq = math.gcd(ROW_TILE, n_ctx)
    nq, ncb = T // tq, n_ctx // tq

    def body(q_ref, k_ref, v_ref, o_ref, do_ref, lse_ref, c_ref, sn_ref, sp_ref, dq_ref, dos_ref):
        qi = pl.program_id(1)

        def rows(n_keys):
            v = v_ref[:n_keys, :]
            dqs = []
            for hh in range(2):
                hs = slice(hh * HEAD_LANES, (hh + 1) * HEAD_LANES)
                k = k_ref[:n_keys, hs]
                do = jnp.where(_own_lanes(do_ref.shape, hh), do_ref[...], 0.0)
                delta = jnp.sum(do * o_ref[...], axis=-1, keepdims=True)
                s = lax.dot_general(q_ref[:, hs], k, NT_DIMS, preferred_element_type=F32) * SOFTMAX_SCALE
                p = jnp.exp(s - lse_ref[hh])
                do16 = do.astype(BF16)
                dp = lax.dot_general(do16, v, NT_DIMS, preferred_element_type=F32)
                ds = p * (dp - delta) * SOFTMAX_SCALE
                dqs.append(jnp.dot(ds.astype(BF16), k, preferred_element_type=F32))
                dos_ref[:, hs] = _with_stat(do16, delta, _delta_lane(hh))
            dq = jnp.concatenate(dqs, axis=1)
            dq_ref[...] = _rope_tiles(dq, c_ref[...], sn_ref[...], sp_ref[...], True).astype(BF16)

        pl.when(qi < ncb)(lambda: rows(n_ctx))
        pl.when(qi >= ncb)(lambda: rows(T))

    tab = pl.BlockSpec((tq, HEAD_LANES), lambda h, i: (i, 0))
    return pl.pallas_call(
        body,
        out_shape=[jax.ShapeDtypeStruct((T, MLA_HEADS * HEAD_LANES), BF16)] * 2,
        grid=(N_PAIRS, nq),
        in_specs=[pl.BlockSpec((tq, 2 * HEAD_LANES), lambda h, i: (i, h)),
                  pl.BlockSpec((T, 2 * HEAD_LANES), lambda h, i: (0, h)),
                  pl.BlockSpec((T, 2 * V_HEAD_DIM), lambda h, i: (0, h)),
                  pl.BlockSpec((tq, 2 * V_HEAD_DIM), lambda h, i: (i, h)),
                  pl.BlockSpec((tq, 2 * V_HEAD_DIM), lambda h, i: (i, h)),
                  pl.BlockSpec((2, tq, 1), lambda h, i: (h, i, 0)), tab, tab, tab],
        out_specs=[pl.BlockSpec((tq, 2 * HEAD_LANES), lambda h, i: (i, h))] * 2,
        compiler_params=_params(("parallel", "parallel"), VMEM_LIMIT), name="attn_bwd_dq")(
            qb, kb, vb, o, do, lse, *tabs)


def attn_bwd_dkv(qs, kb, vb, dos, n_ctx):
    T = qs.shape[0]
    tq = math.gcd(ROW_TILE, n_ctx)
    nq, ncb = T // tq, n_ctx // tq

    def body(q_ref, do_ref, k_ref, v_ref, dk_ref, dv_ref):
        kj = pl.program_id(1)

        def cols(first):
            v = v_ref[...]
            lane = lax.broadcasted_iota(jnp.int32, v.shape, 1)
            dvs = []
            for hh in range(2):
                hs = slice(hh * HEAD_LANES, (hh + 1) * HEAD_LANES)
                q = q_ref[first:, hs]
                do16 = do_ref[first:, hs]
                in_delta = (lane >= _delta_lane(hh)) & (lane < _delta_lane(hh) + 3)
                v_minus = jnp.where(in_delta, -jnp.ones_like(v), v)
                pt = jnp.exp(lax.dot_general(k_ref[:, hs], q, NT_DIMS, preferred_element_type=F32) * SOFTMAX_SCALE)
                dvs.append(jnp.dot(pt.astype(BF16), do16, preferred_element_type=F32))
                dst = pt * lax.dot_general(v_minus, do16, NT_DIMS, preferred_element_type=F32) * SOFTMAX_SCALE
                dk_ref[:, hs] = jnp.dot(dst.astype(BF16), q, preferred_element_type=F32)
            dv_ref[...] = jnp.where(_own_lanes(dvs[0].shape, 0), dvs[0], dvs[1])

        pl.when(kj < ncb)(lambda: cols(0))
        pl.when(kj >= ncb)(lambda: cols(n_ctx))

    return pl.pallas_call(
        body,
        out_shape=[jax.ShapeDtypeStruct((T, MLA_HEADS * HEAD_LANES), F32),
                   jax.ShapeDtypeStruct((T, MLA_HEADS * V_HEAD_DIM), F32)],
        grid=(N_PAIRS, nq),
        in_specs=[pl.BlockSpec((T, 2 * HEAD_LANES), lambda h, j: (0, h)),
                  pl.BlockSpec((T, 2 * HEAD_LANES), lambda h, j: (0, h)),
                  pl.BlockSpec((tq, 2 * HEAD_LANES), lambda h, j: (j, h)),
                  pl.BlockSpec((tq, 2 * V_HEAD_DIM), lambda h, j: (j, h))],
        out_specs=[pl.BlockSpec((tq, 2 * HEAD_LANES), lambda h, j: (j, h)),
                   pl.BlockSpec((tq, 2 * V_HEAD_DIM), lambda h, j: (j, h))],
        compiler_params=_params(("parallel", "parallel"), VMEM_LIMIT), name="attn_bwd_dkv")(
            qs, dos, kb, vb)


def _split_bf16(x):
    hi = x.astype(BF16)
    return hi, (x - hi.astype(F32)).astype(BF16)


def q_heads(cq, gain, w_uq_p, tabs, name="l0_uq"):
    T, K = cq.arr.shape[0], cq.width
    N = w_uq_p.shape[1]
    tm = math.gcd(ROW_TILE, T)

    def body(a_ref, g_ref, w_ref, c_ref, sn_ref, sp_ref, o_ref, n_ref):
        qn = f_rms(a_ref[...], g_ref[0])[0].astype(BF16)
        n_ref[...] = qn
        acc = jnp.dot(qn, w_ref[...], preferred_element_type=F32)
        o_ref[...] = _rope_tiles(acc, c_ref[...], sn_ref[...], sp_ref[...], False).astype(BF16)

    tab = pl.BlockSpec((tm, HEAD_LANES), lambda i: (i, 0))
    return pl.pallas_call(
        body, out_shape=[jax.ShapeDtypeStruct((T, N), BF16), jax.ShapeDtypeStruct((T, K), BF16)], grid=(T // tm,),
        in_specs=[cq.spec(tm), pl.BlockSpec((1, 1, K), lambda i: (0, 0, 0)), pl.BlockSpec((K, N), lambda i: (0, 0)),
                  tab, tab, tab],
        out_specs=[pl.BlockSpec((tm, N), lambda i: (i, 0)), pl.BlockSpec((tm, K), lambda i: (i, 0))],
        compiler_params=_params(("parallel",), VMEM_LIMIT), name=name)(cq.arr, gain, w_uq_p, *tabs)


def kv_heads(ckv, gain, w_kn_p, w_v, kr, spread, tabs, name="l0_ukv"):
    T, K = ckv.arr.shape[0], ckv.width
    N = w_kn_p.shape[1]
    NV = w_v.shape[1]
    tm = math.gcd(ROW_TILE, T)

    def body(a_ref, g_ref, wk_ref, wv_ref, kr_ref, e_ref, c_ref, sn_ref, sp_ref, k_ref, v_ref, n_ref):
        a = f_rms(a_ref[...], g_ref[0])[0].astype(BF16)
        n_ref[...] = a
        hi, lo = _split_bf16(kr_ref[...])
        acc = (jnp.dot(a, wk_ref[...], preferred_element_type=F32)
               + jnp.dot(hi, e_ref[...], preferred_element_type=F32)
               + jnp.dot(lo, e_ref[...], preferred_element_type=F32))
        roped = _rope_tiles(acc, c_ref[...], sn_ref[...], sp_ref[...], False)
        lane = lax.broadcasted_iota(jnp.int32, roped.shape, 1) % HEAD_LANES
        k_ref[...] = jnp.where((lane >= STAT_LANE) & (lane < STAT_LANE + 3), 1.0, roped).astype(BF16)
        v_ref[...] = jnp.dot(a, wv_ref[...], preferred_element_type=F32).astype(BF16)

    tab = pl.BlockSpec((tm, HEAD_LANES), lambda i: (i, 0))
    return pl.pallas_call(
        body, out_shape=[jax.ShapeDtypeStruct((T, N), BF16), jax.ShapeDtypeStruct((T, NV), BF16),
                         jax.ShapeDtypeStruct((T, K), BF16)], grid=(T // tm,),
        in_specs=[ckv.spec(tm), pl.BlockSpec((1, 1, K), lambda i: (0, 0, 0)), pl.BlockSpec((K, N), lambda i: (0, 0)),
                  pl.BlockSpec((K, NV), lambda i: (0, 0)), pl.BlockSpec((tm, QK_ROPE_DIM), lambda i: (i, 0)),
                  pl.BlockSpec((QK_ROPE_DIM, N), lambda i: (0, 0)), tab, tab, tab],
        out_specs=[pl.BlockSpec((tm, N), lambda i: (i, 0)), pl.BlockSpec((tm, NV), lambda i: (i, 0)),
                   pl.BlockSpec((tm, K), lambda i: (i, 0))],
        compiler_params=_params(("parallel",), VMEM_LIMIT), name=name)(ckv.arr, gain, w_kn_p, w_v, kr, spread, *tabs)


def heads_unrope(d, tabs, spread=None, name="unrope"):
    T, N = d.shape
    tm = math.gcd(ROW_TILE, T)

    def body(*refs):
        if spread is None:
            d_ref, c_ref, sn_ref, sp_ref, o_ref = refs
        else:
            d_ref, c_ref, sn_ref, sp_ref, e_ref, o_ref, kr_ref = refs
        g = _rope_tiles(d_ref[...], c_ref[...], sn_ref[...], sp_ref[...], True)
        o_ref[...] = g.astype(BF16)
        if spread is not None:
            hi, lo = _split_bf16(g)
            kr_ref[...] = (lax.dot_general(hi, e_ref[...], NT_DIMS, preferred_element_type=F32)
                           + lax.dot_general(lo, e_ref[...], NT_DIMS, preferred_element_type=F32))

    tab = pl.BlockSpec((tm, HEAD_LANES), lambda i: (i, 0))
    row = pl.BlockSpec((tm, N), lambda i: (i, 0))
    ins, in_specs = [d, *tabs], [row, tab, tab, tab]
    out_shape, out_specs = [jax.ShapeDtypeStruct((T, N), BF16)], [row]
    if spread is not None:
        ins.append(spread)
        in_specs.append(pl.BlockSpec(spread.shape, lambda i: (0, 0)))
        out_shape.append(jax.ShapeDtypeStruct((T, spread.shape[0]), F32))
        out_specs.append(pl.BlockSpec((tm, spread.shape[0]), lambda i: (i, 0)))
    return pl.pallas_call(
        body, out_shape=out_shape, grid=(T // tm,), in_specs=in_specs, out_specs=out_specs,
        compiler_params=_params(("parallel",), VMEM_LIMIT), name=name)(*ins)


def _cmul(ar, ai, br, bi):
    return ar * br - ai * bi, ar * bi + ai * br


def s5_chain(finals, s0, a, n_steps, reverse, name):
    W = finals.shape[-1]
    first = N_SEG - 1 if reverse else 0

    def body(f_ref, s0_ref, a_ref, c_ref):
        pr, pi = jnp.ones((1, W), F32), jnp.zeros((1, W), F32)
        br, bi = a_ref[0], a_ref[1]
        n = n_steps
        while n:
            if n & 1:
                pr, pi = _cmul(pr, pi, br, bi)
            br, bi = _cmul(br, bi, br, bi)
            n >>= 1
        fr, fi = f_ref[0], f_ref[1]
        row = lax.broadcasted_iota(jnp.int32, (N_SEG, W), 0)
        s0r = jnp.broadcast_to(s0_ref[0], (N_SEG, W))
        s0i = jnp.broadcast_to(s0_ref[1], (N_SEG, W))
        cr = jnp.where(row == first, s0r, 0.0)
        ci = jnp.where(row == first, s0i, 0.0)
        shift = N_SEG - 1 if reverse else 1
        for _ in range(N_SEG - 1):
            mr, mi = _cmul(pr, pi, cr, ci)
            tr = pltpu.roll(fr + mr, shift, 0)
            ti = pltpu.roll(fi + mi, shift, 0)
            cr = jnp.where(row == first, s0r, tr)
            ci = jnp.where(row == first, s0i, ti)
        c_ref[0] = cr
        c_ref[1] = ci

    return pl.pallas_call(body, out_shape=jax.ShapeDtypeStruct((2, N_SEG, W), F32), name=name)(finals, s0, a)


def _scan_chunk(bur, bui, st_ref, a_ref, n_steps, reverse):
    for lc in range(S5_LANES // BLK_ST):
        sl = slice(lc * BLK_ST, (lc + 1) * BLK_ST)
        lr = jnp.broadcast_to(a_ref[0, :, sl], (N_SEG, BLK_ST))
        li = jnp.broadcast_to(a_ref[1, :, sl], (N_SEG, BLK_ST))

        def step(jj, carry, sl=sl, lr=lr, li=li):
            sr, si = carry
            j = (n_steps - 1 - jj) if reverse else jj
            r0 = pl.multiple_of(j * N_SEG, N_SEG)
            nr = lr * sr - li * si + bur[pl.ds(r0, N_SEG), sl]
            ni = lr * si + li * sr + bui[pl.ds(r0, N_SEG), sl]
            bur[pl.ds(r0, N_SEG), sl] = nr
            bui[pl.ds(r0, N_SEG), sl] = ni
            return nr, ni

        sr, si = lax.fori_loop(0, n_steps, step, (st_ref[0, :, sl], st_ref[1, :, sl]))
        st_ref[0, :, sl] = sr
        st_ref[1, :, sl] = si


def _project_in(x16, w_re, w_im, bur, bui, adjoint):
    for gb in range(N_BLOCKS):
        xb = x16[:, gb * BLK_CH:(gb + 1) * BLK_CH]
        sl = slice(gb * BLK_ST, (gb + 1) * BLK_ST)
        if adjoint:
            dn = (((1,), (1,)), ((), ()))
            bur[:, sl] = lax.dot_general(xb, w_re[gb], dn, preferred_element_type=F32)
            bui[:, sl] = -lax.dot_general(xb, w_im[gb], dn, preferred_element_type=F32)
        else:
            bur[:, sl] = jnp.dot(xb, w_re[gb], preferred_element_type=F32)
            bui[:, sl] = jnp.dot(xb, w_im[gb], preferred_element_type=F32)


def s5_scan(act, w_re, w_im, a, init, *, reverse, adjoint=False, c_re=None, c_im=None, add=None,
            want_ckpt=False, rows=None, name):
    act_off, N = rows if rows is not None else (0, act.shape[0])
    R = math.gcd(ROW_TILE, N, act_off)
    nch, jc = N // R, R // N_SEG
    with_out = c_re is not None

    def chunk(i):
        return (nch - 1 - i) if reverse else i

    def body(*refs):
        act_ref, wre_ref, wim_ref, a_ref, init_ref = refs[:5]
        k = 5
        if with_out:
            cre_ref, cim_ref = refs[k:k + 2]
            k += 2
        if add is not None:
            add_ref = refs[k]
            k += 1
        if with_out:
            out_ref = refs[k]
            k += 1
        if want_ckpt:
            ck_ref = refs[k]
            k += 1
        fin_ref, bur, bui = refs[k:k + 3]

        @pl.when(pl.program_id(0) == 0)
        def _():
            fin_ref[...] = init_ref[...]

        if want_ckpt:
            ck_ref[0] = fin_ref[...]
        _project_in(act_ref[...].astype(BF16), wre_ref, wim_ref, bur, bui, adjoint)
        _scan_chunk(bur, bui, fin_ref, a_ref, jc, reverse)
        if with_out:
            for gb in range(N_BLOCKS):
                sl = slice(gb * BLK_ST, (gb + 1) * BLK_ST)
                y = (jnp.dot(bur[:, sl].astype(BF16), cre_ref[gb], preferred_element_type=F32)
                     - jnp.dot(bui[:, sl].astype(BF16), cim_ref[gb], preferred_element_type=F32))
                cs = slice(gb * BLK_CH, (gb + 1) * BLK_CH)
                if add is not None:
                    y = y + add_ref[:, cs]
                out_ref[:, cs] = y

    row_spec = pl.BlockSpec((R, D_MODEL), lambda i: (chunk(i), 0))
    act_spec = pl.BlockSpec((R, D_MODEL), lambda i: (chunk(i) + act_off // R, 0))
    w_spec = pl.BlockSpec(w_re.shape, lambda i: (0, 0, 0))
    st_spec = pl.BlockSpec((2, N_SEG, S5_LANES), lambda i: (0, 0, 0))
    ins = [act, w_re, w_im, a, init]
    in_specs = [act_spec, w_spec, w_spec, pl.BlockSpec((2, 1, S5_LANES), lambda i: (0, 0, 0)), st_spec]
    if with_out:
        ins += [c_re, c_im]
        in_specs += [pl.BlockSpec(c_re.shape, lambda i: (0, 0, 0))] * 2
    if add is not None:
        ins.append(add)
        in_specs.append(row_spec)
    out_shape, out_specs = [], []
    if with_out:
        out_shape.append(jax.ShapeDtypeStruct((N, D_MODEL), F32))
        out_specs.append(row_spec)
    if want_ckpt:
        out_shape.append(jax.ShapeDtypeStruct((nch, 2, N_SEG, S5_LANES), F32))
        out_specs.append(pl.BlockSpec((1, 2, N_SEG, S5_LANES), lambda i: (chunk(i), 0, 0, 0)))
    out_shape.append(jax.ShapeDtypeStruct((2, N_SEG, S5_LANES), F32))
    out_specs.append(st_spec)
    res = pl.pallas_call(
        body, out_shape=out_shape, grid=(nch,), in_specs=in_specs, out_specs=out_specs,
        scratch_shapes=[pltpu.VMEM((R, S5_LANES), F32), pltpu.VMEM((R, S5_LANES), F32)],
        compiler_params=_params(("arbitrary",), VMEM_LIMIT), name=name)(*ins)
    res = list(res)
    out = res.pop(0) if with_out else None
    ckpt = res.pop(0) if want_ckpt else None
    return out, ckpt, res[0]


def s5_grads(dy, u, ckpt, b_re, b_im, c_re, c_im, lam, init_adj, *, reverse, add=None, u_off=0, name):
    N = dy.shape[0]
    R = math.gcd(ROW_TILE, N, u_off)
    nch, jc = N // R, R // N_SEG
    W = S5_LANES

    def chunk(i):
        return i if reverse else (nch - 1 - i)

    def body(*refs):
        dy_ref, u_ref, ck_ref, bre_ref, bim_ref, cre_ref, cim_ref, lam_ref, init_ref = refs[:9]
        k = 9
        if add is not None:
            add_ref = refs[k]
            k += 1
        du_ref, dlam_ref, dbre_ref, dbim_ref, dcre_ref, dcim_ref, fin_ref = refs[k:k + 7]
        sr_buf, si_buf, er_buf, ei_buf, st_buf = refs[k + 7:k + 12]

        @pl.when(pl.program_id(0) == 0)
        def _():
            fin_ref[...] = init_ref[...]
            dlam_ref[...] = jnp.zeros_like(dlam_ref)
            dbre_ref[...] = jnp.zeros_like(dbre_ref)
            dbim_ref[...] = jnp.zeros_like(dbim_ref)
            dcre_ref[...] = jnp.zeros_like(dcre_ref)
            dcim_ref[...] = jnp.zeros_like(dcim_ref)

        u16 = u_ref[...].astype(BF16)
        dy16 = dy_ref[...].astype(BF16)
        st_buf[...] = ck_ref[0]
        _project_in(u16, bre_ref, bim_ref, sr_buf, si_buf, False)
        _scan_chunk(sr_buf, si_buf, st_buf, lam_ref, jc, reverse)
        _project_in(dy16, cre_ref, cim_ref, er_buf, ei_buf, True)
        for lc in range(W // BLK_ST):
            sl = slice(lc * BLK_ST, (lc + 1) * BLK_ST)
            lr = jnp.broadcast_to(lam_ref[0, :, sl], (N_SEG, BLK_ST))
            li = jnp.broadcast_to(lam_ref[1, :, sl], (N_SEG, BLK_ST))

            def one(r0, spr, spi, carry, sl=sl, lr=lr, li=li):
                gr, gi, ar, ai = carry
                nr = er_buf[pl.ds(r0, N_SEG), sl] + lr * gr + li * gi
                ni = ei_buf[pl.ds(r0, N_SEG), sl] + lr * gi - li * gr
                er_buf[pl.ds(r0, N_SEG), sl] = nr
                ei_buf[pl.ds(r0, N_SEG), sl] = ni
                return nr, ni, ar + spr * nr + spi * ni, ai + spr * ni - spi * nr

            def step(ff, carry, sl=sl, one=one):
                f = jc - 1 - ff
                j = (jc - 1 - f) if reverse else f
                jp = (j + 1) if reverse else (j - 1)
                r0 = pl.multiple_of(j * N_SEG, N_SEG)
                p0 = pl.multiple_of(jp * N_SEG, N_SEG)
                return one(r0, sr_buf[pl.ds(p0, N_SEG), sl], si_buf[pl.ds(p0, N_SEG), sl], carry)

            carry = (fin_ref[0, :, sl], fin_ref[1, :, sl], dlam_ref[0, :, sl], dlam_ref[1, :, sl])
            carry = lax.fori_loop(0, jc - 1, step, carry)
            r_first = (jc - 1) * N_SEG if reverse else 0
            gr, gi, ar, ai = one(r_first, ck_ref[0, 0, :, sl], ck_ref[0, 1, :, sl], carry)
            fin_ref[0, :, sl] = gr
            fin_ref[1, :, sl] = gi
            dlam_ref[0, :, sl] = ar
            dlam_ref[1, :, sl] = ai
        tn = (((0,), (0,)), ((), ()))
        nt = (((1,), (1,)), ((), ()))
        for gb in range(N_BLOCKS):
            sl = slice(gb * BLK_ST, (gb + 1) * BLK_ST)
            cs = slice(gb * BLK_CH, (gb + 1) * BLK_CH)
            gr16 = er_buf[:, sl].astype(BF16)
            gi16 = ei_buf[:, sl].astype(BF16)
            du = (lax.dot_general(gr16, bre_ref[gb], nt, preferred_element_type=F32)
                  + lax.dot_general(gi16, bim_ref[gb], nt, preferred_element_type=F32))
            if add is not None:
                du = du + add_ref[:, cs]
            du_ref[:, cs] = du
            ub, dyb = u16[:, cs], dy16[:, cs]
            dbre_ref[gb] += lax.dot_general(ub, gr16, tn, preferred_element_type=F32)
            dbim_ref[gb] += lax.dot_general(ub, gi16, tn, preferred_element_type=F32)
            dcre_ref[gb] += lax.dot_general(sr_buf[:, sl].astype(BF16), dyb, tn, preferred_element_type=F32)
            dcim_ref[gb] -= lax.dot_general(si_buf[:, sl].astype(BF16), dyb, tn, preferred_element_type=F32)

    row_spec = pl.BlockSpec((R, D_MODEL), lambda i: (chunk(i), 0))
    st_spec = pl.BlockSpec((2, N_SEG, W), lambda i: (0, 0, 0))
    wb_spec = pl.BlockSpec(b_re.shape, lambda i: (0, 0, 0))
    wc_spec = pl.BlockSpec(c_re.shape, lambda i: (0, 0, 0))
    ins = [dy, u, ckpt, b_re, b_im, c_re, c_im, lam, init_adj]
    u_spec = pl.BlockSpec((R, D_MODEL), lambda i: (chunk(i) + u_off // R, 0))
    in_specs = [row_spec, u_spec, pl.BlockSpec((1, 2, N_SEG, W), lambda i: (chunk(i), 0, 0, 0)),
                wb_spec, wb_spec, wc_spec, wc_spec, pl.BlockSpec((2, 1, W), lambda i: (0, 0, 0)), st_spec]
    if add is not None:
        ins.append(add)
        in_specs.append(row_spec)
    out_shape = [jax.ShapeDtypeStruct((N, D_MODEL), F32), jax.ShapeDtypeStruct((2, N_SEG, W), F32),
                 jax.ShapeDtypeStruct(b_re.shape, F32), jax.ShapeDtypeStruct(b_re.shape, F32),
                 jax.ShapeDtypeStruct(c_re.shape, F32), jax.ShapeDtypeStruct(c_re.shape, F32),
                 jax.ShapeDtypeStruct((2, N_SEG, W), F32)]
    out_specs = [row_spec, st_spec, wb_spec, wb_spec, wc_spec, wc_spec, st_spec]
    return pl.pallas_call(
        body, out_shape=out_shape, grid=(nch,), in_specs=in_specs, out_specs=out_specs,
        scratch_shapes=[pltpu.VMEM((R, W), F32) for _ in range(4)] + [pltpu.VMEM((2, N_SEG, W), F32)],
        compiler_params=_params(("arbitrary",), VMEM_LIMIT), name=name)(*ins)


def adamw(w, g, m, v, name="adamw", after=None):
    n, d = w.shape
    lanes = -(-d // 128) * 128
    tm = n
    while tm * lanes * 4 > (1 << 20) and tm % 16 == 0:
        tm //= 2
    c1 = 1.0 - ADAM_B1 ** ADAM_STEP
    c2 = 1.0 - ADAM_B2 ** ADAM_STEP

    def body(w_ref, g_ref, m_ref, v_ref, *rest):
        d_ref, nm_ref, nv_ref = rest[-3:]
        g_ = g_ref[...]
        m_ = ADAM_B1 * m_ref[...] + (1.0 - ADAM_B1) * g_
        v_ = ADAM_B2 * v_ref[...] + (1.0 - ADAM_B2) * (g_ * g_)
        d_ref[...] = -ADAM_LR * ((m_ / c1) / (jnp.sqrt(v_ / c2) + ADAM_EPS) + ADAM_WD * w_ref[...])
        nm_ref[...] = m_
        nv_ref[...] = v_

    spec = pl.BlockSpec((tm, d), lambda i: (i, 0))
    extra = [] if after is None else [after]
    return pl.pallas_call(
        body, out_shape=[jax.ShapeDtypeStruct((n, d), F32)] * 3, grid=(n // tm,),
        in_specs=[spec] * 4 + [pl.BlockSpec(memory_space=pl.ANY)] * len(extra), out_specs=[spec] * 3,
        compiler_params=_params(("parallel",), VMEM_LIMIT), name=name)(w, g, m, v, *extra)


def _coords():
    return lax.axis_index("x"), lax.axis_index("y"), lax.axis_index("c")


def exchange(arrays, out_shapes, remote, local, name, aliases=None):
    n_in, n_out, n_rem, n_loc = len(arrays), len(out_shapes), len(remote), len(local)

    def at(ref, idx):
        return ref if idx is None else ref.at[idx]

    def body(*refs):
        ins, outs = refs[:n_in], refs[n_in:n_in + n_out]
        send_sems, recv_sems, local_sems = refs[n_in + n_out:]
        me = _coords()
        sends, recvs = [], []
        for k, (flip, ii, src_at, oi, dst_at) in enumerate(remote):
            peer = (me[0] ^ flip[0], me[1] ^ flip[1], me[2] ^ flip[2])
            src = at(ins[ii], src_at(me, peer))
            sends.append(pltpu.make_async_remote_copy(
                src_ref=src, dst_ref=at(outs[oi], dst_at(me)), send_sem=send_sems.at[k], recv_sem=recv_sems.at[k],
                device_id=peer, device_id_type=MESH))
            recvs.append(pltpu.make_async_remote_copy(
                src_ref=src, dst_ref=at(outs[oi], dst_at(peer)), send_sem=send_sems.at[k], recv_sem=recv_sems.at[k],
                device_id=peer, device_id_type=MESH))
        locs = [pltpu.make_async_copy(at(ins[ii], src_at(me)), at(outs[oi], dst_at(me)), local_sems.at[k])
                for k, (ii, src_at, oi, dst_at) in enumerate(local)]
        for cp in locs + sends:
            cp.start()
        for cp in recvs:
            cp.wait_recv()
        for cp in sends:
            cp.wait_send()
        for cp in locs:
            cp.wait()

    hbm = pl.BlockSpec(memory_space=pl.ANY)
    return pl.pallas_call(
        body, out_shape=list(out_shapes), in_specs=[hbm] * n_in, out_specs=[hbm] * n_out,
        scratch_shapes=[pltpu.SemaphoreType.DMA((n_rem,)), pltpu.SemaphoreType.DMA((n_rem,)),
                        pltpu.SemaphoreType.DMA((max(n_loc, 1),))],
        input_output_aliases=aliases or {}, name=name)(*arrays)


ALL_FLIPS = [(dx, dy, dc) for dx in (0, 1) for dy in (0, 1) for dc in (0, 1)][1:]
CHIP_FLIPS = [(1, 0, 0), (0, 1, 0), (1, 1, 0)]
CORE_FLIP = (0, 0, 1)


def _dev_index(p):
    return 4 * p[0] + 2 * p[1] + p[2]


def _chip_index(p):
    return 2 * p[0] + p[1]


def _gather(xs, flips, index, n, name):
    arrays = [x[None] for x in xs]
    outs = [jax.ShapeDtypeStruct((n,) + x.shape, x.dtype) for x in xs]
    remote = [(f, a, lambda me, peer: (0,), a, lambda s: (index(s),)) for a in range(len(xs)) for f in flips]
    local = [(a, lambda me: (0,), a, lambda me: (index(me),)) for a in range(len(xs))]
    return exchange(arrays, outs, remote, local, name)


def allgather_devices(x, name):
    return _gather([x], ALL_FLIPS, _dev_index, N_DEV, name)[0]


def allgather_chips(xs, name):
    return _gather(xs, CHIP_FLIPS, _chip_index, N_CHIP, name)


def gather_halves(xs, name):
    n = len(xs)
    nk = n * len(CHIP_FLIPS)

    def body(*refs):
        ins, outs = refs[:n], refs[n:2 * n]
        ici_send, ici_recv, d2d_send, d2d_recv = refs[2 * n:]
        me = _coords()
        sibling = (me[0], me[1], 1 - me[2])
        first, passed, landed = [], [], []
        for a in range(n):
            half = ins[a].shape[0] // 2
            mine = ins[a].at[pl.ds(pl.multiple_of(me[2] * half, 16), half)]
            for j, flip in enumerate(CHIP_FLIPS):
                k = a * len(CHIP_FLIPS) + j
                peer = (me[0] ^ flip[0], me[1] ^ flip[1], me[2])
                first.append(pltpu.make_async_remote_copy(
                    src_ref=mine, dst_ref=outs[a].at[_chip_index(me), me[2]], send_sem=ici_send.at[k],
                    recv_sem=ici_recv.at[k], device_id=peer, device_id_type=MESH))
                arrived = outs[a].at[_chip_index(peer), me[2]]
                landed.append(pltpu.make_async_remote_copy(
                    src_ref=mine, dst_ref=arrived, send_sem=ici_send.at[k], recv_sem=ici_recv.at[k],
                    device_id=peer, device_id_type=MESH))
                passed.append(pltpu.make_async_remote_copy(
                    src_ref=arrived, dst_ref=arrived, send_sem=d2d_send.at[k], recv_sem=d2d_recv.at[k],
                    device_id=sibling, device_id_type=MESH))
        for cp in first:
            cp.start()
        for k in range(nk):
            landed[k].wait_recv()
            passed[k].start()
        for a in range(n):
            for j, flip in enumerate(CHIP_FLIPS):
                k = a * len(CHIP_FLIPS) + j
                peer_chip = _chip_index((me[0] ^ flip[0], me[1] ^ flip[1]))
                from_sibling = outs[a].at[peer_chip, 1 - me[2]]
                pltpu.make_async_remote_copy(
                    src_ref=from_sibling, dst_ref=from_sibling, send_sem=d2d_send.at[k], recv_sem=d2d_recv.at[k],
                    device_id=sibling, device_id_type=MESH).wait_recv()
        for cp in first + passed:
            cp.wait_send()

    hbm = pl.BlockSpec(memory_space=pl.ANY)
    return pl.pallas_call(
        body, out_shape=[jax.ShapeDtypeStruct((N_CHIP, 2, x.shape[0] // 2, x.shape[1]), x.dtype) for x in xs],
        in_specs=[hbm] * n, out_specs=[hbm] * n,
        scratch_shapes=[pltpu.SemaphoreType.DMA((nk,)) for _ in range(4)], name=name)(*xs)


HBM_SPEC = pl.BlockSpec(memory_space=pltpu.HBM)
SEM_SPEC = pl.BlockSpec(memory_space=pltpu.SEMAPHORE)
DATAFLOW = pltpu.SideEffectType.DATAFLOW_SIDE_EFFECTING


def _at(ref, idx):
    return ref if idx is None else ref.at[idx]


def _peer(me, flip):
    return (me[0] ^ flip[0], me[1] ^ flip[1], me[2] ^ flip[2])


def exchange_start(arrays, land_shapes, remote, name, after=None):
    n_in, n_out, nk = len(arrays), len(land_shapes), len(remote)
    after = list(after or [])
    n_after = len(after)

    def body(*refs):
        srcs, lands = refs[:n_in], refs[n_in:n_in + n_out]
        first_out = n_in + n_out + n_after
        send_sems, recv_sems, token = refs[first_out], refs[first_out + 1], refs[-1]
        me = _coords()
        for k, (flip, ii, src_at, oi, dst_at) in enumerate(remote):
            peer = _peer(me, flip)
            pltpu.make_async_remote_copy(
                src_ref=_at(srcs[ii], src_at(me, peer)), dst_ref=_at(lands[oi], dst_at(me)), send_sem=send_sems.at[k],
                recv_sem=recv_sems.at[k], device_id=peer, device_id_type=MESH).start()
        token[...] = jnp.zeros_like(token)

    lands = [lax.empty(s.shape, s.dtype) for s in land_shapes]
    bufs = list(arrays) + lands
    out = pl.pallas_call(
        body, name=name,
        out_shape=(pltpu.SemaphoreType.DMA((nk,)), pltpu.SemaphoreType.DMA((nk,)),
                   *[pltpu.HBM(b.shape, b.dtype) for b in bufs], jax.ShapeDtypeStruct((8, 128), F32)),
        in_specs=[HBM_SPEC] * len(bufs) + [pl.BlockSpec(memory_space=pl.ANY)] * n_after,
        out_specs=(SEM_SPEC, SEM_SPEC, *[HBM_SPEC] * len(bufs), pl.BlockSpec(memory_space=pltpu.VMEM)),
        input_output_aliases={a: 2 + a for a in range(len(bufs))},
        compiler_params=pltpu.CompilerParams(has_side_effects=DATAFLOW),
    )(*[pltpu.with_memory_space_constraint(b, pltpu.HBM) for b in bufs], *after)
    flight = (out[0], out[1], list(out[2:2 + n_in]), list(out[2 + n_in:2 + n_in + n_out]), remote)
    return flight, out[-1]


def exchange_wait(flight, after, name):
    send_sems, recv_sems, arrays, lands, remote = flight
    n_in, n_out = len(arrays), len(lands)

    def body(*refs):
        srcs, lnds = refs[:n_in], refs[n_in:n_in + n_out]
        s_sems, r_sems = refs[n_in + n_out], refs[n_in + n_out + 1]
        me = _coords()
        for k, (flip, ii, src_at, oi, dst_at) in enumerate(remote):
            peer = _peer(me, flip)
            copy = pltpu.make_async_remote_copy(
                src_ref=_at(srcs[ii], src_at(me, peer)), dst_ref=_at(lnds[oi], dst_at(peer)), send_sem=s_sems.at[k],
                recv_sem=r_sems.at[k], device_id=peer, device_id_type=MESH)
            copy.wait_send()
            copy.wait_recv()

    bufs = list(arrays) + list(lands)
    out = pl.pallas_call(
        body, name=name,
        out_shape=tuple(pltpu.HBM(b.shape, b.dtype) for b in bufs),
        in_specs=[HBM_SPEC] * len(bufs) + [SEM_SPEC, SEM_SPEC, pl.BlockSpec(memory_space=pl.ANY)],
        out_specs=tuple([HBM_SPEC] * len(bufs)),
        input_output_aliases={a: a for a in range(len(bufs))},
        compiler_params=pltpu.CompilerParams(has_side_effects=DATAFLOW),
    )(*bufs, send_sems, recv_sems, after)
    return list(out[:n_in]), list(out[n_in:])


def _half_tile(h, cd):
    return h if h * cd * 4 <= (1 << 20) else math.gcd(512, h)


def pair_add(g, got, core, out_dtype, name):
    _, _, h, cd = g.shape
    th = _half_tile(h, cd)

    def body(c_ref, g_ref, got_ref, o_ref):
        o_ref[0] = (g_ref[0, 0] + got_ref[0]).astype(o_ref.dtype)

    return pl.pallas_call(
        body, out_shape=jax.ShapeDtypeStruct((N_CHIP, h, cd), out_dtype),
        grid_spec=pltpu.PrefetchScalarGridSpec(
            num_scalar_prefetch=1, grid=(N_CHIP, h // th),
            in_specs=[pl.BlockSpec((1, 1, th, cd), lambda q, i, c: (q, c[0], i, 0)),
                      pl.BlockSpec((1, th, cd), lambda q, i, c: (q, i, 0))],
            out_specs=pl.BlockSpec((1, th, cd), lambda q, i, c: (q, i, 0))),
        compiler_params=_params(("parallel", "parallel"), VMEM_LIMIT), name=name)(core, g, got)


def sum_chips(parts, sums, place, name):
    _, h, cd = parts.shape
    th = _half_tile(h, cd)

    def body(pc_ref, p_ref, own_ref, o_ref):
        acc = None
        for q in range(N_CHIP):
            term = jnp.where(pc_ref[1] == q, own_ref[0], p_ref[q]).astype(F32)
            acc = term if acc is None else acc + term
        o_ref[0] = acc

    return pl.pallas_call(
        body, out_shape=jax.ShapeDtypeStruct((2, h, cd), F32),
        grid_spec=pltpu.PrefetchScalarGridSpec(
            num_scalar_prefetch=1, grid=(h // th,),
            in_specs=[pl.BlockSpec((N_CHIP, th, cd), lambda i, pc: (0, i, 0)),
                      pl.BlockSpec((1, th, cd), lambda i, pc: (pc[1], i, 0))],
            out_specs=pl.BlockSpec((1, th, cd), lambda i, pc: (pc[0], i, 0))),
        compiler_params=_params(("parallel",), VMEM_LIMIT), name=name)(place, parts, sums)


def to_segments(a, n_ctx):
    def one(p):
        n = p.shape[0]
        return p.reshape(N_SEG, n // N_SEG, -1).transpose(1, 0, 2).reshape(n, -1)
    return jnp.concatenate([one(a[:n_ctx]), one(a[n_ctx:])], axis=0) if n_ctx else one(a)


def from_segments(a, n_ctx):
    def one(p):
        n = p.shape[0]
        return p.reshape(n // N_SEG, N_SEG, -1).transpose(1, 0, 2).reshape(n, -1)
    return jnp.concatenate([one(a[:n_ctx]), one(a[n_ctx:])], axis=0) if n_ctx else one(a)


def rope_tables(n_ctx, n_lat):
    f32 = np.float32
    rows = n_lat // GRID_W
    row = np.repeat(np.arange(rows), GRID_W).astype(f32)
    col = np.tile(np.arange(GRID_W), rows).astype(f32)
    d = QK_ROPE_DIM // 2
    inv = (f32(1.0) / np.power(f32(ROPE_THETA), np.arange(0, d, 2, dtype=f32) / f32(d))).astype(f32)
    ang = np.concatenate([row[:, None] * inv[None, :], col[:, None] * inv[None, :]], axis=1).astype(f32)
    cos = np.concatenate([np.ones((n_ctx, d), f32), np.cos(ang)], axis=0)
    sin = np.concatenate([np.zeros((n_ctx, d), f32), np.sin(ang)], axis=0)
    q = QK_ROPE_DIM // 4
    T = n_ctx + n_lat
    ones, zeros = np.ones((T, QK_NOPE_DIM), f32), np.zeros((T, QK_NOPE_DIM), f32)
    tail, z8 = np.zeros((T, HEAD_LANES - QK_DIM), f32), np.zeros((T, q), f32)
    cr, cc, sr, sc = cos[:, :q], cos[:, q:], sin[:, :q], sin[:, q:]
    cos_t = np.concatenate([ones, cr, cr, cc, cc, tail], axis=1)
    sin_next = np.concatenate([zeros, -sr, z8, -sc, z8, tail], axis=1)
    sin_prev = np.concatenate([zeros, z8, sr, z8, sc, tail], axis=1)
    return tuple(jnp.asarray(t, F32) for t in (cos_t, sin_next, sin_prev))


def pad_heads(w, used):
    k = w.shape[0]
    return jnp.pad(w.reshape(k, MLA_HEADS, used), ((0, 0), (0, 0), (0, HEAD_LANES - used))).reshape(k, -1)


def unpad_heads(w, used):
    k = w.shape[0]
    return w.reshape(k, MLA_HEADS, HEAD_LANES)[:, :, :used].reshape(k, MLA_HEADS * used)


def rotary_spread():
    lane = np.arange(MLA_HEADS * HEAD_LANES) % HEAD_LANES
    return jnp.asarray(lane[None, :] == (QK_NOPE_DIM + np.arange(QK_ROPE_DIM))[:, None], BF16)


def s5_discretise(a_re, a_im, log_step, b_re, b_im):
    dt = jnp.exp(log_step)[:, None]
    mag = jnp.exp(a_re * dt)
    lb_re = mag * jnp.cos(a_im * dt)
    lb_im = mag * jnp.sin(a_im * dt)
    den = a_re * a_re + a_im * a_im
    nr = lb_re - 1.0
    f_re = ((nr * a_re + lb_im * a_im) / den)[..., None]
    f_im = ((lb_im * a_re - nr * a_im) / den)[..., None]
    return lb_re, lb_im, f_re * b_re - f_im * b_im, f_re * b_im + f_im * b_re


def s5_block_weights(lb_re, lb_im, bb_re, bb_im, c_re, c_im):
    eye = jnp.eye(GROUPS_PER_BLOCK, dtype=F32)
    lam = jnp.stack([lb_re.reshape(1, S5_LANES), lb_im.reshape(1, S5_LANES)])

    def b_blocks(bb):
        t = bb.reshape(N_BLOCKS, GROUPS_PER_BLOCK, S5_STATE, S5_GROUP)
        return jnp.einsum("bgpc,gh->bgchp", t, eye).reshape(N_BLOCKS, BLK_CH, BLK_ST).astype(BF16)

    def c_blocks(cc):
        t = cc.reshape(N_BLOCKS, GROUPS_PER_BLOCK, S5_GROUP, S5_STATE)
        return jnp.einsum("bgcp,gh->bgphc", t, eye).reshape(N_BLOCKS, BLK_ST, BLK_CH).astype(BF16)

    return lam, b_blocks(bb_re), b_blocks(bb_im), c_blocks(c_re), c_blocks(c_im)


def b_block_diag(db):
    t = db.reshape(N_BLOCKS, GROUPS_PER_BLOCK, S5_GROUP, GROUPS_PER_BLOCK, S5_STATE)
    return jnp.einsum("bgchp,gh->bgpc", t, jnp.eye(GROUPS_PER_BLOCK, dtype=F32)).reshape(S5_GROUPS, S5_STATE, S5_GROUP)


def c_block_diag(dc):
    t = dc.reshape(N_BLOCKS, GROUPS_PER_BLOCK, S5_STATE, GROUPS_PER_BLOCK, S5_GROUP)
    return jnp.einsum("bgphc,gh->bgcp", t, jnp.eye(GROUPS_PER_BLOCK, dtype=F32)).reshape(S5_GROUPS, S5_GROUP, S5_STATE)


def conj(a):
    return jnp.stack([a[0], -a[1]])


PACK_TILE = 16 * 128


def pack_flat(parts, dtype):
    flat = [p.reshape(-1).astype(dtype) for p in parts]
    sizes = [f.shape[0] for f in flat]
    total = sum(sizes)
    pad = (-total) % PACK_TILE
    if pad:
        flat.append(jnp.zeros((pad,), dtype))
    offs = np.cumsum([0] + sizes)[:-1].tolist()
    return jnp.concatenate(flat).reshape(-1, 128), offs


def unpack_flat(buf, offs, shapes):
    flat = buf.reshape(-1)
    return [flat[o:o + int(np.prod(s))].reshape(s) for o, s in zip(offs, shapes)]


def s5_forward(p1, n_ctx, dirs):
    saved = []
    y = None
    ctx_rows, lat_rows = (0, n_ctx), (n_ctx, p1.shape[0] - n_ctx)
    zeros_tile = jnp.zeros((2, N_SEG, S5_LANES), F32)
    zeros_row = jnp.zeros((2, 1, S5_LANES), F32)
    for k, (lam, b_re, b_im, c_re, c_im) in enumerate(dirs):
        rev = k == 1
        last = 0 if rev else N_SEG - 1
        _, _, fin = s5_scan(p1, b_re, b_im, lam, zeros_tile, reverse=rev, rows=ctx_rows, name=f"s5_ctx_finals{k}")
        carry_c = s5_chain(fin, zeros_row, lam, n_ctx // N_SEG, rev, name=f"s5_ctx_chain{k}")
        _, ck_c, fin_c = s5_scan(p1, b_re, b_im, lam, carry_c, reverse=rev, want_ckpt=True, rows=ctx_rows,
                                 name=f"s5_ctx_scan{k}")
        s0 = fin_c[:, last:last + 1, :]
        _, _, fin = s5_scan(p1, b_re, b_im, lam, zeros_tile, reverse=rev, rows=lat_rows, name=f"s5_lat_finals{k}")
        carry_l = s5_chain(fin, s0, lam, lat_rows[1] // N_SEG, rev, name=f"s5_lat_chain{k}")
        y, ck_l, _ = s5_scan(p1, b_re, b_im, lam, carry_l, reverse=rev, c_re=c_re, c_im=c_im, add=y,
                             want_ckpt=True, rows=lat_rows, name=f"s5_lat_scan{k}")
        saved.append((ck_c, ck_l))
    return y, saved


def s5_backward(dy_l, du_extra_l, p1, n_ctx, dirs, saved):
    n_lat = p1.shape[0] - n_ctx
    zeros_tile = jnp.zeros((2, N_SEG, S5_LANES), F32)
    zeros_row = jnp.zeros((2, 1, S5_LANES), F32)
    dy_c = jnp.zeros((n_ctx, D_MODEL), F32)
    du_l, du_c = du_extra_l, None
    grads = []
    for k, (lam, b_re, b_im, c_re, c_im) in enumerate(dirs):
        rev = k == 1
        lam_c = conj(lam)
        ck_c, ck_l = saved[k]
        first = N_SEG - 1 if rev else 0
        _, _, fin = s5_scan(dy_l, c_re, c_im, lam_c, zeros_tile, reverse=not rev, adjoint=True,
                            name=f"s5_lat_adj_finals{k}")
        carry = s5_chain(fin, zeros_row, lam_c, n_lat // N_SEG, not rev, name=f"s5_lat_adj_chain{k}")
        du_l, dlam_l, dbr_l, dbi_l, dcr_l, dci_l, fin_a = s5_grads(
            dy_l, p1, ck_l, b_re, b_im, c_re, c_im, lam, carry, reverse=rev, add=du_l, u_off=n_ctx,
            name=f"s5_lat_grads{k}")
        g0 = fin_a[:, first:first + 1, :]
        carry = s5_chain(zeros_tile, g0, lam_c, n_ctx // N_SEG, not rev, name=f"s5_ctx_adj_chain{k}")
        du_c, dlam_c, dbr_c, dbi_c, _, _, _ = s5_grads(
            dy_c, p1, ck_c, b_re, b_im, c_re, c_im, lam, carry, reverse=rev, add=du_c, name=f"s5_ctx_grads{k}")
        dlam = jnp.sum(dlam_l + dlam_c, axis=1)
        grads.append((dlam, b_block_diag(dbr_l + dbr_c), b_block_diag(dbi_l + dbi_c),
                      c_block_diag(dcr_l), c_block_diag(dci_l)))
    return jnp.concatenate([du_c, du_l], axis=0), grads


def local_step(x, ctx, target, mod, w, late=None, reducer=None):
    L, Lc = x.shape[0], ctx.shape[0]
    T = L + Lc
    assert L % Lc == 0 and Lc % (2 * N_SEG) == 0 and L % GRID_W == 0
    D = D_MODEL
    X0 = jnp.concatenate([ctx, x], axis=0)

    def mod_of(i, j):
        return mod[i, :, j, :][:, None, :]

    def vec(v):
        return v.reshape(1, 1, -1).astype(F32)

    g0 = vec(w["norm_g"][0])
    H0, p0 = norm_proj(X0, g0, mod_of(0, 1), mod_of(0, 0), w["mla_w_in"], Lc, "l0_norm_in")
    cq = Rows(p0, Q_LORA_RANK, col_blk=D // Q_LORA_RANK)
    ckv = Rows(p0, KV_LORA_RANK, col_blk=(D + Q_LORA_RANK) // KV_LORA_RANK)
    kr = p0[:, D + Q_LORA_RANK + KV_LORA_RANK:D + P0_HEAD]
    qng, kvng = vec(w["mla_q_norm"]), vec(w["mla_kv_norm"])
    tabs = rope_tables(Lc, L)
    spread = rotary_spread()
    w_uq_p = pad_heads(w["mla_w_uq"], QK_DIM)
    w_ukv3 = w["mla_w_ukv"].reshape(KV_LORA_RANK, MLA_HEADS, QK_NOPE_DIM + V_HEAD_DIM)
    w_kn_p = pad_heads(w_ukv3[:, :, :QK_NOPE_DIM].reshape(KV_LORA_RANK, -1), QK_NOPE_DIM)
    w_v = w_ukv3[:, :, QK_NOPE_DIM:].reshape(KV_LORA_RANK, -1)
    qb, qn = q_heads(cq, qng, w_uq_p, tabs)
    kb, vb, kvn = kv_heads(ckv, kvng, w_kn_p, w_v, kr, spread, tabs)
    o, lse, qs = attn_fwd(qb, kb, vb, Lc)
    X1, og, out0 = mla_post_fwd(o, p0, X0, mod_of(0, 2), w["mla_w_out"], Lc)

    if late is not None:
        w = {**w, **late(X1)}
    X1p = to_segments(X1, Lc)
    tgt_p = to_segments(target, 0)
    g1 = vec(w["norm_g"][1])
    H1, p1 = norm_proj(X1p, g1, mod_of(1, 1), mod_of(1, 0), w["s5_w_in"], Lc, "l1_norm_in")
    disc_fn = lambda *a: tuple(zip(*[s5_discretise(a[0][k], a[1][k], a[2][k], a[3][k], a[4][k]) for k in range(2)]))
    disc, disc_vjp = jax.vjp(disc_fn, w["s5_a_re"], w["s5_a_im"], w["s5_log_step"], w["s5_b_re"], w["s5_b_im"])
    dirs = [s5_block_weights(disc[0][k], disc[1][k], disc[2][k], disc[3][k], w["s5_c_re"][k], w["s5_c_im"][k])
            for k in range(2)]
    y_ssm, s5_saved = s5_forward(p1, Lc, dirs)

    row = lambda v: v.reshape(1, D).astype(F32)
    (lvec, dX2, d_yssm, d_u_act, d_z1, d_fg, d_gt1, d_bg, d_d, gw_glu, gw_out) = s5_tail(
        y_ssm, p1, X1p, tgt_p, Lc, row(w["s5_d"]), row(w["s5_b_glu"]), mod[1, 1:2, 2, :], row(w["final_g"]),
        w["s5_w_glu"], w["s5_w_out"])
    loss = jnp.sum(lvec)
    gw = {"final_g": d_fg.reshape(D), "s5_b_glu": d_bg.reshape(D), "s5_d": d_d.reshape(D),
          "s5_w_glu": gw_glu, "s5_w_out": gw_out}
    dmod = {}

    du_p, s5_g = s5_backward(d_yssm, d_u_act, p1, Lc, dirs, s5_saved)
    d_disc = tuple(tuple(s5_g[k][j - 1].reshape(disc[j][k].shape) if j >= 2 else
                         s5_g[k][0][j].reshape(disc[j][k].shape) for k in range(2)) for j in range(4))
    gw["s5_a_re"], gw["s5_a_im"], gw["s5_log_step"], gw["s5_b_re"], gw["s5_b_im"] = disc_vjp(d_disc)
    gw["s5_c_re"] = jnp.stack([s5_g[0][3], s5_g[1][3]])
    gw["s5_c_im"] = jnp.stack([s5_g[0][4], s5_g[1][4]])
    gw["s5_w_in"] = jnp.concatenate([mm_tn(H1, du_p, name="l1_in_dw_u"), mm_tn(H1[Lc:], d_z1, name="l1_in_dw_z")],
                                    axis=1)
    if reducer is not None:
        g1 = g1 + reducer[0]({n: gw.pop(n) for n in LAYER1_MATS})[0, 0]
    d_X1p, d_g1, d_sc1, d_sh1 = norm_proj_bwd([(du_p, 0, 0), (d_z1, D, Lc)], w["s5_w_in"], X1p, g1, mod_of(1, 1),
                                              mod_of(1, 0), dX2, Lc, "l1_norm_in_bwd")
    d_gt1_full = jnp.concatenate([jnp.zeros((1, 1, D), F32), d_gt1[None]], axis=0)
    dmod[1] = (d_sh1, d_sc1, d_gt1_full)
    d_X1 = from_segments(d_X1p, Lc)

    d_o, d_z0, d_gt0, gw["mla_w_out"] = mla_post_bwd(d_X1, out0, og, o, p0, mod_of(0, 2), w["mla_w_out"], Lc)
    if reducer is not None:
        tabs = (tabs[0] + reducer[1](d_o)[0, 0],) + tabs[1:]
    d_q, dos = attn_bwd_dq(qb, kb, vb, o, d_o, lse, tabs, Lc)
    dk_p, d_v = attn_bwd_dkv(qs, kb, vb, dos, Lc)
    d_k, d_kr = heads_unrope(dk_p, tabs, jnp.pad(spread, ((0, HEAD_LANES - QK_ROPE_DIM), (0, 0))),
                             name="l0_k_unrope")
    d_qn = mm_nt(d_q, w_uq_p, name="l0_uq_dx")
    gw["mla_w_uq"] = unpad_heads(mm_tn(qn, d_q, name="l0_uq_dw"), QK_DIM)
    d_kvn = mm_nt(d_k, w_kn_p, name="l0_ukn_dx") + mm_nt(d_v, w_v, name="l0_uv_dx")
    dw_kn = unpad_heads(mm_tn(kvn, d_k, name="l0_ukn_dw"), QK_NOPE_DIM).reshape(KV_LORA_RANK, MLA_HEADS, QK_NOPE_DIM)
    dw_v = mm_tn(kvn, d_v, name="l0_uv_dw").reshape(KV_LORA_RANK, MLA_HEADS, V_HEAD_DIM)
    gw["mla_w_ukv"] = jnp.concatenate([dw_kn, dw_v], axis=-1).reshape(KV_LORA_RANK, -1)
    d_cq, d_qng = rowwise_bwd(f_rms, [cq], [qng], [d_qn], [0], [0], T, 0, "l0_qnorm_bwd")
    d_ckv, d_kvng = rowwise_bwd(f_rms, [ckv], [kvng], [d_kvn], [0], [0], T, 0, "l0_kvnorm_bwd")
    gw["mla_q_norm"] = d_qng.reshape(-1)
    gw["mla_kv_norm"] = d_kvng.reshape(-1)
    o_cq, o_ckv = D, D + Q_LORA_RANK
    o_kr = o_ckv + KV_LORA_RANK
    d_head = jnp.concatenate([d_cq, d_ckv, d_kr], axis=1)
    gw["mla_w_in"] = jnp.concatenate([mm_tn(H0, d_head, name="l0_in_dw_head")[:, :P0_HEAD],
                                      mm_tn(H0, d_z0, name="l0_in_dw_z")], axis=1)
    d_X0, d_g0, d_sc0, d_sh0 = norm_proj_bwd(
        [(d_z0, 0, 0), (d_cq, o_cq, 0), (d_ckv, o_ckv, 0), (d_kr, o_kr, 0)], w["mla_w_in"], X0, g0, mod_of(0, 1),
        mod_of(0, 0), d_X1, Lc, "l0_norm_in_bwd")
    dmod[0] = (d_sh0, d_sc0, d_gt0)
    gw["norm_g"] = jnp.stack([d_g0.reshape(D), d_g1.reshape(D)])
    dx = d_X0[Lc:]
    dmod_arr = jnp.stack([jnp.stack([dmod[i][j][:, 0, :] for j in range(3)], axis=1) for i in range(2)])
    ready = reducer[2](d_X0) if reducer is not None else {}
    return loss, dx, dmod_arr, gw, ready


SHARDED = {
    "mla_w_in": 1, "mla_w_uq": 1, "mla_w_ukv": 1, "mla_w_out": 0,
    "s5_w_in": 1, "s5_w_glu": 0, "s5_w_out": 0, "s5_d": 0, "s5_b_glu": 0,
}
SHARDED_MATS = ["mla_w_in", "mla_w_uq", "mla_w_ukv", "mla_w_out", "s5_w_in", "s5_w_glu", "s5_w_out"]
SHARDED_VECS = ["s5_d", "s5_b_glu"]
REPLICATED = ["norm_g", "mla_q_norm", "mla_kv_norm", "s5_a_re", "s5_a_im", "s5_log_step", "s5_b_re", "s5_b_im",
              "s5_c_re", "s5_c_im", "final_g"]
WEIGHT_ORDER = ["c_ctx", "ada_w", "ada_b", "norm_g", "mla_w_in", "mla_q_norm", "mla_w_uq", "mla_kv_norm", "mla_w_ukv",
                "mla_w_out", "s5_w_in", "s5_a_re", "s5_a_im", "s5_log_step", "s5_b_re", "s5_b_im", "s5_c_re", "s5_c_im",
                "s5_d", "s5_w_glu", "s5_b_glu", "s5_w_out", "final_g"]


P0_HEAD = Q_LORA_RANK + KV_LORA_RANK + QK_ROPE_DIM


P0_WIDTH = 1536


def w_in_to_kernel_order(w):
    pad = jnp.zeros((w.shape[0], P0_WIDTH - w.shape[1]), w.dtype)
    return jnp.concatenate([w[:, P0_HEAD:], w[:, :P0_HEAD], pad], axis=1)


LAYER0_MATS = ["mla_w_in", "mla_w_uq", "mla_w_ukv", "mla_w_out"]
LAYER1_MATS = ["s5_w_in", "s5_w_glu", "s5_w_out"]


def _whole_matrices(names, own_blocks, gathered):
    chip = _chip_index(_coords())
    full = {}
    for n, own, o in zip(names, own_blocks, gathered):
        slot = lax.broadcasted_iota(jnp.int32, (N_CHIP, 1, 1), 0)
        o = jnp.where(slot == chip, own[None], o.reshape((N_CHIP,) + own.shape))
        full[n] = o.reshape(-1, o.shape[-1]) if SHARDED[n] == 0 else o.transpose(1, 0, 2).reshape(o.shape[1], -1)
    return full


def gather_weights(ws):
    mats = [ws[n].astype(BF16) for n in LAYER0_MATS]
    full = _whole_matrices(LAYER0_MATS, mats, gather_halves(mats, "gather_weights"))
    full["mla_w_in"] = w_in_to_kernel_order(full["mla_w_in"])
    return full


def gather_weights_behind(ws, after):
    mats = [ws[n].astype(BF16) for n in LAYER1_MATS]
    flight, token = exchange_start(
        mats, [jax.ShapeDtypeStruct((N_CHIP,) + m.shape, m.dtype) for m in mats],
        [(f, a, lambda me, peer: None, a, lambda s: (_chip_index(s),)) for a in range(len(mats)) for f in CHIP_FLIPS],
        "gather_l1_start", after=after)

    def finish(after):
        own, got = exchange_wait(flight, after, "gather_l1_wait")
        return _whole_matrices(LAYER1_MATS, own, got)

    return token[0, 0], finish


def _grad_slots(gw, names):
    slots = []
    for n in names:
        g = gw[n]
        if SHARDED[n] == 0:
            slots.append(g.reshape(N_CHIP, 2, g.shape[0] // (2 * N_CHIP), g.shape[1]))
        else:
            k, n4 = g.shape
            slots.append(g.reshape(k, N_CHIP, n4 // N_CHIP).transpose(1, 0, 2)
                         .reshape(N_CHIP, 2, k // 2, n4 // N_CHIP))
    return slots


def _to_sibling_half(count):
    return [(CORE_FLIP, i, lambda me, peer: (slice(None), 1 - me[2]), i, lambda s: None) for i in range(count)]


def _to_chips(count):
    return [(f, i, lambda me, peer: (_chip_index(peer),), i, lambda s: (_chip_index(s),))
            for i in range(count) for f in CHIP_FLIPS]


def _place():
    me = _coords()
    return jnp.stack([me[2], _chip_index(me)]).astype(jnp.int32)


def reduce_behind(names):
    state = {}
    count = len(names)

    def begin(gw):
        slots = _grad_slots(gw, names)
        lands = [jax.ShapeDtypeStruct((N_CHIP,) + s.shape[2:], F32) for s in slots]
        state["in"], token = exchange_start(slots, lands, _to_sibling_half(count), "grads_l1_swap_in_start")
        return token

    def middle(after):
        slots, got = exchange_wait(state["in"], after, "grads_l1_swap_in_wait")
        place = _place()
        sums = [pair_add(s, g, place[:1], BF16, f"grads_pair_{n}") for n, s, g in zip(names, slots, got)]
        lands = [jax.ShapeDtypeStruct(s.shape, s.dtype) for s in sums]
        state["out"], token = exchange_start(sums, lands, _to_chips(count), "grads_l1_scatter_start")
        return token

    def end(after):
        sums, parts = exchange_wait(state["out"], after, "grads_l1_scatter_wait")
        place = _place()
        return {n: sum_chips(p, s, place, f"grads_sum_{n}") for n, p, s in zip(names, parts, sums)}

    return begin, middle, end


def reduce_gradients(gw, ready_halves):
    me = _coords()
    place = _place()
    mat_names = [n for n in SHARDED_MATS if n not in ready_halves]
    slots = dict(zip(mat_names, _grad_slots(gw, mat_names)))
    small_names = REPLICATED + SHARDED_VECS
    small, small_offs = pack_flat([gw[n].astype(F32) for n in small_names], F32)
    small = jnp.pad(small, ((0, (-small.shape[0]) % (N_CHIP * 32)), (0, 0)))
    slots["small"] = small.reshape(N_CHIP, 2, -1, 128)
    names = list(slots)
    count = len(names)
    got = exchange([slots[n] for n in names],
                   [jax.ShapeDtypeStruct((N_CHIP,) + slots[n].shape[2:], F32) for n in names],
                   _to_sibling_half(count), [], "grads_swap_in")
    sums = [pair_add(slots[n], g, place[:1], F32 if n == "small" else BF16, f"grads_pair_{n}")
            for n, g in zip(names, got)]
    parts = exchange(sums, [jax.ShapeDtypeStruct(s.shape, s.dtype) for s in sums], _to_chips(count), [],
                     "grads_scatter")
    halves = {n: sum_chips(p, s, place, f"grads_sum_{n}") for n, p, s in zip(names, parts, sums)}
    halves.update(ready_halves)
    all_names = list(halves)
    fulls = exchange(
        [halves[n] for n in all_names], [jax.ShapeDtypeStruct(halves[n].shape, F32) for n in all_names],
        [(CORE_FLIP, i, lambda me, peer: (me[2],), i, lambda s: (s[2],)) for i in range(len(all_names))], [],
        "grads_swap_out", aliases={i: i for i in range(len(all_names))})
    out = {n: f.reshape(-1, f.shape[-1]) for n, f in zip(all_names, fulls)}
    quarter = out.pop("small")
    gather, token = exchange_start(
        [quarter], [jax.ShapeDtypeStruct((N_CHIP,) + quarter.shape, F32)],
        [(f, 0, lambda me, peer: None, 0, lambda s: (_chip_index(s),)) for f in CHIP_FLIPS],
        "grads_gather_small_start")

    def finish_small(after):
        (own,), (got_small,) = exchange_wait(gather, after, "grads_gather_small_wait")
        slot = lax.broadcasted_iota(jnp.int32, (N_CHIP, 1, 1), 0)
        small_all = jnp.where(slot == _chip_index(me), own[None], got_small)
        vals = unpack_flat(small_all, small_offs, [gw[n].shape for n in small_names])
        res = {}
        for n, v in zip(small_names, vals):
            if n in SHARDED_VECS:
                size = v.shape[0] // N_CHIP
                v = lax.dynamic_slice_in_dim(v, _chip_index(me) * size, size)
            res[n] = v
        return res

    return out, finish_small, token


def kernel(x, c, ctx, c_ctx, ada_w, ada_b, norm_g, mla_w_in, mla_q_norm, mla_w_uq, mla_kv_norm, mla_w_ukv, mla_w_out, s5_w_in, s5_a_re, s5_a_im, s5_log_step, s5_b_re, s5_b_im, s5_c_re, s5_c_im, s5_d, s5_w_glu, s5_b_glu, s5_w_out, final_g, loss_target, m_c_ctx, m_ada_w, m_ada_b, m_norm_g, m_mla_w_in, m_mla_q_norm, m_mla_w_uq, m_mla_kv_norm, m_mla_w_ukv, m_mla_w_out, m_s5_w_in, m_s5_a_re, m_s5_a_im, m_s5_log_step, m_s5_b_re, m_s5_b_im, m_s5_c_re, m_s5_c_im, m_s5_d, m_s5_w_glu, m_s5_b_glu, m_s5_w_out, m_final_g, v_c_ctx, v_ada_w, v_ada_b, v_norm_g, v_mla_w_in, v_mla_q_norm, v_mla_w_uq, v_mla_kv_norm, v_mla_w_ukv, v_mla_w_out, v_s5_w_in, v_s5_a_re, v_s5_a_im, v_s5_log_step, v_s5_b_re, v_s5_b_im, v_s5_c_re, v_s5_c_im, v_s5_d, v_s5_w_glu, v_s5_b_glu, v_s5_w_out, v_final_g):
    args = dict(locals())
    weights = {n: args[n] for n in WEIGHT_ORDER}
    D = D_MODEL
    xi, yi, ci = _coords()
    chip = 2 * xi + yi
    me = 4 * xi + 2 * yi + ci
    n_col = ada_w.shape[2]

    c_all = allgather_devices(jnp.pad(c, ((0, 7), (0, 0))), "gather_c")[:, 0, :]
    cond = jnp.concatenate([c_all, jnp.broadcast_to(c_ctx[None], (8, D))], axis=0)
    (s_cond,) = rowwise_fwd(lambda v: (_silu(v),), [cond], [], [D], [F32], 16, 0, "cond_silu")
    ada_rows = ada_w.reshape(2 * D, n_col)
    mod_cols = jnp.stack([mm_nn(s_cond, ada_rows, name=f"mod_proj{i}", b_blk=i) for i in range(2)])
    vec_tiles = [jnp.pad(weights[n][0].reshape(-1, 128), ((0, 6), (0, 0))) for n in SHARDED_VECS]
    mod_all, *vec_all = allgather_chips([mod_cols] + vec_tiles, "gather_mod")
    mod_all = mod_all.transpose(1, 2, 0, 3).reshape(2, 16, 3 * D) + ada_b[:, None, :]
    mod_l = lax.dynamic_index_in_dim(mod_all, me, axis=1, keepdims=False)
    mod_c = mod_all[:, 8, :]
    mod = jnp.stack([mod_c.reshape(2, 3, D), mod_l.reshape(2, 3, D)], axis=1)

    w = gather_weights({n: weights[n][0] for n in LAYER0_MATS})
    token, late = gather_weights_behind({n: weights[n][0] for n in LAYER1_MATS}, [w["mla_w_out"], mod])
    for n, v in zip(SHARDED_VECS, vec_all):
        w[n] = v[:, :2, :].reshape(-1)
    for n in ["norm_g", "final_g"]:
        w[n] = weights[n]
    for n in ["mla_q_norm", "mla_kv_norm", "s5_a_re", "s5_a_im", "s5_log_step", "s5_b_re", "s5_b_im",
              "s5_c_re", "s5_c_im"]:
        w[n] = weights[n][0]

    loss_me, dx, dmod, gw, ready = local_step(x[0] + token, ctx[0], loss_target[0], mod, w, late,
                                              reduce_behind(LAYER1_MATS))

    dmod_rows, loss_all = _gather([dmod.reshape(2, 2, 3 * D), jnp.broadcast_to(loss_me, (8, 128))],
                                  ALL_FLIPS, _dev_index, N_DEV, "gather_dmod")
    loss = functools.reduce(lambda s, d: s + loss_all[d, 0, 0], range(1, N_DEV), loss_all[0, 0, 0])
    dm = jnp.concatenate([dmod_rows[:, :, 1, :], dmod_rows[:, :, 0, :]], axis=0).transpose(1, 0, 2)
    g_ada_b = jnp.sum(dm, axis=1)
    dm_cols = lax.dynamic_slice_in_dim(dm, chip * n_col, n_col, axis=2)
    g_ada_w = jnp.stack([mm_tn(s_cond, dm_cols[i], name=f"mod_proj_dw{i}") for i in range(2)])
    dmc = jnp.sum(dm_cols[:, 8:, :], axis=1)
    dmc8 = jnp.broadcast_to(dmc[:, None, :], (2, 8, n_col))
    g_sc = (mm_nt(dmc8[0], ada_rows, name="mod_proj_dx0", b_rows=D, b_blk=0)[0]
            + mm_nt(dmc8[1], ada_rows, name="mod_proj_dx1", b_rows=D, b_blk=1)[0])
    g_sc_all = allgather_devices(jnp.broadcast_to(g_sc[None], (8, D)), "gather_dcond")[:, 0, :]
    g_silu_cc = g_sc_all[0] + g_sc_all[2] + g_sc_all[4] + g_sc_all[6]
    (g_c_ctx,) = rowwise_bwd(lambda v: (_silu(v),), [jnp.broadcast_to(c_ctx[None], (8, D))], [],
                             [jnp.broadcast_to(g_silu_cc[None], (8, D))], [0], [], 8, 0, "cond_silu_bwd")
    g_c_ctx = g_c_ctx[0]

    grads = {"c_ctx": g_c_ctx, "ada_w": g_ada_w, "ada_b": g_ada_b}
    deltas, new_m, new_v = {}, {}, {}
    small = [n for n in WEIGHT_ORDER if weights[n].size < 50000]

    def update(n, after=None):
        shp = weights[n].shape
        w2 = weights[n].reshape(-1, shp[-1])
        d_, m_, v_ = adamw(w2, grads[n].reshape(w2.shape), args["m_" + n].reshape(w2.shape),
                           args["v_" + n].reshape(w2.shape), name=f"adamw_{n}", after=after)
        deltas[n], new_m[n], new_v[n] = d_.reshape(shp), m_.reshape(shp), v_.reshape(shp)

    red, finish_small, small_started = reduce_gradients(gw, ready)
    update("ada_w", small_started)
    for n in SHARDED_MATS:
        grads[n] = red[n].reshape(weights[n].shape)
        update(n, small_started)
    red_small = finish_small(jnp.stack([deltas[n][(0,) * deltas[n].ndim]
                                        for n in ["ada_w"] + SHARDED_MATS]))
    for n in REPLICATED + SHARDED_VECS:
        grads[n] = red_small[n].reshape(weights[n].shape)
    for n in WEIGHT_ORDER:
        if n not in small and n not in deltas:
            update(n)
    packs = []
    offs = None
    for src in (weights, grads, {n: args["m_" + n] for n in small}, {n: args["v_" + n] for n in small}):
        buf, offs = pack_flat([src[n] for n in small], F32)
        packs.append(buf)
    outs = adamw(*packs, name="adamw_small")
    for res, dst in zip(outs, (deltas, new_m, new_v)):
        for n, val in zip(small, unpack_flat(res, offs, [weights[n].shape for n in small])):
            dst[n] = val

    return (loss, dx[None], *[grads[n] for n in WEIGHT_ORDER], *[deltas[n] for n in WEIGHT_ORDER],
            *[new_m[n] for n in WEIGHT_ORDER], *[new_v[n] for n in WEIGHT_ORDER])
```

```python
import functools
import math

import jax
import jax.numpy as jnp
import numpy as np
from jax import lax
from jax.experimental import pallas as pl
from jax.experimental.pallas import tpu as pltpu

F32 = jnp.float32
BF16 = jnp.bfloat16

D_MODEL = 1024
GRID_W = 64
EPS = 1e-6
MLA_HEADS = 16
QK_NOPE_DIM = 64
QK_ROPE_DIM = 32
V_HEAD_DIM = 64
Q_LORA_RANK = 256
KV_LORA_RANK = 128
QK_DIM = QK_NOPE_DIM + QK_ROPE_DIM
SOFTMAX_SCALE = QK_DIM ** -0.5
ROPE_THETA = 10000.0
S5_GROUP = 16
S5_GROUPS = D_MODEL // S5_GROUP
S5_STATE = 64
S5_LANES = S5_GROUPS * S5_STATE
N_SEG = 8
GROUPS_PER_BLOCK = 8
N_BLOCKS = S5_GROUPS // GROUPS_PER_BLOCK
BLK_CH = GROUPS_PER_BLOCK * S5_GROUP
BLK_ST = GROUPS_PER_BLOCK * S5_STATE

ADAM_LR = 0.001
ADAM_B1 = 0.9
ADAM_B2 = 0.999
ADAM_EPS = 1e-08
ADAM_WD = 0.01
ADAM_STEP = 10

N_DEV = 8
N_CHIP = 4
MESH = pl.DeviceIdType.MESH
VMEM_LIMIT = 52 * 1024 * 1024
ROW_TILE = 256


def _params(sem=None, vmem=None):
    return pltpu.CompilerParams(dimension_semantics=sem, vmem_limit_bytes=vmem)


def mm_nn(a, b, out_dtype=F32, name="mm_nn", b_blk=0):
    M, K = a.shape
    N = b.shape[1]
    tm = math.gcd(ROW_TILE, M)

    def body(a_ref, b_ref, o_ref):
        o_ref[...] = jnp.dot(a_ref[...].astype(BF16), b_ref[...].astype(BF16),
                             preferred_element_type=F32).astype(o_ref.dtype)

    return pl.pallas_call(
        body, out_shape=jax.ShapeDtypeStruct((M, N), out_dtype), grid=(M // tm,),
        in_specs=[pl.BlockSpec((tm, K), lambda i: (i, 0)), pl.BlockSpec((K, N), lambda i: (b_blk, 0))],
        out_specs=pl.BlockSpec((tm, N), lambda i: (i, 0)),
        compiler_params=_params(("parallel",), VMEM_LIMIT), name=name)(a, b)


def mm_nt(a, b, out_dtype=F32, name="mm_nt", b_rows=None, b_blk=0):
    M, N = a.shape
    K = b.shape[0] if b_rows is None else b_rows
    tm = math.gcd(ROW_TILE, M)

    def body(a_ref, b_ref, o_ref):
        o_ref[...] = lax.dot_general(a_ref[...].astype(BF16), b_ref[...].astype(BF16),
                                     (((1,), (1,)), ((), ())),
                                     preferred_element_type=F32).astype(o_ref.dtype)

    return pl.pallas_call(
        body, out_shape=jax.ShapeDtypeStruct((M, K), out_dtype), grid=(M // tm,),
        in_specs=[pl.BlockSpec((tm, N), lambda i: (i, 0)), pl.BlockSpec((K, N), lambda i: (b_blk, 0))],
        out_specs=pl.BlockSpec((tm, K), lambda i: (i, 0)),
        compiler_params=_params(("parallel",), VMEM_LIMIT), name=name)(a, b)


def mm_tn(a, b, name="mm_tn"):
    M, K = a.shape
    N = b.shape[1]
    tn = math.gcd(512, N) if N % 128 == 0 and N > 512 else N

    def body(a_ref, b_ref, o_ref):
        o_ref[...] = lax.dot_general(a_ref[...].astype(BF16), b_ref[...].astype(BF16),
                                     (((0,), (0,)), ((), ())), preferred_element_type=F32)

    return pl.pallas_call(
        body, out_shape=jax.ShapeDtypeStruct((K, N), F32), grid=(N // tn,),
        in_specs=[pl.BlockSpec((M, K), lambda j: (0, 0)), pl.BlockSpec((M, tn), lambda j: (0, j))],
        out_specs=pl.BlockSpec((K, tn), lambda j: (0, j)),
        compiler_params=_params(("parallel",), VMEM_LIMIT), name=name)(a, b)


class Rows:
    def __init__(self, arr, width=None, row_off=0, col_blk=0):
        self.arr = arr
        self.width = arr.shape[1] if width is None else width
        self.row_off = row_off
        self.col_blk = col_blk

    def spec(self, tm):
        ro, cb = self.row_off // tm, self.col_blk
        return pl.BlockSpec((tm, self.width), lambda i: (i + ro, cb))


def _as_rows(x):
    return x if isinstance(x, Rows) else Rows(x)


def _row_tile(n_rows, n_ctx_rows, rows):
    tm = math.gcd(ROW_TILE, n_rows, n_ctx_rows)
    for r in rows:
        tm = math.gcd(tm, r.row_off)
    return tm


def _bc_spec(arr, n_ctx_blocks):
    g, _, d = arr.shape
    if g == 1:
        return pl.BlockSpec((1, 1, d), lambda i: (0, 0, 0))
    return pl.BlockSpec((1, 1, d), lambda i: ((i >= n_ctx_blocks).astype(jnp.int32), 0, 0))


def rowwise_fwd(fn, rows, bcs, out_dims, out_dtypes, n_rows, n_ctx_rows, name):
    rows = [_as_rows(r) for r in rows]
    tm = _row_tile(n_rows, n_ctx_rows, rows)
    ncb = n_ctx_rows // tm
    nr, nb = len(rows), len(bcs)

    def body(*refs):
        vals = [r[...].astype(F32) for r in refs[:nr]] + [b[0].astype(F32) for b in refs[nr:nr + nb]]
        outs = fn(*vals)
        for o_ref, v in zip(refs[nr + nb:], outs):
            o_ref[...] = v.astype(o_ref.dtype)

    outs = pl.pallas_call(
        body,
        out_shape=[jax.ShapeDtypeStruct((n_rows, d), dt) for d, dt in zip(out_dims, out_dtypes)],
        grid=(n_rows // tm,),
        in_specs=[r.spec(tm) for r in rows] + [_bc_spec(b, ncb) for b in bcs],
        out_specs=[pl.BlockSpec((tm, d), lambda i: (i, 0)) for d in out_dims],
        compiler_params=_params(("parallel",), VMEM_LIMIT), name=name)(*[r.arr for r in rows], *bcs)
    return outs


def rowwise_bwd(fn, rows, bcs, cts, diff_rows, diff_bcs, n_rows, n_ctx_rows, name, ct_extra=None, lat_add=None):
    rows = [_as_rows(r) for r in rows]
    cts = [_as_rows(c) for c in cts]
    extra = [_as_rows(ct_extra)] if ct_extra is not None else []
    tm = _row_tile(n_rows, n_ctx_rows, rows + cts + extra)
    ncb = n_ctx_rows // tm
    nr, nb, nc = len(rows), len(bcs), len(cts)
    ndr, ndb = len(diff_rows), len(diff_bcs)
    n_in = nr + nb + nc + len(extra) + (lat_add is not None)

    def body(*refs):
        i = pl.program_id(0)
        rvals = [r[...].astype(F32) for r in refs[:nr]]
        bvals = [b[0].astype(F32) for b in refs[nr:nr + nb]]
        cvals = [c[...].astype(F32) for c in refs[nr + nb:nr + nb + nc]]
        if extra:
            cvals[0] = cvals[0] + refs[nr + nb + nc][...].astype(F32)
        outs = refs[n_in:]

        def f(*d):
            rv, bv = list(rvals), list(bvals)
            for k, idx in enumerate(diff_rows):
                rv[idx] = d[k]
            for k, idx in enumerate(diff_bcs):
                bv[idx] = d[ndr + k]
            return tuple(fn(*rv, *bv))

        primals = [rvals[k] for k in diff_rows] + [bvals[k] for k in diff_bcs]
        _, vjp = jax.vjp(f, *primals)
        grads = list(vjp(tuple(cvals)))
        if lat_add is not None:
            add = refs[n_in - 1][...]
            grads[0] = grads[0] + (add if lat_add.shape[0] == n_rows else jnp.where(i >= ncb, add, 0.0))
        for k in range(ndr):
            outs[k][...] = grads[k].astype(outs[k].dtype)
        for k, idx in enumerate(diff_bcs):
            o_ref = outs[ndr + k]
            first = (i == 0)
            if bcs[idx].shape[0] == 2:
                first = first | (i == ncb)

            @pl.when(first)
            def _(o_ref=o_ref):
                o_ref[...] = jnp.zeros_like(o_ref)

            o_ref[0] += grads[ndr + k]

    out_shape = [jax.ShapeDtypeStruct((n_rows, rows[k].width), F32) for k in diff_rows]
    out_shape += [jax.ShapeDtypeStruct(bcs[k].shape, F32) for k in diff_bcs]
    out_specs = [pl.BlockSpec((tm, rows[k].width), lambda i: (i, 0)) for k in diff_rows]
    out_specs += [_bc_spec(bcs[k], ncb) for k in diff_bcs]
    ins = [r.arr for r in rows] + list(bcs) + [c.arr for c in cts + extra]
    in_specs = [r.spec(tm) for r in rows] + [_bc_spec(b, ncb) for b in bcs] + [c.spec(tm) for c in cts + extra]
    if lat_add is not None:
        ins.append(lat_add)
        skip = ncb if lat_add.shape[0] != n_rows else 0
        in_specs.append(pl.BlockSpec((tm, lat_add.shape[1]), lambda i: (jnp.maximum(i - skip, 0), 0)))
    outs = pl.pallas_call(
        body, out_shape=out_shape, grid=(n_rows // tm,), in_specs=in_specs, out_specs=out_specs,
        compiler_params=_params(("arbitrary",), VMEM_LIMIT), name=name)(*ins)
    return outs


def _rms(x):
    return x * lax.rsqrt(jnp.mean(x * x, axis=-1, keepdims=True) + EPS)


def _sigmoid(x):
    return 0.5 * (jnp.tanh(0.5 * x) + 1.0)


def _silu(x):
    return x * _sigmoid(x)


def _gelu_tanh(x):
    return 0.5 * x * (1.0 + jnp.tanh(math.sqrt(2.0 / math.pi) * (x + 0.044715 * (x * x * x))))


def f_norm_mod(x, g, sc, sh):
    return ((_rms(x) * g) * (1.0 + sc) + sh,)


def f_rms(x, g):
    return (_rms(x) * g,)


def f_gate(o, z):
    return (o * _silu(z),)


def f_s5_act(y, u, d):
    return (_gelu_tanh(y + d * u),)


def f_s5_glu(ya, gl, z, b):
    return (ya * _sigmoid(gl + b) * _silu(z),)


def norm_proj(x, g, sc, sh, w, n_ctx, name):
    n, d = x.shape
    nw = w.shape[1]
    tm = math.gcd(ROW_TILE, n, n_ctx)
    ncb = n_ctx // tm

    def body(x_ref, g_ref, sc_ref, sh_ref, w_ref, h_ref, p_ref):
        h = f_norm_mod(x_ref[...], g_ref[0], sc_ref[0], sh_ref[0])[0].astype(BF16)
        h_ref[...] = h
        p_ref[...] = jnp.dot(h, w_ref[...], preferred_element_type=F32)

    row = pl.BlockSpec((tm, d), lambda i: (i, 0))
    return pl.pallas_call(
        body, out_shape=[jax.ShapeDtypeStruct((n, d), BF16), jax.ShapeDtypeStruct((n, nw), F32)], grid=(n // tm,),
        in_specs=[row, _bc_spec(g, ncb), _bc_spec(sc, ncb), _bc_spec(sh, ncb), pl.BlockSpec(w.shape, lambda i: (0, 0))],
        out_specs=[row, pl.BlockSpec((tm, nw), lambda i: (i, 0))],
        compiler_params=_params(("parallel",), VMEM_LIMIT), name=name)(x, g, sc, sh, w)


def norm_proj_bwd(terms, w, x, g, sc, sh, add, n_ctx, name):
    n, d = x.shape
    tm = math.gcd(ROW_TILE, n, n_ctx, *[t[2] for t in terms])
    ncb = n_ctx // tm
    nt = len(terms)
    add_skip = ncb if add.shape[0] != n else 0

    def body(*refs):
        i = pl.program_id(0)
        a_refs = refs[:nt]
        w_ref, x_ref, g_ref, sc_ref, sh_ref, add_ref = refs[nt:nt + 6]
        dx_ref, dg_ref, dsc_ref, dsh_ref = refs[nt + 6:]
        d_h = None
        for a_ref, (a, off, first) in zip(a_refs, terms):
            part = lax.dot_general(a_ref[...].astype(BF16), w_ref[:, off:off + a.shape[1]], NT_DIMS,
                                   preferred_element_type=F32)
            if first:
                part = jnp.where(i >= first // tm, part, 0.0)
            d_h = part if d_h is None else d_h + part
        _, vjp = jax.vjp(lambda x_, g_, sc_, sh_: f_norm_mod(x_, g_, sc_, sh_), x_ref[...], g_ref[0], sc_ref[0],
                         sh_ref[0])
        d_x, d_g, d_sc, d_sh = vjp((d_h,))
        extra = add_ref[...]
        dx_ref[...] = d_x + (extra if add_skip == 0 else jnp.where(i >= ncb, extra, 0.0))

        @pl.when(i == 0)
        def _():
            dg_ref[...] = jnp.zeros_like(dg_ref)

        @pl.when((i == 0) | (i == ncb))
        def _():
            dsc_ref[...] = jnp.zeros_like(dsc_ref)
            dsh_ref[...] = jnp.zeros_like(dsh_ref)

        dg_ref[0] += d_g
        dsc_ref[0] += d_sc
        dsh_ref[0] += d_sh

    def a_spec(a, first):
        skip = first // tm
        return pl.BlockSpec((tm, a.shape[1]), lambda i: (jnp.maximum(i - skip, 0), 0))

    row = pl.BlockSpec((tm, d), lambda i: (i, 0))
    return pl.pallas_call(
        body,
        out_shape=[jax.ShapeDtypeStruct((n, d), F32), jax.ShapeDtypeStruct(g.shape, F32),
                   jax.ShapeDtypeStruct(sc.shape, F32), jax.ShapeDtypeStruct(sh.shape, F32)],
        grid=(n // tm,),
        in_specs=[a_spec(a, first) for a, _, first in terms]
        + [pl.BlockSpec(w.shape, lambda i: (0, 0)), row, _bc_spec(g, ncb), _bc_spec(sc, ncb), _bc_spec(sh, ncb),
           pl.BlockSpec((tm, d), lambda i: (jnp.maximum(i - add_skip, 0), 0))],
        out_specs=[row, _bc_spec(g, ncb), _bc_spec(sc, ncb), _bc_spec(sh, ncb)],
        compiler_params=_params(("arbitrary",), VMEM_LIMIT), name=name)(*[t[0] for t in terms], w, x, g, sc, sh, add)


def mla_post_fwd(o, p0, x0, gate, w_out, n_ctx, name="l0_post"):
    n, d = o.shape
    tm = math.gcd(ROW_TILE, n, n_ctx)
    ncb = n_ctx // tm

    def body(o_ref, z_ref, x_ref, gt_ref, w_ref, x1_ref, og_ref, out_ref):
        og = f_gate(o_ref[...], z_ref[...])[0].astype(BF16)
        out = jnp.dot(og, w_ref[...], preferred_element_type=F32)
        og_ref[...] = og
        out_ref[...] = out
        x1_ref[...] = x_ref[...] + gt_ref[0] * out

    row = pl.BlockSpec((tm, d), lambda i: (i, 0))
    return pl.pallas_call(
        body, out_shape=[jax.ShapeDtypeStruct((n, d), F32), jax.ShapeDtypeStruct((n, d), BF16),
                         jax.ShapeDtypeStruct((n, d), F32)],
        grid=(n // tm,),
        in_specs=[row, row, row, _bc_spec(gate, ncb), pl.BlockSpec((d, d), lambda i: (0, 0))],
        out_specs=[row, row, row],
        compiler_params=_params(("parallel",), VMEM_LIMIT), name=name)(o, p0, x0, gate, w_out)


def mla_post_bwd(dx1, out, og, o, p0, gate, w_out, n_ctx, name="l0_post_bwd"):
    n, d = o.shape
    tm = math.gcd(ROW_TILE, n, n_ctx)
    ncb = n_ctx // tm

    def body(dx_ref, out_ref, og_ref, o_ref, z_ref, gt_ref, w_ref, do_ref, dz_ref, dgt_ref, dw_ref):
        i = pl.program_id(0)

        @pl.when(i == 0)
        def _():
            dw_ref[...] = jnp.zeros_like(dw_ref)

        @pl.when((i == 0) | (i == ncb))
        def _():
            dgt_ref[...] = jnp.zeros_like(dgt_ref)

        dx = dx_ref[...]
        dgt_ref[0] += jnp.sum(dx * out_ref[...], axis=0, keepdims=True)
        d_out16 = (gt_ref[0] * dx).astype(BF16)
        dw_ref[...] += lax.dot_general(og_ref[...], d_out16, (((0,), (0,)), ((), ())), preferred_element_type=F32)
        d_og = lax.dot_general(d_out16, w_ref[...], NT_DIMS, preferred_element_type=F32)
        _, gate_vjp = jax.vjp(lambda o_, z_: f_gate(o_, z_), o_ref[...], z_ref[...])
        d_o, d_z = gate_vjp((d_og,))
        do_ref[...] = d_o
        dz_ref[...] = d_z

    row = pl.BlockSpec((tm, d), lambda i: (i, 0))
    mat = pl.BlockSpec((d, d), lambda i: (0, 0))
    return pl.pallas_call(
        body, out_shape=[jax.ShapeDtypeStruct((n, d), F32), jax.ShapeDtypeStruct((n, d), F32),
                         jax.ShapeDtypeStruct(gate.shape, F32), jax.ShapeDtypeStruct((d, d), F32)],
        grid=(n // tm,),
        in_specs=[row, row, row, row, row, _bc_spec(gate, ncb), mat],
        out_specs=[row, row, _bc_spec(gate, ncb), mat],
        compiler_params=_params(("arbitrary",), VMEM_LIMIT), name=name)(dx1, out, og, o, p0, gate, w_out)


def s5_tail(y_ssm, p1, x1p, target, n_ctx, d_vec, b_glu, gate, final_g, w_glu, w_out, name="l1_tail"):
    n, d = y_ssm.shape
    tm = math.gcd(ROW_TILE, n, n_ctx)
    off = n_ctx // tm
    tn_dims = (((0,), (0,)), ((), ()))

    def row_loss(x, g, t):
        e = _rms(x) * g - t
        return 0.5 * (e * e) * (1.0 / d)

    def body(y_ref, u_ref, z_ref, x1_ref, t_ref, d_ref, b_ref, gt_ref, fg_ref, wg_ref, wo_ref,
             l_ref, dx_ref, dy_ref, du_ref, dz_ref, dfg_ref, dgt_ref, db_ref, dd_ref, dwg_ref, dwo_ref):
        @pl.when(pl.program_id(0) == 0)
        def _():
            for r in (l_ref, dfg_ref, dgt_ref, db_ref, dd_ref, dwg_ref, dwo_ref):
                r[...] = jnp.zeros_like(r)

        u, z, tgt, gt = u_ref[...], z_ref[...], t_ref[...], gt_ref[...]
        (ya,), act_vjp = jax.vjp(lambda y_, u_, d_: f_s5_act(y_, u_, d_), y_ref[...], u, d_ref[...])
        ya16 = ya.astype(BF16)
        gl = jnp.dot(ya16, wg_ref[...], preferred_element_type=F32)
        (y3,), glu_vjp = jax.vjp(lambda a_, g_, z_, b_: f_s5_glu(a_, g_, z_, b_), ya, gl, z, b_ref[...])
        y3_16 = y3.astype(BF16)
        out1 = jnp.dot(y3_16, wo_ref[...], preferred_element_type=F32)
        lterm, loss_vjp = jax.vjp(lambda x_, g_: row_loss(x_, g_, tgt), x1_ref[...] + gt * out1, fg_ref[...])
        dx2, dfg = loss_vjp(jnp.ones_like(lterm))
        l_ref[...] += jnp.sum(lterm, axis=0, keepdims=True)
        dfg_ref[...] += dfg
        dx_ref[...] = dx2
        dgt_ref[...] += jnp.sum(dx2 * out1, axis=0, keepdims=True)
        d_out16 = (gt * dx2).astype(BF16)
        dwo_ref[...] += lax.dot_general(y3_16, d_out16, tn_dims, preferred_element_type=F32)
        d_y3 = lax.dot_general(d_out16, wo_ref[...], NT_DIMS, preferred_element_type=F32)
        d_ya, d_gl, d_z, d_b = glu_vjp((d_y3,))
        dz_ref[...] = d_z
        db_ref[...] += d_b
        d_gl16 = d_gl.astype(BF16)
        dwg_ref[...] += lax.dot_general(ya16, d_gl16, tn_dims, preferred_element_type=F32)
        d_ya = d_ya + lax.dot_general(d_gl16, wg_ref[...], NT_DIMS, preferred_element_type=F32)
        d_y, d_u, d_d = act_vjp((d_ya,))
        dy_ref[...] = d_y
        du_ref[...] = d_u
        dd_ref[...] += d_d

    row = pl.BlockSpec((tm, d), lambda i: (i, 0))
    vecs = pl.BlockSpec((1, d), lambda i: (0, 0))
    mat = pl.BlockSpec((d, d), lambda i: (0, 0))
    return pl.pallas_call(
        body,
        out_shape=[jax.ShapeDtypeStruct((1, d), F32)] + [jax.ShapeDtypeStruct((n, d), F32)] * 4
        + [jax.ShapeDtypeStruct((1, d), F32)] * 4 + [jax.ShapeDtypeStruct((d, d), F32)] * 2,
        grid=(n // tm,),
        in_specs=[row, pl.BlockSpec((tm, d), lambda i: (i + off, 0)), pl.BlockSpec((tm, d), lambda i: (i + off, 1)),
                  pl.BlockSpec((tm, d), lambda i: (i + off, 0)), row, vecs, vecs, vecs, vecs, mat, mat],
        out_specs=[vecs, row, row, row, row, vecs, vecs, vecs, vecs, mat, mat],
        compiler_params=_params(("arbitrary",), VMEM_LIMIT), name=name)(
            y_ssm, p1, p1, x1p, target, d_vec, b_glu, gate, final_g, w_glu, w_out)


NT_DIMS = (((1,), (1,)), ((), ()))
HEAD_LANES = 128
N_PAIRS = MLA_HEADS // 2


def _own_lanes(shape, hh):
    lane = lax.broadcasted_iota(jnp.int32, shape, len(shape) - 1)
    return (lane < V_HEAD_DIM) if hh == 0 else (lane >= V_HEAD_DIM)


def _delta_lane(hh):
    return V_HEAD_DIM if hh == 0 else 0


def _rope_tiles(x, cos, sin_next, sin_prev, inverse):
    width = x.shape[-1]
    reps = width // HEAD_LANES
    c, sn, sp = (jnp.tile(t, (1, reps)) for t in (cos, sin_next, sin_prev))
    if inverse:
        return x * c + pltpu.roll(x * sn, 8, 1) + pltpu.roll(x * sp, width - 8, 1)
    return x * c + pltpu.roll(x, width - 8, 1) * sn + pltpu.roll(x, 8, 1) * sp


STAT_LANE = QK_DIM


def _with_stat(x16, col, lane0):
    hi = col.astype(BF16)
    r1 = col - hi.astype(F32)
    mid = r1.astype(BF16)
    lo = (r1 - mid.astype(F32)).astype(BF16)
    lane = lax.broadcasted_iota(jnp.int32, x16.shape, 1)
    return jnp.where(lane == lane0, hi, jnp.where(lane == lane0 + 1, mid, jnp.where(lane == lane0 + 2, lo, x16)))


def attn_fwd(qb, kb, vb, n_ctx):
    T = qb.shape[0]
    tq = math.gcd(ROW_TILE, n_ctx)
    nq, ncb = T // tq, n_ctx // tq

    def body(q_ref, k_ref, v_ref, o_ref, lse_ref, qs_ref):
        qi = pl.program_id(1)

        def rows(n_keys):
            v = v_ref[:n_keys, :]
            outs = []
            for hh in range(2):
                hs = slice(hh * HEAD_LANES, (hh + 1) * HEAD_LANES)
                s = lax.dot_general(q_ref[:, hs], k_ref[:n_keys, hs], NT_DIMS,
                                    preferred_element_type=F32) * SOFTMAX_SCALE
                m = jnp.max(s, axis=-1, keepdims=True)
                p = jnp.exp(s - m)
                l = jnp.sum(p, axis=-1, keepdims=True)
                outs.append(jnp.dot(p.astype(BF16), v, preferred_element_type=F32) / l)
                lse = m + jnp.log(l)
                lse_ref[hh] = lse
                qs_ref[:, hs] = _with_stat(q_ref[:, hs], lse * (-1.0 / SOFTMAX_SCALE), STAT_LANE)
            o_ref[...] = jnp.where(_own_lanes(outs[0].shape, 0), outs[0], outs[1])

        pl.when(qi < ncb)(lambda: rows(n_ctx))
        pl.when(qi >= ncb)(lambda: rows(T))

    return pl.pallas_call(
        body,
        out_shape=[jax.ShapeDtypeStruct((T, MLA_HEADS * V_HEAD_DIM), F32),
                   jax.ShapeDtypeStruct((MLA_HEADS, T, 1), F32), jax.ShapeDtypeStruct(qb.shape, BF16)],
        grid=(N_PAIRS, nq),
        in_specs=[pl.BlockSpec((tq, 2 * HEAD_LANES), lambda h, i: (i, h)),
                  pl.BlockSpec((T, 2 * HEAD_LANES), lambda h, i: (0, h)),
                  pl.BlockSpec((T, 2 * V_HEAD_DIM), lambda h, i: (0, h))],
        out_specs=[pl.BlockSpec((tq, 2 * V_HEAD_DIM), lambda h, i: (i, h)),
                   pl.BlockSpec((2, tq, 1), lambda h, i: (h, i, 0)),
                   pl.BlockSpec((tq, 2 * HEAD_LANES), lambda h, i: (i, h))],
        compiler_params=_params(("parallel", "parallel"), VMEM_LIMIT), name="attn_fwd")(qb, kb, vb)


def attn_bwd_dq(qb, kb, vb, o, do, lse, tabs, n_ctx):
    T = qb.shape[0]
    tq = math.gcd(ROW_TILE, n_ctx)
    nq, ncb = T // tq, n_ctx // tq

    def body(q_ref, k_ref, v_ref, o_ref, do_ref, lse_ref, c_ref, sn_ref, sp_ref, dq_ref, dos_ref):
        qi = pl.program_id(1)

        def rows(n_keys):
            v = v_ref[:n_keys, :]
            dqs = []
            for hh in range(2):
                hs = slice(hh * HEAD_LANES, (hh + 1) * HEAD_LANES)
                k = k_ref[:n_keys, hs]
                do = jnp.where(_own_lanes(do_ref.shape, hh), do_ref[...], 0.0)
                delta = jnp.sum(do * o_ref[...], axis=-1, keepdims=True)
                s = lax.dot_general(q_ref[:, hs], k, NT_DIMS, preferred_element_type=F32) * SOFTMAX_SCALE
                p = jnp.exp(s - lse_ref[hh])
                do16 = do.astype(BF16)
                dp = lax.dot_general(do16, v, NT_DIMS, preferred_element_type=F32)
                ds = p * (dp - delta) * SOFTMAX_SCALE
                dqs.append(jnp.dot(ds.astype(BF16), k, preferred_element_type=F32))
                dos_ref[:, hs] = _with_stat(do16, delta, _delta_lane(hh))
            dq = jnp.concatenate(dqs, axis=1)
            dq_ref[...] = _rope_tiles(dq, c_ref[...], sn_ref[...], sp_ref[...], True).astype(BF16)

        pl.when(qi < ncb)(lambda: rows(n_ctx))
        pl.when(qi >= ncb)(lambda: rows(T))

    tab = pl.BlockSpec((tq, HEAD_LANES), lambda h, i: (i, 0))
    return pl.pallas_call(
        body,
        out_shape=[jax.ShapeDtypeStruct((T, MLA_HEADS * HEAD_LANES), BF16)] * 2,
        grid=(N_PAIRS, nq),
        in_specs=[pl.BlockSpec((tq, 2 * HEAD_LANES), lambda h, i: (i, h)),
                  pl.BlockSpec((T, 2 * HEAD_LANES), lambda h, i: (0, h)),
                  pl.BlockSpec((T, 2 * V_HEAD_DIM), lambda h, i: (0, h)),
                  pl.BlockSpec((tq, 2 * V_HEAD_DIM), lambda h, i: (i, h)),
                  pl.BlockSpec((tq, 2 * V_HEAD_DIM), lambda h, i: (i, h)),
                  pl.BlockSpec((2, tq, 1), lambda h, i: (h, i, 0)), tab, tab, tab],
        out_specs=[pl.BlockSpec((tq, 2 * HEAD_LANES), lambda h, i: (i, h))] * 2,
        compiler_params=_params(("parallel", "parallel"), VMEM_LIMIT), name="attn_bwd_dq")(
            qb, kb, vb, o, do, lse, *tabs)


def attn_bwd_dkv(qs, kb, vb, dos, n_ctx):
    T = qs.shape[0]
    tq = math.gcd(ROW_TILE, n_ctx)
    nq, ncb = T // tq, n_ctx // tq

    def body(q_ref, do_ref, k_ref, v_ref, dk_ref, dv_ref):
        kj = pl.program_id(1)

        def cols(first):
            v = v_ref[...]
            lane = lax.broadcasted_iota(jnp.int32, v.shape, 1)
            dvs = []
            for hh in range(2):
                hs = slice(hh * HEAD_LANES, (hh + 1) * HEAD_LANES)
                q = q_ref[first:, hs]
                do16 = do_ref[first:, hs]
                in_delta = (lane >= _delta_lane(hh)) & (lane < _delta_lane(hh) + 3)
                v_minus = jnp.where(in_delta, -jnp.ones_like(v), v)
                pt = jnp.exp(lax.dot_general(k_ref[:, hs], q, NT_DIMS, preferred_element_type=F32) * SOFTMAX_SCALE)
                dvs.append(jnp.dot(pt.astype(BF16), do16, preferred_element_type=F32))
                dst = pt * lax.dot_general(v_minus, do16, NT_DIMS, preferred_element_type=F32) * SOFTMAX_SCALE
                dk_ref[:, hs] = jnp.dot(dst.astype(BF16), q, preferred_element_type=F32)
            dv_ref[...] = jnp.where(_own_lanes(dvs[0].shape, 0), dvs[0], dvs[1])

        pl.when(kj < ncb)(lambda: cols(0))
        pl.when(kj >= ncb)(lambda: cols(n_ctx))

    return pl.pallas_call(
        body,
        out_shape=[jax.ShapeDtypeStruct((T, MLA_HEADS * HEAD_LANES), F32),
                   jax.ShapeDtypeStruct((T, MLA_HEADS * V_HEAD_DIM), F32)],
        grid=(N_PAIRS, nq),
        in_specs=[pl.BlockSpec((T, 2 * HEAD_LANES), lambda h, j: (0, h)),
                  pl.BlockSpec((T, 2 * HEAD_LANES), lambda h, j: (0, h)),
                  pl.BlockSpec((tq, 2 * HEAD_LANES), lambda h, j: (j, h)),
                  pl.BlockSpec((tq, 2 * V_HEAD_DIM), lambda h, j: (j, h))],
        out_specs=[pl.BlockSpec((tq, 2 * HEAD_LANES), lambda h, j: (j, h)),
                   pl.BlockSpec((tq, 2 * V_HEAD_DIM), lambda h, j: (j, h))],
        compiler_params=_params(("parallel", "parallel"), VMEM_LIMIT), name="attn_bwd_dkv")(
            qs, dos, kb, vb)


def _split_bf16(x):
    hi = x.astype(BF16)
    return hi, (x - hi.astype(F32)).astype(BF16)


def q_heads(cq, gain, w_uq_p, tabs, name="l0_uq"):
    T, K = cq.arr.shape[0], cq.width
    N = w_uq_p.shape[1]
    tm = math.gcd(ROW_TILE, T)

    def body(a_ref, g_ref, w_ref, c_ref, sn_ref, sp_ref, o_ref, n_ref):
        qn = f_rms(a_ref[...], g_ref[0])[0].astype(BF16)
        n_ref[...] = qn
        acc = jnp.dot(qn, w_ref[...], preferred_element_type=F32)
        o_ref[...] = _rope_tiles(acc, c_ref[...], sn_ref[...], sp_ref[...], False).astype(BF16)

    tab = pl.BlockSpec((tm, HEAD_LANES), lambda i: (i, 0))
    return pl.pallas_call(
        body, out_shape=[jax.ShapeDtypeStruct((T, N), BF16), jax.ShapeDtypeStruct((T, K), BF16)], grid=(T // tm,),
        in_specs=[cq.spec(tm), pl.BlockSpec((1, 1, K), lambda i: (0, 0, 0)), pl.BlockSpec((K, N), lambda i: (0, 0)),
                  tab, tab, tab],
        out_specs=[pl.BlockSpec((tm, N), lambda i: (i, 0)), pl.BlockSpec((tm, K), lambda i: (i, 0))],
        compiler_params=_params(("parallel",), VMEM_LIMIT), name=name)(cq.arr, gain, w_uq_p, *tabs)


def kv_heads(ckv, gain, w_kn_p, w_v, kr, spread, tabs, name="l0_ukv"):
    T, K = ckv.arr.shape[0], ckv.width
    N = w_kn_p.shape[1]
    NV = w_v.shape[1]
    tm = math.gcd(ROW_TILE, T)

    def body(a_ref, g_ref, wk_ref, wv_ref, kr_ref, e_ref, c_ref, sn_ref, sp_ref, k_ref, v_ref, n_ref):
        a = f_rms(a_ref[...], g_ref[0])[0].astype(BF16)
        n_ref[...] = a
        hi, lo = _split_bf16(kr_ref[...])
        acc = (jnp.dot(a, wk_ref[...], preferred_element_type=F32)
               + jnp.dot(hi, e_ref[...], preferred_element_type=F32)
               + jnp.dot(lo, e_ref[...], preferred_element_type=F32))
        roped = _rope_tiles(acc, c_ref[...], sn_ref[...], sp_ref[...], False)
        lane = lax.broadcasted_iota(jnp.int32, roped.shape, 1) % HEAD_LANES
        k_ref[...] = jnp.where((lane >= STAT_LANE) & (lane < STAT_LANE + 3), 1.0, roped).astype(BF16)
        v_ref[...] = jnp.dot(a, wv_ref[...], preferred_element_type=F32).astype(BF16)

    tab = pl.BlockSpec((tm, HEAD_LANES), lambda i: (i, 0))
    return pl.pallas_call(
        body, out_shape=[jax.ShapeDtypeStruct((T, N), BF16), jax.ShapeDtypeStruct((T, NV), BF16),
                         jax.ShapeDtypeStruct((T, K), BF16)], grid=(T // tm,),
        in_specs=[ckv.spec(tm), pl.BlockSpec((1, 1, K), lambda i: (0, 0, 0)), pl.BlockSpec((K, N), lambda i: (0, 0)),
                  pl.BlockSpec((K, NV), lambda i: (0, 0)), pl.BlockSpec((tm, QK_ROPE_DIM), lambda i: (i, 0)),
                  pl.BlockSpec((QK_ROPE_DIM, N), lambda i: (0, 0)), tab, tab, tab],
        out_specs=[pl.BlockSpec((tm, N), lambda i: (i, 0)), pl.BlockSpec((tm, NV), lambda i: (i, 0)),
                   pl.BlockSpec((tm, K), lambda i: (i, 0))],
        compiler_params=_params(("parallel",), VMEM_LIMIT), name=name)(ckv.arr, gain, w_kn_p, w_v, kr, spread, *tabs)


def heads_unrope(d, tabs, spread=None, name="unrope"):
    T, N = d.shape
    tm = math.gcd(ROW_TILE, T)

    def body(*refs):
        if spread is None:
            d_ref, c_ref, sn_ref, sp_ref, o_ref = refs
        else:
            d_ref, c_ref, sn_ref, sp_ref, e_ref, o_ref, kr_ref = refs
        g = _rope_tiles(d_ref[...], c_ref[...], sn_ref[...], sp_ref[...], True)
        o_ref[...] = g.astype(BF16)
        if spread is not None:
            hi, lo = _split_bf16(g)
            kr_ref[...] = (lax.dot_general(hi, e_ref[...], NT_DIMS, preferred_element_type=F32)
                           + lax.dot_general(lo, e_ref[...], NT_DIMS, preferred_element_type=F32))

    tab = pl.BlockSpec((tm, HEAD_LANES), lambda i: (i, 0))
    row = pl.BlockSpec((tm, N), lambda i: (i, 0))
    ins, in_specs = [d, *tabs], [row, tab, tab, tab]
    out_shape, out_specs = [jax.ShapeDtypeStruct((T, N), BF16)], [row]
    if spread is not None:
        ins.append(spread)
        in_specs.append(pl.BlockSpec(spread.shape, lambda i: (0, 0)))
        out_shape.append(jax.ShapeDtypeStruct((T, spread.shape[0]), F32))
        out_specs.append(pl.BlockSpec((tm, spread.shape[0]), lambda i: (i, 0)))
    return pl.pallas_call(
        body, out_shape=out_shape, grid=(T // tm,), in_specs=in_specs, out_specs=out_specs,
        compiler_params=_params(("parallel",), VMEM_LIMIT), name=name)(*ins)


def _cmul(ar, ai, br, bi):
    return ar * br - ai * bi, ar * bi + ai * br


def s5_chain(finals, s0, a, n_steps, reverse, name):
    W = finals.shape[-1]
    first = N_SEG - 1 if reverse else 0

    def body(f_ref, s0_ref, a_ref, c_ref):
        pr, pi = jnp.ones((1, W), F32), jnp.zeros((1, W), F32)
        br, bi = a_ref[0], a_ref[1]
        n = n_steps
        while n:
            if n & 1:
                pr, pi = _cmul(pr, pi, br, bi)
            br, bi = _cmul(br, bi, br, bi)
            n >>= 1
        fr, fi = f_ref[0], f_ref[1]
        row = lax.broadcasted_iota(jnp.int32, (N_SEG, W), 0)
        s0r = jnp.broadcast_to(s0_ref[0], (N_SEG, W))
        s0i = jnp.broadcast_to(s0_ref[1], (N_SEG, W))
        cr = jnp.where(row == first, s0r, 0.0)
        ci = jnp.where(row == first, s0i, 0.0)
        shift = N_SEG - 1 if reverse else 1
        for _ in range(N_SEG - 1):
            mr, mi = _cmul(pr, pi, cr, ci)
            tr = pltpu.roll(fr + mr, shift, 0)
            ti = pltpu.roll(fi + mi, shift, 0)
            cr = jnp.where(row == first, s0r, tr)
            ci = jnp.where(row == first, s0i, ti)
        c_ref[0] = cr
        c_ref[1] = ci

    return pl.pallas_call(body, out_shape=jax.ShapeDtypeStruct((2, N_SEG, W), F32), name=name)(finals, s0, a)


def _scan_chunk(bur, bui, st_ref, a_ref, n_steps, reverse):
    for lc in range(S5_LANES // BLK_ST):
        sl = slice(lc * BLK_ST, (lc + 1) * BLK_ST)
        lr = jnp.broadcast_to(a_ref[0, :, sl], (N_SEG, BLK_ST))
        li = jnp.broadcast_to(a_ref[1, :, sl], (N_SEG, BLK_ST))

        def step(jj, carry, sl=sl, lr=lr, li=li):
            sr, si = carry
            j = (n_steps - 1 - jj) if reverse else jj
            r0 = pl.multiple_of(j * N_SEG, N_SEG)
            nr = lr * sr - li * si + bur[pl.ds(r0, N_SEG), sl]
            ni = lr * si + li * sr + bui[pl.ds(r0, N_SEG), sl]
            bur[pl.ds(r0, N_SEG), sl] = nr
            bui[pl.ds(r0, N_SEG), sl] = ni
            return nr, ni

        sr, si = lax.fori_loop(0, n_steps, step, (st_ref[0, :, sl], st_ref[1, :, sl]))
        st_ref[0, :, sl] = sr
        st_ref[1, :, sl] = si


def _project_in(x16, w_re, w_im, bur, bui, adjoint):
    for gb in range(N_BLOCKS):
        xb = x16[:, gb * BLK_CH:(gb + 1) * BLK_CH]
        sl = slice(gb * BLK_ST, (gb + 1) * BLK_ST)
        if adjoint:
            dn = (((1,), (1,)), ((), ()))
            bur[:, sl] = lax.dot_general(xb, w_re[gb], dn, preferred_element_type=F32)
            bui[:, sl] = -lax.dot_general(xb, w_im[gb], dn, preferred_element_type=F32)
        else:
            bur[:, sl] = jnp.dot(xb, w_re[gb], preferred_element_type=F32)
            bui[:, sl] = jnp.dot(xb, w_im[gb], preferred_element_type=F32)


def s5_scan(act, w_re, w_im, a, init, *, reverse, adjoint=False, c_re=None, c_im=None, add=None,
            want_ckpt=False, rows=None, name):
    act_off, N = rows if rows is not None else (0, act.shape[0])
    R = math.gcd(ROW_TILE, N, act_off)
    nch, jc = N // R, R // N_SEG
    with_out = c_re is not None

    def chunk(i):
        return (nch - 1 - i) if reverse else i

    def body(*refs):
        act_ref, wre_ref, wim_ref, a_ref, init_ref = refs[:5]
        k = 5
        if with_out:
            cre_ref, cim_ref = refs[k:k + 2]
            k += 2
        if add is not None:
            add_ref = refs[k]
            k += 1
        if with_out:
            out_ref = refs[k]
            k += 1
        if want_ckpt:
            ck_ref = refs[k]
            k += 1
        fin_ref, bur, bui = refs[k:k + 3]

        @pl.when(pl.program_id(0) == 0)
        def _():
            fin_ref[...] = init_ref[...]

        if want_ckpt:
            ck_ref[0] = fin_ref[...]
        _project_in(act_ref[...].astype(BF16), wre_ref, wim_ref, bur, bui, adjoint)
        _scan_chunk(bur, bui, fin_ref, a_ref, jc, reverse)
        if with_out:
            for gb in range(N_BLOCKS):
                sl = slice(gb * BLK_ST, (gb + 1) * BLK_ST)
                y = (jnp.dot(bur[:, sl].astype(BF16), cre_ref[gb], preferred_element_type=F32)
                     - jnp.dot(bui[:, sl].astype(BF16), cim_ref[gb], preferred_element_type=F32))
                cs = slice(gb * BLK_CH, (gb + 1) * BLK_CH)
                if add is not None:
                    y = y + add_ref[:, cs]
                out_ref[:, cs] = y

    row_spec = pl.BlockSpec((R, D_MODEL), lambda i: (chunk(i), 0))
    act_spec = pl.BlockSpec((R, D_MODEL), lambda i: (chunk(i) + act_off // R, 0))
    w_spec = pl.BlockSpec(w_re.shape, lambda i: (0, 0, 0))
    st_spec = pl.BlockSpec((2, N_SEG, S5_LANES), lambda i: (0, 0, 0))
    ins = [act, w_re, w_im, a, init]
    in_specs = [act_spec, w_spec, w_spec, pl.BlockSpec((2, 1, S5_LANES), lambda i: (0, 0, 0)), st_spec]
    if with_out:
        ins += [c_re, c_im]
        in_specs += [pl.BlockSpec(c_re.shape, lambda i: (0, 0, 0))] * 2
    if add is not None:
        ins.append(add)
        in_specs.append(row_spec)
    out_shape, out_specs = [], []
    if with_out:
        out_shape.append(jax.ShapeDtypeStruct((N, D_MODEL), F32))
        out_specs.append(row_spec)
    if want_ckpt:
        out_shape.append(jax.ShapeDtypeStruct((nch, 2, N_SEG, S5_LANES), F32))
        out_specs.append(pl.BlockSpec((1, 2, N_SEG, S5_LANES), lambda i: (chunk(i), 0, 0, 0)))
    out_shape.append(jax.ShapeDtypeStruct((2, N_SEG, S5_LANES), F32))
    out_specs.append(st_spec)
    res = pl.pallas_call(
        body, out_shape=out_shape, grid=(nch,), in_specs=in_specs, out_specs=out_specs,
        scratch_shapes=[pltpu.VMEM((R, S5_LANES), F32), pltpu.VMEM((R, S5_LANES), F32)],
        compiler_params=_params(("arbitrary",), VMEM_LIMIT), name=name)(*ins)
    res = list(res)
    out = res.pop(0) if with_out else None
    ckpt = res.pop(0) if want_ckpt else None
    return out, ckpt, res[0]


def s5_grads(dy, u, ckpt, b_re, b_im, c_re, c_im, lam, init_adj, *, reverse, add=None, u_off=0, name):
    N = dy.shape[0]
    R = math.gcd(ROW_TILE, N, u_off)
    nch, jc = N // R, R // N_SEG
    W = S5_LANES

    def chunk(i):
        return i if reverse else (nch - 1 - i)

    def body(*refs):
        dy_ref, u_ref, ck_ref, bre_ref, bim_ref, cre_ref, cim_ref, lam_ref, init_ref = refs[:9]
        k = 9
        if add is not None:
            add_ref = refs[k]
            k += 1
        du_ref, dlam_ref, dbre_ref, dbim_ref, dcre_ref, dcim_ref, fin_ref = refs[k:k + 7]
        sr_buf, si_buf, er_buf, ei_buf, st_buf = refs[k + 7:k + 12]

        @pl.when(pl.program_id(0) == 0)
        def _():
            fin_ref[...] = init_ref[...]
            dlam_ref[...] = jnp.zeros_like(dlam_ref)
            dbre_ref[...] = jnp.zeros_like(dbre_ref)
            dbim_ref[...] = jnp.zeros_like(dbim_ref)
            dcre_ref[...] = jnp.zeros_like(dcre_ref)
            dcim_ref[...] = jnp.zeros_like(dcim_ref)

        u16 = u_ref[...].astype(BF16)
        dy16 = dy_ref[...].astype(BF16)
        st_buf[...] = ck_ref[0]
        _project_in(u16, bre_ref, bim_ref, sr_buf, si_buf, False)
        _scan_chunk(sr_buf, si_buf, st_buf, lam_ref, jc, reverse)
        _project_in(dy16, cre_ref, cim_ref, er_buf, ei_buf, True)
        for lc in range(W // BLK_ST):
            sl = slice(lc * BLK_ST, (lc + 1) * BLK_ST)
            lr = jnp.broadcast_to(lam_ref[0, :, sl], (N_SEG, BLK_ST))
            li = jnp.broadcast_to(lam_ref[1, :, sl], (N_SEG, BLK_ST))

            def one(r0, spr, spi, carry, sl=sl, lr=lr, li=li):
                gr, gi, ar, ai = carry
                nr = er_buf[pl.ds(r0, N_SEG), sl] + lr * gr + li * gi
                ni = ei_buf[pl.ds(r0, N_SEG), sl] + lr * gi - li * gr
                er_buf[pl.ds(r0, N_SEG), sl] = nr
                ei_buf[pl.ds(r0, N_SEG), sl] = ni
                return nr, ni, ar + spr * nr + spi * ni, ai + spr * ni - spi * nr

            def step(ff, carry, sl=sl, one=one):
                f = jc - 1 - ff
                j = (jc - 1 - f) if reverse else f
                jp = (j + 1) if reverse else (j - 1)
                r0 = pl.multiple_of(j * N_SEG, N_SEG)
                p0 = pl.multiple_of(jp * N_SEG, N_SEG)
                return one(r0, sr_buf[pl.ds(p0, N_SEG), sl], si_buf[pl.ds(p0, N_SEG), sl], carry)

            carry = (fin_ref[0, :, sl], fin_ref[1, :, sl], dlam_ref[0, :, sl], dlam_ref[1, :, sl])
            carry = lax.fori_loop(0, jc - 1, step, carry)
            r_first = (jc - 1) * N_SEG if reverse else 0
            gr, gi, ar, ai = one(r_first, ck_ref[0, 0, :, sl], ck_ref[0, 1, :, sl], carry)
            fin_ref[0, :, sl] = gr
            fin_ref[1, :, sl] = gi
            dlam_ref[0, :, sl] = ar
            dlam_ref[1, :, sl] = ai
        tn = (((0,), (0,)), ((), ()))
        nt = (((1,), (1,)), ((), ()))
        for gb in range(N_BLOCKS):
            sl = slice(gb * BLK_ST, (gb + 1) * BLK_ST)
            cs = slice(gb * BLK_CH, (gb + 1) * BLK_CH)
            gr16 = er_buf[:, sl].astype(BF16)
            gi16 = ei_buf[:, sl].astype(BF16)
            du = (lax.dot_general(gr16, bre_ref[gb], nt, preferred_element_type=F32)
                  + lax.dot_general(gi16, bim_ref[gb], nt, preferred_element_type=F32))
            if add is not None:
                du = du + add_ref[:, cs]
            du_ref[:, cs] = du
            ub, dyb = u16[:, cs], dy16[:, cs]
            dbre_ref[gb] += lax.dot_general(ub, gr16, tn, preferred_element_type=F32)
            dbim_ref[gb] += lax.dot_general(ub, gi16, tn, preferred_element_type=F32)
            dcre_ref[gb] += lax.dot_general(sr_buf[:, sl].astype(BF16), dyb, tn, preferred_element_type=F32)
            dcim_ref[gb] -= lax.dot_general(si_buf[:, sl].astype(BF16), dyb, tn, preferred_element_type=F32)

    row_spec = pl.BlockSpec((R, D_MODEL), lambda i: (chunk(i), 0))
    st_spec = pl.BlockSpec((2, N_SEG, W), lambda i: (0, 0, 0))
    wb_spec = pl.BlockSpec(b_re.shape, lambda i: (0, 0, 0))
    wc_spec = pl.BlockSpec(c_re.shape, lambda i: (0, 0, 0))
    ins = [dy, u, ckpt, b_re, b_im, c_re, c_im, lam, init_adj]
    u_spec = pl.BlockSpec((R, D_MODEL), lambda i: (chunk(i) + u_off // R, 0))
    in_specs = [row_spec, u_spec, pl.BlockSpec((1, 2, N_SEG, W), lambda i: (chunk(i), 0, 0, 0)),
                wb_spec, wb_spec, wc_spec, wc_spec, pl.BlockSpec((2, 1, W), lambda i: (0, 0, 0)), st_spec]
    if add is not None:
        ins.append(add)
        in_specs.append(row_spec)
    out_shape = [jax.ShapeDtypeStruct((N, D_MODEL), F32), jax.ShapeDtypeStruct((2, N_SEG, W), F32),
                 jax.ShapeDtypeStruct(b_re.shape, F32), jax.ShapeDtypeStruct(b_re.shape, F32),
                 jax.ShapeDtypeStruct(c_re.shape, F32), jax.ShapeDtypeStruct(c_re.shape, F32),
                 jax.ShapeDtypeStruct((2, N_SEG, W), F32)]
    out_specs = [row_spec, st_spec, wb_spec, wb_spec, wc_spec, wc_spec, st_spec]
    return pl.pallas_call(
        body, out_shape=out_shape, grid=(nch,), in_specs=in_specs, out_specs=out_specs,
        scratch_shapes=[pltpu.VMEM((R, W), F32) for _ in range(4)] + [pltpu.VMEM((2, N_SEG, W), F32)],
        compiler_params=_params(("arbitrary",), VMEM_LIMIT), name=name)(*ins)


def adamw(w, g, m, v, name="adamw", after=None):
    n, d = w.shape
    lanes = -(-d // 128) * 128
    tm = n
    while tm * lanes * 4 > (1 << 20) and tm % 16 == 0:
        tm //= 2
    c1 = 1.0 - ADAM_B1 ** ADAM_STEP
    c2 = 1.0 - ADAM_B2 ** ADAM_STEP

    def body(w_ref, g_ref, m_ref, v_ref, *rest):
        d_ref, nm_ref, nv_ref = rest[-3:]
        g_ = g_ref[...]
        m_ = ADAM_B1 * m_ref[...] + (1.0 - ADAM_B1) * g_
        v_ = ADAM_B2 * v_ref[...] + (1.0 - ADAM_B2) * (g_ * g_)
        d_ref[...] = -ADAM_LR * ((m_ / c1) / (jnp.sqrt(v_ / c2) + ADAM_EPS) + ADAM_WD * w_ref[...])
        nm_ref[...] = m_
        nv_ref[...] = v_

    spec = pl.BlockSpec((tm, d), lambda i: (i, 0))
    extra = [] if after is None else [after]
    return pl.pallas_call(
        body, out_shape=[jax.ShapeDtypeStruct((n, d), F32)] * 3, grid=(n // tm,),
        in_specs=[spec] * 4 + [pl.BlockSpec(memory_space=pl.ANY)] * len(extra), out_specs=[spec] * 3,
        compiler_params=_params(("parallel",), VMEM_LIMIT), name=name)(w, g, m, v, *extra)


def _coords():
    return lax.axis_index("x"), lax.axis_index("y"), lax.axis_index("c")


def exchange(arrays, out_shapes, remote, local, name, aliases=None):
    n_in, n_out, n_rem, n_loc = len(arrays), len(out_shapes), len(remote), len(local)

    def at(ref, idx):
        return ref if idx is None else ref.at[idx]

    def body(*refs):
        ins, outs = refs[:n_in], refs[n_in:n_in + n_out]
        send_sems, recv_sems, local_sems = refs[n_in + n_out:]
        me = _coords()
        sends, recvs = [], []
        for k, (flip, ii, src_at, oi, dst_at) in enumerate(remote):
            peer = (me[0] ^ flip[0], me[1] ^ flip[1], me[2] ^ flip[2])
            src = at(ins[ii], src_at(me, peer))
            sends.append(pltpu.make_async_remote_copy(
                src_ref=src, dst_ref=at(outs[oi], dst_at(me)), send_sem=send_sems.at[k], recv_sem=recv_sems.at[k],
                device_id=peer, device_id_type=MESH))
            recvs.append(pltpu.make_async_remote_copy(
                src_ref=src, dst_ref=at(outs[oi], dst_at(peer)), send_sem=send_sems.at[k], recv_sem=recv_sems.at[k],
                device_id=peer, device_id_type=MESH))
        locs = [pltpu.make_async_copy(at(ins[ii], src_at(me)), at(outs[oi], dst_at(me)), local_sems.at[k])
                for k, (ii, src_at, oi, dst_at) in enumerate(local)]
        for cp in locs + sends:
            cp.start()
        for cp in recvs:
            cp.wait_recv()
        for cp in sends:
            cp.wait_send()
        for cp in locs:
            cp.wait()

    hbm = pl.BlockSpec(memory_space=pl.ANY)
    return pl.pallas_call(
        body, out_shape=list(out_shapes), in_specs=[hbm] * n_in, out_specs=[hbm] * n_out,
        scratch_shapes=[pltpu.SemaphoreType.DMA((n_rem,)), pltpu.SemaphoreType.DMA((n_rem,)),
                        pltpu.SemaphoreType.DMA((max(n_loc, 1),))],
        input_output_aliases=aliases or {}, name=name)(*arrays)


ALL_FLIPS = [(dx, dy, dc) for dx in (0, 1) for dy in (0, 1) for dc in (0, 1)][1:]
CHIP_FLIPS = [(1, 0, 0), (0, 1, 0), (1, 1, 0)]
CORE_FLIP = (0, 0, 1)


def _dev_index(p):
    return 4 * p[0] + 2 * p[1] + p[2]


def _chip_index(p):
    return 2 * p[0] + p[1]


def _gather(xs, flips, index, n, name):
    arrays = [x[None] for x in xs]
    outs = [jax.ShapeDtypeStruct((n,) + x.shape, x.dtype) for x in xs]
    remote = [(f, a, lambda me, peer: (0,), a, lambda s: (index(s),)) for a in range(len(xs)) for f in flips]
    local = [(a, lambda me: (0,), a, lambda me: (index(me),)) for a in range(len(xs))]
    return exchange(arrays, outs, remote, local, name)


def allgather_devices(x, name):
    return _gather([x], ALL_FLIPS, _dev_index, N_DEV, name)[0]


def allgather_chips(xs, name):
    return _gather(xs, CHIP_FLIPS, _chip_index, N_CHIP, name)


def gather_halves(xs, name):
    n = len(xs)
    nk = n * len(CHIP_FLIPS)

    def body(*refs):
        ins, outs = refs[:n], refs[n:2 * n]
        ici_send, ici_recv, d2d_send, d2d_recv = refs[2 * n:]
        me = _coords()
        sibling = (me[0], me[1], 1 - me[2])
        first, passed, landed = [], [], []
        for a in range(n):
            half = ins[a].shape[0] // 2
            mine = ins[a].at[pl.ds(pl.multiple_of(me[2] * half, 16), half)]
            for j, flip in enumerate(CHIP_FLIPS):
                k = a * len(CHIP_FLIPS) + j
                peer = (me[0] ^ flip[0], me[1] ^ flip[1], me[2])
                first.append(pltpu.make_async_remote_copy(
                    src_ref=mine, dst_ref=outs[a].at[_chip_index(me), me[2]], send_sem=ici_send.at[k],
                    recv_sem=ici_recv.at[k], device_id=peer, device_id_type=MESH))
                arrived = outs[a].at[_chip_index(peer), me[2]]
                landed.append(pltpu.make_async_remote_copy(
                    src_ref=mine, dst_ref=arrived, send_sem=ici_send.at[k], recv_sem=ici_recv.at[k],
                    device_id=peer, device_id_type=MESH))
                passed.append(pltpu.make_async_remote_copy(
                    src_ref=arrived, dst_ref=arrived, send_sem=d2d_send.at[k], recv_sem=d2d_recv.at[k],
                    device_id=sibling, device_id_type=MESH))
        for cp in first:
            cp.start()
        for k in range(nk):
            landed[k].wait_recv()
            passed[k].start()
        for a in range(n):
            for j, flip in enumerate(CHIP_FLIPS):
                k = a * len(CHIP_FLIPS) + j
                peer_chip = _chip_index((me[0] ^ flip[0], me[1] ^ flip[1]))
                from_sibling = outs[a].at[peer_chip, 1 - me[2]]
                pltpu.make_async_remote_copy(
                    src_ref=from_sibling, dst_ref=from_sibling, send_sem=d2d_send.at[k], recv_sem=d2d_recv.at[k],
                    device_id=sibling, device_id_type=MESH).wait_recv()
        for cp in first + passed:
            cp.wait_send()

    hbm = pl.BlockSpec(memory_space=pl.ANY)
    return pl.pallas_call(
        body, out_shape=[jax.ShapeDtypeStruct((N_CHIP, 2, x.shape[0] // 2, x.shape[1]), x.dtype) for x in xs],
        in_specs=[hbm] * n, out_specs=[hbm] * n,
        scratch_shapes=[pltpu.SemaphoreType.DMA((nk,)) for _ in range(4)], name=name)(*xs)


HBM_SPEC = pl.BlockSpec(memory_space=pltpu.HBM)
SEM_SPEC = pl.BlockSpec(memory_space=pltpu.SEMAPHORE)
DATAFLOW = pltpu.SideEffectType.DATAFLOW_SIDE_EFFECTING


def _at(ref, idx):
    return ref if idx is None else ref.at[idx]


def _peer(me, flip):
    return (me[0] ^ flip[0], me[1] ^ flip[1], me[2] ^ flip[2])


def exchange_start(arrays, land_shapes, remote, name, after=None):
    n_in, n_out, nk = len(arrays), len(land_shapes), len(remote)
    after = list(after or [])
    n_after = len(after)

    def body(*refs):
        srcs, lands = refs[:n_in], refs[n_in:n_in + n_out]
        first_out = n_in + n_out + n_after
        send_sems, recv_sems, token = refs[first_out], refs[first_out + 1], refs[-1]
        me = _coords()
        for k, (flip, ii, src_at, oi, dst_at) in enumerate(remote):
            peer = _peer(me, flip)
            pltpu.make_async_remote_copy(
                src_ref=_at(srcs[ii], src_at(me, peer)), dst_ref=_at(lands[oi], dst_at(me)), send_sem=send_sems.at[k],
                recv_sem=recv_sems.at[k], device_id=peer, device_id_type=MESH).start()
        token[...] = jnp.zeros_like(token)

    lands = [lax.empty(s.shape, s.dtype) for s in land_shapes]
    bufs = list(arrays) + lands
    out = pl.pallas_call(
        body, name=name,
        out_shape=(pltpu.SemaphoreType.DMA((nk,)), pltpu.SemaphoreType.DMA((nk,)),
                   *[pltpu.HBM(b.shape, b.dtype) for b in bufs], jax.ShapeDtypeStruct((8, 128), F32)),
        in_specs=[HBM_SPEC] * len(bufs) + [pl.BlockSpec(memory_space=pl.ANY)] * n_after,
        out_specs=(SEM_SPEC, SEM_SPEC, *[HBM_SPEC] * len(bufs), pl.BlockSpec(memory_space=pltpu.VMEM)),
        input_output_aliases={a: 2 + a for a in range(len(bufs))},
        compiler_params=pltpu.CompilerParams(has_side_effects=DATAFLOW),
    )(*[pltpu.with_memory_space_constraint(b, pltpu.HBM) for b in bufs], *after)
    flight = (out[0], out[1], list(out[2:2 + n_in]), list(out[2 + n_in:2 + n_in + n_out]), remote)
    return flight, out[-1]


def exchange_wait(flight, after, name):
    send_sems, recv_sems, arrays, lands, remote = flight
    n_in, n_out = len(arrays), len(lands)

    def body(*refs):
        srcs, lnds = refs[:n_in], refs[n_in:n_in + n_out]
        s_sems, r_sems = refs[n_in + n_out], refs[n_in + n_out + 1]
        me = _coords()
        for k, (flip, ii, src_at, oi, dst_at) in enumerate(remote):
            peer = _peer(me, flip)
            copy = pltpu.make_async_remote_copy(
                src_ref=_at(srcs[ii], src_at(me, peer)), dst_ref=_at(lnds[oi], dst_at(peer)), send_sem=s_sems.at[k],
                recv_sem=r_sems.at[k], device_id=peer, device_id_type=MESH)
            copy.wait_send()
            copy.wait_recv()

    bufs = list(arrays) + list(lands)
    out = pl.pallas_call(
        body, name=name,
        out_shape=tuple(pltpu.HBM(b.shape, b.dtype) for b in bufs),
        in_specs=[HBM_SPEC] * len(bufs) + [SEM_SPEC, SEM_SPEC, pl.BlockSpec(memory_space=pl.ANY)],
        out_specs=tuple([HBM_SPEC] * len(bufs)),
        input_output_aliases={a: a for a in range(len(bufs))},
        compiler_params=pltpu.CompilerParams(has_side_effects=DATAFLOW),
    )(*bufs, send_sems, recv_sems, after)
    return list(out[:n_in]), list(out[n_in:])


def _half_tile(h, cd):
    return h if h * cd * 4 <= (1 << 20) else math.gcd(512, h)


def pair_add(g, got, core, out_dtype, name):
    _, _, h, cd = g.shape
    th = _half_tile(h, cd)

    def body(c_ref, g_ref, got_ref, o_ref):
        o_ref[0] = (g_ref[0, 0] + got_ref[0]).astype(o_ref.dtype)

    return pl.pallas_call(
        body, out_shape=jax.ShapeDtypeStruct((N_CHIP, h, cd), out_dtype),
        grid_spec=pltpu.PrefetchScalarGridSpec(
            num_scalar_prefetch=1, grid=(N_CHIP, h // th),
            in_specs=[pl.BlockSpec((1, 1, th, cd), lambda q, i, c: (q, c[0], i, 0)),
                      pl.BlockSpec((1, th, cd), lambda q, i, c: (q, i, 0))],
            out_specs=pl.BlockSpec((1, th, cd), lambda q, i, c: (q, i, 0))),
        compiler_params=_params(("parallel", "parallel"), VMEM_LIMIT), name=name)(core, g, got)


def sum_chips(parts, sums, place, name):
    _, h, cd = parts.shape
    th = _half_tile(h, cd)

    def body(pc_ref, p_ref, own_ref, o_ref):
        acc = None
        for q in range(N_CHIP):
            term = jnp.where(pc_ref[1] == q, own_ref[0], p_ref[q]).astype(F32)
            acc = term if acc is None else acc + term
        o_ref[0] = acc

    return pl.pallas_call(
        body, out_shape=jax.ShapeDtypeStruct((2, h, cd), F32),
        grid_spec=pltpu.PrefetchScalarGridSpec(
            num_scalar_prefetch=1, grid=(h // th,),
            in_specs=[pl.BlockSpec((N_CHIP, th, cd), lambda i, pc: (0, i, 0)),
                      pl.BlockSpec((1, th, cd), lambda i, pc: (pc[1], i, 0))],
            out_specs=pl.BlockSpec((1, th, cd), lambda i, pc: (pc[0], i, 0))),
        compiler_params=_params(("parallel",), VMEM_LIMIT), name=name)(place, parts, sums)


def to_segments(a, n_ctx):
    def one(p):
        n = p.shape[0]
        return p.reshape(N_SEG, n // N_SEG, -1).transpose(1, 0, 2).reshape(n, -1)
    return jnp.concatenate([one(a[:n_ctx]), one(a[n_ctx:])], axis=0) if n_ctx else one(a)


def from_segments(a, n_ctx):
    def one(p):
        n = p.shape[0]
        return p.reshape(n // N_SEG, N_SEG, -1).transpose(1, 0, 2).reshape(n, -1)
    return jnp.concatenate([one(a[:n_ctx]), one(a[n_ctx:])], axis=0) if n_ctx else one(a)


def rope_tables(n_ctx, n_lat):
    f32 = np.float32
    rows = n_lat // GRID_W
    row = np.repeat(np.arange(rows), GRID_W).astype(f32)
    col = np.tile(np.arange(GRID_W), rows).astype(f32)
    d = QK_ROPE_DIM // 2
    inv = (f32(1.0) / np.power(f32(ROPE_THETA), np.arange(0, d, 2, dtype=f32) / f32(d))).astype(f32)
    ang = np.concatenate([row[:, None] * inv[None, :], col[:, None] * inv[None, :]], axis=1).astype(f32)
    cos = np.concatenate([np.ones((n_ctx, d), f32), np.cos(ang)], axis=0)
    sin = np.concatenate([np.zeros((n_ctx, d), f32), np.sin(ang)], axis=0)
    q = QK_ROPE_DIM // 4
    T = n_ctx + n_lat
    ones, zeros = np.ones((T, QK_NOPE_DIM), f32), np.zeros((T, QK_NOPE_DIM), f32)
    tail, z8 = np.zeros((T, HEAD_LANES - QK_DIM), f32), np.zeros((T, q), f32)
    cr, cc, sr, sc = cos[:, :q], cos[:, q:], sin[:, :q], sin[:, q:]
    cos_t = np.concatenate([ones, cr, cr, cc, cc, tail], axis=1)
    sin_next = np.concatenate([zeros, -sr, z8, -sc, z8, tail], axis=1)
    sin_prev = np.concatenate([zeros, z8, sr, z8, sc, tail], axis=1)
    return tuple(jnp.asarray(t, F32) for t in (cos_t, sin_next, sin_prev))


def pad_heads(w, used):
    k = w.shape[0]
    return jnp.pad(w.reshape(k, MLA_HEADS, used), ((0, 0), (0, 0), (0, HEAD_LANES - used))).reshape(k, -1)


def unpad_heads(w, used):
    k = w.shape[0]
    return w.reshape(k, MLA_HEADS, HEAD_LANES)[:, :, :used].reshape(k, MLA_HEADS * used)


def rotary_spread():
    lane = np.arange(MLA_HEADS * HEAD_LANES) % HEAD_LANES
    return jnp.asarray(lane[None, :] == (QK_NOPE_DIM + np.arange(QK_ROPE_DIM))[:, None], BF16)


def s5_discretise(a_re, a_im, log_step, b_re, b_im):
    dt = jnp.exp(log_step)[:, None]
    mag = jnp.exp(a_re * dt)
    lb_re = mag * jnp.cos(a_im * dt)
    lb_im = mag * jnp.sin(a_im * dt)
    den = a_re * a_re + a_im * a_im
    nr = lb_re - 1.0
    f_re = ((nr * a_re + lb_im * a_im) / den)[..., None]
    f_im = ((lb_im * a_re - nr * a_im) / den)[..., None]
    return lb_re, lb_im, f_re * b_re - f_im * b_im, f_re * b_im + f_im * b_re


def s5_block_weights(lb_re, lb_im, bb_re, bb_im, c_re, c_im):
    eye = jnp.eye(GROUPS_PER_BLOCK, dtype=F32)
    lam = jnp.stack([lb_re.reshape(1, S5_LANES), lb_im.reshape(1, S5_LANES)])

    def b_blocks(bb):
        t = bb.reshape(N_BLOCKS, GROUPS_PER_BLOCK, S5_STATE, S5_GROUP)
        return jnp.einsum("bgpc,gh->bgchp", t, eye).reshape(N_BLOCKS, BLK_CH, BLK_ST).astype(BF16)

    def c_blocks(cc):
        t = cc.reshape(N_BLOCKS, GROUPS_PER_BLOCK, S5_GROUP, S5_STATE)
        return jnp.einsum("bgcp,gh->bgphc", t, eye).reshape(N_BLOCKS, BLK_ST, BLK_CH).astype(BF16)

    return lam, b_blocks(bb_re), b_blocks(bb_im), c_blocks(c_re), c_blocks(c_im)


def b_block_diag(db):
    t = db.reshape(N_BLOCKS, GROUPS_PER_BLOCK, S5_GROUP, GROUPS_PER_BLOCK, S5_STATE)
    return jnp.einsum("bgchp,gh->bgpc", t, jnp.eye(GROUPS_PER_BLOCK, dtype=F32)).reshape(S5_GROUPS, S5_STATE, S5_GROUP)


def c_block_diag(dc):
    t = dc.reshape(N_BLOCKS, GROUPS_PER_BLOCK, S5_STATE, GROUPS_PER_BLOCK, S5_GROUP)
    return jnp.einsum("bgphc,gh->bgcp", t, jnp.eye(GROUPS_PER_BLOCK, dtype=F32)).reshape(S5_GROUPS, S5_GROUP, S5_STATE)


def conj(a):
    return jnp.stack([a[0], -a[1]])


PACK_TILE = 16 * 128


def pack_flat(parts, dtype):
    flat = [p.reshape(-1).astype(dtype) for p in parts]
    sizes = [f.shape[0] for f in flat]
    total = sum(sizes)
    pad = (-total) % PACK_TILE
    if pad:
        flat.append(jnp.zeros((pad,), dtype))
    offs = np.cumsum([0] + sizes)[:-1].tolist()
    return jnp.concatenate(flat).reshape(-1, 128), offs


def unpack_flat(buf, offs, shapes):
    flat = buf.reshape(-1)
    return [flat[o:o + int(np.prod(s))].reshape(s) for o, s in zip(offs, shapes)]


def s5_forward(p1, n_ctx, dirs):
    saved = []
    y = None
    ctx_rows, lat_rows = (0, n_ctx), (n_ctx, p1.shape[0] - n_ctx)
    zeros_tile = jnp.zeros((2, N_SEG, S5_LANES), F32)
    zeros_row = jnp.zeros((2, 1, S5_LANES), F32)
    for k, (lam, b_re, b_im, c_re, c_im) in enumerate(dirs):
        rev = k == 1
        last = 0 if rev else N_SEG - 1
        _, _, fin = s5_scan(p1, b_re, b_im, lam, zeros_tile, reverse=rev, rows=ctx_rows, name=f"s5_ctx_finals{k}")
        carry_c = s5_chain(fin, zeros_row, lam, n_ctx // N_SEG, rev, name=f"s5_ctx_chain{k}")
        _, ck_c, fin_c = s5_scan(p1, b_re, b_im, lam, carry_c, reverse=rev, want_ckpt=True, rows=ctx_rows,
                                 name=f"s5_ctx_scan{k}")
        s0 = fin_c[:, last:last + 1, :]
        _, _, fin = s5_scan(p1, b_re, b_im, lam, zeros_tile, reverse=rev, rows=lat_rows, name=f"s5_lat_finals{k}")
        carry_l = s5_chain(fin, s0, lam, lat_rows[1] // N_SEG, rev, name=f"s5_lat_chain{k}")
        y, ck_l, _ = s5_scan(p1, b_re, b_im, lam, carry_l, reverse=rev, c_re=c_re, c_im=c_im, add=y,
                             want_ckpt=True, rows=lat_rows, name=f"s5_lat_scan{k}")
        saved.append((ck_c, ck_l))
    return y, saved


def s5_backward(dy_l, du_extra_l, p1, n_ctx, dirs, saved):
    n_lat = p1.shape[0] - n_ctx
    zeros_tile = jnp.zeros((2, N_SEG, S5_LANES), F32)
    zeros_row = jnp.zeros((2, 1, S5_LANES), F32)
    dy_c = jnp.zeros((n_ctx, D_MODEL), F32)
    du_l, du_c = du_extra_l, None
    grads = []
    for k, (lam, b_re, b_im, c_re, c_im) in enumerate(dirs):
        rev = k == 1
        lam_c = conj(lam)
        ck_c, ck_l = saved[k]
        first = N_SEG - 1 if rev else 0
        _, _, fin = s5_scan(dy_l, c_re, c_im, lam_c, zeros_tile, reverse=not rev, adjoint=True,
                            name=f"s5_lat_adj_finals{k}")
        carry = s5_chain(fin, zeros_row, lam_c, n_lat // N_SEG, not rev, name=f"s5_lat_adj_chain{k}")
        du_l, dlam_l, dbr_l, dbi_l, dcr_l, dci_l, fin_a = s5_grads(
            dy_l, p1, ck_l, b_re, b_im, c_re, c_im, lam, carry, reverse=rev, add=du_l, u_off=n_ctx,
            name=f"s5_lat_grads{k}")
        g0 = fin_a[:, first:first + 1, :]
        carry = s5_chain(zeros_tile, g0, lam_c, n_ctx // N_SEG, not rev, name=f"s5_ctx_adj_chain{k}")
        du_c, dlam_c, dbr_c, dbi_c, _, _, _ = s5_grads(
            dy_c, p1, ck_c, b_re, b_im, c_re, c_im, lam, carry, reverse=rev, add=du_c, name=f"s5_ctx_grads{k}")
        dlam = jnp.sum(dlam_l + dlam_c, axis=1)
        grads.append((dlam, b_block_diag(dbr_l + dbr_c), b_block_diag(dbi_l + dbi_c),
                      c_block_diag(dcr_l), c_block_diag(dci_l)))
    return jnp.concatenate([du_c, du_l], axis=0), grads


def local_step(x, ctx, target, mod, w, late=None, reducer=None):
    L, Lc = x.shape[0], ctx.shape[0]
    T = L + Lc
    assert L % Lc == 0 and Lc % (2 * N_SEG) == 0 and L % GRID_W == 0
    D = D_MODEL
    X0 = jnp.concatenate([ctx, x], axis=0)

    def mod_of(i, j):
        return mod[i, :, j, :][:, None, :]

    def vec(v):
        return v.reshape(1, 1, -1).astype(F32)

    g0 = vec(w["norm_g"][0])
    H0, p0 = norm_proj(X0, g0, mod_of(0, 1), mod_of(0, 0), w["mla_w_in"], Lc, "l0_norm_in")
    cq = Rows(p0, Q_LORA_RANK, col_blk=D // Q_LORA_RANK)
    ckv = Rows(p0, KV_LORA_RANK, col_blk=(D + Q_LORA_RANK) // KV_LORA_RANK)
    kr = p0[:, D + Q_LORA_RANK + KV_LORA_RANK:D + P0_HEAD]
    qng, kvng = vec(w["mla_q_norm"]), vec(w["mla_kv_norm"])
    tabs = rope_tables(Lc, L)
    spread = rotary_spread()
    if late is not None:
        w = {**w, **late["qkv"](p0)}
    w_uq_p = pad_heads(w["mla_w_uq"], QK_DIM)
    w_ukv3 = w["mla_w_ukv"].reshape(KV_LORA_RANK, MLA_HEADS, QK_NOPE_DIM + V_HEAD_DIM)
    w_kn_p = pad_heads(w_ukv3[:, :, :QK_NOPE_DIM].reshape(KV_LORA_RANK, -1), QK_NOPE_DIM)
    w_v = w_ukv3[:, :, QK_NOPE_DIM:].reshape(KV_LORA_RANK, -1)
    qb, qn = q_heads(cq, qng, w_uq_p, tabs)
    kb, vb, kvn = kv_heads(ckv, kvng, w_kn_p, w_v, kr, spread, tabs)
    o, lse, qs = attn_fwd(qb, kb, vb, Lc)
    if late is not None:
        w = {**w, **late["out"](o)}
    X1, og, out0 = mla_post_fwd(o, p0, X0, mod_of(0, 2), w["mla_w_out"], Lc)

    if late is not None:
        w = {**w, **late["l1"](X1)}
    X1p = to_segments(X1, Lc)
    tgt_p = to_segments(target, 0)
    g1 = vec(w["norm_g"][1])
    H1, p1 = norm_proj(X1p, g1, mod_of(1, 1), mod_of(1, 0), w["s5_w_in"], Lc, "l1_norm_in")
    disc_fn = lambda *a: tuple(zip(*[s5_discretise(a[0][k], a[1][k], a[2][k], a[3][k], a[4][k]) for k in range(2)]))
    disc, disc_vjp = jax.vjp(disc_fn, w["s5_a_re"], w["s5_a_im"], w["s5_log_step"], w["s5_b_re"], w["s5_b_im"])
    dirs = [s5_block_weights(disc[0][k], disc[1][k], disc[2][k], disc[3][k], w["s5_c_re"][k], w["s5_c_im"][k])
            for k in range(2)]
    y_ssm, s5_saved = s5_forward(p1, Lc, dirs)

    row = lambda v: v.reshape(1, D).astype(F32)
    (lvec, dX2, d_yssm, d_u_act, d_z1, d_fg, d_gt1, d_bg, d_d, gw_glu, gw_out) = s5_tail(
        y_ssm, p1, X1p, tgt_p, Lc, row(w["s5_d"]), row(w["s5_b_glu"]), mod[1, 1:2, 2, :], row(w["final_g"]),
        w["s5_w_glu"], w["s5_w_out"])
    loss = jnp.sum(lvec)
    gw = {"final_g": d_fg.reshape(D), "s5_b_glu": d_bg.reshape(D), "s5_d": d_d.reshape(D),
          "s5_w_glu": gw_glu, "s5_w_out": gw_out}
    dmod = {}

    du_p, s5_g = s5_backward(d_yssm, d_u_act, p1, Lc, dirs, s5_saved)
    d_disc = tuple(tuple(s5_g[k][j - 1].reshape(disc[j][k].shape) if j >= 2 else
                         s5_g[k][0][j].reshape(disc[j][k].shape) for k in range(2)) for j in range(4))
    gw["s5_a_re"], gw["s5_a_im"], gw["s5_log_step"], gw["s5_b_re"], gw["s5_b_im"] = disc_vjp(d_disc)
    gw["s5_c_re"] = jnp.stack([s5_g[0][3], s5_g[1][3]])
    gw["s5_c_im"] = jnp.stack([s5_g[0][4], s5_g[1][4]])
    gw["s5_w_in"] = jnp.concatenate([mm_tn(H1, du_p, name="l1_in_dw_u"), mm_tn(H1[Lc:], d_z1, name="l1_in_dw_z")],
                                    axis=1)
    if reducer is not None:
        g1 = g1 + reducer["l1"][0]({n: gw.pop(n) for n in LAYER1_MATS})[0, 0]
    d_X1p, d_g1, d_sc1, d_sh1 = norm_proj_bwd([(du_p, 0, 0), (d_z1, D, Lc)], w["s5_w_in"], X1p, g1, mod_of(1, 1),
                                              mod_of(1, 0), dX2, Lc, "l1_norm_in_bwd")
    d_gt1_full = jnp.concatenate([jnp.zeros((1, 1, D), F32), d_gt1[None]], axis=0)
    dmod[1] = (d_sh1, d_sc1, d_gt1_full)
    d_X1 = from_segments(d_X1p, Lc)

    d_o, d_z0, d_gt0, gw["mla_w_out"] = mla_post_bwd(d_X1, out0, og, o, p0, mod_of(0, 2), w["mla_w_out"], Lc)
    if reducer is not None:
        started = reducer["l1"][1](d_o)[0, 0] + reducer["out"][0]({"mla_w_out": gw.pop("mla_w_out")})[0, 0]
        tabs = (tabs[0] + started,) + tabs[1:]
    d_q, dos = attn_bwd_dq(qb, kb, vb, o, d_o, lse, tabs, Lc)
    dk_p, d_v = attn_bwd_dkv(qs, kb, vb, dos, Lc)
    if reducer is not None:
        tabs = (tabs[0] + reducer["out"][1](d_q)[0, 0],) + tabs[1:]
    d_k, d_kr = heads_unrope(dk_p, tabs, jnp.pad(spread, ((0, HEAD_LANES - QK_ROPE_DIM), (0, 0))),
                             name="l0_k_unrope")
    d_qn = mm_nt(d_q, w_uq_p, name="l0_uq_dx")
    gw["mla_w_uq"] = unpad_heads(mm_tn(qn, d_q, name="l0_uq_dw"), QK_DIM)
    d_kvn = mm_nt(d_k, w_kn_p, name="l0_ukn_dx") + mm_nt(d_v, w_v, name="l0_uv_dx")
    dw_kn = unpad_heads(mm_tn(kvn, d_k, name="l0_ukn_dw"), QK_NOPE_DIM).reshape(KV_LORA_RANK, MLA_HEADS, QK_NOPE_DIM)
    dw_v = mm_tn(kvn, d_v, name="l0_uv_dw").reshape(KV_LORA_RANK, MLA_HEADS, V_HEAD_DIM)
    gw["mla_w_ukv"] = jnp.concatenate([dw_kn, dw_v], axis=-1).reshape(KV_LORA_RANK, -1)
    d_cq, d_qng = rowwise_bwd(f_rms, [cq], [qng], [d_qn], [0], [0], T, 0, "l0_qnorm_bwd")
    d_ckv, d_kvng = rowwise_bwd(f_rms, [ckv], [kvng], [d_kvn], [0], [0], T, 0, "l0_kvnorm_bwd")
    gw["mla_q_norm"] = d_qng.reshape(-1)
    gw["mla_kv_norm"] = d_kvng.reshape(-1)
    o_cq, o_ckv = D, D + Q_LORA_RANK
    o_kr = o_ckv + KV_LORA_RANK
    d_head = jnp.concatenate([d_cq, d_ckv, d_kr], axis=1)
    gw["mla_w_in"] = jnp.concatenate([mm_tn(H0, d_head, name="l0_in_dw_head")[:, :P0_HEAD],
                                      mm_tn(H0, d_z0, name="l0_in_dw_z")], axis=1)
    d_X0, d_g0, d_sc0, d_sh0 = norm_proj_bwd(
        [(d_z0, 0, 0), (d_cq, o_cq, 0), (d_ckv, o_ckv, 0), (d_kr, o_kr, 0)], w["mla_w_in"], X0, g0, mod_of(0, 1),
        mod_of(0, 0), d_X1, Lc, "l0_norm_in_bwd")
    dmod[0] = (d_sh0, d_sc0, d_gt0)
    gw["norm_g"] = jnp.stack([d_g0.reshape(D), d_g1.reshape(D)])
    dx = d_X0[Lc:]
    dmod_arr = jnp.stack([jnp.stack([dmod[i][j][:, 0, :] for j in range(3)], axis=1) for i in range(2)])
    ready = {**reducer["l1"][2](d_X0), **reducer["out"][2](d_X0)} if reducer is not None else {}
    return loss, dx, dmod_arr, gw, ready


SHARDED = {
    "mla_w_in": 1, "mla_w_uq": 1, "mla_w_ukv": 1, "mla_w_out": 0,
    "s5_w_in": 1, "s5_w_glu": 0, "s5_w_out": 0, "s5_d": 0, "s5_b_glu": 0,
}
SHARDED_MATS = ["mla_w_in", "mla_w_uq", "mla_w_ukv", "mla_w_out", "s5_w_in", "s5_w_glu", "s5_w_out"]
SHARDED_VECS = ["s5_d", "s5_b_glu"]
REPLICATED = ["norm_g", "mla_q_norm", "mla_kv_norm", "s5_a_re", "s5_a_im", "s5_log_step", "s5_b_re", "s5_b_im",
              "s5_c_re", "s5_c_im", "final_g"]
WEIGHT_ORDER = ["c_ctx", "ada_w", "ada_b", "norm_g", "mla_w_in", "mla_q_norm", "mla_w_uq", "mla_kv_norm", "mla_w_ukv",
                "mla_w_out", "s5_w_in", "s5_a_re", "s5_a_im", "s5_log_step", "s5_b_re", "s5_b_im", "s5_c_re", "s5_c_im",
                "s5_d", "s5_w_glu", "s5_b_glu", "s5_w_out", "final_g"]


P0_HEAD = Q_LORA_RANK + KV_LORA_RANK + QK_ROPE_DIM


P0_WIDTH = 1536


def w_in_to_kernel_order(w):
    pad = jnp.zeros((w.shape[0], P0_WIDTH - w.shape[1]), w.dtype)
    return jnp.concatenate([w[:, P0_HEAD:], w[:, :P0_HEAD], pad], axis=1)


LAYER0_MATS = ["mla_w_in", "mla_w_uq", "mla_w_ukv", "mla_w_out"]
LAYER1_MATS = ["s5_w_in", "s5_w_glu", "s5_w_out"]


def _whole_matrices(names, own_blocks, gathered):
    chip = _chip_index(_coords())
    full = {}
    for n, own, o in zip(names, own_blocks, gathered):
        slot = lax.broadcasted_iota(jnp.int32, (N_CHIP, 1, 1), 0)
        o = jnp.where(slot == chip, own[None], o.reshape((N_CHIP,) + own.shape))
        full[n] = o.reshape(-1, o.shape[-1]) if SHARDED[n] == 0 else o.transpose(1, 0, 2).reshape(o.shape[1], -1)
    return full


FIRST_MATS = ["mla_w_in"]
LATER_GROUPS = {"qkv": ["mla_w_uq", "mla_w_ukv"], "out": ["mla_w_out"], "l1": LAYER1_MATS}


def gather_weights(ws):
    mats = [ws[n].astype(BF16) for n in FIRST_MATS]
    full = _whole_matrices(FIRST_MATS, mats, gather_halves(mats, "gather_weights"))
    full["mla_w_in"] = w_in_to_kernel_order(full["mla_w_in"])
    return full


def gather_weights_behind(ws, after):
    token, finish = 0.0, {}
    for group, names in LATER_GROUPS.items():
        mats = [ws[n].astype(BF16) for n in names]
        flight, tok = exchange_start(
            mats, [jax.ShapeDtypeStruct((N_CHIP,) + m.shape, m.dtype) for m in mats],
            [(f, a, lambda me, peer: None, a, lambda s: (_chip_index(s),))
             for a in range(len(mats)) for f in CHIP_FLIPS], f"gather_{group}_start", after=after)
        after = [tok]
        token = token + tok[0, 0]

        def finish_group(after_work, group=group, names=names, flight=flight):
            own, got = exchange_wait(flight, after_work, f"gather_{group}_wait")
            return _whole_matrices(names, own, got)

        finish[group] = finish_group
    return token, finish


def _grad_slots(gw, names):
    slots = []
    for n in names:
        g = gw[n]
        if SHARDED[n] == 0:
            slots.append(g.reshape(N_CHIP, 2, g.shape[0] // (2 * N_CHIP), g.shape[1]))
        else:
            k, n4 = g.shape
            slots.append(g.reshape(k, N_CHIP, n4 // N_CHIP).transpose(1, 0, 2)
                         .reshape(N_CHIP, 2, k // 2, n4 // N_CHIP))
    return slots


def _to_sibling_half(count):
    return [(CORE_FLIP, i, lambda me, peer: (slice(None), 1 - me[2]), i, lambda s: None) for i in range(count)]


def _to_chips(count):
    return [(f, i, lambda me, peer: (_chip_index(peer),), i, lambda s: (_chip_index(s),))
            for i in range(count) for f in CHIP_FLIPS]


def _place():
    me = _coords()
    return jnp.stack([me[2], _chip_index(me)]).astype(jnp.int32)


def reduce_behind(names, tag):
    state = {}
    count = len(names)

    def begin(gw):
        slots = _grad_slots(gw, names)
        lands = [jax.ShapeDtypeStruct((N_CHIP,) + s.shape[2:], F32) for s in slots]
        state["in"], token = exchange_start(slots, lands, _to_sibling_half(count), f"grads_{tag}_swap_in_start")
        return token

    def middle(after):
        slots, got = exchange_wait(state["in"], after, f"grads_{tag}_swap_in_wait")
        place = _place()
        sums = [pair_add(s, g, place[:1], BF16, f"grads_pair_{n}") for n, s, g in zip(names, slots, got)]
        lands = [jax.ShapeDtypeStruct(s.shape, s.dtype) for s in sums]
        state["out"], token = exchange_start(sums, lands, _to_chips(count), f"grads_{tag}_scatter_start")
        return token

    def end(after):
        sums, parts = exchange_wait(state["out"], after, f"grads_{tag}_scatter_wait")
        place = _place()
        return {n: sum_chips(p, s, place, f"grads_sum_{n}") for n, p, s in zip(names, parts, sums)}

    return begin, middle, end


def reduce_gradients(gw, ready_halves):
    me = _coords()
    place = _place()
    mat_names = [n for n in SHARDED_MATS if n not in ready_halves]
    slots = dict(zip(mat_names, _grad_slots(gw, mat_names)))
    small_names = REPLICATED + SHARDED_VECS
    small, small_offs = pack_flat([gw[n].astype(F32) for n in small_names], F32)
    small = jnp.pad(small, ((0, (-small.shape[0]) % (N_CHIP * 32)), (0, 0)))
    slots["small"] = small.reshape(N_CHIP, 2, -1, 128)
    names = list(slots)
    count = len(names)
    got = exchange([slots[n] for n in names],
                   [jax.ShapeDtypeStruct((N_CHIP,) + slots[n].shape[2:], F32) for n in names],
                   _to_sibling_half(count), [], "grads_swap_in")
    sums = [pair_add(slots[n], g, place[:1], F32 if n == "small" else BF16, f"grads_pair_{n}")
            for n, g in zip(names, got)]
    parts = exchange(sums, [jax.ShapeDtypeStruct(s.shape, s.dtype) for s in sums], _to_chips(count), [],
                     "grads_scatter")
    halves = {n: sum_chips(p, s, place, f"grads_sum_{n}") for n, p, s in zip(names, parts, sums)}
    halves.update(ready_halves)
    all_names = list(halves)
    fulls = exchange(
        [halves[n] for n in all_names], [jax.ShapeDtypeStruct(halves[n].shape, F32) for n in all_names],
        [(CORE_FLIP, i, lambda me, peer: (me[2],), i, lambda s: (s[2],)) for i in range(len(all_names))], [],
        "grads_swap_out", aliases={i: i for i in range(len(all_names))})
    out = {n: f.reshape(-1, f.shape[-1]) for n, f in zip(all_names, fulls)}
    quarter = out.pop("small")
    gather, token = exchange_start(
        [quarter], [jax.ShapeDtypeStruct((N_CHIP,) + quarter.shape, F32)],
        [(f, 0, lambda me, peer: None, 0, lambda s: (_chip_index(s),)) for f in CHIP_FLIPS],
        "grads_gather_small_start")

    def finish_small(after):
        (own,), (got_small,) = exchange_wait(gather, after, "grads_gather_small_wait")
        slot = lax.broadcasted_iota(jnp.int32, (N_CHIP, 1, 1), 0)
        small_all = jnp.where(slot == _chip_index(me), own[None], got_small)
        vals = unpack_flat(small_all, small_offs, [gw[n].shape for n in small_names])
        res = {}
        for n, v in zip(small_names, vals):
            if n in SHARDED_VECS:
                size = v.shape[0] // N_CHIP
                v = lax.dynamic_slice_in_dim(v, _chip_index(me) * size, size)
            res[n] = v
        return res

    return out, finish_small, token


def kernel(x, c, ctx, c_ctx, ada_w, ada_b, norm_g, mla_w_in, mla_q_norm, mla_w_uq, mla_kv_norm, mla_w_ukv, mla_w_out, s5_w_in, s5_a_re, s5_a_im, s5_log_step, s5_b_re, s5_b_im, s5_c_re, s5_c_im, s5_d, s5_w_glu, s5_b_glu, s5_w_out, final_g, loss_target, m_c_ctx, m_ada_w, m_ada_b, m_norm_g, m_mla_w_in, m_mla_q_norm, m_mla_w_uq, m_mla_kv_norm, m_mla_w_ukv, m_mla_w_out, m_s5_w_in, m_s5_a_re, m_s5_a_im, m_s5_log_step, m_s5_b_re, m_s5_b_im, m_s5_c_re, m_s5_c_im, m_s5_d, m_s5_w_glu, m_s5_b_glu, m_s5_w_out, m_final_g, v_c_ctx, v_ada_w, v_ada_b, v_norm_g, v_mla_w_in, v_mla_q_norm, v_mla_w_uq, v_mla_kv_norm, v_mla_w_ukv, v_mla_w_out, v_s5_w_in, v_s5_a_re, v_s5_a_im, v_s5_log_step, v_s5_b_re, v_s5_b_im, v_s5_c_re, v_s5_c_im, v_s5_d, v_s5_w_glu, v_s5_b_glu, v_s5_w_out, v_final_g):
    args = dict(locals())
    weights = {n: args[n] for n in WEIGHT_ORDER}
    D = D_MODEL
    xi, yi, ci = _coords()
    chip = 2 * xi + yi
    me = 4 * xi + 2 * yi + ci
    n_col = ada_w.shape[2]

    c_all = allgather_devices(jnp.pad(c, ((0, 7), (0, 0))), "gather_c")[:, 0, :]
    cond = jnp.concatenate([c_all, jnp.broadcast_to(c_ctx[None], (8, D))], axis=0)
    (s_cond,) = rowwise_fwd(lambda v: (_silu(v),), [cond], [], [D], [F32], 16, 0, "cond_silu")
    ada_rows = ada_w.reshape(2 * D, n_col)
    mod_cols = jnp.stack([mm_nn(s_cond, ada_rows, name=f"mod_proj{i}", b_blk=i) for i in range(2)])
    vec_tiles = [jnp.pad(weights[n][0].reshape(-1, 128), ((0, 6), (0, 0))) for n in SHARDED_VECS]
    mod_all, *vec_all = allgather_chips([mod_cols] + vec_tiles, "gather_mod")
    mod_all = mod_all.transpose(1, 2, 0, 3).reshape(2, 16, 3 * D) + ada_b[:, None, :]
    mod_l = lax.dynamic_index_in_dim(mod_all, me, axis=1, keepdims=False)
    mod_c = mod_all[:, 8, :]
    mod = jnp.stack([mod_c.reshape(2, 3, D), mod_l.reshape(2, 3, D)], axis=1)

    w = gather_weights({n: weights[n][0] for n in FIRST_MATS})
    token, late = gather_weights_behind({n: weights[n][0] for n in SHARDED_MATS}, [w["mla_w_in"], mod])
    for n, v in zip(SHARDED_VECS, vec_all):
        w[n] = v[:, :2, :].reshape(-1)
    for n in ["norm_g", "final_g"]:
        w[n] = weights[n]
    for n in ["mla_q_norm", "mla_kv_norm", "s5_a_re", "s5_a_im", "s5_log_step", "s5_b_re", "s5_b_im",
              "s5_c_re", "s5_c_im"]:
        w[n] = weights[n][0]

    reducer = {"l1": reduce_behind(LAYER1_MATS, "l1"), "out": reduce_behind(["mla_w_out"], "out")}
    loss_me, dx, dmod, gw, ready = local_step(x[0] + token, ctx[0], loss_target[0], mod, w, late,
                                              reducer)

    dmod_rows, loss_all = _gather([dmod.reshape(2, 2, 3 * D), jnp.broadcast_to(loss_me, (8, 128))],
                                  ALL_FLIPS, _dev_index, N_DEV, "gather_dmod")
    loss = functools.reduce(lambda s, d: s + loss_all[d, 0, 0], range(1, N_DEV), loss_all[0, 0, 0])
    dm = jnp.concatenate([dmod_rows[:, :, 1, :], dmod_rows[:, :, 0, :]], axis=0).transpose(1, 0, 2)
    g_ada_b = jnp.sum(dm, axis=1)
    dm_cols = lax.dynamic_slice_in_dim(dm, chip * n_col, n_col, axis=2)
    g_ada_w = jnp.stack([mm_tn(s_cond, dm_cols[i], name=f"mod_proj_dw{i}") for i in range(2)])
    dmc = jnp.sum(dm_cols[:, 8:, :], axis=1)
    dmc8 = jnp.broadcast_to(dmc[:, None, :], (2, 8, n_col))
    g_sc = (mm_nt(dmc8[0], ada_rows, name="mod_proj_dx0", b_rows=D, b_blk=0)[0]
            + mm_nt(dmc8[1], ada_rows, name="mod_proj_dx1", b_rows=D, b_blk=1)[0])
    g_sc_all = allgather_devices(jnp.broadcast_to(g_sc[None], (8, D)), "gather_dcond")[:, 0, :]
    g_silu_cc = g_sc_all[0] + g_sc_all[2] + g_sc_all[4] + g_sc_all[6]
    (g_c_ctx,) = rowwise_bwd(lambda v: (_silu(v),), [jnp.broadcast_to(c_ctx[None], (8, D))], [],
                             [jnp.broadcast_to(g_silu_cc[None], (8, D))], [0], [], 8, 0, "cond_silu_bwd")
    g_c_ctx = g_c_ctx[0]

    grads = {"c_ctx": g_c_ctx, "ada_w": g_ada_w, "ada_b": g_ada_b}
    deltas, new_m, new_v = {}, {}, {}
    small = [n for n in WEIGHT_ORDER if weights[n].size < 50000]

    def update(n, after=None):
        shp = weights[n].shape
        w2 = weights[n].reshape(-1, shp[-1])
        d_, m_, v_ = adamw(w2, grads[n].reshape(w2.shape), args["m_" + n].reshape(w2.shape),
                           args["v_" + n].reshape(w2.shape), name=f"adamw_{n}", after=after)
        deltas[n], new_m[n], new_v[n] = d_.reshape(shp), m_.reshape(shp), v_.reshape(shp)

    red, finish_small, small_started = reduce_gradients(gw, ready)
    update("ada_w", small_started)
    for n in SHARDED_MATS:
        grads[n] = red[n].reshape(weights[n].shape)
        update(n, small_started)
    red_small = finish_small(jnp.stack([deltas[n][(0,) * deltas[n].ndim]
                                        for n in ["ada_w"] + SHARDED_MATS]))
    for n in REPLICATED + SHARDED_VECS:
        grads[n] = red_small[n].reshape(weights[n].shape)
    for n in WEIGHT_ORDER:
        if n not in small and n not in deltas:
            update(n)
    packs = []
    offs = None
    for src in (weights, grads, {n: args["m_" + n] for n in small}, {n: args["v_" + n] for n in small}):
        buf, offs = pack_flat([src[n] for n in small], F32)
        packs.append(buf)
    outs = adamw(*packs, name="adamw_small")
    for res, dst in zip(outs, (deltas, new_m, new_v)):
        for n, val in zip(small, unpack_flat(res, offs, [weights[n].shape for n in small])):
            dst[n] = val

    return (loss, dx[None], *[grads[n] for n in WEIGHT_ORDER], *[deltas[n] for n in WEIGHT_ORDER],
            *[new_m[n] for n in WEIGHT_ORDER], *[new_v[n] for n in WEIGHT_ORDER])
```

```python
import functools
import math

import jax
import jax.numpy as jnp
import numpy as np
from jax import lax
from jax.experimental import pallas as pl
from jax.experimental.pallas import tpu as pltpu

F32 = jnp.float32
BF16 = jnp.bfloat16

D_MODEL = 1024
GRID_W = 64
EPS = 1e-6
MLA_HEADS = 16
QK_NOPE_DIM = 64
QK_ROPE_DIM = 32
V_HEAD_DIM = 64
Q_LORA_RANK = 256
KV_LORA_RANK = 128
QK_DIM = QK_NOPE_DIM + QK_ROPE_DIM
SOFTMAX_SCALE = QK_DIM ** -0.5
ROPE_THETA = 10000.0
S5_GROUP = 16
S5_GROUPS = D_MODEL // S5_GROUP
S5_STATE = 64
S5_LANES = S5_GROUPS * S5_STATE
N_SEG = 8
GROUPS_PER_BLOCK = 8
N_BLOCKS = S5_GROUPS // GROUPS_PER_BLOCK
BLK_CH = GROUPS_PER_BLOCK * S5_GROUP
BLK_ST = GROUPS_PER_BLOCK * S5_STATE

ADAM_LR = 0.001
ADAM_B1 = 0.9
ADAM_B2 = 0.999
ADAM_EPS = 1e-08
ADAM_WD = 0.01
ADAM_STEP = 10

N_DEV = 8
N_CHIP = 4
MESH = pl.DeviceIdType.MESH
VMEM_LIMIT = 52 * 1024 * 1024
ROW_TILE = 256


def _params(sem=None, vmem=None):
    return pltpu.CompilerParams(dimension_semantics=sem, vmem_limit_bytes=vmem)


def mm_nn(a, b, out_dtype=F32, name="mm_nn", b_blk=0):
    M, K = a.shape
    N = b.shape[1]
    tm = math.gcd(ROW_TILE, M)

    def body(a_ref, b_ref, o_ref):
        o_ref[...] = jnp.dot(a_ref[...].astype(BF16), b_ref[...].astype(BF16),
                             preferred_element_type=F32).astype(o_ref.dtype)

    return pl.pallas_call(
        body, out_shape=jax.ShapeDtypeStruct((M, N), out_dtype), grid=(M // tm,),
        in_specs=[pl.BlockSpec((tm, K), lambda i: (i, 0)), pl.BlockSpec((K, N), lambda i: (b_blk, 0))],
        out_specs=pl.BlockSpec((tm, N), lambda i: (i, 0)),
        compiler_params=_params(("parallel",), VMEM_LIMIT), name=name)(a, b)


def mm_nt(a, b, out_dtype=F32, name="mm_nt", b_rows=None, b_blk=0):
    M, N = a.shape
    K = b.shape[0] if b_rows is None else b_rows
    tm = math.gcd(ROW_TILE, M)

    def body(a_ref, b_ref, o_ref):
        o_ref[...] = lax.dot_general(a_ref[...].astype(BF16), b_ref[...].astype(BF16),
                                     (((1,), (1,)), ((), ())),
                                     preferred_element_type=F32).astype(o_ref.dtype)

    return pl.pallas_call(
        body, out_shape=jax.ShapeDtypeStruct((M, K), out_dtype), grid=(M // tm,),
        in_specs=[pl.BlockSpec((tm, N), lambda i: (i, 0)), pl.BlockSpec((K, N), lambda i: (b_blk, 0))],
        out_specs=pl.BlockSpec((tm, K), lambda i: (i, 0)),
        compiler_params=_params(("parallel",), VMEM_LIMIT), name=name)(a, b)


def mm_tn(a, b, name="mm_tn"):
    M, K = a.shape
    N = b.shape[1]
    tn = math.gcd(512, N) if N % 128 == 0 and N > 512 else N

    def body(a_ref, b_ref, o_ref):
        o_ref[...] = lax.dot_general(a_ref[...].astype(BF16), b_ref[...].astype(BF16),
                                     (((0,), (0,)), ((), ())), preferred_element_type=F32)

    return pl.pallas_call(
        body, out_shape=jax.ShapeDtypeStruct((K, N), F32), grid=(N // tn,),
        in_specs=[pl.BlockSpec((M, K), lambda j: (0, 0)), pl.BlockSpec((M, tn), lambda j: (0, j))],
        out_specs=pl.BlockSpec((K, tn), lambda j: (0, j)),
        compiler_params=_params(("parallel",), VMEM_LIMIT), name=name)(a, b)


class Rows:
    def __init__(self, arr, width=None, row_off=0, col_blk=0):
        self.arr = arr
        self.width = arr.shape[1] if width is None else width
        self.row_off = row_off
        self.col_blk = col_blk

    def spec(self, tm):
        ro, cb = self.row_off // tm, self.col_blk
        return pl.BlockSpec((tm, self.width), lambda i: (i + ro, cb))


def _as_rows(x):
    return x if isinstance(x, Rows) else Rows(x)


def _row_tile(n_rows, n_ctx_rows, rows):
    tm = math.gcd(ROW_TILE, n_rows, n_ctx_rows)
    for r in rows:
        tm = math.gcd(tm, r.row_off)
    return tm


def _bc_spec(arr, n_ctx_blocks):
    g, _, d = arr.shape
    if g == 1:
        return pl.BlockSpec((1, 1, d), lambda i: (0, 0, 0))
    return pl.BlockSpec((1, 1, d), lambda i: ((i >= n_ctx_blocks).astype(jnp.int32), 0, 0))


def rowwise_fwd(fn, rows, bcs, out_dims, out_dtypes, n_rows, n_ctx_rows, name):
    rows = [_as_rows(r) for r in rows]
    tm = _row_tile(n_rows, n_ctx_rows, rows)
    ncb = n_ctx_rows // tm
    nr, nb = len(rows), len(bcs)

    def body(*refs):
        vals = [r[...].astype(F32) for r in refs[:nr]] + [b[0].astype(F32) for b in refs[nr:nr + nb]]
        outs = fn(*vals)
        for o_ref, v in zip(refs[nr + nb:], outs):
            o_ref[...] = v.astype(o_ref.dtype)

    outs = pl.pallas_call(
        body,
        out_shape=[jax.ShapeDtypeStruct((n_rows, d), dt) for d, dt in zip(out_dims, out_dtypes)],
        grid=(n_rows // tm,),
        in_specs=[r.spec(tm) for r in rows] + [_bc_spec(b, ncb) for b in bcs],
        out_specs=[pl.BlockSpec((tm, d), lambda i: (i, 0)) for d in out_dims],
        compiler_params=_params(("parallel",), VMEM_LIMIT), name=name)(*[r.arr for r in rows], *bcs)
    return outs


def rowwise_bwd(fn, rows, bcs, cts, diff_rows, diff_bcs, n_rows, n_ctx_rows, name, ct_extra=None, lat_add=None):
    rows = [_as_rows(r) for r in rows]
    cts = [_as_rows(c) for c in cts]
    extra = [_as_rows(ct_extra)] if ct_extra is not None else []
    tm = _row_tile(n_rows, n_ctx_rows, rows + cts + extra)
    ncb = n_ctx_rows // tm
    nr, nb, nc = len(rows), len(bcs), len(cts)
    ndr, ndb = len(diff_rows), len(diff_bcs)
    n_in = nr + nb + nc + len(extra) + (lat_add is not None)

    def body(*refs):
        i = pl.program_id(0)
        rvals = [r[...].astype(F32) for r in refs[:nr]]
        bvals = [b[0].astype(F32) for b in refs[nr:nr + nb]]
        cvals = [c[...].astype(F32) for c in refs[nr + nb:nr + nb + nc]]
        if extra:
            cvals[0] = cvals[0] + refs[nr + nb + nc][...].astype(F32)
        outs = refs[n_in:]

        def f(*d):
            rv, bv = list(rvals), list(bvals)
            for k, idx in enumerate(diff_rows):
                rv[idx] = d[k]
            for k, idx in enumerate(diff_bcs):
                bv[idx] = d[ndr + k]
            return tuple(fn(*rv, *bv))

        primals = [rvals[k] for k in diff_rows] + [bvals[k] for k in diff_bcs]
        _, vjp = jax.vjp(f, *primals)
        grads = list(vjp(tuple(cvals)))
        if lat_add is not None:
            add = refs[n_in - 1][...]
            grads[0] = grads[0] + (add if lat_add.shape[0] == n_rows else jnp.where(i >= ncb, add, 0.0))
        for k in range(ndr):
            outs[k][...] = grads[k].astype(outs[k].dtype)
        for k, idx in enumerate(diff_bcs):
            o_ref = outs[ndr + k]
            first = (i == 0)
            if bcs[idx].shape[0] == 2:
                first = first | (i == ncb)

            @pl.when(first)
            def _(o_ref=o_ref):
                o_ref[...] = jnp.zeros_like(o_ref)

            o_ref[0] += grads[ndr + k]

    out_shape = [jax.ShapeDtypeStruct((n_rows, rows[k].width), F32) for k in diff_rows]
    out_shape += [jax.ShapeDtypeStruct(bcs[k].shape, F32) for k in diff_bcs]
    out_specs = [pl.BlockSpec((tm, rows[k].width), lambda i: (i, 0)) for k in diff_rows]
    out_specs += [_bc_spec(bcs[k], ncb) for k in diff_bcs]
    ins = [r.arr for r in rows] + list(bcs) + [c.arr for c in cts + extra]
    in_specs = [r.spec(tm) for r in rows] + [_bc_spec(b, ncb) for b in bcs] + [c.spec(tm) for c in cts + extra]
    if lat_add is not None:
        ins.append(lat_add)
        skip = ncb if lat_add.shape[0] != n_rows else 0
        in_specs.append(pl.BlockSpec((tm, lat_add.shape[1]), lambda i: (jnp.maximum(i - skip, 0), 0)))
    outs = pl.pallas_call(
        body, out_shape=out_shape, grid=(n_rows // tm,), in_specs=in_specs, out_specs=out_specs,
        compiler_params=_params(("arbitrary",), VMEM_LIMIT), name=name)(*ins)
    return outs


def _rms(x):
    return x * lax.rsqrt(jnp.mean(x * x, axis=-1, keepdims=True) + EPS)


def _sigmoid(x):
    return 0.5 * (jnp.tanh(0.5 * x) + 1.0)


def _silu(x):
    return x * _sigmoid(x)


def _gelu_tanh(x):
    return 0.5 * x * (1.0 + jnp.tanh(math.sqrt(2.0 / math.pi) * (x + 0.044715 * (x * x * x))))


def f_norm_mod(x, g, sc, sh):
    return ((_rms(x) * g) * (1.0 + sc) + sh,)


def f_rms(x, g):
    return (_rms(x) * g,)


def f_gate(o, z):
    return (o * _silu(z),)


def f_s5_act(y, u, d):
    return (_gelu_tanh(y + d * u),)


def f_s5_glu(ya, gl, z, b):
    return (ya * _sigmoid(gl + b) * _silu(z),)


def norm_proj(x, g, sc, sh, w, n_ctx, name):
    n, d = x.shape
    nw = w.shape[1]
    tm = math.gcd(ROW_TILE, n, n_ctx)
    ncb = n_ctx // tm

    def body(x_ref, g_ref, sc_ref, sh_ref, w_ref, h_ref, p_ref):
        h = f_norm_mod(x_ref[...], g_ref[0], sc_ref[0], sh_ref[0])[0].astype(BF16)
        h_ref[...] = h
        p_ref[...] = jnp.dot(h, w_ref[...], preferred_element_type=F32)

    row = pl.BlockSpec((tm, d), lambda i: (i, 0))
    return pl.pallas_call(
        body, out_shape=[jax.ShapeDtypeStruct((n, d), BF16), jax.ShapeDtypeStruct((n, nw), F32)], grid=(n // tm,),
        in_specs=[row, _bc_spec(g, ncb), _bc_spec(sc, ncb), _bc_spec(sh, ncb), pl.BlockSpec(w.shape, lambda i: (0, 0))],
        out_specs=[row, pl.BlockSpec((tm, nw), lambda i: (i, 0))],
        compiler_params=_params(("parallel",), VMEM_LIMIT), name=name)(x, g, sc, sh, w)


def norm_proj_bwd(terms, w, x, g, sc, sh, add, n_ctx, name, latent_dx_only=False):
    n, d = x.shape
    tm = math.gcd(ROW_TILE, n, n_ctx, *[t[2] for t in terms])
    ncb = n_ctx // tm
    nt = len(terms)
    add_skip = ncb if add.shape[0] != n else 0

    def body(*refs):
        i = pl.program_id(0)
        a_refs = refs[:nt]
        w_ref, x_ref, g_ref, sc_ref, sh_ref, add_ref = refs[nt:nt + 6]
        dx_ref, dg_ref, dsc_ref, dsh_ref = refs[nt + 6:]
        d_h = None
        for a_ref, (a, off, first) in zip(a_refs, terms):
            part = lax.dot_general(a_ref[...].astype(BF16), w_ref[:, off:off + a.shape[1]], NT_DIMS,
                                   preferred_element_type=F32)
            if first:
                part = jnp.where(i >= first // tm, part, 0.0)
            d_h = part if d_h is None else d_h + part
        _, vjp = jax.vjp(lambda x_, g_, sc_, sh_: f_norm_mod(x_, g_, sc_, sh_), x_ref[...], g_ref[0], sc_ref[0],
                         sh_ref[0])
        d_x, d_g, d_sc, d_sh = vjp((d_h,))
        extra = add_ref[...]
        dx_ref[...] = d_x + (extra if add_skip == 0 else jnp.where(i >= ncb, extra, 0.0))

        @pl.when(i == 0)
        def _():
            dg_ref[...] = jnp.zeros_like(dg_ref)

        @pl.when((i == 0) | (i == ncb))
        def _():
            dsc_ref[...] = jnp.zeros_like(dsc_ref)
            dsh_ref[...] = jnp.zeros_like(dsh_ref)

        dg_ref[0] += d_g
        dsc_ref[0] += d_sc
        dsh_ref[0] += d_sh

    def a_spec(a, first):
        skip = first // tm
        return pl.BlockSpec((tm, a.shape[1]), lambda i: (jnp.maximum(i - skip, 0), 0))

    row = pl.BlockSpec((tm, d), lambda i: (i, 0))
    dx_skip = ncb if latent_dx_only else 0
    return pl.pallas_call(
        body,
        out_shape=[jax.ShapeDtypeStruct((n - dx_skip * tm, d), F32), jax.ShapeDtypeStruct(g.shape, F32),
                   jax.ShapeDtypeStruct(sc.shape, F32), jax.ShapeDtypeStruct(sh.shape, F32)],
        grid=(n // tm,),
        in_specs=[a_spec(a, first) for a, _, first in terms]
        + [pl.BlockSpec(w.shape, lambda i: (0, 0)), row, _bc_spec(g, ncb), _bc_spec(sc, ncb), _bc_spec(sh, ncb),
           pl.BlockSpec((tm, d), lambda i: (jnp.maximum(i - add_skip, 0), 0))],
        out_specs=[pl.BlockSpec((tm, d), lambda i: (jnp.maximum(i - dx_skip, 0), 0)),
                   _bc_spec(g, ncb), _bc_spec(sc, ncb), _bc_spec(sh, ncb)],
        compiler_params=_params(("arbitrary",), VMEM_LIMIT), name=name)(*[t[0] for t in terms], w, x, g, sc, sh, add)


def mla_post_fwd(o, p0, x0, gate, w_out, n_ctx, name="l0_post"):
    n, d = o.shape
    tm = math.gcd(ROW_TILE, n, n_ctx)
    ncb = n_ctx // tm

    def body(o_ref, z_ref, x_ref, gt_ref, w_ref, x1_ref, og_ref, out_ref):
        og = f_gate(o_ref[...], z_ref[...])[0].astype(BF16)
        out = jnp.dot(og, w_ref[...], preferred_element_type=F32)
        og_ref[...] = og
        out_ref[...] = out
        x1_ref[...] = x_ref[...] + gt_ref[0] * out

    row = pl.BlockSpec((tm, d), lambda i: (i, 0))
    return pl.pallas_call(
        body, out_shape=[jax.ShapeDtypeStruct((n, d), F32), jax.ShapeDtypeStruct((n, d), BF16),
                         jax.ShapeDtypeStruct((n, d), F32)],
        grid=(n // tm,),
        in_specs=[row, row, row, _bc_spec(gate, ncb), pl.BlockSpec((d, d), lambda i: (0, 0))],
        out_specs=[row, row, row],
        compiler_params=_params(("parallel",), VMEM_LIMIT), name=name)(o, p0, x0, gate, w_out)


def mla_post_bwd(dx1, out, og, o, p0, gate, w_out, n_ctx, name="l0_post_bwd"):
    n, d = o.shape
    tm = math.gcd(ROW_TILE, n, n_ctx)
    ncb = n_ctx // tm

    def body(dx_ref, out_ref, og_ref, o_ref, z_ref, gt_ref, w_ref, do_ref, dz_ref, dgt_ref, dw_ref):
        i = pl.program_id(0)

        @pl.when(i == 0)
        def _():
            dw_ref[...] = jnp.zeros_like(dw_ref)

        @pl.when((i == 0) | (i == ncb))
        def _():
            dgt_ref[...] = jnp.zeros_like(dgt_ref)

        dx = dx_ref[...]
        dgt_ref[0] += jnp.sum(dx * out_ref[...], axis=0, keepdims=True)
        d_out16 = (gt_ref[0] * dx).astype(BF16)
        dw_ref[...] += lax.dot_general(og_ref[...], d_out16, (((0,), (0,)), ((), ())), preferred_element_type=F32)
        d_og = lax.dot_general(d_out16, w_ref[...], NT_DIMS, preferred_element_type=F32)
        _, gate_vjp = jax.vjp(lambda o_, z_: f_gate(o_, z_), o_ref[...], z_ref[...])
        d_o, d_z = gate_vjp((d_og,))
        do_ref[...] = d_o
        dz_ref[...] = d_z

    row = pl.BlockSpec((tm, d), lambda i: (i, 0))
    mat = pl.BlockSpec((d, d), lambda i: (0, 0))
    return pl.pallas_call(
        body, out_shape=[jax.ShapeDtypeStruct((n, d), F32), jax.ShapeDtypeStruct((n, d), F32),
                         jax.ShapeDtypeStruct(gate.shape, F32), jax.ShapeDtypeStruct((d, d), F32)],
        grid=(n // tm,),
        in_specs=[row, row, row, row, row, _bc_spec(gate, ncb), mat],
        out_specs=[row, row, _bc_spec(gate, ncb), mat],
        compiler_params=_params(("arbitrary",), VMEM_LIMIT), name=name)(dx1, out, og, o, p0, gate, w_out)


def s5_tail(y_ssm, p1, x1p, target, n_ctx, d_vec, b_glu, gate, final_g, w_glu, w_out, name="l1_tail"):
    n, d = y_ssm.shape
    tm = math.gcd(ROW_TILE, n, n_ctx)
    off = n_ctx // tm
    tn_dims = (((0,), (0,)), ((), ()))

    def row_loss(x, g, t):
        e = _rms(x) * g - t
        return 0.5 * (e * e) * (1.0 / d)

    def body(y_ref, u_ref, z_ref, x1_ref, t_ref, d_ref, b_ref, gt_ref, fg_ref, wg_ref, wo_ref,
             l_ref, dx_ref, dy_ref, du_ref, dz_ref, dfg_ref, dgt_ref, db_ref, dd_ref, dwg_ref, dwo_ref):
        @pl.when(pl.program_id(0) == 0)
        def _():
            for r in (l_ref, dfg_ref, dgt_ref, db_ref, dd_ref, dwg_ref, dwo_ref):
                r[...] = jnp.zeros_like(r)

        u, z, tgt, gt = u_ref[...], z_ref[...], t_ref[...], gt_ref[...]
        (ya,), act_vjp = jax.vjp(lambda y_, u_, d_: f_s5_act(y_, u_, d_), y_ref[...], u, d_ref[...])
        ya16 = ya.astype(BF16)
        gl = jnp.dot(ya16, wg_ref[...], preferred_element_type=F32)
        (y3,), glu_vjp = jax.vjp(lambda a_, g_, z_, b_: f_s5_glu(a_, g_, z_, b_), ya, gl, z, b_ref[...])
        y3_16 = y3.astype(BF16)
        out1 = jnp.dot(y3_16, wo_ref[...], preferred_element_type=F32)
        lterm, loss_vjp = jax.vjp(lambda x_, g_: row_loss(x_, g_, tgt), x1_ref[...] + gt * out1, fg_ref[...])
        dx2, dfg = loss_vjp(jnp.ones_like(lterm))
        l_ref[...] += jnp.sum(lterm, axis=0, keepdims=True)
        dfg_ref[...] += dfg
        dx_ref[...] = dx2
        dgt_ref[...] += jnp.sum(dx2 * out1, axis=0, keepdims=True)
        d_out16 = (gt * dx2).astype(BF16)
        dwo_ref[...] += lax.dot_general(y3_16, d_out16, tn_dims, preferred_element_type=F32)
        d_y3 = lax.dot_general(d_out16, wo_ref[...], NT_DIMS, preferred_element_type=F32)
        d_ya, d_gl, d_z, d_b = glu_vjp((d_y3,))
        dz_ref[...] = d_z
        db_ref[...] += d_b
        d_gl16 = d_gl.astype(BF16)
        dwg_ref[...] += lax.dot_general(ya16, d_gl16, tn_dims, preferred_element_type=F32)
        d_ya = d_ya + lax.dot_general(d_gl16, wg_ref[...], NT_DIMS, preferred_element_type=F32)
        d_y, d_u, d_d = act_vjp((d_ya,))
        dy_ref[...] = d_y
        du_ref[...] = d_u
        dd_ref[...] += d_d

    row = pl.BlockSpec((tm, d), lambda i: (i, 0))
    vecs = pl.BlockSpec((1, d), lambda i: (0, 0))
    mat = pl.BlockSpec((d, d), lambda i: (0, 0))
    return pl.pallas_call(
        body,
        out_shape=[jax.ShapeDtypeStruct((1, d), F32)] + [jax.ShapeDtypeStruct((n, d), F32)] * 4
        + [jax.ShapeDtypeStruct((1, d), F32)] * 4 + [jax.ShapeDtypeStruct((d, d), F32)] * 2,
        grid=(n // tm,),
        in_specs=[row, pl.BlockSpec((tm, d), lambda i: (i + off, 0)), pl.BlockSpec((tm, d), lambda i: (i + off, 1)),
                  pl.BlockSpec((tm, d), lambda i: (i + off, 0)), row, vecs, vecs, vecs, vecs, mat, mat],
        out_specs=[vecs, row, row, row, row, vecs, vecs, vecs, vecs, mat, mat],
        compiler_params=_params(("arbitrary",), VMEM_LIMIT), name=name)(
            y_ssm, p1, p1, x1p, target, d_vec, b_glu, gate, final_g, w_glu, w_out)


NT_DIMS = (((1,), (1,)), ((), ()))
HEAD_LANES = 128
N_PAIRS = MLA_HEADS // 2


def _own_lanes(shape, hh):
    lane = lax.broadcasted_iota(jnp.int32, shape, len(shape) - 1)
    return (lane < V_HEAD_DIM) if hh == 0 else (lane >= V_HEAD_DIM)


def _delta_lane(hh):
    return V_HEAD_DIM if hh == 0 else 0


def _rope_tiles(x, cos, sin_next, sin_prev, inverse):
    width = x.shape[-1]
    reps = width // HEAD_LANES
    c, sn, sp = (jnp.tile(t, (1, reps)) for t in (cos, sin_next, sin_prev))
    if inverse:
        return x * c + pltpu.roll(x * sn, 8, 1) + pltpu.roll(x * sp, width - 8, 1)
    return x * c + pltpu.roll(x, width - 8, 1) * sn + pltpu.roll(x, 8, 1) * sp


STAT_LANE = QK_DIM


def _with_stat(x16, col, lane0):
    hi = col.astype(BF16)
    r1 = col - hi.astype(F32)
    mid = r1.astype(BF16)
    lo = (r1 - mid.astype(F32)).astype(BF16)
    lane = lax.broadcasted_iota(jnp.int32, x16.shape, 1)
    return jnp.where(lane == lane0, hi, jnp.where(lane == lane0 + 1, mid, jnp.where(lane == lane0 + 2, lo, x16)))


def attn_fwd(qb, kb, vb, n_ctx):
    T = qb.shape[0]
    tq = math.gcd(ROW_TILE, n_ctx)
    nq, ncb = T // tq, n_ctx // tq

    def body(q_ref, k_ref, v_ref, o_ref, lse_ref, qs_ref):
        qi = pl.program_id(1)

        def rows(n_keys):
            v = v_ref[:n_keys, :]
            outs = []
            for hh in range(2):
                hs = slice(hh * HEAD_LANES, (hh + 1) * HEAD_LANES)
                s = lax.dot_general(q_ref[:, hs], k_ref[:n_keys, hs], NT_DIMS,
                                    preferred_element_type=F32) * SOFTMAX_SCALE
                m = jnp.max(s, axis=-1, keepdims=True)
                p = jnp.exp(s - m)
                l = jnp.sum(p, axis=-1, keepdims=True)
                outs.append(jnp.dot(p.astype(BF16), v, preferred_element_type=F32) / l)
                lse = m + jnp.log(l)
                lse_ref[hh] = lse
                qs_ref[:, hs] = _with_stat(q_ref[:, hs], lse * (-1.0 / SOFTMAX_SCALE), STAT_LANE)
            o_ref[...] = jnp.where(_own_lanes(outs[0].shape, 0), outs[0], outs[1])

        pl.when(qi < ncb)(lambda: rows(n_ctx))
        pl.when(qi >= ncb)(lambda: rows(T))

    return pl.pallas_call(
        body,
        out_shape=[jax.ShapeDtypeStruct((T, MLA_HEADS * V_HEAD_DIM), F32),
                   jax.ShapeDtypeStruct((MLA_HEADS, T, 1), F32), jax.ShapeDtypeStruct(qb.shape, BF16)],
        grid=(N_PAIRS, nq),
        in_specs=[pl.BlockSpec((tq, 2 * HEAD_LANES), lambda h, i: (i, h)),
                  pl.BlockSpec((T, 2 * HEAD_LANES), lambda h, i: (0, h)),
                  pl.BlockSpec((T, 2 * V_HEAD_DIM), lambda h, i: (0, h))],
        out_specs=[pl.BlockSpec((tq, 2 * V_HEAD_DIM), lambda h, i: (i, h)),
                   pl.BlockSpec((2, tq, 1), lambda h, i: (h, i, 0)),
                   pl.BlockSpec((tq, 2 * HEAD_LANES), lambda h, i: (i, h))],
        compiler_params=_params(("parallel", "parallel"), VMEM_LIMIT), name="attn_fwd")(qb, kb, vb)


def attn_bwd_dq(qb, kb, vb, o, do, lse, tabs, n_ctx):
    T = qb.shape[0]
    tq = math.gcd(ROW_TILE, n_ctx)
    nq, ncb = T // tq, n_ctx // tq

    def body(q_ref, k_ref, v_ref, o_ref, do_ref, lse_ref, c_ref, sn_ref, sp_ref, dq_ref, dos_ref):
        qi = pl.program_id(1)

        def rows(n_keys):
            v = v_ref[:n_keys, :]
            dqs = []
            for hh in range(2):
                hs = slice(hh * HEAD_LANES, (hh + 1) * HEAD_LANES)
                k = k_ref[:n_keys, hs]
                do = jnp.where(_own_lanes(do_ref.shape, hh), do_ref[...], 0.0)
                delta = jnp.sum(do * o_ref[...], axis=-1, keepdims=True)
                s = lax.dot_general(q_ref[:, hs], k, NT_DIMS, preferred_element_type=F32) * SOFTMAX_SCALE
                p = jnp.exp(s - lse_ref[hh])
                do16 = do.astype(BF16)
                dp = lax.dot_general(do16, v, NT_DIMS, preferred_element_type=F32)
                ds = p * (dp - delta) * SOFTMAX_SCALE
                dqs.append(jnp.dot(ds.astype(BF16), k, preferred_element_type=F32))
                dos_ref[:, hs] = _with_stat(do16, delta, _delta_lane(hh))
            dq = jnp.concatenate(dqs, axis=1)
            dq_ref[...] = _rope_tiles(dq, c_ref[...], sn_ref[...], sp_ref[...], True).astype(BF16)

        pl.when(qi < ncb)(lambda: rows(n_ctx))
        pl.when(qi >= ncb)(lambda: rows(T))

    tab = pl.BlockSpec((tq, HEAD_LANES), lambda h, i: (i, 0))
    return pl.pallas_call(
        body,
        out_shape=[jax.ShapeDtypeStruct((T, MLA_HEADS * HEAD_LANES), BF16)] * 2,
        grid=(N_PAIRS, nq),
        in_specs=[pl.BlockSpec((tq, 2 * HEAD_LANES), lambda h, i: (i, h)),
                  pl.BlockSpec((T, 2 * HEAD_LANES), lambda h, i: (0, h)),
                  pl.BlockSpec((T, 2 * V_HEAD_DIM), lambda h, i: (0, h)),
                  pl.BlockSpec((tq, 2 * V_HEAD_DIM), lambda h, i: (i, h)),
                  pl.BlockSpec((tq, 2 * V_HEAD_DIM), lambda h, i: (i, h)),
                  pl.BlockSpec((2, tq, 1), lambda h, i: (h, i, 0)), tab, tab, tab],
        out_specs=[pl.BlockSpec((tq, 2 * HEAD_LANES), lambda h, i: (i, h))] * 2,
        compiler_params=_params(("parallel", "parallel"), VMEM_LIMIT), name="attn_bwd_dq")(
            qb, kb, vb, o, do, lse, *tabs)


def attn_bwd_dkv(qs, kb, vb, dos, n_ctx):
    T = qs.shape[0]
    tq = math.gcd(ROW_TILE, n_ctx)
    nq, ncb = T // tq, n_ctx // tq

    def body(q_ref, do_ref, k_ref, v_ref, dk_ref, dv_ref):
        kj = pl.program_id(1)

        def cols(first):
            v = v_ref[...]
            lane = lax.broadcasted_iota(jnp.int32, v.shape, 1)
            dvs = []
            for hh in range(2):
                hs = slice(hh * HEAD_LANES, (hh + 1) * HEAD_LANES)
                q = q_ref[first:, hs]
                do16 = do_ref[first:, hs]
                in_delta = (lane >= _delta_lane(hh)) & (lane < _delta_lane(hh) + 3)
                v_minus = jnp.where(in_delta, -jnp.ones_like(v), v)
                pt = jnp.exp(lax.dot_general(k_ref[:, hs], q, NT_DIMS, preferred_element_type=F32) * SOFTMAX_SCALE)
                dvs.append(jnp.dot(pt.astype(BF16), do16, preferred_element_type=F32))
                dst = pt * lax.dot_general(v_minus, do16, NT_DIMS, preferred_element_type=F32) * SOFTMAX_SCALE
                dk_ref[:, hs] = jnp.dot(dst.astype(BF16), q, preferred_element_type=F32)
            dv_ref[...] = jnp.where(_own_lanes(dvs[0].shape, 0), dvs[0], dvs[1])

        pl.when(kj < ncb)(lambda: cols(0))
        pl.when(kj >= ncb)(lambda: cols(n_ctx))

    return pl.pallas_call(
        body,
        out_shape=[jax.ShapeDtypeStruct((T, MLA_HEADS * HEAD_LANES), F32),
                   jax.ShapeDtypeStruct((T, MLA_HEADS * V_HEAD_DIM), F32)],
        grid=(N_PAIRS, nq),
        in_specs=[pl.BlockSpec((T, 2 * HEAD_LANES), lambda h, j: (0, h)),
                  pl.BlockSpec((T, 2 * HEAD_LANES), lambda h, j: (0, h)),
                  pl.BlockSpec((tq, 2 * HEAD_LANES), lambda h, j: (j, h)),
                  pl.BlockSpec((tq, 2 * V_HEAD_DIM), lambda h, j: (j, h))],
        out_specs=[pl.BlockSpec((tq, 2 * HEAD_LANES), lambda h, j: (j, h)),
                   pl.BlockSpec((tq, 2 * V_HEAD_DIM), lambda h, j: (j, h))],
        compiler_params=_params(("parallel", "parallel"), VMEM_LIMIT), name="attn_bwd_dkv")(
            qs, dos, kb, vb)


def _split_bf16(x):
    hi = x.astype(BF16)
    return hi, (x - hi.astype(F32)).astype(BF16)


def q_heads(cq, gain, w_uq_p, tabs, name="l0_uq"):
    T, K = cq.arr.shape[0], cq.width
    N = w_uq_p.shape[1]
    tm = math.gcd(ROW_TILE, T)

    def body(a_ref, g_ref, w_ref, c_ref, sn_ref, sp_ref, o_ref, n_ref):
        qn = f_rms(a_ref[...], g_ref[0])[0].astype(BF16)
        n_ref[...] = qn
        acc = jnp.dot(qn, w_ref[...], preferred_element_type=F32)
        o_ref[...] = _rope_tiles(acc, c_ref[...], sn_ref[...], sp_ref[...], False).astype(BF16)

    tab = pl.BlockSpec((tm, HEAD_LANES), lambda i: (i, 0))
    return pl.pallas_call(
        body, out_shape=[jax.ShapeDtypeStruct((T, N), BF16), jax.ShapeDtypeStruct((T, K), BF16)], grid=(T // tm,),
        in_specs=[cq.spec(tm), pl.BlockSpec((1, 1, K), lambda i: (0, 0, 0)), pl.BlockSpec((K, N), lambda i: (0, 0)),
                  tab, tab, tab],
        out_specs=[pl.BlockSpec((tm, N), lambda i: (i, 0)), pl.BlockSpec((tm, K), lambda i: (i, 0))],
        compiler_params=_params(("parallel",), VMEM_LIMIT), name=name)(cq.arr, gain, w_uq_p, *tabs)


def kv_heads(ckv, gain, w_kn_p, w_v, kr, spread, tabs, name="l0_ukv"):
    T, K = ckv.arr.shape[0], ckv.width
    N = w_kn_p.shape[1]
    NV = w_v.shape[1]
    tm = math.gcd(ROW_TILE, T)

    def body(a_ref, g_ref, wk_ref, wv_ref, kr_ref, e_ref, c_ref, sn_ref, sp_ref, k_ref, v_ref, n_ref):
        a = f_rms(a_ref[...], g_ref[0])[0].astype(BF16)
        n_ref[...] = a
        hi, lo = _split_bf16(kr_ref[...])
        acc = (jnp.dot(a, wk_ref[...], preferred_element_type=F32)
               + jnp.dot(hi, e_ref[...], preferred_element_type=F32)
               + jnp.dot(lo, e_ref[...], preferred_element_type=F32))
        roped = _rope_tiles(acc, c_ref[...], sn_ref[...], sp_ref[...], False)
        lane = lax.broadcasted_iota(jnp.int32, roped.shape, 1) % HEAD_LANES
        k_ref[...] = jnp.where((lane >= STAT_LANE) & (lane < STAT_LANE + 3), 1.0, roped).astype(BF16)
        v_ref[...] = jnp.dot(a, wv_ref[...], preferred_element_type=F32).astype(BF16)

    tab = pl.BlockSpec((tm, HEAD_LANES), lambda i: (i, 0))
    return pl.pallas_call(
        body, out_shape=[jax.ShapeDtypeStruct((T, N), BF16), jax.ShapeDtypeStruct((T, NV), BF16),
                         jax.ShapeDtypeStruct((T, K), BF16)], grid=(T // tm,),
        in_specs=[ckv.spec(tm), pl.BlockSpec((1, 1, K), lambda i: (0, 0, 0)), pl.BlockSpec((K, N), lambda i: (0, 0)),
                  pl.BlockSpec((K, NV), lambda i: (0, 0)), pl.BlockSpec((tm, QK_ROPE_DIM), lambda i: (i, 0)),
                  pl.BlockSpec((QK_ROPE_DIM, N), lambda i: (0, 0)), tab, tab, tab],
        out_specs=[pl.BlockSpec((tm, N), lambda i: (i, 0)), pl.BlockSpec((tm, NV), lambda i: (i, 0)),
                   pl.BlockSpec((tm, K), lambda i: (i, 0))],
        compiler_params=_params(("parallel",), VMEM_LIMIT), name=name)(ckv.arr, gain, w_kn_p, w_v, kr, spread, *tabs)


def heads_unrope(d, tabs, spread=None, name="unrope"):
    T, N = d.shape
    tm = math.gcd(ROW_TILE, T)

    def body(*refs):
        if spread is None:
            d_ref, c_ref, sn_ref, sp_ref, o_ref = refs
        else:
            d_ref, c_ref, sn_ref, sp_ref, e_ref, o_ref, kr_ref = refs
        g = _rope_tiles(d_ref[...], c_ref[...], sn_ref[...], sp_ref[...], True)
        o_ref[...] = g.astype(BF16)
        if spread is not None:
            hi, lo = _split_bf16(g)
            kr_ref[...] = (lax.dot_general(hi, e_ref[...], NT_DIMS, preferred_element_type=F32)
                           + lax.dot_general(lo, e_ref[...], NT_DIMS, preferred_element_type=F32))

    tab = pl.BlockSpec((tm, HEAD_LANES), lambda i: (i, 0))
    row = pl.BlockSpec((tm, N), lambda i: (i, 0))
    ins, in_specs = [d, *tabs], [row, tab, tab, tab]
    out_shape, out_specs = [jax.ShapeDtypeStruct((T, N), BF16)], [row]
    if spread is not None:
        ins.append(spread)
        in_specs.append(pl.BlockSpec(spread.shape, lambda i: (0, 0)))
        out_shape.append(jax.ShapeDtypeStruct((T, spread.shape[0]), F32))
        out_specs.append(pl.BlockSpec((tm, spread.shape[0]), lambda i: (i, 0)))
    return pl.pallas_call(
        body, out_shape=out_shape, grid=(T // tm,), in_specs=in_specs, out_specs=out_specs,
        compiler_params=_params(("parallel",), VMEM_LIMIT), name=name)(*ins)


def _cmul(ar, ai, br, bi):
    return ar * br - ai * bi, ar * bi + ai * br


def s5_chain(finals, s0, a, n_steps, reverse, name):
    W = finals.shape[-1]
    first = N_SEG - 1 if reverse else 0

    def body(f_ref, s0_ref, a_ref, c_ref):
        pr, pi = jnp.ones((1, W), F32), jnp.zeros((1, W), F32)
        br, bi = a_ref[0], a_ref[1]
        n = n_steps
        while n:
            if n & 1:
                pr, pi = _cmul(pr, pi, br, bi)
            br, bi = _cmul(br, bi, br, bi)
            n >>= 1
        fr, fi = f_ref[0], f_ref[1]
        row = lax.broadcasted_iota(jnp.int32, (N_SEG, W), 0)
        s0r = jnp.broadcast_to(s0_ref[0], (N_SEG, W))
        s0i = jnp.broadcast_to(s0_ref[1], (N_SEG, W))
        cr = jnp.where(row == first, s0r, 0.0)
        ci = jnp.where(row == first, s0i, 0.0)
        shift = N_SEG - 1 if reverse else 1
        for _ in range(N_SEG - 1):
            mr, mi = _cmul(pr, pi, cr, ci)
            tr = pltpu.roll(fr + mr, shift, 0)
            ti = pltpu.roll(fi + mi, shift, 0)
            cr = jnp.where(row == first, s0r, tr)
            ci = jnp.where(row == first, s0i, ti)
        c_ref[0] = cr
        c_ref[1] = ci

    return pl.pallas_call(body, out_shape=jax.ShapeDtypeStruct((2, N_SEG, W), F32), name=name)(finals, s0, a)


def _scan_chunk(bur, bui, st_ref, a_ref, n_steps, reverse):
    for lc in range(S5_LANES // BLK_ST):
        sl = slice(lc * BLK_ST, (lc + 1) * BLK_ST)
        lr = jnp.broadcast_to(a_ref[0, :, sl], (N_SEG, BLK_ST))
        li = jnp.broadcast_to(a_ref[1, :, sl], (N_SEG, BLK_ST))

        def step(jj, carry, sl=sl, lr=lr, li=li):
            sr, si = carry
            j = (n_steps - 1 - jj) if reverse else jj
            r0 = pl.multiple_of(j * N_SEG, N_SEG)
            nr = lr * sr - li * si + bur[pl.ds(r0, N_SEG), sl]
            ni = lr * si + li * sr + bui[pl.ds(r0, N_SEG), sl]
            bur[pl.ds(r0, N_SEG), sl] = nr
            bui[pl.ds(r0, N_SEG), sl] = ni
            return nr, ni

        sr, si = lax.fori_loop(0, n_steps, step, (st_ref[0, :, sl], st_ref[1, :, sl]))
        st_ref[0, :, sl] = sr
        st_ref[1, :, sl] = si


def _project_in(x16, w_re, w_im, bur, bui, adjoint):
    for gb in range(N_BLOCKS):
        xb = x16[:, gb * BLK_CH:(gb + 1) * BLK_CH]
        sl = slice(gb * BLK_ST, (gb + 1) * BLK_ST)
        if adjoint:
            dn = (((1,), (1,)), ((), ()))
            bur[:, sl] = lax.dot_general(xb, w_re[gb], dn, preferred_element_type=F32)
            bui[:, sl] = -lax.dot_general(xb, w_im[gb], dn, preferred_element_type=F32)
        else:
            bur[:, sl] = jnp.dot(xb, w_re[gb], preferred_element_type=F32)
            bui[:, sl] = jnp.dot(xb, w_im[gb], preferred_element_type=F32)


def s5_scan(act, w_re, w_im, a, init, *, reverse, adjoint=False, c_re=None, c_im=None, add=None,
            want_ckpt=False, rows=None, name):
    act_off, N = rows if rows is not None else (0, act.shape[0])
    R = math.gcd(ROW_TILE, N, act_off)
    nch, jc = N // R, R // N_SEG
    with_out = c_re is not None

    def chunk(i):
        return (nch - 1 - i) if reverse else i

    def body(*refs):
        act_ref, wre_ref, wim_ref, a_ref, init_ref = refs[:5]
        k = 5
        if with_out:
            cre_ref, cim_ref = refs[k:k + 2]
            k += 2
        if add is not None:
            add_ref = refs[k]
            k += 1
        if with_out:
            out_ref = refs[k]
            k += 1
        if want_ckpt:
            ck_ref = refs[k]
            k += 1
        fin_ref, bur, bui = refs[k:k + 3]

        @pl.when(pl.program_id(0) == 0)
        def _():
            fin_ref[...] = init_ref[...]

        if want_ckpt:
            ck_ref[0] = fin_ref[...]
        _project_in(act_ref[...].astype(BF16), wre_ref, wim_ref, bur, bui, adjoint)
        _scan_chunk(bur, bui, fin_ref, a_ref, jc, reverse)
        if with_out:
            for gb in range(N_BLOCKS):
                sl = slice(gb * BLK_ST, (gb + 1) * BLK_ST)
                y = (jnp.dot(bur[:, sl].astype(BF16), cre_ref[gb], preferred_element_type=F32)
                     - jnp.dot(bui[:, sl].astype(BF16), cim_ref[gb], preferred_element_type=F32))
                cs = slice(gb * BLK_CH, (gb + 1) * BLK_CH)
                if add is not None:
                    y = y + add_ref[:, cs]
                out_ref[:, cs] = y

    row_spec = pl.BlockSpec((R, D_MODEL), lambda i: (chunk(i), 0))
    act_spec = pl.BlockSpec((R, D_MODEL), lambda i: (chunk(i) + act_off // R, 0))
    w_spec = pl.BlockSpec(w_re.shape, lambda i: (0, 0, 0))
    st_spec = pl.BlockSpec((2, N_SEG, S5_LANES), lambda i: (0, 0, 0))
    ins = [act, w_re, w_im, a, init]
    in_specs = [act_spec, w_spec, w_spec, pl.BlockSpec((2, 1, S5_LANES), lambda i: (0, 0, 0)), st_spec]
    if with_out:
        ins += [c_re, c_im]
        in_specs += [pl.BlockSpec(c_re.shape, lambda i: (0, 0, 0))] * 2
    if add is not None:
        ins.append(add)
        in_specs.append(row_spec)
    out_shape, out_specs = [], []
    if with_out:
        out_shape.append(jax.ShapeDtypeStruct((N, D_MODEL), F32))
        out_specs.append(row_spec)
    if want_ckpt:
        out_shape.append(jax.ShapeDtypeStruct((nch, 2, N_SEG, S5_LANES), F32))
        out_specs.append(pl.BlockSpec((1, 2, N_SEG, S5_LANES), lambda i: (chunk(i), 0, 0, 0)))
    out_shape.append(jax.ShapeDtypeStruct((2, N_SEG, S5_LANES), F32))
    out_specs.append(st_spec)
    res = pl.pallas_call(
        body, out_shape=out_shape, grid=(nch,), in_specs=in_specs, out_specs=out_specs,
        scratch_shapes=[pltpu.VMEM((R, S5_LANES), F32), pltpu.VMEM((R, S5_LANES), F32)],
        compiler_params=_params(("arbitrary",), VMEM_LIMIT), name=name)(*ins)
    res = list(res)
    out = res.pop(0) if with_out else None
    ckpt = res.pop(0) if want_ckpt else None
    return out, ckpt, res[0]


def s5_grads(dy, u, ckpt, b_re, b_im, c_re, c_im, lam, init_adj, *, reverse, add=None, u_off=0, name):
    N = dy.shape[0]
    R = math.gcd(ROW_TILE, N, u_off)
    nch, jc = N // R, R // N_SEG
    W = S5_LANES

    def chunk(i):
        return i if reverse else (nch - 1 - i)

    def body(*refs):
        dy_ref, u_ref, ck_ref, bre_ref, bim_ref, cre_ref, cim_ref, lam_ref, init_ref = refs[:9]
        k = 9
        if add is not None:
            add_ref = refs[k]
            k += 1
        du_ref, dlam_ref, dbre_ref, dbim_ref, dcre_ref, dcim_ref, fin_ref = refs[k:k + 7]
        sr_buf, si_buf, er_buf, ei_buf, st_buf = refs[k + 7:k + 12]

        @pl.when(pl.program_id(0) == 0)
        def _():
            fin_ref[...] = init_ref[...]
            dlam_ref[...] = jnp.zeros_like(dlam_ref)
            dbre_ref[...] = jnp.zeros_like(dbre_ref)
            dbim_ref[...] = jnp.zeros_like(dbim_ref)
            dcre_ref[...] = jnp.zeros_like(dcre_ref)
            dcim_ref[...] = jnp.zeros_like(dcim_ref)

        u16 = u_ref[...].astype(BF16)
        dy16 = dy_ref[...].astype(BF16)
        st_buf[...] = ck_ref[0]
        _project_in(u16, bre_ref, bim_ref, sr_buf, si_buf, False)
        _scan_chunk(sr_buf, si_buf, st_buf, lam_ref, jc, reverse)
        _project_in(dy16, cre_ref, cim_ref, er_buf, ei_buf, True)
        for lc in range(W // BLK_ST):
            sl = slice(lc * BLK_ST, (lc + 1) * BLK_ST)
            lr = jnp.broadcast_to(lam_ref[0, :, sl], (N_SEG, BLK_ST))
            li = jnp.broadcast_to(lam_ref[1, :, sl], (N_SEG, BLK_ST))

            def one(r0, spr, spi, carry, sl=sl, lr=lr, li=li):
                gr, gi, ar, ai = carry
                nr = er_buf[pl.ds(r0, N_SEG), sl] + lr * gr + li * gi
                ni = ei_buf[pl.ds(r0, N_SEG), sl] + lr * gi - li * gr
                er_buf[pl.ds(r0, N_SEG), sl] = nr
                ei_buf[pl.ds(r0, N_SEG), sl] = ni
                return nr, ni, ar + spr * nr + spi * ni, ai + spr * ni - spi * nr

            def step(ff, carry, sl=sl, one=one):
                f = jc - 1 - ff
                j = (jc - 1 - f) if reverse else f
                jp = (j + 1) if reverse else (j - 1)
                r0 = pl.multiple_of(j * N_SEG, N_SEG)
                p0 = pl.multiple_of(jp * N_SEG, N_SEG)
                return one(r0, sr_buf[pl.ds(p0, N_SEG), sl], si_buf[pl.ds(p0, N_SEG), sl], carry)

            carry = (fin_ref[0, :, sl], fin_ref[1, :, sl], dlam_ref[0, :, sl], dlam_ref[1, :, sl])
            carry = lax.fori_loop(0, jc - 1, step, carry)
            r_first = (jc - 1) * N_SEG if reverse else 0
            gr, gi, ar, ai = one(r_first, ck_ref[0, 0, :, sl], ck_ref[0, 1, :, sl], carry)
            fin_ref[0, :, sl] = gr
            fin_ref[1, :, sl] = gi
            dlam_ref[0, :, sl] = ar
            dlam_ref[1, :, sl] = ai
        tn = (((0,), (0,)), ((), ()))
        nt = (((1,), (1,)), ((), ()))
        for gb in range(N_BLOCKS):
            sl = slice(gb * BLK_ST, (gb + 1) * BLK_ST)
            cs = slice(gb * BLK_CH, (gb + 1) * BLK_CH)
            gr16 = er_buf[:, sl].astype(BF16)
            gi16 = ei_buf[:, sl].astype(BF16)
            du = (lax.dot_general(gr16, bre_ref[gb], nt, preferred_element_type=F32)
                  + lax.dot_general(gi16, bim_ref[gb], nt, preferred_element_type=F32))
            if add is not None:
                du = du + add_ref[:, cs]
            du_ref[:, cs] = du
            ub, dyb = u16[:, cs], dy16[:, cs]
            dbre_ref[gb] += lax.dot_general(ub, gr16, tn, preferred_element_type=F32)
            dbim_ref[gb] += lax.dot_general(ub, gi16, tn, preferred_element_type=F32)
            dcre_ref[gb] += lax.dot_general(sr_buf[:, sl].astype(BF16), dyb, tn, preferred_element_type=F32)
            dcim_ref[gb] -= lax.dot_general(si_buf[:, sl].astype(BF16), dyb, tn, preferred_element_type=F32)

    row_spec = pl.BlockSpec((R, D_MODEL), lambda i: (chunk(i), 0))
    st_spec = pl.BlockSpec((2, N_SEG, W), lambda i: (0, 0, 0))
    wb_spec = pl.BlockSpec(b_re.shape, lambda i: (0, 0, 0))
    wc_spec = pl.BlockSpec(c_re.shape, lambda i: (0, 0, 0))
    ins = [dy, u, ckpt, b_re, b_im, c_re, c_im, lam, init_adj]
    u_spec = pl.BlockSpec((R, D_MODEL), lambda i: (chunk(i) + u_off // R, 0))
    in_specs = [row_spec, u_spec, pl.BlockSpec((1, 2, N_SEG, W), lambda i: (chunk(i), 0, 0, 0)),
                wb_spec, wb_spec, wc_spec, wc_spec, pl.BlockSpec((2, 1, W), lambda i: (0, 0, 0)), st_spec]
    if add is not None:
        ins.append(add)
        in_specs.append(row_spec)
    out_shape = [jax.ShapeDtypeStruct((N, D_MODEL), F32), jax.ShapeDtypeStruct((2, N_SEG, W), F32),
                 jax.ShapeDtypeStruct(b_re.shape, F32), jax.ShapeDtypeStruct(b_re.shape, F32),
                 jax.ShapeDtypeStruct(c_re.shape, F32), jax.ShapeDtypeStruct(c_re.shape, F32),
                 jax.ShapeDtypeStruct((2, N_SEG, W), F32)]
    out_specs = [row_spec, st_spec, wb_spec, wb_spec, wc_spec, wc_spec, st_spec]
    return pl.pallas_call(
        body, out_shape=out_shape, grid=(nch,), in_specs=in_specs, out_specs=out_specs,
        scratch_shapes=[pltpu.VMEM((R, W), F32) for _ in range(4)] + [pltpu.VMEM((2, N_SEG, W), F32)],
        compiler_params=_params(("arbitrary",), VMEM_LIMIT), name=name)(*ins)


def adamw(w, g, m, v, name="adamw", after=None):
    n, d = w.shape
    lanes = -(-d // 128) * 128
    tm = n
    while tm * lanes * 4 > (1 << 20) and tm % 16 == 0:
        tm //= 2
    c1 = 1.0 - ADAM_B1 ** ADAM_STEP
    c2 = 1.0 - ADAM_B2 ** ADAM_STEP

    def body(w_ref, g_ref, m_ref, v_ref, *rest):
        d_ref, nm_ref, nv_ref = rest[-3:]
        g_ = g_ref[...]
        m_ = ADAM_B1 * m_ref[...] + (1.0 - ADAM_B1) * g_
        v_ = ADAM_B2 * v_ref[...] + (1.0 - ADAM_B2) * (g_ * g_)
        d_ref[...] = -ADAM_LR * ((m_ / c1) / (jnp.sqrt(v_ / c2) + ADAM_EPS) + ADAM_WD * w_ref[...])
        nm_ref[...] = m_
        nv_ref[...] = v_

    spec = pl.BlockSpec((tm, d), lambda i: (i, 0))
    extra = [] if after is None else [after]
    return pl.pallas_call(
        body, out_shape=[jax.ShapeDtypeStruct((n, d), F32)] * 3, grid=(n // tm,),
        in_specs=[spec] * 4 + [pl.BlockSpec(memory_space=pl.ANY)] * len(extra), out_specs=[spec] * 3,
        compiler_params=_params(("parallel",), VMEM_LIMIT), name=name)(w, g, m, v, *extra)


def _coords():
    return lax.axis_index("x"), lax.axis_index("y"), lax.axis_index("c")


def exchange(arrays, out_shapes, remote, local, name, aliases=None):
    n_in, n_out, n_rem, n_loc = len(arrays), len(out_shapes), len(remote), len(local)

    def at(ref, idx):
        return ref if idx is None else ref.at[idx]

    def body(*refs):
        ins, outs = refs[:n_in], refs[n_in:n_in + n_out]
        send_sems, recv_sems, local_sems = refs[n_in + n_out:]
        me = _coords()
        sends, recvs = [], []
        for k, (flip, ii, src_at, oi, dst_at) in enumerate(remote):
            peer = (me[0] ^ flip[0], me[1] ^ flip[1], me[2] ^ flip[2])
            src = at(ins[ii], src_at(me, peer))
            sends.append(pltpu.make_async_remote_copy(
                src_ref=src, dst_ref=at(outs[oi], dst_at(me)), send_sem=send_sems.at[k], recv_sem=recv_sems.at[k],
                device_id=peer, device_id_type=MESH))
            recvs.append(pltpu.make_async_remote_copy(
                src_ref=src, dst_ref=at(outs[oi], dst_at(peer)), send_sem=send_sems.at[k], recv_sem=recv_sems.at[k],
                device_id=peer, device_id_type=MESH))
        locs = [pltpu.make_async_copy(at(ins[ii], src_at(me)), at(outs[oi], dst_at(me)), local_sems.at[k])
                for k, (ii, src_at, oi, dst_at) in enumerate(local)]
        for cp in locs + sends:
            cp.start()
        for cp in recvs:
            cp.wait_recv()
        for cp in sends:
            cp.wait_send()
        for cp in locs:
            cp.wait()

    hbm = pl.BlockSpec(memory_space=pl.ANY)
    return pl.pallas_call(
        body, out_shape=list(out_shapes), in_specs=[hbm] * n_in, out_specs=[hbm] * n_out,
        scratch_shapes=[pltpu.SemaphoreType.DMA((n_rem,)), pltpu.SemaphoreType.DMA((n_rem,)),
                        pltpu.SemaphoreType.DMA((max(n_loc, 1),))],
        input_output_aliases=aliases or {}, name=name)(*arrays)


ALL_FLIPS = [(dx, dy, dc) for dx in (0, 1) for dy in (0, 1) for dc in (0, 1)][1:]
CHIP_FLIPS = [(1, 0, 0), (0, 1, 0), (1, 1, 0)]
CORE_FLIP = (0, 0, 1)


def _dev_index(p):
    return 4 * p[0] + 2 * p[1] + p[2]


def _chip_index(p):
    return 2 * p[0] + p[1]


def _gather(xs, flips, index, n, name):
    arrays = [x[None] for x in xs]
    outs = [jax.ShapeDtypeStruct((n,) + x.shape, x.dtype) for x in xs]
    remote = [(f, a, lambda me, peer: (0,), a, lambda s: (index(s),)) for a in range(len(xs)) for f in flips]
    local = [(a, lambda me: (0,), a, lambda me: (index(me),)) for a in range(len(xs))]
    return exchange(arrays, outs, remote, local, name)


def allgather_devices(x, name):
    return _gather([x], ALL_FLIPS, _dev_index, N_DEV, name)[0]


def allgather_chips(xs, name):
    return _gather(xs, CHIP_FLIPS, _chip_index, N_CHIP, name)


def gather_halves(xs, name):
    n = len(xs)
    nk = n * len(CHIP_FLIPS)

    def body(*refs):
        ins, outs = refs[:n], refs[n:2 * n]
        ici_send, ici_recv, d2d_send, d2d_recv = refs[2 * n:]
        me = _coords()
        sibling = (me[0], me[1], 1 - me[2])
        first, passed, landed = [], [], []
        for a in range(n):
            half = ins[a].shape[0] // 2
            mine = ins[a].at[pl.ds(pl.multiple_of(me[2] * half, 16), half)]
            for j, flip in enumerate(CHIP_FLIPS):
                k = a * len(CHIP_FLIPS) + j
                peer = (me[0] ^ flip[0], me[1] ^ flip[1], me[2])
                first.append(pltpu.make_async_remote_copy(
                    src_ref=mine, dst_ref=outs[a].at[_chip_index(me), me[2]], send_sem=ici_send.at[k],
                    recv_sem=ici_recv.at[k], device_id=peer, device_id_type=MESH))
                arrived = outs[a].at[_chip_index(peer), me[2]]
                landed.append(pltpu.make_async_remote_copy(
                    src_ref=mine, dst_ref=arrived, send_sem=ici_send.at[k], recv_sem=ici_recv.at[k],
                    device_id=peer, device_id_type=MESH))
                passed.append(pltpu.make_async_remote_copy(
                    src_ref=arrived, dst_ref=arrived, send_sem=d2d_send.at[k], recv_sem=d2d_recv.at[k],
                    device_id=sibling, device_id_type=MESH))
        for cp in first:
            cp.start()
        for k in range(nk):
            landed[k].wait_recv()
            passed[k].start()
        for a in range(n):
            for j, flip in enumerate(CHIP_FLIPS):
                k = a * len(CHIP_FLIPS) + j
                peer_chip = _chip_index((me[0] ^ flip[0], me[1] ^ flip[1]))
                from_sibling = outs[a].at[peer_chip, 1 - me[2]]
                pltpu.make_async_remote_copy(
                    src_ref=from_sibling, dst_ref=from_sibling, send_sem=d2d_send.at[k], recv_sem=d2d_recv.at[k],
                    device_id=sibling, device_id_type=MESH).wait_recv()
        for cp in first + passed:
            cp.wait_send()

    hbm = pl.BlockSpec(memory_space=pl.ANY)
    return pl.pallas_call(
        body, out_shape=[jax.ShapeDtypeStruct((N_CHIP, 2, x.shape[0] // 2, x.shape[1]), x.dtype) for x in xs],
        in_specs=[hbm] * n, out_specs=[hbm] * n,
        scratch_shapes=[pltpu.SemaphoreType.DMA((nk,)) for _ in range(4)], name=name)(*xs)


HBM_SPEC = pl.BlockSpec(memory_space=pltpu.HBM)
SEM_SPEC = pl.BlockSpec(memory_space=pltpu.SEMAPHORE)
DATAFLOW = pltpu.SideEffectType.DATAFLOW_SIDE_EFFECTING


def _at(ref, idx):
    return ref if idx is None else ref.at[idx]


def _peer(me, flip):
    return (me[0] ^ flip[0], me[1] ^ flip[1], me[2] ^ flip[2])


def exchange_start(arrays, land_shapes, remote, name, after=None):
    n_in, n_out, nk = len(arrays), len(land_shapes), len(remote)
    after = list(after or [])
    n_after = len(after)

    def body(*refs):
        srcs, lands = refs[:n_in], refs[n_in:n_in + n_out]
        first_out = n_in + n_out + n_after
        send_sems, recv_sems, token = refs[first_out], refs[first_out + 1], refs[-1]
        me = _coords()
        for k, (flip, ii, src_at, oi, dst_at) in enumerate(remote):
            peer = _peer(me, flip)
            pltpu.make_async_remote_copy(
                src_ref=_at(srcs[ii], src_at(me, peer)), dst_ref=_at(lands[oi], dst_at(me)), send_sem=send_sems.at[k],
                recv_sem=recv_sems.at[k], device_id=peer, device_id_type=MESH).start()
        token[...] = jnp.zeros_like(token)

    lands = [lax.empty(s.shape, s.dtype) for s in land_shapes]
    bufs = list(arrays) + lands
    out = pl.pallas_call(
        body, name=name,
        out_shape=(pltpu.SemaphoreType.DMA((nk,)), pltpu.SemaphoreType.DMA((nk,)),
                   *[pltpu.HBM(b.shape, b.dtype) for b in bufs], jax.ShapeDtypeStruct((8, 128), F32)),
        in_specs=[HBM_SPEC] * len(bufs) + [pl.BlockSpec(memory_space=pl.ANY)] * n_after,
        out_specs=(SEM_SPEC, SEM_SPEC, *[HBM_SPEC] * len(bufs), pl.BlockSpec(memory_space=pltpu.VMEM)),
        input_output_aliases={a: 2 + a for a in range(len(bufs))},
        compiler_params=pltpu.CompilerParams(has_side_effects=DATAFLOW),
    )(*[pltpu.with_memory_space_constraint(b, pltpu.HBM) for b in bufs], *after)
    flight = (out[0], out[1], list(out[2:2 + n_in]), list(out[2 + n_in:2 + n_in + n_out]), remote)
    return flight, out[-1]


def exchange_wait(flight, after, name):
    send_sems, recv_sems, arrays, lands, remote = flight
    n_in, n_out = len(arrays), len(lands)
    after = list(after) if isinstance(after, (list, tuple)) else [after]

    def body(*refs):
        srcs, lnds = refs[:n_in], refs[n_in:n_in + n_out]
        s_sems, r_sems = refs[n_in + n_out], refs[n_in + n_out + 1]
        me = _coords()
        for k, (flip, ii, src_at, oi, dst_at) in enumerate(remote):
            peer = _peer(me, flip)
            copy = pltpu.make_async_remote_copy(
                src_ref=_at(srcs[ii], src_at(me, peer)), dst_ref=_at(lnds[oi], dst_at(peer)), send_sem=s_sems.at[k],
                recv_sem=r_sems.at[k], device_id=peer, device_id_type=MESH)
            copy.wait_send()
            copy.wait_recv()

    bufs = list(arrays) + list(lands)
    out = pl.pallas_call(
        body, name=name,
        out_shape=tuple(pltpu.HBM(b.shape, b.dtype) for b in bufs),
        in_specs=[HBM_SPEC] * len(bufs) + [SEM_SPEC, SEM_SPEC] + [pl.BlockSpec(memory_space=pl.ANY)] * len(after),
        out_specs=tuple([HBM_SPEC] * len(bufs)),
        input_output_aliases={a: a for a in range(len(bufs))},
        compiler_params=pltpu.CompilerParams(has_side_effects=DATAFLOW),
    )(*bufs, send_sems, recv_sems, *after)
    return list(out[:n_in]), list(out[n_in:])


def _half_tile(h, cd):
    return h if h * cd * 4 <= (1 << 20) else math.gcd(512, h)


def pair_add(g, got, core, out_dtype, name):
    _, _, h, cd = g.shape
    th = _half_tile(h, cd)

    def body(c_ref, g_ref, got_ref, o_ref):
        o_ref[0] = (g_ref[0, 0] + got_ref[0]).astype(o_ref.dtype)

    return pl.pallas_call(
        body, out_shape=jax.ShapeDtypeStruct((N_CHIP, h, cd), out_dtype),
        grid_spec=pltpu.PrefetchScalarGridSpec(
            num_scalar_prefetch=1, grid=(N_CHIP, h // th),
            in_specs=[pl.BlockSpec((1, 1, th, cd), lambda q, i, c: (q, c[0], i, 0)),
                      pl.BlockSpec((1, th, cd), lambda q, i, c: (q, i, 0))],
            out_specs=pl.BlockSpec((1, th, cd), lambda q, i, c: (q, i, 0))),
        compiler_params=_params(("parallel", "parallel"), VMEM_LIMIT), name=name)(core, g, got)


def sum_chips(parts, sums, place, name):
    _, h, cd = parts.shape
    th = _half_tile(h, cd)

    def body(pc_ref, p_ref, own_ref, o_ref):
        acc = None
        for q in range(N_CHIP):
            term = jnp.where(pc_ref[1] == q, own_ref[0], p_ref[q]).astype(F32)
            acc = term if acc is None else acc + term
        o_ref[0] = acc

    return pl.pallas_call(
        body, out_shape=jax.ShapeDtypeStruct((2, h, cd), F32),
        grid_spec=pltpu.PrefetchScalarGridSpec(
            num_scalar_prefetch=1, grid=(h // th,),
            in_specs=[pl.BlockSpec((N_CHIP, th, cd), lambda i, pc: (0, i, 0)),
                      pl.BlockSpec((1, th, cd), lambda i, pc: (pc[1], i, 0))],
            out_specs=pl.BlockSpec((1, th, cd), lambda i, pc: (pc[0], i, 0))),
        compiler_params=_params(("parallel",), VMEM_LIMIT), name=name)(place, parts, sums)


def to_segments(a, n_ctx):
    def one(p):
        n = p.shape[0]
        return p.reshape(N_SEG, n // N_SEG, -1).transpose(1, 0, 2).reshape(n, -1)
    return jnp.concatenate([one(a[:n_ctx]), one(a[n_ctx:])], axis=0) if n_ctx else one(a)


def from_segments(a, n_ctx):
    def one(p):
        n = p.shape[0]
        return p.reshape(n // N_SEG, N_SEG, -1).transpose(1, 0, 2).reshape(n, -1)
    return jnp.concatenate([one(a[:n_ctx]), one(a[n_ctx:])], axis=0) if n_ctx else one(a)


def rope_tables(n_ctx, n_lat):
    f32 = np.float32
    rows = n_lat // GRID_W
    row = np.repeat(np.arange(rows), GRID_W).astype(f32)
    col = np.tile(np.arange(GRID_W), rows).astype(f32)
    d = QK_ROPE_DIM // 2
    inv = (f32(1.0) / np.power(f32(ROPE_THETA), np.arange(0, d, 2, dtype=f32) / f32(d))).astype(f32)
    ang = np.concatenate([row[:, None] * inv[None, :], col[:, None] * inv[None, :]], axis=1).astype(f32)
    cos = np.concatenate([np.ones((n_ctx, d), f32), np.cos(ang)], axis=0)
    sin = np.concatenate([np.zeros((n_ctx, d), f32), np.sin(ang)], axis=0)
    q = QK_ROPE_DIM // 4
    T = n_ctx + n_lat
    ones, zeros = np.ones((T, QK_NOPE_DIM), f32), np.zeros((T, QK_NOPE_DIM), f32)
    tail, z8 = np.zeros((T, HEAD_LANES - QK_DIM), f32), np.zeros((T, q), f32)
    cr, cc, sr, sc = cos[:, :q], cos[:, q:], sin[:, :q], sin[:, q:]
    cos_t = np.concatenate([ones, cr, cr, cc, cc, tail], axis=1)
    sin_next = np.concatenate([zeros, -sr, z8, -sc, z8, tail], axis=1)
    sin_prev = np.concatenate([zeros, z8, sr, z8, sc, tail], axis=1)
    return tuple(jnp.asarray(t, F32) for t in (cos_t, sin_next, sin_prev))


def pad_heads(w, used):
    k = w.shape[0]
    return jnp.pad(w.reshape(k, MLA_HEADS, used), ((0, 0), (0, 0), (0, HEAD_LANES - used))).reshape(k, -1)


def unpad_heads(w, used):
    k = w.shape[0]
    return w.reshape(k, MLA_HEADS, HEAD_LANES)[:, :, :used].reshape(k, MLA_HEADS * used)


def rotary_spread():
    lane = np.arange(MLA_HEADS * HEAD_LANES) % HEAD_LANES
    return jnp.asarray(lane[None, :] == (QK_NOPE_DIM + np.arange(QK_ROPE_DIM))[:, None], BF16)


def s5_discretise(a_re, a_im, log_step, b_re, b_im):
    dt = jnp.exp(log_step)[:, None]
    mag = jnp.exp(a_re * dt)
    lb_re = mag * jnp.cos(a_im * dt)
    lb_im = mag * jnp.sin(a_im * dt)
    den = a_re * a_re + a_im * a_im
    nr = lb_re - 1.0
    f_re = ((nr * a_re + lb_im * a_im) / den)[..., None]
    f_im = ((lb_im * a_re - nr * a_im) / den)[..., None]
    return lb_re, lb_im, f_re * b_re - f_im * b_im, f_re * b_im + f_im * b_re


def s5_block_weights(lb_re, lb_im, bb_re, bb_im, c_re, c_im):
    eye = jnp.eye(GROUPS_PER_BLOCK, dtype=F32)
    lam = jnp.stack([lb_re.reshape(1, S5_LANES), lb_im.reshape(1, S5_LANES)])

    def b_blocks(bb):
        t = bb.reshape(N_BLOCKS, GROUPS_PER_BLOCK, S5_STATE, S5_GROUP)
        return jnp.einsum("bgpc,gh->bgchp", t, eye).reshape(N_BLOCKS, BLK_CH, BLK_ST).astype(BF16)

    def c_blocks(cc):
        t = cc.reshape(N_BLOCKS, GROUPS_PER_BLOCK, S5_GROUP, S5_STATE)
        return jnp.einsum("bgcp,gh->bgphc", t, eye).reshape(N_BLOCKS, BLK_ST, BLK_CH).astype(BF16)

    return lam, b_blocks(bb_re), b_blocks(bb_im), c_blocks(c_re), c_blocks(c_im)


def b_block_diag(db):
    t = db.reshape(N_BLOCKS, GROUPS_PER_BLOCK, S5_GROUP, GROUPS_PER_BLOCK, S5_STATE)
    return jnp.einsum("bgchp,gh->bgpc", t, jnp.eye(GROUPS_PER_BLOCK, dtype=F32)).reshape(S5_GROUPS, S5_STATE, S5_GROUP)


def c_block_diag(dc):
    t = dc.reshape(N_BLOCKS, GROUPS_PER_BLOCK, S5_STATE, GROUPS_PER_BLOCK, S5_GROUP)
    return jnp.einsum("bgphc,gh->bgcp", t, jnp.eye(GROUPS_PER_BLOCK, dtype=F32)).reshape(S5_GROUPS, S5_GROUP, S5_STATE)


def conj(a):
    return jnp.stack([a[0], -a[1]])


PACK_TILE = 16 * 128


def pack_flat(parts, dtype):
    flat = [p.reshape(-1).astype(dtype) for p in parts]
    sizes = [f.shape[0] for f in flat]
    total = sum(sizes)
    pad = (-total) % PACK_TILE
    if pad:
        flat.append(jnp.zeros((pad,), dtype))
    offs = np.cumsum([0] + sizes)[:-1].tolist()
    return jnp.concatenate(flat).reshape(-1, 128), offs


def unpack_flat(buf, offs, shapes):
    flat = buf.reshape(-1)
    return [flat[o:o + int(np.prod(s))].reshape(s) for o, s in zip(offs, shapes)]


def s5_forward(p1, n_ctx, dirs):
    saved = []
    y = None
    ctx_rows, lat_rows = (0, n_ctx), (n_ctx, p1.shape[0] - n_ctx)
    zeros_tile = jnp.zeros((2, N_SEG, S5_LANES), F32)
    zeros_row = jnp.zeros((2, 1, S5_LANES), F32)
    for k, (lam, b_re, b_im, c_re, c_im) in enumerate(dirs):
        rev = k == 1
        last = 0 if rev else N_SEG - 1
        _, _, fin = s5_scan(p1, b_re, b_im, lam, zeros_tile, reverse=rev, rows=ctx_rows, name=f"s5_ctx_finals{k}")
        carry_c = s5_chain(fin, zeros_row, lam, n_ctx // N_SEG, rev, name=f"s5_ctx_chain{k}")
        _, ck_c, fin_c = s5_scan(p1, b_re, b_im, lam, carry_c, reverse=rev, want_ckpt=True, rows=ctx_rows,
                                 name=f"s5_ctx_scan{k}")
        s0 = fin_c[:, last:last + 1, :]
        _, _, fin = s5_scan(p1, b_re, b_im, lam, zeros_tile, reverse=rev, rows=lat_rows, name=f"s5_lat_finals{k}")
        carry_l = s5_chain(fin, s0, lam, lat_rows[1] // N_SEG, rev, name=f"s5_lat_chain{k}")
        y, ck_l, _ = s5_scan(p1, b_re, b_im, lam, carry_l, reverse=rev, c_re=c_re, c_im=c_im, add=y,
                             want_ckpt=True, rows=lat_rows, name=f"s5_lat_scan{k}")
        saved.append((ck_c, ck_l))
    return y, saved


def s5_backward(dy_l, du_extra_l, p1, n_ctx, dirs, saved):
    n_lat = p1.shape[0] - n_ctx
    zeros_tile = jnp.zeros((2, N_SEG, S5_LANES), F32)
    zeros_row = jnp.zeros((2, 1, S5_LANES), F32)
    dy_c = jnp.zeros((n_ctx, D_MODEL), F32)
    du_l, du_c = du_extra_l, None
    grads = []
    for k, (lam, b_re, b_im, c_re, c_im) in enumerate(dirs):
        rev = k == 1
        lam_c = conj(lam)
        ck_c, ck_l = saved[k]
        first = N_SEG - 1 if rev else 0
        _, _, fin = s5_scan(dy_l, c_re, c_im, lam_c, zeros_tile, reverse=not rev, adjoint=True,
                            name=f"s5_lat_adj_finals{k}")
        carry = s5_chain(fin, zeros_row, lam_c, n_lat // N_SEG, not rev, name=f"s5_lat_adj_chain{k}")
        du_l, dlam_l, dbr_l, dbi_l, dcr_l, dci_l, fin_a = s5_grads(
            dy_l, p1, ck_l, b_re, b_im, c_re, c_im, lam, carry, reverse=rev, add=du_l, u_off=n_ctx,
            name=f"s5_lat_grads{k}")
        g0 = fin_a[:, first:first + 1, :]
        carry = s5_chain(zeros_tile, g0, lam_c, n_ctx // N_SEG, not rev, name=f"s5_ctx_adj_chain{k}")
        du_c, dlam_c, dbr_c, dbi_c, _, _, _ = s5_grads(
            dy_c, p1, ck_c, b_re, b_im, c_re, c_im, lam, carry, reverse=rev, add=du_c, name=f"s5_ctx_grads{k}")
        dlam = jnp.sum(dlam_l + dlam_c, axis=1)
        grads.append((dlam, b_block_diag(dbr_l + dbr_c), b_block_diag(dbi_l + dbi_c),
                      c_block_diag(dcr_l), c_block_diag(dci_l)))
    return jnp.concatenate([du_c, du_l], axis=0), grads


def local_step(x, ctx, target, mod, w, late=None, reducer=None):
    L, Lc = x.shape[0], ctx.shape[0]
    T = L + Lc
    assert L % Lc == 0 and Lc % (2 * N_SEG) == 0 and L % GRID_W == 0
    D = D_MODEL
    X0 = jnp.concatenate([ctx, x], axis=0)

    def mod_of(i, j):
        return mod[i, :, j, :][:, None, :]

    def vec(v):
        return v.reshape(1, 1, -1).astype(F32)

    g0 = vec(w["norm_g"][0])
    H0, p0 = norm_proj(X0, g0, mod_of(0, 1), mod_of(0, 0), w["mla_w_in"], Lc, "l0_norm_in")
    cq = Rows(p0, Q_LORA_RANK, col_blk=D // Q_LORA_RANK)
    ckv = Rows(p0, KV_LORA_RANK, col_blk=(D + Q_LORA_RANK) // KV_LORA_RANK)
    kr = p0[:, D + Q_LORA_RANK + KV_LORA_RANK:D + P0_HEAD]
    qng, kvng = vec(w["mla_q_norm"]), vec(w["mla_kv_norm"])
    tabs = rope_tables(Lc, L)
    spread = rotary_spread()
    if late is not None:
        w = {**w, **late["qkv"](p0)}
    w_uq_p = pad_heads(w["mla_w_uq"], QK_DIM)
    w_ukv3 = w["mla_w_ukv"].reshape(KV_LORA_RANK, MLA_HEADS, QK_NOPE_DIM + V_HEAD_DIM)
    w_kn_p = pad_heads(w_ukv3[:, :, :QK_NOPE_DIM].reshape(KV_LORA_RANK, -1), QK_NOPE_DIM)
    w_v = w_ukv3[:, :, QK_NOPE_DIM:].reshape(KV_LORA_RANK, -1)
    qb, qn = q_heads(cq, qng, w_uq_p, tabs)
    kb, vb, kvn = kv_heads(ckv, kvng, w_kn_p, w_v, kr, spread, tabs)
    o, lse, qs = attn_fwd(qb, kb, vb, Lc)
    if late is not None:
        w = {**w, **late["out"](o)}
    X1, og, out0 = mla_post_fwd(o, p0, X0, mod_of(0, 2), w["mla_w_out"], Lc)

    if late is not None:
        w = {**w, **late["l1"](X1)}
    X1p = to_segments(X1, Lc)
    tgt_p = to_segments(target, 0)
    g1 = vec(w["norm_g"][1])
    H1, p1 = norm_proj(X1p, g1, mod_of(1, 1), mod_of(1, 0), w["s5_w_in"], Lc, "l1_norm_in")
    disc_fn = lambda *a: tuple(zip(*[s5_discretise(a[0][k], a[1][k], a[2][k], a[3][k], a[4][k]) for k in range(2)]))
    disc, disc_vjp = jax.vjp(disc_fn, w["s5_a_re"], w["s5_a_im"], w["s5_log_step"], w["s5_b_re"], w["s5_b_im"])
    dirs = [s5_block_weights(disc[0][k], disc[1][k], disc[2][k], disc[3][k], w["s5_c_re"][k], w["s5_c_im"][k])
            for k in range(2)]
    y_ssm, s5_saved = s5_forward(p1, Lc, dirs)

    row = lambda v: v.reshape(1, D).astype(F32)
    (lvec, dX2, d_yssm, d_u_act, d_z1, d_fg, d_gt1, d_bg, d_d, gw_glu, gw_out) = s5_tail(
        y_ssm, p1, X1p, tgt_p, Lc, row(w["s5_d"]), row(w["s5_b_glu"]), mod[1, 1:2, 2, :], row(w["final_g"]),
        w["s5_w_glu"], w["s5_w_out"])
    loss = jnp.sum(lvec)
    gw = {"final_g": d_fg.reshape(D), "s5_b_glu": d_bg.reshape(D), "s5_d": d_d.reshape(D),
          "s5_w_glu": gw_glu, "s5_w_out": gw_out}
    dmod = {}

    du_p, s5_g = s5_backward(d_yssm, d_u_act, p1, Lc, dirs, s5_saved)
    d_disc = tuple(tuple(s5_g[k][j - 1].reshape(disc[j][k].shape) if j >= 2 else
                         s5_g[k][0][j].reshape(disc[j][k].shape) for k in range(2)) for j in range(4))
    gw["s5_a_re"], gw["s5_a_im"], gw["s5_log_step"], gw["s5_b_re"], gw["s5_b_im"] = disc_vjp(d_disc)
    gw["s5_c_re"] = jnp.stack([s5_g[0][3], s5_g[1][3]])
    gw["s5_c_im"] = jnp.stack([s5_g[0][4], s5_g[1][4]])
    gw["s5_w_in"] = jnp.concatenate([mm_tn(H1, du_p, name="l1_in_dw_u"), mm_tn(H1[Lc:], d_z1, name="l1_in_dw_z")],
                                    axis=1)
    if reducer is not None:
        g1 = g1 + reducer["l1"][0]({n: gw.pop(n) for n in LAYER1_MATS})[0, 0]
    d_X1p, d_g1, d_sc1, d_sh1 = norm_proj_bwd([(du_p, 0, 0), (d_z1, D, Lc)], w["s5_w_in"], X1p, g1, mod_of(1, 1),
                                              mod_of(1, 0), dX2, Lc, "l1_norm_in_bwd")
    d_gt1_full = jnp.concatenate([jnp.zeros((1, 1, D), F32), d_gt1[None]], axis=0)
    dmod[1] = (d_sh1, d_sc1, d_gt1_full)
    d_X1 = from_segments(d_X1p, Lc)

    d_o, d_z0, d_gt0, gw["mla_w_out"] = mla_post_bwd(d_X1, out0, og, o, p0, mod_of(0, 2), w["mla_w_out"], Lc)
    if reducer is not None:
        started = reducer["l1"][1](d_o)[0, 0] + reducer["out"][0]({"mla_w_out": gw.pop("mla_w_out")})[0, 0]
        tabs = (tabs[0] + started,) + tabs[1:]
    d_q, dos = attn_bwd_dq(qb, kb, vb, o, d_o, lse, tabs, Lc)
    dk_p, d_v = attn_bwd_dkv(qs, kb, vb, dos, Lc)
    if reducer is not None:
        tabs = (tabs[0] + reducer["out"][1](d_q)[0, 0],) + tabs[1:]
    d_k, d_kr = heads_unrope(dk_p, tabs, jnp.pad(spread, ((0, HEAD_LANES - QK_ROPE_DIM), (0, 0))),
                             name="l0_k_unrope")
    d_qn = mm_nt(d_q, w_uq_p, name="l0_uq_dx")
    gw["mla_w_uq"] = unpad_heads(mm_tn(qn, d_q, name="l0_uq_dw"), QK_DIM)
    d_kvn = mm_nt(d_k, w_kn_p, name="l0_ukn_dx") + mm_nt(d_v, w_v, name="l0_uv_dx")
    dw_kn = unpad_heads(mm_tn(kvn, d_k, name="l0_ukn_dw"), QK_NOPE_DIM).reshape(KV_LORA_RANK, MLA_HEADS, QK_NOPE_DIM)
    dw_v = mm_tn(kvn, d_v, name="l0_uv_dw").reshape(KV_LORA_RANK, MLA_HEADS, V_HEAD_DIM)
    gw["mla_w_ukv"] = jnp.concatenate([dw_kn, dw_v], axis=-1).reshape(KV_LORA_RANK, -1)
    d_cq, d_qng = rowwise_bwd(f_rms, [cq], [qng], [d_qn], [0], [0], T, 0, "l0_qnorm_bwd")
    d_ckv, d_kvng = rowwise_bwd(f_rms, [ckv], [kvng], [d_kvn], [0], [0], T, 0, "l0_kvnorm_bwd")
    gw["mla_q_norm"] = d_qng.reshape(-1)
    gw["mla_kv_norm"] = d_kvng.reshape(-1)
    o_cq, o_ckv = D, D + Q_LORA_RANK
    o_kr = o_ckv + KV_LORA_RANK
    d_head = jnp.concatenate([d_cq, d_ckv, d_kr], axis=1)
    gw["mla_w_in"] = jnp.concatenate([mm_tn(H0, d_head, name="l0_in_dw_head")[:, :P0_HEAD],
                                      mm_tn(H0, d_z0, name="l0_in_dw_z")], axis=1)
    dx, d_g0, d_sc0, d_sh0 = norm_proj_bwd(
        [(d_z0, 0, 0), (d_cq, o_cq, 0), (d_ckv, o_ckv, 0), (d_kr, o_kr, 0)], w["mla_w_in"], X0, g0, mod_of(0, 1),
        mod_of(0, 0), d_X1, Lc, "l0_norm_in_bwd", latent_dx_only=True)
    dmod[0] = (d_sh0, d_sc0, d_gt0)
    gw["norm_g"] = jnp.stack([d_g0.reshape(D), d_g1.reshape(D)])
    dmod_arr = jnp.stack([jnp.stack([dmod[i][j][:, 0, :] for j in range(3)], axis=1) for i in range(2)])
    ready = {**reducer["l1"][2](dx), **reducer["out"][2](dx)} if reducer is not None else {}
    return loss, dx, dmod_arr, gw, ready


SHARDED = {
    "mla_w_in": 1, "mla_w_uq": 1, "mla_w_ukv": 1, "mla_w_out": 0,
    "s5_w_in": 1, "s5_w_glu": 0, "s5_w_out": 0, "s5_d": 0, "s5_b_glu": 0,
}
SHARDED_MATS = ["mla_w_in", "mla_w_uq", "mla_w_ukv", "mla_w_out", "s5_w_in", "s5_w_glu", "s5_w_out"]
SHARDED_VECS = ["s5_d", "s5_b_glu"]
REPLICATED = ["norm_g", "mla_q_norm", "mla_kv_norm", "s5_a_re", "s5_a_im", "s5_log_step", "s5_b_re", "s5_b_im",
              "s5_c_re", "s5_c_im", "final_g"]
WEIGHT_ORDER = ["c_ctx", "ada_w", "ada_b", "norm_g", "mla_w_in", "mla_q_norm", "mla_w_uq", "mla_kv_norm", "mla_w_ukv",
                "mla_w_out", "s5_w_in", "s5_a_re", "s5_a_im", "s5_log_step", "s5_b_re", "s5_b_im", "s5_c_re", "s5_c_im",
                "s5_d", "s5_w_glu", "s5_b_glu", "s5_w_out", "final_g"]


P0_HEAD = Q_LORA_RANK + KV_LORA_RANK + QK_ROPE_DIM


P0_WIDTH = 1536


def w_in_to_kernel_order(w):
    pad = jnp.zeros((w.shape[0], P0_WIDTH - w.shape[1]), w.dtype)
    return jnp.concatenate([w[:, P0_HEAD:], w[:, :P0_HEAD], pad], axis=1)


LAYER0_MATS = ["mla_w_in", "mla_w_uq", "mla_w_ukv", "mla_w_out"]
LAYER1_MATS = ["s5_w_in", "s5_w_glu", "s5_w_out"]


def _whole_matrices(names, own_blocks, gathered):
    chip = _chip_index(_coords())
    full = {}
    for n, own, o in zip(names, own_blocks, gathered):
        slot = lax.broadcasted_iota(jnp.int32, (N_CHIP, 1, 1), 0)
        o = jnp.where(slot == chip, own[None], o.reshape((N_CHIP,) + own.shape))
        full[n] = o.reshape(-1, o.shape[-1]) if SHARDED[n] == 0 else o.transpose(1, 0, 2).reshape(o.shape[1], -1)
    return full


FIRST_MATS = ["mla_w_in"]
LATER_GROUPS = {"qkv": ["mla_w_uq", "mla_w_ukv"], "out": ["mla_w_out"], "l1": LAYER1_MATS}


def gather_weights(ws):
    mats = [ws[n].astype(BF16) for n in FIRST_MATS]
    full = _whole_matrices(FIRST_MATS, mats, gather_halves(mats, "gather_weights"))
    full["mla_w_in"] = w_in_to_kernel_order(full["mla_w_in"])
    return full


def gather_weights_behind(ws, after):
    token, finish = 0.0, {}
    for group, names in LATER_GROUPS.items():
        mats = [ws[n].astype(BF16) for n in names]
        flight, tok = exchange_start(
            mats, [jax.ShapeDtypeStruct((N_CHIP,) + m.shape, m.dtype) for m in mats],
            [(f, a, lambda me, peer: None, a, lambda s: (_chip_index(s),))
             for a in range(len(mats)) for f in CHIP_FLIPS], f"gather_{group}_start", after=after)
        after = [tok]
        token = token + tok[0, 0]

        def finish_group(after_work, group=group, names=names, flight=flight):
            own, got = exchange_wait(flight, after_work, f"gather_{group}_wait")
            return _whole_matrices(names, own, got)

        finish[group] = finish_group
    return token, finish


def _grad_slots(gw, names):
    slots = []
    for n in names:
        g = gw[n]
        if SHARDED[n] == 0:
            slots.append(g.reshape(N_CHIP, 2, g.shape[0] // (2 * N_CHIP), g.shape[1]))
        else:
            k, n4 = g.shape
            slots.append(g.reshape(k, N_CHIP, n4 // N_CHIP).transpose(1, 0, 2)
                         .reshape(N_CHIP, 2, k // 2, n4 // N_CHIP))
    return slots


def _to_sibling_half(count):
    return [(CORE_FLIP, i, lambda me, peer: (slice(None), 1 - me[2]), i, lambda s: None) for i in range(count)]


def _to_chips(count):
    return [(f, i, lambda me, peer: (_chip_index(peer),), i, lambda s: (_chip_index(s),))
            for i in range(count) for f in CHIP_FLIPS]


def _place():
    me = _coords()
    return jnp.stack([me[2], _chip_index(me)]).astype(jnp.int32)


def reduce_behind(names, tag):
    state = {}
    count = len(names)

    def begin(gw):
        slots = _grad_slots(gw, names)
        lands = [jax.ShapeDtypeStruct((N_CHIP,) + s.shape[2:], F32) for s in slots]
        state["in"], token = exchange_start(slots, lands, _to_sibling_half(count), f"grads_{tag}_swap_in_start")
        return token

    def middle(after):
        slots, got = exchange_wait(state["in"], after, f"grads_{tag}_swap_in_wait")
        place = _place()
        sums = [pair_add(s, g, place[:1], BF16, f"grads_pair_{n}") for n, s, g in zip(names, slots, got)]
        lands = [jax.ShapeDtypeStruct(s.shape, s.dtype) for s in sums]
        state["out"], token = exchange_start(sums, lands, _to_chips(count), f"grads_{tag}_scatter_start")
        return token

    def end(after):
        sums, parts = exchange_wait(state["out"], after, f"grads_{tag}_scatter_wait")
        place = _place()
        return {n: sum_chips(p, s, place, f"grads_sum_{n}") for n, p, s in zip(names, parts, sums)}

    return begin, middle, end


def reduce_gradients(gw, ready_halves):
    me = _coords()
    place = _place()
    mat_names = [n for n in SHARDED_MATS if n not in ready_halves]
    slots = dict(zip(mat_names, _grad_slots(gw, mat_names)))
    small_names = REPLICATED + SHARDED_VECS
    small, small_offs = pack_flat([gw[n].astype(F32) for n in small_names], F32)
    small = jnp.pad(small, ((0, (-small.shape[0]) % (N_CHIP * 32)), (0, 0)))
    slots["small"] = small.reshape(N_CHIP, 2, -1, 128)
    names = list(slots)
    count = len(names)
    got = exchange([slots[n] for n in names],
                   [jax.ShapeDtypeStruct((N_CHIP,) + slots[n].shape[2:], F32) for n in names],
                   _to_sibling_half(count), [], "grads_swap_in")
    sums = [pair_add(slots[n], g, place[:1], F32 if n == "small" else BF16, f"grads_pair_{n}")
            for n, g in zip(names, got)]
    parts = exchange(sums, [jax.ShapeDtypeStruct(s.shape, s.dtype) for s in sums], _to_chips(count), [],
                     "grads_scatter")
    halves = {n: sum_chips(p, s, place, f"grads_sum_{n}") for n, p, s in zip(names, parts, sums)}
    halves.update(ready_halves)
    all_names = list(halves)
    fulls = exchange(
        [halves[n] for n in all_names], [jax.ShapeDtypeStruct(halves[n].shape, F32) for n in all_names],
        [(CORE_FLIP, i, lambda me, peer: (me[2],), i, lambda s: (s[2],)) for i in range(len(all_names))], [],
        "grads_swap_out", aliases={i: i for i in range(len(all_names))})
    out = {n: f.reshape(-1, f.shape[-1]) for n, f in zip(all_names, fulls)}
    quarter = out.pop("small")
    gather, token = exchange_start(
        [quarter], [jax.ShapeDtypeStruct((N_CHIP,) + quarter.shape, F32)],
        [(f, 0, lambda me, peer: None, 0, lambda s: (_chip_index(s),)) for f in CHIP_FLIPS],
        "grads_gather_small_start")

    def finish_small(after):
        (own,), (got_small,) = exchange_wait(gather, after, "grads_gather_small_wait")
        slot = lax.broadcasted_iota(jnp.int32, (N_CHIP, 1, 1), 0)
        small_all = jnp.where(slot == _chip_index(me), own[None], got_small)
        vals = unpack_flat(small_all, small_offs, [gw[n].shape for n in small_names])
        res = {}
        for n, v in zip(small_names, vals):
            if n in SHARDED_VECS:
                size = v.shape[0] // N_CHIP
                v = lax.dynamic_slice_in_dim(v, _chip_index(me) * size, size)
            res[n] = v
        return res

    return out, finish_small, token


def kernel(x, c, ctx, c_ctx, ada_w, ada_b, norm_g, mla_w_in, mla_q_norm, mla_w_uq, mla_kv_norm, mla_w_ukv, mla_w_out, s5_w_in, s5_a_re, s5_a_im, s5_log_step, s5_b_re, s5_b_im, s5_c_re, s5_c_im, s5_d, s5_w_glu, s5_b_glu, s5_w_out, final_g, loss_target, m_c_ctx, m_ada_w, m_ada_b, m_norm_g, m_mla_w_in, m_mla_q_norm, m_mla_w_uq, m_mla_kv_norm, m_mla_w_ukv, m_mla_w_out, m_s5_w_in, m_s5_a_re, m_s5_a_im, m_s5_log_step, m_s5_b_re, m_s5_b_im, m_s5_c_re, m_s5_c_im, m_s5_d, m_s5_w_glu, m_s5_b_glu, m_s5_w_out, m_final_g, v_c_ctx, v_ada_w, v_ada_b, v_norm_g, v_mla_w_in, v_mla_q_norm, v_mla_w_uq, v_mla_kv_norm, v_mla_w_ukv, v_mla_w_out, v_s5_w_in, v_s5_a_re, v_s5_a_im, v_s5_log_step, v_s5_b_re, v_s5_b_im, v_s5_c_re, v_s5_c_im, v_s5_d, v_s5_w_glu, v_s5_b_glu, v_s5_w_out, v_final_g):
    args = dict(locals())
    weights = {n: args[n] for n in WEIGHT_ORDER}
    D = D_MODEL
    xi, yi, ci = _coords()
    chip = 2 * xi + yi
    me = 4 * xi + 2 * yi + ci
    n_col = ada_w.shape[2]

    c_all = allgather_devices(jnp.pad(c, ((0, 7), (0, 0))), "gather_c")[:, 0, :]
    cond = jnp.concatenate([c_all, jnp.broadcast_to(c_ctx[None], (8, D))], axis=0)
    (s_cond,) = rowwise_fwd(lambda v: (_silu(v),), [cond], [], [D], [F32], 16, 0, "cond_silu")
    ada_rows = ada_w.reshape(2 * D, n_col)
    mod_cols = jnp.stack([mm_nn(s_cond, ada_rows, name=f"mod_proj{i}", b_blk=i) for i in range(2)])
    vec_tiles = [jnp.pad(weights[n][0].reshape(-1, 128), ((0, 6), (0, 0))) for n in SHARDED_VECS]
    mod_all, *vec_all = allgather_chips([mod_cols] + vec_tiles, "gather_mod")
    mod_all = mod_all.transpose(1, 2, 0, 3).reshape(2, 16, 3 * D) + ada_b[:, None, :]
    mine = lax.broadcasted_iota(jnp.int32, (1, 16, 1), 1) == me
    mod_l = jnp.sum(jnp.where(mine, mod_all, 0.0), axis=1)
    mod_c = mod_all[:, 8, :]
    mod = jnp.stack([mod_c.reshape(2, 3, D), mod_l.reshape(2, 3, D)], axis=1)

    w = gather_weights({n: weights[n][0] for n in FIRST_MATS})
    token, late = gather_weights_behind({n: weights[n][0] for n in SHARDED_MATS}, [w["mla_w_in"], mod])
    for n, v in zip(SHARDED_VECS, vec_all):
        w[n] = v[:, :2, :].reshape(-1)
    for n in ["norm_g", "final_g"]:
        w[n] = weights[n]
    for n in ["mla_q_norm", "mla_kv_norm", "s5_a_re", "s5_a_im", "s5_log_step", "s5_b_re", "s5_b_im",
              "s5_c_re", "s5_c_im"]:
        w[n] = weights[n][0]

    reducer = {"l1": reduce_behind(LAYER1_MATS, "l1"), "out": reduce_behind(["mla_w_out"], "out")}
    loss_me, dx, dmod, gw, ready = local_step(x[0], ctx[0], loss_target[0], mod + token, w, late,
                                              reducer)

    dmod_rows, loss_all = _gather([dmod.reshape(2, 2, 3 * D), jnp.broadcast_to(loss_me, (8, 128))],
                                  ALL_FLIPS, _dev_index, N_DEV, "gather_dmod")
    loss = functools.reduce(lambda s, d: s + loss_all[d, 0, 0], range(1, N_DEV), loss_all[0, 0, 0])
    dm = jnp.concatenate([dmod_rows[:, :, 1, :], dmod_rows[:, :, 0, :]], axis=0).transpose(1, 0, 2)
    g_ada_b = jnp.sum(dm, axis=1)
    dm_cols = lax.dynamic_slice_in_dim(dm, chip * n_col, n_col, axis=2)
    g_ada_w = jnp.stack([mm_tn(s_cond, dm_cols[i], name=f"mod_proj_dw{i}") for i in range(2)])
    dmc = jnp.sum(dm_cols[:, 8:, :], axis=1)
    dmc8 = jnp.broadcast_to(dmc[:, None, :], (2, 8, n_col))
    g_sc = (mm_nt(dmc8[0], ada_rows, name="mod_proj_dx0", b_rows=D, b_blk=0)[0]
            + mm_nt(dmc8[1], ada_rows, name="mod_proj_dx1", b_rows=D, b_blk=1)[0])
    g_sc_all = allgather_devices(jnp.broadcast_to(g_sc[None], (8, D)), "gather_dcond")[:, 0, :]
    g_silu_cc = g_sc_all[0] + g_sc_all[2] + g_sc_all[4] + g_sc_all[6]
    (g_c_ctx,) = rowwise_bwd(lambda v: (_silu(v),), [jnp.broadcast_to(c_ctx[None], (8, D))], [],
                             [jnp.broadcast_to(g_silu_cc[None], (8, D))], [0], [], 8, 0, "cond_silu_bwd")
    g_c_ctx = g_c_ctx[0]

    grads = {"c_ctx": g_c_ctx, "ada_w": g_ada_w, "ada_b": g_ada_b}
    deltas, new_m, new_v = {}, {}, {}
    small = [n for n in WEIGHT_ORDER if weights[n].size < 50000]

    def update(n, after=None):
        shp = weights[n].shape
        w2 = weights[n].reshape(-1, shp[-1])
        d_, m_, v_ = adamw(w2, grads[n].reshape(w2.shape), args["m_" + n].reshape(w2.shape),
                           args["v_" + n].reshape(w2.shape), name=f"adamw_{n}", after=after)
        deltas[n], new_m[n], new_v[n] = d_.reshape(shp), m_.reshape(shp), v_.reshape(shp)

    red, finish_small, small_started = reduce_gradients(gw, ready)
    update("ada_w", small_started)
    for n in SHARDED_MATS:
        grads[n] = red[n].reshape(weights[n].shape)
        update(n, small_started)
    red_small = finish_small([deltas[n] for n in ["ada_w"] + SHARDED_MATS])
    for n in REPLICATED + SHARDED_VECS:
        grads[n] = red_small[n].reshape(weights[n].shape)
    for n in WEIGHT_ORDER:
        if n not in small and n not in deltas:
            update(n)
    packs = []
    offs = None
    for src in (weights, grads, {n: args["m_" + n] for n in small}, {n: args["v_" + n] for n in small}):
        buf, offs = pack_flat([src[n] for n in small], F32)
        packs.append(buf)
    outs = adamw(*packs, name="adamw_small")
    for res, dst in zip(outs, (deltas, new_m, new_v)):
        for n, val in zip(small, unpack_flat(res, offs, [weights[n].shape for n in small])):
            dst[n] = val

    return (loss, dx[None], *[grads[n] for n in WEIGHT_ORDER], *[deltas[n] for n in WEIGHT_ORDER],
            *[new_m[n] for n in WEIGHT_ORDER], *[new_v[n] for n in WEIGHT_ORDER])
```

```python
import functools
import math

import jax
import jax.numpy as jnp
import numpy as np
from jax import lax
from jax.experimental import pallas as pl
from jax.experimental.pallas import tpu as pltpu

F32 = jnp.float32
BF16 = jnp.bfloat16

D_MODEL = 1024
GRID_W = 64
EPS = 1e-6
MLA_HEADS = 16
QK_NOPE_DIM = 64
QK_ROPE_DIM = 32
V_HEAD_DIM = 64
Q_LORA_RANK = 256
KV_LORA_RANK = 128
QK_DIM = QK_NOPE_DIM + QK_ROPE_DIM
SOFTMAX_SCALE = QK_DIM ** -0.5
ROPE_THETA = 10000.0
S5_GROUP = 16
S5_GROUPS = D_MODEL // S5_GROUP
S5_STATE = 64
S5_LANES = S5_GROUPS * S5_STATE
N_SEG = 8
GROUPS_PER_BLOCK = 8
N_BLOCKS = S5_GROUPS // GROUPS_PER_BLOCK
BLK_CH = GROUPS_PER_BLOCK * S5_GROUP
BLK_ST = GROUPS_PER_BLOCK * S5_STATE

ADAM_LR = 0.001
ADAM_B1 = 0.9
ADAM_B2 = 0.999
ADAM_EPS = 1e-08
ADAM_WD = 0.01
ADAM_STEP = 10

N_DEV = 8
N_CHIP = 4
MESH = pl.DeviceIdType.MESH
VMEM_LIMIT = 52 * 1024 * 1024
ROW_TILE = 256


def _params(sem=None, vmem=None):
    return pltpu.CompilerParams(dimension_semantics=sem, vmem_limit_bytes=vmem)


def mm_nn(a, b, out_dtype=F32, name="mm_nn", b_blk=0):
    M, K = a.shape
    N = b.shape[1]
    tm = math.gcd(ROW_TILE, M)

    def body(a_ref, b_ref, o_ref):
        o_ref[...] = jnp.dot(a_ref[...].astype(BF16), b_ref[...].astype(BF16),
                             preferred_element_type=F32).astype(o_ref.dtype)

    return pl.pallas_call(
        body, out_shape=jax.ShapeDtypeStruct((M, N), out_dtype), grid=(M // tm,),
        in_specs=[pl.BlockSpec((tm, K), lambda i: (i, 0)), pl.BlockSpec((K, N), lambda i: (b_blk, 0))],
        out_specs=pl.BlockSpec((tm, N), lambda i: (i, 0)),
        compiler_params=_params(("parallel",), VMEM_LIMIT), name=name)(a, b)


def mm_nt(a, b, out_dtype=F32, name="mm_nt", b_rows=None, b_blk=0):
    M, N = a.shape
    K = b.shape[0] if b_rows is None else b_rows
    tm = math.gcd(ROW_TILE, M)

    def body(a_ref, b_ref, o_ref):
        o_ref[...] = lax.dot_general(a_ref[...].astype(BF16), b_ref[...].astype(BF16),
                                     (((1,), (1,)), ((), ())),
                                     preferred_element_type=F32).astype(o_ref.dtype)

    return pl.pallas_call(
        body, out_shape=jax.ShapeDtypeStruct((M, K), out_dtype), grid=(M // tm,),
        in_specs=[pl.BlockSpec((tm, N), lambda i: (i, 0)), pl.BlockSpec((K, N), lambda i: (b_blk, 0))],
        out_specs=pl.BlockSpec((tm, K), lambda i: (i, 0)),
        compiler_params=_params(("parallel",), VMEM_LIMIT), name=name)(a, b)


def mm_tn(a, b, name="mm_tn"):
    M, N = b.shape
    K = a.shape[1]
    skip = a.shape[0] - M
    tn = math.gcd(512, N) if N % 128 == 0 and N > 512 else N

    def body(a_ref, b_ref, o_ref):
        o_ref[...] = lax.dot_general(a_ref[skip:, :].astype(BF16), b_ref[...].astype(BF16),
                                     (((0,), (0,)), ((), ())), preferred_element_type=F32)

    return pl.pallas_call(
        body, out_shape=jax.ShapeDtypeStruct((K, N), F32), grid=(N // tn,),
        in_specs=[pl.BlockSpec(a.shape, lambda j: (0, 0)), pl.BlockSpec((M, tn), lambda j: (0, j))],
        out_specs=pl.BlockSpec((K, tn), lambda j: (0, j)),
        compiler_params=_params(("parallel",), VMEM_LIMIT), name=name)(a, b)


class Rows:
    def __init__(self, arr, width=None, row_off=0, col_blk=0):
        self.arr = arr
        self.width = arr.shape[1] if width is None else width
        self.row_off = row_off
        self.col_blk = col_blk

    def spec(self, tm):
        ro, cb = self.row_off // tm, self.col_blk
        return pl.BlockSpec((tm, self.width), lambda i: (i + ro, cb))


def _as_rows(x):
    return x if isinstance(x, Rows) else Rows(x)


def _row_tile(n_rows, n_ctx_rows, rows):
    tm = math.gcd(ROW_TILE, n_rows, n_ctx_rows)
    for r in rows:
        tm = math.gcd(tm, r.row_off)
    return tm


def _bc_spec(arr, n_ctx_blocks):
    g, _, d = arr.shape
    if g == 1:
        return pl.BlockSpec((1, 1, d), lambda i: (0, 0, 0))
    return pl.BlockSpec((1, 1, d), lambda i: ((i >= n_ctx_blocks).astype(jnp.int32), 0, 0))


def rowwise_fwd(fn, rows, bcs, out_dims, out_dtypes, n_rows, n_ctx_rows, name):
    rows = [_as_rows(r) for r in rows]
    tm = _row_tile(n_rows, n_ctx_rows, rows)
    ncb = n_ctx_rows // tm
    nr, nb = len(rows), len(bcs)

    def body(*refs):
        vals = [r[...].astype(F32) for r in refs[:nr]] + [b[0].astype(F32) for b in refs[nr:nr + nb]]
        outs = fn(*vals)
        for o_ref, v in zip(refs[nr + nb:], outs):
            o_ref[...] = v.astype(o_ref.dtype)

    outs = pl.pallas_call(
        body,
        out_shape=[jax.ShapeDtypeStruct((n_rows, d), dt) for d, dt in zip(out_dims, out_dtypes)],
        grid=(n_rows // tm,),
        in_specs=[r.spec(tm) for r in rows] + [_bc_spec(b, ncb) for b in bcs],
        out_specs=[pl.BlockSpec((tm, d), lambda i: (i, 0)) for d in out_dims],
        compiler_params=_params(("parallel",), VMEM_LIMIT), name=name)(*[r.arr for r in rows], *bcs)
    return outs


def rowwise_bwd(fn, rows, bcs, cts, diff_rows, diff_bcs, n_rows, n_ctx_rows, name, ct_extra=None, lat_add=None):
    rows = [_as_rows(r) for r in rows]
    cts = [_as_rows(c) for c in cts]
    extra = [_as_rows(ct_extra)] if ct_extra is not None else []
    tm = _row_tile(n_rows, n_ctx_rows, rows + cts + extra)
    ncb = n_ctx_rows // tm
    nr, nb, nc = len(rows), len(bcs), len(cts)
    ndr, ndb = len(diff_rows), len(diff_bcs)
    n_in = nr + nb + nc + len(extra) + (lat_add is not None)

    def body(*refs):
        i = pl.program_id(0)
        rvals = [r[...].astype(F32) for r in refs[:nr]]
        bvals = [b[0].astype(F32) for b in refs[nr:nr + nb]]
        cvals = [c[...].astype(F32) for c in refs[nr + nb:nr + nb + nc]]
        if extra:
            cvals[0] = cvals[0] + refs[nr + nb + nc][...].astype(F32)
        outs = refs[n_in:]

        def f(*d):
            rv, bv = list(rvals), list(bvals)
            for k, idx in enumerate(diff_rows):
                rv[idx] = d[k]
            for k, idx in enumerate(diff_bcs):
                bv[idx] = d[ndr + k]
            return tuple(fn(*rv, *bv))

        primals = [rvals[k] for k in diff_rows] + [bvals[k] for k in diff_bcs]
        _, vjp = jax.vjp(f, *primals)
        grads = list(vjp(tuple(cvals)))
        if lat_add is not None:
            add = refs[n_in - 1][...]
            grads[0] = grads[0] + (add if lat_add.shape[0] == n_rows else jnp.where(i >= ncb, add, 0.0))
        for k in range(ndr):
            outs[k][...] = grads[k].astype(outs[k].dtype)
        for k, idx in enumerate(diff_bcs):
            o_ref = outs[ndr + k]
            first = (i == 0)
            if bcs[idx].shape[0] == 2:
                first = first | (i == ncb)

            @pl.when(first)
            def _(o_ref=o_ref):
                o_ref[...] = jnp.zeros_like(o_ref)

            o_ref[0] += grads[ndr + k]

    out_shape = [jax.ShapeDtypeStruct((n_rows, rows[k].width), F32) for k in diff_rows]
    out_shape += [jax.ShapeDtypeStruct(bcs[k].shape, F32) for k in diff_bcs]
    out_specs = [pl.BlockSpec((tm, rows[k].width), lambda i: (i, 0)) for k in diff_rows]
    out_specs += [_bc_spec(bcs[k], ncb) for k in diff_bcs]
    ins = [r.arr for r in rows] + list(bcs) + [c.arr for c in cts + extra]
    in_specs = [r.spec(tm) for r in rows] + [_bc_spec(b, ncb) for b in bcs] + [c.spec(tm) for c in cts + extra]
    if lat_add is not None:
        ins.append(lat_add)
        skip = ncb if lat_add.shape[0] != n_rows else 0
        in_specs.append(pl.BlockSpec((tm, lat_add.shape[1]), lambda i: (jnp.maximum(i - skip, 0), 0)))
    outs = pl.pallas_call(
        body, out_shape=out_shape, grid=(n_rows // tm,), in_specs=in_specs, out_specs=out_specs,
        compiler_params=_params(("arbitrary",), VMEM_LIMIT), name=name)(*ins)
    return outs


def _rms(x):
    return x * lax.rsqrt(jnp.mean(x * x, axis=-1, keepdims=True) + EPS)


def _sigmoid(x):
    return 0.5 * (jnp.tanh(0.5 * x) + 1.0)


def _silu(x):
    return x * _sigmoid(x)


def _gelu_tanh(x):
    return 0.5 * x * (1.0 + jnp.tanh(math.sqrt(2.0 / math.pi) * (x + 0.044715 * (x * x * x))))


def f_norm_mod(x, g, sc, sh):
    return ((_rms(x) * g) * (1.0 + sc) + sh,)


def f_rms(x, g):
    return (_rms(x) * g,)


def f_gate(o, z):
    return (o * _silu(z),)


def f_s5_act(y, u, d):
    return (_gelu_tanh(y + d * u),)


def f_s5_glu(ya, gl, z, b):
    return (ya * _sigmoid(gl + b) * _silu(z),)


def _as_parts(x, n_ctx):
    xs = x if isinstance(x, tuple) else (x,)
    assert len(xs) == 1 or xs[0].shape[0] == n_ctx
    return xs, sum(p.shape[0] for p in xs)


def _parts_specs(xs, tm, ncb):
    d = xs[0].shape[1]
    if len(xs) == 1:
        return [pl.BlockSpec((tm, d), lambda i: (i, 0))]
    return [pl.BlockSpec((tm, d), lambda i: (jnp.minimum(i, ncb - 1), 0)),
            pl.BlockSpec((tm, d), lambda i: (jnp.maximum(i - ncb, 0), 0))]


def _parts_tile(x_refs, ncb):
    if len(x_refs) == 1:
        return x_refs[0][...]
    return jnp.where(pl.program_id(0) < ncb, x_refs[0][...], x_refs[1][...])


def norm_proj(x, g, sc, sh, w, n_ctx, name):
    xs, n = _as_parts(x, n_ctx)
    d = xs[0].shape[1]
    nw = w.shape[1]
    tm = math.gcd(ROW_TILE, n, n_ctx)
    ncb = n_ctx // tm
    nx = len(xs)

    def body(*refs):
        g_ref, sc_ref, sh_ref, w_ref, h_ref, p_ref = refs[nx:]
        h = f_norm_mod(_parts_tile(refs[:nx], ncb), g_ref[0], sc_ref[0], sh_ref[0])[0].astype(BF16)
        h_ref[...] = h
        p_ref[...] = jnp.dot(h, w_ref[...], preferred_element_type=F32)

    row = pl.BlockSpec((tm, d), lambda i: (i, 0))
    return pl.pallas_call(
        body, out_shape=[jax.ShapeDtypeStruct((n, d), BF16), jax.ShapeDtypeStruct((n, nw), F32)], grid=(n // tm,),
        in_specs=_parts_specs(xs, tm, ncb) + [_bc_spec(g, ncb), _bc_spec(sc, ncb), _bc_spec(sh, ncb),
                                              pl.BlockSpec(w.shape, lambda i: (0, 0))],
        out_specs=[row, pl.BlockSpec((tm, nw), lambda i: (i, 0))],
        compiler_params=_params(("parallel",), VMEM_LIMIT), name=name)(*xs, g, sc, sh, w)


def norm_proj_bwd(terms, w, x, g, sc, sh, add, n_ctx, name, latent_dx_only=False):
    xs, n = _as_parts(x, n_ctx)
    d = xs[0].shape[1]
    tm = math.gcd(ROW_TILE, n, n_ctx, *[t[2] for t in terms])
    ncb = n_ctx // tm
    nt = len(terms)
    nx = len(xs)
    add_skip = ncb if add.shape[0] != n else 0

    def body(*refs):
        i = pl.program_id(0)
        a_refs = refs[:nt]
        w_ref, x_refs = refs[nt], refs[nt + 1:nt + 1 + nx]
        g_ref, sc_ref, sh_ref, add_ref, dx_ref, dg_ref, dsc_ref, dsh_ref = refs[nt + 1 + nx:]
        d_h = None
        for a_ref, (a, off, first) in zip(a_refs, terms):
            part = lax.dot_general(a_ref[...].astype(BF16), w_ref[:, off:off + a.shape[1]], NT_DIMS,
                                   preferred_element_type=F32)
            if first:
                part = jnp.where(i >= first // tm, part, 0.0)
            d_h = part if d_h is None else d_h + part
        _, vjp = jax.vjp(lambda x_, g_, sc_, sh_: f_norm_mod(x_, g_, sc_, sh_), _parts_tile(x_refs, ncb), g_ref[0],
                         sc_ref[0], sh_ref[0])
        d_x, d_g, d_sc, d_sh = vjp((d_h,))
        extra = add_ref[...]
        dx_ref[...] = d_x + (extra if add_skip == 0 else jnp.where(i >= ncb, extra, 0.0))

        @pl.when(i == 0)
        def _():
            dg_ref[...] = jnp.zeros_like(dg_ref)

        @pl.when((i == 0) | (i == ncb))
        def _():
            dsc_ref[...] = jnp.zeros_like(dsc_ref)
            dsh_ref[...] = jnp.zeros_like(dsh_ref)

        dg_ref[0] += d_g
        dsc_ref[0] += d_sc
        dsh_ref[0] += d_sh

    def a_spec(a, first):
        skip = first // tm
        return pl.BlockSpec((tm, a.shape[1]), lambda i: (jnp.maximum(i - skip, 0), 0))

    dx_skip = ncb if latent_dx_only else 0
    return pl.pallas_call(
        body,
        out_shape=[jax.ShapeDtypeStruct((n - dx_skip * tm, d), F32), jax.ShapeDtypeStruct(g.shape, F32),
                   jax.ShapeDtypeStruct(sc.shape, F32), jax.ShapeDtypeStruct(sh.shape, F32)],
        grid=(n // tm,),
        in_specs=[a_spec(a, first) for a, _, first in terms]
        + [pl.BlockSpec(w.shape, lambda i: (0, 0))] + _parts_specs(xs, tm, ncb)
        + [_bc_spec(g, ncb), _bc_spec(sc, ncb), _bc_spec(sh, ncb),
           pl.BlockSpec((tm, d), lambda i: (jnp.maximum(i - add_skip, 0), 0))],
        out_specs=[pl.BlockSpec((tm, d), lambda i: (jnp.maximum(i - dx_skip, 0), 0)),
                   _bc_spec(g, ncb), _bc_spec(sc, ncb), _bc_spec(sh, ncb)],
        compiler_params=_params(("arbitrary",), VMEM_LIMIT), name=name)(*[t[0] for t in terms], w, *xs, g, sc, sh, add)


def mla_post_fwd(o, p0, x0, gate, w_out, n_ctx, name="l0_post"):
    n, d = o.shape
    xs, _ = _as_parts(x0, n_ctx)
    tm = math.gcd(ROW_TILE, n, n_ctx)
    ncb = n_ctx // tm
    nx = len(xs)

    def body(o_ref, z_ref, *refs):
        gt_ref, w_ref, x1_ref, og_ref, out_ref = refs[nx:]
        og = f_gate(o_ref[...], z_ref[...])[0].astype(BF16)
        out = jnp.dot(og, w_ref[...], preferred_element_type=F32)
        og_ref[...] = og
        out_ref[...] = out
        x1_ref[...] = _parts_tile(refs[:nx], ncb) + gt_ref[0] * out

    row = pl.BlockSpec((tm, d), lambda i: (i, 0))
    return pl.pallas_call(
        body, out_shape=[jax.ShapeDtypeStruct((n, d), F32), jax.ShapeDtypeStruct((n, d), BF16),
                         jax.ShapeDtypeStruct((n, d), F32)],
        grid=(n // tm,),
        in_specs=[row, row] + _parts_specs(xs, tm, ncb) + [_bc_spec(gate, ncb), pl.BlockSpec((d, d), lambda i: (0, 0))],
        out_specs=[row, row, row],
        compiler_params=_params(("parallel",), VMEM_LIMIT), name=name)(o, p0, *xs, gate, w_out)


def mla_post_bwd(dx1, out, og, o, p0, gate, w_out, n_ctx, name="l0_post_bwd"):
    n, d = o.shape
    tm = math.gcd(ROW_TILE, n, n_ctx)
    ncb = n_ctx // tm

    def body(dx_ref, out_ref, og_ref, o_ref, z_ref, gt_ref, w_ref, do_ref, dz_ref, dgt_ref, dw_ref):
        i = pl.program_id(0)

        @pl.when(i == 0)
        def _():
            dw_ref[...] = jnp.zeros_like(dw_ref)

        @pl.when((i == 0) | (i == ncb))
        def _():
            dgt_ref[...] = jnp.zeros_like(dgt_ref)

        dx = dx_ref[...]
        dgt_ref[0] += jnp.sum(dx * out_ref[...], axis=0, keepdims=True)
        d_out16 = (gt_ref[0] * dx).astype(BF16)
        dw_ref[...] += lax.dot_general(og_ref[...], d_out16, (((0,), (0,)), ((), ())), preferred_element_type=F32)
        d_og = lax.dot_general(d_out16, w_ref[...], NT_DIMS, preferred_element_type=F32)
        _, gate_vjp = jax.vjp(lambda o_, z_: f_gate(o_, z_), o_ref[...], z_ref[...])
        d_o, d_z = gate_vjp((d_og,))
        do_ref[...] = d_o
        dz_ref[...] = d_z

    row = pl.BlockSpec((tm, d), lambda i: (i, 0))
    mat = pl.BlockSpec((d, d), lambda i: (0, 0))
    return pl.pallas_call(
        body, out_shape=[jax.ShapeDtypeStruct((n, d), F32), jax.ShapeDtypeStruct((n, d), F32),
                         jax.ShapeDtypeStruct(gate.shape, F32), jax.ShapeDtypeStruct((d, d), F32)],
        grid=(n // tm,),
        in_specs=[row, row, row, row, row, _bc_spec(gate, ncb), mat],
        out_specs=[row, row, _bc_spec(gate, ncb), mat],
        compiler_params=_params(("arbitrary",), VMEM_LIMIT), name=name)(dx1, out, og, o, p0, gate, w_out)


def s5_tail(y_ssm, p1, x1p, target, n_ctx, d_vec, b_glu, gate, final_g, w_glu, w_out, name="l1_tail"):
    n, d = y_ssm.shape
    tm = math.gcd(ROW_TILE, n, n_ctx)
    off = n_ctx // tm
    tn_dims = (((0,), (0,)), ((), ()))

    def row_loss(x, g, t):
        e = _rms(x) * g - t
        return 0.5 * (e * e) * (1.0 / d)

    def body(y_ref, u_ref, z_ref, x1_ref, t_ref, d_ref, b_ref, gt_ref, fg_ref, wg_ref, wo_ref,
             l_ref, dx_ref, dy_ref, du_ref, dz_ref, dfg_ref, dgt_ref, db_ref, dd_ref, dwg_ref, dwo_ref):
        @pl.when(pl.program_id(0) == 0)
        def _():
            for r in (l_ref, dfg_ref, dgt_ref, db_ref, dd_ref, dwg_ref, dwo_ref):
                r[...] = jnp.zeros_like(r)

        u, z, tgt, gt = u_ref[...], z_ref[...], t_ref[...], gt_ref[...]
        (ya,), act_vjp = jax.vjp(lambda y_, u_, d_: f_s5_act(y_, u_, d_), y_ref[...], u, d_ref[...])
        ya16 = ya.astype(BF16)
        gl = jnp.dot(ya16, wg_ref[...], preferred_element_type=F32)
        (y3,), glu_vjp = jax.vjp(lambda a_, g_, z_, b_: f_s5_glu(a_, g_, z_, b_), ya, gl, z, b_ref[...])
        y3_16 = y3.astype(BF16)
        out1 = jnp.dot(y3_16, wo_ref[...], preferred_element_type=F32)
        lterm, loss_vjp = jax.vjp(lambda x_, g_: row_loss(x_, g_, tgt), x1_ref[...] + gt * out1, fg_ref[...])
        dx2, dfg = loss_vjp(jnp.ones_like(lterm))
        l_ref[...] += jnp.sum(lterm, axis=0, keepdims=True)
        dfg_ref[...] += dfg
        dx_ref[...] = dx2
        dgt_ref[...] += jnp.sum(dx2 * out1, axis=0, keepdims=True)
        d_out16 = (gt * dx2).astype(BF16)
        dwo_ref[...] += lax.dot_general(y3_16, d_out16, tn_dims, preferred_element_type=F32)
        d_y3 = lax.dot_general(d_out16, wo_ref[...], NT_DIMS, preferred_element_type=F32)
        d_ya, d_gl, d_z, d_b = glu_vjp((d_y3,))
        dz_ref[...] = d_z
        db_ref[...] += d_b
        d_gl16 = d_gl.astype(BF16)
        dwg_ref[...] += lax.dot_general(ya16, d_gl16, tn_dims, preferred_element_type=F32)
        d_ya = d_ya + lax.dot_general(d_gl16, wg_ref[...], NT_DIMS, preferred_element_type=F32)
        d_y, d_u, d_d = act_vjp((d_ya,))
        dy_ref[...] = d_y
        du_ref[...] = d_u
        dd_ref[...] += d_d

    row = pl.BlockSpec((tm, d), lambda i: (i, 0))
    vecs = pl.BlockSpec((1, d), lambda i: (0, 0))
    mat = pl.BlockSpec((d, d), lambda i: (0, 0))
    return pl.pallas_call(
        body,
        out_shape=[jax.ShapeDtypeStruct((1, d), F32)] + [jax.ShapeDtypeStruct((n, d), F32)] * 4
        + [jax.ShapeDtypeStruct((1, d), F32)] * 4 + [jax.ShapeDtypeStruct((d, d), F32)] * 2,
        grid=(n // tm,),
        in_specs=[row, pl.BlockSpec((tm, d), lambda i: (i + off, 0)), pl.BlockSpec((tm, d), lambda i: (i + off, 1)),
                  pl.BlockSpec((tm, d), lambda i: (i + off, 0)), row, vecs, vecs, vecs, vecs, mat, mat],
        out_specs=[vecs, row, row, row, row, vecs, vecs, vecs, vecs, mat, mat],
        compiler_params=_params(("arbitrary",), VMEM_LIMIT), name=name)(
            y_ssm, p1, p1, x1p, target, d_vec, b_glu, gate, final_g, w_glu, w_out)


NT_DIMS = (((1,), (1,)), ((), ()))
HEAD_LANES = 128
N_PAIRS = MLA_HEADS // 2


def _own_lanes(shape, hh):
    lane = lax.broadcasted_iota(jnp.int32, shape, len(shape) - 1)
    return (lane < V_HEAD_DIM) if hh == 0 else (lane >= V_HEAD_DIM)


def _delta_lane(hh):
    return V_HEAD_DIM if hh == 0 else 0


def _rope_tiles(x, cos, sin_next, sin_prev, inverse):
    width = x.shape[-1]
    reps = width // HEAD_LANES
    c, sn, sp = (jnp.tile(t, (1, reps)) for t in (cos, sin_next, sin_prev))
    if inverse:
        return x * c + pltpu.roll(x * sn, 8, 1) + pltpu.roll(x * sp, width - 8, 1)
    return x * c + pltpu.roll(x, width - 8, 1) * sn + pltpu.roll(x, 8, 1) * sp


STAT_LANE = QK_DIM


def _with_stat(x16, col, lane0):
    hi = col.astype(BF16)
    r1 = col - hi.astype(F32)
    mid = r1.astype(BF16)
    lo = (r1 - mid.astype(F32)).astype(BF16)
    lane = lax.broadcasted_iota(jnp.int32, x16.shape, 1)
    return jnp.where(lane == lane0, hi, jnp.where(lane == lane0 + 1, mid, jnp.where(lane == lane0 + 2, lo, x16)))


def attn_fwd(qb, kb, vb, n_ctx):
    T = qb.shape[0]
    tq = math.gcd(ROW_TILE, n_ctx)
    nq, ncb = T // tq, n_ctx // tq

    def body(q_ref, k_ref, v_ref, o_ref, lse_ref, qs_ref):
        qi = pl.program_id(1)

        def rows(n_keys):
            v = v_ref[:n_keys, :]
            outs = []
            for hh in range(2):
                hs = slice(hh * HEAD_LANES, (hh + 1) * HEAD_LANES)
                s = lax.dot_general(q_ref[:, hs], k_ref[:n_keys, hs], NT_DIMS,
                                    preferred_element_type=F32) * SOFTMAX_SCALE
                m = jnp.max(s, axis=-1, keepdims=True)
                p = jnp.exp(s - m)
                l = jnp.sum(p, axis=-1, keepdims=True)
                outs.append(jnp.dot(p.astype(BF16), v, preferred_element_type=F32) / l)
                lse = m + jnp.log(l)
                lse_ref[hh] = lse
                qs_ref[:, hs] = _with_stat(q_ref[:, hs], lse * (-1.0 / SOFTMAX_SCALE), STAT_LANE)
            o_ref[...] = jnp.where(_own_lanes(outs[0].shape, 0), outs[0], outs[1])

        pl.when(qi < ncb)(lambda: rows(n_ctx))
        pl.when(qi >= ncb)(lambda: rows(T))

    return pl.pallas_call(
        body,
        out_shape=[jax.ShapeDtypeStruct((T, MLA_HEADS * V_HEAD_DIM), F32),
                   jax.ShapeDtypeStruct((MLA_HEADS, T, 1), F32), jax.ShapeDtypeStruct(qb.shape, BF16)],
        grid=(N_PAIRS, nq),
        in_specs=[pl.BlockSpec((tq, 2 * HEAD_LANES), lambda h, i: (i, h)),
                  pl.BlockSpec((T, 2 * HEAD_LANES), lambda h, i: (0, h)),
                  pl.BlockSpec((T, 2 * V_HEAD_DIM), lambda h, i: (0, h))],
        out_specs=[pl.BlockSpec((tq, 2 * V_HEAD_DIM), lambda h, i: (i, h)),
                   pl.BlockSpec((2, tq, 1), lambda h, i: (h, i, 0)),
                   pl.BlockSpec((tq, 2 * HEAD_LANES), lambda h, i: (i, h))],
        compiler_params=_params(("parallel", "parallel"), VMEM_LIMIT), name="attn_fwd")(qb, kb, vb)


def attn_bwd_dq(qb, kb, vb, o, do, lse, tabs, n_ctx):
    T = qb.shape[0]
    tq = math.gcd(ROW_TILE, n_ctx)
    nq, ncb = T // tq, n_ctx // tq

    def body(q_ref, k_ref, v_ref, o_ref, do_ref, lse_ref, c_ref, sn_ref, sp_ref, dq_ref, dos_ref):
        qi = pl.program_id(1)

        def rows(n_keys):
            v = v_ref[:n_keys, :]
            dqs = []
            for hh in range(2):
                hs = slice(hh * HEAD_LANES, (hh + 1) * HEAD_LANES)
                k = k_ref[:n_keys, hs]
                do = jnp.where(_own_lanes(do_ref.shape, hh), do_ref[...], 0.0)
                delta = jnp.sum(do * o_ref[...], axis=-1, keepdims=True)
                s = lax.dot_general(q_ref[:, hs], k, NT_DIMS, preferred_element_type=F32) * SOFTMAX_SCALE
                p = jnp.exp(s - lse_ref[hh])
                do16 = do.astype(BF16)
                dp = lax.dot_general(do16, v, NT_DIMS, preferred_element_type=F32)
                ds = p * (dp - delta) * SOFTMAX_SCALE
                dqs.append(jnp.dot(ds.astype(BF16), k, preferred_element_type=F32))
                dos_ref[:, hs] = _with_stat(do16, delta, _delta_lane(hh))
            dq = jnp.concatenate(dqs, axis=1)
            dq_ref[...] = _rope_tiles(dq, c_ref[...], sn_ref[...], sp_ref[...], True).astype(BF16)

        pl.when(qi < ncb)(lambda: rows(n_ctx))
        pl.when(qi >= ncb)(lambda: rows(T))

    tab = pl.BlockSpec((tq, HEAD_LANES), lambda h, i: (i, 0))
    return pl.pallas_call(
        body,
        out_shape=[jax.ShapeDtypeStruct((T, MLA_HEADS * HEAD_LANES), BF16)] * 2,
        grid=(N_PAIRS, nq),
        in_specs=[pl.BlockSpec((tq, 2 * HEAD_LANES), lambda h, i: (i, h)),
                  pl.BlockSpec((T, 2 * HEAD_LANES), lambda h, i: (0, h)),
                  pl.BlockSpec((T, 2 * V_HEAD_DIM), lambda h, i: (0, h)),
                  pl.BlockSpec((tq, 2 * V_HEAD_DIM), lambda h, i: (i, h)),
                  pl.BlockSpec((tq, 2 * V_HEAD_DIM), lambda h, i: (i, h)),
                  pl.BlockSpec((2, tq, 1), lambda h, i: (h, i, 0)), tab, tab, tab],
        out_specs=[pl.BlockSpec((tq, 2 * HEAD_LANES), lambda h, i: (i, h))] * 2,
        compiler_params=_params(("parallel", "parallel"), VMEM_LIMIT), name="attn_bwd_dq")(
            qb, kb, vb, o, do, lse, *tabs)


def attn_bwd_dkv(qs, kb, vb, dos, n_ctx):
    T = qs.shape[0]
    tq = math.gcd(ROW_TILE, n_ctx)
    nq, ncb = T // tq, n_ctx // tq

    def body(q_ref, do_ref, k_ref, v_ref, dk_ref, dv_ref):
        kj = pl.program_id(1)

        def cols(first):
            v = v_ref[...]
            lane = lax.broadcasted_iota(jnp.int32, v.shape, 1)
            dvs = []
            for hh in range(2):
                hs = slice(hh * HEAD_LANES, (hh + 1) * HEAD_LANES)
                q = q_ref[first:, hs]
                do16 = do_ref[first:, hs]
                in_delta = (lane >= _delta_lane(hh)) & (lane < _delta_lane(hh) + 3)
                v_minus = jnp.where(in_delta, -jnp.ones_like(v), v)
                pt = jnp.exp(lax.dot_general(k_ref[:, hs], q, NT_DIMS, preferred_element_type=F32) * SOFTMAX_SCALE)
                dvs.append(jnp.dot(pt.astype(BF16), do16, preferred_element_type=F32))
                dst = pt * lax.dot_general(v_minus, do16, NT_DIMS, preferred_element_type=F32) * SOFTMAX_SCALE
                dk_ref[:, hs] = jnp.dot(dst.astype(BF16), q, preferred_element_type=F32)
            dv_ref[...] = jnp.where(_own_lanes(dvs[0].shape, 0), dvs[0], dvs[1])

        pl.when(kj < ncb)(lambda: cols(0))
        pl.when(kj >= ncb)(lambda: cols(n_ctx))

    return pl.pallas_call(
        body,
        out_shape=[jax.ShapeDtypeStruct((T, MLA_HEADS * HEAD_LANES), F32),
                   jax.ShapeDtypeStruct((T, MLA_HEADS * V_HEAD_DIM), F32)],
        grid=(N_PAIRS, nq),
        in_specs=[pl.BlockSpec((T, 2 * HEAD_LANES), lambda h, j: (0, h)),
                  pl.BlockSpec((T, 2 * HEAD_LANES), lambda h, j: (0, h)),
                  pl.BlockSpec((tq, 2 * HEAD_LANES), lambda h, j: (j, h)),
                  pl.BlockSpec((tq, 2 * V_HEAD_DIM), lambda h, j: (j, h))],
        out_specs=[pl.BlockSpec((tq, 2 * HEAD_LANES), lambda h, j: (j, h)),
                   pl.BlockSpec((tq, 2 * V_HEAD_DIM), lambda h, j: (j, h))],
        compiler_params=_params(("parallel", "parallel"), VMEM_LIMIT), name="attn_bwd_dkv")(
            qs, dos, kb, vb)


def _split_bf16(x):
    hi = x.astype(BF16)
    return hi, (x - hi.astype(F32)).astype(BF16)


def q_heads(cq, gain, w_uq_p, tabs, name="l0_uq"):
    T, K = cq.arr.shape[0], cq.width
    N = w_uq_p.shape[1]
    tm = math.gcd(ROW_TILE, T)

    def body(a_ref, g_ref, w_ref, c_ref, sn_ref, sp_ref, o_ref, n_ref):
        qn = f_rms(a_ref[...], g_ref[0])[0].astype(BF16)
        n_ref[...] = qn
        acc = jnp.dot(qn, w_ref[...], preferred_element_type=F32)
        o_ref[...] = _rope_tiles(acc, c_ref[...], sn_ref[...], sp_ref[...], False).astype(BF16)

    tab = pl.BlockSpec((tm, HEAD_LANES), lambda i: (i, 0))
    return pl.pallas_call(
        body, out_shape=[jax.ShapeDtypeStruct((T, N), BF16), jax.ShapeDtypeStruct((T, K), BF16)], grid=(T // tm,),
        in_specs=[cq.spec(tm), pl.BlockSpec((1, 1, K), lambda i: (0, 0, 0)), pl.BlockSpec((K, N), lambda i: (0, 0)),
                  tab, tab, tab],
        out_specs=[pl.BlockSpec((tm, N), lambda i: (i, 0)), pl.BlockSpec((tm, K), lambda i: (i, 0))],
        compiler_params=_params(("parallel",), VMEM_LIMIT), name=name)(cq.arr, gain, w_uq_p, *tabs)


def kv_heads(ckv, gain, w_kn_p, w_v, kr, spread, tabs, name="l0_ukv"):
    T, K = ckv.arr.shape[0], ckv.width
    N = w_kn_p.shape[1]
    NV = w_v.shape[1]
    tm = math.gcd(ROW_TILE, T)

    def body(a_ref, g_ref, wk_ref, wv_ref, kr_ref, e_ref, c_ref, sn_ref, sp_ref, k_ref, v_ref, n_ref):
        a = f_rms(a_ref[...], g_ref[0])[0].astype(BF16)
        n_ref[...] = a
        hi, lo = _split_bf16(kr_ref[...])
        acc = (jnp.dot(a, wk_ref[...], preferred_element_type=F32)
               + jnp.dot(hi, e_ref[...], preferred_element_type=F32)
               + jnp.dot(lo, e_ref[...], preferred_element_type=F32))
        roped = _rope_tiles(acc, c_ref[...], sn_ref[...], sp_ref[...], False)
        lane = lax.broadcasted_iota(jnp.int32, roped.shape, 1) % HEAD_LANES
        k_ref[...] = jnp.where((lane >= STAT_LANE) & (lane < STAT_LANE + 3), 1.0, roped).astype(BF16)
        v_ref[...] = jnp.dot(a, wv_ref[...], preferred_element_type=F32).astype(BF16)

    tab = pl.BlockSpec((tm, HEAD_LANES), lambda i: (i, 0))
    return pl.pallas_call(
        body, out_shape=[jax.ShapeDtypeStruct((T, N), BF16), jax.ShapeDtypeStruct((T, NV), BF16),
                         jax.ShapeDtypeStruct((T, K), BF16)], grid=(T // tm,),
        in_specs=[ckv.spec(tm), pl.BlockSpec((1, 1, K), lambda i: (0, 0, 0)), pl.BlockSpec((K, N), lambda i: (0, 0)),
                  pl.BlockSpec((K, NV), lambda i: (0, 0)), pl.BlockSpec((tm, QK_ROPE_DIM), lambda i: (i, 0)),
                  pl.BlockSpec((QK_ROPE_DIM, N), lambda i: (0, 0)), tab, tab, tab],
        out_specs=[pl.BlockSpec((tm, N), lambda i: (i, 0)), pl.BlockSpec((tm, NV), lambda i: (i, 0)),
                   pl.BlockSpec((tm, K), lambda i: (i, 0))],
        compiler_params=_params(("parallel",), VMEM_LIMIT), name=name)(ckv.arr, gain, w_kn_p, w_v, kr, spread, *tabs)


def heads_unrope(d, tabs, spread=None, name="unrope"):
    T, N = d.shape
    tm = math.gcd(ROW_TILE, T)

    def body(*refs):
        if spread is None:
            d_ref, c_ref, sn_ref, sp_ref, o_ref = refs
        else:
            d_ref, c_ref, sn_ref, sp_ref, e_ref, o_ref, kr_ref = refs
        g = _rope_tiles(d_ref[...], c_ref[...], sn_ref[...], sp_ref[...], True)
        o_ref[...] = g.astype(BF16)
        if spread is not None:
            hi, lo = _split_bf16(g)
            kr_ref[...] = (lax.dot_general(hi, e_ref[...], NT_DIMS, preferred_element_type=F32)
                           + lax.dot_general(lo, e_ref[...], NT_DIMS, preferred_element_type=F32))

    tab = pl.BlockSpec((tm, HEAD_LANES), lambda i: (i, 0))
    row = pl.BlockSpec((tm, N), lambda i: (i, 0))
    ins, in_specs = [d, *tabs], [row, tab, tab, tab]
    out_shape, out_specs = [jax.ShapeDtypeStruct((T, N), BF16)], [row]
    if spread is not None:
        ins.append(spread)
        in_specs.append(pl.BlockSpec(spread.shape, lambda i: (0, 0)))
        out_shape.append(jax.ShapeDtypeStruct((T, spread.shape[0]), F32))
        out_specs.append(pl.BlockSpec((tm, spread.shape[0]), lambda i: (i, 0)))
    return pl.pallas_call(
        body, out_shape=out_shape, grid=(T // tm,), in_specs=in_specs, out_specs=out_specs,
        compiler_params=_params(("parallel",), VMEM_LIMIT), name=name)(*ins)


def _cmul(ar, ai, br, bi):
    return ar * br - ai * bi, ar * bi + ai * br


def s5_chain(finals, s0, a, n_steps, reverse, name):
    W = finals.shape[-1]
    first = N_SEG - 1 if reverse else 0

    def body(f_ref, s0_ref, a_ref, c_ref):
        pr, pi = jnp.ones((1, W), F32), jnp.zeros((1, W), F32)
        br, bi = a_ref[0], a_ref[1]
        n = n_steps
        while n:
            if n & 1:
                pr, pi = _cmul(pr, pi, br, bi)
            br, bi = _cmul(br, bi, br, bi)
            n >>= 1
        fr, fi = f_ref[0], f_ref[1]
        row = lax.broadcasted_iota(jnp.int32, (N_SEG, W), 0)
        s0r = jnp.broadcast_to(s0_ref[0], (N_SEG, W))
        s0i = jnp.broadcast_to(s0_ref[1], (N_SEG, W))
        cr = jnp.where(row == first, s0r, 0.0)
        ci = jnp.where(row == first, s0i, 0.0)
        shift = N_SEG - 1 if reverse else 1
        for _ in range(N_SEG - 1):
            mr, mi = _cmul(pr, pi, cr, ci)
            tr = pltpu.roll(fr + mr, shift, 0)
            ti = pltpu.roll(fi + mi, shift, 0)
            cr = jnp.where(row == first, s0r, tr)
            ci = jnp.where(row == first, s0i, ti)
        c_ref[0] = cr
        c_ref[1] = ci

    return pl.pallas_call(body, out_shape=jax.ShapeDtypeStruct((2, N_SEG, W), F32), name=name)(finals, s0, a)


def _scan_chunk(bur, bui, st_ref, a_ref, n_steps, reverse):
    for lc in range(S5_LANES // BLK_ST):
        sl = slice(lc * BLK_ST, (lc + 1) * BLK_ST)
        lr = jnp.broadcast_to(a_ref[0, :, sl], (N_SEG, BLK_ST))
        li = jnp.broadcast_to(a_ref[1, :, sl], (N_SEG, BLK_ST))

        def step(jj, carry, sl=sl, lr=lr, li=li):
            sr, si = carry
            j = (n_steps - 1 - jj) if reverse else jj
            r0 = pl.multiple_of(j * N_SEG, N_SEG)
            nr = lr * sr - li * si + bur[pl.ds(r0, N_SEG), sl]
            ni = lr * si + li * sr + bui[pl.ds(r0, N_SEG), sl]
            bur[pl.ds(r0, N_SEG), sl] = nr
            bui[pl.ds(r0, N_SEG), sl] = ni
            return nr, ni

        sr, si = lax.fori_loop(0, n_steps, step, (st_ref[0, :, sl], st_ref[1, :, sl]))
        st_ref[0, :, sl] = sr
        st_ref[1, :, sl] = si


def _project_in(x16, w_re, w_im, bur, bui, adjoint):
    for gb in range(N_BLOCKS):
        xb = x16[:, gb * BLK_CH:(gb + 1) * BLK_CH]
        sl = slice(gb * BLK_ST, (gb + 1) * BLK_ST)
        if adjoint:
            dn = (((1,), (1,)), ((), ()))
            bur[:, sl] = lax.dot_general(xb, w_re[gb], dn, preferred_element_type=F32)
            bui[:, sl] = -lax.dot_general(xb, w_im[gb], dn, preferred_element_type=F32)
        else:
            bur[:, sl] = jnp.dot(xb, w_re[gb], preferred_element_type=F32)
            bui[:, sl] = jnp.dot(xb, w_im[gb], preferred_element_type=F32)


def s5_scan(act, w_re, w_im, a, init, *, reverse, adjoint=False, c_re=None, c_im=None, add=None,
            want_ckpt=False, rows=None, name):
    act_off, N = rows if rows is not None else (0, act.shape[0])
    R = math.gcd(ROW_TILE, N, act_off)
    nch, jc = N // R, R // N_SEG
    with_out = c_re is not None

    def chunk(i):
        return (nch - 1 - i) if reverse else i

    def body(*refs):
        act_ref, wre_ref, wim_ref, a_ref, init_ref = refs[:5]
        k = 5
        if with_out:
            cre_ref, cim_ref = refs[k:k + 2]
            k += 2
        if add is not None:
            add_ref = refs[k]
            k += 1
        if with_out:
            out_ref = refs[k]
            k += 1
        if want_ckpt:
            ck_ref = refs[k]
            k += 1
        fin_ref, bur, bui = refs[k:k + 3]

        @pl.when(pl.program_id(0) == 0)
        def _():
            fin_ref[...] = init_ref[...]

        if want_ckpt:
            ck_ref[0] = fin_ref[...]
        _project_in(act_ref[...].astype(BF16), wre_ref, wim_ref, bur, bui, adjoint)
        _scan_chunk(bur, bui, fin_ref, a_ref, jc, reverse)
        if with_out:
            for gb in range(N_BLOCKS):
                sl = slice(gb * BLK_ST, (gb + 1) * BLK_ST)
                y = (jnp.dot(bur[:, sl].astype(BF16), cre_ref[gb], preferred_element_type=F32)
                     - jnp.dot(bui[:, sl].astype(BF16), cim_ref[gb], preferred_element_type=F32))
                cs = slice(gb * BLK_CH, (gb + 1) * BLK_CH)
                if add is not None:
                    y = y + add_ref[:, cs]
                out_ref[:, cs] = y

    row_spec = pl.BlockSpec((R, D_MODEL), lambda i: (chunk(i), 0))
    act_spec = pl.BlockSpec((R, D_MODEL), lambda i: (chunk(i) + act_off // R, 0))
    w_spec = pl.BlockSpec(w_re.shape, lambda i: (0, 0, 0))
    st_spec = pl.BlockSpec((2, N_SEG, S5_LANES), lambda i: (0, 0, 0))
    ins = [act, w_re, w_im, a, init]
    in_specs = [act_spec, w_spec, w_spec, pl.BlockSpec((2, 1, S5_LANES), lambda i: (0, 0, 0)), st_spec]
    if with_out:
        ins += [c_re, c_im]
        in_specs += [pl.BlockSpec(c_re.shape, lambda i: (0, 0, 0))] * 2
    if add is not None:
        ins.append(add)
        in_specs.append(row_spec)
    out_shape, out_specs = [], []
    if with_out:
        out_shape.append(jax.ShapeDtypeStruct((N, D_MODEL), F32))
        out_specs.append(row_spec)
    if want_ckpt:
        out_shape.append(jax.ShapeDtypeStruct((nch, 2, N_SEG, S5_LANES), F32))
        out_specs.append(pl.BlockSpec((1, 2, N_SEG, S5_LANES), lambda i: (chunk(i), 0, 0, 0)))
    out_shape.append(jax.ShapeDtypeStruct((2, N_SEG, S5_LANES), F32))
    out_specs.append(st_spec)
    res = pl.pallas_call(
        body, out_shape=out_shape, grid=(nch,), in_specs=in_specs, out_specs=out_specs,
        scratch_shapes=[pltpu.VMEM((R, S5_LANES), F32), pltpu.VMEM((R, S5_LANES), F32)],
        compiler_params=_params(("arbitrary",), VMEM_LIMIT), name=name)(*ins)
    res = list(res)
    out = res.pop(0) if with_out else None
    ckpt = res.pop(0) if want_ckpt else None
    return out, ckpt, res[0]


def s5_grads(dy, u, ckpt, b_re, b_im, c_re, c_im, lam, init_adj, *, reverse, add=None, u_off=0, du_rows=None,
             du_into=None, name):
    N = dy.shape[0]
    du_total, du_first = du_rows if du_rows is not None else (N, 0)
    R = math.gcd(ROW_TILE, N, u_off, du_first)
    nch, jc = N // R, R // N_SEG
    W = S5_LANES

    def chunk(i):
        return i if reverse else (nch - 1 - i)

    def body(*refs):
        dy_ref, u_ref, ck_ref, bre_ref, bim_ref, cre_ref, cim_ref, lam_ref, init_ref = refs[:9]
        k = 9
        if add is not None:
            add_ref = refs[k]
            k += 1
        k += du_into is not None
        du_ref, dlam_ref, dbre_ref, dbim_ref, dcre_ref, dcim_ref, fin_ref = refs[k:k + 7]
        sr_buf, si_buf, er_buf, ei_buf, st_buf = refs[k + 7:k + 12]

        @pl.when(pl.program_id(0) == 0)
        def _():
            fin_ref[...] = init_ref[...]
            dlam_ref[...] = jnp.zeros_like(dlam_ref)
            dbre_ref[...] = jnp.zeros_like(dbre_ref)
            dbim_ref[...] = jnp.zeros_like(dbim_ref)
            dcre_ref[...] = jnp.zeros_like(dcre_ref)
            dcim_ref[...] = jnp.zeros_like(dcim_ref)

        u16 = u_ref[...].astype(BF16)
        dy16 = dy_ref[...].astype(BF16)
        st_buf[...] = ck_ref[0]
        _project_in(u16, bre_ref, bim_ref, sr_buf, si_buf, False)
        _scan_chunk(sr_buf, si_buf, st_buf, lam_ref, jc, reverse)
        _project_in(dy16, cre_ref, cim_ref, er_buf, ei_buf, True)
        for lc in range(W // BLK_ST):
            sl = slice(lc * BLK_ST, (lc + 1) * BLK_ST)
            lr = jnp.broadcast_to(lam_ref[0, :, sl], (N_SEG, BLK_ST))
            li = jnp.broadcast_to(lam_ref[1, :, sl], (N_SEG, BLK_ST))

            def one(r0, spr, spi, carry, sl=sl, lr=lr, li=li):
                gr, gi, ar, ai = carry
                nr = er_buf[pl.ds(r0, N_SEG), sl] + lr * gr + li * gi
                ni = ei_buf[pl.ds(r0, N_SEG), sl] + lr * gi - li * gr
                er_buf[pl.ds(r0, N_SEG), sl] = nr
                ei_buf[pl.ds(r0, N_SEG), sl] = ni
                return nr, ni, ar + spr * nr + spi * ni, ai + spr * ni - spi * nr

            def step(ff, carry, sl=sl, one=one):
                f = jc - 1 - ff
                j = (jc - 1 - f) if reverse else f
                jp = (j + 1) if reverse else (j - 1)
                r0 = pl.multiple_of(j * N_SEG, N_SEG)
                p0 = pl.multiple_of(jp * N_SEG, N_SEG)
                return one(r0, sr_buf[pl.ds(p0, N_SEG), sl], si_buf[pl.ds(p0, N_SEG), sl], carry)

            carry = (fin_ref[0, :, sl], fin_ref[1, :, sl], dlam_ref[0, :, sl], dlam_ref[1, :, sl])
            carry = lax.fori_loop(0, jc - 1, step, carry)
            r_first = (jc - 1) * N_SEG if reverse else 0
            gr, gi, ar, ai = one(r_first, ck_ref[0, 0, :, sl], ck_ref[0, 1, :, sl], carry)
            fin_ref[0, :, sl] = gr
            fin_ref[1, :, sl] = gi
            dlam_ref[0, :, sl] = ar
            dlam_ref[1, :, sl] = ai
        tn = (((0,), (0,)), ((), ()))
        nt = (((1,), (1,)), ((), ()))
        for gb in range(N_BLOCKS):
            sl = slice(gb * BLK_ST, (gb + 1) * BLK_ST)
            cs = slice(gb * BLK_CH, (gb + 1) * BLK_CH)
            gr16 = er_buf[:, sl].astype(BF16)
            gi16 = ei_buf[:, sl].astype(BF16)
            du = (lax.dot_general(gr16, bre_ref[gb], nt, preferred_element_type=F32)
                  + lax.dot_general(gi16, bim_ref[gb], nt, preferred_element_type=F32))
            if add is not None:
                du = du + add_ref[:, cs]
            du_ref[:, cs] = du
            ub, dyb = u16[:, cs], dy16[:, cs]
            dbre_ref[gb] += lax.dot_general(ub, gr16, tn, preferred_element_type=F32)
            dbim_ref[gb] += lax.dot_general(ub, gi16, tn, preferred_element_type=F32)
            dcre_ref[gb] += lax.dot_general(sr_buf[:, sl].astype(BF16), dyb, tn, preferred_element_type=F32)
            dcim_ref[gb] -= lax.dot_general(si_buf[:, sl].astype(BF16), dyb, tn, preferred_element_type=F32)

    row_spec = pl.BlockSpec((R, D_MODEL), lambda i: (chunk(i), 0))
    st_spec = pl.BlockSpec((2, N_SEG, W), lambda i: (0, 0, 0))
    wb_spec = pl.BlockSpec(b_re.shape, lambda i: (0, 0, 0))
    wc_spec = pl.BlockSpec(c_re.shape, lambda i: (0, 0, 0))
    ins = [dy, u, ckpt, b_re, b_im, c_re, c_im, lam, init_adj]
    u_spec = pl.BlockSpec((R, D_MODEL), lambda i: (chunk(i) + u_off // R, 0))
    in_specs = [row_spec, u_spec, pl.BlockSpec((1, 2, N_SEG, W), lambda i: (chunk(i), 0, 0, 0)),
                wb_spec, wb_spec, wc_spec, wc_spec, pl.BlockSpec((2, 1, W), lambda i: (0, 0, 0)), st_spec]
    if add is not None:
        ins.append(add)
        in_specs.append(row_spec)
    aliases = {}
    if du_into is not None:
        aliases[len(ins)] = 0
        ins.append(du_into)
        in_specs.append(pl.BlockSpec(memory_space=pl.ANY))
    du_spec = pl.BlockSpec((R, D_MODEL), lambda i: (chunk(i) + du_first // R, 0))
    out_shape = [jax.ShapeDtypeStruct((du_total, D_MODEL), F32), jax.ShapeDtypeStruct((2, N_SEG, W), F32),
                 jax.ShapeDtypeStruct(b_re.shape, F32), jax.ShapeDtypeStruct(b_re.shape, F32),
                 jax.ShapeDtypeStruct(c_re.shape, F32), jax.ShapeDtypeStruct(c_re.shape, F32),
                 jax.ShapeDtypeStruct((2, N_SEG, W), F32)]
    out_specs = [du_spec, st_spec, wb_spec, wb_spec, wc_spec, wc_spec, st_spec]
    return pl.pallas_call(
        body, out_shape=out_shape, grid=(nch,), in_specs=in_specs, out_specs=out_specs, input_output_aliases=aliases,
        scratch_shapes=[pltpu.VMEM((R, W), F32) for _ in range(4)] + [pltpu.VMEM((2, N_SEG, W), F32)],
        compiler_params=_params(("arbitrary",), VMEM_LIMIT), name=name)(*ins)


def adamw(w, g, m, v, name="adamw", after=None):
    n, d = w.shape
    lanes = -(-d // 128) * 128
    tm = n
    while tm * lanes * 4 > (1 << 20) and tm % 16 == 0:
        tm //= 2
    c1 = 1.0 - ADAM_B1 ** ADAM_STEP
    c2 = 1.0 - ADAM_B2 ** ADAM_STEP

    def body(w_ref, g_ref, m_ref, v_ref, *rest):
        d_ref, nm_ref, nv_ref = rest[-3:]
        g_ = g_ref[...]
        m_ = ADAM_B1 * m_ref[...] + (1.0 - ADAM_B1) * g_
        v_ = ADAM_B2 * v_ref[...] + (1.0 - ADAM_B2) * (g_ * g_)
        d_ref[...] = -ADAM_LR * ((m_ / c1) / (jnp.sqrt(v_ / c2) + ADAM_EPS) + ADAM_WD * w_ref[...])
        nm_ref[...] = m_
        nv_ref[...] = v_

    spec = pl.BlockSpec((tm, d), lambda i: (i, 0))
    extra = [] if after is None else [after]
    return pl.pallas_call(
        body, out_shape=[jax.ShapeDtypeStruct((n, d), F32)] * 3, grid=(n // tm,),
        in_specs=[spec] * 4 + [pl.BlockSpec(memory_space=pl.ANY)] * len(extra), out_specs=[spec] * 3,
        compiler_params=_params(("parallel",), VMEM_LIMIT), name=name)(w, g, m, v, *extra)


def _coords():
    return lax.axis_index("x"), lax.axis_index("y"), lax.axis_index("c")


def exchange(arrays, out_shapes, remote, local, name, aliases=None):
    n_in, n_out, n_rem, n_loc = len(arrays), len(out_shapes), len(remote), len(local)

    def at(ref, idx):
        return ref if idx is None else ref.at[idx]

    def body(*refs):
        ins, outs = refs[:n_in], refs[n_in:n_in + n_out]
        send_sems, recv_sems, local_sems = refs[n_in + n_out:]
        me = _coords()
        sends, recvs = [], []
        for k, (flip, ii, src_at, oi, dst_at) in enumerate(remote):
            peer = (me[0] ^ flip[0], me[1] ^ flip[1], me[2] ^ flip[2])
            src = at(ins[ii], src_at(me, peer))
            sends.append(pltpu.make_async_remote_copy(
                src_ref=src, dst_ref=at(outs[oi], dst_at(me)), send_sem=send_sems.at[k], recv_sem=recv_sems.at[k],
                device_id=peer, device_id_type=MESH))
            recvs.append(pltpu.make_async_remote_copy(
                src_ref=src, dst_ref=at(outs[oi], dst_at(peer)), send_sem=send_sems.at[k], recv_sem=recv_sems.at[k],
                device_id=peer, device_id_type=MESH))
        locs = [pltpu.make_async_copy(at(ins[ii], src_at(me)), at(outs[oi], dst_at(me)), local_sems.at[k])
                for k, (ii, src_at, oi, dst_at) in enumerate(local)]
        for cp in locs + sends:
            cp.start()
        for cp in recvs:
            cp.wait_recv()
        for cp in sends:
            cp.wait_send()
        for cp in locs:
            cp.wait()

    hbm = pl.BlockSpec(memory_space=pl.ANY)
    return pl.pallas_call(
        body, out_shape=list(out_shapes), in_specs=[hbm] * n_in, out_specs=[hbm] * n_out,
        scratch_shapes=[pltpu.SemaphoreType.DMA((n_rem,)), pltpu.SemaphoreType.DMA((n_rem,)),
                        pltpu.SemaphoreType.DMA((max(n_loc, 1),))],
        input_output_aliases=aliases or {}, name=name)(*arrays)


ALL_FLIPS = [(dx, dy, dc) for dx in (0, 1) for dy in (0, 1) for dc in (0, 1)][1:]
CHIP_FLIPS = [(1, 0, 0), (0, 1, 0), (1, 1, 0)]
CORE_FLIP = (0, 0, 1)


def _dev_index(p):
    return 4 * p[0] + 2 * p[1] + p[2]


def _chip_index(p):
    return 2 * p[0] + p[1]


def _gather(xs, flips, index, n, name):
    arrays = [x[None] for x in xs]
    outs = [jax.ShapeDtypeStruct((n,) + x.shape, x.dtype) for x in xs]
    remote = [(f, a, lambda me, peer: (0,), a, lambda s: (index(s),)) for a in range(len(xs)) for f in flips]
    local = [(a, lambda me: (0,), a, lambda me: (index(me),)) for a in range(len(xs))]
    return exchange(arrays, outs, remote, local, name)


def allgather_devices(x, name):
    return _gather([x], ALL_FLIPS, _dev_index, N_DEV, name)[0]


def allgather_chips(xs, name):
    return _gather(xs, CHIP_FLIPS, _chip_index, N_CHIP, name)


def gather_halves(xs, name):
    n = len(xs)
    nk = n * len(CHIP_FLIPS)

    def body(*refs):
        ins, outs = refs[:n], refs[n:2 * n]
        ici_send, ici_recv, d2d_send, d2d_recv = refs[2 * n:]
        me = _coords()
        sibling = (me[0], me[1], 1 - me[2])
        first, passed, landed = [], [], []
        for a in range(n):
            half = ins[a].shape[0] // 2
            mine = ins[a].at[pl.ds(pl.multiple_of(me[2] * half, 16), half)]
            for j, flip in enumerate(CHIP_FLIPS):
                k = a * len(CHIP_FLIPS) + j
                peer = (me[0] ^ flip[0], me[1] ^ flip[1], me[2])
                first.append(pltpu.make_async_remote_copy(
                    src_ref=mine, dst_ref=outs[a].at[_chip_index(me), me[2]], send_sem=ici_send.at[k],
                    recv_sem=ici_recv.at[k], device_id=peer, device_id_type=MESH))
                arrived = outs[a].at[_chip_index(peer), me[2]]
                landed.append(pltpu.make_async_remote_copy(
                    src_ref=mine, dst_ref=arrived, send_sem=ici_send.at[k], recv_sem=ici_recv.at[k],
                    device_id=peer, device_id_type=MESH))
                passed.append(pltpu.make_async_remote_copy(
                    src_ref=arrived, dst_ref=arrived, send_sem=d2d_send.at[k], recv_sem=d2d_recv.at[k],
                    device_id=sibling, device_id_type=MESH))
        for cp in first:
            cp.start()
        for k in range(nk):
            landed[k].wait_recv()
            passed[k].start()
        for a in range(n):
            for j, flip in enumerate(CHIP_FLIPS):
                k = a * len(CHIP_FLIPS) + j
                peer_chip = _chip_index((me[0] ^ flip[0], me[1] ^ flip[1]))
                from_sibling = outs[a].at[peer_chip, 1 - me[2]]
                pltpu.make_async_remote_copy(
                    src_ref=from_sibling, dst_ref=from_sibling, send_sem=d2d_send.at[k], recv_sem=d2d_recv.at[k],
                    device_id=sibling, device_id_type=MESH).wait_recv()
        for cp in first + passed:
            cp.wait_send()

    hbm = pl.BlockSpec(memory_space=pl.ANY)
    return pl.pallas_call(
        body, out_shape=[jax.ShapeDtypeStruct((N_CHIP, 2, x.shape[0] // 2, x.shape[1]), x.dtype) for x in xs],
        in_specs=[hbm] * n, out_specs=[hbm] * n,
        scratch_shapes=[pltpu.SemaphoreType.DMA((nk,)) for _ in range(4)], name=name)(*xs)


HBM_SPEC = pl.BlockSpec(memory_space=pltpu.HBM)
SEM_SPEC = pl.BlockSpec(memory_space=pltpu.SEMAPHORE)
DATAFLOW = pltpu.SideEffectType.DATAFLOW_SIDE_EFFECTING


def _at(ref, idx):
    return ref if idx is None else ref.at[idx]


def _peer(me, flip):
    return (me[0] ^ flip[0], me[1] ^ flip[1], me[2] ^ flip[2])


def exchange_start(arrays, land_shapes, remote, name, after=None):
    n_in, n_out, nk = len(arrays), len(land_shapes), len(remote)
    after = list(after or [])
    n_after = len(after)

    def body(*refs):
        srcs, lands = refs[:n_in], refs[n_in:n_in + n_out]
        first_out = n_in + n_out + n_after
        send_sems, recv_sems, token = refs[first_out], refs[first_out + 1], refs[-1]
        me = _coords()
        for k, (flip, ii, src_at, oi, dst_at) in enumerate(remote):
            peer = _peer(me, flip)
            pltpu.make_async_remote_copy(
                src_ref=_at(srcs[ii], src_at(me, peer)), dst_ref=_at(lands[oi], dst_at(me)), send_sem=send_sems.at[k],
                recv_sem=recv_sems.at[k], device_id=peer, device_id_type=MESH).start()
        token[...] = jnp.zeros_like(token)

    lands = [lax.empty(s.shape, s.dtype) for s in land_shapes]
    bufs = list(arrays) + lands
    out = pl.pallas_call(
        body, name=name,
        out_shape=(pltpu.SemaphoreType.DMA((nk,)), pltpu.SemaphoreType.DMA((nk,)),
                   *[pltpu.HBM(b.shape, b.dtype) for b in bufs], jax.ShapeDtypeStruct((8, 128), F32)),
        in_specs=[HBM_SPEC] * len(bufs) + [pl.BlockSpec(memory_space=pl.ANY)] * n_after,
        out_specs=(SEM_SPEC, SEM_SPEC, *[HBM_SPEC] * len(bufs), pl.BlockSpec(memory_space=pltpu.VMEM)),
        input_output_aliases={a: 2 + a for a in range(len(bufs))},
        compiler_params=pltpu.CompilerParams(has_side_effects=DATAFLOW),
    )(*[pltpu.with_memory_space_constraint(b, pltpu.HBM) for b in bufs], *after)
    flight = (out[0], out[1], list(out[2:2 + n_in]), list(out[2 + n_in:2 + n_in + n_out]), remote)
    return flight, out[-1]


def exchange_wait(flight, after, name):
    send_sems, recv_sems, arrays, lands, remote = flight
    n_in, n_out = len(arrays), len(lands)
    after = list(after) if isinstance(after, (list, tuple)) else [after]

    def body(*refs):
        srcs, lnds = refs[:n_in], refs[n_in:n_in + n_out]
        s_sems, r_sems = refs[n_in + n_out], refs[n_in + n_out + 1]
        me = _coords()
        for k, (flip, ii, src_at, oi, dst_at) in enumerate(remote):
            peer = _peer(me, flip)
            copy = pltpu.make_async_remote_copy(
                src_ref=_at(srcs[ii], src_at(me, peer)), dst_ref=_at(lnds[oi], dst_at(peer)), send_sem=s_sems.at[k],
                recv_sem=r_sems.at[k], device_id=peer, device_id_type=MESH)
            copy.wait_send()
            copy.wait_recv()

    bufs = list(arrays) + list(lands)
    out = pl.pallas_call(
        body, name=name,
        out_shape=tuple(pltpu.HBM(b.shape, b.dtype) for b in bufs),
        in_specs=[HBM_SPEC] * len(bufs) + [SEM_SPEC, SEM_SPEC] + [pl.BlockSpec(memory_space=pl.ANY)] * len(after),
        out_specs=tuple([HBM_SPEC] * len(bufs)),
        input_output_aliases={a: a for a in range(len(bufs))},
        compiler_params=pltpu.CompilerParams(has_side_effects=DATAFLOW),
    )(*bufs, send_sems, recv_sems, *after)
    return list(out[:n_in]), list(out[n_in:])


def _half_tile(h, cd):
    return h if h * cd * 4 <= (1 << 20) else math.gcd(512, h)


def pair_add(g, got, core, out_dtype, name):
    _, _, h, cd = g.shape
    th = _half_tile(h, cd)

    def body(c_ref, g_ref, got_ref, o_ref):
        o_ref[0] = (g_ref[0, 0] + got_ref[0]).astype(o_ref.dtype)

    return pl.pallas_call(
        body, out_shape=jax.ShapeDtypeStruct((N_CHIP, h, cd), out_dtype),
        grid_spec=pltpu.PrefetchScalarGridSpec(
            num_scalar_prefetch=1, grid=(N_CHIP, h // th),
            in_specs=[pl.BlockSpec((1, 1, th, cd), lambda q, i, c: (q, c[0], i, 0)),
                      pl.BlockSpec((1, th, cd), lambda q, i, c: (q, i, 0))],
            out_specs=pl.BlockSpec((1, th, cd), lambda q, i, c: (q, i, 0))),
        compiler_params=_params(("parallel", "parallel"), VMEM_LIMIT), name=name)(core, g, got)


def sum_chips(parts, sums, place, name):
    _, h, cd = parts.shape
    th = _half_tile(h, cd)

    def body(pc_ref, p_ref, own_ref, o_ref):
        acc = None
        for q in range(N_CHIP):
            term = jnp.where(pc_ref[1] == q, own_ref[0], p_ref[q]).astype(F32)
            acc = term if acc is None else acc + term
        o_ref[0] = acc

    return pl.pallas_call(
        body, out_shape=jax.ShapeDtypeStruct((2, h, cd), F32),
        grid_spec=pltpu.PrefetchScalarGridSpec(
            num_scalar_prefetch=1, grid=(h // th,),
            in_specs=[pl.BlockSpec((N_CHIP, th, cd), lambda i, pc: (0, i, 0)),
                      pl.BlockSpec((1, th, cd), lambda i, pc: (pc[1], i, 0))],
            out_specs=pl.BlockSpec((1, th, cd), lambda i, pc: (pc[0], i, 0))),
        compiler_params=_params(("parallel",), VMEM_LIMIT), name=name)(place, parts, sums)


def to_segments(a, n_ctx):
    def one(p):
        n = p.shape[0]
        return p.reshape(N_SEG, n // N_SEG, -1).transpose(1, 0, 2).reshape(n, -1)
    return jnp.concatenate([one(a[:n_ctx]), one(a[n_ctx:])], axis=0) if n_ctx else one(a)


def from_segments(a, n_ctx):
    def one(p):
        n = p.shape[0]
        return p.reshape(n // N_SEG, N_SEG, -1).transpose(1, 0, 2).reshape(n, -1)
    return jnp.concatenate([one(a[:n_ctx]), one(a[n_ctx:])], axis=0) if n_ctx else one(a)


def rope_tables(n_ctx, n_lat):
    f32 = np.float32
    rows = n_lat // GRID_W
    row = np.repeat(np.arange(rows), GRID_W).astype(f32)
    col = np.tile(np.arange(GRID_W), rows).astype(f32)
    d = QK_ROPE_DIM // 2
    inv = (f32(1.0) / np.power(f32(ROPE_THETA), np.arange(0, d, 2, dtype=f32) / f32(d))).astype(f32)
    ang = np.concatenate([row[:, None] * inv[None, :], col[:, None] * inv[None, :]], axis=1).astype(f32)
    cos = np.concatenate([np.ones((n_ctx, d), f32), np.cos(ang)], axis=0)
    sin = np.concatenate([np.zeros((n_ctx, d), f32), np.sin(ang)], axis=0)
    q = QK_ROPE_DIM // 4
    T = n_ctx + n_lat
    ones, zeros = np.ones((T, QK_NOPE_DIM), f32), np.zeros((T, QK_NOPE_DIM), f32)
    tail, z8 = np.zeros((T, HEAD_LANES - QK_DIM), f32), np.zeros((T, q), f32)
    cr, cc, sr, sc = cos[:, :q], cos[:, q:], sin[:, :q], sin[:, q:]
    cos_t = np.concatenate([ones, cr, cr, cc, cc, tail], axis=1)
    sin_next = np.concatenate([zeros, -sr, z8, -sc, z8, tail], axis=1)
    sin_prev = np.concatenate([zeros, z8, sr, z8, sc, tail], axis=1)
    return tuple(jnp.asarray(t, F32) for t in (cos_t, sin_next, sin_prev))


def pad_heads(w, used):
    k = w.shape[0]
    return jnp.pad(w.reshape(k, MLA_HEADS, used), ((0, 0), (0, 0), (0, HEAD_LANES - used))).reshape(k, -1)


def unpad_heads(w, used):
    k = w.shape[0]
    return w.reshape(k, MLA_HEADS, HEAD_LANES)[:, :, :used].reshape(k, MLA_HEADS * used)


def rotary_spread():
    lane = np.arange(MLA_HEADS * HEAD_LANES) % HEAD_LANES
    return jnp.asarray(lane[None, :] == (QK_NOPE_DIM + np.arange(QK_ROPE_DIM))[:, None], BF16)


def s5_discretise(a_re, a_im, log_step, b_re, b_im):
    dt = jnp.exp(log_step)[:, None]
    mag = jnp.exp(a_re * dt)
    lb_re = mag * jnp.cos(a_im * dt)
    lb_im = mag * jnp.sin(a_im * dt)
    den = a_re * a_re + a_im * a_im
    nr = lb_re - 1.0
    f_re = ((nr * a_re + lb_im * a_im) / den)[:, None, :]
    f_im = ((lb_im * a_re - nr * a_im) / den)[:, None, :]
    return lb_re, lb_im, f_re * b_re - f_im * b_im, f_re * b_im + f_im * b_re


def s5_block_weights(lb_re, lb_im, bb_re, bb_im, c_re, c_im):
    eye = jnp.eye(GROUPS_PER_BLOCK, dtype=F32)
    lam = jnp.stack([lb_re.reshape(1, S5_LANES), lb_im.reshape(1, S5_LANES)])

    def b_blocks(bb):
        t = bb.reshape(N_BLOCKS, GROUPS_PER_BLOCK, S5_GROUP, S5_STATE)
        return jnp.einsum("bgcp,gh->bgchp", t, eye).reshape(N_BLOCKS, BLK_CH, BLK_ST).astype(BF16)

    def c_blocks(cc):
        t = cc.reshape(N_BLOCKS, GROUPS_PER_BLOCK, S5_GROUP, S5_STATE)
        return jnp.einsum("bgcp,gh->bgphc", t, eye).reshape(N_BLOCKS, BLK_ST, BLK_CH).astype(BF16)

    return lam, b_blocks(bb_re), b_blocks(bb_im), c_blocks(c_re), c_blocks(c_im)


def b_block_diag(db):
    t = db.reshape(N_BLOCKS, GROUPS_PER_BLOCK, S5_GROUP, GROUPS_PER_BLOCK, S5_STATE)
    return jnp.einsum("bgchp,gh->bgcp", t, jnp.eye(GROUPS_PER_BLOCK, dtype=F32)).reshape(S5_GROUPS, S5_GROUP, S5_STATE)


def c_block_diag(dc):
    t = dc.reshape(N_BLOCKS, GROUPS_PER_BLOCK, S5_STATE, GROUPS_PER_BLOCK, S5_GROUP)
    return jnp.einsum("bgphc,gh->bgcp", t, jnp.eye(GROUPS_PER_BLOCK, dtype=F32)).reshape(S5_GROUPS, S5_GROUP, S5_STATE)


def conj(a):
    return jnp.stack([a[0], -a[1]])


def _narrow(shape):
    return len(shape) >= 2 and shape[-1] < min(HEAD_LANES, shape[-2])


def _stored(a):
    return jnp.swapaxes(a, -1, -2) if _narrow(a.shape) else a


def _stored_shape(shape):
    return tuple(shape[:-2]) + (shape[-1], shape[-2]) if _narrow(shape) else tuple(shape)


def _from_stored(a, shape):
    return jnp.swapaxes(a, -1, -2) if _narrow(shape) else a


PACK_TILE = 16 * 128


def pack_flat(parts, dtype):
    flat = [p.reshape(-1).astype(dtype) for p in parts]
    sizes = [f.shape[0] for f in flat]
    total = sum(sizes)
    pad = (-total) % PACK_TILE
    if pad:
        flat.append(jnp.zeros((pad,), dtype))
    offs = np.cumsum([0] + sizes)[:-1].tolist()
    return jnp.concatenate(flat).reshape(-1, 128), offs


def unpack_flat(buf, offs, shapes):
    flat = buf.reshape(-1)
    return [flat[o:o + int(np.prod(s))].reshape(s) for o, s in zip(offs, shapes)]


def s5_forward(p1, n_ctx, dirs):
    saved = []
    y = None
    ctx_rows, lat_rows = (0, n_ctx), (n_ctx, p1.shape[0] - n_ctx)
    zeros_tile = jnp.zeros((2, N_SEG, S5_LANES), F32)
    zeros_row = jnp.zeros((2, 1, S5_LANES), F32)
    for k, (lam, b_re, b_im, c_re, c_im) in enumerate(dirs):
        rev = k == 1
        last = 0 if rev else N_SEG - 1
        _, _, fin = s5_scan(p1, b_re, b_im, lam, zeros_tile, reverse=rev, rows=ctx_rows, name=f"s5_ctx_finals{k}")
        carry_c = s5_chain(fin, zeros_row, lam, n_ctx // N_SEG, rev, name=f"s5_ctx_chain{k}")
        _, ck_c, fin_c = s5_scan(p1, b_re, b_im, lam, carry_c, reverse=rev, want_ckpt=True, rows=ctx_rows,
                                 name=f"s5_ctx_scan{k}")
        s0 = fin_c[:, last:last + 1, :]
        _, _, fin = s5_scan(p1, b_re, b_im, lam, zeros_tile, reverse=rev, rows=lat_rows, name=f"s5_lat_finals{k}")
        carry_l = s5_chain(fin, s0, lam, lat_rows[1] // N_SEG, rev, name=f"s5_lat_chain{k}")
        y, ck_l, _ = s5_scan(p1, b_re, b_im, lam, carry_l, reverse=rev, c_re=c_re, c_im=c_im, add=y,
                             want_ckpt=True, rows=lat_rows, name=f"s5_lat_scan{k}")
        saved.append((ck_c, ck_l))
    return y, saved


def s5_backward(dy_l, du_extra_l, p1, n_ctx, dirs, saved):
    n_lat = p1.shape[0] - n_ctx
    zeros_tile = jnp.zeros((2, N_SEG, S5_LANES), F32)
    zeros_row = jnp.zeros((2, 1, S5_LANES), F32)
    dy_c = jnp.zeros((n_ctx, D_MODEL), F32)
    du_l, du_c = du_extra_l, None
    grads = []
    for k, (lam, b_re, b_im, c_re, c_im) in enumerate(dirs):
        rev = k == 1
        lam_c = conj(lam)
        ck_c, ck_l = saved[k]
        first = N_SEG - 1 if rev else 0
        _, _, fin = s5_scan(dy_l, c_re, c_im, lam_c, zeros_tile, reverse=not rev, adjoint=True,
                            name=f"s5_lat_adj_finals{k}")
        carry = s5_chain(fin, zeros_row, lam_c, n_lat // N_SEG, not rev, name=f"s5_lat_adj_chain{k}")
        whole = k == len(dirs) - 1
        du_l, dlam_l, dbr_l, dbi_l, dcr_l, dci_l, fin_a = s5_grads(
            dy_l, p1, ck_l, b_re, b_im, c_re, c_im, lam, carry, reverse=rev, add=du_l, u_off=n_ctx,
            du_rows=(p1.shape[0], n_ctx) if whole else None, name=f"s5_lat_grads{k}")
        g0 = fin_a[:, first:first + 1, :]
        carry = s5_chain(zeros_tile, g0, lam_c, n_ctx // N_SEG, not rev, name=f"s5_ctx_adj_chain{k}")
        du_c, dlam_c, dbr_c, dbi_c, _, _, _ = s5_grads(
            dy_c, p1, ck_c, b_re, b_im, c_re, c_im, lam, carry, reverse=rev, add=du_c,
            du_rows=(p1.shape[0], 0) if whole else None, du_into=du_l if whole else None, name=f"s5_ctx_grads{k}")
        dlam = jnp.sum(dlam_l + dlam_c, axis=1)
        grads.append((dlam, b_block_diag(dbr_l + dbr_c), b_block_diag(dbi_l + dbi_c),
                      c_block_diag(dcr_l), c_block_diag(dci_l)))
    return du_c, grads


def local_step(x, ctx, target, mod, w, late=None, reducer=None):
    L, Lc = x.shape[0], ctx.shape[0]
    T = L + Lc
    assert L % Lc == 0 and Lc % (2 * N_SEG) == 0 and L % GRID_W == 0
    D = D_MODEL
    X0 = (ctx, x)

    def mod_of(i, j):
        return mod[i, :, j, :][:, None, :]

    def vec(v):
        return v.reshape(1, 1, -1).astype(F32)

    g0 = vec(w["norm_g"][0])
    H0, p0 = norm_proj(X0, g0, mod_of(0, 1), mod_of(0, 0), w["mla_w_in"], Lc, "l0_norm_in")
    cq = Rows(p0, Q_LORA_RANK, col_blk=D // Q_LORA_RANK)
    ckv = Rows(p0, KV_LORA_RANK, col_blk=(D + Q_LORA_RANK) // KV_LORA_RANK)
    kr = p0[:, D + Q_LORA_RANK + KV_LORA_RANK:D + P0_HEAD]
    qng, kvng = vec(w["mla_q_norm"]), vec(w["mla_kv_norm"])
    tabs = rope_tables(Lc, L)
    spread = rotary_spread()
    if late is not None:
        w = {**w, **late["qkv"](p0)}
    w_uq_p = pad_heads(w["mla_w_uq"], QK_DIM)
    w_ukv3 = w["mla_w_ukv"].reshape(KV_LORA_RANK, MLA_HEADS, QK_NOPE_DIM + V_HEAD_DIM)
    w_kn_p = pad_heads(w_ukv3[:, :, :QK_NOPE_DIM].reshape(KV_LORA_RANK, -1), QK_NOPE_DIM)
    w_v = w_ukv3[:, :, QK_NOPE_DIM:].reshape(KV_LORA_RANK, -1)
    qb, qn = q_heads(cq, qng, w_uq_p, tabs)
    kb, vb, kvn = kv_heads(ckv, kvng, w_kn_p, w_v, kr, spread, tabs)
    o, lse, qs = attn_fwd(qb, kb, vb, Lc)
    if late is not None:
        w = {**w, **late["out"](o)}
    X1, og, out0 = mla_post_fwd(o, p0, X0, mod_of(0, 2), w["mla_w_out"], Lc)

    if late is not None:
        w = {**w, **late["l1"](X1)}
    X1p = to_segments(X1, Lc)
    tgt_p = to_segments(target, 0)
    g1 = vec(w["norm_g"][1])
    H1, p1 = norm_proj(X1p, g1, mod_of(1, 1), mod_of(1, 0), w["s5_w_in"], Lc, "l1_norm_in")
    disc_fn = lambda *a: tuple(zip(*[s5_discretise(a[0][k], a[1][k], a[2][k], a[3][k], a[4][k]) for k in range(2)]))
    disc, disc_vjp = jax.vjp(disc_fn, w["s5_a_re"], w["s5_a_im"], w["s5_log_step"], _stored(w["s5_b_re"]),
                             _stored(w["s5_b_im"]))
    dirs = [s5_block_weights(disc[0][k], disc[1][k], disc[2][k], disc[3][k], w["s5_c_re"][k], w["s5_c_im"][k])
            for k in range(2)]
    y_ssm, s5_saved = s5_forward(p1, Lc, dirs)

    row = lambda v: v.reshape(1, D).astype(F32)
    (lvec, dX2, d_yssm, d_u_act, d_z1, d_fg, d_gt1, d_bg, d_d, gw_glu, gw_out) = s5_tail(
        y_ssm, p1, X1p, tgt_p, Lc, row(w["s5_d"]), row(w["s5_b_glu"]), mod[1, 1:2, 2, :], row(w["final_g"]),
        w["s5_w_glu"], w["s5_w_out"])
    loss = jnp.sum(lvec)
    gw = {"final_g": d_fg.reshape(D), "s5_b_glu": d_bg.reshape(D), "s5_d": d_d.reshape(D),
          "s5_w_glu": gw_glu, "s5_w_out": gw_out}
    dmod = {}

    du_p, s5_g = s5_backward(d_yssm, d_u_act, p1, Lc, dirs, s5_saved)
    d_disc = tuple(tuple(s5_g[k][j - 1].reshape(disc[j][k].shape) if j >= 2 else
                         s5_g[k][0][j].reshape(disc[j][k].shape) for k in range(2)) for j in range(4))
    gw["s5_a_re"], gw["s5_a_im"], gw["s5_log_step"], d_bt_re, d_bt_im = disc_vjp(d_disc)
    gw["s5_b_re"], gw["s5_b_im"] = jnp.swapaxes(d_bt_re, -1, -2), jnp.swapaxes(d_bt_im, -1, -2)
    gw["s5_c_re"] = jnp.stack([s5_g[0][3], s5_g[1][3]])
    gw["s5_c_im"] = jnp.stack([s5_g[0][4], s5_g[1][4]])
    gw["s5_w_in"] = jnp.concatenate([mm_tn(H1, du_p, name="l1_in_dw_u"), mm_tn(H1, d_z1, name="l1_in_dw_z")],
                                    axis=1)
    if reducer is not None:
        g1 = g1 + reducer["l1"][0]({n: gw.pop(n) for n in LAYER1_MATS})[0, 0]
    d_X1p, d_g1, d_sc1, d_sh1 = norm_proj_bwd([(du_p, 0, 0), (d_z1, D, Lc)], w["s5_w_in"], X1p, g1, mod_of(1, 1),
                                              mod_of(1, 0), dX2, Lc, "l1_norm_in_bwd")
    d_gt1_full = jnp.concatenate([jnp.zeros((1, 1, D), F32), d_gt1[None]], axis=0)
    dmod[1] = (d_sh1, d_sc1, d_gt1_full)
    d_X1 = from_segments(d_X1p, Lc)

    d_o, d_z0, d_gt0, gw["mla_w_out"] = mla_post_bwd(d_X1, out0, og, o, p0, mod_of(0, 2), w["mla_w_out"], Lc)
    if reducer is not None:
        started = reducer["l1"][1](d_o)[0, 0] + reducer["out"][0]({"mla_w_out": gw.pop("mla_w_out")})[0, 0]
        tabs = (tabs[0] + started,) + tabs[1:]
    d_q, dos = attn_bwd_dq(qb, kb, vb, o, d_o, lse, tabs, Lc)
    dk_p, d_v = attn_bwd_dkv(qs, kb, vb, dos, Lc)
    if reducer is not None:
        tabs = (tabs[0] + reducer["out"][1](d_q)[0, 0],) + tabs[1:]
    d_k, d_kr = heads_unrope(dk_p, tabs, jnp.pad(spread, ((0, HEAD_LANES - QK_ROPE_DIM), (0, 0))),
                             name="l0_k_unrope")
    d_qn = mm_nt(d_q, w_uq_p, name="l0_uq_dx")
    gw["mla_w_uq"] = unpad_heads(mm_tn(qn, d_q, name="l0_uq_dw"), QK_DIM)
    d_kvn = mm_nt(d_k, w_kn_p, name="l0_ukn_dx") + mm_nt(d_v, w_v, name="l0_uv_dx")
    dw_kn = unpad_heads(mm_tn(kvn, d_k, name="l0_ukn_dw"), QK_NOPE_DIM).reshape(KV_LORA_RANK, MLA_HEADS, QK_NOPE_DIM)
    dw_v = mm_tn(kvn, d_v, name="l0_uv_dw").reshape(KV_LORA_RANK, MLA_HEADS, V_HEAD_DIM)
    gw["mla_w_ukv"] = jnp.concatenate([dw_kn, dw_v], axis=-1).reshape(KV_LORA_RANK, -1)
    d_cq, d_qng = rowwise_bwd(f_rms, [cq], [qng], [d_qn], [0], [0], T, 0, "l0_qnorm_bwd")
    d_ckv, d_kvng = rowwise_bwd(f_rms, [ckv], [kvng], [d_kvn], [0], [0], T, 0, "l0_kvnorm_bwd")
    gw["mla_q_norm"] = d_qng.reshape(-1)
    gw["mla_kv_norm"] = d_kvng.reshape(-1)
    o_cq, o_ckv = D, D + Q_LORA_RANK
    o_kr = o_ckv + KV_LORA_RANK
    d_head = jnp.concatenate([d_cq, d_ckv, d_kr], axis=1)
    gw["mla_w_in"] = jnp.concatenate([mm_tn(H0, d_head, name="l0_in_dw_head")[:, :P0_HEAD],
                                      mm_tn(H0, d_z0, name="l0_in_dw_z")], axis=1)
    dx, d_g0, d_sc0, d_sh0 = norm_proj_bwd(
        [(d_z0, 0, 0), (d_cq, o_cq, 0), (d_ckv, o_ckv, 0), (d_kr, o_kr, 0)], w["mla_w_in"], X0, g0, mod_of(0, 1),
        mod_of(0, 0), d_X1, Lc, "l0_norm_in_bwd", latent_dx_only=True)
    dmod[0] = (d_sh0, d_sc0, d_gt0)
    gw["norm_g"] = jnp.stack([d_g0.reshape(D), d_g1.reshape(D)])
    dmod_arr = jnp.stack([jnp.stack([dmod[i][j][:, 0, :] for j in range(3)], axis=1) for i in range(2)])
    ready = {**reducer["l1"][2](dx), **reducer["out"][2](dx)} if reducer is not None else {}
    return loss, dx, dmod_arr, gw, ready


SHARDED = {
    "mla_w_in": 1, "mla_w_uq": 1, "mla_w_ukv": 1, "mla_w_out": 0,
    "s5_w_in": 1, "s5_w_glu": 0, "s5_w_out": 0, "s5_d": 0, "s5_b_glu": 0,
}
SHARDED_MATS = ["mla_w_in", "mla_w_uq", "mla_w_ukv", "mla_w_out", "s5_w_in", "s5_w_glu", "s5_w_out"]
SHARDED_VECS = ["s5_d", "s5_b_glu"]
REPLICATED = ["norm_g", "mla_q_norm", "mla_kv_norm", "s5_a_re", "s5_a_im", "s5_log_step", "s5_b_re", "s5_b_im",
              "s5_c_re", "s5_c_im", "final_g"]
WEIGHT_ORDER = ["c_ctx", "ada_w", "ada_b", "norm_g", "mla_w_in", "mla_q_norm", "mla_w_uq", "mla_kv_norm", "mla_w_ukv",
                "mla_w_out", "s5_w_in", "s5_a_re", "s5_a_im", "s5_log_step", "s5_b_re", "s5_b_im", "s5_c_re", "s5_c_im",
                "s5_d", "s5_w_glu", "s5_b_glu", "s5_w_out", "final_g"]


P0_HEAD = Q_LORA_RANK + KV_LORA_RANK + QK_ROPE_DIM


P0_WIDTH = 1536


def w_in_to_kernel_order(w):
    pad = jnp.zeros((w.shape[0], P0_WIDTH - w.shape[1]), w.dtype)
    return jnp.concatenate([w[:, P0_HEAD:], w[:, :P0_HEAD], pad], axis=1)


LAYER0_MATS = ["mla_w_in", "mla_w_uq", "mla_w_ukv", "mla_w_out"]
LAYER1_MATS = ["s5_w_in", "s5_w_glu", "s5_w_out"]


def _whole_matrices(names, own_blocks, gathered):
    chip = _chip_index(_coords())
    full = {}
    for n, own, o in zip(names, own_blocks, gathered):
        slot = lax.broadcasted_iota(jnp.int32, (N_CHIP, 1, 1), 0)
        o = jnp.where(slot == chip, own[None], o.reshape((N_CHIP,) + own.shape))
        full[n] = o.reshape(-1, o.shape[-1]) if SHARDED[n] == 0 else o.transpose(1, 0, 2).reshape(o.shape[1], -1)
    return full


FIRST_MATS = ["mla_w_in"]
LATER_GROUPS = {"qkv": ["mla_w_uq", "mla_w_ukv"], "out": ["mla_w_out"], "l1": LAYER1_MATS}


def gather_weights(ws):
    mats = [ws[n].astype(BF16) for n in FIRST_MATS]
    full = _whole_matrices(FIRST_MATS, mats, gather_halves(mats, "gather_weights"))
    full["mla_w_in"] = w_in_to_kernel_order(full["mla_w_in"])
    return full


def gather_weights_behind(ws, after):
    token, finish = 0.0, {}
    for group, names in LATER_GROUPS.items():
        mats = [ws[n].astype(BF16) for n in names]
        flight, tok = exchange_start(
            mats, [jax.ShapeDtypeStruct((N_CHIP,) + m.shape, m.dtype) for m in mats],
            [(f, a, lambda me, peer: None, a, lambda s: (_chip_index(s),))
             for a in range(len(mats)) for f in CHIP_FLIPS], f"gather_{group}_start", after=after)
        after = [tok]
        token = token + tok[0, 0]

        def finish_group(after_work, group=group, names=names, flight=flight):
            own, got = exchange_wait(flight, after_work, f"gather_{group}_wait")
            return _whole_matrices(names, own, got)

        finish[group] = finish_group
    return token, finish


def _grad_slots(gw, names):
    slots = []
    for n in names:
        g = gw[n]
        if SHARDED[n] == 0:
            slots.append(g.reshape(N_CHIP, 2, g.shape[0] // (2 * N_CHIP), g.shape[1]))
        else:
            k, n4 = g.shape
            slots.append(g.reshape(k, N_CHIP, n4 // N_CHIP).transpose(1, 0, 2)
                         .reshape(N_CHIP, 2, k // 2, n4 // N_CHIP))
    return slots


def _to_sibling_half(count):
    return [(CORE_FLIP, i, lambda me, peer: (slice(None), 1 - me[2]), i, lambda s: None) for i in range(count)]


def _to_chips(count):
    return [(f, i, lambda me, peer: (_chip_index(peer),), i, lambda s: (_chip_index(s),))
            for i in range(count) for f in CHIP_FLIPS]


def _place():
    me = _coords()
    return jnp.stack([me[2], _chip_index(me)]).astype(jnp.int32)


def reduce_behind(names, tag):
    state = {}
    count = len(names)

    def begin(gw):
        slots = _grad_slots(gw, names)
        lands = [jax.ShapeDtypeStruct((N_CHIP,) + s.shape[2:], F32) for s in slots]
        state["in"], token = exchange_start(slots, lands, _to_sibling_half(count), f"grads_{tag}_swap_in_start")
        return token

    def middle(after):
        slots, got = exchange_wait(state["in"], after, f"grads_{tag}_swap_in_wait")
        place = _place()
        sums = [pair_add(s, g, place[:1], BF16, f"grads_pair_{n}") for n, s, g in zip(names, slots, got)]
        lands = [jax.ShapeDtypeStruct(s.shape, s.dtype) for s in sums]
        state["out"], token = exchange_start(sums, lands, _to_chips(count), f"grads_{tag}_scatter_start")
        return token

    def end(after):
        sums, parts = exchange_wait(state["out"], after, f"grads_{tag}_scatter_wait")
        place = _place()
        return {n: sum_chips(p, s, place, f"grads_sum_{n}") for n, p, s in zip(names, parts, sums)}

    return begin, middle, end


def reduce_gradients(gw, ready_halves):
    me = _coords()
    place = _place()
    mat_names = [n for n in SHARDED_MATS if n not in ready_halves]
    slots = dict(zip(mat_names, _grad_slots(gw, mat_names)))
    small_names = REPLICATED + SHARDED_VECS
    small, small_offs = pack_flat([_stored(gw[n]).astype(F32) for n in small_names], F32)
    small = jnp.pad(small, ((0, (-small.shape[0]) % (N_CHIP * 32)), (0, 0)))
    slots["small"] = small.reshape(N_CHIP, 2, -1, 128)
    names = list(slots)
    count = len(names)
    got = exchange([slots[n] for n in names],
                   [jax.ShapeDtypeStruct((N_CHIP,) + slots[n].shape[2:], F32) for n in names],
                   _to_sibling_half(count), [], "grads_swap_in")
    sums = [pair_add(slots[n], g, place[:1], F32 if n == "small" else BF16, f"grads_pair_{n}")
            for n, g in zip(names, got)]
    parts = exchange(sums, [jax.ShapeDtypeStruct(s.shape, s.dtype) for s in sums], _to_chips(count), [],
                     "grads_scatter")
    halves = {n: sum_chips(p, s, place, f"grads_sum_{n}") for n, p, s in zip(names, parts, sums)}
    halves.update(ready_halves)
    all_names = list(halves)
    fulls = exchange(
        [halves[n] for n in all_names], [jax.ShapeDtypeStruct(halves[n].shape, F32) for n in all_names],
        [(CORE_FLIP, i, lambda me, peer: (me[2],), i, lambda s: (s[2],)) for i in range(len(all_names))], [],
        "grads_swap_out", aliases={i: i for i in range(len(all_names))})
    out = {n: f.reshape(-1, f.shape[-1]) for n, f in zip(all_names, fulls)}
    quarter = out.pop("small")
    gather, token = exchange_start(
        [quarter], [jax.ShapeDtypeStruct((N_CHIP,) + quarter.shape, F32)],
        [(f, 0, lambda me, peer: None, 0, lambda s: (_chip_index(s),)) for f in CHIP_FLIPS],
        "grads_gather_small_start")

    def finish_small(after):
        (own,), (got_small,) = exchange_wait(gather, after, "grads_gather_small_wait")
        slot = lax.broadcasted_iota(jnp.int32, (N_CHIP, 1, 1), 0)
        small_all = jnp.where(slot == _chip_index(me), own[None], got_small)
        vals = unpack_flat(small_all, small_offs, [_stored_shape(gw[n].shape) for n in small_names])
        res = {}
        for n, v in zip(small_names, vals):
            v = _from_stored(v, gw[n].shape)
            if n in SHARDED_VECS:
                size = v.shape[0] // N_CHIP
                v = lax.dynamic_slice_in_dim(v, _chip_index(me) * size, size)
            res[n] = v
        return res

    return out, finish_small, token


def kernel(x, c, ctx, c_ctx, ada_w, ada_b, norm_g, mla_w_in, mla_q_norm, mla_w_uq, mla_kv_norm, mla_w_ukv, mla_w_out, s5_w_in, s5_a_re, s5_a_im, s5_log_step, s5_b_re, s5_b_im, s5_c_re, s5_c_im, s5_d, s5_w_glu, s5_b_glu, s5_w_out, final_g, loss_target, m_c_ctx, m_ada_w, m_ada_b, m_norm_g, m_mla_w_in, m_mla_q_norm, m_mla_w_uq, m_mla_kv_norm, m_mla_w_ukv, m_mla_w_out, m_s5_w_in, m_s5_a_re, m_s5_a_im, m_s5_log_step, m_s5_b_re, m_s5_b_im, m_s5_c_re, m_s5_c_im, m_s5_d, m_s5_w_glu, m_s5_b_glu, m_s5_w_out, m_final_g, v_c_ctx, v_ada_w, v_ada_b, v_norm_g, v_mla_w_in, v_mla_q_norm, v_mla_w_uq, v_mla_kv_norm, v_mla_w_ukv, v_mla_w_out, v_s5_w_in, v_s5_a_re, v_s5_a_im, v_s5_log_step, v_s5_b_re, v_s5_b_im, v_s5_c_re, v_s5_c_im, v_s5_d, v_s5_w_glu, v_s5_b_glu, v_s5_w_out, v_final_g):
    args = dict(locals())
    weights = {n: args[n] for n in WEIGHT_ORDER}
    D = D_MODEL
    xi, yi, ci = _coords()
    chip = 2 * xi + yi
    me = 4 * xi + 2 * yi + ci
    n_col = ada_w.shape[2]

    c_all = allgather_devices(jnp.pad(c, ((0, 7), (0, 0))), "gather_c")[:, 0, :]
    cond = jnp.concatenate([c_all, jnp.broadcast_to(c_ctx[None], (8, D))], axis=0)
    (s_cond,) = rowwise_fwd(lambda v: (_silu(v),), [cond], [], [D], [F32], 16, 0, "cond_silu")
    ada_rows = ada_w.reshape(2 * D, n_col)
    mod_cols = jnp.stack([mm_nn(s_cond, ada_rows, name=f"mod_proj{i}", b_blk=i) for i in range(2)])
    vec_tiles = [jnp.pad(weights[n][0].reshape(-1, 128), ((0, 6), (0, 0))) for n in SHARDED_VECS]
    mod_all, *vec_all = allgather_chips([mod_cols] + vec_tiles, "gather_mod")
    mod_all = mod_all.transpose(1, 2, 0, 3).reshape(2, 16, 3 * D) + ada_b[:, None, :]
    mine = lax.broadcasted_iota(jnp.int32, (1, 16, 1), 1) == me
    mod_l = jnp.sum(jnp.where(mine, mod_all, 0.0), axis=1)
    mod_c = mod_all[:, 8, :]
    mod = jnp.stack([mod_c.reshape(2, 3, D), mod_l.reshape(2, 3, D)], axis=1)

    w = gather_weights({n: weights[n][0] for n in FIRST_MATS})
    token, late = gather_weights_behind({n: weights[n][0] for n in SHARDED_MATS}, [w["mla_w_in"], mod])
    for n, v in zip(SHARDED_VECS, vec_all):
        w[n] = v[:, :2, :].reshape(-1)
    for n in ["norm_g", "final_g"]:
        w[n] = weights[n]
    for n in ["mla_q_norm", "mla_kv_norm", "s5_a_re", "s5_a_im", "s5_log_step", "s5_b_re", "s5_b_im",
              "s5_c_re", "s5_c_im"]:
        w[n] = weights[n][0]

    reducer = {"l1": reduce_behind(LAYER1_MATS, "l1"), "out": reduce_behind(["mla_w_out"], "out")}
    loss_me, dx, dmod, gw, ready = local_step(x[0], ctx[0], loss_target[0], mod + token, w, late,
                                              reducer)

    dmod_rows, loss_all = _gather([dmod.reshape(2, 2, 3 * D), jnp.broadcast_to(loss_me, (8, 128))],
                                  ALL_FLIPS, _dev_index, N_DEV, "gather_dmod")
    loss = functools.reduce(lambda s, d: s + loss_all[d, 0, 0], range(1, N_DEV), loss_all[0, 0, 0])
    dm = jnp.concatenate([dmod_rows[:, :, 1, :], dmod_rows[:, :, 0, :]], axis=0).transpose(1, 0, 2)
    g_ada_b = jnp.sum(dm, axis=1)
    dm_cols = lax.dynamic_slice_in_dim(dm, chip * n_col, n_col, axis=2)
    g_ada_w = jnp.stack([mm_tn(s_cond, dm_cols[i], name=f"mod_proj_dw{i}") for i in range(2)])
    dmc = jnp.sum(dm_cols[:, 8:, :], axis=1)
    dmc8 = jnp.broadcast_to(dmc[:, None, :], (2, 8, n_col))
    g_sc = (mm_nt(dmc8[0], ada_rows, name="mod_proj_dx0", b_rows=D, b_blk=0)[0]
            + mm_nt(dmc8[1], ada_rows, name="mod_proj_dx1", b_rows=D, b_blk=1)[0])
    g_sc_all = allgather_devices(jnp.broadcast_to(g_sc[None], (8, D)), "gather_dcond")[:, 0, :]
    g_silu_cc = g_sc_all[0] + g_sc_all[2] + g_sc_all[4] + g_sc_all[6]
    (g_c_ctx,) = rowwise_bwd(lambda v: (_silu(v),), [jnp.broadcast_to(c_ctx[None], (8, D))], [],
                             [jnp.broadcast_to(g_silu_cc[None], (8, D))], [0], [], 8, 0, "cond_silu_bwd")
    g_c_ctx = g_c_ctx[0]

    grads = {"c_ctx": g_c_ctx, "ada_w": g_ada_w, "ada_b": g_ada_b}
    deltas, new_m, new_v = {}, {}, {}
    small = [n for n in WEIGHT_ORDER if weights[n].size < 50000]

    def update(n, after=None):
        shp = weights[n].shape
        rows = lambda a: _stored(a.reshape(shp)).reshape(-1, _stored_shape(shp)[-1])
        back = lambda a: _from_stored(a.reshape(_stored_shape(shp)), shp)
        d_, m_, v_ = adamw(rows(weights[n]), rows(grads[n]), rows(args["m_" + n]), rows(args["v_" + n]),
                           name=f"adamw_{n}", after=after)
        deltas[n], new_m[n], new_v[n] = back(d_), back(m_), back(v_)

    red, finish_small, small_started = reduce_gradients(gw, ready)
    update("ada_w", small_started)
    for n in SHARDED_MATS:
        grads[n] = red[n].reshape(weights[n].shape)
        update(n, small_started)
    red_small = finish_small([deltas[n] for n in ["ada_w"] + SHARDED_MATS])
    for n in REPLICATED + SHARDED_VECS:
        grads[n] = red_small[n].reshape(weights[n].shape)
    for n in WEIGHT_ORDER:
        if n not in small and n not in deltas:
            update(n)
    packs = []
    offs = None
    for src in (weights, grads, {n: args["m_" + n] for n in small}, {n: args["v_" + n] for n in small}):
        buf, offs = pack_flat([src[n] for n in small], F32)
        packs.append(buf)
    outs = adamw(*packs, name="adamw_small")
    for res, dst in zip(outs, (deltas, new_m, new_v)):
        for n, val in zip(small, unpack_flat(res, offs, [weights[n].shape for n in small])):
            dst[n] = val

    return (loss, dx[None], *[grads[n] for n in WEIGHT_ORDER], *[deltas[n] for n in WEIGHT_ORDER],
            *[new_m[n] for n in WEIGHT_ORDER], *[new_v[n] for n in WEIGHT_ORDER])
```

```python
import functools
import math

import jax
import jax.numpy as jnp
import numpy as np
from jax import lax
from jax.experimental import pallas as pl
from jax.experimental.pallas import tpu as pltpu

F32 = jnp.float32
BF16 = jnp.bfloat16

D_MODEL = 1024
GRID_W = 64
EPS = 1e-6
MLA_HEADS = 16
QK_NOPE_DIM = 64
QK_ROPE_DIM = 32
V_HEAD_DIM = 64
Q_LORA_RANK = 256
KV_LORA_RANK = 128
QK_DIM = QK_NOPE_DIM + QK_ROPE_DIM
SOFTMAX_SCALE = QK_DIM ** -0.5
ROPE_THETA = 10000.0
S5_GROUP = 16
S5_GROUPS = D_MODEL // S5_GROUP
S5_STATE = 64
S5_LANES = S5_GROUPS * S5_STATE
N_SEG = 8
GROUPS_PER_BLOCK = 8
N_BLOCKS = S5_GROUPS // GROUPS_PER_BLOCK
BLK_CH = GROUPS_PER_BLOCK * S5_GROUP
BLK_ST = GROUPS_PER_BLOCK * S5_STATE

ADAM_LR = 0.001
ADAM_B1 = 0.9
ADAM_B2 = 0.999
ADAM_EPS = 1e-08
ADAM_WD = 0.01
ADAM_STEP = 10

N_DEV = 8
N_CHIP = 4
MESH = pl.DeviceIdType.MESH
VMEM_LIMIT = 52 * 1024 * 1024
ROW_TILE = 256


def _params(sem=None, vmem=None):
    return pltpu.CompilerParams(dimension_semantics=sem, vmem_limit_bytes=vmem)


def mm_nn(a, b, out_dtype=F32, name="mm_nn", b_blk=0):
    M, K = a.shape
    N = b.shape[1]
    tm = math.gcd(ROW_TILE, M)

    def body(a_ref, b_ref, o_ref):
        o_ref[...] = jnp.dot(a_ref[...].astype(BF16), b_ref[...].astype(BF16),
                             preferred_element_type=F32).astype(o_ref.dtype)

    return pl.pallas_call(
        body, out_shape=jax.ShapeDtypeStruct((M, N), out_dtype), grid=(M // tm,),
        in_specs=[pl.BlockSpec((tm, K), lambda i: (i, 0)), pl.BlockSpec((K, N), lambda i: (b_blk, 0))],
        out_specs=pl.BlockSpec((tm, N), lambda i: (i, 0)),
        compiler_params=_params(("parallel",), VMEM_LIMIT), name=name)(a, b)


def mm_nt(a, b, out_dtype=F32, name="mm_nt", b_rows=None, b_blk=0):
    M, N = a.shape
    K = b.shape[0] if b_rows is None else b_rows
    tm = math.gcd(ROW_TILE, M)

    def body(a_ref, b_ref, o_ref):
        o_ref[...] = lax.dot_general(a_ref[...].astype(BF16), b_ref[...].astype(BF16),
                                     (((1,), (1,)), ((), ())),
                                     preferred_element_type=F32).astype(o_ref.dtype)

    return pl.pallas_call(
        body, out_shape=jax.ShapeDtypeStruct((M, K), out_dtype), grid=(M // tm,),
        in_specs=[pl.BlockSpec((tm, N), lambda i: (i, 0)), pl.BlockSpec((K, N), lambda i: (b_blk, 0))],
        out_specs=pl.BlockSpec((tm, K), lambda i: (i, 0)),
        compiler_params=_params(("parallel",), VMEM_LIMIT), name=name)(a, b)


def mm_tn(a, b, name="mm_tn"):
    M, N = b.shape
    K = a.shape[1]
    skip = a.shape[0] - M
    tn = math.gcd(512, N) if N % 128 == 0 and N > 512 else N

    def body(a_ref, b_ref, o_ref):
        o_ref[...] = lax.dot_general(a_ref[skip:, :].astype(BF16), b_ref[...].astype(BF16),
                                     (((0,), (0,)), ((), ())), preferred_element_type=F32)

    return pl.pallas_call(
        body, out_shape=jax.ShapeDtypeStruct((K, N), F32), grid=(N // tn,),
        in_specs=[pl.BlockSpec(a.shape, lambda j: (0, 0)), pl.BlockSpec((M, tn), lambda j: (0, j))],
        out_specs=pl.BlockSpec((K, tn), lambda j: (0, j)),
        compiler_params=_params(("parallel",), VMEM_LIMIT), name=name)(a, b)


class Rows:
    def __init__(self, arr, width=None, row_off=0, col_blk=0):
        self.arr = arr
        self.width = arr.shape[1] if width is None else width
        self.row_off = row_off
        self.col_blk = col_blk

    def spec(self, tm):
        ro, cb = self.row_off // tm, self.col_blk
        return pl.BlockSpec((tm, self.width), lambda i: (i + ro, cb))


def _as_rows(x):
    return x if isinstance(x, Rows) else Rows(x)


def _row_tile(n_rows, n_ctx_rows, rows):
    tm = math.gcd(ROW_TILE, n_rows, n_ctx_rows)
    for r in rows:
        tm = math.gcd(tm, r.row_off)
    return tm


def _bc_spec(arr, n_ctx_blocks):
    g, _, d = arr.shape
    if g == 1:
        return pl.BlockSpec((1, 1, d), lambda i: (0, 0, 0))
    return pl.BlockSpec((1, 1, d), lambda i: ((i >= n_ctx_blocks).astype(jnp.int32), 0, 0))


def rowwise_fwd(fn, rows, bcs, out_dims, out_dtypes, n_rows, n_ctx_rows, name):
    rows = [_as_rows(r) for r in rows]
    tm = _row_tile(n_rows, n_ctx_rows, rows)
    ncb = n_ctx_rows // tm
    nr, nb = len(rows), len(bcs)

    def body(*refs):
        vals = [r[...].astype(F32) for r in refs[:nr]] + [b[0].astype(F32) for b in refs[nr:nr + nb]]
        outs = fn(*vals)
        for o_ref, v in zip(refs[nr + nb:], outs):
            o_ref[...] = v.astype(o_ref.dtype)

    outs = pl.pallas_call(
        body,
        out_shape=[jax.ShapeDtypeStruct((n_rows, d), dt) for d, dt in zip(out_dims, out_dtypes)],
        grid=(n_rows // tm,),
        in_specs=[r.spec(tm) for r in rows] + [_bc_spec(b, ncb) for b in bcs],
        out_specs=[pl.BlockSpec((tm, d), lambda i: (i, 0)) for d in out_dims],
        compiler_params=_params(("parallel",), VMEM_LIMIT), name=name)(*[r.arr for r in rows], *bcs)
    return outs


def rowwise_bwd(fn, rows, bcs, cts, diff_rows, diff_bcs, n_rows, n_ctx_rows, name, ct_extra=None, lat_add=None):
    rows = [_as_rows(r) for r in rows]
    cts = [_as_rows(c) for c in cts]
    extra = [_as_rows(ct_extra)] if ct_extra is not None else []
    tm = _row_tile(n_rows, n_ctx_rows, rows + cts + extra)
    ncb = n_ctx_rows // tm
    nr, nb, nc = len(rows), len(bcs), len(cts)
    ndr, ndb = len(diff_rows), len(diff_bcs)
    n_in = nr + nb + nc + len(extra) + (lat_add is not None)

    def body(*refs):
        i = pl.program_id(0)
        rvals = [r[...].astype(F32) for r in refs[:nr]]
        bvals = [b[0].astype(F32) for b in refs[nr:nr + nb]]
        cvals = [c[...].astype(F32) for c in refs[nr + nb:nr + nb + nc]]
        if extra:
            cvals[0] = cvals[0] + refs[nr + nb + nc][...].astype(F32)
        outs = refs[n_in:]

        def f(*d):
            rv, bv = list(rvals), list(bvals)
            for k, idx in enumerate(diff_rows):
                rv[idx] = d[k]
            for k, idx in enumerate(diff_bcs):
                bv[idx] = d[ndr + k]
            return tuple(fn(*rv, *bv))

        primals = [rvals[k] for k in diff_rows] + [bvals[k] for k in diff_bcs]
        _, vjp = jax.vjp(f, *primals)
        grads = list(vjp(tuple(cvals)))
        if lat_add is not None:
            add = refs[n_in - 1][...]
            grads[0] = grads[0] + (add if lat_add.shape[0] == n_rows else jnp.where(i >= ncb, add, 0.0))
        for k in range(ndr):
            outs[k][...] = grads[k].astype(outs[k].dtype)
        for k, idx in enumerate(diff_bcs):
            o_ref = outs[ndr + k]
            first = (i == 0)
            if bcs[idx].shape[0] == 2:
                first = first | (i == ncb)

            @pl.when(first)
            def _(o_ref=o_ref):
                o_ref[...] = jnp.zeros_like(o_ref)

            o_ref[0] += grads[ndr + k]

    out_shape = [jax.ShapeDtypeStruct((n_rows, rows[k].width), F32) for k in diff_rows]
    out_shape += [jax.ShapeDtypeStruct(bcs[k].shape, F32) for k in diff_bcs]
    out_specs = [pl.BlockSpec((tm, rows[k].width), lambda i: (i, 0)) for k in diff_rows]
    out_specs += [_bc_spec(bcs[k], ncb) for k in diff_bcs]
    ins = [r.arr for r in rows] + list(bcs) + [c.arr for c in cts + extra]
    in_specs = [r.spec(tm) for r in rows] + [_bc_spec(b, ncb) for b in bcs] + [c.spec(tm) for c in cts + extra]
    if lat_add is not None:
        ins.append(lat_add)
        skip = ncb if lat_add.shape[0] != n_rows else 0
        in_specs.append(pl.BlockSpec((tm, lat_add.shape[1]), lambda i: (jnp.maximum(i - skip, 0), 0)))
    outs = pl.pallas_call(
        body, out_shape=out_shape, grid=(n_rows // tm,), in_specs=in_specs, out_specs=out_specs,
        compiler_params=_params(("arbitrary",), VMEM_LIMIT), name=name)(*ins)
    return outs


def _rms(x):
    return x * lax.rsqrt(jnp.mean(x * x, axis=-1, keepdims=True) + EPS)


def _sigmoid(x):
    return 0.5 * (jnp.tanh(0.5 * x) + 1.0)


def _silu(x):
    return x * _sigmoid(x)


def _gelu_tanh(x):
    return 0.5 * x * (1.0 + jnp.tanh(math.sqrt(2.0 / math.pi) * (x + 0.044715 * (x * x * x))))


def f_norm_mod(x, g, sc, sh):
    return ((_rms(x) * g) * (1.0 + sc) + sh,)


def f_rms(x, g):
    return (_rms(x) * g,)


def f_gate(o, z):
    return (o * _silu(z),)


def f_s5_act(y, u, d):
    return (_gelu_tanh(y + d * u),)


def f_s5_glu(ya, gl, z, b):
    return (ya * _sigmoid(gl + b) * _silu(z),)


def _as_parts(x, n_ctx):
    xs = x if isinstance(x, tuple) else (x,)
    assert len(xs) == 1 or xs[0].shape[0] == n_ctx
    return xs, sum(p.shape[0] for p in xs)


def _parts_specs(xs, tm, ncb):
    d = xs[0].shape[1]
    if len(xs) == 1:
        return [pl.BlockSpec((tm, d), lambda i: (i, 0))]
    return [pl.BlockSpec((tm, d), lambda i: (jnp.minimum(i, ncb - 1), 0)),
            pl.BlockSpec((tm, d), lambda i: (jnp.maximum(i - ncb, 0), 0))]


def _parts_tile(x_refs, ncb):
    if len(x_refs) == 1:
        return x_refs[0][...]
    return jnp.where(pl.program_id(0) < ncb, x_refs[0][...], x_refs[1][...])


def norm_proj(x, g, sc, sh, w, n_ctx, name):
    xs, n = _as_parts(x, n_ctx)
    d = xs[0].shape[1]
    nw = w.shape[1]
    tm = math.gcd(ROW_TILE, n, n_ctx)
    ncb = n_ctx // tm
    nx = len(xs)

    def body(*refs):
        g_ref, sc_ref, sh_ref, w_ref, h_ref, p_ref = refs[nx:]
        h = f_norm_mod(_parts_tile(refs[:nx], ncb), g_ref[0], sc_ref[0], sh_ref[0])[0].astype(BF16)
        h_ref[...] = h
        p_ref[...] = jnp.dot(h, w_ref[...], preferred_element_type=F32)

    row = pl.BlockSpec((tm, d), lambda i: (i, 0))
    return pl.pallas_call(
        body, out_shape=[jax.ShapeDtypeStruct((n, d), BF16), jax.ShapeDtypeStruct((n, nw), F32)], grid=(n // tm,),
        in_specs=_parts_specs(xs, tm, ncb) + [_bc_spec(g, ncb), _bc_spec(sc, ncb), _bc_spec(sh, ncb),
                                              pl.BlockSpec(w.shape, lambda i: (0, 0))],
        out_specs=[row, pl.BlockSpec((tm, nw), lambda i: (i, 0))],
        compiler_params=_params(("parallel",), VMEM_LIMIT), name=name)(*xs, g, sc, sh, w)


def norm_proj_bwd(terms, w, x, g, sc, sh, add, n_ctx, name, latent_dx_only=False, dx_from_segments=False):
    xs, n = _as_parts(x, n_ctx)
    adds = add if isinstance(add, tuple) else (add,)
    d = xs[0].shape[1]
    tm = math.gcd(ROW_TILE, n, n_ctx, *[t[2] for t in terms])
    ncb = n_ctx // tm
    nt, nx, na = len(terms), len(xs), len(adds)
    add_skip = ncb if na == 1 and add.shape[0] != n else 0
    n_dx = 2 if dx_from_segments else 1
    tj = tm // N_SEG
    assert not dx_from_segments or (ncb == 1 and not latent_dx_only)

    def body(*refs):
        i = pl.program_id(0)
        a_refs = refs[:nt]
        w_ref, x_refs = refs[nt], refs[nt + 1:nt + 1 + nx]
        g_ref, sc_ref, sh_ref = refs[nt + 1 + nx:nt + 4 + nx]
        add_refs = refs[nt + 4 + nx:nt + 4 + nx + na]
        outs = refs[nt + 4 + nx + na:]
        dx_refs, (dg_ref, dsc_ref, dsh_ref) = outs[:n_dx], outs[n_dx:n_dx + 3]
        d_h = None
        for a_ref, (a, off, first) in zip(a_refs, terms):
            part = lax.dot_general(a_ref[...].astype(BF16), w_ref[:, off:off + a.shape[1]], NT_DIMS,
                                   preferred_element_type=F32)
            if first:
                part = jnp.where(i >= first // tm, part, 0.0)
            d_h = part if d_h is None else d_h + part
        _, vjp = jax.vjp(lambda x_, g_, sc_, sh_: f_norm_mod(x_, g_, sc_, sh_), _parts_tile(x_refs, ncb), g_ref[0],
                         sc_ref[0], sh_ref[0])
        d_x, d_g, d_sc, d_sh = vjp((d_h,))
        extra = _parts_tile(add_refs, ncb)
        d_x = d_x + (extra if add_skip == 0 else jnp.where(i >= ncb, extra, 0.0))
        if dx_from_segments:
            slabs = outs[n_dx + 3]
            for c in range(d // HEAD_LANES):
                slabs[c] = d_x[:, c * HEAD_LANES:(c + 1) * HEAD_LANES]
            for k, here in enumerate([i < ncb, i >= ncb]):
                @pl.when(here)
                def _(k=k):
                    for c in range(d // HEAD_LANES):
                        for seg in range(N_SEG):
                            dx_refs[k][seg, :, c * HEAD_LANES:(c + 1) * HEAD_LANES] = (
                                slabs[c, pl.ds(seg, tj, stride=N_SEG), :])
        else:
            dx_refs[0][...] = d_x

        @pl.when(i == 0)
        def _():
            dg_ref[...] = jnp.zeros_like(dg_ref)

        @pl.when((i == 0) | (i == ncb))
        def _():
            dsc_ref[...] = jnp.zeros_like(dsc_ref)
            dsh_ref[...] = jnp.zeros_like(dsh_ref)

        dg_ref[0] += d_g
        dsc_ref[0] += d_sc
        dsh_ref[0] += d_sh

    def a_spec(a, first):
        skip = first // tm
        return pl.BlockSpec((tm, a.shape[1]), lambda i: (jnp.maximum(i - skip, 0), 0))

    dx_skip = ncb if latent_dx_only else 0
    if dx_from_segments:
        dx_shapes = [jax.ShapeDtypeStruct((N_SEG, n_ctx // N_SEG, d), F32),
                     jax.ShapeDtypeStruct((N_SEG, (n - n_ctx) // N_SEG, d), F32)]
        dx_specs = [pl.BlockSpec((N_SEG, tj, d), lambda i: (0, 0, 0)),
                    pl.BlockSpec((N_SEG, tj, d), lambda i: (0, jnp.maximum(i - ncb, 0), 0))]
    else:
        dx_shapes = [jax.ShapeDtypeStruct((n - dx_skip * tm, d), F32)]
        dx_specs = [pl.BlockSpec((tm, d), lambda i: (jnp.maximum(i - dx_skip, 0), 0))]
    add_specs = (_parts_specs(adds, tm, ncb) if na == 2 else
                 [pl.BlockSpec((tm, d), lambda i: (jnp.maximum(i - add_skip, 0), 0))])
    res = pl.pallas_call(
        body,
        out_shape=dx_shapes + [jax.ShapeDtypeStruct(g.shape, F32), jax.ShapeDtypeStruct(sc.shape, F32),
                               jax.ShapeDtypeStruct(sh.shape, F32)],
        grid=(n // tm,),
        in_specs=[a_spec(a, first) for a, _, first in terms]
        + [pl.BlockSpec(w.shape, lambda i: (0, 0))] + _parts_specs(xs, tm, ncb)
        + [_bc_spec(g, ncb), _bc_spec(sc, ncb), _bc_spec(sh, ncb)] + add_specs,
        out_specs=dx_specs + [_bc_spec(g, ncb), _bc_spec(sc, ncb), _bc_spec(sh, ncb)],
        scratch_shapes=[pltpu.VMEM((d // HEAD_LANES, tm, HEAD_LANES), F32)] if dx_from_segments else [],
        compiler_params=_params(("arbitrary",), VMEM_LIMIT), name=name)(
            *[t[0] for t in terms], w, *xs, g, sc, sh, *adds)
    if dx_from_segments:
        return ((res[0].reshape(n_ctx, d), res[1].reshape(n - n_ctx, d)), *res[2:])
    return res


def mla_post_fwd(o, p0, x0, gate, w_out, n_ctx, name="l0_post"):
    n, d = o.shape
    xs, _ = _as_parts(x0, n_ctx)
    tm = math.gcd(ROW_TILE, n, n_ctx)
    ncb = n_ctx // tm
    nx = len(xs)

    def body(o_ref, z_ref, *refs):
        gt_ref, w_ref, x1_ref, og_ref, out_ref = refs[nx:]
        og = f_gate(o_ref[...], z_ref[...])[0].astype(BF16)
        out = jnp.dot(og, w_ref[...], preferred_element_type=F32)
        og_ref[...] = og
        out_ref[...] = out
        x1_ref[...] = _parts_tile(refs[:nx], ncb) + gt_ref[0] * out

    row = pl.BlockSpec((tm, d), lambda i: (i, 0))
    return pl.pallas_call(
        body, out_shape=[jax.ShapeDtypeStruct((n, d), F32), jax.ShapeDtypeStruct((n, d), BF16),
                         jax.ShapeDtypeStruct((n, d), F32)],
        grid=(n // tm,),
        in_specs=[row, row] + _parts_specs(xs, tm, ncb) + [_bc_spec(gate, ncb), pl.BlockSpec((d, d), lambda i: (0, 0))],
        out_specs=[row, row, row],
        compiler_params=_params(("parallel",), VMEM_LIMIT), name=name)(o, p0, *xs, gate, w_out)


def mla_post_bwd(dx1, out, og, o, p0, gate, w_out, n_ctx, name="l0_post_bwd"):
    n, d = o.shape
    dxs, _ = _as_parts(dx1, n_ctx)
    tm = math.gcd(ROW_TILE, n, n_ctx)
    ncb = n_ctx // tm
    nx = len(dxs)

    def body(*refs):
        out_ref, og_ref, o_ref, z_ref, gt_ref, w_ref, do_ref, dz_ref, dgt_ref, dw_ref = refs[nx:]
        i = pl.program_id(0)

        @pl.when(i == 0)
        def _():
            dw_ref[...] = jnp.zeros_like(dw_ref)

        @pl.when((i == 0) | (i == ncb))
        def _():
            dgt_ref[...] = jnp.zeros_like(dgt_ref)

        dx = _parts_tile(refs[:nx], ncb)
        dgt_ref[0] += jnp.sum(dx * out_ref[...], axis=0, keepdims=True)
        d_out16 = (gt_ref[0] * dx).astype(BF16)
        dw_ref[...] += lax.dot_general(og_ref[...], d_out16, (((0,), (0,)), ((), ())), preferred_element_type=F32)
        d_og = lax.dot_general(d_out16, w_ref[...], NT_DIMS, preferred_element_type=F32)
        _, gate_vjp = jax.vjp(lambda o_, z_: f_gate(o_, z_), o_ref[...], z_ref[...])
        d_o, d_z = gate_vjp((d_og,))
        do_ref[...] = d_o
        dz_ref[...] = d_z

    row = pl.BlockSpec((tm, d), lambda i: (i, 0))
    mat = pl.BlockSpec((d, d), lambda i: (0, 0))
    return pl.pallas_call(
        body, out_shape=[jax.ShapeDtypeStruct((n, d), F32), jax.ShapeDtypeStruct((n, d), F32),
                         jax.ShapeDtypeStruct(gate.shape, F32), jax.ShapeDtypeStruct((d, d), F32)],
        grid=(n // tm,),
        in_specs=_parts_specs(dxs, tm, ncb) + [row, row, row, row, _bc_spec(gate, ncb), mat],
        out_specs=[row, row, _bc_spec(gate, ncb), mat],
        compiler_params=_params(("arbitrary",), VMEM_LIMIT), name=name)(*dxs, out, og, o, p0, gate, w_out)


def s5_tail(y_ssm, p1, x1p, target, n_ctx, d_vec, b_glu, gate, final_g, w_glu, w_out, name="l1_tail"):
    n, d = y_ssm.shape
    tm = math.gcd(ROW_TILE, n, n_ctx)
    off = n_ctx // tm
    tn_dims = (((0,), (0,)), ((), ()))

    def row_loss(x, g, t):
        e = _rms(x) * g - t
        return 0.5 * (e * e) * (1.0 / d)

    def body(y_ref, u_ref, z_ref, x1_ref, t_ref, d_ref, b_ref, gt_ref, fg_ref, wg_ref, wo_ref,
             l_ref, dx_ref, dy_ref, du_ref, dz_ref, dfg_ref, dgt_ref, db_ref, dd_ref, dwg_ref, dwo_ref):
        @pl.when(pl.program_id(0) == 0)
        def _():
            for r in (l_ref, dfg_ref, dgt_ref, db_ref, dd_ref, dwg_ref, dwo_ref):
                r[...] = jnp.zeros_like(r)

        u, z, tgt, gt = u_ref[...], z_ref[...], t_ref[...], gt_ref[...]
        (ya,), act_vjp = jax.vjp(lambda y_, u_, d_: f_s5_act(y_, u_, d_), y_ref[...], u, d_ref[...])
        ya16 = ya.astype(BF16)
        gl = jnp.dot(ya16, wg_ref[...], preferred_element_type=F32)
        (y3,), glu_vjp = jax.vjp(lambda a_, g_, z_, b_: f_s5_glu(a_, g_, z_, b_), ya, gl, z, b_ref[...])
        y3_16 = y3.astype(BF16)
        out1 = jnp.dot(y3_16, wo_ref[...], preferred_element_type=F32)
        lterm, loss_vjp = jax.vjp(lambda x_, g_: row_loss(x_, g_, tgt), x1_ref[...] + gt * out1, fg_ref[...])
        dx2, dfg = loss_vjp(jnp.ones_like(lterm))
        l_ref[...] += jnp.sum(lterm, axis=0, keepdims=True)
        dfg_ref[...] += dfg
        dx_ref[...] = dx2
        dgt_ref[...] += jnp.sum(dx2 * out1, axis=0, keepdims=True)
        d_out16 = (gt * dx2).astype(BF16)
        dwo_ref[...] += lax.dot_general(y3_16, d_out16, tn_dims, preferred_element_type=F32)
        d_y3 = lax.dot_general(d_out16, wo_ref[...], NT_DIMS, preferred_element_type=F32)
        d_ya, d_gl, d_z, d_b = glu_vjp((d_y3,))
        dz_ref[...] = d_z
        db_ref[...] += d_b
        d_gl16 = d_gl.astype(BF16)
        dwg_ref[...] += lax.dot_general(ya16, d_gl16, tn_dims, preferred_element_type=F32)
        d_ya = d_ya + lax.dot_general(d_gl16, wg_ref[...], NT_DIMS, preferred_element_type=F32)
        d_y, d_u, d_d = act_vjp((d_ya,))
        dy_ref[...] = d_y
        du_ref[...] = d_u
        dd_ref[...] += d_d

    row = pl.BlockSpec((tm, d), lambda i: (i, 0))
    vecs = pl.BlockSpec((1, d), lambda i: (0, 0))
    mat = pl.BlockSpec((d, d), lambda i: (0, 0))
    return pl.pallas_call(
        body,
        out_shape=[jax.ShapeDtypeStruct((1, d), F32)] + [jax.ShapeDtypeStruct((n, d), F32)] * 4
        + [jax.ShapeDtypeStruct((1, d), F32)] * 4 + [jax.ShapeDtypeStruct((d, d), F32)] * 2,
        grid=(n // tm,),
        in_specs=[row, pl.BlockSpec((tm, d), lambda i: (i + off, 0)), pl.BlockSpec((tm, d), lambda i: (i + off, 1)),
                  pl.BlockSpec((tm, d), lambda i: (i + off, 0)), row, vecs, vecs, vecs, vecs, mat, mat],
        out_specs=[vecs, row, row, row, row, vecs, vecs, vecs, vecs, mat, mat],
        compiler_params=_params(("arbitrary",), VMEM_LIMIT), name=name)(
            y_ssm, p1, p1, x1p, target, d_vec, b_glu, gate, final_g, w_glu, w_out)


NT_DIMS = (((1,), (1,)), ((), ()))
HEAD_LANES = 128
N_PAIRS = MLA_HEADS // 2


def _own_lanes(shape, hh):
    lane = lax.broadcasted_iota(jnp.int32, shape, len(shape) - 1)
    return (lane < V_HEAD_DIM) if hh == 0 else (lane >= V_HEAD_DIM)


def _delta_lane(hh):
    return V_HEAD_DIM if hh == 0 else 0


def _rope_tiles(x, cos, sin_next, sin_prev, inverse):
    width = x.shape[-1]
    reps = width // HEAD_LANES
    c, sn, sp = (jnp.tile(t, (1, reps)) for t in (cos, sin_next, sin_prev))
    if inverse:
        return x * c + pltpu.roll(x * sn, 8, 1) + pltpu.roll(x * sp, width - 8, 1)
    return x * c + pltpu.roll(x, width - 8, 1) * sn + pltpu.roll(x, 8, 1) * sp


STAT_LANE = QK_DIM


def _with_stat(x16, col, lane0):
    hi = col.astype(BF16)
    r1 = col - hi.astype(F32)
    mid = r1.astype(BF16)
    lo = (r1 - mid.astype(F32)).astype(BF16)
    lane = lax.broadcasted_iota(jnp.int32, x16.shape, 1)
    return jnp.where(lane == lane0, hi, jnp.where(lane == lane0 + 1, mid, jnp.where(lane == lane0 + 2, lo, x16)))


def attn_fwd(qb, kb, vb, n_ctx):
    T = qb.shape[0]
    tq = math.gcd(ROW_TILE, n_ctx)
    nq, ncb = T // tq, n_ctx // tq

    def body(q_ref, k_ref, v_ref, o_ref, lse_ref, qs_ref):
        qi = pl.program_id(1)

        def rows(n_keys):
            v = v_ref[:n_keys, :]
            outs = []
            for hh in range(2):
                hs = slice(hh * HEAD_LANES, (hh + 1) * HEAD_LANES)
                s = lax.dot_general(q_ref[:, hs], k_ref[:n_keys, hs], NT_DIMS,
                                    preferred_element_type=F32) * SOFTMAX_SCALE
                m = jnp.max(s, axis=-1, keepdims=True)
                p = jnp.exp(s - m)
                l = jnp.sum(p, axis=-1, keepdims=True)
                outs.append(jnp.dot(p.astype(BF16), v, preferred_element_type=F32) / l)
                lse = m + jnp.log(l)
                lse_ref[hh] = lse
                qs_ref[:, hs] = _with_stat(q_ref[:, hs], lse * (-1.0 / SOFTMAX_SCALE), STAT_LANE)
            o_ref[...] = jnp.where(_own_lanes(outs[0].shape, 0), outs[0], outs[1])

        pl.when(qi < ncb)(lambda: rows(n_ctx))
        pl.when(qi >= ncb)(lambda: rows(T))

    return pl.pallas_call(
        body,
        out_shape=[jax.ShapeDtypeStruct((T, MLA_HEADS * V_HEAD_DIM), F32),
                   jax.ShapeDtypeStruct((MLA_HEADS, T, 1), F32), jax.ShapeDtypeStruct(qb.shape, BF16)],
        grid=(N_PAIRS, nq),
        in_specs=[pl.BlockSpec((tq, 2 * HEAD_LANES), lambda h, i: (i, h)),
                  pl.BlockSpec((T, 2 * HEAD_LANES), lambda h, i: (0, h)),
                  pl.BlockSpec((T, 2 * V_HEAD_DIM), lambda h, i: (0, h))],
        out_specs=[pl.BlockSpec((tq, 2 * V_HEAD_DIM), lambda h, i: (i, h)),
                   pl.BlockSpec((2, tq, 1), lambda h, i: (h, i, 0)),
                   pl.BlockSpec((tq, 2 * HEAD_LANES), lambda h, i: (i, h))],
        compiler_params=_params(("parallel", "parallel"), VMEM_LIMIT), name="attn_fwd")(qb, kb, vb)


def attn_bwd_dq(qb, kb, vb, o, do, lse, tabs, n_ctx):
    T = qb.shape[0]
    tq = math.gcd(ROW_TILE, n_ctx)
    nq, ncb = T // tq, n_ctx // tq

    def body(q_ref, k_ref, v_ref, o_ref, do_ref, lse_ref, c_ref, sn_ref, sp_ref, dq_ref, dos_ref):
        qi = pl.program_id(1)

        def rows(n_keys):
            v = v_ref[:n_keys, :]
            dqs = []
            for hh in range(2):
                hs = slice(hh * HEAD_LANES, (hh + 1) * HEAD_LANES)
                k = k_ref[:n_keys, hs]
                do = jnp.where(_own_lanes(do_ref.shape, hh), do_ref[...], 0.0)
                delta = jnp.sum(do * o_ref[...], axis=-1, keepdims=True)
                s = lax.dot_general(q_ref[:, hs], k, NT_DIMS, preferred_element_type=F32) * SOFTMAX_SCALE
                p = jnp.exp(s - lse_ref[hh])
                do16 = do.astype(BF16)
                dp = lax.dot_general(do16, v, NT_DIMS, preferred_element_type=F32)
                ds = p * (dp - delta) * SOFTMAX_SCALE
                dqs.append(jnp.dot(ds.astype(BF16), k, preferred_element_type=F32))
                dos_ref[:, hs] = _with_stat(do16, delta, _delta_lane(hh))
            dq = jnp.concatenate(dqs, axis=1)
            dq_ref[...] = _rope_tiles(dq, c_ref[...], sn_ref[...], sp_ref[...], True).astype(BF16)

        pl.when(qi < ncb)(lambda: rows(n_ctx))
        pl.when(qi >= ncb)(lambda: rows(T))

    tab = pl.BlockSpec((tq, HEAD_LANES), lambda h, i: (i, 0))
    return pl.pallas_call(
        body,
        out_shape=[jax.ShapeDtypeStruct((T, MLA_HEADS * HEAD_LANES), BF16)] * 2,
        grid=(N_PAIRS, nq),
        in_specs=[pl.BlockSpec((tq, 2 * HEAD_LANES), lambda h, i: (i, h)),
                  pl.BlockSpec((T, 2 * HEAD_LANES), lambda h, i: (0, h)),
                  pl.BlockSpec((T, 2 * V_HEAD_DIM), lambda h, i: (0, h)),
                  pl.BlockSpec((tq, 2 * V_HEAD_DIM), lambda h, i: (i, h)),
                  pl.BlockSpec((tq, 2 * V_HEAD_DIM), lambda h, i: (i, h)),
                  pl.BlockSpec((2, tq, 1), lambda h, i: (h, i, 0)), tab, tab, tab],
        out_specs=[pl.BlockSpec((tq, 2 * HEAD_LANES), lambda h, i: (i, h))] * 2,
        compiler_params=_params(("parallel", "parallel"), VMEM_LIMIT), name="attn_bwd_dq")(
            qb, kb, vb, o, do, lse, *tabs)


def attn_bwd_dkv(qs, kb, vb, dos, n_ctx):
    T = qs.shape[0]
    tq = math.gcd(ROW_TILE, n_ctx)
    nq, ncb = T // tq, n_ctx // tq

    def body(q_ref, do_ref, k_ref, v_ref, dk_ref, dv_ref):
        kj = pl.program_id(1)

        def cols(first):
            v = v_ref[...]
            lane = lax.broadcasted_iota(jnp.int32, v.shape, 1)
            dvs = []
            for hh in range(2):
                hs = slice(hh * HEAD_LANES, (hh + 1) * HEAD_LANES)
                q = q_ref[first:, hs]
                do16 = do_ref[first:, hs]
                in_delta = (lane >= _delta_lane(hh)) & (lane < _delta_lane(hh) + 3)
                v_minus = jnp.where(in_delta, -jnp.ones_like(v), v)
                pt = jnp.exp(lax.dot_general(k_ref[:, hs], q, NT_DIMS, preferred_element_type=F32) * SOFTMAX_SCALE)
                dvs.append(jnp.dot(pt.astype(BF16), do16, preferred_element_type=F32))
                dst = pt * lax.dot_general(v_minus, do16, NT_DIMS, preferred_element_type=F32) * SOFTMAX_SCALE
                dk_ref[:, hs] = jnp.dot(dst.astype(BF16), q, preferred_element_type=F32)
            dv_ref[...] = jnp.where(_own_lanes(dvs[0].shape, 0), dvs[0], dvs[1])

        pl.when(kj < ncb)(lambda: cols(0))
        pl.when(kj >= ncb)(lambda: cols(n_ctx))

    return pl.pallas_call(
        body,
        out_shape=[jax.ShapeDtypeStruct((T, MLA_HEADS * HEAD_LANES), F32),
                   jax.ShapeDtypeStruct((T, MLA_HEADS * V_HEAD_DIM), F32)],
        grid=(N_PAIRS, nq),
        in_specs=[pl.BlockSpec((T, 2 * HEAD_LANES), lambda h, j: (0, h)),
                  pl.BlockSpec((T, 2 * HEAD_LANES), lambda h, j: (0, h)),
                  pl.BlockSpec((tq, 2 * HEAD_LANES), lambda h, j: (j, h)),
                  pl.BlockSpec((tq, 2 * V_HEAD_DIM), lambda h, j: (j, h))],
        out_specs=[pl.BlockSpec((tq, 2 * HEAD_LANES), lambda h, j: (j, h)),
                   pl.BlockSpec((tq, 2 * V_HEAD_DIM), lambda h, j: (j, h))],
        compiler_params=_params(("parallel", "parallel"), VMEM_LIMIT), name="attn_bwd_dkv")(
            qs, dos, kb, vb)


def _split_bf16(x):
    hi = x.astype(BF16)
    return hi, (x - hi.astype(F32)).astype(BF16)


def q_heads(cq, gain, w_uq_p, tabs, name="l0_uq"):
    T, K = cq.arr.shape[0], cq.width
    N = w_uq_p.shape[1]
    tm = math.gcd(ROW_TILE, T)

    def body(a_ref, g_ref, w_ref, c_ref, sn_ref, sp_ref, o_ref, n_ref):
        qn = f_rms(a_ref[...], g_ref[0])[0].astype(BF16)
        n_ref[...] = qn
        acc = jnp.dot(qn, w_ref[...], preferred_element_type=F32)
        o_ref[...] = _rope_tiles(acc, c_ref[...], sn_ref[...], sp_ref[...], False).astype(BF16)

    tab = pl.BlockSpec((tm, HEAD_LANES), lambda i: (i, 0))
    return pl.pallas_call(
        body, out_shape=[jax.ShapeDtypeStruct((T, N), BF16), jax.ShapeDtypeStruct((T, K), BF16)], grid=(T // tm,),
        in_specs=[cq.spec(tm), pl.BlockSpec((1, 1, K), lambda i: (0, 0, 0)), pl.BlockSpec((K, N), lambda i: (0, 0)),
                  tab, tab, tab],
        out_specs=[pl.BlockSpec((tm, N), lambda i: (i, 0)), pl.BlockSpec((tm, K), lambda i: (i, 0))],
        compiler_params=_params(("parallel",), VMEM_LIMIT), name=name)(cq.arr, gain, w_uq_p, *tabs)


def kv_heads(ckv, gain, w_kn_p, w_v, kr, spread, tabs, name="l0_ukv"):
    T, K = ckv.arr.shape[0], ckv.width
    N = w_kn_p.shape[1]
    NV = w_v.shape[1]
    tm = math.gcd(ROW_TILE, T)

    def body(a_ref, g_ref, wk_ref, wv_ref, kr_ref, e_ref, c_ref, sn_ref, sp_ref, k_ref, v_ref, n_ref):
        a = f_rms(a_ref[...], g_ref[0])[0].astype(BF16)
        n_ref[...] = a
        hi, lo = _split_bf16(kr_ref[...])
        acc = (jnp.dot(a, wk_ref[...], preferred_element_type=F32)
               + jnp.dot(hi, e_ref[...], preferred_element_type=F32)
               + jnp.dot(lo, e_ref[...], preferred_element_type=F32))
        roped = _rope_tiles(acc, c_ref[...], sn_ref[...], sp_ref[...], False)
        lane = lax.broadcasted_iota(jnp.int32, roped.shape, 1) % HEAD_LANES
        k_ref[...] = jnp.where((lane >= STAT_LANE) & (lane < STAT_LANE + 3), 1.0, roped).astype(BF16)
        v_ref[...] = jnp.dot(a, wv_ref[...], preferred_element_type=F32).astype(BF16)

    tab = pl.BlockSpec((tm, HEAD_LANES), lambda i: (i, 0))
    return pl.pallas_call(
        body, out_shape=[jax.ShapeDtypeStruct((T, N), BF16), jax.ShapeDtypeStruct((T, NV), BF16),
                         jax.ShapeDtypeStruct((T, K), BF16)], grid=(T // tm,),
        in_specs=[ckv.spec(tm), pl.BlockSpec((1, 1, K), lambda i: (0, 0, 0)), pl.BlockSpec((K, N), lambda i: (0, 0)),
                  pl.BlockSpec((K, NV), lambda i: (0, 0)), pl.BlockSpec((tm, QK_ROPE_DIM), lambda i: (i, 0)),
                  pl.BlockSpec((QK_ROPE_DIM, N), lambda i: (0, 0)), tab, tab, tab],
        out_specs=[pl.BlockSpec((tm, N), lambda i: (i, 0)), pl.BlockSpec((tm, NV), lambda i: (i, 0)),
                   pl.BlockSpec((tm, K), lambda i: (i, 0))],
        compiler_params=_params(("parallel",), VMEM_LIMIT), name=name)(ckv.arr, gain, w_kn_p, w_v, kr, spread, *tabs)


def heads_unrope(d, tabs, spread=None, name="unrope"):
    T, N = d.shape
    tm = math.gcd(ROW_TILE, T)

    def body(*refs):
        if spread is None:
            d_ref, c_ref, sn_ref, sp_ref, o_ref = refs
        else:
            d_ref, c_ref, sn_ref, sp_ref, e_ref, o_ref, kr_ref = refs
        g = _rope_tiles(d_ref[...], c_ref[...], sn_ref[...], sp_ref[...], True)
        o_ref[...] = g.astype(BF16)
        if spread is not None:
            hi, lo = _split_bf16(g)
            kr_ref[...] = (lax.dot_general(hi, e_ref[...], NT_DIMS, preferred_element_type=F32)
                           + lax.dot_general(lo, e_ref[...], NT_DIMS, preferred_element_type=F32))

    tab = pl.BlockSpec((tm, HEAD_LANES), lambda i: (i, 0))
    row = pl.BlockSpec((tm, N), lambda i: (i, 0))
    ins, in_specs = [d, *tabs], [row, tab, tab, tab]
    out_shape, out_specs = [jax.ShapeDtypeStruct((T, N), BF16)], [row]
    if spread is not None:
        ins.append(spread)
        in_specs.append(pl.BlockSpec(spread.shape, lambda i: (0, 0)))
        out_shape.append(jax.ShapeDtypeStruct((T, spread.shape[0]), F32))
        out_specs.append(pl.BlockSpec((tm, spread.shape[0]), lambda i: (i, 0)))
    return pl.pallas_call(
        body, out_shape=out_shape, grid=(T // tm,), in_specs=in_specs, out_specs=out_specs,
        compiler_params=_params(("parallel",), VMEM_LIMIT), name=name)(*ins)


def _cmul(ar, ai, br, bi):
    return ar * br - ai * bi, ar * bi + ai * br


def s5_chain(finals, s0, a, n_steps, reverse, name):
    W = finals.shape[-1]
    first = N_SEG - 1 if reverse else 0

    def body(f_ref, s0_ref, a_ref, c_ref):
        pr, pi = jnp.ones((1, W), F32), jnp.zeros((1, W), F32)
        br, bi = a_ref[0], a_ref[1]
        n = n_steps
        while n:
            if n & 1:
                pr, pi = _cmul(pr, pi, br, bi)
            br, bi = _cmul(br, bi, br, bi)
            n >>= 1
        fr, fi = f_ref[0], f_ref[1]
        row = lax.broadcasted_iota(jnp.int32, (N_SEG, W), 0)
        s0r = jnp.broadcast_to(s0_ref[0], (N_SEG, W))
        s0i = jnp.broadcast_to(s0_ref[1], (N_SEG, W))
        cr = jnp.where(row == first, s0r, 0.0)
        ci = jnp.where(row == first, s0i, 0.0)
        shift = N_SEG - 1 if reverse else 1
        for _ in range(N_SEG - 1):
            mr, mi = _cmul(pr, pi, cr, ci)
            tr = pltpu.roll(fr + mr, shift, 0)
            ti = pltpu.roll(fi + mi, shift, 0)
            cr = jnp.where(row == first, s0r, tr)
            ci = jnp.where(row == first, s0i, ti)
        c_ref[0] = cr
        c_ref[1] = ci

    return pl.pallas_call(body, out_shape=jax.ShapeDtypeStruct((2, N_SEG, W), F32), name=name)(finals, s0, a)


def _scan_chunk(bur, bui, st_ref, a_ref, n_steps, reverse):
    for lc in range(S5_LANES // BLK_ST):
        sl = slice(lc * BLK_ST, (lc + 1) * BLK_ST)
        lr = jnp.broadcast_to(a_ref[0, :, sl], (N_SEG, BLK_ST))
        li = jnp.broadcast_to(a_ref[1, :, sl], (N_SEG, BLK_ST))

        def step(jj, carry, sl=sl, lr=lr, li=li):
            sr, si = carry
            j = (n_steps - 1 - jj) if reverse else jj
            r0 = pl.multiple_of(j * N_SEG, N_SEG)
            nr = lr * sr - li * si + bur[pl.ds(r0, N_SEG), sl]
            ni = lr * si + li * sr + bui[pl.ds(r0, N_SEG), sl]
            bur[pl.ds(r0, N_SEG), sl] = nr
            bui[pl.ds(r0, N_SEG), sl] = ni
            return nr, ni

        sr, si = lax.fori_loop(0, n_steps, step, (st_ref[0, :, sl], st_ref[1, :, sl]))
        st_ref[0, :, sl] = sr
        st_ref[1, :, sl] = si


def _project_in(x16, w_re, w_im, bur, bui, adjoint):
    for gb in range(N_BLOCKS):
        xb = x16[:, gb * BLK_CH:(gb + 1) * BLK_CH]
        sl = slice(gb * BLK_ST, (gb + 1) * BLK_ST)
        if adjoint:
            dn = (((1,), (1,)), ((), ()))
            bur[:, sl] = lax.dot_general(xb, w_re[gb], dn, preferred_element_type=F32)
            bui[:, sl] = -lax.dot_general(xb, w_im[gb], dn, preferred_element_type=F32)
        else:
            bur[:, sl] = jnp.dot(xb, w_re[gb], preferred_element_type=F32)
            bui[:, sl] = jnp.dot(xb, w_im[gb], preferred_element_type=F32)


def s5_scan(act, w_re, w_im, a, init, *, reverse, adjoint=False, c_re=None, c_im=None, add=None,
            want_ckpt=False, rows=None, name):
    act_off, N = rows if rows is not None else (0, act.shape[0])
    R = math.gcd(ROW_TILE, N, act_off)
    nch, jc = N // R, R // N_SEG
    with_out = c_re is not None

    def chunk(i):
        return (nch - 1 - i) if reverse else i

    def body(*refs):
        act_ref, wre_ref, wim_ref, a_ref, init_ref = refs[:5]
        k = 5
        if with_out:
            cre_ref, cim_ref = refs[k:k + 2]
            k += 2
        if add is not None:
            add_ref = refs[k]
            k += 1
        if with_out:
            out_ref = refs[k]
            k += 1
        if want_ckpt:
            ck_ref = refs[k]
            k += 1
        fin_ref, bur, bui = refs[k:k + 3]

        @pl.when(pl.program_id(0) == 0)
        def _():
            fin_ref[...] = init_ref[...]

        if want_ckpt:
            ck_ref[0] = fin_ref[...]
        _project_in(act_ref[...].astype(BF16), wre_ref, wim_ref, bur, bui, adjoint)
        _scan_chunk(bur, bui, fin_ref, a_ref, jc, reverse)
        if with_out:
            for gb in range(N_BLOCKS):
                sl = slice(gb * BLK_ST, (gb + 1) * BLK_ST)
                y = (jnp.dot(bur[:, sl].astype(BF16), cre_ref[gb], preferred_element_type=F32)
                     - jnp.dot(bui[:, sl].astype(BF16), cim_ref[gb], preferred_element_type=F32))
                cs = slice(gb * BLK_CH, (gb + 1) * BLK_CH)
                if add is not None:
                    y = y + add_ref[:, cs]
                out_ref[:, cs] = y

    row_spec = pl.BlockSpec((R, D_MODEL), lambda i: (chunk(i), 0))
    act_spec = pl.BlockSpec((R, D_MODEL), lambda i: (chunk(i) + act_off // R, 0))
    w_spec = pl.BlockSpec(w_re.shape, lambda i: (0, 0, 0))
    st_spec = pl.BlockSpec((2, N_SEG, S5_LANES), lambda i: (0, 0, 0))
    ins = [act, w_re, w_im, a, init]
    in_specs = [act_spec, w_spec, w_spec, pl.BlockSpec((2, 1, S5_LANES), lambda i: (0, 0, 0)), st_spec]
    if with_out:
        ins += [c_re, c_im]
        in_specs += [pl.BlockSpec(c_re.shape, lambda i: (0, 0, 0))] * 2
    if add is not None:
        ins.append(add)
        in_specs.append(row_spec)
    out_shape, out_specs = [], []
    if with_out:
        out_shape.append(jax.ShapeDtypeStruct((N, D_MODEL), F32))
        out_specs.append(row_spec)
    if want_ckpt:
        out_shape.append(jax.ShapeDtypeStruct((nch, 2, N_SEG, S5_LANES), F32))
        out_specs.append(pl.BlockSpec((1, 2, N_SEG, S5_LANES), lambda i: (chunk(i), 0, 0, 0)))
    out_shape.append(jax.ShapeDtypeStruct((2, N_SEG, S5_LANES), F32))
    out_specs.append(st_spec)
    res = pl.pallas_call(
        body, out_shape=out_shape, grid=(nch,), in_specs=in_specs, out_specs=out_specs,
        scratch_shapes=[pltpu.VMEM((R, S5_LANES), F32), pltpu.VMEM((R, S5_LANES), F32)],
        compiler_params=_params(("arbitrary",), VMEM_LIMIT), name=name)(*ins)
    res = list(res)
    out = res.pop(0) if with_out else None
    ckpt = res.pop(0) if want_ckpt else None
    return out, ckpt, res[0]


def s5_grads(dy, u, ckpt, b_re, b_im, c_re, c_im, lam, init_adj, *, reverse, add=None, u_off=0, du_rows=None,
             du_into=None, name):
    N = dy.shape[0]
    du_total, du_first = du_rows if du_rows is not None else (N, 0)
    R = math.gcd(ROW_TILE, N, u_off, du_first)
    nch, jc = N // R, R // N_SEG
    W = S5_LANES

    def chunk(i):
        return i if reverse else (nch - 1 - i)

    def body(*refs):
        dy_ref, u_ref, ck_ref, bre_ref, bim_ref, cre_ref, cim_ref, lam_ref, init_ref = refs[:9]
        k = 9
        if add is not None:
            add_ref = refs[k]
            k += 1
        k += du_into is not None
        du_ref, dlam_ref, dbre_ref, dbim_ref, dcre_ref, dcim_ref, fin_ref = refs[k:k + 7]
        sr_buf, si_buf, er_buf, ei_buf, st_buf = refs[k + 7:k + 12]

        @pl.when(pl.program_id(0) == 0)
        def _():
            fin_ref[...] = init_ref[...]
            dlam_ref[...] = jnp.zeros_like(dlam_ref)
            dbre_ref[...] = jnp.zeros_like(dbre_ref)
            dbim_ref[...] = jnp.zeros_like(dbim_ref)
            dcre_ref[...] = jnp.zeros_like(dcre_ref)
            dcim_ref[...] = jnp.zeros_like(dcim_ref)

        u16 = u_ref[...].astype(BF16)
        dy16 = dy_ref[...].astype(BF16)
        st_buf[...] = ck_ref[0]
        _project_in(u16, bre_ref, bim_ref, sr_buf, si_buf, False)
        _scan_chunk(sr_buf, si_buf, st_buf, lam_ref, jc, reverse)
        _project_in(dy16, cre_ref, cim_ref, er_buf, ei_buf, True)
        for lc in range(W // BLK_ST):
            sl = slice(lc * BLK_ST, (lc + 1) * BLK_ST)
            lr = jnp.broadcast_to(lam_ref[0, :, sl], (N_SEG, BLK_ST))
            li = jnp.broadcast_to(lam_ref[1, :, sl], (N_SEG, BLK_ST))

            def one(r0, spr, spi, carry, sl=sl, lr=lr, li=li):
                gr, gi, ar, ai = carry
                nr = er_buf[pl.ds(r0, N_SEG), sl] + lr * gr + li * gi
                ni = ei_buf[pl.ds(r0, N_SEG), sl] + lr * gi - li * gr
                er_buf[pl.ds(r0, N_SEG), sl] = nr
                ei_buf[pl.ds(r0, N_SEG), sl] = ni
                return nr, ni, ar + spr * nr + spi * ni, ai + spr * ni - spi * nr

            def step(ff, carry, sl=sl, one=one):
                f = jc - 1 - ff
                j = (jc - 1 - f) if reverse else f
                jp = (j + 1) if reverse else (j - 1)
                r0 = pl.multiple_of(j * N_SEG, N_SEG)
                p0 = pl.multiple_of(jp * N_SEG, N_SEG)
                return one(r0, sr_buf[pl.ds(p0, N_SEG), sl], si_buf[pl.ds(p0, N_SEG), sl], carry)

            carry = (fin_ref[0, :, sl], fin_ref[1, :, sl], dlam_ref[0, :, sl], dlam_ref[1, :, sl])
            carry = lax.fori_loop(0, jc - 1, step, carry)
            r_first = (jc - 1) * N_SEG if reverse else 0
            gr, gi, ar, ai = one(r_first, ck_ref[0, 0, :, sl], ck_ref[0, 1, :, sl], carry)
            fin_ref[0, :, sl] = gr
            fin_ref[1, :, sl] = gi
            dlam_ref[0, :, sl] = ar
            dlam_ref[1, :, sl] = ai
        tn = (((0,), (0,)), ((), ()))
        nt = (((1,), (1,)), ((), ()))
        for gb in range(N_BLOCKS):
            sl = slice(gb * BLK_ST, (gb + 1) * BLK_ST)
            cs = slice(gb * BLK_CH, (gb + 1) * BLK_CH)
            gr16 = er_buf[:, sl].astype(BF16)
            gi16 = ei_buf[:, sl].astype(BF16)
            du = (lax.dot_general(gr16, bre_ref[gb], nt, preferred_element_type=F32)
                  + lax.dot_general(gi16, bim_ref[gb], nt, preferred_element_type=F32))
            if add is not None:
                du = du + add_ref[:, cs]
            du_ref[:, cs] = du
            ub, dyb = u16[:, cs], dy16[:, cs]
            dbre_ref[gb] += lax.dot_general(ub, gr16, tn, preferred_element_type=F32)
            dbim_ref[gb] += lax.dot_general(ub, gi16, tn, preferred_element_type=F32)
            dcre_ref[gb] += lax.dot_general(sr_buf[:, sl].astype(BF16), dyb, tn, preferred_element_type=F32)
            dcim_ref[gb] -= lax.dot_general(si_buf[:, sl].astype(BF16), dyb, tn, preferred_element_type=F32)

    row_spec = pl.BlockSpec((R, D_MODEL), lambda i: (chunk(i), 0))
    st_spec = pl.BlockSpec((2, N_SEG, W), lambda i: (0, 0, 0))
    wb_spec = pl.BlockSpec(b_re.shape, lambda i: (0, 0, 0))
    wc_spec = pl.BlockSpec(c_re.shape, lambda i: (0, 0, 0))
    ins = [dy, u, ckpt, b_re, b_im, c_re, c_im, lam, init_adj]
    u_spec = pl.BlockSpec((R, D_MODEL), lambda i: (chunk(i) + u_off // R, 0))
    in_specs = [row_spec, u_spec, pl.BlockSpec((1, 2, N_SEG, W), lambda i: (chunk(i), 0, 0, 0)),
                wb_spec, wb_spec, wc_spec, wc_spec, pl.BlockSpec((2, 1, W), lambda i: (0, 0, 0)), st_spec]
    if add is not None:
        ins.append(add)
        in_specs.append(row_spec)
    aliases = {}
    if du_into is not None:
        aliases[len(ins)] = 0
        ins.append(du_into)
        in_specs.append(pl.BlockSpec(memory_space=pl.ANY))
    du_spec = pl.BlockSpec((R, D_MODEL), lambda i: (chunk(i) + du_first // R, 0))
    out_shape = [jax.ShapeDtypeStruct((du_total, D_MODEL), F32), jax.ShapeDtypeStruct((2, N_SEG, W), F32),
                 jax.ShapeDtypeStruct(b_re.shape, F32), jax.ShapeDtypeStruct(b_re.shape, F32),
                 jax.ShapeDtypeStruct(c_re.shape, F32), jax.ShapeDtypeStruct(c_re.shape, F32),
                 jax.ShapeDtypeStruct((2, N_SEG, W), F32)]
    out_specs = [du_spec, st_spec, wb_spec, wb_spec, wc_spec, wc_spec, st_spec]
    return pl.pallas_call(
        body, out_shape=out_shape, grid=(nch,), in_specs=in_specs, out_specs=out_specs, input_output_aliases=aliases,
        scratch_shapes=[pltpu.VMEM((R, W), F32) for _ in range(4)] + [pltpu.VMEM((2, N_SEG, W), F32)],
        compiler_params=_params(("arbitrary",), VMEM_LIMIT), name=name)(*ins)


def adamw(w, g, m, v, name="adamw", after=None):
    n, d = w.shape
    lanes = -(-d // 128) * 128
    tm = n
    while tm * lanes * 4 > (1 << 20) and tm % 16 == 0:
        tm //= 2
    c1 = 1.0 - ADAM_B1 ** ADAM_STEP
    c2 = 1.0 - ADAM_B2 ** ADAM_STEP

    def body(w_ref, g_ref, m_ref, v_ref, *rest):
        d_ref, nm_ref, nv_ref = rest[-3:]
        g_ = g_ref[...]
        m_ = ADAM_B1 * m_ref[...] + (1.0 - ADAM_B1) * g_
        v_ = ADAM_B2 * v_ref[...] + (1.0 - ADAM_B2) * (g_ * g_)
        d_ref[...] = -ADAM_LR * ((m_ / c1) / (jnp.sqrt(v_ / c2) + ADAM_EPS) + ADAM_WD * w_ref[...])
        nm_ref[...] = m_
        nv_ref[...] = v_

    spec = pl.BlockSpec((tm, d), lambda i: (i, 0))
    extra = [] if after is None else [after]
    return pl.pallas_call(
        body, out_shape=[jax.ShapeDtypeStruct((n, d), F32)] * 3, grid=(n // tm,),
        in_specs=[spec] * 4 + [pl.BlockSpec(memory_space=pl.ANY)] * len(extra), out_specs=[spec] * 3,
        compiler_params=_params(("parallel",), VMEM_LIMIT), name=name)(w, g, m, v, *extra)


def _coords():
    return lax.axis_index("x"), lax.axis_index("y"), lax.axis_index("c")


def exchange(arrays, out_shapes, remote, local, name, aliases=None):
    n_in, n_out, n_rem, n_loc = len(arrays), len(out_shapes), len(remote), len(local)

    def at(ref, idx):
        return ref if idx is None else ref.at[idx]

    def body(*refs):
        ins, outs = refs[:n_in], refs[n_in:n_in + n_out]
        send_sems, recv_sems, local_sems = refs[n_in + n_out:]
        me = _coords()
        sends, recvs = [], []
        for k, (flip, ii, src_at, oi, dst_at) in enumerate(remote):
            peer = (me[0] ^ flip[0], me[1] ^ flip[1], me[2] ^ flip[2])
            src = at(ins[ii], src_at(me, peer))
            sends.append(pltpu.make_async_remote_copy(
                src_ref=src, dst_ref=at(outs[oi], dst_at(me)), send_sem=send_sems.at[k], recv_sem=recv_sems.at[k],
                device_id=peer, device_id_type=MESH))
            recvs.append(pltpu.make_async_remote_copy(
                src_ref=src, dst_ref=at(outs[oi], dst_at(peer)), send_sem=send_sems.at[k], recv_sem=recv_sems.at[k],
                device_id=peer, device_id_type=MESH))
        locs = [pltpu.make_async_copy(at(ins[ii], src_at(me)), at(outs[oi], dst_at(me)), local_sems.at[k])
                for k, (ii, src_at, oi, dst_at) in enumerate(local)]
        for cp in locs + sends:
            cp.start()
        for cp in recvs:
            cp.wait_recv()
        for cp in sends:
            cp.wait_send()
        for cp in locs:
            cp.wait()

    hbm = pl.BlockSpec(memory_space=pl.ANY)
    return pl.pallas_call(
        body, out_shape=list(out_shapes), in_specs=[hbm] * n_in, out_specs=[hbm] * n_out,
        scratch_shapes=[pltpu.SemaphoreType.DMA((n_rem,)), pltpu.SemaphoreType.DMA((n_rem,)),
                        pltpu.SemaphoreType.DMA((max(n_loc, 1),))],
        input_output_aliases=aliases or {}, name=name)(*arrays)


ALL_FLIPS = [(dx, dy, dc) for dx in (0, 1) for dy in (0, 1) for dc in (0, 1)][1:]
CHIP_FLIPS = [(1, 0, 0), (0, 1, 0), (1, 1, 0)]
CORE_FLIP = (0, 0, 1)


def _dev_index(p):
    return 4 * p[0] + 2 * p[1] + p[2]


def _chip_index(p):
    return 2 * p[0] + p[1]


def _gather(xs, flips, index, n, name):
    arrays = [x[None] for x in xs]
    outs = [jax.ShapeDtypeStruct((n,) + x.shape, x.dtype) for x in xs]
    remote = [(f, a, lambda me, peer: (0,), a, lambda s: (index(s),)) for a in range(len(xs)) for f in flips]
    local = [(a, lambda me: (0,), a, lambda me: (index(me),)) for a in range(len(xs))]
    return exchange(arrays, outs, remote, local, name)


def allgather_devices(x, name):
    return _gather([x], ALL_FLIPS, _dev_index, N_DEV, name)[0]


def allgather_chips(xs, name):
    return _gather(xs, CHIP_FLIPS, _chip_index, N_CHIP, name)


def gather_halves(xs, name):
    n = len(xs)
    nk = n * len(CHIP_FLIPS)

    def body(*refs):
        ins, outs = refs[:n], refs[n:2 * n]
        ici_send, ici_recv, d2d_send, d2d_recv = refs[2 * n:]
        me = _coords()
        sibling = (me[0], me[1], 1 - me[2])
        first, passed, landed = [], [], []
        for a in range(n):
            half = ins[a].shape[0] // 2
            mine = ins[a].at[pl.ds(pl.multiple_of(me[2] * half, 16), half)]
            for j, flip in enumerate(CHIP_FLIPS):
                k = a * len(CHIP_FLIPS) + j
                peer = (me[0] ^ flip[0], me[1] ^ flip[1], me[2])
                first.append(pltpu.make_async_remote_copy(
                    src_ref=mine, dst_ref=outs[a].at[_chip_index(me), me[2]], send_sem=ici_send.at[k],
                    recv_sem=ici_recv.at[k], device_id=peer, device_id_type=MESH))
                arrived = outs[a].at[_chip_index(peer), me[2]]
                landed.append(pltpu.make_async_remote_copy(
                    src_ref=mine, dst_ref=arrived, send_sem=ici_send.at[k], recv_sem=ici_recv.at[k],
                    device_id=peer, device_id_type=MESH))
                passed.append(pltpu.make_async_remote_copy(
                    src_ref=arrived, dst_ref=arrived, send_sem=d2d_send.at[k], recv_sem=d2d_recv.at[k],
                    device_id=sibling, device_id_type=MESH))
        for cp in first:
            cp.start()
        for k in range(nk):
            landed[k].wait_recv()
            passed[k].start()
        for a in range(n):
            for j, flip in enumerate(CHIP_FLIPS):
                k = a * len(CHIP_FLIPS) + j
                peer_chip = _chip_index((me[0] ^ flip[0], me[1] ^ flip[1]))
                from_sibling = outs[a].at[peer_chip, 1 - me[2]]
                pltpu.make_async_remote_copy(
                    src_ref=from_sibling, dst_ref=from_sibling, send_sem=d2d_send.at[k], recv_sem=d2d_recv.at[k],
                    device_id=sibling, device_id_type=MESH).wait_recv()
        for cp in first + passed:
            cp.wait_send()

    hbm = pl.BlockSpec(memory_space=pl.ANY)
    return pl.pallas_call(
        body, out_shape=[jax.ShapeDtypeStruct((N_CHIP, 2, x.shape[0] // 2, x.shape[1]), x.dtype) for x in xs],
        in_specs=[hbm] * n, out_specs=[hbm] * n,
        scratch_shapes=[pltpu.SemaphoreType.DMA((nk,)) for _ in range(4)], name=name)(*xs)


HBM_SPEC = pl.BlockSpec(memory_space=pltpu.HBM)
SEM_SPEC = pl.BlockSpec(memory_space=pltpu.SEMAPHORE)
DATAFLOW = pltpu.SideEffectType.DATAFLOW_SIDE_EFFECTING


def _at(ref, idx):
    return ref if idx is None else ref.at[idx]


def _peer(me, flip):
    return (me[0] ^ flip[0], me[1] ^ flip[1], me[2] ^ flip[2])


def exchange_start(arrays, land_shapes, remote, name, after=None):
    n_in, n_out, nk = len(arrays), len(land_shapes), len(remote)
    after = list(after or [])
    n_after = len(after)

    def body(*refs):
        srcs, lands = refs[:n_in], refs[n_in:n_in + n_out]
        first_out = n_in + n_out + n_after
        send_sems, recv_sems, token = refs[first_out], refs[first_out + 1], refs[-1]
        me = _coords()
        for k, (flip, ii, src_at, oi, dst_at) in enumerate(remote):
            peer = _peer(me, flip)
            pltpu.make_async_remote_copy(
                src_ref=_at(srcs[ii], src_at(me, peer)), dst_ref=_at(lands[oi], dst_at(me)), send_sem=send_sems.at[k],
                recv_sem=recv_sems.at[k], device_id=peer, device_id_type=MESH).start()
        token[...] = jnp.zeros_like(token)

    lands = [lax.empty(s.shape, s.dtype) for s in land_shapes]
    bufs = list(arrays) + lands
    out = pl.pallas_call(
        body, name=name,
        out_shape=(pltpu.SemaphoreType.DMA((nk,)), pltpu.SemaphoreType.DMA((nk,)),
                   *[pltpu.HBM(b.shape, b.dtype) for b in bufs], jax.ShapeDtypeStruct((8, 128), F32)),
        in_specs=[HBM_SPEC] * len(bufs) + [pl.BlockSpec(memory_space=pl.ANY)] * n_after,
        out_specs=(SEM_SPEC, SEM_SPEC, *[HBM_SPEC] * len(bufs), pl.BlockSpec(memory_space=pltpu.VMEM)),
        input_output_aliases={a: 2 + a for a in range(len(bufs))},
        compiler_params=pltpu.CompilerParams(has_side_effects=DATAFLOW),
    )(*[pltpu.with_memory_space_constraint(b, pltpu.HBM) for b in bufs], *after)
    flight = (out[0], out[1], list(out[2:2 + n_in]), list(out[2 + n_in:2 + n_in + n_out]), remote)
    return flight, out[-1]


def exchange_wait(flight, after, name):
    send_sems, recv_sems, arrays, lands, remote = flight
    n_in, n_out = len(arrays), len(lands)
    after = list(after) if isinstance(after, (list, tuple)) else [after]

    def body(*refs):
        srcs, lnds = refs[:n_in], refs[n_in:n_in + n_out]
        s_sems, r_sems = refs[n_in + n_out], refs[n_in + n_out + 1]
        me = _coords()
        for k, (flip, ii, src_at, oi, dst_at) in enumerate(remote):
            peer = _peer(me, flip)
            copy = pltpu.make_async_remote_copy(
                src_ref=_at(srcs[ii], src_at(me, peer)), dst_ref=_at(lnds[oi], dst_at(peer)), send_sem=s_sems.at[k],
                recv_sem=r_sems.at[k], device_id=peer, device_id_type=MESH)
            copy.wait_send()
            copy.wait_recv()

    bufs = list(arrays) + list(lands)
    out = pl.pallas_call(
        body, name=name,
        out_shape=tuple(pltpu.HBM(b.shape, b.dtype) for b in bufs),
        in_specs=[HBM_SPEC] * len(bufs) + [SEM_SPEC, SEM_SPEC] + [pl.BlockSpec(memory_space=pl.ANY)] * len(after),
        out_specs=tuple([HBM_SPEC] * len(bufs)),
        input_output_aliases={a: a for a in range(len(bufs))},
        compiler_params=pltpu.CompilerParams(has_side_effects=DATAFLOW),
    )(*bufs, send_sems, recv_sems, *after)
    return list(out[:n_in]), list(out[n_in:])


def _half_tile(h, cd):
    return h if h * cd * 4 <= (1 << 20) else math.gcd(512, h)


def pair_add(g, got, core, out_dtype, name):
    _, _, h, cd = g.shape
    th = _half_tile(h, cd)

    def body(c_ref, g_ref, got_ref, o_ref):
        o_ref[0] = (g_ref[0, 0] + got_ref[0]).astype(o_ref.dtype)

    return pl.pallas_call(
        body, out_shape=jax.ShapeDtypeStruct((N_CHIP, h, cd), out_dtype),
        grid_spec=pltpu.PrefetchScalarGridSpec(
            num_scalar_prefetch=1, grid=(N_CHIP, h // th),
            in_specs=[pl.BlockSpec((1, 1, th, cd), lambda q, i, c: (q, c[0], i, 0)),
                      pl.BlockSpec((1, th, cd), lambda q, i, c: (q, i, 0))],
            out_specs=pl.BlockSpec((1, th, cd), lambda q, i, c: (q, i, 0))),
        compiler_params=_params(("parallel", "parallel"), VMEM_LIMIT), name=name)(core, g, got)


def sum_chips(parts, sums, place, name):
    _, h, cd = parts.shape
    th = _half_tile(h, cd)

    def body(pc_ref, p_ref, own_ref, o_ref):
        acc = None
        for q in range(N_CHIP):
            term = jnp.where(pc_ref[1] == q, own_ref[0], p_ref[q]).astype(F32)
            acc = term if acc is None else acc + term
        o_ref[0] = acc

    return pl.pallas_call(
        body, out_shape=jax.ShapeDtypeStruct((2, h, cd), F32),
        grid_spec=pltpu.PrefetchScalarGridSpec(
            num_scalar_prefetch=1, grid=(h // th,),
            in_specs=[pl.BlockSpec((N_CHIP, th, cd), lambda i, pc: (0, i, 0)),
                      pl.BlockSpec((1, th, cd), lambda i, pc: (pc[1], i, 0))],
            out_specs=pl.BlockSpec((1, th, cd), lambda i, pc: (pc[0], i, 0))),
        compiler_params=_params(("parallel",), VMEM_LIMIT), name=name)(place, parts, sums)


def to_segments(a, n_ctx):
    def one(p):
        n = p.shape[0]
        return p.reshape(N_SEG, n // N_SEG, -1).transpose(1, 0, 2).reshape(n, -1)
    return jnp.concatenate([one(a[:n_ctx]), one(a[n_ctx:])], axis=0) if n_ctx else one(a)


def rows_to_segments(a, n_ctx, name):
    n, d = a.shape
    tj = n_ctx // N_SEG
    per_seg = (n - n_ctx) // N_SEG // tj
    assert n_ctx % N_SEG == 0 and (n - n_ctx) % (N_SEG * tj) == 0

    def body(*refs):
        out_ref, slabs = refs[N_SEG:]
        for c in range(d // HEAD_LANES):
            cols = slice(c * HEAD_LANES, (c + 1) * HEAD_LANES)
            for seg in range(N_SEG):
                slabs[c, pl.ds(seg, tj, stride=N_SEG), :] = refs[seg][:, cols]
            out_ref[:, cols] = slabs[c]

    def seg_spec(seg):
        return pl.BlockSpec((tj, d), lambda i: (jnp.where(i == 0, seg, N_SEG + seg * per_seg + i - 1), 0))

    return pl.pallas_call(
        body, out_shape=jax.ShapeDtypeStruct((n, d), a.dtype), grid=(1 + per_seg,),
        in_specs=[seg_spec(seg) for seg in range(N_SEG)], out_specs=pl.BlockSpec((N_SEG * tj, d), lambda i: (i, 0)),
        scratch_shapes=[pltpu.VMEM((d // HEAD_LANES, N_SEG * tj, HEAD_LANES), a.dtype)],
        compiler_params=_params(("parallel",), VMEM_LIMIT), name=name)(*[a] * N_SEG)


def rope_tables(n_ctx, n_lat):
    f32 = np.float32
    rows = n_lat // GRID_W
    row = np.repeat(np.arange(rows), GRID_W).astype(f32)
    col = np.tile(np.arange(GRID_W), rows).astype(f32)
    d = QK_ROPE_DIM // 2
    inv = (f32(1.0) / np.power(f32(ROPE_THETA), np.arange(0, d, 2, dtype=f32) / f32(d))).astype(f32)
    ang = np.concatenate([row[:, None] * inv[None, :], col[:, None] * inv[None, :]], axis=1).astype(f32)
    cos = np.concatenate([np.ones((n_ctx, d), f32), np.cos(ang)], axis=0)
    sin = np.concatenate([np.zeros((n_ctx, d), f32), np.sin(ang)], axis=0)
    q = QK_ROPE_DIM // 4
    T = n_ctx + n_lat
    ones, zeros = np.ones((T, QK_NOPE_DIM), f32), np.zeros((T, QK_NOPE_DIM), f32)
    tail, z8 = np.zeros((T, HEAD_LANES - QK_DIM), f32), np.zeros((T, q), f32)
    cr, cc, sr, sc = cos[:, :q], cos[:, q:], sin[:, :q], sin[:, q:]
    cos_t = np.concatenate([ones, cr, cr, cc, cc, tail], axis=1)
    sin_next = np.concatenate([zeros, -sr, z8, -sc, z8, tail], axis=1)
    sin_prev = np.concatenate([zeros, z8, sr, z8, sc, tail], axis=1)
    return tuple(jnp.asarray(t, F32) for t in (cos_t, sin_next, sin_prev))


def pad_heads(w, used):
    k = w.shape[0]
    return jnp.pad(w.reshape(k, MLA_HEADS, used), ((0, 0), (0, 0), (0, HEAD_LANES - used))).reshape(k, -1)


def unpad_heads(w, used):
    k = w.shape[0]
    return w.reshape(k, MLA_HEADS, HEAD_LANES)[:, :, :used].reshape(k, MLA_HEADS * used)


def rotary_spread():
    lane = np.arange(MLA_HEADS * HEAD_LANES) % HEAD_LANES
    return jnp.asarray(lane[None, :] == (QK_NOPE_DIM + np.arange(QK_ROPE_DIM))[:, None], BF16)


def s5_discretise(a_re, a_im, log_step, b_re, b_im):
    dt = jnp.exp(log_step)[:, None]
    mag = jnp.exp(a_re * dt)
    lb_re = mag * jnp.cos(a_im * dt)
    lb_im = mag * jnp.sin(a_im * dt)
    den = a_re * a_re + a_im * a_im
    nr = lb_re - 1.0
    f_re = ((nr * a_re + lb_im * a_im) / den)[:, None, :]
    f_im = ((lb_im * a_re - nr * a_im) / den)[:, None, :]
    return lb_re, lb_im, f_re * b_re - f_im * b_im, f_re * b_im + f_im * b_re


def s5_block_weights(lb_re, lb_im, bb_re, bb_im, c_re, c_im):
    eye = jnp.eye(GROUPS_PER_BLOCK, dtype=F32)
    lam = jnp.stack([lb_re.reshape(1, S5_LANES), lb_im.reshape(1, S5_LANES)])

    def b_blocks(bb):
        t = bb.reshape(N_BLOCKS, GROUPS_PER_BLOCK, S5_GROUP, S5_STATE)
        return jnp.einsum("bgcp,gh->bgchp", t, eye).reshape(N_BLOCKS, BLK_CH, BLK_ST).astype(BF16)

    def c_blocks(cc):
        t = cc.reshape(N_BLOCKS, GROUPS_PER_BLOCK, S5_GROUP, S5_STATE)
        return jnp.einsum("bgcp,gh->bgphc", t, eye).reshape(N_BLOCKS, BLK_ST, BLK_CH).astype(BF16)

    return lam, b_blocks(bb_re), b_blocks(bb_im), c_blocks(c_re), c_blocks(c_im)


def b_block_diag(db):
    t = db.reshape(N_BLOCKS, GROUPS_PER_BLOCK, S5_GROUP, GROUPS_PER_BLOCK, S5_STATE)
    return jnp.einsum("bgchp,gh->bgcp", t, jnp.eye(GROUPS_PER_BLOCK, dtype=F32)).reshape(S5_GROUPS, S5_GROUP, S5_STATE)


def c_block_diag(dc):
    t = dc.reshape(N_BLOCKS, GROUPS_PER_BLOCK, S5_STATE, GROUPS_PER_BLOCK, S5_GROUP)
    return jnp.einsum("bgphc,gh->bgcp", t, jnp.eye(GROUPS_PER_BLOCK, dtype=F32)).reshape(S5_GROUPS, S5_GROUP, S5_STATE)


def conj(a):
    return jnp.stack([a[0], -a[1]])


def _narrow(shape):
    return len(shape) >= 2 and shape[-1] < min(HEAD_LANES, shape[-2])


def _stored(a):
    return jnp.swapaxes(a, -1, -2) if _narrow(a.shape) else a


def _stored_shape(shape):
    return tuple(shape[:-2]) + (shape[-1], shape[-2]) if _narrow(shape) else tuple(shape)


def _from_stored(a, shape):
    return jnp.swapaxes(a, -1, -2) if _narrow(shape) else a


PACK_TILE = 16 * 128


def pack_flat(parts, dtype):
    flat = [p.reshape(-1).astype(dtype) for p in parts]
    sizes = [f.shape[0] for f in flat]
    total = sum(sizes)
    pad = (-total) % PACK_TILE
    if pad:
        flat.append(jnp.zeros((pad,), dtype))
    offs = np.cumsum([0] + sizes)[:-1].tolist()
    return jnp.concatenate(flat).reshape(-1, 128), offs


def unpack_flat(buf, offs, shapes):
    flat = buf.reshape(-1)
    return [flat[o:o + int(np.prod(s))].reshape(s) for o, s in zip(offs, shapes)]


def s5_forward(p1, n_ctx, dirs):
    saved = []
    y = None
    ctx_rows, lat_rows = (0, n_ctx), (n_ctx, p1.shape[0] - n_ctx)
    zeros_tile = jnp.zeros((2, N_SEG, S5_LANES), F32)
    zeros_row = jnp.zeros((2, 1, S5_LANES), F32)
    for k, (lam, b_re, b_im, c_re, c_im) in enumerate(dirs):
        rev = k == 1
        last = 0 if rev else N_SEG - 1
        _, _, fin = s5_scan(p1, b_re, b_im, lam, zeros_tile, reverse=rev, rows=ctx_rows, name=f"s5_ctx_finals{k}")
        carry_c = s5_chain(fin, zeros_row, lam, n_ctx // N_SEG, rev, name=f"s5_ctx_chain{k}")
        _, ck_c, fin_c = s5_scan(p1, b_re, b_im, lam, carry_c, reverse=rev, want_ckpt=True, rows=ctx_rows,
                                 name=f"s5_ctx_scan{k}")
        s0 = fin_c[:, last:last + 1, :]
        _, _, fin = s5_scan(p1, b_re, b_im, lam, zeros_tile, reverse=rev, rows=lat_rows, name=f"s5_lat_finals{k}")
        carry_l = s5_chain(fin, s0, lam, lat_rows[1] // N_SEG, rev, name=f"s5_lat_chain{k}")
        y, ck_l, _ = s5_scan(p1, b_re, b_im, lam, carry_l, reverse=rev, c_re=c_re, c_im=c_im, add=y,
                             want_ckpt=True, rows=lat_rows, name=f"s5_lat_scan{k}")
        saved.append((ck_c, ck_l))
    return y, saved


def s5_backward(dy_l, du_extra_l, p1, n_ctx, dirs, saved):
    n_lat = p1.shape[0] - n_ctx
    zeros_tile = jnp.zeros((2, N_SEG, S5_LANES), F32)
    zeros_row = jnp.zeros((2, 1, S5_LANES), F32)
    dy_c = jnp.zeros((n_ctx, D_MODEL), F32)
    du_l, du_c = du_extra_l, None
    grads = []
    for k, (lam, b_re, b_im, c_re, c_im) in enumerate(dirs):
        rev = k == 1
        lam_c = conj(lam)
        ck_c, ck_l = saved[k]
        first = N_SEG - 1 if rev else 0
        _, _, fin = s5_scan(dy_l, c_re, c_im, lam_c, zeros_tile, reverse=not rev, adjoint=True,
                            name=f"s5_lat_adj_finals{k}")
        carry = s5_chain(fin, zeros_row, lam_c, n_lat // N_SEG, not rev, name=f"s5_lat_adj_chain{k}")
        whole = k == len(dirs) - 1
        du_l, dlam_l, dbr_l, dbi_l, dcr_l, dci_l, fin_a = s5_grads(
            dy_l, p1, ck_l, b_re, b_im, c_re, c_im, lam, carry, reverse=rev, add=du_l, u_off=n_ctx,
            du_rows=(p1.shape[0], n_ctx) if whole else None, name=f"s5_lat_grads{k}")
        g0 = fin_a[:, first:first + 1, :]
        carry = s5_chain(zeros_tile, g0, lam_c, n_ctx // N_SEG, not rev, name=f"s5_ctx_adj_chain{k}")
        du_c, dlam_c, dbr_c, dbi_c, _, _, _ = s5_grads(
            dy_c, p1, ck_c, b_re, b_im, c_re, c_im, lam, carry, reverse=rev, add=du_c,
            du_rows=(p1.shape[0], 0) if whole else None, du_into=du_l if whole else None, name=f"s5_ctx_grads{k}")
        dlam = jnp.sum(dlam_l + dlam_c, axis=1)
        grads.append((dlam, b_block_diag(dbr_l + dbr_c), b_block_diag(dbi_l + dbi_c),
                      c_block_diag(dcr_l), c_block_diag(dci_l)))
    return du_c, grads


def local_step(x, ctx, target, mod, w, late=None, reducer=None):
    L, Lc = x.shape[0], ctx.shape[0]
    T = L + Lc
    assert L % Lc == 0 and Lc % (2 * N_SEG) == 0 and L % GRID_W == 0
    D = D_MODEL
    X0 = (ctx, x)

    def mod_of(i, j):
        return mod[i, :, j, :][:, None, :]

    def vec(v):
        return v.reshape(1, 1, -1).astype(F32)

    g0 = vec(w["norm_g"][0])
    H0, p0 = norm_proj(X0, g0, mod_of(0, 1), mod_of(0, 0), w["mla_w_in"], Lc, "l0_norm_in")
    cq = Rows(p0, Q_LORA_RANK, col_blk=D // Q_LORA_RANK)
    ckv = Rows(p0, KV_LORA_RANK, col_blk=(D + Q_LORA_RANK) // KV_LORA_RANK)
    kr = p0[:, D + Q_LORA_RANK + KV_LORA_RANK:D + P0_HEAD]
    qng, kvng = vec(w["mla_q_norm"]), vec(w["mla_kv_norm"])
    tabs = rope_tables(Lc, L)
    spread = rotary_spread()
    if late is not None:
        w = {**w, **late["qkv"](p0)}
    w_uq_p = pad_heads(w["mla_w_uq"], QK_DIM)
    w_ukv3 = w["mla_w_ukv"].reshape(KV_LORA_RANK, MLA_HEADS, QK_NOPE_DIM + V_HEAD_DIM)
    w_kn_p = pad_heads(w_ukv3[:, :, :QK_NOPE_DIM].reshape(KV_LORA_RANK, -1), QK_NOPE_DIM)
    w_v = w_ukv3[:, :, QK_NOPE_DIM:].reshape(KV_LORA_RANK, -1)
    qb, qn = q_heads(cq, qng, w_uq_p, tabs)
    kb, vb, kvn = kv_heads(ckv, kvng, w_kn_p, w_v, kr, spread, tabs)
    o, lse, qs = attn_fwd(qb, kb, vb, Lc)
    if late is not None:
        w = {**w, **late["out"](o)}
    X1, og, out0 = mla_post_fwd(o, p0, X0, mod_of(0, 2), w["mla_w_out"], Lc)

    if late is not None:
        w = {**w, **late["l1"](X1)}
    X1p = rows_to_segments(X1, Lc, "l1_to_segments")
    tgt_p = to_segments(target, 0)
    g1 = vec(w["norm_g"][1])
    H1, p1 = norm_proj(X1p, g1, mod_of(1, 1), mod_of(1, 0), w["s5_w_in"], Lc, "l1_norm_in")
    disc_fn = lambda *a: tuple(zip(*[s5_discretise(a[0][k], a[1][k], a[2][k], a[3][k], a[4][k]) for k in range(2)]))
    disc, disc_vjp = jax.vjp(disc_fn, w["s5_a_re"], w["s5_a_im"], w["s5_log_step"], _stored(w["s5_b_re"]),
                             _stored(w["s5_b_im"]))
    dirs = [s5_block_weights(disc[0][k], disc[1][k], disc[2][k], disc[3][k], w["s5_c_re"][k], w["s5_c_im"][k])
            for k in range(2)]
    y_ssm, s5_saved = s5_forward(p1, Lc, dirs)

    row = lambda v: v.reshape(1, D).astype(F32)
    (lvec, dX2, d_yssm, d_u_act, d_z1, d_fg, d_gt1, d_bg, d_d, gw_glu, gw_out) = s5_tail(
        y_ssm, p1, X1p, tgt_p, Lc, row(w["s5_d"]), row(w["s5_b_glu"]), mod[1, 1:2, 2, :], row(w["final_g"]),
        w["s5_w_glu"], w["s5_w_out"])
    loss = jnp.sum(lvec)
    gw = {"final_g": d_fg.reshape(D), "s5_b_glu": d_bg.reshape(D), "s5_d": d_d.reshape(D),
          "s5_w_glu": gw_glu, "s5_w_out": gw_out}
    dmod = {}

    du_p, s5_g = s5_backward(d_yssm, d_u_act, p1, Lc, dirs, s5_saved)
    d_disc = tuple(tuple(s5_g[k][j - 1].reshape(disc[j][k].shape) if j >= 2 else
                         s5_g[k][0][j].reshape(disc[j][k].shape) for k in range(2)) for j in range(4))
    gw["s5_a_re"], gw["s5_a_im"], gw["s5_log_step"], d_bt_re, d_bt_im = disc_vjp(d_disc)
    gw["s5_b_re"], gw["s5_b_im"] = jnp.swapaxes(d_bt_re, -1, -2), jnp.swapaxes(d_bt_im, -1, -2)
    gw["s5_c_re"] = jnp.stack([s5_g[0][3], s5_g[1][3]])
    gw["s5_c_im"] = jnp.stack([s5_g[0][4], s5_g[1][4]])
    gw["s5_w_in"] = jnp.concatenate([mm_tn(H1, du_p, name="l1_in_dw_u"), mm_tn(H1, d_z1, name="l1_in_dw_z")],
                                    axis=1)
    if reducer is not None:
        g1 = g1 + reducer["l1"][0]({n: gw.pop(n) for n in LAYER1_MATS})[0, 0]
    d_X1, d_g1, d_sc1, d_sh1 = norm_proj_bwd([(du_p, 0, 0), (d_z1, D, Lc)], w["s5_w_in"], X1p, g1, mod_of(1, 1),
                                             mod_of(1, 0), dX2, Lc, "l1_norm_in_bwd", dx_from_segments=True)
    d_gt1_full = jnp.concatenate([jnp.zeros((1, 1, D), F32), d_gt1[None]], axis=0)
    dmod[1] = (d_sh1, d_sc1, d_gt1_full)

    d_o, d_z0, d_gt0, gw["mla_w_out"] = mla_post_bwd(d_X1, out0, og, o, p0, mod_of(0, 2), w["mla_w_out"], Lc)
    if reducer is not None:
        started = reducer["l1"][1](d_o)[0, 0] + reducer["out"][0]({"mla_w_out": gw.pop("mla_w_out")})[0, 0]
        tabs = (tabs[0] + started,) + tabs[1:]
    d_q, dos = attn_bwd_dq(qb, kb, vb, o, d_o, lse, tabs, Lc)
    dk_p, d_v = attn_bwd_dkv(qs, kb, vb, dos, Lc)
    if reducer is not None:
        tabs = (tabs[0] + reducer["out"][1](d_q)[0, 0],) + tabs[1:]
    d_k, d_kr = heads_unrope(dk_p, tabs, jnp.pad(spread, ((0, HEAD_LANES - QK_ROPE_DIM), (0, 0))),
                             name="l0_k_unrope")
    d_qn = mm_nt(d_q, w_uq_p, name="l0_uq_dx")
    gw["mla_w_uq"] = unpad_heads(mm_tn(qn, d_q, name="l0_uq_dw"), QK_DIM)
    d_kvn = mm_nt(d_k, w_kn_p, name="l0_ukn_dx") + mm_nt(d_v, w_v, name="l0_uv_dx")
    dw_kn = unpad_heads(mm_tn(kvn, d_k, name="l0_ukn_dw"), QK_NOPE_DIM).reshape(KV_LORA_RANK, MLA_HEADS, QK_NOPE_DIM)
    dw_v = mm_tn(kvn, d_v, name="l0_uv_dw").reshape(KV_LORA_RANK, MLA_HEADS, V_HEAD_DIM)
    gw["mla_w_ukv"] = jnp.concatenate([dw_kn, dw_v], axis=-1).reshape(KV_LORA_RANK, -1)
    d_cq, d_qng = rowwise_bwd(f_rms, [cq], [qng], [d_qn], [0], [0], T, 0, "l0_qnorm_bwd")
    d_ckv, d_kvng = rowwise_bwd(f_rms, [ckv], [kvng], [d_kvn], [0], [0], T, 0, "l0_kvnorm_bwd")
    gw["mla_q_norm"] = d_qng.reshape(-1)
    gw["mla_kv_norm"] = d_kvng.reshape(-1)
    o_cq, o_ckv = D, D + Q_LORA_RANK
    o_kr = o_ckv + KV_LORA_RANK
    d_head = jnp.concatenate([d_cq, d_ckv, d_kr], axis=1)
    gw["mla_w_in"] = jnp.concatenate([mm_tn(H0, d_head, name="l0_in_dw_head")[:, :P0_HEAD],
                                      mm_tn(H0, d_z0, name="l0_in_dw_z")], axis=1)
    dx, d_g0, d_sc0, d_sh0 = norm_proj_bwd(
        [(d_z0, 0, 0), (d_cq, o_cq, 0), (d_ckv, o_ckv, 0), (d_kr, o_kr, 0)], w["mla_w_in"], X0, g0, mod_of(0, 1),
        mod_of(0, 0), d_X1, Lc, "l0_norm_in_bwd", latent_dx_only=True)
    dmod[0] = (d_sh0, d_sc0, d_gt0)
    gw["norm_g"] = jnp.stack([d_g0.reshape(D), d_g1.reshape(D)])
    dmod_arr = jnp.stack([jnp.stack([dmod[i][j][:, 0, :] for j in range(3)], axis=1) for i in range(2)])
    ready = {**reducer["l1"][2](dx), **reducer["out"][2](dx)} if reducer is not None else {}
    return loss, dx, dmod_arr, gw, ready


SHARDED = {
    "mla_w_in": 1, "mla_w_uq": 1, "mla_w_ukv": 1, "mla_w_out": 0,
    "s5_w_in": 1, "s5_w_glu": 0, "s5_w_out": 0, "s5_d": 0, "s5_b_glu": 0,
}
SHARDED_MATS = ["mla_w_in", "mla_w_uq", "mla_w_ukv", "mla_w_out", "s5_w_in", "s5_w_glu", "s5_w_out"]
SHARDED_VECS = ["s5_d", "s5_b_glu"]
REPLICATED = ["norm_g", "mla_q_norm", "mla_kv_norm", "s5_a_re", "s5_a_im", "s5_log_step", "s5_b_re", "s5_b_im",
              "s5_c_re", "s5_c_im", "final_g"]
WEIGHT_ORDER = ["c_ctx", "ada_w", "ada_b", "norm_g", "mla_w_in", "mla_q_norm", "mla_w_uq", "mla_kv_norm", "mla_w_ukv",
                "mla_w_out", "s5_w_in", "s5_a_re", "s5_a_im", "s5_log_step", "s5_b_re", "s5_b_im", "s5_c_re", "s5_c_im",
                "s5_d", "s5_w_glu", "s5_b_glu", "s5_w_out", "final_g"]


P0_HEAD = Q_LORA_RANK + KV_LORA_RANK + QK_ROPE_DIM


P0_WIDTH = 1536


def w_in_to_kernel_order(w):
    pad = jnp.zeros((w.shape[0], P0_WIDTH - w.shape[1]), w.dtype)
    return jnp.concatenate([w[:, P0_HEAD:], w[:, :P0_HEAD], pad], axis=1)


LAYER0_MATS = ["mla_w_in", "mla_w_uq", "mla_w_ukv", "mla_w_out"]
LAYER1_MATS = ["s5_w_in", "s5_w_glu", "s5_w_out"]


def _whole_matrices(names, own_blocks, gathered):
    chip = _chip_index(_coords())
    full = {}
    for n, own, o in zip(names, own_blocks, gathered):
        slot = lax.broadcasted_iota(jnp.int32, (N_CHIP, 1, 1), 0)
        o = jnp.where(slot == chip, own[None], o.reshape((N_CHIP,) + own.shape))
        full[n] = o.reshape(-1, o.shape[-1]) if SHARDED[n] == 0 else o.transpose(1, 0, 2).reshape(o.shape[1], -1)
    return full


FIRST_MATS = ["mla_w_in"]
LATER_GROUPS = {"qkv": ["mla_w_uq", "mla_w_ukv"], "out": ["mla_w_out"], "l1": LAYER1_MATS}


def gather_weights(ws):
    mats = [ws[n].astype(BF16) for n in FIRST_MATS]
    full = _whole_matrices(FIRST_MATS, mats, gather_halves(mats, "gather_weights"))
    full["mla_w_in"] = w_in_to_kernel_order(full["mla_w_in"])
    return full


def gather_weights_behind(ws, after):
    token, finish = 0.0, {}
    for group, names in LATER_GROUPS.items():
        mats = [ws[n].astype(BF16) for n in names]
        flight, tok = exchange_start(
            mats, [jax.ShapeDtypeStruct((N_CHIP,) + m.shape, m.dtype) for m in mats],
            [(f, a, lambda me, peer: None, a, lambda s: (_chip_index(s),))
             for a in range(len(mats)) for f in CHIP_FLIPS], f"gather_{group}_start", after=after)
        after = [tok]
        token = token + tok[0, 0]

        def finish_group(after_work, group=group, names=names, flight=flight):
            own, got = exchange_wait(flight, after_work, f"gather_{group}_wait")
            return _whole_matrices(names, own, got)

        finish[group] = finish_group
    return token, finish


def _grad_slots(gw, names):
    slots = []
    for n in names:
        g = gw[n]
        if SHARDED[n] == 0:
            slots.append(g.reshape(N_CHIP, 2, g.shape[0] // (2 * N_CHIP), g.shape[1]))
        else:
            k, n4 = g.shape
            slots.append(g.reshape(k, N_CHIP, n4 // N_CHIP).transpose(1, 0, 2)
                         .reshape(N_CHIP, 2, k // 2, n4 // N_CHIP))
    return slots


def _to_sibling_half(count):
    return [(CORE_FLIP, i, lambda me, peer: (slice(None), 1 - me[2]), i, lambda s: None) for i in range(count)]


def _to_chips(count):
    return [(f, i, lambda me, peer: (_chip_index(peer),), i, lambda s: (_chip_index(s),))
            for i in range(count) for f in CHIP_FLIPS]


def _place():
    me = _coords()
    return jnp.stack([me[2], _chip_index(me)]).astype(jnp.int32)


def reduce_behind(names, tag):
    state = {}
    count = len(names)

    def begin(gw):
        slots = _grad_slots(gw, names)
        lands = [jax.ShapeDtypeStruct((N_CHIP,) + s.shape[2:], F32) for s in slots]
        state["in"], token = exchange_start(slots, lands, _to_sibling_half(count), f"grads_{tag}_swap_in_start")
        return token

    def middle(after):
        slots, got = exchange_wait(state["in"], after, f"grads_{tag}_swap_in_wait")
        place = _place()
        sums = [pair_add(s, g, place[:1], BF16, f"grads_pair_{n}") for n, s, g in zip(names, slots, got)]
        lands = [jax.ShapeDtypeStruct(s.shape, s.dtype) for s in sums]
        state["out"], token = exchange_start(sums, lands, _to_chips(count), f"grads_{tag}_scatter_start")
        return token

    def end(after):
        sums, parts = exchange_wait(state["out"], after, f"grads_{tag}_scatter_wait")
        place = _place()
        return {n: sum_chips(p, s, place, f"grads_sum_{n}") for n, p, s in zip(names, parts, sums)}

    return begin, middle, end


def reduce_gradients(gw, ready_halves):
    me = _coords()
    place = _place()
    mat_names = [n for n in SHARDED_MATS if n not in ready_halves]
    slots = dict(zip(mat_names, _grad_slots(gw, mat_names)))
    small_names = REPLICATED + SHARDED_VECS
    small, small_offs = pack_flat([_stored(gw[n]).astype(F32) for n in small_names], F32)
    small = jnp.pad(small, ((0, (-small.shape[0]) % (N_CHIP * 32)), (0, 0)))
    slots["small"] = small.reshape(N_CHIP, 2, -1, 128)
    names = list(slots)
    count = len(names)
    got = exchange([slots[n] for n in names],
                   [jax.ShapeDtypeStruct((N_CHIP,) + slots[n].shape[2:], F32) for n in names],
                   _to_sibling_half(count), [], "grads_swap_in")
    sums = [pair_add(slots[n], g, place[:1], F32 if n == "small" else BF16, f"grads_pair_{n}")
            for n, g in zip(names, got)]
    parts = exchange(sums, [jax.ShapeDtypeStruct(s.shape, s.dtype) for s in sums], _to_chips(count), [],
                     "grads_scatter")
    halves = {n: sum_chips(p, s, place, f"grads_sum_{n}") for n, p, s in zip(names, parts, sums)}
    halves.update(ready_halves)
    all_names = list(halves)
    fulls = exchange(
        [halves[n] for n in all_names], [jax.ShapeDtypeStruct(halves[n].shape, F32) for n in all_names],
        [(CORE_FLIP, i, lambda me, peer: (me[2],), i, lambda s: (s[2],)) for i in range(len(all_names))], [],
        "grads_swap_out", aliases={i: i for i in range(len(all_names))})
    out = {n: f.reshape(-1, f.shape[-1]) for n, f in zip(all_names, fulls)}
    quarter = out.pop("small")
    gather, token = exchange_start(
        [quarter], [jax.ShapeDtypeStruct((N_CHIP,) + quarter.shape, F32)],
        [(f, 0, lambda me, peer: None, 0, lambda s: (_chip_index(s),)) for f in CHIP_FLIPS],
        "grads_gather_small_start")

    def finish_small(after):
        (own,), (got_small,) = exchange_wait(gather, after, "grads_gather_small_wait")
        slot = lax.broadcasted_iota(jnp.int32, (N_CHIP, 1, 1), 0)
        small_all = jnp.where(slot == _chip_index(me), own[None], got_small)
        vals = unpack_flat(small_all, small_offs, [_stored_shape(gw[n].shape) for n in small_names])
        res = {}
        for n, v in zip(small_names, vals):
            v = _from_stored(v, gw[n].shape)
            if n in SHARDED_VECS:
                size = v.shape[0] // N_CHIP
                v = lax.dynamic_slice_in_dim(v, _chip_index(me) * size, size)
            res[n] = v
        return res

    return out, finish_small, token


def kernel(x, c, ctx, c_ctx, ada_w, ada_b, norm_g, mla_w_in, mla_q_norm, mla_w_uq, mla_kv_norm, mla_w_ukv, mla_w_out, s5_w_in, s5_a_re, s5_a_im, s5_log_step, s5_b_re, s5_b_im, s5_c_re, s5_c_im, s5_d, s5_w_glu, s5_b_glu, s5_w_out, final_g, loss_target, m_c_ctx, m_ada_w, m_ada_b, m_norm_g, m_mla_w_in, m_mla_q_norm, m_mla_w_uq, m_mla_kv_norm, m_mla_w_ukv, m_mla_w_out, m_s5_w_in, m_s5_a_re, m_s5_a_im, m_s5_log_step, m_s5_b_re, m_s5_b_im, m_s5_c_re, m_s5_c_im, m_s5_d, m_s5_w_glu, m_s5_b_glu, m_s5_w_out, m_final_g, v_c_ctx, v_ada_w, v_ada_b, v_norm_g, v_mla_w_in, v_mla_q_norm, v_mla_w_uq, v_mla_kv_norm, v_mla_w_ukv, v_mla_w_out, v_s5_w_in, v_s5_a_re, v_s5_a_im, v_s5_log_step, v_s5_b_re, v_s5_b_im, v_s5_c_re, v_s5_c_im, v_s5_d, v_s5_w_glu, v_s5_b_glu, v_s5_w_out, v_final_g):
    args = dict(locals())
    weights = {n: args[n] for n in WEIGHT_ORDER}
    D = D_MODEL
    xi, yi, ci = _coords()
    chip = 2 * xi + yi
    me = 4 * xi + 2 * yi + ci
    n_col = ada_w.shape[2]

    c_all = allgather_devices(jnp.pad(c, ((0, 7), (0, 0))), "gather_c")[:, 0, :]
    cond = jnp.concatenate([c_all, jnp.broadcast_to(c_ctx[None], (8, D))], axis=0)
    (s_cond,) = rowwise_fwd(lambda v: (_silu(v),), [cond], [], [D], [F32], 16, 0, "cond_silu")
    ada_rows = ada_w.reshape(2 * D, n_col)
    mod_cols = jnp.stack([mm_nn(s_cond, ada_rows, name=f"mod_proj{i}", b_blk=i) for i in range(2)])
    vec_tiles = [jnp.pad(weights[n][0].reshape(-1, 128), ((0, 6), (0, 0))) for n in SHARDED_VECS]
    mod_all, *vec_all = allgather_chips([mod_cols] + vec_tiles, "gather_mod")
    mod_all = mod_all.transpose(1, 2, 0, 3).reshape(2, 16, 3 * D) + ada_b[:, None, :]
    mine = lax.broadcasted_iota(jnp.int32, (1, 16, 1), 1) == me
    mod_l = jnp.sum(jnp.where(mine, mod_all, 0.0), axis=1)
    mod_c = mod_all[:, 8, :]
    mod = jnp.stack([mod_c.reshape(2, 3, D), mod_l.reshape(2, 3, D)], axis=1)

    w = gather_weights({n: weights[n][0] for n in FIRST_MATS})
    token, late = gather_weights_behind({n: weights[n][0] for n in SHARDED_MATS}, [w["mla_w_in"], mod])
    for n, v in zip(SHARDED_VECS, vec_all):
        w[n] = v[:, :2, :].reshape(-1)
    for n in ["norm_g", "final_g"]:
        w[n] = weights[n]
    for n in ["mla_q_norm", "mla_kv_norm", "s5_a_re", "s5_a_im", "s5_log_step", "s5_b_re", "s5_b_im",
              "s5_c_re", "s5_c_im"]:
        w[n] = weights[n][0]

    reducer = {"l1": reduce_behind(LAYER1_MATS, "l1"), "out": reduce_behind(["mla_w_out"], "out")}
    loss_me, dx, dmod, gw, ready = local_step(x[0], ctx[0], loss_target[0], mod + token, w, late,
                                              reducer)

    dmod_rows, loss_all = _gather([dmod.reshape(2, 2, 3 * D), jnp.broadcast_to(loss_me, (8, 128))],
                                  ALL_FLIPS, _dev_index, N_DEV, "gather_dmod")
    loss = functools.reduce(lambda s, d: s + loss_all[d, 0, 0], range(1, N_DEV), loss_all[0, 0, 0])
    dm = jnp.concatenate([dmod_rows[:, :, 1, :], dmod_rows[:, :, 0, :]], axis=0).transpose(1, 0, 2)
    g_ada_b = jnp.sum(dm, axis=1)
    dm_cols = lax.dynamic_slice_in_dim(dm, chip * n_col, n_col, axis=2)
    g_ada_w = jnp.stack([mm_tn(s_cond, dm_cols[i], name=f"mod_proj_dw{i}") for i in range(2)])
    dmc = jnp.sum(dm_cols[:, 8:, :], axis=1)
    dmc8 = jnp.broadcast_to(dmc[:, None, :], (2, 8, n_col))
    g_sc = (mm_nt(dmc8[0], ada_rows, name="mod_proj_dx0", b_rows=D, b_blk=0)[0]
            + mm_nt(dmc8[1], ada_rows, name="mod_proj_dx1", b_rows=D, b_blk=1)[0])
    g_sc_all = allgather_devices(jnp.broadcast_to(g_sc[None], (8, D)), "gather_dcond")[:, 0, :]
    g_silu_cc = g_sc_all[0] + g_sc_all[2] + g_sc_all[4] + g_sc_all[6]
    (g_c_ctx,) = rowwise_bwd(lambda v: (_silu(v),), [jnp.broadcast_to(c_ctx[None], (8, D))], [],
                             [jnp.broadcast_to(g_silu_cc[None], (8, D))], [0], [], 8, 0, "cond_silu_bwd")
    g_c_ctx = g_c_ctx[0]

    grads = {"c_ctx": g_c_ctx, "ada_w": g_ada_w, "ada_b": g_ada_b}
    deltas, new_m, new_v = {}, {}, {}
    small = [n for n in WEIGHT_ORDER if weights[n].size < 50000]

    def update(n, after=None):
        shp = weights[n].shape
        rows = lambda a: _stored(a.reshape(shp)).reshape(-1, _stored_shape(shp)[-1])
        back = lambda a: _from_stored(a.reshape(_stored_shape(shp)), shp)
        d_, m_, v_ = adamw(rows(weights[n]), rows(grads[n]), rows(args["m_" + n]), rows(args["v_" + n]),
                           name=f"adamw_{n}", after=after)
        deltas[n], new_m[n], new_v[n] = back(d_), back(m_), back(v_)

    red, finish_small, small_started = reduce_gradients(gw, ready)
    update("ada_w", small_started)
    for n in SHARDED_MATS:
        grads[n] = red[n].reshape(weights[n].shape)
        update(n, small_started)
    red_small = finish_small([deltas[n] for n in ["ada_w"] + SHARDED_MATS])
    for n in REPLICATED + SHARDED_VECS:
        grads[n] = red_small[n].reshape(weights[n].shape)
    for n in WEIGHT_ORDER:
        if n not in small and n not in deltas:
            update(n)
    packs = []
    offs = None
    for src in (weights, grads, {n: args["m_" + n] for n in small}, {n: args["v_" + n] for n in small}):
        buf, offs = pack_flat([src[n] for n in small], F32)
        packs.append(buf)
    outs = adamw(*packs, name="adamw_small")
    for res, dst in zip(outs, (deltas, new_m, new_v)):
        for n, val in zip(small, unpack_flat(res, offs, [weights[n].shape for n in small])):
            dst[n] = val

    return (loss, dx[None], *[grads[n] for n in WEIGHT_ORDER], *[deltas[n] for n in WEIGHT_ORDER],
            *[new_m[n] for n in WEIGHT_ORDER], *[new_v[n] for n in WEIGHT_ORDER])
```

```python
import functools
import math

import jax
import jax.numpy as jnp
import numpy as np
from jax import lax
from jax.experimental import pallas as pl
from jax.experimental.pallas import tpu as pltpu

F32 = jnp.float32
BF16 = jnp.bfloat16

D_MODEL = 1024
GRID_W = 64
EPS = 1e-6
MLA_HEADS = 16
QK_NOPE_DIM = 64
QK_ROPE_DIM = 32
V_HEAD_DIM = 64
Q_LORA_RANK = 256
KV_LORA_RANK = 128
QK_DIM = QK_NOPE_DIM + QK_ROPE_DIM
SOFTMAX_SCALE = QK_DIM ** -0.5
ROPE_THETA = 10000.0
S5_GROUP = 16
S5_GROUPS = D_MODEL // S5_GROUP
S5_STATE = 64
S5_LANES = S5_GROUPS * S5_STATE
N_SEG = 8
GROUPS_PER_BLOCK = 8
N_BLOCKS = S5_GROUPS // GROUPS_PER_BLOCK
BLK_CH = GROUPS_PER_BLOCK * S5_GROUP
BLK_ST = GROUPS_PER_BLOCK * S5_STATE

ADAM_LR = 0.001
ADAM_B1 = 0.9
ADAM_B2 = 0.999
ADAM_EPS = 1e-08
ADAM_WD = 0.01
ADAM_STEP = 10

N_DEV = 8
N_CHIP = 4
MESH = pl.DeviceIdType.MESH
VMEM_LIMIT = 52 * 1024 * 1024
ROW_TILE = 256


def _params(sem=None, vmem=None):
    return pltpu.CompilerParams(dimension_semantics=sem, vmem_limit_bytes=vmem)


def mm_nn(a, b, out_dtype=F32, name="mm_nn", b_blk=0):
    M, K = a.shape
    N = b.shape[1]
    tm = math.gcd(ROW_TILE, M)

    def body(a_ref, b_ref, o_ref):
        o_ref[...] = jnp.dot(a_ref[...].astype(BF16), b_ref[...].astype(BF16),
                             preferred_element_type=F32).astype(o_ref.dtype)

    return pl.pallas_call(
        body, out_shape=jax.ShapeDtypeStruct((M, N), out_dtype), grid=(M // tm,),
        in_specs=[pl.BlockSpec((tm, K), lambda i: (i, 0)), pl.BlockSpec((K, N), lambda i: (b_blk, 0))],
        out_specs=pl.BlockSpec((tm, N), lambda i: (i, 0)),
        compiler_params=_params(("parallel",), VMEM_LIMIT), name=name)(a, b)


def mm_nt(a, b, out_dtype=F32, name="mm_nt", b_rows=None, b_blk=0):
    M, N = a.shape
    K = b.shape[0] if b_rows is None else b_rows
    tm = math.gcd(ROW_TILE, M)

    def body(a_ref, b_ref, o_ref):
        o_ref[...] = lax.dot_general(a_ref[...].astype(BF16), b_ref[...].astype(BF16),
                                     (((1,), (1,)), ((), ())),
                                     preferred_element_type=F32).astype(o_ref.dtype)

    return pl.pallas_call(
        body, out_shape=jax.ShapeDtypeStruct((M, K), out_dtype), grid=(M // tm,),
        in_specs=[pl.BlockSpec((tm, N), lambda i: (i, 0)), pl.BlockSpec((K, N), lambda i: (b_blk, 0))],
        out_specs=pl.BlockSpec((tm, K), lambda i: (i, 0)),
        compiler_params=_params(("parallel",), VMEM_LIMIT), name=name)(a, b)


def mm_tn(a, b, name="mm_tn", b_more=None):
    M, N = b.shape
    K = a.shape[1]
    tn = math.gcd(512, N) if N % 128 == 0 and N > 512 else N
    bs = [b] if b_more is None else [b, b_more]
    assert all(x.shape[1] == N for x in bs)
    nb = N // tn

    def body(a_ref, *refs):
        o_ref = refs[-1]
        for k, b_ref in enumerate(refs[:-1]):
            def product(b_ref=b_ref):
                o_ref[...] = lax.dot_general(a_ref[a.shape[0] - b_ref.shape[0]:, :].astype(BF16),
                                             b_ref[...].astype(BF16), (((0,), (0,)), ((), ())),
                                             preferred_element_type=F32)
            if len(bs) == 1:
                product()
            else:
                pl.when(pl.program_id(0) // nb == k)(product)

    def b_spec(k, rows):
        return pl.BlockSpec((rows, tn), lambda j: (0, jnp.clip(j - k * nb, 0, nb - 1)))

    return pl.pallas_call(
        body, out_shape=jax.ShapeDtypeStruct((K, N * len(bs)), F32), grid=(nb * len(bs),),
        in_specs=[pl.BlockSpec(a.shape, lambda j: (0, 0))] + [b_spec(k, x.shape[0]) for k, x in enumerate(bs)],
        out_specs=pl.BlockSpec((K, tn), lambda j: (0, j)),
        compiler_params=_params(("parallel",), VMEM_LIMIT), name=name)(a, *bs)


class Rows:
    def __init__(self, arr, width=None, row_off=0, col_blk=0):
        self.arr = arr
        self.width = arr.shape[1] if width is None else width
        self.row_off = row_off
        self.col_blk = col_blk

    def spec(self, tm):
        ro, cb = self.row_off // tm, self.col_blk
        return pl.BlockSpec((tm, self.width), lambda i: (i + ro, cb))


def _as_rows(x):
    return x if isinstance(x, Rows) else Rows(x)


def _row_tile(n_rows, n_ctx_rows, rows):
    tm = math.gcd(ROW_TILE, n_rows, n_ctx_rows)
    for r in rows:
        tm = math.gcd(tm, r.row_off)
    return tm


def _bc_spec(arr, n_ctx_blocks):
    g, _, d = arr.shape
    if g == 1:
        return pl.BlockSpec((1, 1, d), lambda i: (0, 0, 0))
    return pl.BlockSpec((1, 1, d), lambda i: ((i >= n_ctx_blocks).astype(jnp.int32), 0, 0))


def rowwise_fwd(fn, rows, bcs, out_dims, out_dtypes, n_rows, n_ctx_rows, name):
    rows = [_as_rows(r) for r in rows]
    tm = _row_tile(n_rows, n_ctx_rows, rows)
    ncb = n_ctx_rows // tm
    nr, nb = len(rows), len(bcs)

    def body(*refs):
        vals = [r[...].astype(F32) for r in refs[:nr]] + [b[0].astype(F32) for b in refs[nr:nr + nb]]
        outs = fn(*vals)
        for o_ref, v in zip(refs[nr + nb:], outs):
            o_ref[...] = v.astype(o_ref.dtype)

    outs = pl.pallas_call(
        body,
        out_shape=[jax.ShapeDtypeStruct((n_rows, d), dt) for d, dt in zip(out_dims, out_dtypes)],
        grid=(n_rows // tm,),
        in_specs=[r.spec(tm) for r in rows] + [_bc_spec(b, ncb) for b in bcs],
        out_specs=[pl.BlockSpec((tm, d), lambda i: (i, 0)) for d in out_dims],
        compiler_params=_params(("parallel",), VMEM_LIMIT), name=name)(*[r.arr for r in rows], *bcs)
    return outs


def rowwise_bwd(fn, rows, bcs, cts, diff_rows, diff_bcs, n_rows, n_ctx_rows, name, ct_extra=None, lat_add=None):
    rows = [_as_rows(r) for r in rows]
    cts = [_as_rows(c) for c in cts]
    extra = [_as_rows(ct_extra)] if ct_extra is not None else []
    tm = _row_tile(n_rows, n_ctx_rows, rows + cts + extra)
    ncb = n_ctx_rows // tm
    nr, nb, nc = len(rows), len(bcs), len(cts)
    ndr, ndb = len(diff_rows), len(diff_bcs)
    n_in = nr + nb + nc + len(extra) + (lat_add is not None)

    def body(*refs):
        i = pl.program_id(0)
        rvals = [r[...].astype(F32) for r in refs[:nr]]
        bvals = [b[0].astype(F32) for b in refs[nr:nr + nb]]
        cvals = [c[...].astype(F32) for c in refs[nr + nb:nr + nb + nc]]
        if extra:
            cvals[0] = cvals[0] + refs[nr + nb + nc][...].astype(F32)
        outs = refs[n_in:]

        def f(*d):
            rv, bv = list(rvals), list(bvals)
            for k, idx in enumerate(diff_rows):
                rv[idx] = d[k]
            for k, idx in enumerate(diff_bcs):
                bv[idx] = d[ndr + k]
            return tuple(fn(*rv, *bv))

        primals = [rvals[k] for k in diff_rows] + [bvals[k] for k in diff_bcs]
        _, vjp = jax.vjp(f, *primals)
        grads = list(vjp(tuple(cvals)))
        if lat_add is not None:
            add = refs[n_in - 1][...]
            grads[0] = grads[0] + (add if lat_add.shape[0] == n_rows else jnp.where(i >= ncb, add, 0.0))
        for k in range(ndr):
            outs[k][...] = grads[k].astype(outs[k].dtype)
        for k, idx in enumerate(diff_bcs):
            o_ref = outs[ndr + k]
            first = (i == 0)
            if bcs[idx].shape[0] == 2:
                first = first | (i == ncb)

            @pl.when(first)
            def _(o_ref=o_ref):
                o_ref[...] = jnp.zeros_like(o_ref)

            o_ref[0] += grads[ndr + k]

    out_shape = [jax.ShapeDtypeStruct((n_rows, rows[k].width), F32) for k in diff_rows]
    out_shape += [jax.ShapeDtypeStruct(bcs[k].shape, F32) for k in diff_bcs]
    out_specs = [pl.BlockSpec((tm, rows[k].width), lambda i: (i, 0)) for k in diff_rows]
    out_specs += [_bc_spec(bcs[k], ncb) for k in diff_bcs]
    ins = [r.arr for r in rows] + list(bcs) + [c.arr for c in cts + extra]
    in_specs = [r.spec(tm) for r in rows] + [_bc_spec(b, ncb) for b in bcs] + [c.spec(tm) for c in cts + extra]
    if lat_add is not None:
        ins.append(lat_add)
        skip = ncb if lat_add.shape[0] != n_rows else 0
        in_specs.append(pl.BlockSpec((tm, lat_add.shape[1]), lambda i: (jnp.maximum(i - skip, 0), 0)))
    outs = pl.pallas_call(
        body, out_shape=out_shape, grid=(n_rows // tm,), in_specs=in_specs, out_specs=out_specs,
        compiler_params=_params(("arbitrary",), VMEM_LIMIT), name=name)(*ins)
    return outs


def _rms(x):
    return x * lax.rsqrt(jnp.mean(x * x, axis=-1, keepdims=True) + EPS)


def _sigmoid(x):
    return 0.5 * (jnp.tanh(0.5 * x) + 1.0)


def _silu(x):
    return x * _sigmoid(x)


def _gelu_tanh(x):
    return 0.5 * x * (1.0 + jnp.tanh(math.sqrt(2.0 / math.pi) * (x + 0.044715 * (x * x * x))))


def f_norm_mod(x, g, sc, sh):
    return ((_rms(x) * g) * (1.0 + sc) + sh,)


def f_rms(x, g):
    return (_rms(x) * g,)


def f_gate(o, z):
    return (o * _silu(z),)


def f_s5_act(y, u, d):
    return (_gelu_tanh(y + d * u),)


def f_s5_glu(ya, gl, z, b):
    return (ya * _sigmoid(gl + b) * _silu(z),)


def _as_parts(x, n_ctx):
    xs = x if isinstance(x, tuple) else (x,)
    assert len(xs) == 1 or xs[0].shape[0] == n_ctx
    return xs, sum(p.shape[0] for p in xs)


def _parts_specs(xs, tm, ncb):
    d = xs[0].shape[1]
    if len(xs) == 1:
        return [pl.BlockSpec((tm, d), lambda i: (i, 0))]
    return [pl.BlockSpec((tm, d), lambda i: (jnp.minimum(i, ncb - 1), 0)),
            pl.BlockSpec((tm, d), lambda i: (jnp.maximum(i - ncb, 0), 0))]


def _parts_tile(x_refs, ncb):
    if len(x_refs) == 1:
        return x_refs[0][...]
    return jnp.where(pl.program_id(0) < ncb, x_refs[0][...], x_refs[1][...])


def norm_proj(x, g, sc, sh, w, n_ctx, name):
    xs, n = _as_parts(x, n_ctx)
    d = xs[0].shape[1]
    nw = w.shape[1]
    tm = math.gcd(ROW_TILE, n, n_ctx)
    ncb = n_ctx // tm
    nx = len(xs)

    def body(*refs):
        g_ref, sc_ref, sh_ref, w_ref, h_ref, p_ref = refs[nx:]
        h = f_norm_mod(_parts_tile(refs[:nx], ncb), g_ref[0], sc_ref[0], sh_ref[0])[0].astype(BF16)
        h_ref[...] = h
        p_ref[...] = jnp.dot(h, w_ref[...], preferred_element_type=F32)

    row = pl.BlockSpec((tm, d), lambda i: (i, 0))
    return pl.pallas_call(
        body, out_shape=[jax.ShapeDtypeStruct((n, d), BF16), jax.ShapeDtypeStruct((n, nw), F32)], grid=(n // tm,),
        in_specs=_parts_specs(xs, tm, ncb) + [_bc_spec(g, ncb), _bc_spec(sc, ncb), _bc_spec(sh, ncb),
                                              pl.BlockSpec(w.shape, lambda i: (0, 0))],
        out_specs=[row, pl.BlockSpec((tm, nw), lambda i: (i, 0))],
        compiler_params=_params(("parallel",), VMEM_LIMIT), name=name)(*xs, g, sc, sh, w)


def norm_proj_bwd(terms, w, x, g, sc, sh, add, n_ctx, name, latent_dx_only=False, dx_from_segments=False):
    xs, n = _as_parts(x, n_ctx)
    adds = add if isinstance(add, tuple) else (add,)
    d = xs[0].shape[1]
    tm = math.gcd(ROW_TILE, n, n_ctx, *[t[2] for t in terms])
    ncb = n_ctx // tm
    nt, nx, na = len(terms), len(xs), len(adds)
    add_skip = ncb if na == 1 and add.shape[0] != n else 0
    n_dx = 2 if dx_from_segments else 1
    tj = tm // N_SEG
    assert not dx_from_segments or (ncb == 1 and not latent_dx_only)

    def body(*refs):
        i = pl.program_id(0)
        a_refs = refs[:nt]
        w_ref, x_refs = refs[nt], refs[nt + 1:nt + 1 + nx]
        g_ref, sc_ref, sh_ref = refs[nt + 1 + nx:nt + 4 + nx]
        add_refs = refs[nt + 4 + nx:nt + 4 + nx + na]
        outs = refs[nt + 4 + nx + na:]
        dx_refs, (dg_ref, dsc_ref, dsh_ref) = outs[:n_dx], outs[n_dx:n_dx + 3]
        d_h = None
        for a_ref, (a, off, first) in zip(a_refs, terms):
            part = lax.dot_general(a_ref[...].astype(BF16), w_ref[:, off:off + a.shape[1]], NT_DIMS,
                                   preferred_element_type=F32)
            if first:
                part = jnp.where(i >= first // tm, part, 0.0)
            d_h = part if d_h is None else d_h + part
        _, vjp = jax.vjp(lambda x_, g_, sc_, sh_: f_norm_mod(x_, g_, sc_, sh_), _parts_tile(x_refs, ncb), g_ref[0],
                         sc_ref[0], sh_ref[0])
        d_x, d_g, d_sc, d_sh = vjp((d_h,))
        extra = _parts_tile(add_refs, ncb)
        d_x = d_x + (extra if add_skip == 0 else jnp.where(i >= ncb, extra, 0.0))
        if dx_from_segments:
            slabs = outs[n_dx + 3]
            for c in range(d // HEAD_LANES):
                slabs[c] = d_x[:, c * HEAD_LANES:(c + 1) * HEAD_LANES]
            for k, here in enumerate([i < ncb, i >= ncb]):
                @pl.when(here)
                def _(k=k):
                    for c in range(d // HEAD_LANES):
                        for seg in range(N_SEG):
                            dx_refs[k][seg, :, c * HEAD_LANES:(c + 1) * HEAD_LANES] = (
                                slabs[c, pl.ds(seg, tj, stride=N_SEG), :])
        else:
            dx_refs[0][...] = d_x

        @pl.when(i == 0)
        def _():
            dg_ref[...] = jnp.zeros_like(dg_ref)

        @pl.when((i == 0) | (i == ncb))
        def _():
            dsc_ref[...] = jnp.zeros_like(dsc_ref)
            dsh_ref[...] = jnp.zeros_like(dsh_ref)

        dg_ref[0] += d_g
        dsc_ref[0] += d_sc
        dsh_ref[0] += d_sh

    def a_spec(a, first):
        skip = first // tm
        return pl.BlockSpec((tm, a.shape[1]), lambda i: (jnp.maximum(i - skip, 0), 0))

    dx_skip = ncb if latent_dx_only else 0
    if dx_from_segments:
        dx_shapes = [jax.ShapeDtypeStruct((N_SEG, n_ctx // N_SEG, d), F32),
                     jax.ShapeDtypeStruct((N_SEG, (n - n_ctx) // N_SEG, d), F32)]
        dx_specs = [pl.BlockSpec((N_SEG, tj, d), lambda i: (0, 0, 0)),
                    pl.BlockSpec((N_SEG, tj, d), lambda i: (0, jnp.maximum(i - ncb, 0), 0))]
    else:
        dx_shapes = [jax.ShapeDtypeStruct((n - dx_skip * tm, d), F32)]
        dx_specs = [pl.BlockSpec((tm, d), lambda i: (jnp.maximum(i - dx_skip, 0), 0))]
    add_specs = (_parts_specs(adds, tm, ncb) if na == 2 else
                 [pl.BlockSpec((tm, d), lambda i: (jnp.maximum(i - add_skip, 0), 0))])
    res = pl.pallas_call(
        body,
        out_shape=dx_shapes + [jax.ShapeDtypeStruct(g.shape, F32), jax.ShapeDtypeStruct(sc.shape, F32),
                               jax.ShapeDtypeStruct(sh.shape, F32)],
        grid=(n // tm,),
        in_specs=[a_spec(a, first) for a, _, first in terms]
        + [pl.BlockSpec(w.shape, lambda i: (0, 0))] + _parts_specs(xs, tm, ncb)
        + [_bc_spec(g, ncb), _bc_spec(sc, ncb), _bc_spec(sh, ncb)] + add_specs,
        out_specs=dx_specs + [_bc_spec(g, ncb), _bc_spec(sc, ncb), _bc_spec(sh, ncb)],
        scratch_shapes=[pltpu.VMEM((d // HEAD_LANES, tm, HEAD_LANES), F32)] if dx_from_segments else [],
        compiler_params=_params(("arbitrary",), VMEM_LIMIT), name=name)(
            *[t[0] for t in terms], w, *xs, g, sc, sh, *adds)
    if dx_from_segments:
        return ((res[0].reshape(n_ctx, d), res[1].reshape(n - n_ctx, d)), *res[2:])
    return res


def mla_post_fwd(o, p0, x0, gate, w_out, n_ctx, name="l0_post"):
    n, d = o.shape
    xs, _ = _as_parts(x0, n_ctx)
    tm = math.gcd(ROW_TILE, n, n_ctx)
    ncb = n_ctx // tm
    nx = len(xs)

    def body(o_ref, z_ref, *refs):
        gt_ref, w_ref, x1_ref, og_ref, out_ref = refs[nx:]
        og = f_gate(o_ref[...], z_ref[...])[0].astype(BF16)
        out = jnp.dot(og, w_ref[...], preferred_element_type=F32)
        og_ref[...] = og
        out_ref[...] = out
        x1_ref[...] = _parts_tile(refs[:nx], ncb) + gt_ref[0] * out

    row = pl.BlockSpec((tm, d), lambda i: (i, 0))
    return pl.pallas_call(
        body, out_shape=[jax.ShapeDtypeStruct((n, d), F32), jax.ShapeDtypeStruct((n, d), BF16),
                         jax.ShapeDtypeStruct((n, d), F32)],
        grid=(n // tm,),
        in_specs=[row, row] + _parts_specs(xs, tm, ncb) + [_bc_spec(gate, ncb), pl.BlockSpec((d, d), lambda i: (0, 0))],
        out_specs=[row, row, row],
        compiler_params=_params(("parallel",), VMEM_LIMIT), name=name)(o, p0, *xs, gate, w_out)


def mla_post_bwd(dx1, out, og, o, p0, gate, w_out, n_ctx, name="l0_post_bwd"):
    n, d = o.shape
    dxs, _ = _as_parts(dx1, n_ctx)
    tm = math.gcd(ROW_TILE, n, n_ctx)
    ncb = n_ctx // tm
    nx = len(dxs)

    def body(*refs):
        out_ref, og_ref, o_ref, z_ref, gt_ref, w_ref, do_ref, dz_ref, dgt_ref, dw_ref = refs[nx:]
        i = pl.program_id(0)

        @pl.when(i == 0)
        def _():
            dw_ref[...] = jnp.zeros_like(dw_ref)

        @pl.when((i == 0) | (i == ncb))
        def _():
            dgt_ref[...] = jnp.zeros_like(dgt_ref)

        dx = _parts_tile(refs[:nx], ncb)
        dgt_ref[0] += jnp.sum(dx * out_ref[...], axis=0, keepdims=True)
        d_out16 = (gt_ref[0] * dx).astype(BF16)
        dw_ref[...] += lax.dot_general(og_ref[...], d_out16, (((0,), (0,)), ((), ())), preferred_element_type=F32)
        d_og = lax.dot_general(d_out16, w_ref[...], NT_DIMS, preferred_element_type=F32)
        _, gate_vjp = jax.vjp(lambda o_, z_: f_gate(o_, z_), o_ref[...], z_ref[...])
        d_o, d_z = gate_vjp((d_og,))
        do_ref[...] = d_o
        dz_ref[...] = d_z

    row = pl.BlockSpec((tm, d), lambda i: (i, 0))
    mat = pl.BlockSpec((d, d), lambda i: (0, 0))
    return pl.pallas_call(
        body, out_shape=[jax.ShapeDtypeStruct((n, d), F32), jax.ShapeDtypeStruct((n, d), F32),
                         jax.ShapeDtypeStruct(gate.shape, F32), jax.ShapeDtypeStruct((d, d), F32)],
        grid=(n // tm,),
        in_specs=_parts_specs(dxs, tm, ncb) + [row, row, row, row, _bc_spec(gate, ncb), mat],
        out_specs=[row, row, _bc_spec(gate, ncb), mat],
        compiler_params=_params(("arbitrary",), VMEM_LIMIT), name=name)(*dxs, out, og, o, p0, gate, w_out)


def s5_tail(y_ssm, p1, x1p, target, n_ctx, d_vec, b_glu, gate, final_g, w_glu, w_out, name="l1_tail"):
    n, d = y_ssm.shape
    tm = math.gcd(ROW_TILE, n, n_ctx)
    off = n_ctx // tm
    tn_dims = (((0,), (0,)), ((), ()))

    def row_loss(x, g, t):
        e = _rms(x) * g - t
        return 0.5 * (e * e) * (1.0 / d)

    def body(y_ref, u_ref, z_ref, x1_ref, t_ref, d_ref, b_ref, gt_ref, fg_ref, wg_ref, wo_ref,
             l_ref, dx_ref, dy_ref, du_ref, dz_ref, dfg_ref, dgt_ref, db_ref, dd_ref, dwg_ref, dwo_ref):
        @pl.when(pl.program_id(0) == 0)
        def _():
            for r in (l_ref, dfg_ref, dgt_ref, db_ref, dd_ref, dwg_ref, dwo_ref):
                r[...] = jnp.zeros_like(r)

        u, z, tgt, gt = u_ref[...], z_ref[...], t_ref[...], gt_ref[...]
        (ya,), act_vjp = jax.vjp(lambda y_, u_, d_: f_s5_act(y_, u_, d_), y_ref[...], u, d_ref[...])
        ya16 = ya.astype(BF16)
        gl = jnp.dot(ya16, wg_ref[...], preferred_element_type=F32)
        (y3,), glu_vjp = jax.vjp(lambda a_, g_, z_, b_: f_s5_glu(a_, g_, z_, b_), ya, gl, z, b_ref[...])
        y3_16 = y3.astype(BF16)
        out1 = jnp.dot(y3_16, wo_ref[...], preferred_element_type=F32)
        lterm, loss_vjp = jax.vjp(lambda x_, g_: row_loss(x_, g_, tgt), x1_ref[...] + gt * out1, fg_ref[...])
        dx2, dfg = loss_vjp(jnp.ones_like(lterm))
        l_ref[...] += jnp.sum(lterm, axis=0, keepdims=True)
        dfg_ref[...] += dfg
        dx_ref[...] = dx2
        dgt_ref[...] += jnp.sum(dx2 * out1, axis=0, keepdims=True)
        d_out16 = (gt * dx2).astype(BF16)
        dwo_ref[...] += lax.dot_general(y3_16, d_out16, tn_dims, preferred_element_type=F32)
        d_y3 = lax.dot_general(d_out16, wo_ref[...], NT_DIMS, preferred_element_type=F32)
        d_ya, d_gl, d_z, d_b = glu_vjp((d_y3,))
        dz_ref[...] = d_z
        db_ref[...] += d_b
        d_gl16 = d_gl.astype(BF16)
        dwg_ref[...] += lax.dot_general(ya16, d_gl16, tn_dims, preferred_element_type=F32)
        d_ya = d_ya + lax.dot_general(d_gl16, wg_ref[...], NT_DIMS, preferred_element_type=F32)
        d_y, d_u, d_d = act_vjp((d_ya,))
        dy_ref[...] = d_y
        du_ref[...] = d_u
        dd_ref[...] += d_d

    row = pl.BlockSpec((tm, d), lambda i: (i, 0))
    vecs = pl.BlockSpec((1, d), lambda i: (0, 0))
    mat = pl.BlockSpec((d, d), lambda i: (0, 0))
    return pl.pallas_call(
        body,
        out_shape=[jax.ShapeDtypeStruct((1, d), F32)] + [jax.ShapeDtypeStruct((n, d), F32)] * 4
        + [jax.ShapeDtypeStruct((1, d), F32)] * 4 + [jax.ShapeDtypeStruct((d, d), F32)] * 2,
        grid=(n // tm,),
        in_specs=[row, pl.BlockSpec((tm, d), lambda i: (i + off, 0)), pl.BlockSpec((tm, d), lambda i: (i + off, 1)),
                  pl.BlockSpec((tm, d), lambda i: (i + off, 0)), row, vecs, vecs, vecs, vecs, mat, mat],
        out_specs=[vecs, row, row, row, row, vecs, vecs, vecs, vecs, mat, mat],
        compiler_params=_params(("arbitrary",), VMEM_LIMIT), name=name)(
            y_ssm, p1, p1, x1p, target, d_vec, b_glu, gate, final_g, w_glu, w_out)


NT_DIMS = (((1,), (1,)), ((), ()))
HEAD_LANES = 128
N_PAIRS = MLA_HEADS // 2


def _own_lanes(shape, hh):
    lane = lax.broadcasted_iota(jnp.int32, shape, len(shape) - 1)
    return (lane < V_HEAD_DIM) if hh == 0 else (lane >= V_HEAD_DIM)


def _delta_lane(hh):
    return V_HEAD_DIM if hh == 0 else 0


def _rope_tiles(x, cos, sin_next, sin_prev, inverse):
    width = x.shape[-1]
    reps = width // HEAD_LANES
    c, sn, sp = (jnp.tile(t, (1, reps)) for t in (cos, sin_next, sin_prev))
    if inverse:
        return x * c + pltpu.roll(x * sn, 8, 1) + pltpu.roll(x * sp, width - 8, 1)
    return x * c + pltpu.roll(x, width - 8, 1) * sn + pltpu.roll(x, 8, 1) * sp


STAT_LANE = QK_DIM


def _with_stat(x16, col, lane0):
    hi = col.astype(BF16)
    r1 = col - hi.astype(F32)
    mid = r1.astype(BF16)
    lo = (r1 - mid.astype(F32)).astype(BF16)
    lane = lax.broadcasted_iota(jnp.int32, x16.shape, 1)
    return jnp.where(lane == lane0, hi, jnp.where(lane == lane0 + 1, mid, jnp.where(lane == lane0 + 2, lo, x16)))


def attn_fwd(qb, kb, vb, n_ctx):
    T = qb.shape[0]
    tq = math.gcd(ROW_TILE, n_ctx)
    nq, ncb = T // tq, n_ctx // tq

    def body(q_ref, k_ref, v_ref, o_ref, lse_ref, qs_ref):
        qi = pl.program_id(1)

        def rows(n_keys):
            v = v_ref[:n_keys, :]
            outs = []
            for hh in range(2):
                hs = slice(hh * HEAD_LANES, (hh + 1) * HEAD_LANES)
                s = lax.dot_general(q_ref[:, hs], k_ref[:n_keys, hs], NT_DIMS,
                                    preferred_element_type=F32) * SOFTMAX_SCALE
                m = jnp.max(s, axis=-1, keepdims=True)
                p = jnp.exp(s - m)
                l = jnp.sum(p, axis=-1, keepdims=True)
                outs.append(jnp.dot(p.astype(BF16), v, preferred_element_type=F32) / l)
                lse = m + jnp.log(l)
                lse_ref[hh] = lse
                qs_ref[:, hs] = _with_stat(q_ref[:, hs], lse * (-1.0 / SOFTMAX_SCALE), STAT_LANE)
            o_ref[...] = jnp.where(_own_lanes(outs[0].shape, 0), outs[0], outs[1])

        pl.when(qi < ncb)(lambda: rows(n_ctx))
        pl.when(qi >= ncb)(lambda: rows(T))

    return pl.pallas_call(
        body,
        out_shape=[jax.ShapeDtypeStruct((T, MLA_HEADS * V_HEAD_DIM), F32),
                   jax.ShapeDtypeStruct((MLA_HEADS, T, 1), F32), jax.ShapeDtypeStruct(qb.shape, BF16)],
        grid=(N_PAIRS, nq),
        in_specs=[pl.BlockSpec((tq, 2 * HEAD_LANES), lambda h, i: (i, h)),
                  pl.BlockSpec((T, 2 * HEAD_LANES), lambda h, i: (0, h)),
                  pl.BlockSpec((T, 2 * V_HEAD_DIM), lambda h, i: (0, h))],
        out_specs=[pl.BlockSpec((tq, 2 * V_HEAD_DIM), lambda h, i: (i, h)),
                   pl.BlockSpec((2, tq, 1), lambda h, i: (h, i, 0)),
                   pl.BlockSpec((tq, 2 * HEAD_LANES), lambda h, i: (i, h))],
        compiler_params=_params(("parallel", "parallel"), VMEM_LIMIT), name="attn_fwd")(qb, kb, vb)


def attn_bwd_dq(qb, kb, vb, o, do, lse, tabs, n_ctx):
    T = qb.shape[0]
    tq = math.gcd(ROW_TILE, n_ctx)
    nq, ncb = T // tq, n_ctx // tq

    def body(q_ref, k_ref, v_ref, o_ref, do_ref, lse_ref, c_ref, sn_ref, sp_ref, dq_ref, dos_ref):
        qi = pl.program_id(1)

        def rows(n_keys):
            v = v_ref[:n_keys, :]
            dqs = []
            for hh in range(2):
                hs = slice(hh * HEAD_LANES, (hh + 1) * HEAD_LANES)
                k = k_ref[:n_keys, hs]
                do = jnp.where(_own_lanes(do_ref.shape, hh), do_ref[...], 0.0)
                delta = jnp.sum(do * o_ref[...], axis=-1, keepdims=True)
                s = lax.dot_general(q_ref[:, hs], k, NT_DIMS, preferred_element_type=F32) * SOFTMAX_SCALE
                p = jnp.exp(s - lse_ref[hh])
                do16 = do.astype(BF16)
                dp = lax.dot_general(do16, v, NT_DIMS, preferred_element_type=F32)
                ds = p * (dp - delta) * SOFTMAX_SCALE
                dqs.append(jnp.dot(ds.astype(BF16), k, preferred_element_type=F32))
                dos_ref[:, hs] = _with_stat(do16, delta, _delta_lane(hh))
            dq = jnp.concatenate(dqs, axis=1)
            dq_ref[...] = _rope_tiles(dq, c_ref[...], sn_ref[...], sp_ref[...], True).astype(BF16)

        pl.when(qi < ncb)(lambda: rows(n_ctx))
        pl.when(qi >= ncb)(lambda: rows(T))

    tab = pl.BlockSpec((tq, HEAD_LANES), lambda h, i: (i, 0))
    return pl.pallas_call(
        body,
        out_shape=[jax.ShapeDtypeStruct((T, MLA_HEADS * HEAD_LANES), BF16)] * 2,
        grid=(N_PAIRS, nq),
        in_specs=[pl.BlockSpec((tq, 2 * HEAD_LANES), lambda h, i: (i, h)),
                  pl.BlockSpec((T, 2 * HEAD_LANES), lambda h, i: (0, h)),
                  pl.BlockSpec((T, 2 * V_HEAD_DIM), lambda h, i: (0, h)),
                  pl.BlockSpec((tq, 2 * V_HEAD_DIM), lambda h, i: (i, h)),
                  pl.BlockSpec((tq, 2 * V_HEAD_DIM), lambda h, i: (i, h)),
                  pl.BlockSpec((2, tq, 1), lambda h, i: (h, i, 0)), tab, tab, tab],
        out_specs=[pl.BlockSpec((tq, 2 * HEAD_LANES), lambda h, i: (i, h))] * 2,
        compiler_params=_params(("parallel", "parallel"), VMEM_LIMIT), name="attn_bwd_dq")(
            qb, kb, vb, o, do, lse, *tabs)


def attn_bwd_dkv(qs, kb, vb, dos, n_ctx):
    T = qs.shape[0]
    tq = math.gcd(ROW_TILE, n_ctx)
    nq, ncb = T // tq, n_ctx // tq

    def body(q_ref, do_ref, k_ref, v_ref, dk_ref, dv_ref):
        kj = pl.program_id(1)

        def cols(first):
            v = v_ref[...]
            lane = lax.broadcasted_iota(jnp.int32, v.shape, 1)
            dvs = []
            for hh in range(2):
                hs = slice(hh * HEAD_LANES, (hh + 1) * HEAD_LANES)
                q = q_ref[first:, hs]
                do16 = do_ref[first:, hs]
                in_delta = (lane >= _delta_lane(hh)) & (lane < _delta_lane(hh) + 3)
                v_minus = jnp.where(in_delta, -jnp.ones_like(v), v)
                pt = jnp.exp(lax.dot_general(k_ref[:, hs], q, NT_DIMS, preferred_element_type=F32) * SOFTMAX_SCALE)
                dvs.append(jnp.dot(pt.astype(BF16), do16, preferred_element_type=F32))
                dst = pt * lax.dot_general(v_minus, do16, NT_DIMS, preferred_element_type=F32) * SOFTMAX_SCALE
                dk_ref[:, hs] = jnp.dot(dst.astype(BF16), q, preferred_element_type=F32)
            dv_ref[...] = jnp.where(_own_lanes(dvs[0].shape, 0), dvs[0], dvs[1])

        pl.when(kj < ncb)(lambda: cols(0))
        pl.when(kj >= ncb)(lambda: cols(n_ctx))

    return pl.pallas_call(
        body,
        out_shape=[jax.ShapeDtypeStruct((T, MLA_HEADS * HEAD_LANES), F32),
                   jax.ShapeDtypeStruct((T, MLA_HEADS * V_HEAD_DIM), F32)],
        grid=(N_PAIRS, nq),
        in_specs=[pl.BlockSpec((T, 2 * HEAD_LANES), lambda h, j: (0, h)),
                  pl.BlockSpec((T, 2 * HEAD_LANES), lambda h, j: (0, h)),
                  pl.BlockSpec((tq, 2 * HEAD_LANES), lambda h, j: (j, h)),
                  pl.BlockSpec((tq, 2 * V_HEAD_DIM), lambda h, j: (j, h))],
        out_specs=[pl.BlockSpec((tq, 2 * HEAD_LANES), lambda h, j: (j, h)),
                   pl.BlockSpec((tq, 2 * V_HEAD_DIM), lambda h, j: (j, h))],
        compiler_params=_params(("parallel", "parallel"), VMEM_LIMIT), name="attn_bwd_dkv")(
            qs, dos, kb, vb)


def _split_bf16(x):
    hi = x.astype(BF16)
    return hi, (x - hi.astype(F32)).astype(BF16)


def q_heads(cq, gain, w_uq_p, tabs, name="l0_uq"):
    T, K = cq.arr.shape[0], cq.width
    N = w_uq_p.shape[1]
    tm = math.gcd(ROW_TILE, T)

    def body(a_ref, g_ref, w_ref, c_ref, sn_ref, sp_ref, o_ref, n_ref):
        qn = f_rms(a_ref[...], g_ref[0])[0].astype(BF16)
        n_ref[...] = qn
        acc = jnp.dot(qn, w_ref[...], preferred_element_type=F32)
        o_ref[...] = _rope_tiles(acc, c_ref[...], sn_ref[...], sp_ref[...], False).astype(BF16)

    tab = pl.BlockSpec((tm, HEAD_LANES), lambda i: (i, 0))
    return pl.pallas_call(
        body, out_shape=[jax.ShapeDtypeStruct((T, N), BF16), jax.ShapeDtypeStruct((T, K), BF16)], grid=(T // tm,),
        in_specs=[cq.spec(tm), pl.BlockSpec((1, 1, K), lambda i: (0, 0, 0)), pl.BlockSpec((K, N), lambda i: (0, 0)),
                  tab, tab, tab],
        out_specs=[pl.BlockSpec((tm, N), lambda i: (i, 0)), pl.BlockSpec((tm, K), lambda i: (i, 0))],
        compiler_params=_params(("parallel",), VMEM_LIMIT), name=name)(cq.arr, gain, w_uq_p, *tabs)


def kv_heads(ckv, gain, w_kn_p, w_v, kr, spread, tabs, name="l0_ukv"):
    T, K = ckv.arr.shape[0], ckv.width
    N = w_kn_p.shape[1]
    NV = w_v.shape[1]
    tm = math.gcd(ROW_TILE, T)

    def body(a_ref, g_ref, wk_ref, wv_ref, kr_ref, e_ref, c_ref, sn_ref, sp_ref, k_ref, v_ref, n_ref):
        a = f_rms(a_ref[...], g_ref[0])[0].astype(BF16)
        n_ref[...] = a
        hi, lo = _split_bf16(kr_ref[...])
        acc = (jnp.dot(a, wk_ref[...], preferred_element_type=F32)
               + jnp.dot(hi, e_ref[...], preferred_element_type=F32)
               + jnp.dot(lo, e_ref[...], preferred_element_type=F32))
        roped = _rope_tiles(acc, c_ref[...], sn_ref[...], sp_ref[...], False)
        lane = lax.broadcasted_iota(jnp.int32, roped.shape, 1) % HEAD_LANES
        k_ref[...] = jnp.where((lane >= STAT_LANE) & (lane < STAT_LANE + 3), 1.0, roped).astype(BF16)
        v_ref[...] = jnp.dot(a, wv_ref[...], preferred_element_type=F32).astype(BF16)

    tab = pl.BlockSpec((tm, HEAD_LANES), lambda i: (i, 0))
    return pl.pallas_call(
        body, out_shape=[jax.ShapeDtypeStruct((T, N), BF16), jax.ShapeDtypeStruct((T, NV), BF16),
                         jax.ShapeDtypeStruct((T, K), BF16)], grid=(T // tm,),
        in_specs=[ckv.spec(tm), pl.BlockSpec((1, 1, K), lambda i: (0, 0, 0)), pl.BlockSpec((K, N), lambda i: (0, 0)),
                  pl.BlockSpec((K, NV), lambda i: (0, 0)), pl.BlockSpec((tm, QK_ROPE_DIM), lambda i: (i, 0)),
                  pl.BlockSpec((QK_ROPE_DIM, N), lambda i: (0, 0)), tab, tab, tab],
        out_specs=[pl.BlockSpec((tm, N), lambda i: (i, 0)), pl.BlockSpec((tm, NV), lambda i: (i, 0)),
                   pl.BlockSpec((tm, K), lambda i: (i, 0))],
        compiler_params=_params(("parallel",), VMEM_LIMIT), name=name)(ckv.arr, gain, w_kn_p, w_v, kr, spread, *tabs)


def heads_unrope(d, tabs, spread=None, name="unrope"):
    T, N = d.shape
    tm = math.gcd(ROW_TILE, T)

    def body(*refs):
        if spread is None:
            d_ref, c_ref, sn_ref, sp_ref, o_ref = refs
        else:
            d_ref, c_ref, sn_ref, sp_ref, e_ref, o_ref, kr_ref = refs
        g = _rope_tiles(d_ref[...], c_ref[...], sn_ref[...], sp_ref[...], True)
        o_ref[...] = g.astype(BF16)
        if spread is not None:
            hi, lo = _split_bf16(g)
            kr_ref[...] = (lax.dot_general(hi, e_ref[...], NT_DIMS, preferred_element_type=F32)
                           + lax.dot_general(lo, e_ref[...], NT_DIMS, preferred_element_type=F32))

    tab = pl.BlockSpec((tm, HEAD_LANES), lambda i: (i, 0))
    row = pl.BlockSpec((tm, N), lambda i: (i, 0))
    ins, in_specs = [d, *tabs], [row, tab, tab, tab]
    out_shape, out_specs = [jax.ShapeDtypeStruct((T, N), BF16)], [row]
    if spread is not None:
        ins.append(spread)
        in_specs.append(pl.BlockSpec(spread.shape, lambda i: (0, 0)))
        out_shape.append(jax.ShapeDtypeStruct((T, spread.shape[0]), F32))
        out_specs.append(pl.BlockSpec((tm, spread.shape[0]), lambda i: (i, 0)))
    return pl.pallas_call(
        body, out_shape=out_shape, grid=(T // tm,), in_specs=in_specs, out_specs=out_specs,
        compiler_params=_params(("parallel",), VMEM_LIMIT), name=name)(*ins)


def _cmul(ar, ai, br, bi):
    return ar * br - ai * bi, ar * bi + ai * br


def s5_chain(finals, s0, a, n_steps, reverse, name):
    W = finals.shape[-1]
    first = N_SEG - 1 if reverse else 0

    def body(f_ref, s0_ref, a_ref, c_ref):
        pr, pi = jnp.ones((1, W), F32), jnp.zeros((1, W), F32)
        br, bi = a_ref[0], a_ref[1]
        n = n_steps
        while n:
            if n & 1:
                pr, pi = _cmul(pr, pi, br, bi)
            br, bi = _cmul(br, bi, br, bi)
            n >>= 1
        fr, fi = f_ref[0], f_ref[1]
        row = lax.broadcasted_iota(jnp.int32, (N_SEG, W), 0)
        s0r = jnp.broadcast_to(s0_ref[0], (N_SEG, W))
        s0i = jnp.broadcast_to(s0_ref[1], (N_SEG, W))
        cr = jnp.where(row == first, s0r, 0.0)
        ci = jnp.where(row == first, s0i, 0.0)
        shift = N_SEG - 1 if reverse else 1
        for _ in range(N_SEG - 1):
            mr, mi = _cmul(pr, pi, cr, ci)
            tr = pltpu.roll(fr + mr, shift, 0)
            ti = pltpu.roll(fi + mi, shift, 0)
            cr = jnp.where(row == first, s0r, tr)
            ci = jnp.where(row == first, s0i, ti)
        c_ref[0] = cr
        c_ref[1] = ci

    return pl.pallas_call(body, out_shape=jax.ShapeDtypeStruct((2, N_SEG, W), F32), name=name)(finals, s0, a)


def _scan_chunk(bur, bui, st_ref, a_ref, n_steps, reverse):
    for lc in range(S5_LANES // BLK_ST):
        sl = slice(lc * BLK_ST, (lc + 1) * BLK_ST)
        lr = jnp.broadcast_to(a_ref[0, :, sl], (N_SEG, BLK_ST))
        li = jnp.broadcast_to(a_ref[1, :, sl], (N_SEG, BLK_ST))

        def step(jj, carry, sl=sl, lr=lr, li=li):
            sr, si = carry
            j = (n_steps - 1 - jj) if reverse else jj
            r0 = pl.multiple_of(j * N_SEG, N_SEG)
            nr = lr * sr - li * si + bur[pl.ds(r0, N_SEG), sl]
            ni = lr * si + li * sr + bui[pl.ds(r0, N_SEG), sl]
            bur[pl.ds(r0, N_SEG), sl] = nr
            bui[pl.ds(r0, N_SEG), sl] = ni
            return nr, ni

        sr, si = lax.fori_loop(0, n_steps, step, (st_ref[0, :, sl], st_ref[1, :, sl]))
        st_ref[0, :, sl] = sr
        st_ref[1, :, sl] = si


def _project_in(x16, w_re, w_im, bur, bui, adjoint):
    for gb in range(N_BLOCKS):
        xb = x16[:, gb * BLK_CH:(gb + 1) * BLK_CH]
        sl = slice(gb * BLK_ST, (gb + 1) * BLK_ST)
        if adjoint:
            dn = (((1,), (1,)), ((), ()))
            bur[:, sl] = lax.dot_general(xb, w_re[gb], dn, preferred_element_type=F32)
            bui[:, sl] = -lax.dot_general(xb, w_im[gb], dn, preferred_element_type=F32)
        else:
            bur[:, sl] = jnp.dot(xb, w_re[gb], preferred_element_type=F32)
            bui[:, sl] = jnp.dot(xb, w_im[gb], preferred_element_type=F32)


def s5_scan(act, w_re, w_im, a, init, *, reverse, adjoint=False, c_re=None, c_im=None, add=None,
            want_ckpt=False, rows=None, name):
    act_off, N = rows if rows is not None else (0, act.shape[0])
    R = math.gcd(ROW_TILE, N, act_off)
    nch, jc = N // R, R // N_SEG
    with_out = c_re is not None

    def chunk(i):
        return (nch - 1 - i) if reverse else i

    def body(*refs):
        act_ref, wre_ref, wim_ref, a_ref, init_ref = refs[:5]
        k = 5
        if with_out:
            cre_ref, cim_ref = refs[k:k + 2]
            k += 2
        if add is not None:
            add_ref = refs[k]
            k += 1
        if with_out:
            out_ref = refs[k]
            k += 1
        if want_ckpt:
            ck_ref = refs[k]
            k += 1
        fin_ref, bur, bui = refs[k:k + 3]

        @pl.when(pl.program_id(0) == 0)
        def _():
            fin_ref[...] = init_ref[...]

        if want_ckpt:
            ck_ref[0] = fin_ref[...]
        _project_in(act_ref[...].astype(BF16), wre_ref, wim_ref, bur, bui, adjoint)
        _scan_chunk(bur, bui, fin_ref, a_ref, jc, reverse)
        if with_out:
            for gb in range(N_BLOCKS):
                sl = slice(gb * BLK_ST, (gb + 1) * BLK_ST)
                y = (jnp.dot(bur[:, sl].astype(BF16), cre_ref[gb], preferred_element_type=F32)
                     - jnp.dot(bui[:, sl].astype(BF16), cim_ref[gb], preferred_element_type=F32))
                cs = slice(gb * BLK_CH, (gb + 1) * BLK_CH)
                if add is not None:
                    y = y + add_ref[:, cs]
                out_ref[:, cs] = y

    row_spec = pl.BlockSpec((R, D_MODEL), lambda i: (chunk(i), 0))
    act_spec = pl.BlockSpec((R, D_MODEL), lambda i: (chunk(i) + act_off // R, 0))
    w_spec = pl.BlockSpec(w_re.shape, lambda i: (0, 0, 0))
    st_spec = pl.BlockSpec((2, N_SEG, S5_LANES), lambda i: (0, 0, 0))
    ins = [act, w_re, w_im, a, init]
    in_specs = [act_spec, w_spec, w_spec, pl.BlockSpec((2, 1, S5_LANES), lambda i: (0, 0, 0)), st_spec]
    if with_out:
        ins += [c_re, c_im]
        in_specs += [pl.BlockSpec(c_re.shape, lambda i: (0, 0, 0))] * 2
    if add is not None:
        ins.append(add)
        in_specs.append(row_spec)
    out_shape, out_specs = [], []
    if with_out:
        out_shape.append(jax.ShapeDtypeStruct((N, D_MODEL), F32))
        out_specs.append(row_spec)
    if want_ckpt:
        out_shape.append(jax.ShapeDtypeStruct((nch, 2, N_SEG, S5_LANES), F32))
        out_specs.append(pl.BlockSpec((1, 2, N_SEG, S5_LANES), lambda i: (chunk(i), 0, 0, 0)))
    out_shape.append(jax.ShapeDtypeStruct((2, N_SEG, S5_LANES), F32))
    out_specs.append(st_spec)
    res = pl.pallas_call(
        body, out_shape=out_shape, grid=(nch,), in_specs=in_specs, out_specs=out_specs,
        scratch_shapes=[pltpu.VMEM((R, S5_LANES), F32), pltpu.VMEM((R, S5_LANES), F32)],
        compiler_params=_params(("arbitrary",), VMEM_LIMIT), name=name)(*ins)
    res = list(res)
    out = res.pop(0) if with_out else None
    ckpt = res.pop(0) if want_ckpt else None
    return out, ckpt, res[0]


def s5_grads(dy, u, ckpt, b_re, b_im, c_re, c_im, lam, init_adj, *, reverse, add=None, u_off=0, du_rows=None,
             du_into=None, name):
    N = dy.shape[0]
    du_total, du_first = du_rows if du_rows is not None else (N, 0)
    R = math.gcd(ROW_TILE, N, u_off, du_first)
    nch, jc = N // R, R // N_SEG
    W = S5_LANES

    def chunk(i):
        return i if reverse else (nch - 1 - i)

    def body(*refs):
        dy_ref, u_ref, ck_ref, bre_ref, bim_ref, cre_ref, cim_ref, lam_ref, init_ref = refs[:9]
        k = 9
        if add is not None:
            add_ref = refs[k]
            k += 1
        k += du_into is not None
        du_ref, dlam_ref, dbre_ref, dbim_ref, dcre_ref, dcim_ref, fin_ref = refs[k:k + 7]
        sr_buf, si_buf, er_buf, ei_buf, st_buf = refs[k + 7:k + 12]

        @pl.when(pl.program_id(0) == 0)
        def _():
            fin_ref[...] = init_ref[...]
            dlam_ref[...] = jnp.zeros_like(dlam_ref)
            dbre_ref[...] = jnp.zeros_like(dbre_ref)
            dbim_ref[...] = jnp.zeros_like(dbim_ref)
            dcre_ref[...] = jnp.zeros_like(dcre_ref)
            dcim_ref[...] = jnp.zeros_like(dcim_ref)

        u16 = u_ref[...].astype(BF16)
        dy16 = dy_ref[...].astype(BF16)
        st_buf[...] = ck_ref[0]
        _project_in(u16, bre_ref, bim_ref, sr_buf, si_buf, False)
        _scan_chunk(sr_buf, si_buf, st_buf, lam_ref, jc, reverse)
        _project_in(dy16, cre_ref, cim_ref, er_buf, ei_buf, True)
        for lc in range(W // BLK_ST):
            sl = slice(lc * BLK_ST, (lc + 1) * BLK_ST)
            lr = jnp.broadcast_to(lam_ref[0, :, sl], (N_SEG, BLK_ST))
            li = jnp.broadcast_to(lam_ref[1, :, sl], (N_SEG, BLK_ST))

            def one(r0, spr, spi, carry, sl=sl, lr=lr, li=li):
                gr, gi, ar, ai = carry
                nr = er_buf[pl.ds(r0, N_SEG), sl] + lr * gr + li * gi
                ni = ei_buf[pl.ds(r0, N_SEG), sl] + lr * gi - li * gr
                er_buf[pl.ds(r0, N_SEG), sl] = nr
                ei_buf[pl.ds(r0, N_SEG), sl] = ni
                return nr, ni, ar + spr * nr + spi * ni, ai + spr * ni - spi * nr

            def step(ff, carry, sl=sl, one=one):
                f = jc - 1 - ff
                j = (jc - 1 - f) if reverse else f
                jp = (j + 1) if reverse else (j - 1)
                r0 = pl.multiple_of(j * N_SEG, N_SEG)
                p0 = pl.multiple_of(jp * N_SEG, N_SEG)
                return one(r0, sr_buf[pl.ds(p0, N_SEG), sl], si_buf[pl.ds(p0, N_SEG), sl], carry)

            carry = (fin_ref[0, :, sl], fin_ref[1, :, sl], dlam_ref[0, :, sl], dlam_ref[1, :, sl])
            carry = lax.fori_loop(0, jc - 1, step, carry)
            r_first = (jc - 1) * N_SEG if reverse else 0
            gr, gi, ar, ai = one(r_first, ck_ref[0, 0, :, sl], ck_ref[0, 1, :, sl], carry)
            fin_ref[0, :, sl] = gr
            fin_ref[1, :, sl] = gi
            dlam_ref[0, :, sl] = ar
            dlam_ref[1, :, sl] = ai
        tn = (((0,), (0,)), ((), ()))
        nt = (((1,), (1,)), ((), ()))
        for gb in range(N_BLOCKS):
            sl = slice(gb * BLK_ST, (gb + 1) * BLK_ST)
            cs = slice(gb * BLK_CH, (gb + 1) * BLK_CH)
            gr16 = er_buf[:, sl].astype(BF16)
            gi16 = ei_buf[:, sl].astype(BF16)
            du = (lax.dot_general(gr16, bre_ref[gb], nt, preferred_element_type=F32)
                  + lax.dot_general(gi16, bim_ref[gb], nt, preferred_element_type=F32))
            if add is not None:
                du = du + add_ref[:, cs]
            du_ref[:, cs] = du
            ub, dyb = u16[:, cs], dy16[:, cs]
            dbre_ref[gb] += lax.dot_general(ub, gr16, tn, preferred_element_type=F32)
            dbim_ref[gb] += lax.dot_general(ub, gi16, tn, preferred_element_type=F32)
            dcre_ref[gb] += lax.dot_general(sr_buf[:, sl].astype(BF16), dyb, tn, preferred_element_type=F32)
            dcim_ref[gb] -= lax.dot_general(si_buf[:, sl].astype(BF16), dyb, tn, preferred_element_type=F32)

    row_spec = pl.BlockSpec((R, D_MODEL), lambda i: (chunk(i), 0))
    st_spec = pl.BlockSpec((2, N_SEG, W), lambda i: (0, 0, 0))
    wb_spec = pl.BlockSpec(b_re.shape, lambda i: (0, 0, 0))
    wc_spec = pl.BlockSpec(c_re.shape, lambda i: (0, 0, 0))
    ins = [dy, u, ckpt, b_re, b_im, c_re, c_im, lam, init_adj]
    u_spec = pl.BlockSpec((R, D_MODEL), lambda i: (chunk(i) + u_off // R, 0))
    in_specs = [row_spec, u_spec, pl.BlockSpec((1, 2, N_SEG, W), lambda i: (chunk(i), 0, 0, 0)),
                wb_spec, wb_spec, wc_spec, wc_spec, pl.BlockSpec((2, 1, W), lambda i: (0, 0, 0)), st_spec]
    if add is not None:
        ins.append(add)
        in_specs.append(row_spec)
    aliases = {}
    if du_into is not None:
        aliases[len(ins)] = 0
        ins.append(du_into)
        in_specs.append(pl.BlockSpec(memory_space=pl.ANY))
    du_spec = pl.BlockSpec((R, D_MODEL), lambda i: (chunk(i) + du_first // R, 0))
    out_shape = [jax.ShapeDtypeStruct((du_total, D_MODEL), F32), jax.ShapeDtypeStruct((2, N_SEG, W), F32),
                 jax.ShapeDtypeStruct(b_re.shape, F32), jax.ShapeDtypeStruct(b_re.shape, F32),
                 jax.ShapeDtypeStruct(c_re.shape, F32), jax.ShapeDtypeStruct(c_re.shape, F32),
                 jax.ShapeDtypeStruct((2, N_SEG, W), F32)]
    out_specs = [du_spec, st_spec, wb_spec, wb_spec, wc_spec, wc_spec, st_spec]
    return pl.pallas_call(
        body, out_shape=out_shape, grid=(nch,), in_specs=in_specs, out_specs=out_specs, input_output_aliases=aliases,
        scratch_shapes=[pltpu.VMEM((R, W), F32) for _ in range(4)] + [pltpu.VMEM((2, N_SEG, W), F32)],
        compiler_params=_params(("arbitrary",), VMEM_LIMIT), name=name)(*ins)


def adamw(w, g, m, v, name="adamw", after=None):
    n, d = w.shape
    lanes = -(-d // 128) * 128
    tm = n
    while tm * lanes * 4 > (1 << 20) and tm % 16 == 0:
        tm //= 2
    c1 = 1.0 - ADAM_B1 ** ADAM_STEP
    c2 = 1.0 - ADAM_B2 ** ADAM_STEP

    def body(w_ref, g_ref, m_ref, v_ref, *rest):
        d_ref, nm_ref, nv_ref = rest[-3:]
        g_ = g_ref[...]
        m_ = ADAM_B1 * m_ref[...] + (1.0 - ADAM_B1) * g_
        v_ = ADAM_B2 * v_ref[...] + (1.0 - ADAM_B2) * (g_ * g_)
        d_ref[...] = -ADAM_LR * ((m_ / c1) / (jnp.sqrt(v_ / c2) + ADAM_EPS) + ADAM_WD * w_ref[...])
        nm_ref[...] = m_
        nv_ref[...] = v_

    spec = pl.BlockSpec((tm, d), lambda i: (i, 0))
    extra = [] if after is None else [after]
    return pl.pallas_call(
        body, out_shape=[jax.ShapeDtypeStruct((n, d), F32)] * 3, grid=(n // tm,),
        in_specs=[spec] * 4 + [pl.BlockSpec(memory_space=pl.ANY)] * len(extra), out_specs=[spec] * 3,
        compiler_params=_params(("parallel",), VMEM_LIMIT), name=name)(w, g, m, v, *extra)


def _coords():
    return lax.axis_index("x"), lax.axis_index("y"), lax.axis_index("c")


def exchange(arrays, out_shapes, remote, local, name, aliases=None):
    n_in, n_out, n_rem, n_loc = len(arrays), len(out_shapes), len(remote), len(local)

    def at(ref, idx):
        return ref if idx is None else ref.at[idx]

    def body(*refs):
        ins, outs = refs[:n_in], refs[n_in:n_in + n_out]
        send_sems, recv_sems, local_sems = refs[n_in + n_out:]
        me = _coords()
        sends, recvs = [], []
        for k, (flip, ii, src_at, oi, dst_at) in enumerate(remote):
            peer = (me[0] ^ flip[0], me[1] ^ flip[1], me[2] ^ flip[2])
            src = at(ins[ii], src_at(me, peer))
            sends.append(pltpu.make_async_remote_copy(
                src_ref=src, dst_ref=at(outs[oi], dst_at(me)), send_sem=send_sems.at[k], recv_sem=recv_sems.at[k],
                device_id=peer, device_id_type=MESH))
            recvs.append(pltpu.make_async_remote_copy(
                src_ref=src, dst_ref=at(outs[oi], dst_at(peer)), send_sem=send_sems.at[k], recv_sem=recv_sems.at[k],
                device_id=peer, device_id_type=MESH))
        locs = [pltpu.make_async_copy(at(ins[ii], src_at(me)), at(outs[oi], dst_at(me)), local_sems.at[k])
                for k, (ii, src_at, oi, dst_at) in enumerate(local)]
        for cp in locs + sends:
            cp.start()
        for cp in recvs:
            cp.wait_recv()
        for cp in sends:
            cp.wait_send()
        for cp in locs:
            cp.wait()

    hbm = pl.BlockSpec(memory_space=pl.ANY)
    return pl.pallas_call(
        body, out_shape=list(out_shapes), in_specs=[hbm] * n_in, out_specs=[hbm] * n_out,
        scratch_shapes=[pltpu.SemaphoreType.DMA((n_rem,)), pltpu.SemaphoreType.DMA((n_rem,)),
                        pltpu.SemaphoreType.DMA((max(n_loc, 1),))],
        input_output_aliases=aliases or {}, name=name)(*arrays)


ALL_FLIPS = [(dx, dy, dc) for dx in (0, 1) for dy in (0, 1) for dc in (0, 1)][1:]
CHIP_FLIPS = [(1, 0, 0), (0, 1, 0), (1, 1, 0)]
CORE_FLIP = (0, 0, 1)


def _dev_index(p):
    return 4 * p[0] + 2 * p[1] + p[2]


def _chip_index(p):
    return 2 * p[0] + p[1]


def _gather(xs, flips, index, n, name):
    arrays = [x[None] for x in xs]
    outs = [jax.ShapeDtypeStruct((n,) + x.shape, x.dtype) for x in xs]
    remote = [(f, a, lambda me, peer: (0,), a, lambda s: (index(s),)) for a in range(len(xs)) for f in flips]
    local = [(a, lambda me: (0,), a, lambda me: (index(me),)) for a in range(len(xs))]
    return exchange(arrays, outs, remote, local, name)


def allgather_devices(x, name):
    return _gather([x], ALL_FLIPS, _dev_index, N_DEV, name)[0]


def allgather_chips(xs, name):
    return _gather(xs, CHIP_FLIPS, _chip_index, N_CHIP, name)


def gather_halves(xs, name):
    n = len(xs)
    nk = n * len(CHIP_FLIPS)

    def body(*refs):
        ins, outs = refs[:n], refs[n:2 * n]
        ici_send, ici_recv, d2d_send, d2d_recv = refs[2 * n:]
        me = _coords()
        sibling = (me[0], me[1], 1 - me[2])
        first, passed, landed = [], [], []
        for a in range(n):
            half = ins[a].shape[0] // 2
            mine = ins[a].at[pl.ds(pl.multiple_of(me[2] * half, 16), half)]
            for j, flip in enumerate(CHIP_FLIPS):
                k = a * len(CHIP_FLIPS) + j
                peer = (me[0] ^ flip[0], me[1] ^ flip[1], me[2])
                first.append(pltpu.make_async_remote_copy(
                    src_ref=mine, dst_ref=outs[a].at[_chip_index(me), me[2]], send_sem=ici_send.at[k],
                    recv_sem=ici_recv.at[k], device_id=peer, device_id_type=MESH))
                arrived = outs[a].at[_chip_index(peer), me[2]]
                landed.append(pltpu.make_async_remote_copy(
                    src_ref=mine, dst_ref=arrived, send_sem=ici_send.at[k], recv_sem=ici_recv.at[k],
                    device_id=peer, device_id_type=MESH))
                passed.append(pltpu.make_async_remote_copy(
                    src_ref=arrived, dst_ref=arrived, send_sem=d2d_send.at[k], recv_sem=d2d_recv.at[k],
                    device_id=sibling, device_id_type=MESH))
        for cp in first:
            cp.start()
        for k in range(nk):
            landed[k].wait_recv()
            passed[k].start()
        for a in range(n):
            for j, flip in enumerate(CHIP_FLIPS):
                k = a * len(CHIP_FLIPS) + j
                peer_chip = _chip_index((me[0] ^ flip[0], me[1] ^ flip[1]))
                from_sibling = outs[a].at[peer_chip, 1 - me[2]]
                pltpu.make_async_remote_copy(
                    src_ref=from_sibling, dst_ref=from_sibling, send_sem=d2d_send.at[k], recv_sem=d2d_recv.at[k],
                    device_id=sibling, device_id_type=MESH).wait_recv()
        for cp in first + passed:
            cp.wait_send()

    hbm = pl.BlockSpec(memory_space=pl.ANY)
    return pl.pallas_call(
        body, out_shape=[jax.ShapeDtypeStruct((N_CHIP, 2, x.shape[0] // 2, x.shape[1]), x.dtype) for x in xs],
        in_specs=[hbm] * n, out_specs=[hbm] * n,
        scratch_shapes=[pltpu.SemaphoreType.DMA((nk,)) for _ in range(4)], name=name)(*xs)


HBM_SPEC = pl.BlockSpec(memory_space=pltpu.HBM)
SEM_SPEC = pl.BlockSpec(memory_space=pltpu.SEMAPHORE)
DATAFLOW = pltpu.SideEffectType.DATAFLOW_SIDE_EFFECTING


def _at(ref, idx):
    return ref if idx is None else ref.at[idx]


def _peer(me, flip):
    return (me[0] ^ flip[0], me[1] ^ flip[1], me[2] ^ flip[2])


def exchange_start(arrays, land_shapes, remote, name, after=None):
    n_in, n_out, nk = len(arrays), len(land_shapes), len(remote)
    after = list(after or [])
    n_after = len(after)

    def body(*refs):
        srcs, lands = refs[:n_in], refs[n_in:n_in + n_out]
        first_out = n_in + n_out + n_after
        send_sems, recv_sems, token = refs[first_out], refs[first_out + 1], refs[-1]
        me = _coords()
        for k, (flip, ii, src_at, oi, dst_at) in enumerate(remote):
            peer = _peer(me, flip)
            pltpu.make_async_remote_copy(
                src_ref=_at(srcs[ii], src_at(me, peer)), dst_ref=_at(lands[oi], dst_at(me)), send_sem=send_sems.at[k],
                recv_sem=recv_sems.at[k], device_id=peer, device_id_type=MESH).start()
        token[...] = jnp.zeros_like(token)

    lands = [lax.empty(s.shape, s.dtype) for s in land_shapes]
    bufs = list(arrays) + lands
    out = pl.pallas_call(
        body, name=name,
        out_shape=(pltpu.SemaphoreType.DMA((nk,)), pltpu.SemaphoreType.DMA((nk,)),
                   *[pltpu.HBM(b.shape, b.dtype) for b in bufs], jax.ShapeDtypeStruct((8, 128), F32)),
        in_specs=[HBM_SPEC] * len(bufs) + [pl.BlockSpec(memory_space=pl.ANY)] * n_after,
        out_specs=(SEM_SPEC, SEM_SPEC, *[HBM_SPEC] * len(bufs), pl.BlockSpec(memory_space=pltpu.VMEM)),
        input_output_aliases={a: 2 + a for a in range(len(bufs))},
        compiler_params=pltpu.CompilerParams(has_side_effects=DATAFLOW),
    )(*[pltpu.with_memory_space_constraint(b, pltpu.HBM) for b in bufs], *after)
    flight = (out[0], out[1], list(out[2:2 + n_in]), list(out[2 + n_in:2 + n_in + n_out]), remote)
    return flight, out[-1]


def exchange_wait(flight, after, name):
    send_sems, recv_sems, arrays, lands, remote = flight
    n_in, n_out = len(arrays), len(lands)
    after = list(after) if isinstance(after, (list, tuple)) else [after]

    def body(*refs):
        srcs, lnds = refs[:n_in], refs[n_in:n_in + n_out]
        s_sems, r_sems = refs[n_in + n_out], refs[n_in + n_out + 1]
        me = _coords()
        for k, (flip, ii, src_at, oi, dst_at) in enumerate(remote):
            peer = _peer(me, flip)
            copy = pltpu.make_async_remote_copy(
                src_ref=_at(srcs[ii], src_at(me, peer)), dst_ref=_at(lnds[oi], dst_at(peer)), send_sem=s_sems.at[k],
                recv_sem=r_sems.at[k], device_id=peer, device_id_type=MESH)
            copy.wait_send()
            copy.wait_recv()

    bufs = list(arrays) + list(lands)
    out = pl.pallas_call(
        body, name=name,
        out_shape=tuple(pltpu.HBM(b.shape, b.dtype) for b in bufs),
        in_specs=[HBM_SPEC] * len(bufs) + [SEM_SPEC, SEM_SPEC] + [pl.BlockSpec(memory_space=pl.ANY)] * len(after),
        out_specs=tuple([HBM_SPEC] * len(bufs)),
        input_output_aliases={a: a for a in range(len(bufs))},
        compiler_params=pltpu.CompilerParams(has_side_effects=DATAFLOW),
    )(*bufs, send_sems, recv_sems, *after)
    return list(out[:n_in]), list(out[n_in:])


def _half_tile(h, cd):
    return h if h * cd * 4 <= (1 << 20) else math.gcd(512, h)


def pair_add(g, got, core, out_dtype, name):
    _, _, h, cd = g.shape
    th = _half_tile(h, cd)

    def body(c_ref, g_ref, got_ref, o_ref):
        o_ref[0] = (g_ref[0, 0] + got_ref[0]).astype(o_ref.dtype)

    return pl.pallas_call(
        body, out_shape=jax.ShapeDtypeStruct((N_CHIP, h, cd), out_dtype),
        grid_spec=pltpu.PrefetchScalarGridSpec(
            num_scalar_prefetch=1, grid=(N_CHIP, h // th),
            in_specs=[pl.BlockSpec((1, 1, th, cd), lambda q, i, c: (q, c[0], i, 0)),
                      pl.BlockSpec((1, th, cd), lambda q, i, c: (q, i, 0))],
            out_specs=pl.BlockSpec((1, th, cd), lambda q, i, c: (q, i, 0))),
        compiler_params=_params(("parallel", "parallel"), VMEM_LIMIT), name=name)(core, g, got)


def sum_chips(parts, sums, place, name):
    _, h, cd = parts.shape
    th = _half_tile(h, cd)

    def body(pc_ref, p_ref, own_ref, o_ref):
        acc = None
        for q in range(N_CHIP):
            term = jnp.where(pc_ref[1] == q, own_ref[0], p_ref[q]).astype(F32)
            acc = term if acc is None else acc + term
        o_ref[0] = acc

    return pl.pallas_call(
        body, out_shape=jax.ShapeDtypeStruct((2, h, cd), F32),
        grid_spec=pltpu.PrefetchScalarGridSpec(
            num_scalar_prefetch=1, grid=(h // th,),
            in_specs=[pl.BlockSpec((N_CHIP, th, cd), lambda i, pc: (0, i, 0)),
                      pl.BlockSpec((1, th, cd), lambda i, pc: (pc[1], i, 0))],
            out_specs=pl.BlockSpec((1, th, cd), lambda i, pc: (pc[0], i, 0))),
        compiler_params=_params(("parallel",), VMEM_LIMIT), name=name)(place, parts, sums)


def to_segments(a, n_ctx):
    def one(p):
        n = p.shape[0]
        return p.reshape(N_SEG, n // N_SEG, -1).transpose(1, 0, 2).reshape(n, -1)
    return jnp.concatenate([one(a[:n_ctx]), one(a[n_ctx:])], axis=0) if n_ctx else one(a)


def rows_to_segments(a, n_ctx, name):
    n, d = a.shape
    tj = n_ctx // N_SEG
    per_seg = (n - n_ctx) // N_SEG // tj
    assert n_ctx % N_SEG == 0 and (n - n_ctx) % (N_SEG * tj) == 0

    def body(*refs):
        out_ref, slabs = refs[N_SEG:]
        for c in range(d // HEAD_LANES):
            cols = slice(c * HEAD_LANES, (c + 1) * HEAD_LANES)
            for seg in range(N_SEG):
                slabs[c, pl.ds(seg, tj, stride=N_SEG), :] = refs[seg][:, cols]
            out_ref[:, cols] = slabs[c]

    def seg_spec(seg):
        return pl.BlockSpec((tj, d), lambda i: (jnp.where(i == 0, seg, N_SEG + seg * per_seg + i - 1), 0))

    return pl.pallas_call(
        body, out_shape=jax.ShapeDtypeStruct((n, d), a.dtype), grid=(1 + per_seg,),
        in_specs=[seg_spec(seg) for seg in range(N_SEG)], out_specs=pl.BlockSpec((N_SEG * tj, d), lambda i: (i, 0)),
        scratch_shapes=[pltpu.VMEM((d // HEAD_LANES, N_SEG * tj, HEAD_LANES), a.dtype)],
        compiler_params=_params(("parallel",), VMEM_LIMIT), name=name)(*[a] * N_SEG)


def rope_tables(n_ctx, n_lat):
    f32 = np.float32
    rows = n_lat // GRID_W
    row = np.repeat(np.arange(rows), GRID_W).astype(f32)
    col = np.tile(np.arange(GRID_W), rows).astype(f32)
    d = QK_ROPE_DIM // 2
    inv = (f32(1.0) / np.power(f32(ROPE_THETA), np.arange(0, d, 2, dtype=f32) / f32(d))).astype(f32)
    ang = np.concatenate([row[:, None] * inv[None, :], col[:, None] * inv[None, :]], axis=1).astype(f32)
    cos = np.concatenate([np.ones((n_ctx, d), f32), np.cos(ang)], axis=0)
    sin = np.concatenate([np.zeros((n_ctx, d), f32), np.sin(ang)], axis=0)
    q = QK_ROPE_DIM // 4
    T = n_ctx + n_lat
    ones, zeros = np.ones((T, QK_NOPE_DIM), f32), np.zeros((T, QK_NOPE_DIM), f32)
    tail, z8 = np.zeros((T, HEAD_LANES - QK_DIM), f32), np.zeros((T, q), f32)
    cr, cc, sr, sc = cos[:, :q], cos[:, q:], sin[:, :q], sin[:, q:]
    cos_t = np.concatenate([ones, cr, cr, cc, cc, tail], axis=1)
    sin_next = np.concatenate([zeros, -sr, z8, -sc, z8, tail], axis=1)
    sin_prev = np.concatenate([zeros, z8, sr, z8, sc, tail], axis=1)
    return tuple(jnp.asarray(t, F32) for t in (cos_t, sin_next, sin_prev))


def pad_heads(w, used):
    k = w.shape[0]
    return jnp.pad(w.reshape(k, MLA_HEADS, used), ((0, 0), (0, 0), (0, HEAD_LANES - used))).reshape(k, -1)


def unpad_heads(w, used):
    k = w.shape[0]
    return w.reshape(k, MLA_HEADS, HEAD_LANES)[:, :, :used].reshape(k, MLA_HEADS * used)


def rotary_spread():
    lane = np.arange(MLA_HEADS * HEAD_LANES) % HEAD_LANES
    return jnp.asarray(lane[None, :] == (QK_NOPE_DIM + np.arange(QK_ROPE_DIM))[:, None], BF16)


def s5_discretise(a_re, a_im, log_step, b_re, b_im):
    dt = jnp.exp(log_step)[:, None]
    mag = jnp.exp(a_re * dt)
    lb_re = mag * jnp.cos(a_im * dt)
    lb_im = mag * jnp.sin(a_im * dt)
    den = a_re * a_re + a_im * a_im
    nr = lb_re - 1.0
    f_re = ((nr * a_re + lb_im * a_im) / den)[:, None, :]
    f_im = ((lb_im * a_re - nr * a_im) / den)[:, None, :]
    return lb_re, lb_im, f_re * b_re - f_im * b_im, f_re * b_im + f_im * b_re


def s5_block_weights(lb_re, lb_im, bb_re, bb_im, c_re, c_im):
    eye = jnp.eye(GROUPS_PER_BLOCK, dtype=F32)
    lam = jnp.stack([lb_re.reshape(1, S5_LANES), lb_im.reshape(1, S5_LANES)])

    def b_blocks(bb):
        t = bb.reshape(N_BLOCKS, GROUPS_PER_BLOCK, S5_GROUP, S5_STATE)
        return jnp.einsum("bgcp,gh->bgchp", t, eye).reshape(N_BLOCKS, BLK_CH, BLK_ST).astype(BF16)

    def c_blocks(cc):
        t = cc.reshape(N_BLOCKS, GROUPS_PER_BLOCK, S5_GROUP, S5_STATE)
        return jnp.einsum("bgcp,gh->bgphc", t, eye).reshape(N_BLOCKS, BLK_ST, BLK_CH).astype(BF16)

    return lam, b_blocks(bb_re), b_blocks(bb_im), c_blocks(c_re), c_blocks(c_im)


def b_block_diag(db):
    t = db.reshape(N_BLOCKS, GROUPS_PER_BLOCK, S5_GROUP, GROUPS_PER_BLOCK, S5_STATE)
    return jnp.einsum("bgchp,gh->bgcp", t, jnp.eye(GROUPS_PER_BLOCK, dtype=F32)).reshape(S5_GROUPS, S5_GROUP, S5_STATE)


def c_block_diag(dc):
    t = dc.reshape(N_BLOCKS, GROUPS_PER_BLOCK, S5_STATE, GROUPS_PER_BLOCK, S5_GROUP)
    return jnp.einsum("bgphc,gh->bgcp", t, jnp.eye(GROUPS_PER_BLOCK, dtype=F32)).reshape(S5_GROUPS, S5_GROUP, S5_STATE)


def conj(a):
    return jnp.stack([a[0], -a[1]])


def _narrow(shape):
    return len(shape) >= 2 and shape[-1] < min(HEAD_LANES, shape[-2])


def _stored(a):
    return jnp.swapaxes(a, -1, -2) if _narrow(a.shape) else a


def _stored_shape(shape):
    return tuple(shape[:-2]) + (shape[-1], shape[-2]) if _narrow(shape) else tuple(shape)


def _from_stored(a, shape):
    return jnp.swapaxes(a, -1, -2) if _narrow(shape) else a


PACK_TILE = 16 * 128


def pack_flat(parts, dtype):
    flat = [p.reshape(-1).astype(dtype) for p in parts]
    sizes = [f.shape[0] for f in flat]
    total = sum(sizes)
    pad = (-total) % PACK_TILE
    if pad:
        flat.append(jnp.zeros((pad,), dtype))
    offs = np.cumsum([0] + sizes)[:-1].tolist()
    return jnp.concatenate(flat).reshape(-1, 128), offs


def unpack_flat(buf, offs, shapes):
    flat = buf.reshape(-1)
    return [flat[o:o + int(np.prod(s))].reshape(s) for o, s in zip(offs, shapes)]


def s5_forward(p1, n_ctx, dirs):
    saved = []
    y = None
    ctx_rows, lat_rows = (0, n_ctx), (n_ctx, p1.shape[0] - n_ctx)
    zeros_tile = jnp.zeros((2, N_SEG, S5_LANES), F32)
    zeros_row = jnp.zeros((2, 1, S5_LANES), F32)
    for k, (lam, b_re, b_im, c_re, c_im) in enumerate(dirs):
        rev = k == 1
        last = 0 if rev else N_SEG - 1
        _, _, fin = s5_scan(p1, b_re, b_im, lam, zeros_tile, reverse=rev, rows=ctx_rows, name=f"s5_ctx_finals{k}")
        carry_c = s5_chain(fin, zeros_row, lam, n_ctx // N_SEG, rev, name=f"s5_ctx_chain{k}")
        _, ck_c, fin_c = s5_scan(p1, b_re, b_im, lam, carry_c, reverse=rev, want_ckpt=True, rows=ctx_rows,
                                 name=f"s5_ctx_scan{k}")
        s0 = fin_c[:, last:last + 1, :]
        _, _, fin = s5_scan(p1, b_re, b_im, lam, zeros_tile, reverse=rev, rows=lat_rows, name=f"s5_lat_finals{k}")
        carry_l = s5_chain(fin, s0, lam, lat_rows[1] // N_SEG, rev, name=f"s5_lat_chain{k}")
        y, ck_l, _ = s5_scan(p1, b_re, b_im, lam, carry_l, reverse=rev, c_re=c_re, c_im=c_im, add=y,
                             want_ckpt=True, rows=lat_rows, name=f"s5_lat_scan{k}")
        saved.append((ck_c, ck_l))
    return y, saved


def s5_backward(dy_l, du_extra_l, p1, n_ctx, dirs, saved):
    n_lat = p1.shape[0] - n_ctx
    zeros_tile = jnp.zeros((2, N_SEG, S5_LANES), F32)
    zeros_row = jnp.zeros((2, 1, S5_LANES), F32)
    dy_c = jnp.zeros((n_ctx, D_MODEL), F32)
    du_l, du_c = du_extra_l, None
    grads = []
    for k, (lam, b_re, b_im, c_re, c_im) in enumerate(dirs):
        rev = k == 1
        lam_c = conj(lam)
        ck_c, ck_l = saved[k]
        first = N_SEG - 1 if rev else 0
        _, _, fin = s5_scan(dy_l, c_re, c_im, lam_c, zeros_tile, reverse=not rev, adjoint=True,
                            name=f"s5_lat_adj_finals{k}")
        carry = s5_chain(fin, zeros_row, lam_c, n_lat // N_SEG, not rev, name=f"s5_lat_adj_chain{k}")
        whole = k == len(dirs) - 1
        du_l, dlam_l, dbr_l, dbi_l, dcr_l, dci_l, fin_a = s5_grads(
            dy_l, p1, ck_l, b_re, b_im, c_re, c_im, lam, carry, reverse=rev, add=du_l, u_off=n_ctx,
            du_rows=(p1.shape[0], n_ctx) if whole else None, name=f"s5_lat_grads{k}")
        g0 = fin_a[:, first:first + 1, :]
        carry = s5_chain(zeros_tile, g0, lam_c, n_ctx // N_SEG, not rev, name=f"s5_ctx_adj_chain{k}")
        du_c, dlam_c, dbr_c, dbi_c, _, _, _ = s5_grads(
            dy_c, p1, ck_c, b_re, b_im, c_re, c_im, lam, carry, reverse=rev, add=du_c,
            du_rows=(p1.shape[0], 0) if whole else None, du_into=du_l if whole else None, name=f"s5_ctx_grads{k}")
        dlam = jnp.sum(dlam_l + dlam_c, axis=1)
        grads.append((dlam, b_block_diag(dbr_l + dbr_c), b_block_diag(dbi_l + dbi_c),
                      c_block_diag(dcr_l), c_block_diag(dci_l)))
    return du_c, grads


def local_step(x, ctx, target, mod, w, late=None, reducer=None):
    L, Lc = x.shape[0], ctx.shape[0]
    T = L + Lc
    assert L % Lc == 0 and Lc % (2 * N_SEG) == 0 and L % GRID_W == 0
    D = D_MODEL
    X0 = (ctx, x)

    def mod_of(i, j):
        return mod[i, :, j, :][:, None, :]

    def vec(v):
        return v.reshape(1, 1, -1).astype(F32)

    g0 = vec(w["norm_g"][0])
    H0, p0 = norm_proj(X0, g0, mod_of(0, 1), mod_of(0, 0), w["mla_w_in"], Lc, "l0_norm_in")
    cq = Rows(p0, Q_LORA_RANK, col_blk=D // Q_LORA_RANK)
    ckv = Rows(p0, KV_LORA_RANK, col_blk=(D + Q_LORA_RANK) // KV_LORA_RANK)
    kr = p0[:, D + Q_LORA_RANK + KV_LORA_RANK:D + P0_HEAD]
    qng, kvng = vec(w["mla_q_norm"]), vec(w["mla_kv_norm"])
    tabs = rope_tables(Lc, L)
    spread = rotary_spread()
    if late is not None:
        w = {**w, **late["qkv"](p0)}
    w_uq_p = pad_heads(w["mla_w_uq"], QK_DIM)
    w_ukv3 = w["mla_w_ukv"].reshape(KV_LORA_RANK, MLA_HEADS, QK_NOPE_DIM + V_HEAD_DIM)
    w_kn_p = pad_heads(w_ukv3[:, :, :QK_NOPE_DIM].reshape(KV_LORA_RANK, -1), QK_NOPE_DIM)
    w_v = w_ukv3[:, :, QK_NOPE_DIM:].reshape(KV_LORA_RANK, -1)
    qb, qn = q_heads(cq, qng, w_uq_p, tabs)
    kb, vb, kvn = kv_heads(ckv, kvng, w_kn_p, w_v, kr, spread, tabs)
    o, lse, qs = attn_fwd(qb, kb, vb, Lc)
    if late is not None:
        w = {**w, **late["out"](o)}
    X1, og, out0 = mla_post_fwd(o, p0, X0, mod_of(0, 2), w["mla_w_out"], Lc)

    if late is not None:
        w = {**w, **late["l1"](X1)}
    X1p = rows_to_segments(X1, Lc, "l1_to_segments")
    tgt_p = to_segments(target, 0)
    g1 = vec(w["norm_g"][1])
    H1, p1 = norm_proj(X1p, g1, mod_of(1, 1), mod_of(1, 0), w["s5_w_in"], Lc, "l1_norm_in")
    disc_fn = lambda *a: tuple(zip(*[s5_discretise(a[0][k], a[1][k], a[2][k], a[3][k], a[4][k]) for k in range(2)]))
    disc, disc_vjp = jax.vjp(disc_fn, w["s5_a_re"], w["s5_a_im"], w["s5_log_step"], _stored(w["s5_b_re"]),
                             _stored(w["s5_b_im"]))
    dirs = [s5_block_weights(disc[0][k], disc[1][k], disc[2][k], disc[3][k], w["s5_c_re"][k], w["s5_c_im"][k])
            for k in range(2)]
    y_ssm, s5_saved = s5_forward(p1, Lc, dirs)

    row = lambda v: v.reshape(1, D).astype(F32)
    (lvec, dX2, d_yssm, d_u_act, d_z1, d_fg, d_gt1, d_bg, d_d, gw_glu, gw_out) = s5_tail(
        y_ssm, p1, X1p, tgt_p, Lc, row(w["s5_d"]), row(w["s5_b_glu"]), mod[1, 1:2, 2, :], row(w["final_g"]),
        w["s5_w_glu"], w["s5_w_out"])
    loss = jnp.sum(lvec)
    gw = {"final_g": d_fg.reshape(D), "s5_b_glu": d_bg.reshape(D), "s5_d": d_d.reshape(D),
          "s5_w_glu": gw_glu, "s5_w_out": gw_out}
    dmod = {}

    du_p, s5_g = s5_backward(d_yssm, d_u_act, p1, Lc, dirs, s5_saved)
    d_disc = tuple(tuple(s5_g[k][j - 1].reshape(disc[j][k].shape) if j >= 2 else
                         s5_g[k][0][j].reshape(disc[j][k].shape) for k in range(2)) for j in range(4))
    gw["s5_a_re"], gw["s5_a_im"], gw["s5_log_step"], d_bt_re, d_bt_im = disc_vjp(d_disc)
    gw["s5_b_re"], gw["s5_b_im"] = jnp.swapaxes(d_bt_re, -1, -2), jnp.swapaxes(d_bt_im, -1, -2)
    gw["s5_c_re"] = jnp.stack([s5_g[0][3], s5_g[1][3]])
    gw["s5_c_im"] = jnp.stack([s5_g[0][4], s5_g[1][4]])
    gw["s5_w_in"] = mm_tn(H1, du_p, name="l1_in_dw", b_more=d_z1)
    if reducer is not None:
        g1 = g1 + reducer["l1"][0]({n: gw.pop(n) for n in LAYER1_MATS})[0, 0]
    d_X1, d_g1, d_sc1, d_sh1 = norm_proj_bwd([(du_p, 0, 0), (d_z1, D, Lc)], w["s5_w_in"], X1p, g1, mod_of(1, 1),
                                             mod_of(1, 0), dX2, Lc, "l1_norm_in_bwd", dx_from_segments=True)
    d_gt1_full = jnp.concatenate([jnp.zeros((1, 1, D), F32), d_gt1[None]], axis=0)
    dmod[1] = (d_sh1, d_sc1, d_gt1_full)

    d_o, d_z0, d_gt0, gw["mla_w_out"] = mla_post_bwd(d_X1, out0, og, o, p0, mod_of(0, 2), w["mla_w_out"], Lc)
    if reducer is not None:
        started = reducer["l1"][1](d_o)[0, 0] + reducer["out"][0]({"mla_w_out": gw.pop("mla_w_out")})[0, 0]
        tabs = (tabs[0] + started,) + tabs[1:]
    d_q, dos = attn_bwd_dq(qb, kb, vb, o, d_o, lse, tabs, Lc)
    dk_p, d_v = attn_bwd_dkv(qs, kb, vb, dos, Lc)
    if reducer is not None:
        tabs = (tabs[0] + reducer["out"][1](d_q)[0, 0],) + tabs[1:]
    d_k, d_kr = heads_unrope(dk_p, tabs, jnp.pad(spread, ((0, HEAD_LANES - QK_ROPE_DIM), (0, 0))),
                             name="l0_k_unrope")
    d_qn = mm_nt(d_q, w_uq_p, name="l0_uq_dx")
    gw["mla_w_uq"] = unpad_heads(mm_tn(qn, d_q, name="l0_uq_dw"), QK_DIM)
    d_kvn = mm_nt(d_k, w_kn_p, name="l0_ukn_dx") + mm_nt(d_v, w_v, name="l0_uv_dx")
    dw_kn = unpad_heads(mm_tn(kvn, d_k, name="l0_ukn_dw"), QK_NOPE_DIM).reshape(KV_LORA_RANK, MLA_HEADS, QK_NOPE_DIM)
    dw_v = mm_tn(kvn, d_v, name="l0_uv_dw").reshape(KV_LORA_RANK, MLA_HEADS, V_HEAD_DIM)
    gw["mla_w_ukv"] = jnp.concatenate([dw_kn, dw_v], axis=-1).reshape(KV_LORA_RANK, -1)
    d_cq, d_qng = rowwise_bwd(f_rms, [cq], [qng], [d_qn], [0], [0], T, 0, "l0_qnorm_bwd")
    d_ckv, d_kvng = rowwise_bwd(f_rms, [ckv], [kvng], [d_kvn], [0], [0], T, 0, "l0_kvnorm_bwd")
    gw["mla_q_norm"] = d_qng.reshape(-1)
    gw["mla_kv_norm"] = d_kvng.reshape(-1)
    o_cq, o_ckv = D, D + Q_LORA_RANK
    o_kr = o_ckv + KV_LORA_RANK
    d_head = jnp.concatenate([d_cq, d_ckv, d_kr], axis=1)
    gw["mla_w_in"] = jnp.concatenate([mm_tn(H0, d_head, name="l0_in_dw_head")[:, :P0_HEAD],
                                      mm_tn(H0, d_z0, name="l0_in_dw_z")], axis=1)
    dx, d_g0, d_sc0, d_sh0 = norm_proj_bwd(
        [(d_z0, 0, 0), (d_cq, o_cq, 0), (d_ckv, o_ckv, 0), (d_kr, o_kr, 0)], w["mla_w_in"], X0, g0, mod_of(0, 1),
        mod_of(0, 0), d_X1, Lc, "l0_norm_in_bwd", latent_dx_only=True)
    dmod[0] = (d_sh0, d_sc0, d_gt0)
    gw["norm_g"] = jnp.stack([d_g0.reshape(D), d_g1.reshape(D)])
    dmod_arr = jnp.stack([jnp.stack([dmod[i][j][:, 0, :] for j in range(3)], axis=1) for i in range(2)])
    ready = {**reducer["l1"][2](dx), **reducer["out"][2](dx)} if reducer is not None else {}
    return loss, dx, dmod_arr, gw, ready


SHARDED = {
    "mla_w_in": 1, "mla_w_uq": 1, "mla_w_ukv": 1, "mla_w_out": 0,
    "s5_w_in": 1, "s5_w_glu": 0, "s5_w_out": 0, "s5_d": 0, "s5_b_glu": 0,
}
SHARDED_MATS = ["mla_w_in", "mla_w_uq", "mla_w_ukv", "mla_w_out", "s5_w_in", "s5_w_glu", "s5_w_out"]
SHARDED_VECS = ["s5_d", "s5_b_glu"]
REPLICATED = ["norm_g", "mla_q_norm", "mla_kv_norm", "s5_a_re", "s5_a_im", "s5_log_step", "s5_b_re", "s5_b_im",
              "s5_c_re", "s5_c_im", "final_g"]
WEIGHT_ORDER = ["c_ctx", "ada_w", "ada_b", "norm_g", "mla_w_in", "mla_q_norm", "mla_w_uq", "mla_kv_norm", "mla_w_ukv",
                "mla_w_out", "s5_w_in", "s5_a_re", "s5_a_im", "s5_log_step", "s5_b_re", "s5_b_im", "s5_c_re", "s5_c_im",
                "s5_d", "s5_w_glu", "s5_b_glu", "s5_w_out", "final_g"]


P0_HEAD = Q_LORA_RANK + KV_LORA_RANK + QK_ROPE_DIM


P0_WIDTH = 1536


def w_in_to_kernel_order(w):
    pad = jnp.zeros((w.shape[0], P0_WIDTH - w.shape[1]), w.dtype)
    return jnp.concatenate([w[:, P0_HEAD:], w[:, :P0_HEAD], pad], axis=1)


LAYER0_MATS = ["mla_w_in", "mla_w_uq", "mla_w_ukv", "mla_w_out"]
LAYER1_MATS = ["s5_w_in", "s5_w_glu", "s5_w_out"]


def _whole_matrices(names, own_blocks, gathered):
    chip = _chip_index(_coords())
    full = {}
    for n, own, o in zip(names, own_blocks, gathered):
        slot = lax.broadcasted_iota(jnp.int32, (N_CHIP, 1, 1), 0)
        o = jnp.where(slot == chip, own[None], o.reshape((N_CHIP,) + own.shape))
        full[n] = o.reshape(-1, o.shape[-1]) if SHARDED[n] == 0 else o.transpose(1, 0, 2).reshape(o.shape[1], -1)
    return full


FIRST_MATS = ["mla_w_in"]
LATER_GROUPS = {"qkv": ["mla_w_uq", "mla_w_ukv"], "out": ["mla_w_out"], "l1": LAYER1_MATS}


def gather_weights(ws):
    mats = [ws[n].astype(BF16) for n in FIRST_MATS]
    full = _whole_matrices(FIRST_MATS, mats, gather_halves(mats, "gather_weights"))
    full["mla_w_in"] = w_in_to_kernel_order(full["mla_w_in"])
    return full


def gather_weights_behind(ws, after):
    token, finish = 0.0, {}
    for group, names in LATER_GROUPS.items():
        mats = [ws[n].astype(BF16) for n in names]
        flight, tok = exchange_start(
            mats, [jax.ShapeDtypeStruct((N_CHIP,) + m.shape, m.dtype) for m in mats],
            [(f, a, lambda me, peer: None, a, lambda s: (_chip_index(s),))
             for a in range(len(mats)) for f in CHIP_FLIPS], f"gather_{group}_start", after=after)
        after = [tok]
        token = token + tok[0, 0]

        def finish_group(after_work, group=group, names=names, flight=flight):
            own, got = exchange_wait(flight, after_work, f"gather_{group}_wait")
            return _whole_matrices(names, own, got)

        finish[group] = finish_group
    return token, finish


def _grad_slots(gw, names):
    slots = []
    for n in names:
        g = gw[n]
        if SHARDED[n] == 0:
            slots.append(g.reshape(N_CHIP, 2, g.shape[0] // (2 * N_CHIP), g.shape[1]))
        else:
            k, n4 = g.shape
            slots.append(g.reshape(k, N_CHIP, n4 // N_CHIP).transpose(1, 0, 2)
                         .reshape(N_CHIP, 2, k // 2, n4 // N_CHIP))
    return slots


def _to_sibling_half(count):
    return [(CORE_FLIP, i, lambda me, peer: (slice(None), 1 - me[2]), i, lambda s: None) for i in range(count)]


def _to_chips(count):
    return [(f, i, lambda me, peer: (_chip_index(peer),), i, lambda s: (_chip_index(s),))
            for i in range(count) for f in CHIP_FLIPS]


def _place():
    me = _coords()
    return jnp.stack([me[2], _chip_index(me)]).astype(jnp.int32)


def reduce_behind(names, tag):
    state = {}
    count = len(names)

    def begin(gw):
        slots = _grad_slots(gw, names)
        lands = [jax.ShapeDtypeStruct((N_CHIP,) + s.shape[2:], F32) for s in slots]
        state["in"], token = exchange_start(slots, lands, _to_sibling_half(count), f"grads_{tag}_swap_in_start")
        return token

    def middle(after):
        slots, got = exchange_wait(state["in"], after, f"grads_{tag}_swap_in_wait")
        place = _place()
        sums = [pair_add(s, g, place[:1], BF16, f"grads_pair_{n}") for n, s, g in zip(names, slots, got)]
        lands = [jax.ShapeDtypeStruct(s.shape, s.dtype) for s in sums]
        state["out"], token = exchange_start(sums, lands, _to_chips(count), f"grads_{tag}_scatter_start")
        return token

    def end(after):
        sums, parts = exchange_wait(state["out"], after, f"grads_{tag}_scatter_wait")
        place = _place()
        return {n: sum_chips(p, s, place, f"grads_sum_{n}") for n, p, s in zip(names, parts, sums)}

    return begin, middle, end


def reduce_gradients(gw, ready_halves, also=None):
    me = _coords()
    place = _place()
    mat_names = [n for n in SHARDED_MATS if n not in ready_halves]
    slots = dict(zip(mat_names, _grad_slots(gw, mat_names)))
    gw = {**gw, **(also or {})}
    small_names = REPLICATED + SHARDED_VECS + list(also or {})
    small, small_offs = pack_flat([_stored(gw[n]).astype(F32) for n in small_names], F32)
    small = jnp.pad(small, ((0, (-small.shape[0]) % (N_CHIP * 32)), (0, 0)))
    slots["small"] = small.reshape(N_CHIP, 2, -1, 128)
    names = list(slots)
    count = len(names)
    got = exchange([slots[n] for n in names],
                   [jax.ShapeDtypeStruct((N_CHIP,) + slots[n].shape[2:], F32) for n in names],
                   _to_sibling_half(count), [], "grads_swap_in")
    sums = [pair_add(slots[n], g, place[:1], F32 if n == "small" else BF16, f"grads_pair_{n}")
            for n, g in zip(names, got)]
    parts = exchange(sums, [jax.ShapeDtypeStruct(s.shape, s.dtype) for s in sums], _to_chips(count), [],
                     "grads_scatter")
    halves = {n: sum_chips(p, s, place, f"grads_sum_{n}") for n, p, s in zip(names, parts, sums)}
    halves.update(ready_halves)
    all_names = list(halves)
    fulls = exchange(
        [halves[n] for n in all_names], [jax.ShapeDtypeStruct(halves[n].shape, F32) for n in all_names],
        [(CORE_FLIP, i, lambda me, peer: (me[2],), i, lambda s: (s[2],)) for i in range(len(all_names))], [],
        "grads_swap_out", aliases={i: i for i in range(len(all_names))})
    out = {n: f.reshape(-1, f.shape[-1]) for n, f in zip(all_names, fulls)}
    quarter = out.pop("small")
    gather, token = exchange_start(
        [quarter], [jax.ShapeDtypeStruct((N_CHIP,) + quarter.shape, F32)],
        [(f, 0, lambda me, peer: None, 0, lambda s: (_chip_index(s),)) for f in CHIP_FLIPS],
        "grads_gather_small_start")

    def finish_small(after):
        (own,), (got_small,) = exchange_wait(gather, after, "grads_gather_small_wait")
        slot = lax.broadcasted_iota(jnp.int32, (N_CHIP, 1, 1), 0)
        small_all = jnp.where(slot == _chip_index(me), own[None], got_small)
        vals = unpack_flat(small_all, small_offs, [_stored_shape(gw[n].shape) for n in small_names])
        res = {}
        for n, v in zip(small_names, vals):
            v = _from_stored(v, gw[n].shape)
            if n in SHARDED_VECS:
                size = v.shape[0] // N_CHIP
                v = lax.dynamic_slice_in_dim(v, _chip_index(me) * size, size)
            res[n] = v
        return res

    return out, finish_small, token


def kernel(x, c, ctx, c_ctx, ada_w, ada_b, norm_g, mla_w_in, mla_q_norm, mla_w_uq, mla_kv_norm, mla_w_ukv, mla_w_out, s5_w_in, s5_a_re, s5_a_im, s5_log_step, s5_b_re, s5_b_im, s5_c_re, s5_c_im, s5_d, s5_w_glu, s5_b_glu, s5_w_out, final_g, loss_target, m_c_ctx, m_ada_w, m_ada_b, m_norm_g, m_mla_w_in, m_mla_q_norm, m_mla_w_uq, m_mla_kv_norm, m_mla_w_ukv, m_mla_w_out, m_s5_w_in, m_s5_a_re, m_s5_a_im, m_s5_log_step, m_s5_b_re, m_s5_b_im, m_s5_c_re, m_s5_c_im, m_s5_d, m_s5_w_glu, m_s5_b_glu, m_s5_w_out, m_final_g, v_c_ctx, v_ada_w, v_ada_b, v_norm_g, v_mla_w_in, v_mla_q_norm, v_mla_w_uq, v_mla_kv_norm, v_mla_w_ukv, v_mla_w_out, v_s5_w_in, v_s5_a_re, v_s5_a_im, v_s5_log_step, v_s5_b_re, v_s5_b_im, v_s5_c_re, v_s5_c_im, v_s5_d, v_s5_w_glu, v_s5_b_glu, v_s5_w_out, v_final_g):
    args = dict(locals())
    weights = {n: args[n] for n in WEIGHT_ORDER}
    D = D_MODEL
    xi, yi, ci = _coords()
    chip = 2 * xi + yi
    me = 4 * xi + 2 * yi + ci
    n_col = ada_w.shape[2]

    c_all = allgather_devices(jnp.pad(c, ((0, 7), (0, 0))), "gather_c")[:, 0, :]
    cond = jnp.concatenate([c_all, jnp.broadcast_to(c_ctx[None], (8, D))], axis=0)
    (s_cond,) = rowwise_fwd(lambda v: (_silu(v),), [cond], [], [D], [F32], 16, 0, "cond_silu")
    ada_rows = ada_w.reshape(2 * D, n_col)
    mod_cols = jnp.stack([mm_nn(s_cond, ada_rows, name=f"mod_proj{i}", b_blk=i) for i in range(2)])
    vec_tiles = [jnp.pad(weights[n][0].reshape(-1, 128), ((0, 6), (0, 0))) for n in SHARDED_VECS]
    mod_all, *vec_all = allgather_chips([mod_cols] + vec_tiles, "gather_mod")
    mod_all = mod_all.transpose(1, 2, 0, 3).reshape(2, 16, 3 * D) + ada_b[:, None, :]
    mine = lax.broadcasted_iota(jnp.int32, (1, 16, 1), 1) == me
    mod_l = jnp.sum(jnp.where(mine, mod_all, 0.0), axis=1)
    mod_c = mod_all[:, 8, :]
    mod = jnp.stack([mod_c.reshape(2, 3, D), mod_l.reshape(2, 3, D)], axis=1)

    w = gather_weights({n: weights[n][0] for n in FIRST_MATS})
    token, late = gather_weights_behind({n: weights[n][0] for n in SHARDED_MATS}, [w["mla_w_in"], mod])
    for n, v in zip(SHARDED_VECS, vec_all):
        w[n] = v[:, :2, :].reshape(-1)
    for n in ["norm_g", "final_g"]:
        w[n] = weights[n]
    for n in ["mla_q_norm", "mla_kv_norm", "s5_a_re", "s5_a_im", "s5_log_step", "s5_b_re", "s5_b_im",
              "s5_c_re", "s5_c_im"]:
        w[n] = weights[n][0]

    reducer = {"l1": reduce_behind(LAYER1_MATS, "l1"), "out": reduce_behind(["mla_w_out"], "out")}
    loss_me, dx, dmod, gw, ready = local_step(x[0], ctx[0], loss_target[0], mod + token, w, late,
                                              reducer)

    dmod_rows, loss_all = _gather([dmod.reshape(2, 2, 3 * D), jnp.broadcast_to(loss_me, (8, 128))],
                                  ALL_FLIPS, _dev_index, N_DEV, "gather_dmod")
    loss = functools.reduce(lambda s, d: s + loss_all[d, 0, 0], range(1, N_DEV), loss_all[0, 0, 0])
    dm = jnp.concatenate([dmod_rows[:, :, 1, :], dmod_rows[:, :, 0, :]], axis=0).transpose(1, 0, 2)
    g_ada_b = jnp.sum(dm, axis=1)
    dm_cols = lax.dynamic_slice_in_dim(dm, chip * n_col, n_col, axis=2)
    g_ada_w = jnp.stack([mm_tn(s_cond, dm_cols[i], name=f"mod_proj_dw{i}") for i in range(2)])
    dmc = jnp.sum(dm_cols[:, 8:, :], axis=1)
    dmc8 = jnp.broadcast_to(dmc[:, None, :], (2, 8, n_col))
    g_sc = (mm_nt(dmc8[0], ada_rows, name="mod_proj_dx0", b_rows=D, b_blk=0)[0]
            + mm_nt(dmc8[1], ada_rows, name="mod_proj_dx1", b_rows=D, b_blk=1)[0])
    g_silu_part = jnp.where(ci == 0, g_sc, 0.0)

    grads = {"ada_w": g_ada_w, "ada_b": g_ada_b}
    deltas, new_m, new_v = {}, {}, {}
    small = [n for n in WEIGHT_ORDER if weights[n].size < 50000]

    def update(n, after=None):
        shp = weights[n].shape
        rows = lambda a: _stored(a.reshape(shp)).reshape(-1, _stored_shape(shp)[-1])
        back = lambda a: _from_stored(a.reshape(_stored_shape(shp)), shp)
        d_, m_, v_ = adamw(rows(weights[n]), rows(grads[n]), rows(args["m_" + n]), rows(args["v_" + n]),
                           name=f"adamw_{n}", after=after)
        deltas[n], new_m[n], new_v[n] = back(d_), back(m_), back(v_)

    red, finish_small, small_started = reduce_gradients(gw, ready, {"silu_c_ctx": g_silu_part})
    update("ada_w", small_started)
    for n in SHARDED_MATS:
        grads[n] = red[n].reshape(weights[n].shape)
        update(n, small_started)
    red_small = finish_small([deltas[n] for n in ["ada_w"] + SHARDED_MATS])
    for n in REPLICATED + SHARDED_VECS:
        grads[n] = red_small[n].reshape(weights[n].shape)
    (g_c_ctx,) = rowwise_bwd(lambda v: (_silu(v),), [jnp.broadcast_to(c_ctx[None], (8, D))], [],
                             [jnp.broadcast_to(red_small["silu_c_ctx"][None], (8, D))], [0], [], 8, 0, "cond_silu_bwd")
    grads["c_ctx"] = g_c_ctx[0]
    for n in WEIGHT_ORDER:
        if n not in small and n not in deltas:
            update(n)
    packs = []
    offs = None
    for src in (weights, grads, {n: args["m_" + n] for n in small}, {n: args["v_" + n] for n in small}):
        buf, offs = pack_flat([src[n] for n in small], F32)
        packs.append(buf)
    outs = adamw(*packs, name="adamw_small")
    for res, dst in zip(outs, (deltas, new_m, new_v)):
        for n, val in zip(small, unpack_flat(res, offs, [weights[n].shape for n in small])):
            dst[n] = val

    return (loss, dx[None], *[grads[n] for n in WEIGHT_ORDER], *[deltas[n] for n in WEIGHT_ORDER],
            *[new_m[n] for n in WEIGHT_ORDER], *[new_v[n] for n in WEIGHT_ORDER])
```

```python
import functools
import math

import jax
import jax.numpy as jnp
import numpy as np
from jax import lax
from jax.experimental import pallas as pl
from jax.experimental.pallas import tpu as pltpu

F32 = jnp.float32
BF16 = jnp.bfloat16

D_MODEL = 1024
GRID_W = 64
EPS = 1e-6
MLA_HEADS = 16
QK_NOPE_DIM = 64
QK_ROPE_DIM = 32
V_HEAD_DIM = 64
Q_LORA_RANK = 256
KV_LORA_RANK = 128
QK_DIM = QK_NOPE_DIM + QK_ROPE_DIM
SOFTMAX_SCALE = QK_DIM ** -0.5
ROPE_THETA = 10000.0
S5_GROUP = 16
S5_GROUPS = D_MODEL // S5_GROUP
S5_STATE = 64
S5_LANES = S5_GROUPS * S5_STATE
N_SEG = 8
GROUPS_PER_BLOCK = 8
N_BLOCKS = S5_GROUPS // GROUPS_PER_BLOCK
BLK_CH = GROUPS_PER_BLOCK * S5_GROUP
BLK_ST = GROUPS_PER_BLOCK * S5_STATE

ADAM_LR = 0.001
ADAM_B1 = 0.9
ADAM_B2 = 0.999
ADAM_EPS = 1e-08
ADAM_WD = 0.01
ADAM_STEP = 10

N_DEV = 8
N_CHIP = 4
MESH = pl.DeviceIdType.MESH
VMEM_LIMIT = 52 * 1024 * 1024
ROW_TILE = 256


def _params(sem=None, vmem=None):
    return pltpu.CompilerParams(dimension_semantics=sem, vmem_limit_bytes=vmem)


def mm_nn(a, b, out_dtype=F32, name="mm_nn", b_blk=0):
    M, K = a.shape
    N = b.shape[1]
    tm = math.gcd(ROW_TILE, M)

    def body(a_ref, b_ref, o_ref):
        o_ref[...] = jnp.dot(a_ref[...].astype(BF16), b_ref[...].astype(BF16),
                             preferred_element_type=F32).astype(o_ref.dtype)

    return pl.pallas_call(
        body, out_shape=jax.ShapeDtypeStruct((M, N), out_dtype), grid=(M // tm,),
        in_specs=[pl.BlockSpec((tm, K), lambda i: (i, 0)), pl.BlockSpec((K, N), lambda i: (b_blk, 0))],
        out_specs=pl.BlockSpec((tm, N), lambda i: (i, 0)),
        compiler_params=_params(("parallel",), VMEM_LIMIT), name=name)(a, b)


def mm_nt(a, b, out_dtype=F32, name="mm_nt", b_rows=None, b_blk=0):
    M, N = a.shape
    K = b.shape[0] if b_rows is None else b_rows
    tm = math.gcd(ROW_TILE, M)

    def body(a_ref, b_ref, o_ref):
        o_ref[...] = lax.dot_general(a_ref[...].astype(BF16), b_ref[...].astype(BF16),
                                     (((1,), (1,)), ((), ())),
                                     preferred_element_type=F32).astype(o_ref.dtype)

    return pl.pallas_call(
        body, out_shape=jax.ShapeDtypeStruct((M, K), out_dtype), grid=(M // tm,),
        in_specs=[pl.BlockSpec((tm, N), lambda i: (i, 0)), pl.BlockSpec((K, N), lambda i: (b_blk, 0))],
        out_specs=pl.BlockSpec((tm, K), lambda i: (i, 0)),
        compiler_params=_params(("parallel",), VMEM_LIMIT), name=name)(a, b)


def mm_tn(a, b, name="mm_tn", b_more=None):
    M, N = b.shape
    K = a.shape[1]
    tn = math.gcd(512, N) if N % 128 == 0 and N > 512 else N
    bs = [b] if b_more is None else [b, b_more]
    assert all(x.shape[1] == N for x in bs)
    nb = N // tn

    def body(a_ref, *refs):
        o_ref = refs[-1]
        for k, b_ref in enumerate(refs[:-1]):
            def product(b_ref=b_ref):
                o_ref[...] = lax.dot_general(a_ref[a.shape[0] - b_ref.shape[0]:, :].astype(BF16),
                                             b_ref[...].astype(BF16), (((0,), (0,)), ((), ())),
                                             preferred_element_type=F32)
            if len(bs) == 1:
                product()
            else:
                pl.when(pl.program_id(0) // nb == k)(product)

    def b_spec(k, rows):
        return pl.BlockSpec((rows, tn), lambda j: (0, jnp.clip(j - k * nb, 0, nb - 1)))

    return pl.pallas_call(
        body, out_shape=jax.ShapeDtypeStruct((K, N * len(bs)), F32), grid=(nb * len(bs),),
        in_specs=[pl.BlockSpec(a.shape, lambda j: (0, 0))] + [b_spec(k, x.shape[0]) for k, x in enumerate(bs)],
        out_specs=pl.BlockSpec((K, tn), lambda j: (0, j)),
        compiler_params=_params(("parallel",), VMEM_LIMIT), name=name)(a, *bs)


class Rows:
    def __init__(self, arr, width=None, row_off=0, col_blk=0):
        self.arr = arr
        self.width = arr.shape[1] if width is None else width
        self.row_off = row_off
        self.col_blk = col_blk

    def spec(self, tm):
        ro, cb = self.row_off // tm, self.col_blk
        return pl.BlockSpec((tm, self.width), lambda i: (i + ro, cb))


def _as_rows(x):
    return x if isinstance(x, Rows) else Rows(x)


def _row_tile(n_rows, n_ctx_rows, rows):
    tm = math.gcd(ROW_TILE, n_rows, n_ctx_rows)
    for r in rows:
        tm = math.gcd(tm, r.row_off)
    return tm


def _bc_spec(arr, n_ctx_blocks):
    g, _, d = arr.shape
    if g == 1:
        return pl.BlockSpec((1, 1, d), lambda i: (0, 0, 0))
    return pl.BlockSpec((1, 1, d), lambda i: ((i >= n_ctx_blocks).astype(jnp.int32), 0, 0))


def rowwise_fwd(fn, rows, bcs, out_dims, out_dtypes, n_rows, n_ctx_rows, name):
    rows = [_as_rows(r) for r in rows]
    tm = _row_tile(n_rows, n_ctx_rows, rows)
    ncb = n_ctx_rows // tm
    nr, nb = len(rows), len(bcs)

    def body(*refs):
        vals = [r[...].astype(F32) for r in refs[:nr]] + [b[0].astype(F32) for b in refs[nr:nr + nb]]
        outs = fn(*vals)
        for o_ref, v in zip(refs[nr + nb:], outs):
            o_ref[...] = v.astype(o_ref.dtype)

    outs = pl.pallas_call(
        body,
        out_shape=[jax.ShapeDtypeStruct((n_rows, d), dt) for d, dt in zip(out_dims, out_dtypes)],
        grid=(n_rows // tm,),
        in_specs=[r.spec(tm) for r in rows] + [_bc_spec(b, ncb) for b in bcs],
        out_specs=[pl.BlockSpec((tm, d), lambda i: (i, 0)) for d in out_dims],
        compiler_params=_params(("parallel",), VMEM_LIMIT), name=name)(*[r.arr for r in rows], *bcs)
    return outs


def rowwise_bwd(fn, rows, bcs, cts, diff_rows, diff_bcs, n_rows, n_ctx_rows, name, ct_extra=None, lat_add=None):
    rows = [_as_rows(r) for r in rows]
    cts = [_as_rows(c) for c in cts]
    extra = [_as_rows(ct_extra)] if ct_extra is not None else []
    tm = _row_tile(n_rows, n_ctx_rows, rows + cts + extra)
    ncb = n_ctx_rows // tm
    nr, nb, nc = len(rows), len(bcs), len(cts)
    ndr, ndb = len(diff_rows), len(diff_bcs)
    n_in = nr + nb + nc + len(extra) + (lat_add is not None)

    def body(*refs):
        i = pl.program_id(0)
        rvals = [r[...].astype(F32) for r in refs[:nr]]
        bvals = [b[0].astype(F32) for b in refs[nr:nr + nb]]
        cvals = [c[...].astype(F32) for c in refs[nr + nb:nr + nb + nc]]
        if extra:
            cvals[0] = cvals[0] + refs[nr + nb + nc][...].astype(F32)
        outs = refs[n_in:]

        def f(*d):
            rv, bv = list(rvals), list(bvals)
            for k, idx in enumerate(diff_rows):
                rv[idx] = d[k]
            for k, idx in enumerate(diff_bcs):
                bv[idx] = d[ndr + k]
            return tuple(fn(*rv, *bv))

        primals = [rvals[k] for k in diff_rows] + [bvals[k] for k in diff_bcs]
        _, vjp = jax.vjp(f, *primals)
        grads = list(vjp(tuple(cvals)))
        if lat_add is not None:
            add = refs[n_in - 1][...]
            grads[0] = grads[0] + (add if lat_add.shape[0] == n_rows else jnp.where(i >= ncb, add, 0.0))
        for k in range(ndr):
            outs[k][...] = grads[k].astype(outs[k].dtype)
        for k, idx in enumerate(diff_bcs):
            o_ref = outs[ndr + k]
            first = (i == 0)
            if bcs[idx].shape[0] == 2:
                first = first | (i == ncb)

            @pl.when(first)
            def _(o_ref=o_ref):
                o_ref[...] = jnp.zeros_like(o_ref)

            o_ref[0] += grads[ndr + k]

    out_shape = [jax.ShapeDtypeStruct((n_rows, rows[k].width), F32) for k in diff_rows]
    out_shape += [jax.ShapeDtypeStruct(bcs[k].shape, F32) for k in diff_bcs]
    out_specs = [pl.BlockSpec((tm, rows[k].width), lambda i: (i, 0)) for k in diff_rows]
    out_specs += [_bc_spec(bcs[k], ncb) for k in diff_bcs]
    ins = [r.arr for r in rows] + list(bcs) + [c.arr for c in cts + extra]
    in_specs = [r.spec(tm) for r in rows] + [_bc_spec(b, ncb) for b in bcs] + [c.spec(tm) for c in cts + extra]
    if lat_add is not None:
        ins.append(lat_add)
        skip = ncb if lat_add.shape[0] != n_rows else 0
        in_specs.append(pl.BlockSpec((tm, lat_add.shape[1]), lambda i: (jnp.maximum(i - skip, 0), 0)))
    outs = pl.pallas_call(
        body, out_shape=out_shape, grid=(n_rows // tm,), in_specs=in_specs, out_specs=out_specs,
        compiler_params=_params(("arbitrary",), VMEM_LIMIT), name=name)(*ins)
    return outs


def _rms(x):
    return x * lax.rsqrt(jnp.mean(x * x, axis=-1, keepdims=True) + EPS)


def _sigmoid(x):
    return 0.5 * (jnp.tanh(0.5 * x) + 1.0)


def _silu(x):
    return x * _sigmoid(x)


def _gelu_tanh(x):
    return 0.5 * x * (1.0 + jnp.tanh(math.sqrt(2.0 / math.pi) * (x + 0.044715 * (x * x * x))))


def f_norm_mod(x, g, sc, sh):
    return ((_rms(x) * g) * (1.0 + sc) + sh,)


def f_rms(x, g):
    return (_rms(x) * g,)


def f_gate(o, z):
    return (o * _silu(z),)


def f_s5_act(y, u, d):
    return (_gelu_tanh(y + d * u),)


def f_s5_glu(ya, gl, z, b):
    return (ya * _sigmoid(gl + b) * _silu(z),)


def _as_parts(x, n_ctx):
    xs = x if isinstance(x, tuple) else (x,)
    assert len(xs) == 1 or xs[0].shape[0] == n_ctx
    return xs, sum(p.shape[0] for p in xs)


def _parts_specs(xs, tm, ncb):
    d = xs[0].shape[1]
    if len(xs) == 1:
        return [pl.BlockSpec((tm, d), lambda i: (i, 0))]
    return [pl.BlockSpec((tm, d), lambda i: (jnp.minimum(i, ncb - 1), 0)),
            pl.BlockSpec((tm, d), lambda i: (jnp.maximum(i - ncb, 0), 0))]


def _parts_tile(x_refs, ncb):
    if len(x_refs) == 1:
        return x_refs[0][...]
    return jnp.where(pl.program_id(0) < ncb, x_refs[0][...], x_refs[1][...])


def norm_proj(x, g, sc, sh, w, n_ctx, name):
    xs, n = _as_parts(x, n_ctx)
    d = xs[0].shape[1]
    nw = w.shape[1]
    tm = math.gcd(ROW_TILE, n, n_ctx)
    ncb = n_ctx // tm
    nx = len(xs)

    def body(*refs):
        g_ref, sc_ref, sh_ref, w_ref, h_ref, p_ref = refs[nx:]
        h = f_norm_mod(_parts_tile(refs[:nx], ncb), g_ref[0], sc_ref[0], sh_ref[0])[0].astype(BF16)
        h_ref[...] = h
        p_ref[...] = jnp.dot(h, w_ref[...], preferred_element_type=F32)

    row = pl.BlockSpec((tm, d), lambda i: (i, 0))
    return pl.pallas_call(
        body, out_shape=[jax.ShapeDtypeStruct((n, d), BF16), jax.ShapeDtypeStruct((n, nw), F32)], grid=(n // tm,),
        in_specs=_parts_specs(xs, tm, ncb) + [_bc_spec(g, ncb), _bc_spec(sc, ncb), _bc_spec(sh, ncb),
                                              pl.BlockSpec(w.shape, lambda i: (0, 0))],
        out_specs=[row, pl.BlockSpec((tm, nw), lambda i: (i, 0))],
        compiler_params=_params(("parallel",), VMEM_LIMIT), name=name)(*xs, g, sc, sh, w)


def norm_proj_bwd(terms, w, x, g, sc, sh, add, n_ctx, name, latent_dx_only=False, dx_from_segments=False):
    xs, n = _as_parts(x, n_ctx)
    adds = add if isinstance(add, tuple) else (add,)
    d = xs[0].shape[1]
    tm = math.gcd(ROW_TILE, n, n_ctx, *[t[2] for t in terms])
    ncb = n_ctx // tm
    nt, nx, na = len(terms), len(xs), len(adds)
    add_skip = ncb if na == 1 and add.shape[0] != n else 0
    n_dx = 2 if dx_from_segments else 1
    tj = tm // N_SEG
    assert not dx_from_segments or (ncb == 1 and not latent_dx_only)

    def body(*refs):
        i = pl.program_id(0)
        a_refs = refs[:nt]
        w_ref, x_refs = refs[nt], refs[nt + 1:nt + 1 + nx]
        g_ref, sc_ref, sh_ref = refs[nt + 1 + nx:nt + 4 + nx]
        add_refs = refs[nt + 4 + nx:nt + 4 + nx + na]
        outs = refs[nt + 4 + nx + na:]
        dx_refs, (dg_ref, dsc_ref, dsh_ref) = outs[:n_dx], outs[n_dx:n_dx + 3]
        d_h = None
        for a_ref, (a, off, first) in zip(a_refs, terms):
            part = lax.dot_general(a_ref[...].astype(BF16), w_ref[:, off:off + a.shape[1]], NT_DIMS,
                                   preferred_element_type=F32)
            if first:
                part = jnp.where(i >= first // tm, part, 0.0)
            d_h = part if d_h is None else d_h + part
        _, vjp = jax.vjp(lambda x_, g_, sc_, sh_: f_norm_mod(x_, g_, sc_, sh_), _parts_tile(x_refs, ncb), g_ref[0],
                         sc_ref[0], sh_ref[0])
        d_x, d_g, d_sc, d_sh = vjp((d_h,))
        extra = _parts_tile(add_refs, ncb)
        d_x = d_x + (extra if add_skip == 0 else jnp.where(i >= ncb, extra, 0.0))
        if dx_from_segments:
            slabs = outs[n_dx + 3]
            for c in range(d // HEAD_LANES):
                slabs[c] = d_x[:, c * HEAD_LANES:(c + 1) * HEAD_LANES]
            for k, here in enumerate([i < ncb, i >= ncb]):
                @pl.when(here)
                def _(k=k):
                    for c in range(d // HEAD_LANES):
                        for seg in range(N_SEG):
                            dx_refs[k][seg, :, c * HEAD_LANES:(c + 1) * HEAD_LANES] = (
                                slabs[c, pl.ds(seg, tj, stride=N_SEG), :])
        else:
            dx_refs[0][...] = d_x

        @pl.when(i == 0)
        def _():
            dg_ref[...] = jnp.zeros_like(dg_ref)

        @pl.when((i == 0) | (i == ncb))
        def _():
            dsc_ref[...] = jnp.zeros_like(dsc_ref)
            dsh_ref[...] = jnp.zeros_like(dsh_ref)

        dg_ref[0] += d_g
        dsc_ref[0] += d_sc
        dsh_ref[0] += d_sh

    def a_spec(a, first):
        skip = first // tm
        return pl.BlockSpec((tm, a.shape[1]), lambda i: (jnp.maximum(i - skip, 0), 0))

    dx_skip = ncb if latent_dx_only else 0
    if dx_from_segments:
        dx_shapes = [jax.ShapeDtypeStruct((N_SEG, n_ctx // N_SEG, d), F32),
                     jax.ShapeDtypeStruct((N_SEG, (n - n_ctx) // N_SEG, d), F32)]
        dx_specs = [pl.BlockSpec((N_SEG, tj, d), lambda i: (0, 0, 0)),
                    pl.BlockSpec((N_SEG, tj, d), lambda i: (0, jnp.maximum(i - ncb, 0), 0))]
    else:
        dx_shapes = [jax.ShapeDtypeStruct((n - dx_skip * tm, d), F32)]
        dx_specs = [pl.BlockSpec((tm, d), lambda i: (jnp.maximum(i - dx_skip, 0), 0))]
    add_specs = (_parts_specs(adds, tm, ncb) if na == 2 else
                 [pl.BlockSpec((tm, d), lambda i: (jnp.maximum(i - add_skip, 0), 0))])
    res = pl.pallas_call(
        body,
        out_shape=dx_shapes + [jax.ShapeDtypeStruct(g.shape, F32), jax.ShapeDtypeStruct(sc.shape, F32),
                               jax.ShapeDtypeStruct(sh.shape, F32)],
        grid=(n // tm,),
        in_specs=[a_spec(a, first) for a, _, first in terms]
        + [pl.BlockSpec(w.shape, lambda i: (0, 0))] + _parts_specs(xs, tm, ncb)
        + [_bc_spec(g, ncb), _bc_spec(sc, ncb), _bc_spec(sh, ncb)] + add_specs,
        out_specs=dx_specs + [_bc_spec(g, ncb), _bc_spec(sc, ncb), _bc_spec(sh, ncb)],
        scratch_shapes=[pltpu.VMEM((d // HEAD_LANES, tm, HEAD_LANES), F32)] if dx_from_segments else [],
        compiler_params=_params(("arbitrary",), VMEM_LIMIT), name=name)(
            *[t[0] for t in terms], w, *xs, g, sc, sh, *adds)
    if dx_from_segments:
        return ((res[0].reshape(n_ctx, d), res[1].reshape(n - n_ctx, d)), *res[2:])
    return res


def mla_post_fwd(o, p0, x0, gate, w_out, n_ctx, name="l0_post"):
    n, d = o.shape
    xs, _ = _as_parts(x0, n_ctx)
    tm = math.gcd(ROW_TILE, n, n_ctx)
    ncb = n_ctx // tm
    nx = len(xs)

    def body(o_ref, z_ref, *refs):
        gt_ref, w_ref, x1_ref, og_ref, out_ref = refs[nx:]
        og = f_gate(o_ref[...], z_ref[...])[0].astype(BF16)
        out = jnp.dot(og, w_ref[...], preferred_element_type=F32)
        og_ref[...] = og
        out_ref[...] = out
        x1_ref[...] = _parts_tile(refs[:nx], ncb) + gt_ref[0] * out

    row = pl.BlockSpec((tm, d), lambda i: (i, 0))
    return pl.pallas_call(
        body, out_shape=[jax.ShapeDtypeStruct((n, d), F32), jax.ShapeDtypeStruct((n, d), BF16),
                         jax.ShapeDtypeStruct((n, d), F32)],
        grid=(n // tm,),
        in_specs=[row, row] + _parts_specs(xs, tm, ncb) + [_bc_spec(gate, ncb), pl.BlockSpec((d, d), lambda i: (0, 0))],
        out_specs=[row, row, row],
        compiler_params=_params(("parallel",), VMEM_LIMIT), name=name)(o, p0, *xs, gate, w_out)


def mla_post_bwd(dx1, out, og, o, p0, gate, w_out, n_ctx, name="l0_post_bwd"):
    n, d = o.shape
    dxs, _ = _as_parts(dx1, n_ctx)
    tm = math.gcd(ROW_TILE, n, n_ctx)
    ncb = n_ctx // tm
    nx = len(dxs)

    def body(*refs):
        out_ref, og_ref, o_ref, z_ref, gt_ref, w_ref, do_ref, dz_ref, dgt_ref, dw_ref = refs[nx:]
        i = pl.program_id(0)

        @pl.when(i == 0)
        def _():
            dw_ref[...] = jnp.zeros_like(dw_ref)

        @pl.when((i == 0) | (i == ncb))
        def _():
            dgt_ref[...] = jnp.zeros_like(dgt_ref)

        dx = _parts_tile(refs[:nx], ncb)
        dgt_ref[0] += jnp.sum(dx * out_ref[...], axis=0, keepdims=True)
        d_out16 = (gt_ref[0] * dx).astype(BF16)
        dw_ref[...] += lax.dot_general(og_ref[...], d_out16, (((0,), (0,)), ((), ())), preferred_element_type=F32)
        d_og = lax.dot_general(d_out16, w_ref[...], NT_DIMS, preferred_element_type=F32)
        _, gate_vjp = jax.vjp(lambda o_, z_: f_gate(o_, z_), o_ref[...], z_ref[...])
        d_o, d_z = gate_vjp((d_og,))
        do_ref[...] = d_o
        dz_ref[...] = d_z

    row = pl.BlockSpec((tm, d), lambda i: (i, 0))
    mat = pl.BlockSpec((d, d), lambda i: (0, 0))
    return pl.pallas_call(
        body, out_shape=[jax.ShapeDtypeStruct((n, d), F32), jax.ShapeDtypeStruct((n, d), F32),
                         jax.ShapeDtypeStruct(gate.shape, F32), jax.ShapeDtypeStruct((d, d), F32)],
        grid=(n // tm,),
        in_specs=_parts_specs(dxs, tm, ncb) + [row, row, row, row, _bc_spec(gate, ncb), mat],
        out_specs=[row, row, _bc_spec(gate, ncb), mat],
        compiler_params=_params(("arbitrary",), VMEM_LIMIT), name=name)(*dxs, out, og, o, p0, gate, w_out)


def s5_tail(y_ssm, p1, x1p, target, n_ctx, d_vec, b_glu, gate, final_g, w_glu, w_out, name="l1_tail"):
    n, d = y_ssm.shape
    tm = math.gcd(ROW_TILE, n, n_ctx)
    off = n_ctx // tm
    tn_dims = (((0,), (0,)), ((), ()))

    def row_loss(x, g, t):
        e = _rms(x) * g - t
        return 0.5 * (e * e) * (1.0 / d)

    def body(y_ref, u_ref, z_ref, x1_ref, t_ref, d_ref, b_ref, gt_ref, fg_ref, wg_ref, wo_ref,
             l_ref, dx_ref, dy_ref, du_ref, dz_ref, dfg_ref, dgt_ref, db_ref, dd_ref, dwg_ref, dwo_ref):
        @pl.when(pl.program_id(0) == 0)
        def _():
            for r in (l_ref, dfg_ref, dgt_ref, db_ref, dd_ref, dwg_ref, dwo_ref):
                r[...] = jnp.zeros_like(r)

        u, z, tgt, gt = u_ref[...], z_ref[...], t_ref[...], gt_ref[...]
        (ya,), act_vjp = jax.vjp(lambda y_, u_, d_: f_s5_act(y_, u_, d_), y_ref[...], u, d_ref[...])
        ya16 = ya.astype(BF16)
        gl = jnp.dot(ya16, wg_ref[...], preferred_element_type=F32)
        (y3,), glu_vjp = jax.vjp(lambda a_, g_, z_, b_: f_s5_glu(a_, g_, z_, b_), ya, gl, z, b_ref[...])
        y3_16 = y3.astype(BF16)
        out1 = jnp.dot(y3_16, wo_ref[...], preferred_element_type=F32)
        lterm, loss_vjp = jax.vjp(lambda x_, g_: row_loss(x_, g_, tgt), x1_ref[...] + gt * out1, fg_ref[...])
        dx2, dfg = loss_vjp(jnp.ones_like(lterm))
        l_ref[...] += jnp.sum(lterm, axis=0, keepdims=True)
        dfg_ref[...] += dfg
        dx_ref[...] = dx2
        dgt_ref[...] += jnp.sum(dx2 * out1, axis=0, keepdims=True)
        d_out16 = (gt * dx2).astype(BF16)
        dwo_ref[...] += lax.dot_general(y3_16, d_out16, tn_dims, preferred_element_type=F32)
        d_y3 = lax.dot_general(d_out16, wo_ref[...], NT_DIMS, preferred_element_type=F32)
        d_ya, d_gl, d_z, d_b = glu_vjp((d_y3,))
        dz_ref[...] = d_z
        db_ref[...] += d_b
        d_gl16 = d_gl.astype(BF16)
        dwg_ref[...] += lax.dot_general(ya16, d_gl16, tn_dims, preferred_element_type=F32)
        d_ya = d_ya + lax.dot_general(d_gl16, wg_ref[...], NT_DIMS, preferred_element_type=F32)
        d_y, d_u, d_d = act_vjp((d_ya,))
        dy_ref[...] = d_y
        du_ref[...] = d_u
        dd_ref[...] += d_d

    row = pl.BlockSpec((tm, d), lambda i: (i, 0))
    vecs = pl.BlockSpec((1, d), lambda i: (0, 0))
    mat = pl.BlockSpec((d, d), lambda i: (0, 0))
    return pl.pallas_call(
        body,
        out_shape=[jax.ShapeDtypeStruct((1, d), F32)] + [jax.ShapeDtypeStruct((n, d), F32)] * 4
        + [jax.ShapeDtypeStruct((1, d), F32)] * 4 + [jax.ShapeDtypeStruct((d, d), F32)] * 2,
        grid=(n // tm,),
        in_specs=[row, pl.BlockSpec((tm, d), lambda i: (i + off, 0)), pl.BlockSpec((tm, d), lambda i: (i + off, 1)),
                  pl.BlockSpec((tm, d), lambda i: (i + off, 0)), row, vecs, vecs, vecs, vecs, mat, mat],
        out_specs=[vecs, row, row, row, row, vecs, vecs, vecs, vecs, mat, mat],
        compiler_params=_params(("arbitrary",), VMEM_LIMIT), name=name)(
            y_ssm, p1, p1, x1p, target, d_vec, b_glu, gate, final_g, w_glu, w_out)


NT_DIMS = (((1,), (1,)), ((), ()))
HEAD_LANES = 128
N_PAIRS = MLA_HEADS // 2


def _own_lanes(shape, hh):
    lane = lax.broadcasted_iota(jnp.int32, shape, len(shape) - 1)
    return (lane < V_HEAD_DIM) if hh == 0 else (lane >= V_HEAD_DIM)


def _delta_lane(hh):
    return V_HEAD_DIM if hh == 0 else 0


def _rope_tiles(x, cos, sin_next, sin_prev, inverse):
    width = x.shape[-1]
    reps = width // HEAD_LANES
    c, sn, sp = (jnp.tile(t, (1, reps)) for t in (cos, sin_next, sin_prev))
    if inverse:
        return x * c + pltpu.roll(x * sn, 8, 1) + pltpu.roll(x * sp, width - 8, 1)
    return x * c + pltpu.roll(x, width - 8, 1) * sn + pltpu.roll(x, 8, 1) * sp


STAT_LANE = QK_DIM


def _with_stat(x16, col, lane0):
    hi = col.astype(BF16)
    r1 = col - hi.astype(F32)
    mid = r1.astype(BF16)
    lo = (r1 - mid.astype(F32)).astype(BF16)
    lane = lax.broadcasted_iota(jnp.int32, x16.shape, 1)
    return jnp.where(lane == lane0, hi, jnp.where(lane == lane0 + 1, mid, jnp.where(lane == lane0 + 2, lo, x16)))


def attn_fwd(qb, kb, vb, n_ctx):
    T = qb.shape[0]
    tq = math.gcd(ROW_TILE, n_ctx)
    nq, ncb = T // tq, n_ctx // tq

    def body(q_ref, k_ref, v_ref, o_ref, lse_ref, qs_ref):
        qi = pl.program_id(1)

        def rows(n_keys):
            v = v_ref[:n_keys, :]
            outs = []
            for hh in range(2):
                hs = slice(hh * HEAD_LANES, (hh + 1) * HEAD_LANES)
                s = lax.dot_general(q_ref[:, hs], k_ref[:n_keys, hs], NT_DIMS,
                                    preferred_element_type=F32) * SOFTMAX_SCALE
                m = jnp.max(s, axis=-1, keepdims=True)
                p = jnp.exp(s - m)
                l = jnp.sum(p, axis=-1, keepdims=True)
                outs.append(jnp.dot(p.astype(BF16), v, preferred_element_type=F32) / l)
                lse = m + jnp.log(l)
                lse_ref[hh] = lse
                qs_ref[:, hs] = _with_stat(q_ref[:, hs], lse * (-1.0 / SOFTMAX_SCALE), STAT_LANE)
            o_ref[...] = jnp.where(_own_lanes(outs[0].shape, 0), outs[0], outs[1])

        pl.when(qi < ncb)(lambda: rows(n_ctx))
        pl.when(qi >= ncb)(lambda: rows(T))

    return pl.pallas_call(
        body,
        out_shape=[jax.ShapeDtypeStruct((T, MLA_HEADS * V_HEAD_DIM), F32),
                   jax.ShapeDtypeStruct((MLA_HEADS, T, 1), F32), jax.ShapeDtypeStruct(qb.shape, BF16)],
        grid=(N_PAIRS, nq),
        in_specs=[pl.BlockSpec((tq, 2 * HEAD_LANES), lambda h, i: (i, h)),
                  pl.BlockSpec((T, 2 * HEAD_LANES), lambda h, i: (0, h)),
                  pl.BlockSpec((T, 2 * V_HEAD_DIM), lambda h, i: (0, h))],
        out_specs=[pl.BlockSpec((tq, 2 * V_HEAD_DIM), lambda h, i: (i, h)),
                   pl.BlockSpec((2, tq, 1), lambda h, i: (h, i, 0)),
                   pl.BlockSpec((tq, 2 * HEAD_LANES), lambda h, i: (i, h))],
        compiler_params=_params(("parallel", "parallel"), VMEM_LIMIT), name="attn_fwd")(qb, kb, vb)


def attn_bwd_dq(qb, kb, vb, o, do, lse, tabs, n_ctx):
    T = qb.shape[0]
    tq = math.gcd(ROW_TILE, n_ctx)
    nq, ncb = T // tq, n_ctx // tq

    def body(q_ref, k_ref, v_ref, o_ref, do_ref, lse_ref, c_ref, sn_ref, sp_ref, dq_ref, dos_ref):
        qi = pl.program_id(1)

        def rows(n_keys):
            v = v_ref[:n_keys, :]
            dqs = []
            for hh in range(2):
                hs = slice(hh * HEAD_LANES, (hh + 1) * HEAD_LANES)
                k = k_ref[:n_keys, hs]
                do = jnp.where(_own_lanes(do_ref.shape, hh), do_ref[...], 0.0)
                delta = jnp.sum(do * o_ref[...], axis=-1, keepdims=True)
                s = lax.dot_general(q_ref[:, hs], k, NT_DIMS, preferred_element_type=F32) * SOFTMAX_SCALE
                p = jnp.exp(s - lse_ref[hh])
                do16 = do.astype(BF16)
                dp = lax.dot_general(do16, v, NT_DIMS, preferred_element_type=F32)
                ds = p * (dp - delta) * SOFTMAX_SCALE
                dqs.append(jnp.dot(ds.astype(BF16), k, preferred_element_type=F32))
                dos_ref[:, hs] = _with_stat(do16, delta, _delta_lane(hh))
            dq = jnp.concatenate(dqs, axis=1)
            dq_ref[...] = _rope_tiles(dq, c_ref[...], sn_ref[...], sp_ref[...], True).astype(BF16)

        pl.when(qi < ncb)(lambda: rows(n_ctx))
        pl.when(qi >= ncb)(lambda: rows(T))

    tab = pl.BlockSpec((tq, HEAD_LANES), lambda h, i: (i, 0))
    return pl.pallas_call(
        body,
        out_shape=[jax.ShapeDtypeStruct((T, MLA_HEADS * HEAD_LANES), BF16)] * 2,
        grid=(N_PAIRS, nq),
        in_specs=[pl.BlockSpec((tq, 2 * HEAD_LANES), lambda h, i: (i, h)),
                  pl.BlockSpec((T, 2 * HEAD_LANES), lambda h, i: (0, h)),
                  pl.BlockSpec((T, 2 * V_HEAD_DIM), lambda h, i: (0, h)),
                  pl.BlockSpec((tq, 2 * V_HEAD_DIM), lambda h, i: (i, h)),
                  pl.BlockSpec((tq, 2 * V_HEAD_DIM), lambda h, i: (i, h)),
                  pl.BlockSpec((2, tq, 1), lambda h, i: (h, i, 0)), tab, tab, tab],
        out_specs=[pl.BlockSpec((tq, 2 * HEAD_LANES), lambda h, i: (i, h))] * 2,
        compiler_params=_params(("parallel", "parallel"), VMEM_LIMIT), name="attn_bwd_dq")(
            qb, kb, vb, o, do, lse, *tabs)


def attn_bwd_dkv(qs, kb, vb, dos, n_ctx):
    T = qs.shape[0]
    tq = math.gcd(ROW_TILE, n_ctx)
    nq, ncb = T // tq, n_ctx // tq

    def body(q_ref, do_ref, k_ref, v_ref, dk_ref, dv_ref):
        kj = pl.program_id(1)

        def cols(first):
            v = v_ref[...]
            lane = lax.broadcasted_iota(jnp.int32, v.shape, 1)
            dvs = []
            for hh in range(2):
                hs = slice(hh * HEAD_LANES, (hh + 1) * HEAD_LANES)
                q = q_ref[first:, hs]
                do16 = do_ref[first:, hs]
                in_delta = (lane >= _delta_lane(hh)) & (lane < _delta_lane(hh) + 3)
                v_minus = jnp.where(in_delta, -jnp.ones_like(v), v)
                pt = jnp.exp(lax.dot_general(k_ref[:, hs], q, NT_DIMS, preferred_element_type=F32) * SOFTMAX_SCALE)
                dvs.append(jnp.dot(pt.astype(BF16), do16, preferred_element_type=F32))
                dst = pt * lax.dot_general(v_minus, do16, NT_DIMS, preferred_element_type=F32) * SOFTMAX_SCALE
                dk_ref[:, hs] = jnp.dot(dst.astype(BF16), q, preferred_element_type=F32)
            dv_ref[...] = jnp.where(_own_lanes(dvs[0].shape, 0), dvs[0], dvs[1])

        pl.when(kj < ncb)(lambda: cols(0))
        pl.when(kj >= ncb)(lambda: cols(n_ctx))

    return pl.pallas_call(
        body,
        out_shape=[jax.ShapeDtypeStruct((T, MLA_HEADS * HEAD_LANES), F32),
                   jax.ShapeDtypeStruct((T, MLA_HEADS * V_HEAD_DIM), F32)],
        grid=(N_PAIRS, nq),
        in_specs=[pl.BlockSpec((T, 2 * HEAD_LANES), lambda h, j: (0, h)),
                  pl.BlockSpec((T, 2 * HEAD_LANES), lambda h, j: (0, h)),
                  pl.BlockSpec((tq, 2 * HEAD_LANES), lambda h, j: (j, h)),
                  pl.BlockSpec((tq, 2 * V_HEAD_DIM), lambda h, j: (j, h))],
        out_specs=[pl.BlockSpec((tq, 2 * HEAD_LANES), lambda h, j: (j, h)),
                   pl.BlockSpec((tq, 2 * V_HEAD_DIM), lambda h, j: (j, h))],
        compiler_params=_params(("parallel", "parallel"), VMEM_LIMIT), name="attn_bwd_dkv")(
            qs, dos, kb, vb)


def _split_bf16(x):
    hi = x.astype(BF16)
    return hi, (x - hi.astype(F32)).astype(BF16)


def q_heads(cq, gain, w_uq_p, tabs, name="l0_uq"):
    T, K = cq.arr.shape[0], cq.width
    N = w_uq_p.shape[1]
    tm = math.gcd(ROW_TILE, T)

    def body(a_ref, g_ref, w_ref, c_ref, sn_ref, sp_ref, o_ref, n_ref):
        qn = f_rms(a_ref[...], g_ref[0])[0].astype(BF16)
        n_ref[...] = qn
        acc = jnp.dot(qn, w_ref[...], preferred_element_type=F32)
        o_ref[...] = _rope_tiles(acc, c_ref[...], sn_ref[...], sp_ref[...], False).astype(BF16)

    tab = pl.BlockSpec((tm, HEAD_LANES), lambda i: (i, 0))
    return pl.pallas_call(
        body, out_shape=[jax.ShapeDtypeStruct((T, N), BF16), jax.ShapeDtypeStruct((T, K), BF16)], grid=(T // tm,),
        in_specs=[cq.spec(tm), pl.BlockSpec((1, 1, K), lambda i: (0, 0, 0)), pl.BlockSpec((K, N), lambda i: (0, 0)),
                  tab, tab, tab],
        out_specs=[pl.BlockSpec((tm, N), lambda i: (i, 0)), pl.BlockSpec((tm, K), lambda i: (i, 0))],
        compiler_params=_params(("parallel",), VMEM_LIMIT), name=name)(cq.arr, gain, w_uq_p, *tabs)


def kv_heads(ckv, gain, w_kn_p, w_v, kr, spread, tabs, name="l0_ukv"):
    T, K = ckv.arr.shape[0], ckv.width
    N = w_kn_p.shape[1]
    NV = w_v.shape[1]
    tm = math.gcd(ROW_TILE, T)

    def body(a_ref, g_ref, wk_ref, wv_ref, kr_ref, e_ref, c_ref, sn_ref, sp_ref, k_ref, v_ref, n_ref):
        a = f_rms(a_ref[...], g_ref[0])[0].astype(BF16)
        n_ref[...] = a
        hi, lo = _split_bf16(kr_ref[...])
        acc = (jnp.dot(a, wk_ref[...], preferred_element_type=F32)
               + jnp.dot(hi, e_ref[...], preferred_element_type=F32)
               + jnp.dot(lo, e_ref[...], preferred_element_type=F32))
        roped = _rope_tiles(acc, c_ref[...], sn_ref[...], sp_ref[...], False)
        lane = lax.broadcasted_iota(jnp.int32, roped.shape, 1) % HEAD_LANES
        k_ref[...] = jnp.where((lane >= STAT_LANE) & (lane < STAT_LANE + 3), 1.0, roped).astype(BF16)
        v_ref[...] = jnp.dot(a, wv_ref[...], preferred_element_type=F32).astype(BF16)

    tab = pl.BlockSpec((tm, HEAD_LANES), lambda i: (i, 0))
    return pl.pallas_call(
        body, out_shape=[jax.ShapeDtypeStruct((T, N), BF16), jax.ShapeDtypeStruct((T, NV), BF16),
                         jax.ShapeDtypeStruct((T, K), BF16)], grid=(T // tm,),
        in_specs=[ckv.spec(tm), pl.BlockSpec((1, 1, K), lambda i: (0, 0, 0)), pl.BlockSpec((K, N), lambda i: (0, 0)),
                  pl.BlockSpec((K, NV), lambda i: (0, 0)), pl.BlockSpec((tm, QK_ROPE_DIM), lambda i: (i, 0)),
                  pl.BlockSpec((QK_ROPE_DIM, N), lambda i: (0, 0)), tab, tab, tab],
        out_specs=[pl.BlockSpec((tm, N), lambda i: (i, 0)), pl.BlockSpec((tm, NV), lambda i: (i, 0)),
                   pl.BlockSpec((tm, K), lambda i: (i, 0))],
        compiler_params=_params(("parallel",), VMEM_LIMIT), name=name)(ckv.arr, gain, w_kn_p, w_v, kr, spread, *tabs)


def heads_unrope(d, tabs, spread=None, name="unrope"):
    T, N = d.shape
    tm = math.gcd(ROW_TILE, T)

    def body(*refs):
        if spread is None:
            d_ref, c_ref, sn_ref, sp_ref, o_ref = refs
        else:
            d_ref, c_ref, sn_ref, sp_ref, e_ref, o_ref, kr_ref = refs
        g = _rope_tiles(d_ref[...], c_ref[...], sn_ref[...], sp_ref[...], True)
        o_ref[...] = g.astype(BF16)
        if spread is not None:
            hi, lo = _split_bf16(g)
            kr_ref[...] = (lax.dot_general(hi, e_ref[...], NT_DIMS, preferred_element_type=F32)
                           + lax.dot_general(lo, e_ref[...], NT_DIMS, preferred_element_type=F32))

    tab = pl.BlockSpec((tm, HEAD_LANES), lambda i: (i, 0))
    row = pl.BlockSpec((tm, N), lambda i: (i, 0))
    ins, in_specs = [d, *tabs], [row, tab, tab, tab]
    out_shape, out_specs = [jax.ShapeDtypeStruct((T, N), BF16)], [row]
    if spread is not None:
        ins.append(spread)
        in_specs.append(pl.BlockSpec(spread.shape, lambda i: (0, 0)))
        out_shape.append(jax.ShapeDtypeStruct((T, spread.shape[0]), F32))
        out_specs.append(pl.BlockSpec((tm, spread.shape[0]), lambda i: (i, 0)))
    return pl.pallas_call(
        body, out_shape=out_shape, grid=(T // tm,), in_specs=in_specs, out_specs=out_specs,
        compiler_params=_params(("parallel",), VMEM_LIMIT), name=name)(*ins)


def _cmul(ar, ai, br, bi):
    return ar * br - ai * bi, ar * bi + ai * br


def s5_chain(finals, s0, a, n_steps, reverse, name):
    W = finals.shape[-1]
    first = N_SEG - 1 if reverse else 0

    def body(f_ref, s0_ref, a_ref, c_ref):
        pr, pi = jnp.ones((1, W), F32), jnp.zeros((1, W), F32)
        br, bi = a_ref[0], a_ref[1]
        n = n_steps
        while n:
            if n & 1:
                pr, pi = _cmul(pr, pi, br, bi)
            br, bi = _cmul(br, bi, br, bi)
            n >>= 1
        fr, fi = f_ref[0], f_ref[1]
        row = lax.broadcasted_iota(jnp.int32, (N_SEG, W), 0)
        s0r = jnp.broadcast_to(s0_ref[0], (N_SEG, W))
        s0i = jnp.broadcast_to(s0_ref[1], (N_SEG, W))
        cr = jnp.where(row == first, s0r, 0.0)
        ci = jnp.where(row == first, s0i, 0.0)
        shift = N_SEG - 1 if reverse else 1
        for _ in range(N_SEG - 1):
            mr, mi = _cmul(pr, pi, cr, ci)
            tr = pltpu.roll(fr + mr, shift, 0)
            ti = pltpu.roll(fi + mi, shift, 0)
            cr = jnp.where(row == first, s0r, tr)
            ci = jnp.where(row == first, s0i, ti)
        c_ref[0] = cr
        c_ref[1] = ci

    return pl.pallas_call(body, out_shape=jax.ShapeDtypeStruct((2, N_SEG, W), F32), name=name)(finals, s0, a)


def _scan_chunk(bur, bui, st_ref, a_ref, n_steps, reverse):
    for lc in range(S5_LANES // BLK_ST):
        sl = slice(lc * BLK_ST, (lc + 1) * BLK_ST)
        lr = jnp.broadcast_to(a_ref[0, :, sl], (N_SEG, BLK_ST))
        li = jnp.broadcast_to(a_ref[1, :, sl], (N_SEG, BLK_ST))

        def step(jj, carry, sl=sl, lr=lr, li=li):
            sr, si = carry
            j = (n_steps - 1 - jj) if reverse else jj
            r0 = pl.multiple_of(j * N_SEG, N_SEG)
            nr = lr * sr - li * si + bur[pl.ds(r0, N_SEG), sl]
            ni = lr * si + li * sr + bui[pl.ds(r0, N_SEG), sl]
            bur[pl.ds(r0, N_SEG), sl] = nr
            bui[pl.ds(r0, N_SEG), sl] = ni
            return nr, ni

        sr, si = lax.fori_loop(0, n_steps, step, (st_ref[0, :, sl], st_ref[1, :, sl]))
        st_ref[0, :, sl] = sr
        st_ref[1, :, sl] = si


def _project_in(x16, w_re, w_im, bur, bui, adjoint):
    for gb in range(N_BLOCKS):
        xb = x16[:, gb * BLK_CH:(gb + 1) * BLK_CH]
        sl = slice(gb * BLK_ST, (gb + 1) * BLK_ST)
        if adjoint:
            dn = (((1,), (1,)), ((), ()))
            bur[:, sl] = lax.dot_general(xb, w_re[gb], dn, preferred_element_type=F32)
            bui[:, sl] = -lax.dot_general(xb, w_im[gb], dn, preferred_element_type=F32)
        else:
            bur[:, sl] = jnp.dot(xb, w_re[gb], preferred_element_type=F32)
            bui[:, sl] = jnp.dot(xb, w_im[gb], preferred_element_type=F32)


def s5_scan(act, w_re, w_im, a, init, *, reverse, adjoint=False, c_re=None, c_im=None, add=None,
            want_ckpt=False, rows=None, name):
    act_off, N = rows if rows is not None else (0, act.shape[0])
    R = math.gcd(ROW_TILE, N, act_off)
    nch, jc = N // R, R // N_SEG
    with_out = c_re is not None

    def chunk(i):
        return (nch - 1 - i) if reverse else i

    def body(*refs):
        act_ref, wre_ref, wim_ref, a_ref, init_ref = refs[:5]
        k = 5
        if with_out:
            cre_ref, cim_ref = refs[k:k + 2]
            k += 2
        if add is not None:
            add_ref = refs[k]
            k += 1
        if with_out:
            out_ref = refs[k]
            k += 1
        if want_ckpt:
            ck_ref = refs[k]
            k += 1
        fin_ref, bur, bui = refs[k:k + 3]

        @pl.when(pl.program_id(0) == 0)
        def _():
            fin_ref[...] = init_ref[...]

        if want_ckpt:
            ck_ref[0] = fin_ref[...]
        _project_in(act_ref[...].astype(BF16), wre_ref, wim_ref, bur, bui, adjoint)
        _scan_chunk(bur, bui, fin_ref, a_ref, jc, reverse)
        if with_out:
            for gb in range(N_BLOCKS):
                sl = slice(gb * BLK_ST, (gb + 1) * BLK_ST)
                y = (jnp.dot(bur[:, sl].astype(BF16), cre_ref[gb], preferred_element_type=F32)
                     - jnp.dot(bui[:, sl].astype(BF16), cim_ref[gb], preferred_element_type=F32))
                cs = slice(gb * BLK_CH, (gb + 1) * BLK_CH)
                if add is not None:
                    y = y + add_ref[:, cs]
                out_ref[:, cs] = y

    row_spec = pl.BlockSpec((R, D_MODEL), lambda i: (chunk(i), 0))
    act_spec = pl.BlockSpec((R, D_MODEL), lambda i: (chunk(i) + act_off // R, 0))
    w_spec = pl.BlockSpec(w_re.shape, lambda i: (0, 0, 0))
    st_spec = pl.BlockSpec((2, N_SEG, S5_LANES), lambda i: (0, 0, 0))
    ins = [act, w_re, w_im, a, init]
    in_specs = [act_spec, w_spec, w_spec, pl.BlockSpec((2, 1, S5_LANES), lambda i: (0, 0, 0)), st_spec]
    if with_out:
        ins += [c_re, c_im]
        in_specs += [pl.BlockSpec(c_re.shape, lambda i: (0, 0, 0))] * 2
    if add is not None:
        ins.append(add)
        in_specs.append(row_spec)
    out_shape, out_specs = [], []
    if with_out:
        out_shape.append(jax.ShapeDtypeStruct((N, D_MODEL), F32))
        out_specs.append(row_spec)
    if want_ckpt:
        out_shape.append(jax.ShapeDtypeStruct((nch, 2, N_SEG, S5_LANES), F32))
        out_specs.append(pl.BlockSpec((1, 2, N_SEG, S5_LANES), lambda i: (chunk(i), 0, 0, 0)))
    out_shape.append(jax.ShapeDtypeStruct((2, N_SEG, S5_LANES), F32))
    out_specs.append(st_spec)
    res = pl.pallas_call(
        body, out_shape=out_shape, grid=(nch,), in_specs=in_specs, out_specs=out_specs,
        scratch_shapes=[pltpu.VMEM((R, S5_LANES), F32), pltpu.VMEM((R, S5_LANES), F32)],
        compiler_params=_params(("arbitrary",), VMEM_LIMIT), name=name)(*ins)
    res = list(res)
    out = res.pop(0) if with_out else None
    ckpt = res.pop(0) if want_ckpt else None
    return out, ckpt, res[0]


def s5_grads(dy, u, ckpt, b_re, b_im, c_re, c_im, lam, init_adj, *, reverse, add=None, u_off=0, du_rows=None,
             du_into=None, name):
    N = dy.shape[0]
    du_total, du_first = du_rows if du_rows is not None else (N, 0)
    R = math.gcd(ROW_TILE, N, u_off, du_first)
    nch, jc = N // R, R // N_SEG
    W = S5_LANES

    def chunk(i):
        return i if reverse else (nch - 1 - i)

    def body(*refs):
        dy_ref, u_ref, ck_ref, bre_ref, bim_ref, cre_ref, cim_ref, lam_ref, init_ref = refs[:9]
        k = 9
        if add is not None:
            add_ref = refs[k]
            k += 1
        k += du_into is not None
        du_ref, dlam_ref, dbre_ref, dbim_ref, dcre_ref, dcim_ref, fin_ref = refs[k:k + 7]
        sr_buf, si_buf, er_buf, ei_buf, st_buf = refs[k + 7:k + 12]

        @pl.when(pl.program_id(0) == 0)
        def _():
            fin_ref[...] = init_ref[...]
            dlam_ref[...] = jnp.zeros_like(dlam_ref)
            dbre_ref[...] = jnp.zeros_like(dbre_ref)
            dbim_ref[...] = jnp.zeros_like(dbim_ref)
            dcre_ref[...] = jnp.zeros_like(dcre_ref)
            dcim_ref[...] = jnp.zeros_like(dcim_ref)

        u16 = u_ref[...].astype(BF16)
        dy16 = dy_ref[...].astype(BF16)
        st_buf[...] = ck_ref[0]
        _project_in(u16, bre_ref, bim_ref, sr_buf, si_buf, False)
        _scan_chunk(sr_buf, si_buf, st_buf, lam_ref, jc, reverse)
        _project_in(dy16, cre_ref, cim_ref, er_buf, ei_buf, True)
        for lc in range(W // BLK_ST):
            sl = slice(lc * BLK_ST, (lc + 1) * BLK_ST)
            lr = jnp.broadcast_to(lam_ref[0, :, sl], (N_SEG, BLK_ST))
            li = jnp.broadcast_to(lam_ref[1, :, sl], (N_SEG, BLK_ST))

            def one(r0, spr, spi, carry, sl=sl, lr=lr, li=li):
                gr, gi, ar, ai = carry
                nr = er_buf[pl.ds(r0, N_SEG), sl] + lr * gr + li * gi
                ni = ei_buf[pl.ds(r0, N_SEG), sl] + lr * gi - li * gr
                er_buf[pl.ds(r0, N_SEG), sl] = nr
                ei_buf[pl.ds(r0, N_SEG), sl] = ni
                return nr, ni, ar + spr * nr + spi * ni, ai + spr * ni - spi * nr

            def step(ff, carry, sl=sl, one=one):
                f = jc - 1 - ff
                j = (jc - 1 - f) if reverse else f
                jp = (j + 1) if reverse else (j - 1)
                r0 = pl.multiple_of(j * N_SEG, N_SEG)
                p0 = pl.multiple_of(jp * N_SEG, N_SEG)
                return one(r0, sr_buf[pl.ds(p0, N_SEG), sl], si_buf[pl.ds(p0, N_SEG), sl], carry)

            carry = (fin_ref[0, :, sl], fin_ref[1, :, sl], dlam_ref[0, :, sl], dlam_ref[1, :, sl])
            carry = lax.fori_loop(0, jc - 1, step, carry)
            r_first = (jc - 1) * N_SEG if reverse else 0
            gr, gi, ar, ai = one(r_first, ck_ref[0, 0, :, sl], ck_ref[0, 1, :, sl], carry)
            fin_ref[0, :, sl] = gr
            fin_ref[1, :, sl] = gi
            dlam_ref[0, :, sl] = ar
            dlam_ref[1, :, sl] = ai
        tn = (((0,), (0,)), ((), ()))
        nt = (((1,), (1,)), ((), ()))
        for gb in range(N_BLOCKS):
            sl = slice(gb * BLK_ST, (gb + 1) * BLK_ST)
            cs = slice(gb * BLK_CH, (gb + 1) * BLK_CH)
            gr16 = er_buf[:, sl].astype(BF16)
            gi16 = ei_buf[:, sl].astype(BF16)
            du = (lax.dot_general(gr16, bre_ref[gb], nt, preferred_element_type=F32)
                  + lax.dot_general(gi16, bim_ref[gb], nt, preferred_element_type=F32))
            if add is not None:
                du = du + add_ref[:, cs]
            du_ref[:, cs] = du
            ub, dyb = u16[:, cs], dy16[:, cs]
            dbre_ref[gb] += lax.dot_general(ub, gr16, tn, preferred_element_type=F32)
            dbim_ref[gb] += lax.dot_general(ub, gi16, tn, preferred_element_type=F32)
            dcre_ref[gb] += lax.dot_general(sr_buf[:, sl].astype(BF16), dyb, tn, preferred_element_type=F32)
            dcim_ref[gb] -= lax.dot_general(si_buf[:, sl].astype(BF16), dyb, tn, preferred_element_type=F32)

    row_spec = pl.BlockSpec((R, D_MODEL), lambda i: (chunk(i), 0))
    st_spec = pl.BlockSpec((2, N_SEG, W), lambda i: (0, 0, 0))
    wb_spec = pl.BlockSpec(b_re.shape, lambda i: (0, 0, 0))
    wc_spec = pl.BlockSpec(c_re.shape, lambda i: (0, 0, 0))
    ins = [dy, u, ckpt, b_re, b_im, c_re, c_im, lam, init_adj]
    u_spec = pl.BlockSpec((R, D_MODEL), lambda i: (chunk(i) + u_off // R, 0))
    in_specs = [row_spec, u_spec, pl.BlockSpec((1, 2, N_SEG, W), lambda i: (chunk(i), 0, 0, 0)),
                wb_spec, wb_spec, wc_spec, wc_spec, pl.BlockSpec((2, 1, W), lambda i: (0, 0, 0)), st_spec]
    if add is not None:
        ins.append(add)
        in_specs.append(row_spec)
    aliases = {}
    if du_into is not None:
        aliases[len(ins)] = 0
        ins.append(du_into)
        in_specs.append(pl.BlockSpec(memory_space=pl.ANY))
    du_spec = pl.BlockSpec((R, D_MODEL), lambda i: (chunk(i) + du_first // R, 0))
    out_shape = [jax.ShapeDtypeStruct((du_total, D_MODEL), F32), jax.ShapeDtypeStruct((2, N_SEG, W), F32),
                 jax.ShapeDtypeStruct(b_re.shape, F32), jax.ShapeDtypeStruct(b_re.shape, F32),
                 jax.ShapeDtypeStruct(c_re.shape, F32), jax.ShapeDtypeStruct(c_re.shape, F32),
                 jax.ShapeDtypeStruct((2, N_SEG, W), F32)]
    out_specs = [du_spec, st_spec, wb_spec, wb_spec, wc_spec, wc_spec, st_spec]
    return pl.pallas_call(
        body, out_shape=out_shape, grid=(nch,), in_specs=in_specs, out_specs=out_specs, input_output_aliases=aliases,
        scratch_shapes=[pltpu.VMEM((R, W), F32) for _ in range(4)] + [pltpu.VMEM((2, N_SEG, W), F32)],
        compiler_params=_params(("arbitrary",), VMEM_LIMIT), name=name)(*ins)


def adamw(w, g, m, v, name="adamw", after=None):
    n, d = w.shape
    lanes = -(-d // 128) * 128
    tm = n
    while tm * lanes * 4 > (1 << 20) and tm % 16 == 0:
        tm //= 2
    c1 = 1.0 - ADAM_B1 ** ADAM_STEP
    c2 = 1.0 - ADAM_B2 ** ADAM_STEP

    def body(w_ref, g_ref, m_ref, v_ref, *rest):
        d_ref, nm_ref, nv_ref = rest[-3:]
        g_ = g_ref[...]
        m_ = ADAM_B1 * m_ref[...] + (1.0 - ADAM_B1) * g_
        v_ = ADAM_B2 * v_ref[...] + (1.0 - ADAM_B2) * (g_ * g_)
        d_ref[...] = -ADAM_LR * ((m_ / c1) / (jnp.sqrt(v_ / c2) + ADAM_EPS) + ADAM_WD * w_ref[...])
        nm_ref[...] = m_
        nv_ref[...] = v_

    spec = pl.BlockSpec((tm, d), lambda i: (i, 0))
    extra = [] if after is None else [after]
    return pl.pallas_call(
        body, out_shape=[jax.ShapeDtypeStruct((n, d), F32)] * 3, grid=(n // tm,),
        in_specs=[spec] * 4 + [pl.BlockSpec(memory_space=pl.ANY)] * len(extra), out_specs=[spec] * 3,
        compiler_params=_params(("parallel",), VMEM_LIMIT), name=name)(w, g, m, v, *extra)


def _coords():
    return lax.axis_index("x"), lax.axis_index("y"), lax.axis_index("c")


def exchange(arrays, out_shapes, remote, local, name, aliases=None):
    n_in, n_out, n_rem, n_loc = len(arrays), len(out_shapes), len(remote), len(local)

    def at(ref, idx):
        return ref if idx is None else ref.at[idx]

    def body(*refs):
        ins, outs = refs[:n_in], refs[n_in:n_in + n_out]
        send_sems, recv_sems, local_sems = refs[n_in + n_out:]
        me = _coords()
        sends, recvs = [], []
        for k, (flip, ii, src_at, oi, dst_at) in enumerate(remote):
            peer = (me[0] ^ flip[0], me[1] ^ flip[1], me[2] ^ flip[2])
            src = at(ins[ii], src_at(me, peer))
            sends.append(pltpu.make_async_remote_copy(
                src_ref=src, dst_ref=at(outs[oi], dst_at(me)), send_sem=send_sems.at[k], recv_sem=recv_sems.at[k],
                device_id=peer, device_id_type=MESH))
            recvs.append(pltpu.make_async_remote_copy(
                src_ref=src, dst_ref=at(outs[oi], dst_at(peer)), send_sem=send_sems.at[k], recv_sem=recv_sems.at[k],
                device_id=peer, device_id_type=MESH))
        locs = [pltpu.make_async_copy(at(ins[ii], src_at(me)), at(outs[oi], dst_at(me)), local_sems.at[k])
                for k, (ii, src_at, oi, dst_at) in enumerate(local)]
        for cp in locs + sends:
            cp.start()
        for cp in recvs:
            cp.wait_recv()
        for cp in sends:
            cp.wait_send()
        for cp in locs:
            cp.wait()

    hbm = pl.BlockSpec(memory_space=pl.ANY)
    return pl.pallas_call(
        body, out_shape=list(out_shapes), in_specs=[hbm] * n_in, out_specs=[hbm] * n_out,
        scratch_shapes=[pltpu.SemaphoreType.DMA((n_rem,)), pltpu.SemaphoreType.DMA((n_rem,)),
                        pltpu.SemaphoreType.DMA((max(n_loc, 1),))],
        input_output_aliases=aliases or {}, name=name)(*arrays)


ALL_FLIPS = [(dx, dy, dc) for dx in (0, 1) for dy in (0, 1) for dc in (0, 1)][1:]
CHIP_FLIPS = [(1, 0, 0), (0, 1, 0), (1, 1, 0)]
CORE_FLIP = (0, 0, 1)


def _dev_index(p):
    return 4 * p[0] + 2 * p[1] + p[2]


def _chip_index(p):
    return 2 * p[0] + p[1]


def _gather(xs, flips, index, n, name):
    arrays = [x[None] for x in xs]
    outs = [jax.ShapeDtypeStruct((n,) + x.shape, x.dtype) for x in xs]
    remote = [(f, a, lambda me, peer: (0,), a, lambda s: (index(s),)) for a in range(len(xs)) for f in flips]
    local = [(a, lambda me: (0,), a, lambda me: (index(me),)) for a in range(len(xs))]
    return exchange(arrays, outs, remote, local, name)


def allgather_devices(x, name):
    return _gather([x], ALL_FLIPS, _dev_index, N_DEV, name)[0]


def allgather_chips(xs, name):
    return _gather(xs, CHIP_FLIPS, _chip_index, N_CHIP, name)


def gather_halves(xs, name):
    n = len(xs)
    nk = n * len(CHIP_FLIPS)

    def body(*refs):
        ins, outs = refs[:n], refs[n:2 * n]
        ici_send, ici_recv, d2d_send, d2d_recv = refs[2 * n:]
        me = _coords()
        sibling = (me[0], me[1], 1 - me[2])
        first, passed, landed = [], [], []
        for a in range(n):
            half = ins[a].shape[0] // 2
            mine = ins[a].at[pl.ds(pl.multiple_of(me[2] * half, 16), half)]
            for j, flip in enumerate(CHIP_FLIPS):
                k = a * len(CHIP_FLIPS) + j
                peer = (me[0] ^ flip[0], me[1] ^ flip[1], me[2])
                first.append(pltpu.make_async_remote_copy(
                    src_ref=mine, dst_ref=outs[a].at[_chip_index(me), me[2]], send_sem=ici_send.at[k],
                    recv_sem=ici_recv.at[k], device_id=peer, device_id_type=MESH))
                arrived = outs[a].at[_chip_index(peer), me[2]]
                landed.append(pltpu.make_async_remote_copy(
                    src_ref=mine, dst_ref=arrived, send_sem=ici_send.at[k], recv_sem=ici_recv.at[k],
                    device_id=peer, device_id_type=MESH))
                passed.append(pltpu.make_async_remote_copy(
                    src_ref=arrived, dst_ref=arrived, send_sem=d2d_send.at[k], recv_sem=d2d_recv.at[k],
                    device_id=sibling, device_id_type=MESH))
        for cp in first:
            cp.start()
        for k in range(nk):
            landed[k].wait_recv()
            passed[k].start()
        for a in range(n):
            for j, flip in enumerate(CHIP_FLIPS):
                k = a * len(CHIP_FLIPS) + j
                peer_chip = _chip_index((me[0] ^ flip[0], me[1] ^ flip[1]))
                from_sibling = outs[a].at[peer_chip, 1 - me[2]]
                pltpu.make_async_remote_copy(
                    src_ref=from_sibling, dst_ref=from_sibling, send_sem=d2d_send.at[k], recv_sem=d2d_recv.at[k],
                    device_id=sibling, device_id_type=MESH).wait_recv()
        for cp in first + passed:
            cp.wait_send()

    hbm = pl.BlockSpec(memory_space=pl.ANY)
    return pl.pallas_call(
        body, out_shape=[jax.ShapeDtypeStruct((N_CHIP, 2, x.shape[0] // 2, x.shape[1]), x.dtype) for x in xs],
        in_specs=[hbm] * n, out_specs=[hbm] * n,
        scratch_shapes=[pltpu.SemaphoreType.DMA((nk,)) for _ in range(4)], name=name)(*xs)


HBM_SPEC = pl.BlockSpec(memory_space=pltpu.HBM)
SEM_SPEC = pl.BlockSpec(memory_space=pltpu.SEMAPHORE)
DATAFLOW = pltpu.SideEffectType.DATAFLOW_SIDE_EFFECTING


def _at(ref, idx):
    return ref if idx is None else ref.at[idx]


def _peer(me, flip):
    return (me[0] ^ flip[0], me[1] ^ flip[1], me[2] ^ flip[2])


def exchange_start(arrays, land_shapes, remote, name, after=None):
    n_in, n_out, nk = len(arrays), len(land_shapes), len(remote)
    after = list(after or [])
    n_after = len(after)

    def body(*refs):
        srcs, lands = refs[:n_in], refs[n_in:n_in + n_out]
        first_out = n_in + n_out + n_after
        send_sems, recv_sems, token = refs[first_out], refs[first_out + 1], refs[-1]
        me = _coords()
        for k, (flip, ii, src_at, oi, dst_at) in enumerate(remote):
            peer = _peer(me, flip)
            pltpu.make_async_remote_copy(
                src_ref=_at(srcs[ii], src_at(me, peer)), dst_ref=_at(lands[oi], dst_at(me)), send_sem=send_sems.at[k],
                recv_sem=recv_sems.at[k], device_id=peer, device_id_type=MESH).start()
        token[...] = jnp.zeros_like(token)

    lands = [lax.empty(s.shape, s.dtype) for s in land_shapes]
    bufs = list(arrays) + lands
    out = pl.pallas_call(
        body, name=name,
        out_shape=(pltpu.SemaphoreType.DMA((nk,)), pltpu.SemaphoreType.DMA((nk,)),
                   *[pltpu.HBM(b.shape, b.dtype) for b in bufs], jax.ShapeDtypeStruct((8, 128), F32)),
        in_specs=[HBM_SPEC] * len(bufs) + [pl.BlockSpec(memory_space=pl.ANY)] * n_after,
        out_specs=(SEM_SPEC, SEM_SPEC, *[HBM_SPEC] * len(bufs), pl.BlockSpec(memory_space=pltpu.VMEM)),
        input_output_aliases={a: 2 + a for a in range(len(bufs))},
        compiler_params=pltpu.CompilerParams(has_side_effects=DATAFLOW),
    )(*[pltpu.with_memory_space_constraint(b, pltpu.HBM) for b in bufs], *after)
    flight = (out[0], out[1], list(out[2:2 + n_in]), list(out[2 + n_in:2 + n_in + n_out]), remote)
    return flight, out[-1]


def exchange_wait(flight, after, name):
    send_sems, recv_sems, arrays, lands, remote = flight
    n_in, n_out = len(arrays), len(lands)
    after = list(after) if isinstance(after, (list, tuple)) else [after]

    def body(*refs):
        srcs, lnds = refs[:n_in], refs[n_in:n_in + n_out]
        s_sems, r_sems = refs[n_in + n_out], refs[n_in + n_out + 1]
        me = _coords()
        for k, (flip, ii, src_at, oi, dst_at) in enumerate(remote):
            peer = _peer(me, flip)
            copy = pltpu.make_async_remote_copy(
                src_ref=_at(srcs[ii], src_at(me, peer)), dst_ref=_at(lnds[oi], dst_at(peer)), send_sem=s_sems.at[k],
                recv_sem=r_sems.at[k], device_id=peer, device_id_type=MESH)
            copy.wait_send()
            copy.wait_recv()

    bufs = list(arrays) + list(lands)
    out = pl.pallas_call(
        body, name=name,
        out_shape=tuple(pltpu.HBM(b.shape, b.dtype) for b in bufs),
        in_specs=[HBM_SPEC] * len(bufs) + [SEM_SPEC, SEM_SPEC] + [pl.BlockSpec(memory_space=pl.ANY)] * len(after),
        out_specs=tuple([HBM_SPEC] * len(bufs)),
        input_output_aliases={a: a for a in range(len(bufs))},
        compiler_params=pltpu.CompilerParams(has_side_effects=DATAFLOW),
    )(*bufs, send_sems, recv_sems, *after)
    return list(out[:n_in]), list(out[n_in:])


def _half_tile(h, cd):
    return h if h * cd * 4 <= (1 << 20) else math.gcd(512, h)


def pair_add(g, got, core, out_dtype, name):
    _, _, h, cd = g.shape
    th = _half_tile(h, cd)

    def body(c_ref, g_ref, got_ref, o_ref):
        o_ref[0] = (g_ref[0, 0] + got_ref[0]).astype(o_ref.dtype)

    return pl.pallas_call(
        body, out_shape=jax.ShapeDtypeStruct((N_CHIP, h, cd), out_dtype),
        grid_spec=pltpu.PrefetchScalarGridSpec(
            num_scalar_prefetch=1, grid=(N_CHIP, h // th),
            in_specs=[pl.BlockSpec((1, 1, th, cd), lambda q, i, c: (q, c[0], i, 0)),
                      pl.BlockSpec((1, th, cd), lambda q, i, c: (q, i, 0))],
            out_specs=pl.BlockSpec((1, th, cd), lambda q, i, c: (q, i, 0))),
        compiler_params=_params(("parallel", "parallel"), VMEM_LIMIT), name=name)(core, g, got)


def sum_chips(parts, sums, place, name):
    _, h, cd = parts.shape
    th = _half_tile(h, cd)

    def body(pc_ref, p_ref, own_ref, o_ref):
        acc = None
        for q in range(N_CHIP):
            term = jnp.where(pc_ref[1] == q, own_ref[0], p_ref[q]).astype(F32)
            acc = term if acc is None else acc + term
        o_ref[0] = acc

    return pl.pallas_call(
        body, out_shape=jax.ShapeDtypeStruct((2, h, cd), F32),
        grid_spec=pltpu.PrefetchScalarGridSpec(
            num_scalar_prefetch=1, grid=(h // th,),
            in_specs=[pl.BlockSpec((N_CHIP, th, cd), lambda i, pc: (0, i, 0)),
                      pl.BlockSpec((1, th, cd), lambda i, pc: (pc[1], i, 0))],
            out_specs=pl.BlockSpec((1, th, cd), lambda i, pc: (pc[0], i, 0))),
        compiler_params=_params(("parallel",), VMEM_LIMIT), name=name)(place, parts, sums)


def to_segments(a, n_ctx):
    def one(p):
        n = p.shape[0]
        return p.reshape(N_SEG, n // N_SEG, -1).transpose(1, 0, 2).reshape(n, -1)
    return jnp.concatenate([one(a[:n_ctx]), one(a[n_ctx:])], axis=0) if n_ctx else one(a)


def rows_to_segments(a, n_ctx, name):
    n, d = a.shape
    tj = n_ctx // N_SEG
    per_seg = (n - n_ctx) // N_SEG // tj
    assert n_ctx % N_SEG == 0 and (n - n_ctx) % (N_SEG * tj) == 0

    def body(*refs):
        out_ref, slabs = refs[N_SEG:]
        for c in range(d // HEAD_LANES):
            cols = slice(c * HEAD_LANES, (c + 1) * HEAD_LANES)
            for seg in range(N_SEG):
                slabs[c, pl.ds(seg, tj, stride=N_SEG), :] = refs[seg][:, cols]
            out_ref[:, cols] = slabs[c]

    def seg_spec(seg):
        return pl.BlockSpec((tj, d), lambda i: (jnp.where(i == 0, seg, N_SEG + seg * per_seg + i - 1), 0))

    return pl.pallas_call(
        body, out_shape=jax.ShapeDtypeStruct((n, d), a.dtype), grid=(1 + per_seg,),
        in_specs=[seg_spec(seg) for seg in range(N_SEG)], out_specs=pl.BlockSpec((N_SEG * tj, d), lambda i: (i, 0)),
        scratch_shapes=[pltpu.VMEM((d // HEAD_LANES, N_SEG * tj, HEAD_LANES), a.dtype)],
        compiler_params=_params(("parallel",), VMEM_LIMIT), name=name)(*[a] * N_SEG)


def rope_tables(n_ctx, n_lat):
    f32 = np.float32
    rows = n_lat // GRID_W
    row = np.repeat(np.arange(rows), GRID_W).astype(f32)
    col = np.tile(np.arange(GRID_W), rows).astype(f32)
    d = QK_ROPE_DIM // 2
    inv = (f32(1.0) / np.power(f32(ROPE_THETA), np.arange(0, d, 2, dtype=f32) / f32(d))).astype(f32)
    ang = np.concatenate([row[:, None] * inv[None, :], col[:, None] * inv[None, :]], axis=1).astype(f32)
    cos = np.concatenate([np.ones((n_ctx, d), f32), np.cos(ang)], axis=0)
    sin = np.concatenate([np.zeros((n_ctx, d), f32), np.sin(ang)], axis=0)
    q = QK_ROPE_DIM // 4
    T = n_ctx + n_lat
    ones, zeros = np.ones((T, QK_NOPE_DIM), f32), np.zeros((T, QK_NOPE_DIM), f32)
    tail, z8 = np.zeros((T, HEAD_LANES - QK_DIM), f32), np.zeros((T, q), f32)
    cr, cc, sr, sc = cos[:, :q], cos[:, q:], sin[:, :q], sin[:, q:]
    cos_t = np.concatenate([ones, cr, cr, cc, cc, tail], axis=1)
    sin_next = np.concatenate([zeros, -sr, z8, -sc, z8, tail], axis=1)
    sin_prev = np.concatenate([zeros, z8, sr, z8, sc, tail], axis=1)
    return tuple(jnp.asarray(t, F32) for t in (cos_t, sin_next, sin_prev))


def pad_heads(w, used):
    k = w.shape[0]
    return jnp.pad(w.reshape(k, MLA_HEADS, used), ((0, 0), (0, 0), (0, HEAD_LANES - used))).reshape(k, -1)


def unpad_heads(w, used):
    k = w.shape[0]
    return w.reshape(k, MLA_HEADS, HEAD_LANES)[:, :, :used].reshape(k, MLA_HEADS * used)


def rotary_spread():
    lane = np.arange(MLA_HEADS * HEAD_LANES) % HEAD_LANES
    return jnp.asarray(lane[None, :] == (QK_NOPE_DIM + np.arange(QK_ROPE_DIM))[:, None], BF16)


def s5_discretise(a_re, a_im, log_step, b_re, b_im):
    dt = jnp.exp(log_step)[:, None]
    mag = jnp.exp(a_re * dt)
    lb_re = mag * jnp.cos(a_im * dt)
    lb_im = mag * jnp.sin(a_im * dt)
    den = a_re * a_re + a_im * a_im
    nr = lb_re - 1.0
    f_re = ((nr * a_re + lb_im * a_im) / den)[:, None, :]
    f_im = ((lb_im * a_re - nr * a_im) / den)[:, None, :]
    return lb_re, lb_im, f_re * b_re - f_im * b_im, f_re * b_im + f_im * b_re


def s5_block_weights(lb_re, lb_im, bb_re, bb_im, c_re, c_im):
    eye = jnp.eye(GROUPS_PER_BLOCK, dtype=F32)
    lam = jnp.stack([lb_re.reshape(1, S5_LANES), lb_im.reshape(1, S5_LANES)])

    def b_blocks(bb):
        t = bb.reshape(N_BLOCKS, GROUPS_PER_BLOCK, S5_GROUP, S5_STATE)
        return jnp.einsum("bgcp,gh->bgchp", t, eye).reshape(N_BLOCKS, BLK_CH, BLK_ST).astype(BF16)

    def c_blocks(cc):
        t = cc.reshape(N_BLOCKS, GROUPS_PER_BLOCK, S5_GROUP, S5_STATE)
        return jnp.einsum("bgcp,gh->bgphc", t, eye).reshape(N_BLOCKS, BLK_ST, BLK_CH).astype(BF16)

    return lam, b_blocks(bb_re), b_blocks(bb_im), c_blocks(c_re), c_blocks(c_im)


def b_block_diag(db):
    blocks = [db[:, g * S5_GROUP:(g + 1) * S5_GROUP, g * S5_STATE:(g + 1) * S5_STATE]
              for g in range(GROUPS_PER_BLOCK)]
    return jnp.stack(blocks, axis=1).reshape(S5_GROUPS, S5_GROUP, S5_STATE)


def c_block_diag(dc):
    blocks = [dc[:, g * S5_STATE:(g + 1) * S5_STATE, g * S5_GROUP:(g + 1) * S5_GROUP]
              for g in range(GROUPS_PER_BLOCK)]
    return jnp.swapaxes(jnp.stack(blocks, axis=1), -1, -2).reshape(S5_GROUPS, S5_GROUP, S5_STATE)


def conj(a):
    return jnp.stack([a[0], -a[1]])


def _narrow(shape):
    return len(shape) >= 2 and shape[-1] < min(HEAD_LANES, shape[-2])


def _stored(a):
    return jnp.swapaxes(a, -1, -2) if _narrow(a.shape) else a


def _stored_shape(shape):
    return tuple(shape[:-2]) + (shape[-1], shape[-2]) if _narrow(shape) else tuple(shape)


def _from_stored(a, shape):
    return jnp.swapaxes(a, -1, -2) if _narrow(shape) else a


PACK_TILE = 16 * 128


def pack_flat(parts, dtype, multiple=PACK_TILE):
    flat = [p.reshape(-1).astype(dtype) for p in parts]
    sizes = [f.shape[0] for f in flat]
    total = sum(sizes)
    pad = (-total) % multiple
    if pad:
        flat.append(jnp.zeros((pad,), dtype))
    offs = np.cumsum([0] + sizes)[:-1].tolist()
    return jnp.concatenate(flat).reshape(-1, 128), offs


def unpack_flat(buf, offs, shapes):
    flat = buf.reshape(-1)
    return [flat[o:o + int(np.prod(s))].reshape(s) for o, s in zip(offs, shapes)]


def s5_forward(p1, n_ctx, dirs):
    saved = []
    y = None
    ctx_rows, lat_rows = (0, n_ctx), (n_ctx, p1.shape[0] - n_ctx)
    zeros_tile = jnp.zeros((2, N_SEG, S5_LANES), F32)
    zeros_row = jnp.zeros((2, 1, S5_LANES), F32)
    for k, (lam, b_re, b_im, c_re, c_im) in enumerate(dirs):
        rev = k == 1
        last = 0 if rev else N_SEG - 1
        _, _, fin = s5_scan(p1, b_re, b_im, lam, zeros_tile, reverse=rev, rows=ctx_rows, name=f"s5_ctx_finals{k}")
        carry_c = s5_chain(fin, zeros_row, lam, n_ctx // N_SEG, rev, name=f"s5_ctx_chain{k}")
        _, ck_c, fin_c = s5_scan(p1, b_re, b_im, lam, carry_c, reverse=rev, want_ckpt=True, rows=ctx_rows,
                                 name=f"s5_ctx_scan{k}")
        s0 = fin_c[:, last:last + 1, :]
        _, _, fin = s5_scan(p1, b_re, b_im, lam, zeros_tile, reverse=rev, rows=lat_rows, name=f"s5_lat_finals{k}")
        carry_l = s5_chain(fin, s0, lam, lat_rows[1] // N_SEG, rev, name=f"s5_lat_chain{k}")
        y, ck_l, _ = s5_scan(p1, b_re, b_im, lam, carry_l, reverse=rev, c_re=c_re, c_im=c_im, add=y,
                             want_ckpt=True, rows=lat_rows, name=f"s5_lat_scan{k}")
        saved.append((ck_c, ck_l))
    return y, saved


def s5_backward(dy_l, du_extra_l, p1, n_ctx, dirs, saved):
    n_lat = p1.shape[0] - n_ctx
    zeros_tile = jnp.zeros((2, N_SEG, S5_LANES), F32)
    zeros_row = jnp.zeros((2, 1, S5_LANES), F32)
    dy_c = jnp.zeros((n_ctx, D_MODEL), F32)
    du_l, du_c = du_extra_l, None
    grads = []
    for k, (lam, b_re, b_im, c_re, c_im) in enumerate(dirs):
        rev = k == 1
        lam_c = conj(lam)
        ck_c, ck_l = saved[k]
        first = N_SEG - 1 if rev else 0
        _, _, fin = s5_scan(dy_l, c_re, c_im, lam_c, zeros_tile, reverse=not rev, adjoint=True,
                            name=f"s5_lat_adj_finals{k}")
        carry = s5_chain(fin, zeros_row, lam_c, n_lat // N_SEG, not rev, name=f"s5_lat_adj_chain{k}")
        whole = k == len(dirs) - 1
        du_l, dlam_l, dbr_l, dbi_l, dcr_l, dci_l, fin_a = s5_grads(
            dy_l, p1, ck_l, b_re, b_im, c_re, c_im, lam, carry, reverse=rev, add=du_l, u_off=n_ctx,
            du_rows=(p1.shape[0], n_ctx) if whole else None, name=f"s5_lat_grads{k}")
        g0 = fin_a[:, first:first + 1, :]
        carry = s5_chain(zeros_tile, g0, lam_c, n_ctx // N_SEG, not rev, name=f"s5_ctx_adj_chain{k}")
        du_c, dlam_c, dbr_c, dbi_c, _, _, _ = s5_grads(
            dy_c, p1, ck_c, b_re, b_im, c_re, c_im, lam, carry, reverse=rev, add=du_c,
            du_rows=(p1.shape[0], 0) if whole else None, du_into=du_l if whole else None, name=f"s5_ctx_grads{k}")
        dlam = jnp.sum(dlam_l + dlam_c, axis=1)
        grads.append((dlam, b_block_diag(dbr_l + dbr_c), b_block_diag(dbi_l + dbi_c),
                      c_block_diag(dcr_l), c_block_diag(dci_l)))
    return du_c, grads


def local_step(x, ctx, target, mod, w, late=None, reducer=None):
    L, Lc = x.shape[0], ctx.shape[0]
    T = L + Lc
    assert L % Lc == 0 and Lc % (2 * N_SEG) == 0 and L % GRID_W == 0
    D = D_MODEL
    X0 = (ctx, x)

    def mod_of(i, j):
        return mod[i, :, j, :][:, None, :]

    def vec(v):
        return v.reshape(1, 1, -1).astype(F32)

    g0 = vec(w["norm_g"][0])
    H0, p0 = norm_proj(X0, g0, mod_of(0, 1), mod_of(0, 0), w["mla_w_in"], Lc, "l0_norm_in")
    cq = Rows(p0, Q_LORA_RANK, col_blk=D // Q_LORA_RANK)
    ckv = Rows(p0, KV_LORA_RANK, col_blk=(D + Q_LORA_RANK) // KV_LORA_RANK)
    kr = p0[:, D + Q_LORA_RANK + KV_LORA_RANK:D + P0_HEAD]
    qng, kvng = vec(w["mla_q_norm"]), vec(w["mla_kv_norm"])
    tabs = rope_tables(Lc, L)
    spread = rotary_spread()
    if late is not None:
        w = {**w, **late["qkv"](p0)}
    w_uq_p = pad_heads(w["mla_w_uq"], QK_DIM)
    w_ukv3 = w["mla_w_ukv"].reshape(KV_LORA_RANK, MLA_HEADS, QK_NOPE_DIM + V_HEAD_DIM)
    w_kn_p = pad_heads(w_ukv3[:, :, :QK_NOPE_DIM].reshape(KV_LORA_RANK, -1), QK_NOPE_DIM)
    w_v = w_ukv3[:, :, QK_NOPE_DIM:].reshape(KV_LORA_RANK, -1)
    qb, qn = q_heads(cq, qng, w_uq_p, tabs)
    kb, vb, kvn = kv_heads(ckv, kvng, w_kn_p, w_v, kr, spread, tabs)
    o, lse, qs = attn_fwd(qb, kb, vb, Lc)
    if late is not None:
        w = {**w, **late["out"](o)}
    X1, og, out0 = mla_post_fwd(o, p0, X0, mod_of(0, 2), w["mla_w_out"], Lc)

    if late is not None:
        w = {**w, **late["l1"](X1)}
    X1p = rows_to_segments(X1, Lc, "l1_to_segments")
    tgt_p = to_segments(target, 0)
    g1 = vec(w["norm_g"][1])
    H1, p1 = norm_proj(X1p, g1, mod_of(1, 1), mod_of(1, 0), w["s5_w_in"], Lc, "l1_norm_in")
    disc_fn = lambda *a: tuple(zip(*[s5_discretise(a[0][k], a[1][k], a[2][k], a[3][k], a[4][k]) for k in range(2)]))
    disc, disc_vjp = jax.vjp(disc_fn, w["s5_a_re"], w["s5_a_im"], w["s5_log_step"], _stored(w["s5_b_re"]),
                             _stored(w["s5_b_im"]))
    dirs = [s5_block_weights(disc[0][k], disc[1][k], disc[2][k], disc[3][k], w["s5_c_re"][k], w["s5_c_im"][k])
            for k in range(2)]
    y_ssm, s5_saved = s5_forward(p1, Lc, dirs)

    row = lambda v: v.reshape(1, D).astype(F32)
    (lvec, dX2, d_yssm, d_u_act, d_z1, d_fg, d_gt1, d_bg, d_d, gw_glu, gw_out) = s5_tail(
        y_ssm, p1, X1p, tgt_p, Lc, row(w["s5_d"]), row(w["s5_b_glu"]), mod[1, 1:2, 2, :], row(w["final_g"]),
        w["s5_w_glu"], w["s5_w_out"])
    loss = jnp.sum(lvec)
    gw = {"final_g": d_fg.reshape(D), "s5_b_glu": d_bg.reshape(D), "s5_d": d_d.reshape(D),
          "s5_w_glu": gw_glu, "s5_w_out": gw_out}
    dmod = {}

    du_p, s5_g = s5_backward(d_yssm, d_u_act, p1, Lc, dirs, s5_saved)
    d_disc = tuple(tuple(s5_g[k][j - 1].reshape(disc[j][k].shape) if j >= 2 else
                         s5_g[k][0][j].reshape(disc[j][k].shape) for k in range(2)) for j in range(4))
    gw["s5_a_re"], gw["s5_a_im"], gw["s5_log_step"], d_bt_re, d_bt_im = disc_vjp(d_disc)
    gw["s5_b_re"], gw["s5_b_im"] = jnp.swapaxes(d_bt_re, -1, -2), jnp.swapaxes(d_bt_im, -1, -2)
    gw["s5_c_re"] = jnp.stack([s5_g[0][3], s5_g[1][3]])
    gw["s5_c_im"] = jnp.stack([s5_g[0][4], s5_g[1][4]])
    gw["s5_w_in"] = mm_tn(H1, du_p, name="l1_in_dw", b_more=d_z1)
    if reducer is not None:
        g1 = g1 + reducer["l1"][0]({n: gw.pop(n) for n in LAYER1_MATS})[0, 0]
    d_X1, d_g1, d_sc1, d_sh1 = norm_proj_bwd([(du_p, 0, 0), (d_z1, D, Lc)], w["s5_w_in"], X1p, g1, mod_of(1, 1),
                                             mod_of(1, 0), dX2, Lc, "l1_norm_in_bwd", dx_from_segments=True)
    d_gt1_full = jnp.concatenate([jnp.zeros((1, 1, D), F32), d_gt1[None]], axis=0)
    dmod[1] = (d_sh1, d_sc1, d_gt1_full)

    d_o, d_z0, d_gt0, gw["mla_w_out"] = mla_post_bwd(d_X1, out0, og, o, p0, mod_of(0, 2), w["mla_w_out"], Lc)
    if reducer is not None:
        started = reducer["l1"][1](d_o)[0, 0] + reducer["out"][0]({"mla_w_out": gw.pop("mla_w_out")})[0, 0]
        tabs = (tabs[0] + started,) + tabs[1:]
    d_q, dos = attn_bwd_dq(qb, kb, vb, o, d_o, lse, tabs, Lc)
    dk_p, d_v = attn_bwd_dkv(qs, kb, vb, dos, Lc)
    if reducer is not None:
        tabs = (tabs[0] + reducer["out"][1](d_q)[0, 0],) + tabs[1:]
    d_k, d_kr = heads_unrope(dk_p, tabs, jnp.pad(spread, ((0, HEAD_LANES - QK_ROPE_DIM), (0, 0))),
                             name="l0_k_unrope")
    d_qn = mm_nt(d_q, w_uq_p, name="l0_uq_dx")
    gw["mla_w_uq"] = unpad_heads(mm_tn(qn, d_q, name="l0_uq_dw"), QK_DIM)
    d_kvn = mm_nt(d_k, w_kn_p, name="l0_ukn_dx") + mm_nt(d_v, w_v, name="l0_uv_dx")
    dw_kn = unpad_heads(mm_tn(kvn, d_k, name="l0_ukn_dw"), QK_NOPE_DIM).reshape(KV_LORA_RANK, MLA_HEADS, QK_NOPE_DIM)
    dw_v = mm_tn(kvn, d_v, name="l0_uv_dw").reshape(KV_LORA_RANK, MLA_HEADS, V_HEAD_DIM)
    gw["mla_w_ukv"] = jnp.concatenate([dw_kn, dw_v], axis=-1).reshape(KV_LORA_RANK, -1)
    d_cq, d_qng = rowwise_bwd(f_rms, [cq], [qng], [d_qn], [0], [0], T, 0, "l0_qnorm_bwd")
    d_ckv, d_kvng = rowwise_bwd(f_rms, [ckv], [kvng], [d_kvn], [0], [0], T, 0, "l0_kvnorm_bwd")
    gw["mla_q_norm"] = d_qng.reshape(-1)
    gw["mla_kv_norm"] = d_kvng.reshape(-1)
    o_cq, o_ckv = D, D + Q_LORA_RANK
    o_kr = o_ckv + KV_LORA_RANK
    d_head = jnp.concatenate([d_cq, d_ckv, d_kr], axis=1)
    gw["mla_w_in"] = jnp.concatenate([mm_tn(H0, d_head, name="l0_in_dw_head")[:, :P0_HEAD],
                                      mm_tn(H0, d_z0, name="l0_in_dw_z")], axis=1)
    dx, d_g0, d_sc0, d_sh0 = norm_proj_bwd(
        [(d_z0, 0, 0), (d_cq, o_cq, 0), (d_ckv, o_ckv, 0), (d_kr, o_kr, 0)], w["mla_w_in"], X0, g0, mod_of(0, 1),
        mod_of(0, 0), d_X1, Lc, "l0_norm_in_bwd", latent_dx_only=True)
    dmod[0] = (d_sh0, d_sc0, d_gt0)
    gw["norm_g"] = jnp.stack([d_g0.reshape(D), d_g1.reshape(D)])
    dmod_arr = jnp.stack([jnp.stack([dmod[i][j][:, 0, :] for j in range(3)], axis=1) for i in range(2)])
    ready = {**reducer["l1"][2](dx), **reducer["out"][2](dx)} if reducer is not None else {}
    return loss, dx, dmod_arr, gw, ready


SHARDED = {
    "mla_w_in": 1, "mla_w_uq": 1, "mla_w_ukv": 1, "mla_w_out": 0,
    "s5_w_in": 1, "s5_w_glu": 0, "s5_w_out": 0, "s5_d": 0, "s5_b_glu": 0,
}
SHARDED_MATS = ["mla_w_in", "mla_w_uq", "mla_w_ukv", "mla_w_out", "s5_w_in", "s5_w_glu", "s5_w_out"]
SHARDED_VECS = ["s5_d", "s5_b_glu"]
REPLICATED = ["norm_g", "mla_q_norm", "mla_kv_norm", "s5_a_re", "s5_a_im", "s5_log_step", "s5_b_re", "s5_b_im",
              "s5_c_re", "s5_c_im", "final_g"]
WEIGHT_ORDER = ["c_ctx", "ada_w", "ada_b", "norm_g", "mla_w_in", "mla_q_norm", "mla_w_uq", "mla_kv_norm", "mla_w_ukv",
                "mla_w_out", "s5_w_in", "s5_a_re", "s5_a_im", "s5_log_step", "s5_b_re", "s5_b_im", "s5_c_re", "s5_c_im",
                "s5_d", "s5_w_glu", "s5_b_glu", "s5_w_out", "final_g"]


P0_HEAD = Q_LORA_RANK + KV_LORA_RANK + QK_ROPE_DIM


P0_WIDTH = 1536


def w_in_to_kernel_order(w):
    pad = jnp.zeros((w.shape[0], P0_WIDTH - w.shape[1]), w.dtype)
    return jnp.concatenate([w[:, P0_HEAD:], w[:, :P0_HEAD], pad], axis=1)


LAYER0_MATS = ["mla_w_in", "mla_w_uq", "mla_w_ukv", "mla_w_out"]
LAYER1_MATS = ["s5_w_in", "s5_w_glu", "s5_w_out"]


def _whole_matrices(names, own_blocks, gathered):
    chip = _chip_index(_coords())
    full = {}
    for n, own, o in zip(names, own_blocks, gathered):
        slot = lax.broadcasted_iota(jnp.int32, (N_CHIP, 1, 1), 0)
        o = jnp.where(slot == chip, own[None], o.reshape((N_CHIP,) + own.shape))
        full[n] = o.reshape(-1, o.shape[-1]) if SHARDED[n] == 0 else o.transpose(1, 0, 2).reshape(o.shape[1], -1)
    return full


FIRST_MATS = ["mla_w_in"]
LATER_GROUPS = {"qkv": ["mla_w_uq", "mla_w_ukv"], "out": ["mla_w_out"], "l1": LAYER1_MATS}


def gather_weights(ws):
    mats = [ws[n].astype(BF16) for n in FIRST_MATS]
    full = _whole_matrices(FIRST_MATS, mats, gather_halves(mats, "gather_weights"))
    full["mla_w_in"] = w_in_to_kernel_order(full["mla_w_in"])
    return full


def gather_weights_behind(ws, after):
    token, finish = 0.0, {}
    for group, names in LATER_GROUPS.items():
        mats = [ws[n].astype(BF16) for n in names]
        flight, tok = exchange_start(
            mats, [jax.ShapeDtypeStruct((N_CHIP,) + m.shape, m.dtype) for m in mats],
            [(f, a, lambda me, peer: None, a, lambda s: (_chip_index(s),))
             for a in range(len(mats)) for f in CHIP_FLIPS], f"gather_{group}_start", after=after)
        after = [tok]
        token = token + tok[0, 0]

        def finish_group(after_work, group=group, names=names, flight=flight):
            own, got = exchange_wait(flight, after_work, f"gather_{group}_wait")
            return _whole_matrices(names, own, got)

        finish[group] = finish_group
    return token, finish


def _grad_slots(gw, names):
    slots = []
    for n in names:
        g = gw[n]
        if SHARDED[n] == 0:
            slots.append(g.reshape(N_CHIP, 2, g.shape[0] // (2 * N_CHIP), g.shape[1]))
        else:
            k, n4 = g.shape
            slots.append(g.reshape(k, N_CHIP, n4 // N_CHIP).transpose(1, 0, 2)
                         .reshape(N_CHIP, 2, k // 2, n4 // N_CHIP))
    return slots


def _to_sibling_half(count):
    return [(CORE_FLIP, i, lambda me, peer: (slice(None), 1 - me[2]), i, lambda s: None) for i in range(count)]


def _to_chips(count):
    return [(f, i, lambda me, peer: (_chip_index(peer),), i, lambda s: (_chip_index(s),))
            for i in range(count) for f in CHIP_FLIPS]


def _place():
    me = _coords()
    return jnp.stack([me[2], _chip_index(me)]).astype(jnp.int32)


def reduce_behind(names, tag):
    state = {}
    count = len(names)

    def begin(gw):
        slots = _grad_slots(gw, names)
        lands = [jax.ShapeDtypeStruct((N_CHIP,) + s.shape[2:], F32) for s in slots]
        state["in"], token = exchange_start(slots, lands, _to_sibling_half(count), f"grads_{tag}_swap_in_start")
        return token

    def middle(after):
        slots, got = exchange_wait(state["in"], after, f"grads_{tag}_swap_in_wait")
        place = _place()
        sums = [pair_add(s, g, place[:1], BF16, f"grads_pair_{n}") for n, s, g in zip(names, slots, got)]
        lands = [jax.ShapeDtypeStruct(s.shape, s.dtype) for s in sums]
        state["out"], token = exchange_start(sums, lands, _to_chips(count), f"grads_{tag}_scatter_start")
        return token

    def end(after):
        sums, parts = exchange_wait(state["out"], after, f"grads_{tag}_scatter_wait")
        place = _place()
        return {n: sum_chips(p, s, place, f"grads_sum_{n}") for n, p, s in zip(names, parts, sums)}

    return begin, middle, end


def reduce_gradients(gw, ready_halves, also=None):
    me = _coords()
    place = _place()
    mat_names = [n for n in SHARDED_MATS if n not in ready_halves]
    slots = dict(zip(mat_names, _grad_slots(gw, mat_names)))
    gw = {**gw, **(also or {})}
    small_names = REPLICATED + SHARDED_VECS + list(also or {})
    small, small_offs = pack_flat([_stored(gw[n]).astype(F32) for n in small_names], F32, N_CHIP * 32 * 128)
    slots["small"] = small.reshape(N_CHIP, 2, -1, 128)
    names = list(slots)
    count = len(names)
    got = exchange([slots[n] for n in names],
                   [jax.ShapeDtypeStruct((N_CHIP,) + slots[n].shape[2:], F32) for n in names],
                   _to_sibling_half(count), [], "grads_swap_in")
    sums = [pair_add(slots[n], g, place[:1], F32 if n == "small" else BF16, f"grads_pair_{n}")
            for n, g in zip(names, got)]
    parts = exchange(sums, [jax.ShapeDtypeStruct(s.shape, s.dtype) for s in sums], _to_chips(count), [],
                     "grads_scatter")
    halves = {n: sum_chips(p, s, place, f"grads_sum_{n}") for n, p, s in zip(names, parts, sums)}
    halves.update(ready_halves)
    all_names = list(halves)
    fulls = exchange(
        [halves[n] for n in all_names], [jax.ShapeDtypeStruct(halves[n].shape, F32) for n in all_names],
        [(CORE_FLIP, i, lambda me, peer: (me[2],), i, lambda s: (s[2],)) for i in range(len(all_names))], [],
        "grads_swap_out", aliases={i: i for i in range(len(all_names))})
    out = {n: f.reshape(-1, f.shape[-1]) for n, f in zip(all_names, fulls)}
    quarter = out.pop("small")
    gather, token = exchange_start(
        [quarter], [jax.ShapeDtypeStruct((N_CHIP,) + quarter.shape, F32)],
        [(f, 0, lambda me, peer: None, 0, lambda s: (_chip_index(s),)) for f in CHIP_FLIPS],
        "grads_gather_small_start")

    def finish_small(after):
        (own,), (got_small,) = exchange_wait(gather, after, "grads_gather_small_wait")
        slot = lax.broadcasted_iota(jnp.int32, (N_CHIP, 1, 1), 0)
        small_all = jnp.where(slot == _chip_index(me), own[None], got_small)
        vals = unpack_flat(small_all, small_offs, [_stored_shape(gw[n].shape) for n in small_names])
        res = {}
        for n, v in zip(small_names, vals):
            v = _from_stored(v, gw[n].shape)
            if n in SHARDED_VECS:
                size = v.shape[0] // N_CHIP
                v = lax.dynamic_slice_in_dim(v, _chip_index(me) * size, size)
            res[n] = v
        return res

    return out, finish_small, token


def kernel(x, c, ctx, c_ctx, ada_w, ada_b, norm_g, mla_w_in, mla_q_norm, mla_w_uq, mla_kv_norm, mla_w_ukv, mla_w_out, s5_w_in, s5_a_re, s5_a_im, s5_log_step, s5_b_re, s5_b_im, s5_c_re, s5_c_im, s5_d, s5_w_glu, s5_b_glu, s5_w_out, final_g, loss_target, m_c_ctx, m_ada_w, m_ada_b, m_norm_g, m_mla_w_in, m_mla_q_norm, m_mla_w_uq, m_mla_kv_norm, m_mla_w_ukv, m_mla_w_out, m_s5_w_in, m_s5_a_re, m_s5_a_im, m_s5_log_step, m_s5_b_re, m_s5_b_im, m_s5_c_re, m_s5_c_im, m_s5_d, m_s5_w_glu, m_s5_b_glu, m_s5_w_out, m_final_g, v_c_ctx, v_ada_w, v_ada_b, v_norm_g, v_mla_w_in, v_mla_q_norm, v_mla_w_uq, v_mla_kv_norm, v_mla_w_ukv, v_mla_w_out, v_s5_w_in, v_s5_a_re, v_s5_a_im, v_s5_log_step, v_s5_b_re, v_s5_b_im, v_s5_c_re, v_s5_c_im, v_s5_d, v_s5_w_glu, v_s5_b_glu, v_s5_w_out, v_final_g):
    args = dict(locals())
    weights = {n: args[n] for n in WEIGHT_ORDER}
    D = D_MODEL
    xi, yi, ci = _coords()
    chip = 2 * xi + yi
    me = 4 * xi + 2 * yi + ci
    n_col = ada_w.shape[2]

    c_all = allgather_devices(jnp.pad(c, ((0, 7), (0, 0))), "gather_c")[:, 0, :]
    cond = jnp.concatenate([c_all, jnp.broadcast_to(c_ctx[None], (8, D))], axis=0)
    (s_cond,) = rowwise_fwd(lambda v: (_silu(v),), [cond], [], [D], [F32], 16, 0, "cond_silu")
    ada_rows = ada_w.reshape(2 * D, n_col)
    mod_cols = jnp.stack([mm_nn(s_cond, ada_rows, name=f"mod_proj{i}", b_blk=i) for i in range(2)])
    vec_tiles = [jnp.pad(weights[n][0].reshape(-1, 128), ((0, 6), (0, 0))) for n in SHARDED_VECS]
    mod_all, *vec_all = allgather_chips([mod_cols] + vec_tiles, "gather_mod")
    mod_all = mod_all.transpose(1, 2, 0, 3).reshape(2, 16, 3 * D) + ada_b[:, None, :]
    mine = lax.broadcasted_iota(jnp.int32, (1, 16, 1), 1) == me
    mod_l = jnp.sum(jnp.where(mine, mod_all, 0.0), axis=1)
    mod_c = mod_all[:, 8, :]
    mod = jnp.stack([mod_c.reshape(2, 3, D), mod_l.reshape(2, 3, D)], axis=1)

    w = gather_weights({n: weights[n][0] for n in FIRST_MATS})
    token, late = gather_weights_behind({n: weights[n][0] for n in SHARDED_MATS}, [w["mla_w_in"], mod])
    for n, v in zip(SHARDED_VECS, vec_all):
        w[n] = v[:, :2, :].reshape(-1)
    for n in ["norm_g", "final_g"]:
        w[n] = weights[n]
    for n in ["mla_q_norm", "mla_kv_norm", "s5_a_re", "s5_a_im", "s5_log_step", "s5_b_re", "s5_b_im",
              "s5_c_re", "s5_c_im"]:
        w[n] = weights[n][0]

    reducer = {"l1": reduce_behind(LAYER1_MATS, "l1"), "out": reduce_behind(["mla_w_out"], "out")}
    loss_me, dx, dmod, gw, ready = local_step(x[0], ctx[0], loss_target[0], mod + token, w, late,
                                              reducer)

    dmod_rows, loss_all = _gather([dmod.reshape(2, 2, 3 * D), jnp.broadcast_to(loss_me, (8, 128))],
                                  ALL_FLIPS, _dev_index, N_DEV, "gather_dmod")
    loss = functools.reduce(lambda s, d: s + loss_all[d, 0, 0], range(1, N_DEV), loss_all[0, 0, 0])
    dm = jnp.concatenate([dmod_rows[:, :, 1, :], dmod_rows[:, :, 0, :]], axis=0).transpose(1, 0, 2)
    g_ada_b = jnp.sum(dm, axis=1)
    dm_cols = lax.dynamic_slice_in_dim(dm, chip * n_col, n_col, axis=2)
    g_ada_w = jnp.stack([mm_tn(s_cond, dm_cols[i], name=f"mod_proj_dw{i}") for i in range(2)])
    dmc = jnp.sum(dm_cols[:, 8:, :], axis=1)
    dmc8 = jnp.broadcast_to(dmc[:, None, :], (2, 8, n_col))
    g_sc = (mm_nt(dmc8[0], ada_rows, name="mod_proj_dx0", b_rows=D, b_blk=0)[0]
            + mm_nt(dmc8[1], ada_rows, name="mod_proj_dx1", b_rows=D, b_blk=1)[0])
    g_silu_part = jnp.where(ci == 0, g_sc, 0.0)

    grads = {"ada_w": g_ada_w, "ada_b": g_ada_b}
    deltas, new_m, new_v = {}, {}, {}
    small = [n for n in WEIGHT_ORDER if weights[n].size < 50000]

    def update(n, after=None):
        shp = weights[n].shape
        rows = lambda a: _stored(a.reshape(shp)).reshape(-1, _stored_shape(shp)[-1])
        back = lambda a: _from_stored(a.reshape(_stored_shape(shp)), shp)
        d_, m_, v_ = adamw(rows(weights[n]), rows(grads[n]), rows(args["m_" + n]), rows(args["v_" + n]),
                           name=f"adamw_{n}", after=after)
        deltas[n], new_m[n], new_v[n] = back(d_), back(m_), back(v_)

    red, finish_small, small_started = reduce_gradients(gw, ready, {"silu_c_ctx": g_silu_part})
    update("ada_w", small_started)
    for n in SHARDED_MATS:
        grads[n] = red[n].reshape(weights[n].shape)
        update(n, small_started)
    red_small = finish_small([deltas[n] for n in ["ada_w"] + SHARDED_MATS])
    for n in REPLICATED + SHARDED_VECS:
        grads[n] = red_small[n].reshape(weights[n].shape)
    (g_c_ctx,) = rowwise_bwd(lambda v: (_silu(v),), [jnp.broadcast_to(c_ctx[None], (8, D))], [],
                             [jnp.broadcast_to(red_small["silu_c_ctx"][None], (8, D))], [0], [], 8, 0, "cond_silu_bwd")
    grads["c_ctx"] = g_c_ctx[0]
    for n in WEIGHT_ORDER:
        if n not in small and n not in deltas:
            update(n)
    packs = []
    offs = None
    for src in (weights, grads, {n: args["m_" + n] for n in small}, {n: args["v_" + n] for n in small}):
        buf, offs = pack_flat([src[n] for n in small], F32)
        packs.append(buf)
    outs = adamw(*packs, name="adamw_small")
    for res, dst in zip(outs, (deltas, new_m, new_v)):
        for n, val in zip(small, unpack_flat(res, offs, [weights[n].shape for n in small])):
            dst[n] = val

    return (loss, dx[None], *[grads[n] for n in WEIGHT_ORDER], *[deltas[n] for n in WEIGHT_ORDER],
            *[new_m[n] for n in WEIGHT_ORDER], *[new_v[n] for n in WEIGHT_ORDER])
```

```python
import functools
import math

import jax
import jax.numpy as jnp
import numpy as np
from jax import lax
from jax.experimental import pallas as pl
from jax.experimental.pallas import tpu as pltpu

F32 = jnp.float32
BF16 = jnp.bfloat16

D_MODEL = 1024
GRID_W = 64
EPS = 1e-6
MLA_HEADS = 16
QK_NOPE_DIM = 64
QK_ROPE_DIM = 32
V_HEAD_DIM = 64
Q_LORA_RANK = 256
KV_LORA_RANK = 128
QK_DIM = QK_NOPE_DIM + QK_ROPE_DIM
SOFTMAX_SCALE = QK_DIM ** -0.5
ROPE_THETA = 10000.0
S5_GROUP = 16
S5_GROUPS = D_MODEL // S5_GROUP
S5_STATE = 64
S5_LANES = S5_GROUPS * S5_STATE
N_SEG = 8
GROUPS_PER_BLOCK = 8
N_BLOCKS = S5_GROUPS // GROUPS_PER_BLOCK
BLK_CH = GROUPS_PER_BLOCK * S5_GROUP
BLK_ST = GROUPS_PER_BLOCK * S5_STATE

ADAM_LR = 0.001
ADAM_B1 = 0.9
ADAM_B2 = 0.999
ADAM_EPS = 1e-08
ADAM_WD = 0.01
ADAM_STEP = 10

N_DEV = 8
N_CHIP = 4
MESH = pl.DeviceIdType.MESH
VMEM_LIMIT = 52 * 1024 * 1024
ROW_TILE = 256


def _params(sem=None, vmem=None):
    return pltpu.CompilerParams(dimension_semantics=sem, vmem_limit_bytes=vmem)


def mm_nn(a, b, out_dtype=F32, name="mm_nn", b_blk=0):
    M, K = a.shape
    N = b.shape[1]
    tm = math.gcd(ROW_TILE, M)

    def body(a_ref, b_ref, o_ref):
        o_ref[...] = jnp.dot(a_ref[...].astype(BF16), b_ref[...].astype(BF16),
                             preferred_element_type=F32).astype(o_ref.dtype)

    return pl.pallas_call(
        body, out_shape=jax.ShapeDtypeStruct((M, N), out_dtype), grid=(M // tm,),
        in_specs=[pl.BlockSpec((tm, K), lambda i: (i, 0)), pl.BlockSpec((K, N), lambda i: (b_blk, 0))],
        out_specs=pl.BlockSpec((tm, N), lambda i: (i, 0)),
        compiler_params=_params(("parallel",), VMEM_LIMIT), name=name)(a, b)


def mm_nt(a, b, out_dtype=F32, name="mm_nt", b_rows=None, b_blk=0):
    M, N = a.shape
    K = b.shape[0] if b_rows is None else b_rows
    tm = math.gcd(ROW_TILE, M)

    def body(a_ref, b_ref, o_ref):
        o_ref[...] = lax.dot_general(a_ref[...].astype(BF16), b_ref[...].astype(BF16),
                                     (((1,), (1,)), ((), ())),
                                     preferred_element_type=F32).astype(o_ref.dtype)

    return pl.pallas_call(
        body, out_shape=jax.ShapeDtypeStruct((M, K), out_dtype), grid=(M // tm,),
        in_specs=[pl.BlockSpec((tm, N), lambda i: (i, 0)), pl.BlockSpec((K, N), lambda i: (b_blk, 0))],
        out_specs=pl.BlockSpec((tm, K), lambda i: (i, 0)),
        compiler_params=_params(("parallel",), VMEM_LIMIT), name=name)(a, b)


def mm_tn(a, b, name="mm_tn", b_more=None):
    M, N = b.shape
    K = a.shape[1]
    tn = math.gcd(512, N) if N % 128 == 0 and N > 512 else N
    bs = [b] if b_more is None else [b, b_more]
    assert all(x.shape[1] == N for x in bs)
    nb = N // tn

    def body(a_ref, *refs):
        o_ref = refs[-1]
        for k, b_ref in enumerate(refs[:-1]):
            def product(b_ref=b_ref):
                o_ref[...] = lax.dot_general(a_ref[a.shape[0] - b_ref.shape[0]:, :].astype(BF16),
                                             b_ref[...].astype(BF16), (((0,), (0,)), ((), ())),
                                             preferred_element_type=F32)
            if len(bs) == 1:
                product()
            else:
                pl.when(pl.program_id(0) // nb == k)(product)

    def b_spec(k, rows):
        return pl.BlockSpec((rows, tn), lambda j: (0, jnp.clip(j - k * nb, 0, nb - 1)))

    return pl.pallas_call(
        body, out_shape=jax.ShapeDtypeStruct((K, N * len(bs)), F32), grid=(nb * len(bs),),
        in_specs=[pl.BlockSpec(a.shape, lambda j: (0, 0))] + [b_spec(k, x.shape[0]) for k, x in enumerate(bs)],
        out_specs=pl.BlockSpec((K, tn), lambda j: (0, j)),
        compiler_params=_params(("parallel",), VMEM_LIMIT), name=name)(a, *bs)


class Rows:
    def __init__(self, arr, width=None, row_off=0, col_blk=0):
        self.arr = arr
        self.width = arr.shape[1] if width is None else width
        self.row_off = row_off
        self.col_blk = col_blk

    def spec(self, tm):
        ro, cb = self.row_off // tm, self.col_blk
        return pl.BlockSpec((tm, self.width), lambda i: (i + ro, cb))


def _as_rows(x):
    return x if isinstance(x, Rows) else Rows(x)


def _row_tile(n_rows, n_ctx_rows, rows):
    tm = math.gcd(ROW_TILE, n_rows, n_ctx_rows)
    for r in rows:
        tm = math.gcd(tm, r.row_off)
    return tm


def _bc_spec(arr, n_ctx_blocks):
    g, _, d = arr.shape
    if g == 1:
        return pl.BlockSpec((1, 1, d), lambda i: (0, 0, 0))
    return pl.BlockSpec((1, 1, d), lambda i: ((i >= n_ctx_blocks).astype(jnp.int32), 0, 0))


def rowwise_fwd(fn, rows, bcs, out_dims, out_dtypes, n_rows, n_ctx_rows, name):
    rows = [_as_rows(r) for r in rows]
    tm = _row_tile(n_rows, n_ctx_rows, rows)
    ncb = n_ctx_rows // tm
    nr, nb = len(rows), len(bcs)

    def body(*refs):
        vals = [r[...].astype(F32) for r in refs[:nr]] + [b[0].astype(F32) for b in refs[nr:nr + nb]]
        outs = fn(*vals)
        for o_ref, v in zip(refs[nr + nb:], outs):
            o_ref[...] = v.astype(o_ref.dtype)

    outs = pl.pallas_call(
        body,
        out_shape=[jax.ShapeDtypeStruct((n_rows, d), dt) for d, dt in zip(out_dims, out_dtypes)],
        grid=(n_rows // tm,),
        in_specs=[r.spec(tm) for r in rows] + [_bc_spec(b, ncb) for b in bcs],
        out_specs=[pl.BlockSpec((tm, d), lambda i: (i, 0)) for d in out_dims],
        compiler_params=_params(("parallel",), VMEM_LIMIT), name=name)(*[r.arr for r in rows], *bcs)
    return outs


def rowwise_bwd(fn, rows, bcs, cts, diff_rows, diff_bcs, n_rows, n_ctx_rows, name, ct_extra=None, lat_add=None):
    rows = [_as_rows(r) for r in rows]
    cts = [_as_rows(c) for c in cts]
    extra = [_as_rows(ct_extra)] if ct_extra is not None else []
    tm = _row_tile(n_rows, n_ctx_rows, rows + cts + extra)
    ncb = n_ctx_rows // tm
    nr, nb, nc = len(rows), len(bcs), len(cts)
    ndr, ndb = len(diff_rows), len(diff_bcs)
    n_in = nr + nb + nc + len(extra) + (lat_add is not None)

    def body(*refs):
        i = pl.program_id(0)
        rvals = [r[...].astype(F32) for r in refs[:nr]]
        bvals = [b[0].astype(F32) for b in refs[nr:nr + nb]]
        cvals = [c[...].astype(F32) for c in refs[nr + nb:nr + nb + nc]]
        if extra:
            cvals[0] = cvals[0] + refs[nr + nb + nc][...].astype(F32)
        outs = refs[n_in:]

        def f(*d):
            rv, bv = list(rvals), list(bvals)
            for k, idx in enumerate(diff_rows):
                rv[idx] = d[k]
            for k, idx in enumerate(diff_bcs):
                bv[idx] = d[ndr + k]
            return tuple(fn(*rv, *bv))

        primals = [rvals[k] for k in diff_rows] + [bvals[k] for k in diff_bcs]
        _, vjp = jax.vjp(f, *primals)
        grads = list(vjp(tuple(cvals)))
        if lat_add is not None:
            add = refs[n_in - 1][...]
            grads[0] = grads[0] + (add if lat_add.shape[0] == n_rows else jnp.where(i >= ncb, add, 0.0))
        for k in range(ndr):
            outs[k][...] = grads[k].astype(outs[k].dtype)
        for k, idx in enumerate(diff_bcs):
            o_ref = outs[ndr + k]
            first = (i == 0)
            if bcs[idx].shape[0] == 2:
                first = first | (i == ncb)

            @pl.when(first)
            def _(o_ref=o_ref):
                o_ref[...] = jnp.zeros_like(o_ref)

            o_ref[0] += grads[ndr + k]

    out_shape = [jax.ShapeDtypeStruct((n_rows, rows[k].width), F32) for k in diff_rows]
    out_shape += [jax.ShapeDtypeStruct(bcs[k].shape, F32) for k in diff_bcs]
    out_specs = [pl.BlockSpec((tm, rows[k].width), lambda i: (i, 0)) for k in diff_rows]
    out_specs += [_bc_spec(bcs[k], ncb) for k in diff_bcs]
    ins = [r.arr for r in rows] + list(bcs) + [c.arr for c in cts + extra]
    in_specs = [r.spec(tm) for r in rows] + [_bc_spec(b, ncb) for b in bcs] + [c.spec(tm) for c in cts + extra]
    if lat_add is not None:
        ins.append(lat_add)
        skip = ncb if lat_add.shape[0] != n_rows else 0
        in_specs.append(pl.BlockSpec((tm, lat_add.shape[1]), lambda i: (jnp.maximum(i - skip, 0), 0)))
    outs = pl.pallas_call(
        body, out_shape=out_shape, grid=(n_rows // tm,), in_specs=in_specs, out_specs=out_specs,
        compiler_params=_params(("arbitrary",), VMEM_LIMIT), name=name)(*ins)
    return outs


def _rms(x):
    return x * lax.rsqrt(jnp.mean(x * x, axis=-1, keepdims=True) + EPS)


def _sigmoid(x):
    return 0.5 * (jnp.tanh(0.5 * x) + 1.0)


def _silu(x):
    return x * _sigmoid(x)


def _gelu_tanh(x):
    return 0.5 * x * (1.0 + jnp.tanh(math.sqrt(2.0 / math.pi) * (x + 0.044715 * (x * x * x))))


def f_norm_mod(x, g, sc, sh):
    return ((_rms(x) * g) * (1.0 + sc) + sh,)


def f_rms(x, g):
    return (_rms(x) * g,)


def f_gate(o, z):
    return (o * _silu(z),)


def f_s5_act(y, u, d):
    return (_gelu_tanh(y + d * u),)


def f_s5_glu(ya, gl, z, b):
    return (ya * _sigmoid(gl + b) * _silu(z),)


def _as_parts(x, n_ctx):
    xs = x if isinstance(x, tuple) else (x,)
    assert len(xs) == 1 or xs[0].shape[0] == n_ctx
    return xs, sum(p.shape[0] for p in xs)


def _parts_specs(xs, tm, ncb):
    d = xs[0].shape[1]
    if len(xs) == 1:
        return [pl.BlockSpec((tm, d), lambda i: (i, 0))]
    return [pl.BlockSpec((tm, d), lambda i: (jnp.minimum(i, ncb - 1), 0)),
            pl.BlockSpec((tm, d), lambda i: (jnp.maximum(i - ncb, 0), 0))]


def _parts_tile(x_refs, ncb):
    if len(x_refs) == 1:
        return x_refs[0][...]
    return jnp.where(pl.program_id(0) < ncb, x_refs[0][...], x_refs[1][...])


def norm_proj(x, g, sc, sh, w, n_ctx, name):
    xs, n = _as_parts(x, n_ctx)
    d = xs[0].shape[1]
    nw = w.shape[1]
    tm = math.gcd(ROW_TILE, n, n_ctx)
    ncb = n_ctx // tm
    nx = len(xs)

    def body(*refs):
        g_ref, sc_ref, sh_ref, w_ref, h_ref, p_ref = refs[nx:]
        h = f_norm_mod(_parts_tile(refs[:nx], ncb), g_ref[0], sc_ref[0], sh_ref[0])[0].astype(BF16)
        h_ref[...] = h
        p_ref[...] = jnp.dot(h, w_ref[...], preferred_element_type=F32)

    row = pl.BlockSpec((tm, d), lambda i: (i, 0))
    return pl.pallas_call(
        body, out_shape=[jax.ShapeDtypeStruct((n, d), BF16), jax.ShapeDtypeStruct((n, nw), F32)], grid=(n // tm,),
        in_specs=_parts_specs(xs, tm, ncb) + [_bc_spec(g, ncb), _bc_spec(sc, ncb), _bc_spec(sh, ncb),
                                              pl.BlockSpec(w.shape, lambda i: (0, 0))],
        out_specs=[row, pl.BlockSpec((tm, nw), lambda i: (i, 0))],
        compiler_params=_params(("parallel",), VMEM_LIMIT), name=name)(*xs, g, sc, sh, w)


def norm_proj_bwd(terms, w, x, g, sc, sh, add, n_ctx, name, latent_dx_only=False, dx_from_segments=False):
    xs, n = _as_parts(x, n_ctx)
    adds = add if isinstance(add, tuple) else (add,)
    d = xs[0].shape[1]
    tm = math.gcd(ROW_TILE, n, n_ctx, *[t[2] for t in terms])
    ncb = n_ctx // tm
    nt, nx, na = len(terms), len(xs), len(adds)
    add_skip = ncb if na == 1 and add.shape[0] != n else 0
    n_dx = 2 if dx_from_segments else 1
    tj = tm // N_SEG
    assert not dx_from_segments or (ncb == 1 and not latent_dx_only)

    def body(*refs):
        i = pl.program_id(0)
        a_refs = refs[:nt]
        w_ref, x_refs = refs[nt], refs[nt + 1:nt + 1 + nx]
        g_ref, sc_ref, sh_ref = refs[nt + 1 + nx:nt + 4 + nx]
        add_refs = refs[nt + 4 + nx:nt + 4 + nx + na]
        outs = refs[nt + 4 + nx + na:]
        dx_refs, (dg_ref, dsc_ref, dsh_ref) = outs[:n_dx], outs[n_dx:n_dx + 3]
        d_h = None
        for a_ref, (a, off, first) in zip(a_refs, terms):
            part = lax.dot_general(a_ref[...].astype(BF16), w_ref[:, off:off + a.shape[1]], NT_DIMS,
                                   preferred_element_type=F32)
            if first:
                part = jnp.where(i >= first // tm, part, 0.0)
            d_h = part if d_h is None else d_h + part
        _, vjp = jax.vjp(lambda x_, g_, sc_, sh_: f_norm_mod(x_, g_, sc_, sh_), _parts_tile(x_refs, ncb), g_ref[0],
                         sc_ref[0], sh_ref[0])
        d_x, d_g, d_sc, d_sh = vjp((d_h,))
        extra = _parts_tile(add_refs, ncb)
        d_x = d_x + (extra if add_skip == 0 else jnp.where(i >= ncb, extra, 0.0))
        if dx_from_segments:
            slabs = outs[n_dx + 3]
            for c in range(d // HEAD_LANES):
                slabs[c] = d_x[:, c * HEAD_LANES:(c + 1) * HEAD_LANES]
            for k, here in enumerate([i < ncb, i >= ncb]):
                @pl.when(here)
                def _(k=k):
                    for c in range(d // HEAD_LANES):
                        for seg in range(N_SEG):
                            dx_refs[k][seg, :, c * HEAD_LANES:(c + 1) * HEAD_LANES] = (
                                slabs[c, pl.ds(seg, tj, stride=N_SEG), :])
        else:
            dx_refs[0][...] = d_x

        @pl.when(i == 0)
        def _():
            dg_ref[...] = jnp.zeros_like(dg_ref)

        @pl.when((i == 0) | (i == ncb))
        def _():
            dsc_ref[...] = jnp.zeros_like(dsc_ref)
            dsh_ref[...] = jnp.zeros_like(dsh_ref)

        dg_ref[0] += d_g
        dsc_ref[0] += d_sc
        dsh_ref[0] += d_sh

    def a_spec(a, first):
        skip = first // tm
        return pl.BlockSpec((tm, a.shape[1]), lambda i: (jnp.maximum(i - skip, 0), 0))

    dx_skip = ncb if latent_dx_only else 0
    if dx_from_segments:
        dx_shapes = [jax.ShapeDtypeStruct((N_SEG, n_ctx // N_SEG, d), F32),
                     jax.ShapeDtypeStruct((N_SEG, (n - n_ctx) // N_SEG, d), F32)]
        dx_specs = [pl.BlockSpec((N_SEG, tj, d), lambda i: (0, 0, 0)),
                    pl.BlockSpec((N_SEG, tj, d), lambda i: (0, jnp.maximum(i - ncb, 0), 0))]
    else:
        dx_shapes = [jax.ShapeDtypeStruct((n - dx_skip * tm, d), F32)]
        dx_specs = [pl.BlockSpec((tm, d), lambda i: (jnp.maximum(i - dx_skip, 0), 0))]
    add_specs = (_parts_specs(adds, tm, ncb) if na == 2 else
                 [pl.BlockSpec((tm, d), lambda i: (jnp.maximum(i - add_skip, 0), 0))])
    res = pl.pallas_call(
        body,
        out_shape=dx_shapes + [jax.ShapeDtypeStruct(g.shape, F32), jax.ShapeDtypeStruct(sc.shape, F32),
                               jax.ShapeDtypeStruct(sh.shape, F32)],
        grid=(n // tm,),
        in_specs=[a_spec(a, first) for a, _, first in terms]
        + [pl.BlockSpec(w.shape, lambda i: (0, 0))] + _parts_specs(xs, tm, ncb)
        + [_bc_spec(g, ncb), _bc_spec(sc, ncb), _bc_spec(sh, ncb)] + add_specs,
        out_specs=dx_specs + [_bc_spec(g, ncb), _bc_spec(sc, ncb), _bc_spec(sh, ncb)],
        scratch_shapes=[pltpu.VMEM((d // HEAD_LANES, tm, HEAD_LANES), F32)] if dx_from_segments else [],
        compiler_params=_params(("arbitrary",), VMEM_LIMIT), name=name)(
            *[t[0] for t in terms], w, *xs, g, sc, sh, *adds)
    if dx_from_segments:
        return ((res[0].reshape(n_ctx, d), res[1].reshape(n - n_ctx, d)), *res[2:])
    return res


def mla_post_fwd(o, p0, x0, gate, w_out, n_ctx, name="l0_post"):
    n, d = o.shape
    xs, _ = _as_parts(x0, n_ctx)
    tm = math.gcd(ROW_TILE, n, n_ctx)
    ncb = n_ctx // tm
    nx = len(xs)

    def body(o_ref, z_ref, *refs):
        gt_ref, w_ref, x1_ref, og_ref, out_ref = refs[nx:]
        og = f_gate(o_ref[...], z_ref[...])[0].astype(BF16)
        out = jnp.dot(og, w_ref[...], preferred_element_type=F32)
        og_ref[...] = og
        out_ref[...] = out
        x1_ref[...] = _parts_tile(refs[:nx], ncb) + gt_ref[0] * out

    row = pl.BlockSpec((tm, d), lambda i: (i, 0))
    return pl.pallas_call(
        body, out_shape=[jax.ShapeDtypeStruct((n, d), F32), jax.ShapeDtypeStruct((n, d), BF16),
                         jax.ShapeDtypeStruct((n, d), F32)],
        grid=(n // tm,),
        in_specs=[row, row] + _parts_specs(xs, tm, ncb) + [_bc_spec(gate, ncb), pl.BlockSpec((d, d), lambda i: (0, 0))],
        out_specs=[row, row, row],
        compiler_params=_params(("parallel",), VMEM_LIMIT), name=name)(o, p0, *xs, gate, w_out)


def mla_post_bwd(dx1, out, og, o, p0, gate, w_out, n_ctx, name="l0_post_bwd"):
    n, d = o.shape
    dxs, _ = _as_parts(dx1, n_ctx)
    tm = math.gcd(ROW_TILE, n, n_ctx)
    ncb = n_ctx // tm
    nx = len(dxs)

    def body(*refs):
        out_ref, og_ref, o_ref, z_ref, gt_ref, w_ref, do_ref, dz_ref, dgt_ref, dw_ref = refs[nx:]
        i = pl.program_id(0)

        @pl.when(i == 0)
        def _():
            dw_ref[...] = jnp.zeros_like(dw_ref)

        @pl.when((i == 0) | (i == ncb))
        def _():
            dgt_ref[...] = jnp.zeros_like(dgt_ref)

        dx = _parts_tile(refs[:nx], ncb)
        dgt_ref[0] += jnp.sum(dx * out_ref[...], axis=0, keepdims=True)
        d_out16 = (gt_ref[0] * dx).astype(BF16)
        dw_ref[...] += lax.dot_general(og_ref[...], d_out16, (((0,), (0,)), ((), ())), preferred_element_type=F32)
        d_og = lax.dot_general(d_out16, w_ref[...], NT_DIMS, preferred_element_type=F32)
        _, gate_vjp = jax.vjp(lambda o_, z_: f_gate(o_, z_), o_ref[...], z_ref[...])
        d_o, d_z = gate_vjp((d_og,))
        do_ref[...] = d_o
        dz_ref[...] = d_z

    row = pl.BlockSpec((tm, d), lambda i: (i, 0))
    mat = pl.BlockSpec((d, d), lambda i: (0, 0))
    return pl.pallas_call(
        body, out_shape=[jax.ShapeDtypeStruct((n, d), F32), jax.ShapeDtypeStruct((n, d), F32),
                         jax.ShapeDtypeStruct(gate.shape, F32), jax.ShapeDtypeStruct((d, d), F32)],
        grid=(n // tm,),
        in_specs=_parts_specs(dxs, tm, ncb) + [row, row, row, row, _bc_spec(gate, ncb), mat],
        out_specs=[row, row, _bc_spec(gate, ncb), mat],
        compiler_params=_params(("arbitrary",), VMEM_LIMIT), name=name)(*dxs, out, og, o, p0, gate, w_out)


def s5_tail(y_ssm, p1, x1p, target, n_ctx, d_vec, b_glu, gate, final_g, w_glu, w_out, name="l1_tail"):
    n, d = y_ssm.shape
    tm = math.gcd(ROW_TILE, n, n_ctx)
    off = n_ctx // tm
    tn_dims = (((0,), (0,)), ((), ()))

    def row_loss(x, g, t):
        e = _rms(x) * g - t
        return 0.5 * (e * e) * (1.0 / d)

    def body(y_ref, u_ref, z_ref, x1_ref, t_ref, d_ref, b_ref, gt_ref, fg_ref, wg_ref, wo_ref,
             l_ref, dx_ref, dy_ref, du_ref, dz_ref, dfg_ref, dgt_ref, db_ref, dd_ref, dwg_ref, dwo_ref):
        @pl.when(pl.program_id(0) == 0)
        def _():
            for r in (l_ref, dfg_ref, dgt_ref, db_ref, dd_ref, dwg_ref, dwo_ref):
                r[...] = jnp.zeros_like(r)

        u, z, tgt, gt = u_ref[...], z_ref[...], t_ref[...], gt_ref[...]
        (ya,), act_vjp = jax.vjp(lambda y_, u_, d_: f_s5_act(y_, u_, d_), y_ref[...], u, d_ref[...])
        ya16 = ya.astype(BF16)
        gl = jnp.dot(ya16, wg_ref[...], preferred_element_type=F32)
        (y3,), glu_vjp = jax.vjp(lambda a_, g_, z_, b_: f_s5_glu(a_, g_, z_, b_), ya, gl, z, b_ref[...])
        y3_16 = y3.astype(BF16)
        out1 = jnp.dot(y3_16, wo_ref[...], preferred_element_type=F32)
        lterm, loss_vjp = jax.vjp(lambda x_, g_: row_loss(x_, g_, tgt), x1_ref[...] + gt * out1, fg_ref[...])
        dx2, dfg = loss_vjp(jnp.ones_like(lterm))
        l_ref[...] += jnp.sum(lterm, axis=0, keepdims=True)
        dfg_ref[...] += dfg
        dx_ref[...] = dx2
        dgt_ref[...] += jnp.sum(dx2 * out1, axis=0, keepdims=True)
        d_out16 = (gt * dx2).astype(BF16)
        dwo_ref[...] += lax.dot_general(y3_16, d_out16, tn_dims, preferred_element_type=F32)
        d_y3 = lax.dot_general(d_out16, wo_ref[...], NT_DIMS, preferred_element_type=F32)
        d_ya, d_gl, d_z, d_b = glu_vjp((d_y3,))
        dz_ref[...] = d_z
        db_ref[...] += d_b
        d_gl16 = d_gl.astype(BF16)
        dwg_ref[...] += lax.dot_general(ya16, d_gl16, tn_dims, preferred_element_type=F32)
        d_ya = d_ya + lax.dot_general(d_gl16, wg_ref[...], NT_DIMS, preferred_element_type=F32)
        d_y, d_u, d_d = act_vjp((d_ya,))
        dy_ref[...] = d_y
        du_ref[...] = d_u
        dd_ref[...] += d_d

    row = pl.BlockSpec((tm, d), lambda i: (i, 0))
    vecs = pl.BlockSpec((1, d), lambda i: (0, 0))
    mat = pl.BlockSpec((d, d), lambda i: (0, 0))
    return pl.pallas_call(
        body,
        out_shape=[jax.ShapeDtypeStruct((1, d), F32)] + [jax.ShapeDtypeStruct((n, d), F32)] * 4
        + [jax.ShapeDtypeStruct((1, d), F32)] * 4 + [jax.ShapeDtypeStruct((d, d), F32)] * 2,
        grid=(n // tm,),
        in_specs=[row, pl.BlockSpec((tm, d), lambda i: (i + off, 0)), pl.BlockSpec((tm, d), lambda i: (i + off, 1)),
                  pl.BlockSpec((tm, d), lambda i: (i + off, 0)), row, vecs, vecs, vecs, vecs, mat, mat],
        out_specs=[vecs, row, row, row, row, vecs, vecs, vecs, vecs, mat, mat],
        compiler_params=_params(("arbitrary",), VMEM_LIMIT), name=name)(
            y_ssm, p1, p1, x1p, target, d_vec, b_glu, gate, final_g, w_glu, w_out)


NT_DIMS = (((1,), (1,)), ((), ()))
HEAD_LANES = 128
N_PAIRS = MLA_HEADS // 2


def _own_lanes(shape, hh):
    lane = lax.broadcasted_iota(jnp.int32, shape, len(shape) - 1)
    return (lane < V_HEAD_DIM) if hh == 0 else (lane >= V_HEAD_DIM)


def _delta_lane(hh):
    return V_HEAD_DIM if hh == 0 else 0


def _rope_tiles(x, cos, sin_next, sin_prev, inverse):
    width = x.shape[-1]
    reps = width // HEAD_LANES
    c, sn, sp = (jnp.tile(t, (1, reps)) for t in (cos, sin_next, sin_prev))
    if inverse:
        return x * c + pltpu.roll(x * sn, 8, 1) + pltpu.roll(x * sp, width - 8, 1)
    return x * c + pltpu.roll(x, width - 8, 1) * sn + pltpu.roll(x, 8, 1) * sp


STAT_LANE = QK_DIM


def _with_stat(x16, col, lane0):
    hi = col.astype(BF16)
    r1 = col - hi.astype(F32)
    mid = r1.astype(BF16)
    lo = (r1 - mid.astype(F32)).astype(BF16)
    lane = lax.broadcasted_iota(jnp.int32, x16.shape, 1)
    return jnp.where(lane == lane0, hi, jnp.where(lane == lane0 + 1, mid, jnp.where(lane == lane0 + 2, lo, x16)))


def attn_fwd(qb, kb, vb, n_ctx):
    T = qb.shape[0]
    tq = math.gcd(ROW_TILE, n_ctx)
    nq, ncb = T // tq, n_ctx // tq

    def body(q_ref, k_ref, v_ref, o_ref, lse_ref, qs_ref):
        qi = pl.program_id(1)

        def rows(n_keys):
            v = v_ref[:n_keys, :]
            outs = []
            for hh in range(2):
                hs = slice(hh * HEAD_LANES, (hh + 1) * HEAD_LANES)
                s = lax.dot_general(q_ref[:, hs], k_ref[:n_keys, hs], NT_DIMS,
                                    preferred_element_type=F32) * SOFTMAX_SCALE
                m = jnp.max(s, axis=-1, keepdims=True)
                p = jnp.exp(s - m)
                l = jnp.sum(p, axis=-1, keepdims=True)
                outs.append(jnp.dot(p.astype(BF16), v, preferred_element_type=F32) / l)
                lse = m + jnp.log(l)
                lse_ref[hh] = lse
                qs_ref[:, hs] = _with_stat(q_ref[:, hs], lse * (-1.0 / SOFTMAX_SCALE), STAT_LANE)
            o_ref[...] = jnp.where(_own_lanes(outs[0].shape, 0), outs[0], outs[1])

        pl.when(qi < ncb)(lambda: rows(n_ctx))
        pl.when(qi >= ncb)(lambda: rows(T))

    return pl.pallas_call(
        body,
        out_shape=[jax.ShapeDtypeStruct((T, MLA_HEADS * V_HEAD_DIM), F32),
                   jax.ShapeDtypeStruct((MLA_HEADS, T, 1), F32), jax.ShapeDtypeStruct(qb.shape, BF16)],
        grid=(N_PAIRS, nq),
        in_specs=[pl.BlockSpec((tq, 2 * HEAD_LANES), lambda h, i: (i, h)),
                  pl.BlockSpec((T, 2 * HEAD_LANES), lambda h, i: (0, h)),
                  pl.BlockSpec((T, 2 * V_HEAD_DIM), lambda h, i: (0, h))],
        out_specs=[pl.BlockSpec((tq, 2 * V_HEAD_DIM), lambda h, i: (i, h)),
                   pl.BlockSpec((2, tq, 1), lambda h, i: (h, i, 0)),
                   pl.BlockSpec((tq, 2 * HEAD_LANES), lambda h, i: (i, h))],
        compiler_params=_params(("parallel", "parallel"), VMEM_LIMIT), name="attn_fwd")(qb, kb, vb)


def attn_bwd_dq(qb, kb, vb, o, do, lse, tabs, n_ctx):
    T = qb.shape[0]
    tq = math.gcd(ROW_TILE, n_ctx)
    nq, ncb = T // tq, n_ctx // tq

    def body(q_ref, k_ref, v_ref, o_ref, do_ref, lse_ref, c_ref, sn_ref, sp_ref, dq_ref, dos_ref):
        qi = pl.program_id(1)

        def rows(n_keys):
            v = v_ref[:n_keys, :]
            dqs = []
            for hh in range(2):
                hs = slice(hh * HEAD_LANES, (hh + 1) * HEAD_LANES)
                k = k_ref[:n_keys, hs]
                do = jnp.where(_own_lanes(do_ref.shape, hh), do_ref[...], 0.0)
                delta = jnp.sum(do * o_ref[...], axis=-1, keepdims=True)
                s = lax.dot_general(q_ref[:, hs], k, NT_DIMS, preferred_element_type=F32) * SOFTMAX_SCALE
                p = jnp.exp(s - lse_ref[hh])
                do16 = do.astype(BF16)
                dp = lax.dot_general(do16, v, NT_DIMS, preferred_element_type=F32)
                ds = p * (dp - delta) * SOFTMAX_SCALE
                dqs.append(jnp.dot(ds.astype(BF16), k, preferred_element_type=F32))
                dos_ref[:, hs] = _with_stat(do16, delta, _delta_lane(hh))
            dq = jnp.concatenate(dqs, axis=1)
            dq_ref[...] = _rope_tiles(dq, c_ref[...], sn_ref[...], sp_ref[...], True).astype(BF16)

        pl.when(qi < ncb)(lambda: rows(n_ctx))
        pl.when(qi >= ncb)(lambda: rows(T))

    tab = pl.BlockSpec((tq, HEAD_LANES), lambda h, i: (i, 0))
    return pl.pallas_call(
        body,
        out_shape=[jax.ShapeDtypeStruct((T, MLA_HEADS * HEAD_LANES), BF16)] * 2,
        grid=(N_PAIRS, nq),
        in_specs=[pl.BlockSpec((tq, 2 * HEAD_LANES), lambda h, i: (i, h)),
                  pl.BlockSpec((T, 2 * HEAD_LANES), lambda h, i: (0, h)),
                  pl.BlockSpec((T, 2 * V_HEAD_DIM), lambda h, i: (0, h)),
                  pl.BlockSpec((tq, 2 * V_HEAD_DIM), lambda h, i: (i, h)),
                  pl.BlockSpec((tq, 2 * V_HEAD_DIM), lambda h, i: (i, h)),
                  pl.BlockSpec((2, tq, 1), lambda h, i: (h, i, 0)), tab, tab, tab],
        out_specs=[pl.BlockSpec((tq, 2 * HEAD_LANES), lambda h, i: (i, h))] * 2,
        compiler_params=_params(("parallel", "parallel"), VMEM_LIMIT), name="attn_bwd_dq")(
            qb, kb, vb, o, do, lse, *tabs)


def attn_bwd_dkv(qs, kb, vb, dos, n_ctx):
    T = qs.shape[0]
    tq = math.gcd(ROW_TILE, n_ctx)
    nq, ncb = T // tq, n_ctx // tq

    def body(q_ref, do_ref, k_ref, v_ref, dk_ref, dv_ref):
        kj = pl.program_id(1)

        def cols(first):
            v = v_ref[...]
            lane = lax.broadcasted_iota(jnp.int32, v.shape, 1)
            dvs = []
            for hh in range(2):
                hs = slice(hh * HEAD_LANES, (hh + 1) * HEAD_LANES)
                q = q_ref[first:, hs]
                do16 = do_ref[first:, hs]
                in_delta = (lane >= _delta_lane(hh)) & (lane < _delta_lane(hh) + 3)
                v_minus = jnp.where(in_delta, -jnp.ones_like(v), v)
                pt = jnp.exp(lax.dot_general(k_ref[:, hs], q, NT_DIMS, preferred_element_type=F32) * SOFTMAX_SCALE)
                dvs.append(jnp.dot(pt.astype(BF16), do16, preferred_element_type=F32))
                dst = pt * lax.dot_general(v_minus, do16, NT_DIMS, preferred_element_type=F32) * SOFTMAX_SCALE
                dk_ref[:, hs] = jnp.dot(dst.astype(BF16), q, preferred_element_type=F32)
            dv_ref[...] = jnp.where(_own_lanes(dvs[0].shape, 0), dvs[0], dvs[1])

        pl.when(kj < ncb)(lambda: cols(0))
        pl.when(kj >= ncb)(lambda: cols(n_ctx))

    return pl.pallas_call(
        body,
        out_shape=[jax.ShapeDtypeStruct((T, MLA_HEADS * HEAD_LANES), F32),
                   jax.ShapeDtypeStruct((T, MLA_HEADS * V_HEAD_DIM), F32)],
        grid=(N_PAIRS, nq),
        in_specs=[pl.BlockSpec((T, 2 * HEAD_LANES), lambda h, j: (0, h)),
                  pl.BlockSpec((T, 2 * HEAD_LANES), lambda h, j: (0, h)),
                  pl.BlockSpec((tq, 2 * HEAD_LANES), lambda h, j: (j, h)),
                  pl.BlockSpec((tq, 2 * V_HEAD_DIM), lambda h, j: (j, h))],
        out_specs=[pl.BlockSpec((tq, 2 * HEAD_LANES), lambda h, j: (j, h)),
                   pl.BlockSpec((tq, 2 * V_HEAD_DIM), lambda h, j: (j, h))],
        compiler_params=_params(("parallel", "parallel"), VMEM_LIMIT), name="attn_bwd_dkv")(
            qs, dos, kb, vb)


def _split_bf16(x):
    hi = x.astype(BF16)
    return hi, (x - hi.astype(F32)).astype(BF16)


def q_heads(cq, gain, w_uq_p, tabs, name="l0_uq"):
    T, K = cq.arr.shape[0], cq.width
    N = w_uq_p.shape[1]
    tm = math.gcd(ROW_TILE, T)

    def body(a_ref, g_ref, w_ref, c_ref, sn_ref, sp_ref, o_ref, n_ref):
        qn = f_rms(a_ref[...], g_ref[0])[0].astype(BF16)
        n_ref[...] = qn
        acc = jnp.dot(qn, w_ref[...], preferred_element_type=F32)
        o_ref[...] = _rope_tiles(acc, c_ref[...], sn_ref[...], sp_ref[...], False).astype(BF16)

    tab = pl.BlockSpec((tm, HEAD_LANES), lambda i: (i, 0))
    return pl.pallas_call(
        body, out_shape=[jax.ShapeDtypeStruct((T, N), BF16), jax.ShapeDtypeStruct((T, K), BF16)], grid=(T // tm,),
        in_specs=[cq.spec(tm), pl.BlockSpec((1, 1, K), lambda i: (0, 0, 0)), pl.BlockSpec((K, N), lambda i: (0, 0)),
                  tab, tab, tab],
        out_specs=[pl.BlockSpec((tm, N), lambda i: (i, 0)), pl.BlockSpec((tm, K), lambda i: (i, 0))],
        compiler_params=_params(("parallel",), VMEM_LIMIT), name=name)(cq.arr, gain, w_uq_p, *tabs)


def kv_heads(ckv, gain, w_kn_p, w_v, kr, spread, tabs, name="l0_ukv"):
    T, K = ckv.arr.shape[0], ckv.width
    N = w_kn_p.shape[1]
    NV = w_v.shape[1]
    tm = math.gcd(ROW_TILE, T)

    def body(a_ref, g_ref, wk_ref, wv_ref, kr_ref, e_ref, c_ref, sn_ref, sp_ref, k_ref, v_ref, n_ref):
        a = f_rms(a_ref[...], g_ref[0])[0].astype(BF16)
        n_ref[...] = a
        hi, lo = _split_bf16(kr_ref[...])
        acc = (jnp.dot(a, wk_ref[...], preferred_element_type=F32)
               + jnp.dot(hi, e_ref[...], preferred_element_type=F32)
               + jnp.dot(lo, e_ref[...], preferred_element_type=F32))
        roped = _rope_tiles(acc, c_ref[...], sn_ref[...], sp_ref[...], False)
        lane = lax.broadcasted_iota(jnp.int32, roped.shape, 1) % HEAD_LANES
        k_ref[...] = jnp.where((lane >= STAT_LANE) & (lane < STAT_LANE + 3), 1.0, roped).astype(BF16)
        v_ref[...] = jnp.dot(a, wv_ref[...], preferred_element_type=F32).astype(BF16)

    tab = pl.BlockSpec((tm, HEAD_LANES), lambda i: (i, 0))
    return pl.pallas_call(
        body, out_shape=[jax.ShapeDtypeStruct((T, N), BF16), jax.ShapeDtypeStruct((T, NV), BF16),
                         jax.ShapeDtypeStruct((T, K), BF16)], grid=(T // tm,),
        in_specs=[ckv.spec(tm), pl.BlockSpec((1, 1, K), lambda i: (0, 0, 0)), pl.BlockSpec((K, N), lambda i: (0, 0)),
                  pl.BlockSpec((K, NV), lambda i: (0, 0)), kr.spec(tm),
                  pl.BlockSpec((kr.width, N), lambda i: (0, 0)), tab, tab, tab],
        out_specs=[pl.BlockSpec((tm, N), lambda i: (i, 0)), pl.BlockSpec((tm, NV), lambda i: (i, 0)),
                   pl.BlockSpec((tm, K), lambda i: (i, 0))],
        compiler_params=_params(("parallel",), VMEM_LIMIT), name=name)(
            ckv.arr, gain, w_kn_p, w_v, kr.arr, spread, *tabs)


def heads_unrope(d, tabs, spread=None, name="unrope"):
    T, N = d.shape
    tm = math.gcd(ROW_TILE, T)

    def body(*refs):
        if spread is None:
            d_ref, c_ref, sn_ref, sp_ref, o_ref = refs
        else:
            d_ref, c_ref, sn_ref, sp_ref, e_ref, o_ref, kr_ref = refs
        g = _rope_tiles(d_ref[...], c_ref[...], sn_ref[...], sp_ref[...], True)
        o_ref[...] = g.astype(BF16)
        if spread is not None:
            hi, lo = _split_bf16(g)
            kr_ref[...] = (lax.dot_general(hi, e_ref[...], NT_DIMS, preferred_element_type=F32)
                           + lax.dot_general(lo, e_ref[...], NT_DIMS, preferred_element_type=F32))

    tab = pl.BlockSpec((tm, HEAD_LANES), lambda i: (i, 0))
    row = pl.BlockSpec((tm, N), lambda i: (i, 0))
    ins, in_specs = [d, *tabs], [row, tab, tab, tab]
    out_shape, out_specs = [jax.ShapeDtypeStruct((T, N), BF16)], [row]
    if spread is not None:
        ins.append(spread)
        in_specs.append(pl.BlockSpec(spread.shape, lambda i: (0, 0)))
        out_shape.append(jax.ShapeDtypeStruct((T, spread.shape[0]), F32))
        out_specs.append(pl.BlockSpec((tm, spread.shape[0]), lambda i: (i, 0)))
    return pl.pallas_call(
        body, out_shape=out_shape, grid=(T // tm,), in_specs=in_specs, out_specs=out_specs,
        compiler_params=_params(("parallel",), VMEM_LIMIT), name=name)(*ins)


def _cmul(ar, ai, br, bi):
    return ar * br - ai * bi, ar * bi + ai * br


def s5_chain(finals, s0, a, n_steps, reverse, name):
    W = finals.shape[-1]
    first = N_SEG - 1 if reverse else 0

    def body(f_ref, s0_ref, a_ref, c_ref):
        pr, pi = jnp.ones((1, W), F32), jnp.zeros((1, W), F32)
        br, bi = a_ref[0], a_ref[1]
        n = n_steps
        while n:
            if n & 1:
                pr, pi = _cmul(pr, pi, br, bi)
            br, bi = _cmul(br, bi, br, bi)
            n >>= 1
        fr, fi = f_ref[0], f_ref[1]
        row = lax.broadcasted_iota(jnp.int32, (N_SEG, W), 0)
        s0r = jnp.broadcast_to(s0_ref[0], (N_SEG, W))
        s0i = jnp.broadcast_to(s0_ref[1], (N_SEG, W))
        cr = jnp.where(row == first, s0r, 0.0)
        ci = jnp.where(row == first, s0i, 0.0)
        shift = N_SEG - 1 if reverse else 1
        for _ in range(N_SEG - 1):
            mr, mi = _cmul(pr, pi, cr, ci)
            tr = pltpu.roll(fr + mr, shift, 0)
            ti = pltpu.roll(fi + mi, shift, 0)
            cr = jnp.where(row == first, s0r, tr)
            ci = jnp.where(row == first, s0i, ti)
        c_ref[0] = cr
        c_ref[1] = ci

    return pl.pallas_call(body, out_shape=jax.ShapeDtypeStruct((2, N_SEG, W), F32), name=name)(finals, s0, a)


def _scan_chunk(bur, bui, st_ref, a_ref, n_steps, reverse):
    for lc in range(S5_LANES // BLK_ST):
        sl = slice(lc * BLK_ST, (lc + 1) * BLK_ST)
        lr = jnp.broadcast_to(a_ref[0, :, sl], (N_SEG, BLK_ST))
        li = jnp.broadcast_to(a_ref[1, :, sl], (N_SEG, BLK_ST))

        def step(jj, carry, sl=sl, lr=lr, li=li):
            sr, si = carry
            j = (n_steps - 1 - jj) if reverse else jj
            r0 = pl.multiple_of(j * N_SEG, N_SEG)
            nr = lr * sr - li * si + bur[pl.ds(r0, N_SEG), sl]
            ni = lr * si + li * sr + bui[pl.ds(r0, N_SEG), sl]
            bur[pl.ds(r0, N_SEG), sl] = nr
            bui[pl.ds(r0, N_SEG), sl] = ni
            return nr, ni

        sr, si = lax.fori_loop(0, n_steps, step, (st_ref[0, :, sl], st_ref[1, :, sl]))
        st_ref[0, :, sl] = sr
        st_ref[1, :, sl] = si


def _project_in(x16, w_re, w_im, bur, bui, adjoint):
    for gb in range(N_BLOCKS):
        xb = x16[:, gb * BLK_CH:(gb + 1) * BLK_CH]
        sl = slice(gb * BLK_ST, (gb + 1) * BLK_ST)
        if adjoint:
            dn = (((1,), (1,)), ((), ()))
            bur[:, sl] = lax.dot_general(xb, w_re[gb], dn, preferred_element_type=F32)
            bui[:, sl] = -lax.dot_general(xb, w_im[gb], dn, preferred_element_type=F32)
        else:
            bur[:, sl] = jnp.dot(xb, w_re[gb], preferred_element_type=F32)
            bui[:, sl] = jnp.dot(xb, w_im[gb], preferred_element_type=F32)


def s5_scan(act, w_re, w_im, a, init, *, reverse, adjoint=False, c_re=None, c_im=None, add=None,
            want_ckpt=False, rows=None, name):
    act_off, N = rows if rows is not None else (0, act.shape[0])
    R = math.gcd(ROW_TILE, N, act_off)
    nch, jc = N // R, R // N_SEG
    with_out = c_re is not None

    def chunk(i):
        return (nch - 1 - i) if reverse else i

    def body(*refs):
        act_ref, wre_ref, wim_ref, a_ref, init_ref = refs[:5]
        k = 5
        if with_out:
            cre_ref, cim_ref = refs[k:k + 2]
            k += 2
        if add is not None:
            add_ref = refs[k]
            k += 1
        if with_out:
            out_ref = refs[k]
            k += 1
        if want_ckpt:
            ck_ref = refs[k]
            k += 1
        fin_ref, bur, bui = refs[k:k + 3]

        @pl.when(pl.program_id(0) == 0)
        def _():
            fin_ref[...] = init_ref[...]

        if want_ckpt:
            ck_ref[0] = fin_ref[...]
        _project_in(act_ref[...].astype(BF16), wre_ref, wim_ref, bur, bui, adjoint)
        _scan_chunk(bur, bui, fin_ref, a_ref, jc, reverse)
        if with_out:
            for gb in range(N_BLOCKS):
                sl = slice(gb * BLK_ST, (gb + 1) * BLK_ST)
                y = (jnp.dot(bur[:, sl].astype(BF16), cre_ref[gb], preferred_element_type=F32)
                     - jnp.dot(bui[:, sl].astype(BF16), cim_ref[gb], preferred_element_type=F32))
                cs = slice(gb * BLK_CH, (gb + 1) * BLK_CH)
                if add is not None:
                    y = y + add_ref[:, cs]
                out_ref[:, cs] = y

    row_spec = pl.BlockSpec((R, D_MODEL), lambda i: (chunk(i), 0))
    act_spec = pl.BlockSpec((R, D_MODEL), lambda i: (chunk(i) + act_off // R, 0))
    w_spec = pl.BlockSpec(w_re.shape, lambda i: (0, 0, 0))
    st_spec = pl.BlockSpec((2, N_SEG, S5_LANES), lambda i: (0, 0, 0))
    ins = [act, w_re, w_im, a, init]
    in_specs = [act_spec, w_spec, w_spec, pl.BlockSpec((2, 1, S5_LANES), lambda i: (0, 0, 0)), st_spec]
    if with_out:
        ins += [c_re, c_im]
        in_specs += [pl.BlockSpec(c_re.shape, lambda i: (0, 0, 0))] * 2
    if add is not None:
        ins.append(add)
        in_specs.append(row_spec)
    out_shape, out_specs = [], []
    if with_out:
        out_shape.append(jax.ShapeDtypeStruct((N, D_MODEL), F32))
        out_specs.append(row_spec)
    if want_ckpt:
        out_shape.append(jax.ShapeDtypeStruct((nch, 2, N_SEG, S5_LANES), F32))
        out_specs.append(pl.BlockSpec((1, 2, N_SEG, S5_LANES), lambda i: (chunk(i), 0, 0, 0)))
    out_shape.append(jax.ShapeDtypeStruct((2, N_SEG, S5_LANES), F32))
    out_specs.append(st_spec)
    res = pl.pallas_call(
        body, out_shape=out_shape, grid=(nch,), in_specs=in_specs, out_specs=out_specs,
        scratch_shapes=[pltpu.VMEM((R, S5_LANES), F32), pltpu.VMEM((R, S5_LANES), F32)],
        compiler_params=_params(("arbitrary",), VMEM_LIMIT), name=name)(*ins)
    res = list(res)
    out = res.pop(0) if with_out else None
    ckpt = res.pop(0) if want_ckpt else None
    return out, ckpt, res[0]


def s5_grads(dy, u, ckpt, b_re, b_im, c_re, c_im, lam, init_adj, *, reverse, add=None, u_off=0, du_rows=None,
             du_into=None, name):
    N = dy.shape[0]
    du_total, du_first = du_rows if du_rows is not None else (N, 0)
    R = math.gcd(ROW_TILE, N, u_off, du_first)
    nch, jc = N // R, R // N_SEG
    W = S5_LANES

    def chunk(i):
        return i if reverse else (nch - 1 - i)

    def body(*refs):
        dy_ref, u_ref, ck_ref, bre_ref, bim_ref, cre_ref, cim_ref, lam_ref, init_ref = refs[:9]
        k = 9
        if add is not None:
            add_ref = refs[k]
            k += 1
        k += du_into is not None
        du_ref, dlam_ref, dbre_ref, dbim_ref, dcre_ref, dcim_ref, fin_ref = refs[k:k + 7]
        sr_buf, si_buf, er_buf, ei_buf, st_buf = refs[k + 7:k + 12]

        @pl.when(pl.program_id(0) == 0)
        def _():
            fin_ref[...] = init_ref[...]
            dlam_ref[...] = jnp.zeros_like(dlam_ref)
            dbre_ref[...] = jnp.zeros_like(dbre_ref)
            dbim_ref[...] = jnp.zeros_like(dbim_ref)
            dcre_ref[...] = jnp.zeros_like(dcre_ref)
            dcim_ref[...] = jnp.zeros_like(dcim_ref)

        u16 = u_ref[...].astype(BF16)
        dy16 = dy_ref[...].astype(BF16)
        st_buf[...] = ck_ref[0]
        _project_in(u16, bre_ref, bim_ref, sr_buf, si_buf, False)
        _scan_chunk(sr_buf, si_buf, st_buf, lam_ref, jc, reverse)
        _project_in(dy16, cre_ref, cim_ref, er_buf, ei_buf, True)
        for lc in range(W // BLK_ST):
            sl = slice(lc * BLK_ST, (lc + 1) * BLK_ST)
            lr = jnp.broadcast_to(lam_ref[0, :, sl], (N_SEG, BLK_ST))
            li = jnp.broadcast_to(lam_ref[1, :, sl], (N_SEG, BLK_ST))

            def one(r0, spr, spi, carry, sl=sl, lr=lr, li=li):
                gr, gi, ar, ai = carry
                nr = er_buf[pl.ds(r0, N_SEG), sl] + lr * gr + li * gi
                ni = ei_buf[pl.ds(r0, N_SEG), sl] + lr * gi - li * gr
                er_buf[pl.ds(r0, N_SEG), sl] = nr
                ei_buf[pl.ds(r0, N_SEG), sl] = ni
                return nr, ni, ar + spr * nr + spi * ni, ai + spr * ni - spi * nr

            def step(ff, carry, sl=sl, one=one):
                f = jc - 1 - ff
                j = (jc - 1 - f) if reverse else f
                jp = (j + 1) if reverse else (j - 1)
                r0 = pl.multiple_of(j * N_SEG, N_SEG)
                p0 = pl.multiple_of(jp * N_SEG, N_SEG)
                return one(r0, sr_buf[pl.ds(p0, N_SEG), sl], si_buf[pl.ds(p0, N_SEG), sl], carry)

            carry = (fin_ref[0, :, sl], fin_ref[1, :, sl], dlam_ref[0, :, sl], dlam_ref[1, :, sl])
            carry = lax.fori_loop(0, jc - 1, step, carry)
            r_first = (jc - 1) * N_SEG if reverse else 0
            gr, gi, ar, ai = one(r_first, ck_ref[0, 0, :, sl], ck_ref[0, 1, :, sl], carry)
            fin_ref[0, :, sl] = gr
            fin_ref[1, :, sl] = gi
            dlam_ref[0, :, sl] = ar
            dlam_ref[1, :, sl] = ai
        tn = (((0,), (0,)), ((), ()))
        nt = (((1,), (1,)), ((), ()))
        for gb in range(N_BLOCKS):
            sl = slice(gb * BLK_ST, (gb + 1) * BLK_ST)
            cs = slice(gb * BLK_CH, (gb + 1) * BLK_CH)
            gr16 = er_buf[:, sl].astype(BF16)
            gi16 = ei_buf[:, sl].astype(BF16)
            du = (lax.dot_general(gr16, bre_ref[gb], nt, preferred_element_type=F32)
                  + lax.dot_general(gi16, bim_ref[gb], nt, preferred_element_type=F32))
            if add is not None:
                du = du + add_ref[:, cs]
            du_ref[:, cs] = du
            ub, dyb = u16[:, cs], dy16[:, cs]
            dbre_ref[gb] += lax.dot_general(ub, gr16, tn, preferred_element_type=F32)
            dbim_ref[gb] += lax.dot_general(ub, gi16, tn, preferred_element_type=F32)
            dcre_ref[gb] += lax.dot_general(sr_buf[:, sl].astype(BF16), dyb, tn, preferred_element_type=F32)
            dcim_ref[gb] -= lax.dot_general(si_buf[:, sl].astype(BF16), dyb, tn, preferred_element_type=F32)

    row_spec = pl.BlockSpec((R, D_MODEL), lambda i: (chunk(i), 0))
    st_spec = pl.BlockSpec((2, N_SEG, W), lambda i: (0, 0, 0))
    wb_spec = pl.BlockSpec(b_re.shape, lambda i: (0, 0, 0))
    wc_spec = pl.BlockSpec(c_re.shape, lambda i: (0, 0, 0))
    ins = [dy, u, ckpt, b_re, b_im, c_re, c_im, lam, init_adj]
    u_spec = pl.BlockSpec((R, D_MODEL), lambda i: (chunk(i) + u_off // R, 0))
    in_specs = [row_spec, u_spec, pl.BlockSpec((1, 2, N_SEG, W), lambda i: (chunk(i), 0, 0, 0)),
                wb_spec, wb_spec, wc_spec, wc_spec, pl.BlockSpec((2, 1, W), lambda i: (0, 0, 0)), st_spec]
    if add is not None:
        ins.append(add)
        in_specs.append(row_spec)
    aliases = {}
    if du_into is not None:
        aliases[len(ins)] = 0
        ins.append(du_into)
        in_specs.append(pl.BlockSpec(memory_space=pl.ANY))
    du_spec = pl.BlockSpec((R, D_MODEL), lambda i: (chunk(i) + du_first // R, 0))
    out_shape = [jax.ShapeDtypeStruct((du_total, D_MODEL), F32), jax.ShapeDtypeStruct((2, N_SEG, W), F32),
                 jax.ShapeDtypeStruct(b_re.shape, F32), jax.ShapeDtypeStruct(b_re.shape, F32),
                 jax.ShapeDtypeStruct(c_re.shape, F32), jax.ShapeDtypeStruct(c_re.shape, F32),
                 jax.ShapeDtypeStruct((2, N_SEG, W), F32)]
    out_specs = [du_spec, st_spec, wb_spec, wb_spec, wc_spec, wc_spec, st_spec]
    return pl.pallas_call(
        body, out_shape=out_shape, grid=(nch,), in_specs=in_specs, out_specs=out_specs, input_output_aliases=aliases,
        scratch_shapes=[pltpu.VMEM((R, W), F32) for _ in range(4)] + [pltpu.VMEM((2, N_SEG, W), F32)],
        compiler_params=_params(("arbitrary",), VMEM_LIMIT), name=name)(*ins)


def adamw(w, g, m, v, name="adamw", after=None):
    n, d = w.shape
    lanes = -(-d // 128) * 128
    tm = n
    while tm * lanes * 4 > (1 << 20) and tm % 16 == 0:
        tm //= 2
    c1 = 1.0 - ADAM_B1 ** ADAM_STEP
    c2 = 1.0 - ADAM_B2 ** ADAM_STEP

    def body(w_ref, g_ref, m_ref, v_ref, *rest):
        d_ref, nm_ref, nv_ref = rest[-3:]
        g_ = g_ref[...]
        m_ = ADAM_B1 * m_ref[...] + (1.0 - ADAM_B1) * g_
        v_ = ADAM_B2 * v_ref[...] + (1.0 - ADAM_B2) * (g_ * g_)
        d_ref[...] = -ADAM_LR * ((m_ / c1) / (jnp.sqrt(v_ / c2) + ADAM_EPS) + ADAM_WD * w_ref[...])
        nm_ref[...] = m_
        nv_ref[...] = v_

    spec = pl.BlockSpec((tm, d), lambda i: (i, 0))
    extra = [] if after is None else [after]
    return pl.pallas_call(
        body, out_shape=[jax.ShapeDtypeStruct((n, d), F32)] * 3, grid=(n // tm,),
        in_specs=[spec] * 4 + [pl.BlockSpec(memory_space=pl.ANY)] * len(extra), out_specs=[spec] * 3,
        compiler_params=_params(("parallel",), VMEM_LIMIT), name=name)(w, g, m, v, *extra)


def _coords():
    return lax.axis_index("x"), lax.axis_index("y"), lax.axis_index("c")


def exchange(arrays, out_shapes, remote, local, name, aliases=None):
    n_in, n_out, n_rem, n_loc = len(arrays), len(out_shapes), len(remote), len(local)

    def at(ref, idx):
        return ref if idx is None else ref.at[idx]

    def body(*refs):
        ins, outs = refs[:n_in], refs[n_in:n_in + n_out]
        send_sems, recv_sems, local_sems = refs[n_in + n_out:]
        me = _coords()
        sends, recvs = [], []
        for k, (flip, ii, src_at, oi, dst_at) in enumerate(remote):
            peer = (me[0] ^ flip[0], me[1] ^ flip[1], me[2] ^ flip[2])
            src = at(ins[ii], src_at(me, peer))
            sends.append(pltpu.make_async_remote_copy(
                src_ref=src, dst_ref=at(outs[oi], dst_at(me)), send_sem=send_sems.at[k], recv_sem=recv_sems.at[k],
                device_id=peer, device_id_type=MESH))
            recvs.append(pltpu.make_async_remote_copy(
                src_ref=src, dst_ref=at(outs[oi], dst_at(peer)), send_sem=send_sems.at[k], recv_sem=recv_sems.at[k],
                device_id=peer, device_id_type=MESH))
        locs = [pltpu.make_async_copy(at(ins[ii], src_at(me)), at(outs[oi], dst_at(me)), local_sems.at[k])
                for k, (ii, src_at, oi, dst_at) in enumerate(local)]
        for cp in locs + sends:
            cp.start()
        for cp in recvs:
            cp.wait_recv()
        for cp in sends:
            cp.wait_send()
        for cp in locs:
            cp.wait()

    hbm = pl.BlockSpec(memory_space=pl.ANY)
    return pl.pallas_call(
        body, out_shape=list(out_shapes), in_specs=[hbm] * n_in, out_specs=[hbm] * n_out,
        scratch_shapes=[pltpu.SemaphoreType.DMA((n_rem,)), pltpu.SemaphoreType.DMA((n_rem,)),
                        pltpu.SemaphoreType.DMA((max(n_loc, 1),))],
        input_output_aliases=aliases or {}, name=name)(*arrays)


ALL_FLIPS = [(dx, dy, dc) for dx in (0, 1) for dy in (0, 1) for dc in (0, 1)][1:]
CHIP_FLIPS = [(1, 0, 0), (0, 1, 0), (1, 1, 0)]
CORE_FLIP = (0, 0, 1)


def _dev_index(p):
    return 4 * p[0] + 2 * p[1] + p[2]


def _chip_index(p):
    return 2 * p[0] + p[1]


def _gather(xs, flips, index, n, name):
    arrays = [x[None] for x in xs]
    outs = [jax.ShapeDtypeStruct((n,) + x.shape, x.dtype) for x in xs]
    remote = [(f, a, lambda me, peer: (0,), a, lambda s: (index(s),)) for a in range(len(xs)) for f in flips]
    local = [(a, lambda me: (0,), a, lambda me: (index(me),)) for a in range(len(xs))]
    return exchange(arrays, outs, remote, local, name)


def allgather_devices(x, name):
    return _gather([x], ALL_FLIPS, _dev_index, N_DEV, name)[0]


def allgather_chips(xs, name):
    return _gather(xs, CHIP_FLIPS, _chip_index, N_CHIP, name)


def gather_halves(xs, name):
    n = len(xs)
    nk = n * len(CHIP_FLIPS)

    def body(*refs):
        ins, outs = refs[:n], refs[n:2 * n]
        ici_send, ici_recv, d2d_send, d2d_recv = refs[2 * n:]
        me = _coords()
        sibling = (me[0], me[1], 1 - me[2])
        first, passed, landed = [], [], []
        for a in range(n):
            half = ins[a].shape[0] // 2
            mine = ins[a].at[pl.ds(pl.multiple_of(me[2] * half, 16), half)]
            for j, flip in enumerate(CHIP_FLIPS):
                k = a * len(CHIP_FLIPS) + j
                peer = (me[0] ^ flip[0], me[1] ^ flip[1], me[2])
                first.append(pltpu.make_async_remote_copy(
                    src_ref=mine, dst_ref=outs[a].at[_chip_index(me), me[2]], send_sem=ici_send.at[k],
                    recv_sem=ici_recv.at[k], device_id=peer, device_id_type=MESH))
                arrived = outs[a].at[_chip_index(peer), me[2]]
                landed.append(pltpu.make_async_remote_copy(
                    src_ref=mine, dst_ref=arrived, send_sem=ici_send.at[k], recv_sem=ici_recv.at[k],
                    device_id=peer, device_id_type=MESH))
                passed.append(pltpu.make_async_remote_copy(
                    src_ref=arrived, dst_ref=arrived, send_sem=d2d_send.at[k], recv_sem=d2d_recv.at[k],
                    device_id=sibling, device_id_type=MESH))
        for cp in first:
            cp.start()
        for k in range(nk):
            landed[k].wait_recv()
            passed[k].start()
        for a in range(n):
            for j, flip in enumerate(CHIP_FLIPS):
                k = a * len(CHIP_FLIPS) + j
                peer_chip = _chip_index((me[0] ^ flip[0], me[1] ^ flip[1]))
                from_sibling = outs[a].at[peer_chip, 1 - me[2]]
                pltpu.make_async_remote_copy(
                    src_ref=from_sibling, dst_ref=from_sibling, send_sem=d2d_send.at[k], recv_sem=d2d_recv.at[k],
                    device_id=sibling, device_id_type=MESH).wait_recv()
        for cp in first + passed:
            cp.wait_send()

    hbm = pl.BlockSpec(memory_space=pl.ANY)
    return pl.pallas_call(
        body, out_shape=[jax.ShapeDtypeStruct((N_CHIP, 2, x.shape[0] // 2, x.shape[1]), x.dtype) for x in xs],
        in_specs=[hbm] * n, out_specs=[hbm] * n,
        scratch_shapes=[pltpu.SemaphoreType.DMA((nk,)) for _ in range(4)], name=name)(*xs)


HBM_SPEC = pl.BlockSpec(memory_space=pltpu.HBM)
SEM_SPEC = pl.BlockSpec(memory_space=pltpu.SEMAPHORE)
DATAFLOW = pltpu.SideEffectType.DATAFLOW_SIDE_EFFECTING


def _at(ref, idx):
    return ref if idx is None else ref.at[idx]


def _peer(me, flip):
    return (me[0] ^ flip[0], me[1] ^ flip[1], me[2] ^ flip[2])


def exchange_start(arrays, land_shapes, remote, name, after=None):
    n_in, n_out, nk = len(arrays), len(land_shapes), len(remote)
    after = list(after or [])
    n_after = len(after)

    def body(*refs):
        srcs, lands = refs[:n_in], refs[n_in:n_in + n_out]
        first_out = n_in + n_out + n_after
        send_sems, recv_sems, token = refs[first_out], refs[first_out + 1], refs[-1]
        me = _coords()
        for k, (flip, ii, src_at, oi, dst_at) in enumerate(remote):
            peer = _peer(me, flip)
            pltpu.make_async_remote_copy(
                src_ref=_at(srcs[ii], src_at(me, peer)), dst_ref=_at(lands[oi], dst_at(me)), send_sem=send_sems.at[k],
                recv_sem=recv_sems.at[k], device_id=peer, device_id_type=MESH).start()
        token[...] = jnp.zeros_like(token)

    lands = [lax.empty(s.shape, s.dtype) for s in land_shapes]
    bufs = list(arrays) + lands
    out = pl.pallas_call(
        body, name=name,
        out_shape=(pltpu.SemaphoreType.DMA((nk,)), pltpu.SemaphoreType.DMA((nk,)),
                   *[pltpu.HBM(b.shape, b.dtype) for b in bufs], jax.ShapeDtypeStruct((8, 128), F32)),
        in_specs=[HBM_SPEC] * len(bufs) + [pl.BlockSpec(memory_space=pl.ANY)] * n_after,
        out_specs=(SEM_SPEC, SEM_SPEC, *[HBM_SPEC] * len(bufs), pl.BlockSpec(memory_space=pltpu.VMEM)),
        input_output_aliases={a: 2 + a for a in range(len(bufs))},
        compiler_params=pltpu.CompilerParams(has_side_effects=DATAFLOW),
    )(*[pltpu.with_memory_space_constraint(b, pltpu.HBM) for b in bufs], *after)
    flight = (out[0], out[1], list(out[2:2 + n_in]), list(out[2 + n_in:2 + n_in + n_out]), remote)
    return flight, out[-1]


def exchange_wait(flight, after, name):
    send_sems, recv_sems, arrays, lands, remote = flight
    n_in, n_out = len(arrays), len(lands)
    after = list(after) if isinstance(after, (list, tuple)) else [after]

    def body(*refs):
        srcs, lnds = refs[:n_in], refs[n_in:n_in + n_out]
        s_sems, r_sems = refs[n_in + n_out], refs[n_in + n_out + 1]
        me = _coords()
        for k, (flip, ii, src_at, oi, dst_at) in enumerate(remote):
            peer = _peer(me, flip)
            copy = pltpu.make_async_remote_copy(
                src_ref=_at(srcs[ii], src_at(me, peer)), dst_ref=_at(lnds[oi], dst_at(peer)), send_sem=s_sems.at[k],
                recv_sem=r_sems.at[k], device_id=peer, device_id_type=MESH)
            copy.wait_send()
            copy.wait_recv()

    bufs = list(arrays) + list(lands)
    out = pl.pallas_call(
        body, name=name,
        out_shape=tuple(pltpu.HBM(b.shape, b.dtype) for b in bufs),
        in_specs=[HBM_SPEC] * len(bufs) + [SEM_SPEC, SEM_SPEC] + [pl.BlockSpec(memory_space=pl.ANY)] * len(after),
        out_specs=tuple([HBM_SPEC] * len(bufs)),
        input_output_aliases={a: a for a in range(len(bufs))},
        compiler_params=pltpu.CompilerParams(has_side_effects=DATAFLOW),
    )(*bufs, send_sems, recv_sems, *after)
    return list(out[:n_in]), list(out[n_in:])


def _half_tile(h, cd):
    return h if h * cd * 4 <= (1 << 20) else math.gcd(512, h)


def pair_add(g, got, core, out_dtype, name):
    _, _, h, cd = g.shape
    th = _half_tile(h, cd)

    def body(c_ref, g_ref, got_ref, o_ref):
        o_ref[0] = (g_ref[0, 0] + got_ref[0]).astype(o_ref.dtype)

    return pl.pallas_call(
        body, out_shape=jax.ShapeDtypeStruct((N_CHIP, h, cd), out_dtype),
        grid_spec=pltpu.PrefetchScalarGridSpec(
            num_scalar_prefetch=1, grid=(N_CHIP, h // th),
            in_specs=[pl.BlockSpec((1, 1, th, cd), lambda q, i, c: (q, c[0], i, 0)),
                      pl.BlockSpec((1, th, cd), lambda q, i, c: (q, i, 0))],
            out_specs=pl.BlockSpec((1, th, cd), lambda q, i, c: (q, i, 0))),
        compiler_params=_params(("parallel", "parallel"), VMEM_LIMIT), name=name)(core, g, got)


def sum_chips(parts, sums, place, name):
    _, h, cd = parts.shape
    th = _half_tile(h, cd)

    def body(pc_ref, p_ref, own_ref, o_ref):
        acc = None
        for q in range(N_CHIP):
            term = jnp.where(pc_ref[1] == q, own_ref[0], p_ref[q]).astype(F32)
            acc = term if acc is None else acc + term
        o_ref[0] = acc

    return pl.pallas_call(
        body, out_shape=jax.ShapeDtypeStruct((2, h, cd), F32),
        grid_spec=pltpu.PrefetchScalarGridSpec(
            num_scalar_prefetch=1, grid=(h // th,),
            in_specs=[pl.BlockSpec((N_CHIP, th, cd), lambda i, pc: (0, i, 0)),
                      pl.BlockSpec((1, th, cd), lambda i, pc: (pc[1], i, 0))],
            out_specs=pl.BlockSpec((1, th, cd), lambda i, pc: (pc[0], i, 0))),
        compiler_params=_params(("parallel",), VMEM_LIMIT), name=name)(place, parts, sums)


def to_segments(a, n_ctx):
    def one(p):
        n = p.shape[0]
        return p.reshape(N_SEG, n // N_SEG, -1).transpose(1, 0, 2).reshape(n, -1)
    return jnp.concatenate([one(a[:n_ctx]), one(a[n_ctx:])], axis=0) if n_ctx else one(a)


def rows_to_segments(a, n_ctx, name):
    n, d = a.shape
    tj = n_ctx // N_SEG
    per_seg = (n - n_ctx) // N_SEG // tj
    assert n_ctx % N_SEG == 0 and (n - n_ctx) % (N_SEG * tj) == 0

    def body(*refs):
        out_ref, slabs = refs[N_SEG:]
        for c in range(d // HEAD_LANES):
            cols = slice(c * HEAD_LANES, (c + 1) * HEAD_LANES)
            for seg in range(N_SEG):
                slabs[c, pl.ds(seg, tj, stride=N_SEG), :] = refs[seg][:, cols]
            out_ref[:, cols] = slabs[c]

    def seg_spec(seg):
        return pl.BlockSpec((tj, d), lambda i: (jnp.where(i == 0, seg, N_SEG + seg * per_seg + i - 1), 0))

    return pl.pallas_call(
        body, out_shape=jax.ShapeDtypeStruct((n, d), a.dtype), grid=(1 + per_seg,),
        in_specs=[seg_spec(seg) for seg in range(N_SEG)], out_specs=pl.BlockSpec((N_SEG * tj, d), lambda i: (i, 0)),
        scratch_shapes=[pltpu.VMEM((d // HEAD_LANES, N_SEG * tj, HEAD_LANES), a.dtype)],
        compiler_params=_params(("parallel",), VMEM_LIMIT), name=name)(*[a] * N_SEG)


def rope_tables(n_ctx, n_lat):
    f32 = np.float32
    rows = n_lat // GRID_W
    row = np.repeat(np.arange(rows), GRID_W).astype(f32)
    col = np.tile(np.arange(GRID_W), rows).astype(f32)
    d = QK_ROPE_DIM // 2
    inv = (f32(1.0) / np.power(f32(ROPE_THETA), np.arange(0, d, 2, dtype=f32) / f32(d))).astype(f32)
    ang = np.concatenate([row[:, None] * inv[None, :], col[:, None] * inv[None, :]], axis=1).astype(f32)
    cos = np.concatenate([np.ones((n_ctx, d), f32), np.cos(ang)], axis=0)
    sin = np.concatenate([np.zeros((n_ctx, d), f32), np.sin(ang)], axis=0)
    q = QK_ROPE_DIM // 4
    T = n_ctx + n_lat
    ones, zeros = np.ones((T, QK_NOPE_DIM), f32), np.zeros((T, QK_NOPE_DIM), f32)
    tail, z8 = np.zeros((T, HEAD_LANES - QK_DIM), f32), np.zeros((T, q), f32)
    cr, cc, sr, sc = cos[:, :q], cos[:, q:], sin[:, :q], sin[:, q:]
    cos_t = np.concatenate([ones, cr, cr, cc, cc, tail], axis=1)
    sin_next = np.concatenate([zeros, -sr, z8, -sc, z8, tail], axis=1)
    sin_prev = np.concatenate([zeros, z8, sr, z8, sc, tail], axis=1)
    return tuple(jnp.asarray(t, F32) for t in (cos_t, sin_next, sin_prev))


def pad_heads(w, used):
    k = w.shape[0]
    return jnp.pad(w.reshape(k, MLA_HEADS, used), ((0, 0), (0, 0), (0, HEAD_LANES - used))).reshape(k, -1)


def unpad_heads(w, used):
    k = w.shape[0]
    return w.reshape(k, MLA_HEADS, HEAD_LANES)[:, :, :used].reshape(k, MLA_HEADS * used)


def rotary_spread():
    lane = np.arange(MLA_HEADS * HEAD_LANES) % HEAD_LANES
    return jnp.asarray(lane[None, :] == (QK_NOPE_DIM + np.arange(QK_ROPE_DIM))[:, None], BF16)


def s5_discretise(a_re, a_im, log_step, b_re, b_im):
    dt = jnp.exp(log_step)[:, None]
    mag = jnp.exp(a_re * dt)
    lb_re = mag * jnp.cos(a_im * dt)
    lb_im = mag * jnp.sin(a_im * dt)
    den = a_re * a_re + a_im * a_im
    nr = lb_re - 1.0
    f_re = ((nr * a_re + lb_im * a_im) / den)[:, None, :]
    f_im = ((lb_im * a_re - nr * a_im) / den)[:, None, :]
    return lb_re, lb_im, f_re * b_re - f_im * b_im, f_re * b_im + f_im * b_re


def s5_block_weights(lb_re, lb_im, bb_re, bb_im, c_re, c_im):
    eye = jnp.eye(GROUPS_PER_BLOCK, dtype=F32)
    lam = jnp.stack([lb_re.reshape(1, S5_LANES), lb_im.reshape(1, S5_LANES)])

    def b_blocks(bb):
        t = bb.reshape(N_BLOCKS, GROUPS_PER_BLOCK, S5_GROUP, S5_STATE)
        return jnp.einsum("bgcp,gh->bgchp", t, eye).reshape(N_BLOCKS, BLK_CH, BLK_ST).astype(BF16)

    def c_blocks(cc):
        t = cc.reshape(N_BLOCKS, GROUPS_PER_BLOCK, S5_GROUP, S5_STATE)
        return jnp.einsum("bgcp,gh->bgphc", t, eye).reshape(N_BLOCKS, BLK_ST, BLK_CH).astype(BF16)

    return lam, b_blocks(bb_re), b_blocks(bb_im), c_blocks(c_re), c_blocks(c_im)


def b_block_diag(db):
    t = db.reshape(N_BLOCKS, GROUPS_PER_BLOCK, S5_GROUP, GROUPS_PER_BLOCK, S5_STATE)
    return jnp.einsum("bgchp,gh->bgcp", t, jnp.eye(GROUPS_PER_BLOCK, dtype=F32)).reshape(S5_GROUPS, S5_GROUP, S5_STATE)


def c_block_diag(dc):
    t = dc.reshape(N_BLOCKS, GROUPS_PER_BLOCK, S5_STATE, GROUPS_PER_BLOCK, S5_GROUP)
    return jnp.einsum("bgphc,gh->bgcp", t, jnp.eye(GROUPS_PER_BLOCK, dtype=F32)).reshape(S5_GROUPS, S5_GROUP, S5_STATE)


def conj(a):
    return jnp.stack([a[0], -a[1]])


def _narrow(shape):
    return len(shape) >= 2 and shape[-1] < min(HEAD_LANES, shape[-2])


def _stored(a):
    return jnp.swapaxes(a, -1, -2) if _narrow(a.shape) else a


def _stored_shape(shape):
    return tuple(shape[:-2]) + (shape[-1], shape[-2]) if _narrow(shape) else tuple(shape)


def _from_stored(a, shape):
    return jnp.swapaxes(a, -1, -2) if _narrow(shape) else a


PACK_TILE = 16 * 128


def pack_flat(parts, dtype, multiple=PACK_TILE):
    flat = [p.reshape(-1).astype(dtype) for p in parts]
    sizes = [f.shape[0] for f in flat]
    total = sum(sizes)
    pad = (-total) % multiple
    if pad:
        flat.append(jnp.zeros((pad,), dtype))
    offs = np.cumsum([0] + sizes)[:-1].tolist()
    return jnp.concatenate(flat).reshape(-1, 128), offs


def unpack_flat(buf, offs, shapes):
    flat = buf.reshape(-1)
    return [flat[o:o + int(np.prod(s))].reshape(s) for o, s in zip(offs, shapes)]


def s5_forward(p1, n_ctx, dirs):
    saved = []
    y = None
    ctx_rows, lat_rows = (0, n_ctx), (n_ctx, p1.shape[0] - n_ctx)
    zeros_tile = jnp.zeros((2, N_SEG, S5_LANES), F32)
    zeros_row = jnp.zeros((2, 1, S5_LANES), F32)
    for k, (lam, b_re, b_im, c_re, c_im) in enumerate(dirs):
        rev = k == 1
        last = 0 if rev else N_SEG - 1
        _, _, fin = s5_scan(p1, b_re, b_im, lam, zeros_tile, reverse=rev, rows=ctx_rows, name=f"s5_ctx_finals{k}")
        carry_c = s5_chain(fin, zeros_row, lam, n_ctx // N_SEG, rev, name=f"s5_ctx_chain{k}")
        _, ck_c, fin_c = s5_scan(p1, b_re, b_im, lam, carry_c, reverse=rev, want_ckpt=True, rows=ctx_rows,
                                 name=f"s5_ctx_scan{k}")
        s0 = fin_c[:, last:last + 1, :]
        _, _, fin = s5_scan(p1, b_re, b_im, lam, zeros_tile, reverse=rev, rows=lat_rows, name=f"s5_lat_finals{k}")
        carry_l = s5_chain(fin, s0, lam, lat_rows[1] // N_SEG, rev, name=f"s5_lat_chain{k}")
        y, ck_l, _ = s5_scan(p1, b_re, b_im, lam, carry_l, reverse=rev, c_re=c_re, c_im=c_im, add=y,
                             want_ckpt=True, rows=lat_rows, name=f"s5_lat_scan{k}")
        saved.append((ck_c, ck_l))
    return y, saved


def s5_backward(dy_l, du_extra_l, p1, n_ctx, dirs, saved):
    n_lat = p1.shape[0] - n_ctx
    zeros_tile = jnp.zeros((2, N_SEG, S5_LANES), F32)
    zeros_row = jnp.zeros((2, 1, S5_LANES), F32)
    dy_c = jnp.zeros((n_ctx, D_MODEL), F32)
    du_l, du_c = du_extra_l, None
    grads = []
    for k, (lam, b_re, b_im, c_re, c_im) in enumerate(dirs):
        rev = k == 1
        lam_c = conj(lam)
        ck_c, ck_l = saved[k]
        first = N_SEG - 1 if rev else 0
        _, _, fin = s5_scan(dy_l, c_re, c_im, lam_c, zeros_tile, reverse=not rev, adjoint=True,
                            name=f"s5_lat_adj_finals{k}")
        carry = s5_chain(fin, zeros_row, lam_c, n_lat // N_SEG, not rev, name=f"s5_lat_adj_chain{k}")
        whole = k == len(dirs) - 1
        du_l, dlam_l, dbr_l, dbi_l, dcr_l, dci_l, fin_a = s5_grads(
            dy_l, p1, ck_l, b_re, b_im, c_re, c_im, lam, carry, reverse=rev, add=du_l, u_off=n_ctx,
            du_rows=(p1.shape[0], n_ctx) if whole else None, name=f"s5_lat_grads{k}")
        g0 = fin_a[:, first:first + 1, :]
        carry = s5_chain(zeros_tile, g0, lam_c, n_ctx // N_SEG, not rev, name=f"s5_ctx_adj_chain{k}")
        du_c, dlam_c, dbr_c, dbi_c, _, _, _ = s5_grads(
            dy_c, p1, ck_c, b_re, b_im, c_re, c_im, lam, carry, reverse=rev, add=du_c,
            du_rows=(p1.shape[0], 0) if whole else None, du_into=du_l if whole else None, name=f"s5_ctx_grads{k}")
        dlam = jnp.sum(dlam_l + dlam_c, axis=1)
        grads.append((dlam, b_block_diag(dbr_l + dbr_c), b_block_diag(dbi_l + dbi_c),
                      c_block_diag(dcr_l), c_block_diag(dci_l)))
    return du_c, grads


def local_step(x, ctx, target, mod, w, late=None, reducer=None):
    L, Lc = x.shape[0], ctx.shape[0]
    T = L + Lc
    assert L % Lc == 0 and Lc % (2 * N_SEG) == 0 and L % GRID_W == 0
    D = D_MODEL
    X0 = (ctx, x)

    def mod_of(i, j):
        return mod[i, :, j, :][:, None, :]

    def vec(v):
        return v.reshape(1, 1, -1).astype(F32)

    g0 = vec(w["norm_g"][0])
    H0, p0 = norm_proj(X0, g0, mod_of(0, 1), mod_of(0, 0), w["mla_w_in"], Lc, "l0_norm_in")
    cq = Rows(p0, Q_LORA_RANK, col_blk=D // Q_LORA_RANK)
    ckv = Rows(p0, KV_LORA_RANK, col_blk=(D + Q_LORA_RANK) // KV_LORA_RANK)
    kr = Rows(p0, HEAD_LANES, col_blk=(D + Q_LORA_RANK + KV_LORA_RANK) // HEAD_LANES)
    qng, kvng = vec(w["mla_q_norm"]), vec(w["mla_kv_norm"])
    tabs = rope_tables(Lc, L)
    spread = jnp.pad(rotary_spread(), ((0, HEAD_LANES - QK_ROPE_DIM), (0, 0)))
    if late is not None:
        w = {**w, **late["qkv"](p0)}
    w_uq_p = pad_heads(w["mla_w_uq"], QK_DIM)
    w_ukv3 = w["mla_w_ukv"].reshape(KV_LORA_RANK, MLA_HEADS, QK_NOPE_DIM + V_HEAD_DIM)
    w_kn_p = pad_heads(w_ukv3[:, :, :QK_NOPE_DIM].reshape(KV_LORA_RANK, -1), QK_NOPE_DIM)
    w_v = w_ukv3[:, :, QK_NOPE_DIM:].reshape(KV_LORA_RANK, -1)
    qb, qn = q_heads(cq, qng, w_uq_p, tabs)
    kb, vb, kvn = kv_heads(ckv, kvng, w_kn_p, w_v, kr, spread, tabs)
    o, lse, qs = attn_fwd(qb, kb, vb, Lc)
    if late is not None:
        w = {**w, **late["out"](o)}
    X1, og, out0 = mla_post_fwd(o, p0, X0, mod_of(0, 2), w["mla_w_out"], Lc)

    if late is not None:
        w = {**w, **late["l1"](X1)}
    X1p = rows_to_segments(X1, Lc, "l1_to_segments")
    tgt_p = to_segments(target, 0)
    g1 = vec(w["norm_g"][1])
    H1, p1 = norm_proj(X1p, g1, mod_of(1, 1), mod_of(1, 0), w["s5_w_in"], Lc, "l1_norm_in")
    disc_fn = lambda *a: tuple(zip(*[s5_discretise(a[0][k], a[1][k], a[2][k], a[3][k], a[4][k]) for k in range(2)]))
    disc, disc_vjp = jax.vjp(disc_fn, w["s5_a_re"], w["s5_a_im"], w["s5_log_step"], _stored(w["s5_b_re"]),
                             _stored(w["s5_b_im"]))
    dirs = [s5_block_weights(disc[0][k], disc[1][k], disc[2][k], disc[3][k], w["s5_c_re"][k], w["s5_c_im"][k])
            for k in range(2)]
    y_ssm, s5_saved = s5_forward(p1, Lc, dirs)

    row = lambda v: v.reshape(1, D).astype(F32)
    (lvec, dX2, d_yssm, d_u_act, d_z1, d_fg, d_gt1, d_bg, d_d, gw_glu, gw_out) = s5_tail(
        y_ssm, p1, X1p, tgt_p, Lc, row(w["s5_d"]), row(w["s5_b_glu"]), mod[1, 1:2, 2, :], row(w["final_g"]),
        w["s5_w_glu"], w["s5_w_out"])
    loss = jnp.sum(lvec)
    gw = {"final_g": d_fg.reshape(D), "s5_b_glu": d_bg.reshape(D), "s5_d": d_d.reshape(D),
          "s5_w_glu": gw_glu, "s5_w_out": gw_out}
    dmod = {}

    du_p, s5_g = s5_backward(d_yssm, d_u_act, p1, Lc, dirs, s5_saved)
    d_disc = tuple(tuple(s5_g[k][j - 1].reshape(disc[j][k].shape) if j >= 2 else
                         s5_g[k][0][j].reshape(disc[j][k].shape) for k in range(2)) for j in range(4))
    gw["s5_a_re"], gw["s5_a_im"], gw["s5_log_step"], d_bt_re, d_bt_im = disc_vjp(d_disc)
    gw["s5_b_re"], gw["s5_b_im"] = jnp.swapaxes(d_bt_re, -1, -2), jnp.swapaxes(d_bt_im, -1, -2)
    gw["s5_c_re"] = jnp.stack([s5_g[0][3], s5_g[1][3]])
    gw["s5_c_im"] = jnp.stack([s5_g[0][4], s5_g[1][4]])
    gw["s5_w_in"] = mm_tn(H1, du_p, name="l1_in_dw", b_more=d_z1)
    if reducer is not None:
        g1 = g1 + reducer["l1"][0]({n: gw.pop(n) for n in LAYER1_MATS})[0, 0]
    d_X1, d_g1, d_sc1, d_sh1 = norm_proj_bwd([(du_p, 0, 0), (d_z1, D, Lc)], w["s5_w_in"], X1p, g1, mod_of(1, 1),
                                             mod_of(1, 0), dX2, Lc, "l1_norm_in_bwd", dx_from_segments=True)
    d_gt1_full = jnp.concatenate([jnp.zeros((1, 1, D), F32), d_gt1[None]], axis=0)
    dmod[1] = (d_sh1, d_sc1, d_gt1_full)

    d_o, d_z0, d_gt0, gw["mla_w_out"] = mla_post_bwd(d_X1, out0, og, o, p0, mod_of(0, 2), w["mla_w_out"], Lc)
    if reducer is not None:
        started = reducer["l1"][1](d_o)[0, 0] + reducer["out"][0]({"mla_w_out": gw.pop("mla_w_out")})[0, 0]
        tabs = (tabs[0] + started,) + tabs[1:]
    d_q, dos = attn_bwd_dq(qb, kb, vb, o, d_o, lse, tabs, Lc)
    dk_p, d_v = attn_bwd_dkv(qs, kb, vb, dos, Lc)
    if reducer is not None:
        tabs = (tabs[0] + reducer["out"][1](d_q)[0, 0],) + tabs[1:]
    d_k, d_kr = heads_unrope(dk_p, tabs, spread,
                             name="l0_k_unrope")
    d_qn = mm_nt(d_q, w_uq_p, name="l0_uq_dx")
    gw["mla_w_uq"] = unpad_heads(mm_tn(qn, d_q, name="l0_uq_dw"), QK_DIM)
    d_kvn = mm_nt(d_k, w_kn_p, name="l0_ukn_dx") + mm_nt(d_v, w_v, name="l0_uv_dx")
    dw_kn = unpad_heads(mm_tn(kvn, d_k, name="l0_ukn_dw"), QK_NOPE_DIM).reshape(KV_LORA_RANK, MLA_HEADS, QK_NOPE_DIM)
    dw_v = mm_tn(kvn, d_v, name="l0_uv_dw").reshape(KV_LORA_RANK, MLA_HEADS, V_HEAD_DIM)
    gw["mla_w_ukv"] = jnp.concatenate([dw_kn, dw_v], axis=-1).reshape(KV_LORA_RANK, -1)
    d_cq, d_qng = rowwise_bwd(f_rms, [cq], [qng], [d_qn], [0], [0], T, 0, "l0_qnorm_bwd")
    d_ckv, d_kvng = rowwise_bwd(f_rms, [ckv], [kvng], [d_kvn], [0], [0], T, 0, "l0_kvnorm_bwd")
    gw["mla_q_norm"] = d_qng.reshape(-1)
    gw["mla_kv_norm"] = d_kvng.reshape(-1)
    o_cq, o_ckv = D, D + Q_LORA_RANK
    o_kr = o_ckv + KV_LORA_RANK
    d_head = jnp.concatenate([d_cq, d_ckv, d_kr], axis=1)
    gw["mla_w_in"] = jnp.concatenate([mm_tn(H0, d_head, name="l0_in_dw_head")[:, :P0_HEAD],
                                      mm_tn(H0, d_z0, name="l0_in_dw_z")], axis=1)
    dx, d_g0, d_sc0, d_sh0 = norm_proj_bwd(
        [(d_z0, 0, 0), (d_cq, o_cq, 0), (d_ckv, o_ckv, 0), (d_kr, o_kr, 0)], w["mla_w_in"], X0, g0, mod_of(0, 1),
        mod_of(0, 0), d_X1, Lc, "l0_norm_in_bwd", latent_dx_only=True)
    dmod[0] = (d_sh0, d_sc0, d_gt0)
    gw["norm_g"] = jnp.stack([d_g0.reshape(D), d_g1.reshape(D)])
    dmod_arr = jnp.stack([jnp.stack([dmod[i][j][:, 0, :] for j in range(3)], axis=1) for i in range(2)])
    ready = {**reducer["l1"][2](dx), **reducer["out"][2](dx)} if reducer is not None else {}
    return loss, dx, dmod_arr, gw, ready


SHARDED = {
    "mla_w_in": 1, "mla_w_uq": 1, "mla_w_ukv": 1, "mla_w_out": 0,
    "s5_w_in": 1, "s5_w_glu": 0, "s5_w_out": 0, "s5_d": 0, "s5_b_glu": 0,
}
SHARDED_MATS = ["mla_w_in", "mla_w_uq", "mla_w_ukv", "mla_w_out", "s5_w_in", "s5_w_glu", "s5_w_out"]
SHARDED_VECS = ["s5_d", "s5_b_glu"]
REPLICATED = ["norm_g", "mla_q_norm", "mla_kv_norm", "s5_a_re", "s5_a_im", "s5_log_step", "s5_b_re", "s5_b_im",
              "s5_c_re", "s5_c_im", "final_g"]
WEIGHT_ORDER = ["c_ctx", "ada_w", "ada_b", "norm_g", "mla_w_in", "mla_q_norm", "mla_w_uq", "mla_kv_norm", "mla_w_ukv",
                "mla_w_out", "s5_w_in", "s5_a_re", "s5_a_im", "s5_log_step", "s5_b_re", "s5_b_im", "s5_c_re", "s5_c_im",
                "s5_d", "s5_w_glu", "s5_b_glu", "s5_w_out", "final_g"]


P0_HEAD = Q_LORA_RANK + KV_LORA_RANK + QK_ROPE_DIM


P0_WIDTH = 1536


def w_in_to_kernel_order(w):
    pad = jnp.zeros((w.shape[0], P0_WIDTH - w.shape[1]), w.dtype)
    return jnp.concatenate([w[:, P0_HEAD:], w[:, :P0_HEAD], pad], axis=1)


LAYER0_MATS = ["mla_w_in", "mla_w_uq", "mla_w_ukv", "mla_w_out"]
LAYER1_MATS = ["s5_w_in", "s5_w_glu", "s5_w_out"]


def _whole_matrices(names, own_blocks, gathered):
    chip = _chip_index(_coords())
    full = {}
    for n, own, o in zip(names, own_blocks, gathered):
        slot = lax.broadcasted_iota(jnp.int32, (N_CHIP, 1, 1), 0)
        o = jnp.where(slot == chip, own[None], o.reshape((N_CHIP,) + own.shape))
        full[n] = o.reshape(-1, o.shape[-1]) if SHARDED[n] == 0 else o.transpose(1, 0, 2).reshape(o.shape[1], -1)
    return full


FIRST_MATS = ["mla_w_in"]
LATER_GROUPS = {"qkv": ["mla_w_uq", "mla_w_ukv"], "out": ["mla_w_out"], "l1": LAYER1_MATS}


def gather_weights(ws):
    mats = [ws[n].astype(BF16) for n in FIRST_MATS]
    full = _whole_matrices(FIRST_MATS, mats, gather_halves(mats, "gather_weights"))
    full["mla_w_in"] = w_in_to_kernel_order(full["mla_w_in"])
    return full


def gather_weights_behind(ws, after):
    token, finish = 0.0, {}
    for group, names in LATER_GROUPS.items():
        mats = [ws[n].astype(BF16) for n in names]
        flight, tok = exchange_start(
            mats, [jax.ShapeDtypeStruct((N_CHIP,) + m.shape, m.dtype) for m in mats],
            [(f, a, lambda me, peer: None, a, lambda s: (_chip_index(s),))
             for a in range(len(mats)) for f in CHIP_FLIPS], f"gather_{group}_start", after=after)
        after = [tok]
        token = token + tok[0, 0]

        def finish_group(after_work, group=group, names=names, flight=flight):
            own, got = exchange_wait(flight, after_work, f"gather_{group}_wait")
            return _whole_matrices(names, own, got)

        finish[group] = finish_group
    return token, finish


def _grad_slots(gw, names):
    slots = []
    for n in names:
        g = gw[n]
        if SHARDED[n] == 0:
            slots.append(g.reshape(N_CHIP, 2, g.shape[0] // (2 * N_CHIP), g.shape[1]))
        else:
            k, n4 = g.shape
            slots.append(g.reshape(k, N_CHIP, n4 // N_CHIP).transpose(1, 0, 2)
                         .reshape(N_CHIP, 2, k // 2, n4 // N_CHIP))
    return slots


def _to_sibling_half(count):
    return [(CORE_FLIP, i, lambda me, peer: (slice(None), 1 - me[2]), i, lambda s: None) for i in range(count)]


def _to_chips(count):
    return [(f, i, lambda me, peer: (_chip_index(peer),), i, lambda s: (_chip_index(s),))
            for i in range(count) for f in CHIP_FLIPS]


def _place():
    me = _coords()
    return jnp.stack([me[2], _chip_index(me)]).astype(jnp.int32)


def reduce_behind(names, tag):
    state = {}
    count = len(names)

    def begin(gw):
        slots = _grad_slots(gw, names)
        lands = [jax.ShapeDtypeStruct((N_CHIP,) + s.shape[2:], F32) for s in slots]
        state["in"], token = exchange_start(slots, lands, _to_sibling_half(count), f"grads_{tag}_swap_in_start")
        return token

    def middle(after):
        slots, got = exchange_wait(state["in"], after, f"grads_{tag}_swap_in_wait")
        place = _place()
        sums = [pair_add(s, g, place[:1], BF16, f"grads_pair_{n}") for n, s, g in zip(names, slots, got)]
        lands = [jax.ShapeDtypeStruct(s.shape, s.dtype) for s in sums]
        state["out"], token = exchange_start(sums, lands, _to_chips(count), f"grads_{tag}_scatter_start")
        return token

    def end(after):
        sums, parts = exchange_wait(state["out"], after, f"grads_{tag}_scatter_wait")
        place = _place()
        return {n: sum_chips(p, s, place, f"grads_sum_{n}") for n, p, s in zip(names, parts, sums)}

    return begin, middle, end


def reduce_gradients(gw, ready_halves, also=None):
    me = _coords()
    place = _place()
    mat_names = [n for n in SHARDED_MATS if n not in ready_halves]
    slots = dict(zip(mat_names, _grad_slots(gw, mat_names)))
    gw = {**gw, **(also or {})}
    small_names = REPLICATED + SHARDED_VECS + list(also or {})
    small, small_offs = pack_flat([_stored(gw[n]).astype(F32) for n in small_names], F32, N_CHIP * 32 * 128)
    slots["small"] = small.reshape(N_CHIP, 2, -1, 128)
    names = list(slots)
    count = len(names)
    got = exchange([slots[n] for n in names],
                   [jax.ShapeDtypeStruct((N_CHIP,) + slots[n].shape[2:], F32) for n in names],
                   _to_sibling_half(count), [], "grads_swap_in")
    sums = [pair_add(slots[n], g, place[:1], F32 if n == "small" else BF16, f"grads_pair_{n}")
            for n, g in zip(names, got)]
    parts = exchange(sums, [jax.ShapeDtypeStruct(s.shape, s.dtype) for s in sums], _to_chips(count), [],
                     "grads_scatter")
    halves = {n: sum_chips(p, s, place, f"grads_sum_{n}") for n, p, s in zip(names, parts, sums)}
    halves.update(ready_halves)
    all_names = list(halves)
    fulls = exchange(
        [halves[n] for n in all_names], [jax.ShapeDtypeStruct(halves[n].shape, F32) for n in all_names],
        [(CORE_FLIP, i, lambda me, peer: (me[2],), i, lambda s: (s[2],)) for i in range(len(all_names))], [],
        "grads_swap_out", aliases={i: i for i in range(len(all_names))})
    out = {n: f.reshape(-1, f.shape[-1]) for n, f in zip(all_names, fulls)}
    quarter = out.pop("small")
    gather, token = exchange_start(
        [quarter], [jax.ShapeDtypeStruct((N_CHIP,) + quarter.shape, F32)],
        [(f, 0, lambda me, peer: None, 0, lambda s: (_chip_index(s),)) for f in CHIP_FLIPS],
        "grads_gather_small_start")

    def finish_small(after):
        (own,), (got_small,) = exchange_wait(gather, after, "grads_gather_small_wait")
        slot = lax.broadcasted_iota(jnp.int32, (N_CHIP, 1, 1), 0)
        small_all = jnp.where(slot == _chip_index(me), own[None], got_small)
        vals = unpack_flat(small_all, small_offs, [_stored_shape(gw[n].shape) for n in small_names])
        res = {}
        for n, v in zip(small_names, vals):
            v = _from_stored(v, gw[n].shape)
            if n in SHARDED_VECS:
                size = v.shape[0] // N_CHIP
                v = lax.dynamic_slice_in_dim(v, _chip_index(me) * size, size)
            res[n] = v
        return res

    return out, finish_small, token


def kernel(x, c, ctx, c_ctx, ada_w, ada_b, norm_g, mla_w_in, mla_q_norm, mla_w_uq, mla_kv_norm, mla_w_ukv, mla_w_out, s5_w_in, s5_a_re, s5_a_im, s5_log_step, s5_b_re, s5_b_im, s5_c_re, s5_c_im, s5_d, s5_w_glu, s5_b_glu, s5_w_out, final_g, loss_target, m_c_ctx, m_ada_w, m_ada_b, m_norm_g, m_mla_w_in, m_mla_q_norm, m_mla_w_uq, m_mla_kv_norm, m_mla_w_ukv, m_mla_w_out, m_s5_w_in, m_s5_a_re, m_s5_a_im, m_s5_log_step, m_s5_b_re, m_s5_b_im, m_s5_c_re, m_s5_c_im, m_s5_d, m_s5_w_glu, m_s5_b_glu, m_s5_w_out, m_final_g, v_c_ctx, v_ada_w, v_ada_b, v_norm_g, v_mla_w_in, v_mla_q_norm, v_mla_w_uq, v_mla_kv_norm, v_mla_w_ukv, v_mla_w_out, v_s5_w_in, v_s5_a_re, v_s5_a_im, v_s5_log_step, v_s5_b_re, v_s5_b_im, v_s5_c_re, v_s5_c_im, v_s5_d, v_s5_w_glu, v_s5_b_glu, v_s5_w_out, v_final_g):
    args = dict(locals())
    weights = {n: args[n] for n in WEIGHT_ORDER}
    D = D_MODEL
    xi, yi, ci = _coords()
    chip = 2 * xi + yi
    me = 4 * xi + 2 * yi + ci
    n_col = ada_w.shape[2]

    c_all = allgather_devices(jnp.pad(c, ((0, 7), (0, 0))), "gather_c")[:, 0, :]
    cond = jnp.concatenate([c_all, jnp.broadcast_to(c_ctx[None], (8, D))], axis=0)
    (s_cond,) = rowwise_fwd(lambda v: (_silu(v),), [cond], [], [D], [F32], 16, 0, "cond_silu")
    ada_rows = ada_w.reshape(2 * D, n_col)
    mod_cols = jnp.stack([mm_nn(s_cond, ada_rows, name=f"mod_proj{i}", b_blk=i) for i in range(2)])
    vec_tiles = [jnp.pad(weights[n][0].reshape(-1, 128), ((0, 6), (0, 0))) for n in SHARDED_VECS]
    mod_all, *vec_all = allgather_chips([mod_cols] + vec_tiles, "gather_mod")
    mod_all = mod_all.transpose(1, 2, 0, 3).reshape(2, 16, 3 * D) + ada_b[:, None, :]
    mine = lax.broadcasted_iota(jnp.int32, (1, 16, 1), 1) == me
    mod_l = jnp.sum(jnp.where(mine, mod_all, 0.0), axis=1)
    mod_c = mod_all[:, 8, :]
    mod = jnp.stack([mod_c.reshape(2, 3, D), mod_l.reshape(2, 3, D)], axis=1)

    w = gather_weights({n: weights[n][0] for n in FIRST_MATS})
    token, late = gather_weights_behind({n: weights[n][0] for n in SHARDED_MATS}, [w["mla_w_in"], mod])
    for n, v in zip(SHARDED_VECS, vec_all):
        w[n] = v[:, :2, :].reshape(-1)
    for n in ["norm_g", "final_g"]:
        w[n] = weights[n]
    for n in ["mla_q_norm", "mla_kv_norm", "s5_a_re", "s5_a_im", "s5_log_step", "s5_b_re", "s5_b_im",
              "s5_c_re", "s5_c_im"]:
        w[n] = weights[n][0]

    reducer = {"l1": reduce_behind(LAYER1_MATS, "l1"), "out": reduce_behind(["mla_w_out"], "out")}
    loss_me, dx, dmod, gw, ready = local_step(x[0], ctx[0], loss_target[0], mod + token, w, late,
                                              reducer)

    dmod_rows, loss_all = _gather([dmod.reshape(2, 2, 3 * D), jnp.broadcast_to(loss_me, (8, 128))],
                                  ALL_FLIPS, _dev_index, N_DEV, "gather_dmod")
    loss = functools.reduce(lambda s, d: s + loss_all[d, 0, 0], range(1, N_DEV), loss_all[0, 0, 0])
    dm = jnp.concatenate([dmod_rows[:, :, 1, :], dmod_rows[:, :, 0, :]], axis=0).transpose(1, 0, 2)
    g_ada_b = jnp.sum(dm, axis=1)
    dm_cols = lax.dynamic_slice_in_dim(dm, chip * n_col, n_col, axis=2)
    g_ada_w = jnp.stack([mm_tn(s_cond, dm_cols[i], name=f"mod_proj_dw{i}") for i in range(2)])
    dmc = jnp.sum(dm_cols[:, 8:, :], axis=1)
    dmc8 = jnp.broadcast_to(dmc[:, None, :], (2, 8, n_col))
    g_sc = (mm_nt(dmc8[0], ada_rows, name="mod_proj_dx0", b_rows=D, b_blk=0)[0]
            + mm_nt(dmc8[1], ada_rows, name="mod_proj_dx1", b_rows=D, b_blk=1)[0])
    g_silu_part = jnp.where(ci == 0, g_sc, 0.0)

    grads = {"ada_w": g_ada_w, "ada_b": g_ada_b}
    deltas, new_m, new_v = {}, {}, {}
    small = [n for n in WEIGHT_ORDER if weights[n].size < 50000]

    def update(n, after=None):
        shp = weights[n].shape
        rows = lambda a: _stored(a.reshape(shp)).reshape(-1, _stored_shape(shp)[-1])
        back = lambda a: _from_stored(a.reshape(_stored_shape(shp)), shp)
        d_, m_, v_ = adamw(rows(weights[n]), rows(grads[n]), rows(args["m_" + n]), rows(args["v_" + n]),
                           name=f"adamw_{n}", after=after)
        deltas[n], new_m[n], new_v[n] = back(d_), back(m_), back(v_)

    red, finish_small, small_started = reduce_gradients(gw, ready, {"silu_c_ctx": g_silu_part})
    update("ada_w", small_started)
    for n in SHARDED_MATS:
        grads[n] = red[n].reshape(weights[n].shape)
        update(n, small_started)
    red_small = finish_small([deltas[n] for n in ["ada_w"] + SHARDED_MATS])
    for n in REPLICATED + SHARDED_VECS:
        grads[n] = red_small[n].reshape(weights[n].shape)
    (g_c_ctx,) = rowwise_bwd(lambda v: (_silu(v),), [jnp.broadcast_to(c_ctx[None], (8, D))], [],
                             [jnp.broadcast_to(red_small["silu_c_ctx"][None], (8, D))], [0], [], 8, 0, "cond_silu_bwd")
    grads["c_ctx"] = g_c_ctx[0]
    for n in WEIGHT_ORDER:
        if n not in small and n not in deltas:
            update(n)
    packs = []
    offs = None
    for src in (weights, grads, {n: args["m_" + n] for n in small}, {n: args["v_" + n] for n in small}):
        buf, offs = pack_flat([src[n] for n in small], F32)
        packs.append(buf)
    outs = adamw(*packs, name="adamw_small")
    for res, dst in zip(outs, (deltas, new_m, new_v)):
        for n, val in zip(small, unpack_flat(res, offs, [weights[n].shape for n in small])):
            dst[n] = val

    return (loss, dx[None], *[grads[n] for n in WEIGHT_ORDER], *[deltas[n] for n in WEIGHT_ORDER],
            *[new_m[n] for n in WEIGHT_ORDER], *[new_v[n] for n in WEIGHT_ORDER])
```

```python
import functools
import math

import jax
import jax.numpy as jnp
import numpy as np
from jax import lax
from jax.experimental import pallas as pl
from jax.experimental.pallas import tpu as pltpu

F32 = jnp.float32
BF16 = jnp.bfloat16

D_MODEL = 1024
GRID_W = 64
EPS = 1e-6
MLA_HEADS = 16
QK_NOPE_DIM = 64
QK_ROPE_DIM = 32
V_HEAD_DIM = 64
Q_LORA_RANK = 256
KV_LORA_RANK = 128
QK_DIM = QK_NOPE_DIM + QK_ROPE_DIM
SOFTMAX_SCALE = QK_DIM ** -0.5
ROPE_THETA = 10000.0
S5_GROUP = 16
S5_GROUPS = D_MODEL // S5_GROUP
S5_STATE = 64
S5_LANES = S5_GROUPS * S5_STATE
N_SEG = 8
GROUPS_PER_BLOCK = 8
N_BLOCKS = S5_GROUPS // GROUPS_PER_BLOCK
BLK_CH = GROUPS_PER_BLOCK * S5_GROUP
BLK_ST = GROUPS_PER_BLOCK * S5_STATE

ADAM_LR = 0.001
ADAM_B1 = 0.9
ADAM_B2 = 0.999
ADAM_EPS = 1e-08
ADAM_WD = 0.01
ADAM_STEP = 10

N_DEV = 8
N_CHIP = 4
MESH = pl.DeviceIdType.MESH
VMEM_LIMIT = 52 * 1024 * 1024
ROW_TILE = 256


def _params(sem=None, vmem=None):
    return pltpu.CompilerParams(dimension_semantics=sem, vmem_limit_bytes=vmem)


def mm_nn(a, b, out_dtype=F32, name="mm_nn", b_blk=0):
    M, K = a.shape
    N = b.shape[1]
    tm = math.gcd(ROW_TILE, M)

    def body(a_ref, b_ref, o_ref):
        o_ref[...] = jnp.dot(a_ref[...].astype(BF16), b_ref[...].astype(BF16),
                             preferred_element_type=F32).astype(o_ref.dtype)

    return pl.pallas_call(
        body, out_shape=jax.ShapeDtypeStruct((M, N), out_dtype), grid=(M // tm,),
        in_specs=[pl.BlockSpec((tm, K), lambda i: (i, 0)), pl.BlockSpec((K, N), lambda i: (b_blk, 0))],
        out_specs=pl.BlockSpec((tm, N), lambda i: (i, 0)),
        compiler_params=_params(("parallel",), VMEM_LIMIT), name=name)(a, b)


def mm_nt(a, b, out_dtype=F32, name="mm_nt", b_rows=None, b_blk=0):
    M, N = a.shape
    K = b.shape[0] if b_rows is None else b_rows
    tm = math.gcd(ROW_TILE, M)

    def body(a_ref, b_ref, o_ref):
        o_ref[...] = lax.dot_general(a_ref[...].astype(BF16), b_ref[...].astype(BF16),
                                     (((1,), (1,)), ((), ())),
                                     preferred_element_type=F32).astype(o_ref.dtype)

    return pl.pallas_call(
        body, out_shape=jax.ShapeDtypeStruct((M, K), out_dtype), grid=(M // tm,),
        in_specs=[pl.BlockSpec((tm, N), lambda i: (i, 0)), pl.BlockSpec((K, N), lambda i: (b_blk, 0))],
        out_specs=pl.BlockSpec((tm, K), lambda i: (i, 0)),
        compiler_params=_params(("parallel",), VMEM_LIMIT), name=name)(a, b)


def mm_tn(a, b, name="mm_tn", b_more=None):
    M, N = b.shape
    K = a.shape[1]
    tn = math.gcd(512, N) if N % 128 == 0 and N > 512 else N
    bs = [b] if b_more is None else [b, b_more]
    assert all(x.shape[1] == N for x in bs)
    nb = N // tn

    def body(a_ref, *refs):
        o_ref = refs[-1]
        for k, b_ref in enumerate(refs[:-1]):
            def product(b_ref=b_ref):
                o_ref[...] = lax.dot_general(a_ref[a.shape[0] - b_ref.shape[0]:, :].astype(BF16),
                                             b_ref[...].astype(BF16), (((0,), (0,)), ((), ())),
                                             preferred_element_type=F32)
            if len(bs) == 1:
                product()
            else:
                pl.when(pl.program_id(0) // nb == k)(product)

    def b_spec(k, rows):
        return pl.BlockSpec((rows, tn), lambda j: (0, jnp.clip(j - k * nb, 0, nb - 1)))

    return pl.pallas_call(
        body, out_shape=jax.ShapeDtypeStruct((K, N * len(bs)), F32), grid=(nb * len(bs),),
        in_specs=[pl.BlockSpec(a.shape, lambda j: (0, 0))] + [b_spec(k, x.shape[0]) for k, x in enumerate(bs)],
        out_specs=pl.BlockSpec((K, tn), lambda j: (0, j)),
        compiler_params=_params(("parallel",), VMEM_LIMIT), name=name)(a, *bs)


class Rows:
    def __init__(self, arr, width=None, row_off=0, col_blk=0):
        self.arr = arr
        self.width = arr.shape[1] if width is None else width
        self.row_off = row_off
        self.col_blk = col_blk

    def spec(self, tm):
        ro, cb = self.row_off // tm, self.col_blk
        return pl.BlockSpec((tm, self.width), lambda i: (i + ro, cb))


def _as_rows(x):
    return x if isinstance(x, Rows) else Rows(x)


def _row_tile(n_rows, n_ctx_rows, rows):
    tm = math.gcd(ROW_TILE, n_rows, n_ctx_rows)
    for r in rows:
        tm = math.gcd(tm, r.row_off)
    return tm


def _bc_spec(arr, n_ctx_blocks):
    g, _, d = arr.shape
    if g == 1:
        return pl.BlockSpec((1, 1, d), lambda i: (0, 0, 0))
    return pl.BlockSpec((1, 1, d), lambda i: ((i >= n_ctx_blocks).astype(jnp.int32), 0, 0))


def rowwise_fwd(fn, rows, bcs, out_dims, out_dtypes, n_rows, n_ctx_rows, name):
    rows = [_as_rows(r) for r in rows]
    tm = _row_tile(n_rows, n_ctx_rows, rows)
    ncb = n_ctx_rows // tm
    nr, nb = len(rows), len(bcs)

    def body(*refs):
        vals = [r[...].astype(F32) for r in refs[:nr]] + [b[0].astype(F32) for b in refs[nr:nr + nb]]
        outs = fn(*vals)
        for o_ref, v in zip(refs[nr + nb:], outs):
            o_ref[...] = v.astype(o_ref.dtype)

    outs = pl.pallas_call(
        body,
        out_shape=[jax.ShapeDtypeStruct((n_rows, d), dt) for d, dt in zip(out_dims, out_dtypes)],
        grid=(n_rows // tm,),
        in_specs=[r.spec(tm) for r in rows] + [_bc_spec(b, ncb) for b in bcs],
        out_specs=[pl.BlockSpec((tm, d), lambda i: (i, 0)) for d in out_dims],
        compiler_params=_params(("parallel",), VMEM_LIMIT), name=name)(*[r.arr for r in rows], *bcs)
    return outs


def rowwise_bwd(fn, rows, bcs, cts, diff_rows, diff_bcs, n_rows, n_ctx_rows, name, ct_extra=None, lat_add=None):
    rows = [_as_rows(r) for r in rows]
    cts = [_as_rows(c) for c in cts]
    extra = [_as_rows(ct_extra)] if ct_extra is not None else []
    tm = _row_tile(n_rows, n_ctx_rows, rows + cts + extra)
    ncb = n_ctx_rows // tm
    nr, nb, nc = len(rows), len(bcs), len(cts)
    ndr, ndb = len(diff_rows), len(diff_bcs)
    n_in = nr + nb + nc + len(extra) + (lat_add is not None)

    def body(*refs):
        i = pl.program_id(0)
        rvals = [r[...].astype(F32) for r in refs[:nr]]
        bvals = [b[0].astype(F32) for b in refs[nr:nr + nb]]
        cvals = [c[...].astype(F32) for c in refs[nr + nb:nr + nb + nc]]
        if extra:
            cvals[0] = cvals[0] + refs[nr + nb + nc][...].astype(F32)
        outs = refs[n_in:]

        def f(*d):
            rv, bv = list(rvals), list(bvals)
            for k, idx in enumerate(diff_rows):
                rv[idx] = d[k]
            for k, idx in enumerate(diff_bcs):
                bv[idx] = d[ndr + k]
            return tuple(fn(*rv, *bv))

        primals = [rvals[k] for k in diff_rows] + [bvals[k] for k in diff_bcs]
        _, vjp = jax.vjp(f, *primals)
        grads = list(vjp(tuple(cvals)))
        if lat_add is not None:
            add = refs[n_in - 1][...]
            grads[0] = grads[0] + (add if lat_add.shape[0] == n_rows else jnp.where(i >= ncb, add, 0.0))
        for k in range(ndr):
            outs[k][...] = grads[k].astype(outs[k].dtype)
        for k, idx in enumerate(diff_bcs):
            o_ref = outs[ndr + k]
            first = (i == 0)
            if bcs[idx].shape[0] == 2:
                first = first | (i == ncb)

            @pl.when(first)
            def _(o_ref=o_ref):
                o_ref[...] = jnp.zeros_like(o_ref)

            o_ref[0] += grads[ndr + k]

    out_shape = [jax.ShapeDtypeStruct((n_rows, rows[k].width), F32) for k in diff_rows]
    out_shape += [jax.ShapeDtypeStruct(bcs[k].shape, F32) for k in diff_bcs]
    out_specs = [pl.BlockSpec((tm, rows[k].width), lambda i: (i, 0)) for k in diff_rows]
    out_specs += [_bc_spec(bcs[k], ncb) for k in diff_bcs]
    ins = [r.arr for r in rows] + list(bcs) + [c.arr for c in cts + extra]
    in_specs = [r.spec(tm) for r in rows] + [_bc_spec(b, ncb) for b in bcs] + [c.spec(tm) for c in cts + extra]
    if lat_add is not None:
        ins.append(lat_add)
        skip = ncb if lat_add.shape[0] != n_rows else 0
        in_specs.append(pl.BlockSpec((tm, lat_add.shape[1]), lambda i: (jnp.maximum(i - skip, 0), 0)))
    outs = pl.pallas_call(
        body, out_shape=out_shape, grid=(n_rows // tm,), in_specs=in_specs, out_specs=out_specs,
        compiler_params=_params(("arbitrary",), VMEM_LIMIT), name=name)(*ins)
    return outs


def _rms(x):
    return x * lax.rsqrt(jnp.mean(x * x, axis=-1, keepdims=True) + EPS)


def _sigmoid(x):
    return 0.5 * (jnp.tanh(0.5 * x) + 1.0)


def _silu(x):
    return x * _sigmoid(x)


def _gelu_tanh(x):
    return 0.5 * x * (1.0 + jnp.tanh(math.sqrt(2.0 / math.pi) * (x + 0.044715 * (x * x * x))))


def f_norm_mod(x, g, sc, sh):
    return ((_rms(x) * g) * (1.0 + sc) + sh,)


def f_rms(x, g):
    return (_rms(x) * g,)


def f_gate(o, z):
    return (o * _silu(z),)


def f_s5_act(y, u, d):
    return (_gelu_tanh(y + d * u),)


def f_s5_glu(ya, gl, z, b):
    return (ya * _sigmoid(gl + b) * _silu(z),)


def _as_parts(x, n_ctx):
    xs = x if isinstance(x, tuple) else (x,)
    assert len(xs) == 1 or xs[0].shape[0] == n_ctx
    return xs, sum(p.shape[0] for p in xs)


def _parts_specs(xs, tm, ncb):
    d = xs[0].shape[1]
    if len(xs) == 1:
        return [pl.BlockSpec((tm, d), lambda i: (i, 0))]
    return [pl.BlockSpec((tm, d), lambda i: (jnp.minimum(i, ncb - 1), 0)),
            pl.BlockSpec((tm, d), lambda i: (jnp.maximum(i - ncb, 0), 0))]


def _parts_tile(x_refs, ncb):
    if len(x_refs) == 1:
        return x_refs[0][...]
    return jnp.where(pl.program_id(0) < ncb, x_refs[0][...], x_refs[1][...])


def norm_proj(x, g, sc, sh, w, n_ctx, name):
    xs, n = _as_parts(x, n_ctx)
    d = xs[0].shape[1]
    nw = w.shape[1]
    tm = math.gcd(ROW_TILE, n, n_ctx)
    ncb = n_ctx // tm
    nx = len(xs)

    def body(*refs):
        g_ref, sc_ref, sh_ref, w_ref, h_ref, p_ref = refs[nx:]
        h = f_norm_mod(_parts_tile(refs[:nx], ncb), g_ref[0], sc_ref[0], sh_ref[0])[0].astype(BF16)
        h_ref[...] = h
        p_ref[...] = jnp.dot(h, w_ref[...], preferred_element_type=F32)

    row = pl.BlockSpec((tm, d), lambda i: (i, 0))
    return pl.pallas_call(
        body, out_shape=[jax.ShapeDtypeStruct((n, d), BF16), jax.ShapeDtypeStruct((n, nw), F32)], grid=(n // tm,),
        in_specs=_parts_specs(xs, tm, ncb) + [_bc_spec(g, ncb), _bc_spec(sc, ncb), _bc_spec(sh, ncb),
                                              pl.BlockSpec(w.shape, lambda i: (0, 0))],
        out_specs=[row, pl.BlockSpec((tm, nw), lambda i: (i, 0))],
        compiler_params=_params(("parallel",), VMEM_LIMIT), name=name)(*xs, g, sc, sh, w)


def norm_proj_bwd(terms, w, x, g, sc, sh, add, n_ctx, name, latent_dx_only=False, dx_from_segments=False):
    xs, n = _as_parts(x, n_ctx)
    adds = add if isinstance(add, tuple) else (add,)
    d = xs[0].shape[1]
    tm = math.gcd(ROW_TILE, n, n_ctx, *[t[2] for t in terms])
    ncb = n_ctx // tm
    nt, nx, na = len(terms), len(xs), len(adds)
    add_skip = ncb if na == 1 and add.shape[0] != n else 0
    n_dx = 2 if dx_from_segments else 1
    tj = tm // N_SEG
    assert not dx_from_segments or (ncb == 1 and not latent_dx_only)

    def body(*refs):
        i = pl.program_id(0)
        a_refs = refs[:nt]
        w_ref, x_refs = refs[nt], refs[nt + 1:nt + 1 + nx]
        g_ref, sc_ref, sh_ref = refs[nt + 1 + nx:nt + 4 + nx]
        add_refs = refs[nt + 4 + nx:nt + 4 + nx + na]
        outs = refs[nt + 4 + nx + na:]
        dx_refs, (dg_ref, dsc_ref, dsh_ref) = outs[:n_dx], outs[n_dx:n_dx + 3]
        d_h = None
        for a_ref, (a, off, first) in zip(a_refs, terms):
            part = lax.dot_general(a_ref[...].astype(BF16), w_ref[:, off:off + a.shape[1]], NT_DIMS,
                                   preferred_element_type=F32)
            if first:
                part = jnp.where(i >= first // tm, part, 0.0)
            d_h = part if d_h is None else d_h + part
        _, vjp = jax.vjp(lambda x_, g_, sc_, sh_: f_norm_mod(x_, g_, sc_, sh_), _parts_tile(x_refs, ncb), g_ref[0],
                         sc_ref[0], sh_ref[0])
        d_x, d_g, d_sc, d_sh = vjp((d_h,))
        extra = _parts_tile(add_refs, ncb)
        d_x = d_x + (extra if add_skip == 0 else jnp.where(i >= ncb, extra, 0.0))
        if dx_from_segments:
            slabs = outs[n_dx + 3]
            for c in range(d // HEAD_LANES):
                slabs[c] = d_x[:, c * HEAD_LANES:(c + 1) * HEAD_LANES]
            for k, here in enumerate([i < ncb, i >= ncb]):
                @pl.when(here)
                def _(k=k):
                    for c in range(d // HEAD_LANES):
                        for seg in range(N_SEG):
                            dx_refs[k][seg, :, c * HEAD_LANES:(c + 1) * HEAD_LANES] = (
                                slabs[c, pl.ds(seg, tj, stride=N_SEG), :])
        else:
            dx_refs[0][...] = d_x

        @pl.when(i == 0)
        def _():
            dg_ref[...] = jnp.zeros_like(dg_ref)

        @pl.when((i == 0) | (i == ncb))
        def _():
            dsc_ref[...] = jnp.zeros_like(dsc_ref)
            dsh_ref[...] = jnp.zeros_like(dsh_ref)

        dg_ref[0] += d_g
        dsc_ref[0] += d_sc
        dsh_ref[0] += d_sh

    def a_spec(a, first):
        skip = first // tm
        return pl.BlockSpec((tm, a.shape[1]), lambda i: (jnp.maximum(i - skip, 0), 0))

    dx_skip = ncb if latent_dx_only else 0
    if dx_from_segments:
        dx_shapes = [jax.ShapeDtypeStruct((N_SEG, n_ctx // N_SEG, d), F32),
                     jax.ShapeDtypeStruct((N_SEG, (n - n_ctx) // N_SEG, d), F32)]
        dx_specs = [pl.BlockSpec((N_SEG, tj, d), lambda i: (0, 0, 0)),
                    pl.BlockSpec((N_SEG, tj, d), lambda i: (0, jnp.maximum(i - ncb, 0), 0))]
    else:
        dx_shapes = [jax.ShapeDtypeStruct((n - dx_skip * tm, d), F32)]
        dx_specs = [pl.BlockSpec((tm, d), lambda i: (jnp.maximum(i - dx_skip, 0), 0))]
    add_specs = (_parts_specs(adds, tm, ncb) if na == 2 else
                 [pl.BlockSpec((tm, d), lambda i: (jnp.maximum(i - add_skip, 0), 0))])
    res = pl.pallas_call(
        body,
        out_shape=dx_shapes + [jax.ShapeDtypeStruct(g.shape, F32), jax.ShapeDtypeStruct(sc.shape, F32),
                               jax.ShapeDtypeStruct(sh.shape, F32)],
        grid=(n // tm,),
        in_specs=[a_spec(a, first) for a, _, first in terms]
        + [pl.BlockSpec(w.shape, lambda i: (0, 0))] + _parts_specs(xs, tm, ncb)
        + [_bc_spec(g, ncb), _bc_spec(sc, ncb), _bc_spec(sh, ncb)] + add_specs,
        out_specs=dx_specs + [_bc_spec(g, ncb), _bc_spec(sc, ncb), _bc_spec(sh, ncb)],
        scratch_shapes=[pltpu.VMEM((d // HEAD_LANES, tm, HEAD_LANES), F32)] if dx_from_segments else [],
        compiler_params=_params(("arbitrary",), VMEM_LIMIT), name=name)(
            *[t[0] for t in terms], w, *xs, g, sc, sh, *adds)
    if dx_from_segments:
        return ((res[0].reshape(n_ctx, d), res[1].reshape(n - n_ctx, d)), *res[2:])
    return res


def mla_post_fwd(o, p0, x0, gate, w_out, n_ctx, name="l0_post"):
    n, d = o.shape
    xs, _ = _as_parts(x0, n_ctx)
    tm = math.gcd(ROW_TILE, n, n_ctx)
    ncb = n_ctx // tm
    nx = len(xs)

    def body(o_ref, z_ref, *refs):
        gt_ref, w_ref, x1_ref, og_ref, out_ref = refs[nx:]
        og = f_gate(o_ref[...], z_ref[...])[0].astype(BF16)
        out = jnp.dot(og, w_ref[...], preferred_element_type=F32)
        og_ref[...] = og
        out_ref[...] = out
        x1_ref[...] = _parts_tile(refs[:nx], ncb) + gt_ref[0] * out

    row = pl.BlockSpec((tm, d), lambda i: (i, 0))
    return pl.pallas_call(
        body, out_shape=[jax.ShapeDtypeStruct((n, d), F32), jax.ShapeDtypeStruct((n, d), BF16),
                         jax.ShapeDtypeStruct((n, d), F32)],
        grid=(n // tm,),
        in_specs=[row, row] + _parts_specs(xs, tm, ncb) + [_bc_spec(gate, ncb), pl.BlockSpec((d, d), lambda i: (0, 0))],
        out_specs=[row, row, row],
        compiler_params=_params(("parallel",), VMEM_LIMIT), name=name)(o, p0, *xs, gate, w_out)


def mla_post_bwd(dx1, out, og, o, p0, gate, w_out, n_ctx, name="l0_post_bwd"):
    n, d = o.shape
    dxs, _ = _as_parts(dx1, n_ctx)
    tm = math.gcd(ROW_TILE, n, n_ctx)
    ncb = n_ctx // tm
    nx = len(dxs)

    def body(*refs):
        out_ref, og_ref, o_ref, z_ref, gt_ref, w_ref, do_ref, dz_ref, dgt_ref, dw_ref = refs[nx:]
        i = pl.program_id(0)

        @pl.when(i == 0)
        def _():
            dw_ref[...] = jnp.zeros_like(dw_ref)

        @pl.when((i == 0) | (i == ncb))
        def _():
            dgt_ref[...] = jnp.zeros_like(dgt_ref)

        dx = _parts_tile(refs[:nx], ncb)
        dgt_ref[0] += jnp.sum(dx * out_ref[...], axis=0, keepdims=True)
        d_out16 = (gt_ref[0] * dx).astype(BF16)
        dw_ref[...] += lax.dot_general(og_ref[...], d_out16, (((0,), (0,)), ((), ())), preferred_element_type=F32)
        d_og = lax.dot_general(d_out16, w_ref[...], NT_DIMS, preferred_element_type=F32)
        _, gate_vjp = jax.vjp(lambda o_, z_: f_gate(o_, z_), o_ref[...], z_ref[...])
        d_o, d_z = gate_vjp((d_og,))
        do_ref[...] = d_o
        dz_ref[...] = d_z

    row = pl.BlockSpec((tm, d), lambda i: (i, 0))
    mat = pl.BlockSpec((d, d), lambda i: (0, 0))
    return pl.pallas_call(
        body, out_shape=[jax.ShapeDtypeStruct((n, d), F32), jax.ShapeDtypeStruct((n, d), F32),
                         jax.ShapeDtypeStruct(gate.shape, F32), jax.ShapeDtypeStruct((d, d), F32)],
        grid=(n // tm,),
        in_specs=_parts_specs(dxs, tm, ncb) + [row, row, row, row, _bc_spec(gate, ncb), mat],
        out_specs=[row, row, _bc_spec(gate, ncb), mat],
        compiler_params=_params(("arbitrary",), VMEM_LIMIT), name=name)(*dxs, out, og, o, p0, gate, w_out)


def s5_tail(y_ssm, p1, x1p, target, n_ctx, d_vec, b_glu, gate, final_g, w_glu, w_out, name="l1_tail"):
    n, d = y_ssm.shape
    tm = math.gcd(ROW_TILE, n, n_ctx)
    off = n_ctx // tm
    tn_dims = (((0,), (0,)), ((), ()))

    def row_loss(x, g, t):
        e = _rms(x) * g - t
        return 0.5 * (e * e) * (1.0 / d)

    def body(y_ref, u_ref, z_ref, x1_ref, t_ref, d_ref, b_ref, gt_ref, fg_ref, wg_ref, wo_ref,
             l_ref, dx_ref, dy_ref, du_ref, dz_ref, dfg_ref, dgt_ref, db_ref, dd_ref, dwg_ref, dwo_ref):
        @pl.when(pl.program_id(0) == 0)
        def _():
            for r in (l_ref, dfg_ref, dgt_ref, db_ref, dd_ref, dwg_ref, dwo_ref):
                r[...] = jnp.zeros_like(r)

        u, z, tgt, gt = u_ref[...], z_ref[...], t_ref[...], gt_ref[...]
        (ya,), act_vjp = jax.vjp(lambda y_, u_, d_: f_s5_act(y_, u_, d_), y_ref[...], u, d_ref[...])
        ya16 = ya.astype(BF16)
        gl = jnp.dot(ya16, wg_ref[...], preferred_element_type=F32)
        (y3,), glu_vjp = jax.vjp(lambda a_, g_, z_, b_: f_s5_glu(a_, g_, z_, b_), ya, gl, z, b_ref[...])
        y3_16 = y3.astype(BF16)
        out1 = jnp.dot(y3_16, wo_ref[...], preferred_element_type=F32)
        lterm, loss_vjp = jax.vjp(lambda x_, g_: row_loss(x_, g_, tgt), x1_ref[...] + gt * out1, fg_ref[...])
        dx2, dfg = loss_vjp(jnp.ones_like(lterm))
        l_ref[...] += jnp.sum(lterm, axis=0, keepdims=True)
        dfg_ref[...] += dfg
        dx_ref[...] = dx2
        dgt_ref[...] += jnp.sum(dx2 * out1, axis=0, keepdims=True)
        d_out16 = (gt * dx2).astype(BF16)
        dwo_ref[...] += lax.dot_general(y3_16, d_out16, tn_dims, preferred_element_type=F32)
        d_y3 = lax.dot_general(d_out16, wo_ref[...], NT_DIMS, preferred_element_type=F32)
        d_ya, d_gl, d_z, d_b = glu_vjp((d_y3,))
        dz_ref[...] = d_z
        db_ref[...] += d_b
        d_gl16 = d_gl.astype(BF16)
        dwg_ref[...] += lax.dot_general(ya16, d_gl16, tn_dims, preferred_element_type=F32)
        d_ya = d_ya + lax.dot_general(d_gl16, wg_ref[...], NT_DIMS, preferred_element_type=F32)
        d_y, d_u, d_d = act_vjp((d_ya,))
        dy_ref[...] = d_y
        du_ref[...] = d_u
        dd_ref[...] += d_d

    row = pl.BlockSpec((tm, d), lambda i: (i, 0))
    vecs = pl.BlockSpec((1, d), lambda i: (0, 0))
    mat = pl.BlockSpec((d, d), lambda i: (0, 0))
    return pl.pallas_call(
        body,
        out_shape=[jax.ShapeDtypeStruct((1, d), F32)] + [jax.ShapeDtypeStruct((n, d), F32)] * 4
        + [jax.ShapeDtypeStruct((1, d), F32)] * 4 + [jax.ShapeDtypeStruct((d, d), F32)] * 2,
        grid=(n // tm,),
        in_specs=[row, pl.BlockSpec((tm, d), lambda i: (i + off, 0)), pl.BlockSpec((tm, d), lambda i: (i + off, 1)),
                  pl.BlockSpec((tm, d), lambda i: (i + off, 0)), row, vecs, vecs, vecs, vecs, mat, mat],
        out_specs=[vecs, row, row, row, row, vecs, vecs, vecs, vecs, mat, mat],
        compiler_params=_params(("arbitrary",), VMEM_LIMIT), name=name)(
            y_ssm, p1, p1, x1p, target, d_vec, b_glu, gate, final_g, w_glu, w_out)


NT_DIMS = (((1,), (1,)), ((), ()))
HEAD_LANES = 128
N_PAIRS = MLA_HEADS // 2


def _own_lanes(shape, hh):
    lane = lax.broadcasted_iota(jnp.int32, shape, len(shape) - 1)
    return (lane < V_HEAD_DIM) if hh == 0 else (lane >= V_HEAD_DIM)


def _delta_lane(hh):
    return V_HEAD_DIM if hh == 0 else 0


def _rope_tiles(x, cos, sin_next, sin_prev, inverse):
    width = x.shape[-1]
    reps = width // HEAD_LANES
    c, sn, sp = (jnp.tile(t, (1, reps)) for t in (cos, sin_next, sin_prev))
    if inverse:
        return x * c + pltpu.roll(x * sn, 8, 1) + pltpu.roll(x * sp, width - 8, 1)
    return x * c + pltpu.roll(x, width - 8, 1) * sn + pltpu.roll(x, 8, 1) * sp


STAT_LANE = QK_DIM


def _with_stat(x16, col, lane0):
    hi = col.astype(BF16)
    r1 = col - hi.astype(F32)
    mid = r1.astype(BF16)
    lo = (r1 - mid.astype(F32)).astype(BF16)
    lane = lax.broadcasted_iota(jnp.int32, x16.shape, 1)
    return jnp.where(lane == lane0, hi, jnp.where(lane == lane0 + 1, mid, jnp.where(lane == lane0 + 2, lo, x16)))


def attn_fwd(qb, kb, vb, n_ctx):
    T = qb.shape[0]
    tq = math.gcd(ROW_TILE, n_ctx)
    nq, ncb = T // tq, n_ctx // tq

    def body(q_ref, k_ref, v_ref, o_ref, lse_ref, qs_ref):
        qi = pl.program_id(1)

        def rows(n_keys):
            v = v_ref[:n_keys, :]
            outs = []
            for hh in range(2):
                hs = slice(hh * HEAD_LANES, (hh + 1) * HEAD_LANES)
                s = lax.dot_general(q_ref[:, hs], k_ref[:n_keys, hs], NT_DIMS,
                                    preferred_element_type=F32) * SOFTMAX_SCALE
                m = jnp.max(s, axis=-1, keepdims=True)
                p = jnp.exp(s - m)
                l = jnp.sum(p, axis=-1, keepdims=True)
                outs.append(jnp.dot(p.astype(BF16), v, preferred_element_type=F32) / l)
                lse = m + jnp.log(l)
                lse_ref[hh] = lse
                qs_ref[:, hs] = _with_stat(q_ref[:, hs], lse * (-1.0 / SOFTMAX_SCALE), STAT_LANE)
            o_ref[...] = jnp.where(_own_lanes(outs[0].shape, 0), outs[0], outs[1])

        pl.when(qi < ncb)(lambda: rows(n_ctx))
        pl.when(qi >= ncb)(lambda: rows(T))

    return pl.pallas_call(
        body,
        out_shape=[jax.ShapeDtypeStruct((T, MLA_HEADS * V_HEAD_DIM), F32),
                   jax.ShapeDtypeStruct((MLA_HEADS, T, 1), F32), jax.ShapeDtypeStruct(qb.shape, BF16)],
        grid=(N_PAIRS, nq),
        in_specs=[pl.BlockSpec((tq, 2 * HEAD_LANES), lambda h, i: (i, h)),
                  pl.BlockSpec((T, 2 * HEAD_LANES), lambda h, i: (0, h)),
                  pl.BlockSpec((T, 2 * V_HEAD_DIM), lambda h, i: (0, h))],
        out_specs=[pl.BlockSpec((tq, 2 * V_HEAD_DIM), lambda h, i: (i, h)),
                   pl.BlockSpec((2, tq, 1), lambda h, i: (h, i, 0)),
                   pl.BlockSpec((tq, 2 * HEAD_LANES), lambda h, i: (i, h))],
        compiler_params=_params(("parallel", "parallel"), VMEM_LIMIT), name="attn_fwd")(qb, kb, vb)


def attn_bwd_dq(qb, kb, vb, o, do, lse, tabs, n_ctx):
    T = qb.shape[0]
    tq = math.gcd(ROW_TILE, n_ctx)
    nq, ncb = T // tq, n_ctx // tq

    def body(q_ref, k_ref, v_ref, o_ref, do_ref, lse_ref, c_ref, sn_ref, sp_ref, dq_ref, dos_ref):
        qi = pl.program_id(1)

        def rows(n_keys):
            v = v_ref[:n_keys, :]
            dqs = []
            for hh in range(2):
                hs = slice(hh * HEAD_LANES, (hh + 1) * HEAD_LANES)
                k = k_ref[:n_keys, hs]
                do = jnp.where(_own_lanes(do_ref.shape, hh), do_ref[...], 0.0)
                delta = jnp.sum(do * o_ref[...], axis=-1, keepdims=True)
                s = lax.dot_general(q_ref[:, hs], k, NT_DIMS, preferred_element_type=F32) * SOFTMAX_SCALE
                p = jnp.exp(s - lse_ref[hh])
                do16 = do.astype(BF16)
                dp = lax.dot_general(do16, v, NT_DIMS, preferred_element_type=F32)
                ds = p * (dp - delta) * SOFTMAX_SCALE
                dqs.append(jnp.dot(ds.astype(BF16), k, preferred_element_type=F32))
                dos_ref[:, hs] = _with_stat(do16, delta, _delta_lane(hh))
            dq = jnp.concatenate(dqs, axis=1)
            dq_ref[...] = _rope_tiles(dq, c_ref[...], sn_ref[...], sp_ref[...], True).astype(BF16)

        pl.when(qi < ncb)(lambda: rows(n_ctx))
        pl.when(qi >= ncb)(lambda: rows(T))

    tab = pl.BlockSpec((tq, HEAD_LANES), lambda h, i: (i, 0))
    return pl.pallas_call(
        body,
        out_shape=[jax.ShapeDtypeStruct((T, MLA_HEADS * HEAD_LANES), BF16)] * 2,
        grid=(N_PAIRS, nq),
        in_specs=[pl.BlockSpec((tq, 2 * HEAD_LANES), lambda h, i: (i, h)),
                  pl.BlockSpec((T, 2 * HEAD_LANES), lambda h, i: (0, h)),
                  pl.BlockSpec((T, 2 * V_HEAD_DIM), lambda h, i: (0, h)),
                  pl.BlockSpec((tq, 2 * V_HEAD_DIM), lambda h, i: (i, h)),
                  pl.BlockSpec((tq, 2 * V_HEAD_DIM), lambda h, i: (i, h)),
                  pl.BlockSpec((2, tq, 1), lambda h, i: (h, i, 0)), tab, tab, tab],
        out_specs=[pl.BlockSpec((tq, 2 * HEAD_LANES), lambda h, i: (i, h))] * 2,
        compiler_params=_params(("parallel", "parallel"), VMEM_LIMIT), name="attn_bwd_dq")(
            qb, kb, vb, o, do, lse, *tabs)


def attn_bwd_dkv(qs, kb, vb, dos, n_ctx):
    T = qs.shape[0]
    tq = math.gcd(ROW_TILE, n_ctx)
    nq, ncb = T // tq, n_ctx // tq

    def body(q_ref, do_ref, k_ref, v_ref, dk_ref, dv_ref):
        kj = pl.program_id(1)

        def cols(first):
            v = v_ref[...]
            lane = lax.broadcasted_iota(jnp.int32, v.shape, 1)
            dvs = []
            for hh in range(2):
                hs = slice(hh * HEAD_LANES, (hh + 1) * HEAD_LANES)
                q = q_ref[first:, hs]
                do16 = do_ref[first:, hs]
                in_delta = (lane >= _delta_lane(hh)) & (lane < _delta_lane(hh) + 3)
                v_minus = jnp.where(in_delta, -jnp.ones_like(v), v)
                pt = jnp.exp(lax.dot_general(k_ref[:, hs], q, NT_DIMS, preferred_element_type=F32) * SOFTMAX_SCALE)
                dvs.append(jnp.dot(pt.astype(BF16), do16, preferred_element_type=F32))
                dst = pt * lax.dot_general(v_minus, do16, NT_DIMS, preferred_element_type=F32) * SOFTMAX_SCALE
                dk_ref[:, hs] = jnp.dot(dst.astype(BF16), q, preferred_element_type=F32)
            dv_ref[...] = jnp.where(_own_lanes(dvs[0].shape, 0), dvs[0], dvs[1])

        pl.when(kj < ncb)(lambda: cols(0))
        pl.when(kj >= ncb)(lambda: cols(n_ctx))

    return pl.pallas_call(
        body,
        out_shape=[jax.ShapeDtypeStruct((T, MLA_HEADS * HEAD_LANES), F32),
                   jax.ShapeDtypeStruct((T, MLA_HEADS * V_HEAD_DIM), F32)],
        grid=(N_PAIRS, nq),
        in_specs=[pl.BlockSpec((T, 2 * HEAD_LANES), lambda h, j: (0, h)),
                  pl.BlockSpec((T, 2 * HEAD_LANES), lambda h, j: (0, h)),
                  pl.BlockSpec((tq, 2 * HEAD_LANES), lambda h, j: (j, h)),
                  pl.BlockSpec((tq, 2 * V_HEAD_DIM), lambda h, j: (j, h))],
        out_specs=[pl.BlockSpec((tq, 2 * HEAD_LANES), lambda h, j: (j, h)),
                   pl.BlockSpec((tq, 2 * V_HEAD_DIM), lambda h, j: (j, h))],
        compiler_params=_params(("parallel", "parallel"), VMEM_LIMIT), name="attn_bwd_dkv")(
            qs, dos, kb, vb)


def _split_bf16(x):
    hi = x.astype(BF16)
    return hi, (x - hi.astype(F32)).astype(BF16)


def q_heads(cq, gain, w_uq_p, tabs, name="l0_uq"):
    T, K = cq.arr.shape[0], cq.width
    N = w_uq_p.shape[1]
    tm = math.gcd(ROW_TILE, T)

    def body(a_ref, g_ref, w_ref, c_ref, sn_ref, sp_ref, o_ref, n_ref):
        qn = f_rms(a_ref[...], g_ref[0])[0].astype(BF16)
        n_ref[...] = qn
        acc = jnp.dot(qn, w_ref[...], preferred_element_type=F32)
        o_ref[...] = _rope_tiles(acc, c_ref[...], sn_ref[...], sp_ref[...], False).astype(BF16)

    tab = pl.BlockSpec((tm, HEAD_LANES), lambda i: (i, 0))
    return pl.pallas_call(
        body, out_shape=[jax.ShapeDtypeStruct((T, N), BF16), jax.ShapeDtypeStruct((T, K), BF16)], grid=(T // tm,),
        in_specs=[cq.spec(tm), pl.BlockSpec((1, 1, K), lambda i: (0, 0, 0)), pl.BlockSpec((K, N), lambda i: (0, 0)),
                  tab, tab, tab],
        out_specs=[pl.BlockSpec((tm, N), lambda i: (i, 0)), pl.BlockSpec((tm, K), lambda i: (i, 0))],
        compiler_params=_params(("parallel",), VMEM_LIMIT), name=name)(cq.arr, gain, w_uq_p, *tabs)


def kv_heads(ckv, gain, w_kn_p, w_v, kr, spread, tabs, name="l0_ukv"):
    T, K = ckv.arr.shape[0], ckv.width
    N = w_kn_p.shape[1]
    NV = w_v.shape[1]
    tm = math.gcd(ROW_TILE, T)

    def body(a_ref, g_ref, wk_ref, wv_ref, kr_ref, e_ref, c_ref, sn_ref, sp_ref, k_ref, v_ref, n_ref):
        a = f_rms(a_ref[...], g_ref[0])[0].astype(BF16)
        n_ref[...] = a
        hi, lo = _split_bf16(kr_ref[...])
        acc = (jnp.dot(a, wk_ref[...], preferred_element_type=F32)
               + jnp.dot(hi, e_ref[...], preferred_element_type=F32)
               + jnp.dot(lo, e_ref[...], preferred_element_type=F32))
        roped = _rope_tiles(acc, c_ref[...], sn_ref[...], sp_ref[...], False)
        lane = lax.broadcasted_iota(jnp.int32, roped.shape, 1) % HEAD_LANES
        k_ref[...] = jnp.where((lane >= STAT_LANE) & (lane < STAT_LANE + 3), 1.0, roped).astype(BF16)
        v_ref[...] = jnp.dot(a, wv_ref[...], preferred_element_type=F32).astype(BF16)

    tab = pl.BlockSpec((tm, HEAD_LANES), lambda i: (i, 0))
    return pl.pallas_call(
        body, out_shape=[jax.ShapeDtypeStruct((T, N), BF16), jax.ShapeDtypeStruct((T, NV), BF16),
                         jax.ShapeDtypeStruct((T, K), BF16)], grid=(T // tm,),
        in_specs=[ckv.spec(tm), pl.BlockSpec((1, 1, K), lambda i: (0, 0, 0)), pl.BlockSpec((K, N), lambda i: (0, 0)),
                  pl.BlockSpec((K, NV), lambda i: (0, 0)), kr.spec(tm),
                  pl.BlockSpec((kr.width, N), lambda i: (0, 0)), tab, tab, tab],
        out_specs=[pl.BlockSpec((tm, N), lambda i: (i, 0)), pl.BlockSpec((tm, NV), lambda i: (i, 0)),
                   pl.BlockSpec((tm, K), lambda i: (i, 0))],
        compiler_params=_params(("parallel",), VMEM_LIMIT), name=name)(
            ckv.arr, gain, w_kn_p, w_v, kr.arr, spread, *tabs)


def heads_unrope(d, tabs, spread=None, name="unrope"):
    T, N = d.shape
    tm = math.gcd(ROW_TILE, T)

    def body(*refs):
        if spread is None:
            d_ref, c_ref, sn_ref, sp_ref, o_ref = refs
        else:
            d_ref, c_ref, sn_ref, sp_ref, e_ref, o_ref, kr_ref = refs
        g = _rope_tiles(d_ref[...], c_ref[...], sn_ref[...], sp_ref[...], True)
        o_ref[...] = g.astype(BF16)
        if spread is not None:
            hi, lo = _split_bf16(g)
            kr_ref[...] = (lax.dot_general(hi, e_ref[...], NT_DIMS, preferred_element_type=F32)
                           + lax.dot_general(lo, e_ref[...], NT_DIMS, preferred_element_type=F32))

    tab = pl.BlockSpec((tm, HEAD_LANES), lambda i: (i, 0))
    row = pl.BlockSpec((tm, N), lambda i: (i, 0))
    ins, in_specs = [d, *tabs], [row, tab, tab, tab]
    out_shape, out_specs = [jax.ShapeDtypeStruct((T, N), BF16)], [row]
    if spread is not None:
        ins.append(spread)
        in_specs.append(pl.BlockSpec(spread.shape, lambda i: (0, 0)))
        out_shape.append(jax.ShapeDtypeStruct((T, spread.shape[0]), F32))
        out_specs.append(pl.BlockSpec((tm, spread.shape[0]), lambda i: (i, 0)))
    return pl.pallas_call(
        body, out_shape=out_shape, grid=(T // tm,), in_specs=in_specs, out_specs=out_specs,
        compiler_params=_params(("parallel",), VMEM_LIMIT), name=name)(*ins)


def _cmul(ar, ai, br, bi):
    return ar * br - ai * bi, ar * bi + ai * br


def s5_chain(finals, s0, a, n_steps, reverse, name):
    W = finals.shape[-1]
    first = N_SEG - 1 if reverse else 0

    def body(f_ref, s0_ref, a_ref, c_ref):
        pr, pi = jnp.ones((1, W), F32), jnp.zeros((1, W), F32)
        br, bi = a_ref[0], a_ref[1]
        n = n_steps
        while n:
            if n & 1:
                pr, pi = _cmul(pr, pi, br, bi)
            br, bi = _cmul(br, bi, br, bi)
            n >>= 1
        fr, fi = f_ref[0], f_ref[1]
        row = lax.broadcasted_iota(jnp.int32, (N_SEG, W), 0)
        s0r = jnp.broadcast_to(s0_ref[0], (N_SEG, W))
        s0i = jnp.broadcast_to(s0_ref[1], (N_SEG, W))
        cr = jnp.where(row == first, s0r, 0.0)
        ci = jnp.where(row == first, s0i, 0.0)
        shift = N_SEG - 1 if reverse else 1
        for _ in range(N_SEG - 1):
            mr, mi = _cmul(pr, pi, cr, ci)
            tr = pltpu.roll(fr + mr, shift, 0)
            ti = pltpu.roll(fi + mi, shift, 0)
            cr = jnp.where(row == first, s0r, tr)
            ci = jnp.where(row == first, s0i, ti)
        c_ref[0] = cr
        c_ref[1] = ci

    return pl.pallas_call(body, out_shape=jax.ShapeDtypeStruct((2, N_SEG, W), F32), name=name)(finals, s0, a)


def _scan_chunk(bur, bui, st_ref, a_ref, n_steps, reverse):
    for lc in range(S5_LANES // BLK_ST):
        sl = slice(lc * BLK_ST, (lc + 1) * BLK_ST)
        lr = jnp.broadcast_to(a_ref[0, :, sl], (N_SEG, BLK_ST))
        li = jnp.broadcast_to(a_ref[1, :, sl], (N_SEG, BLK_ST))

        def step(jj, carry, sl=sl, lr=lr, li=li):
            sr, si = carry
            j = (n_steps - 1 - jj) if reverse else jj
            r0 = pl.multiple_of(j * N_SEG, N_SEG)
            nr = lr * sr - li * si + bur[pl.ds(r0, N_SEG), sl]
            ni = lr * si + li * sr + bui[pl.ds(r0, N_SEG), sl]
            bur[pl.ds(r0, N_SEG), sl] = nr
            bui[pl.ds(r0, N_SEG), sl] = ni
            return nr, ni

        sr, si = lax.fori_loop(0, n_steps, step, (st_ref[0, :, sl], st_ref[1, :, sl]))
        st_ref[0, :, sl] = sr
        st_ref[1, :, sl] = si


def _project_in(x16, w_re, w_im, bur, bui, adjoint):
    for gb in range(N_BLOCKS):
        xb = x16[:, gb * BLK_CH:(gb + 1) * BLK_CH]
        sl = slice(gb * BLK_ST, (gb + 1) * BLK_ST)
        if adjoint:
            dn = (((1,), (1,)), ((), ()))
            bur[:, sl] = lax.dot_general(xb, w_re[gb], dn, preferred_element_type=F32)
            bui[:, sl] = -lax.dot_general(xb, w_im[gb], dn, preferred_element_type=F32)
        else:
            bur[:, sl] = jnp.dot(xb, w_re[gb], preferred_element_type=F32)
            bui[:, sl] = jnp.dot(xb, w_im[gb], preferred_element_type=F32)


def s5_scan(act, w_re, w_im, a, init, *, reverse, adjoint=False, c_re=None, c_im=None, add=None,
            want_ckpt=False, rows=None, name):
    act_off, N = rows if rows is not None else (0, act.shape[0])
    R = math.gcd(ROW_TILE, N, act_off)
    nch, jc = N // R, R // N_SEG
    with_out = c_re is not None

    def chunk(i):
        return (nch - 1 - i) if reverse else i

    def body(*refs):
        act_ref, wre_ref, wim_ref, a_ref, init_ref = refs[:5]
        k = 5
        if with_out:
            cre_ref, cim_ref = refs[k:k + 2]
            k += 2
        if add is not None:
            add_ref = refs[k]
            k += 1
        if with_out:
            out_ref = refs[k]
            k += 1
        if want_ckpt:
            ck_ref = refs[k]
            k += 1
        fin_ref, bur, bui = refs[k:k + 3]

        @pl.when(pl.program_id(0) == 0)
        def _():
            fin_ref[...] = init_ref[...]

        if want_ckpt:
            ck_ref[0] = fin_ref[...]
        _project_in(act_ref[...].astype(BF16), wre_ref, wim_ref, bur, bui, adjoint)
        _scan_chunk(bur, bui, fin_ref, a_ref, jc, reverse)
        if with_out:
            for gb in range(N_BLOCKS):
                sl = slice(gb * BLK_ST, (gb + 1) * BLK_ST)
                y = (jnp.dot(bur[:, sl].astype(BF16), cre_ref[gb], preferred_element_type=F32)
                     - jnp.dot(bui[:, sl].astype(BF16), cim_ref[gb], preferred_element_type=F32))
                cs = slice(gb * BLK_CH, (gb + 1) * BLK_CH)
                if add is not None:
                    y = y + add_ref[:, cs]
                out_ref[:, cs] = y

    row_spec = pl.BlockSpec((R, D_MODEL), lambda i: (chunk(i), 0))
    act_spec = pl.BlockSpec((R, D_MODEL), lambda i: (chunk(i) + act_off // R, 0))
    w_spec = pl.BlockSpec(w_re.shape, lambda i: (0, 0, 0))
    st_spec = pl.BlockSpec((2, N_SEG, S5_LANES), lambda i: (0, 0, 0))
    ins = [act, w_re, w_im, a, init]
    in_specs = [act_spec, w_spec, w_spec, pl.BlockSpec((2, 1, S5_LANES), lambda i: (0, 0, 0)), st_spec]
    if with_out:
        ins += [c_re, c_im]
        in_specs += [pl.BlockSpec(c_re.shape, lambda i: (0, 0, 0))] * 2
    if add is not None:
        ins.append(add)
        in_specs.append(row_spec)
    out_shape, out_specs = [], []
    if with_out:
        out_shape.append(jax.ShapeDtypeStruct((N, D_MODEL), F32))
        out_specs.append(row_spec)
    if want_ckpt:
        out_shape.append(jax.ShapeDtypeStruct((nch, 2, N_SEG, S5_LANES), F32))
        out_specs.append(pl.BlockSpec((1, 2, N_SEG, S5_LANES), lambda i: (chunk(i), 0, 0, 0)))
    out_shape.append(jax.ShapeDtypeStruct((2, N_SEG, S5_LANES), F32))
    out_specs.append(st_spec)
    res = pl.pallas_call(
        body, out_shape=out_shape, grid=(nch,), in_specs=in_specs, out_specs=out_specs,
        scratch_shapes=[pltpu.VMEM((R, S5_LANES), F32), pltpu.VMEM((R, S5_LANES), F32)],
        compiler_params=_params(("arbitrary",), VMEM_LIMIT), name=name)(*ins)
    res = list(res)
    out = res.pop(0) if with_out else None
    ckpt = res.pop(0) if want_ckpt else None
    return out, ckpt, res[0]


def s5_grads(dy, u, ckpt, b_re, b_im, c_re, c_im, lam, init_adj, *, reverse, add=None, u_off=0, du_rows=None,
             du_into=None, n_rows=None, name):
    N = n_rows if dy is None else dy.shape[0]
    du_total, du_first = du_rows if du_rows is not None else (N, 0)
    R = math.gcd(ROW_TILE, N, u_off, du_first)
    nch, jc = N // R, R // N_SEG
    W = S5_LANES

    def chunk(i):
        return i if reverse else (nch - 1 - i)

    def body(*refs):
        if dy is None:
            refs = (None,) + refs
        dy_ref, u_ref, ck_ref, bre_ref, bim_ref, cre_ref, cim_ref, lam_ref, init_ref = refs[:9]
        k = 9
        if add is not None:
            add_ref = refs[k]
            k += 1
        k += du_into is not None
        du_ref, dlam_ref, dbre_ref, dbim_ref, dcre_ref, dcim_ref, fin_ref = refs[k:k + 7]
        sr_buf, si_buf, er_buf, ei_buf, st_buf = refs[k + 7:k + 12]

        @pl.when(pl.program_id(0) == 0)
        def _():
            fin_ref[...] = init_ref[...]
            dlam_ref[...] = jnp.zeros_like(dlam_ref)
            dbre_ref[...] = jnp.zeros_like(dbre_ref)
            dbim_ref[...] = jnp.zeros_like(dbim_ref)
            dcre_ref[...] = jnp.zeros_like(dcre_ref)
            dcim_ref[...] = jnp.zeros_like(dcim_ref)

        u16 = u_ref[...].astype(BF16)
        st_buf[...] = ck_ref[0]
        _project_in(u16, bre_ref, bim_ref, sr_buf, si_buf, False)
        _scan_chunk(sr_buf, si_buf, st_buf, lam_ref, jc, reverse)
        if dy is None:
            er_buf[...] = jnp.zeros_like(er_buf)
            ei_buf[...] = jnp.zeros_like(ei_buf)
        else:
            dy16 = dy_ref[...].astype(BF16)
            _project_in(dy16, cre_ref, cim_ref, er_buf, ei_buf, True)
        for lc in range(W // BLK_ST):
            sl = slice(lc * BLK_ST, (lc + 1) * BLK_ST)
            lr = jnp.broadcast_to(lam_ref[0, :, sl], (N_SEG, BLK_ST))
            li = jnp.broadcast_to(lam_ref[1, :, sl], (N_SEG, BLK_ST))

            def one(r0, spr, spi, carry, sl=sl, lr=lr, li=li):
                gr, gi, ar, ai = carry
                nr = er_buf[pl.ds(r0, N_SEG), sl] + lr * gr + li * gi
                ni = ei_buf[pl.ds(r0, N_SEG), sl] + lr * gi - li * gr
                er_buf[pl.ds(r0, N_SEG), sl] = nr
                ei_buf[pl.ds(r0, N_SEG), sl] = ni
                return nr, ni, ar + spr * nr + spi * ni, ai + spr * ni - spi * nr

            def step(ff, carry, sl=sl, one=one):
                f = jc - 1 - ff
                j = (jc - 1 - f) if reverse else f
                jp = (j + 1) if reverse else (j - 1)
                r0 = pl.multiple_of(j * N_SEG, N_SEG)
                p0 = pl.multiple_of(jp * N_SEG, N_SEG)
                return one(r0, sr_buf[pl.ds(p0, N_SEG), sl], si_buf[pl.ds(p0, N_SEG), sl], carry)

            carry = (fin_ref[0, :, sl], fin_ref[1, :, sl], dlam_ref[0, :, sl], dlam_ref[1, :, sl])
            carry = lax.fori_loop(0, jc - 1, step, carry)
            r_first = (jc - 1) * N_SEG if reverse else 0
            gr, gi, ar, ai = one(r_first, ck_ref[0, 0, :, sl], ck_ref[0, 1, :, sl], carry)
            fin_ref[0, :, sl] = gr
            fin_ref[1, :, sl] = gi
            dlam_ref[0, :, sl] = ar
            dlam_ref[1, :, sl] = ai
        tn = (((0,), (0,)), ((), ()))
        nt = (((1,), (1,)), ((), ()))
        for gb in range(N_BLOCKS):
            sl = slice(gb * BLK_ST, (gb + 1) * BLK_ST)
            cs = slice(gb * BLK_CH, (gb + 1) * BLK_CH)
            gr16 = er_buf[:, sl].astype(BF16)
            gi16 = ei_buf[:, sl].astype(BF16)
            du = (lax.dot_general(gr16, bre_ref[gb], nt, preferred_element_type=F32)
                  + lax.dot_general(gi16, bim_ref[gb], nt, preferred_element_type=F32))
            if add is not None:
                du = du + add_ref[:, cs]
            du_ref[:, cs] = du
            ub = u16[:, cs]
            dbre_ref[gb] += lax.dot_general(ub, gr16, tn, preferred_element_type=F32)
            dbim_ref[gb] += lax.dot_general(ub, gi16, tn, preferred_element_type=F32)
            if dy is not None:
                dyb = dy16[:, cs]
                dcre_ref[gb] += lax.dot_general(sr_buf[:, sl].astype(BF16), dyb, tn, preferred_element_type=F32)
                dcim_ref[gb] -= lax.dot_general(si_buf[:, sl].astype(BF16), dyb, tn, preferred_element_type=F32)

    row_spec = pl.BlockSpec((R, D_MODEL), lambda i: (chunk(i), 0))
    st_spec = pl.BlockSpec((2, N_SEG, W), lambda i: (0, 0, 0))
    wb_spec = pl.BlockSpec(b_re.shape, lambda i: (0, 0, 0))
    wc_spec = pl.BlockSpec(c_re.shape, lambda i: (0, 0, 0))
    ins = [dy, u, ckpt, b_re, b_im, c_re, c_im, lam, init_adj]
    u_spec = pl.BlockSpec((R, D_MODEL), lambda i: (chunk(i) + u_off // R, 0))
    in_specs = [row_spec, u_spec, pl.BlockSpec((1, 2, N_SEG, W), lambda i: (chunk(i), 0, 0, 0)),
                wb_spec, wb_spec, wc_spec, wc_spec, pl.BlockSpec((2, 1, W), lambda i: (0, 0, 0)), st_spec]
    if dy is None:
        ins, in_specs = ins[1:], in_specs[1:]
    if add is not None:
        ins.append(add)
        in_specs.append(row_spec)
    aliases = {}
    if du_into is not None:
        aliases[len(ins)] = 0
        ins.append(du_into)
        in_specs.append(pl.BlockSpec(memory_space=pl.ANY))
    du_spec = pl.BlockSpec((R, D_MODEL), lambda i: (chunk(i) + du_first // R, 0))
    out_shape = [jax.ShapeDtypeStruct((du_total, D_MODEL), F32), jax.ShapeDtypeStruct((2, N_SEG, W), F32),
                 jax.ShapeDtypeStruct(b_re.shape, F32), jax.ShapeDtypeStruct(b_re.shape, F32),
                 jax.ShapeDtypeStruct(c_re.shape, F32), jax.ShapeDtypeStruct(c_re.shape, F32),
                 jax.ShapeDtypeStruct((2, N_SEG, W), F32)]
    out_specs = [du_spec, st_spec, wb_spec, wb_spec, wc_spec, wc_spec, st_spec]
    return pl.pallas_call(
        body, out_shape=out_shape, grid=(nch,), in_specs=in_specs, out_specs=out_specs, input_output_aliases=aliases,
        scratch_shapes=[pltpu.VMEM((R, W), F32) for _ in range(4)] + [pltpu.VMEM((2, N_SEG, W), F32)],
        compiler_params=_params(("arbitrary",), VMEM_LIMIT), name=name)(*ins)


def adamw(w, g, m, v, name="adamw", after=None):
    n, d = w.shape
    lanes = -(-d // 128) * 128
    tm = n
    while tm * lanes * 4 > (1 << 20) and tm % 16 == 0:
        tm //= 2
    c1 = 1.0 - ADAM_B1 ** ADAM_STEP
    c2 = 1.0 - ADAM_B2 ** ADAM_STEP

    def body(w_ref, g_ref, m_ref, v_ref, *rest):
        d_ref, nm_ref, nv_ref = rest[-3:]
        g_ = g_ref[...]
        m_ = ADAM_B1 * m_ref[...] + (1.0 - ADAM_B1) * g_
        v_ = ADAM_B2 * v_ref[...] + (1.0 - ADAM_B2) * (g_ * g_)
        d_ref[...] = -ADAM_LR * ((m_ / c1) / (jnp.sqrt(v_ / c2) + ADAM_EPS) + ADAM_WD * w_ref[...])
        nm_ref[...] = m_
        nv_ref[...] = v_

    spec = pl.BlockSpec((tm, d), lambda i: (i, 0))
    extra = [] if after is None else [after]
    return pl.pallas_call(
        body, out_shape=[jax.ShapeDtypeStruct((n, d), F32)] * 3, grid=(n // tm,),
        in_specs=[spec] * 4 + [pl.BlockSpec(memory_space=pl.ANY)] * len(extra), out_specs=[spec] * 3,
        compiler_params=_params(("parallel",), VMEM_LIMIT), name=name)(w, g, m, v, *extra)


def _coords():
    return lax.axis_index("x"), lax.axis_index("y"), lax.axis_index("c")


def exchange(arrays, out_shapes, remote, local, name, aliases=None):
    n_in, n_out, n_rem, n_loc = len(arrays), len(out_shapes), len(remote), len(local)

    def at(ref, idx):
        return ref if idx is None else ref.at[idx]

    def body(*refs):
        ins, outs = refs[:n_in], refs[n_in:n_in + n_out]
        send_sems, recv_sems, local_sems = refs[n_in + n_out:]
        me = _coords()
        sends, recvs = [], []
        for k, (flip, ii, src_at, oi, dst_at) in enumerate(remote):
            peer = (me[0] ^ flip[0], me[1] ^ flip[1], me[2] ^ flip[2])
            src = at(ins[ii], src_at(me, peer))
            sends.append(pltpu.make_async_remote_copy(
                src_ref=src, dst_ref=at(outs[oi], dst_at(me)), send_sem=send_sems.at[k], recv_sem=recv_sems.at[k],
                device_id=peer, device_id_type=MESH))
            recvs.append(pltpu.make_async_remote_copy(
                src_ref=src, dst_ref=at(outs[oi], dst_at(peer)), send_sem=send_sems.at[k], recv_sem=recv_sems.at[k],
                device_id=peer, device_id_type=MESH))
        locs = [pltpu.make_async_copy(at(ins[ii], src_at(me)), at(outs[oi], dst_at(me)), local_sems.at[k])
                for k, (ii, src_at, oi, dst_at) in enumerate(local)]
        for cp in locs + sends:
            cp.start()
        for cp in recvs:
            cp.wait_recv()
        for cp in sends:
            cp.wait_send()
        for cp in locs:
            cp.wait()

    hbm = pl.BlockSpec(memory_space=pl.ANY)
    return pl.pallas_call(
        body, out_shape=list(out_shapes), in_specs=[hbm] * n_in, out_specs=[hbm] * n_out,
        scratch_shapes=[pltpu.SemaphoreType.DMA((n_rem,)), pltpu.SemaphoreType.DMA((n_rem,)),
                        pltpu.SemaphoreType.DMA((max(n_loc, 1),))],
        input_output_aliases=aliases or {}, name=name)(*arrays)


ALL_FLIPS = [(dx, dy, dc) for dx in (0, 1) for dy in (0, 1) for dc in (0, 1)][1:]
CHIP_FLIPS = [(1, 0, 0), (0, 1, 0), (1, 1, 0)]
CORE_FLIP = (0, 0, 1)


def _dev_index(p):
    return 4 * p[0] + 2 * p[1] + p[2]


def _chip_index(p):
    return 2 * p[0] + p[1]


def _gather(xs, flips, index, n, name):
    arrays = [x[None] for x in xs]
    outs = [jax.ShapeDtypeStruct((n,) + x.shape, x.dtype) for x in xs]
    remote = [(f, a, lambda me, peer: (0,), a, lambda s: (index(s),)) for a in range(len(xs)) for f in flips]
    local = [(a, lambda me: (0,), a, lambda me: (index(me),)) for a in range(len(xs))]
    return exchange(arrays, outs, remote, local, name)


def allgather_devices(x, name):
    return _gather([x], ALL_FLIPS, _dev_index, N_DEV, name)[0]


def allgather_chips(xs, name):
    return _gather(xs, CHIP_FLIPS, _chip_index, N_CHIP, name)


def gather_halves(xs, name):
    n = len(xs)
    nk = n * len(CHIP_FLIPS)

    def body(*refs):
        ins, outs = refs[:n], refs[n:2 * n]
        ici_send, ici_recv, d2d_send, d2d_recv = refs[2 * n:]
        me = _coords()
        sibling = (me[0], me[1], 1 - me[2])
        first, passed, landed = [], [], []
        for a in range(n):
            half = ins[a].shape[0] // 2
            mine = ins[a].at[pl.ds(pl.multiple_of(me[2] * half, 16), half)]
            for j, flip in enumerate(CHIP_FLIPS):
                k = a * len(CHIP_FLIPS) + j
                peer = (me[0] ^ flip[0], me[1] ^ flip[1], me[2])
                first.append(pltpu.make_async_remote_copy(
                    src_ref=mine, dst_ref=outs[a].at[_chip_index(me), me[2]], send_sem=ici_send.at[k],
                    recv_sem=ici_recv.at[k], device_id=peer, device_id_type=MESH))
                arrived = outs[a].at[_chip_index(peer), me[2]]
                landed.append(pltpu.make_async_remote_copy(
                    src_ref=mine, dst_ref=arrived, send_sem=ici_send.at[k], recv_sem=ici_recv.at[k],
                    device_id=peer, device_id_type=MESH))
                passed.append(pltpu.make_async_remote_copy(
                    src_ref=arrived, dst_ref=arrived, send_sem=d2d_send.at[k], recv_sem=d2d_recv.at[k],
                    device_id=sibling, device_id_type=MESH))
        for cp in first:
            cp.start()
        for k in range(nk):
            landed[k].wait_recv()
            passed[k].start()
        for a in range(n):
            for j, flip in enumerate(CHIP_FLIPS):
                k = a * len(CHIP_FLIPS) + j
                peer_chip = _chip_index((me[0] ^ flip[0], me[1] ^ flip[1]))
                from_sibling = outs[a].at[peer_chip, 1 - me[2]]
                pltpu.make_async_remote_copy(
                    src_ref=from_sibling, dst_ref=from_sibling, send_sem=d2d_send.at[k], recv_sem=d2d_recv.at[k],
                    device_id=sibling, device_id_type=MESH).wait_recv()
        for cp in first + passed:
            cp.wait_send()

    hbm = pl.BlockSpec(memory_space=pl.ANY)
    return pl.pallas_call(
        body, out_shape=[jax.ShapeDtypeStruct((N_CHIP, 2, x.shape[0] // 2, x.shape[1]), x.dtype) for x in xs],
        in_specs=[hbm] * n, out_specs=[hbm] * n,
        scratch_shapes=[pltpu.SemaphoreType.DMA((nk,)) for _ in range(4)], name=name)(*xs)


HBM_SPEC = pl.BlockSpec(memory_space=pltpu.HBM)
SEM_SPEC = pl.BlockSpec(memory_space=pltpu.SEMAPHORE)
DATAFLOW = pltpu.SideEffectType.DATAFLOW_SIDE_EFFECTING


def _at(ref, idx):
    return ref if idx is None else ref.at[idx]


def _peer(me, flip):
    return (me[0] ^ flip[0], me[1] ^ flip[1], me[2] ^ flip[2])


def exchange_start(arrays, land_shapes, remote, name, after=None):
    n_in, n_out, nk = len(arrays), len(land_shapes), len(remote)
    after = list(after or [])
    n_after = len(after)

    def body(*refs):
        srcs, lands = refs[:n_in], refs[n_in:n_in + n_out]
        first_out = n_in + n_out + n_after
        send_sems, recv_sems, token = refs[first_out], refs[first_out + 1], refs[-1]
        me = _coords()
        for k, (flip, ii, src_at, oi, dst_at) in enumerate(remote):
            peer = _peer(me, flip)
            pltpu.make_async_remote_copy(
                src_ref=_at(srcs[ii], src_at(me, peer)), dst_ref=_at(lands[oi], dst_at(me)), send_sem=send_sems.at[k],
                recv_sem=recv_sems.at[k], device_id=peer, device_id_type=MESH).start()
        token[...] = jnp.zeros_like(token)

    lands = [lax.empty(s.shape, s.dtype) for s in land_shapes]
    bufs = list(arrays) + lands
    out = pl.pallas_call(
        body, name=name,
        out_shape=(pltpu.SemaphoreType.DMA((nk,)), pltpu.SemaphoreType.DMA((nk,)),
                   *[pltpu.HBM(b.shape, b.dtype) for b in bufs], jax.ShapeDtypeStruct((8, 128), F32)),
        in_specs=[HBM_SPEC] * len(bufs) + [pl.BlockSpec(memory_space=pl.ANY)] * n_after,
        out_specs=(SEM_SPEC, SEM_SPEC, *[HBM_SPEC] * len(bufs), pl.BlockSpec(memory_space=pltpu.VMEM)),
        input_output_aliases={a: 2 + a for a in range(len(bufs))},
        compiler_params=pltpu.CompilerParams(has_side_effects=DATAFLOW),
    )(*[pltpu.with_memory_space_constraint(b, pltpu.HBM) for b in bufs], *after)
    flight = (out[0], out[1], list(out[2:2 + n_in]), list(out[2 + n_in:2 + n_in + n_out]), remote)
    return flight, out[-1]


def exchange_wait(flight, after, name):
    send_sems, recv_sems, arrays, lands, remote = flight
    n_in, n_out = len(arrays), len(lands)
    after = list(after) if isinstance(after, (list, tuple)) else [after]

    def body(*refs):
        srcs, lnds = refs[:n_in], refs[n_in:n_in + n_out]
        s_sems, r_sems = refs[n_in + n_out], refs[n_in + n_out + 1]
        me = _coords()
        for k, (flip, ii, src_at, oi, dst_at) in enumerate(remote):
            peer = _peer(me, flip)
            copy = pltpu.make_async_remote_copy(
                src_ref=_at(srcs[ii], src_at(me, peer)), dst_ref=_at(lnds[oi], dst_at(peer)), send_sem=s_sems.at[k],
                recv_sem=r_sems.at[k], device_id=peer, device_id_type=MESH)
            copy.wait_send()
            copy.wait_recv()

    bufs = list(arrays) + list(lands)
    out = pl.pallas_call(
        body, name=name,
        out_shape=tuple(pltpu.HBM(b.shape, b.dtype) for b in bufs),
        in_specs=[HBM_SPEC] * len(bufs) + [SEM_SPEC, SEM_SPEC] + [pl.BlockSpec(memory_space=pl.ANY)] * len(after),
        out_specs=tuple([HBM_SPEC] * len(bufs)),
        input_output_aliases={a: a for a in range(len(bufs))},
        compiler_params=pltpu.CompilerParams(has_side_effects=DATAFLOW),
    )(*bufs, send_sems, recv_sems, *after)
    return list(out[:n_in]), list(out[n_in:])


def _half_tile(h, cd):
    return h if h * cd * 4 <= (1 << 20) else math.gcd(512, h)


def pair_add(g, got, core, out_dtype, name):
    _, _, h, cd = g.shape
    th = _half_tile(h, cd)

    def body(c_ref, g_ref, got_ref, o_ref):
        o_ref[0] = (g_ref[0, 0] + got_ref[0]).astype(o_ref.dtype)

    return pl.pallas_call(
        body, out_shape=jax.ShapeDtypeStruct((N_CHIP, h, cd), out_dtype),
        grid_spec=pltpu.PrefetchScalarGridSpec(
            num_scalar_prefetch=1, grid=(N_CHIP, h // th),
            in_specs=[pl.BlockSpec((1, 1, th, cd), lambda q, i, c: (q, c[0], i, 0)),
                      pl.BlockSpec((1, th, cd), lambda q, i, c: (q, i, 0))],
            out_specs=pl.BlockSpec((1, th, cd), lambda q, i, c: (q, i, 0))),
        compiler_params=_params(("parallel", "parallel"), VMEM_LIMIT), name=name)(core, g, got)


def sum_chips(parts, sums, place, name):
    _, h, cd = parts.shape
    th = _half_tile(h, cd)

    def body(pc_ref, p_ref, own_ref, o_ref):
        acc = None
        for q in range(N_CHIP):
            term = jnp.where(pc_ref[1] == q, own_ref[0], p_ref[q]).astype(F32)
            acc = term if acc is None else acc + term
        o_ref[0] = acc

    return pl.pallas_call(
        body, out_shape=jax.ShapeDtypeStruct((2, h, cd), F32),
        grid_spec=pltpu.PrefetchScalarGridSpec(
            num_scalar_prefetch=1, grid=(h // th,),
            in_specs=[pl.BlockSpec((N_CHIP, th, cd), lambda i, pc: (0, i, 0)),
                      pl.BlockSpec((1, th, cd), lambda i, pc: (pc[1], i, 0))],
            out_specs=pl.BlockSpec((1, th, cd), lambda i, pc: (pc[0], i, 0))),
        compiler_params=_params(("parallel",), VMEM_LIMIT), name=name)(place, parts, sums)


def to_segments(a, n_ctx):
    def one(p):
        n = p.shape[0]
        return p.reshape(N_SEG, n // N_SEG, -1).transpose(1, 0, 2).reshape(n, -1)
    return jnp.concatenate([one(a[:n_ctx]), one(a[n_ctx:])], axis=0) if n_ctx else one(a)


def rows_to_segments(a, n_ctx, name):
    n, d = a.shape
    tj = n_ctx // N_SEG
    per_seg = (n - n_ctx) // N_SEG // tj
    assert n_ctx % N_SEG == 0 and (n - n_ctx) % (N_SEG * tj) == 0

    def body(*refs):
        out_ref, slabs = refs[N_SEG:]
        for c in range(d // HEAD_LANES):
            cols = slice(c * HEAD_LANES, (c + 1) * HEAD_LANES)
            for seg in range(N_SEG):
                slabs[c, pl.ds(seg, tj, stride=N_SEG), :] = refs[seg][:, cols]
            out_ref[:, cols] = slabs[c]

    def seg_spec(seg):
        return pl.BlockSpec((tj, d), lambda i: (jnp.where(i == 0, seg, N_SEG + seg * per_seg + i - 1), 0))

    return pl.pallas_call(
        body, out_shape=jax.ShapeDtypeStruct((n, d), a.dtype), grid=(1 + per_seg,),
        in_specs=[seg_spec(seg) for seg in range(N_SEG)], out_specs=pl.BlockSpec((N_SEG * tj, d), lambda i: (i, 0)),
        scratch_shapes=[pltpu.VMEM((d // HEAD_LANES, N_SEG * tj, HEAD_LANES), a.dtype)],
        compiler_params=_params(("parallel",), VMEM_LIMIT), name=name)(*[a] * N_SEG)


def rope_tables(n_ctx, n_lat):
    f32 = np.float32
    rows = n_lat // GRID_W
    row = np.repeat(np.arange(rows), GRID_W).astype(f32)
    col = np.tile(np.arange(GRID_W), rows).astype(f32)
    d = QK_ROPE_DIM // 2
    inv = (f32(1.0) / np.power(f32(ROPE_THETA), np.arange(0, d, 2, dtype=f32) / f32(d))).astype(f32)
    ang = np.concatenate([row[:, None] * inv[None, :], col[:, None] * inv[None, :]], axis=1).astype(f32)
    cos = np.concatenate([np.ones((n_ctx, d), f32), np.cos(ang)], axis=0)
    sin = np.concatenate([np.zeros((n_ctx, d), f32), np.sin(ang)], axis=0)
    q = QK_ROPE_DIM // 4
    T = n_ctx + n_lat
    ones, zeros = np.ones((T, QK_NOPE_DIM), f32), np.zeros((T, QK_NOPE_DIM), f32)
    tail, z8 = np.zeros((T, HEAD_LANES - QK_DIM), f32), np.zeros((T, q), f32)
    cr, cc, sr, sc = cos[:, :q], cos[:, q:], sin[:, :q], sin[:, q:]
    cos_t = np.concatenate([ones, cr, cr, cc, cc, tail], axis=1)
    sin_next = np.concatenate([zeros, -sr, z8, -sc, z8, tail], axis=1)
    sin_prev = np.concatenate([zeros, z8, sr, z8, sc, tail], axis=1)
    return tuple(jnp.asarray(t, F32) for t in (cos_t, sin_next, sin_prev))


def pad_heads(w, used):
    k = w.shape[0]
    return jnp.pad(w.reshape(k, MLA_HEADS, used), ((0, 0), (0, 0), (0, HEAD_LANES - used))).reshape(k, -1)


def unpad_heads(w, used):
    k = w.shape[0]
    return w.reshape(k, MLA_HEADS, HEAD_LANES)[:, :, :used].reshape(k, MLA_HEADS * used)


def rotary_spread():
    lane = np.arange(MLA_HEADS * HEAD_LANES) % HEAD_LANES
    return jnp.asarray(lane[None, :] == (QK_NOPE_DIM + np.arange(QK_ROPE_DIM))[:, None], BF16)


def s5_discretise(a_re, a_im, log_step, b_re, b_im):
    dt = jnp.exp(log_step)[:, None]
    mag = jnp.exp(a_re * dt)
    lb_re = mag * jnp.cos(a_im * dt)
    lb_im = mag * jnp.sin(a_im * dt)
    den = a_re * a_re + a_im * a_im
    nr = lb_re - 1.0
    f_re = ((nr * a_re + lb_im * a_im) / den)[:, None, :]
    f_im = ((lb_im * a_re - nr * a_im) / den)[:, None, :]
    return lb_re, lb_im, f_re * b_re - f_im * b_im, f_re * b_im + f_im * b_re


def s5_block_weights(lb_re, lb_im, bb_re, bb_im, c_re, c_im):
    eye = jnp.eye(GROUPS_PER_BLOCK, dtype=F32)
    lam = jnp.stack([lb_re.reshape(1, S5_LANES), lb_im.reshape(1, S5_LANES)])

    def b_blocks(bb):
        t = bb.reshape(N_BLOCKS, GROUPS_PER_BLOCK, S5_GROUP, S5_STATE)
        return jnp.einsum("bgcp,gh->bgchp", t, eye).reshape(N_BLOCKS, BLK_CH, BLK_ST).astype(BF16)

    def c_blocks(cc):
        t = cc.reshape(N_BLOCKS, GROUPS_PER_BLOCK, S5_GROUP, S5_STATE)
        return jnp.einsum("bgcp,gh->bgphc", t, eye).reshape(N_BLOCKS, BLK_ST, BLK_CH).astype(BF16)

    return lam, b_blocks(bb_re), b_blocks(bb_im), c_blocks(c_re), c_blocks(c_im)


def b_block_diag(db):
    t = db.reshape(N_BLOCKS, GROUPS_PER_BLOCK, S5_GROUP, GROUPS_PER_BLOCK, S5_STATE)
    return jnp.einsum("bgchp,gh->bgcp", t, jnp.eye(GROUPS_PER_BLOCK, dtype=F32)).reshape(S5_GROUPS, S5_GROUP, S5_STATE)


def c_block_diag(dc):
    t = dc.reshape(N_BLOCKS, GROUPS_PER_BLOCK, S5_STATE, GROUPS_PER_BLOCK, S5_GROUP)
    return jnp.einsum("bgphc,gh->bgcp", t, jnp.eye(GROUPS_PER_BLOCK, dtype=F32)).reshape(S5_GROUPS, S5_GROUP, S5_STATE)


def conj(a):
    return jnp.stack([a[0], -a[1]])


def _narrow(shape):
    return len(shape) >= 2 and shape[-1] < min(HEAD_LANES, shape[-2])


def _stored(a):
    return jnp.swapaxes(a, -1, -2) if _narrow(a.shape) else a


def _stored_shape(shape):
    return tuple(shape[:-2]) + (shape[-1], shape[-2]) if _narrow(shape) else tuple(shape)


def _from_stored(a, shape):
    return jnp.swapaxes(a, -1, -2) if _narrow(shape) else a


PACK_TILE = 16 * 128


def pack_flat(parts, dtype, multiple=PACK_TILE):
    flat = [p.reshape(-1).astype(dtype) for p in parts]
    sizes = [f.shape[0] for f in flat]
    total = sum(sizes)
    pad = (-total) % multiple
    if pad:
        flat.append(jnp.zeros((pad,), dtype))
    offs = np.cumsum([0] + sizes)[:-1].tolist()
    return jnp.concatenate(flat).reshape(-1, 128), offs


def unpack_flat(buf, offs, shapes):
    flat = buf.reshape(-1)
    return [flat[o:o + int(np.prod(s))].reshape(s) for o, s in zip(offs, shapes)]


def s5_forward(p1, n_ctx, dirs):
    saved = []
    y = None
    ctx_rows, lat_rows = (0, n_ctx), (n_ctx, p1.shape[0] - n_ctx)
    zeros_tile = jnp.zeros((2, N_SEG, S5_LANES), F32)
    zeros_row = jnp.zeros((2, 1, S5_LANES), F32)
    for k, (lam, b_re, b_im, c_re, c_im) in enumerate(dirs):
        rev = k == 1
        last = 0 if rev else N_SEG - 1
        _, _, fin = s5_scan(p1, b_re, b_im, lam, zeros_tile, reverse=rev, rows=ctx_rows, name=f"s5_ctx_finals{k}")
        carry_c = s5_chain(fin, zeros_row, lam, n_ctx // N_SEG, rev, name=f"s5_ctx_chain{k}")
        _, ck_c, fin_c = s5_scan(p1, b_re, b_im, lam, carry_c, reverse=rev, want_ckpt=True, rows=ctx_rows,
                                 name=f"s5_ctx_scan{k}")
        s0 = fin_c[:, last:last + 1, :]
        _, _, fin = s5_scan(p1, b_re, b_im, lam, zeros_tile, reverse=rev, rows=lat_rows, name=f"s5_lat_finals{k}")
        carry_l = s5_chain(fin, s0, lam, lat_rows[1] // N_SEG, rev, name=f"s5_lat_chain{k}")
        y, ck_l, _ = s5_scan(p1, b_re, b_im, lam, carry_l, reverse=rev, c_re=c_re, c_im=c_im, add=y,
                             want_ckpt=True, rows=lat_rows, name=f"s5_lat_scan{k}")
        saved.append((ck_c, ck_l))
    return y, saved


def s5_backward(dy_l, du_extra_l, p1, n_ctx, dirs, saved):
    n_lat = p1.shape[0] - n_ctx
    zeros_tile = jnp.zeros((2, N_SEG, S5_LANES), F32)
    zeros_row = jnp.zeros((2, 1, S5_LANES), F32)
    du_l, du_c = du_extra_l, None
    grads = []
    for k, (lam, b_re, b_im, c_re, c_im) in enumerate(dirs):
        rev = k == 1
        lam_c = conj(lam)
        ck_c, ck_l = saved[k]
        first = N_SEG - 1 if rev else 0
        _, _, fin = s5_scan(dy_l, c_re, c_im, lam_c, zeros_tile, reverse=not rev, adjoint=True,
                            name=f"s5_lat_adj_finals{k}")
        carry = s5_chain(fin, zeros_row, lam_c, n_lat // N_SEG, not rev, name=f"s5_lat_adj_chain{k}")
        whole = k == len(dirs) - 1
        du_l, dlam_l, dbr_l, dbi_l, dcr_l, dci_l, fin_a = s5_grads(
            dy_l, p1, ck_l, b_re, b_im, c_re, c_im, lam, carry, reverse=rev, add=du_l, u_off=n_ctx,
            du_rows=(p1.shape[0], n_ctx) if whole else None, name=f"s5_lat_grads{k}")
        g0 = fin_a[:, first:first + 1, :]
        carry = s5_chain(zeros_tile, g0, lam_c, n_ctx // N_SEG, not rev, name=f"s5_ctx_adj_chain{k}")
        du_c, dlam_c, dbr_c, dbi_c, _, _, _ = s5_grads(
            None, p1, ck_c, b_re, b_im, c_re, c_im, lam, carry, reverse=rev, add=du_c, n_rows=n_ctx,
            du_rows=(p1.shape[0], 0) if whole else None, du_into=du_l if whole else None, name=f"s5_ctx_grads{k}")
        dlam = jnp.sum(dlam_l + dlam_c, axis=1)
        grads.append((dlam, b_block_diag(dbr_l + dbr_c), b_block_diag(dbi_l + dbi_c),
                      c_block_diag(dcr_l), c_block_diag(dci_l)))
    return du_c, grads


def local_step(x, ctx, target, mod, w, late=None, reducer=None):
    L, Lc = x.shape[0], ctx.shape[0]
    T = L + Lc
    assert L % Lc == 0 and Lc % (2 * N_SEG) == 0 and L % GRID_W == 0
    D = D_MODEL
    X0 = (ctx, x)

    def mod_of(i, j):
        return mod[i, :, j, :][:, None, :]

    def vec(v):
        return v.reshape(1, 1, -1).astype(F32)

    g0 = vec(w["norm_g"][0])
    H0, p0 = norm_proj(X0, g0, mod_of(0, 1), mod_of(0, 0), w["mla_w_in"], Lc, "l0_norm_in")
    cq = Rows(p0, Q_LORA_RANK, col_blk=D // Q_LORA_RANK)
    ckv = Rows(p0, KV_LORA_RANK, col_blk=(D + Q_LORA_RANK) // KV_LORA_RANK)
    kr = Rows(p0, HEAD_LANES, col_blk=(D + Q_LORA_RANK + KV_LORA_RANK) // HEAD_LANES)
    qng, kvng = vec(w["mla_q_norm"]), vec(w["mla_kv_norm"])
    tabs = rope_tables(Lc, L)
    spread = jnp.pad(rotary_spread(), ((0, HEAD_LANES - QK_ROPE_DIM), (0, 0)))
    if late is not None:
        w = {**w, **late["qkv"](p0)}
    w_uq_p = pad_heads(w["mla_w_uq"], QK_DIM)
    w_ukv3 = w["mla_w_ukv"].reshape(KV_LORA_RANK, MLA_HEADS, QK_NOPE_DIM + V_HEAD_DIM)
    w_kn_p = pad_heads(w_ukv3[:, :, :QK_NOPE_DIM].reshape(KV_LORA_RANK, -1), QK_NOPE_DIM)
    w_v = w_ukv3[:, :, QK_NOPE_DIM:].reshape(KV_LORA_RANK, -1)
    qb, qn = q_heads(cq, qng, w_uq_p, tabs)
    kb, vb, kvn = kv_heads(ckv, kvng, w_kn_p, w_v, kr, spread, tabs)
    o, lse, qs = attn_fwd(qb, kb, vb, Lc)
    if late is not None:
        w = {**w, **late["out"](o)}
    X1, og, out0 = mla_post_fwd(o, p0, X0, mod_of(0, 2), w["mla_w_out"], Lc)

    if late is not None:
        w = {**w, **late["l1"](X1)}
    X1p = rows_to_segments(X1, Lc, "l1_to_segments")
    tgt_p = to_segments(target, 0)
    g1 = vec(w["norm_g"][1])
    H1, p1 = norm_proj(X1p, g1, mod_of(1, 1), mod_of(1, 0), w["s5_w_in"], Lc, "l1_norm_in")
    disc_fn = lambda *a: tuple(zip(*[s5_discretise(a[0][k], a[1][k], a[2][k], a[3][k], a[4][k]) for k in range(2)]))
    disc, disc_vjp = jax.vjp(disc_fn, w["s5_a_re"], w["s5_a_im"], w["s5_log_step"], _stored(w["s5_b_re"]),
                             _stored(w["s5_b_im"]))
    dirs = [s5_block_weights(disc[0][k], disc[1][k], disc[2][k], disc[3][k], w["s5_c_re"][k], w["s5_c_im"][k])
            for k in range(2)]
    y_ssm, s5_saved = s5_forward(p1, Lc, dirs)

    row = lambda v: v.reshape(1, D).astype(F32)
    (lvec, dX2, d_yssm, d_u_act, d_z1, d_fg, d_gt1, d_bg, d_d, gw_glu, gw_out) = s5_tail(
        y_ssm, p1, X1p, tgt_p, Lc, row(w["s5_d"]), row(w["s5_b_glu"]), mod[1, 1:2, 2, :], row(w["final_g"]),
        w["s5_w_glu"], w["s5_w_out"])
    loss = jnp.sum(lvec)
    gw = {"final_g": d_fg.reshape(D), "s5_b_glu": d_bg.reshape(D), "s5_d": d_d.reshape(D),
          "s5_w_glu": gw_glu, "s5_w_out": gw_out}
    dmod = {}

    du_p, s5_g = s5_backward(d_yssm, d_u_act, p1, Lc, dirs, s5_saved)
    d_disc = tuple(tuple(s5_g[k][j - 1].reshape(disc[j][k].shape) if j >= 2 else
                         s5_g[k][0][j].reshape(disc[j][k].shape) for k in range(2)) for j in range(4))
    gw["s5_a_re"], gw["s5_a_im"], gw["s5_log_step"], d_bt_re, d_bt_im = disc_vjp(d_disc)
    gw["s5_b_re"], gw["s5_b_im"] = jnp.swapaxes(d_bt_re, -1, -2), jnp.swapaxes(d_bt_im, -1, -2)
    gw["s5_c_re"] = jnp.stack([s5_g[0][3], s5_g[1][3]])
    gw["s5_c_im"] = jnp.stack([s5_g[0][4], s5_g[1][4]])
    gw["s5_w_in"] = mm_tn(H1, du_p, name="l1_in_dw", b_more=d_z1)
    if reducer is not None:
        g1 = g1 + reducer["l1"][0]({n: gw.pop(n) for n in LAYER1_MATS})[0, 0]
    d_X1, d_g1, d_sc1, d_sh1 = norm_proj_bwd([(du_p, 0, 0), (d_z1, D, Lc)], w["s5_w_in"], X1p, g1, mod_of(1, 1),
                                             mod_of(1, 0), dX2, Lc, "l1_norm_in_bwd", dx_from_segments=True)
    d_gt1_full = jnp.concatenate([jnp.zeros((1, 1, D), F32), d_gt1[None]], axis=0)
    dmod[1] = (d_sh1, d_sc1, d_gt1_full)

    d_o, d_z0, d_gt0, gw["mla_w_out"] = mla_post_bwd(d_X1, out0, og, o, p0, mod_of(0, 2), w["mla_w_out"], Lc)
    if reducer is not None:
        started = reducer["l1"][1](d_o)[0, 0] + reducer["out"][0]({"mla_w_out": gw.pop("mla_w_out")})[0, 0]
        tabs = (tabs[0] + started,) + tabs[1:]
    d_q, dos = attn_bwd_dq(qb, kb, vb, o, d_o, lse, tabs, Lc)
    dk_p, d_v = attn_bwd_dkv(qs, kb, vb, dos, Lc)
    if reducer is not None:
        tabs = (tabs[0] + reducer["out"][1](d_q)[0, 0],) + tabs[1:]
    d_k, d_kr = heads_unrope(dk_p, tabs, spread,
                             name="l0_k_unrope")
    d_qn = mm_nt(d_q, w_uq_p, name="l0_uq_dx")
    gw["mla_w_uq"] = unpad_heads(mm_tn(qn, d_q, name="l0_uq_dw"), QK_DIM)
    d_kvn = mm_nt(d_k, w_kn_p, name="l0_ukn_dx") + mm_nt(d_v, w_v, name="l0_uv_dx")
    dw_kn = unpad_heads(mm_tn(kvn, d_k, name="l0_ukn_dw"), QK_NOPE_DIM).reshape(KV_LORA_RANK, MLA_HEADS, QK_NOPE_DIM)
    dw_v = mm_tn(kvn, d_v, name="l0_uv_dw").reshape(KV_LORA_RANK, MLA_HEADS, V_HEAD_DIM)
    gw["mla_w_ukv"] = jnp.concatenate([dw_kn, dw_v], axis=-1).reshape(KV_LORA_RANK, -1)
    d_cq, d_qng = rowwise_bwd(f_rms, [cq], [qng], [d_qn], [0], [0], T, 0, "l0_qnorm_bwd")
    d_ckv, d_kvng = rowwise_bwd(f_rms, [ckv], [kvng], [d_kvn], [0], [0], T, 0, "l0_kvnorm_bwd")
    gw["mla_q_norm"] = d_qng.reshape(-1)
    gw["mla_kv_norm"] = d_kvng.reshape(-1)
    o_cq, o_ckv = D, D + Q_LORA_RANK
    o_kr = o_ckv + KV_LORA_RANK
    d_head = jnp.concatenate([d_cq, d_ckv, d_kr], axis=1)
    gw["mla_w_in"] = jnp.concatenate([mm_tn(H0, d_head, name="l0_in_dw_head")[:, :P0_HEAD],
                                      mm_tn(H0, d_z0, name="l0_in_dw_z")], axis=1)
    dx, d_g0, d_sc0, d_sh0 = norm_proj_bwd(
        [(d_z0, 0, 0), (d_cq, o_cq, 0), (d_ckv, o_ckv, 0), (d_kr, o_kr, 0)], w["mla_w_in"], X0, g0, mod_of(0, 1),
        mod_of(0, 0), d_X1, Lc, "l0_norm_in_bwd", latent_dx_only=True)
    dmod[0] = (d_sh0, d_sc0, d_gt0)
    gw["norm_g"] = jnp.stack([d_g0.reshape(D), d_g1.reshape(D)])
    dmod_arr = jnp.stack([jnp.stack([dmod[i][j][:, 0, :] for j in range(3)], axis=1) for i in range(2)])
    ready = {**reducer["l1"][2](dx), **reducer["out"][2](dx)} if reducer is not None else {}
    return loss, dx, dmod_arr, gw, ready


SHARDED = {
    "mla_w_in": 1, "mla_w_uq": 1, "mla_w_ukv": 1, "mla_w_out": 0,
    "s5_w_in": 1, "s5_w_glu": 0, "s5_w_out": 0, "s5_d": 0, "s5_b_glu": 0,
}
SHARDED_MATS = ["mla_w_in", "mla_w_uq", "mla_w_ukv", "mla_w_out", "s5_w_in", "s5_w_glu", "s5_w_out"]
SHARDED_VECS = ["s5_d", "s5_b_glu"]
REPLICATED = ["norm_g", "mla_q_norm", "mla_kv_norm", "s5_a_re", "s5_a_im", "s5_log_step", "s5_b_re", "s5_b_im",
              "s5_c_re", "s5_c_im", "final_g"]
WEIGHT_ORDER = ["c_ctx", "ada_w", "ada_b", "norm_g", "mla_w_in", "mla_q_norm", "mla_w_uq", "mla_kv_norm", "mla_w_ukv",
                "mla_w_out", "s5_w_in", "s5_a_re", "s5_a_im", "s5_log_step", "s5_b_re", "s5_b_im", "s5_c_re", "s5_c_im",
                "s5_d", "s5_w_glu", "s5_b_glu", "s5_w_out", "final_g"]


P0_HEAD = Q_LORA_RANK + KV_LORA_RANK + QK_ROPE_DIM


P0_WIDTH = 1536


def w_in_to_kernel_order(w):
    pad = jnp.zeros((w.shape[0], P0_WIDTH - w.shape[1]), w.dtype)
    return jnp.concatenate([w[:, P0_HEAD:], w[:, :P0_HEAD], pad], axis=1)


LAYER0_MATS = ["mla_w_in", "mla_w_uq", "mla_w_ukv", "mla_w_out"]
LAYER1_MATS = ["s5_w_in", "s5_w_glu", "s5_w_out"]


def _whole_matrices(names, own_blocks, gathered):
    chip = _chip_index(_coords())
    full = {}
    for n, own, o in zip(names, own_blocks, gathered):
        slot = lax.broadcasted_iota(jnp.int32, (N_CHIP, 1, 1), 0)
        o = jnp.where(slot == chip, own[None], o.reshape((N_CHIP,) + own.shape))
        full[n] = o.reshape(-1, o.shape[-1]) if SHARDED[n] == 0 else o.transpose(1, 0, 2).reshape(o.shape[1], -1)
    return full


FIRST_MATS = ["mla_w_in"]
LATER_GROUPS = {"qkv": ["mla_w_uq", "mla_w_ukv"], "out": ["mla_w_out"], "l1": LAYER1_MATS}


def gather_weights(ws):
    mats = [ws[n].astype(BF16) for n in FIRST_MATS]
    full = _whole_matrices(FIRST_MATS, mats, gather_halves(mats, "gather_weights"))
    full["mla_w_in"] = w_in_to_kernel_order(full["mla_w_in"])
    return full


def gather_weights_behind(ws, after):
    token, finish = 0.0, {}
    for group, names in LATER_GROUPS.items():
        mats = [ws[n].astype(BF16) for n in names]
        flight, tok = exchange_start(
            mats, [jax.ShapeDtypeStruct((N_CHIP,) + m.shape, m.dtype) for m in mats],
            [(f, a, lambda me, peer: None, a, lambda s: (_chip_index(s),))
             for a in range(len(mats)) for f in CHIP_FLIPS], f"gather_{group}_start", after=after)
        after = [tok]
        token = token + tok[0, 0]

        def finish_group(after_work, group=group, names=names, flight=flight):
            own, got = exchange_wait(flight, after_work, f"gather_{group}_wait")
            return _whole_matrices(names, own, got)

        finish[group] = finish_group
    return token, finish


def _grad_slots(gw, names):
    slots = []
    for n in names:
        g = gw[n]
        if SHARDED[n] == 0:
            slots.append(g.reshape(N_CHIP, 2, g.shape[0] // (2 * N_CHIP), g.shape[1]))
        else:
            k, n4 = g.shape
            slots.append(g.reshape(k, N_CHIP, n4 // N_CHIP).transpose(1, 0, 2)
                         .reshape(N_CHIP, 2, k // 2, n4 // N_CHIP))
    return slots


def _to_sibling_half(count):
    return [(CORE_FLIP, i, lambda me, peer: (slice(None), 1 - me[2]), i, lambda s: None) for i in range(count)]


def _to_chips(count):
    return [(f, i, lambda me, peer: (_chip_index(peer),), i, lambda s: (_chip_index(s),))
            for i in range(count) for f in CHIP_FLIPS]


def _place():
    me = _coords()
    return jnp.stack([me[2], _chip_index(me)]).astype(jnp.int32)


def reduce_behind(names, tag):
    state = {}
    count = len(names)

    def begin(gw):
        slots = _grad_slots(gw, names)
        lands = [jax.ShapeDtypeStruct((N_CHIP,) + s.shape[2:], F32) for s in slots]
        state["in"], token = exchange_start(slots, lands, _to_sibling_half(count), f"grads_{tag}_swap_in_start")
        return token

    def middle(after):
        slots, got = exchange_wait(state["in"], after, f"grads_{tag}_swap_in_wait")
        place = _place()
        sums = [pair_add(s, g, place[:1], BF16, f"grads_pair_{n}") for n, s, g in zip(names, slots, got)]
        lands = [jax.ShapeDtypeStruct(s.shape, s.dtype) for s in sums]
        state["out"], token = exchange_start(sums, lands, _to_chips(count), f"grads_{tag}_scatter_start")
        return token

    def end(after):
        sums, parts = exchange_wait(state["out"], after, f"grads_{tag}_scatter_wait")
        place = _place()
        return {n: sum_chips(p, s, place, f"grads_sum_{n}") for n, p, s in zip(names, parts, sums)}

    return begin, middle, end


def reduce_gradients(gw, ready_halves, also=None):
    me = _coords()
    place = _place()
    mat_names = [n for n in SHARDED_MATS if n not in ready_halves]
    slots = dict(zip(mat_names, _grad_slots(gw, mat_names)))
    gw = {**gw, **(also or {})}
    small_names = REPLICATED + SHARDED_VECS + list(also or {})
    small, small_offs = pack_flat([_stored(gw[n]).astype(F32) for n in small_names], F32, N_CHIP * 32 * 128)
    slots["small"] = small.reshape(N_CHIP, 2, -1, 128)
    names = list(slots)
    count = len(names)
    got = exchange([slots[n] for n in names],
                   [jax.ShapeDtypeStruct((N_CHIP,) + slots[n].shape[2:], F32) for n in names],
                   _to_sibling_half(count), [], "grads_swap_in")
    sums = [pair_add(slots[n], g, place[:1], F32 if n == "small" else BF16, f"grads_pair_{n}")
            for n, g in zip(names, got)]
    parts = exchange(sums, [jax.ShapeDtypeStruct(s.shape, s.dtype) for s in sums], _to_chips(count), [],
                     "grads_scatter")
    halves = {n: sum_chips(p, s, place, f"grads_sum_{n}") for n, p, s in zip(names, parts, sums)}
    halves.update(ready_halves)
    all_names = list(halves)
    fulls = exchange(
        [halves[n] for n in all_names], [jax.ShapeDtypeStruct(halves[n].shape, F32) for n in all_names],
        [(CORE_FLIP, i, lambda me, peer: (me[2],), i, lambda s: (s[2],)) for i in range(len(all_names))], [],
        "grads_swap_out", aliases={i: i for i in range(len(all_names))})
    out = {n: f.reshape(-1, f.shape[-1]) for n, f in zip(all_names, fulls)}
    quarter = out.pop("small")
    gather, token = exchange_start(
        [quarter], [jax.ShapeDtypeStruct((N_CHIP,) + quarter.shape, F32)],
        [(f, 0, lambda me, peer: None, 0, lambda s: (_chip_index(s),)) for f in CHIP_FLIPS],
        "grads_gather_small_start")

    def finish_small(after):
        (own,), (got_small,) = exchange_wait(gather, after, "grads_gather_small_wait")
        slot = lax.broadcasted_iota(jnp.int32, (N_CHIP, 1, 1), 0)
        small_all = jnp.where(slot == _chip_index(me), own[None], got_small)
        vals = unpack_flat(small_all, small_offs, [_stored_shape(gw[n].shape) for n in small_names])
        res = {}
        for n, v in zip(small_names, vals):
            v = _from_stored(v, gw[n].shape)
            if n in SHARDED_VECS:
                size = v.shape[0] // N_CHIP
                v = lax.dynamic_slice_in_dim(v, _chip_index(me) * size, size)
            res[n] = v
        return res

    return out, finish_small, token


def kernel(x, c, ctx, c_ctx, ada_w, ada_b, norm_g, mla_w_in, mla_q_norm, mla_w_uq, mla_kv_norm, mla_w_ukv, mla_w_out, s5_w_in, s5_a_re, s5_a_im, s5_log_step, s5_b_re, s5_b_im, s5_c_re, s5_c_im, s5_d, s5_w_glu, s5_b_glu, s5_w_out, final_g, loss_target, m_c_ctx, m_ada_w, m_ada_b, m_norm_g, m_mla_w_in, m_mla_q_norm, m_mla_w_uq, m_mla_kv_norm, m_mla_w_ukv, m_mla_w_out, m_s5_w_in, m_s5_a_re, m_s5_a_im, m_s5_log_step, m_s5_b_re, m_s5_b_im, m_s5_c_re, m_s5_c_im, m_s5_d, m_s5_w_glu, m_s5_b_glu, m_s5_w_out, m_final_g, v_c_ctx, v_ada_w, v_ada_b, v_norm_g, v_mla_w_in, v_mla_q_norm, v_mla_w_uq, v_mla_kv_norm, v_mla_w_ukv, v_mla_w_out, v_s5_w_in, v_s5_a_re, v_s5_a_im, v_s5_log_step, v_s5_b_re, v_s5_b_im, v_s5_c_re, v_s5_c_im, v_s5_d, v_s5_w_glu, v_s5_b_glu, v_s5_w_out, v_final_g):
    args = dict(locals())
    weights = {n: args[n] for n in WEIGHT_ORDER}
    D = D_MODEL
    xi, yi, ci = _coords()
    chip = 2 * xi + yi
    me = 4 * xi + 2 * yi + ci
    n_col = ada_w.shape[2]

    c_all = allgather_devices(jnp.pad(c, ((0, 7), (0, 0))), "gather_c")[:, 0, :]
    cond = jnp.concatenate([c_all, jnp.broadcast_to(c_ctx[None], (8, D))], axis=0)
    (s_cond,) = rowwise_fwd(lambda v: (_silu(v),), [cond], [], [D], [F32], 16, 0, "cond_silu")
    ada_rows = ada_w.reshape(2 * D, n_col)
    mod_cols = jnp.stack([mm_nn(s_cond, ada_rows, name=f"mod_proj{i}", b_blk=i) for i in range(2)])
    vec_tiles = [jnp.pad(weights[n][0].reshape(-1, 128), ((0, 6), (0, 0))) for n in SHARDED_VECS]
    mod_all, *vec_all = allgather_chips([mod_cols] + vec_tiles, "gather_mod")
    mod_all = mod_all.transpose(1, 2, 0, 3).reshape(2, 16, 3 * D) + ada_b[:, None, :]
    mine = lax.broadcasted_iota(jnp.int32, (1, 16, 1), 1) == me
    mod_l = jnp.sum(jnp.where(mine, mod_all, 0.0), axis=1)
    mod_c = mod_all[:, 8, :]
    mod = jnp.stack([mod_c.reshape(2, 3, D), mod_l.reshape(2, 3, D)], axis=1)

    w = gather_weights({n: weights[n][0] for n in FIRST_MATS})
    token, late = gather_weights_behind({n: weights[n][0] for n in SHARDED_MATS}, [w["mla_w_in"], mod])
    for n, v in zip(SHARDED_VECS, vec_all):
        w[n] = v[:, :2, :].reshape(-1)
    for n in ["norm_g", "final_g"]:
        w[n] = weights[n]
    for n in ["mla_q_norm", "mla_kv_norm", "s5_a_re", "s5_a_im", "s5_log_step", "s5_b_re", "s5_b_im",
              "s5_c_re", "s5_c_im"]:
        w[n] = weights[n][0]

    reducer = {"l1": reduce_behind(LAYER1_MATS, "l1"), "out": reduce_behind(["mla_w_out"], "out")}
    loss_me, dx, dmod, gw, ready = local_step(x[0], ctx[0], loss_target[0], mod + token, w, late,
                                              reducer)

    dmod_rows, loss_all = _gather([dmod.reshape(2, 2, 3 * D), jnp.broadcast_to(loss_me, (8, 128))],
                                  ALL_FLIPS, _dev_index, N_DEV, "gather_dmod")
    loss = functools.reduce(lambda s, d: s + loss_all[d, 0, 0], range(1, N_DEV), loss_all[0, 0, 0])
    dm = jnp.concatenate([dmod_rows[:, :, 1, :], dmod_rows[:, :, 0, :]], axis=0).transpose(1, 0, 2)
    g_ada_b = jnp.sum(dm, axis=1)
    dm_cols = lax.dynamic_slice_in_dim(dm, chip * n_col, n_col, axis=2)
    g_ada_w = jnp.stack([mm_tn(s_cond, dm_cols[i], name=f"mod_proj_dw{i}") for i in range(2)])
    dmc = jnp.sum(dm_cols[:, 8:, :], axis=1)
    dmc8 = jnp.broadcast_to(dmc[:, None, :], (2, 8, n_col))
    g_sc = (mm_nt(dmc8[0], ada_rows, name="mod_proj_dx0", b_rows=D, b_blk=0)[0]
            + mm_nt(dmc8[1], ada_rows, name="mod_proj_dx1", b_rows=D, b_blk=1)[0])
    g_silu_part = jnp.where(ci == 0, g_sc, 0.0)

    grads = {"ada_w": g_ada_w, "ada_b": g_ada_b}
    deltas, new_m, new_v = {}, {}, {}
    small = [n for n in WEIGHT_ORDER if weights[n].size < 50000]

    def update(n, after=None):
        shp = weights[n].shape
        rows = lambda a: _stored(a.reshape(shp)).reshape(-1, _stored_shape(shp)[-1])
        back = lambda a: _from_stored(a.reshape(_stored_shape(shp)), shp)
        d_, m_, v_ = adamw(rows(weights[n]), rows(grads[n]), rows(args["m_" + n]), rows(args["v_" + n]),
                           name=f"adamw_{n}", after=after)
        deltas[n], new_m[n], new_v[n] = back(d_), back(m_), back(v_)

    red, finish_small, small_started = reduce_gradients(gw, ready, {"silu_c_ctx": g_silu_part})
    update("ada_w", small_started)
    for n in SHARDED_MATS:
        grads[n] = red[n].reshape(weights[n].shape)
        update(n, small_started)
    red_small = finish_small([deltas[n] for n in ["ada_w"] + SHARDED_MATS])
    for n in REPLICATED + SHARDED_VECS:
        grads[n] = red_small[n].reshape(weights[n].shape)
    (g_c_ctx,) = rowwise_bwd(lambda v: (_silu(v),), [jnp.broadcast_to(c_ctx[None], (8, D))], [],
                             [jnp.broadcast_to(red_small["silu_c_ctx"][None], (8, D))], [0], [], 8, 0, "cond_silu_bwd")
    grads["c_ctx"] = g_c_ctx[0]
    for n in WEIGHT_ORDER:
        if n not in small and n not in deltas:
            update(n)
    packs = []
    offs = None
    for src in (weights, grads, {n: args["m_" + n] for n in small}, {n: args["v_" + n] for n in small}):
        buf, offs = pack_flat([src[n] for n in small], F32)
        packs.append(buf)
    outs = adamw(*packs, name="adamw_small")
    for res, dst in zip(outs, (deltas, new_m, new_v)):
        for n, val in zip(small, unpack_flat(res, offs, [weights[n].shape for n in small])):
            dst[n] = val

    return (loss, dx[None], *[grads[n] for n in WEIGHT_ORDER], *[deltas[n] for n in WEIGHT_ORDER],
            *[new_m[n] for n in WEIGHT_ORDER], *[new_v[n] for n in WEIGHT_ORDER])
```

```python
import functools
import math

import jax
import jax.numpy as jnp
import numpy as np
from jax import lax
from jax.experimental import pallas as pl
from jax.experimental.pallas import tpu as pltpu

F32 = jnp.float32
BF16 = jnp.bfloat16

D_MODEL = 1024
GRID_W = 64
EPS = 1e-6
MLA_HEADS = 16
QK_NOPE_DIM = 64
QK_ROPE_DIM = 32
V_HEAD_DIM = 64
Q_LORA_RANK = 256
KV_LORA_RANK = 128
QK_DIM = QK_NOPE_DIM + QK_ROPE_DIM
SOFTMAX_SCALE = QK_DIM ** -0.5
ROPE_THETA = 10000.0
S5_GROUP = 16
S5_GROUPS = D_MODEL // S5_GROUP
S5_STATE = 64
S5_LANES = S5_GROUPS * S5_STATE
N_SEG = 8
GROUPS_PER_BLOCK = 8
N_BLOCKS = S5_GROUPS // GROUPS_PER_BLOCK
BLK_CH = GROUPS_PER_BLOCK * S5_GROUP
BLK_ST = GROUPS_PER_BLOCK * S5_STATE

ADAM_LR = 0.001
ADAM_B1 = 0.9
ADAM_B2 = 0.999
ADAM_EPS = 1e-08
ADAM_WD = 0.01
ADAM_STEP = 10

N_DEV = 8
N_CHIP = 4
MESH = pl.DeviceIdType.MESH
VMEM_LIMIT = 52 * 1024 * 1024
ROW_TILE = 256


def _params(sem=None, vmem=None):
    return pltpu.CompilerParams(dimension_semantics=sem, vmem_limit_bytes=vmem)


def mm_nn(a, b, out_dtype=F32, name="mm_nn", b_blk=0):
    M, K = a.shape
    N = b.shape[1]
    tm = math.gcd(ROW_TILE, M)

    def body(a_ref, b_ref, o_ref):
        o_ref[...] = jnp.dot(a_ref[...].astype(BF16), b_ref[...].astype(BF16),
                             preferred_element_type=F32).astype(o_ref.dtype)

    return pl.pallas_call(
        body, out_shape=jax.ShapeDtypeStruct((M, N), out_dtype), grid=(M // tm,),
        in_specs=[pl.BlockSpec((tm, K), lambda i: (i, 0)), pl.BlockSpec((K, N), lambda i: (b_blk, 0))],
        out_specs=pl.BlockSpec((tm, N), lambda i: (i, 0)),
        compiler_params=_params(("parallel",), VMEM_LIMIT), name=name)(a, b)


def mm_nt(a, b, out_dtype=F32, name="mm_nt", b_rows=None, b_blk=0):
    M, N = a.shape
    K = b.shape[0] if b_rows is None else b_rows
    tm = math.gcd(ROW_TILE, M)

    def body(a_ref, b_ref, o_ref):
        o_ref[...] = lax.dot_general(a_ref[...].astype(BF16), b_ref[...].astype(BF16),
                                     (((1,), (1,)), ((), ())),
                                     preferred_element_type=F32).astype(o_ref.dtype)

    return pl.pallas_call(
        body, out_shape=jax.ShapeDtypeStruct((M, K), out_dtype), grid=(M // tm,),
        in_specs=[pl.BlockSpec((tm, N), lambda i: (i, 0)), pl.BlockSpec((K, N), lambda i: (b_blk, 0))],
        out_specs=pl.BlockSpec((tm, K), lambda i: (i, 0)),
        compiler_params=_params(("parallel",), VMEM_LIMIT), name=name)(a, b)


def mm_tn(a, b, name="mm_tn", b_more=None):
    M, N = b.shape
    K = a.shape[1]
    tn = math.gcd(512, N) if N % 128 == 0 and N > 512 else N
    bs = [b] if b_more is None else [b, b_more]
    assert all(x.shape[1] == N for x in bs)
    nb = N // tn

    def body(a_ref, *refs):
        o_ref = refs[-1]
        for k, b_ref in enumerate(refs[:-1]):
            def product(b_ref=b_ref):
                o_ref[...] = lax.dot_general(a_ref[a.shape[0] - b_ref.shape[0]:, :].astype(BF16),
                                             b_ref[...].astype(BF16), (((0,), (0,)), ((), ())),
                                             preferred_element_type=F32)
            if len(bs) == 1:
                product()
            else:
                pl.when(pl.program_id(0) // nb == k)(product)

    def b_spec(k, rows):
        return pl.BlockSpec((rows, tn), lambda j: (0, jnp.clip(j - k * nb, 0, nb - 1)))

    return pl.pallas_call(
        body, out_shape=jax.ShapeDtypeStruct((K, N * len(bs)), F32), grid=(nb * len(bs),),
        in_specs=[pl.BlockSpec(a.shape, lambda j: (0, 0))] + [b_spec(k, x.shape[0]) for k, x in enumerate(bs)],
        out_specs=pl.BlockSpec((K, tn), lambda j: (0, j)),
        compiler_params=_params(("parallel",), VMEM_LIMIT), name=name)(a, *bs)


class Rows:
    def __init__(self, arr, width=None, row_off=0, col_blk=0):
        self.arr = arr
        self.width = arr.shape[1] if width is None else width
        self.row_off = row_off
        self.col_blk = col_blk

    def spec(self, tm):
        ro, cb = self.row_off // tm, self.col_blk
        return pl.BlockSpec((tm, self.width), lambda i: (i + ro, cb))


def _as_rows(x):
    return x if isinstance(x, Rows) else Rows(x)


def _row_tile(n_rows, n_ctx_rows, rows):
    tm = math.gcd(ROW_TILE, n_rows, n_ctx_rows)
    for r in rows:
        tm = math.gcd(tm, r.row_off)
    return tm


def _bc_spec(arr, n_ctx_blocks):
    g, _, d = arr.shape
    if g == 1:
        return pl.BlockSpec((1, 1, d), lambda i: (0, 0, 0))
    return pl.BlockSpec((1, 1, d), lambda i: ((i >= n_ctx_blocks).astype(jnp.int32), 0, 0))


def rowwise_fwd(fn, rows, bcs, out_dims, out_dtypes, n_rows, n_ctx_rows, name):
    rows = [_as_rows(r) for r in rows]
    tm = _row_tile(n_rows, n_ctx_rows, rows)
    ncb = n_ctx_rows // tm
    nr, nb = len(rows), len(bcs)

    def body(*refs):
        vals = [r[...].astype(F32) for r in refs[:nr]] + [b[0].astype(F32) for b in refs[nr:nr + nb]]
        outs = fn(*vals)
        for o_ref, v in zip(refs[nr + nb:], outs):
            o_ref[...] = v.astype(o_ref.dtype)

    outs = pl.pallas_call(
        body,
        out_shape=[jax.ShapeDtypeStruct((n_rows, d), dt) for d, dt in zip(out_dims, out_dtypes)],
        grid=(n_rows // tm,),
        in_specs=[r.spec(tm) for r in rows] + [_bc_spec(b, ncb) for b in bcs],
        out_specs=[pl.BlockSpec((tm, d), lambda i: (i, 0)) for d in out_dims],
        compiler_params=_params(("parallel",), VMEM_LIMIT), name=name)(*[r.arr for r in rows], *bcs)
    return outs


def rowwise_bwd(fn, rows, bcs, cts, diff_rows, diff_bcs, n_rows, n_ctx_rows, name, ct_extra=None, lat_add=None):
    rows = [_as_rows(r) for r in rows]
    cts = [_as_rows(c) for c in cts]
    extra = [_as_rows(ct_extra)] if ct_extra is not None else []
    tm = _row_tile(n_rows, n_ctx_rows, rows + cts + extra)
    ncb = n_ctx_rows // tm
    nr, nb, nc = len(rows), len(bcs), len(cts)
    ndr, ndb = len(diff_rows), len(diff_bcs)
    n_in = nr + nb + nc + len(extra) + (lat_add is not None)

    def body(*refs):
        i = pl.program_id(0)
        rvals = [r[...].astype(F32) for r in refs[:nr]]
        bvals = [b[0].astype(F32) for b in refs[nr:nr + nb]]
        cvals = [c[...].astype(F32) for c in refs[nr + nb:nr + nb + nc]]
        if extra:
            cvals[0] = cvals[0] + refs[nr + nb + nc][...].astype(F32)
        outs = refs[n_in:]

        def f(*d):
            rv, bv = list(rvals), list(bvals)
            for k, idx in enumerate(diff_rows):
                rv[idx] = d[k]
            for k, idx in enumerate(diff_bcs):
                bv[idx] = d[ndr + k]
            return tuple(fn(*rv, *bv))

        primals = [rvals[k] for k in diff_rows] + [bvals[k] for k in diff_bcs]
        _, vjp = jax.vjp(f, *primals)
        grads = list(vjp(tuple(cvals)))
        if lat_add is not None:
            add = refs[n_in - 1][...]
            grads[0] = grads[0] + (add if lat_add.shape[0] == n_rows else jnp.where(i >= ncb, add, 0.0))
        for k in range(ndr):
            outs[k][...] = grads[k].astype(outs[k].dtype)
        for k, idx in enumerate(diff_bcs):
            o_ref = outs[ndr + k]
            first = (i == 0)
            if bcs[idx].shape[0] == 2:
                first = first | (i == ncb)

            @pl.when(first)
            def _(o_ref=o_ref):
                o_ref[...] = jnp.zeros_like(o_ref)

            o_ref[0] += grads[ndr + k]

    out_shape = [jax.ShapeDtypeStruct((n_rows, rows[k].width), F32) for k in diff_rows]
    out_shape += [jax.ShapeDtypeStruct(bcs[k].shape, F32) for k in diff_bcs]
    out_specs = [pl.BlockSpec((tm, rows[k].width), lambda i: (i, 0)) for k in diff_rows]
    out_specs += [_bc_spec(bcs[k], ncb) for k in diff_bcs]
    ins = [r.arr for r in rows] + list(bcs) + [c.arr for c in cts + extra]
    in_specs = [r.spec(tm) for r in rows] + [_bc_spec(b, ncb) for b in bcs] + [c.spec(tm) for c in cts + extra]
    if lat_add is not None:
        ins.append(lat_add)
        skip = ncb if lat_add.shape[0] != n_rows else 0
        in_specs.append(pl.BlockSpec((tm, lat_add.shape[1]), lambda i: (jnp.maximum(i - skip, 0), 0)))
    outs = pl.pallas_call(
        body, out_shape=out_shape, grid=(n_rows // tm,), in_specs=in_specs, out_specs=out_specs,
        compiler_params=_params(("arbitrary",), VMEM_LIMIT), name=name)(*ins)
    return outs


def _rms(x):
    return x * lax.rsqrt(jnp.mean(x * x, axis=-1, keepdims=True) + EPS)


def _sigmoid(x):
    return 0.5 * (jnp.tanh(0.5 * x) + 1.0)


def _silu(x):
    return x * _sigmoid(x)


def _gelu_tanh(x):
    return 0.5 * x * (1.0 + jnp.tanh(math.sqrt(2.0 / math.pi) * (x + 0.044715 * (x * x * x))))


def f_norm_mod(x, g, sc, sh):
    return ((_rms(x) * g) * (1.0 + sc) + sh,)


def f_rms(x, g):
    return (_rms(x) * g,)


def f_gate(o, z):
    return (o * _silu(z),)


def f_s5_act(y, u, d):
    return (_gelu_tanh(y + d * u),)


def f_s5_glu(ya, gl, z, b):
    return (ya * _sigmoid(gl + b) * _silu(z),)


def _as_parts(x, n_ctx):
    xs = x if isinstance(x, tuple) else (x,)
    assert len(xs) == 1 or xs[0].shape[0] == n_ctx
    return xs, sum(p.shape[0] for p in xs)


def _parts_specs(xs, tm, ncb):
    d = xs[0].shape[1]
    if len(xs) == 1:
        return [pl.BlockSpec((tm, d), lambda i: (i, 0))]
    return [pl.BlockSpec((tm, d), lambda i: (jnp.minimum(i, ncb - 1), 0)),
            pl.BlockSpec((tm, d), lambda i: (jnp.maximum(i - ncb, 0), 0))]


def _parts_tile(x_refs, ncb):
    if len(x_refs) == 1:
        return x_refs[0][...]
    return jnp.where(pl.program_id(0) < ncb, x_refs[0][...], x_refs[1][...])


def norm_proj(x, g, sc, sh, w, n_ctx, name):
    xs, n = _as_parts(x, n_ctx)
    d = xs[0].shape[1]
    nw = w.shape[1]
    tm = math.gcd(ROW_TILE, n, n_ctx)
    ncb = n_ctx // tm
    nx = len(xs)

    def body(*refs):
        g_ref, sc_ref, sh_ref, w_ref, h_ref, p_ref = refs[nx:]
        h = f_norm_mod(_parts_tile(refs[:nx], ncb), g_ref[0], sc_ref[0], sh_ref[0])[0].astype(BF16)
        h_ref[...] = h
        p_ref[...] = jnp.dot(h, w_ref[...], preferred_element_type=F32)

    row = pl.BlockSpec((tm, d), lambda i: (i, 0))
    return pl.pallas_call(
        body, out_shape=[jax.ShapeDtypeStruct((n, d), BF16), jax.ShapeDtypeStruct((n, nw), F32)], grid=(n // tm,),
        in_specs=_parts_specs(xs, tm, ncb) + [_bc_spec(g, ncb), _bc_spec(sc, ncb), _bc_spec(sh, ncb),
                                              pl.BlockSpec(w.shape, lambda i: (0, 0))],
        out_specs=[row, pl.BlockSpec((tm, nw), lambda i: (i, 0))],
        compiler_params=_params(("parallel",), VMEM_LIMIT), name=name)(*xs, g, sc, sh, w)


def norm_proj_bwd(terms, w, x, g, sc, sh, add, n_ctx, name, latent_dx_only=False, dx_from_segments=False):
    xs, n = _as_parts(x, n_ctx)
    adds = add if isinstance(add, tuple) else (add,)
    d = xs[0].shape[1]
    tm = math.gcd(ROW_TILE, n, n_ctx, *[t[2] for t in terms])
    ncb = n_ctx // tm
    nt, nx, na = len(terms), len(xs), len(adds)
    add_skip = ncb if na == 1 and add.shape[0] != n else 0
    n_dx = 2 if dx_from_segments else 1
    tj = tm // N_SEG
    assert not dx_from_segments or (ncb == 1 and not latent_dx_only)

    def body(*refs):
        i = pl.program_id(0)
        a_refs = refs[:nt]
        w_ref, x_refs = refs[nt], refs[nt + 1:nt + 1 + nx]
        g_ref, sc_ref, sh_ref = refs[nt + 1 + nx:nt + 4 + nx]
        add_refs = refs[nt + 4 + nx:nt + 4 + nx + na]
        outs = refs[nt + 4 + nx + na:]
        dx_refs, (dg_ref, dsc_ref, dsh_ref) = outs[:n_dx], outs[n_dx:n_dx + 3]
        d_h = None
        for a_ref, (a, off, first) in zip(a_refs, terms):
            part = lax.dot_general(a_ref[...].astype(BF16), w_ref[:, off:off + a.shape[1]], NT_DIMS,
                                   preferred_element_type=F32)
            if first:
                part = jnp.where(i >= first // tm, part, 0.0)
            d_h = part if d_h is None else d_h + part
        _, vjp = jax.vjp(lambda x_, g_, sc_, sh_: f_norm_mod(x_, g_, sc_, sh_), _parts_tile(x_refs, ncb), g_ref[0],
                         sc_ref[0], sh_ref[0])
        d_x, d_g, d_sc, d_sh = vjp((d_h,))
        extra = _parts_tile(add_refs, ncb)
        d_x = d_x + (extra if add_skip == 0 else jnp.where(i >= ncb, extra, 0.0))
        if dx_from_segments:
            slabs = outs[n_dx + 3]
            for c in range(d // HEAD_LANES):
                slabs[c] = d_x[:, c * HEAD_LANES:(c + 1) * HEAD_LANES]
            for k, here in enumerate([i < ncb, i >= ncb]):
                @pl.when(here)
                def _(k=k):
                    for c in range(d // HEAD_LANES):
                        for seg in range(N_SEG):
                            dx_refs[k][seg, :, c * HEAD_LANES:(c + 1) * HEAD_LANES] = (
                                slabs[c, pl.ds(seg, tj, stride=N_SEG), :])
        else:
            dx_refs[0][...] = d_x

        @pl.when(i == 0)
        def _():
            dg_ref[...] = jnp.zeros_like(dg_ref)

        @pl.when((i == 0) | (i == ncb))
        def _():
            dsc_ref[...] = jnp.zeros_like(dsc_ref)
            dsh_ref[...] = jnp.zeros_like(dsh_ref)

        dg_ref[0] += d_g
        dsc_ref[0] += d_sc
        dsh_ref[0] += d_sh

    def a_spec(a, first):
        skip = first // tm
        return pl.BlockSpec((tm, a.shape[1]), lambda i: (jnp.maximum(i - skip, 0), 0))

    dx_skip = ncb if latent_dx_only else 0
    if dx_from_segments:
        dx_shapes = [jax.ShapeDtypeStruct((N_SEG, n_ctx // N_SEG, d), F32),
                     jax.ShapeDtypeStruct((N_SEG, (n - n_ctx) // N_SEG, d), F32)]
        dx_specs = [pl.BlockSpec((N_SEG, tj, d), lambda i: (0, 0, 0)),
                    pl.BlockSpec((N_SEG, tj, d), lambda i: (0, jnp.maximum(i - ncb, 0), 0))]
    else:
        dx_shapes = [jax.ShapeDtypeStruct((n - dx_skip * tm, d), F32)]
        dx_specs = [pl.BlockSpec((tm, d), lambda i: (jnp.maximum(i - dx_skip, 0), 0))]
    add_specs = (_parts_specs(adds, tm, ncb) if na == 2 else
                 [pl.BlockSpec((tm, d), lambda i: (jnp.maximum(i - add_skip, 0), 0))])
    res = pl.pallas_call(
        body,
        out_shape=dx_shapes + [jax.ShapeDtypeStruct(g.shape, F32), jax.ShapeDtypeStruct(sc.shape, F32),
                               jax.ShapeDtypeStruct(sh.shape, F32)],
        grid=(n // tm,),
        in_specs=[a_spec(a, first) for a, _, first in terms]
        + [pl.BlockSpec(w.shape, lambda i: (0, 0))] + _parts_specs(xs, tm, ncb)
        + [_bc_spec(g, ncb), _bc_spec(sc, ncb), _bc_spec(sh, ncb)] + add_specs,
        out_specs=dx_specs + [_bc_spec(g, ncb), _bc_spec(sc, ncb), _bc_spec(sh, ncb)],
        scratch_shapes=[pltpu.VMEM((d // HEAD_LANES, tm, HEAD_LANES), F32)] if dx_from_segments else [],
        compiler_params=_params(("arbitrary",), VMEM_LIMIT), name=name)(
            *[t[0] for t in terms], w, *xs, g, sc, sh, *adds)
    if dx_from_segments:
        return ((res[0].reshape(n_ctx, d), res[1].reshape(n - n_ctx, d)), *res[2:])
    return res


def mla_post_fwd(o, p0, x0, gate, w_out, n_ctx, name="l0_post"):
    n, d = o.shape
    xs, _ = _as_parts(x0, n_ctx)
    tm = math.gcd(ROW_TILE, n, n_ctx)
    ncb = n_ctx // tm
    nx = len(xs)

    def body(o_ref, z_ref, *refs):
        gt_ref, w_ref, x1_ref, og_ref, out_ref = refs[nx:]
        og = f_gate(o_ref[...], z_ref[...])[0].astype(BF16)
        out = jnp.dot(og, w_ref[...], preferred_element_type=F32)
        og_ref[...] = og
        out_ref[...] = out
        x1_ref[...] = _parts_tile(refs[:nx], ncb) + gt_ref[0] * out

    row = pl.BlockSpec((tm, d), lambda i: (i, 0))
    return pl.pallas_call(
        body, out_shape=[jax.ShapeDtypeStruct((n, d), F32), jax.ShapeDtypeStruct((n, d), BF16),
                         jax.ShapeDtypeStruct((n, d), F32)],
        grid=(n // tm,),
        in_specs=[row, row] + _parts_specs(xs, tm, ncb) + [_bc_spec(gate, ncb), pl.BlockSpec((d, d), lambda i: (0, 0))],
        out_specs=[row, row, row],
        compiler_params=_params(("parallel",), VMEM_LIMIT), name=name)(o, p0, *xs, gate, w_out)


def mla_post_bwd(dx1, out, og, o, p0, gate, w_out, n_ctx, name="l0_post_bwd"):
    n, d = o.shape
    dxs, _ = _as_parts(dx1, n_ctx)
    tm = math.gcd(ROW_TILE, n, n_ctx)
    ncb = n_ctx // tm
    nx = len(dxs)

    def body(*refs):
        out_ref, og_ref, o_ref, z_ref, gt_ref, w_ref, do_ref, dz_ref, dgt_ref, dw_ref = refs[nx:]
        i = pl.program_id(0)

        @pl.when(i == 0)
        def _():
            dw_ref[...] = jnp.zeros_like(dw_ref)

        @pl.when((i == 0) | (i == ncb))
        def _():
            dgt_ref[...] = jnp.zeros_like(dgt_ref)

        dx = _parts_tile(refs[:nx], ncb)
        dgt_ref[0] += jnp.sum(dx * out_ref[...], axis=0, keepdims=True)
        d_out16 = (gt_ref[0] * dx).astype(BF16)
        dw_ref[...] += lax.dot_general(og_ref[...], d_out16, (((0,), (0,)), ((), ())), preferred_element_type=F32)
        d_og = lax.dot_general(d_out16, w_ref[...], NT_DIMS, preferred_element_type=F32)
        _, gate_vjp = jax.vjp(lambda o_, z_: f_gate(o_, z_), o_ref[...], z_ref[...])
        d_o, d_z = gate_vjp((d_og,))
        do_ref[...] = d_o
        dz_ref[...] = d_z

    row = pl.BlockSpec((tm, d), lambda i: (i, 0))
    mat = pl.BlockSpec((d, d), lambda i: (0, 0))
    return pl.pallas_call(
        body, out_shape=[jax.ShapeDtypeStruct((n, d), F32), jax.ShapeDtypeStruct((n, d), F32),
                         jax.ShapeDtypeStruct(gate.shape, F32), jax.ShapeDtypeStruct((d, d), F32)],
        grid=(n // tm,),
        in_specs=_parts_specs(dxs, tm, ncb) + [row, row, row, row, _bc_spec(gate, ncb), mat],
        out_specs=[row, row, _bc_spec(gate, ncb), mat],
        compiler_params=_params(("arbitrary",), VMEM_LIMIT), name=name)(*dxs, out, og, o, p0, gate, w_out)


def s5_tail(y_ssm, p1, x1p, target, n_ctx, d_vec, b_glu, gate, final_g, w_glu, w_out, name="l1_tail"):
    n, d = y_ssm.shape
    tm = math.gcd(ROW_TILE, n, n_ctx)
    off = n_ctx // tm
    tn_dims = (((0,), (0,)), ((), ()))

    def row_loss(x, g, t):
        e = _rms(x) * g - t
        return 0.5 * (e * e) * (1.0 / d)

    def body(y_ref, u_ref, z_ref, x1_ref, t_ref, d_ref, b_ref, gt_ref, fg_ref, wg_ref, wo_ref,
             l_ref, dx_ref, dy_ref, du_ref, dz_ref, dfg_ref, dgt_ref, db_ref, dd_ref, dwg_ref, dwo_ref):
        @pl.when(pl.program_id(0) == 0)
        def _():
            for r in (l_ref, dfg_ref, dgt_ref, db_ref, dd_ref, dwg_ref, dwo_ref):
                r[...] = jnp.zeros_like(r)

        u, z, tgt, gt = u_ref[...], z_ref[...], t_ref[...], gt_ref[...]
        (ya,), act_vjp = jax.vjp(lambda y_, u_, d_: f_s5_act(y_, u_, d_), y_ref[...], u, d_ref[...])
        ya16 = ya.astype(BF16)
        gl = jnp.dot(ya16, wg_ref[...], preferred_element_type=F32)
        (y3,), glu_vjp = jax.vjp(lambda a_, g_, z_, b_: f_s5_glu(a_, g_, z_, b_), ya, gl, z, b_ref[...])
        y3_16 = y3.astype(BF16)
        out1 = jnp.dot(y3_16, wo_ref[...], preferred_element_type=F32)
        lterm, loss_vjp = jax.vjp(lambda x_, g_: row_loss(x_, g_, tgt), x1_ref[...] + gt * out1, fg_ref[...])
        dx2, dfg = loss_vjp(jnp.ones_like(lterm))
        l_ref[...] += jnp.sum(lterm, axis=0, keepdims=True)
        dfg_ref[...] += dfg
        dx_ref[...] = dx2
        dgt_ref[...] += jnp.sum(dx2 * out1, axis=0, keepdims=True)
        d_out16 = (gt * dx2).astype(BF16)
        dwo_ref[...] += lax.dot_general(y3_16, d_out16, tn_dims, preferred_element_type=F32)
        d_y3 = lax.dot_general(d_out16, wo_ref[...], NT_DIMS, preferred_element_type=F32)
        d_ya, d_gl, d_z, d_b = glu_vjp((d_y3,))
        dz_ref[...] = d_z
        db_ref[...] += d_b
        d_gl16 = d_gl.astype(BF16)
        dwg_ref[...] += lax.dot_general(ya16, d_gl16, tn_dims, preferred_element_type=F32)
        d_ya = d_ya + lax.dot_general(d_gl16, wg_ref[...], NT_DIMS, preferred_element_type=F32)
        d_y, d_u, d_d = act_vjp((d_ya,))
        dy_ref[...] = d_y
        du_ref[...] = d_u
        dd_ref[...] += d_d

    row = pl.BlockSpec((tm, d), lambda i: (i, 0))
    vecs = pl.BlockSpec((1, d), lambda i: (0, 0))
    mat = pl.BlockSpec((d, d), lambda i: (0, 0))
    return pl.pallas_call(
        body,
        out_shape=[jax.ShapeDtypeStruct((1, d), F32)] + [jax.ShapeDtypeStruct((n, d), F32)] * 4
        + [jax.ShapeDtypeStruct((1, d), F32)] * 4 + [jax.ShapeDtypeStruct((d, d), F32)] * 2,
        grid=(n // tm,),
        in_specs=[row, pl.BlockSpec((tm, d), lambda i: (i + off, 0)), pl.BlockSpec((tm, d), lambda i: (i + off, 1)),
                  pl.BlockSpec((tm, d), lambda i: (i + off, 0)), row, vecs, vecs, vecs, vecs, mat, mat],
        out_specs=[vecs, row, row, row, row, vecs, vecs, vecs, vecs, mat, mat],
        compiler_params=_params(("arbitrary",), VMEM_LIMIT), name=name)(
            y_ssm, p1, p1, x1p, target, d_vec, b_glu, gate, final_g, w_glu, w_out)


NT_DIMS = (((1,), (1,)), ((), ()))
HEAD_LANES = 128
N_PAIRS = MLA_HEADS // 2


def _own_lanes(shape, hh):
    lane = lax.broadcasted_iota(jnp.int32, shape, len(shape) - 1)
    return (lane < V_HEAD_DIM) if hh == 0 else (lane >= V_HEAD_DIM)


def _delta_lane(hh):
    return V_HEAD_DIM if hh == 0 else 0


def _rope_tiles(x, cos, sin_next, sin_prev, inverse):
    width = x.shape[-1]
    reps = width // HEAD_LANES
    c, sn, sp = (jnp.tile(t, (1, reps)) for t in (cos, sin_next, sin_prev))
    if inverse:
        return x * c + pltpu.roll(x * sn, 8, 1) + pltpu.roll(x * sp, width - 8, 1)
    return x * c + pltpu.roll(x, width - 8, 1) * sn + pltpu.roll(x, 8, 1) * sp


STAT_LANE = QK_DIM


def _with_stat(x16, col, lane0):
    hi = col.astype(BF16)
    r1 = col - hi.astype(F32)
    mid = r1.astype(BF16)
    lo = (r1 - mid.astype(F32)).astype(BF16)
    lane = lax.broadcasted_iota(jnp.int32, x16.shape, 1)
    return jnp.where(lane == lane0, hi, jnp.where(lane == lane0 + 1, mid, jnp.where(lane == lane0 + 2, lo, x16)))


def attn_fwd(qb, kb, vb, n_ctx):
    T = qb.shape[0]
    tq = math.gcd(ROW_TILE, n_ctx)
    nq, ncb = T // tq, n_ctx // tq

    def body(q_ref, k_ref, v_ref, o_ref, lse_ref, qs_ref):
        qi = pl.program_id(1)

        def rows(n_keys):
            v = v_ref[:n_keys, :]
            outs = []
            for hh in range(2):
                hs = slice(hh * HEAD_LANES, (hh + 1) * HEAD_LANES)
                s = lax.dot_general(q_ref[:, hs], k_ref[:n_keys, hs], NT_DIMS,
                                    preferred_element_type=F32) * SOFTMAX_SCALE
                m = jnp.max(s, axis=-1, keepdims=True)
                p = jnp.exp(s - m)
                l = jnp.sum(p, axis=-1, keepdims=True)
                outs.append(jnp.dot(p.astype(BF16), v, preferred_element_type=F32) / l)
                lse = m + jnp.log(l)
                lse_ref[hh] = lse
                qs_ref[:, hs] = _with_stat(q_ref[:, hs], lse * (-1.0 / SOFTMAX_SCALE), STAT_LANE)
            o_ref[...] = jnp.where(_own_lanes(outs[0].shape, 0), outs[0], outs[1])

        pl.when(qi < ncb)(lambda: rows(n_ctx))
        pl.when(qi >= ncb)(lambda: rows(T))

    return pl.pallas_call(
        body,
        out_shape=[jax.ShapeDtypeStruct((T, MLA_HEADS * V_HEAD_DIM), F32),
                   jax.ShapeDtypeStruct((MLA_HEADS, T, 1), F32), jax.ShapeDtypeStruct(qb.shape, BF16)],
        grid=(N_PAIRS, nq),
        in_specs=[pl.BlockSpec((tq, 2 * HEAD_LANES), lambda h, i: (i, h)),
                  pl.BlockSpec((T, 2 * HEAD_LANES), lambda h, i: (0, h)),
                  pl.BlockSpec((T, 2 * V_HEAD_DIM), lambda h, i: (0, h))],
        out_specs=[pl.BlockSpec((tq, 2 * V_HEAD_DIM), lambda h, i: (i, h)),
                   pl.BlockSpec((2, tq, 1), lambda h, i: (h, i, 0)),
                   pl.BlockSpec((tq, 2 * HEAD_LANES), lambda h, i: (i, h))],
        compiler_params=_params(("parallel", "parallel"), VMEM_LIMIT), name="attn_fwd")(qb, kb, vb)


def attn_bwd_dq(qb, kb, vb, o, do, lse, tabs, n_ctx):
    T = qb.shape[0]
    tq = math.gcd(ROW_TILE, n_ctx)
    nq, ncb = T // tq, n_ctx // tq

    def body(q_ref, k_ref, v_ref, o_ref, do_ref, lse_ref, c_ref, sn_ref, sp_ref, dq_ref, dos_ref):
        qi = pl.program_id(1)

        def rows(n_keys):
            v = v_ref[:n_keys, :]
            dqs = []
            for hh in range(2):
                hs = slice(hh * HEAD_LANES, (hh + 1) * HEAD_LANES)
                k = k_ref[:n_keys, hs]
                do = jnp.where(_own_lanes(do_ref.shape, hh), do_ref[...], 0.0)
                delta = jnp.sum(do * o_ref[...], axis=-1, keepdims=True)
                s = lax.dot_general(q_ref[:, hs], k, NT_DIMS, preferred_element_type=F32) * SOFTMAX_SCALE
                p = jnp.exp(s - lse_ref[hh])
                do16 = do.astype(BF16)
                dp = lax.dot_general(do16, v, NT_DIMS, preferred_element_type=F32)
                ds = p * (dp - delta) * SOFTMAX_SCALE
                dqs.append(jnp.dot(ds.astype(BF16), k, preferred_element_type=F32))
                dos_ref[:, hs] = _with_stat(do16, delta, _delta_lane(hh))
            dq = jnp.concatenate(dqs, axis=1)
            dq_ref[...] = _rope_tiles(dq, c_ref[...], sn_ref[...], sp_ref[...], True).astype(BF16)

        pl.when(qi < ncb)(lambda: rows(n_ctx))
        pl.when(qi >= ncb)(lambda: rows(T))

    tab = pl.BlockSpec((tq, HEAD_LANES), lambda h, i: (i, 0))
    return pl.pallas_call(
        body,
        out_shape=[jax.ShapeDtypeStruct((T, MLA_HEADS * HEAD_LANES), BF16)] * 2,
        grid=(N_PAIRS, nq),
        in_specs=[pl.BlockSpec((tq, 2 * HEAD_LANES), lambda h, i: (i, h)),
                  pl.BlockSpec((T, 2 * HEAD_LANES), lambda h, i: (0, h)),
                  pl.BlockSpec((T, 2 * V_HEAD_DIM), lambda h, i: (0, h)),
                  pl.BlockSpec((tq, 2 * V_HEAD_DIM), lambda h, i: (i, h)),
                  pl.BlockSpec((tq, 2 * V_HEAD_DIM), lambda h, i: (i, h)),
                  pl.BlockSpec((2, tq, 1), lambda h, i: (h, i, 0)), tab, tab, tab],
        out_specs=[pl.BlockSpec((tq, 2 * HEAD_LANES), lambda h, i: (i, h))] * 2,
        compiler_params=_params(("parallel", "parallel"), VMEM_LIMIT), name="attn_bwd_dq")(
            qb, kb, vb, o, do, lse, *tabs)


def attn_bwd_dkv(qs, kb, vb, dos, n_ctx):
    T = qs.shape[0]
    tq = math.gcd(ROW_TILE, n_ctx)
    nq, ncb = T // tq, n_ctx // tq

    def body(q_ref, do_ref, k_ref, v_ref, dk_ref, dv_ref):
        kj = pl.program_id(1)

        def cols(first):
            v = v_ref[...]
            lane = lax.broadcasted_iota(jnp.int32, v.shape, 1)
            dvs = []
            for hh in range(2):
                hs = slice(hh * HEAD_LANES, (hh + 1) * HEAD_LANES)
                q = q_ref[first:, hs]
                do16 = do_ref[first:, hs]
                in_delta = (lane >= _delta_lane(hh)) & (lane < _delta_lane(hh) + 3)
                v_minus = jnp.where(in_delta, -jnp.ones_like(v), v)
                pt = jnp.exp(lax.dot_general(k_ref[:, hs], q, NT_DIMS, preferred_element_type=F32) * SOFTMAX_SCALE)
                dvs.append(jnp.dot(pt.astype(BF16), do16, preferred_element_type=F32))
                dst = pt * lax.dot_general(v_minus, do16, NT_DIMS, preferred_element_type=F32) * SOFTMAX_SCALE
                dk_ref[:, hs] = jnp.dot(dst.astype(BF16), q, preferred_element_type=F32)
            dv_ref[...] = jnp.where(_own_lanes(dvs[0].shape, 0), dvs[0], dvs[1])

        pl.when(kj < ncb)(lambda: cols(0))
        pl.when(kj >= ncb)(lambda: cols(n_ctx))

    return pl.pallas_call(
        body,
        out_shape=[jax.ShapeDtypeStruct((T, MLA_HEADS * HEAD_LANES), F32),
                   jax.ShapeDtypeStruct((T, MLA_HEADS * V_HEAD_DIM), F32)],
        grid=(N_PAIRS, nq),
        in_specs=[pl.BlockSpec((T, 2 * HEAD_LANES), lambda h, j: (0, h)),
                  pl.BlockSpec((T, 2 * HEAD_LANES), lambda h, j: (0, h)),
                  pl.BlockSpec((tq, 2 * HEAD_LANES), lambda h, j: (j, h)),
                  pl.BlockSpec((tq, 2 * V_HEAD_DIM), lambda h, j: (j, h))],
        out_specs=[pl.BlockSpec((tq, 2 * HEAD_LANES), lambda h, j: (j, h)),
                   pl.BlockSpec((tq, 2 * V_HEAD_DIM), lambda h, j: (j, h))],
        compiler_params=_params(("parallel", "parallel"), VMEM_LIMIT), name="attn_bwd_dkv")(
            qs, dos, kb, vb)


def _split_bf16(x):
    hi = x.astype(BF16)
    return hi, (x - hi.astype(F32)).astype(BF16)


def q_heads(cq, gain, w_uq_p, tabs, name="l0_uq"):
    T, K = cq.arr.shape[0], cq.width
    N = w_uq_p.shape[1]
    tm = math.gcd(ROW_TILE, T)

    def body(a_ref, g_ref, w_ref, c_ref, sn_ref, sp_ref, o_ref, n_ref):
        qn = f_rms(a_ref[...], g_ref[0])[0].astype(BF16)
        n_ref[...] = qn
        acc = jnp.dot(qn, w_ref[...], preferred_element_type=F32)
        o_ref[...] = _rope_tiles(acc, c_ref[...], sn_ref[...], sp_ref[...], False).astype(BF16)

    tab = pl.BlockSpec((tm, HEAD_LANES), lambda i: (i, 0))
    return pl.pallas_call(
        body, out_shape=[jax.ShapeDtypeStruct((T, N), BF16), jax.ShapeDtypeStruct((T, K), BF16)], grid=(T // tm,),
        in_specs=[cq.spec(tm), pl.BlockSpec((1, 1, K), lambda i: (0, 0, 0)), pl.BlockSpec((K, N), lambda i: (0, 0)),
                  tab, tab, tab],
        out_specs=[pl.BlockSpec((tm, N), lambda i: (i, 0)), pl.BlockSpec((tm, K), lambda i: (i, 0))],
        compiler_params=_params(("parallel",), VMEM_LIMIT), name=name)(cq.arr, gain, w_uq_p, *tabs)


def kv_heads(ckv, gain, w_kn_p, w_v, kr, spread, tabs, name="l0_ukv"):
    T, K = ckv.arr.shape[0], ckv.width
    N = w_kn_p.shape[1]
    NV = w_v.shape[1]
    tm = math.gcd(ROW_TILE, T)

    def body(a_ref, g_ref, wk_ref, wv_ref, kr_ref, e_ref, c_ref, sn_ref, sp_ref, k_ref, v_ref, n_ref):
        a = f_rms(a_ref[...], g_ref[0])[0].astype(BF16)
        n_ref[...] = a
        hi, lo = _split_bf16(kr_ref[...])
        acc = (jnp.dot(a, wk_ref[...], preferred_element_type=F32)
               + jnp.dot(hi, e_ref[...], preferred_element_type=F32)
               + jnp.dot(lo, e_ref[...], preferred_element_type=F32))
        roped = _rope_tiles(acc, c_ref[...], sn_ref[...], sp_ref[...], False)
        lane = lax.broadcasted_iota(jnp.int32, roped.shape, 1) % HEAD_LANES
        k_ref[...] = jnp.where((lane >= STAT_LANE) & (lane < STAT_LANE + 3), 1.0, roped).astype(BF16)
        v_ref[...] = jnp.dot(a, wv_ref[...], preferred_element_type=F32).astype(BF16)

    tab = pl.BlockSpec((tm, HEAD_LANES), lambda i: (i, 0))
    return pl.pallas_call(
        body, out_shape=[jax.ShapeDtypeStruct((T, N), BF16), jax.ShapeDtypeStruct((T, NV), BF16),
                         jax.ShapeDtypeStruct((T, K), BF16)], grid=(T // tm,),
        in_specs=[ckv.spec(tm), pl.BlockSpec((1, 1, K), lambda i: (0, 0, 0)), pl.BlockSpec((K, N), lambda i: (0, 0)),
                  pl.BlockSpec((K, NV), lambda i: (0, 0)), kr.spec(tm),
                  pl.BlockSpec((kr.width, N), lambda i: (0, 0)), tab, tab, tab],
        out_specs=[pl.BlockSpec((tm, N), lambda i: (i, 0)), pl.BlockSpec((tm, NV), lambda i: (i, 0)),
                   pl.BlockSpec((tm, K), lambda i: (i, 0))],
        compiler_params=_params(("parallel",), VMEM_LIMIT), name=name)(
            ckv.arr, gain, w_kn_p, w_v, kr.arr, spread, *tabs)


def heads_unrope(d, tabs, spread=None, name="unrope"):
    T, N = d.shape
    tm = math.gcd(ROW_TILE, T)

    def body(*refs):
        if spread is None:
            d_ref, c_ref, sn_ref, sp_ref, o_ref = refs
        else:
            d_ref, c_ref, sn_ref, sp_ref, e_ref, o_ref, kr_ref = refs
        g = _rope_tiles(d_ref[...], c_ref[...], sn_ref[...], sp_ref[...], True)
        o_ref[...] = g.astype(BF16)
        if spread is not None:
            hi, lo = _split_bf16(g)
            kr_ref[...] = (lax.dot_general(hi, e_ref[...], NT_DIMS, preferred_element_type=F32)
                           + lax.dot_general(lo, e_ref[...], NT_DIMS, preferred_element_type=F32))

    tab = pl.BlockSpec((tm, HEAD_LANES), lambda i: (i, 0))
    row = pl.BlockSpec((tm, N), lambda i: (i, 0))
    ins, in_specs = [d, *tabs], [row, tab, tab, tab]
    out_shape, out_specs = [jax.ShapeDtypeStruct((T, N), BF16)], [row]
    if spread is not None:
        ins.append(spread)
        in_specs.append(pl.BlockSpec(spread.shape, lambda i: (0, 0)))
        out_shape.append(jax.ShapeDtypeStruct((T, spread.shape[0]), F32))
        out_specs.append(pl.BlockSpec((tm, spread.shape[0]), lambda i: (i, 0)))
    return pl.pallas_call(
        body, out_shape=out_shape, grid=(T // tm,), in_specs=in_specs, out_specs=out_specs,
        compiler_params=_params(("parallel",), VMEM_LIMIT), name=name)(*ins)


def _cmul(ar, ai, br, bi):
    return ar * br - ai * bi, ar * bi + ai * br


def s5_chain(finals, s0, a, n_steps, reverse, name):
    W = finals.shape[-1]
    first = N_SEG - 1 if reverse else 0

    def body(f_ref, s0_ref, a_ref, c_ref):
        pr, pi = jnp.ones((1, W), F32), jnp.zeros((1, W), F32)
        br, bi = a_ref[0], a_ref[1]
        n = n_steps
        while n:
            if n & 1:
                pr, pi = _cmul(pr, pi, br, bi)
            br, bi = _cmul(br, bi, br, bi)
            n >>= 1
        fr, fi = f_ref[0], f_ref[1]
        row = lax.broadcasted_iota(jnp.int32, (N_SEG, W), 0)
        s0r = jnp.broadcast_to(s0_ref[0], (N_SEG, W))
        s0i = jnp.broadcast_to(s0_ref[1], (N_SEG, W))
        cr = jnp.where(row == first, s0r, 0.0)
        ci = jnp.where(row == first, s0i, 0.0)
        shift = N_SEG - 1 if reverse else 1
        for _ in range(N_SEG - 1):
            mr, mi = _cmul(pr, pi, cr, ci)
            tr = pltpu.roll(fr + mr, shift, 0)
            ti = pltpu.roll(fi + mi, shift, 0)
            cr = jnp.where(row == first, s0r, tr)
            ci = jnp.where(row == first, s0i, ti)
        c_ref[0] = cr
        c_ref[1] = ci

    return pl.pallas_call(body, out_shape=jax.ShapeDtypeStruct((2, N_SEG, W), F32), name=name)(finals, s0, a)


def _scan_chunk(bur, bui, st_ref, a_ref, n_steps, reverse):
    for lc in range(S5_LANES // BLK_ST):
        sl = slice(lc * BLK_ST, (lc + 1) * BLK_ST)
        lr = jnp.broadcast_to(a_ref[0, :, sl], (N_SEG, BLK_ST))
        li = jnp.broadcast_to(a_ref[1, :, sl], (N_SEG, BLK_ST))

        def step(jj, carry, sl=sl, lr=lr, li=li):
            sr, si = carry
            j = (n_steps - 1 - jj) if reverse else jj
            r0 = pl.multiple_of(j * N_SEG, N_SEG)
            nr = lr * sr - li * si + bur[pl.ds(r0, N_SEG), sl]
            ni = lr * si + li * sr + bui[pl.ds(r0, N_SEG), sl]
            bur[pl.ds(r0, N_SEG), sl] = nr
            bui[pl.ds(r0, N_SEG), sl] = ni
            return nr, ni

        sr, si = lax.fori_loop(0, n_steps, step, (st_ref[0, :, sl], st_ref[1, :, sl]))
        st_ref[0, :, sl] = sr
        st_ref[1, :, sl] = si


def _project_in(x16, w_re, w_im, bur, bui, adjoint):
    for gb in range(N_BLOCKS):
        xb = x16[:, gb * BLK_CH:(gb + 1) * BLK_CH]
        sl = slice(gb * BLK_ST, (gb + 1) * BLK_ST)
        if adjoint:
            dn = (((1,), (1,)), ((), ()))
            bur[:, sl] = lax.dot_general(xb, w_re[gb], dn, preferred_element_type=F32)
            bui[:, sl] = -lax.dot_general(xb, w_im[gb], dn, preferred_element_type=F32)
        else:
            bur[:, sl] = jnp.dot(xb, w_re[gb], preferred_element_type=F32)
            bui[:, sl] = jnp.dot(xb, w_im[gb], preferred_element_type=F32)


def s5_scan(act, w_re, w_im, a, init, *, reverse, adjoint=False, c_re=None, c_im=None, add=None,
            want_ckpt=False, rows=None, name):
    act_off, N = rows if rows is not None else (0, act.shape[0])
    R = math.gcd(ROW_TILE, N, act_off)
    nch, jc = N // R, R // N_SEG
    with_out = c_re is not None

    def chunk(i):
        return (nch - 1 - i) if reverse else i

    def body(*refs):
        act_ref, wre_ref, wim_ref, a_ref, init_ref = refs[:5]
        k = 5
        if with_out:
            cre_ref, cim_ref = refs[k:k + 2]
            k += 2
        if add is not None:
            add_ref = refs[k]
            k += 1
        if with_out:
            out_ref = refs[k]
            k += 1
        if want_ckpt:
            ck_ref = refs[k]
            k += 1
        fin_ref, bur, bui = refs[k:k + 3]

        @pl.when(pl.program_id(0) == 0)
        def _():
            fin_ref[...] = init_ref[...]

        if want_ckpt:
            ck_ref[0] = fin_ref[...]
        _project_in(act_ref[...].astype(BF16), wre_ref, wim_ref, bur, bui, adjoint)
        _scan_chunk(bur, bui, fin_ref, a_ref, jc, reverse)
        if with_out:
            for gb in range(N_BLOCKS):
                sl = slice(gb * BLK_ST, (gb + 1) * BLK_ST)
                y = (jnp.dot(bur[:, sl].astype(BF16), cre_ref[gb], preferred_element_type=F32)
                     - jnp.dot(bui[:, sl].astype(BF16), cim_ref[gb], preferred_element_type=F32))
                cs = slice(gb * BLK_CH, (gb + 1) * BLK_CH)
                if add is not None:
                    y = y + add_ref[:, cs]
                out_ref[:, cs] = y

    row_spec = pl.BlockSpec((R, D_MODEL), lambda i: (chunk(i), 0))
    act_spec = pl.BlockSpec((R, D_MODEL), lambda i: (chunk(i) + act_off // R, 0))
    w_spec = pl.BlockSpec(w_re.shape, lambda i: (0, 0, 0))
    st_spec = pl.BlockSpec((2, N_SEG, S5_LANES), lambda i: (0, 0, 0))
    ins = [act, w_re, w_im, a, init]
    in_specs = [act_spec, w_spec, w_spec, pl.BlockSpec((2, 1, S5_LANES), lambda i: (0, 0, 0)), st_spec]
    if with_out:
        ins += [c_re, c_im]
        in_specs += [pl.BlockSpec(c_re.shape, lambda i: (0, 0, 0))] * 2
    if add is not None:
        ins.append(add)
        in_specs.append(row_spec)
    out_shape, out_specs = [], []
    if with_out:
        out_shape.append(jax.ShapeDtypeStruct((N, D_MODEL), F32))
        out_specs.append(row_spec)
    if want_ckpt:
        out_shape.append(jax.ShapeDtypeStruct((nch, 2, N_SEG, S5_LANES), F32))
        out_specs.append(pl.BlockSpec((1, 2, N_SEG, S5_LANES), lambda i: (chunk(i), 0, 0, 0)))
    out_shape.append(jax.ShapeDtypeStruct((2, N_SEG, S5_LANES), F32))
    out_specs.append(st_spec)
    res = pl.pallas_call(
        body, out_shape=out_shape, grid=(nch,), in_specs=in_specs, out_specs=out_specs,
        scratch_shapes=[pltpu.VMEM((R, S5_LANES), F32), pltpu.VMEM((R, S5_LANES), F32)],
        compiler_params=_params(("arbitrary",), VMEM_LIMIT), name=name)(*ins)
    res = list(res)
    out = res.pop(0) if with_out else None
    ckpt = res.pop(0) if want_ckpt else None
    return out, ckpt, res[0]


def s5_grads(dy, u, ckpt, b_re, b_im, c_re, c_im, lam, init_adj, *, reverse, add=None, u_off=0, du_rows=None,
             du_into=None, n_rows=None, name):
    N = n_rows if dy is None else dy.shape[0]
    du_total, du_first = du_rows if du_rows is not None else (N, 0)
    R = math.gcd(ROW_TILE, N, u_off, du_first)
    nch, jc = N // R, R // N_SEG
    W = S5_LANES

    def chunk(i):
        return i if reverse else (nch - 1 - i)

    def body(*refs):
        if dy is None:
            refs = (None,) + refs[:4] + (None, None) + refs[4:]
        dy_ref, u_ref, ck_ref, bre_ref, bim_ref, cre_ref, cim_ref, lam_ref, init_ref = refs[:9]
        k = 9
        if add is not None:
            add_ref = refs[k]
            k += 1
        k += du_into is not None
        outs = refs[k:]
        if dy is None:
            outs = outs[:4] + (None, None) + outs[4:]
        du_ref, dlam_ref, dbre_ref, dbim_ref, dcre_ref, dcim_ref, fin_ref = outs[:7]
        sr_buf, si_buf, er_buf, ei_buf, st_buf = outs[7:12]

        @pl.when(pl.program_id(0) == 0)
        def _():
            fin_ref[...] = init_ref[...]
            dlam_ref[...] = jnp.zeros_like(dlam_ref)
            dbre_ref[...] = jnp.zeros_like(dbre_ref)
            dbim_ref[...] = jnp.zeros_like(dbim_ref)
            if dy is not None:
                dcre_ref[...] = jnp.zeros_like(dcre_ref)
                dcim_ref[...] = jnp.zeros_like(dcim_ref)

        u16 = u_ref[...].astype(BF16)
        st_buf[...] = ck_ref[0]
        _project_in(u16, bre_ref, bim_ref, sr_buf, si_buf, False)
        _scan_chunk(sr_buf, si_buf, st_buf, lam_ref, jc, reverse)
        if dy is None:
            er_buf[...] = jnp.zeros_like(er_buf)
            ei_buf[...] = jnp.zeros_like(ei_buf)
        else:
            dy16 = dy_ref[...].astype(BF16)
            _project_in(dy16, cre_ref, cim_ref, er_buf, ei_buf, True)
        for lc in range(W // BLK_ST):
            sl = slice(lc * BLK_ST, (lc + 1) * BLK_ST)
            lr = jnp.broadcast_to(lam_ref[0, :, sl], (N_SEG, BLK_ST))
            li = jnp.broadcast_to(lam_ref[1, :, sl], (N_SEG, BLK_ST))

            def one(r0, spr, spi, carry, sl=sl, lr=lr, li=li):
                gr, gi, ar, ai = carry
                nr = er_buf[pl.ds(r0, N_SEG), sl] + lr * gr + li * gi
                ni = ei_buf[pl.ds(r0, N_SEG), sl] + lr * gi - li * gr
                er_buf[pl.ds(r0, N_SEG), sl] = nr
                ei_buf[pl.ds(r0, N_SEG), sl] = ni
                return nr, ni, ar + spr * nr + spi * ni, ai + spr * ni - spi * nr

            def step(ff, carry, sl=sl, one=one):
                f = jc - 1 - ff
                j = (jc - 1 - f) if reverse else f
                jp = (j + 1) if reverse else (j - 1)
                r0 = pl.multiple_of(j * N_SEG, N_SEG)
                p0 = pl.multiple_of(jp * N_SEG, N_SEG)
                return one(r0, sr_buf[pl.ds(p0, N_SEG), sl], si_buf[pl.ds(p0, N_SEG), sl], carry)

            carry = (fin_ref[0, :, sl], fin_ref[1, :, sl], dlam_ref[0, :, sl], dlam_ref[1, :, sl])
            carry = lax.fori_loop(0, jc - 1, step, carry)
            r_first = (jc - 1) * N_SEG if reverse else 0
            gr, gi, ar, ai = one(r_first, ck_ref[0, 0, :, sl], ck_ref[0, 1, :, sl], carry)
            fin_ref[0, :, sl] = gr
            fin_ref[1, :, sl] = gi
            dlam_ref[0, :, sl] = ar
            dlam_ref[1, :, sl] = ai
        tn = (((0,), (0,)), ((), ()))
        nt = (((1,), (1,)), ((), ()))
        for gb in range(N_BLOCKS):
            sl = slice(gb * BLK_ST, (gb + 1) * BLK_ST)
            cs = slice(gb * BLK_CH, (gb + 1) * BLK_CH)
            gr16 = er_buf[:, sl].astype(BF16)
            gi16 = ei_buf[:, sl].astype(BF16)
            du = (lax.dot_general(gr16, bre_ref[gb], nt, preferred_element_type=F32)
                  + lax.dot_general(gi16, bim_ref[gb], nt, preferred_element_type=F32))
            if add is not None:
                du = du + add_ref[:, cs]
            du_ref[:, cs] = du
            ub = u16[:, cs]
            dbre_ref[gb] += lax.dot_general(ub, gr16, tn, preferred_element_type=F32)
            dbim_ref[gb] += lax.dot_general(ub, gi16, tn, preferred_element_type=F32)
            if dy is not None:
                dyb = dy16[:, cs]
                dcre_ref[gb] += lax.dot_general(sr_buf[:, sl].astype(BF16), dyb, tn, preferred_element_type=F32)
                dcim_ref[gb] -= lax.dot_general(si_buf[:, sl].astype(BF16), dyb, tn, preferred_element_type=F32)

    row_spec = pl.BlockSpec((R, D_MODEL), lambda i: (chunk(i), 0))
    st_spec = pl.BlockSpec((2, N_SEG, W), lambda i: (0, 0, 0))
    wb_spec = pl.BlockSpec(b_re.shape, lambda i: (0, 0, 0))
    wc_spec = pl.BlockSpec(c_re.shape, lambda i: (0, 0, 0))
    ins = [dy, u, ckpt, b_re, b_im, c_re, c_im, lam, init_adj]
    u_spec = pl.BlockSpec((R, D_MODEL), lambda i: (chunk(i) + u_off // R, 0))
    in_specs = [row_spec, u_spec, pl.BlockSpec((1, 2, N_SEG, W), lambda i: (chunk(i), 0, 0, 0)),
                wb_spec, wb_spec, wc_spec, wc_spec, pl.BlockSpec((2, 1, W), lambda i: (0, 0, 0)), st_spec]
    if dy is None:
        ins, in_specs = ins[1:5] + ins[7:], in_specs[1:5] + in_specs[7:]
    if add is not None:
        ins.append(add)
        in_specs.append(row_spec)
    aliases = {}
    if du_into is not None:
        aliases[len(ins)] = 0
        ins.append(du_into)
        in_specs.append(pl.BlockSpec(memory_space=pl.ANY))
    du_spec = pl.BlockSpec((R, D_MODEL), lambda i: (chunk(i) + du_first // R, 0))
    out_shape = [jax.ShapeDtypeStruct((du_total, D_MODEL), F32), jax.ShapeDtypeStruct((2, N_SEG, W), F32),
                 jax.ShapeDtypeStruct(b_re.shape, F32), jax.ShapeDtypeStruct(b_re.shape, F32),
                 jax.ShapeDtypeStruct(c_re.shape, F32), jax.ShapeDtypeStruct(c_re.shape, F32),
                 jax.ShapeDtypeStruct((2, N_SEG, W), F32)]
    out_specs = [du_spec, st_spec, wb_spec, wb_spec, wc_spec, wc_spec, st_spec]
    if dy is None:
        out_shape, out_specs = out_shape[:4] + out_shape[6:], out_specs[:4] + out_specs[6:]
    res = pl.pallas_call(
        body, out_shape=out_shape, grid=(nch,), in_specs=in_specs, out_specs=out_specs, input_output_aliases=aliases,
        scratch_shapes=[pltpu.VMEM((R, W), F32) for _ in range(4)] + [pltpu.VMEM((2, N_SEG, W), F32)],
        compiler_params=_params(("arbitrary",), VMEM_LIMIT), name=name)(*ins)
    return res if dy is not None else [*res[:4], None, None, res[4]]


def adamw(w, g, m, v, name="adamw", after=None):
    n, d = w.shape
    lanes = -(-d // 128) * 128
    tm = n
    while tm * lanes * 4 > (1 << 20) and tm % 16 == 0:
        tm //= 2
    c1 = 1.0 - ADAM_B1 ** ADAM_STEP
    c2 = 1.0 - ADAM_B2 ** ADAM_STEP

    def body(w_ref, g_ref, m_ref, v_ref, *rest):
        d_ref, nm_ref, nv_ref = rest[-3:]
        g_ = g_ref[...]
        m_ = ADAM_B1 * m_ref[...] + (1.0 - ADAM_B1) * g_
        v_ = ADAM_B2 * v_ref[...] + (1.0 - ADAM_B2) * (g_ * g_)
        d_ref[...] = -ADAM_LR * ((m_ / c1) / (jnp.sqrt(v_ / c2) + ADAM_EPS) + ADAM_WD * w_ref[...])
        nm_ref[...] = m_
        nv_ref[...] = v_

    spec = pl.BlockSpec((tm, d), lambda i: (i, 0))
    extra = [] if after is None else [after]
    return pl.pallas_call(
        body, out_shape=[jax.ShapeDtypeStruct((n, d), F32)] * 3, grid=(n // tm,),
        in_specs=[spec] * 4 + [pl.BlockSpec(memory_space=pl.ANY)] * len(extra), out_specs=[spec] * 3,
        compiler_params=_params(("parallel",), VMEM_LIMIT), name=name)(w, g, m, v, *extra)


def _coords():
    return lax.axis_index("x"), lax.axis_index("y"), lax.axis_index("c")


def exchange(arrays, out_shapes, remote, local, name, aliases=None):
    n_in, n_out, n_rem, n_loc = len(arrays), len(out_shapes), len(remote), len(local)

    def at(ref, idx):
        return ref if idx is None else ref.at[idx]

    def body(*refs):
        ins, outs = refs[:n_in], refs[n_in:n_in + n_out]
        send_sems, recv_sems, local_sems = refs[n_in + n_out:]
        me = _coords()
        sends, recvs = [], []
        for k, (flip, ii, src_at, oi, dst_at) in enumerate(remote):
            peer = (me[0] ^ flip[0], me[1] ^ flip[1], me[2] ^ flip[2])
            src = at(ins[ii], src_at(me, peer))
            sends.append(pltpu.make_async_remote_copy(
                src_ref=src, dst_ref=at(outs[oi], dst_at(me)), send_sem=send_sems.at[k], recv_sem=recv_sems.at[k],
                device_id=peer, device_id_type=MESH))
            recvs.append(pltpu.make_async_remote_copy(
                src_ref=src, dst_ref=at(outs[oi], dst_at(peer)), send_sem=send_sems.at[k], recv_sem=recv_sems.at[k],
                device_id=peer, device_id_type=MESH))
        locs = [pltpu.make_async_copy(at(ins[ii], src_at(me)), at(outs[oi], dst_at(me)), local_sems.at[k])
                for k, (ii, src_at, oi, dst_at) in enumerate(local)]
        for cp in locs + sends:
            cp.start()
        for cp in recvs:
            cp.wait_recv()
        for cp in sends:
            cp.wait_send()
        for cp in locs:
            cp.wait()

    hbm = pl.BlockSpec(memory_space=pl.ANY)
    return pl.pallas_call(
        body, out_shape=list(out_shapes), in_specs=[hbm] * n_in, out_specs=[hbm] * n_out,
        scratch_shapes=[pltpu.SemaphoreType.DMA((n_rem,)), pltpu.SemaphoreType.DMA((n_rem,)),
                        pltpu.SemaphoreType.DMA((max(n_loc, 1),))],
        input_output_aliases=aliases or {}, name=name)(*arrays)


ALL_FLIPS = [(dx, dy, dc) for dx in (0, 1) for dy in (0, 1) for dc in (0, 1)][1:]
CHIP_FLIPS = [(1, 0, 0), (0, 1, 0), (1, 1, 0)]
CORE_FLIP = (0, 0, 1)


def _dev_index(p):
    return 4 * p[0] + 2 * p[1] + p[2]


def _chip_index(p):
    return 2 * p[0] + p[1]


def _gather(xs, flips, index, n, name):
    arrays = [x[None] for x in xs]
    outs = [jax.ShapeDtypeStruct((n,) + x.shape, x.dtype) for x in xs]
    remote = [(f, a, lambda me, peer: (0,), a, lambda s: (index(s),)) for a in range(len(xs)) for f in flips]
    local = [(a, lambda me: (0,), a, lambda me: (index(me),)) for a in range(len(xs))]
    return exchange(arrays, outs, remote, local, name)


def allgather_devices(x, name):
    return _gather([x], ALL_FLIPS, _dev_index, N_DEV, name)[0]


def allgather_chips(xs, name):
    return _gather(xs, CHIP_FLIPS, _chip_index, N_CHIP, name)


def gather_halves(xs, name):
    n = len(xs)
    nk = n * len(CHIP_FLIPS)

    def body(*refs):
        ins, outs = refs[:n], refs[n:2 * n]
        ici_send, ici_recv, d2d_send, d2d_recv = refs[2 * n:]
        me = _coords()
        sibling = (me[0], me[1], 1 - me[2])
        first, passed, landed = [], [], []
        for a in range(n):
            half = ins[a].shape[0] // 2
            mine = ins[a].at[pl.ds(pl.multiple_of(me[2] * half, 16), half)]
            for j, flip in enumerate(CHIP_FLIPS):
                k = a * len(CHIP_FLIPS) + j
                peer = (me[0] ^ flip[0], me[1] ^ flip[1], me[2])
                first.append(pltpu.make_async_remote_copy(
                    src_ref=mine, dst_ref=outs[a].at[_chip_index(me), me[2]], send_sem=ici_send.at[k],
                    recv_sem=ici_recv.at[k], device_id=peer, device_id_type=MESH))
                arrived = outs[a].at[_chip_index(peer), me[2]]
                landed.append(pltpu.make_async_remote_copy(
                    src_ref=mine, dst_ref=arrived, send_sem=ici_send.at[k], recv_sem=ici_recv.at[k],
                    device_id=peer, device_id_type=MESH))
                passed.append(pltpu.make_async_remote_copy(
                    src_ref=arrived, dst_ref=arrived, send_sem=d2d_send.at[k], recv_sem=d2d_recv.at[k],
                    device_id=sibling, device_id_type=MESH))
        for cp in first:
            cp.start()
        for k in range(nk):
            landed[k].wait_recv()
            passed[k].start()
        for a in range(n):
            for j, flip in enumerate(CHIP_FLIPS):
                k = a * len(CHIP_FLIPS) + j
                peer_chip = _chip_index((me[0] ^ flip[0], me[1] ^ flip[1]))
                from_sibling = outs[a].at[peer_chip, 1 - me[2]]
                pltpu.make_async_remote_copy(
                    src_ref=from_sibling, dst_ref=from_sibling, send_sem=d2d_send.at[k], recv_sem=d2d_recv.at[k],
                    device_id=sibling, device_id_type=MESH).wait_recv()
        for cp in first + passed:
            cp.wait_send()

    hbm = pl.BlockSpec(memory_space=pl.ANY)
    return pl.pallas_call(
        body, out_shape=[jax.ShapeDtypeStruct((N_CHIP, 2, x.shape[0] // 2, x.shape[1]), x.dtype) for x in xs],
        in_specs=[hbm] * n, out_specs=[hbm] * n,
        scratch_shapes=[pltpu.SemaphoreType.DMA((nk,)) for _ in range(4)], name=name)(*xs)


HBM_SPEC = pl.BlockSpec(memory_space=pltpu.HBM)
SEM_SPEC = pl.BlockSpec(memory_space=pltpu.SEMAPHORE)
DATAFLOW = pltpu.SideEffectType.DATAFLOW_SIDE_EFFECTING


def _at(ref, idx):
    return ref if idx is None else ref.at[idx]


def _peer(me, flip):
    return (me[0] ^ flip[0], me[1] ^ flip[1], me[2] ^ flip[2])


def exchange_start(arrays, land_shapes, remote, name, after=None):
    n_in, n_out, nk = len(arrays), len(land_shapes), len(remote)
    after = list(after or [])
    n_after = len(after)

    def body(*refs):
        srcs, lands = refs[:n_in], refs[n_in:n_in + n_out]
        first_out = n_in + n_out + n_after
        send_sems, recv_sems, token = refs[first_out], refs[first_out + 1], refs[-1]
        me = _coords()
        for k, (flip, ii, src_at, oi, dst_at) in enumerate(remote):
            peer = _peer(me, flip)
            pltpu.make_async_remote_copy(
                src_ref=_at(srcs[ii], src_at(me, peer)), dst_ref=_at(lands[oi], dst_at(me)), send_sem=send_sems.at[k],
                recv_sem=recv_sems.at[k], device_id=peer, device_id_type=MESH).start()
        token[...] = jnp.zeros_like(token)

    lands = [lax.empty(s.shape, s.dtype) for s in land_shapes]
    bufs = list(arrays) + lands
    out = pl.pallas_call(
        body, name=name,
        out_shape=(pltpu.SemaphoreType.DMA((nk,)), pltpu.SemaphoreType.DMA((nk,)),
                   *[pltpu.HBM(b.shape, b.dtype) for b in bufs], jax.ShapeDtypeStruct((8, 128), F32)),
        in_specs=[HBM_SPEC] * len(bufs) + [pl.BlockSpec(memory_space=pl.ANY)] * n_after,
        out_specs=(SEM_SPEC, SEM_SPEC, *[HBM_SPEC] * len(bufs), pl.BlockSpec(memory_space=pltpu.VMEM)),
        input_output_aliases={a: 2 + a for a in range(len(bufs))},
        compiler_params=pltpu.CompilerParams(has_side_effects=DATAFLOW),
    )(*[pltpu.with_memory_space_constraint(b, pltpu.HBM) for b in bufs], *after)
    flight = (out[0], out[1], list(out[2:2 + n_in]), list(out[2 + n_in:2 + n_in + n_out]), remote)
    return flight, out[-1]


def exchange_wait(flight, after, name):
    send_sems, recv_sems, arrays, lands, remote = flight
    n_in, n_out = len(arrays), len(lands)
    after = list(after) if isinstance(after, (list, tuple)) else [after]

    def body(*refs):
        srcs, lnds = refs[:n_in], refs[n_in:n_in + n_out]
        s_sems, r_sems = refs[n_in + n_out], refs[n_in + n_out + 1]
        me = _coords()
        for k, (flip, ii, src_at, oi, dst_at) in enumerate(remote):
            peer = _peer(me, flip)
            copy = pltpu.make_async_remote_copy(
                src_ref=_at(srcs[ii], src_at(me, peer)), dst_ref=_at(lnds[oi], dst_at(peer)), send_sem=s_sems.at[k],
                recv_sem=r_sems.at[k], device_id=peer, device_id_type=MESH)
            copy.wait_send()
            copy.wait_recv()

    bufs = list(arrays) + list(lands)
    out = pl.pallas_call(
        body, name=name,
        out_shape=tuple(pltpu.HBM(b.shape, b.dtype) for b in bufs),
        in_specs=[HBM_SPEC] * len(bufs) + [SEM_SPEC, SEM_SPEC] + [pl.BlockSpec(memory_space=pl.ANY)] * len(after),
        out_specs=tuple([HBM_SPEC] * len(bufs)),
        input_output_aliases={a: a for a in range(len(bufs))},
        compiler_params=pltpu.CompilerParams(has_side_effects=DATAFLOW),
    )(*bufs, send_sems, recv_sems, *after)
    return list(out[:n_in]), list(out[n_in:])


def _half_tile(h, cd):
    return h if h * cd * 4 <= (1 << 20) else math.gcd(512, h)


def pair_add(g, got, core, out_dtype, name):
    _, _, h, cd = g.shape
    th = _half_tile(h, cd)

    def body(c_ref, g_ref, got_ref, o_ref):
        o_ref[0] = (g_ref[0, 0] + got_ref[0]).astype(o_ref.dtype)

    return pl.pallas_call(
        body, out_shape=jax.ShapeDtypeStruct((N_CHIP, h, cd), out_dtype),
        grid_spec=pltpu.PrefetchScalarGridSpec(
            num_scalar_prefetch=1, grid=(N_CHIP, h // th),
            in_specs=[pl.BlockSpec((1, 1, th, cd), lambda q, i, c: (q, c[0], i, 0)),
                      pl.BlockSpec((1, th, cd), lambda q, i, c: (q, i, 0))],
            out_specs=pl.BlockSpec((1, th, cd), lambda q, i, c: (q, i, 0))),
        compiler_params=_params(("parallel", "parallel"), VMEM_LIMIT), name=name)(core, g, got)


def sum_chips(parts, sums, place, name):
    _, h, cd = parts.shape
    th = _half_tile(h, cd)

    def body(pc_ref, p_ref, own_ref, o_ref):
        acc = None
        for q in range(N_CHIP):
            term = jnp.where(pc_ref[1] == q, own_ref[0], p_ref[q]).astype(F32)
            acc = term if acc is None else acc + term
        o_ref[0] = acc

    return pl.pallas_call(
        body, out_shape=jax.ShapeDtypeStruct((2, h, cd), F32),
        grid_spec=pltpu.PrefetchScalarGridSpec(
            num_scalar_prefetch=1, grid=(h // th,),
            in_specs=[pl.BlockSpec((N_CHIP, th, cd), lambda i, pc: (0, i, 0)),
                      pl.BlockSpec((1, th, cd), lambda i, pc: (pc[1], i, 0))],
            out_specs=pl.BlockSpec((1, th, cd), lambda i, pc: (pc[0], i, 0))),
        compiler_params=_params(("parallel",), VMEM_LIMIT), name=name)(place, parts, sums)


def to_segments(a, n_ctx):
    def one(p):
        n = p.shape[0]
        return p.reshape(N_SEG, n // N_SEG, -1).transpose(1, 0, 2).reshape(n, -1)
    return jnp.concatenate([one(a[:n_ctx]), one(a[n_ctx:])], axis=0) if n_ctx else one(a)


def rows_to_segments(a, n_ctx, name):
    n, d = a.shape
    tj = n_ctx // N_SEG
    per_seg = (n - n_ctx) // N_SEG // tj
    assert n_ctx % N_SEG == 0 and (n - n_ctx) % (N_SEG * tj) == 0

    def body(*refs):
        out_ref, slabs = refs[N_SEG:]
        for c in range(d // HEAD_LANES):
            cols = slice(c * HEAD_LANES, (c + 1) * HEAD_LANES)
            for seg in range(N_SEG):
                slabs[c, pl.ds(seg, tj, stride=N_SEG), :] = refs[seg][:, cols]
            out_ref[:, cols] = slabs[c]

    def seg_spec(seg):
        return pl.BlockSpec((tj, d), lambda i: (jnp.where(i == 0, seg, N_SEG + seg * per_seg + i - 1), 0))

    return pl.pallas_call(
        body, out_shape=jax.ShapeDtypeStruct((n, d), a.dtype), grid=(1 + per_seg,),
        in_specs=[seg_spec(seg) for seg in range(N_SEG)], out_specs=pl.BlockSpec((N_SEG * tj, d), lambda i: (i, 0)),
        scratch_shapes=[pltpu.VMEM((d // HEAD_LANES, N_SEG * tj, HEAD_LANES), a.dtype)],
        compiler_params=_params(("parallel",), VMEM_LIMIT), name=name)(*[a] * N_SEG)


def rope_tables(n_ctx, n_lat):
    f32 = np.float32
    rows = n_lat // GRID_W
    row = np.repeat(np.arange(rows), GRID_W).astype(f32)
    col = np.tile(np.arange(GRID_W), rows).astype(f32)
    d = QK_ROPE_DIM // 2
    inv = (f32(1.0) / np.power(f32(ROPE_THETA), np.arange(0, d, 2, dtype=f32) / f32(d))).astype(f32)
    ang = np.concatenate([row[:, None] * inv[None, :], col[:, None] * inv[None, :]], axis=1).astype(f32)
    cos = np.concatenate([np.ones((n_ctx, d), f32), np.cos(ang)], axis=0)
    sin = np.concatenate([np.zeros((n_ctx, d), f32), np.sin(ang)], axis=0)
    q = QK_ROPE_DIM // 4
    T = n_ctx + n_lat
    ones, zeros = np.ones((T, QK_NOPE_DIM), f32), np.zeros((T, QK_NOPE_DIM), f32)
    tail, z8 = np.zeros((T, HEAD_LANES - QK_DIM), f32), np.zeros((T, q), f32)
    cr, cc, sr, sc = cos[:, :q], cos[:, q:], sin[:, :q], sin[:, q:]
    cos_t = np.concatenate([ones, cr, cr, cc, cc, tail], axis=1)
    sin_next = np.concatenate([zeros, -sr, z8, -sc, z8, tail], axis=1)
    sin_prev = np.concatenate([zeros, z8, sr, z8, sc, tail], axis=1)
    return tuple(jnp.asarray(t, F32) for t in (cos_t, sin_next, sin_prev))


def pad_heads(w, used):
    k = w.shape[0]
    return jnp.pad(w.reshape(k, MLA_HEADS, used), ((0, 0), (0, 0), (0, HEAD_LANES - used))).reshape(k, -1)


def unpad_heads(w, used):
    k = w.shape[0]
    return w.reshape(k, MLA_HEADS, HEAD_LANES)[:, :, :used].reshape(k, MLA_HEADS * used)


def rotary_spread():
    lane = np.arange(MLA_HEADS * HEAD_LANES) % HEAD_LANES
    return jnp.asarray(lane[None, :] == (QK_NOPE_DIM + np.arange(QK_ROPE_DIM))[:, None], BF16)


def s5_discretise(a_re, a_im, log_step, b_re, b_im):
    dt = jnp.exp(log_step)[:, None]
    mag = jnp.exp(a_re * dt)
    lb_re = mag * jnp.cos(a_im * dt)
    lb_im = mag * jnp.sin(a_im * dt)
    den = a_re * a_re + a_im * a_im
    nr = lb_re - 1.0
    f_re = ((nr * a_re + lb_im * a_im) / den)[:, None, :]
    f_im = ((lb_im * a_re - nr * a_im) / den)[:, None, :]
    return lb_re, lb_im, f_re * b_re - f_im * b_im, f_re * b_im + f_im * b_re


def s5_block_weights(lb_re, lb_im, bb_re, bb_im, c_re, c_im):
    eye = jnp.eye(GROUPS_PER_BLOCK, dtype=F32)
    lam = jnp.stack([lb_re.reshape(1, S5_LANES), lb_im.reshape(1, S5_LANES)])

    def b_blocks(bb):
        t = bb.reshape(N_BLOCKS, GROUPS_PER_BLOCK, S5_GROUP, S5_STATE)
        return jnp.einsum("bgcp,gh->bgchp", t, eye).reshape(N_BLOCKS, BLK_CH, BLK_ST).astype(BF16)

    def c_blocks(cc):
        t = cc.reshape(N_BLOCKS, GROUPS_PER_BLOCK, S5_GROUP, S5_STATE)
        return jnp.einsum("bgcp,gh->bgphc", t, eye).reshape(N_BLOCKS, BLK_ST, BLK_CH).astype(BF16)

    return lam, b_blocks(bb_re), b_blocks(bb_im), c_blocks(c_re), c_blocks(c_im)


def b_block_diag(db):
    t = db.reshape(N_BLOCKS, GROUPS_PER_BLOCK, S5_GROUP, GROUPS_PER_BLOCK, S5_STATE)
    return jnp.einsum("bgchp,gh->bgcp", t, jnp.eye(GROUPS_PER_BLOCK, dtype=F32)).reshape(S5_GROUPS, S5_GROUP, S5_STATE)


def c_block_diag(dc):
    t = dc.reshape(N_BLOCKS, GROUPS_PER_BLOCK, S5_STATE, GROUPS_PER_BLOCK, S5_GROUP)
    return jnp.einsum("bgphc,gh->bgcp", t, jnp.eye(GROUPS_PER_BLOCK, dtype=F32)).reshape(S5_GROUPS, S5_GROUP, S5_STATE)


def conj(a):
    return jnp.stack([a[0], -a[1]])


def _narrow(shape):
    return len(shape) >= 2 and shape[-1] < min(HEAD_LANES, shape[-2])


def _stored(a):
    return jnp.swapaxes(a, -1, -2) if _narrow(a.shape) else a


def _stored_shape(shape):
    return tuple(shape[:-2]) + (shape[-1], shape[-2]) if _narrow(shape) else tuple(shape)


def _from_stored(a, shape):
    return jnp.swapaxes(a, -1, -2) if _narrow(shape) else a


PACK_TILE = 16 * 128


def pack_flat(parts, dtype, multiple=PACK_TILE):
    flat = [p.reshape(-1).astype(dtype) for p in parts]
    sizes = [f.shape[0] for f in flat]
    total = sum(sizes)
    pad = (-total) % multiple
    if pad:
        flat.append(jnp.zeros((pad,), dtype))
    offs = np.cumsum([0] + sizes)[:-1].tolist()
    return jnp.concatenate(flat).reshape(-1, 128), offs


def unpack_flat(buf, offs, shapes):
    flat = buf.reshape(-1)
    return [flat[o:o + int(np.prod(s))].reshape(s) for o, s in zip(offs, shapes)]


def s5_forward(p1, n_ctx, dirs):
    saved = []
    y = None
    ctx_rows, lat_rows = (0, n_ctx), (n_ctx, p1.shape[0] - n_ctx)
    zeros_tile = jnp.zeros((2, N_SEG, S5_LANES), F32)
    zeros_row = jnp.zeros((2, 1, S5_LANES), F32)
    for k, (lam, b_re, b_im, c_re, c_im) in enumerate(dirs):
        rev = k == 1
        last = 0 if rev else N_SEG - 1
        _, _, fin = s5_scan(p1, b_re, b_im, lam, zeros_tile, reverse=rev, rows=ctx_rows, name=f"s5_ctx_finals{k}")
        carry_c = s5_chain(fin, zeros_row, lam, n_ctx // N_SEG, rev, name=f"s5_ctx_chain{k}")
        _, ck_c, fin_c = s5_scan(p1, b_re, b_im, lam, carry_c, reverse=rev, want_ckpt=True, rows=ctx_rows,
                                 name=f"s5_ctx_scan{k}")
        s0 = fin_c[:, last:last + 1, :]
        _, _, fin = s5_scan(p1, b_re, b_im, lam, zeros_tile, reverse=rev, rows=lat_rows, name=f"s5_lat_finals{k}")
        carry_l = s5_chain(fin, s0, lam, lat_rows[1] // N_SEG, rev, name=f"s5_lat_chain{k}")
        y, ck_l, _ = s5_scan(p1, b_re, b_im, lam, carry_l, reverse=rev, c_re=c_re, c_im=c_im, add=y,
                             want_ckpt=True, rows=lat_rows, name=f"s5_lat_scan{k}")
        saved.append((ck_c, ck_l))
    return y, saved


def s5_backward(dy_l, du_extra_l, p1, n_ctx, dirs, saved):
    n_lat = p1.shape[0] - n_ctx
    zeros_tile = jnp.zeros((2, N_SEG, S5_LANES), F32)
    zeros_row = jnp.zeros((2, 1, S5_LANES), F32)
    du_l, du_c = du_extra_l, None
    grads = []
    for k, (lam, b_re, b_im, c_re, c_im) in enumerate(dirs):
        rev = k == 1
        lam_c = conj(lam)
        ck_c, ck_l = saved[k]
        first = N_SEG - 1 if rev else 0
        _, _, fin = s5_scan(dy_l, c_re, c_im, lam_c, zeros_tile, reverse=not rev, adjoint=True,
                            name=f"s5_lat_adj_finals{k}")
        carry = s5_chain(fin, zeros_row, lam_c, n_lat // N_SEG, not rev, name=f"s5_lat_adj_chain{k}")
        whole = k == len(dirs) - 1
        du_l, dlam_l, dbr_l, dbi_l, dcr_l, dci_l, fin_a = s5_grads(
            dy_l, p1, ck_l, b_re, b_im, c_re, c_im, lam, carry, reverse=rev, add=du_l, u_off=n_ctx,
            du_rows=(p1.shape[0], n_ctx) if whole else None, name=f"s5_lat_grads{k}")
        g0 = fin_a[:, first:first + 1, :]
        carry = s5_chain(zeros_tile, g0, lam_c, n_ctx // N_SEG, not rev, name=f"s5_ctx_adj_chain{k}")
        du_c, dlam_c, dbr_c, dbi_c, _, _, _ = s5_grads(
            None, p1, ck_c, b_re, b_im, c_re, c_im, lam, carry, reverse=rev, add=du_c, n_rows=n_ctx,
            du_rows=(p1.shape[0], 0) if whole else None, du_into=du_l if whole else None, name=f"s5_ctx_grads{k}")
        dlam = jnp.sum(dlam_l + dlam_c, axis=1)
        grads.append((dlam, b_block_diag(dbr_l + dbr_c), b_block_diag(dbi_l + dbi_c),
                      c_block_diag(dcr_l), c_block_diag(dci_l)))
    return du_c, grads


def local_step(x, ctx, target, mod, w, late=None, reducer=None):
    L, Lc = x.shape[0], ctx.shape[0]
    T = L + Lc
    assert L % Lc == 0 and Lc % (2 * N_SEG) == 0 and L % GRID_W == 0
    D = D_MODEL
    X0 = (ctx, x)

    def mod_of(i, j):
        return mod[i, :, j, :][:, None, :]

    def vec(v):
        return v.reshape(1, 1, -1).astype(F32)

    g0 = vec(w["norm_g"][0])
    H0, p0 = norm_proj(X0, g0, mod_of(0, 1), mod_of(0, 0), w["mla_w_in"], Lc, "l0_norm_in")
    cq = Rows(p0, Q_LORA_RANK, col_blk=D // Q_LORA_RANK)
    ckv = Rows(p0, KV_LORA_RANK, col_blk=(D + Q_LORA_RANK) // KV_LORA_RANK)
    kr = Rows(p0, HEAD_LANES, col_blk=(D + Q_LORA_RANK + KV_LORA_RANK) // HEAD_LANES)
    qng, kvng = vec(w["mla_q_norm"]), vec(w["mla_kv_norm"])
    tabs = rope_tables(Lc, L)
    spread = jnp.pad(rotary_spread(), ((0, HEAD_LANES - QK_ROPE_DIM), (0, 0)))
    if late is not None:
        w = {**w, **late["qkv"](p0)}
    w_uq_p = pad_heads(w["mla_w_uq"], QK_DIM)
    w_ukv3 = w["mla_w_ukv"].reshape(KV_LORA_RANK, MLA_HEADS, QK_NOPE_DIM + V_HEAD_DIM)
    w_kn_p = pad_heads(w_ukv3[:, :, :QK_NOPE_DIM].reshape(KV_LORA_RANK, -1), QK_NOPE_DIM)
    w_v = w_ukv3[:, :, QK_NOPE_DIM:].reshape(KV_LORA_RANK, -1)
    qb, qn = q_heads(cq, qng, w_uq_p, tabs)
    kb, vb, kvn = kv_heads(ckv, kvng, w_kn_p, w_v, kr, spread, tabs)
    o, lse, qs = attn_fwd(qb, kb, vb, Lc)
    if late is not None:
        w = {**w, **late["out"](o)}
    X1, og, out0 = mla_post_fwd(o, p0, X0, mod_of(0, 2), w["mla_w_out"], Lc)

    if late is not None:
        w = {**w, **late["l1"](X1)}
    X1p = rows_to_segments(X1, Lc, "l1_to_segments")
    tgt_p = to_segments(target, 0)
    g1 = vec(w["norm_g"][1])
    H1, p1 = norm_proj(X1p, g1, mod_of(1, 1), mod_of(1, 0), w["s5_w_in"], Lc, "l1_norm_in")
    disc_fn = lambda *a: tuple(zip(*[s5_discretise(a[0][k], a[1][k], a[2][k], a[3][k], a[4][k]) for k in range(2)]))
    disc, disc_vjp = jax.vjp(disc_fn, w["s5_a_re"], w["s5_a_im"], w["s5_log_step"], _stored(w["s5_b_re"]),
                             _stored(w["s5_b_im"]))
    dirs = [s5_block_weights(disc[0][k], disc[1][k], disc[2][k], disc[3][k], w["s5_c_re"][k], w["s5_c_im"][k])
            for k in range(2)]
    y_ssm, s5_saved = s5_forward(p1, Lc, dirs)

    row = lambda v: v.reshape(1, D).astype(F32)
    (lvec, dX2, d_yssm, d_u_act, d_z1, d_fg, d_gt1, d_bg, d_d, gw_glu, gw_out) = s5_tail(
        y_ssm, p1, X1p, tgt_p, Lc, row(w["s5_d"]), row(w["s5_b_glu"]), mod[1, 1:2, 2, :], row(w["final_g"]),
        w["s5_w_glu"], w["s5_w_out"])
    loss = jnp.sum(lvec)
    gw = {"final_g": d_fg.reshape(D), "s5_b_glu": d_bg.reshape(D), "s5_d": d_d.reshape(D),
          "s5_w_glu": gw_glu, "s5_w_out": gw_out}
    dmod = {}

    du_p, s5_g = s5_backward(d_yssm, d_u_act, p1, Lc, dirs, s5_saved)
    d_disc = tuple(tuple(s5_g[k][j - 1].reshape(disc[j][k].shape) if j >= 2 else
                         s5_g[k][0][j].reshape(disc[j][k].shape) for k in range(2)) for j in range(4))
    gw["s5_a_re"], gw["s5_a_im"], gw["s5_log_step"], d_bt_re, d_bt_im = disc_vjp(d_disc)
    gw["s5_b_re"], gw["s5_b_im"] = jnp.swapaxes(d_bt_re, -1, -2), jnp.swapaxes(d_bt_im, -1, -2)
    gw["s5_c_re"] = jnp.stack([s5_g[0][3], s5_g[1][3]])
    gw["s5_c_im"] = jnp.stack([s5_g[0][4], s5_g[1][4]])
    gw["s5_w_in"] = mm_tn(H1, du_p, name="l1_in_dw", b_more=d_z1)
    if reducer is not None:
        g1 = g1 + reducer["l1"][0]({n: gw.pop(n) for n in LAYER1_MATS})[0, 0]
    d_X1, d_g1, d_sc1, d_sh1 = norm_proj_bwd([(du_p, 0, 0), (d_z1, D, Lc)], w["s5_w_in"], X1p, g1, mod_of(1, 1),
                                             mod_of(1, 0), dX2, Lc, "l1_norm_in_bwd", dx_from_segments=True)
    d_gt1_full = jnp.concatenate([jnp.zeros((1, 1, D), F32), d_gt1[None]], axis=0)
    dmod[1] = (d_sh1, d_sc1, d_gt1_full)

    d_o, d_z0, d_gt0, gw["mla_w_out"] = mla_post_bwd(d_X1, out0, og, o, p0, mod_of(0, 2), w["mla_w_out"], Lc)
    if reducer is not None:
        started = reducer["l1"][1](d_o)[0, 0] + reducer["out"][0]({"mla_w_out": gw.pop("mla_w_out")})[0, 0]
        tabs = (tabs[0] + started,) + tabs[1:]
    d_q, dos = attn_bwd_dq(qb, kb, vb, o, d_o, lse, tabs, Lc)
    dk_p, d_v = attn_bwd_dkv(qs, kb, vb, dos, Lc)
    if reducer is not None:
        tabs = (tabs[0] + reducer["out"][1](d_q)[0, 0],) + tabs[1:]
    d_k, d_kr = heads_unrope(dk_p, tabs, spread,
                             name="l0_k_unrope")
    d_qn = mm_nt(d_q, w_uq_p, name="l0_uq_dx")
    gw["mla_w_uq"] = unpad_heads(mm_tn(qn, d_q, name="l0_uq_dw"), QK_DIM)
    d_kvn = mm_nt(d_k, w_kn_p, name="l0_ukn_dx") + mm_nt(d_v, w_v, name="l0_uv_dx")
    dw_kn = unpad_heads(mm_tn(kvn, d_k, name="l0_ukn_dw"), QK_NOPE_DIM).reshape(KV_LORA_RANK, MLA_HEADS, QK_NOPE_DIM)
    dw_v = mm_tn(kvn, d_v, name="l0_uv_dw").reshape(KV_LORA_RANK, MLA_HEADS, V_HEAD_DIM)
    gw["mla_w_ukv"] = jnp.concatenate([dw_kn, dw_v], axis=-1).reshape(KV_LORA_RANK, -1)
    d_cq, d_qng = rowwise_bwd(f_rms, [cq], [qng], [d_qn], [0], [0], T, 0, "l0_qnorm_bwd")
    d_ckv, d_kvng = rowwise_bwd(f_rms, [ckv], [kvng], [d_kvn], [0], [0], T, 0, "l0_kvnorm_bwd")
    gw["mla_q_norm"] = d_qng.reshape(-1)
    gw["mla_kv_norm"] = d_kvng.reshape(-1)
    o_cq, o_ckv = D, D + Q_LORA_RANK
    o_kr = o_ckv + KV_LORA_RANK
    d_head = jnp.concatenate([d_cq, d_ckv, d_kr], axis=1)
    gw["mla_w_in"] = jnp.concatenate([mm_tn(H0, d_head, name="l0_in_dw_head")[:, :P0_HEAD],
                                      mm_tn(H0, d_z0, name="l0_in_dw_z")], axis=1)
    dx, d_g0, d_sc0, d_sh0 = norm_proj_bwd(
        [(d_z0, 0, 0), (d_cq, o_cq, 0), (d_ckv, o_ckv, 0), (d_kr, o_kr, 0)], w["mla_w_in"], X0, g0, mod_of(0, 1),
        mod_of(0, 0), d_X1, Lc, "l0_norm_in_bwd", latent_dx_only=True)
    dmod[0] = (d_sh0, d_sc0, d_gt0)
    gw["norm_g"] = jnp.stack([d_g0.reshape(D), d_g1.reshape(D)])
    dmod_arr = jnp.stack([jnp.stack([dmod[i][j][:, 0, :] for j in range(3)], axis=1) for i in range(2)])
    ready = {**reducer["l1"][2](dx), **reducer["out"][2](dx)} if reducer is not None else {}
    return loss, dx, dmod_arr, gw, ready


SHARDED = {
    "mla_w_in": 1, "mla_w_uq": 1, "mla_w_ukv": 1, "mla_w_out": 0,
    "s5_w_in": 1, "s5_w_glu": 0, "s5_w_out": 0, "s5_d": 0, "s5_b_glu": 0,
}
SHARDED_MATS = ["mla_w_in", "mla_w_uq", "mla_w_ukv", "mla_w_out", "s5_w_in", "s5_w_glu", "s5_w_out"]
SHARDED_VECS = ["s5_d", "s5_b_glu"]
REPLICATED = ["norm_g", "mla_q_norm", "mla_kv_norm", "s5_a_re", "s5_a_im", "s5_log_step", "s5_b_re", "s5_b_im",
              "s5_c_re", "s5_c_im", "final_g"]
WEIGHT_ORDER = ["c_ctx", "ada_w", "ada_b", "norm_g", "mla_w_in", "mla_q_norm", "mla_w_uq", "mla_kv_norm", "mla_w_ukv",
                "mla_w_out", "s5_w_in", "s5_a_re", "s5_a_im", "s5_log_step", "s5_b_re", "s5_b_im", "s5_c_re", "s5_c_im",
                "s5_d", "s5_w_glu", "s5_b_glu", "s5_w_out", "final_g"]


P0_HEAD = Q_LORA_RANK + KV_LORA_RANK + QK_ROPE_DIM


P0_WIDTH = 1536


def w_in_to_kernel_order(w):
    pad = jnp.zeros((w.shape[0], P0_WIDTH - w.shape[1]), w.dtype)
    return jnp.concatenate([w[:, P0_HEAD:], w[:, :P0_HEAD], pad], axis=1)


LAYER0_MATS = ["mla_w_in", "mla_w_uq", "mla_w_ukv", "mla_w_out"]
LAYER1_MATS = ["s5_w_in", "s5_w_glu", "s5_w_out"]


def _whole_matrices(names, own_blocks, gathered):
    chip = _chip_index(_coords())
    full = {}
    for n, own, o in zip(names, own_blocks, gathered):
        slot = lax.broadcasted_iota(jnp.int32, (N_CHIP, 1, 1), 0)
        o = jnp.where(slot == chip, own[None], o.reshape((N_CHIP,) + own.shape))
        full[n] = o.reshape(-1, o.shape[-1]) if SHARDED[n] == 0 else o.transpose(1, 0, 2).reshape(o.shape[1], -1)
    return full


FIRST_MATS = ["mla_w_in"]
LATER_GROUPS = {"qkv": ["mla_w_uq", "mla_w_ukv"], "out": ["mla_w_out"], "l1": LAYER1_MATS}


def gather_weights(ws):
    mats = [ws[n].astype(BF16) for n in FIRST_MATS]
    full = _whole_matrices(FIRST_MATS, mats, gather_halves(mats, "gather_weights"))
    full["mla_w_in"] = w_in_to_kernel_order(full["mla_w_in"])
    return full


def gather_weights_behind(ws, after):
    token, finish = 0.0, {}
    for group, names in LATER_GROUPS.items():
        mats = [ws[n].astype(BF16) for n in names]
        flight, tok = exchange_start(
            mats, [jax.ShapeDtypeStruct((N_CHIP,) + m.shape, m.dtype) for m in mats],
            [(f, a, lambda me, peer: None, a, lambda s: (_chip_index(s),))
             for a in range(len(mats)) for f in CHIP_FLIPS], f"gather_{group}_start", after=after)
        after = [tok]
        token = token + tok[0, 0]

        def finish_group(after_work, group=group, names=names, flight=flight):
            own, got = exchange_wait(flight, after_work, f"gather_{group}_wait")
            return _whole_matrices(names, own, got)

        finish[group] = finish_group
    return token, finish


def _grad_slots(gw, names):
    slots = []
    for n in names:
        g = gw[n]
        if SHARDED[n] == 0:
            slots.append(g.reshape(N_CHIP, 2, g.shape[0] // (2 * N_CHIP), g.shape[1]))
        else:
            k, n4 = g.shape
            slots.append(g.reshape(k, N_CHIP, n4 // N_CHIP).transpose(1, 0, 2)
                         .reshape(N_CHIP, 2, k // 2, n4 // N_CHIP))
    return slots


def _to_sibling_half(count):
    return [(CORE_FLIP, i, lambda me, peer: (slice(None), 1 - me[2]), i, lambda s: None) for i in range(count)]


def _to_chips(count):
    return [(f, i, lambda me, peer: (_chip_index(peer),), i, lambda s: (_chip_index(s),))
            for i in range(count) for f in CHIP_FLIPS]


def _place():
    me = _coords()
    return jnp.stack([me[2], _chip_index(me)]).astype(jnp.int32)


def reduce_behind(names, tag):
    state = {}
    count = len(names)

    def begin(gw):
        slots = _grad_slots(gw, names)
        lands = [jax.ShapeDtypeStruct((N_CHIP,) + s.shape[2:], F32) for s in slots]
        state["in"], token = exchange_start(slots, lands, _to_sibling_half(count), f"grads_{tag}_swap_in_start")
        return token

    def middle(after):
        slots, got = exchange_wait(state["in"], after, f"grads_{tag}_swap_in_wait")
        place = _place()
        sums = [pair_add(s, g, place[:1], BF16, f"grads_pair_{n}") for n, s, g in zip(names, slots, got)]
        lands = [jax.ShapeDtypeStruct(s.shape, s.dtype) for s in sums]
        state["out"], token = exchange_start(sums, lands, _to_chips(count), f"grads_{tag}_scatter_start")
        return token

    def end(after):
        sums, parts = exchange_wait(state["out"], after, f"grads_{tag}_scatter_wait")
        place = _place()
        return {n: sum_chips(p, s, place, f"grads_sum_{n}") for n, p, s in zip(names, parts, sums)}

    return begin, middle, end


def reduce_gradients(gw, ready_halves, also=None):
    me = _coords()
    place = _place()
    mat_names = [n for n in SHARDED_MATS if n not in ready_halves]
    slots = dict(zip(mat_names, _grad_slots(gw, mat_names)))
    gw = {**gw, **(also or {})}
    small_names = REPLICATED + SHARDED_VECS + list(also or {})
    small, small_offs = pack_flat([_stored(gw[n]).astype(F32) for n in small_names], F32, N_CHIP * 32 * 128)
    slots["small"] = small.reshape(N_CHIP, 2, -1, 128)
    names = list(slots)
    count = len(names)
    got = exchange([slots[n] for n in names],
                   [jax.ShapeDtypeStruct((N_CHIP,) + slots[n].shape[2:], F32) for n in names],
                   _to_sibling_half(count), [], "grads_swap_in")
    sums = [pair_add(slots[n], g, place[:1], F32 if n == "small" else BF16, f"grads_pair_{n}")
            for n, g in zip(names, got)]
    parts = exchange(sums, [jax.ShapeDtypeStruct(s.shape, s.dtype) for s in sums], _to_chips(count), [],
                     "grads_scatter")
    halves = {n: sum_chips(p, s, place, f"grads_sum_{n}") for n, p, s in zip(names, parts, sums)}
    halves.update(ready_halves)
    all_names = list(halves)
    fulls = exchange(
        [halves[n] for n in all_names], [jax.ShapeDtypeStruct(halves[n].shape, F32) for n in all_names],
        [(CORE_FLIP, i, lambda me, peer: (me[2],), i, lambda s: (s[2],)) for i in range(len(all_names))], [],
        "grads_swap_out", aliases={i: i for i in range(len(all_names))})
    out = {n: f.reshape(-1, f.shape[-1]) for n, f in zip(all_names, fulls)}
    quarter = out.pop("small")
    gather, token = exchange_start(
        [quarter], [jax.ShapeDtypeStruct((N_CHIP,) + quarter.shape, F32)],
        [(f, 0, lambda me, peer: None, 0, lambda s: (_chip_index(s),)) for f in CHIP_FLIPS],
        "grads_gather_small_start")

    def finish_small(after):
        (own,), (got_small,) = exchange_wait(gather, after, "grads_gather_small_wait")
        slot = lax.broadcasted_iota(jnp.int32, (N_CHIP, 1, 1), 0)
        small_all = jnp.where(slot == _chip_index(me), own[None], got_small)
        vals = unpack_flat(small_all, small_offs, [_stored_shape(gw[n].shape) for n in small_names])
        res = {}
        for n, v in zip(small_names, vals):
            v = _from_stored(v, gw[n].shape)
            if n in SHARDED_VECS:
                size = v.shape[0] // N_CHIP
                v = lax.dynamic_slice_in_dim(v, _chip_index(me) * size, size)
            res[n] = v
        return res

    return out, finish_small, token


def kernel(x, c, ctx, c_ctx, ada_w, ada_b, norm_g, mla_w_in, mla_q_norm, mla_w_uq, mla_kv_norm, mla_w_ukv, mla_w_out, s5_w_in, s5_a_re, s5_a_im, s5_log_step, s5_b_re, s5_b_im, s5_c_re, s5_c_im, s5_d, s5_w_glu, s5_b_glu, s5_w_out, final_g, loss_target, m_c_ctx, m_ada_w, m_ada_b, m_norm_g, m_mla_w_in, m_mla_q_norm, m_mla_w_uq, m_mla_kv_norm, m_mla_w_ukv, m_mla_w_out, m_s5_w_in, m_s5_a_re, m_s5_a_im, m_s5_log_step, m_s5_b_re, m_s5_b_im, m_s5_c_re, m_s5_c_im, m_s5_d, m_s5_w_glu, m_s5_b_glu, m_s5_w_out, m_final_g, v_c_ctx, v_ada_w, v_ada_b, v_norm_g, v_mla_w_in, v_mla_q_norm, v_mla_w_uq, v_mla_kv_norm, v_mla_w_ukv, v_mla_w_out, v_s5_w_in, v_s5_a_re, v_s5_a_im, v_s5_log_step, v_s5_b_re, v_s5_b_im, v_s5_c_re, v_s5_c_im, v_s5_d, v_s5_w_glu, v_s5_b_glu, v_s5_w_out, v_final_g):
    args = dict(locals())
    weights = {n: args[n] for n in WEIGHT_ORDER}
    D = D_MODEL
    xi, yi, ci = _coords()
    chip = 2 * xi + yi
    me = 4 * xi + 2 * yi + ci
    n_col = ada_w.shape[2]

    c_all = allgather_devices(jnp.pad(c, ((0, 7), (0, 0))), "gather_c")[:, 0, :]
    cond = jnp.concatenate([c_all, jnp.broadcast_to(c_ctx[None], (8, D))], axis=0)
    (s_cond,) = rowwise_fwd(lambda v: (_silu(v),), [cond], [], [D], [F32], 16, 0, "cond_silu")
    ada_rows = ada_w.reshape(2 * D, n_col)
    mod_cols = jnp.stack([mm_nn(s_cond, ada_rows, name=f"mod_proj{i}", b_blk=i) for i in range(2)])
    vec_tiles = [jnp.pad(weights[n][0].reshape(-1, 128), ((0, 6), (0, 0))) for n in SHARDED_VECS]
    mod_all, *vec_all = allgather_chips([mod_cols] + vec_tiles, "gather_mod")
    mod_all = mod_all.transpose(1, 2, 0, 3).reshape(2, 16, 3 * D) + ada_b[:, None, :]
    mine = lax.broadcasted_iota(jnp.int32, (1, 16, 1), 1) == me
    mod_l = jnp.sum(jnp.where(mine, mod_all, 0.0), axis=1)
    mod_c = mod_all[:, 8, :]
    mod = jnp.stack([mod_c.reshape(2, 3, D), mod_l.reshape(2, 3, D)], axis=1)

    w = gather_weights({n: weights[n][0] for n in FIRST_MATS})
    token, late = gather_weights_behind({n: weights[n][0] for n in SHARDED_MATS}, [w["mla_w_in"], mod])
    for n, v in zip(SHARDED_VECS, vec_all):
        w[n] = v[:, :2, :].reshape(-1)
    for n in ["norm_g", "final_g"]:
        w[n] = weights[n]
    for n in ["mla_q_norm", "mla_kv_norm", "s5_a_re", "s5_a_im", "s5_log_step", "s5_b_re", "s5_b_im",
              "s5_c_re", "s5_c_im"]:
        w[n] = weights[n][0]

    reducer = {"l1": reduce_behind(LAYER1_MATS, "l1"), "out": reduce_behind(["mla_w_out"], "out")}
    loss_me, dx, dmod, gw, ready = local_step(x[0], ctx[0], loss_target[0], mod + token, w, late,
                                              reducer)

    dmod_rows, loss_all = _gather([dmod.reshape(2, 2, 3 * D), jnp.broadcast_to(loss_me, (8, 128))],
                                  ALL_FLIPS, _dev_index, N_DEV, "gather_dmod")
    loss = functools.reduce(lambda s, d: s + loss_all[d, 0, 0], range(1, N_DEV), loss_all[0, 0, 0])
    dm = jnp.concatenate([dmod_rows[:, :, 1, :], dmod_rows[:, :, 0, :]], axis=0).transpose(1, 0, 2)
    g_ada_b = jnp.sum(dm, axis=1)
    dm_cols = lax.dynamic_slice_in_dim(dm, chip * n_col, n_col, axis=2)
    g_ada_w = jnp.stack([mm_tn(s_cond, dm_cols[i], name=f"mod_proj_dw{i}") for i in range(2)])
    dmc = jnp.sum(dm_cols[:, 8:, :], axis=1)
    dmc8 = jnp.broadcast_to(dmc[:, None, :], (2, 8, n_col))
    g_sc = (mm_nt(dmc8[0], ada_rows, name="mod_proj_dx0", b_rows=D, b_blk=0)[0]
            + mm_nt(dmc8[1], ada_rows, name="mod_proj_dx1", b_rows=D, b_blk=1)[0])
    g_silu_part = jnp.where(ci == 0, g_sc, 0.0)

    grads = {"ada_w": g_ada_w, "ada_b": g_ada_b}
    deltas, new_m, new_v = {}, {}, {}
    small = [n for n in WEIGHT_ORDER if weights[n].size < 50000]

    def update(n, after=None):
        shp = weights[n].shape
        rows = lambda a: _stored(a.reshape(shp)).reshape(-1, _stored_shape(shp)[-1])
        back = lambda a: _from_stored(a.reshape(_stored_shape(shp)), shp)
        d_, m_, v_ = adamw(rows(weights[n]), rows(grads[n]), rows(args["m_" + n]), rows(args["v_" + n]),
                           name=f"adamw_{n}", after=after)
        deltas[n], new_m[n], new_v[n] = back(d_), back(m_), back(v_)

    red, finish_small, small_started = reduce_gradients(gw, ready, {"silu_c_ctx": g_silu_part})
    update("ada_w", small_started)
    for n in SHARDED_MATS:
        grads[n] = red[n].reshape(weights[n].shape)
        update(n, small_started)
    red_small = finish_small([deltas[n] for n in ["ada_w"] + SHARDED_MATS])
    for n in REPLICATED + SHARDED_VECS:
        grads[n] = red_small[n].reshape(weights[n].shape)
    (g_c_ctx,) = rowwise_bwd(lambda v: (_silu(v),), [jnp.broadcast_to(c_ctx[None], (8, D))], [],
                             [jnp.broadcast_to(red_small["silu_c_ctx"][None], (8, D))], [0], [], 8, 0, "cond_silu_bwd")
    grads["c_ctx"] = g_c_ctx[0]
    for n in WEIGHT_ORDER:
        if n not in small and n not in deltas:
            update(n)
    packs = []
    offs = None
    for src in (weights, grads, {n: args["m_" + n] for n in small}, {n: args["v_" + n] for n in small}):
        buf, offs = pack_flat([src[n] for n in small], F32)
        packs.append(buf)
    outs = adamw(*packs, name="adamw_small")
    for res, dst in zip(outs, (deltas, new_m, new_v)):
        for n, val in zip(small, unpack_flat(res, offs, [weights[n].shape for n in small])):
            dst[n] = val

    return (loss, dx[None], *[grads[n] for n in WEIGHT_ORDER], *[deltas[n] for n in WEIGHT_ORDER],
            *[new_m[n] for n in WEIGHT_ORDER], *[new_v[n] for n in WEIGHT_ORDER])
```

```python
import functools
import math

import jax
import jax.numpy as jnp
import numpy as np
from jax import lax
from jax.experimental import pallas as pl
from jax.experimental.pallas import tpu as pltpu

F32 = jnp.float32
BF16 = jnp.bfloat16

D_MODEL = 1024
GRID_W = 64
EPS = 1e-6
MLA_HEADS = 16
QK_NOPE_DIM = 64
QK_ROPE_DIM = 32
V_HEAD_DIM = 64
Q_LORA_RANK = 256
KV_LORA_RANK = 128
QK_DIM = QK_NOPE_DIM + QK_ROPE_DIM
SOFTMAX_SCALE = QK_DIM ** -0.5
ROPE_THETA = 10000.0
S5_GROUP = 16
S5_GROUPS = D_MODEL // S5_GROUP
S5_STATE = 64
S5_LANES = S5_GROUPS * S5_STATE
N_SEG = 8
GROUPS_PER_BLOCK = 8
N_BLOCKS = S5_GROUPS // GROUPS_PER_BLOCK
BLK_CH = GROUPS_PER_BLOCK * S5_GROUP
BLK_ST = GROUPS_PER_BLOCK * S5_STATE

ADAM_LR = 0.001
ADAM_B1 = 0.9
ADAM_B2 = 0.999
ADAM_EPS = 1e-08
ADAM_WD = 0.01
ADAM_STEP = 10

N_DEV = 8
N_CHIP = 4
MESH = pl.DeviceIdType.MESH
VMEM_LIMIT = 52 * 1024 * 1024
ROW_TILE = 256


def _params(sem=None, vmem=None):
    return pltpu.CompilerParams(dimension_semantics=sem, vmem_limit_bytes=vmem)


def mm_nn(a, b, out_dtype=F32, name="mm_nn", b_blk=0):
    M, K = a.shape
    N = b.shape[1]
    tm = math.gcd(ROW_TILE, M)

    def body(a_ref, b_ref, o_ref):
        o_ref[...] = jnp.dot(a_ref[...].astype(BF16), b_ref[...].astype(BF16),
                             preferred_element_type=F32).astype(o_ref.dtype)

    return pl.pallas_call(
        body, out_shape=jax.ShapeDtypeStruct((M, N), out_dtype), grid=(M // tm,),
        in_specs=[pl.BlockSpec((tm, K), lambda i: (i, 0)), pl.BlockSpec((K, N), lambda i: (b_blk, 0))],
        out_specs=pl.BlockSpec((tm, N), lambda i: (i, 0)),
        compiler_params=_params(("parallel",), VMEM_LIMIT), name=name)(a, b)


def mm_nt(a, b, out_dtype=F32, name="mm_nt", b_rows=None, b_blk=0):
    M, N = a.shape
    K = b.shape[0] if b_rows is None else b_rows
    tm = math.gcd(ROW_TILE, M)

    def body(a_ref, b_ref, o_ref):
        o_ref[...] = lax.dot_general(a_ref[...].astype(BF16), b_ref[...].astype(BF16),
                                     (((1,), (1,)), ((), ())),
                                     preferred_element_type=F32).astype(o_ref.dtype)

    return pl.pallas_call(
        body, out_shape=jax.ShapeDtypeStruct((M, K), out_dtype), grid=(M // tm,),
        in_specs=[pl.BlockSpec((tm, N), lambda i: (i, 0)), pl.BlockSpec((K, N), lambda i: (b_blk, 0))],
        out_specs=pl.BlockSpec((tm, K), lambda i: (i, 0)),
        compiler_params=_params(("parallel",), VMEM_LIMIT), name=name)(a, b)


def mm_tn(a, b, name="mm_tn", b_more=None):
    M, N = b.shape
    K = a.shape[1]
    tn = math.gcd(512, N) if N % 128 == 0 and N > 512 else N
    bs = [b] if b_more is None else [b, b_more]
    assert all(x.shape[1] == N for x in bs)
    nb = N // tn

    def body(a_ref, *refs):
        o_ref = refs[-1]
        for k, b_ref in enumerate(refs[:-1]):
            def product(b_ref=b_ref):
                o_ref[...] = lax.dot_general(a_ref[a.shape[0] - b_ref.shape[0]:, :].astype(BF16),
                                             b_ref[...].astype(BF16), (((0,), (0,)), ((), ())),
                                             preferred_element_type=F32)
            if len(bs) == 1:
                product()
            else:
                pl.when(pl.program_id(0) // nb == k)(product)

    def b_spec(k, rows):
        return pl.BlockSpec((rows, tn), lambda j: (0, jnp.clip(j - k * nb, 0, nb - 1)))

    return pl.pallas_call(
        body, out_shape=jax.ShapeDtypeStruct((K, N * len(bs)), F32), grid=(nb * len(bs),),
        in_specs=[pl.BlockSpec(a.shape, lambda j: (0, 0))] + [b_spec(k, x.shape[0]) for k, x in enumerate(bs)],
        out_specs=pl.BlockSpec((K, tn), lambda j: (0, j)),
        compiler_params=_params(("parallel",), VMEM_LIMIT), name=name)(a, *bs)


class Rows:
    def __init__(self, arr, width=None, row_off=0, col_blk=0):
        self.arr = arr
        self.width = arr.shape[1] if width is None else width
        self.row_off = row_off
        self.col_blk = col_blk

    def spec(self, tm):
        ro, cb = self.row_off // tm, self.col_blk
        return pl.BlockSpec((tm, self.width), lambda i: (i + ro, cb))


def _as_rows(x):
    return x if isinstance(x, Rows) else Rows(x)


def _row_tile(n_rows, n_ctx_rows, rows):
    tm = math.gcd(ROW_TILE, n_rows, n_ctx_rows)
    for r in rows:
        tm = math.gcd(tm, r.row_off)
    return tm


def _bc_spec(arr, n_ctx_blocks):
    g, _, d = arr.shape
    if g == 1:
        return pl.BlockSpec((1, 1, d), lambda i: (0, 0, 0))
    return pl.BlockSpec((1, 1, d), lambda i: ((i >= n_ctx_blocks).astype(jnp.int32), 0, 0))


def rowwise_fwd(fn, rows, bcs, out_dims, out_dtypes, n_rows, n_ctx_rows, name):
    rows = [_as_rows(r) for r in rows]
    tm = _row_tile(n_rows, n_ctx_rows, rows)
    ncb = n_ctx_rows // tm
    nr, nb = len(rows), len(bcs)

    def body(*refs):
        vals = [r[...].astype(F32) for r in refs[:nr]] + [b[0].astype(F32) for b in refs[nr:nr + nb]]
        outs = fn(*vals)
        for o_ref, v in zip(refs[nr + nb:], outs):
            o_ref[...] = v.astype(o_ref.dtype)

    outs = pl.pallas_call(
        body,
        out_shape=[jax.ShapeDtypeStruct((n_rows, d), dt) for d, dt in zip(out_dims, out_dtypes)],
        grid=(n_rows // tm,),
        in_specs=[r.spec(tm) for r in rows] + [_bc_spec(b, ncb) for b in bcs],
        out_specs=[pl.BlockSpec((tm, d), lambda i: (i, 0)) for d in out_dims],
        compiler_params=_params(("parallel",), VMEM_LIMIT), name=name)(*[r.arr for r in rows], *bcs)
    return outs


def rowwise_bwd(fn, rows, bcs, cts, diff_rows, diff_bcs, n_rows, n_ctx_rows, name, ct_extra=None, lat_add=None):
    rows = [_as_rows(r) for r in rows]
    cts = [_as_rows(c) for c in cts]
    extra = [_as_rows(ct_extra)] if ct_extra is not None else []
    tm = _row_tile(n_rows, n_ctx_rows, rows + cts + extra)
    ncb = n_ctx_rows // tm
    nr, nb, nc = len(rows), len(bcs), len(cts)
    ndr, ndb = len(diff_rows), len(diff_bcs)
    n_in = nr + nb + nc + len(extra) + (lat_add is not None)

    def body(*refs):
        i = pl.program_id(0)
        rvals = [r[...].astype(F32) for r in refs[:nr]]
        bvals = [b[0].astype(F32) for b in refs[nr:nr + nb]]
        cvals = [c[...].astype(F32) for c in refs[nr + nb:nr + nb + nc]]
        if extra:
            cvals[0] = cvals[0] + refs[nr + nb + nc][...].astype(F32)
        outs = refs[n_in:]

        def f(*d):
            rv, bv = list(rvals), list(bvals)
            for k, idx in enumerate(diff_rows):
                rv[idx] = d[k]
            for k, idx in enumerate(diff_bcs):
                bv[idx] = d[ndr + k]
            return tuple(fn(*rv, *bv))

        primals = [rvals[k] for k in diff_rows] + [bvals[k] for k in diff_bcs]
        _, vjp = jax.vjp(f, *primals)
        grads = list(vjp(tuple(cvals)))
        if lat_add is not None:
            add = refs[n_in - 1][...]
            grads[0] = grads[0] + (add if lat_add.shape[0] == n_rows else jnp.where(i >= ncb, add, 0.0))
        for k in range(ndr):
            outs[k][...] = grads[k].astype(outs[k].dtype)
        for k, idx in enumerate(diff_bcs):
            o_ref = outs[ndr + k]
            first = (i == 0)
            if bcs[idx].shape[0] == 2:
                first = first | (i == ncb)

            @pl.when(first)
            def _(o_ref=o_ref):
                o_ref[...] = jnp.zeros_like(o_ref)

            o_ref[0] += grads[ndr + k]

    out_shape = [jax.ShapeDtypeStruct((n_rows, rows[k].width), F32) for k in diff_rows]
    out_shape += [jax.ShapeDtypeStruct(bcs[k].shape, F32) for k in diff_bcs]
    out_specs = [pl.BlockSpec((tm, rows[k].width), lambda i: (i, 0)) for k in diff_rows]
    out_specs += [_bc_spec(bcs[k], ncb) for k in diff_bcs]
    ins = [r.arr for r in rows] + list(bcs) + [c.arr for c in cts + extra]
    in_specs = [r.spec(tm) for r in rows] + [_bc_spec(b, ncb) for b in bcs] + [c.spec(tm) for c in cts + extra]
    if lat_add is not None:
        ins.append(lat_add)
        skip = ncb if lat_add.shape[0] != n_rows else 0
        in_specs.append(pl.BlockSpec((tm, lat_add.shape[1]), lambda i: (jnp.maximum(i - skip, 0), 0)))
    outs = pl.pallas_call(
        body, out_shape=out_shape, grid=(n_rows // tm,), in_specs=in_specs, out_specs=out_specs,
        compiler_params=_params(("arbitrary",), VMEM_LIMIT), name=name)(*ins)
    return outs


def _rms(x):
    return x * lax.rsqrt(jnp.mean(x * x, axis=-1, keepdims=True) + EPS)


def _sigmoid(x):
    return 0.5 * (jnp.tanh(0.5 * x) + 1.0)


def _silu(x):
    return x * _sigmoid(x)


def _gelu_tanh(x):
    return 0.5 * x * (1.0 + jnp.tanh(math.sqrt(2.0 / math.pi) * (x + 0.044715 * (x * x * x))))


def f_norm_mod(x, g, sc, sh):
    return ((_rms(x) * g) * (1.0 + sc) + sh,)


def f_rms(x, g):
    return (_rms(x) * g,)


def f_gate(o, z):
    return (o * _silu(z),)


def f_s5_act(y, u, d):
    return (_gelu_tanh(y + d * u),)


def f_s5_glu(ya, gl, z, b):
    return (ya * _sigmoid(gl + b) * _silu(z),)


def _as_parts(x, n_ctx):
    xs = x if isinstance(x, tuple) else (x,)
    assert len(xs) == 1 or xs[0].shape[0] == n_ctx
    return xs, sum(p.shape[0] for p in xs)


def _parts_specs(xs, tm, ncb):
    d = xs[0].shape[1]
    if len(xs) == 1:
        return [pl.BlockSpec((tm, d), lambda i: (i, 0))]
    return [pl.BlockSpec((tm, d), lambda i: (jnp.minimum(i, ncb - 1), 0)),
            pl.BlockSpec((tm, d), lambda i: (jnp.maximum(i - ncb, 0), 0))]


def _parts_tile(x_refs, ncb):
    if len(x_refs) == 1:
        return x_refs[0][...]
    return jnp.where(pl.program_id(0) < ncb, x_refs[0][...], x_refs[1][...])


def norm_proj(x, g, sc, sh, w, n_ctx, name):
    xs, n = _as_parts(x, n_ctx)
    d = xs[0].shape[1]
    nw = w.shape[1]
    tm = math.gcd(ROW_TILE, n, n_ctx)
    ncb = n_ctx // tm
    nx = len(xs)

    def body(*refs):
        g_ref, sc_ref, sh_ref, w_ref, h_ref, p_ref = refs[nx:]
        h = f_norm_mod(_parts_tile(refs[:nx], ncb), g_ref[0], sc_ref[0], sh_ref[0])[0].astype(BF16)
        h_ref[...] = h
        p_ref[...] = jnp.dot(h, w_ref[...], preferred_element_type=F32)

    row = pl.BlockSpec((tm, d), lambda i: (i, 0))
    return pl.pallas_call(
        body, out_shape=[jax.ShapeDtypeStruct((n, d), BF16), jax.ShapeDtypeStruct((n, nw), F32)], grid=(n // tm,),
        in_specs=_parts_specs(xs, tm, ncb) + [_bc_spec(g, ncb), _bc_spec(sc, ncb), _bc_spec(sh, ncb),
                                              pl.BlockSpec(w.shape, lambda i: (0, 0))],
        out_specs=[row, pl.BlockSpec((tm, nw), lambda i: (i, 0))],
        compiler_params=_params(("parallel",), VMEM_LIMIT), name=name)(*xs, g, sc, sh, w)


def norm_proj_bwd(terms, w, x, g, sc, sh, add, n_ctx, name, latent_dx_only=False, dx_from_segments=False):
    xs, n = _as_parts(x, n_ctx)
    adds = add if isinstance(add, tuple) else (add,)
    d = xs[0].shape[1]
    tm = math.gcd(ROW_TILE, n, n_ctx, *[t[2] for t in terms])
    ncb = n_ctx // tm
    nt, nx, na = len(terms), len(xs), len(adds)
    add_skip = ncb if na == 1 and add.shape[0] != n else 0
    n_dx = 2 if dx_from_segments else 1
    tj = tm // N_SEG
    assert not dx_from_segments or (ncb == 1 and not latent_dx_only)

    def body(*refs):
        i = pl.program_id(0)
        a_refs = refs[:nt]
        w_ref, x_refs = refs[nt], refs[nt + 1:nt + 1 + nx]
        g_ref, sc_ref, sh_ref = refs[nt + 1 + nx:nt + 4 + nx]
        add_refs = refs[nt + 4 + nx:nt + 4 + nx + na]
        outs = refs[nt + 4 + nx + na:]
        dx_refs, (dg_ref, dsc_ref, dsh_ref) = outs[:n_dx], outs[n_dx:n_dx + 3]
        d_h = None
        for a_ref, (a, off, first) in zip(a_refs, terms):
            part = lax.dot_general(a_ref[...].astype(BF16), w_ref[:, off:off + a.shape[1]], NT_DIMS,
                                   preferred_element_type=F32)
            if first:
                part = jnp.where(i >= first // tm, part, 0.0)
            d_h = part if d_h is None else d_h + part
        _, vjp = jax.vjp(lambda x_, g_, sc_, sh_: f_norm_mod(x_, g_, sc_, sh_), _parts_tile(x_refs, ncb), g_ref[0],
                         sc_ref[0], sh_ref[0])
        d_x, d_g, d_sc, d_sh = vjp((d_h,))
        extra = _parts_tile(add_refs, ncb)
        d_x = d_x + (extra if add_skip == 0 else jnp.where(i >= ncb, extra, 0.0))
        if dx_from_segments:
            slabs = outs[n_dx + 3]
            for c in range(d // HEAD_LANES):
                slabs[c] = d_x[:, c * HEAD_LANES:(c + 1) * HEAD_LANES]
            for k, here in enumerate([i < ncb, i >= ncb]):
                @pl.when(here)
                def _(k=k):
                    for c in range(d // HEAD_LANES):
                        for seg in range(N_SEG):
                            dx_refs[k][seg, :, c * HEAD_LANES:(c + 1) * HEAD_LANES] = (
                                slabs[c, pl.ds(seg, tj, stride=N_SEG), :])
        else:
            dx_refs[0][...] = d_x

        @pl.when(i == 0)
        def _():
            dg_ref[...] = jnp.zeros_like(dg_ref)

        @pl.when((i == 0) | (i == ncb))
        def _():
            dsc_ref[...] = jnp.zeros_like(dsc_ref)
            dsh_ref[...] = jnp.zeros_like(dsh_ref)

        dg_ref[0] += d_g
        dsc_ref[0] += d_sc
        dsh_ref[0] += d_sh

    def a_spec(a, first):
        skip = first // tm
        return pl.BlockSpec((tm, a.shape[1]), lambda i: (jnp.maximum(i - skip, 0), 0))

    dx_skip = ncb if latent_dx_only else 0
    if dx_from_segments:
        dx_shapes = [jax.ShapeDtypeStruct((N_SEG, n_ctx // N_SEG, d), F32),
                     jax.ShapeDtypeStruct((N_SEG, (n - n_ctx) // N_SEG, d), F32)]
        dx_specs = [pl.BlockSpec((N_SEG, tj, d), lambda i: (0, 0, 0)),
                    pl.BlockSpec((N_SEG, tj, d), lambda i: (0, jnp.maximum(i - ncb, 0), 0))]
    else:
        dx_shapes = [jax.ShapeDtypeStruct((n - dx_skip * tm, d), F32)]
        dx_specs = [pl.BlockSpec((tm, d), lambda i: (jnp.maximum(i - dx_skip, 0), 0))]
    add_specs = (_parts_specs(adds, tm, ncb) if na == 2 else
                 [pl.BlockSpec((tm, d), lambda i: (jnp.maximum(i - add_skip, 0), 0))])
    res = pl.pallas_call(
        body,
        out_shape=dx_shapes + [jax.ShapeDtypeStruct(g.shape, F32), jax.ShapeDtypeStruct(sc.shape, F32),
                               jax.ShapeDtypeStruct(sh.shape, F32)],
        grid=(n // tm,),
        in_specs=[a_spec(a, first) for a, _, first in terms]
        + [pl.BlockSpec(w.shape, lambda i: (0, 0))] + _parts_specs(xs, tm, ncb)
        + [_bc_spec(g, ncb), _bc_spec(sc, ncb), _bc_spec(sh, ncb)] + add_specs,
        out_specs=dx_specs + [_bc_spec(g, ncb), _bc_spec(sc, ncb), _bc_spec(sh, ncb)],
        scratch_shapes=[pltpu.VMEM((d // HEAD_LANES, tm, HEAD_LANES), F32)] if dx_from_segments else [],
        compiler_params=_params(("arbitrary",), VMEM_LIMIT), name=name)(
            *[t[0] for t in terms], w, *xs, g, sc, sh, *adds)
    if dx_from_segments:
        return ((res[0].reshape(n_ctx, d), res[1].reshape(n - n_ctx, d)), *res[2:])
    return res


def mla_post_fwd(o, p0, x0, gate, w_out, n_ctx, name="l0_post"):
    n, d = o.shape
    xs, _ = _as_parts(x0, n_ctx)
    tm = math.gcd(ROW_TILE, n, n_ctx)
    ncb = n_ctx // tm
    nx = len(xs)

    def body(o_ref, z_ref, *refs):
        gt_ref, w_ref, x1_ref, og_ref, out_ref = refs[nx:]
        og = f_gate(o_ref[...], z_ref[...])[0].astype(BF16)
        out = jnp.dot(og, w_ref[...], preferred_element_type=F32)
        og_ref[...] = og
        out_ref[...] = out
        x1_ref[...] = _parts_tile(refs[:nx], ncb) + gt_ref[0] * out

    row = pl.BlockSpec((tm, d), lambda i: (i, 0))
    return pl.pallas_call(
        body, out_shape=[jax.ShapeDtypeStruct((n, d), F32), jax.ShapeDtypeStruct((n, d), BF16),
                         jax.ShapeDtypeStruct((n, d), F32)],
        grid=(n // tm,),
        in_specs=[row, row] + _parts_specs(xs, tm, ncb) + [_bc_spec(gate, ncb), pl.BlockSpec((d, d), lambda i: (0, 0))],
        out_specs=[row, row, row],
        compiler_params=_params(("parallel",), VMEM_LIMIT), name=name)(o, p0, *xs, gate, w_out)


def mla_post_bwd(dx1, out, og, o, p0, gate, w_out, n_ctx, name="l0_post_bwd"):
    n, d = o.shape
    dxs, _ = _as_parts(dx1, n_ctx)
    tm = math.gcd(ROW_TILE, n, n_ctx)
    ncb = n_ctx // tm
    nx = len(dxs)

    def body(*refs):
        out_ref, og_ref, o_ref, z_ref, gt_ref, w_ref, do_ref, dz_ref, dgt_ref, dw_ref = refs[nx:]
        i = pl.program_id(0)

        @pl.when(i == 0)
        def _():
            dw_ref[...] = jnp.zeros_like(dw_ref)

        @pl.when((i == 0) | (i == ncb))
        def _():
            dgt_ref[...] = jnp.zeros_like(dgt_ref)

        dx = _parts_tile(refs[:nx], ncb)
        dgt_ref[0] += jnp.sum(dx * out_ref[...], axis=0, keepdims=True)
        d_out16 = (gt_ref[0] * dx).astype(BF16)
        dw_ref[...] += lax.dot_general(og_ref[...], d_out16, (((0,), (0,)), ((), ())), preferred_element_type=F32)
        d_og = lax.dot_general(d_out16, w_ref[...], NT_DIMS, preferred_element_type=F32)
        _, gate_vjp = jax.vjp(lambda o_, z_: f_gate(o_, z_), o_ref[...], z_ref[...])
        d_o, d_z = gate_vjp((d_og,))
        do_ref[...] = d_o
        dz_ref[...] = d_z

    row = pl.BlockSpec((tm, d), lambda i: (i, 0))
    mat = pl.BlockSpec((d, d), lambda i: (0, 0))
    return pl.pallas_call(
        body, out_shape=[jax.ShapeDtypeStruct((n, d), F32), jax.ShapeDtypeStruct((n, d), F32),
                         jax.ShapeDtypeStruct(gate.shape, F32), jax.ShapeDtypeStruct((d, d), F32)],
        grid=(n // tm,),
        in_specs=_parts_specs(dxs, tm, ncb) + [row, row, row, row, _bc_spec(gate, ncb), mat],
        out_specs=[row, row, _bc_spec(gate, ncb), mat],
        compiler_params=_params(("arbitrary",), VMEM_LIMIT), name=name)(*dxs, out, og, o, p0, gate, w_out)


def s5_tail(y_ssm, p1, x1p, target, n_ctx, d_vec, b_glu, gate, final_g, w_glu, w_out, name="l1_tail"):
    n, d = y_ssm.shape
    tm = math.gcd(ROW_TILE, n, n_ctx)
    off = n_ctx // tm
    tn_dims = (((0,), (0,)), ((), ()))

    def row_loss(x, g, t):
        e = _rms(x) * g - t
        return 0.5 * (e * e) * (1.0 / d)

    def body(y_ref, u_ref, z_ref, x1_ref, t_ref, d_ref, b_ref, gt_ref, fg_ref, wg_ref, wo_ref,
             l_ref, dx_ref, dy_ref, du_ref, dz_ref, dfg_ref, dgt_ref, db_ref, dd_ref, dwg_ref, dwo_ref):
        @pl.when(pl.program_id(0) == 0)
        def _():
            for r in (l_ref, dfg_ref, dgt_ref, db_ref, dd_ref, dwg_ref, dwo_ref):
                r[...] = jnp.zeros_like(r)

        u, z, tgt, gt = u_ref[...], z_ref[...], t_ref[...], gt_ref[...]
        (ya,), act_vjp = jax.vjp(lambda y_, u_, d_: f_s5_act(y_, u_, d_), y_ref[...], u, d_ref[...])
        ya16 = ya.astype(BF16)
        gl = jnp.dot(ya16, wg_ref[...], preferred_element_type=F32)
        (y3,), glu_vjp = jax.vjp(lambda a_, g_, z_, b_: f_s5_glu(a_, g_, z_, b_), ya, gl, z, b_ref[...])
        y3_16 = y3.astype(BF16)
        out1 = jnp.dot(y3_16, wo_ref[...], preferred_element_type=F32)
        lterm, loss_vjp = jax.vjp(lambda x_, g_: row_loss(x_, g_, tgt), x1_ref[...] + gt * out1, fg_ref[...])
        dx2, dfg = loss_vjp(jnp.ones_like(lterm))
        l_ref[...] += jnp.sum(lterm, axis=0, keepdims=True)
        dfg_ref[...] += dfg
        dx_ref[...] = dx2
        dgt_ref[...] += jnp.sum(dx2 * out1, axis=0, keepdims=True)
        d_out16 = (gt * dx2).astype(BF16)
        dwo_ref[...] += lax.dot_general(y3_16, d_out16, tn_dims, preferred_element_type=F32)
        d_y3 = lax.dot_general(d_out16, wo_ref[...], NT_DIMS, preferred_element_type=F32)
        d_ya, d_gl, d_z, d_b = glu_vjp((d_y3,))
        dz_ref[...] = d_z
        db_ref[...] += d_b
        d_gl16 = d_gl.astype(BF16)
        dwg_ref[...] += lax.dot_general(ya16, d_gl16, tn_dims, preferred_element_type=F32)
        d_ya = d_ya + lax.dot_general(d_gl16, wg_ref[...], NT_DIMS, preferred_element_type=F32)
        d_y, d_u, d_d = act_vjp((d_ya,))
        dy_ref[...] = d_y
        du_ref[...] = d_u
        dd_ref[...] += d_d

    row = pl.BlockSpec((tm, d), lambda i: (i, 0))
    vecs = pl.BlockSpec((1, d), lambda i: (0, 0))
    mat = pl.BlockSpec((d, d), lambda i: (0, 0))
    return pl.pallas_call(
        body,
        out_shape=[jax.ShapeDtypeStruct((1, d), F32)] + [jax.ShapeDtypeStruct((n, d), F32)] * 4
        + [jax.ShapeDtypeStruct((1, d), F32)] * 4 + [jax.ShapeDtypeStruct((d, d), F32)] * 2,
        grid=(n // tm,),
        in_specs=[row, pl.BlockSpec((tm, d), lambda i: (i + off, 0)), pl.BlockSpec((tm, d), lambda i: (i + off, 1)),
                  pl.BlockSpec((tm, d), lambda i: (i + off, 0)), row, vecs, vecs, vecs, vecs, mat, mat],
        out_specs=[vecs, row, row, row, row, vecs, vecs, vecs, vecs, mat, mat],
        compiler_params=_params(("arbitrary",), VMEM_LIMIT), name=name)(
            y_ssm, p1, p1, x1p, target, d_vec, b_glu, gate, final_g, w_glu, w_out)


NT_DIMS = (((1,), (1,)), ((), ()))
HEAD_LANES = 128
N_PAIRS = MLA_HEADS // 2


def _own_lanes(shape, hh):
    lane = lax.broadcasted_iota(jnp.int32, shape, len(shape) - 1)
    return (lane < V_HEAD_DIM) if hh == 0 else (lane >= V_HEAD_DIM)


def _delta_lane(hh):
    return V_HEAD_DIM if hh == 0 else 0


def _rope_tiles(x, cos, sin_next, sin_prev, inverse):
    width = x.shape[-1]
    reps = width // HEAD_LANES
    c, sn, sp = (jnp.tile(t, (1, reps)) for t in (cos, sin_next, sin_prev))
    if inverse:
        return x * c + pltpu.roll(x * sn, 8, 1) + pltpu.roll(x * sp, width - 8, 1)
    return x * c + pltpu.roll(x, width - 8, 1) * sn + pltpu.roll(x, 8, 1) * sp


STAT_LANE = QK_DIM


def _with_stat(x16, col, lane0):
    hi = col.astype(BF16)
    r1 = col - hi.astype(F32)
    mid = r1.astype(BF16)
    lo = (r1 - mid.astype(F32)).astype(BF16)
    lane = lax.broadcasted_iota(jnp.int32, x16.shape, 1)
    return jnp.where(lane == lane0, hi, jnp.where(lane == lane0 + 1, mid, jnp.where(lane == lane0 + 2, lo, x16)))


def attn_fwd(qb, kb, vb, n_ctx):
    T = qb.shape[0]
    tq = math.gcd(ROW_TILE, n_ctx)
    nq, ncb = T // tq, n_ctx // tq

    def body(q_ref, k_ref, v_ref, o_ref, lse_ref, qs_ref):
        qi = pl.program_id(1)

        def rows(n_keys):
            v = v_ref[:n_keys, :]
            outs = []
            for hh in range(2):
                hs = slice(hh * HEAD_LANES, (hh + 1) * HEAD_LANES)
                s = lax.dot_general(q_ref[:, hs], k_ref[:n_keys, hs], NT_DIMS,
                                    preferred_element_type=F32) * SOFTMAX_SCALE
                m = jnp.max(s, axis=-1, keepdims=True)
                p = jnp.exp(s - m)
                l = jnp.sum(p, axis=-1, keepdims=True)
                outs.append(jnp.dot(p.astype(BF16), v, preferred_element_type=F32) / l)
                lse = m + jnp.log(l)
                lse_ref[hh] = lse
                qs_ref[:, hs] = _with_stat(q_ref[:, hs], lse * (-1.0 / SOFTMAX_SCALE), STAT_LANE)
            o_ref[...] = jnp.where(_own_lanes(outs[0].shape, 0), outs[0], outs[1])

        pl.when(qi < ncb)(lambda: rows(n_ctx))
        pl.when(qi >= ncb)(lambda: rows(T))

    return pl.pallas_call(
        body,
        out_shape=[jax.ShapeDtypeStruct((T, MLA_HEADS * V_HEAD_DIM), F32),
                   jax.ShapeDtypeStruct((MLA_HEADS, T, 1), F32), jax.ShapeDtypeStruct(qb.shape, BF16)],
        grid=(N_PAIRS, nq),
        in_specs=[pl.BlockSpec((tq, 2 * HEAD_LANES), lambda h, i: (i, h)),
                  pl.BlockSpec((T, 2 * HEAD_LANES), lambda h, i: (0, h)),
                  pl.BlockSpec((T, 2 * V_HEAD_DIM), lambda h, i: (0, h))],
        out_specs=[pl.BlockSpec((tq, 2 * V_HEAD_DIM), lambda h, i: (i, h)),
                   pl.BlockSpec((2, tq, 1), lambda h, i: (h, i, 0)),
                   pl.BlockSpec((tq, 2 * HEAD_LANES), lambda h, i: (i, h))],
        compiler_params=_params(("parallel", "parallel"), VMEM_LIMIT), name="attn_fwd")(qb, kb, vb)


def attn_bwd_dq(qb, kb, vb, o, do, lse, tabs, n_ctx):
    T = qb.shape[0]
    tq = math.gcd(ROW_TILE, n_ctx)
    nq, ncb = T // tq, n_ctx // tq

    def body(q_ref, k_ref, v_ref, o_ref, do_ref, lse_ref, c_ref, sn_ref, sp_ref, dq_ref, dos_ref):
        qi = pl.program_id(1)

        def rows(n_keys):
            v = v_ref[:n_keys, :]
            dqs = []
            for hh in range(2):
                hs = slice(hh * HEAD_LANES, (hh + 1) * HEAD_LANES)
                k = k_ref[:n_keys, hs]
                do = jnp.where(_own_lanes(do_ref.shape, hh), do_ref[...], 0.0)
                delta = jnp.sum(do * o_ref[...], axis=-1, keepdims=True)
                s = lax.dot_general(q_ref[:, hs], k, NT_DIMS, preferred_element_type=F32) * SOFTMAX_SCALE
                p = jnp.exp(s - lse_ref[hh])
                do16 = do.astype(BF16)
                dp = lax.dot_general(do16, v, NT_DIMS, preferred_element_type=F32)
                ds = p * (dp - delta) * SOFTMAX_SCALE
                dqs.append(jnp.dot(ds.astype(BF16), k, preferred_element_type=F32))
                dos_ref[:, hs] = _with_stat(do16, delta, _delta_lane(hh))
            dq = jnp.concatenate(dqs, axis=1)
            dq_ref[...] = _rope_tiles(dq, c_ref[...], sn_ref[...], sp_ref[...], True).astype(BF16)

        pl.when(qi < ncb)(lambda: rows(n_ctx))
        pl.when(qi >= ncb)(lambda: rows(T))

    tab = pl.BlockSpec((tq, HEAD_LANES), lambda h, i: (i, 0))
    return pl.pallas_call(
        body,
        out_shape=[jax.ShapeDtypeStruct((T, MLA_HEADS * HEAD_LANES), BF16)] * 2,
        grid=(N_PAIRS, nq),
        in_specs=[pl.BlockSpec((tq, 2 * HEAD_LANES), lambda h, i: (i, h)),
                  pl.BlockSpec((T, 2 * HEAD_LANES), lambda h, i: (0, h)),
                  pl.BlockSpec((T, 2 * V_HEAD_DIM), lambda h, i: (0, h)),
                  pl.BlockSpec((tq, 2 * V_HEAD_DIM), lambda h, i: (i, h)),
                  pl.BlockSpec((tq, 2 * V_HEAD_DIM), lambda h, i: (i, h)),
                  pl.BlockSpec((2, tq, 1), lambda h, i: (h, i, 0)), tab, tab, tab],
        out_specs=[pl.BlockSpec((tq, 2 * HEAD_LANES), lambda h, i: (i, h))] * 2,
        compiler_params=_params(("parallel", "parallel"), VMEM_LIMIT), name="attn_bwd_dq")(
            qb, kb, vb, o, do, lse, *tabs)


def attn_bwd_dkv(qs, kb, vb, dos, n_ctx):
    T = qs.shape[0]
    tq = math.gcd(ROW_TILE, n_ctx)
    nq, ncb = T // tq, n_ctx // tq

    def body(q_ref, do_ref, k_ref, v_ref, dk_ref, dv_ref):
        kj = pl.program_id(1)

        def cols(first):
            v = v_ref[...]
            lane = lax.broadcasted_iota(jnp.int32, v.shape, 1)
            dvs = []
            for hh in range(2):
                hs = slice(hh * HEAD_LANES, (hh + 1) * HEAD_LANES)
                q = q_ref[first:, hs]
                do16 = do_ref[first:, hs]
                in_delta = (lane >= _delta_lane(hh)) & (lane < _delta_lane(hh) + 3)
                v_minus = jnp.where(in_delta, -jnp.ones_like(v), v)
                pt = jnp.exp(lax.dot_general(k_ref[:, hs], q, NT_DIMS, preferred_element_type=F32) * SOFTMAX_SCALE)
                dvs.append(jnp.dot(pt.astype(BF16), do16, preferred_element_type=F32))
                dst = pt * lax.dot_general(v_minus, do16, NT_DIMS, preferred_element_type=F32) * SOFTMAX_SCALE
                dk_ref[:, hs] = jnp.dot(dst.astype(BF16), q, preferred_element_type=F32)
            dv_ref[...] = jnp.where(_own_lanes(dvs[0].shape, 0), dvs[0], dvs[1])

        pl.when(kj < ncb)(lambda: cols(0))
        pl.when(kj >= ncb)(lambda: cols(n_ctx))

    return pl.pallas_call(
        body,
        out_shape=[jax.ShapeDtypeStruct((T, MLA_HEADS * HEAD_LANES), F32),
                   jax.ShapeDtypeStruct((T, MLA_HEADS * V_HEAD_DIM), F32)],
        grid=(N_PAIRS, nq),
        in_specs=[pl.BlockSpec((T, 2 * HEAD_LANES), lambda h, j: (0, h)),
                  pl.BlockSpec((T, 2 * HEAD_LANES), lambda h, j: (0, h)),
                  pl.BlockSpec((tq, 2 * HEAD_LANES), lambda h, j: (j, h)),
                  pl.BlockSpec((tq, 2 * V_HEAD_DIM), lambda h, j: (j, h))],
        out_specs=[pl.BlockSpec((tq, 2 * HEAD_LANES), lambda h, j: (j, h)),
                   pl.BlockSpec((tq, 2 * V_HEAD_DIM), lambda h, j: (j, h))],
        compiler_params=_params(("parallel", "parallel"), VMEM_LIMIT), name="attn_bwd_dkv")(
            qs, dos, kb, vb)


def _split_bf16(x):
    hi = x.astype(BF16)
    return hi, (x - hi.astype(F32)).astype(BF16)


def q_heads(cq, gain, w_uq_p, tabs, name="l0_uq"):
    T, K = cq.arr.shape[0], cq.width
    N = w_uq_p.shape[1]
    tm = math.gcd(ROW_TILE, T)

    def body(a_ref, g_ref, w_ref, c_ref, sn_ref, sp_ref, o_ref, n_ref):
        qn = f_rms(a_ref[...], g_ref[0])[0].astype(BF16)
        n_ref[...] = qn
        acc = jnp.dot(qn, w_ref[...], preferred_element_type=F32)
        o_ref[...] = _rope_tiles(acc, c_ref[...], sn_ref[...], sp_ref[...], False).astype(BF16)

    tab = pl.BlockSpec((tm, HEAD_LANES), lambda i: (i, 0))
    return pl.pallas_call(
        body, out_shape=[jax.ShapeDtypeStruct((T, N), BF16), jax.ShapeDtypeStruct((T, K), BF16)], grid=(T // tm,),
        in_specs=[cq.spec(tm), pl.BlockSpec((1, 1, K), lambda i: (0, 0, 0)), pl.BlockSpec((K, N), lambda i: (0, 0)),
                  tab, tab, tab],
        out_specs=[pl.BlockSpec((tm, N), lambda i: (i, 0)), pl.BlockSpec((tm, K), lambda i: (i, 0))],
        compiler_params=_params(("parallel",), VMEM_LIMIT), name=name)(cq.arr, gain, w_uq_p, *tabs)


def kv_heads(ckv, gain, w_kn_p, w_v, kr, spread, tabs, name="l0_ukv"):
    T, K = ckv.arr.shape[0], ckv.width
    N = w_kn_p.shape[1]
    NV = w_v.shape[1]
    tm = math.gcd(ROW_TILE, T)

    def body(a_ref, g_ref, wk_ref, wv_ref, kr_ref, e_ref, c_ref, sn_ref, sp_ref, k_ref, v_ref, n_ref):
        a = f_rms(a_ref[...], g_ref[0])[0].astype(BF16)
        n_ref[...] = a
        hi, lo = _split_bf16(kr_ref[...])
        acc = (jnp.dot(a, wk_ref[...], preferred_element_type=F32)
               + jnp.dot(hi, e_ref[...], preferred_element_type=F32)
               + jnp.dot(lo, e_ref[...], preferred_element_type=F32))
        roped = _rope_tiles(acc, c_ref[...], sn_ref[...], sp_ref[...], False)
        lane = lax.broadcasted_iota(jnp.int32, roped.shape, 1) % HEAD_LANES
        k_ref[...] = jnp.where((lane >= STAT_LANE) & (lane < STAT_LANE + 3), 1.0, roped).astype(BF16)
        v_ref[...] = jnp.dot(a, wv_ref[...], preferred_element_type=F32).astype(BF16)

    tab = pl.BlockSpec((tm, HEAD_LANES), lambda i: (i, 0))
    return pl.pallas_call(
        body, out_shape=[jax.ShapeDtypeStruct((T, N), BF16), jax.ShapeDtypeStruct((T, NV), BF16),
                         jax.ShapeDtypeStruct((T, K), BF16)], grid=(T // tm,),
        in_specs=[ckv.spec(tm), pl.BlockSpec((1, 1, K), lambda i: (0, 0, 0)), pl.BlockSpec((K, N), lambda i: (0, 0)),
                  pl.BlockSpec((K, NV), lambda i: (0, 0)), kr.spec(tm),
                  pl.BlockSpec((kr.width, N), lambda i: (0, 0)), tab, tab, tab],
        out_specs=[pl.BlockSpec((tm, N), lambda i: (i, 0)), pl.BlockSpec((tm, NV), lambda i: (i, 0)),
                   pl.BlockSpec((tm, K), lambda i: (i, 0))],
        compiler_params=_params(("parallel",), VMEM_LIMIT), name=name)(
            ckv.arr, gain, w_kn_p, w_v, kr.arr, spread, *tabs)


def heads_unrope(d, tabs, spread=None, name="unrope"):
    T, N = d.shape
    tm = math.gcd(ROW_TILE, T)

    def body(*refs):
        if spread is None:
            d_ref, c_ref, sn_ref, sp_ref, o_ref = refs
        else:
            d_ref, c_ref, sn_ref, sp_ref, e_ref, o_ref, kr_ref = refs
        g = _rope_tiles(d_ref[...], c_ref[...], sn_ref[...], sp_ref[...], True)
        o_ref[...] = g.astype(BF16)
        if spread is not None:
            hi, lo = _split_bf16(g)
            kr_ref[...] = (lax.dot_general(hi, e_ref[...], NT_DIMS, preferred_element_type=F32)
                           + lax.dot_general(lo, e_ref[...], NT_DIMS, preferred_element_type=F32))

    tab = pl.BlockSpec((tm, HEAD_LANES), lambda i: (i, 0))
    row = pl.BlockSpec((tm, N), lambda i: (i, 0))
    ins, in_specs = [d, *tabs], [row, tab, tab, tab]
    out_shape, out_specs = [jax.ShapeDtypeStruct((T, N), BF16)], [row]
    if spread is not None:
        ins.append(spread)
        in_specs.append(pl.BlockSpec(spread.shape, lambda i: (0, 0)))
        out_shape.append(jax.ShapeDtypeStruct((T, spread.shape[0]), F32))
        out_specs.append(pl.BlockSpec((tm, spread.shape[0]), lambda i: (i, 0)))
    return pl.pallas_call(
        body, out_shape=out_shape, grid=(T // tm,), in_specs=in_specs, out_specs=out_specs,
        compiler_params=_params(("parallel",), VMEM_LIMIT), name=name)(*ins)


def _cmul(ar, ai, br, bi):
    return ar * br - ai * bi, ar * bi + ai * br


def s5_chain(finals, s0, a, n_steps, reverse, name):
    W = finals.shape[-1]
    first = N_SEG - 1 if reverse else 0

    def body(f_ref, s0_ref, a_ref, c_ref):
        pr, pi = jnp.ones((1, W), F32), jnp.zeros((1, W), F32)
        br, bi = a_ref[0], a_ref[1]
        n = n_steps
        while n:
            if n & 1:
                pr, pi = _cmul(pr, pi, br, bi)
            br, bi = _cmul(br, bi, br, bi)
            n >>= 1
        fr, fi = f_ref[0], f_ref[1]
        row = lax.broadcasted_iota(jnp.int32, (N_SEG, W), 0)
        s0r = jnp.broadcast_to(s0_ref[0], (N_SEG, W))
        s0i = jnp.broadcast_to(s0_ref[1], (N_SEG, W))
        cr = jnp.where(row == first, s0r, 0.0)
        ci = jnp.where(row == first, s0i, 0.0)
        shift = N_SEG - 1 if reverse else 1
        for _ in range(N_SEG - 1):
            mr, mi = _cmul(pr, pi, cr, ci)
            tr = pltpu.roll(fr + mr, shift, 0)
            ti = pltpu.roll(fi + mi, shift, 0)
            cr = jnp.where(row == first, s0r, tr)
            ci = jnp.where(row == first, s0i, ti)
        c_ref[0] = cr
        c_ref[1] = ci

    return pl.pallas_call(body, out_shape=jax.ShapeDtypeStruct((2, N_SEG, W), F32), name=name)(finals, s0, a)


def _scan_chunk(bur, bui, st_ref, a_ref, n_steps, reverse):
    for lc in range(S5_LANES // BLK_ST):
        sl = slice(lc * BLK_ST, (lc + 1) * BLK_ST)
        lr = jnp.broadcast_to(a_ref[0, :, sl], (N_SEG, BLK_ST))
        li = jnp.broadcast_to(a_ref[1, :, sl], (N_SEG, BLK_ST))

        def step(jj, carry, sl=sl, lr=lr, li=li):
            sr, si = carry
            j = (n_steps - 1 - jj) if reverse else jj
            r0 = pl.multiple_of(j * N_SEG, N_SEG)
            nr = lr * sr - li * si + bur[pl.ds(r0, N_SEG), sl]
            ni = lr * si + li * sr + bui[pl.ds(r0, N_SEG), sl]
            bur[pl.ds(r0, N_SEG), sl] = nr
            bui[pl.ds(r0, N_SEG), sl] = ni
            return nr, ni

        sr, si = lax.fori_loop(0, n_steps, step, (st_ref[0, :, sl], st_ref[1, :, sl]))
        st_ref[0, :, sl] = sr
        st_ref[1, :, sl] = si


def _project_in(x16, w_re, w_im, bur, bui, adjoint):
    for gb in range(N_BLOCKS):
        xb = x16[:, gb * BLK_CH:(gb + 1) * BLK_CH]
        sl = slice(gb * BLK_ST, (gb + 1) * BLK_ST)
        if adjoint:
            dn = (((1,), (1,)), ((), ()))
            bur[:, sl] = lax.dot_general(xb, w_re[gb], dn, preferred_element_type=F32)
            bui[:, sl] = -lax.dot_general(xb, w_im[gb], dn, preferred_element_type=F32)
        else:
            bur[:, sl] = jnp.dot(xb, w_re[gb], preferred_element_type=F32)
            bui[:, sl] = jnp.dot(xb, w_im[gb], preferred_element_type=F32)


def s5_scan(act, w_re, w_im, a, init, *, reverse, adjoint=False, c_re=None, c_im=None, add=None,
            want_ckpt=False, rows=None, name):
    act_off, N = rows if rows is not None else (0, act.shape[0])
    R = math.gcd(ROW_TILE, N, act_off)
    nch, jc = N // R, R // N_SEG
    with_out = c_re is not None

    def chunk(i):
        return (nch - 1 - i) if reverse else i

    def body(*refs):
        act_ref, wre_ref, wim_ref, a_ref, init_ref = refs[:5]
        k = 5
        if with_out:
            cre_ref, cim_ref = refs[k:k + 2]
            k += 2
        if add is not None:
            add_ref = refs[k]
            k += 1
        if with_out:
            out_ref = refs[k]
            k += 1
        if want_ckpt:
            ck_ref = refs[k]
            k += 1
        fin_ref, bur, bui = refs[k:k + 3]

        @pl.when(pl.program_id(0) == 0)
        def _():
            fin_ref[...] = init_ref[...]

        if want_ckpt:
            ck_ref[0] = fin_ref[...]
        _project_in(act_ref[...].astype(BF16), wre_ref, wim_ref, bur, bui, adjoint)
        _scan_chunk(bur, bui, fin_ref, a_ref, jc, reverse)
        if with_out:
            for gb in range(N_BLOCKS):
                sl = slice(gb * BLK_ST, (gb + 1) * BLK_ST)
                y = (jnp.dot(bur[:, sl].astype(BF16), cre_ref[gb], preferred_element_type=F32)
                     - jnp.dot(bui[:, sl].astype(BF16), cim_ref[gb], preferred_element_type=F32))
                cs = slice(gb * BLK_CH, (gb + 1) * BLK_CH)
                if add is not None:
                    y = y + add_ref[:, cs]
                out_ref[:, cs] = y

    row_spec = pl.BlockSpec((R, D_MODEL), lambda i: (chunk(i), 0))
    act_spec = pl.BlockSpec((R, D_MODEL), lambda i: (chunk(i) + act_off // R, 0))
    w_spec = pl.BlockSpec(w_re.shape, lambda i: (0, 0, 0))
    st_spec = pl.BlockSpec((2, N_SEG, S5_LANES), lambda i: (0, 0, 0))
    ins = [act, w_re, w_im, a, init]
    in_specs = [act_spec, w_spec, w_spec, pl.BlockSpec((2, 1, S5_LANES), lambda i: (0, 0, 0)), st_spec]
    if with_out:
        ins += [c_re, c_im]
        in_specs += [pl.BlockSpec(c_re.shape, lambda i: (0, 0, 0))] * 2
    if add is not None:
        ins.append(add)
        in_specs.append(row_spec)
    out_shape, out_specs = [], []
    if with_out:
        out_shape.append(jax.ShapeDtypeStruct((N, D_MODEL), F32))
        out_specs.append(row_spec)
    if want_ckpt:
        out_shape.append(jax.ShapeDtypeStruct((nch, 2, N_SEG, S5_LANES), F32))
        out_specs.append(pl.BlockSpec((1, 2, N_SEG, S5_LANES), lambda i: (chunk(i), 0, 0, 0)))
    out_shape.append(jax.ShapeDtypeStruct((2, N_SEG, S5_LANES), F32))
    out_specs.append(st_spec)
    res = pl.pallas_call(
        body, out_shape=out_shape, grid=(nch,), in_specs=in_specs, out_specs=out_specs,
        scratch_shapes=[pltpu.VMEM((R, S5_LANES), F32), pltpu.VMEM((R, S5_LANES), F32)],
        compiler_params=_params(("arbitrary",), VMEM_LIMIT), name=name)(*ins)
    res = list(res)
    out = res.pop(0) if with_out else None
    ckpt = res.pop(0) if want_ckpt else None
    return out, ckpt, res[0]


def s5_grads(dy, u, ckpt, b_re, b_im, c_re, c_im, lam, init_adj, *, reverse, add=None, u_off=0, du_rows=None,
             du_into=None, n_rows=None, sums_from=None, name):
    N = n_rows if dy is None else dy.shape[0]
    du_total, du_first = du_rows if du_rows is not None else (N, 0)
    R = math.gcd(ROW_TILE, N, u_off, du_first)
    nch, jc = N // R, R // N_SEG
    W = S5_LANES

    def chunk(i):
        return i if reverse else (nch - 1 - i)

    def body(*refs):
        if dy is None:
            refs = (None,) + refs[:4] + (None, None) + refs[4:]
        dy_ref, u_ref, ck_ref, bre_ref, bim_ref, cre_ref, cim_ref, lam_ref, init_ref = refs[:9]
        k = 9
        if add is not None:
            add_ref = refs[k]
            k += 1
        k += du_into is not None
        sum_refs = refs[k:k + 3] if sums_from is not None else None
        outs = refs[k + (3 if sums_from is not None else 0):]
        if dy is None:
            outs = outs[:4] + (None, None) + outs[4:]
        du_ref, dlam_ref, dbre_ref, dbim_ref, dcre_ref, dcim_ref, fin_ref = outs[:7]
        sr_buf, si_buf, er_buf, ei_buf, st_buf = outs[7:12]

        @pl.when(pl.program_id(0) == 0)
        def _():
            fin_ref[...] = init_ref[...]
            for acc_ref, k3 in zip((dlam_ref, dbre_ref, dbim_ref), range(3)):
                acc_ref[...] = jnp.zeros_like(acc_ref) if sum_refs is None else sum_refs[k3][...]
            if dy is not None:
                dcre_ref[...] = jnp.zeros_like(dcre_ref)
                dcim_ref[...] = jnp.zeros_like(dcim_ref)

        u16 = u_ref[...].astype(BF16)
        st_buf[...] = ck_ref[0]
        _project_in(u16, bre_ref, bim_ref, sr_buf, si_buf, False)
        _scan_chunk(sr_buf, si_buf, st_buf, lam_ref, jc, reverse)
        if dy is None:
            er_buf[...] = jnp.zeros_like(er_buf)
            ei_buf[...] = jnp.zeros_like(ei_buf)
        else:
            dy16 = dy_ref[...].astype(BF16)
            _project_in(dy16, cre_ref, cim_ref, er_buf, ei_buf, True)
        for lc in range(W // BLK_ST):
            sl = slice(lc * BLK_ST, (lc + 1) * BLK_ST)
            lr = jnp.broadcast_to(lam_ref[0, :, sl], (N_SEG, BLK_ST))
            li = jnp.broadcast_to(lam_ref[1, :, sl], (N_SEG, BLK_ST))

            def one(r0, spr, spi, carry, sl=sl, lr=lr, li=li):
                gr, gi, ar, ai = carry
                nr = er_buf[pl.ds(r0, N_SEG), sl] + lr * gr + li * gi
                ni = ei_buf[pl.ds(r0, N_SEG), sl] + lr * gi - li * gr
                er_buf[pl.ds(r0, N_SEG), sl] = nr
                ei_buf[pl.ds(r0, N_SEG), sl] = ni
                return nr, ni, ar + spr * nr + spi * ni, ai + spr * ni - spi * nr

            def step(ff, carry, sl=sl, one=one):
                f = jc - 1 - ff
                j = (jc - 1 - f) if reverse else f
                jp = (j + 1) if reverse else (j - 1)
                r0 = pl.multiple_of(j * N_SEG, N_SEG)
                p0 = pl.multiple_of(jp * N_SEG, N_SEG)
                return one(r0, sr_buf[pl.ds(p0, N_SEG), sl], si_buf[pl.ds(p0, N_SEG), sl], carry)

            carry = (fin_ref[0, :, sl], fin_ref[1, :, sl], dlam_ref[0, :, sl], dlam_ref[1, :, sl])
            carry = lax.fori_loop(0, jc - 1, step, carry)
            r_first = (jc - 1) * N_SEG if reverse else 0
            gr, gi, ar, ai = one(r_first, ck_ref[0, 0, :, sl], ck_ref[0, 1, :, sl], carry)
            fin_ref[0, :, sl] = gr
            fin_ref[1, :, sl] = gi
            dlam_ref[0, :, sl] = ar
            dlam_ref[1, :, sl] = ai
        tn = (((0,), (0,)), ((), ()))
        nt = (((1,), (1,)), ((), ()))
        for gb in range(N_BLOCKS):
            sl = slice(gb * BLK_ST, (gb + 1) * BLK_ST)
            cs = slice(gb * BLK_CH, (gb + 1) * BLK_CH)
            gr16 = er_buf[:, sl].astype(BF16)
            gi16 = ei_buf[:, sl].astype(BF16)
            du = (lax.dot_general(gr16, bre_ref[gb], nt, preferred_element_type=F32)
                  + lax.dot_general(gi16, bim_ref[gb], nt, preferred_element_type=F32))
            if add is not None:
                du = du + add_ref[:, cs]
            du_ref[:, cs] = du
            ub = u16[:, cs]
            dbre_ref[gb] += lax.dot_general(ub, gr16, tn, preferred_element_type=F32)
            dbim_ref[gb] += lax.dot_general(ub, gi16, tn, preferred_element_type=F32)
            if dy is not None:
                dyb = dy16[:, cs]
                dcre_ref[gb] += lax.dot_general(sr_buf[:, sl].astype(BF16), dyb, tn, preferred_element_type=F32)
                dcim_ref[gb] -= lax.dot_general(si_buf[:, sl].astype(BF16), dyb, tn, preferred_element_type=F32)

    row_spec = pl.BlockSpec((R, D_MODEL), lambda i: (chunk(i), 0))
    st_spec = pl.BlockSpec((2, N_SEG, W), lambda i: (0, 0, 0))
    wb_spec = pl.BlockSpec(b_re.shape, lambda i: (0, 0, 0))
    wc_spec = pl.BlockSpec(c_re.shape, lambda i: (0, 0, 0))
    ins = [dy, u, ckpt, b_re, b_im, c_re, c_im, lam, init_adj]
    u_spec = pl.BlockSpec((R, D_MODEL), lambda i: (chunk(i) + u_off // R, 0))
    in_specs = [row_spec, u_spec, pl.BlockSpec((1, 2, N_SEG, W), lambda i: (chunk(i), 0, 0, 0)),
                wb_spec, wb_spec, wc_spec, wc_spec, pl.BlockSpec((2, 1, W), lambda i: (0, 0, 0)), st_spec]
    if dy is None:
        ins, in_specs = ins[1:5] + ins[7:], in_specs[1:5] + in_specs[7:]
    if add is not None:
        ins.append(add)
        in_specs.append(row_spec)
    aliases = {}
    if du_into is not None:
        aliases[len(ins)] = 0
        ins.append(du_into)
        in_specs.append(pl.BlockSpec(memory_space=pl.ANY))
    if sums_from is not None:
        ins.extend(sums_from)
        in_specs.extend([st_spec, wb_spec, wb_spec])
    du_spec =pl.BlockSpec((R, D_MODEL), lambda i: (chunk(i) + du_first // R, 0))
    out_shape = [jax.ShapeDtypeStruct((du_total, D_MODEL), F32), jax.ShapeDtypeStruct((2, N_SEG, W), F32),
                 jax.ShapeDtypeStruct(b_re.shape, F32), jax.ShapeDtypeStruct(b_re.shape, F32),
                 jax.ShapeDtypeStruct(c_re.shape, F32), jax.ShapeDtypeStruct(c_re.shape, F32),
                 jax.ShapeDtypeStruct((2, N_SEG, W), F32)]
    out_specs = [du_spec, st_spec, wb_spec, wb_spec, wc_spec, wc_spec, st_spec]
    if dy is None:
        out_shape, out_specs = out_shape[:4] + out_shape[6:], out_specs[:4] + out_specs[6:]
    res = pl.pallas_call(
        body, out_shape=out_shape, grid=(nch,), in_specs=in_specs, out_specs=out_specs, input_output_aliases=aliases,
        scratch_shapes=[pltpu.VMEM((R, W), F32) for _ in range(4)] + [pltpu.VMEM((2, N_SEG, W), F32)],
        compiler_params=_params(("arbitrary",), VMEM_LIMIT), name=name)(*ins)
    return res if dy is not None else [*res[:4], None, None, res[4]]


def adamw(w, g, m, v, name="adamw", after=None):
    n, d = w.shape
    lanes = -(-d // 128) * 128
    tm = n
    while tm * lanes * 4 > (1 << 20) and tm % 16 == 0:
        tm //= 2
    c1 = 1.0 - ADAM_B1 ** ADAM_STEP
    c2 = 1.0 - ADAM_B2 ** ADAM_STEP

    def body(w_ref, g_ref, m_ref, v_ref, *rest):
        d_ref, nm_ref, nv_ref = rest[-3:]
        g_ = g_ref[...]
        m_ = ADAM_B1 * m_ref[...] + (1.0 - ADAM_B1) * g_
        v_ = ADAM_B2 * v_ref[...] + (1.0 - ADAM_B2) * (g_ * g_)
        d_ref[...] = -ADAM_LR * ((m_ / c1) / (jnp.sqrt(v_ / c2) + ADAM_EPS) + ADAM_WD * w_ref[...])
        nm_ref[...] = m_
        nv_ref[...] = v_

    spec = pl.BlockSpec((tm, d), lambda i: (i, 0))
    extra = [] if after is None else [after]
    return pl.pallas_call(
        body, out_shape=[jax.ShapeDtypeStruct((n, d), F32)] * 3, grid=(n // tm,),
        in_specs=[spec] * 4 + [pl.BlockSpec(memory_space=pl.ANY)] * len(extra), out_specs=[spec] * 3,
        compiler_params=_params(("parallel",), VMEM_LIMIT), name=name)(w, g, m, v, *extra)


def _coords():
    return lax.axis_index("x"), lax.axis_index("y"), lax.axis_index("c")


def exchange(arrays, out_shapes, remote, local, name, aliases=None):
    n_in, n_out, n_rem, n_loc = len(arrays), len(out_shapes), len(remote), len(local)

    def at(ref, idx):
        return ref if idx is None else ref.at[idx]

    def body(*refs):
        ins, outs = refs[:n_in], refs[n_in:n_in + n_out]
        send_sems, recv_sems, local_sems = refs[n_in + n_out:]
        me = _coords()
        sends, recvs = [], []
        for k, (flip, ii, src_at, oi, dst_at) in enumerate(remote):
            peer = (me[0] ^ flip[0], me[1] ^ flip[1], me[2] ^ flip[2])
            src = at(ins[ii], src_at(me, peer))
            sends.append(pltpu.make_async_remote_copy(
                src_ref=src, dst_ref=at(outs[oi], dst_at(me)), send_sem=send_sems.at[k], recv_sem=recv_sems.at[k],
                device_id=peer, device_id_type=MESH))
            recvs.append(pltpu.make_async_remote_copy(
                src_ref=src, dst_ref=at(outs[oi], dst_at(peer)), send_sem=send_sems.at[k], recv_sem=recv_sems.at[k],
                device_id=peer, device_id_type=MESH))
        locs = [pltpu.make_async_copy(at(ins[ii], src_at(me)), at(outs[oi], dst_at(me)), local_sems.at[k])
                for k, (ii, src_at, oi, dst_at) in enumerate(local)]
        for cp in locs + sends:
            cp.start()
        for cp in recvs:
            cp.wait_recv()
        for cp in sends:
            cp.wait_send()
        for cp in locs:
            cp.wait()

    hbm = pl.BlockSpec(memory_space=pl.ANY)
    return pl.pallas_call(
        body, out_shape=list(out_shapes), in_specs=[hbm] * n_in, out_specs=[hbm] * n_out,
        scratch_shapes=[pltpu.SemaphoreType.DMA((n_rem,)), pltpu.SemaphoreType.DMA((n_rem,)),
                        pltpu.SemaphoreType.DMA((max(n_loc, 1),))],
        input_output_aliases=aliases or {}, name=name)(*arrays)


ALL_FLIPS = [(dx, dy, dc) for dx in (0, 1) for dy in (0, 1) for dc in (0, 1)][1:]
CHIP_FLIPS = [(1, 0, 0), (0, 1, 0), (1, 1, 0)]
CORE_FLIP = (0, 0, 1)


def _dev_index(p):
    return 4 * p[0] + 2 * p[1] + p[2]


def _chip_index(p):
    return 2 * p[0] + p[1]


def _gather(xs, flips, index, n, name):
    arrays = [x[None] for x in xs]
    outs = [jax.ShapeDtypeStruct((n,) + x.shape, x.dtype) for x in xs]
    remote = [(f, a, lambda me, peer: (0,), a, lambda s: (index(s),)) for a in range(len(xs)) for f in flips]
    local = [(a, lambda me: (0,), a, lambda me: (index(me),)) for a in range(len(xs))]
    return exchange(arrays, outs, remote, local, name)


def allgather_devices(x, name):
    return _gather([x], ALL_FLIPS, _dev_index, N_DEV, name)[0]


def allgather_chips(xs, name):
    return _gather(xs, CHIP_FLIPS, _chip_index, N_CHIP, name)


def gather_halves(xs, name):
    n = len(xs)
    nk = n * len(CHIP_FLIPS)

    def body(*refs):
        ins, outs = refs[:n], refs[n:2 * n]
        ici_send, ici_recv, d2d_send, d2d_recv = refs[2 * n:]
        me = _coords()
        sibling = (me[0], me[1], 1 - me[2])
        first, passed, landed = [], [], []
        for a in range(n):
            half = ins[a].shape[0] // 2
            mine = ins[a].at[pl.ds(pl.multiple_of(me[2] * half, 16), half)]
            for j, flip in enumerate(CHIP_FLIPS):
                k = a * len(CHIP_FLIPS) + j
                peer = (me[0] ^ flip[0], me[1] ^ flip[1], me[2])
                first.append(pltpu.make_async_remote_copy(
                    src_ref=mine, dst_ref=outs[a].at[_chip_index(me), me[2]], send_sem=ici_send.at[k],
                    recv_sem=ici_recv.at[k], device_id=peer, device_id_type=MESH))
                arrived = outs[a].at[_chip_index(peer), me[2]]
                landed.append(pltpu.make_async_remote_copy(
                    src_ref=mine, dst_ref=arrived, send_sem=ici_send.at[k], recv_sem=ici_recv.at[k],
                    device_id=peer, device_id_type=MESH))
                passed.append(pltpu.make_async_remote_copy(
                    src_ref=arrived, dst_ref=arrived, send_sem=d2d_send.at[k], recv_sem=d2d_recv.at[k],
                    device_id=sibling, device_id_type=MESH))
        for cp in first:
            cp.start()
        for k in range(nk):
            landed[k].wait_recv()
            passed[k].start()
        for a in range(n):
            for j, flip in enumerate(CHIP_FLIPS):
                k = a * len(CHIP_FLIPS) + j
                peer_chip = _chip_index((me[0] ^ flip[0], me[1] ^ flip[1]))
                from_sibling = outs[a].at[peer_chip, 1 - me[2]]
                pltpu.make_async_remote_copy(
                    src_ref=from_sibling, dst_ref=from_sibling, send_sem=d2d_send.at[k], recv_sem=d2d_recv.at[k],
                    device_id=sibling, device_id_type=MESH).wait_recv()
        for cp in first + passed:
            cp.wait_send()

    hbm = pl.BlockSpec(memory_space=pl.ANY)
    return pl.pallas_call(
        body, out_shape=[jax.ShapeDtypeStruct((N_CHIP, 2, x.shape[0] // 2, x.shape[1]), x.dtype) for x in xs],
        in_specs=[hbm] * n, out_specs=[hbm] * n,
        scratch_shapes=[pltpu.SemaphoreType.DMA((nk,)) for _ in range(4)], name=name)(*xs)


HBM_SPEC = pl.BlockSpec(memory_space=pltpu.HBM)
SEM_SPEC = pl.BlockSpec(memory_space=pltpu.SEMAPHORE)
DATAFLOW = pltpu.SideEffectType.DATAFLOW_SIDE_EFFECTING


def _at(ref, idx):
    return ref if idx is None else ref.at[idx]


def _peer(me, flip):
    return (me[0] ^ flip[0], me[1] ^ flip[1], me[2] ^ flip[2])


def exchange_start(arrays, land_shapes, remote, name, after=None):
    n_in, n_out, nk = len(arrays), len(land_shapes), len(remote)
    after = list(after or [])
    n_after = len(after)

    def body(*refs):
        srcs, lands = refs[:n_in], refs[n_in:n_in + n_out]
        first_out = n_in + n_out + n_after
        send_sems, recv_sems, token = refs[first_out], refs[first_out + 1], refs[-1]
        me = _coords()
        for k, (flip, ii, src_at, oi, dst_at) in enumerate(remote):
            peer = _peer(me, flip)
            pltpu.make_async_remote_copy(
                src_ref=_at(srcs[ii], src_at(me, peer)), dst_ref=_at(lands[oi], dst_at(me)), send_sem=send_sems.at[k],
                recv_sem=recv_sems.at[k], device_id=peer, device_id_type=MESH).start()
        token[...] = jnp.zeros_like(token)

    lands = [lax.empty(s.shape, s.dtype) for s in land_shapes]
    bufs = list(arrays) + lands
    out = pl.pallas_call(
        body, name=name,
        out_shape=(pltpu.SemaphoreType.DMA((nk,)), pltpu.SemaphoreType.DMA((nk,)),
                   *[pltpu.HBM(b.shape, b.dtype) for b in bufs], jax.ShapeDtypeStruct((8, 128), F32)),
        in_specs=[HBM_SPEC] * len(bufs) + [pl.BlockSpec(memory_space=pl.ANY)] * n_after,
        out_specs=(SEM_SPEC, SEM_SPEC, *[HBM_SPEC] * len(bufs), pl.BlockSpec(memory_space=pltpu.VMEM)),
        input_output_aliases={a: 2 + a for a in range(len(bufs))},
        compiler_params=pltpu.CompilerParams(has_side_effects=DATAFLOW),
    )(*[pltpu.with_memory_space_constraint(b, pltpu.HBM) for b in bufs], *after)
    flight = (out[0], out[1], list(out[2:2 + n_in]), list(out[2 + n_in:2 + n_in + n_out]), remote)
    return flight, out[-1]


def exchange_wait(flight, after, name):
    send_sems, recv_sems, arrays, lands, remote = flight
    n_in, n_out = len(arrays), len(lands)
    after = list(after) if isinstance(after, (list, tuple)) else [after]

    def body(*refs):
        srcs, lnds = refs[:n_in], refs[n_in:n_in + n_out]
        s_sems, r_sems = refs[n_in + n_out], refs[n_in + n_out + 1]
        me = _coords()
        for k, (flip, ii, src_at, oi, dst_at) in enumerate(remote):
            peer = _peer(me, flip)
            copy = pltpu.make_async_remote_copy(
                src_ref=_at(srcs[ii], src_at(me, peer)), dst_ref=_at(lnds[oi], dst_at(peer)), send_sem=s_sems.at[k],
                recv_sem=r_sems.at[k], device_id=peer, device_id_type=MESH)
            copy.wait_send()
            copy.wait_recv()

    bufs = list(arrays) + list(lands)
    out = pl.pallas_call(
        body, name=name,
        out_shape=tuple(pltpu.HBM(b.shape, b.dtype) for b in bufs),
        in_specs=[HBM_SPEC] * len(bufs) + [SEM_SPEC, SEM_SPEC] + [pl.BlockSpec(memory_space=pl.ANY)] * len(after),
        out_specs=tuple([HBM_SPEC] * len(bufs)),
        input_output_aliases={a: a for a in range(len(bufs))},
        compiler_params=pltpu.CompilerParams(has_side_effects=DATAFLOW),
    )(*bufs, send_sems, recv_sems, *after)
    return list(out[:n_in]), list(out[n_in:])


def _half_tile(h, cd):
    return h if h * cd * 4 <= (1 << 20) else math.gcd(512, h)


def pair_add(g, got, core, out_dtype, name):
    _, _, h, cd = g.shape
    th = _half_tile(h, cd)

    def body(c_ref, g_ref, got_ref, o_ref):
        o_ref[0] = (g_ref[0, 0] + got_ref[0]).astype(o_ref.dtype)

    return pl.pallas_call(
        body, out_shape=jax.ShapeDtypeStruct((N_CHIP, h, cd), out_dtype),
        grid_spec=pltpu.PrefetchScalarGridSpec(
            num_scalar_prefetch=1, grid=(N_CHIP, h // th),
            in_specs=[pl.BlockSpec((1, 1, th, cd), lambda q, i, c: (q, c[0], i, 0)),
                      pl.BlockSpec((1, th, cd), lambda q, i, c: (q, i, 0))],
            out_specs=pl.BlockSpec((1, th, cd), lambda q, i, c: (q, i, 0))),
        compiler_params=_params(("parallel", "parallel"), VMEM_LIMIT), name=name)(core, g, got)


def sum_chips(parts, sums, place, name):
    _, h, cd = parts.shape
    th = _half_tile(h, cd)

    def body(pc_ref, p_ref, own_ref, o_ref):
        acc = None
        for q in range(N_CHIP):
            term = jnp.where(pc_ref[1] == q, own_ref[0], p_ref[q]).astype(F32)
            acc = term if acc is None else acc + term
        o_ref[0] = acc

    return pl.pallas_call(
        body, out_shape=jax.ShapeDtypeStruct((2, h, cd), F32),
        grid_spec=pltpu.PrefetchScalarGridSpec(
            num_scalar_prefetch=1, grid=(h // th,),
            in_specs=[pl.BlockSpec((N_CHIP, th, cd), lambda i, pc: (0, i, 0)),
                      pl.BlockSpec((1, th, cd), lambda i, pc: (pc[1], i, 0))],
            out_specs=pl.BlockSpec((1, th, cd), lambda i, pc: (pc[0], i, 0))),
        compiler_params=_params(("parallel",), VMEM_LIMIT), name=name)(place, parts, sums)


def to_segments(a, n_ctx):
    def one(p):
        n = p.shape[0]
        return p.reshape(N_SEG, n // N_SEG, -1).transpose(1, 0, 2).reshape(n, -1)
    return jnp.concatenate([one(a[:n_ctx]), one(a[n_ctx:])], axis=0) if n_ctx else one(a)


def rows_to_segments(a, n_ctx, name):
    n, d = a.shape
    tj = n_ctx // N_SEG
    per_seg = (n - n_ctx) // N_SEG // tj
    assert n_ctx % N_SEG == 0 and (n - n_ctx) % (N_SEG * tj) == 0

    def body(*refs):
        out_ref, slabs = refs[N_SEG:]
        for c in range(d // HEAD_LANES):
            cols = slice(c * HEAD_LANES, (c + 1) * HEAD_LANES)
            for seg in range(N_SEG):
                slabs[c, pl.ds(seg, tj, stride=N_SEG), :] = refs[seg][:, cols]
            out_ref[:, cols] = slabs[c]

    def seg_spec(seg):
        return pl.BlockSpec((tj, d), lambda i: (jnp.where(i == 0, seg, N_SEG + seg * per_seg + i - 1), 0))

    return pl.pallas_call(
        body, out_shape=jax.ShapeDtypeStruct((n, d), a.dtype), grid=(1 + per_seg,),
        in_specs=[seg_spec(seg) for seg in range(N_SEG)], out_specs=pl.BlockSpec((N_SEG * tj, d), lambda i: (i, 0)),
        scratch_shapes=[pltpu.VMEM((d // HEAD_LANES, N_SEG * tj, HEAD_LANES), a.dtype)],
        compiler_params=_params(("parallel",), VMEM_LIMIT), name=name)(*[a] * N_SEG)


def rope_tables(n_ctx, n_lat):
    f32 = np.float32
    rows = n_lat // GRID_W
    row = np.repeat(np.arange(rows), GRID_W).astype(f32)
    col = np.tile(np.arange(GRID_W), rows).astype(f32)
    d = QK_ROPE_DIM // 2
    inv = (f32(1.0) / np.power(f32(ROPE_THETA), np.arange(0, d, 2, dtype=f32) / f32(d))).astype(f32)
    ang = np.concatenate([row[:, None] * inv[None, :], col[:, None] * inv[None, :]], axis=1).astype(f32)
    cos = np.concatenate([np.ones((n_ctx, d), f32), np.cos(ang)], axis=0)
    sin = np.concatenate([np.zeros((n_ctx, d), f32), np.sin(ang)], axis=0)
    q = QK_ROPE_DIM // 4
    T = n_ctx + n_lat
    ones, zeros = np.ones((T, QK_NOPE_DIM), f32), np.zeros((T, QK_NOPE_DIM), f32)
    tail, z8 = np.zeros((T, HEAD_LANES - QK_DIM), f32), np.zeros((T, q), f32)
    cr, cc, sr, sc = cos[:, :q], cos[:, q:], sin[:, :q], sin[:, q:]
    cos_t = np.concatenate([ones, cr, cr, cc, cc, tail], axis=1)
    sin_next = np.concatenate([zeros, -sr, z8, -sc, z8, tail], axis=1)
    sin_prev = np.concatenate([zeros, z8, sr, z8, sc, tail], axis=1)
    return tuple(jnp.asarray(t, F32) for t in (cos_t, sin_next, sin_prev))


def pad_heads(w, used):
    k = w.shape[0]
    return jnp.pad(w.reshape(k, MLA_HEADS, used), ((0, 0), (0, 0), (0, HEAD_LANES - used))).reshape(k, -1)


def unpad_heads(w, used):
    k = w.shape[0]
    return w.reshape(k, MLA_HEADS, HEAD_LANES)[:, :, :used].reshape(k, MLA_HEADS * used)


def rotary_spread():
    lane = np.arange(MLA_HEADS * HEAD_LANES) % HEAD_LANES
    return jnp.asarray(lane[None, :] == (QK_NOPE_DIM + np.arange(QK_ROPE_DIM))[:, None], BF16)


def s5_discretise(a_re, a_im, log_step, b_re, b_im):
    dt = jnp.exp(log_step)[:, None]
    mag = jnp.exp(a_re * dt)
    lb_re = mag * jnp.cos(a_im * dt)
    lb_im = mag * jnp.sin(a_im * dt)
    den = a_re * a_re + a_im * a_im
    nr = lb_re - 1.0
    f_re = ((nr * a_re + lb_im * a_im) / den)[:, None, :]
    f_im = ((lb_im * a_re - nr * a_im) / den)[:, None, :]
    return lb_re, lb_im, f_re * b_re - f_im * b_im, f_re * b_im + f_im * b_re


def s5_block_weights(lb_re, lb_im, bb_re, bb_im, c_re, c_im):
    eye = jnp.eye(GROUPS_PER_BLOCK, dtype=F32)
    lam = jnp.stack([lb_re.reshape(1, S5_LANES), lb_im.reshape(1, S5_LANES)])

    def b_blocks(bb):
        t = bb.reshape(N_BLOCKS, GROUPS_PER_BLOCK, S5_GROUP, S5_STATE)
        return jnp.einsum("bgcp,gh->bgchp", t, eye).reshape(N_BLOCKS, BLK_CH, BLK_ST).astype(BF16)

    def c_blocks(cc):
        t = cc.reshape(N_BLOCKS, GROUPS_PER_BLOCK, S5_GROUP, S5_STATE)
        return jnp.einsum("bgcp,gh->bgphc", t, eye).reshape(N_BLOCKS, BLK_ST, BLK_CH).astype(BF16)

    return lam, b_blocks(bb_re), b_blocks(bb_im), c_blocks(c_re), c_blocks(c_im)


def b_block_diag(db):
    t = db.reshape(N_BLOCKS, GROUPS_PER_BLOCK, S5_GROUP, GROUPS_PER_BLOCK, S5_STATE)
    return jnp.einsum("bgchp,gh->bgcp", t, jnp.eye(GROUPS_PER_BLOCK, dtype=F32)).reshape(S5_GROUPS, S5_GROUP, S5_STATE)


def c_block_diag(dc):
    t = dc.reshape(N_BLOCKS, GROUPS_PER_BLOCK, S5_STATE, GROUPS_PER_BLOCK, S5_GROUP)
    return jnp.einsum("bgphc,gh->bgcp", t, jnp.eye(GROUPS_PER_BLOCK, dtype=F32)).reshape(S5_GROUPS, S5_GROUP, S5_STATE)


def conj(a):
    return jnp.stack([a[0], -a[1]])


def _narrow(shape):
    return len(shape) >= 2 and shape[-1] < min(HEAD_LANES, shape[-2])


def _stored(a):
    return jnp.swapaxes(a, -1, -2) if _narrow(a.shape) else a


def _stored_shape(shape):
    return tuple(shape[:-2]) + (shape[-1], shape[-2]) if _narrow(shape) else tuple(shape)


def _from_stored(a, shape):
    return jnp.swapaxes(a, -1, -2) if _narrow(shape) else a


PACK_TILE = 16 * 128


def pack_flat(parts, dtype, multiple=PACK_TILE):
    flat = [p.reshape(-1).astype(dtype) for p in parts]
    sizes = [f.shape[0] for f in flat]
    total = sum(sizes)
    pad = (-total) % multiple
    if pad:
        flat.append(jnp.zeros((pad,), dtype))
    offs = np.cumsum([0] + sizes)[:-1].tolist()
    return jnp.concatenate(flat).reshape(-1, 128), offs


def unpack_flat(buf, offs, shapes):
    flat = buf.reshape(-1)
    return [flat[o:o + int(np.prod(s))].reshape(s) for o, s in zip(offs, shapes)]


def s5_forward(p1, n_ctx, dirs):
    saved = []
    y = None
    ctx_rows, lat_rows = (0, n_ctx), (n_ctx, p1.shape[0] - n_ctx)
    zeros_tile = jnp.zeros((2, N_SEG, S5_LANES), F32)
    zeros_row = jnp.zeros((2, 1, S5_LANES), F32)
    for k, (lam, b_re, b_im, c_re, c_im) in enumerate(dirs):
        rev = k == 1
        last = 0 if rev else N_SEG - 1
        _, _, fin = s5_scan(p1, b_re, b_im, lam, zeros_tile, reverse=rev, rows=ctx_rows, name=f"s5_ctx_finals{k}")
        carry_c = s5_chain(fin, zeros_row, lam, n_ctx // N_SEG, rev, name=f"s5_ctx_chain{k}")
        _, ck_c, fin_c = s5_scan(p1, b_re, b_im, lam, carry_c, reverse=rev, want_ckpt=True, rows=ctx_rows,
                                 name=f"s5_ctx_scan{k}")
        s0 = fin_c[:, last:last + 1, :]
        _, _, fin = s5_scan(p1, b_re, b_im, lam, zeros_tile, reverse=rev, rows=lat_rows, name=f"s5_lat_finals{k}")
        carry_l = s5_chain(fin, s0, lam, lat_rows[1] // N_SEG, rev, name=f"s5_lat_chain{k}")
        y, ck_l, _ = s5_scan(p1, b_re, b_im, lam, carry_l, reverse=rev, c_re=c_re, c_im=c_im, add=y,
                             want_ckpt=True, rows=lat_rows, name=f"s5_lat_scan{k}")
        saved.append((ck_c, ck_l))
    return y, saved


def s5_backward(dy_l, du_extra_l, p1, n_ctx, dirs, saved):
    n_lat = p1.shape[0] - n_ctx
    zeros_tile = jnp.zeros((2, N_SEG, S5_LANES), F32)
    zeros_row = jnp.zeros((2, 1, S5_LANES), F32)
    du_l, du_c = du_extra_l, None
    grads = []
    for k, (lam, b_re, b_im, c_re, c_im) in enumerate(dirs):
        rev = k == 1
        lam_c = conj(lam)
        ck_c, ck_l = saved[k]
        first = N_SEG - 1 if rev else 0
        _, _, fin = s5_scan(dy_l, c_re, c_im, lam_c, zeros_tile, reverse=not rev, adjoint=True,
                            name=f"s5_lat_adj_finals{k}")
        carry = s5_chain(fin, zeros_row, lam_c, n_lat // N_SEG, not rev, name=f"s5_lat_adj_chain{k}")
        whole = k == len(dirs) - 1
        du_l, dlam_l, dbr_l, dbi_l, dcr_l, dci_l, fin_a = s5_grads(
            dy_l, p1, ck_l, b_re, b_im, c_re, c_im, lam, carry, reverse=rev, add=du_l, u_off=n_ctx,
            du_rows=(p1.shape[0], n_ctx) if whole else None, name=f"s5_lat_grads{k}")
        g0 = fin_a[:, first:first + 1, :]
        carry = s5_chain(zeros_tile, g0, lam_c, n_ctx // N_SEG, not rev, name=f"s5_ctx_adj_chain{k}")
        du_c, dlam_c, dbr_c, dbi_c, _, _, _ = s5_grads(
            None, p1, ck_c, b_re, b_im, c_re, c_im, lam, carry, reverse=rev, add=du_c, n_rows=n_ctx,
            du_rows=(p1.shape[0], 0) if whole else None, du_into=du_l if whole else None,
            sums_from=(dlam_l, dbr_l, dbi_l), name=f"s5_ctx_grads{k}")
        grads.append((jnp.sum(dlam_c, axis=1), b_block_diag(dbr_c), b_block_diag(dbi_c),
                      c_block_diag(dcr_l), c_block_diag(dci_l)))
    return du_c, grads


def local_step(x, ctx, target, mod, w, late=None, reducer=None):
    L, Lc = x.shape[0], ctx.shape[0]
    T = L + Lc
    assert L % Lc == 0 and Lc % (2 * N_SEG) == 0 and L % GRID_W == 0
    D = D_MODEL
    X0 = (ctx, x)

    def mod_of(i, j):
        return mod[i, :, j, :][:, None, :]

    def vec(v):
        return v.reshape(1, 1, -1).astype(F32)

    g0 = vec(w["norm_g"][0])
    H0, p0 = norm_proj(X0, g0, mod_of(0, 1), mod_of(0, 0), w["mla_w_in"], Lc, "l0_norm_in")
    cq = Rows(p0, Q_LORA_RANK, col_blk=D // Q_LORA_RANK)
    ckv = Rows(p0, KV_LORA_RANK, col_blk=(D + Q_LORA_RANK) // KV_LORA_RANK)
    kr = Rows(p0, HEAD_LANES, col_blk=(D + Q_LORA_RANK + KV_LORA_RANK) // HEAD_LANES)
    qng, kvng = vec(w["mla_q_norm"]), vec(w["mla_kv_norm"])
    tabs = rope_tables(Lc, L)
    spread = jnp.pad(rotary_spread(), ((0, HEAD_LANES - QK_ROPE_DIM), (0, 0)))
    if late is not None:
        w = {**w, **late["qkv"](p0)}
    w_uq_p = pad_heads(w["mla_w_uq"], QK_DIM)
    w_ukv3 = w["mla_w_ukv"].reshape(KV_LORA_RANK, MLA_HEADS, QK_NOPE_DIM + V_HEAD_DIM)
    w_kn_p = pad_heads(w_ukv3[:, :, :QK_NOPE_DIM].reshape(KV_LORA_RANK, -1), QK_NOPE_DIM)
    w_v = w_ukv3[:, :, QK_NOPE_DIM:].reshape(KV_LORA_RANK, -1)
    qb, qn = q_heads(cq, qng, w_uq_p, tabs)
    kb, vb, kvn = kv_heads(ckv, kvng, w_kn_p, w_v, kr, spread, tabs)
    o, lse, qs = attn_fwd(qb, kb, vb, Lc)
    if late is not None:
        w = {**w, **late["out"](o)}
    X1, og, out0 = mla_post_fwd(o, p0, X0, mod_of(0, 2), w["mla_w_out"], Lc)

    if late is not None:
        w = {**w, **late["l1"](X1)}
    X1p = rows_to_segments(X1, Lc, "l1_to_segments")
    tgt_p = to_segments(target, 0)
    g1 = vec(w["norm_g"][1])
    H1, p1 = norm_proj(X1p, g1, mod_of(1, 1), mod_of(1, 0), w["s5_w_in"], Lc, "l1_norm_in")
    disc_fn = lambda *a: tuple(zip(*[s5_discretise(a[0][k], a[1][k], a[2][k], a[3][k], a[4][k]) for k in range(2)]))
    disc, disc_vjp = jax.vjp(disc_fn, w["s5_a_re"], w["s5_a_im"], w["s5_log_step"], _stored(w["s5_b_re"]),
                             _stored(w["s5_b_im"]))
    dirs = [s5_block_weights(disc[0][k], disc[1][k], disc[2][k], disc[3][k], w["s5_c_re"][k], w["s5_c_im"][k])
            for k in range(2)]
    y_ssm, s5_saved = s5_forward(p1, Lc, dirs)

    row = lambda v: v.reshape(1, D).astype(F32)
    (lvec, dX2, d_yssm, d_u_act, d_z1, d_fg, d_gt1, d_bg, d_d, gw_glu, gw_out) = s5_tail(
        y_ssm, p1, X1p, tgt_p, Lc, row(w["s5_d"]), row(w["s5_b_glu"]), mod[1, 1:2, 2, :], row(w["final_g"]),
        w["s5_w_glu"], w["s5_w_out"])
    loss = jnp.sum(lvec)
    gw = {"final_g": d_fg.reshape(D), "s5_b_glu": d_bg.reshape(D), "s5_d": d_d.reshape(D),
          "s5_w_glu": gw_glu, "s5_w_out": gw_out}
    dmod = {}

    du_p, s5_g = s5_backward(d_yssm, d_u_act, p1, Lc, dirs, s5_saved)
    d_disc = tuple(tuple(s5_g[k][j - 1].reshape(disc[j][k].shape) if j >= 2 else
                         s5_g[k][0][j].reshape(disc[j][k].shape) for k in range(2)) for j in range(4))
    gw["s5_a_re"], gw["s5_a_im"], gw["s5_log_step"], d_bt_re, d_bt_im = disc_vjp(d_disc)
    gw["s5_b_re"], gw["s5_b_im"] = jnp.swapaxes(d_bt_re, -1, -2), jnp.swapaxes(d_bt_im, -1, -2)
    gw["s5_c_re"] = jnp.stack([s5_g[0][3], s5_g[1][3]])
    gw["s5_c_im"] = jnp.stack([s5_g[0][4], s5_g[1][4]])
    gw["s5_w_in"] = mm_tn(H1, du_p, name="l1_in_dw", b_more=d_z1)
    if reducer is not None:
        g1 = g1 + reducer["l1"][0]({n: gw.pop(n) for n in LAYER1_MATS})[0, 0]
    d_X1, d_g1, d_sc1, d_sh1 = norm_proj_bwd([(du_p, 0, 0), (d_z1, D, Lc)], w["s5_w_in"], X1p, g1, mod_of(1, 1),
                                             mod_of(1, 0), dX2, Lc, "l1_norm_in_bwd", dx_from_segments=True)
    d_gt1_full = jnp.concatenate([jnp.zeros((1, 1, D), F32), d_gt1[None]], axis=0)
    dmod[1] = (d_sh1, d_sc1, d_gt1_full)

    d_o, d_z0, d_gt0, gw["mla_w_out"] = mla_post_bwd(d_X1, out0, og, o, p0, mod_of(0, 2), w["mla_w_out"], Lc)
    if reducer is not None:
        started = reducer["l1"][1](d_o)[0, 0] + reducer["out"][0]({"mla_w_out": gw.pop("mla_w_out")})[0, 0]
        tabs = (tabs[0] + started,) + tabs[1:]
    d_q, dos = attn_bwd_dq(qb, kb, vb, o, d_o, lse, tabs, Lc)
    dk_p, d_v = attn_bwd_dkv(qs, kb, vb, dos, Lc)
    if reducer is not None:
        tabs = (tabs[0] + reducer["out"][1](d_q)[0, 0],) + tabs[1:]
    d_k, d_kr = heads_unrope(dk_p, tabs, spread,
                             name="l0_k_unrope")
    d_qn = mm_nt(d_q, w_uq_p, name="l0_uq_dx")
    gw["mla_w_uq"] = unpad_heads(mm_tn(qn, d_q, name="l0_uq_dw"), QK_DIM)
    d_kvn = mm_nt(d_k, w_kn_p, name="l0_ukn_dx") + mm_nt(d_v, w_v, name="l0_uv_dx")
    dw_kn = unpad_heads(mm_tn(kvn, d_k, name="l0_ukn_dw"), QK_NOPE_DIM).reshape(KV_LORA_RANK, MLA_HEADS, QK_NOPE_DIM)
    dw_v = mm_tn(kvn, d_v, name="l0_uv_dw").reshape(KV_LORA_RANK, MLA_HEADS, V_HEAD_DIM)
    gw["mla_w_ukv"] = jnp.concatenate([dw_kn, dw_v], axis=-1).reshape(KV_LORA_RANK, -1)
    d_cq, d_qng = rowwise_bwd(f_rms, [cq], [qng], [d_qn], [0], [0], T, 0, "l0_qnorm_bwd")
    d_ckv, d_kvng = rowwise_bwd(f_rms, [ckv], [kvng], [d_kvn], [0], [0], T, 0, "l0_kvnorm_bwd")
    gw["mla_q_norm"] = d_qng.reshape(-1)
    gw["mla_kv_norm"] = d_kvng.reshape(-1)
    o_cq, o_ckv = D, D + Q_LORA_RANK
    o_kr = o_ckv + KV_LORA_RANK
    d_head = jnp.concatenate([d_cq, d_ckv, d_kr], axis=1)
    gw["mla_w_in"] = jnp.concatenate([mm_tn(H0, d_head, name="l0_in_dw_head")[:, :P0_HEAD],
                                      mm_tn(H0, d_z0, name="l0_in_dw_z")], axis=1)
    dx, d_g0, d_sc0, d_sh0 = norm_proj_bwd(
        [(d_z0, 0, 0), (d_cq, o_cq, 0), (d_ckv, o_ckv, 0), (d_kr, o_kr, 0)], w["mla_w_in"], X0, g0, mod_of(0, 1),
        mod_of(0, 0), d_X1, Lc, "l0_norm_in_bwd", latent_dx_only=True)
    dmod[0] = (d_sh0, d_sc0, d_gt0)
    gw["norm_g"] = jnp.stack([d_g0.reshape(D), d_g1.reshape(D)])
    dmod_arr = jnp.stack([jnp.stack([dmod[i][j][:, 0, :] for j in range(3)], axis=1) for i in range(2)])
    ready = {**reducer["l1"][2](dx), **reducer["out"][2](dx)} if reducer is not None else {}
    return loss, dx, dmod_arr, gw, ready


SHARDED = {
    "mla_w_in": 1, "mla_w_uq": 1, "mla_w_ukv": 1, "mla_w_out": 0,
    "s5_w_in": 1, "s5_w_glu": 0, "s5_w_out": 0, "s5_d": 0, "s5_b_glu": 0,
}
SHARDED_MATS = ["mla_w_in", "mla_w_uq", "mla_w_ukv", "mla_w_out", "s5_w_in", "s5_w_glu", "s5_w_out"]
SHARDED_VECS = ["s5_d", "s5_b_glu"]
REPLICATED = ["norm_g", "mla_q_norm", "mla_kv_norm", "s5_a_re", "s5_a_im", "s5_log_step", "s5_b_re", "s5_b_im",
              "s5_c_re", "s5_c_im", "final_g"]
WEIGHT_ORDER = ["c_ctx", "ada_w", "ada_b", "norm_g", "mla_w_in", "mla_q_norm", "mla_w_uq", "mla_kv_norm", "mla_w_ukv",
                "mla_w_out", "s5_w_in", "s5_a_re", "s5_a_im", "s5_log_step", "s5_b_re", "s5_b_im", "s5_c_re", "s5_c_im",
                "s5_d", "s5_w_glu", "s5_b_glu", "s5_w_out", "final_g"]


P0_HEAD = Q_LORA_RANK + KV_LORA_RANK + QK_ROPE_DIM


P0_WIDTH = 1536


def w_in_to_kernel_order(w):
    pad = jnp.zeros((w.shape[0], P0_WIDTH - w.shape[1]), w.dtype)
    return jnp.concatenate([w[:, P0_HEAD:], w[:, :P0_HEAD], pad], axis=1)


LAYER0_MATS = ["mla_w_in", "mla_w_uq", "mla_w_ukv", "mla_w_out"]
LAYER1_MATS = ["s5_w_in", "s5_w_glu", "s5_w_out"]


def _whole_matrices(names, own_blocks, gathered):
    chip = _chip_index(_coords())
    full = {}
    for n, own, o in zip(names, own_blocks, gathered):
        slot = lax.broadcasted_iota(jnp.int32, (N_CHIP, 1, 1), 0)
        o = jnp.where(slot == chip, own[None], o.reshape((N_CHIP,) + own.shape))
        full[n] = o.reshape(-1, o.shape[-1]) if SHARDED[n] == 0 else o.transpose(1, 0, 2).reshape(o.shape[1], -1)
    return full


FIRST_MATS = ["mla_w_in"]
LATER_GROUPS = {"qkv": ["mla_w_uq", "mla_w_ukv"], "out": ["mla_w_out"], "l1": LAYER1_MATS}


def gather_weights(ws):
    mats = [ws[n].astype(BF16) for n in FIRST_MATS]
    full = _whole_matrices(FIRST_MATS, mats, gather_halves(mats, "gather_weights"))
    full["mla_w_in"] = w_in_to_kernel_order(full["mla_w_in"])
    return full


def gather_weights_behind(ws, after):
    token, finish = 0.0, {}
    for group, names in LATER_GROUPS.items():
        mats = [ws[n].astype(BF16) for n in names]
        flight, tok = exchange_start(
            mats, [jax.ShapeDtypeStruct((N_CHIP,) + m.shape, m.dtype) for m in mats],
            [(f, a, lambda me, peer: None, a, lambda s: (_chip_index(s),))
             for a in range(len(mats)) for f in CHIP_FLIPS], f"gather_{group}_start", after=after)
        after = [tok]
        token = token + tok[0, 0]

        def finish_group(after_work, group=group, names=names, flight=flight):
            own, got = exchange_wait(flight, after_work, f"gather_{group}_wait")
            return _whole_matrices(names, own, got)

        finish[group] = finish_group
    return token, finish


def _grad_slots(gw, names):
    slots = []
    for n in names:
        g = gw[n]
        if SHARDED[n] == 0:
            slots.append(g.reshape(N_CHIP, 2, g.shape[0] // (2 * N_CHIP), g.shape[1]))
        else:
            k, n4 = g.shape
            slots.append(g.reshape(k, N_CHIP, n4 // N_CHIP).transpose(1, 0, 2)
                         .reshape(N_CHIP, 2, k // 2, n4 // N_CHIP))
    return slots


def _to_sibling_half(count):
    return [(CORE_FLIP, i, lambda me, peer: (slice(None), 1 - me[2]), i, lambda s: None) for i in range(count)]


def _to_chips(count):
    return [(f, i, lambda me, peer: (_chip_index(peer),), i, lambda s: (_chip_index(s),))
            for i in range(count) for f in CHIP_FLIPS]


def _place():
    me = _coords()
    return jnp.stack([me[2], _chip_index(me)]).astype(jnp.int32)


def reduce_behind(names, tag):
    state = {}
    count = len(names)

    def begin(gw):
        slots = _grad_slots(gw, names)
        lands = [jax.ShapeDtypeStruct((N_CHIP,) + s.shape[2:], F32) for s in slots]
        state["in"], token = exchange_start(slots, lands, _to_sibling_half(count), f"grads_{tag}_swap_in_start")
        return token

    def middle(after):
        slots, got = exchange_wait(state["in"], after, f"grads_{tag}_swap_in_wait")
        place = _place()
        sums = [pair_add(s, g, place[:1], BF16, f"grads_pair_{n}") for n, s, g in zip(names, slots, got)]
        lands = [jax.ShapeDtypeStruct(s.shape, s.dtype) for s in sums]
        state["out"], token = exchange_start(sums, lands, _to_chips(count), f"grads_{tag}_scatter_start")
        return token

    def end(after):
        sums, parts = exchange_wait(state["out"], after, f"grads_{tag}_scatter_wait")
        place = _place()
        return {n: sum_chips(p, s, place, f"grads_sum_{n}") for n, p, s in zip(names, parts, sums)}

    return begin, middle, end


def reduce_gradients(gw, ready_halves, also=None):
    me = _coords()
    place = _place()
    mat_names = [n for n in SHARDED_MATS if n not in ready_halves]
    slots = dict(zip(mat_names, _grad_slots(gw, mat_names)))
    gw = {**gw, **(also or {})}
    small_names = REPLICATED + SHARDED_VECS + list(also or {})
    small, small_offs = pack_flat([_stored(gw[n]).astype(F32) for n in small_names], F32, N_CHIP * 32 * 128)
    slots["small"] = small.reshape(N_CHIP, 2, -1, 128)
    names = list(slots)
    count = len(names)
    got = exchange([slots[n] for n in names],
                   [jax.ShapeDtypeStruct((N_CHIP,) + slots[n].shape[2:], F32) for n in names],
                   _to_sibling_half(count), [], "grads_swap_in")
    sums = [pair_add(slots[n], g, place[:1], F32 if n == "small" else BF16, f"grads_pair_{n}")
            for n, g in zip(names, got)]
    parts = exchange(sums, [jax.ShapeDtypeStruct(s.shape, s.dtype) for s in sums], _to_chips(count), [],
                     "grads_scatter")
    halves = {n: sum_chips(p, s, place, f"grads_sum_{n}") for n, p, s in zip(names, parts, sums)}
    halves.update(ready_halves)
    all_names = list(halves)
    fulls = exchange(
        [halves[n] for n in all_names], [jax.ShapeDtypeStruct(halves[n].shape, F32) for n in all_names],
        [(CORE_FLIP, i, lambda me, peer: (me[2],), i, lambda s: (s[2],)) for i in range(len(all_names))], [],
        "grads_swap_out", aliases={i: i for i in range(len(all_names))})
    out = {n: f.reshape(-1, f.shape[-1]) for n, f in zip(all_names, fulls)}
    quarter = out.pop("small")
    gather, token = exchange_start(
        [quarter], [jax.ShapeDtypeStruct((N_CHIP,) + quarter.shape, F32)],
        [(f, 0, lambda me, peer: None, 0, lambda s: (_chip_index(s),)) for f in CHIP_FLIPS],
        "grads_gather_small_start")

    def finish_small(after):
        (own,), (got_small,) = exchange_wait(gather, after, "grads_gather_small_wait")
        slot = lax.broadcasted_iota(jnp.int32, (N_CHIP, 1, 1), 0)
        small_all = jnp.where(slot == _chip_index(me), own[None], got_small)
        vals = unpack_flat(small_all, small_offs, [_stored_shape(gw[n].shape) for n in small_names])
        res = {}
        for n, v in zip(small_names, vals):
            v = _from_stored(v, gw[n].shape)
            if n in SHARDED_VECS:
                size = v.shape[0] // N_CHIP
                v = lax.dynamic_slice_in_dim(v, _chip_index(me) * size, size)
            res[n] = v
        return res

    return out, finish_small, token


def kernel(x, c, ctx, c_ctx, ada_w, ada_b, norm_g, mla_w_in, mla_q_norm, mla_w_uq, mla_kv_norm, mla_w_ukv, mla_w_out, s5_w_in, s5_a_re, s5_a_im, s5_log_step, s5_b_re, s5_b_im, s5_c_re, s5_c_im, s5_d, s5_w_glu, s5_b_glu, s5_w_out, final_g, loss_target, m_c_ctx, m_ada_w, m_ada_b, m_norm_g, m_mla_w_in, m_mla_q_norm, m_mla_w_uq, m_mla_kv_norm, m_mla_w_ukv, m_mla_w_out, m_s5_w_in, m_s5_a_re, m_s5_a_im, m_s5_log_step, m_s5_b_re, m_s5_b_im, m_s5_c_re, m_s5_c_im, m_s5_d, m_s5_w_glu, m_s5_b_glu, m_s5_w_out, m_final_g, v_c_ctx, v_ada_w, v_ada_b, v_norm_g, v_mla_w_in, v_mla_q_norm, v_mla_w_uq, v_mla_kv_norm, v_mla_w_ukv, v_mla_w_out, v_s5_w_in, v_s5_a_re, v_s5_a_im, v_s5_log_step, v_s5_b_re, v_s5_b_im, v_s5_c_re, v_s5_c_im, v_s5_d, v_s5_w_glu, v_s5_b_glu, v_s5_w_out, v_final_g):
    args = dict(locals())
    weights = {n: args[n] for n in WEIGHT_ORDER}
    D = D_MODEL
    xi, yi, ci = _coords()
    chip = 2 * xi + yi
    me = 4 * xi + 2 * yi + ci
    n_col = ada_w.shape[2]

    c_all = allgather_devices(jnp.pad(c, ((0, 7), (0, 0))), "gather_c")[:, 0, :]
    cond = jnp.concatenate([c_all, jnp.broadcast_to(c_ctx[None], (8, D))], axis=0)
    (s_cond,) = rowwise_fwd(lambda v: (_silu(v),), [cond], [], [D], [F32], 16, 0, "cond_silu")
    ada_rows = ada_w.reshape(2 * D, n_col)
    mod_cols = jnp.stack([mm_nn(s_cond, ada_rows, name=f"mod_proj{i}", b_blk=i) for i in range(2)])
    vec_tiles = [jnp.pad(weights[n][0].reshape(-1, 128), ((0, 6), (0, 0))) for n in SHARDED_VECS]
    mod_all, *vec_all = allgather_chips([mod_cols] + vec_tiles, "gather_mod")
    mod_all = mod_all.transpose(1, 2, 0, 3).reshape(2, 16, 3 * D) + ada_b[:, None, :]
    mine = lax.broadcasted_iota(jnp.int32, (1, 16, 1), 1) == me
    mod_l = jnp.sum(jnp.where(mine, mod_all, 0.0), axis=1)
    mod_c = mod_all[:, 8, :]
    mod = jnp.stack([mod_c.reshape(2, 3, D), mod_l.reshape(2, 3, D)], axis=1)

    w = gather_weights({n: weights[n][0] for n in FIRST_MATS})
    token, late = gather_weights_behind({n: weights[n][0] for n in SHARDED_MATS}, [w["mla_w_in"], mod])
    for n, v in zip(SHARDED_VECS, vec_all):
        w[n] = v[:, :2, :].reshape(-1)
    for n in ["norm_g", "final_g"]:
        w[n] = weights[n]
    for n in ["mla_q_norm", "mla_kv_norm", "s5_a_re", "s5_a_im", "s5_log_step", "s5_b_re", "s5_b_im",
              "s5_c_re", "s5_c_im"]:
        w[n] = weights[n][0]

    reducer = {"l1": reduce_behind(LAYER1_MATS, "l1"), "out": reduce_behind(["mla_w_out"], "out")}
    loss_me, dx, dmod, gw, ready = local_step(x[0], ctx[0], loss_target[0], mod + token, w, late,
                                              reducer)

    dmod_rows, loss_all = _gather([dmod.reshape(2, 2, 3 * D), jnp.broadcast_to(loss_me, (8, 128))],
                                  ALL_FLIPS, _dev_index, N_DEV, "gather_dmod")
    loss = functools.reduce(lambda s, d: s + loss_all[d, 0, 0], range(1, N_DEV), loss_all[0, 0, 0])
    dm = jnp.concatenate([dmod_rows[:, :, 1, :], dmod_rows[:, :, 0, :]], axis=0).transpose(1, 0, 2)
    g_ada_b = jnp.sum(dm, axis=1)
    dm_cols = lax.dynamic_slice_in_dim(dm, chip * n_col, n_col, axis=2)
    g_ada_w = jnp.stack([mm_tn(s_cond, dm_cols[i], name=f"mod_proj_dw{i}") for i in range(2)])
    dmc = jnp.sum(dm_cols[:, 8:, :], axis=1)
    dmc8 = jnp.broadcast_to(dmc[:, None, :], (2, 8, n_col))
    g_sc = (mm_nt(dmc8[0], ada_rows, name="mod_proj_dx0", b_rows=D, b_blk=0)[0]
            + mm_nt(dmc8[1], ada_rows, name="mod_proj_dx1", b_rows=D, b_blk=1)[0])
    g_silu_part = jnp.where(ci == 0, g_sc, 0.0)

    grads = {"ada_w": g_ada_w, "ada_b": g_ada_b}
    deltas, new_m, new_v = {}, {}, {}
    small = [n for n in WEIGHT_ORDER if weights[n].size < 50000]

    def update(n, after=None):
        shp = weights[n].shape
        rows = lambda a: _stored(a.reshape(shp)).reshape(-1, _stored_shape(shp)[-1])
        back = lambda a: _from_stored(a.reshape(_stored_shape(shp)), shp)
        d_, m_, v_ = adamw(rows(weights[n]), rows(grads[n]), rows(args["m_" + n]), rows(args["v_" + n]),
                           name=f"adamw_{n}", after=after)
        deltas[n], new_m[n], new_v[n] = back(d_), back(m_), back(v_)

    red, finish_small, small_started = reduce_gradients(gw, ready, {"silu_c_ctx": g_silu_part})
    update("ada_w", small_started)
    for n in SHARDED_MATS:
        grads[n] = red[n].reshape(weights[n].shape)
        update(n, small_started)
    red_small = finish_small([deltas[n] for n in ["ada_w"] + SHARDED_MATS])
    for n in REPLICATED + SHARDED_VECS:
        grads[n] = red_small[n].reshape(weights[n].shape)
    (g_c_ctx,) = rowwise_bwd(lambda v: (_silu(v),), [jnp.broadcast_to(c_ctx[None], (8, D))], [],
                             [jnp.broadcast_to(red_small["silu_c_ctx"][None], (8, D))], [0], [], 8, 0, "cond_silu_bwd")
    grads["c_ctx"] = g_c_ctx[0]
    for n in WEIGHT_ORDER:
        if n not in small and n not in deltas:
            update(n)
    packs = []
    offs = None
    for src in (weights, grads, {n: args["m_" + n] for n in small}, {n: args["v_" + n] for n in small}):
        buf, offs = pack_flat([src[n] for n in small], F32)
        packs.append(buf)
    outs = adamw(*packs, name="adamw_small")
    for res, dst in zip(outs, (deltas, new_m, new_v)):
        for n, val in zip(small, unpack_flat(res, offs, [weights[n].shape for n in small])):
            dst[n] = val

    return (loss, dx[None], *[grads[n] for n in WEIGHT_ORDER], *[deltas[n] for n in WEIGHT_ORDER],
            *[new_m[n] for n in WEIGHT_ORDER], *[new_v[n] for n in WEIGHT_ORDER])
```

```python
import functools
import math

import jax
import jax.numpy as jnp
import numpy as np
from jax import lax
from jax.experimental import pallas as pl
from jax.experimental.pallas import tpu as pltpu

F32 = jnp.float32
BF16 = jnp.bfloat16

D_MODEL = 1024
GRID_W = 64
EPS = 1e-6
MLA_HEADS = 16
QK_NOPE_DIM = 64
QK_ROPE_DIM = 32
V_HEAD_DIM = 64
Q_LORA_RANK = 256
KV_LORA_RANK = 128
QK_DIM = QK_NOPE_DIM + QK_ROPE_DIM
SOFTMAX_SCALE = QK_DIM ** -0.5
ROPE_THETA = 10000.0
S5_GROUP = 16
S5_GROUPS = D_MODEL // S5_GROUP
S5_STATE = 64
S5_LANES = S5_GROUPS * S5_STATE
N_SEG = 8
GROUPS_PER_BLOCK = 8
N_BLOCKS = S5_GROUPS // GROUPS_PER_BLOCK
BLK_CH = GROUPS_PER_BLOCK * S5_GROUP
BLK_ST = GROUPS_PER_BLOCK * S5_STATE

ADAM_LR = 0.001
ADAM_B1 = 0.9
ADAM_B2 = 0.999
ADAM_EPS = 1e-08
ADAM_WD = 0.01
ADAM_STEP = 10

N_DEV = 8
N_CHIP = 4
MESH = pl.DeviceIdType.MESH
VMEM_LIMIT = 52 * 1024 * 1024
ROW_TILE = 256


def _params(sem=None, vmem=None):
    return pltpu.CompilerParams(dimension_semantics=sem, vmem_limit_bytes=vmem)


def mm_nn(a, b, out_dtype=F32, name="mm_nn", b_blk=0):
    M, K = a.shape
    N = b.shape[1]
    tm = math.gcd(ROW_TILE, M)

    def body(a_ref, b_ref, o_ref):
        o_ref[...] = jnp.dot(a_ref[...].astype(BF16), b_ref[...].astype(BF16),
                             preferred_element_type=F32).astype(o_ref.dtype)

    return pl.pallas_call(
        body, out_shape=jax.ShapeDtypeStruct((M, N), out_dtype), grid=(M // tm,),
        in_specs=[pl.BlockSpec((tm, K), lambda i: (i, 0)), pl.BlockSpec((K, N), lambda i: (b_blk, 0))],
        out_specs=pl.BlockSpec((tm, N), lambda i: (i, 0)),
        compiler_params=_params(("parallel",), VMEM_LIMIT), name=name)(a, b)


def mm_nt(a, b, out_dtype=F32, name="mm_nt", b_rows=None, b_blk=0):
    M, N = a.shape
    K = b.shape[0] if b_rows is None else b_rows
    tm = math.gcd(ROW_TILE, M)

    def body(a_ref, b_ref, o_ref):
        o_ref[...] = lax.dot_general(a_ref[...].astype(BF16), b_ref[...].astype(BF16),
                                     (((1,), (1,)), ((), ())),
                                     preferred_element_type=F32).astype(o_ref.dtype)

    return pl.pallas_call(
        body, out_shape=jax.ShapeDtypeStruct((M, K), out_dtype), grid=(M // tm,),
        in_specs=[pl.BlockSpec((tm, N), lambda i: (i, 0)), pl.BlockSpec((K, N), lambda i: (b_blk, 0))],
        out_specs=pl.BlockSpec((tm, K), lambda i: (i, 0)),
        compiler_params=_params(("parallel",), VMEM_LIMIT), name=name)(a, b)


def mm_tn(a, b, name="mm_tn", b_more=None):
    M, N = b.shape
    K = a.shape[1]
    tn = math.gcd(512, N) if N % 128 == 0 and N > 512 else N
    bs = [b] if b_more is None else [b, b_more]
    assert all(x.shape[1] == N for x in bs)
    nb = N // tn

    def body(a_ref, *refs):
        o_ref = refs[-1]
        for k, b_ref in enumerate(refs[:-1]):
            def product(b_ref=b_ref):
                o_ref[...] = lax.dot_general(a_ref[a.shape[0] - b_ref.shape[0]:, :].astype(BF16),
                                             b_ref[...].astype(BF16), (((0,), (0,)), ((), ())),
                                             preferred_element_type=F32)
            if len(bs) == 1:
                product()
            else:
                pl.when(pl.program_id(0) // nb == k)(product)

    def b_spec(k, rows):
        return pl.BlockSpec((rows, tn), lambda j: (0, jnp.clip(j - k * nb, 0, nb - 1)))

    return pl.pallas_call(
        body, out_shape=jax.ShapeDtypeStruct((K, N * len(bs)), F32), grid=(nb * len(bs),),
        in_specs=[pl.BlockSpec(a.shape, lambda j: (0, 0))] + [b_spec(k, x.shape[0]) for k, x in enumerate(bs)],
        out_specs=pl.BlockSpec((K, tn), lambda j: (0, j)),
        compiler_params=_params(("parallel",), VMEM_LIMIT), name=name)(a, *bs)


class Rows:
    def __init__(self, arr, width=None, row_off=0, col_blk=0):
        self.arr = arr
        self.width = arr.shape[1] if width is None else width
        self.row_off = row_off
        self.col_blk = col_blk

    def spec(self, tm):
        ro, cb = self.row_off // tm, self.col_blk
        return pl.BlockSpec((tm, self.width), lambda i: (i + ro, cb))


def _as_rows(x):
    return x if isinstance(x, Rows) else Rows(x)


def _row_tile(n_rows, n_ctx_rows, rows):
    tm = math.gcd(ROW_TILE, n_rows, n_ctx_rows)
    for r in rows:
        tm = math.gcd(tm, r.row_off)
    return tm


def _bc_spec(arr, n_ctx_blocks):
    g, _, d = arr.shape
    if g == 1:
        return pl.BlockSpec((1, 1, d), lambda i: (0, 0, 0))
    return pl.BlockSpec((1, 1, d), lambda i: ((i >= n_ctx_blocks).astype(jnp.int32), 0, 0))


def rowwise_fwd(fn, rows, bcs, out_dims, out_dtypes, n_rows, n_ctx_rows, name):
    rows = [_as_rows(r) for r in rows]
    tm = _row_tile(n_rows, n_ctx_rows, rows)
    ncb = n_ctx_rows // tm
    nr, nb = len(rows), len(bcs)

    def body(*refs):
        vals = [r[...].astype(F32) for r in refs[:nr]] + [b[0].astype(F32) for b in refs[nr:nr + nb]]
        outs = fn(*vals)
        for o_ref, v in zip(refs[nr + nb:], outs):
            o_ref[...] = v.astype(o_ref.dtype)

    outs = pl.pallas_call(
        body,
        out_shape=[jax.ShapeDtypeStruct((n_rows, d), dt) for d, dt in zip(out_dims, out_dtypes)],
        grid=(n_rows // tm,),
        in_specs=[r.spec(tm) for r in rows] + [_bc_spec(b, ncb) for b in bcs],
        out_specs=[pl.BlockSpec((tm, d), lambda i: (i, 0)) for d in out_dims],
        compiler_params=_params(("parallel",), VMEM_LIMIT), name=name)(*[r.arr for r in rows], *bcs)
    return outs


def rowwise_bwd(fn, rows, bcs, cts, diff_rows, diff_bcs, n_rows, n_ctx_rows, name, ct_extra=None, lat_add=None):
    rows = [_as_rows(r) for r in rows]
    cts = [_as_rows(c) for c in cts]
    extra = [_as_rows(ct_extra)] if ct_extra is not None else []
    tm = _row_tile(n_rows, n_ctx_rows, rows + cts + extra)
    ncb = n_ctx_rows // tm
    nr, nb, nc = len(rows), len(bcs), len(cts)
    ndr, ndb = len(diff_rows), len(diff_bcs)
    n_in = nr + nb + nc + len(extra) + (lat_add is not None)

    def body(*refs):
        i = pl.program_id(0)
        rvals = [r[...].astype(F32) for r in refs[:nr]]
        bvals = [b[0].astype(F32) for b in refs[nr:nr + nb]]
        cvals = [c[...].astype(F32) for c in refs[nr + nb:nr + nb + nc]]
        if extra:
            cvals[0] = cvals[0] + refs[nr + nb + nc][...].astype(F32)
        outs = refs[n_in:]

        def f(*d):
            rv, bv = list(rvals), list(bvals)
            for k, idx in enumerate(diff_rows):
                rv[idx] = d[k]
            for k, idx in enumerate(diff_bcs):
                bv[idx] = d[ndr + k]
            return tuple(fn(*rv, *bv))

        primals = [rvals[k] for k in diff_rows] + [bvals[k] for k in diff_bcs]
        _, vjp = jax.vjp(f, *primals)
        grads = list(vjp(tuple(cvals)))
        if lat_add is not None:
            add = refs[n_in - 1][...]
            grads[0] = grads[0] + (add if lat_add.shape[0] == n_rows else jnp.where(i >= ncb, add, 0.0))
        for k in range(ndr):
            outs[k][...] = grads[k].astype(outs[k].dtype)
        for k, idx in enumerate(diff_bcs):
            o_ref = outs[ndr + k]
            first = (i == 0)
            if bcs[idx].shape[0] == 2:
                first = first | (i == ncb)

            @pl.when(first)
            def _(o_ref=o_ref):
                o_ref[...] = jnp.zeros_like(o_ref)

            o_ref[0] += grads[ndr + k]

    out_shape = [jax.ShapeDtypeStruct((n_rows, rows[k].width), F32) for k in diff_rows]
    out_shape += [jax.ShapeDtypeStruct(bcs[k].shape, F32) for k in diff_bcs]
    out_specs = [pl.BlockSpec((tm, rows[k].width), lambda i: (i, 0)) for k in diff_rows]
    out_specs += [_bc_spec(bcs[k], ncb) for k in diff_bcs]
    ins = [r.arr for r in rows] + list(bcs) + [c.arr for c in cts + extra]
    in_specs = [r.spec(tm) for r in rows] + [_bc_spec(b, ncb) for b in bcs] + [c.spec(tm) for c in cts + extra]
    if lat_add is not None:
        ins.append(lat_add)
        skip = ncb if lat_add.shape[0] != n_rows else 0
        in_specs.append(pl.BlockSpec((tm, lat_add.shape[1]), lambda i: (jnp.maximum(i - skip, 0), 0)))
    outs = pl.pallas_call(
        body, out_shape=out_shape, grid=(n_rows // tm,), in_specs=in_specs, out_specs=out_specs,
        compiler_params=_params(("arbitrary",), VMEM_LIMIT), name=name)(*ins)
    return outs


def _rms(x):
    return x * lax.rsqrt(jnp.mean(x * x, axis=-1, keepdims=True) + EPS)


def _sigmoid(x):
    return 0.5 * (jnp.tanh(0.5 * x) + 1.0)


def _silu(x):
    return x * _sigmoid(x)


def _gelu_tanh(x):
    return 0.5 * x * (1.0 + jnp.tanh(math.sqrt(2.0 / math.pi) * (x + 0.044715 * (x * x * x))))


def f_norm_mod(x, g, sc, sh):
    return ((_rms(x) * g) * (1.0 + sc) + sh,)


def f_rms(x, g):
    return (_rms(x) * g,)


def f_gate(o, z):
    return (o * _silu(z),)


def f_s5_act(y, u, d):
    return (_gelu_tanh(y + d * u),)


def f_s5_glu(ya, gl, z, b):
    return (ya * _sigmoid(gl + b) * _silu(z),)


def _as_parts(x, n_ctx):
    xs = x if isinstance(x, tuple) else (x,)
    assert len(xs) == 1 or xs[0].shape[0] == n_ctx
    return xs, sum(p.shape[0] for p in xs)


def _parts_specs(xs, tm, ncb):
    d = xs[0].shape[1]
    if len(xs) == 1:
        return [pl.BlockSpec((tm, d), lambda i: (i, 0))]
    return [pl.BlockSpec((tm, d), lambda i: (jnp.minimum(i, ncb - 1), 0)),
            pl.BlockSpec((tm, d), lambda i: (jnp.maximum(i - ncb, 0), 0))]


def _parts_tile(x_refs, ncb):
    if len(x_refs) == 1:
        return x_refs[0][...]
    return jnp.where(pl.program_id(0) < ncb, x_refs[0][...], x_refs[1][...])


def norm_proj(x, g, sc, sh, w, n_ctx, name):
    xs, n = _as_parts(x, n_ctx)
    d = xs[0].shape[1]
    nw = w.shape[1]
    tm = math.gcd(ROW_TILE, n, n_ctx)
    ncb = n_ctx // tm
    nx = len(xs)

    def body(*refs):
        g_ref, sc_ref, sh_ref, w_ref, h_ref, p_ref = refs[nx:]
        h = f_norm_mod(_parts_tile(refs[:nx], ncb), g_ref[0], sc_ref[0], sh_ref[0])[0].astype(BF16)
        h_ref[...] = h
        p_ref[...] = jnp.dot(h, w_ref[...], preferred_element_type=F32)

    row = pl.BlockSpec((tm, d), lambda i: (i, 0))
    return pl.pallas_call(
        body, out_shape=[jax.ShapeDtypeStruct((n, d), BF16), jax.ShapeDtypeStruct((n, nw), F32)], grid=(n // tm,),
        in_specs=_parts_specs(xs, tm, ncb) + [_bc_spec(g, ncb), _bc_spec(sc, ncb), _bc_spec(sh, ncb),
                                              pl.BlockSpec(w.shape, lambda i: (0, 0))],
        out_specs=[row, pl.BlockSpec((tm, nw), lambda i: (i, 0))],
        compiler_params=_params(("parallel",), VMEM_LIMIT), name=name)(*xs, g, sc, sh, w)


def norm_proj_bwd(terms, w, x, g, sc, sh, add, n_ctx, name, latent_dx_only=False, dx_from_segments=False):
    xs, n = _as_parts(x, n_ctx)
    adds = add if isinstance(add, tuple) else (add,)
    d = xs[0].shape[1]
    tm = math.gcd(ROW_TILE, n, n_ctx, *[t[2] for t in terms])
    ncb = n_ctx // tm
    nt, nx, na = len(terms), len(xs), len(adds)
    add_skip = ncb if na == 1 and add.shape[0] != n else 0
    n_dx = 2 if dx_from_segments else 1
    tj = tm // N_SEG
    assert not dx_from_segments or (ncb == 1 and not latent_dx_only)

    def body(*refs):
        i = pl.program_id(0)
        a_refs = refs[:nt]
        w_ref, x_refs = refs[nt], refs[nt + 1:nt + 1 + nx]
        g_ref, sc_ref, sh_ref = refs[nt + 1 + nx:nt + 4 + nx]
        add_refs = refs[nt + 4 + nx:nt + 4 + nx + na]
        outs = refs[nt + 4 + nx + na:]
        dx_refs, (dg_ref, dsc_ref, dsh_ref) = outs[:n_dx], outs[n_dx:n_dx + 3]
        d_h = None
        for a_ref, (a, off, first) in zip(a_refs, terms):
            part = lax.dot_general(a_ref[...].astype(BF16), w_ref[:, off:off + a.shape[1]], NT_DIMS,
                                   preferred_element_type=F32)
            if first:
                part = jnp.where(i >= first // tm, part, 0.0)
            d_h = part if d_h is None else d_h + part
        _, vjp = jax.vjp(lambda x_, g_, sc_, sh_: f_norm_mod(x_, g_, sc_, sh_), _parts_tile(x_refs, ncb), g_ref[0],
                         sc_ref[0], sh_ref[0])
        d_x, d_g, d_sc, d_sh = vjp((d_h,))
        extra = _parts_tile(add_refs, ncb)
        d_x = d_x + (extra if add_skip == 0 else jnp.where(i >= ncb, extra, 0.0))
        if dx_from_segments:
            slabs = outs[n_dx + 3]
            for c in range(d // HEAD_LANES):
                slabs[c] = d_x[:, c * HEAD_LANES:(c + 1) * HEAD_LANES]
            for k, here in enumerate([i < ncb, i >= ncb]):
                @pl.when(here)
                def _(k=k):
                    for c in range(d // HEAD_LANES):
                        for seg in range(N_SEG):
                            dx_refs[k][seg, :, c * HEAD_LANES:(c + 1) * HEAD_LANES] = (
                                slabs[c, pl.ds(seg, tj, stride=N_SEG), :])
        else:
            dx_refs[0][...] = d_x

        @pl.when(i == 0)
        def _():
            dg_ref[...] = jnp.zeros_like(dg_ref)

        @pl.when((i == 0) | (i == ncb))
        def _():
            dsc_ref[...] = jnp.zeros_like(dsc_ref)
            dsh_ref[...] = jnp.zeros_like(dsh_ref)

        dg_ref[0] += d_g
        dsc_ref[0] += d_sc
        dsh_ref[0] += d_sh

    def a_spec(a, first):
        skip = first // tm
        return pl.BlockSpec((tm, a.shape[1]), lambda i: (jnp.maximum(i - skip, 0), 0))

    dx_skip = ncb if latent_dx_only else 0
    if dx_from_segments:
        dx_shapes = [jax.ShapeDtypeStruct((N_SEG, n_ctx // N_SEG, d), F32),
                     jax.ShapeDtypeStruct((N_SEG, (n - n_ctx) // N_SEG, d), F32)]
        dx_specs = [pl.BlockSpec((N_SEG, tj, d), lambda i: (0, 0, 0)),
                    pl.BlockSpec((N_SEG, tj, d), lambda i: (0, jnp.maximum(i - ncb, 0), 0))]
    else:
        dx_shapes = [jax.ShapeDtypeStruct((n - dx_skip * tm, d), F32)]
        dx_specs = [pl.BlockSpec((tm, d), lambda i: (jnp.maximum(i - dx_skip, 0), 0))]
    add_specs = (_parts_specs(adds, tm, ncb) if na == 2 else
                 [pl.BlockSpec((tm, d), lambda i: (jnp.maximum(i - add_skip, 0), 0))])
    res = pl.pallas_call(
        body,
        out_shape=dx_shapes + [jax.ShapeDtypeStruct(g.shape, F32), jax.ShapeDtypeStruct(sc.shape, F32),
                               jax.ShapeDtypeStruct(sh.shape, F32)],
        grid=(n // tm,),
        in_specs=[a_spec(a, first) for a, _, first in terms]
        + [pl.BlockSpec(w.shape, lambda i: (0, 0))] + _parts_specs(xs, tm, ncb)
        + [_bc_spec(g, ncb), _bc_spec(sc, ncb), _bc_spec(sh, ncb)] + add_specs,
        out_specs=dx_specs + [_bc_spec(g, ncb), _bc_spec(sc, ncb), _bc_spec(sh, ncb)],
        scratch_shapes=[pltpu.VMEM((d // HEAD_LANES, tm, HEAD_LANES), F32)] if dx_from_segments else [],
        compiler_params=_params(("arbitrary",), VMEM_LIMIT), name=name)(
            *[t[0] for t in terms], w, *xs, g, sc, sh, *adds)
    if dx_from_segments:
        return ((res[0].reshape(n_ctx, d), res[1].reshape(n - n_ctx, d)), *res[2:])
    return res


def mla_post_fwd(o, p0, x0, gate, w_out, n_ctx, name="l0_post"):
    n, d = o.shape
    xs, _ = _as_parts(x0, n_ctx)
    tm = math.gcd(ROW_TILE, n, n_ctx)
    ncb = n_ctx // tm
    nx = len(xs)

    def body(o_ref, z_ref, *refs):
        gt_ref, w_ref, x1_ref, og_ref, out_ref = refs[nx:]
        og = f_gate(o_ref[...], z_ref[...])[0].astype(BF16)
        out = jnp.dot(og, w_ref[...], preferred_element_type=F32)
        og_ref[...] = og
        out_ref[...] = out
        x1_ref[...] = _parts_tile(refs[:nx], ncb) + gt_ref[0] * out

    row = pl.BlockSpec((tm, d), lambda i: (i, 0))
    return pl.pallas_call(
        body, out_shape=[jax.ShapeDtypeStruct((n, d), F32), jax.ShapeDtypeStruct((n, d), BF16),
                         jax.ShapeDtypeStruct((n, d), F32)],
        grid=(n // tm,),
        in_specs=[row, row] + _parts_specs(xs, tm, ncb) + [_bc_spec(gate, ncb), pl.BlockSpec((d, d), lambda i: (0, 0))],
        out_specs=[row, row, row],
        compiler_params=_params(("parallel",), VMEM_LIMIT), name=name)(o, p0, *xs, gate, w_out)


def mla_post_bwd(dx1, out, og, o, p0, gate, w_out, n_ctx, name="l0_post_bwd"):
    n, d = o.shape
    dxs, _ = _as_parts(dx1, n_ctx)
    tm = math.gcd(ROW_TILE, n, n_ctx)
    ncb = n_ctx // tm
    nx = len(dxs)

    def body(*refs):
        out_ref, og_ref, o_ref, z_ref, gt_ref, w_ref, do_ref, dz_ref, dgt_ref, dw_ref = refs[nx:]
        i = pl.program_id(0)

        @pl.when(i == 0)
        def _():
            dw_ref[...] = jnp.zeros_like(dw_ref)

        @pl.when((i == 0) | (i == ncb))
        def _():
            dgt_ref[...] = jnp.zeros_like(dgt_ref)

        dx = _parts_tile(refs[:nx], ncb)
        dgt_ref[0] += jnp.sum(dx * out_ref[...], axis=0, keepdims=True)
        d_out16 = (gt_ref[0] * dx).astype(BF16)
        dw_ref[...] += lax.dot_general(og_ref[...], d_out16, (((0,), (0,)), ((), ())), preferred_element_type=F32)
        d_og = lax.dot_general(d_out16, w_ref[...], NT_DIMS, preferred_element_type=F32)
        _, gate_vjp = jax.vjp(lambda o_, z_: f_gate(o_, z_), o_ref[...], z_ref[...])
        d_o, d_z = gate_vjp((d_og,))
        do_ref[...] = d_o
        dz_ref[...] = d_z

    row = pl.BlockSpec((tm, d), lambda i: (i, 0))
    mat = pl.BlockSpec((d, d), lambda i: (0, 0))
    return pl.pallas_call(
        body, out_shape=[jax.ShapeDtypeStruct((n, d), F32), jax.ShapeDtypeStruct((n, d), F32),
                         jax.ShapeDtypeStruct(gate.shape, F32), jax.ShapeDtypeStruct((d, d), F32)],
        grid=(n // tm,),
        in_specs=_parts_specs(dxs, tm, ncb) + [row, row, row, row, _bc_spec(gate, ncb), mat],
        out_specs=[row, row, _bc_spec(gate, ncb), mat],
        compiler_params=_params(("arbitrary",), VMEM_LIMIT), name=name)(*dxs, out, og, o, p0, gate, w_out)


def s5_tail(y_ssm, p1, x1p, target, n_ctx, d_vec, b_glu, gate, final_g, w_glu, w_out, name="l1_tail"):
    n, d = y_ssm.shape
    tm = math.gcd(ROW_TILE, n, n_ctx)
    off = n_ctx // tm
    tn_dims = (((0,), (0,)), ((), ()))

    def row_loss(x, g, t):
        e = _rms(x) * g - t
        return 0.5 * (e * e) * (1.0 / d)

    def body(y_ref, u_ref, z_ref, x1_ref, t_ref, d_ref, b_ref, gt_ref, fg_ref, wg_ref, wo_ref,
             l_ref, dx_ref, dy_ref, du_ref, dz_ref, dfg_ref, dgt_ref, db_ref, dd_ref, dwg_ref, dwo_ref):
        @pl.when(pl.program_id(0) == 0)
        def _():
            for r in (l_ref, dfg_ref, dgt_ref, db_ref, dd_ref, dwg_ref, dwo_ref):
                r[...] = jnp.zeros_like(r)

        u, z, tgt, gt = u_ref[...], z_ref[...], t_ref[...], gt_ref[...]
        (ya,), act_vjp = jax.vjp(lambda y_, u_, d_: f_s5_act(y_, u_, d_), y_ref[...], u, d_ref[...])
        ya16 = ya.astype(BF16)
        gl = jnp.dot(ya16, wg_ref[...], preferred_element_type=F32)
        (y3,), glu_vjp = jax.vjp(lambda a_, g_, z_, b_: f_s5_glu(a_, g_, z_, b_), ya, gl, z, b_ref[...])
        y3_16 = y3.astype(BF16)
        out1 = jnp.dot(y3_16, wo_ref[...], preferred_element_type=F32)
        lterm, loss_vjp = jax.vjp(lambda x_, g_: row_loss(x_, g_, tgt), x1_ref[...] + gt * out1, fg_ref[...])
        dx2, dfg = loss_vjp(jnp.ones_like(lterm))
        l_ref[...] += jnp.sum(lterm, axis=0, keepdims=True)
        dfg_ref[...] += dfg
        dx_ref[...] = dx2
        dgt_ref[...] += jnp.sum(dx2 * out1, axis=0, keepdims=True)
        d_out16 = (gt * dx2).astype(BF16)
        dwo_ref[...] += lax.dot_general(y3_16, d_out16, tn_dims, preferred_element_type=F32)
        d_y3 = lax.dot_general(d_out16, wo_ref[...], NT_DIMS, preferred_element_type=F32)
        d_ya, d_gl, d_z, d_b = glu_vjp((d_y3,))
        dz_ref[...] = d_z
        db_ref[...] += d_b
        d_gl16 = d_gl.astype(BF16)
        dwg_ref[...] += lax.dot_general(ya16, d_gl16, tn_dims, preferred_element_type=F32)
        d_ya = d_ya + lax.dot_general(d_gl16, wg_ref[...], NT_DIMS, preferred_element_type=F32)
        d_y, d_u, d_d = act_vjp((d_ya,))
        dy_ref[...] = d_y
        du_ref[...] = d_u
        dd_ref[...] += d_d

    row = pl.BlockSpec((tm, d), lambda i: (i, 0))
    vecs = pl.BlockSpec((1, d), lambda i: (0, 0))
    mat = pl.BlockSpec((d, d), lambda i: (0, 0))
    return pl.pallas_call(
        body,
        out_shape=[jax.ShapeDtypeStruct((1, d), F32)] + [jax.ShapeDtypeStruct((n, d), F32)] * 4
        + [jax.ShapeDtypeStruct((1, d), F32)] * 4 + [jax.ShapeDtypeStruct((d, d), F32)] * 2,
        grid=(n // tm,),
        in_specs=[row, pl.BlockSpec((tm, d), lambda i: (i + off, 0)), pl.BlockSpec((tm, d), lambda i: (i + off, 1)),
                  pl.BlockSpec((tm, d), lambda i: (i + off, 0)), row, vecs, vecs, vecs, vecs, mat, mat],
        out_specs=[vecs, row, row, row, row, vecs, vecs, vecs, vecs, mat, mat],
        compiler_params=_params(("arbitrary",), VMEM_LIMIT), name=name)(
            y_ssm, p1, p1, x1p, target, d_vec, b_glu, gate, final_g, w_glu, w_out)


NT_DIMS = (((1,), (1,)), ((), ()))
HEAD_LANES = 128
N_PAIRS = MLA_HEADS // 2


def _own_lanes(shape, hh):
    lane = lax.broadcasted_iota(jnp.int32, shape, len(shape) - 1)
    return (lane < V_HEAD_DIM) if hh == 0 else (lane >= V_HEAD_DIM)


def _delta_lane(hh):
    return V_HEAD_DIM if hh == 0 else 0


def _rope_tiles(x, cos, sin_next, sin_prev, inverse):
    width = x.shape[-1]
    reps = width // HEAD_LANES
    c, sn, sp = (jnp.tile(t, (1, reps)) for t in (cos, sin_next, sin_prev))
    if inverse:
        return x * c + pltpu.roll(x * sn, 8, 1) + pltpu.roll(x * sp, width - 8, 1)
    return x * c + pltpu.roll(x, width - 8, 1) * sn + pltpu.roll(x, 8, 1) * sp


STAT_LANE = QK_DIM


def _with_stat(x16, col, lane0):
    hi = col.astype(BF16)
    r1 = col - hi.astype(F32)
    mid = r1.astype(BF16)
    lo = (r1 - mid.astype(F32)).astype(BF16)
    lane = lax.broadcasted_iota(jnp.int32, x16.shape, 1)
    return jnp.where(lane == lane0, hi, jnp.where(lane == lane0 + 1, mid, jnp.where(lane == lane0 + 2, lo, x16)))


def attn_fwd(qb, kb, vb, n_ctx):
    T = qb.shape[0]
    tq = math.gcd(ROW_TILE, n_ctx)
    nq, ncb = T // tq, n_ctx // tq

    def body(q_ref, k_ref, v_ref, o_ref, lse_ref, qs_ref):
        qi = pl.program_id(1)

        def rows(n_keys):
            v = v_ref[:n_keys, :]
            outs = []
            for hh in range(2):
                hs = slice(hh * HEAD_LANES, (hh + 1) * HEAD_LANES)
                s = lax.dot_general(q_ref[:, hs], k_ref[:n_keys, hs], NT_DIMS,
                                    preferred_element_type=F32) * SOFTMAX_SCALE
                m = jnp.max(s, axis=-1, keepdims=True)
                p = jnp.exp(s - m)
                l = jnp.sum(p, axis=-1, keepdims=True)
                outs.append(jnp.dot(p.astype(BF16), v, preferred_element_type=F32) / l)
                lse = m + jnp.log(l)
                lse_ref[hh] = lse
                qs_ref[:, hs] = _with_stat(q_ref[:, hs], lse * (-1.0 / SOFTMAX_SCALE), STAT_LANE)
            o_ref[...] = jnp.where(_own_lanes(outs[0].shape, 0), outs[0], outs[1])

        pl.when(qi < ncb)(lambda: rows(n_ctx))
        pl.when(qi >= ncb)(lambda: rows(T))

    return pl.pallas_call(
        body,
        out_shape=[jax.ShapeDtypeStruct((T, MLA_HEADS * V_HEAD_DIM), F32),
                   jax.ShapeDtypeStruct((MLA_HEADS, T, 1), F32), jax.ShapeDtypeStruct(qb.shape, BF16)],
        grid=(N_PAIRS, nq),
        in_specs=[pl.BlockSpec((tq, 2 * HEAD_LANES), lambda h, i: (i, h)),
                  pl.BlockSpec((T, 2 * HEAD_LANES), lambda h, i: (0, h)),
                  pl.BlockSpec((T, 2 * V_HEAD_DIM), lambda h, i: (0, h))],
        out_specs=[pl.BlockSpec((tq, 2 * V_HEAD_DIM), lambda h, i: (i, h)),
                   pl.BlockSpec((2, tq, 1), lambda h, i: (h, i, 0)),
                   pl.BlockSpec((tq, 2 * HEAD_LANES), lambda h, i: (i, h))],
        compiler_params=_params(("parallel", "parallel"), VMEM_LIMIT), name="attn_fwd")(qb, kb, vb)


def attn_bwd_dq(qb, kb, vb, o, do, lse, tabs, n_ctx):
    T = qb.shape[0]
    tq = math.gcd(ROW_TILE, n_ctx)
    nq, ncb = T // tq, n_ctx // tq

    def body(q_ref, k_ref, v_ref, o_ref, do_ref, lse_ref, c_ref, sn_ref, sp_ref, dq_ref, dos_ref):
        qi = pl.program_id(1)

        def rows(n_keys):
            v = v_ref[:n_keys, :]
            dqs = []
            for hh in range(2):
                hs = slice(hh * HEAD_LANES, (hh + 1) * HEAD_LANES)
                k = k_ref[:n_keys, hs]
                do = jnp.where(_own_lanes(do_ref.shape, hh), do_ref[...], 0.0)
                delta = jnp.sum(do * o_ref[...], axis=-1, keepdims=True)
                s = lax.dot_general(q_ref[:, hs], k, NT_DIMS, preferred_element_type=F32) * SOFTMAX_SCALE
                p = jnp.exp(s - lse_ref[hh])
                do16 = do.astype(BF16)
                dp = lax.dot_general(do16, v, NT_DIMS, preferred_element_type=F32)
                ds = p * (dp - delta) * SOFTMAX_SCALE
                dqs.append(jnp.dot(ds.astype(BF16), k, preferred_element_type=F32))
                dos_ref[:, hs] = _with_stat(do16, delta, _delta_lane(hh))
            dq = jnp.concatenate(dqs, axis=1)
            dq_ref[...] = _rope_tiles(dq, c_ref[...], sn_ref[...], sp_ref[...], True).astype(BF16)

        pl.when(qi < ncb)(lambda: rows(n_ctx))
        pl.when(qi >= ncb)(lambda: rows(T))

    tab = pl.BlockSpec((tq, HEAD_LANES), lambda h, i: (i, 0))
    return pl.pallas_call(
        body,
        out_shape=[jax.ShapeDtypeStruct((T, MLA_HEADS * HEAD_LANES), BF16)] * 2,
        grid=(N_PAIRS, nq),
        in_specs=[pl.BlockSpec((tq, 2 * HEAD_LANES), lambda h, i: (i, h)),
                  pl.BlockSpec((T, 2 * HEAD_LANES), lambda h, i: (0, h)),
                  pl.BlockSpec((T, 2 * V_HEAD_DIM), lambda h, i: (0, h)),
                  pl.BlockSpec((tq, 2 * V_HEAD_DIM), lambda h, i: (i, h)),
                  pl.BlockSpec((tq, 2 * V_HEAD_DIM), lambda h, i: (i, h)),
                  pl.BlockSpec((2, tq, 1), lambda h, i: (h, i, 0)), tab, tab, tab],
        out_specs=[pl.BlockSpec((tq, 2 * HEAD_LANES), lambda h, i: (i, h))] * 2,
        compiler_params=_params(("parallel", "parallel"), VMEM_LIMIT), name="attn_bwd_dq")(
            qb, kb, vb, o, do, lse, *tabs)


def attn_bwd(qb, kb, vb, o, do, lse, tabs, n_ctx):
    T = qb.shape[0]
    tq = math.gcd(ROW_TILE, n_ctx)
    nq, ncb = T // tq, n_ctx // tq
    tn = (((0,), (0,)), ((), ()))

    def body(q_ref, k_ref, v_ref, o_ref, do_ref, lse_ref, c_ref, sn_ref, sp_ref, dq_ref, dk_ref, dv_ref, dkt, dvt):
        qi = pl.program_id(1)

        @pl.when(qi == 0)
        def _():
            dkt[...] = jnp.zeros_like(dkt)
            dvt[...] = jnp.zeros_like(dvt)

        def rows(n_keys):
            v = v_ref[:n_keys, :]
            dqs = []
            for hh in range(2):
                hs = slice(hh * HEAD_LANES, (hh + 1) * HEAD_LANES)
                k = k_ref[:n_keys, hs]
                q = q_ref[:, hs]
                do = jnp.where(_own_lanes(do_ref.shape, hh), do_ref[...], 0.0)
                delta = jnp.sum(do * o_ref[...], axis=-1, keepdims=True)
                s = lax.dot_general(q, k, NT_DIMS, preferred_element_type=F32) * SOFTMAX_SCALE
                p = jnp.exp(s - lse_ref[hh])
                do16 = do.astype(BF16)
                dp = lax.dot_general(do16, v, NT_DIMS, preferred_element_type=F32)
                ds16 = (p * (dp - delta) * SOFTMAX_SCALE).astype(BF16)
                dqs.append(jnp.dot(ds16, k, preferred_element_type=F32))
                dkt[hs, :n_keys] += lax.dot_general(q, ds16, tn, preferred_element_type=F32)
                dvt[:, :n_keys] += lax.dot_general(do16, p.astype(BF16), tn, preferred_element_type=F32)
            dq = jnp.concatenate(dqs, axis=1)
            dq_ref[...] = _rope_tiles(dq, c_ref[...], sn_ref[...], sp_ref[...], True).astype(BF16)

        pl.when(qi < ncb)(lambda: rows(n_ctx))
        pl.when(qi >= ncb)(lambda: rows(T))

        @pl.when(qi == nq - 1)
        def _():
            dk_ref[...] = dkt[...].T
            dv_ref[...] = dvt[...].T

    tab = pl.BlockSpec((tq, HEAD_LANES), lambda h, i: (i, 0))
    return pl.pallas_call(
        body,
        out_shape=[jax.ShapeDtypeStruct((T, MLA_HEADS * HEAD_LANES), BF16),
                   jax.ShapeDtypeStruct((T, MLA_HEADS * HEAD_LANES), F32),
                   jax.ShapeDtypeStruct((T, MLA_HEADS * V_HEAD_DIM), F32)],
        grid=(N_PAIRS, nq),
        in_specs=[pl.BlockSpec((tq, 2 * HEAD_LANES), lambda h, i: (i, h)),
                  pl.BlockSpec((T, 2 * HEAD_LANES), lambda h, i: (0, h)),
                  pl.BlockSpec((T, 2 * V_HEAD_DIM), lambda h, i: (0, h)),
                  pl.BlockSpec((tq, 2 * V_HEAD_DIM), lambda h, i: (i, h)),
                  pl.BlockSpec((tq, 2 * V_HEAD_DIM), lambda h, i: (i, h)),
                  pl.BlockSpec((2, tq, 1), lambda h, i: (h, i, 0)), tab, tab, tab],
        out_specs=[pl.BlockSpec((tq, 2 * HEAD_LANES), lambda h, i: (i, h)),
                   pl.BlockSpec((T, 2 * HEAD_LANES), lambda h, i: (0, h)),
                   pl.BlockSpec((T, 2 * V_HEAD_DIM), lambda h, i: (0, h))],
        scratch_shapes=[pltpu.VMEM((2 * HEAD_LANES, T), F32), pltpu.VMEM((2 * V_HEAD_DIM, T), F32)],
        compiler_params=_params(("parallel", "arbitrary"), VMEM_LIMIT), name="attn_bwd")(
            qb, kb, vb, o, do, lse, *tabs)


def attn_bwd_dkv(qs, kb, vb, dos, n_ctx):
    T = qs.shape[0]
    tq = math.gcd(ROW_TILE, n_ctx)
    nq, ncb = T // tq, n_ctx // tq

    def body(q_ref, do_ref, k_ref, v_ref, dk_ref, dv_ref):
        kj = pl.program_id(1)

        def cols(first):
            v = v_ref[...]
            lane = lax.broadcasted_iota(jnp.int32, v.shape, 1)
            dvs = []
            for hh in range(2):
                hs = slice(hh * HEAD_LANES, (hh + 1) * HEAD_LANES)
                q = q_ref[first:, hs]
                do16 = do_ref[first:, hs]
                in_delta = (lane >= _delta_lane(hh)) & (lane < _delta_lane(hh) + 3)
                v_minus = jnp.where(in_delta, -jnp.ones_like(v), v)
                pt = jnp.exp(lax.dot_general(k_ref[:, hs], q, NT_DIMS, preferred_element_type=F32) * SOFTMAX_SCALE)
                dvs.append(jnp.dot(pt.astype(BF16), do16, preferred_element_type=F32))
                dst = pt * lax.dot_general(v_minus, do16, NT_DIMS, preferred_element_type=F32) * SOFTMAX_SCALE
                dk_ref[:, hs] = jnp.dot(dst.astype(BF16), q, preferred_element_type=F32)
            dv_ref[...] = jnp.where(_own_lanes(dvs[0].shape, 0), dvs[0], dvs[1])

        pl.when(kj < ncb)(lambda: cols(0))
        pl.when(kj >= ncb)(lambda: cols(n_ctx))

    return pl.pallas_call(
        body,
        out_shape=[jax.ShapeDtypeStruct((T, MLA_HEADS * HEAD_LANES), F32),
                   jax.ShapeDtypeStruct((T, MLA_HEADS * V_HEAD_DIM), F32)],
        grid=(N_PAIRS, nq),
        in_specs=[pl.BlockSpec((T, 2 * HEAD_LANES), lambda h, j: (0, h)),
                  pl.BlockSpec((T, 2 * HEAD_LANES), lambda h, j: (0, h)),
                  pl.BlockSpec((tq, 2 * HEAD_LANES), lambda h, j: (j, h)),
                  pl.BlockSpec((tq, 2 * V_HEAD_DIM), lambda h, j: (j, h))],
        out_specs=[pl.BlockSpec((tq, 2 * HEAD_LANES), lambda h, j: (j, h)),
                   pl.BlockSpec((tq, 2 * V_HEAD_DIM), lambda h, j: (j, h))],
        compiler_params=_params(("parallel", "parallel"), VMEM_LIMIT), name="attn_bwd_dkv")(
            qs, dos, kb, vb)


def _split_bf16(x):
    hi = x.astype(BF16)
    return hi, (x - hi.astype(F32)).astype(BF16)


def q_heads(cq, gain, w_uq_p, tabs, name="l0_uq"):
    T, K = cq.arr.shape[0], cq.width
    N = w_uq_p.shape[1]
    tm = math.gcd(ROW_TILE, T)

    def body(a_ref, g_ref, w_ref, c_ref, sn_ref, sp_ref, o_ref, n_ref):
        qn = f_rms(a_ref[...], g_ref[0])[0].astype(BF16)
        n_ref[...] = qn
        acc = jnp.dot(qn, w_ref[...], preferred_element_type=F32)
        o_ref[...] = _rope_tiles(acc, c_ref[...], sn_ref[...], sp_ref[...], False).astype(BF16)

    tab = pl.BlockSpec((tm, HEAD_LANES), lambda i: (i, 0))
    return pl.pallas_call(
        body, out_shape=[jax.ShapeDtypeStruct((T, N), BF16), jax.ShapeDtypeStruct((T, K), BF16)], grid=(T // tm,),
        in_specs=[cq.spec(tm), pl.BlockSpec((1, 1, K), lambda i: (0, 0, 0)), pl.BlockSpec((K, N), lambda i: (0, 0)),
                  tab, tab, tab],
        out_specs=[pl.BlockSpec((tm, N), lambda i: (i, 0)), pl.BlockSpec((tm, K), lambda i: (i, 0))],
        compiler_params=_params(("parallel",), VMEM_LIMIT), name=name)(cq.arr, gain, w_uq_p, *tabs)


def kv_heads(ckv, gain, w_kn_p, w_v, kr, spread, tabs, name="l0_ukv"):
    T, K = ckv.arr.shape[0], ckv.width
    N = w_kn_p.shape[1]
    NV = w_v.shape[1]
    tm = math.gcd(ROW_TILE, T)

    def body(a_ref, g_ref, wk_ref, wv_ref, kr_ref, e_ref, c_ref, sn_ref, sp_ref, k_ref, v_ref, n_ref):
        a = f_rms(a_ref[...], g_ref[0])[0].astype(BF16)
        n_ref[...] = a
        hi, lo = _split_bf16(kr_ref[...])
        acc = (jnp.dot(a, wk_ref[...], preferred_element_type=F32)
               + jnp.dot(hi, e_ref[...], preferred_element_type=F32)
               + jnp.dot(lo, e_ref[...], preferred_element_type=F32))
        roped = _rope_tiles(acc, c_ref[...], sn_ref[...], sp_ref[...], False)
        lane = lax.broadcasted_iota(jnp.int32, roped.shape, 1) % HEAD_LANES
        k_ref[...] = jnp.where((lane >= STAT_LANE) & (lane < STAT_LANE + 3), 1.0, roped).astype(BF16)
        v_ref[...] = jnp.dot(a, wv_ref[...], preferred_element_type=F32).astype(BF16)

    tab = pl.BlockSpec((tm, HEAD_LANES), lambda i: (i, 0))
    return pl.pallas_call(
        body, out_shape=[jax.ShapeDtypeStruct((T, N), BF16), jax.ShapeDtypeStruct((T, NV), BF16),
                         jax.ShapeDtypeStruct((T, K), BF16)], grid=(T // tm,),
        in_specs=[ckv.spec(tm), pl.BlockSpec((1, 1, K), lambda i: (0, 0, 0)), pl.BlockSpec((K, N), lambda i: (0, 0)),
                  pl.BlockSpec((K, NV), lambda i: (0, 0)), kr.spec(tm),
                  pl.BlockSpec((kr.width, N), lambda i: (0, 0)), tab, tab, tab],
        out_specs=[pl.BlockSpec((tm, N), lambda i: (i, 0)), pl.BlockSpec((tm, NV), lambda i: (i, 0)),
                   pl.BlockSpec((tm, K), lambda i: (i, 0))],
        compiler_params=_params(("parallel",), VMEM_LIMIT), name=name)(
            ckv.arr, gain, w_kn_p, w_v, kr.arr, spread, *tabs)


def heads_unrope(d, tabs, spread=None, name="unrope"):
    T, N = d.shape
    tm = math.gcd(ROW_TILE, T)

    def body(*refs):
        if spread is None:
            d_ref, c_ref, sn_ref, sp_ref, o_ref = refs
        else:
            d_ref, c_ref, sn_ref, sp_ref, e_ref, o_ref, kr_ref = refs
        g = _rope_tiles(d_ref[...], c_ref[...], sn_ref[...], sp_ref[...], True)
        o_ref[...] = g.astype(BF16)
        if spread is not None:
            hi, lo = _split_bf16(g)
            kr_ref[...] = (lax.dot_general(hi, e_ref[...], NT_DIMS, preferred_element_type=F32)
                           + lax.dot_general(lo, e_ref[...], NT_DIMS, preferred_element_type=F32))

    tab = pl.BlockSpec((tm, HEAD_LANES), lambda i: (i, 0))
    row = pl.BlockSpec((tm, N), lambda i: (i, 0))
    ins, in_specs = [d, *tabs], [row, tab, tab, tab]
    out_shape, out_specs = [jax.ShapeDtypeStruct((T, N), BF16)], [row]
    if spread is not None:
        ins.append(spread)
        in_specs.append(pl.BlockSpec(spread.shape, lambda i: (0, 0)))
        out_shape.append(jax.ShapeDtypeStruct((T, spread.shape[0]), F32))
        out_specs.append(pl.BlockSpec((tm, spread.shape[0]), lambda i: (i, 0)))
    return pl.pallas_call(
        body, out_shape=out_shape, grid=(T // tm,), in_specs=in_specs, out_specs=out_specs,
        compiler_params=_params(("parallel",), VMEM_LIMIT), name=name)(*ins)


def _cmul(ar, ai, br, bi):
    return ar * br - ai * bi, ar * bi + ai * br


def s5_chain(finals, s0, a, n_steps, reverse, name):
    W = finals.shape[-1]
    first = N_SEG - 1 if reverse else 0

    def body(f_ref, s0_ref, a_ref, c_ref):
        pr, pi = jnp.ones((1, W), F32), jnp.zeros((1, W), F32)
        br, bi = a_ref[0], a_ref[1]
        n = n_steps
        while n:
            if n & 1:
                pr, pi = _cmul(pr, pi, br, bi)
            br, bi = _cmul(br, bi, br, bi)
            n >>= 1
        fr, fi = f_ref[0], f_ref[1]
        row = lax.broadcasted_iota(jnp.int32, (N_SEG, W), 0)
        s0r = jnp.broadcast_to(s0_ref[0], (N_SEG, W))
        s0i = jnp.broadcast_to(s0_ref[1], (N_SEG, W))
        cr = jnp.where(row == first, s0r, 0.0)
        ci = jnp.where(row == first, s0i, 0.0)
        shift = N_SEG - 1 if reverse else 1
        for _ in range(N_SEG - 1):
            mr, mi = _cmul(pr, pi, cr, ci)
            tr = pltpu.roll(fr + mr, shift, 0)
            ti = pltpu.roll(fi + mi, shift, 0)
            cr = jnp.where(row == first, s0r, tr)
            ci = jnp.where(row == first, s0i, ti)
        c_ref[0] = cr
        c_ref[1] = ci

    return pl.pallas_call(body, out_shape=jax.ShapeDtypeStruct((2, N_SEG, W), F32), name=name)(finals, s0, a)


def _scan_chunk(bur, bui, st_ref, a_ref, n_steps, reverse):
    for lc in range(S5_LANES // BLK_ST):
        sl = slice(lc * BLK_ST, (lc + 1) * BLK_ST)
        lr = jnp.broadcast_to(a_ref[0, :, sl], (N_SEG, BLK_ST))
        li = jnp.broadcast_to(a_ref[1, :, sl], (N_SEG, BLK_ST))

        def step(jj, carry, sl=sl, lr=lr, li=li):
            sr, si = carry
            j = (n_steps - 1 - jj) if reverse else jj
            r0 = pl.multiple_of(j * N_SEG, N_SEG)
            nr = lr * sr - li * si + bur[pl.ds(r0, N_SEG), sl]
            ni = lr * si + li * sr + bui[pl.ds(r0, N_SEG), sl]
            bur[pl.ds(r0, N_SEG), sl] = nr
            bui[pl.ds(r0, N_SEG), sl] = ni
            return nr, ni

        sr, si = lax.fori_loop(0, n_steps, step, (st_ref[0, :, sl], st_ref[1, :, sl]))
        st_ref[0, :, sl] = sr
        st_ref[1, :, sl] = si


def _project_in(x16, w_re, w_im, bur, bui, adjoint):
    for gb in range(N_BLOCKS):
        xb = x16[:, gb * BLK_CH:(gb + 1) * BLK_CH]
        sl = slice(gb * BLK_ST, (gb + 1) * BLK_ST)
        if adjoint:
            dn = (((1,), (1,)), ((), ()))
            bur[:, sl] = lax.dot_general(xb, w_re[gb], dn, preferred_element_type=F32)
            bui[:, sl] = -lax.dot_general(xb, w_im[gb], dn, preferred_element_type=F32)
        else:
            bur[:, sl] = jnp.dot(xb, w_re[gb], preferred_element_type=F32)
            bui[:, sl] = jnp.dot(xb, w_im[gb], preferred_element_type=F32)


def s5_scan(act, w_re, w_im, a, init, *, reverse, adjoint=False, c_re=None, c_im=None, add=None,
            want_ckpt=False, rows=None, name):
    act_off, N = rows if rows is not None else (0, act.shape[0])
    R = math.gcd(ROW_TILE, N, act_off)
    nch, jc = N // R, R // N_SEG
    with_out = c_re is not None

    def chunk(i):
        return (nch - 1 - i) if reverse else i

    def body(*refs):
        act_ref, wre_ref, wim_ref, a_ref, init_ref = refs[:5]
        k = 5
        if with_out:
            cre_ref, cim_ref = refs[k:k + 2]
            k += 2
        if add is not None:
            add_ref = refs[k]
            k += 1
        if with_out:
            out_ref = refs[k]
            k += 1
        if want_ckpt:
            ck_ref = refs[k]
            k += 1
        fin_ref, bur, bui = refs[k:k + 3]

        @pl.when(pl.program_id(0) == 0)
        def _():
            fin_ref[...] = init_ref[...]

        if want_ckpt:
            ck_ref[0] = fin_ref[...]
        _project_in(act_ref[...].astype(BF16), wre_ref, wim_ref, bur, bui, adjoint)
        _scan_chunk(bur, bui, fin_ref, a_ref, jc, reverse)
        if with_out:
            for gb in range(N_BLOCKS):
                sl = slice(gb * BLK_ST, (gb + 1) * BLK_ST)
                y = (jnp.dot(bur[:, sl].astype(BF16), cre_ref[gb], preferred_element_type=F32)
                     - jnp.dot(bui[:, sl].astype(BF16), cim_ref[gb], preferred_element_type=F32))
                cs = slice(gb * BLK_CH, (gb + 1) * BLK_CH)
                if add is not None:
                    y = y + add_ref[:, cs]
                out_ref[:, cs] = y

    row_spec = pl.BlockSpec((R, D_MODEL), lambda i: (chunk(i), 0))
    act_spec = pl.BlockSpec((R, D_MODEL), lambda i: (chunk(i) + act_off // R, 0))
    w_spec = pl.BlockSpec(w_re.shape, lambda i: (0, 0, 0))
    st_spec = pl.BlockSpec((2, N_SEG, S5_LANES), lambda i: (0, 0, 0))
    ins = [act, w_re, w_im, a, init]
    in_specs = [act_spec, w_spec, w_spec, pl.BlockSpec((2, 1, S5_LANES), lambda i: (0, 0, 0)), st_spec]
    if with_out:
        ins += [c_re, c_im]
        in_specs += [pl.BlockSpec(c_re.shape, lambda i: (0, 0, 0))] * 2
    if add is not None:
        ins.append(add)
        in_specs.append(row_spec)
    out_shape, out_specs = [], []
    if with_out:
        out_shape.append(jax.ShapeDtypeStruct((N, D_MODEL), F32))
        out_specs.append(row_spec)
    if want_ckpt:
        out_shape.append(jax.ShapeDtypeStruct((nch, 2, N_SEG, S5_LANES), F32))
        out_specs.append(pl.BlockSpec((1, 2, N_SEG, S5_LANES), lambda i: (chunk(i), 0, 0, 0)))
    out_shape.append(jax.ShapeDtypeStruct((2, N_SEG, S5_LANES), F32))
    out_specs.append(st_spec)
    res = pl.pallas_call(
        body, out_shape=out_shape, grid=(nch,), in_specs=in_specs, out_specs=out_specs,
        scratch_shapes=[pltpu.VMEM((R, S5_LANES), F32), pltpu.VMEM((R, S5_LANES), F32)],
        compiler_params=_params(("arbitrary",), VMEM_LIMIT), name=name)(*ins)
    res = list(res)
    out = res.pop(0) if with_out else None
    ckpt = res.pop(0) if want_ckpt else None
    return out, ckpt, res[0]


def s5_grads(dy, u, ckpt, b_re, b_im, c_re, c_im, lam, init_adj, *, reverse, add=None, u_off=0, du_rows=None,
             du_into=None, n_rows=None, name):
    N = n_rows if dy is None else dy.shape[0]
    du_total, du_first = du_rows if du_rows is not None else (N, 0)
    R = math.gcd(ROW_TILE, N, u_off, du_first)
    nch, jc = N // R, R // N_SEG
    W = S5_LANES

    def chunk(i):
        return i if reverse else (nch - 1 - i)

    def body(*refs):
        if dy is None:
            refs = (None,) + refs[:4] + (None, None) + refs[4:]
        dy_ref, u_ref, ck_ref, bre_ref, bim_ref, cre_ref, cim_ref, lam_ref, init_ref = refs[:9]
        k = 9
        if add is not None:
            add_ref = refs[k]
            k += 1
        k += du_into is not None
        outs = refs[k:]
        if dy is None:
            outs = outs[:4] + (None, None) + outs[4:]
        du_ref, dlam_ref, dbre_ref, dbim_ref, dcre_ref, dcim_ref, fin_ref = outs[:7]
        sr_buf, si_buf, er_buf, ei_buf, st_buf = outs[7:12]

        @pl.when(pl.program_id(0) == 0)
        def _():
            fin_ref[...] = init_ref[...]
            dlam_ref[...] = jnp.zeros_like(dlam_ref)
            dbre_ref[...] = jnp.zeros_like(dbre_ref)
            dbim_ref[...] = jnp.zeros_like(dbim_ref)
            if dy is not None:
                dcre_ref[...] = jnp.zeros_like(dcre_ref)
                dcim_ref[...] = jnp.zeros_like(dcim_ref)

        u16 = u_ref[...].astype(BF16)
        st_buf[...] = ck_ref[0]
        _project_in(u16, bre_ref, bim_ref, sr_buf, si_buf, False)
        _scan_chunk(sr_buf, si_buf, st_buf, lam_ref, jc, reverse)
        if dy is None:
            er_buf[...] = jnp.zeros_like(er_buf)
            ei_buf[...] = jnp.zeros_like(ei_buf)
        else:
            dy16 = dy_ref[...].astype(BF16)
            _project_in(dy16, cre_ref, cim_ref, er_buf, ei_buf, True)
        for lc in range(W // BLK_ST):
            sl = slice(lc * BLK_ST, (lc + 1) * BLK_ST)
            lr = jnp.broadcast_to(lam_ref[0, :, sl], (N_SEG, BLK_ST))
            li = jnp.broadcast_to(lam_ref[1, :, sl], (N_SEG, BLK_ST))

            def one(r0, spr, spi, carry, sl=sl, lr=lr, li=li):
                gr, gi, ar, ai = carry
                nr = er_buf[pl.ds(r0, N_SEG), sl] + lr * gr + li * gi
                ni = ei_buf[pl.ds(r0, N_SEG), sl] + lr * gi - li * gr
                er_buf[pl.ds(r0, N_SEG), sl] = nr
                ei_buf[pl.ds(r0, N_SEG), sl] = ni
                return nr, ni, ar + spr * nr + spi * ni, ai + spr * ni - spi * nr

            def step(ff, carry, sl=sl, one=one):
                f = jc - 1 - ff
                j = (jc - 1 - f) if reverse else f
                jp = (j + 1) if reverse else (j - 1)
                r0 = pl.multiple_of(j * N_SEG, N_SEG)
                p0 = pl.multiple_of(jp * N_SEG, N_SEG)
                return one(r0, sr_buf[pl.ds(p0, N_SEG), sl], si_buf[pl.ds(p0, N_SEG), sl], carry)

            carry = (fin_ref[0, :, sl], fin_ref[1, :, sl], dlam_ref[0, :, sl], dlam_ref[1, :, sl])
            carry = lax.fori_loop(0, jc - 1, step, carry)
            r_first = (jc - 1) * N_SEG if reverse else 0
            gr, gi, ar, ai = one(r_first, ck_ref[0, 0, :, sl], ck_ref[0, 1, :, sl], carry)
            fin_ref[0, :, sl] = gr
            fin_ref[1, :, sl] = gi
            dlam_ref[0, :, sl] = ar
            dlam_ref[1, :, sl] = ai
        tn = (((0,), (0,)), ((), ()))
        nt = (((1,), (1,)), ((), ()))
        for gb in range(N_BLOCKS):
            sl = slice(gb * BLK_ST, (gb + 1) * BLK_ST)
            cs = slice(gb * BLK_CH, (gb + 1) * BLK_CH)
            gr16 = er_buf[:, sl].astype(BF16)
            gi16 = ei_buf[:, sl].astype(BF16)
            du = (lax.dot_general(gr16, bre_ref[gb], nt, preferred_element_type=F32)
                  + lax.dot_general(gi16, bim_ref[gb], nt, preferred_element_type=F32))
            if add is not None:
                du = du + add_ref[:, cs]
            du_ref[:, cs] = du
            ub = u16[:, cs]
            dbre_ref[gb] += lax.dot_general(ub, gr16, tn, preferred_element_type=F32)
            dbim_ref[gb] += lax.dot_general(ub, gi16, tn, preferred_element_type=F32)
            if dy is not None:
                dyb = dy16[:, cs]
                dcre_ref[gb] += lax.dot_general(sr_buf[:, sl].astype(BF16), dyb, tn, preferred_element_type=F32)
                dcim_ref[gb] -= lax.dot_general(si_buf[:, sl].astype(BF16), dyb, tn, preferred_element_type=F32)

    row_spec = pl.BlockSpec((R, D_MODEL), lambda i: (chunk(i), 0))
    st_spec = pl.BlockSpec((2, N_SEG, W), lambda i: (0, 0, 0))
    wb_spec = pl.BlockSpec(b_re.shape, lambda i: (0, 0, 0))
    wc_spec = pl.BlockSpec(c_re.shape, lambda i: (0, 0, 0))
    ins = [dy, u, ckpt, b_re, b_im, c_re, c_im, lam, init_adj]
    u_spec = pl.BlockSpec((R, D_MODEL), lambda i: (chunk(i) + u_off // R, 0))
    in_specs = [row_spec, u_spec, pl.BlockSpec((1, 2, N_SEG, W), lambda i: (chunk(i), 0, 0, 0)),
                wb_spec, wb_spec, wc_spec, wc_spec, pl.BlockSpec((2, 1, W), lambda i: (0, 0, 0)), st_spec]
    if dy is None:
        ins, in_specs = ins[1:5] + ins[7:], in_specs[1:5] + in_specs[7:]
    if add is not None:
        ins.append(add)
        in_specs.append(row_spec)
    aliases = {}
    if du_into is not None:
        aliases[len(ins)] = 0
        ins.append(du_into)
        in_specs.append(pl.BlockSpec(memory_space=pl.ANY))
    du_spec = pl.BlockSpec((R, D_MODEL), lambda i: (chunk(i) + du_first // R, 0))
    out_shape = [jax.ShapeDtypeStruct((du_total, D_MODEL), F32), jax.ShapeDtypeStruct((2, N_SEG, W), F32),
                 jax.ShapeDtypeStruct(b_re.shape, F32), jax.ShapeDtypeStruct(b_re.shape, F32),
                 jax.ShapeDtypeStruct(c_re.shape, F32), jax.ShapeDtypeStruct(c_re.shape, F32),
                 jax.ShapeDtypeStruct((2, N_SEG, W), F32)]
    out_specs = [du_spec, st_spec, wb_spec, wb_spec, wc_spec, wc_spec, st_spec]
    if dy is None:
        out_shape, out_specs = out_shape[:4] + out_shape[6:], out_specs[:4] + out_specs[6:]
    res = pl.pallas_call(
        body, out_shape=out_shape, grid=(nch,), in_specs=in_specs, out_specs=out_specs, input_output_aliases=aliases,
        scratch_shapes=[pltpu.VMEM((R, W), F32) for _ in range(4)] + [pltpu.VMEM((2, N_SEG, W), F32)],
        compiler_params=_params(("arbitrary",), VMEM_LIMIT), name=name)(*ins)
    return res if dy is not None else [*res[:4], None, None, res[4]]


def adamw(w, g, m, v, name="adamw", after=None):
    n, d = w.shape
    lanes = -(-d // 128) * 128
    tm = n
    while tm * lanes * 4 > (1 << 20) and tm % 16 == 0:
        tm //= 2
    c1 = 1.0 - ADAM_B1 ** ADAM_STEP
    c2 = 1.0 - ADAM_B2 ** ADAM_STEP

    def body(w_ref, g_ref, m_ref, v_ref, *rest):
        d_ref, nm_ref, nv_ref = rest[-3:]
        g_ = g_ref[...]
        m_ = ADAM_B1 * m_ref[...] + (1.0 - ADAM_B1) * g_
        v_ = ADAM_B2 * v_ref[...] + (1.0 - ADAM_B2) * (g_ * g_)
        d_ref[...] = -ADAM_LR * ((m_ / c1) / (jnp.sqrt(v_ / c2) + ADAM_EPS) + ADAM_WD * w_ref[...])
        nm_ref[...] = m_
        nv_ref[...] = v_

    spec = pl.BlockSpec((tm, d), lambda i: (i, 0))
    extra = [] if after is None else [after]
    return pl.pallas_call(
        body, out_shape=[jax.ShapeDtypeStruct((n, d), F32)] * 3, grid=(n // tm,),
        in_specs=[spec] * 4 + [pl.BlockSpec(memory_space=pl.ANY)] * len(extra), out_specs=[spec] * 3,
        compiler_params=_params(("parallel",), VMEM_LIMIT), name=name)(w, g, m, v, *extra)


def _coords():
    return lax.axis_index("x"), lax.axis_index("y"), lax.axis_index("c")


def exchange(arrays, out_shapes, remote, local, name, aliases=None):
    n_in, n_out, n_rem, n_loc = len(arrays), len(out_shapes), len(remote), len(local)

    def at(ref, idx):
        return ref if idx is None else ref.at[idx]

    def body(*refs):
        ins, outs = refs[:n_in], refs[n_in:n_in + n_out]
        send_sems, recv_sems, local_sems = refs[n_in + n_out:]
        me = _coords()
        sends, recvs = [], []
        for k, (flip, ii, src_at, oi, dst_at) in enumerate(remote):
            peer = (me[0] ^ flip[0], me[1] ^ flip[1], me[2] ^ flip[2])
            src = at(ins[ii], src_at(me, peer))
            sends.append(pltpu.make_async_remote_copy(
                src_ref=src, dst_ref=at(outs[oi], dst_at(me)), send_sem=send_sems.at[k], recv_sem=recv_sems.at[k],
                device_id=peer, device_id_type=MESH))
            recvs.append(pltpu.make_async_remote_copy(
                src_ref=src, dst_ref=at(outs[oi], dst_at(peer)), send_sem=send_sems.at[k], recv_sem=recv_sems.at[k],
                device_id=peer, device_id_type=MESH))
        locs = [pltpu.make_async_copy(at(ins[ii], src_at(me)), at(outs[oi], dst_at(me)), local_sems.at[k])
                for k, (ii, src_at, oi, dst_at) in enumerate(local)]
        for cp in locs + sends:
            cp.start()
        for cp in recvs:
            cp.wait_recv()
        for cp in sends:
            cp.wait_send()
        for cp in locs:
            cp.wait()

    hbm = pl.BlockSpec(memory_space=pl.ANY)
    return pl.pallas_call(
        body, out_shape=list(out_shapes), in_specs=[hbm] * n_in, out_specs=[hbm] * n_out,
        scratch_shapes=[pltpu.SemaphoreType.DMA((n_rem,)), pltpu.SemaphoreType.DMA((n_rem,)),
                        pltpu.SemaphoreType.DMA((max(n_loc, 1),))],
        input_output_aliases=aliases or {}, name=name)(*arrays)


ALL_FLIPS = [(dx, dy, dc) for dx in (0, 1) for dy in (0, 1) for dc in (0, 1)][1:]
CHIP_FLIPS = [(1, 0, 0), (0, 1, 0), (1, 1, 0)]
CORE_FLIP = (0, 0, 1)


def _dev_index(p):
    return 4 * p[0] + 2 * p[1] + p[2]


def _chip_index(p):
    return 2 * p[0] + p[1]


def _gather(xs, flips, index, n, name):
    arrays = [x[None] for x in xs]
    outs = [jax.ShapeDtypeStruct((n,) + x.shape, x.dtype) for x in xs]
    remote = [(f, a, lambda me, peer: (0,), a, lambda s: (index(s),)) for a in range(len(xs)) for f in flips]
    local = [(a, lambda me: (0,), a, lambda me: (index(me),)) for a in range(len(xs))]
    return exchange(arrays, outs, remote, local, name)


def allgather_devices(x, name):
    return _gather([x], ALL_FLIPS, _dev_index, N_DEV, name)[0]


def allgather_chips(xs, name):
    return _gather(xs, CHIP_FLIPS, _chip_index, N_CHIP, name)


def gather_halves(xs, name):
    n = len(xs)
    nk = n * len(CHIP_FLIPS)

    def body(*refs):
        ins, outs = refs[:n], refs[n:2 * n]
        ici_send, ici_recv, d2d_send, d2d_recv = refs[2 * n:]
        me = _coords()
        sibling = (me[0], me[1], 1 - me[2])
        first, passed, landed = [], [], []
        for a in range(n):
            half = ins[a].shape[0] // 2
            mine = ins[a].at[pl.ds(pl.multiple_of(me[2] * half, 16), half)]
            for j, flip in enumerate(CHIP_FLIPS):
                k = a * len(CHIP_FLIPS) + j
                peer = (me[0] ^ flip[0], me[1] ^ flip[1], me[2])
                first.append(pltpu.make_async_remote_copy(
                    src_ref=mine, dst_ref=outs[a].at[_chip_index(me), me[2]], send_sem=ici_send.at[k],
                    recv_sem=ici_recv.at[k], device_id=peer, device_id_type=MESH))
                arrived = outs[a].at[_chip_index(peer), me[2]]
                landed.append(pltpu.make_async_remote_copy(
                    src_ref=mine, dst_ref=arrived, send_sem=ici_send.at[k], recv_sem=ici_recv.at[k],
                    device_id=peer, device_id_type=MESH))
                passed.append(pltpu.make_async_remote_copy(
                    src_ref=arrived, dst_ref=arrived, send_sem=d2d_send.at[k], recv_sem=d2d_recv.at[k],
                    device_id=sibling, device_id_type=MESH))
        for cp in first:
            cp.start()
        for k in range(nk):
            landed[k].wait_recv()
            passed[k].start()
        for a in range(n):
            for j, flip in enumerate(CHIP_FLIPS):
                k = a * len(CHIP_FLIPS) + j
                peer_chip = _chip_index((me[0] ^ flip[0], me[1] ^ flip[1]))
                from_sibling = outs[a].at[peer_chip, 1 - me[2]]
                pltpu.make_async_remote_copy(
                    src_ref=from_sibling, dst_ref=from_sibling, send_sem=d2d_send.at[k], recv_sem=d2d_recv.at[k],
                    device_id=sibling, device_id_type=MESH).wait_recv()
        for cp in first + passed:
            cp.wait_send()

    hbm = pl.BlockSpec(memory_space=pl.ANY)
    return pl.pallas_call(
        body, out_shape=[jax.ShapeDtypeStruct((N_CHIP, 2, x.shape[0] // 2, x.shape[1]), x.dtype) for x in xs],
        in_specs=[hbm] * n, out_specs=[hbm] * n,
        scratch_shapes=[pltpu.SemaphoreType.DMA((nk,)) for _ in range(4)], name=name)(*xs)


HBM_SPEC = pl.BlockSpec(memory_space=pltpu.HBM)
SEM_SPEC = pl.BlockSpec(memory_space=pltpu.SEMAPHORE)
DATAFLOW = pltpu.SideEffectType.DATAFLOW_SIDE_EFFECTING


def _at(ref, idx):
    return ref if idx is None else ref.at[idx]


def _peer(me, flip):
    return (me[0] ^ flip[0], me[1] ^ flip[1], me[2] ^ flip[2])


def exchange_start(arrays, land_shapes, remote, name, after=None):
    n_in, n_out, nk = len(arrays), len(land_shapes), len(remote)
    after = list(after or [])
    n_after = len(after)

    def body(*refs):
        srcs, lands = refs[:n_in], refs[n_in:n_in + n_out]
        first_out = n_in + n_out + n_after
        send_sems, recv_sems, token = refs[first_out], refs[first_out + 1], refs[-1]
        me = _coords()
        for k, (flip, ii, src_at, oi, dst_at) in enumerate(remote):
            peer = _peer(me, flip)
            pltpu.make_async_remote_copy(
                src_ref=_at(srcs[ii], src_at(me, peer)), dst_ref=_at(lands[oi], dst_at(me)), send_sem=send_sems.at[k],
                recv_sem=recv_sems.at[k], device_id=peer, device_id_type=MESH).start()
        token[...] = jnp.zeros_like(token)

    lands = [lax.empty(s.shape, s.dtype) for s in land_shapes]
    bufs = list(arrays) + lands
    out = pl.pallas_call(
        body, name=name,
        out_shape=(pltpu.SemaphoreType.DMA((nk,)), pltpu.SemaphoreType.DMA((nk,)),
                   *[pltpu.HBM(b.shape, b.dtype) for b in bufs], jax.ShapeDtypeStruct((8, 128), F32)),
        in_specs=[HBM_SPEC] * len(bufs) + [pl.BlockSpec(memory_space=pl.ANY)] * n_after,
        out_specs=(SEM_SPEC, SEM_SPEC, *[HBM_SPEC] * len(bufs), pl.BlockSpec(memory_space=pltpu.VMEM)),
        input_output_aliases={a: 2 + a for a in range(len(bufs))},
        compiler_params=pltpu.CompilerParams(has_side_effects=DATAFLOW),
    )(*[pltpu.with_memory_space_constraint(b, pltpu.HBM) for b in bufs], *after)
    flight = (out[0], out[1], list(out[2:2 + n_in]), list(out[2 + n_in:2 + n_in + n_out]), remote)
    return flight, out[-1]


def exchange_wait(flight, after, name):
    send_sems, recv_sems, arrays, lands, remote = flight
    n_in, n_out = len(arrays), len(lands)
    after = list(after) if isinstance(after, (list, tuple)) else [after]

    def body(*refs):
        srcs, lnds = refs[:n_in], refs[n_in:n_in + n_out]
        s_sems, r_sems = refs[n_in + n_out], refs[n_in + n_out + 1]
        me = _coords()
        for k, (flip, ii, src_at, oi, dst_at) in enumerate(remote):
            peer = _peer(me, flip)
            copy = pltpu.make_async_remote_copy(
                src_ref=_at(srcs[ii], src_at(me, peer)), dst_ref=_at(lnds[oi], dst_at(peer)), send_sem=s_sems.at[k],
                recv_sem=r_sems.at[k], device_id=peer, device_id_type=MESH)
            copy.wait_send()
            copy.wait_recv()

    bufs = list(arrays) + list(lands)
    out = pl.pallas_call(
        body, name=name,
        out_shape=tuple(pltpu.HBM(b.shape, b.dtype) for b in bufs),
        in_specs=[HBM_SPEC] * len(bufs) + [SEM_SPEC, SEM_SPEC] + [pl.BlockSpec(memory_space=pl.ANY)] * len(after),
        out_specs=tuple([HBM_SPEC] * len(bufs)),
        input_output_aliases={a: a for a in range(len(bufs))},
        compiler_params=pltpu.CompilerParams(has_side_effects=DATAFLOW),
    )(*bufs, send_sems, recv_sems, *after)
    return list(out[:n_in]), list(out[n_in:])


def _half_tile(h, cd):
    return h if h * cd * 4 <= (1 << 20) else math.gcd(512, h)


def pair_add(g, got, core, out_dtype, name):
    _, _, h, cd = g.shape
    th = _half_tile(h, cd)

    def body(c_ref, g_ref, got_ref, o_ref):
        o_ref[0] = (g_ref[0, 0] + got_ref[0]).astype(o_ref.dtype)

    return pl.pallas_call(
        body, out_shape=jax.ShapeDtypeStruct((N_CHIP, h, cd), out_dtype),
        grid_spec=pltpu.PrefetchScalarGridSpec(
            num_scalar_prefetch=1, grid=(N_CHIP, h // th),
            in_specs=[pl.BlockSpec((1, 1, th, cd), lambda q, i, c: (q, c[0], i, 0)),
                      pl.BlockSpec((1, th, cd), lambda q, i, c: (q, i, 0))],
            out_specs=pl.BlockSpec((1, th, cd), lambda q, i, c: (q, i, 0))),
        compiler_params=_params(("parallel", "parallel"), VMEM_LIMIT), name=name)(core, g, got)


def sum_chips(parts, sums, place, name):
    _, h, cd = parts.shape
    th = _half_tile(h, cd)

    def body(pc_ref, p_ref, own_ref, o_ref):
        acc = None
        for q in range(N_CHIP):
            term = jnp.where(pc_ref[1] == q, own_ref[0], p_ref[q]).astype(F32)
            acc = term if acc is None else acc + term
        o_ref[0] = acc

    return pl.pallas_call(
        body, out_shape=jax.ShapeDtypeStruct((2, h, cd), F32),
        grid_spec=pltpu.PrefetchScalarGridSpec(
            num_scalar_prefetch=1, grid=(h // th,),
            in_specs=[pl.BlockSpec((N_CHIP, th, cd), lambda i, pc: (0, i, 0)),
                      pl.BlockSpec((1, th, cd), lambda i, pc: (pc[1], i, 0))],
            out_specs=pl.BlockSpec((1, th, cd), lambda i, pc: (pc[0], i, 0))),
        compiler_params=_params(("parallel",), VMEM_LIMIT), name=name)(place, parts, sums)


def to_segments(a, n_ctx):
    def one(p):
        n = p.shape[0]
        return p.reshape(N_SEG, n // N_SEG, -1).transpose(1, 0, 2).reshape(n, -1)
    return jnp.concatenate([one(a[:n_ctx]), one(a[n_ctx:])], axis=0) if n_ctx else one(a)


def rows_to_segments(a, n_ctx, name):
    n, d = a.shape
    tj = n_ctx // N_SEG
    per_seg = (n - n_ctx) // N_SEG // tj
    assert n_ctx % N_SEG == 0 and (n - n_ctx) % (N_SEG * tj) == 0

    def body(*refs):
        out_ref, slabs = refs[N_SEG:]
        for c in range(d // HEAD_LANES):
            cols = slice(c * HEAD_LANES, (c + 1) * HEAD_LANES)
            for seg in range(N_SEG):
                slabs[c, pl.ds(seg, tj, stride=N_SEG), :] = refs[seg][:, cols]
            out_ref[:, cols] = slabs[c]

    def seg_spec(seg):
        return pl.BlockSpec((tj, d), lambda i: (jnp.where(i == 0, seg, N_SEG + seg * per_seg + i - 1), 0))

    return pl.pallas_call(
        body, out_shape=jax.ShapeDtypeStruct((n, d), a.dtype), grid=(1 + per_seg,),
        in_specs=[seg_spec(seg) for seg in range(N_SEG)], out_specs=pl.BlockSpec((N_SEG * tj, d), lambda i: (i, 0)),
        scratch_shapes=[pltpu.VMEM((d // HEAD_LANES, N_SEG * tj, HEAD_LANES), a.dtype)],
        compiler_params=_params(("parallel",), VMEM_LIMIT), name=name)(*[a] * N_SEG)


def rope_tables(n_ctx, n_lat):
    f32 = np.float32
    rows = n_lat // GRID_W
    row = np.repeat(np.arange(rows), GRID_W).astype(f32)
    col = np.tile(np.arange(GRID_W), rows).astype(f32)
    d = QK_ROPE_DIM // 2
    inv = (f32(1.0) / np.power(f32(ROPE_THETA), np.arange(0, d, 2, dtype=f32) / f32(d))).astype(f32)
    ang = np.concatenate([row[:, None] * inv[None, :], col[:, None] * inv[None, :]], axis=1).astype(f32)
    cos = np.concatenate([np.ones((n_ctx, d), f32), np.cos(ang)], axis=0)
    sin = np.concatenate([np.zeros((n_ctx, d), f32), np.sin(ang)], axis=0)
    q = QK_ROPE_DIM // 4
    T = n_ctx + n_lat
    ones, zeros = np.ones((T, QK_NOPE_DIM), f32), np.zeros((T, QK_NOPE_DIM), f32)
    tail, z8 = np.zeros((T, HEAD_LANES - QK_DIM), f32), np.zeros((T, q), f32)
    cr, cc, sr, sc = cos[:, :q], cos[:, q:], sin[:, :q], sin[:, q:]
    cos_t = np.concatenate([ones, cr, cr, cc, cc, tail], axis=1)
    sin_next = np.concatenate([zeros, -sr, z8, -sc, z8, tail], axis=1)
    sin_prev = np.concatenate([zeros, z8, sr, z8, sc, tail], axis=1)
    return tuple(jnp.asarray(t, F32) for t in (cos_t, sin_next, sin_prev))


def pad_heads(w, used):
    k = w.shape[0]
    return jnp.pad(w.reshape(k, MLA_HEADS, used), ((0, 0), (0, 0), (0, HEAD_LANES - used))).reshape(k, -1)


def unpad_heads(w, used):
    k = w.shape[0]
    return w.reshape(k, MLA_HEADS, HEAD_LANES)[:, :, :used].reshape(k, MLA_HEADS * used)


def rotary_spread():
    lane = np.arange(MLA_HEADS * HEAD_LANES) % HEAD_LANES
    return jnp.asarray(lane[None, :] == (QK_NOPE_DIM + np.arange(QK_ROPE_DIM))[:, None], BF16)


def s5_discretise(a_re, a_im, log_step, b_re, b_im):
    dt = jnp.exp(log_step)[:, None]
    mag = jnp.exp(a_re * dt)
    lb_re = mag * jnp.cos(a_im * dt)
    lb_im = mag * jnp.sin(a_im * dt)
    den = a_re * a_re + a_im * a_im
    nr = lb_re - 1.0
    f_re = ((nr * a_re + lb_im * a_im) / den)[:, None, :]
    f_im = ((lb_im * a_re - nr * a_im) / den)[:, None, :]
    return lb_re, lb_im, f_re * b_re - f_im * b_im, f_re * b_im + f_im * b_re


def s5_block_weights(lb_re, lb_im, bb_re, bb_im, c_re, c_im):
    eye = jnp.eye(GROUPS_PER_BLOCK, dtype=F32)
    lam = jnp.stack([lb_re.reshape(1, S5_LANES), lb_im.reshape(1, S5_LANES)])

    def b_blocks(bb):
        t = bb.reshape(N_BLOCKS, GROUPS_PER_BLOCK, S5_GROUP, S5_STATE)
        return jnp.einsum("bgcp,gh->bgchp", t, eye).reshape(N_BLOCKS, BLK_CH, BLK_ST).astype(BF16)

    def c_blocks(cc):
        t = cc.reshape(N_BLOCKS, GROUPS_PER_BLOCK, S5_GROUP, S5_STATE)
        return jnp.einsum("bgcp,gh->bgphc", t, eye).reshape(N_BLOCKS, BLK_ST, BLK_CH).astype(BF16)

    return lam, b_blocks(bb_re), b_blocks(bb_im), c_blocks(c_re), c_blocks(c_im)


def b_block_diag(db):
    t = db.reshape(N_BLOCKS, GROUPS_PER_BLOCK, S5_GROUP, GROUPS_PER_BLOCK, S5_STATE)
    return jnp.einsum("bgchp,gh->bgcp", t, jnp.eye(GROUPS_PER_BLOCK, dtype=F32)).reshape(S5_GROUPS, S5_GROUP, S5_STATE)


def c_block_diag(dc):
    t = dc.reshape(N_BLOCKS, GROUPS_PER_BLOCK, S5_STATE, GROUPS_PER_BLOCK, S5_GROUP)
    return jnp.einsum("bgphc,gh->bgcp", t, jnp.eye(GROUPS_PER_BLOCK, dtype=F32)).reshape(S5_GROUPS, S5_GROUP, S5_STATE)


def conj(a):
    return jnp.stack([a[0], -a[1]])


def _narrow(shape):
    return len(shape) >= 2 and shape[-1] < min(HEAD_LANES, shape[-2])


def _stored(a):
    return jnp.swapaxes(a, -1, -2) if _narrow(a.shape) else a


def _stored_shape(shape):
    return tuple(shape[:-2]) + (shape[-1], shape[-2]) if _narrow(shape) else tuple(shape)


def _from_stored(a, shape):
    return jnp.swapaxes(a, -1, -2) if _narrow(shape) else a


PACK_TILE = 16 * 128


def pack_flat(parts, dtype, multiple=PACK_TILE):
    flat = [p.reshape(-1).astype(dtype) for p in parts]
    sizes = [f.shape[0] for f in flat]
    total = sum(sizes)
    pad = (-total) % multiple
    if pad:
        flat.append(jnp.zeros((pad,), dtype))
    offs = np.cumsum([0] + sizes)[:-1].tolist()
    return jnp.concatenate(flat).reshape(-1, 128), offs


def unpack_flat(buf, offs, shapes):
    flat = buf.reshape(-1)
    return [flat[o:o + int(np.prod(s))].reshape(s) for o, s in zip(offs, shapes)]


def s5_forward(p1, n_ctx, dirs):
    saved = []
    y = None
    ctx_rows, lat_rows = (0, n_ctx), (n_ctx, p1.shape[0] - n_ctx)
    zeros_tile = jnp.zeros((2, N_SEG, S5_LANES), F32)
    zeros_row = jnp.zeros((2, 1, S5_LANES), F32)
    for k, (lam, b_re, b_im, c_re, c_im) in enumerate(dirs):
        rev = k == 1
        last = 0 if rev else N_SEG - 1
        _, _, fin = s5_scan(p1, b_re, b_im, lam, zeros_tile, reverse=rev, rows=ctx_rows, name=f"s5_ctx_finals{k}")
        carry_c = s5_chain(fin, zeros_row, lam, n_ctx // N_SEG, rev, name=f"s5_ctx_chain{k}")
        _, ck_c, fin_c = s5_scan(p1, b_re, b_im, lam, carry_c, reverse=rev, want_ckpt=True, rows=ctx_rows,
                                 name=f"s5_ctx_scan{k}")
        s0 = fin_c[:, last:last + 1, :]
        _, _, fin = s5_scan(p1, b_re, b_im, lam, zeros_tile, reverse=rev, rows=lat_rows, name=f"s5_lat_finals{k}")
        carry_l = s5_chain(fin, s0, lam, lat_rows[1] // N_SEG, rev, name=f"s5_lat_chain{k}")
        y, ck_l, _ = s5_scan(p1, b_re, b_im, lam, carry_l, reverse=rev, c_re=c_re, c_im=c_im, add=y,
                             want_ckpt=True, rows=lat_rows, name=f"s5_lat_scan{k}")
        saved.append((ck_c, ck_l))
    return y, saved


def s5_backward(dy_l, du_extra_l, p1, n_ctx, dirs, saved):
    n_lat = p1.shape[0] - n_ctx
    zeros_tile = jnp.zeros((2, N_SEG, S5_LANES), F32)
    zeros_row = jnp.zeros((2, 1, S5_LANES), F32)
    du_l, du_c = du_extra_l, None
    grads = []
    for k, (lam, b_re, b_im, c_re, c_im) in enumerate(dirs):
        rev = k == 1
        lam_c = conj(lam)
        ck_c, ck_l = saved[k]
        first = N_SEG - 1 if rev else 0
        _, _, fin = s5_scan(dy_l, c_re, c_im, lam_c, zeros_tile, reverse=not rev, adjoint=True,
                            name=f"s5_lat_adj_finals{k}")
        carry = s5_chain(fin, zeros_row, lam_c, n_lat // N_SEG, not rev, name=f"s5_lat_adj_chain{k}")
        whole = k == len(dirs) - 1
        du_l, dlam_l, dbr_l, dbi_l, dcr_l, dci_l, fin_a = s5_grads(
            dy_l, p1, ck_l, b_re, b_im, c_re, c_im, lam, carry, reverse=rev, add=du_l, u_off=n_ctx,
            du_rows=(p1.shape[0], n_ctx) if whole else None, name=f"s5_lat_grads{k}")
        g0 = fin_a[:, first:first + 1, :]
        carry = s5_chain(zeros_tile, g0, lam_c, n_ctx // N_SEG, not rev, name=f"s5_ctx_adj_chain{k}")
        du_c, dlam_c, dbr_c, dbi_c, _, _, _ = s5_grads(
            None, p1, ck_c, b_re, b_im, c_re, c_im, lam, carry, reverse=rev, add=du_c, n_rows=n_ctx,
            du_rows=(p1.shape[0], 0) if whole else None, du_into=du_l if whole else None, name=f"s5_ctx_grads{k}")
        dlam = jnp.sum(dlam_l + dlam_c, axis=1)
        grads.append((dlam, b_block_diag(dbr_l + dbr_c), b_block_diag(dbi_l + dbi_c),
                      c_block_diag(dcr_l), c_block_diag(dci_l)))
    return du_c, grads


def local_step(x, ctx, target, mod, w, late=None, reducer=None):
    L, Lc = x.shape[0], ctx.shape[0]
    T = L + Lc
    assert L % Lc == 0 and Lc % (2 * N_SEG) == 0 and L % GRID_W == 0
    D = D_MODEL
    X0 = (ctx, x)

    def mod_of(i, j):
        return mod[i, :, j, :][:, None, :]

    def vec(v):
        return v.reshape(1, 1, -1).astype(F32)

    g0 = vec(w["norm_g"][0])
    H0, p0 = norm_proj(X0, g0, mod_of(0, 1), mod_of(0, 0), w["mla_w_in"], Lc, "l0_norm_in")
    cq = Rows(p0, Q_LORA_RANK, col_blk=D // Q_LORA_RANK)
    ckv = Rows(p0, KV_LORA_RANK, col_blk=(D + Q_LORA_RANK) // KV_LORA_RANK)
    kr = Rows(p0, HEAD_LANES, col_blk=(D + Q_LORA_RANK + KV_LORA_RANK) // HEAD_LANES)
    qng, kvng = vec(w["mla_q_norm"]), vec(w["mla_kv_norm"])
    tabs = rope_tables(Lc, L)
    spread = jnp.pad(rotary_spread(), ((0, HEAD_LANES - QK_ROPE_DIM), (0, 0)))
    if late is not None:
        w = {**w, **late["qkv"](p0)}
    w_uq_p = pad_heads(w["mla_w_uq"], QK_DIM)
    w_ukv3 = w["mla_w_ukv"].reshape(KV_LORA_RANK, MLA_HEADS, QK_NOPE_DIM + V_HEAD_DIM)
    w_kn_p = pad_heads(w_ukv3[:, :, :QK_NOPE_DIM].reshape(KV_LORA_RANK, -1), QK_NOPE_DIM)
    w_v = w_ukv3[:, :, QK_NOPE_DIM:].reshape(KV_LORA_RANK, -1)
    qb, qn = q_heads(cq, qng, w_uq_p, tabs)
    kb, vb, kvn = kv_heads(ckv, kvng, w_kn_p, w_v, kr, spread, tabs)
    o, lse, qs = attn_fwd(qb, kb, vb, Lc)
    if late is not None:
        w = {**w, **late["out"](o)}
    X1, og, out0 = mla_post_fwd(o, p0, X0, mod_of(0, 2), w["mla_w_out"], Lc)

    if late is not None:
        w = {**w, **late["l1"](X1)}
    X1p = rows_to_segments(X1, Lc, "l1_to_segments")
    tgt_p = to_segments(target, 0)
    g1 = vec(w["norm_g"][1])
    H1, p1 = norm_proj(X1p, g1, mod_of(1, 1), mod_of(1, 0), w["s5_w_in"], Lc, "l1_norm_in")
    disc_fn = lambda *a: tuple(zip(*[s5_discretise(a[0][k], a[1][k], a[2][k], a[3][k], a[4][k]) for k in range(2)]))
    disc, disc_vjp = jax.vjp(disc_fn, w["s5_a_re"], w["s5_a_im"], w["s5_log_step"], _stored(w["s5_b_re"]),
                             _stored(w["s5_b_im"]))
    dirs = [s5_block_weights(disc[0][k], disc[1][k], disc[2][k], disc[3][k], w["s5_c_re"][k], w["s5_c_im"][k])
            for k in range(2)]
    y_ssm, s5_saved = s5_forward(p1, Lc, dirs)

    row = lambda v: v.reshape(1, D).astype(F32)
    (lvec, dX2, d_yssm, d_u_act, d_z1, d_fg, d_gt1, d_bg, d_d, gw_glu, gw_out) = s5_tail(
        y_ssm, p1, X1p, tgt_p, Lc, row(w["s5_d"]), row(w["s5_b_glu"]), mod[1, 1:2, 2, :], row(w["final_g"]),
        w["s5_w_glu"], w["s5_w_out"])
    loss = jnp.sum(lvec)
    gw = {"final_g": d_fg.reshape(D), "s5_b_glu": d_bg.reshape(D), "s5_d": d_d.reshape(D),
          "s5_w_glu": gw_glu, "s5_w_out": gw_out}
    dmod = {}

    du_p, s5_g = s5_backward(d_yssm, d_u_act, p1, Lc, dirs, s5_saved)
    d_disc = tuple(tuple(s5_g[k][j - 1].reshape(disc[j][k].shape) if j >= 2 else
                         s5_g[k][0][j].reshape(disc[j][k].shape) for k in range(2)) for j in range(4))
    gw["s5_a_re"], gw["s5_a_im"], gw["s5_log_step"], d_bt_re, d_bt_im = disc_vjp(d_disc)
    gw["s5_b_re"], gw["s5_b_im"] = jnp.swapaxes(d_bt_re, -1, -2), jnp.swapaxes(d_bt_im, -1, -2)
    gw["s5_c_re"] = jnp.stack([s5_g[0][3], s5_g[1][3]])
    gw["s5_c_im"] = jnp.stack([s5_g[0][4], s5_g[1][4]])
    gw["s5_w_in"] = mm_tn(H1, du_p, name="l1_in_dw", b_more=d_z1)
    if reducer is not None:
        g1 = g1 + reducer["l1"][0]({n: gw.pop(n) for n in LAYER1_MATS})[0, 0]
    d_X1, d_g1, d_sc1, d_sh1 = norm_proj_bwd([(du_p, 0, 0), (d_z1, D, Lc)], w["s5_w_in"], X1p, g1, mod_of(1, 1),
                                             mod_of(1, 0), dX2, Lc, "l1_norm_in_bwd", dx_from_segments=True)
    d_gt1_full = jnp.concatenate([jnp.zeros((1, 1, D), F32), d_gt1[None]], axis=0)
    dmod[1] = (d_sh1, d_sc1, d_gt1_full)

    d_o, d_z0, d_gt0, gw["mla_w_out"] = mla_post_bwd(d_X1, out0, og, o, p0, mod_of(0, 2), w["mla_w_out"], Lc)
    if reducer is not None:
        started = reducer["l1"][1](d_o)[0, 0] + reducer["out"][0]({"mla_w_out": gw.pop("mla_w_out")})[0, 0]
        tabs = (tabs[0] + started,) + tabs[1:]
    d_q, dk_p, d_v = attn_bwd(qb, kb, vb, o, d_o, lse, tabs, Lc)
    if reducer is not None:
        tabs = (tabs[0] + reducer["out"][1](d_q)[0, 0],) + tabs[1:]
    d_k, d_kr = heads_unrope(dk_p, tabs, spread,
                             name="l0_k_unrope")
    d_qn = mm_nt(d_q, w_uq_p, name="l0_uq_dx")
    gw["mla_w_uq"] = unpad_heads(mm_tn(qn, d_q, name="l0_uq_dw"), QK_DIM)
    d_kvn = mm_nt(d_k, w_kn_p, name="l0_ukn_dx") + mm_nt(d_v, w_v, name="l0_uv_dx")
    dw_kn = unpad_heads(mm_tn(kvn, d_k, name="l0_ukn_dw"), QK_NOPE_DIM).reshape(KV_LORA_RANK, MLA_HEADS, QK_NOPE_DIM)
    dw_v = mm_tn(kvn, d_v, name="l0_uv_dw").reshape(KV_LORA_RANK, MLA_HEADS, V_HEAD_DIM)
    gw["mla_w_ukv"] = jnp.concatenate([dw_kn, dw_v], axis=-1).reshape(KV_LORA_RANK, -1)
    d_cq, d_qng = rowwise_bwd(f_rms, [cq], [qng], [d_qn], [0], [0], T, 0, "l0_qnorm_bwd")
    d_ckv, d_kvng = rowwise_bwd(f_rms, [ckv], [kvng], [d_kvn], [0], [0], T, 0, "l0_kvnorm_bwd")
    gw["mla_q_norm"] = d_qng.reshape(-1)
    gw["mla_kv_norm"] = d_kvng.reshape(-1)
    o_cq, o_ckv = D, D + Q_LORA_RANK
    o_kr = o_ckv + KV_LORA_RANK
    d_head = jnp.concatenate([d_cq, d_ckv, d_kr], axis=1)
    gw["mla_w_in"] = jnp.concatenate([mm_tn(H0, d_head, name="l0_in_dw_head")[:, :P0_HEAD],
                                      mm_tn(H0, d_z0, name="l0_in_dw_z")], axis=1)
    dx, d_g0, d_sc0, d_sh0 = norm_proj_bwd(
        [(d_z0, 0, 0), (d_cq, o_cq, 0), (d_ckv, o_ckv, 0), (d_kr, o_kr, 0)], w["mla_w_in"], X0, g0, mod_of(0, 1),
        mod_of(0, 0), d_X1, Lc, "l0_norm_in_bwd", latent_dx_only=True)
    dmod[0] = (d_sh0, d_sc0, d_gt0)
    gw["norm_g"] = jnp.stack([d_g0.reshape(D), d_g1.reshape(D)])
    dmod_arr = jnp.stack([jnp.stack([dmod[i][j][:, 0, :] for j in range(3)], axis=1) for i in range(2)])
    ready = {**reducer["l1"][2](dx), **reducer["out"][2](dx)} if reducer is not None else {}
    return loss, dx, dmod_arr, gw, ready


SHARDED = {
    "mla_w_in": 1, "mla_w_uq": 1, "mla_w_ukv": 1, "mla_w_out": 0,
    "s5_w_in": 1, "s5_w_glu": 0, "s5_w_out": 0, "s5_d": 0, "s5_b_glu": 0,
}
SHARDED_MATS = ["mla_w_in", "mla_w_uq", "mla_w_ukv", "mla_w_out", "s5_w_in", "s5_w_glu", "s5_w_out"]
SHARDED_VECS = ["s5_d", "s5_b_glu"]
REPLICATED = ["norm_g", "mla_q_norm", "mla_kv_norm", "s5_a_re", "s5_a_im", "s5_log_step", "s5_b_re", "s5_b_im",
              "s5_c_re", "s5_c_im", "final_g"]
WEIGHT_ORDER = ["c_ctx", "ada_w", "ada_b", "norm_g", "mla_w_in", "mla_q_norm", "mla_w_uq", "mla_kv_norm", "mla_w_ukv",
                "mla_w_out", "s5_w_in", "s5_a_re", "s5_a_im", "s5_log_step", "s5_b_re", "s5_b_im", "s5_c_re", "s5_c_im",
                "s5_d", "s5_w_glu", "s5_b_glu", "s5_w_out", "final_g"]


P0_HEAD = Q_LORA_RANK + KV_LORA_RANK + QK_ROPE_DIM


P0_WIDTH = 1536


def w_in_to_kernel_order(w):
    pad = jnp.zeros((w.shape[0], P0_WIDTH - w.shape[1]), w.dtype)
    return jnp.concatenate([w[:, P0_HEAD:], w[:, :P0_HEAD], pad], axis=1)


LAYER0_MATS = ["mla_w_in", "mla_w_uq", "mla_w_ukv", "mla_w_out"]
LAYER1_MATS = ["s5_w_in", "s5_w_glu", "s5_w_out"]


def _whole_matrices(names, own_blocks, gathered):
    chip = _chip_index(_coords())
    full = {}
    for n, own, o in zip(names, own_blocks, gathered):
        slot = lax.broadcasted_iota(jnp.int32, (N_CHIP, 1, 1), 0)
        o = jnp.where(slot == chip, own[None], o.reshape((N_CHIP,) + own.shape))
        full[n] = o.reshape(-1, o.shape[-1]) if SHARDED[n] == 0 else o.transpose(1, 0, 2).reshape(o.shape[1], -1)
    return full


FIRST_MATS = ["mla_w_in"]
LATER_GROUPS = {"qkv": ["mla_w_uq", "mla_w_ukv"], "out": ["mla_w_out"], "l1": LAYER1_MATS}


def gather_weights(ws):
    mats = [ws[n].astype(BF16) for n in FIRST_MATS]
    full = _whole_matrices(FIRST_MATS, mats, gather_halves(mats, "gather_weights"))
    full["mla_w_in"] = w_in_to_kernel_order(full["mla_w_in"])
    return full


def gather_weights_behind(ws, after):
    token, finish = 0.0, {}
    for group, names in LATER_GROUPS.items():
        mats = [ws[n].astype(BF16) for n in names]
        flight, tok = exchange_start(
            mats, [jax.ShapeDtypeStruct((N_CHIP,) + m.shape, m.dtype) for m in mats],
            [(f, a, lambda me, peer: None, a, lambda s: (_chip_index(s),))
             for a in range(len(mats)) for f in CHIP_FLIPS], f"gather_{group}_start", after=after)
        after = [tok]
        token = token + tok[0, 0]

        def finish_group(after_work, group=group, names=names, flight=flight):
            own, got = exchange_wait(flight, after_work, f"gather_{group}_wait")
            return _whole_matrices(names, own, got)

        finish[group] = finish_group
    return token, finish


def _grad_slots(gw, names):
    slots = []
    for n in names:
        g = gw[n]
        if SHARDED[n] == 0:
            slots.append(g.reshape(N_CHIP, 2, g.shape[0] // (2 * N_CHIP), g.shape[1]))
        else:
            k, n4 = g.shape
            slots.append(g.reshape(k, N_CHIP, n4 // N_CHIP).transpose(1, 0, 2)
                         .reshape(N_CHIP, 2, k // 2, n4 // N_CHIP))
    return slots


def _to_sibling_half(count):
    return [(CORE_FLIP, i, lambda me, peer: (slice(None), 1 - me[2]), i, lambda s: None) for i in range(count)]


def _to_chips(count):
    return [(f, i, lambda me, peer: (_chip_index(peer),), i, lambda s: (_chip_index(s),))
            for i in range(count) for f in CHIP_FLIPS]


def _place():
    me = _coords()
    return jnp.stack([me[2], _chip_index(me)]).astype(jnp.int32)


def reduce_behind(names, tag):
    state = {}
    count = len(names)

    def begin(gw):
        slots = _grad_slots(gw, names)
        lands = [jax.ShapeDtypeStruct((N_CHIP,) + s.shape[2:], F32) for s in slots]
        state["in"], token = exchange_start(slots, lands, _to_sibling_half(count), f"grads_{tag}_swap_in_start")
        return token

    def middle(after):
        slots, got = exchange_wait(state["in"], after, f"grads_{tag}_swap_in_wait")
        place = _place()
        sums = [pair_add(s, g, place[:1], BF16, f"grads_pair_{n}") for n, s, g in zip(names, slots, got)]
        lands = [jax.ShapeDtypeStruct(s.shape, s.dtype) for s in sums]
        state["out"], token = exchange_start(sums, lands, _to_chips(count), f"grads_{tag}_scatter_start")
        return token

    def end(after):
        sums, parts = exchange_wait(state["out"], after, f"grads_{tag}_scatter_wait")
        place = _place()
        return {n: sum_chips(p, s, place, f"grads_sum_{n}") for n, p, s in zip(names, parts, sums)}

    return begin, middle, end


def reduce_gradients(gw, ready_halves, also=None):
    me = _coords()
    place = _place()
    mat_names = [n for n in SHARDED_MATS if n not in ready_halves]
    slots = dict(zip(mat_names, _grad_slots(gw, mat_names)))
    gw = {**gw, **(also or {})}
    small_names = REPLICATED + SHARDED_VECS + list(also or {})
    small, small_offs = pack_flat([_stored(gw[n]).astype(F32) for n in small_names], F32, N_CHIP * 32 * 128)
    slots["small"] = small.reshape(N_CHIP, 2, -1, 128)
    names = list(slots)
    count = len(names)
    got = exchange([slots[n] for n in names],
                   [jax.ShapeDtypeStruct((N_CHIP,) + slots[n].shape[2:], F32) for n in names],
                   _to_sibling_half(count), [], "grads_swap_in")
    sums = [pair_add(slots[n], g, place[:1], F32 if n == "small" else BF16, f"grads_pair_{n}")
            for n, g in zip(names, got)]
    parts = exchange(sums, [jax.ShapeDtypeStruct(s.shape, s.dtype) for s in sums], _to_chips(count), [],
                     "grads_scatter")
    halves = {n: sum_chips(p, s, place, f"grads_sum_{n}") for n, p, s in zip(names, parts, sums)}
    halves.update(ready_halves)
    all_names = list(halves)
    fulls = exchange(
        [halves[n] for n in all_names], [jax.ShapeDtypeStruct(halves[n].shape, F32) for n in all_names],
        [(CORE_FLIP, i, lambda me, peer: (me[2],), i, lambda s: (s[2],)) for i in range(len(all_names))], [],
        "grads_swap_out", aliases={i: i for i in range(len(all_names))})
    out = {n: f.reshape(-1, f.shape[-1]) for n, f in zip(all_names, fulls)}
    quarter = out.pop("small")
    gather, token = exchange_start(
        [quarter], [jax.ShapeDtypeStruct((N_CHIP,) + quarter.shape, F32)],
        [(f, 0, lambda me, peer: None, 0, lambda s: (_chip_index(s),)) for f in CHIP_FLIPS],
        "grads_gather_small_start")

    def finish_small(after):
        (own,), (got_small,) = exchange_wait(gather, after, "grads_gather_small_wait")
        slot = lax.broadcasted_iota(jnp.int32, (N_CHIP, 1, 1), 0)
        small_all = jnp.where(slot == _chip_index(me), own[None], got_small)
        vals = unpack_flat(small_all, small_offs, [_stored_shape(gw[n].shape) for n in small_names])
        res = {}
        for n, v in zip(small_names, vals):
            v = _from_stored(v, gw[n].shape)
            if n in SHARDED_VECS:
                size = v.shape[0] // N_CHIP
                v = lax.dynamic_slice_in_dim(v, _chip_index(me) * size, size)
            res[n] = v
        return res

    return out, finish_small, token


def kernel(x, c, ctx, c_ctx, ada_w, ada_b, norm_g, mla_w_in, mla_q_norm, mla_w_uq, mla_kv_norm, mla_w_ukv, mla_w_out, s5_w_in, s5_a_re, s5_a_im, s5_log_step, s5_b_re, s5_b_im, s5_c_re, s5_c_im, s5_d, s5_w_glu, s5_b_glu, s5_w_out, final_g, loss_target, m_c_ctx, m_ada_w, m_ada_b, m_norm_g, m_mla_w_in, m_mla_q_norm, m_mla_w_uq, m_mla_kv_norm, m_mla_w_ukv, m_mla_w_out, m_s5_w_in, m_s5_a_re, m_s5_a_im, m_s5_log_step, m_s5_b_re, m_s5_b_im, m_s5_c_re, m_s5_c_im, m_s5_d, m_s5_w_glu, m_s5_b_glu, m_s5_w_out, m_final_g, v_c_ctx, v_ada_w, v_ada_b, v_norm_g, v_mla_w_in, v_mla_q_norm, v_mla_w_uq, v_mla_kv_norm, v_mla_w_ukv, v_mla_w_out, v_s5_w_in, v_s5_a_re, v_s5_a_im, v_s5_log_step, v_s5_b_re, v_s5_b_im, v_s5_c_re, v_s5_c_im, v_s5_d, v_s5_w_glu, v_s5_b_glu, v_s5_w_out, v_final_g):
    args = dict(locals())
    weights = {n: args[n] for n in WEIGHT_ORDER}
    D = D_MODEL
    xi, yi, ci = _coords()
    chip = 2 * xi + yi
    me = 4 * xi + 2 * yi + ci
    n_col = ada_w.shape[2]

    c_all = allgather_devices(jnp.pad(c, ((0, 7), (0, 0))), "gather_c")[:, 0, :]
    cond = jnp.concatenate([c_all, jnp.broadcast_to(c_ctx[None], (8, D))], axis=0)
    (s_cond,) = rowwise_fwd(lambda v: (_silu(v),), [cond], [], [D], [F32], 16, 0, "cond_silu")
    ada_rows = ada_w.reshape(2 * D, n_col)
    mod_cols = jnp.stack([mm_nn(s_cond, ada_rows, name=f"mod_proj{i}", b_blk=i) for i in range(2)])
    vec_tiles = [jnp.pad(weights[n][0].reshape(-1, 128), ((0, 6), (0, 0))) for n in SHARDED_VECS]
    mod_all, *vec_all = allgather_chips([mod_cols] + vec_tiles, "gather_mod")
    mod_all = mod_all.transpose(1, 2, 0, 3).reshape(2, 16, 3 * D) + ada_b[:, None, :]
    mine = lax.broadcasted_iota(jnp.int32, (1, 16, 1), 1) == me
    mod_l = jnp.sum(jnp.where(mine, mod_all, 0.0), axis=1)
    mod_c = mod_all[:, 8, :]
    mod = jnp.stack([mod_c.reshape(2, 3, D), mod_l.reshape(2, 3, D)], axis=1)

    w = gather_weights({n: weights[n][0] for n in FIRST_MATS})
    token, late = gather_weights_behind({n: weights[n][0] for n in SHARDED_MATS}, [w["mla_w_in"], mod])
    for n, v in zip(SHARDED_VECS, vec_all):
        w[n] = v[:, :2, :].reshape(-1)
    for n in ["norm_g", "final_g"]:
        w[n] = weights[n]
    for n in ["mla_q_norm", "mla_kv_norm", "s5_a_re", "s5_a_im", "s5_log_step", "s5_b_re", "s5_b_im",
              "s5_c_re", "s5_c_im"]:
        w[n] = weights[n][0]

    reducer = {"l1": reduce_behind(LAYER1_MATS, "l1"), "out": reduce_behind(["mla_w_out"], "out")}
    loss_me, dx, dmod, gw, ready = local_step(x[0], ctx[0], loss_target[0], mod + token, w, late,
                                              reducer)

    dmod_rows, loss_all = _gather([dmod.reshape(2, 2, 3 * D), jnp.broadcast_to(loss_me, (8, 128))],
                                  ALL_FLIPS, _dev_index, N_DEV, "gather_dmod")
    loss = functools.reduce(lambda s, d: s + loss_all[d, 0, 0], range(1, N_DEV), loss_all[0, 0, 0])
    dm = jnp.concatenate([dmod_rows[:, :, 1, :], dmod_rows[:, :, 0, :]], axis=0).transpose(1, 0, 2)
    g_ada_b = jnp.sum(dm, axis=1)
    dm_cols = lax.dynamic_slice_in_dim(dm, chip * n_col, n_col, axis=2)
    g_ada_w = jnp.stack([mm_tn(s_cond, dm_cols[i], name=f"mod_proj_dw{i}") for i in range(2)])
    dmc = jnp.sum(dm_cols[:, 8:, :], axis=1)
    dmc8 = jnp.broadcast_to(dmc[:, None, :], (2, 8, n_col))
    g_sc = (mm_nt(dmc8[0], ada_rows, name="mod_proj_dx0", b_rows=D, b_blk=0)[0]
            + mm_nt(dmc8[1], ada_rows, name="mod_proj_dx1", b_rows=D, b_blk=1)[0])
    g_silu_part = jnp.where(ci == 0, g_sc, 0.0)

    grads = {"ada_w": g_ada_w, "ada_b": g_ada_b}
    deltas, new_m, new_v = {}, {}, {}
    small = [n for n in WEIGHT_ORDER if weights[n].size < 50000]

    def update(n, after=None):
        shp = weights[n].shape
        rows = lambda a: _stored(a.reshape(shp)).reshape(-1, _stored_shape(shp)[-1])
        back = lambda a: _from_stored(a.reshape(_stored_shape(shp)), shp)
        d_, m_, v_ = adamw(rows(weights[n]), rows(grads[n]), rows(args["m_" + n]), rows(args["v_" + n]),
                           name=f"adamw_{n}", after=after)
        deltas[n], new_m[n], new_v[n] = back(d_), back(m_), back(v_)

    red, finish_small, small_started = reduce_gradients(gw, ready, {"silu_c_ctx": g_silu_part})
    update("ada_w", small_started)
    for n in SHARDED_MATS:
        grads[n] = red[n].reshape(weights[n].shape)
        update(n, small_started)
    red_small = finish_small([deltas[n] for n in ["ada_w"] + SHARDED_MATS])
    for n in REPLICATED + SHARDED_VECS:
        grads[n] = red_small[n].reshape(weights[n].shape)
    (g_c_ctx,) = rowwise_bwd(lambda v: (_silu(v),), [jnp.broadcast_to(c_ctx[None], (8, D))], [],
                             [jnp.broadcast_to(red_small["silu_c_ctx"][None], (8, D))], [0], [], 8, 0, "cond_silu_bwd")
    grads["c_ctx"] = g_c_ctx[0]
    for n in WEIGHT_ORDER:
        if n not in small and n not in deltas:
            update(n)
    packs = []
    offs = None
    for src in (weights, grads, {n: args["m_" + n] for n in small}, {n: args["v_" + n] for n in small}):
        buf, offs = pack_flat([src[n] for n in small], F32)
        packs.append(buf)
    outs = adamw(*packs, name="adamw_small")
    for res, dst in zip(outs, (deltas, new_m, new_v)):
        for n, val in zip(small, unpack_flat(res, offs, [weights[n].shape for n in small])):
            dst[n] = val

    return (loss, dx[None], *[grads[n] for n in WEIGHT_ORDER], *[deltas[n] for n in WEIGHT_ORDER],
            *[new_m[n] for n in WEIGHT_ORDER], *[new_v[n] for n in WEIGHT_ORDER])
```

```python
import functools
import math

import jax
import jax.numpy as jnp
import numpy as np
from jax import lax
from jax.experimental import pallas as pl
from jax.experimental.pallas import tpu as pltpu

F32 = jnp.float32
BF16 = jnp.bfloat16

D_MODEL = 1024
GRID_W = 64
EPS = 1e-6
MLA_HEADS = 16
QK_NOPE_DIM = 64
QK_ROPE_DIM = 32
V_HEAD_DIM = 64
Q_LORA_RANK = 256
KV_LORA_RANK = 128
QK_DIM = QK_NOPE_DIM + QK_ROPE_DIM
SOFTMAX_SCALE = QK_DIM ** -0.5
ROPE_THETA = 10000.0
S5_GROUP = 16
S5_GROUPS = D_MODEL // S5_GROUP
S5_STATE = 64
S5_LANES = S5_GROUPS * S5_STATE
N_SEG = 8
GROUPS_PER_BLOCK = 8
N_BLOCKS = S5_GROUPS // GROUPS_PER_BLOCK
BLK_CH = GROUPS_PER_BLOCK * S5_GROUP
BLK_ST = GROUPS_PER_BLOCK * S5_STATE

ADAM_LR = 0.001
ADAM_B1 = 0.9
ADAM_B2 = 0.999
ADAM_EPS = 1e-08
ADAM_WD = 0.01
ADAM_STEP = 10

N_DEV = 8
N_CHIP = 4
MESH = pl.DeviceIdType.MESH
VMEM_LIMIT = 52 * 1024 * 1024
ROW_TILE = 256


def _params(sem=None, vmem=None):
    return pltpu.CompilerParams(dimension_semantics=sem, vmem_limit_bytes=vmem)


def mm_nn(a, b, out_dtype=F32, name="mm_nn", b_blk=0):
    M, K = a.shape
    N = b.shape[1]
    tm = math.gcd(ROW_TILE, M)

    def body(a_ref, b_ref, o_ref):
        o_ref[...] = jnp.dot(a_ref[...].astype(BF16), b_ref[...].astype(BF16),
                             preferred_element_type=F32).astype(o_ref.dtype)

    return pl.pallas_call(
        body, out_shape=jax.ShapeDtypeStruct((M, N), out_dtype), grid=(M // tm,),
        in_specs=[pl.BlockSpec((tm, K), lambda i: (i, 0)), pl.BlockSpec((K, N), lambda i: (b_blk, 0))],
        out_specs=pl.BlockSpec((tm, N), lambda i: (i, 0)),
        compiler_params=_params(("parallel",), VMEM_LIMIT), name=name)(a, b)


def mm_nt(a, b, out_dtype=F32, name="mm_nt", b_rows=None, b_blk=0):
    M, N = a.shape
    K = b.shape[0] if b_rows is None else b_rows
    tm = math.gcd(ROW_TILE, M)

    def body(a_ref, b_ref, o_ref):
        o_ref[...] = lax.dot_general(a_ref[...].astype(BF16), b_ref[...].astype(BF16),
                                     (((1,), (1,)), ((), ())),
                                     preferred_element_type=F32).astype(o_ref.dtype)

    return pl.pallas_call(
        body, out_shape=jax.ShapeDtypeStruct((M, K), out_dtype), grid=(M // tm,),
        in_specs=[pl.BlockSpec((tm, N), lambda i: (i, 0)), pl.BlockSpec((K, N), lambda i: (b_blk, 0))],
        out_specs=pl.BlockSpec((tm, K), lambda i: (i, 0)),
        compiler_params=_params(("parallel",), VMEM_LIMIT), name=name)(a, b)


def mm_tn(a, b, name="mm_tn", b_more=None):
    M, N = b.shape
    K = a.shape[1]
    tn = math.gcd(512, N) if N % 128 == 0 and N > 512 else N
    bs = [b] if b_more is None else [b, b_more]
    assert all(x.shape[1] == N for x in bs)
    nb = N // tn

    def body(a_ref, *refs):
        o_ref = refs[-1]
        for k, b_ref in enumerate(refs[:-1]):
            def product(b_ref=b_ref):
                o_ref[...] = lax.dot_general(a_ref[a.shape[0] - b_ref.shape[0]:, :].astype(BF16),
                                             b_ref[...].astype(BF16), (((0,), (0,)), ((), ())),
                                             preferred_element_type=F32)
            if len(bs) == 1:
                product()
            else:
                pl.when(pl.program_id(0) // nb == k)(product)

    def b_spec(k, rows):
        return pl.BlockSpec((rows, tn), lambda j: (0, jnp.clip(j - k * nb, 0, nb - 1)))

    return pl.pallas_call(
        body, out_shape=jax.ShapeDtypeStruct((K, N * len(bs)), F32), grid=(nb * len(bs),),
        in_specs=[pl.BlockSpec(a.shape, lambda j: (0, 0))] + [b_spec(k, x.shape[0]) for k, x in enumerate(bs)],
        out_specs=pl.BlockSpec((K, tn), lambda j: (0, j)),
        compiler_params=_params(("parallel",), VMEM_LIMIT), name=name)(a, *bs)


class Rows:
    def __init__(self, arr, width=None, row_off=0, col_blk=0):
        self.arr = arr
        self.width = arr.shape[1] if width is None else width
        self.row_off = row_off
        self.col_blk = col_blk

    def spec(self, tm):
        ro, cb = self.row_off // tm, self.col_blk
        return pl.BlockSpec((tm, self.width), lambda i: (i + ro, cb))


def _as_rows(x):
    return x if isinstance(x, Rows) else Rows(x)


def _row_tile(n_rows, n_ctx_rows, rows):
    tm = math.gcd(ROW_TILE, n_rows, n_ctx_rows)
    for r in rows:
        tm = math.gcd(tm, r.row_off)
    return tm


def _bc_spec(arr, n_ctx_blocks):
    g, _, d = arr.shape
    if g == 1:
        return pl.BlockSpec((1, 1, d), lambda i: (0, 0, 0))
    return pl.BlockSpec((1, 1, d), lambda i: ((i >= n_ctx_blocks).astype(jnp.int32), 0, 0))


def rowwise_fwd(fn, rows, bcs, out_dims, out_dtypes, n_rows, n_ctx_rows, name):
    rows = [_as_rows(r) for r in rows]
    tm = _row_tile(n_rows, n_ctx_rows, rows)
    ncb = n_ctx_rows // tm
    nr, nb = len(rows), len(bcs)

    def body(*refs):
        vals = [r[...].astype(F32) for r in refs[:nr]] + [b[0].astype(F32) for b in refs[nr:nr + nb]]
        outs = fn(*vals)
        for o_ref, v in zip(refs[nr + nb:], outs):
            o_ref[...] = v.astype(o_ref.dtype)

    outs = pl.pallas_call(
        body,
        out_shape=[jax.ShapeDtypeStruct((n_rows, d), dt) for d, dt in zip(out_dims, out_dtypes)],
        grid=(n_rows // tm,),
        in_specs=[r.spec(tm) for r in rows] + [_bc_spec(b, ncb) for b in bcs],
        out_specs=[pl.BlockSpec((tm, d), lambda i: (i, 0)) for d in out_dims],
        compiler_params=_params(("parallel",), VMEM_LIMIT), name=name)(*[r.arr for r in rows], *bcs)
    return outs


def rowwise_bwd(fn, rows, bcs, cts, diff_rows, diff_bcs, n_rows, n_ctx_rows, name, ct_extra=None, lat_add=None):
    rows = [_as_rows(r) for r in rows]
    cts = [_as_rows(c) for c in cts]
    extra = [_as_rows(ct_extra)] if ct_extra is not None else []
    tm = _row_tile(n_rows, n_ctx_rows, rows + cts + extra)
    ncb = n_ctx_rows // tm
    nr, nb, nc = len(rows), len(bcs), len(cts)
    ndr, ndb = len(diff_rows), len(diff_bcs)
    n_in = nr + nb + nc + len(extra) + (lat_add is not None)

    def body(*refs):
        i = pl.program_id(0)
        rvals = [r[...].astype(F32) for r in refs[:nr]]
        bvals = [b[0].astype(F32) for b in refs[nr:nr + nb]]
        cvals = [c[...].astype(F32) for c in refs[nr + nb:nr + nb + nc]]
        if extra:
            cvals[0] = cvals[0] + refs[nr + nb + nc][...].astype(F32)
        outs = refs[n_in:]

        def f(*d):
            rv, bv = list(rvals), list(bvals)
            for k, idx in enumerate(diff_rows):
                rv[idx] = d[k]
            for k, idx in enumerate(diff_bcs):
                bv[idx] = d[ndr + k]
            return tuple(fn(*rv, *bv))

        primals = [rvals[k] for k in diff_rows] + [bvals[k] for k in diff_bcs]
        _, vjp = jax.vjp(f, *primals)
        grads = list(vjp(tuple(cvals)))
        if lat_add is not None:
            add = refs[n_in - 1][...]
            grads[0] = grads[0] + (add if lat_add.shape[0] == n_rows else jnp.where(i >= ncb, add, 0.0))
        for k in range(ndr):
            outs[k][...] = grads[k].astype(outs[k].dtype)
        for k, idx in enumerate(diff_bcs):
            o_ref = outs[ndr + k]
            first = (i == 0)
            if bcs[idx].shape[0] == 2:
                first = first | (i == ncb)

            @pl.when(first)
            def _(o_ref=o_ref):
                o_ref[...] = jnp.zeros_like(o_ref)

            o_ref[0] += grads[ndr + k]

    out_shape = [jax.ShapeDtypeStruct((n_rows, rows[k].width), F32) for k in diff_rows]
    out_shape += [jax.ShapeDtypeStruct(bcs[k].shape, F32) for k in diff_bcs]
    out_specs = [pl.BlockSpec((tm, rows[k].width), lambda i: (i, 0)) for k in diff_rows]
    out_specs += [_bc_spec(bcs[k], ncb) for k in diff_bcs]
    ins = [r.arr for r in rows] + list(bcs) + [c.arr for c in cts + extra]
    in_specs = [r.spec(tm) for r in rows] + [_bc_spec(b, ncb) for b in bcs] + [c.spec(tm) for c in cts + extra]
    if lat_add is not None:
        ins.append(lat_add)
        skip = ncb if lat_add.shape[0] != n_rows else 0
        in_specs.append(pl.BlockSpec((tm, lat_add.shape[1]), lambda i: (jnp.maximum(i - skip, 0), 0)))
    outs = pl.pallas_call(
        body, out_shape=out_shape, grid=(n_rows // tm,), in_specs=in_specs, out_specs=out_specs,
        compiler_params=_params(("arbitrary",), VMEM_LIMIT), name=name)(*ins)
    return outs


def _rms(x):
    return x * lax.rsqrt(jnp.mean(x * x, axis=-1, keepdims=True) + EPS)


def _sigmoid(x):
    return 0.5 * (jnp.tanh(0.5 * x) + 1.0)


def _silu(x):
    return x * _sigmoid(x)


def _gelu_tanh(x):
    return 0.5 * x * (1.0 + jnp.tanh(math.sqrt(2.0 / math.pi) * (x + 0.044715 * (x * x * x))))


def f_norm_mod(x, g, sc, sh):
    return ((_rms(x) * g) * (1.0 + sc) + sh,)


def f_rms(x, g):
    return (_rms(x) * g,)


def f_gate(o, z):
    return (o * _silu(z),)


def f_s5_act(y, u, d):
    return (_gelu_tanh(y + d * u),)


def f_s5_glu(ya, gl, z, b):
    return (ya * _sigmoid(gl + b) * _silu(z),)


def _as_parts(x, n_ctx):
    xs = x if isinstance(x, tuple) else (x,)
    assert len(xs) == 1 or xs[0].shape[0] == n_ctx
    return xs, sum(p.shape[0] for p in xs)


def _parts_specs(xs, tm, ncb):
    d = xs[0].shape[1]
    if len(xs) == 1:
        return [pl.BlockSpec((tm, d), lambda i: (i, 0))]
    return [pl.BlockSpec((tm, d), lambda i: (jnp.minimum(i, ncb - 1), 0)),
            pl.BlockSpec((tm, d), lambda i: (jnp.maximum(i - ncb, 0), 0))]


def _parts_tile(x_refs, ncb):
    if len(x_refs) == 1:
        return x_refs[0][...]
    return jnp.where(pl.program_id(0) < ncb, x_refs[0][...], x_refs[1][...])


def norm_proj(x, g, sc, sh, w, n_ctx, name):
    xs, n = _as_parts(x, n_ctx)
    d = xs[0].shape[1]
    nw = w.shape[1]
    tm = math.gcd(ROW_TILE, n, n_ctx)
    ncb = n_ctx // tm
    nx = len(xs)

    def body(*refs):
        g_ref, sc_ref, sh_ref, w_ref, h_ref, p_ref = refs[nx:]
        h = f_norm_mod(_parts_tile(refs[:nx], ncb), g_ref[0], sc_ref[0], sh_ref[0])[0].astype(BF16)
        h_ref[...] = h
        p_ref[...] = jnp.dot(h, w_ref[...], preferred_element_type=F32)

    row = pl.BlockSpec((tm, d), lambda i: (i, 0))
    return pl.pallas_call(
        body, out_shape=[jax.ShapeDtypeStruct((n, d), BF16), jax.ShapeDtypeStruct((n, nw), F32)], grid=(n // tm,),
        in_specs=_parts_specs(xs, tm, ncb) + [_bc_spec(g, ncb), _bc_spec(sc, ncb), _bc_spec(sh, ncb),
                                              pl.BlockSpec(w.shape, lambda i: (0, 0))],
        out_specs=[row, pl.BlockSpec((tm, nw), lambda i: (i, 0))],
        compiler_params=_params(("parallel",), VMEM_LIMIT), name=name)(*xs, g, sc, sh, w)


def norm_proj_bwd(terms, w, x, g, sc, sh, add, n_ctx, name, latent_dx_only=False, dx_from_segments=False):
    xs, n = _as_parts(x, n_ctx)
    adds = add if isinstance(add, tuple) else (add,)
    d = xs[0].shape[1]
    tm = math.gcd(ROW_TILE, n, n_ctx, *[t[2] for t in terms])
    ncb = n_ctx // tm
    nt, nx, na = len(terms), len(xs), len(adds)
    add_skip = ncb if na == 1 and add.shape[0] != n else 0
    n_dx = 2 if dx_from_segments else 1
    tj = tm // N_SEG
    assert not dx_from_segments or (ncb == 1 and not latent_dx_only)

    def body(*refs):
        i = pl.program_id(0)
        a_refs = refs[:nt]
        w_ref, x_refs = refs[nt], refs[nt + 1:nt + 1 + nx]
        g_ref, sc_ref, sh_ref = refs[nt + 1 + nx:nt + 4 + nx]
        add_refs = refs[nt + 4 + nx:nt + 4 + nx + na]
        outs = refs[nt + 4 + nx + na:]
        dx_refs, (dg_ref, dsc_ref, dsh_ref) = outs[:n_dx], outs[n_dx:n_dx + 3]
        d_h = None
        for a_ref, (a, off, first) in zip(a_refs, terms):
            part = lax.dot_general(a_ref[...].astype(BF16), w_ref[:, off:off + a.shape[1]], NT_DIMS,
                                   preferred_element_type=F32)
            if first:
                part = jnp.where(i >= first // tm, part, 0.0)
            d_h = part if d_h is None else d_h + part
        _, vjp = jax.vjp(lambda x_, g_, sc_, sh_: f_norm_mod(x_, g_, sc_, sh_), _parts_tile(x_refs, ncb), g_ref[0],
                         sc_ref[0], sh_ref[0])
        d_x, d_g, d_sc, d_sh = vjp((d_h,))
        extra = _parts_tile(add_refs, ncb)
        d_x = d_x + (extra if add_skip == 0 else jnp.where(i >= ncb, extra, 0.0))
        if dx_from_segments:
            slabs = outs[n_dx + 3]
            for c in range(d // HEAD_LANES):
                slabs[c] = d_x[:, c * HEAD_LANES:(c + 1) * HEAD_LANES]
            for k, here in enumerate([i < ncb, i >= ncb]):
                @pl.when(here)
                def _(k=k):
                    for c in range(d // HEAD_LANES):
                        for seg in range(N_SEG):
                            dx_refs[k][seg, :, c * HEAD_LANES:(c + 1) * HEAD_LANES] = (
                                slabs[c, pl.ds(seg, tj, stride=N_SEG), :])
        else:
            dx_refs[0][...] = d_x

        @pl.when(i == 0)
        def _():
            dg_ref[...] = jnp.zeros_like(dg_ref)

        @pl.when((i == 0) | (i == ncb))
        def _():
            dsc_ref[...] = jnp.zeros_like(dsc_ref)
            dsh_ref[...] = jnp.zeros_like(dsh_ref)

        dg_ref[0] += d_g
        dsc_ref[0] += d_sc
        dsh_ref[0] += d_sh

    def a_spec(a, first):
        skip = first // tm
        return pl.BlockSpec((tm, a.shape[1]), lambda i: (jnp.maximum(i - skip, 0), 0))

    dx_skip = ncb if latent_dx_only else 0
    if dx_from_segments:
        dx_shapes = [jax.ShapeDtypeStruct((N_SEG, n_ctx // N_SEG, d), F32),
                     jax.ShapeDtypeStruct((N_SEG, (n - n_ctx) // N_SEG, d), F32)]
        dx_specs = [pl.BlockSpec((N_SEG, tj, d), lambda i: (0, 0, 0)),
                    pl.BlockSpec((N_SEG, tj, d), lambda i: (0, jnp.maximum(i - ncb, 0), 0))]
    else:
        dx_shapes = [jax.ShapeDtypeStruct((n - dx_skip * tm, d), F32)]
        dx_specs = [pl.BlockSpec((tm, d), lambda i: (jnp.maximum(i - dx_skip, 0), 0))]
    add_specs = (_parts_specs(adds, tm, ncb) if na == 2 else
                 [pl.BlockSpec((tm, d), lambda i: (jnp.maximum(i - add_skip, 0), 0))])
    res = pl.pallas_call(
        body,
        out_shape=dx_shapes + [jax.ShapeDtypeStruct(g.shape, F32), jax.ShapeDtypeStruct(sc.shape, F32),
                               jax.ShapeDtypeStruct(sh.shape, F32)],
        grid=(n // tm,),
        in_specs=[a_spec(a, first) for a, _, first in terms]
        + [pl.BlockSpec(w.shape, lambda i: (0, 0))] + _parts_specs(xs, tm, ncb)
        + [_bc_spec(g, ncb), _bc_spec(sc, ncb), _bc_spec(sh, ncb)] + add_specs,
        out_specs=dx_specs + [_bc_spec(g, ncb), _bc_spec(sc, ncb), _bc_spec(sh, ncb)],
        scratch_shapes=[pltpu.VMEM((d // HEAD_LANES, tm, HEAD_LANES), F32)] if dx_from_segments else [],
        compiler_params=_params(("arbitrary",), VMEM_LIMIT), name=name)(
            *[t[0] for t in terms], w, *xs, g, sc, sh, *adds)
    if dx_from_segments:
        return ((res[0].reshape(n_ctx, d), res[1].reshape(n - n_ctx, d)), *res[2:])
    return res


def mla_post_fwd(o, p0, x0, gate, w_out, n_ctx, name="l0_post"):
    n, d = o.shape
    xs, _ = _as_parts(x0, n_ctx)
    tm = math.gcd(ROW_TILE, n, n_ctx)
    ncb = n_ctx // tm
    nx = len(xs)

    def body(o_ref, z_ref, *refs):
        gt_ref, w_ref, x1_ref, og_ref, out_ref = refs[nx:]
        og = f_gate(o_ref[...], z_ref[...])[0].astype(BF16)
        out = jnp.dot(og, w_ref[...], preferred_element_type=F32)
        og_ref[...] = og
        out_ref[...] = out
        x1_ref[...] = _parts_tile(refs[:nx], ncb) + gt_ref[0] * out

    row = pl.BlockSpec((tm, d), lambda i: (i, 0))
    return pl.pallas_call(
        body, out_shape=[jax.ShapeDtypeStruct((n, d), F32), jax.ShapeDtypeStruct((n, d), BF16),
                         jax.ShapeDtypeStruct((n, d), F32)],
        grid=(n // tm,),
        in_specs=[row, row] + _parts_specs(xs, tm, ncb) + [_bc_spec(gate, ncb), pl.BlockSpec((d, d), lambda i: (0, 0))],
        out_specs=[row, row, row],
        compiler_params=_params(("parallel",), VMEM_LIMIT), name=name)(o, p0, *xs, gate, w_out)


def mla_post_bwd(dx1, out, og, o, p0, gate, w_out, n_ctx, name="l0_post_bwd"):
    n, d = o.shape
    dxs, _ = _as_parts(dx1, n_ctx)
    tm = math.gcd(ROW_TILE, n, n_ctx)
    ncb = n_ctx // tm
    nx = len(dxs)

    def body(*refs):
        out_ref, og_ref, o_ref, z_ref, gt_ref, w_ref, do_ref, dz_ref, dgt_ref, dw_ref = refs[nx:]
        i = pl.program_id(0)

        @pl.when(i == 0)
        def _():
            dw_ref[...] = jnp.zeros_like(dw_ref)

        @pl.when((i == 0) | (i == ncb))
        def _():
            dgt_ref[...] = jnp.zeros_like(dgt_ref)

        dx = _parts_tile(refs[:nx], ncb)
        dgt_ref[0] += jnp.sum(dx * out_ref[...], axis=0, keepdims=True)
        d_out16 = (gt_ref[0] * dx).astype(BF16)
        dw_ref[...] += lax.dot_general(og_ref[...], d_out16, (((0,), (0,)), ((), ())), preferred_element_type=F32)
        d_og = lax.dot_general(d_out16, w_ref[...], NT_DIMS, preferred_element_type=F32)
        _, gate_vjp = jax.vjp(lambda o_, z_: f_gate(o_, z_), o_ref[...], z_ref[...])
        d_o, d_z = gate_vjp((d_og,))
        do_ref[...] = d_o
        dz_ref[...] = d_z

    row = pl.BlockSpec((tm, d), lambda i: (i, 0))
    mat = pl.BlockSpec((d, d), lambda i: (0, 0))
    return pl.pallas_call(
        body, out_shape=[jax.ShapeDtypeStruct((n, d), F32), jax.ShapeDtypeStruct((n, d), F32),
                         jax.ShapeDtypeStruct(gate.shape, F32), jax.ShapeDtypeStruct((d, d), F32)],
        grid=(n // tm,),
        in_specs=_parts_specs(dxs, tm, ncb) + [row, row, row, row, _bc_spec(gate, ncb), mat],
        out_specs=[row, row, _bc_spec(gate, ncb), mat],
        compiler_params=_params(("arbitrary",), VMEM_LIMIT), name=name)(*dxs, out, og, o, p0, gate, w_out)


def s5_tail(y_ssm, p1, x1p, target, n_ctx, d_vec, b_glu, gate, final_g, w_glu, w_out, name="l1_tail"):
    n, d = y_ssm.shape
    tm = math.gcd(ROW_TILE, n, n_ctx)
    off = n_ctx // tm
    tn_dims = (((0,), (0,)), ((), ()))

    def row_loss(x, g, t):
        e = _rms(x) * g - t
        return 0.5 * (e * e) * (1.0 / d)

    def body(y_ref, u_ref, z_ref, x1_ref, t_ref, d_ref, b_ref, gt_ref, fg_ref, wg_ref, wo_ref,
             l_ref, dx_ref, dy_ref, du_ref, dz_ref, dfg_ref, dgt_ref, db_ref, dd_ref, dwg_ref, dwo_ref):
        @pl.when(pl.program_id(0) == 0)
        def _():
            for r in (l_ref, dfg_ref, dgt_ref, db_ref, dd_ref, dwg_ref, dwo_ref):
                r[...] = jnp.zeros_like(r)

        u, z, tgt, gt = u_ref[...], z_ref[...], t_ref[...], gt_ref[...]
        (ya,), act_vjp = jax.vjp(lambda y_, u_, d_: f_s5_act(y_, u_, d_), y_ref[...], u, d_ref[...])
        ya16 = ya.astype(BF16)
        gl = jnp.dot(ya16, wg_ref[...], preferred_element_type=F32)
        (y3,), glu_vjp = jax.vjp(lambda a_, g_, z_, b_: f_s5_glu(a_, g_, z_, b_), ya, gl, z, b_ref[...])
        y3_16 = y3.astype(BF16)
        out1 = jnp.dot(y3_16, wo_ref[...], preferred_element_type=F32)
        lterm, loss_vjp = jax.vjp(lambda x_, g_: row_loss(x_, g_, tgt), x1_ref[...] + gt * out1, fg_ref[...])
        dx2, dfg = loss_vjp(jnp.ones_like(lterm))
        l_ref[...] += jnp.sum(lterm, axis=0, keepdims=True)
        dfg_ref[...] += dfg
        dx_ref[...] = dx2
        dgt_ref[...] += jnp.sum(dx2 * out1, axis=0, keepdims=True)
        d_out16 = (gt * dx2).astype(BF16)
        dwo_ref[...] += lax.dot_general(y3_16, d_out16, tn_dims, preferred_element_type=F32)
        d_y3 = lax.dot_general(d_out16, wo_ref[...], NT_DIMS, preferred_element_type=F32)
        d_ya, d_gl, d_z, d_b = glu_vjp((d_y3,))
        dz_ref[...] = d_z
        db_ref[...] += d_b
        d_gl16 = d_gl.astype(BF16)
        dwg_ref[...] += lax.dot_general(ya16, d_gl16, tn_dims, preferred_element_type=F32)
        d_ya = d_ya + lax.dot_general(d_gl16, wg_ref[...], NT_DIMS, preferred_element_type=F32)
        d_y, d_u, d_d = act_vjp((d_ya,))
        dy_ref[...] = d_y
        du_ref[...] = d_u
        dd_ref[...] += d_d

    row = pl.BlockSpec((tm, d), lambda i: (i, 0))
    vecs = pl.BlockSpec((1, d), lambda i: (0, 0))
    mat = pl.BlockSpec((d, d), lambda i: (0, 0))
    return pl.pallas_call(
        body,
        out_shape=[jax.ShapeDtypeStruct((1, d), F32)] + [jax.ShapeDtypeStruct((n, d), F32)] * 4
        + [jax.ShapeDtypeStruct((1, d), F32)] * 4 + [jax.ShapeDtypeStruct((d, d), F32)] * 2,
        grid=(n // tm,),
        in_specs=[row, pl.BlockSpec((tm, d), lambda i: (i + off, 0)), pl.BlockSpec((tm, d), lambda i: (i + off, 1)),
                  pl.BlockSpec((tm, d), lambda i: (i + off, 0)), row, vecs, vecs, vecs, vecs, mat, mat],
        out_specs=[vecs, row, row, row, row, vecs, vecs, vecs, vecs, mat, mat],
        compiler_params=_params(("arbitrary",), VMEM_LIMIT), name=name)(
            y_ssm, p1, p1, x1p, target, d_vec, b_glu, gate, final_g, w_glu, w_out)


NT_DIMS = (((1,), (1,)), ((), ()))
HEAD_LANES = 128
N_PAIRS = MLA_HEADS // 2


def _own_lanes(shape, hh):
    lane = lax.broadcasted_iota(jnp.int32, shape, len(shape) - 1)
    return (lane < V_HEAD_DIM) if hh == 0 else (lane >= V_HEAD_DIM)


def _delta_lane(hh):
    return V_HEAD_DIM if hh == 0 else 0


def _rope_tiles(x, cos, sin_next, sin_prev, inverse):
    width = x.shape[-1]
    reps = width // HEAD_LANES
    c, sn, sp = (jnp.tile(t, (1, reps)) for t in (cos, sin_next, sin_prev))
    if inverse:
        return x * c + pltpu.roll(x * sn, 8, 1) + pltpu.roll(x * sp, width - 8, 1)
    return x * c + pltpu.roll(x, width - 8, 1) * sn + pltpu.roll(x, 8, 1) * sp


STAT_LANE = QK_DIM


def _with_stat(x16, col, lane0):
    hi = col.astype(BF16)
    r1 = col - hi.astype(F32)
    mid = r1.astype(BF16)
    lo = (r1 - mid.astype(F32)).astype(BF16)
    lane = lax.broadcasted_iota(jnp.int32, x16.shape, 1)
    return jnp.where(lane == lane0, hi, jnp.where(lane == lane0 + 1, mid, jnp.where(lane == lane0 + 2, lo, x16)))


def attn_fwd(qb, kb, vb, n_ctx):
    T = qb.shape[0]
    tq = math.gcd(ROW_TILE, n_ctx)
    nq, ncb = T // tq, n_ctx // tq

    def body(q_ref, k_ref, v_ref, o_ref, lse_ref):
        qi = pl.program_id(1)

        def rows(n_keys):
            v = v_ref[:n_keys, :]
            outs = []
            for hh in range(2):
                hs = slice(hh * HEAD_LANES, (hh + 1) * HEAD_LANES)
                s = lax.dot_general(q_ref[:, hs], k_ref[:n_keys, hs], NT_DIMS,
                                    preferred_element_type=F32) * SOFTMAX_SCALE
                m = jnp.max(s, axis=-1, keepdims=True)
                p = jnp.exp(s - m)
                l = jnp.sum(p, axis=-1, keepdims=True)
                outs.append(jnp.dot(p.astype(BF16), v, preferred_element_type=F32) / l)
                lse_ref[hh] = m + jnp.log(l)
            o_ref[...] = jnp.where(_own_lanes(outs[0].shape, 0), outs[0], outs[1])

        pl.when(qi < ncb)(lambda: rows(n_ctx))
        pl.when(qi >= ncb)(lambda: rows(T))

    return pl.pallas_call(
        body,
        out_shape=[jax.ShapeDtypeStruct((T, MLA_HEADS * V_HEAD_DIM), F32),
                   jax.ShapeDtypeStruct((MLA_HEADS, T, 1), F32)],
        grid=(N_PAIRS, nq),
        in_specs=[pl.BlockSpec((tq, 2 * HEAD_LANES), lambda h, i: (i, h)),
                  pl.BlockSpec((T, 2 * HEAD_LANES), lambda h, i: (0, h)),
                  pl.BlockSpec((T, 2 * V_HEAD_DIM), lambda h, i: (0, h))],
        out_specs=[pl.BlockSpec((tq, 2 * V_HEAD_DIM), lambda h, i: (i, h)),
                   pl.BlockSpec((2, tq, 1), lambda h, i: (h, i, 0))],
        compiler_params=_params(("parallel", "parallel"), VMEM_LIMIT), name="attn_fwd")(qb, kb, vb)


def attn_bwd_dq(qb, kb, vb, o, do, lse, tabs, n_ctx):
    T = qb.shape[0]
    tq = math.gcd(ROW_TILE, n_ctx)
    nq, ncb = T // tq, n_ctx // tq

    def body(q_ref, k_ref, v_ref, o_ref, do_ref, lse_ref, c_ref, sn_ref, sp_ref, dq_ref, dos_ref):
        qi = pl.program_id(1)

        def rows(n_keys):
            v = v_ref[:n_keys, :]
            dqs = []
            for hh in range(2):
                hs = slice(hh * HEAD_LANES, (hh + 1) * HEAD_LANES)
                k = k_ref[:n_keys, hs]
                do = jnp.where(_own_lanes(do_ref.shape, hh), do_ref[...], 0.0)
                delta = jnp.sum(do * o_ref[...], axis=-1, keepdims=True)
                s = lax.dot_general(q_ref[:, hs], k, NT_DIMS, preferred_element_type=F32) * SOFTMAX_SCALE
                p = jnp.exp(s - lse_ref[hh])
                do16 = do.astype(BF16)
                dp = lax.dot_general(do16, v, NT_DIMS, preferred_element_type=F32)
                ds = p * (dp - delta) * SOFTMAX_SCALE
                dqs.append(jnp.dot(ds.astype(BF16), k, preferred_element_type=F32))
                dos_ref[:, hs] = _with_stat(do16, delta, _delta_lane(hh))
            dq = jnp.concatenate(dqs, axis=1)
            dq_ref[...] = _rope_tiles(dq, c_ref[...], sn_ref[...], sp_ref[...], True).astype(BF16)

        pl.when(qi < ncb)(lambda: rows(n_ctx))
        pl.when(qi >= ncb)(lambda: rows(T))

    tab = pl.BlockSpec((tq, HEAD_LANES), lambda h, i: (i, 0))
    return pl.pallas_call(
        body,
        out_shape=[jax.ShapeDtypeStruct((T, MLA_HEADS * HEAD_LANES), BF16)] * 2,
        grid=(N_PAIRS, nq),
        in_specs=[pl.BlockSpec((tq, 2 * HEAD_LANES), lambda h, i: (i, h)),
                  pl.BlockSpec((T, 2 * HEAD_LANES), lambda h, i: (0, h)),
                  pl.BlockSpec((T, 2 * V_HEAD_DIM), lambda h, i: (0, h)),
                  pl.BlockSpec((tq, 2 * V_HEAD_DIM), lambda h, i: (i, h)),
                  pl.BlockSpec((tq, 2 * V_HEAD_DIM), lambda h, i: (i, h)),
                  pl.BlockSpec((2, tq, 1), lambda h, i: (h, i, 0)), tab, tab, tab],
        out_specs=[pl.BlockSpec((tq, 2 * HEAD_LANES), lambda h, i: (i, h))] * 2,
        compiler_params=_params(("parallel", "parallel"), VMEM_LIMIT), name="attn_bwd_dq")(
            qb, kb, vb, o, do, lse, *tabs)


def attn_bwd(qb, kb, vb, o, do, lse, tabs, n_ctx):
    T = qb.shape[0]
    tq = math.gcd(ROW_TILE, n_ctx)
    nq, ncb = T // tq, n_ctx // tq
    tn = (((0,), (0,)), ((), ()))

    def body(q_ref, k_ref, v_ref, o_ref, do_ref, lse_ref, c_ref, sn_ref, sp_ref, dq_ref, dk_ref, dv_ref, dkt, dvt):
        qi = pl.program_id(1)

        @pl.when(qi == 0)
        def _():
            dkt[...] = jnp.zeros_like(dkt)
            dvt[...] = jnp.zeros_like(dvt)

        def rows(n_keys):
            v = v_ref[:n_keys, :]
            dqs = []
            for hh in range(2):
                hs = slice(hh * HEAD_LANES, (hh + 1) * HEAD_LANES)
                k = k_ref[:n_keys, hs]
                q = q_ref[:, hs]
                do = jnp.where(_own_lanes(do_ref.shape, hh), do_ref[...], 0.0)
                delta = jnp.sum(do * o_ref[...], axis=-1, keepdims=True)
                s = lax.dot_general(q, k, NT_DIMS, preferred_element_type=F32) * SOFTMAX_SCALE
                p = jnp.exp(s - lse_ref[hh])
                do16 = do.astype(BF16)
                dp = lax.dot_general(do16, v, NT_DIMS, preferred_element_type=F32)
                ds16 = (p * (dp - delta) * SOFTMAX_SCALE).astype(BF16)
                dqs.append(jnp.dot(ds16, k, preferred_element_type=F32))
                dkt[hs, :n_keys] += lax.dot_general(q, ds16, tn, preferred_element_type=F32)
                dvt[:, :n_keys] += lax.dot_general(do16, p.astype(BF16), tn, preferred_element_type=F32)
            dq = jnp.concatenate(dqs, axis=1)
            dq_ref[...] = _rope_tiles(dq, c_ref[...], sn_ref[...], sp_ref[...], True).astype(BF16)

        pl.when(qi < ncb)(lambda: rows(n_ctx))
        pl.when(qi >= ncb)(lambda: rows(T))

        @pl.when(qi == nq - 1)
        def _():
            dk_ref[...] = dkt[...].T
            dv_ref[...] = dvt[...].T

    tab = pl.BlockSpec((tq, HEAD_LANES), lambda h, i: (i, 0))
    return pl.pallas_call(
        body,
        out_shape=[jax.ShapeDtypeStruct((T, MLA_HEADS * HEAD_LANES), BF16),
                   jax.ShapeDtypeStruct((T, MLA_HEADS * HEAD_LANES), F32),
                   jax.ShapeDtypeStruct((T, MLA_HEADS * V_HEAD_DIM), F32)],
        grid=(N_PAIRS, nq),
        in_specs=[pl.BlockSpec((tq, 2 * HEAD_LANES), lambda h, i: (i, h)),
                  pl.BlockSpec((T, 2 * HEAD_LANES), lambda h, i: (0, h)),
                  pl.BlockSpec((T, 2 * V_HEAD_DIM), lambda h, i: (0, h)),
                  pl.BlockSpec((tq, 2 * V_HEAD_DIM), lambda h, i: (i, h)),
                  pl.BlockSpec((tq, 2 * V_HEAD_DIM), lambda h, i: (i, h)),
                  pl.BlockSpec((2, tq, 1), lambda h, i: (h, i, 0)), tab, tab, tab],
        out_specs=[pl.BlockSpec((tq, 2 * HEAD_LANES), lambda h, i: (i, h)),
                   pl.BlockSpec((T, 2 * HEAD_LANES), lambda h, i: (0, h)),
                   pl.BlockSpec((T, 2 * V_HEAD_DIM), lambda h, i: (0, h))],
        scratch_shapes=[pltpu.VMEM((2 * HEAD_LANES, T), F32), pltpu.VMEM((2 * V_HEAD_DIM, T), F32)],
        compiler_params=_params(("parallel", "arbitrary"), VMEM_LIMIT), name="attn_bwd")(
            qb, kb, vb, o, do, lse, *tabs)


def attn_bwd_dkv(qs, kb, vb, dos, n_ctx):
    T = qs.shape[0]
    tq = math.gcd(ROW_TILE, n_ctx)
    nq, ncb = T // tq, n_ctx // tq

    def body(q_ref, do_ref, k_ref, v_ref, dk_ref, dv_ref):
        kj = pl.program_id(1)

        def cols(first):
            v = v_ref[...]
            lane = lax.broadcasted_iota(jnp.int32, v.shape, 1)
            dvs = []
            for hh in range(2):
                hs = slice(hh * HEAD_LANES, (hh + 1) * HEAD_LANES)
                q = q_ref[first:, hs]
                do16 = do_ref[first:, hs]
                in_delta = (lane >= _delta_lane(hh)) & (lane < _delta_lane(hh) + 3)
                v_minus = jnp.where(in_delta, -jnp.ones_like(v), v)
                pt = jnp.exp(lax.dot_general(k_ref[:, hs], q, NT_DIMS, preferred_element_type=F32) * SOFTMAX_SCALE)
                dvs.append(jnp.dot(pt.astype(BF16), do16, preferred_element_type=F32))
                dst = pt * lax.dot_general(v_minus, do16, NT_DIMS, preferred_element_type=F32) * SOFTMAX_SCALE
                dk_ref[:, hs] = jnp.dot(dst.astype(BF16), q, preferred_element_type=F32)
            dv_ref[...] = jnp.where(_own_lanes(dvs[0].shape, 0), dvs[0], dvs[1])

        pl.when(kj < ncb)(lambda: cols(0))
        pl.when(kj >= ncb)(lambda: cols(n_ctx))

    return pl.pallas_call(
        body,
        out_shape=[jax.ShapeDtypeStruct((T, MLA_HEADS * HEAD_LANES), F32),
                   jax.ShapeDtypeStruct((T, MLA_HEADS * V_HEAD_DIM), F32)],
        grid=(N_PAIRS, nq),
        in_specs=[pl.BlockSpec((T, 2 * HEAD_LANES), lambda h, j: (0, h)),
                  pl.BlockSpec((T, 2 * HEAD_LANES), lambda h, j: (0, h)),
                  pl.BlockSpec((tq, 2 * HEAD_LANES), lambda h, j: (j, h)),
                  pl.BlockSpec((tq, 2 * V_HEAD_DIM), lambda h, j: (j, h))],
        out_specs=[pl.BlockSpec((tq, 2 * HEAD_LANES), lambda h, j: (j, h)),
                   pl.BlockSpec((tq, 2 * V_HEAD_DIM), lambda h, j: (j, h))],
        compiler_params=_params(("parallel", "parallel"), VMEM_LIMIT), name="attn_bwd_dkv")(
            qs, dos, kb, vb)


def _split_bf16(x):
    hi = x.astype(BF16)
    return hi, (x - hi.astype(F32)).astype(BF16)


def q_heads(cq, gain, w_uq_p, tabs, name="l0_uq"):
    T, K = cq.arr.shape[0], cq.width
    N = w_uq_p.shape[1]
    tm = math.gcd(ROW_TILE, T)

    def body(a_ref, g_ref, w_ref, c_ref, sn_ref, sp_ref, o_ref, n_ref):
        qn = f_rms(a_ref[...], g_ref[0])[0].astype(BF16)
        n_ref[...] = qn
        acc = jnp.dot(qn, w_ref[...], preferred_element_type=F32)
        o_ref[...] = _rope_tiles(acc, c_ref[...], sn_ref[...], sp_ref[...], False).astype(BF16)

    tab = pl.BlockSpec((tm, HEAD_LANES), lambda i: (i, 0))
    return pl.pallas_call(
        body, out_shape=[jax.ShapeDtypeStruct((T, N), BF16), jax.ShapeDtypeStruct((T, K), BF16)], grid=(T // tm,),
        in_specs=[cq.spec(tm), pl.BlockSpec((1, 1, K), lambda i: (0, 0, 0)), pl.BlockSpec((K, N), lambda i: (0, 0)),
                  tab, tab, tab],
        out_specs=[pl.BlockSpec((tm, N), lambda i: (i, 0)), pl.BlockSpec((tm, K), lambda i: (i, 0))],
        compiler_params=_params(("parallel",), VMEM_LIMIT), name=name)(cq.arr, gain, w_uq_p, *tabs)


def kv_heads(ckv, gain, w_kn_p, w_v, kr, spread, tabs, name="l0_ukv"):
    T, K = ckv.arr.shape[0], ckv.width
    N = w_kn_p.shape[1]
    NV = w_v.shape[1]
    tm = math.gcd(ROW_TILE, T)

    def body(a_ref, g_ref, wk_ref, wv_ref, kr_ref, e_ref, c_ref, sn_ref, sp_ref, k_ref, v_ref, n_ref):
        a = f_rms(a_ref[...], g_ref[0])[0].astype(BF16)
        n_ref[...] = a
        hi, lo = _split_bf16(kr_ref[...])
        acc = (jnp.dot(a, wk_ref[...], preferred_element_type=F32)
               + jnp.dot(hi, e_ref[...], preferred_element_type=F32)
               + jnp.dot(lo, e_ref[...], preferred_element_type=F32))
        roped = _rope_tiles(acc, c_ref[...], sn_ref[...], sp_ref[...], False)
        lane = lax.broadcasted_iota(jnp.int32, roped.shape, 1) % HEAD_LANES
        k_ref[...] = jnp.where((lane >= STAT_LANE) & (lane < STAT_LANE + 3), 1.0, roped).astype(BF16)
        v_ref[...] = jnp.dot(a, wv_ref[...], preferred_element_type=F32).astype(BF16)

    tab = pl.BlockSpec((tm, HEAD_LANES), lambda i: (i, 0))
    return pl.pallas_call(
        body, out_shape=[jax.ShapeDtypeStruct((T, N), BF16), jax.ShapeDtypeStruct((T, NV), BF16),
                         jax.ShapeDtypeStruct((T, K), BF16)], grid=(T // tm,),
        in_specs=[ckv.spec(tm), pl.BlockSpec((1, 1, K), lambda i: (0, 0, 0)), pl.BlockSpec((K, N), lambda i: (0, 0)),
                  pl.BlockSpec((K, NV), lambda i: (0, 0)), kr.spec(tm),
                  pl.BlockSpec((kr.width, N), lambda i: (0, 0)), tab, tab, tab],
        out_specs=[pl.BlockSpec((tm, N), lambda i: (i, 0)), pl.BlockSpec((tm, NV), lambda i: (i, 0)),
                   pl.BlockSpec((tm, K), lambda i: (i, 0))],
        compiler_params=_params(("parallel",), VMEM_LIMIT), name=name)(
            ckv.arr, gain, w_kn_p, w_v, kr.arr, spread, *tabs)


def heads_unrope(d, tabs, spread=None, name="unrope"):
    T, N = d.shape
    tm = math.gcd(ROW_TILE, T)

    def body(*refs):
        if spread is None:
            d_ref, c_ref, sn_ref, sp_ref, o_ref = refs
        else:
            d_ref, c_ref, sn_ref, sp_ref, e_ref, o_ref, kr_ref = refs
        g = _rope_tiles(d_ref[...], c_ref[...], sn_ref[...], sp_ref[...], True)
        o_ref[...] = g.astype(BF16)
        if spread is not None:
            hi, lo = _split_bf16(g)
            kr_ref[...] = (lax.dot_general(hi, e_ref[...], NT_DIMS, preferred_element_type=F32)
                           + lax.dot_general(lo, e_ref[...], NT_DIMS, preferred_element_type=F32))

    tab = pl.BlockSpec((tm, HEAD_LANES), lambda i: (i, 0))
    row = pl.BlockSpec((tm, N), lambda i: (i, 0))
    ins, in_specs = [d, *tabs], [row, tab, tab, tab]
    out_shape, out_specs = [jax.ShapeDtypeStruct((T, N), BF16)], [row]
    if spread is not None:
        ins.append(spread)
        in_specs.append(pl.BlockSpec(spread.shape, lambda i: (0, 0)))
        out_shape.append(jax.ShapeDtypeStruct((T, spread.shape[0]), F32))
        out_specs.append(pl.BlockSpec((tm, spread.shape[0]), lambda i: (i, 0)))
    return pl.pallas_call(
        body, out_shape=out_shape, grid=(T // tm,), in_specs=in_specs, out_specs=out_specs,
        compiler_params=_params(("parallel",), VMEM_LIMIT), name=name)(*ins)


def _cmul(ar, ai, br, bi):
    return ar * br - ai * bi, ar * bi + ai * br


def s5_chain(finals, s0, a, n_steps, reverse, name):
    W = finals.shape[-1]
    first = N_SEG - 1 if reverse else 0

    def body(f_ref, s0_ref, a_ref, c_ref):
        pr, pi = jnp.ones((1, W), F32), jnp.zeros((1, W), F32)
        br, bi = a_ref[0], a_ref[1]
        n = n_steps
        while n:
            if n & 1:
                pr, pi = _cmul(pr, pi, br, bi)
            br, bi = _cmul(br, bi, br, bi)
            n >>= 1
        fr, fi = f_ref[0], f_ref[1]
        row = lax.broadcasted_iota(jnp.int32, (N_SEG, W), 0)
        s0r = jnp.broadcast_to(s0_ref[0], (N_SEG, W))
        s0i = jnp.broadcast_to(s0_ref[1], (N_SEG, W))
        cr = jnp.where(row == first, s0r, 0.0)
        ci = jnp.where(row == first, s0i, 0.0)
        shift = N_SEG - 1 if reverse else 1
        for _ in range(N_SEG - 1):
            mr, mi = _cmul(pr, pi, cr, ci)
            tr = pltpu.roll(fr + mr, shift, 0)
            ti = pltpu.roll(fi + mi, shift, 0)
            cr = jnp.where(row == first, s0r, tr)
            ci = jnp.where(row == first, s0i, ti)
        c_ref[0] = cr
        c_ref[1] = ci

    return pl.pallas_call(body, out_shape=jax.ShapeDtypeStruct((2, N_SEG, W), F32), name=name)(finals, s0, a)


def _scan_chunk(bur, bui, st_ref, a_ref, n_steps, reverse):
    for lc in range(S5_LANES // BLK_ST):
        sl = slice(lc * BLK_ST, (lc + 1) * BLK_ST)
        lr = jnp.broadcast_to(a_ref[0, :, sl], (N_SEG, BLK_ST))
        li = jnp.broadcast_to(a_ref[1, :, sl], (N_SEG, BLK_ST))

        def step(jj, carry, sl=sl, lr=lr, li=li):
            sr, si = carry
            j = (n_steps - 1 - jj) if reverse else jj
            r0 = pl.multiple_of(j * N_SEG, N_SEG)
            nr = lr * sr - li * si + bur[pl.ds(r0, N_SEG), sl]
            ni = lr * si + li * sr + bui[pl.ds(r0, N_SEG), sl]
            bur[pl.ds(r0, N_SEG), sl] = nr
            bui[pl.ds(r0, N_SEG), sl] = ni
            return nr, ni

        sr, si = lax.fori_loop(0, n_steps, step, (st_ref[0, :, sl], st_ref[1, :, sl]))
        st_ref[0, :, sl] = sr
        st_ref[1, :, sl] = si


def _project_in(x16, w_re, w_im, bur, bui, adjoint):
    for gb in range(N_BLOCKS):
        xb = x16[:, gb * BLK_CH:(gb + 1) * BLK_CH]
        sl = slice(gb * BLK_ST, (gb + 1) * BLK_ST)
        if adjoint:
            dn = (((1,), (1,)), ((), ()))
            bur[:, sl] = lax.dot_general(xb, w_re[gb], dn, preferred_element_type=F32)
            bui[:, sl] = -lax.dot_general(xb, w_im[gb], dn, preferred_element_type=F32)
        else:
            bur[:, sl] = jnp.dot(xb, w_re[gb], preferred_element_type=F32)
            bui[:, sl] = jnp.dot(xb, w_im[gb], preferred_element_type=F32)


def s5_scan(act, w_re, w_im, a, init, *, reverse, adjoint=False, c_re=None, c_im=None, add=None,
            want_ckpt=False, rows=None, name):
    act_off, N = rows if rows is not None else (0, act.shape[0])
    R = math.gcd(ROW_TILE, N, act_off)
    nch, jc = N // R, R // N_SEG
    with_out = c_re is not None

    def chunk(i):
        return (nch - 1 - i) if reverse else i

    def body(*refs):
        act_ref, wre_ref, wim_ref, a_ref, init_ref = refs[:5]
        k = 5
        if with_out:
            cre_ref, cim_ref = refs[k:k + 2]
            k += 2
        if add is not None:
            add_ref = refs[k]
            k += 1
        if with_out:
            out_ref = refs[k]
            k += 1
        if want_ckpt:
            ck_ref = refs[k]
            k += 1
        fin_ref, bur, bui = refs[k:k + 3]

        @pl.when(pl.program_id(0) == 0)
        def _():
            fin_ref[...] = init_ref[...]

        if want_ckpt:
            ck_ref[0] = fin_ref[...]
        _project_in(act_ref[...].astype(BF16), wre_ref, wim_ref, bur, bui, adjoint)
        _scan_chunk(bur, bui, fin_ref, a_ref, jc, reverse)
        if with_out:
            for gb in range(N_BLOCKS):
                sl = slice(gb * BLK_ST, (gb + 1) * BLK_ST)
                y = (jnp.dot(bur[:, sl].astype(BF16), cre_ref[gb], preferred_element_type=F32)
                     - jnp.dot(bui[:, sl].astype(BF16), cim_ref[gb], preferred_element_type=F32))
                cs = slice(gb * BLK_CH, (gb + 1) * BLK_CH)
                if add is not None:
                    y = y + add_ref[:, cs]
                out_ref[:, cs] = y

    row_spec = pl.BlockSpec((R, D_MODEL), lambda i: (chunk(i), 0))
    act_spec = pl.BlockSpec((R, D_MODEL), lambda i: (chunk(i) + act_off // R, 0))
    w_spec = pl.BlockSpec(w_re.shape, lambda i: (0, 0, 0))
    st_spec = pl.BlockSpec((2, N_SEG, S5_LANES), lambda i: (0, 0, 0))
    ins = [act, w_re, w_im, a, init]
    in_specs = [act_spec, w_spec, w_spec, pl.BlockSpec((2, 1, S5_LANES), lambda i: (0, 0, 0)), st_spec]
    if with_out:
        ins += [c_re, c_im]
        in_specs += [pl.BlockSpec(c_re.shape, lambda i: (0, 0, 0))] * 2
    if add is not None:
        ins.append(add)
        in_specs.append(row_spec)
    out_shape, out_specs = [], []
    if with_out:
        out_shape.append(jax.ShapeDtypeStruct((N, D_MODEL), F32))
        out_specs.append(row_spec)
    if want_ckpt:
        out_shape.append(jax.ShapeDtypeStruct((nch, 2, N_SEG, S5_LANES), F32))
        out_specs.append(pl.BlockSpec((1, 2, N_SEG, S5_LANES), lambda i: (chunk(i), 0, 0, 0)))
    out_shape.append(jax.ShapeDtypeStruct((2, N_SEG, S5_LANES), F32))
    out_specs.append(st_spec)
    res = pl.pallas_call(
        body, out_shape=out_shape, grid=(nch,), in_specs=in_specs, out_specs=out_specs,
        scratch_shapes=[pltpu.VMEM((R, S5_LANES), F32), pltpu.VMEM((R, S5_LANES), F32)],
        compiler_params=_params(("arbitrary",), VMEM_LIMIT), name=name)(*ins)
    res = list(res)
    out = res.pop(0) if with_out else None
    ckpt = res.pop(0) if want_ckpt else None
    return out, ckpt, res[0]


def s5_grads(dy, u, ckpt, b_re, b_im, c_re, c_im, lam, init_adj, *, reverse, add=None, u_off=0, du_rows=None,
             du_into=None, n_rows=None, name):
    N = n_rows if dy is None else dy.shape[0]
    du_total, du_first = du_rows if du_rows is not None else (N, 0)
    R = math.gcd(ROW_TILE, N, u_off, du_first)
    nch, jc = N // R, R // N_SEG
    W = S5_LANES

    def chunk(i):
        return i if reverse else (nch - 1 - i)

    def body(*refs):
        if dy is None:
            refs = (None,) + refs[:4] + (None, None) + refs[4:]
        dy_ref, u_ref, ck_ref, bre_ref, bim_ref, cre_ref, cim_ref, lam_ref, init_ref = refs[:9]
        k = 9
        if add is not None:
            add_ref = refs[k]
            k += 1
        k += du_into is not None
        outs = refs[k:]
        if dy is None:
            outs = outs[:4] + (None, None) + outs[4:]
        du_ref, dlam_ref, dbre_ref, dbim_ref, dcre_ref, dcim_ref, fin_ref = outs[:7]
        sr_buf, si_buf, er_buf, ei_buf, st_buf = outs[7:12]

        @pl.when(pl.program_id(0) == 0)
        def _():
            fin_ref[...] = init_ref[...]
            dlam_ref[...] = jnp.zeros_like(dlam_ref)
            dbre_ref[...] = jnp.zeros_like(dbre_ref)
            dbim_ref[...] = jnp.zeros_like(dbim_ref)
            if dy is not None:
                dcre_ref[...] = jnp.zeros_like(dcre_ref)
                dcim_ref[...] = jnp.zeros_like(dcim_ref)

        u16 = u_ref[...].astype(BF16)
        st_buf[...] = ck_ref[0]
        _project_in(u16, bre_ref, bim_ref, sr_buf, si_buf, False)
        _scan_chunk(sr_buf, si_buf, st_buf, lam_ref, jc, reverse)
        if dy is None:
            er_buf[...] = jnp.zeros_like(er_buf)
            ei_buf[...] = jnp.zeros_like(ei_buf)
        else:
            dy16 = dy_ref[...].astype(BF16)
            _project_in(dy16, cre_ref, cim_ref, er_buf, ei_buf, True)
        for lc in range(W // BLK_ST):
            sl = slice(lc * BLK_ST, (lc + 1) * BLK_ST)
            lr = jnp.broadcast_to(lam_ref[0, :, sl], (N_SEG, BLK_ST))
            li = jnp.broadcast_to(lam_ref[1, :, sl], (N_SEG, BLK_ST))

            def one(r0, spr, spi, carry, sl=sl, lr=lr, li=li):
                gr, gi, ar, ai = carry
                nr = er_buf[pl.ds(r0, N_SEG), sl] + lr * gr + li * gi
                ni = ei_buf[pl.ds(r0, N_SEG), sl] + lr * gi - li * gr
                er_buf[pl.ds(r0, N_SEG), sl] = nr
                ei_buf[pl.ds(r0, N_SEG), sl] = ni
                return nr, ni, ar + spr * nr + spi * ni, ai + spr * ni - spi * nr

            def step(ff, carry, sl=sl, one=one):
                f = jc - 1 - ff
                j = (jc - 1 - f) if reverse else f
                jp = (j + 1) if reverse else (j - 1)
                r0 = pl.multiple_of(j * N_SEG, N_SEG)
                p0 = pl.multiple_of(jp * N_SEG, N_SEG)
                return one(r0, sr_buf[pl.ds(p0, N_SEG), sl], si_buf[pl.ds(p0, N_SEG), sl], carry)

            carry = (fin_ref[0, :, sl], fin_ref[1, :, sl], dlam_ref[0, :, sl], dlam_ref[1, :, sl])
            carry = lax.fori_loop(0, jc - 1, step, carry)
            r_first = (jc - 1) * N_SEG if reverse else 0
            gr, gi, ar, ai = one(r_first, ck_ref[0, 0, :, sl], ck_ref[0, 1, :, sl], carry)
            fin_ref[0, :, sl] = gr
            fin_ref[1, :, sl] = gi
            dlam_ref[0, :, sl] = ar
            dlam_ref[1, :, sl] = ai
        tn = (((0,), (0,)), ((), ()))
        nt = (((1,), (1,)), ((), ()))
        for gb in range(N_BLOCKS):
            sl = slice(gb * BLK_ST, (gb + 1) * BLK_ST)
            cs = slice(gb * BLK_CH, (gb + 1) * BLK_CH)
            gr16 = er_buf[:, sl].astype(BF16)
            gi16 = ei_buf[:, sl].astype(BF16)
            du = (lax.dot_general(gr16, bre_ref[gb], nt, preferred_element_type=F32)
                  + lax.dot_general(gi16, bim_ref[gb], nt, preferred_element_type=F32))
            if add is not None:
                du = du + add_ref[:, cs]
            du_ref[:, cs] = du
            ub = u16[:, cs]
            dbre_ref[gb] += lax.dot_general(ub, gr16, tn, preferred_element_type=F32)
            dbim_ref[gb] += lax.dot_general(ub, gi16, tn, preferred_element_type=F32)
            if dy is not None:
                dyb = dy16[:, cs]
                dcre_ref[gb] += lax.dot_general(sr_buf[:, sl].astype(BF16), dyb, tn, preferred_element_type=F32)
                dcim_ref[gb] -= lax.dot_general(si_buf[:, sl].astype(BF16), dyb, tn, preferred_element_type=F32)

    row_spec = pl.BlockSpec((R, D_MODEL), lambda i: (chunk(i), 0))
    st_spec = pl.BlockSpec((2, N_SEG, W), lambda i: (0, 0, 0))
    wb_spec = pl.BlockSpec(b_re.shape, lambda i: (0, 0, 0))
    wc_spec = pl.BlockSpec(c_re.shape, lambda i: (0, 0, 0))
    ins = [dy, u, ckpt, b_re, b_im, c_re, c_im, lam, init_adj]
    u_spec = pl.BlockSpec((R, D_MODEL), lambda i: (chunk(i) + u_off // R, 0))
    in_specs = [row_spec, u_spec, pl.BlockSpec((1, 2, N_SEG, W), lambda i: (chunk(i), 0, 0, 0)),
                wb_spec, wb_spec, wc_spec, wc_spec, pl.BlockSpec((2, 1, W), lambda i: (0, 0, 0)), st_spec]
    if dy is None:
        ins, in_specs = ins[1:5] + ins[7:], in_specs[1:5] + in_specs[7:]
    if add is not None:
        ins.append(add)
        in_specs.append(row_spec)
    aliases = {}
    if du_into is not None:
        aliases[len(ins)] = 0
        ins.append(du_into)
        in_specs.append(pl.BlockSpec(memory_space=pl.ANY))
    du_spec = pl.BlockSpec((R, D_MODEL), lambda i: (chunk(i) + du_first // R, 0))
    out_shape = [jax.ShapeDtypeStruct((du_total, D_MODEL), F32), jax.ShapeDtypeStruct((2, N_SEG, W), F32),
                 jax.ShapeDtypeStruct(b_re.shape, F32), jax.ShapeDtypeStruct(b_re.shape, F32),
                 jax.ShapeDtypeStruct(c_re.shape, F32), jax.ShapeDtypeStruct(c_re.shape, F32),
                 jax.ShapeDtypeStruct((2, N_SEG, W), F32)]
    out_specs = [du_spec, st_spec, wb_spec, wb_spec, wc_spec, wc_spec, st_spec]
    if dy is None:
        out_shape, out_specs = out_shape[:4] + out_shape[6:], out_specs[:4] + out_specs[6:]
    res = pl.pallas_call(
        body, out_shape=out_shape, grid=(nch,), in_specs=in_specs, out_specs=out_specs, input_output_aliases=aliases,
        scratch_shapes=[pltpu.VMEM((R, W), F32) for _ in range(4)] + [pltpu.VMEM((2, N_SEG, W), F32)],
        compiler_params=_params(("arbitrary",), VMEM_LIMIT), name=name)(*ins)
    return res if dy is not None else [*res[:4], None, None, res[4]]


def adamw(w, g, m, v, name="adamw", after=None):
    n, d = w.shape
    lanes = -(-d // 128) * 128
    tm = n
    while tm * lanes * 4 > (1 << 20) and tm % 16 == 0:
        tm //= 2
    c1 = 1.0 - ADAM_B1 ** ADAM_STEP
    c2 = 1.0 - ADAM_B2 ** ADAM_STEP

    def body(w_ref, g_ref, m_ref, v_ref, *rest):
        d_ref, nm_ref, nv_ref = rest[-3:]
        g_ = g_ref[...]
        m_ = ADAM_B1 * m_ref[...] + (1.0 - ADAM_B1) * g_
        v_ = ADAM_B2 * v_ref[...] + (1.0 - ADAM_B2) * (g_ * g_)
        d_ref[...] = -ADAM_LR * ((m_ / c1) / (jnp.sqrt(v_ / c2) + ADAM_EPS) + ADAM_WD * w_ref[...])
        nm_ref[...] = m_
        nv_ref[...] = v_

    spec = pl.BlockSpec((tm, d), lambda i: (i, 0))
    extra = [] if after is None else [after]
    return pl.pallas_call(
        body, out_shape=[jax.ShapeDtypeStruct((n, d), F32)] * 3, grid=(n // tm,),
        in_specs=[spec] * 4 + [pl.BlockSpec(memory_space=pl.ANY)] * len(extra), out_specs=[spec] * 3,
        compiler_params=_params(("parallel",), VMEM_LIMIT), name=name)(w, g, m, v, *extra)


def _coords():
    return lax.axis_index("x"), lax.axis_index("y"), lax.axis_index("c")


def exchange(arrays, out_shapes, remote, local, name, aliases=None):
    n_in, n_out, n_rem, n_loc = len(arrays), len(out_shapes), len(remote), len(local)

    def at(ref, idx):
        return ref if idx is None else ref.at[idx]

    def body(*refs):
        ins, outs = refs[:n_in], refs[n_in:n_in + n_out]
        send_sems, recv_sems, local_sems = refs[n_in + n_out:]
        me = _coords()
        sends, recvs = [], []
        for k, (flip, ii, src_at, oi, dst_at) in enumerate(remote):
            peer = (me[0] ^ flip[0], me[1] ^ flip[1], me[2] ^ flip[2])
            src = at(ins[ii], src_at(me, peer))
            sends.append(pltpu.make_async_remote_copy(
                src_ref=src, dst_ref=at(outs[oi], dst_at(me)), send_sem=send_sems.at[k], recv_sem=recv_sems.at[k],
                device_id=peer, device_id_type=MESH))
            recvs.append(pltpu.make_async_remote_copy(
                src_ref=src, dst_ref=at(outs[oi], dst_at(peer)), send_sem=send_sems.at[k], recv_sem=recv_sems.at[k],
                device_id=peer, device_id_type=MESH))
        locs = [pltpu.make_async_copy(at(ins[ii], src_at(me)), at(outs[oi], dst_at(me)), local_sems.at[k])
                for k, (ii, src_at, oi, dst_at) in enumerate(local)]
        for cp in locs + sends:
            cp.start()
        for cp in recvs:
            cp.wait_recv()
        for cp in sends:
            cp.wait_send()
        for cp in locs:
            cp.wait()

    hbm = pl.BlockSpec(memory_space=pl.ANY)
    return pl.pallas_call(
        body, out_shape=list(out_shapes), in_specs=[hbm] * n_in, out_specs=[hbm] * n_out,
        scratch_shapes=[pltpu.SemaphoreType.DMA((n_rem,)), pltpu.SemaphoreType.DMA((n_rem,)),
                        pltpu.SemaphoreType.DMA((max(n_loc, 1),))],
        input_output_aliases=aliases or {}, name=name)(*arrays)


ALL_FLIPS = [(dx, dy, dc) for dx in (0, 1) for dy in (0, 1) for dc in (0, 1)][1:]
CHIP_FLIPS = [(1, 0, 0), (0, 1, 0), (1, 1, 0)]
CORE_FLIP = (0, 0, 1)


def _dev_index(p):
    return 4 * p[0] + 2 * p[1] + p[2]


def _chip_index(p):
    return 2 * p[0] + p[1]


def _gather(xs, flips, index, n, name):
    arrays = [x[None] for x in xs]
    outs = [jax.ShapeDtypeStruct((n,) + x.shape, x.dtype) for x in xs]
    remote = [(f, a, lambda me, peer: (0,), a, lambda s: (index(s),)) for a in range(len(xs)) for f in flips]
    local = [(a, lambda me: (0,), a, lambda me: (index(me),)) for a in range(len(xs))]
    return exchange(arrays, outs, remote, local, name)


def allgather_devices(x, name):
    return _gather([x], ALL_FLIPS, _dev_index, N_DEV, name)[0]


def allgather_chips(xs, name):
    return _gather(xs, CHIP_FLIPS, _chip_index, N_CHIP, name)


def gather_halves(xs, name):
    n = len(xs)
    nk = n * len(CHIP_FLIPS)

    def body(*refs):
        ins, outs = refs[:n], refs[n:2 * n]
        ici_send, ici_recv, d2d_send, d2d_recv = refs[2 * n:]
        me = _coords()
        sibling = (me[0], me[1], 1 - me[2])
        first, passed, landed = [], [], []
        for a in range(n):
            half = ins[a].shape[0] // 2
            mine = ins[a].at[pl.ds(pl.multiple_of(me[2] * half, 16), half)]
            for j, flip in enumerate(CHIP_FLIPS):
                k = a * len(CHIP_FLIPS) + j
                peer = (me[0] ^ flip[0], me[1] ^ flip[1], me[2])
                first.append(pltpu.make_async_remote_copy(
                    src_ref=mine, dst_ref=outs[a].at[_chip_index(me), me[2]], send_sem=ici_send.at[k],
                    recv_sem=ici_recv.at[k], device_id=peer, device_id_type=MESH))
                arrived = outs[a].at[_chip_index(peer), me[2]]
                landed.append(pltpu.make_async_remote_copy(
                    src_ref=mine, dst_ref=arrived, send_sem=ici_send.at[k], recv_sem=ici_recv.at[k],
                    device_id=peer, device_id_type=MESH))
                passed.append(pltpu.make_async_remote_copy(
                    src_ref=arrived, dst_ref=arrived, send_sem=d2d_send.at[k], recv_sem=d2d_recv.at[k],
                    device_id=sibling, device_id_type=MESH))
        for cp in first:
            cp.start()
        for k in range(nk):
            landed[k].wait_recv()
            passed[k].start()
        for a in range(n):
            for j, flip in enumerate(CHIP_FLIPS):
                k = a * len(CHIP_FLIPS) + j
                peer_chip = _chip_index((me[0] ^ flip[0], me[1] ^ flip[1]))
                from_sibling = outs[a].at[peer_chip, 1 - me[2]]
                pltpu.make_async_remote_copy(
                    src_ref=from_sibling, dst_ref=from_sibling, send_sem=d2d_send.at[k], recv_sem=d2d_recv.at[k],
                    device_id=sibling, device_id_type=MESH).wait_recv()
        for cp in first + passed:
            cp.wait_send()

    hbm = pl.BlockSpec(memory_space=pl.ANY)
    return pl.pallas_call(
        body, out_shape=[jax.ShapeDtypeStruct((N_CHIP, 2, x.shape[0] // 2, x.shape[1]), x.dtype) for x in xs],
        in_specs=[hbm] * n, out_specs=[hbm] * n,
        scratch_shapes=[pltpu.SemaphoreType.DMA((nk,)) for _ in range(4)], name=name)(*xs)


HBM_SPEC = pl.BlockSpec(memory_space=pltpu.HBM)
SEM_SPEC = pl.BlockSpec(memory_space=pltpu.SEMAPHORE)
DATAFLOW = pltpu.SideEffectType.DATAFLOW_SIDE_EFFECTING


def _at(ref, idx):
    return ref if idx is None else ref.at[idx]


def _peer(me, flip):
    return (me[0] ^ flip[0], me[1] ^ flip[1], me[2] ^ flip[2])


def exchange_start(arrays, land_shapes, remote, name, after=None):
    n_in, n_out, nk = len(arrays), len(land_shapes), len(remote)
    after = list(after or [])
    n_after = len(after)

    def body(*refs):
        srcs, lands = refs[:n_in], refs[n_in:n_in + n_out]
        first_out = n_in + n_out + n_after
        send_sems, recv_sems, token = refs[first_out], refs[first_out + 1], refs[-1]
        me = _coords()
        for k, (flip, ii, src_at, oi, dst_at) in enumerate(remote):
            peer = _peer(me, flip)
            pltpu.make_async_remote_copy(
                src_ref=_at(srcs[ii], src_at(me, peer)), dst_ref=_at(lands[oi], dst_at(me)), send_sem=send_sems.at[k],
                recv_sem=recv_sems.at[k], device_id=peer, device_id_type=MESH).start()
        token[...] = jnp.zeros_like(token)

    lands = [lax.empty(s.shape, s.dtype) for s in land_shapes]
    bufs = list(arrays) + lands
    out = pl.pallas_call(
        body, name=name,
        out_shape=(pltpu.SemaphoreType.DMA((nk,)), pltpu.SemaphoreType.DMA((nk,)),
                   *[pltpu.HBM(b.shape, b.dtype) for b in bufs], jax.ShapeDtypeStruct((8, 128), F32)),
        in_specs=[HBM_SPEC] * len(bufs) + [pl.BlockSpec(memory_space=pl.ANY)] * n_after,
        out_specs=(SEM_SPEC, SEM_SPEC, *[HBM_SPEC] * len(bufs), pl.BlockSpec(memory_space=pltpu.VMEM)),
        input_output_aliases={a: 2 + a for a in range(len(bufs))},
        compiler_params=pltpu.CompilerParams(has_side_effects=DATAFLOW),
    )(*[pltpu.with_memory_space_constraint(b, pltpu.HBM) for b in bufs], *after)
    flight = (out[0], out[1], list(out[2:2 + n_in]), list(out[2 + n_in:2 + n_in + n_out]), remote)
    return flight, out[-1]


def exchange_wait(flight, after, name):
    send_sems, recv_sems, arrays, lands, remote = flight
    n_in, n_out = len(arrays), len(lands)
    after = list(after) if isinstance(after, (list, tuple)) else [after]

    def body(*refs):
        srcs, lnds = refs[:n_in], refs[n_in:n_in + n_out]
        s_sems, r_sems = refs[n_in + n_out], refs[n_in + n_out + 1]
        me = _coords()
        for k, (flip, ii, src_at, oi, dst_at) in enumerate(remote):
            peer = _peer(me, flip)
            copy = pltpu.make_async_remote_copy(
                src_ref=_at(srcs[ii], src_at(me, peer)), dst_ref=_at(lnds[oi], dst_at(peer)), send_sem=s_sems.at[k],
                recv_sem=r_sems.at[k], device_id=peer, device_id_type=MESH)
            copy.wait_send()
            copy.wait_recv()

    bufs = list(arrays) + list(lands)
    out = pl.pallas_call(
        body, name=name,
        out_shape=tuple(pltpu.HBM(b.shape, b.dtype) for b in bufs),
        in_specs=[HBM_SPEC] * len(bufs) + [SEM_SPEC, SEM_SPEC] + [pl.BlockSpec(memory_space=pl.ANY)] * len(after),
        out_specs=tuple([HBM_SPEC] * len(bufs)),
        input_output_aliases={a: a for a in range(len(bufs))},
        compiler_params=pltpu.CompilerParams(has_side_effects=DATAFLOW),
    )(*bufs, send_sems, recv_sems, *after)
    return list(out[:n_in]), list(out[n_in:])


def _half_tile(h, cd):
    return h if h * cd * 4 <= (1 << 20) else math.gcd(512, h)


def pair_add(g, got, core, out_dtype, name):
    _, _, h, cd = g.shape
    th = _half_tile(h, cd)

    def body(c_ref, g_ref, got_ref, o_ref):
        o_ref[0] = (g_ref[0, 0] + got_ref[0]).astype(o_ref.dtype)

    return pl.pallas_call(
        body, out_shape=jax.ShapeDtypeStruct((N_CHIP, h, cd), out_dtype),
        grid_spec=pltpu.PrefetchScalarGridSpec(
            num_scalar_prefetch=1, grid=(N_CHIP, h // th),
            in_specs=[pl.BlockSpec((1, 1, th, cd), lambda q, i, c: (q, c[0], i, 0)),
                      pl.BlockSpec((1, th, cd), lambda q, i, c: (q, i, 0))],
            out_specs=pl.BlockSpec((1, th, cd), lambda q, i, c: (q, i, 0))),
        compiler_params=_params(("parallel", "parallel"), VMEM_LIMIT), name=name)(core, g, got)


def sum_chips(parts, sums, place, name):
    _, h, cd = parts.shape
    th = _half_tile(h, cd)

    def body(pc_ref, p_ref, own_ref, o_ref):
        acc = None
        for q in range(N_CHIP):
            term = jnp.where(pc_ref[1] == q, own_ref[0], p_ref[q]).astype(F32)
            acc = term if acc is None else acc + term
        o_ref[0] = acc

    return pl.pallas_call(
        body, out_shape=jax.ShapeDtypeStruct((2, h, cd), F32),
        grid_spec=pltpu.PrefetchScalarGridSpec(
            num_scalar_prefetch=1, grid=(h // th,),
            in_specs=[pl.BlockSpec((N_CHIP, th, cd), lambda i, pc: (0, i, 0)),
                      pl.BlockSpec((1, th, cd), lambda i, pc: (pc[1], i, 0))],
            out_specs=pl.BlockSpec((1, th, cd), lambda i, pc: (pc[0], i, 0))),
        compiler_params=_params(("parallel",), VMEM_LIMIT), name=name)(place, parts, sums)


def to_segments(a, n_ctx):
    def one(p):
        n = p.shape[0]
        return p.reshape(N_SEG, n // N_SEG, -1).transpose(1, 0, 2).reshape(n, -1)
    return jnp.concatenate([one(a[:n_ctx]), one(a[n_ctx:])], axis=0) if n_ctx else one(a)


def rows_to_segments(a, n_ctx, name):
    n, d = a.shape
    tj = n_ctx // N_SEG
    per_seg = (n - n_ctx) // N_SEG // tj
    assert n_ctx % N_SEG == 0 and (n - n_ctx) % (N_SEG * tj) == 0

    def body(*refs):
        out_ref, slabs = refs[N_SEG:]
        for c in range(d // HEAD_LANES):
            cols = slice(c * HEAD_LANES, (c + 1) * HEAD_LANES)
            for seg in range(N_SEG):
                slabs[c, pl.ds(seg, tj, stride=N_SEG), :] = refs[seg][:, cols]
            out_ref[:, cols] = slabs[c]

    def seg_spec(seg):
        return pl.BlockSpec((tj, d), lambda i: (jnp.where(i == 0, seg, N_SEG + seg * per_seg + i - 1), 0))

    return pl.pallas_call(
        body, out_shape=jax.ShapeDtypeStruct((n, d), a.dtype), grid=(1 + per_seg,),
        in_specs=[seg_spec(seg) for seg in range(N_SEG)], out_specs=pl.BlockSpec((N_SEG * tj, d), lambda i: (i, 0)),
        scratch_shapes=[pltpu.VMEM((d // HEAD_LANES, N_SEG * tj, HEAD_LANES), a.dtype)],
        compiler_params=_params(("parallel",), VMEM_LIMIT), name=name)(*[a] * N_SEG)


def rope_tables(n_ctx, n_lat):
    f32 = np.float32
    rows = n_lat // GRID_W
    row = np.repeat(np.arange(rows), GRID_W).astype(f32)
    col = np.tile(np.arange(GRID_W), rows).astype(f32)
    d = QK_ROPE_DIM // 2
    inv = (f32(1.0) / np.power(f32(ROPE_THETA), np.arange(0, d, 2, dtype=f32) / f32(d))).astype(f32)
    ang = np.concatenate([row[:, None] * inv[None, :], col[:, None] * inv[None, :]], axis=1).astype(f32)
    cos = np.concatenate([np.ones((n_ctx, d), f32), np.cos(ang)], axis=0)
    sin = np.concatenate([np.zeros((n_ctx, d), f32), np.sin(ang)], axis=0)
    q = QK_ROPE_DIM // 4
    T = n_ctx + n_lat
    ones, zeros = np.ones((T, QK_NOPE_DIM), f32), np.zeros((T, QK_NOPE_DIM), f32)
    tail, z8 = np.zeros((T, HEAD_LANES - QK_DIM), f32), np.zeros((T, q), f32)
    cr, cc, sr, sc = cos[:, :q], cos[:, q:], sin[:, :q], sin[:, q:]
    cos_t = np.concatenate([ones, cr, cr, cc, cc, tail], axis=1)
    sin_next = np.concatenate([zeros, -sr, z8, -sc, z8, tail], axis=1)
    sin_prev = np.concatenate([zeros, z8, sr, z8, sc, tail], axis=1)
    return tuple(jnp.asarray(t, F32) for t in (cos_t, sin_next, sin_prev))


def pad_heads(w, used):
    k = w.shape[0]
    return jnp.pad(w.reshape(k, MLA_HEADS, used), ((0, 0), (0, 0), (0, HEAD_LANES - used))).reshape(k, -1)


def unpad_heads(w, used):
    k = w.shape[0]
    return w.reshape(k, MLA_HEADS, HEAD_LANES)[:, :, :used].reshape(k, MLA_HEADS * used)


def rotary_spread():
    lane = np.arange(MLA_HEADS * HEAD_LANES) % HEAD_LANES
    return jnp.asarray(lane[None, :] == (QK_NOPE_DIM + np.arange(QK_ROPE_DIM))[:, None], BF16)


def s5_discretise(a_re, a_im, log_step, b_re, b_im):
    dt = jnp.exp(log_step)[:, None]
    mag = jnp.exp(a_re * dt)
    lb_re = mag * jnp.cos(a_im * dt)
    lb_im = mag * jnp.sin(a_im * dt)
    den = a_re * a_re + a_im * a_im
    nr = lb_re - 1.0
    f_re = ((nr * a_re + lb_im * a_im) / den)[:, None, :]
    f_im = ((lb_im * a_re - nr * a_im) / den)[:, None, :]
    return lb_re, lb_im, f_re * b_re - f_im * b_im, f_re * b_im + f_im * b_re


def s5_block_weights(lb_re, lb_im, bb_re, bb_im, c_re, c_im):
    eye = jnp.eye(GROUPS_PER_BLOCK, dtype=F32)
    lam = jnp.stack([lb_re.reshape(1, S5_LANES), lb_im.reshape(1, S5_LANES)])

    def b_blocks(bb):
        t = bb.reshape(N_BLOCKS, GROUPS_PER_BLOCK, S5_GROUP, S5_STATE)
        return jnp.einsum("bgcp,gh->bgchp", t, eye).reshape(N_BLOCKS, BLK_CH, BLK_ST).astype(BF16)

    def c_blocks(cc):
        t = cc.reshape(N_BLOCKS, GROUPS_PER_BLOCK, S5_GROUP, S5_STATE)
        return jnp.einsum("bgcp,gh->bgphc", t, eye).reshape(N_BLOCKS, BLK_ST, BLK_CH).astype(BF16)

    return lam, b_blocks(bb_re), b_blocks(bb_im), c_blocks(c_re), c_blocks(c_im)


def b_block_diag(db):
    t = db.reshape(N_BLOCKS, GROUPS_PER_BLOCK, S5_GROUP, GROUPS_PER_BLOCK, S5_STATE)
    return jnp.einsum("bgchp,gh->bgcp", t, jnp.eye(GROUPS_PER_BLOCK, dtype=F32)).reshape(S5_GROUPS, S5_GROUP, S5_STATE)


def c_block_diag(dc):
    t = dc.reshape(N_BLOCKS, GROUPS_PER_BLOCK, S5_STATE, GROUPS_PER_BLOCK, S5_GROUP)
    return jnp.einsum("bgphc,gh->bgcp", t, jnp.eye(GROUPS_PER_BLOCK, dtype=F32)).reshape(S5_GROUPS, S5_GROUP, S5_STATE)


def conj(a):
    return jnp.stack([a[0], -a[1]])


def _narrow(shape):
    return len(shape) >= 2 and shape[-1] < min(HEAD_LANES, shape[-2])


def _stored(a):
    return jnp.swapaxes(a, -1, -2) if _narrow(a.shape) else a


def _stored_shape(shape):
    return tuple(shape[:-2]) + (shape[-1], shape[-2]) if _narrow(shape) else tuple(shape)


def _from_stored(a, shape):
    return jnp.swapaxes(a, -1, -2) if _narrow(shape) else a


PACK_TILE = 16 * 128


def pack_flat(parts, dtype, multiple=PACK_TILE):
    flat = [p.reshape(-1).astype(dtype) for p in parts]
    sizes = [f.shape[0] for f in flat]
    total = sum(sizes)
    pad = (-total) % multiple
    if pad:
        flat.append(jnp.zeros((pad,), dtype))
    offs = np.cumsum([0] + sizes)[:-1].tolist()
    return jnp.concatenate(flat).reshape(-1, 128), offs


def unpack_flat(buf, offs, shapes):
    flat = buf.reshape(-1)
    return [flat[o:o + int(np.prod(s))].reshape(s) for o, s in zip(offs, shapes)]


def s5_forward(p1, n_ctx, dirs):
    saved = []
    y = None
    ctx_rows, lat_rows = (0, n_ctx), (n_ctx, p1.shape[0] - n_ctx)
    zeros_tile = jnp.zeros((2, N_SEG, S5_LANES), F32)
    zeros_row = jnp.zeros((2, 1, S5_LANES), F32)
    for k, (lam, b_re, b_im, c_re, c_im) in enumerate(dirs):
        rev = k == 1
        last = 0 if rev else N_SEG - 1
        _, _, fin = s5_scan(p1, b_re, b_im, lam, zeros_tile, reverse=rev, rows=ctx_rows, name=f"s5_ctx_finals{k}")
        carry_c = s5_chain(fin, zeros_row, lam, n_ctx // N_SEG, rev, name=f"s5_ctx_chain{k}")
        _, ck_c, fin_c = s5_scan(p1, b_re, b_im, lam, carry_c, reverse=rev, want_ckpt=True, rows=ctx_rows,
                                 name=f"s5_ctx_scan{k}")
        s0 = fin_c[:, last:last + 1, :]
        _, _, fin = s5_scan(p1, b_re, b_im, lam, zeros_tile, reverse=rev, rows=lat_rows, name=f"s5_lat_finals{k}")
        carry_l = s5_chain(fin, s0, lam, lat_rows[1] // N_SEG, rev, name=f"s5_lat_chain{k}")
        y, ck_l, _ = s5_scan(p1, b_re, b_im, lam, carry_l, reverse=rev, c_re=c_re, c_im=c_im, add=y,
                             want_ckpt=True, rows=lat_rows, name=f"s5_lat_scan{k}")
        saved.append((ck_c, ck_l))
    return y, saved


def s5_backward(dy_l, du_extra_l, p1, n_ctx, dirs, saved):
    n_lat = p1.shape[0] - n_ctx
    zeros_tile = jnp.zeros((2, N_SEG, S5_LANES), F32)
    zeros_row = jnp.zeros((2, 1, S5_LANES), F32)
    du_l, du_c = du_extra_l, None
    grads = []
    for k, (lam, b_re, b_im, c_re, c_im) in enumerate(dirs):
        rev = k == 1
        lam_c = conj(lam)
        ck_c, ck_l = saved[k]
        first = N_SEG - 1 if rev else 0
        _, _, fin = s5_scan(dy_l, c_re, c_im, lam_c, zeros_tile, reverse=not rev, adjoint=True,
                            name=f"s5_lat_adj_finals{k}")
        carry = s5_chain(fin, zeros_row, lam_c, n_lat // N_SEG, not rev, name=f"s5_lat_adj_chain{k}")
        whole = k == len(dirs) - 1
        du_l, dlam_l, dbr_l, dbi_l, dcr_l, dci_l, fin_a = s5_grads(
            dy_l, p1, ck_l, b_re, b_im, c_re, c_im, lam, carry, reverse=rev, add=du_l, u_off=n_ctx,
            du_rows=(p1.shape[0], n_ctx) if whole else None, name=f"s5_lat_grads{k}")
        g0 = fin_a[:, first:first + 1, :]
        carry = s5_chain(zeros_tile, g0, lam_c, n_ctx // N_SEG, not rev, name=f"s5_ctx_adj_chain{k}")
        du_c, dlam_c, dbr_c, dbi_c, _, _, _ = s5_grads(
            None, p1, ck_c, b_re, b_im, c_re, c_im, lam, carry, reverse=rev, add=du_c, n_rows=n_ctx,
            du_rows=(p1.shape[0], 0) if whole else None, du_into=du_l if whole else None, name=f"s5_ctx_grads{k}")
        dlam = jnp.sum(dlam_l + dlam_c, axis=1)
        grads.append((dlam, b_block_diag(dbr_l + dbr_c), b_block_diag(dbi_l + dbi_c),
                      c_block_diag(dcr_l), c_block_diag(dci_l)))
    return du_c, grads


def local_step(x, ctx, target, mod, w, late=None, reducer=None):
    L, Lc = x.shape[0], ctx.shape[0]
    T = L + Lc
    assert L % Lc == 0 and Lc % (2 * N_SEG) == 0 and L % GRID_W == 0
    D = D_MODEL
    X0 = (ctx, x)

    def mod_of(i, j):
        return mod[i, :, j, :][:, None, :]

    def vec(v):
        return v.reshape(1, 1, -1).astype(F32)

    g0 = vec(w["norm_g"][0])
    H0, p0 = norm_proj(X0, g0, mod_of(0, 1), mod_of(0, 0), w["mla_w_in"], Lc, "l0_norm_in")
    cq = Rows(p0, Q_LORA_RANK, col_blk=D // Q_LORA_RANK)
    ckv = Rows(p0, KV_LORA_RANK, col_blk=(D + Q_LORA_RANK) // KV_LORA_RANK)
    kr = Rows(p0, HEAD_LANES, col_blk=(D + Q_LORA_RANK + KV_LORA_RANK) // HEAD_LANES)
    qng, kvng = vec(w["mla_q_norm"]), vec(w["mla_kv_norm"])
    tabs = rope_tables(Lc, L)
    spread = jnp.pad(rotary_spread(), ((0, HEAD_LANES - QK_ROPE_DIM), (0, 0)))
    if late is not None:
        w = {**w, **late["qkv"](p0)}
    w_uq_p = pad_heads(w["mla_w_uq"], QK_DIM)
    w_ukv3 = w["mla_w_ukv"].reshape(KV_LORA_RANK, MLA_HEADS, QK_NOPE_DIM + V_HEAD_DIM)
    w_kn_p = pad_heads(w_ukv3[:, :, :QK_NOPE_DIM].reshape(KV_LORA_RANK, -1), QK_NOPE_DIM)
    w_v = w_ukv3[:, :, QK_NOPE_DIM:].reshape(KV_LORA_RANK, -1)
    qb, qn = q_heads(cq, qng, w_uq_p, tabs)
    kb, vb, kvn = kv_heads(ckv, kvng, w_kn_p, w_v, kr, spread, tabs)
    o, lse = attn_fwd(qb, kb, vb, Lc)
    if late is not None:
        w = {**w, **late["out"](o)}
    X1, og, out0 = mla_post_fwd(o, p0, X0, mod_of(0, 2), w["mla_w_out"], Lc)

    if late is not None:
        w = {**w, **late["l1"](X1)}
    X1p = rows_to_segments(X1, Lc, "l1_to_segments")
    tgt_p = to_segments(target, 0)
    g1 = vec(w["norm_g"][1])
    H1, p1 = norm_proj(X1p, g1, mod_of(1, 1), mod_of(1, 0), w["s5_w_in"], Lc, "l1_norm_in")
    disc_fn = lambda *a: tuple(zip(*[s5_discretise(a[0][k], a[1][k], a[2][k], a[3][k], a[4][k]) for k in range(2)]))
    disc, disc_vjp = jax.vjp(disc_fn, w["s5_a_re"], w["s5_a_im"], w["s5_log_step"], _stored(w["s5_b_re"]),
                             _stored(w["s5_b_im"]))
    dirs = [s5_block_weights(disc[0][k], disc[1][k], disc[2][k], disc[3][k], w["s5_c_re"][k], w["s5_c_im"][k])
            for k in range(2)]
    y_ssm, s5_saved = s5_forward(p1, Lc, dirs)

    row = lambda v: v.reshape(1, D).astype(F32)
    (lvec, dX2, d_yssm, d_u_act, d_z1, d_fg, d_gt1, d_bg, d_d, gw_glu, gw_out) = s5_tail(
        y_ssm, p1, X1p, tgt_p, Lc, row(w["s5_d"]), row(w["s5_b_glu"]), mod[1, 1:2, 2, :], row(w["final_g"]),
        w["s5_w_glu"], w["s5_w_out"])
    loss = jnp.sum(lvec)
    gw = {"final_g": d_fg.reshape(D), "s5_b_glu": d_bg.reshape(D), "s5_d": d_d.reshape(D),
          "s5_w_glu": gw_glu, "s5_w_out": gw_out}
    dmod = {}

    du_p, s5_g = s5_backward(d_yssm, d_u_act, p1, Lc, dirs, s5_saved)
    d_disc = tuple(tuple(s5_g[k][j - 1].reshape(disc[j][k].shape) if j >= 2 else
                         s5_g[k][0][j].reshape(disc[j][k].shape) for k in range(2)) for j in range(4))
    gw["s5_a_re"], gw["s5_a_im"], gw["s5_log_step"], d_bt_re, d_bt_im = disc_vjp(d_disc)
    gw["s5_b_re"], gw["s5_b_im"] = jnp.swapaxes(d_bt_re, -1, -2), jnp.swapaxes(d_bt_im, -1, -2)
    gw["s5_c_re"] = jnp.stack([s5_g[0][3], s5_g[1][3]])
    gw["s5_c_im"] = jnp.stack([s5_g[0][4], s5_g[1][4]])
    gw["s5_w_in"] = mm_tn(H1, du_p, name="l1_in_dw", b_more=d_z1)
    if reducer is not None:
        g1 = g1 + reducer["l1"][0]({n: gw.pop(n) for n in LAYER1_MATS})[0, 0]
    d_X1, d_g1, d_sc1, d_sh1 = norm_proj_bwd([(du_p, 0, 0), (d_z1, D, Lc)], w["s5_w_in"], X1p, g1, mod_of(1, 1),
                                             mod_of(1, 0), dX2, Lc, "l1_norm_in_bwd", dx_from_segments=True)
    d_gt1_full = jnp.concatenate([jnp.zeros((1, 1, D), F32), d_gt1[None]], axis=0)
    dmod[1] = (d_sh1, d_sc1, d_gt1_full)

    d_o, d_z0, d_gt0, gw["mla_w_out"] = mla_post_bwd(d_X1, out0, og, o, p0, mod_of(0, 2), w["mla_w_out"], Lc)
    if reducer is not None:
        started = reducer["l1"][1](d_o)[0, 0] + reducer["out"][0]({"mla_w_out": gw.pop("mla_w_out")})[0, 0]
        tabs = (tabs[0] + started,) + tabs[1:]
    d_q, dk_p, d_v = attn_bwd(qb, kb, vb, o, d_o, lse, tabs, Lc)
    if reducer is not None:
        tabs = (tabs[0] + reducer["out"][1](d_q)[0, 0],) + tabs[1:]
    d_k, d_kr = heads_unrope(dk_p, tabs, spread,
                             name="l0_k_unrope")
    d_qn = mm_nt(d_q, w_uq_p, name="l0_uq_dx")
    gw["mla_w_uq"] = unpad_heads(mm_tn(qn, d_q, name="l0_uq_dw"), QK_DIM)
    d_kvn = mm_nt(d_k, w_kn_p, name="l0_ukn_dx") + mm_nt(d_v, w_v, name="l0_uv_dx")
    dw_kn = unpad_heads(mm_tn(kvn, d_k, name="l0_ukn_dw"), QK_NOPE_DIM).reshape(KV_LORA_RANK, MLA_HEADS, QK_NOPE_DIM)
    dw_v = mm_tn(kvn, d_v, name="l0_uv_dw").reshape(KV_LORA_RANK, MLA_HEADS, V_HEAD_DIM)
    gw["mla_w_ukv"] = jnp.concatenate([dw_kn, dw_v], axis=-1).reshape(KV_LORA_RANK, -1)
    d_cq, d_qng = rowwise_bwd(f_rms, [cq], [qng], [d_qn], [0], [0], T, 0, "l0_qnorm_bwd")
    d_ckv, d_kvng = rowwise_bwd(f_rms, [ckv], [kvng], [d_kvn], [0], [0], T, 0, "l0_kvnorm_bwd")
    gw["mla_q_norm"] = d_qng.reshape(-1)
    gw["mla_kv_norm"] = d_kvng.reshape(-1)
    o_cq, o_ckv = D, D + Q_LORA_RANK
    o_kr = o_ckv + KV_LORA_RANK
    d_head = jnp.concatenate([d_cq, d_ckv, d_kr], axis=1)
    gw["mla_w_in"] = jnp.concatenate([mm_tn(H0, d_head, name="l0_in_dw_head")[:, :P0_HEAD],
                                      mm_tn(H0, d_z0, name="l0_in_dw_z")], axis=1)
    dx, d_g0, d_sc0, d_sh0 = norm_proj_bwd(
        [(d_z0, 0, 0), (d_cq, o_cq, 0), (d_ckv, o_ckv, 0), (d_kr, o_kr, 0)], w["mla_w_in"], X0, g0, mod_of(0, 1),
        mod_of(0, 0), d_X1, Lc, "l0_norm_in_bwd", latent_dx_only=True)
    dmod[0] = (d_sh0, d_sc0, d_gt0)
    gw["norm_g"] = jnp.stack([d_g0.reshape(D), d_g1.reshape(D)])
    dmod_arr = jnp.stack([jnp.stack([dmod[i][j][:, 0, :] for j in range(3)], axis=1) for i in range(2)])
    ready = {**reducer["l1"][2](dx), **reducer["out"][2](dx)} if reducer is not None else {}
    return loss, dx, dmod_arr, gw, ready


SHARDED = {
    "mla_w_in": 1, "mla_w_uq": 1, "mla_w_ukv": 1, "mla_w_out": 0,
    "s5_w_in": 1, "s5_w_glu": 0, "s5_w_out": 0, "s5_d": 0, "s5_b_glu": 0,
}
SHARDED_MATS = ["mla_w_in", "mla_w_uq", "mla_w_ukv", "mla_w_out", "s5_w_in", "s5_w_glu", "s5_w_out"]
SHARDED_VECS = ["s5_d", "s5_b_glu"]
REPLICATED = ["norm_g", "mla_q_norm", "mla_kv_norm", "s5_a_re", "s5_a_im", "s5_log_step", "s5_b_re", "s5_b_im",
              "s5_c_re", "s5_c_im", "final_g"]
WEIGHT_ORDER = ["c_ctx", "ada_w", "ada_b", "norm_g", "mla_w_in", "mla_q_norm", "mla_w_uq", "mla_kv_norm", "mla_w_ukv",
                "mla_w_out", "s5_w_in", "s5_a_re", "s5_a_im", "s5_log_step", "s5_b_re", "s5_b_im", "s5_c_re", "s5_c_im",
                "s5_d", "s5_w_glu", "s5_b_glu", "s5_w_out", "final_g"]


P0_HEAD = Q_LORA_RANK + KV_LORA_RANK + QK_ROPE_DIM


P0_WIDTH = 1536


def w_in_to_kernel_order(w):
    pad = jnp.zeros((w.shape[0], P0_WIDTH - w.shape[1]), w.dtype)
    return jnp.concatenate([w[:, P0_HEAD:], w[:, :P0_HEAD], pad], axis=1)


LAYER0_MATS = ["mla_w_in", "mla_w_uq", "mla_w_ukv", "mla_w_out"]
LAYER1_MATS = ["s5_w_in", "s5_w_glu", "s5_w_out"]


def _whole_matrices(names, own_blocks, gathered):
    chip = _chip_index(_coords())
    full = {}
    for n, own, o in zip(names, own_blocks, gathered):
        slot = lax.broadcasted_iota(jnp.int32, (N_CHIP, 1, 1), 0)
        o = jnp.where(slot == chip, own[None], o.reshape((N_CHIP,) + own.shape))
        full[n] = o.reshape(-1, o.shape[-1]) if SHARDED[n] == 0 else o.transpose(1, 0, 2).reshape(o.shape[1], -1)
    return full


FIRST_MATS = ["mla_w_in"]
LATER_GROUPS = {"qkv": ["mla_w_uq", "mla_w_ukv"], "out": ["mla_w_out"], "l1": LAYER1_MATS}


def gather_weights(ws):
    mats = [ws[n].astype(BF16) for n in FIRST_MATS]
    full = _whole_matrices(FIRST_MATS, mats, gather_halves(mats, "gather_weights"))
    full["mla_w_in"] = w_in_to_kernel_order(full["mla_w_in"])
    return full


def gather_weights_behind(ws, after):
    token, finish = 0.0, {}
    for group, names in LATER_GROUPS.items():
        mats = [ws[n].astype(BF16) for n in names]
        flight, tok = exchange_start(
            mats, [jax.ShapeDtypeStruct((N_CHIP,) + m.shape, m.dtype) for m in mats],
            [(f, a, lambda me, peer: None, a, lambda s: (_chip_index(s),))
             for a in range(len(mats)) for f in CHIP_FLIPS], f"gather_{group}_start", after=after)
        after = [tok]
        token = token + tok[0, 0]

        def finish_group(after_work, group=group, names=names, flight=flight):
            own, got = exchange_wait(flight, after_work, f"gather_{group}_wait")
            return _whole_matrices(names, own, got)

        finish[group] = finish_group
    return token, finish


def _grad_slots(gw, names):
    slots = []
    for n in names:
        g = gw[n]
        if SHARDED[n] == 0:
            slots.append(g.reshape(N_CHIP, 2, g.shape[0] // (2 * N_CHIP), g.shape[1]))
        else:
            k, n4 = g.shape
            slots.append(g.reshape(k, N_CHIP, n4 // N_CHIP).transpose(1, 0, 2)
                         .reshape(N_CHIP, 2, k // 2, n4 // N_CHIP))
    return slots


def _to_sibling_half(count):
    return [(CORE_FLIP, i, lambda me, peer: (slice(None), 1 - me[2]), i, lambda s: None) for i in range(count)]


def _to_chips(count):
    return [(f, i, lambda me, peer: (_chip_index(peer),), i, lambda s: (_chip_index(s),))
            for i in range(count) for f in CHIP_FLIPS]


def _place():
    me = _coords()
    return jnp.stack([me[2], _chip_index(me)]).astype(jnp.int32)


def reduce_behind(names, tag):
    state = {}
    count = len(names)

    def begin(gw):
        slots = _grad_slots(gw, names)
        lands = [jax.ShapeDtypeStruct((N_CHIP,) + s.shape[2:], F32) for s in slots]
        state["in"], token = exchange_start(slots, lands, _to_sibling_half(count), f"grads_{tag}_swap_in_start")
        return token

    def middle(after):
        slots, got = exchange_wait(state["in"], after, f"grads_{tag}_swap_in_wait")
        place = _place()
        sums = [pair_add(s, g, place[:1], BF16, f"grads_pair_{n}") for n, s, g in zip(names, slots, got)]
        lands = [jax.ShapeDtypeStruct(s.shape, s.dtype) for s in sums]
        state["out"], token = exchange_start(sums, lands, _to_chips(count), f"grads_{tag}_scatter_start")
        return token

    def end(after):
        sums, parts = exchange_wait(state["out"], after, f"grads_{tag}_scatter_wait")
        place = _place()
        return {n: sum_chips(p, s, place, f"grads_sum_{n}") for n, p, s in zip(names, parts, sums)}

    return begin, middle, end


def reduce_gradients(gw, ready_halves, also=None):
    me = _coords()
    place = _place()
    mat_names = [n for n in SHARDED_MATS if n not in ready_halves]
    slots = dict(zip(mat_names, _grad_slots(gw, mat_names)))
    gw = {**gw, **(also or {})}
    small_names = REPLICATED + SHARDED_VECS + list(also or {})
    small, small_offs = pack_flat([_stored(gw[n]).astype(F32) for n in small_names], F32, N_CHIP * 32 * 128)
    slots["small"] = small.reshape(N_CHIP, 2, -1, 128)
    names = list(slots)
    count = len(names)
    got = exchange([slots[n] for n in names],
                   [jax.ShapeDtypeStruct((N_CHIP,) + slots[n].shape[2:], F32) for n in names],
                   _to_sibling_half(count), [], "grads_swap_in")
    sums = [pair_add(slots[n], g, place[:1], F32 if n == "small" else BF16, f"grads_pair_{n}")
            for n, g in zip(names, got)]
    parts = exchange(sums, [jax.ShapeDtypeStruct(s.shape, s.dtype) for s in sums], _to_chips(count), [],
                     "grads_scatter")
    halves = {n: sum_chips(p, s, place, f"grads_sum_{n}") for n, p, s in zip(names, parts, sums)}
    halves.update(ready_halves)
    all_names = list(halves)
    fulls = exchange(
        [halves[n] for n in all_names], [jax.ShapeDtypeStruct(halves[n].shape, F32) for n in all_names],
        [(CORE_FLIP, i, lambda me, peer: (me[2],), i, lambda s: (s[2],)) for i in range(len(all_names))], [],
        "grads_swap_out", aliases={i: i for i in range(len(all_names))})
    out = {n: f.reshape(-1, f.shape[-1]) for n, f in zip(all_names, fulls)}
    quarter = out.pop("small")
    gather, token = exchange_start(
        [quarter], [jax.ShapeDtypeStruct((N_CHIP,) + quarter.shape, F32)],
        [(f, 0, lambda me, peer: None, 0, lambda s: (_chip_index(s),)) for f in CHIP_FLIPS],
        "grads_gather_small_start")

    def finish_small(after):
        (own,), (got_small,) = exchange_wait(gather, after, "grads_gather_small_wait")
        slot = lax.broadcasted_iota(jnp.int32, (N_CHIP, 1, 1), 0)
        small_all = jnp.where(slot == _chip_index(me), own[None], got_small)
        vals = unpack_flat(small_all, small_offs, [_stored_shape(gw[n].shape) for n in small_names])
        res = {}
        for n, v in zip(small_names, vals):
            v = _from_stored(v, gw[n].shape)
            if n in SHARDED_VECS:
                size = v.shape[0] // N_CHIP
                v = lax.dynamic_slice_in_dim(v, _chip_index(me) * size, size)
            res[n] = v
        return res

    return out, finish_small, token


def kernel(x, c, ctx, c_ctx, ada_w, ada_b, norm_g, mla_w_in, mla_q_norm, mla_w_uq, mla_kv_norm, mla_w_ukv, mla_w_out, s5_w_in, s5_a_re, s5_a_im, s5_log_step, s5_b_re, s5_b_im, s5_c_re, s5_c_im, s5_d, s5_w_glu, s5_b_glu, s5_w_out, final_g, loss_target, m_c_ctx, m_ada_w, m_ada_b, m_norm_g, m_mla_w_in, m_mla_q_norm, m_mla_w_uq, m_mla_kv_norm, m_mla_w_ukv, m_mla_w_out, m_s5_w_in, m_s5_a_re, m_s5_a_im, m_s5_log_step, m_s5_b_re, m_s5_b_im, m_s5_c_re, m_s5_c_im, m_s5_d, m_s5_w_glu, m_s5_b_glu, m_s5_w_out, m_final_g, v_c_ctx, v_ada_w, v_ada_b, v_norm_g, v_mla_w_in, v_mla_q_norm, v_mla_w_uq, v_mla_kv_norm, v_mla_w_ukv, v_mla_w_out, v_s5_w_in, v_s5_a_re, v_s5_a_im, v_s5_log_step, v_s5_b_re, v_s5_b_im, v_s5_c_re, v_s5_c_im, v_s5_d, v_s5_w_glu, v_s5_b_glu, v_s5_w_out, v_final_g):
    args = dict(locals())
    weights = {n: args[n] for n in WEIGHT_ORDER}
    D = D_MODEL
    xi, yi, ci = _coords()
    chip = 2 * xi + yi
    me = 4 * xi + 2 * yi + ci
    n_col = ada_w.shape[2]

    c_all = allgather_devices(jnp.pad(c, ((0, 7), (0, 0))), "gather_c")[:, 0, :]
    cond = jnp.concatenate([c_all, jnp.broadcast_to(c_ctx[None], (8, D))], axis=0)
    (s_cond,) = rowwise_fwd(lambda v: (_silu(v),), [cond], [], [D], [F32], 16, 0, "cond_silu")
    ada_rows = ada_w.reshape(2 * D, n_col)
    mod_cols = jnp.stack([mm_nn(s_cond, ada_rows, name=f"mod_proj{i}", b_blk=i) for i in range(2)])
    vec_tiles = [jnp.pad(weights[n][0].reshape(-1, 128), ((0, 6), (0, 0))) for n in SHARDED_VECS]
    mod_all, *vec_all = allgather_chips([mod_cols] + vec_tiles, "gather_mod")
    mod_all = mod_all.transpose(1, 2, 0, 3).reshape(2, 16, 3 * D) + ada_b[:, None, :]
    mine = lax.broadcasted_iota(jnp.int32, (1, 16, 1), 1) == me
    mod_l = jnp.sum(jnp.where(mine, mod_all, 0.0), axis=1)
    mod_c = mod_all[:, 8, :]
    mod = jnp.stack([mod_c.reshape(2, 3, D), mod_l.reshape(2, 3, D)], axis=1)

    w = gather_weights({n: weights[n][0] for n in FIRST_MATS})
    token, late = gather_weights_behind({n: weights[n][0] for n in SHARDED_MATS}, [w["mla_w_in"], mod])
    for n, v in zip(SHARDED_VECS, vec_all):
        w[n] = v[:, :2, :].reshape(-1)
    for n in ["norm_g", "final_g"]:
        w[n] = weights[n]
    for n in ["mla_q_norm", "mla_kv_norm", "s5_a_re", "s5_a_im", "s5_log_step", "s5_b_re", "s5_b_im",
              "s5_c_re", "s5_c_im"]:
        w[n] = weights[n][0]

    reducer = {"l1": reduce_behind(LAYER1_MATS, "l1"), "out": reduce_behind(["mla_w_out"], "out")}
    loss_me, dx, dmod, gw, ready = local_step(x[0], ctx[0], loss_target[0], mod + token, w, late,
                                              reducer)

    dmod_rows, loss_all = _gather([dmod.reshape(2, 2, 3 * D), jnp.broadcast_to(loss_me, (8, 128))],
                                  ALL_FLIPS, _dev_index, N_DEV, "gather_dmod")
    loss = functools.reduce(lambda s, d: s + loss_all[d, 0, 0], range(1, N_DEV), loss_all[0, 0, 0])
    dm = jnp.concatenate([dmod_rows[:, :, 1, :], dmod_rows[:, :, 0, :]], axis=0).transpose(1, 0, 2)
    g_ada_b = jnp.sum(dm, axis=1)
    dm_cols = lax.dynamic_slice_in_dim(dm, chip * n_col, n_col, axis=2)
    g_ada_w = jnp.stack([mm_tn(s_cond, dm_cols[i], name=f"mod_proj_dw{i}") for i in range(2)])
    dmc = jnp.sum(dm_cols[:, 8:, :], axis=1)
    dmc8 = jnp.broadcast_to(dmc[:, None, :], (2, 8, n_col))
    g_sc = (mm_nt(dmc8[0], ada_rows, name="mod_proj_dx0", b_rows=D, b_blk=0)[0]
            + mm_nt(dmc8[1], ada_rows, name="mod_proj_dx1", b_rows=D, b_blk=1)[0])
    g_silu_part = jnp.where(ci == 0, g_sc, 0.0)

    grads = {"ada_w": g_ada_w, "ada_b": g_ada_b}
    deltas, new_m, new_v = {}, {}, {}
    small = [n for n in WEIGHT_ORDER if weights[n].size < 50000]

    def update(n, after=None):
        shp = weights[n].shape
        rows = lambda a: _stored(a.reshape(shp)).reshape(-1, _stored_shape(shp)[-1])
        back = lambda a: _from_stored(a.reshape(_stored_shape(shp)), shp)
        d_, m_, v_ = adamw(rows(weights[n]), rows(grads[n]), rows(args["m_" + n]), rows(args["v_" + n]),
                           name=f"adamw_{n}", after=after)
        deltas[n], new_m[n], new_v[n] = back(d_), back(m_), back(v_)

    red, finish_small, small_started = reduce_gradients(gw, ready, {"silu_c_ctx": g_silu_part})
    update("ada_w", small_started)
    for n in SHARDED_MATS:
        grads[n] = red[n].reshape(weights[n].shape)
        update(n, small_started)
    red_small = finish_small([deltas[n] for n in ["ada_w"] + SHARDED_MATS])
    for n in REPLICATED + SHARDED_VECS:
        grads[n] = red_small[n].reshape(weights[n].shape)
    (g_c_ctx,) = rowwise_bwd(lambda v: (_silu(v),), [jnp.broadcast_to(c_ctx[None], (8, D))], [],
                             [jnp.broadcast_to(red_small["silu_c_ctx"][None], (8, D))], [0], [], 8, 0, "cond_silu_bwd")
    grads["c_ctx"] = g_c_ctx[0]
    for n in WEIGHT_ORDER:
        if n not in small and n not in deltas:
            update(n)
    packs = []
    offs = None
    for src in (weights, grads, {n: args["m_" + n] for n in small}, {n: args["v_" + n] for n in small}):
        buf, offs = pack_flat([src[n] for n in small], F32)
        packs.append(buf)
    outs = adamw(*packs, name="adamw_small")
    for res, dst in zip(outs, (deltas, new_m, new_v)):
        for n, val in zip(small, unpack_flat(res, offs, [weights[n].shape for n in small])):
            dst[n] = val

    return (loss, dx[None], *[grads[n] for n in WEIGHT_ORDER], *[deltas[n] for n in WEIGHT_ORDER],
            *[new_m[n] for n in WEIGHT_ORDER], *[new_v[n] for n in WEIGHT_ORDER])
```

```python
import functools
import math

import jax
import jax.numpy as jnp
import numpy as np
from jax import lax
from jax.experimental import pallas as pl
from jax.experimental.pallas import tpu as pltpu

F32 = jnp.float32
BF16 = jnp.bfloat16

D_MODEL = 1024
GRID_W = 64
EPS = 1e-6
MLA_HEADS = 16
QK_NOPE_DIM = 64
QK_ROPE_DIM = 32
V_HEAD_DIM = 64
Q_LORA_RANK = 256
KV_LORA_RANK = 128
QK_DIM = QK_NOPE_DIM + QK_ROPE_DIM
SOFTMAX_SCALE = QK_DIM ** -0.5
ROPE_THETA = 10000.0
S5_GROUP = 16
S5_GROUPS = D_MODEL // S5_GROUP
S5_STATE = 64
S5_LANES = S5_GROUPS * S5_STATE
N_SEG = 8
GROUPS_PER_BLOCK = 8
N_BLOCKS = S5_GROUPS // GROUPS_PER_BLOCK
BLK_CH = GROUPS_PER_BLOCK * S5_GROUP
BLK_ST = GROUPS_PER_BLOCK * S5_STATE

ADAM_LR = 0.001
ADAM_B1 = 0.9
ADAM_B2 = 0.999
ADAM_EPS = 1e-08
ADAM_WD = 0.01
ADAM_STEP = 10

N_DEV = 8
N_CHIP = 4
MESH = pl.DeviceIdType.MESH
VMEM_LIMIT = 52 * 1024 * 1024
ROW_TILE = 256


def _params(sem=None, vmem=None):
    return pltpu.CompilerParams(dimension_semantics=sem, vmem_limit_bytes=vmem)


def mm_nn(a, b, out_dtype=F32, name="mm_nn", b_blk=0):
    M, K = a.shape
    N = b.shape[1]
    tm = math.gcd(ROW_TILE, M)

    def body(a_ref, b_ref, o_ref):
        o_ref[...] = jnp.dot(a_ref[...].astype(BF16), b_ref[...].astype(BF16),
                             preferred_element_type=F32).astype(o_ref.dtype)

    return pl.pallas_call(
        body, out_shape=jax.ShapeDtypeStruct((M, N), out_dtype), grid=(M // tm,),
        in_specs=[pl.BlockSpec((tm, K), lambda i: (i, 0)), pl.BlockSpec((K, N), lambda i: (b_blk, 0))],
        out_specs=pl.BlockSpec((tm, N), lambda i: (i, 0)),
        compiler_params=_params(("parallel",), VMEM_LIMIT), name=name)(a, b)


def mm_nt(a, b, out_dtype=F32, name="mm_nt", b_rows=None, b_blk=0):
    M, N = a.shape
    K = b.shape[0] if b_rows is None else b_rows
    tm = math.gcd(ROW_TILE, M)

    def body(a_ref, b_ref, o_ref):
        o_ref[...] = lax.dot_general(a_ref[...].astype(BF16), b_ref[...].astype(BF16),
                                     (((1,), (1,)), ((), ())),
                                     preferred_element_type=F32).astype(o_ref.dtype)

    return pl.pallas_call(
        body, out_shape=jax.ShapeDtypeStruct((M, K), out_dtype), grid=(M // tm,),
        in_specs=[pl.BlockSpec((tm, N), lambda i: (i, 0)), pl.BlockSpec((K, N), lambda i: (b_blk, 0))],
        out_specs=pl.BlockSpec((tm, K), lambda i: (i, 0)),
        compiler_params=_params(("parallel",), VMEM_LIMIT), name=name)(a, b)


def mm_tn(a, b, name="mm_tn", b_more=None):
    M, N = b.shape
    K = a.shape[1]
    tn = math.gcd(512, N) if N % 128 == 0 and N > 512 else N
    bs = [b] if b_more is None else [b, b_more]
    assert all(x.shape[1] == N for x in bs)
    nb = N // tn

    def body(a_ref, *refs):
        o_ref = refs[-1]
        for k, b_ref in enumerate(refs[:-1]):
            def product(b_ref=b_ref):
                o_ref[...] = lax.dot_general(a_ref[a.shape[0] - b_ref.shape[0]:, :].astype(BF16),
                                             b_ref[...].astype(BF16), (((0,), (0,)), ((), ())),
                                             preferred_element_type=F32)
            if len(bs) == 1:
                product()
            else:
                pl.when(pl.program_id(0) // nb == k)(product)

    def b_spec(k, rows):
        return pl.BlockSpec((rows, tn), lambda j: (0, jnp.clip(j - k * nb, 0, nb - 1)))

    return pl.pallas_call(
        body, out_shape=jax.ShapeDtypeStruct((K, N * len(bs)), F32), grid=(nb * len(bs),),
        in_specs=[pl.BlockSpec(a.shape, lambda j: (0, 0))] + [b_spec(k, x.shape[0]) for k, x in enumerate(bs)],
        out_specs=pl.BlockSpec((K, tn), lambda j: (0, j)),
        compiler_params=_params(("parallel",), VMEM_LIMIT), name=name)(a, *bs)


class Rows:
    def __init__(self, arr, width=None, row_off=0, col_blk=0):
        self.arr = arr
        self.width = arr.shape[1] if width is None else width
        self.row_off = row_off
        self.col_blk = col_blk

    def spec(self, tm):
        ro, cb = self.row_off // tm, self.col_blk
        return pl.BlockSpec((tm, self.width), lambda i: (i + ro, cb))


def _as_rows(x):
    return x if isinstance(x, Rows) else Rows(x)


def _row_tile(n_rows, n_ctx_rows, rows):
    tm = math.gcd(ROW_TILE, n_rows, n_ctx_rows)
    for r in rows:
        tm = math.gcd(tm, r.row_off)
    return tm


def _bc_spec(arr, n_ctx_blocks):
    g, _, d = arr.shape
    if g == 1:
        return pl.BlockSpec((1, 1, d), lambda i: (0, 0, 0))
    return pl.BlockSpec((1, 1, d), lambda i: ((i >= n_ctx_blocks).astype(jnp.int32), 0, 0))


def rowwise_fwd(fn, rows, bcs, out_dims, out_dtypes, n_rows, n_ctx_rows, name):
    rows = [_as_rows(r) for r in rows]
    tm = _row_tile(n_rows, n_ctx_rows, rows)
    ncb = n_ctx_rows // tm
    nr, nb = len(rows), len(bcs)

    def body(*refs):
        vals = [r[...].astype(F32) for r in refs[:nr]] + [b[0].astype(F32) for b in refs[nr:nr + nb]]
        outs = fn(*vals)
        for o_ref, v in zip(refs[nr + nb:], outs):
            o_ref[...] = v.astype(o_ref.dtype)

    outs = pl.pallas_call(
        body,
        out_shape=[jax.ShapeDtypeStruct((n_rows, d), dt) for d, dt in zip(out_dims, out_dtypes)],
        grid=(n_rows // tm,),
        in_specs=[r.spec(tm) for r in rows] + [_bc_spec(b, ncb) for b in bcs],
        out_specs=[pl.BlockSpec((tm, d), lambda i: (i, 0)) for d in out_dims],
        compiler_params=_params(("parallel",), VMEM_LIMIT), name=name)(*[r.arr for r in rows], *bcs)
    return outs


def rowwise_bwd(fn, rows, bcs, cts, diff_rows, diff_bcs, n_rows, n_ctx_rows, name, ct_extra=None, lat_add=None):
    rows = [_as_rows(r) for r in rows]
    cts = [_as_rows(c) for c in cts]
    extra = [_as_rows(ct_extra)] if ct_extra is not None else []
    tm = _row_tile(n_rows, n_ctx_rows, rows + cts + extra)
    ncb = n_ctx_rows // tm
    nr, nb, nc = len(rows), len(bcs), len(cts)
    ndr, ndb = len(diff_rows), len(diff_bcs)
    n_in = nr + nb + nc + len(extra) + (lat_add is not None)

    def body(*refs):
        i = pl.program_id(0)
        rvals = [r[...].astype(F32) for r in refs[:nr]]
        bvals = [b[0].astype(F32) for b in refs[nr:nr + nb]]
        cvals = [c[...].astype(F32) for c in refs[nr + nb:nr + nb + nc]]
        if extra:
            cvals[0] = cvals[0] + refs[nr + nb + nc][...].astype(F32)
        outs = refs[n_in:]

        def f(*d):
            rv, bv = list(rvals), list(bvals)
            for k, idx in enumerate(diff_rows):
                rv[idx] = d[k]
            for k, idx in enumerate(diff_bcs):
                bv[idx] = d[ndr + k]
            return tuple(fn(*rv, *bv))

        primals = [rvals[k] for k in diff_rows] + [bvals[k] for k in diff_bcs]
        _, vjp = jax.vjp(f, *primals)
        grads = list(vjp(tuple(cvals)))
        if lat_add is not None:
            add = refs[n_in - 1][...]
            grads[0] = grads[0] + (add if lat_add.shape[0] == n_rows else jnp.where(i >= ncb, add, 0.0))
        for k in range(ndr):
            outs[k][...] = grads[k].astype(outs[k].dtype)
        for k, idx in enumerate(diff_bcs):
            o_ref = outs[ndr + k]
            first = (i == 0)
            if bcs[idx].shape[0] == 2:
                first = first | (i == ncb)

            @pl.when(first)
            def _(o_ref=o_ref):
                o_ref[...] = jnp.zeros_like(o_ref)

            o_ref[0] += grads[ndr + k]

    out_shape = [jax.ShapeDtypeStruct((n_rows, rows[k].width), F32) for k in diff_rows]
    out_shape += [jax.ShapeDtypeStruct(bcs[k].shape, F32) for k in diff_bcs]
    out_specs = [pl.BlockSpec((tm, rows[k].width), lambda i: (i, 0)) for k in diff_rows]
    out_specs += [_bc_spec(bcs[k], ncb) for k in diff_bcs]
    ins = [r.arr for r in rows] + list(bcs) + [c.arr for c in cts + extra]
    in_specs = [r.spec(tm) for r in rows] + [_bc_spec(b, ncb) for b in bcs] + [c.spec(tm) for c in cts + extra]
    if lat_add is not None:
        ins.append(lat_add)
        skip = ncb if lat_add.shape[0] != n_rows else 0
        in_specs.append(pl.BlockSpec((tm, lat_add.shape[1]), lambda i: (jnp.maximum(i - skip, 0), 0)))
    outs = pl.pallas_call(
        body, out_shape=out_shape, grid=(n_rows // tm,), in_specs=in_specs, out_specs=out_specs,
        compiler_params=_params(("arbitrary",), VMEM_LIMIT), name=name)(*ins)
    return outs


def _rms(x):
    return x * lax.rsqrt(jnp.mean(x * x, axis=-1, keepdims=True) + EPS)


def _sigmoid(x):
    return 0.5 * (jnp.tanh(0.5 * x) + 1.0)


def _silu(x):
    return x * _sigmoid(x)


def _gelu_tanh(x):
    return 0.5 * x * (1.0 + jnp.tanh(math.sqrt(2.0 / math.pi) * (x + 0.044715 * (x * x * x))))


def f_norm_mod(x, g, sc, sh):
    return ((_rms(x) * g) * (1.0 + sc) + sh,)


def f_rms(x, g):
    return (_rms(x) * g,)


def f_gate(o, z):
    return (o * _silu(z),)


def f_s5_act(y, u, d):
    return (_gelu_tanh(y + d * u),)


def f_s5_glu(ya, gl, z, b):
    return (ya * _sigmoid(gl + b) * _silu(z),)


def _as_parts(x, n_ctx):
    xs = x if isinstance(x, tuple) else (x,)
    assert len(xs) == 1 or xs[0].shape[0] == n_ctx
    return xs, sum(p.shape[0] for p in xs)


def _parts_specs(xs, tm, ncb):
    d = xs[0].shape[1]
    if len(xs) == 1:
        return [pl.BlockSpec((tm, d), lambda i: (i, 0))]
    return [pl.BlockSpec((tm, d), lambda i: (jnp.minimum(i, ncb - 1), 0)),
            pl.BlockSpec((tm, d), lambda i: (jnp.maximum(i - ncb, 0), 0))]


def _parts_tile(x_refs, ncb):
    if len(x_refs) == 1:
        return x_refs[0][...]
    return jnp.where(pl.program_id(0) < ncb, x_refs[0][...], x_refs[1][...])


def norm_proj(x, g, sc, sh, w, n_ctx, name):
    xs, n = _as_parts(x, n_ctx)
    d = xs[0].shape[1]
    nw = w.shape[1]
    tm = math.gcd(ROW_TILE, n, n_ctx)
    ncb = n_ctx // tm
    nx = len(xs)

    def body(*refs):
        g_ref, sc_ref, sh_ref, w_ref, h_ref, p_ref = refs[nx:]
        h = f_norm_mod(_parts_tile(refs[:nx], ncb), g_ref[0], sc_ref[0], sh_ref[0])[0].astype(BF16)
        h_ref[...] = h
        p_ref[...] = jnp.dot(h, w_ref[...], preferred_element_type=F32)

    row = pl.BlockSpec((tm, d), lambda i: (i, 0))
    return pl.pallas_call(
        body, out_shape=[jax.ShapeDtypeStruct((n, d), BF16), jax.ShapeDtypeStruct((n, nw), F32)], grid=(n // tm,),
        in_specs=_parts_specs(xs, tm, ncb) + [_bc_spec(g, ncb), _bc_spec(sc, ncb), _bc_spec(sh, ncb),
                                              pl.BlockSpec(w.shape, lambda i: (0, 0))],
        out_specs=[row, pl.BlockSpec((tm, nw), lambda i: (i, 0))],
        compiler_params=_params(("parallel",), VMEM_LIMIT), name=name)(*xs, g, sc, sh, w)


def norm_proj_bwd(terms, w, x, g, sc, sh, add, n_ctx, name, latent_dx_only=False, dx_from_segments=False):
    xs, n = _as_parts(x, n_ctx)
    adds = add if isinstance(add, tuple) else (add,)
    d = xs[0].shape[1]
    tm = math.gcd(ROW_TILE, n, n_ctx, *[t[2] for t in terms])
    ncb = n_ctx // tm
    nt, nx, na = len(terms), len(xs), len(adds)
    add_skip = ncb if na == 1 and add.shape[0] != n else 0
    n_dx = 2 if dx_from_segments else 1
    tj = tm // N_SEG
    assert not dx_from_segments or (ncb == 1 and not latent_dx_only)

    def body(*refs):
        i = pl.program_id(0)
        a_refs = refs[:nt]
        w_ref, x_refs = refs[nt], refs[nt + 1:nt + 1 + nx]
        g_ref, sc_ref, sh_ref = refs[nt + 1 + nx:nt + 4 + nx]
        add_refs = refs[nt + 4 + nx:nt + 4 + nx + na]
        outs = refs[nt + 4 + nx + na:]
        dx_refs, (dg_ref, dsc_ref, dsh_ref) = outs[:n_dx], outs[n_dx:n_dx + 3]
        d_h = None
        for a_ref, (a, off, first) in zip(a_refs, terms):
            part = lax.dot_general(a_ref[...].astype(BF16), w_ref[:, off:off + a.shape[1]], NT_DIMS,
                                   preferred_element_type=F32)
            if first:
                part = jnp.where(i >= first // tm, part, 0.0)
            d_h = part if d_h is None else d_h + part
        _, vjp = jax.vjp(lambda x_, g_, sc_, sh_: f_norm_mod(x_, g_, sc_, sh_), _parts_tile(x_refs, ncb), g_ref[0],
                         sc_ref[0], sh_ref[0])
        d_x, d_g, d_sc, d_sh = vjp((d_h,))
        extra = _parts_tile(add_refs, ncb)
        d_x = d_x + (extra if add_skip == 0 else jnp.where(i >= ncb, extra, 0.0))
        if dx_from_segments:
            slabs = outs[n_dx + 3]
            for c in range(d // HEAD_LANES):
                slabs[c] = d_x[:, c * HEAD_LANES:(c + 1) * HEAD_LANES]
            for k, here in enumerate([i < ncb, i >= ncb]):
                @pl.when(here)
                def _(k=k):
                    for c in range(d // HEAD_LANES):
                        for seg in range(N_SEG):
                            dx_refs[k][seg, :, c * HEAD_LANES:(c + 1) * HEAD_LANES] = (
                                slabs[c, pl.ds(seg, tj, stride=N_SEG), :])
        else:
            dx_refs[0][...] = d_x

        @pl.when(i == 0)
        def _():
            dg_ref[...] = jnp.zeros_like(dg_ref)

        @pl.when((i == 0) | (i == ncb))
        def _():
            dsc_ref[...] = jnp.zeros_like(dsc_ref)
            dsh_ref[...] = jnp.zeros_like(dsh_ref)

        dg_ref[0] += d_g
        dsc_ref[0] += d_sc
        dsh_ref[0] += d_sh

    def a_spec(a, first):
        skip = first // tm
        return pl.BlockSpec((tm, a.shape[1]), lambda i: (jnp.maximum(i - skip, 0), 0))

    dx_skip = ncb if latent_dx_only else 0
    if dx_from_segments:
        dx_shapes = [jax.ShapeDtypeStruct((N_SEG, n_ctx // N_SEG, d), F32),
                     jax.ShapeDtypeStruct((N_SEG, (n - n_ctx) // N_SEG, d), F32)]
        dx_specs = [pl.BlockSpec((N_SEG, tj, d), lambda i: (0, 0, 0)),
                    pl.BlockSpec((N_SEG, tj, d), lambda i: (0, jnp.maximum(i - ncb, 0), 0))]
    else:
        dx_shapes = [jax.ShapeDtypeStruct((n - dx_skip * tm, d), F32)]
        dx_specs = [pl.BlockSpec((tm, d), lambda i: (jnp.maximum(i - dx_skip, 0), 0))]
    add_specs = (_parts_specs(adds, tm, ncb) if na == 2 else
                 [pl.BlockSpec((tm, d), lambda i: (jnp.maximum(i - add_skip, 0), 0))])
    res = pl.pallas_call(
        body,
        out_shape=dx_shapes + [jax.ShapeDtypeStruct(g.shape, F32), jax.ShapeDtypeStruct(sc.shape, F32),
                               jax.ShapeDtypeStruct(sh.shape, F32)],
        grid=(n // tm,),
        in_specs=[a_spec(a, first) for a, _, first in terms]
        + [pl.BlockSpec(w.shape, lambda i: (0, 0))] + _parts_specs(xs, tm, ncb)
        + [_bc_spec(g, ncb), _bc_spec(sc, ncb), _bc_spec(sh, ncb)] + add_specs,
        out_specs=dx_specs + [_bc_spec(g, ncb), _bc_spec(sc, ncb), _bc_spec(sh, ncb)],
        scratch_shapes=[pltpu.VMEM((d // HEAD_LANES, tm, HEAD_LANES), F32)] if dx_from_segments else [],
        compiler_params=_params(("arbitrary",), VMEM_LIMIT), name=name)(
            *[t[0] for t in terms], w, *xs, g, sc, sh, *adds)
    if dx_from_segments:
        return ((res[0].reshape(n_ctx, d), res[1].reshape(n - n_ctx, d)), *res[2:])
    return res


def mla_post_fwd(o, p0, x0, gate, w_out, n_ctx, name="l0_post"):
    n, d = o.shape
    xs, _ = _as_parts(x0, n_ctx)
    tm = math.gcd(ROW_TILE, n, n_ctx)
    ncb = n_ctx // tm
    nx = len(xs)

    def body(o_ref, z_ref, *refs):
        gt_ref, w_ref, x1_ref, og_ref, out_ref = refs[nx:]
        og = f_gate(o_ref[...], z_ref[...])[0].astype(BF16)
        out = jnp.dot(og, w_ref[...], preferred_element_type=F32)
        og_ref[...] = og
        out_ref[...] = out
        x1_ref[...] = _parts_tile(refs[:nx], ncb) + gt_ref[0] * out

    row = pl.BlockSpec((tm, d), lambda i: (i, 0))
    return pl.pallas_call(
        body, out_shape=[jax.ShapeDtypeStruct((n, d), F32), jax.ShapeDtypeStruct((n, d), BF16),
                         jax.ShapeDtypeStruct((n, d), F32)],
        grid=(n // tm,),
        in_specs=[row, row] + _parts_specs(xs, tm, ncb) + [_bc_spec(gate, ncb), pl.BlockSpec((d, d), lambda i: (0, 0))],
        out_specs=[row, row, row],
        compiler_params=_params(("parallel",), VMEM_LIMIT), name=name)(o, p0, *xs, gate, w_out)


def mla_post_bwd(dx1, out, og, o, p0, gate, w_out, n_ctx, name="l0_post_bwd"):
    n, d = o.shape
    dxs, _ = _as_parts(dx1, n_ctx)
    tm = math.gcd(ROW_TILE, n, n_ctx)
    ncb = n_ctx // tm
    nx = len(dxs)

    def body(*refs):
        out_ref, og_ref, o_ref, z_ref, gt_ref, w_ref, do_ref, dz_ref, dgt_ref, dw_ref = refs[nx:]
        i = pl.program_id(0)

        @pl.when(i == 0)
        def _():
            dw_ref[...] = jnp.zeros_like(dw_ref)

        @pl.when((i == 0) | (i == ncb))
        def _():
            dgt_ref[...] = jnp.zeros_like(dgt_ref)

        dx = _parts_tile(refs[:nx], ncb)
        dgt_ref[0] += jnp.sum(dx * out_ref[...], axis=0, keepdims=True)
        d_out16 = (gt_ref[0] * dx).astype(BF16)
        dw_ref[...] += lax.dot_general(og_ref[...], d_out16, (((0,), (0,)), ((), ())), preferred_element_type=F32)
        d_og = lax.dot_general(d_out16, w_ref[...], NT_DIMS, preferred_element_type=F32)
        _, gate_vjp = jax.vjp(lambda o_, z_: f_gate(o_, z_), o_ref[...], z_ref[...])
        d_o, d_z = gate_vjp((d_og,))
        do_ref[...] = d_o
        dz_ref[...] = d_z

    row = pl.BlockSpec((tm, d), lambda i: (i, 0))
    mat = pl.BlockSpec((d, d), lambda i: (0, 0))
    return pl.pallas_call(
        body, out_shape=[jax.ShapeDtypeStruct((n, d), F32), jax.ShapeDtypeStruct((n, d), F32),
                         jax.ShapeDtypeStruct(gate.shape, F32), jax.ShapeDtypeStruct((d, d), F32)],
        grid=(n // tm,),
        in_specs=_parts_specs(dxs, tm, ncb) + [row, row, row, row, _bc_spec(gate, ncb), mat],
        out_specs=[row, row, _bc_spec(gate, ncb), mat],
        compiler_params=_params(("arbitrary",), VMEM_LIMIT), name=name)(*dxs, out, og, o, p0, gate, w_out)


def s5_tail(y_ssm, p1, x1p, target, n_ctx, d_vec, b_glu, gate, final_g, w_glu, w_out, name="l1_tail"):
    n, d = y_ssm.shape
    tm = math.gcd(ROW_TILE, n, n_ctx)
    off = n_ctx // tm
    tn_dims = (((0,), (0,)), ((), ()))

    def row_loss(x, g, t):
        e = _rms(x) * g - t
        return 0.5 * (e * e) * (1.0 / d)

    def body(y_ref, u_ref, z_ref, x1_ref, t_ref, d_ref, b_ref, gt_ref, fg_ref, wg_ref, wo_ref,
             l_ref, dx_ref, dy_ref, du_ref, dz_ref, dfg_ref, dgt_ref, db_ref, dd_ref, dwg_ref, dwo_ref):
        @pl.when(pl.program_id(0) == 0)
        def _():
            for r in (l_ref, dfg_ref, dgt_ref, db_ref, dd_ref, dwg_ref, dwo_ref):
                r[...] = jnp.zeros_like(r)

        u, z, tgt, gt = u_ref[...], z_ref[...], t_ref[...], gt_ref[...]
        (ya,), act_vjp = jax.vjp(lambda y_, u_, d_: f_s5_act(y_, u_, d_), y_ref[...], u, d_ref[...])
        ya16 = ya.astype(BF16)
        gl = jnp.dot(ya16, wg_ref[...], preferred_element_type=F32)
        (y3,), glu_vjp = jax.vjp(lambda a_, g_, z_, b_: f_s5_glu(a_, g_, z_, b_), ya, gl, z, b_ref[...])
        y3_16 = y3.astype(BF16)
        out1 = jnp.dot(y3_16, wo_ref[...], preferred_element_type=F32)
        lterm, loss_vjp = jax.vjp(lambda x_, g_: row_loss(x_, g_, tgt), x1_ref[...] + gt * out1, fg_ref[...])
        dx2, dfg = loss_vjp(jnp.ones_like(lterm))
        l_ref[...] += jnp.sum(lterm, axis=0, keepdims=True)
        dfg_ref[...] += dfg
        dx_ref[...] = dx2
        dgt_ref[...] += jnp.sum(dx2 * out1, axis=0, keepdims=True)
        d_out16 = (gt * dx2).astype(BF16)
        dwo_ref[...] += lax.dot_general(y3_16, d_out16, tn_dims, preferred_element_type=F32)
        d_y3 = lax.dot_general(d_out16, wo_ref[...], NT_DIMS, preferred_element_type=F32)
        d_ya, d_gl, d_z, d_b = glu_vjp((d_y3,))
        dz_ref[...] = d_z
        db_ref[...] += d_b
        d_gl16 = d_gl.astype(BF16)
        dwg_ref[...] += lax.dot_general(ya16, d_gl16, tn_dims, preferred_element_type=F32)
        d_ya = d_ya + lax.dot_general(d_gl16, wg_ref[...], NT_DIMS, preferred_element_type=F32)
        d_y, d_u, d_d = act_vjp((d_ya,))
        dy_ref[...] = d_y
        du_ref[...] = d_u
        dd_ref[...] += d_d

    row = pl.BlockSpec((tm, d), lambda i: (i, 0))
    vecs = pl.BlockSpec((1, d), lambda i: (0, 0))
    mat = pl.BlockSpec((d, d), lambda i: (0, 0))
    return pl.pallas_call(
        body,
        out_shape=[jax.ShapeDtypeStruct((1, d), F32)] + [jax.ShapeDtypeStruct((n, d), F32)] * 4
        + [jax.ShapeDtypeStruct((1, d), F32)] * 4 + [jax.ShapeDtypeStruct((d, d), F32)] * 2,
        grid=(n // tm,),
        in_specs=[row, pl.BlockSpec((tm, d), lambda i: (i + off, 0)), pl.BlockSpec((tm, d), lambda i: (i + off, 1)),
                  pl.BlockSpec((tm, d), lambda i: (i + off, 0)), row, vecs, vecs, vecs, vecs, mat, mat],
        out_specs=[vecs, row, row, row, row, vecs, vecs, vecs, vecs, mat, mat],
        compiler_params=_params(("arbitrary",), VMEM_LIMIT), name=name)(
            y_ssm, p1, p1, x1p, target, d_vec, b_glu, gate, final_g, w_glu, w_out)


NT_DIMS = (((1,), (1,)), ((), ()))
HEAD_LANES = 128
N_PAIRS = MLA_HEADS // 2


def _own_lanes(shape, hh):
    lane = lax.broadcasted_iota(jnp.int32, shape, len(shape) - 1)
    return (lane < V_HEAD_DIM) if hh == 0 else (lane >= V_HEAD_DIM)


def _delta_lane(hh):
    return V_HEAD_DIM if hh == 0 else 0


def _rope_tiles(x, cos, sin_next, sin_prev, inverse):
    width = x.shape[-1]
    reps = width // HEAD_LANES
    c, sn, sp = (jnp.tile(t, (1, reps)) for t in (cos, sin_next, sin_prev))
    if inverse:
        return x * c + pltpu.roll(x * sn, 8, 1) + pltpu.roll(x * sp, width - 8, 1)
    return x * c + pltpu.roll(x, width - 8, 1) * sn + pltpu.roll(x, 8, 1) * sp


STAT_LANE = QK_DIM


def _with_stat(x16, col, lane0):
    hi = col.astype(BF16)
    r1 = col - hi.astype(F32)
    mid = r1.astype(BF16)
    lo = (r1 - mid.astype(F32)).astype(BF16)
    lane = lax.broadcasted_iota(jnp.int32, x16.shape, 1)
    return jnp.where(lane == lane0, hi, jnp.where(lane == lane0 + 1, mid, jnp.where(lane == lane0 + 2, lo, x16)))


def attn_fwd(qb, kb, vb, n_ctx):
    T = qb.shape[0]
    tq = math.gcd(ROW_TILE, n_ctx)
    nq, ncb = T // tq, n_ctx // tq

    def body(q_ref, k_ref, v_ref, o_ref, lse_ref):
        qi = pl.program_id(1)

        def rows(n_keys):
            v = v_ref[:n_keys, :]
            outs = []
            for hh in range(2):
                hs = slice(hh * HEAD_LANES, (hh + 1) * HEAD_LANES)
                s = lax.dot_general(q_ref[:, hs], k_ref[:n_keys, hs], NT_DIMS,
                                    preferred_element_type=F32) * SOFTMAX_SCALE
                m = jnp.max(s, axis=-1, keepdims=True)
                p = jnp.exp(s - m)
                l = jnp.sum(p, axis=-1, keepdims=True)
                outs.append(jnp.dot(p.astype(BF16), v, preferred_element_type=F32) / l)
                lse_ref[hh] = m + jnp.log(l)
            o_ref[...] = jnp.where(_own_lanes(outs[0].shape, 0), outs[0], outs[1])

        pl.when(qi < ncb)(lambda: rows(n_ctx))
        pl.when(qi >= ncb)(lambda: rows(T))

    return pl.pallas_call(
        body,
        out_shape=[jax.ShapeDtypeStruct((T, MLA_HEADS * V_HEAD_DIM), F32),
                   jax.ShapeDtypeStruct((MLA_HEADS, T, 1), F32)],
        grid=(N_PAIRS, nq),
        in_specs=[pl.BlockSpec((tq, 2 * HEAD_LANES), lambda h, i: (i, h)),
                  pl.BlockSpec((T, 2 * HEAD_LANES), lambda h, i: (0, h)),
                  pl.BlockSpec((T, 2 * V_HEAD_DIM), lambda h, i: (0, h))],
        out_specs=[pl.BlockSpec((tq, 2 * V_HEAD_DIM), lambda h, i: (i, h)),
                   pl.BlockSpec((2, tq, 1), lambda h, i: (h, i, 0))],
        compiler_params=_params(("parallel", "parallel"), VMEM_LIMIT), name="attn_fwd")(qb, kb, vb)


def attn_bwd_dq(qb, kb, vb, o, do, lse, tabs, n_ctx):
    T = qb.shape[0]
    tq = math.gcd(ROW_TILE, n_ctx)
    nq, ncb = T // tq, n_ctx // tq

    def body(q_ref, k_ref, v_ref, o_ref, do_ref, lse_ref, c_ref, sn_ref, sp_ref, dq_ref, dos_ref):
        qi = pl.program_id(1)

        def rows(n_keys):
            v = v_ref[:n_keys, :]
            dqs = []
            for hh in range(2):
                hs = slice(hh * HEAD_LANES, (hh + 1) * HEAD_LANES)
                k = k_ref[:n_keys, hs]
                do = jnp.where(_own_lanes(do_ref.shape, hh), do_ref[...], 0.0)
                delta = jnp.sum(do * o_ref[...], axis=-1, keepdims=True)
                s = lax.dot_general(q_ref[:, hs], k, NT_DIMS, preferred_element_type=F32) * SOFTMAX_SCALE
                p = jnp.exp(s - lse_ref[hh])
                do16 = do.astype(BF16)
                dp = lax.dot_general(do16, v, NT_DIMS, preferred_element_type=F32)
                ds = p * (dp - delta) * SOFTMAX_SCALE
                dqs.append(jnp.dot(ds.astype(BF16), k, preferred_element_type=F32))
                dos_ref[:, hs] = _with_stat(do16, delta, _delta_lane(hh))
            dq = jnp.concatenate(dqs, axis=1)
            dq_ref[...] = _rope_tiles(dq, c_ref[...], sn_ref[...], sp_ref[...], True).astype(BF16)

        pl.when(qi < ncb)(lambda: rows(n_ctx))
        pl.when(qi >= ncb)(lambda: rows(T))

    tab = pl.BlockSpec((tq, HEAD_LANES), lambda h, i: (i, 0))
    return pl.pallas_call(
        body,
        out_shape=[jax.ShapeDtypeStruct((T, MLA_HEADS * HEAD_LANES), BF16)] * 2,
        grid=(N_PAIRS, nq),
        in_specs=[pl.BlockSpec((tq, 2 * HEAD_LANES), lambda h, i: (i, h)),
                  pl.BlockSpec((T, 2 * HEAD_LANES), lambda h, i: (0, h)),
                  pl.BlockSpec((T, 2 * V_HEAD_DIM), lambda h, i: (0, h)),
                  pl.BlockSpec((tq, 2 * V_HEAD_DIM), lambda h, i: (i, h)),
                  pl.BlockSpec((tq, 2 * V_HEAD_DIM), lambda h, i: (i, h)),
                  pl.BlockSpec((2, tq, 1), lambda h, i: (h, i, 0)), tab, tab, tab],
        out_specs=[pl.BlockSpec((tq, 2 * HEAD_LANES), lambda h, i: (i, h))] * 2,
        compiler_params=_params(("parallel", "parallel"), VMEM_LIMIT), name="attn_bwd_dq")(
            qb, kb, vb, o, do, lse, *tabs)


def attn_bwd(qb, kb, vb, o, do, lse, tabs, n_ctx):
    T = qb.shape[0]
    tq = math.gcd(ROW_TILE, n_ctx)
    nq, ncb = T // tq, n_ctx // tq
    tn = (((0,), (0,)), ((), ()))

    def body(q_ref, k_ref, v_ref, o_ref, do_ref, lse_ref, c_ref, sn_ref, sp_ref, dq_ref, dk_ref, dv_ref, dkt, dvt):
        qi = pl.program_id(1)

        @pl.when(qi == 0)
        def _():
            dkt[...] = jnp.zeros_like(dkt)
            dvt[...] = jnp.zeros_like(dvt)

        def rows(n_keys):
            v = v_ref[:n_keys, :]
            dqs = []
            for hh in range(2):
                hs = slice(hh * HEAD_LANES, (hh + 1) * HEAD_LANES)
                k = k_ref[:n_keys, hs]
                q = q_ref[:, hs]
                do = jnp.where(_own_lanes(do_ref.shape, hh), do_ref[...], 0.0)
                delta = jnp.sum(do * o_ref[...], axis=-1, keepdims=True)
                s = lax.dot_general(q, k, NT_DIMS, preferred_element_type=F32) * SOFTMAX_SCALE
                p = jnp.exp(s - lse_ref[hh])
                do16 = do.astype(BF16)
                dp = lax.dot_general(do16, v, NT_DIMS, preferred_element_type=F32)
                ds16 = (p * (dp - delta)).astype(BF16)
                dqs.append(jnp.dot(ds16, k, preferred_element_type=F32) * SOFTMAX_SCALE)
                dkt[hs, :n_keys] += lax.dot_general(q, ds16, tn, preferred_element_type=F32) * SOFTMAX_SCALE
                dvt[:, :n_keys] += lax.dot_general(do16, p.astype(BF16), tn, preferred_element_type=F32)
            dq = jnp.concatenate(dqs, axis=1)
            dq_ref[...] = _rope_tiles(dq, c_ref[...], sn_ref[...], sp_ref[...], True).astype(BF16)

        pl.when(qi < ncb)(lambda: rows(n_ctx))
        pl.when(qi >= ncb)(lambda: rows(T))

        @pl.when(qi == nq - 1)
        def _():
            dk_ref[...] = dkt[...].T
            dv_ref[...] = dvt[...].T

    tab = pl.BlockSpec((tq, HEAD_LANES), lambda h, i: (i, 0))
    return pl.pallas_call(
        body,
        out_shape=[jax.ShapeDtypeStruct((T, MLA_HEADS * HEAD_LANES), BF16),
                   jax.ShapeDtypeStruct((T, MLA_HEADS * HEAD_LANES), F32),
                   jax.ShapeDtypeStruct((T, MLA_HEADS * V_HEAD_DIM), F32)],
        grid=(N_PAIRS, nq),
        in_specs=[pl.BlockSpec((tq, 2 * HEAD_LANES), lambda h, i: (i, h)),
                  pl.BlockSpec((T, 2 * HEAD_LANES), lambda h, i: (0, h)),
                  pl.BlockSpec((T, 2 * V_HEAD_DIM), lambda h, i: (0, h)),
                  pl.BlockSpec((tq, 2 * V_HEAD_DIM), lambda h, i: (i, h)),
                  pl.BlockSpec((tq, 2 * V_HEAD_DIM), lambda h, i: (i, h)),
                  pl.BlockSpec((2, tq, 1), lambda h, i: (h, i, 0)), tab, tab, tab],
        out_specs=[pl.BlockSpec((tq, 2 * HEAD_LANES), lambda h, i: (i, h)),
                   pl.BlockSpec((T, 2 * HEAD_LANES), lambda h, i: (0, h)),
                   pl.BlockSpec((T, 2 * V_HEAD_DIM), lambda h, i: (0, h))],
        scratch_shapes=[pltpu.VMEM((2 * HEAD_LANES, T), F32), pltpu.VMEM((2 * V_HEAD_DIM, T), F32)],
        compiler_params=_params(("parallel", "arbitrary"), VMEM_LIMIT), name="attn_bwd")(
            qb, kb, vb, o, do, lse, *tabs)


def attn_bwd_dkv(qs, kb, vb, dos, n_ctx):
    T = qs.shape[0]
    tq = math.gcd(ROW_TILE, n_ctx)
    nq, ncb = T // tq, n_ctx // tq

    def body(q_ref, do_ref, k_ref, v_ref, dk_ref, dv_ref):
        kj = pl.program_id(1)

        def cols(first):
            v = v_ref[...]
            lane = lax.broadcasted_iota(jnp.int32, v.shape, 1)
            dvs = []
            for hh in range(2):
                hs = slice(hh * HEAD_LANES, (hh + 1) * HEAD_LANES)
                q = q_ref[first:, hs]
                do16 = do_ref[first:, hs]
                in_delta = (lane >= _delta_lane(hh)) & (lane < _delta_lane(hh) + 3)
                v_minus = jnp.where(in_delta, -jnp.ones_like(v), v)
                pt = jnp.exp(lax.dot_general(k_ref[:, hs], q, NT_DIMS, preferred_element_type=F32) * SOFTMAX_SCALE)
                dvs.append(jnp.dot(pt.astype(BF16), do16, preferred_element_type=F32))
                dst = pt * lax.dot_general(v_minus, do16, NT_DIMS, preferred_element_type=F32) * SOFTMAX_SCALE
                dk_ref[:, hs] = jnp.dot(dst.astype(BF16), q, preferred_element_type=F32)
            dv_ref[...] = jnp.where(_own_lanes(dvs[0].shape, 0), dvs[0], dvs[1])

        pl.when(kj < ncb)(lambda: cols(0))
        pl.when(kj >= ncb)(lambda: cols(n_ctx))

    return pl.pallas_call(
        body,
        out_shape=[jax.ShapeDtypeStruct((T, MLA_HEADS * HEAD_LANES), F32),
                   jax.ShapeDtypeStruct((T, MLA_HEADS * V_HEAD_DIM), F32)],
        grid=(N_PAIRS, nq),
        in_specs=[pl.BlockSpec((T, 2 * HEAD_LANES), lambda h, j: (0, h)),
                  pl.BlockSpec((T, 2 * HEAD_LANES), lambda h, j: (0, h)),
                  pl.BlockSpec((tq, 2 * HEAD_LANES), lambda h, j: (j, h)),
                  pl.BlockSpec((tq, 2 * V_HEAD_DIM), lambda h, j: (j, h))],
        out_specs=[pl.BlockSpec((tq, 2 * HEAD_LANES), lambda h, j: (j, h)),
                   pl.BlockSpec((tq, 2 * V_HEAD_DIM), lambda h, j: (j, h))],
        compiler_params=_params(("parallel", "parallel"), VMEM_LIMIT), name="attn_bwd_dkv")(
            qs, dos, kb, vb)


def _split_bf16(x):
    hi = x.astype(BF16)
    return hi, (x - hi.astype(F32)).astype(BF16)


def q_heads(cq, gain, w_uq_p, tabs, name="l0_uq"):
    T, K = cq.arr.shape[0], cq.width
    N = w_uq_p.shape[1]
    tm = math.gcd(ROW_TILE, T)

    def body(a_ref, g_ref, w_ref, c_ref, sn_ref, sp_ref, o_ref, n_ref):
        qn = f_rms(a_ref[...], g_ref[0])[0].astype(BF16)
        n_ref[...] = qn
        acc = jnp.dot(qn, w_ref[...], preferred_element_type=F32)
        o_ref[...] = _rope_tiles(acc, c_ref[...], sn_ref[...], sp_ref[...], False).astype(BF16)

    tab = pl.BlockSpec((tm, HEAD_LANES), lambda i: (i, 0))
    return pl.pallas_call(
        body, out_shape=[jax.ShapeDtypeStruct((T, N), BF16), jax.ShapeDtypeStruct((T, K), BF16)], grid=(T // tm,),
        in_specs=[cq.spec(tm), pl.BlockSpec((1, 1, K), lambda i: (0, 0, 0)), pl.BlockSpec((K, N), lambda i: (0, 0)),
                  tab, tab, tab],
        out_specs=[pl.BlockSpec((tm, N), lambda i: (i, 0)), pl.BlockSpec((tm, K), lambda i: (i, 0))],
        compiler_params=_params(("parallel",), VMEM_LIMIT), name=name)(cq.arr, gain, w_uq_p, *tabs)


def kv_heads(ckv, gain, w_kn_p, w_v, kr, spread, tabs, name="l0_ukv"):
    T, K = ckv.arr.shape[0], ckv.width
    N = w_kn_p.shape[1]
    NV = w_v.shape[1]
    tm = math.gcd(ROW_TILE, T)

    def body(a_ref, g_ref, wk_ref, wv_ref, kr_ref, e_ref, c_ref, sn_ref, sp_ref, k_ref, v_ref, n_ref):
        a = f_rms(a_ref[...], g_ref[0])[0].astype(BF16)
        n_ref[...] = a
        hi, lo = _split_bf16(kr_ref[...])
        acc = (jnp.dot(a, wk_ref[...], preferred_element_type=F32)
               + jnp.dot(hi, e_ref[...], preferred_element_type=F32)
               + jnp.dot(lo, e_ref[...], preferred_element_type=F32))
        roped = _rope_tiles(acc, c_ref[...], sn_ref[...], sp_ref[...], False)
        lane = lax.broadcasted_iota(jnp.int32, roped.shape, 1) % HEAD_LANES
        k_ref[...] = jnp.where((lane >= STAT_LANE) & (lane < STAT_LANE + 3), 1.0, roped).astype(BF16)
        v_ref[...] = jnp.dot(a, wv_ref[...], preferred_element_type=F32).astype(BF16)

    tab = pl.BlockSpec((tm, HEAD_LANES), lambda i: (i, 0))
    return pl.pallas_call(
        body, out_shape=[jax.ShapeDtypeStruct((T, N), BF16), jax.ShapeDtypeStruct((T, NV), BF16),
                         jax.ShapeDtypeStruct((T, K), BF16)], grid=(T // tm,),
        in_specs=[ckv.spec(tm), pl.BlockSpec((1, 1, K), lambda i: (0, 0, 0)), pl.BlockSpec((K, N), lambda i: (0, 0)),
                  pl.BlockSpec((K, NV), lambda i: (0, 0)), kr.spec(tm),
                  pl.BlockSpec((kr.width, N), lambda i: (0, 0)), tab, tab, tab],
        out_specs=[pl.BlockSpec((tm, N), lambda i: (i, 0)), pl.BlockSpec((tm, NV), lambda i: (i, 0)),
                   pl.BlockSpec((tm, K), lambda i: (i, 0))],
        compiler_params=_params(("parallel",), VMEM_LIMIT), name=name)(
            ckv.arr, gain, w_kn_p, w_v, kr.arr, spread, *tabs)


def heads_unrope(d, tabs, spread=None, name="unrope"):
    T, N = d.shape
    tm = math.gcd(ROW_TILE, T)

    def body(*refs):
        if spread is None:
            d_ref, c_ref, sn_ref, sp_ref, o_ref = refs
        else:
            d_ref, c_ref, sn_ref, sp_ref, e_ref, o_ref, kr_ref = refs
        g = _rope_tiles(d_ref[...], c_ref[...], sn_ref[...], sp_ref[...], True)
        o_ref[...] = g.astype(BF16)
        if spread is not None:
            hi, lo = _split_bf16(g)
            kr_ref[...] = (lax.dot_general(hi, e_ref[...], NT_DIMS, preferred_element_type=F32)
                           + lax.dot_general(lo, e_ref[...], NT_DIMS, preferred_element_type=F32))

    tab = pl.BlockSpec((tm, HEAD_LANES), lambda i: (i, 0))
    row = pl.BlockSpec((tm, N), lambda i: (i, 0))
    ins, in_specs = [d, *tabs], [row, tab, tab, tab]
    out_shape, out_specs = [jax.ShapeDtypeStruct((T, N), BF16)], [row]
    if spread is not None:
        ins.append(spread)
        in_specs.append(pl.BlockSpec(spread.shape, lambda i: (0, 0)))
        out_shape.append(jax.ShapeDtypeStruct((T, spread.shape[0]), F32))
        out_specs.append(pl.BlockSpec((tm, spread.shape[0]), lambda i: (i, 0)))
    return pl.pallas_call(
        body, out_shape=out_shape, grid=(T // tm,), in_specs=in_specs, out_specs=out_specs,
        compiler_params=_params(("parallel",), VMEM_LIMIT), name=name)(*ins)


def _cmul(ar, ai, br, bi):
    return ar * br - ai * bi, ar * bi + ai * br


def s5_chain(finals, s0, a, n_steps, reverse, name):
    W = finals.shape[-1]
    first = N_SEG - 1 if reverse else 0

    def body(f_ref, s0_ref, a_ref, c_ref):
        pr, pi = jnp.ones((1, W), F32), jnp.zeros((1, W), F32)
        br, bi = a_ref[0], a_ref[1]
        n = n_steps
        while n:
            if n & 1:
                pr, pi = _cmul(pr, pi, br, bi)
            br, bi = _cmul(br, bi, br, bi)
            n >>= 1
        fr, fi = f_ref[0], f_ref[1]
        row = lax.broadcasted_iota(jnp.int32, (N_SEG, W), 0)
        s0r = jnp.broadcast_to(s0_ref[0], (N_SEG, W))
        s0i = jnp.broadcast_to(s0_ref[1], (N_SEG, W))
        cr = jnp.where(row == first, s0r, 0.0)
        ci = jnp.where(row == first, s0i, 0.0)
        shift = N_SEG - 1 if reverse else 1
        for _ in range(N_SEG - 1):
            mr, mi = _cmul(pr, pi, cr, ci)
            tr = pltpu.roll(fr + mr, shift, 0)
            ti = pltpu.roll(fi + mi, shift, 0)
            cr = jnp.where(row == first, s0r, tr)
            ci = jnp.where(row == first, s0i, ti)
        c_ref[0] = cr
        c_ref[1] = ci

    return pl.pallas_call(body, out_shape=jax.ShapeDtypeStruct((2, N_SEG, W), F32), name=name)(finals, s0, a)


def _scan_chunk(bur, bui, st_ref, a_ref, n_steps, reverse):
    for lc in range(S5_LANES // BLK_ST):
        sl = slice(lc * BLK_ST, (lc + 1) * BLK_ST)
        lr = jnp.broadcast_to(a_ref[0, :, sl], (N_SEG, BLK_ST))
        li = jnp.broadcast_to(a_ref[1, :, sl], (N_SEG, BLK_ST))

        def step(jj, carry, sl=sl, lr=lr, li=li):
            sr, si = carry
            j = (n_steps - 1 - jj) if reverse else jj
            r0 = pl.multiple_of(j * N_SEG, N_SEG)
            nr = lr * sr - li * si + bur[pl.ds(r0, N_SEG), sl]
            ni = lr * si + li * sr + bui[pl.ds(r0, N_SEG), sl]
            bur[pl.ds(r0, N_SEG), sl] = nr
            bui[pl.ds(r0, N_SEG), sl] = ni
            return nr, ni

        sr, si = lax.fori_loop(0, n_steps, step, (st_ref[0, :, sl], st_ref[1, :, sl]))
        st_ref[0, :, sl] = sr
        st_ref[1, :, sl] = si


def _project_in(x16, w_re, w_im, bur, bui, adjoint):
    for gb in range(N_BLOCKS):
        xb = x16[:, gb * BLK_CH:(gb + 1) * BLK_CH]
        sl = slice(gb * BLK_ST, (gb + 1) * BLK_ST)
        if adjoint:
            dn = (((1,), (1,)), ((), ()))
            bur[:, sl] = lax.dot_general(xb, w_re[gb], dn, preferred_element_type=F32)
            bui[:, sl] = -lax.dot_general(xb, w_im[gb], dn, preferred_element_type=F32)
        else:
            bur[:, sl] = jnp.dot(xb, w_re[gb], preferred_element_type=F32)
            bui[:, sl] = jnp.dot(xb, w_im[gb], preferred_element_type=F32)


def s5_scan(act, w_re, w_im, a, init, *, reverse, adjoint=False, c_re=None, c_im=None, add=None,
            want_ckpt=False, rows=None, name):
    act_off, N = rows if rows is not None else (0, act.shape[0])
    R = math.gcd(ROW_TILE, N, act_off)
    nch, jc = N // R, R // N_SEG
    with_out = c_re is not None

    def chunk(i):
        return (nch - 1 - i) if reverse else i

    def body(*refs):
        act_ref, wre_ref, wim_ref, a_ref, init_ref = refs[:5]
        k = 5
        if with_out:
            cre_ref, cim_ref = refs[k:k + 2]
            k += 2
        if add is not None:
            add_ref = refs[k]
            k += 1
        if with_out:
            out_ref = refs[k]
            k += 1
        if want_ckpt:
            ck_ref = refs[k]
            k += 1
        fin_ref, bur, bui = refs[k:k + 3]

        @pl.when(pl.program_id(0) == 0)
        def _():
            fin_ref[...] = init_ref[...]

        if want_ckpt:
            ck_ref[0] = fin_ref[...]
        _project_in(act_ref[...].astype(BF16), wre_ref, wim_ref, bur, bui, adjoint)
        _scan_chunk(bur, bui, fin_ref, a_ref, jc, reverse)
        if with_out:
            for gb in range(N_BLOCKS):
                sl = slice(gb * BLK_ST, (gb + 1) * BLK_ST)
                y = (jnp.dot(bur[:, sl].astype(BF16), cre_ref[gb], preferred_element_type=F32)
                     - jnp.dot(bui[:, sl].astype(BF16), cim_ref[gb], preferred_element_type=F32))
                cs = slice(gb * BLK_CH, (gb + 1) * BLK_CH)
                if add is not None:
                    y = y + add_ref[:, cs]
                out_ref[:, cs] = y

    row_spec = pl.BlockSpec((R, D_MODEL), lambda i: (chunk(i), 0))
    act_spec = pl.BlockSpec((R, D_MODEL), lambda i: (chunk(i) + act_off // R, 0))
    w_spec = pl.BlockSpec(w_re.shape, lambda i: (0, 0, 0))
    st_spec = pl.BlockSpec((2, N_SEG, S5_LANES), lambda i: (0, 0, 0))
    ins = [act, w_re, w_im, a, init]
    in_specs = [act_spec, w_spec, w_spec, pl.BlockSpec((2, 1, S5_LANES), lambda i: (0, 0, 0)), st_spec]
    if with_out:
        ins += [c_re, c_im]
        in_specs += [pl.BlockSpec(c_re.shape, lambda i: (0, 0, 0))] * 2
    if add is not None:
        ins.append(add)
        in_specs.append(row_spec)
    out_shape, out_specs = [], []
    if with_out:
        out_shape.append(jax.ShapeDtypeStruct((N, D_MODEL), F32))
        out_specs.append(row_spec)
    if want_ckpt:
        out_shape.append(jax.ShapeDtypeStruct((nch, 2, N_SEG, S5_LANES), F32))
        out_specs.append(pl.BlockSpec((1, 2, N_SEG, S5_LANES), lambda i: (chunk(i), 0, 0, 0)))
    out_shape.append(jax.ShapeDtypeStruct((2, N_SEG, S5_LANES), F32))
    out_specs.append(st_spec)
    res = pl.pallas_call(
        body, out_shape=out_shape, grid=(nch,), in_specs=in_specs, out_specs=out_specs,
        scratch_shapes=[pltpu.VMEM((R, S5_LANES), F32), pltpu.VMEM((R, S5_LANES), F32)],
        compiler_params=_params(("arbitrary",), VMEM_LIMIT), name=name)(*ins)
    res = list(res)
    out = res.pop(0) if with_out else None
    ckpt = res.pop(0) if want_ckpt else None
    return out, ckpt, res[0]


def s5_grads(dy, u, ckpt, b_re, b_im, c_re, c_im, lam, init_adj, *, reverse, add=None, u_off=0, du_rows=None,
             du_into=None, n_rows=None, name):
    N = n_rows if dy is None else dy.shape[0]
    du_total, du_first = du_rows if du_rows is not None else (N, 0)
    R = math.gcd(ROW_TILE, N, u_off, du_first)
    nch, jc = N // R, R // N_SEG
    W = S5_LANES

    def chunk(i):
        return i if reverse else (nch - 1 - i)

    def body(*refs):
        if dy is None:
            refs = (None,) + refs[:4] + (None, None) + refs[4:]
        dy_ref, u_ref, ck_ref, bre_ref, bim_ref, cre_ref, cim_ref, lam_ref, init_ref = refs[:9]
        k = 9
        if add is not None:
            add_ref = refs[k]
            k += 1
        k += du_into is not None
        outs = refs[k:]
        if dy is None:
            outs = outs[:4] + (None, None) + outs[4:]
        du_ref, dlam_ref, dbre_ref, dbim_ref, dcre_ref, dcim_ref, fin_ref = outs[:7]
        sr_buf, si_buf, er_buf, ei_buf, st_buf = outs[7:12]

        @pl.when(pl.program_id(0) == 0)
        def _():
            fin_ref[...] = init_ref[...]
            dlam_ref[...] = jnp.zeros_like(dlam_ref)
            dbre_ref[...] = jnp.zeros_like(dbre_ref)
            dbim_ref[...] = jnp.zeros_like(dbim_ref)
            if dy is not None:
                dcre_ref[...] = jnp.zeros_like(dcre_ref)
                dcim_ref[...] = jnp.zeros_like(dcim_ref)

        u16 = u_ref[...].astype(BF16)
        st_buf[...] = ck_ref[0]
        _project_in(u16, bre_ref, bim_ref, sr_buf, si_buf, False)
        _scan_chunk(sr_buf, si_buf, st_buf, lam_ref, jc, reverse)
        if dy is None:
            er_buf[...] = jnp.zeros_like(er_buf)
            ei_buf[...] = jnp.zeros_like(ei_buf)
        else:
            dy16 = dy_ref[...].astype(BF16)
            _project_in(dy16, cre_ref, cim_ref, er_buf, ei_buf, True)
        for lc in range(W // BLK_ST):
            sl = slice(lc * BLK_ST, (lc + 1) * BLK_ST)
            lr = jnp.broadcast_to(lam_ref[0, :, sl], (N_SEG, BLK_ST))
            li = jnp.broadcast_to(lam_ref[1, :, sl], (N_SEG, BLK_ST))

            def one(r0, spr, spi, carry, sl=sl, lr=lr, li=li):
                gr, gi, ar, ai = carry
                nr = er_buf[pl.ds(r0, N_SEG), sl] + lr * gr + li * gi
                ni = ei_buf[pl.ds(r0, N_SEG), sl] + lr * gi - li * gr
                er_buf[pl.ds(r0, N_SEG), sl] = nr
                ei_buf[pl.ds(r0, N_SEG), sl] = ni
                return nr, ni, ar + spr * nr + spi * ni, ai + spr * ni - spi * nr

            def step(ff, carry, sl=sl, one=one):
                f = jc - 1 - ff
                j = (jc - 1 - f) if reverse else f
                jp = (j + 1) if reverse else (j - 1)
                r0 = pl.multiple_of(j * N_SEG, N_SEG)
                p0 = pl.multiple_of(jp * N_SEG, N_SEG)
                return one(r0, sr_buf[pl.ds(p0, N_SEG), sl], si_buf[pl.ds(p0, N_SEG), sl], carry)

            carry = (fin_ref[0, :, sl], fin_ref[1, :, sl], dlam_ref[0, :, sl], dlam_ref[1, :, sl])
            carry = lax.fori_loop(0, jc - 1, step, carry)
            r_first = (jc - 1) * N_SEG if reverse else 0
            gr, gi, ar, ai = one(r_first, ck_ref[0, 0, :, sl], ck_ref[0, 1, :, sl], carry)
            fin_ref[0, :, sl] = gr
            fin_ref[1, :, sl] = gi
            dlam_ref[0, :, sl] = ar
            dlam_ref[1, :, sl] = ai
        tn = (((0,), (0,)), ((), ()))
        nt = (((1,), (1,)), ((), ()))
        for gb in range(N_BLOCKS):
            sl = slice(gb * BLK_ST, (gb + 1) * BLK_ST)
            cs = slice(gb * BLK_CH, (gb + 1) * BLK_CH)
            gr16 = er_buf[:, sl].astype(BF16)
            gi16 = ei_buf[:, sl].astype(BF16)
            du = (lax.dot_general(gr16, bre_ref[gb], nt, preferred_element_type=F32)
                  + lax.dot_general(gi16, bim_ref[gb], nt, preferred_element_type=F32))
            if add is not None:
                du = du + add_ref[:, cs]
            du_ref[:, cs] = du
            ub = u16[:, cs]
            dbre_ref[gb] += lax.dot_general(ub, gr16, tn, preferred_element_type=F32)
            dbim_ref[gb] += lax.dot_general(ub, gi16, tn, preferred_element_type=F32)
            if dy is not None:
                dyb = dy16[:, cs]
                dcre_ref[gb] += lax.dot_general(sr_buf[:, sl].astype(BF16), dyb, tn, preferred_element_type=F32)
                dcim_ref[gb] -= lax.dot_general(si_buf[:, sl].astype(BF16), dyb, tn, preferred_element_type=F32)

    row_spec = pl.BlockSpec((R, D_MODEL), lambda i: (chunk(i), 0))
    st_spec = pl.BlockSpec((2, N_SEG, W), lambda i: (0, 0, 0))
    wb_spec = pl.BlockSpec(b_re.shape, lambda i: (0, 0, 0))
    wc_spec = pl.BlockSpec(c_re.shape, lambda i: (0, 0, 0))
    ins = [dy, u, ckpt, b_re, b_im, c_re, c_im, lam, init_adj]
    u_spec = pl.BlockSpec((R, D_MODEL), lambda i: (chunk(i) + u_off // R, 0))
    in_specs = [row_spec, u_spec, pl.BlockSpec((1, 2, N_SEG, W), lambda i: (chunk(i), 0, 0, 0)),
                wb_spec, wb_spec, wc_spec, wc_spec, pl.BlockSpec((2, 1, W), lambda i: (0, 0, 0)), st_spec]
    if dy is None:
        ins, in_specs = ins[1:5] + ins[7:], in_specs[1:5] + in_specs[7:]
    if add is not None:
        ins.append(add)
        in_specs.append(row_spec)
    aliases = {}
    if du_into is not None:
        aliases[len(ins)] = 0
        ins.append(du_into)
        in_specs.append(pl.BlockSpec(memory_space=pl.ANY))
    du_spec = pl.BlockSpec((R, D_MODEL), lambda i: (chunk(i) + du_first // R, 0))
    out_shape = [jax.ShapeDtypeStruct((du_total, D_MODEL), F32), jax.ShapeDtypeStruct((2, N_SEG, W), F32),
                 jax.ShapeDtypeStruct(b_re.shape, F32), jax.ShapeDtypeStruct(b_re.shape, F32),
                 jax.ShapeDtypeStruct(c_re.shape, F32), jax.ShapeDtypeStruct(c_re.shape, F32),
                 jax.ShapeDtypeStruct((2, N_SEG, W), F32)]
    out_specs = [du_spec, st_spec, wb_spec, wb_spec, wc_spec, wc_spec, st_spec]
    if dy is None:
        out_shape, out_specs = out_shape[:4] + out_shape[6:], out_specs[:4] + out_specs[6:]
    res = pl.pallas_call(
        body, out_shape=out_shape, grid=(nch,), in_specs=in_specs, out_specs=out_specs, input_output_aliases=aliases,
        scratch_shapes=[pltpu.VMEM((R, W), F32) for _ in range(4)] + [pltpu.VMEM((2, N_SEG, W), F32)],
        compiler_params=_params(("arbitrary",), VMEM_LIMIT), name=name)(*ins)
    return res if dy is not None else [*res[:4], None, None, res[4]]


def adamw(w, g, m, v, name="adamw", after=None):
    n, d = w.shape
    lanes = -(-d // 128) * 128
    tm = n
    while tm * lanes * 4 > (1 << 20) and tm % 16 == 0:
        tm //= 2
    c1 = 1.0 - ADAM_B1 ** ADAM_STEP
    c2 = 1.0 - ADAM_B2 ** ADAM_STEP

    def body(w_ref, g_ref, m_ref, v_ref, *rest):
        d_ref, nm_ref, nv_ref = rest[-3:]
        g_ = g_ref[...]
        m_ = ADAM_B1 * m_ref[...] + (1.0 - ADAM_B1) * g_
        v_ = ADAM_B2 * v_ref[...] + (1.0 - ADAM_B2) * (g_ * g_)
        d_ref[...] = -ADAM_LR * ((m_ / c1) / (jnp.sqrt(v_ / c2) + ADAM_EPS) + ADAM_WD * w_ref[...])
        nm_ref[...] = m_
        nv_ref[...] = v_

    spec = pl.BlockSpec((tm, d), lambda i: (i, 0))
    extra = [] if after is None else [after]
    return pl.pallas_call(
        body, out_shape=[jax.ShapeDtypeStruct((n, d), F32)] * 3, grid=(n // tm,),
        in_specs=[spec] * 4 + [pl.BlockSpec(memory_space=pl.ANY)] * len(extra), out_specs=[spec] * 3,
        compiler_params=_params(("parallel",), VMEM_LIMIT), name=name)(w, g, m, v, *extra)


def _coords():
    return lax.axis_index("x"), lax.axis_index("y"), lax.axis_index("c")


def exchange(arrays, out_shapes, remote, local, name, aliases=None):
    n_in, n_out, n_rem, n_loc = len(arrays), len(out_shapes), len(remote), len(local)

    def at(ref, idx):
        return ref if idx is None else ref.at[idx]

    def body(*refs):
        ins, outs = refs[:n_in], refs[n_in:n_in + n_out]
        send_sems, recv_sems, local_sems = refs[n_in + n_out:]
        me = _coords()
        sends, recvs = [], []
        for k, (flip, ii, src_at, oi, dst_at) in enumerate(remote):
            peer = (me[0] ^ flip[0], me[1] ^ flip[1], me[2] ^ flip[2])
            src = at(ins[ii], src_at(me, peer))
            sends.append(pltpu.make_async_remote_copy(
                src_ref=src, dst_ref=at(outs[oi], dst_at(me)), send_sem=send_sems.at[k], recv_sem=recv_sems.at[k],
                device_id=peer, device_id_type=MESH))
            recvs.append(pltpu.make_async_remote_copy(
                src_ref=src, dst_ref=at(outs[oi], dst_at(peer)), send_sem=send_sems.at[k], recv_sem=recv_sems.at[k],
                device_id=peer, device_id_type=MESH))
        locs = [pltpu.make_async_copy(at(ins[ii], src_at(me)), at(outs[oi], dst_at(me)), local_sems.at[k])
                for k, (ii, src_at, oi, dst_at) in enumerate(local)]
        for cp in locs + sends:
            cp.start()
        for cp in recvs:
            cp.wait_recv()
        for cp in sends:
            cp.wait_send()
        for cp in locs:
            cp.wait()

    hbm = pl.BlockSpec(memory_space=pl.ANY)
    return pl.pallas_call(
        body, out_shape=list(out_shapes), in_specs=[hbm] * n_in, out_specs=[hbm] * n_out,
        scratch_shapes=[pltpu.SemaphoreType.DMA((n_rem,)), pltpu.SemaphoreType.DMA((n_rem,)),
                        pltpu.SemaphoreType.DMA((max(n_loc, 1),))],
        input_output_aliases=aliases or {}, name=name)(*arrays)


ALL_FLIPS = [(dx, dy, dc) for dx in (0, 1) for dy in (0, 1) for dc in (0, 1)][1:]
CHIP_FLIPS = [(1, 0, 0), (0, 1, 0), (1, 1, 0)]
CORE_FLIP = (0, 0, 1)


def _dev_index(p):
    return 4 * p[0] + 2 * p[1] + p[2]


def _chip_index(p):
    return 2 * p[0] + p[1]


def _gather(xs, flips, index, n, name):
    arrays = [x[None] for x in xs]
    outs = [jax.ShapeDtypeStruct((n,) + x.shape, x.dtype) for x in xs]
    remote = [(f, a, lambda me, peer: (0,), a, lambda s: (index(s),)) for a in range(len(xs)) for f in flips]
    local = [(a, lambda me: (0,), a, lambda me: (index(me),)) for a in range(len(xs))]
    return exchange(arrays, outs, remote, local, name)


def allgather_devices(x, name):
    return _gather([x], ALL_FLIPS, _dev_index, N_DEV, name)[0]


def allgather_chips(xs, name):
    return _gather(xs, CHIP_FLIPS, _chip_index, N_CHIP, name)


def gather_halves(xs, name):
    n = len(xs)
    nk = n * len(CHIP_FLIPS)

    def body(*refs):
        ins, outs = refs[:n], refs[n:2 * n]
        ici_send, ici_recv, d2d_send, d2d_recv = refs[2 * n:]
        me = _coords()
        sibling = (me[0], me[1], 1 - me[2])
        first, passed, landed = [], [], []
        for a in range(n):
            half = ins[a].shape[0] // 2
            mine = ins[a].at[pl.ds(pl.multiple_of(me[2] * half, 16), half)]
            for j, flip in enumerate(CHIP_FLIPS):
                k = a * len(CHIP_FLIPS) + j
                peer = (me[0] ^ flip[0], me[1] ^ flip[1], me[2])
                first.append(pltpu.make_async_remote_copy(
                    src_ref=mine, dst_ref=outs[a].at[_chip_index(me), me[2]], send_sem=ici_send.at[k],
                    recv_sem=ici_recv.at[k], device_id=peer, device_id_type=MESH))
                arrived = outs[a].at[_chip_index(peer), me[2]]
                landed.append(pltpu.make_async_remote_copy(
                    src_ref=mine, dst_ref=arrived, send_sem=ici_send.at[k], recv_sem=ici_recv.at[k],
                    device_id=peer, device_id_type=MESH))
                passed.append(pltpu.make_async_remote_copy(
                    src_ref=arrived, dst_ref=arrived, send_sem=d2d_send.at[k], recv_sem=d2d_recv.at[k],
                    device_id=sibling, device_id_type=MESH))
        for cp in first:
            cp.start()
        for k in range(nk):
            landed[k].wait_recv()
            passed[k].start()
        for a in range(n):
            for j, flip in enumerate(CHIP_FLIPS):
                k = a * len(CHIP_FLIPS) + j
                peer_chip = _chip_index((me[0] ^ flip[0], me[1] ^ flip[1]))
                from_sibling = outs[a].at[peer_chip, 1 - me[2]]
                pltpu.make_async_remote_copy(
                    src_ref=from_sibling, dst_ref=from_sibling, send_sem=d2d_send.at[k], recv_sem=d2d_recv.at[k],
                    device_id=sibling, device_id_type=MESH).wait_recv()
        for cp in first + passed:
            cp.wait_send()

    hbm = pl.BlockSpec(memory_space=pl.ANY)
    return pl.pallas_call(
        body, out_shape=[jax.ShapeDtypeStruct((N_CHIP, 2, x.shape[0] // 2, x.shape[1]), x.dtype) for x in xs],
        in_specs=[hbm] * n, out_specs=[hbm] * n,
        scratch_shapes=[pltpu.SemaphoreType.DMA((nk,)) for _ in range(4)], name=name)(*xs)


HBM_SPEC = pl.BlockSpec(memory_space=pltpu.HBM)
SEM_SPEC = pl.BlockSpec(memory_space=pltpu.SEMAPHORE)
DATAFLOW = pltpu.SideEffectType.DATAFLOW_SIDE_EFFECTING


def _at(ref, idx):
    return ref if idx is None else ref.at[idx]


def _peer(me, flip):
    return (me[0] ^ flip[0], me[1] ^ flip[1], me[2] ^ flip[2])


def exchange_start(arrays, land_shapes, remote, name, after=None):
    n_in, n_out, nk = len(arrays), len(land_shapes), len(remote)
    after = list(after or [])
    n_after = len(after)

    def body(*refs):
        srcs, lands = refs[:n_in], refs[n_in:n_in + n_out]
        first_out = n_in + n_out + n_after
        send_sems, recv_sems, token = refs[first_out], refs[first_out + 1], refs[-1]
        me = _coords()
        for k, (flip, ii, src_at, oi, dst_at) in enumerate(remote):
            peer = _peer(me, flip)
            pltpu.make_async_remote_copy(
                src_ref=_at(srcs[ii], src_at(me, peer)), dst_ref=_at(lands[oi], dst_at(me)), send_sem=send_sems.at[k],
                recv_sem=recv_sems.at[k], device_id=peer, device_id_type=MESH).start()
        token[...] = jnp.zeros_like(token)

    lands = [lax.empty(s.shape, s.dtype) for s in land_shapes]
    bufs = list(arrays) + lands
    out = pl.pallas_call(
        body, name=name,
        out_shape=(pltpu.SemaphoreType.DMA((nk,)), pltpu.SemaphoreType.DMA((nk,)),
                   *[pltpu.HBM(b.shape, b.dtype) for b in bufs], jax.ShapeDtypeStruct((8, 128), F32)),
        in_specs=[HBM_SPEC] * len(bufs) + [pl.BlockSpec(memory_space=pl.ANY)] * n_after,
        out_specs=(SEM_SPEC, SEM_SPEC, *[HBM_SPEC] * len(bufs), pl.BlockSpec(memory_space=pltpu.VMEM)),
        input_output_aliases={a: 2 + a for a in range(len(bufs))},
        compiler_params=pltpu.CompilerParams(has_side_effects=DATAFLOW),
    )(*[pltpu.with_memory_space_constraint(b, pltpu.HBM) for b in bufs], *after)
    flight = (out[0], out[1], list(out[2:2 + n_in]), list(out[2 + n_in:2 + n_in + n_out]), remote)
    return flight, out[-1]


def exchange_wait(flight, after, name):
    send_sems, recv_sems, arrays, lands, remote = flight
    n_in, n_out = len(arrays), len(lands)
    after = list(after) if isinstance(after, (list, tuple)) else [after]

    def body(*refs):
        srcs, lnds = refs[:n_in], refs[n_in:n_in + n_out]
        s_sems, r_sems = refs[n_in + n_out], refs[n_in + n_out + 1]
        me = _coords()
        for k, (flip, ii, src_at, oi, dst_at) in enumerate(remote):
            peer = _peer(me, flip)
            copy = pltpu.make_async_remote_copy(
                src_ref=_at(srcs[ii], src_at(me, peer)), dst_ref=_at(lnds[oi], dst_at(peer)), send_sem=s_sems.at[k],
                recv_sem=r_sems.at[k], device_id=peer, device_id_type=MESH)
            copy.wait_send()
            copy.wait_recv()

    bufs = list(arrays) + list(lands)
    out = pl.pallas_call(
        body, name=name,
        out_shape=tuple(pltpu.HBM(b.shape, b.dtype) for b in bufs),
        in_specs=[HBM_SPEC] * len(bufs) + [SEM_SPEC, SEM_SPEC] + [pl.BlockSpec(memory_space=pl.ANY)] * len(after),
        out_specs=tuple([HBM_SPEC] * len(bufs)),
        input_output_aliases={a: a for a in range(len(bufs))},
        compiler_params=pltpu.CompilerParams(has_side_effects=DATAFLOW),
    )(*bufs, send_sems, recv_sems, *after)
    return list(out[:n_in]), list(out[n_in:])


def _half_tile(h, cd):
    return h if h * cd * 4 <= (1 << 20) else math.gcd(512, h)


def pair_add(g, got, core, out_dtype, name):
    _, _, h, cd = g.shape
    th = _half_tile(h, cd)

    def body(c_ref, g_ref, got_ref, o_ref):
        o_ref[0] = (g_ref[0, 0] + got_ref[0]).astype(o_ref.dtype)

    return pl.pallas_call(
        body, out_shape=jax.ShapeDtypeStruct((N_CHIP, h, cd), out_dtype),
        grid_spec=pltpu.PrefetchScalarGridSpec(
            num_scalar_prefetch=1, grid=(N_CHIP, h // th),
            in_specs=[pl.BlockSpec((1, 1, th, cd), lambda q, i, c: (q, c[0], i, 0)),
                      pl.BlockSpec((1, th, cd), lambda q, i, c: (q, i, 0))],
            out_specs=pl.BlockSpec((1, th, cd), lambda q, i, c: (q, i, 0))),
        compiler_params=_params(("parallel", "parallel"), VMEM_LIMIT), name=name)(core, g, got)


def sum_chips(parts, sums, place, name):
    _, h, cd = parts.shape
    th = _half_tile(h, cd)

    def body(pc_ref, p_ref, own_ref, o_ref):
        acc = None
        for q in range(N_CHIP):
            term = jnp.where(pc_ref[1] == q, own_ref[0], p_ref[q]).astype(F32)
            acc = term if acc is None else acc + term
        o_ref[0] = acc

    return pl.pallas_call(
        body, out_shape=jax.ShapeDtypeStruct((2, h, cd), F32),
        grid_spec=pltpu.PrefetchScalarGridSpec(
            num_scalar_prefetch=1, grid=(h // th,),
            in_specs=[pl.BlockSpec((N_CHIP, th, cd), lambda i, pc: (0, i, 0)),
                      pl.BlockSpec((1, th, cd), lambda i, pc: (pc[1], i, 0))],
            out_specs=pl.BlockSpec((1, th, cd), lambda i, pc: (pc[0], i, 0))),
        compiler_params=_params(("parallel",), VMEM_LIMIT), name=name)(place, parts, sums)


def to_segments(a, n_ctx):
    def one(p):
        n = p.shape[0]
        return p.reshape(N_SEG, n // N_SEG, -1).transpose(1, 0, 2).reshape(n, -1)
    return jnp.concatenate([one(a[:n_ctx]), one(a[n_ctx:])], axis=0) if n_ctx else one(a)


def rows_to_segments(a, n_ctx, name):
    n, d = a.shape
    tj = n_ctx // N_SEG
    per_seg = (n - n_ctx) // N_SEG // tj
    assert n_ctx % N_SEG == 0 and (n - n_ctx) % (N_SEG * tj) == 0

    def body(*refs):
        out_ref, slabs = refs[N_SEG:]
        for c in range(d // HEAD_LANES):
            cols = slice(c * HEAD_LANES, (c + 1) * HEAD_LANES)
            for seg in range(N_SEG):
                slabs[c, pl.ds(seg, tj, stride=N_SEG), :] = refs[seg][:, cols]
            out_ref[:, cols] = slabs[c]

    def seg_spec(seg):
        return pl.BlockSpec((tj, d), lambda i: (jnp.where(i == 0, seg, N_SEG + seg * per_seg + i - 1), 0))

    return pl.pallas_call(
        body, out_shape=jax.ShapeDtypeStruct((n, d), a.dtype), grid=(1 + per_seg,),
        in_specs=[seg_spec(seg) for seg in range(N_SEG)], out_specs=pl.BlockSpec((N_SEG * tj, d), lambda i: (i, 0)),
        scratch_shapes=[pltpu.VMEM((d // HEAD_LANES, N_SEG * tj, HEAD_LANES), a.dtype)],
        compiler_params=_params(("parallel",), VMEM_LIMIT), name=name)(*[a] * N_SEG)


def rope_tables(n_ctx, n_lat):
    f32 = np.float32
    rows = n_lat // GRID_W
    row = np.repeat(np.arange(rows), GRID_W).astype(f32)
    col = np.tile(np.arange(GRID_W), rows).astype(f32)
    d = QK_ROPE_DIM // 2
    inv = (f32(1.0) / np.power(f32(ROPE_THETA), np.arange(0, d, 2, dtype=f32) / f32(d))).astype(f32)
    ang = np.concatenate([row[:, None] * inv[None, :], col[:, None] * inv[None, :]], axis=1).astype(f32)
    cos = np.concatenate([np.ones((n_ctx, d), f32), np.cos(ang)], axis=0)
    sin = np.concatenate([np.zeros((n_ctx, d), f32), np.sin(ang)], axis=0)
    q = QK_ROPE_DIM // 4
    T = n_ctx + n_lat
    ones, zeros = np.ones((T, QK_NOPE_DIM), f32), np.zeros((T, QK_NOPE_DIM), f32)
    tail, z8 = np.zeros((T, HEAD_LANES - QK_DIM), f32), np.zeros((T, q), f32)
    cr, cc, sr, sc = cos[:, :q], cos[:, q:], sin[:, :q], sin[:, q:]
    cos_t = np.concatenate([ones, cr, cr, cc, cc, tail], axis=1)
    sin_next = np.concatenate([zeros, -sr, z8, -sc, z8, tail], axis=1)
    sin_prev = np.concatenate([zeros, z8, sr, z8, sc, tail], axis=1)
    return tuple(jnp.asarray(t, F32) for t in (cos_t, sin_next, sin_prev))


def pad_heads(w, used):
    k = w.shape[0]
    return jnp.pad(w.reshape(k, MLA_HEADS, used), ((0, 0), (0, 0), (0, HEAD_LANES - used))).reshape(k, -1)


def unpad_heads(w, used):
    k = w.shape[0]
    return w.reshape(k, MLA_HEADS, HEAD_LANES)[:, :, :used].reshape(k, MLA_HEADS * used)


def rotary_spread():
    lane = np.arange(MLA_HEADS * HEAD_LANES) % HEAD_LANES
    return jnp.asarray(lane[None, :] == (QK_NOPE_DIM + np.arange(QK_ROPE_DIM))[:, None], BF16)


def s5_discretise(a_re, a_im, log_step, b_re, b_im):
    dt = jnp.exp(log_step)[:, None]
    mag = jnp.exp(a_re * dt)
    lb_re = mag * jnp.cos(a_im * dt)
    lb_im = mag * jnp.sin(a_im * dt)
    den = a_re * a_re + a_im * a_im
    nr = lb_re - 1.0
    f_re = ((nr * a_re + lb_im * a_im) / den)[:, None, :]
    f_im = ((lb_im * a_re - nr * a_im) / den)[:, None, :]
    return lb_re, lb_im, f_re * b_re - f_im * b_im, f_re * b_im + f_im * b_re


def s5_block_weights(lb_re, lb_im, bb_re, bb_im, c_re, c_im):
    eye = jnp.eye(GROUPS_PER_BLOCK, dtype=F32)
    lam = jnp.stack([lb_re.reshape(1, S5_LANES), lb_im.reshape(1, S5_LANES)])

    def b_blocks(bb):
        t = bb.reshape(N_BLOCKS, GROUPS_PER_BLOCK, S5_GROUP, S5_STATE)
        return jnp.einsum("bgcp,gh->bgchp", t, eye).reshape(N_BLOCKS, BLK_CH, BLK_ST).astype(BF16)

    def c_blocks(cc):
        t = cc.reshape(N_BLOCKS, GROUPS_PER_BLOCK, S5_GROUP, S5_STATE)
        return jnp.einsum("bgcp,gh->bgphc", t, eye).reshape(N_BLOCKS, BLK_ST, BLK_CH).astype(BF16)

    return lam, b_blocks(bb_re), b_blocks(bb_im), c_blocks(c_re), c_blocks(c_im)


def b_block_diag(db):
    t = db.reshape(N_BLOCKS, GROUPS_PER_BLOCK, S5_GROUP, GROUPS_PER_BLOCK, S5_STATE)
    return jnp.einsum("bgchp,gh->bgcp", t, jnp.eye(GROUPS_PER_BLOCK, dtype=F32)).reshape(S5_GROUPS, S5_GROUP, S5_STATE)


def c_block_diag(dc):
    t = dc.reshape(N_BLOCKS, GROUPS_PER_BLOCK, S5_STATE, GROUPS_PER_BLOCK, S5_GROUP)
    return jnp.einsum("bgphc,gh->bgcp", t, jnp.eye(GROUPS_PER_BLOCK, dtype=F32)).reshape(S5_GROUPS, S5_GROUP, S5_STATE)


def conj(a):
    return jnp.stack([a[0], -a[1]])


def _narrow(shape):
    return len(shape) >= 2 and shape[-1] < min(HEAD_LANES, shape[-2])


def _stored(a):
    return jnp.swapaxes(a, -1, -2) if _narrow(a.shape) else a


def _stored_shape(shape):
    return tuple(shape[:-2]) + (shape[-1], shape[-2]) if _narrow(shape) else tuple(shape)


def _from_stored(a, shape):
    return jnp.swapaxes(a, -1, -2) if _narrow(shape) else a


PACK_TILE = 16 * 128


def pack_flat(parts, dtype, multiple=PACK_TILE):
    flat = [p.reshape(-1).astype(dtype) for p in parts]
    sizes = [f.shape[0] for f in flat]
    total = sum(sizes)
    pad = (-total) % multiple
    if pad:
        flat.append(jnp.zeros((pad,), dtype))
    offs = np.cumsum([0] + sizes)[:-1].tolist()
    return jnp.concatenate(flat).reshape(-1, 128), offs


def unpack_flat(buf, offs, shapes):
    flat = buf.reshape(-1)
    return [flat[o:o + int(np.prod(s))].reshape(s) for o, s in zip(offs, shapes)]


def s5_forward(p1, n_ctx, dirs):
    saved = []
    y = None
    ctx_rows, lat_rows = (0, n_ctx), (n_ctx, p1.shape[0] - n_ctx)
    zeros_tile = jnp.zeros((2, N_SEG, S5_LANES), F32)
    zeros_row = jnp.zeros((2, 1, S5_LANES), F32)
    for k, (lam, b_re, b_im, c_re, c_im) in enumerate(dirs):
        rev = k == 1
        last = 0 if rev else N_SEG - 1
        _, _, fin = s5_scan(p1, b_re, b_im, lam, zeros_tile, reverse=rev, rows=ctx_rows, name=f"s5_ctx_finals{k}")
        carry_c = s5_chain(fin, zeros_row, lam, n_ctx // N_SEG, rev, name=f"s5_ctx_chain{k}")
        _, ck_c, fin_c = s5_scan(p1, b_re, b_im, lam, carry_c, reverse=rev, want_ckpt=True, rows=ctx_rows,
                                 name=f"s5_ctx_scan{k}")
        s0 = fin_c[:, last:last + 1, :]
        _, _, fin = s5_scan(p1, b_re, b_im, lam, zeros_tile, reverse=rev, rows=lat_rows, name=f"s5_lat_finals{k}")
        carry_l = s5_chain(fin, s0, lam, lat_rows[1] // N_SEG, rev, name=f"s5_lat_chain{k}")
        y, ck_l, _ = s5_scan(p1, b_re, b_im, lam, carry_l, reverse=rev, c_re=c_re, c_im=c_im, add=y,
                             want_ckpt=True, rows=lat_rows, name=f"s5_lat_scan{k}")
        saved.append((ck_c, ck_l))
    return y, saved


def s5_backward(dy_l, du_extra_l, p1, n_ctx, dirs, saved):
    n_lat = p1.shape[0] - n_ctx
    zeros_tile = jnp.zeros((2, N_SEG, S5_LANES), F32)
    zeros_row = jnp.zeros((2, 1, S5_LANES), F32)
    du_l, du_c = du_extra_l, None
    grads = []
    for k, (lam, b_re, b_im, c_re, c_im) in enumerate(dirs):
        rev = k == 1
        lam_c = conj(lam)
        ck_c, ck_l = saved[k]
        first = N_SEG - 1 if rev else 0
        _, _, fin = s5_scan(dy_l, c_re, c_im, lam_c, zeros_tile, reverse=not rev, adjoint=True,
                            name=f"s5_lat_adj_finals{k}")
        carry = s5_chain(fin, zeros_row, lam_c, n_lat // N_SEG, not rev, name=f"s5_lat_adj_chain{k}")
        whole = k == len(dirs) - 1
        du_l, dlam_l, dbr_l, dbi_l, dcr_l, dci_l, fin_a = s5_grads(
            dy_l, p1, ck_l, b_re, b_im, c_re, c_im, lam, carry, reverse=rev, add=du_l, u_off=n_ctx,
            du_rows=(p1.shape[0], n_ctx) if whole else None, name=f"s5_lat_grads{k}")
        g0 = fin_a[:, first:first + 1, :]
        carry = s5_chain(zeros_tile, g0, lam_c, n_ctx // N_SEG, not rev, name=f"s5_ctx_adj_chain{k}")
        du_c, dlam_c, dbr_c, dbi_c, _, _, _ = s5_grads(
            None, p1, ck_c, b_re, b_im, c_re, c_im, lam, carry, reverse=rev, add=du_c, n_rows=n_ctx,
            du_rows=(p1.shape[0], 0) if whole else None, du_into=du_l if whole else None, name=f"s5_ctx_grads{k}")
        dlam = jnp.sum(dlam_l + dlam_c, axis=1)
        grads.append((dlam, b_block_diag(dbr_l + dbr_c), b_block_diag(dbi_l + dbi_c),
                      c_block_diag(dcr_l), c_block_diag(dci_l)))
    return du_c, grads


def local_step(x, ctx, target, mod, w, late=None, reducer=None):
    L, Lc = x.shape[0], ctx.shape[0]
    T = L + Lc
    assert L % Lc == 0 and Lc % (2 * N_SEG) == 0 and L % GRID_W == 0
    D = D_MODEL
    X0 = (ctx, x)

    def mod_of(i, j):
        return mod[i, :, j, :][:, None, :]

    def vec(v):
        return v.reshape(1, 1, -1).astype(F32)

    g0 = vec(w["norm_g"][0])
    H0, p0 = norm_proj(X0, g0, mod_of(0, 1), mod_of(0, 0), w["mla_w_in"], Lc, "l0_norm_in")
    cq = Rows(p0, Q_LORA_RANK, col_blk=D // Q_LORA_RANK)
    ckv = Rows(p0, KV_LORA_RANK, col_blk=(D + Q_LORA_RANK) // KV_LORA_RANK)
    kr = Rows(p0, HEAD_LANES, col_blk=(D + Q_LORA_RANK + KV_LORA_RANK) // HEAD_LANES)
    qng, kvng = vec(w["mla_q_norm"]), vec(w["mla_kv_norm"])
    tabs = rope_tables(Lc, L)
    spread = jnp.pad(rotary_spread(), ((0, HEAD_LANES - QK_ROPE_DIM), (0, 0)))
    if late is not None:
        w = {**w, **late["qkv"](p0)}
    w_uq_p = pad_heads(w["mla_w_uq"], QK_DIM)
    w_ukv3 = w["mla_w_ukv"].reshape(KV_LORA_RANK, MLA_HEADS, QK_NOPE_DIM + V_HEAD_DIM)
    w_kn_p = pad_heads(w_ukv3[:, :, :QK_NOPE_DIM].reshape(KV_LORA_RANK, -1), QK_NOPE_DIM)
    w_v = w_ukv3[:, :, QK_NOPE_DIM:].reshape(KV_LORA_RANK, -1)
    qb, qn = q_heads(cq, qng, w_uq_p, tabs)
    kb, vb, kvn = kv_heads(ckv, kvng, w_kn_p, w_v, kr, spread, tabs)
    o, lse = attn_fwd(qb, kb, vb, Lc)
    if late is not None:
        w = {**w, **late["out"](o)}
    X1, og, out0 = mla_post_fwd(o, p0, X0, mod_of(0, 2), w["mla_w_out"], Lc)

    if late is not None:
        w = {**w, **late["l1"](X1)}
    X1p = rows_to_segments(X1, Lc, "l1_to_segments")
    tgt_p = to_segments(target, 0)
    g1 = vec(w["norm_g"][1])
    H1, p1 = norm_proj(X1p, g1, mod_of(1, 1), mod_of(1, 0), w["s5_w_in"], Lc, "l1_norm_in")
    disc_fn = lambda *a: tuple(zip(*[s5_discretise(a[0][k], a[1][k], a[2][k], a[3][k], a[4][k]) for k in range(2)]))
    disc, disc_vjp = jax.vjp(disc_fn, w["s5_a_re"], w["s5_a_im"], w["s5_log_step"], _stored(w["s5_b_re"]),
                             _stored(w["s5_b_im"]))
    dirs = [s5_block_weights(disc[0][k], disc[1][k], disc[2][k], disc[3][k], w["s5_c_re"][k], w["s5_c_im"][k])
            for k in range(2)]
    y_ssm, s5_saved = s5_forward(p1, Lc, dirs)

    row = lambda v: v.reshape(1, D).astype(F32)
    (lvec, dX2, d_yssm, d_u_act, d_z1, d_fg, d_gt1, d_bg, d_d, gw_glu, gw_out) = s5_tail(
        y_ssm, p1, X1p, tgt_p, Lc, row(w["s5_d"]), row(w["s5_b_glu"]), mod[1, 1:2, 2, :], row(w["final_g"]),
        w["s5_w_glu"], w["s5_w_out"])
    loss = jnp.sum(lvec)
    gw = {"final_g": d_fg.reshape(D), "s5_b_glu": d_bg.reshape(D), "s5_d": d_d.reshape(D),
          "s5_w_glu": gw_glu, "s5_w_out": gw_out}
    dmod = {}

    du_p, s5_g = s5_backward(d_yssm, d_u_act, p1, Lc, dirs, s5_saved)
    d_disc = tuple(tuple(s5_g[k][j - 1].reshape(disc[j][k].shape) if j >= 2 else
                         s5_g[k][0][j].reshape(disc[j][k].shape) for k in range(2)) for j in range(4))
    gw["s5_a_re"], gw["s5_a_im"], gw["s5_log_step"], d_bt_re, d_bt_im = disc_vjp(d_disc)
    gw["s5_b_re"], gw["s5_b_im"] = jnp.swapaxes(d_bt_re, -1, -2), jnp.swapaxes(d_bt_im, -1, -2)
    gw["s5_c_re"] = jnp.stack([s5_g[0][3], s5_g[1][3]])
    gw["s5_c_im"] = jnp.stack([s5_g[0][4], s5_g[1][4]])
    gw["s5_w_in"] = mm_tn(H1, du_p, name="l1_in_dw", b_more=d_z1)
    if reducer is not None:
        g1 = g1 + reducer["l1"][0]({n: gw.pop(n) for n in LAYER1_MATS})[0, 0]
    d_X1, d_g1, d_sc1, d_sh1 = norm_proj_bwd([(du_p, 0, 0), (d_z1, D, Lc)], w["s5_w_in"], X1p, g1, mod_of(1, 1),
                                             mod_of(1, 0), dX2, Lc, "l1_norm_in_bwd", dx_from_segments=True)
    d_gt1_full = jnp.concatenate([jnp.zeros((1, 1, D), F32), d_gt1[None]], axis=0)
    dmod[1] = (d_sh1, d_sc1, d_gt1_full)

    d_o, d_z0, d_gt0, gw["mla_w_out"] = mla_post_bwd(d_X1, out0, og, o, p0, mod_of(0, 2), w["mla_w_out"], Lc)
    if reducer is not None:
        started = reducer["l1"][1](d_o)[0, 0] + reducer["out"][0]({"mla_w_out": gw.pop("mla_w_out")})[0, 0]
        tabs = (tabs[0] + started,) + tabs[1:]
    d_q, dk_p, d_v = attn_bwd(qb, kb, vb, o, d_o, lse, tabs, Lc)
    if reducer is not None:
        tabs = (tabs[0] + reducer["out"][1](d_q)[0, 0],) + tabs[1:]
    d_k, d_kr = heads_unrope(dk_p, tabs, spread,
                             name="l0_k_unrope")
    d_qn = mm_nt(d_q, w_uq_p, name="l0_uq_dx")
    gw["mla_w_uq"] = unpad_heads(mm_tn(qn, d_q, name="l0_uq_dw"), QK_DIM)
    d_kvn = mm_nt(d_k, w_kn_p, name="l0_ukn_dx") + mm_nt(d_v, w_v, name="l0_uv_dx")
    dw_kn = unpad_heads(mm_tn(kvn, d_k, name="l0_ukn_dw"), QK_NOPE_DIM).reshape(KV_LORA_RANK, MLA_HEADS, QK_NOPE_DIM)
    dw_v = mm_tn(kvn, d_v, name="l0_uv_dw").reshape(KV_LORA_RANK, MLA_HEADS, V_HEAD_DIM)
    gw["mla_w_ukv"] = jnp.concatenate([dw_kn, dw_v], axis=-1).reshape(KV_LORA_RANK, -1)
    d_cq, d_qng = rowwise_bwd(f_rms, [cq], [qng], [d_qn], [0], [0], T, 0, "l0_qnorm_bwd")
    d_ckv, d_kvng = rowwise_bwd(f_rms, [ckv], [kvng], [d_kvn], [0], [0], T, 0, "l0_kvnorm_bwd")
    gw["mla_q_norm"] = d_qng.reshape(-1)
    gw["mla_kv_norm"] = d_kvng.reshape(-1)
    o_cq, o_ckv = D, D + Q_LORA_RANK
    o_kr = o_ckv + KV_LORA_RANK
    d_head = jnp.concatenate([d_cq, d_ckv, d_kr], axis=1)
    gw["mla_w_in"] = jnp.concatenate([mm_tn(H0, d_head, name="l0_in_dw_head")[:, :P0_HEAD],
                                      mm_tn(H0, d_z0, name="l0_in_dw_z")], axis=1)
    dx, d_g0, d_sc0, d_sh0 = norm_proj_bwd(
        [(d_z0, 0, 0), (d_cq, o_cq, 0), (d_ckv, o_ckv, 0), (d_kr, o_kr, 0)], w["mla_w_in"], X0, g0, mod_of(0, 1),
        mod_of(0, 0), d_X1, Lc, "l0_norm_in_bwd", latent_dx_only=True)
    dmod[0] = (d_sh0, d_sc0, d_gt0)
    gw["norm_g"] = jnp.stack([d_g0.reshape(D), d_g1.reshape(D)])
    dmod_arr = jnp.stack([jnp.stack([dmod[i][j][:, 0, :] for j in range(3)], axis=1) for i in range(2)])
    ready = {**reducer["l1"][2](dx), **reducer["out"][2](dx)} if reducer is not None else {}
    return loss, dx, dmod_arr, gw, ready


SHARDED = {
    "mla_w_in": 1, "mla_w_uq": 1, "mla_w_ukv": 1, "mla_w_out": 0,
    "s5_w_in": 1, "s5_w_glu": 0, "s5_w_out": 0, "s5_d": 0, "s5_b_glu": 0,
}
SHARDED_MATS = ["mla_w_in", "mla_w_uq", "mla_w_ukv", "mla_w_out", "s5_w_in", "s5_w_glu", "s5_w_out"]
SHARDED_VECS = ["s5_d", "s5_b_glu"]
REPLICATED = ["norm_g", "mla_q_norm", "mla_kv_norm", "s5_a_re", "s5_a_im", "s5_log_step", "s5_b_re", "s5_b_im",
              "s5_c_re", "s5_c_im", "final_g"]
WEIGHT_ORDER = ["c_ctx", "ada_w", "ada_b", "norm_g", "mla_w_in", "mla_q_norm", "mla_w_uq", "mla_kv_norm", "mla_w_ukv",
                "mla_w_out", "s5_w_in", "s5_a_re", "s5_a_im", "s5_log_step", "s5_b_re", "s5_b_im", "s5_c_re", "s5_c_im",
                "s5_d", "s5_w_glu", "s5_b_glu", "s5_w_out", "final_g"]


P0_HEAD = Q_LORA_RANK + KV_LORA_RANK + QK_ROPE_DIM


P0_WIDTH = 1536


def w_in_to_kernel_order(w):
    pad = jnp.zeros((w.shape[0], P0_WIDTH - w.shape[1]), w.dtype)
    return jnp.concatenate([w[:, P0_HEAD:], w[:, :P0_HEAD], pad], axis=1)


LAYER0_MATS = ["mla_w_in", "mla_w_uq", "mla_w_ukv", "mla_w_out"]
LAYER1_MATS = ["s5_w_in", "s5_w_glu", "s5_w_out"]


def _whole_matrices(names, own_blocks, gathered):
    chip = _chip_index(_coords())
    full = {}
    for n, own, o in zip(names, own_blocks, gathered):
        slot = lax.broadcasted_iota(jnp.int32, (N_CHIP, 1, 1), 0)
        o = jnp.where(slot == chip, own[None], o.reshape((N_CHIP,) + own.shape))
        full[n] = o.reshape(-1, o.shape[-1]) if SHARDED[n] == 0 else o.transpose(1, 0, 2).reshape(o.shape[1], -1)
    return full


FIRST_MATS = ["mla_w_in"]
LATER_GROUPS = {"qkv": ["mla_w_uq", "mla_w_ukv"], "out": ["mla_w_out"], "l1": LAYER1_MATS}


def gather_weights(ws):
    mats = [ws[n].astype(BF16) for n in FIRST_MATS]
    full = _whole_matrices(FIRST_MATS, mats, gather_halves(mats, "gather_weights"))
    full["mla_w_in"] = w_in_to_kernel_order(full["mla_w_in"])
    return full


def gather_weights_behind(ws, after):
    token, finish = 0.0, {}
    for group, names in LATER_GROUPS.items():
        mats = [ws[n].astype(BF16) for n in names]
        flight, tok = exchange_start(
            mats, [jax.ShapeDtypeStruct((N_CHIP,) + m.shape, m.dtype) for m in mats],
            [(f, a, lambda me, peer: None, a, lambda s: (_chip_index(s),))
             for a in range(len(mats)) for f in CHIP_FLIPS], f"gather_{group}_start", after=after)
        after = [tok]
        token = token + tok[0, 0]

        def finish_group(after_work, group=group, names=names, flight=flight):
            own, got = exchange_wait(flight, after_work, f"gather_{group}_wait")
            return _whole_matrices(names, own, got)

        finish[group] = finish_group
    return token, finish


def _grad_slots(gw, names):
    slots = []
    for n in names:
        g = gw[n]
        if SHARDED[n] == 0:
            slots.append(g.reshape(N_CHIP, 2, g.shape[0] // (2 * N_CHIP), g.shape[1]))
        else:
            k, n4 = g.shape
            slots.append(g.reshape(k, N_CHIP, n4 // N_CHIP).transpose(1, 0, 2)
                         .reshape(N_CHIP, 2, k // 2, n4 // N_CHIP))
    return slots


def _to_sibling_half(count):
    return [(CORE_FLIP, i, lambda me, peer: (slice(None), 1 - me[2]), i, lambda s: None) for i in range(count)]


def _to_chips(count):
    return [(f, i, lambda me, peer: (_chip_index(peer),), i, lambda s: (_chip_index(s),))
            for i in range(count) for f in CHIP_FLIPS]


def _place():
    me = _coords()
    return jnp.stack([me[2], _chip_index(me)]).astype(jnp.int32)


def reduce_behind(names, tag):
    state = {}
    count = len(names)

    def begin(gw):
        slots = _grad_slots(gw, names)
        lands = [jax.ShapeDtypeStruct((N_CHIP,) + s.shape[2:], F32) for s in slots]
        state["in"], token = exchange_start(slots, lands, _to_sibling_half(count), f"grads_{tag}_swap_in_start")
        return token

    def middle(after):
        slots, got = exchange_wait(state["in"], after, f"grads_{tag}_swap_in_wait")
        place = _place()
        sums = [pair_add(s, g, place[:1], BF16, f"grads_pair_{n}") for n, s, g in zip(names, slots, got)]
        lands = [jax.ShapeDtypeStruct(s.shape, s.dtype) for s in sums]
        state["out"], token = exchange_start(sums, lands, _to_chips(count), f"grads_{tag}_scatter_start")
        return token

    def end(after):
        sums, parts = exchange_wait(state["out"], after, f"grads_{tag}_scatter_wait")
        place = _place()
        return {n: sum_chips(p, s, place, f"grads_sum_{n}") for n, p, s in zip(names, parts, sums)}

    return begin, middle, end


def reduce_gradients(gw, ready_halves, also=None):
    me = _coords()
    place = _place()
    mat_names = [n for n in SHARDED_MATS if n not in ready_halves]
    slots = dict(zip(mat_names, _grad_slots(gw, mat_names)))
    gw = {**gw, **(also or {})}
    small_names = REPLICATED + SHARDED_VECS + list(also or {})
    small, small_offs = pack_flat([_stored(gw[n]).astype(F32) for n in small_names], F32, N_CHIP * 32 * 128)
    slots["small"] = small.reshape(N_CHIP, 2, -1, 128)
    names = list(slots)
    count = len(names)
    got = exchange([slots[n] for n in names],
                   [jax.ShapeDtypeStruct((N_CHIP,) + slots[n].shape[2:], F32) for n in names],
                   _to_sibling_half(count), [], "grads_swap_in")
    sums = [pair_add(slots[n], g, place[:1], F32 if n == "small" else BF16, f"grads_pair_{n}")
            for n, g in zip(names, got)]
    parts = exchange(sums, [jax.ShapeDtypeStruct(s.shape, s.dtype) for s in sums], _to_chips(count), [],
                     "grads_scatter")
    halves = {n: sum_chips(p, s, place, f"grads_sum_{n}") for n, p, s in zip(names, parts, sums)}
    halves.update(ready_halves)
    all_names = list(halves)
    fulls = exchange(
        [halves[n] for n in all_names], [jax.ShapeDtypeStruct(halves[n].shape, F32) for n in all_names],
        [(CORE_FLIP, i, lambda me, peer: (me[2],), i, lambda s: (s[2],)) for i in range(len(all_names))], [],
        "grads_swap_out", aliases={i: i for i in range(len(all_names))})
    out = {n: f.reshape(-1, f.shape[-1]) for n, f in zip(all_names, fulls)}
    quarter = out.pop("small")
    gather, token = exchange_start(
        [quarter], [jax.ShapeDtypeStruct((N_CHIP,) + quarter.shape, F32)],
        [(f, 0, lambda me, peer: None, 0, lambda s: (_chip_index(s),)) for f in CHIP_FLIPS],
        "grads_gather_small_start")

    def finish_small(after):
        (own,), (got_small,) = exchange_wait(gather, after, "grads_gather_small_wait")
        slot = lax.broadcasted_iota(jnp.int32, (N_CHIP, 1, 1), 0)
        small_all = jnp.where(slot == _chip_index(me), own[None], got_small)
        vals = unpack_flat(small_all, small_offs, [_stored_shape(gw[n].shape) for n in small_names])
        res = {}
        for n, v in zip(small_names, vals):
            v = _from_stored(v, gw[n].shape)
            if n in SHARDED_VECS:
                size = v.shape[0] // N_CHIP
                v = lax.dynamic_slice_in_dim(v, _chip_index(me) * size, size)
            res[n] = v
        return res

    return out, finish_small, token


def kernel(x, c, ctx, c_ctx, ada_w, ada_b, norm_g, mla_w_in, mla_q_norm, mla_w_uq, mla_kv_norm, mla_w_ukv, mla_w_out, s5_w_in, s5_a_re, s5_a_im, s5_log_step, s5_b_re, s5_b_im, s5_c_re, s5_c_im, s5_d, s5_w_glu, s5_b_glu, s5_w_out, final_g, loss_target, m_c_ctx, m_ada_w, m_ada_b, m_norm_g, m_mla_w_in, m_mla_q_norm, m_mla_w_uq, m_mla_kv_norm, m_mla_w_ukv, m_mla_w_out, m_s5_w_in, m_s5_a_re, m_s5_a_im, m_s5_log_step, m_s5_b_re, m_s5_b_im, m_s5_c_re, m_s5_c_im, m_s5_d, m_s5_w_glu, m_s5_b_glu, m_s5_w_out, m_final_g, v_c_ctx, v_ada_w, v_ada_b, v_norm_g, v_mla_w_in, v_mla_q_norm, v_mla_w_uq, v_mla_kv_norm, v_mla_w_ukv, v_mla_w_out, v_s5_w_in, v_s5_a_re, v_s5_a_im, v_s5_log_step, v_s5_b_re, v_s5_b_im, v_s5_c_re, v_s5_c_im, v_s5_d, v_s5_w_glu, v_s5_b_glu, v_s5_w_out, v_final_g):
    args = dict(locals())
    weights = {n: args[n] for n in WEIGHT_ORDER}
    D = D_MODEL
    xi, yi, ci = _coords()
    chip = 2 * xi + yi
    me = 4 * xi + 2 * yi + ci
    n_col = ada_w.shape[2]

    c_all = allgather_devices(jnp.pad(c, ((0, 7), (0, 0))), "gather_c")[:, 0, :]
    cond = jnp.concatenate([c_all, jnp.broadcast_to(c_ctx[None], (8, D))], axis=0)
    (s_cond,) = rowwise_fwd(lambda v: (_silu(v),), [cond], [], [D], [F32], 16, 0, "cond_silu")
    ada_rows = ada_w.reshape(2 * D, n_col)
    mod_cols = jnp.stack([mm_nn(s_cond, ada_rows, name=f"mod_proj{i}", b_blk=i) for i in range(2)])
    vec_tiles = [jnp.pad(weights[n][0].reshape(-1, 128), ((0, 6), (0, 0))) for n in SHARDED_VECS]
    mod_all, *vec_all = allgather_chips([mod_cols] + vec_tiles, "gather_mod")
    mod_all = mod_all.transpose(1, 2, 0, 3).reshape(2, 16, 3 * D) + ada_b[:, None, :]
    mine = lax.broadcasted_iota(jnp.int32, (1, 16, 1), 1) == me
    mod_l = jnp.sum(jnp.where(mine, mod_all, 0.0), axis=1)
    mod_c = mod_all[:, 8, :]
    mod = jnp.stack([mod_c.reshape(2, 3, D), mod_l.reshape(2, 3, D)], axis=1)

    w = gather_weights({n: weights[n][0] for n in FIRST_MATS})
    token, late = gather_weights_behind({n: weights[n][0] for n in SHARDED_MATS}, [w["mla_w_in"], mod])
    for n, v in zip(SHARDED_VECS, vec_all):
        w[n] = v[:, :2, :].reshape(-1)
    for n in ["norm_g", "final_g"]:
        w[n] = weights[n]
    for n in ["mla_q_norm", "mla_kv_norm", "s5_a_re", "s5_a_im", "s5_log_step", "s5_b_re", "s5_b_im",
              "s5_c_re", "s5_c_im"]:
        w[n] = weights[n][0]

    reducer = {"l1": reduce_behind(LAYER1_MATS, "l1"), "out": reduce_behind(["mla_w_out"], "out")}
    loss_me, dx, dmod, gw, ready = local_step(x[0], ctx[0], loss_target[0], mod + token, w, late,
                                              reducer)

    dmod_rows, loss_all = _gather([dmod.reshape(2, 2, 3 * D), jnp.broadcast_to(loss_me, (8, 128))],
                                  ALL_FLIPS, _dev_index, N_DEV, "gather_dmod")
    loss = functools.reduce(lambda s, d: s + loss_all[d, 0, 0], range(1, N_DEV), loss_all[0, 0, 0])
    dm = jnp.concatenate([dmod_rows[:, :, 1, :], dmod_rows[:, :, 0, :]], axis=0).transpose(1, 0, 2)
    g_ada_b = jnp.sum(dm, axis=1)
    dm_cols = lax.dynamic_slice_in_dim(dm, chip * n_col, n_col, axis=2)
    g_ada_w = jnp.stack([mm_tn(s_cond, dm_cols[i], name=f"mod_proj_dw{i}") for i in range(2)])
    dmc = jnp.sum(dm_cols[:, 8:, :], axis=1)
    dmc8 = jnp.broadcast_to(dmc[:, None, :], (2, 8, n_col))
    g_sc = (mm_nt(dmc8[0], ada_rows, name="mod_proj_dx0", b_rows=D, b_blk=0)[0]
            + mm_nt(dmc8[1], ada_rows, name="mod_proj_dx1", b_rows=D, b_blk=1)[0])
    g_silu_part = jnp.where(ci == 0, g_sc, 0.0)

    grads = {"ada_w": g_ada_w, "ada_b": g_ada_b}
    deltas, new_m, new_v = {}, {}, {}
    small = [n for n in WEIGHT_ORDER if weights[n].size < 50000]

    def update(n, after=None):
        shp = weights[n].shape
        rows = lambda a: _stored(a.reshape(shp)).reshape(-1, _stored_shape(shp)[-1])
        back = lambda a: _from_stored(a.reshape(_stored_shape(shp)), shp)
        d_, m_, v_ = adamw(rows(weights[n]), rows(grads[n]), rows(args["m_" + n]), rows(args["v_" + n]),
                           name=f"adamw_{n}", after=after)
        deltas[n], new_m[n], new_v[n] = back(d_), back(m_), back(v_)

    red, finish_small, small_started = reduce_gradients(gw, ready, {"silu_c_ctx": g_silu_part})
    update("ada_w", small_started)
    for n in SHARDED_MATS:
        grads[n] = red[n].reshape(weights[n].shape)
        update(n, small_started)
    red_small = finish_small([deltas[n] for n in ["ada_w"] + SHARDED_MATS])
    for n in REPLICATED + SHARDED_VECS:
        grads[n] = red_small[n].reshape(weights[n].shape)
    (g_c_ctx,) = rowwise_bwd(lambda v: (_silu(v),), [jnp.broadcast_to(c_ctx[None], (8, D))], [],
                             [jnp.broadcast_to(red_small["silu_c_ctx"][None], (8, D))], [0], [], 8, 0, "cond_silu_bwd")
    grads["c_ctx"] = g_c_ctx[0]
    for n in WEIGHT_ORDER:
        if n not in small and n not in deltas:
            update(n)
    packs = []
    offs = None
    for src in (weights, grads, {n: args["m_" + n] for n in small}, {n: args["v_" + n] for n in small}):
        buf, offs = pack_flat([src[n] for n in small], F32)
        packs.append(buf)
    outs = adamw(*packs, name="adamw_small")
    for res, dst in zip(outs, (deltas, new_m, new_v)):
        for n, val in zip(small, unpack_flat(res, offs, [weights[n].shape for n in small])):
            dst[n] = val

    return (loss, dx[None], *[grads[n] for n in WEIGHT_ORDER], *[deltas[n] for n in WEIGHT_ORDER],
            *[new_m[n] for n in WEIGHT_ORDER], *[new_v[n] for n in WEIGHT_ORDER])
```

```python
import functools
import math

import jax
import jax.numpy as jnp
import numpy as np
from jax import lax
from jax.experimental import pallas as pl
from jax.experimental.pallas import tpu as pltpu

F32 = jnp.float32
BF16 = jnp.bfloat16

D_MODEL = 1024
GRID_W = 64
EPS = 1e-6
MLA_HEADS = 16
QK_NOPE_DIM = 64
QK_ROPE_DIM = 32
V_HEAD_DIM = 64
Q_LORA_RANK = 256
KV_LORA_RANK = 128
QK_DIM = QK_NOPE_DIM + QK_ROPE_DIM
SOFTMAX_SCALE = QK_DIM ** -0.5
ROPE_THETA = 10000.0
S5_GROUP = 16
S5_GROUPS = D_MODEL // S5_GROUP
S5_STATE = 64
S5_LANES = S5_GROUPS * S5_STATE
N_SEG = 8
GROUPS_PER_BLOCK = 8
N_BLOCKS = S5_GROUPS // GROUPS_PER_BLOCK
BLK_CH = GROUPS_PER_BLOCK * S5_GROUP
BLK_ST = GROUPS_PER_BLOCK * S5_STATE

ADAM_LR = 0.001
ADAM_B1 = 0.9
ADAM_B2 = 0.999
ADAM_EPS = 1e-08
ADAM_WD = 0.01
ADAM_STEP = 10

N_DEV = 8
N_CHIP = 4
MESH = pl.DeviceIdType.MESH
VMEM_LIMIT = 52 * 1024 * 1024
ROW_TILE = 256


def _params(sem=None, vmem=None):
    return pltpu.CompilerParams(dimension_semantics=sem, vmem_limit_bytes=vmem)


def mm_nn(a, b, out_dtype=F32, name="mm_nn", b_blk=0):
    M, K = a.shape
    N = b.shape[1]
    tm = math.gcd(ROW_TILE, M)

    def body(a_ref, b_ref, o_ref):
        o_ref[...] = jnp.dot(a_ref[...].astype(BF16), b_ref[...].astype(BF16),
                             preferred_element_type=F32).astype(o_ref.dtype)

    return pl.pallas_call(
        body, out_shape=jax.ShapeDtypeStruct((M, N), out_dtype), grid=(M // tm,),
        in_specs=[pl.BlockSpec((tm, K), lambda i: (i, 0)), pl.BlockSpec((K, N), lambda i: (b_blk, 0))],
        out_specs=pl.BlockSpec((tm, N), lambda i: (i, 0)),
        compiler_params=_params(("parallel",), VMEM_LIMIT), name=name)(a, b)


def mm_nt(a, b, out_dtype=F32, name="mm_nt", b_rows=None, b_blk=0, more=None):
    M, N = a.shape
    K = b.shape[0] if b_rows is None else b_rows
    tm = math.gcd(ROW_TILE, M)
    extra = list(more) if more is not None else []

    def body(a_ref, b_ref, *refs):
        o_ref = refs[-1]
        acc = lax.dot_general(a_ref[...].astype(BF16), b_ref[...].astype(BF16), (((1,), (1,)), ((), ())),
                              preferred_element_type=F32)
        if extra:
            acc = acc + lax.dot_general(refs[0][...].astype(BF16), refs[1][...].astype(BF16),
                                        (((1,), (1,)), ((), ())), preferred_element_type=F32)
        o_ref[...] = acc.astype(o_ref.dtype)

    more_specs = ([pl.BlockSpec((tm, extra[0].shape[1]), lambda i: (i, 0)),
                   pl.BlockSpec(extra[1].shape, lambda i: (0, 0))] if extra else [])
    return pl.pallas_call(
        body, out_shape=jax.ShapeDtypeStruct((M, K), out_dtype), grid=(M // tm,),
        in_specs=[pl.BlockSpec((tm, N), lambda i: (i, 0)), pl.BlockSpec((K, N), lambda i: (b_blk, 0))] + more_specs,
        out_specs=pl.BlockSpec((tm, K), lambda i: (i, 0)),
        compiler_params=_params(("parallel",), VMEM_LIMIT), name=name)(a, b, *extra)


def mm_tn(a, b, name="mm_tn", b_more=None):
    M, N = b.shape
    K = a.shape[1]
    tn = math.gcd(512, N) if N % 128 == 0 and N > 512 else N
    bs = [b] if b_more is None else [b, b_more]
    assert all(x.shape[1] == N for x in bs)
    nb = N // tn

    def body(a_ref, *refs):
        o_ref = refs[-1]
        for k, b_ref in enumerate(refs[:-1]):
            def product(b_ref=b_ref):
                o_ref[...] = lax.dot_general(a_ref[a.shape[0] - b_ref.shape[0]:, :].astype(BF16),
                                             b_ref[...].astype(BF16), (((0,), (0,)), ((), ())),
                                             preferred_element_type=F32)
            if len(bs) == 1:
                product()
            else:
                pl.when(pl.program_id(0) // nb == k)(product)

    def b_spec(k, rows):
        return pl.BlockSpec((rows, tn), lambda j: (0, jnp.clip(j - k * nb, 0, nb - 1)))

    return pl.pallas_call(
        body, out_shape=jax.ShapeDtypeStruct((K, N * len(bs)), F32), grid=(nb * len(bs),),
        in_specs=[pl.BlockSpec(a.shape, lambda j: (0, 0))] + [b_spec(k, x.shape[0]) for k, x in enumerate(bs)],
        out_specs=pl.BlockSpec((K, tn), lambda j: (0, j)),
        compiler_params=_params(("parallel",), VMEM_LIMIT), name=name)(a, *bs)


class Rows:
    def __init__(self, arr, width=None, row_off=0, col_blk=0):
        self.arr = arr
        self.width = arr.shape[1] if width is None else width
        self.row_off = row_off
        self.col_blk = col_blk

    def spec(self, tm):
        ro, cb = self.row_off // tm, self.col_blk
        return pl.BlockSpec((tm, self.width), lambda i: (i + ro, cb))


def _as_rows(x):
    return x if isinstance(x, Rows) else Rows(x)


def _row_tile(n_rows, n_ctx_rows, rows):
    tm = math.gcd(ROW_TILE, n_rows, n_ctx_rows)
    for r in rows:
        tm = math.gcd(tm, r.row_off)
    return tm


def _bc_spec(arr, n_ctx_blocks):
    g, _, d = arr.shape
    if g == 1:
        return pl.BlockSpec((1, 1, d), lambda i: (0, 0, 0))
    return pl.BlockSpec((1, 1, d), lambda i: ((i >= n_ctx_blocks).astype(jnp.int32), 0, 0))


def rowwise_fwd(fn, rows, bcs, out_dims, out_dtypes, n_rows, n_ctx_rows, name):
    rows = [_as_rows(r) for r in rows]
    tm = _row_tile(n_rows, n_ctx_rows, rows)
    ncb = n_ctx_rows // tm
    nr, nb = len(rows), len(bcs)

    def body(*refs):
        vals = [r[...].astype(F32) for r in refs[:nr]] + [b[0].astype(F32) for b in refs[nr:nr + nb]]
        outs = fn(*vals)
        for o_ref, v in zip(refs[nr + nb:], outs):
            o_ref[...] = v.astype(o_ref.dtype)

    outs = pl.pallas_call(
        body,
        out_shape=[jax.ShapeDtypeStruct((n_rows, d), dt) for d, dt in zip(out_dims, out_dtypes)],
        grid=(n_rows // tm,),
        in_specs=[r.spec(tm) for r in rows] + [_bc_spec(b, ncb) for b in bcs],
        out_specs=[pl.BlockSpec((tm, d), lambda i: (i, 0)) for d in out_dims],
        compiler_params=_params(("parallel",), VMEM_LIMIT), name=name)(*[r.arr for r in rows], *bcs)
    return outs


def rowwise_bwd(fn, rows, bcs, cts, diff_rows, diff_bcs, n_rows, n_ctx_rows, name, ct_extra=None, lat_add=None):
    rows = [_as_rows(r) for r in rows]
    cts = [_as_rows(c) for c in cts]
    extra = [_as_rows(ct_extra)] if ct_extra is not None else []
    tm = _row_tile(n_rows, n_ctx_rows, rows + cts + extra)
    ncb = n_ctx_rows // tm
    nr, nb, nc = len(rows), len(bcs), len(cts)
    ndr, ndb = len(diff_rows), len(diff_bcs)
    n_in = nr + nb + nc + len(extra) + (lat_add is not None)

    def body(*refs):
        i = pl.program_id(0)
        rvals = [r[...].astype(F32) for r in refs[:nr]]
        bvals = [b[0].astype(F32) for b in refs[nr:nr + nb]]
        cvals = [c[...].astype(F32) for c in refs[nr + nb:nr + nb + nc]]
        if extra:
            cvals[0] = cvals[0] + refs[nr + nb + nc][...].astype(F32)
        outs = refs[n_in:]

        def f(*d):
            rv, bv = list(rvals), list(bvals)
            for k, idx in enumerate(diff_rows):
                rv[idx] = d[k]
            for k, idx in enumerate(diff_bcs):
                bv[idx] = d[ndr + k]
            return tuple(fn(*rv, *bv))

        primals = [rvals[k] for k in diff_rows] + [bvals[k] for k in diff_bcs]
        _, vjp = jax.vjp(f, *primals)
        grads = list(vjp(tuple(cvals)))
        if lat_add is not None:
            add = refs[n_in - 1][...]
            grads[0] = grads[0] + (add if lat_add.shape[0] == n_rows else jnp.where(i >= ncb, add, 0.0))
        for k in range(ndr):
            outs[k][...] = grads[k].astype(outs[k].dtype)
        for k, idx in enumerate(diff_bcs):
            o_ref = outs[ndr + k]
            first = (i == 0)
            if bcs[idx].shape[0] == 2:
                first = first | (i == ncb)

            @pl.when(first)
            def _(o_ref=o_ref):
                o_ref[...] = jnp.zeros_like(o_ref)

            o_ref[0] += grads[ndr + k]

    out_shape = [jax.ShapeDtypeStruct((n_rows, rows[k].width), F32) for k in diff_rows]
    out_shape += [jax.ShapeDtypeStruct(bcs[k].shape, F32) for k in diff_bcs]
    out_specs = [pl.BlockSpec((tm, rows[k].width), lambda i: (i, 0)) for k in diff_rows]
    out_specs += [_bc_spec(bcs[k], ncb) for k in diff_bcs]
    ins = [r.arr for r in rows] + list(bcs) + [c.arr for c in cts + extra]
    in_specs = [r.spec(tm) for r in rows] + [_bc_spec(b, ncb) for b in bcs] + [c.spec(tm) for c in cts + extra]
    if lat_add is not None:
        ins.append(lat_add)
        skip = ncb if lat_add.shape[0] != n_rows else 0
        in_specs.append(pl.BlockSpec((tm, lat_add.shape[1]), lambda i: (jnp.maximum(i - skip, 0), 0)))
    outs = pl.pallas_call(
        body, out_shape=out_shape, grid=(n_rows // tm,), in_specs=in_specs, out_specs=out_specs,
        compiler_params=_params(("arbitrary",), VMEM_LIMIT), name=name)(*ins)
    return outs


def _rms(x):
    return x * lax.rsqrt(jnp.mean(x * x, axis=-1, keepdims=True) + EPS)


def _sigmoid(x):
    return 0.5 * (jnp.tanh(0.5 * x) + 1.0)


def _silu(x):
    return x * _sigmoid(x)


def _gelu_tanh(x):
    return 0.5 * x * (1.0 + jnp.tanh(math.sqrt(2.0 / math.pi) * (x + 0.044715 * (x * x * x))))


def f_norm_mod(x, g, sc, sh):
    return ((_rms(x) * g) * (1.0 + sc) + sh,)


def f_rms(x, g):
    return (_rms(x) * g,)


def f_gate(o, z):
    return (o * _silu(z),)


def f_s5_act(y, u, d):
    return (_gelu_tanh(y + d * u),)


def f_s5_glu(ya, gl, z, b):
    return (ya * _sigmoid(gl + b) * _silu(z),)


def _as_parts(x, n_ctx):
    xs = x if isinstance(x, tuple) else (x,)
    assert len(xs) == 1 or xs[0].shape[0] == n_ctx
    return xs, sum(p.shape[0] for p in xs)


def _parts_specs(xs, tm, ncb):
    d = xs[0].shape[1]
    if len(xs) == 1:
        return [pl.BlockSpec((tm, d), lambda i: (i, 0))]
    return [pl.BlockSpec((tm, d), lambda i: (jnp.minimum(i, ncb - 1), 0)),
            pl.BlockSpec((tm, d), lambda i: (jnp.maximum(i - ncb, 0), 0))]


def _parts_tile(x_refs, ncb):
    if len(x_refs) == 1:
        return x_refs[0][...]
    return jnp.where(pl.program_id(0) < ncb, x_refs[0][...], x_refs[1][...])


def norm_proj(x, g, sc, sh, w, n_ctx, name):
    xs, n = _as_parts(x, n_ctx)
    d = xs[0].shape[1]
    nw = w.shape[1]
    tm = math.gcd(ROW_TILE, n, n_ctx)
    ncb = n_ctx // tm
    nx = len(xs)

    def body(*refs):
        g_ref, sc_ref, sh_ref, w_ref, h_ref, p_ref = refs[nx:]
        h = f_norm_mod(_parts_tile(refs[:nx], ncb), g_ref[0], sc_ref[0], sh_ref[0])[0].astype(BF16)
        h_ref[...] = h
        p_ref[...] = jnp.dot(h, w_ref[...], preferred_element_type=F32)

    row = pl.BlockSpec((tm, d), lambda i: (i, 0))
    return pl.pallas_call(
        body, out_shape=[jax.ShapeDtypeStruct((n, d), BF16), jax.ShapeDtypeStruct((n, nw), F32)], grid=(n // tm,),
        in_specs=_parts_specs(xs, tm, ncb) + [_bc_spec(g, ncb), _bc_spec(sc, ncb), _bc_spec(sh, ncb),
                                              pl.BlockSpec(w.shape, lambda i: (0, 0))],
        out_specs=[row, pl.BlockSpec((tm, nw), lambda i: (i, 0))],
        compiler_params=_params(("parallel",), VMEM_LIMIT), name=name)(*xs, g, sc, sh, w)


def norm_proj_bwd(terms, w, x, g, sc, sh, add, n_ctx, name, latent_dx_only=False, dx_from_segments=False):
    xs, n = _as_parts(x, n_ctx)
    adds = add if isinstance(add, tuple) else (add,)
    d = xs[0].shape[1]
    tm = math.gcd(ROW_TILE, n, n_ctx, *[t[2] for t in terms])
    ncb = n_ctx // tm
    nt, nx, na = len(terms), len(xs), len(adds)
    add_skip = ncb if na == 1 and add.shape[0] != n else 0
    n_dx = 2 if dx_from_segments else 1
    tj = tm // N_SEG
    assert not dx_from_segments or (ncb == 1 and not latent_dx_only)

    def body(*refs):
        i = pl.program_id(0)
        a_refs = refs[:nt]
        w_ref, x_refs = refs[nt], refs[nt + 1:nt + 1 + nx]
        g_ref, sc_ref, sh_ref = refs[nt + 1 + nx:nt + 4 + nx]
        add_refs = refs[nt + 4 + nx:nt + 4 + nx + na]
        outs = refs[nt + 4 + nx + na:]
        dx_refs, (dg_ref, dsc_ref, dsh_ref) = outs[:n_dx], outs[n_dx:n_dx + 3]
        d_h = None
        for a_ref, (a, off, first) in zip(a_refs, terms):
            part = lax.dot_general(a_ref[...].astype(BF16), w_ref[:, off:off + a.shape[1]], NT_DIMS,
                                   preferred_element_type=F32)
            if first:
                part = jnp.where(i >= first // tm, part, 0.0)
            d_h = part if d_h is None else d_h + part
        _, vjp = jax.vjp(lambda x_, g_, sc_, sh_: f_norm_mod(x_, g_, sc_, sh_), _parts_tile(x_refs, ncb), g_ref[0],
                         sc_ref[0], sh_ref[0])
        d_x, d_g, d_sc, d_sh = vjp((d_h,))
        extra = _parts_tile(add_refs, ncb)
        d_x = d_x + (extra if add_skip == 0 else jnp.where(i >= ncb, extra, 0.0))
        if dx_from_segments:
            slabs = outs[n_dx + 3]
            for c in range(d // HEAD_LANES):
                slabs[c] = d_x[:, c * HEAD_LANES:(c + 1) * HEAD_LANES]
            for k, here in enumerate([i < ncb, i >= ncb]):
                @pl.when(here)
                def _(k=k):
                    for c in range(d // HEAD_LANES):
                        for seg in range(N_SEG):
                            dx_refs[k][seg, :, c * HEAD_LANES:(c + 1) * HEAD_LANES] = (
                                slabs[c, pl.ds(seg, tj, stride=N_SEG), :])
        else:
            dx_refs[0][...] = d_x

        @pl.when(i == 0)
        def _():
            dg_ref[...] = jnp.zeros_like(dg_ref)

        @pl.when((i == 0) | (i == ncb))
        def _():
            dsc_ref[...] = jnp.zeros_like(dsc_ref)
            dsh_ref[...] = jnp.zeros_like(dsh_ref)

        dg_ref[0] += d_g
        dsc_ref[0] += d_sc
        dsh_ref[0] += d_sh

    def a_spec(a, first):
        skip = first // tm
        return pl.BlockSpec((tm, a.shape[1]), lambda i: (jnp.maximum(i - skip, 0), 0))

    dx_skip = ncb if latent_dx_only else 0
    if dx_from_segments:
        dx_shapes = [jax.ShapeDtypeStruct((N_SEG, n_ctx // N_SEG, d), F32),
                     jax.ShapeDtypeStruct((N_SEG, (n - n_ctx) // N_SEG, d), F32)]
        dx_specs = [pl.BlockSpec((N_SEG, tj, d), lambda i: (0, 0, 0)),
                    pl.BlockSpec((N_SEG, tj, d), lambda i: (0, jnp.maximum(i - ncb, 0), 0))]
    else:
        dx_shapes = [jax.ShapeDtypeStruct((n - dx_skip * tm, d), F32)]
        dx_specs = [pl.BlockSpec((tm, d), lambda i: (jnp.maximum(i - dx_skip, 0), 0))]
    add_specs = (_parts_specs(adds, tm, ncb) if na == 2 else
                 [pl.BlockSpec((tm, d), lambda i: (jnp.maximum(i - add_skip, 0), 0))])
    res = pl.pallas_call(
        body,
        out_shape=dx_shapes + [jax.ShapeDtypeStruct(g.shape, F32), jax.ShapeDtypeStruct(sc.shape, F32),
                               jax.ShapeDtypeStruct(sh.shape, F32)],
        grid=(n // tm,),
        in_specs=[a_spec(a, first) for a, _, first in terms]
        + [pl.BlockSpec(w.shape, lambda i: (0, 0))] + _parts_specs(xs, tm, ncb)
        + [_bc_spec(g, ncb), _bc_spec(sc, ncb), _bc_spec(sh, ncb)] + add_specs,
        out_specs=dx_specs + [_bc_spec(g, ncb), _bc_spec(sc, ncb), _bc_spec(sh, ncb)],
        scratch_shapes=[pltpu.VMEM((d // HEAD_LANES, tm, HEAD_LANES), F32)] if dx_from_segments else [],
        compiler_params=_params(("arbitrary",), VMEM_LIMIT), name=name)(
            *[t[0] for t in terms], w, *xs, g, sc, sh, *adds)
    if dx_from_segments:
        return ((res[0].reshape(n_ctx, d), res[1].reshape(n - n_ctx, d)), *res[2:])
    return res


def mla_post_fwd(o, p0, x0, gate, w_out, n_ctx, name="l0_post"):
    n, d = o.shape
    xs, _ = _as_parts(x0, n_ctx)
    tm = math.gcd(ROW_TILE, n, n_ctx)
    ncb = n_ctx // tm
    nx = len(xs)

    def body(o_ref, z_ref, *refs):
        gt_ref, w_ref, x1_ref, og_ref, out_ref = refs[nx:]
        og = f_gate(o_ref[...], z_ref[...])[0].astype(BF16)
        out = jnp.dot(og, w_ref[...], preferred_element_type=F32)
        og_ref[...] = og
        out_ref[...] = out
        x1_ref[...] = _parts_tile(refs[:nx], ncb) + gt_ref[0] * out

    row = pl.BlockSpec((tm, d), lambda i: (i, 0))
    return pl.pallas_call(
        body, out_shape=[jax.ShapeDtypeStruct((n, d), F32), jax.ShapeDtypeStruct((n, d), BF16),
                         jax.ShapeDtypeStruct((n, d), F32)],
        grid=(n // tm,),
        in_specs=[row, row] + _parts_specs(xs, tm, ncb) + [_bc_spec(gate, ncb), pl.BlockSpec((d, d), lambda i: (0, 0))],
        out_specs=[row, row, row],
        compiler_params=_params(("parallel",), VMEM_LIMIT), name=name)(o, p0, *xs, gate, w_out)


def mla_post_bwd(dx1, out, og, o, p0, gate, w_out, n_ctx, name="l0_post_bwd"):
    n, d = o.shape
    dxs, _ = _as_parts(dx1, n_ctx)
    tm = math.gcd(ROW_TILE, n, n_ctx)
    ncb = n_ctx // tm
    nx = len(dxs)

    def body(*refs):
        out_ref, og_ref, o_ref, z_ref, gt_ref, w_ref, do_ref, dz_ref, dgt_ref, dw_ref = refs[nx:]
        i = pl.program_id(0)

        @pl.when(i == 0)
        def _():
            dw_ref[...] = jnp.zeros_like(dw_ref)

        @pl.when((i == 0) | (i == ncb))
        def _():
            dgt_ref[...] = jnp.zeros_like(dgt_ref)

        dx = _parts_tile(refs[:nx], ncb)
        dgt_ref[0] += jnp.sum(dx * out_ref[...], axis=0, keepdims=True)
        d_out16 = (gt_ref[0] * dx).astype(BF16)
        dw_ref[...] += lax.dot_general(og_ref[...], d_out16, (((0,), (0,)), ((), ())), preferred_element_type=F32)
        d_og = lax.dot_general(d_out16, w_ref[...], NT_DIMS, preferred_element_type=F32)
        _, gate_vjp = jax.vjp(lambda o_, z_: f_gate(o_, z_), o_ref[...], z_ref[...])
        d_o, d_z = gate_vjp((d_og,))
        do_ref[...] = d_o
        dz_ref[...] = d_z

    row = pl.BlockSpec((tm, d), lambda i: (i, 0))
    mat = pl.BlockSpec((d, d), lambda i: (0, 0))
    return pl.pallas_call(
        body, out_shape=[jax.ShapeDtypeStruct((n, d), F32), jax.ShapeDtypeStruct((n, d), F32),
                         jax.ShapeDtypeStruct(gate.shape, F32), jax.ShapeDtypeStruct((d, d), F32)],
        grid=(n // tm,),
        in_specs=_parts_specs(dxs, tm, ncb) + [row, row, row, row, _bc_spec(gate, ncb), mat],
        out_specs=[row, row, _bc_spec(gate, ncb), mat],
        compiler_params=_params(("arbitrary",), VMEM_LIMIT), name=name)(*dxs, out, og, o, p0, gate, w_out)


def s5_tail(y_ssm, p1, x1p, target, n_ctx, d_vec, b_glu, gate, final_g, w_glu, w_out, name="l1_tail"):
    n, d = y_ssm.shape
    tm = math.gcd(ROW_TILE, n, n_ctx)
    off = n_ctx // tm
    tn_dims = (((0,), (0,)), ((), ()))

    def row_loss(x, g, t):
        e = _rms(x) * g - t
        return 0.5 * (e * e) * (1.0 / d)

    def body(y_ref, u_ref, z_ref, x1_ref, t_ref, d_ref, b_ref, gt_ref, fg_ref, wg_ref, wo_ref,
             l_ref, dx_ref, dy_ref, du_ref, dz_ref, dfg_ref, dgt_ref, db_ref, dd_ref, dwg_ref, dwo_ref):
        @pl.when(pl.program_id(0) == 0)
        def _():
            for r in (l_ref, dfg_ref, dgt_ref, db_ref, dd_ref, dwg_ref, dwo_ref):
                r[...] = jnp.zeros_like(r)

        u, z, tgt, gt = u_ref[...], z_ref[...], t_ref[...], gt_ref[...]
        (ya,), act_vjp = jax.vjp(lambda y_, u_, d_: f_s5_act(y_, u_, d_), y_ref[...], u, d_ref[...])
        ya16 = ya.astype(BF16)
        gl = jnp.dot(ya16, wg_ref[...], preferred_element_type=F32)
        (y3,), glu_vjp = jax.vjp(lambda a_, g_, z_, b_: f_s5_glu(a_, g_, z_, b_), ya, gl, z, b_ref[...])
        y3_16 = y3.astype(BF16)
        out1 = jnp.dot(y3_16, wo_ref[...], preferred_element_type=F32)
        lterm, loss_vjp = jax.vjp(lambda x_, g_: row_loss(x_, g_, tgt), x1_ref[...] + gt * out1, fg_ref[...])
        dx2, dfg = loss_vjp(jnp.ones_like(lterm))
        l_ref[...] += jnp.sum(lterm, axis=0, keepdims=True)
        dfg_ref[...] += dfg
        dx_ref[...] = dx2
        dgt_ref[...] += jnp.sum(dx2 * out1, axis=0, keepdims=True)
        d_out16 = (gt * dx2).astype(BF16)
        dwo_ref[...] += lax.dot_general(y3_16, d_out16, tn_dims, preferred_element_type=F32)
        d_y3 = lax.dot_general(d_out16, wo_ref[...], NT_DIMS, preferred_element_type=F32)
        d_ya, d_gl, d_z, d_b = glu_vjp((d_y3,))
        dz_ref[...] = d_z
        db_ref[...] += d_b
        d_gl16 = d_gl.astype(BF16)
        dwg_ref[...] += lax.dot_general(ya16, d_gl16, tn_dims, preferred_element_type=F32)
        d_ya = d_ya + lax.dot_general(d_gl16, wg_ref[...], NT_DIMS, preferred_element_type=F32)
        d_y, d_u, d_d = act_vjp((d_ya,))
        dy_ref[...] = d_y
        du_ref[...] = d_u
        dd_ref[...] += d_d

    row = pl.BlockSpec((tm, d), lambda i: (i, 0))
    vecs = pl.BlockSpec((1, d), lambda i: (0, 0))
    mat = pl.BlockSpec((d, d), lambda i: (0, 0))
    return pl.pallas_call(
        body,
        out_shape=[jax.ShapeDtypeStruct((1, d), F32)] + [jax.ShapeDtypeStruct((n, d), F32)] * 4
        + [jax.ShapeDtypeStruct((1, d), F32)] * 4 + [jax.ShapeDtypeStruct((d, d), F32)] * 2,
        grid=(n // tm,),
        in_specs=[row, pl.BlockSpec((tm, d), lambda i: (i + off, 0)), pl.BlockSpec((tm, d), lambda i: (i + off, 1)),
                  pl.BlockSpec((tm, d), lambda i: (i + off, 0)), row, vecs, vecs, vecs, vecs, mat, mat],
        out_specs=[vecs, row, row, row, row, vecs, vecs, vecs, vecs, mat, mat],
        compiler_params=_params(("arbitrary",), VMEM_LIMIT), name=name)(
            y_ssm, p1, p1, x1p, target, d_vec, b_glu, gate, final_g, w_glu, w_out)


NT_DIMS = (((1,), (1,)), ((), ()))
HEAD_LANES = 128
N_PAIRS = MLA_HEADS // 2


def _own_lanes(shape, hh):
    lane = lax.broadcasted_iota(jnp.int32, shape, len(shape) - 1)
    return (lane < V_HEAD_DIM) if hh == 0 else (lane >= V_HEAD_DIM)


def _delta_lane(hh):
    return V_HEAD_DIM if hh == 0 else 0


def _rope_tiles(x, cos, sin_next, sin_prev, inverse):
    width = x.shape[-1]
    reps = width // HEAD_LANES
    c, sn, sp = (jnp.tile(t, (1, reps)) for t in (cos, sin_next, sin_prev))
    if inverse:
        return x * c + pltpu.roll(x * sn, 8, 1) + pltpu.roll(x * sp, width - 8, 1)
    return x * c + pltpu.roll(x, width - 8, 1) * sn + pltpu.roll(x, 8, 1) * sp


STAT_LANE = QK_DIM


def _with_stat(x16, col, lane0):
    hi = col.astype(BF16)
    r1 = col - hi.astype(F32)
    mid = r1.astype(BF16)
    lo = (r1 - mid.astype(F32)).astype(BF16)
    lane = lax.broadcasted_iota(jnp.int32, x16.shape, 1)
    return jnp.where(lane == lane0, hi, jnp.where(lane == lane0 + 1, mid, jnp.where(lane == lane0 + 2, lo, x16)))


def attn_fwd(qb, kb, vb, n_ctx):
    T = qb.shape[0]
    tq = math.gcd(ROW_TILE, n_ctx)
    nq, ncb = T // tq, n_ctx // tq

    def body(q_ref, k_ref, v_ref, o_ref, lse_ref):
        qi = pl.program_id(1)

        def rows(n_keys):
            v = v_ref[:n_keys, :]
            outs = []
            for hh in range(2):
                hs = slice(hh * HEAD_LANES, (hh + 1) * HEAD_LANES)
                s = lax.dot_general(q_ref[:, hs], k_ref[:n_keys, hs], NT_DIMS,
                                    preferred_element_type=F32) * SOFTMAX_SCALE
                m = jnp.max(s, axis=-1, keepdims=True)
                p = jnp.exp(s - m)
                l = jnp.sum(p, axis=-1, keepdims=True)
                outs.append(jnp.dot(p.astype(BF16), v, preferred_element_type=F32) / l)
                lse_ref[hh] = m + jnp.log(l)
            o_ref[...] = jnp.where(_own_lanes(outs[0].shape, 0), outs[0], outs[1])

        pl.when(qi < ncb)(lambda: rows(n_ctx))
        pl.when(qi >= ncb)(lambda: rows(T))

    return pl.pallas_call(
        body,
        out_shape=[jax.ShapeDtypeStruct((T, MLA_HEADS * V_HEAD_DIM), F32),
                   jax.ShapeDtypeStruct((MLA_HEADS, T, 1), F32)],
        grid=(N_PAIRS, nq),
        in_specs=[pl.BlockSpec((tq, 2 * HEAD_LANES), lambda h, i: (i, h)),
                  pl.BlockSpec((T, 2 * HEAD_LANES), lambda h, i: (0, h)),
                  pl.BlockSpec((T, 2 * V_HEAD_DIM), lambda h, i: (0, h))],
        out_specs=[pl.BlockSpec((tq, 2 * V_HEAD_DIM), lambda h, i: (i, h)),
                   pl.BlockSpec((2, tq, 1), lambda h, i: (h, i, 0))],
        compiler_params=_params(("parallel", "parallel"), VMEM_LIMIT), name="attn_fwd")(qb, kb, vb)


def attn_bwd_dq(qb, kb, vb, o, do, lse, tabs, n_ctx):
    T = qb.shape[0]
    tq = math.gcd(ROW_TILE, n_ctx)
    nq, ncb = T // tq, n_ctx // tq

    def body(q_ref, k_ref, v_ref, o_ref, do_ref, lse_ref, c_ref, sn_ref, sp_ref, dq_ref, dos_ref):
        qi = pl.program_id(1)

        def rows(n_keys):
            v = v_ref[:n_keys, :]
            dqs = []
            for hh in range(2):
                hs = slice(hh * HEAD_LANES, (hh + 1) * HEAD_LANES)
                k = k_ref[:n_keys, hs]
                do = jnp.where(_own_lanes(do_ref.shape, hh), do_ref[...], 0.0)
                delta = jnp.sum(do * o_ref[...], axis=-1, keepdims=True)
                s = lax.dot_general(q_ref[:, hs], k, NT_DIMS, preferred_element_type=F32) * SOFTMAX_SCALE
                p = jnp.exp(s - lse_ref[hh])
                do16 = do.astype(BF16)
                dp = lax.dot_general(do16, v, NT_DIMS, preferred_element_type=F32)
                ds = p * (dp - delta) * SOFTMAX_SCALE
                dqs.append(jnp.dot(ds.astype(BF16), k, preferred_element_type=F32))
                dos_ref[:, hs] = _with_stat(do16, delta, _delta_lane(hh))
            dq = jnp.concatenate(dqs, axis=1)
            dq_ref[...] = _rope_tiles(dq, c_ref[...], sn_ref[...], sp_ref[...], True).astype(BF16)

        pl.when(qi < ncb)(lambda: rows(n_ctx))
        pl.when(qi >= ncb)(lambda: rows(T))

    tab = pl.BlockSpec((tq, HEAD_LANES), lambda h, i: (i, 0))
    return pl.pallas_call(
        body,
        out_shape=[jax.ShapeDtypeStruct((T, MLA_HEADS * HEAD_LANES), BF16)] * 2,
        grid=(N_PAIRS, nq),
        in_specs=[pl.BlockSpec((tq, 2 * HEAD_LANES), lambda h, i: (i, h)),
                  pl.BlockSpec((T, 2 * HEAD_LANES), lambda h, i: (0, h)),
                  pl.BlockSpec((T, 2 * V_HEAD_DIM), lambda h, i: (0, h)),
                  pl.BlockSpec((tq, 2 * V_HEAD_DIM), lambda h, i: (i, h)),
                  pl.BlockSpec((tq, 2 * V_HEAD_DIM), lambda h, i: (i, h)),
                  pl.BlockSpec((2, tq, 1), lambda h, i: (h, i, 0)), tab, tab, tab],
        out_specs=[pl.BlockSpec((tq, 2 * HEAD_LANES), lambda h, i: (i, h))] * 2,
        compiler_params=_params(("parallel", "parallel"), VMEM_LIMIT), name="attn_bwd_dq")(
            qb, kb, vb, o, do, lse, *tabs)


def attn_bwd(qb, kb, vb, o, do, lse, tabs, n_ctx):
    T = qb.shape[0]
    tq = math.gcd(ROW_TILE, n_ctx)
    nq, ncb = T // tq, n_ctx // tq
    tn = (((0,), (0,)), ((), ()))

    def body(q_ref, k_ref, v_ref, o_ref, do_ref, lse_ref, c_ref, sn_ref, sp_ref, dq_ref, dk_ref, dv_ref, dkt, dvt):
        qi = pl.program_id(1)

        @pl.when(qi == 0)
        def _():
            dkt[...] = jnp.zeros_like(dkt)
            dvt[...] = jnp.zeros_like(dvt)

        def rows(n_keys):
            v = v_ref[:n_keys, :]
            dqs = []
            for hh in range(2):
                hs = slice(hh * HEAD_LANES, (hh + 1) * HEAD_LANES)
                k = k_ref[:n_keys, hs]
                q = q_ref[:, hs]
                do = jnp.where(_own_lanes(do_ref.shape, hh), do_ref[...], 0.0)
                delta = jnp.sum(do * o_ref[...], axis=-1, keepdims=True)
                s = lax.dot_general(q, k, NT_DIMS, preferred_element_type=F32) * SOFTMAX_SCALE
                p = jnp.exp(s - lse_ref[hh])
                do16 = do.astype(BF16)
                dp = lax.dot_general(do16, v, NT_DIMS, preferred_element_type=F32)
                ds16 = (p * (dp - delta) * SOFTMAX_SCALE).astype(BF16)
                dqs.append(jnp.dot(ds16, k, preferred_element_type=F32))
                dkt[hs, :n_keys] += lax.dot_general(q, ds16, tn, preferred_element_type=F32)
                dvt[:, :n_keys] += lax.dot_general(do16, p.astype(BF16), tn, preferred_element_type=F32)
            dq = jnp.concatenate(dqs, axis=1)
            dq_ref[...] = _rope_tiles(dq, c_ref[...], sn_ref[...], sp_ref[...], True).astype(BF16)

        pl.when(qi < ncb)(lambda: rows(n_ctx))
        pl.when(qi >= ncb)(lambda: rows(T))

        @pl.when(qi == nq - 1)
        def _():
            dk_ref[...] = dkt[...].T
            dv_ref[...] = dvt[...].T

    tab = pl.BlockSpec((tq, HEAD_LANES), lambda h, i: (i, 0))
    return pl.pallas_call(
        body,
        out_shape=[jax.ShapeDtypeStruct((T, MLA_HEADS * HEAD_LANES), BF16),
                   jax.ShapeDtypeStruct((T, MLA_HEADS * HEAD_LANES), F32),
                   jax.ShapeDtypeStruct((T, MLA_HEADS * V_HEAD_DIM), F32)],
        grid=(N_PAIRS, nq),
        in_specs=[pl.BlockSpec((tq, 2 * HEAD_LANES), lambda h, i: (i, h)),
                  pl.BlockSpec((T, 2 * HEAD_LANES), lambda h, i: (0, h)),
                  pl.BlockSpec((T, 2 * V_HEAD_DIM), lambda h, i: (0, h)),
                  pl.BlockSpec((tq, 2 * V_HEAD_DIM), lambda h, i: (i, h)),
                  pl.BlockSpec((tq, 2 * V_HEAD_DIM), lambda h, i: (i, h)),
                  pl.BlockSpec((2, tq, 1), lambda h, i: (h, i, 0)), tab, tab, tab],
        out_specs=[pl.BlockSpec((tq, 2 * HEAD_LANES), lambda h, i: (i, h)),
                   pl.BlockSpec((T, 2 * HEAD_LANES), lambda h, i: (0, h)),
                   pl.BlockSpec((T, 2 * V_HEAD_DIM), lambda h, i: (0, h))],
        scratch_shapes=[pltpu.VMEM((2 * HEAD_LANES, T), F32), pltpu.VMEM((2 * V_HEAD_DIM, T), F32)],
        compiler_params=_params(("parallel", "arbitrary"), VMEM_LIMIT), name="attn_bwd")(
            qb, kb, vb, o, do, lse, *tabs)


def attn_bwd_dkv(qs, kb, vb, dos, n_ctx):
    T = qs.shape[0]
    tq = math.gcd(ROW_TILE, n_ctx)
    nq, ncb = T // tq, n_ctx // tq

    def body(q_ref, do_ref, k_ref, v_ref, dk_ref, dv_ref):
        kj = pl.program_id(1)

        def cols(first):
            v = v_ref[...]
            lane = lax.broadcasted_iota(jnp.int32, v.shape, 1)
            dvs = []
            for hh in range(2):
                hs = slice(hh * HEAD_LANES, (hh + 1) * HEAD_LANES)
                q = q_ref[first:, hs]
                do16 = do_ref[first:, hs]
                in_delta = (lane >= _delta_lane(hh)) & (lane < _delta_lane(hh) + 3)
                v_minus = jnp.where(in_delta, -jnp.ones_like(v), v)
                pt = jnp.exp(lax.dot_general(k_ref[:, hs], q, NT_DIMS, preferred_element_type=F32) * SOFTMAX_SCALE)
                dvs.append(jnp.dot(pt.astype(BF16), do16, preferred_element_type=F32))
                dst = pt * lax.dot_general(v_minus, do16, NT_DIMS, preferred_element_type=F32) * SOFTMAX_SCALE
                dk_ref[:, hs] = jnp.dot(dst.astype(BF16), q, preferred_element_type=F32)
            dv_ref[...] = jnp.where(_own_lanes(dvs[0].shape, 0), dvs[0], dvs[1])

        pl.when(kj < ncb)(lambda: cols(0))
        pl.when(kj >= ncb)(lambda: cols(n_ctx))

    return pl.pallas_call(
        body,
        out_shape=[jax.ShapeDtypeStruct((T, MLA_HEADS * HEAD_LANES), F32),
                   jax.ShapeDtypeStruct((T, MLA_HEADS * V_HEAD_DIM), F32)],
        grid=(N_PAIRS, nq),
        in_specs=[pl.BlockSpec((T, 2 * HEAD_LANES), lambda h, j: (0, h)),
                  pl.BlockSpec((T, 2 * HEAD_LANES), lambda h, j: (0, h)),
                  pl.BlockSpec((tq, 2 * HEAD_LANES), lambda h, j: (j, h)),
                  pl.BlockSpec((tq, 2 * V_HEAD_DIM), lambda h, j: (j, h))],
        out_specs=[pl.BlockSpec((tq, 2 * HEAD_LANES), lambda h, j: (j, h)),
                   pl.BlockSpec((tq, 2 * V_HEAD_DIM), lambda h, j: (j, h))],
        compiler_params=_params(("parallel", "parallel"), VMEM_LIMIT), name="attn_bwd_dkv")(
            qs, dos, kb, vb)


def _split_bf16(x):
    hi = x.astype(BF16)
    return hi, (x - hi.astype(F32)).astype(BF16)


def q_heads(cq, gain, w_uq_p, tabs, name="l0_uq"):
    T, K = cq.arr.shape[0], cq.width
    N = w_uq_p.shape[1]
    tm = math.gcd(ROW_TILE, T)

    def body(a_ref, g_ref, w_ref, c_ref, sn_ref, sp_ref, o_ref, n_ref):
        qn = f_rms(a_ref[...], g_ref[0])[0].astype(BF16)
        n_ref[...] = qn
        acc = jnp.dot(qn, w_ref[...], preferred_element_type=F32)
        o_ref[...] = _rope_tiles(acc, c_ref[...], sn_ref[...], sp_ref[...], False).astype(BF16)

    tab = pl.BlockSpec((tm, HEAD_LANES), lambda i: (i, 0))
    return pl.pallas_call(
        body, out_shape=[jax.ShapeDtypeStruct((T, N), BF16), jax.ShapeDtypeStruct((T, K), BF16)], grid=(T // tm,),
        in_specs=[cq.spec(tm), pl.BlockSpec((1, 1, K), lambda i: (0, 0, 0)), pl.BlockSpec((K, N), lambda i: (0, 0)),
                  tab, tab, tab],
        out_specs=[pl.BlockSpec((tm, N), lambda i: (i, 0)), pl.BlockSpec((tm, K), lambda i: (i, 0))],
        compiler_params=_params(("parallel",), VMEM_LIMIT), name=name)(cq.arr, gain, w_uq_p, *tabs)


def kv_heads(ckv, gain, w_kn_p, w_v, kr, spread, tabs, name="l0_ukv"):
    T, K = ckv.arr.shape[0], ckv.width
    N = w_kn_p.shape[1]
    NV = w_v.shape[1]
    tm = math.gcd(ROW_TILE, T)

    def body(a_ref, g_ref, wk_ref, wv_ref, kr_ref, e_ref, c_ref, sn_ref, sp_ref, k_ref, v_ref, n_ref):
        a = f_rms(a_ref[...], g_ref[0])[0].astype(BF16)
        n_ref[...] = a
        hi, lo = _split_bf16(kr_ref[...])
        acc = (jnp.dot(a, wk_ref[...], preferred_element_type=F32)
               + jnp.dot(hi, e_ref[...], preferred_element_type=F32)
               + jnp.dot(lo, e_ref[...], preferred_element_type=F32))
        roped = _rope_tiles(acc, c_ref[...], sn_ref[...], sp_ref[...], False)
        lane = lax.broadcasted_iota(jnp.int32, roped.shape, 1) % HEAD_LANES
        k_ref[...] = jnp.where((lane >= STAT_LANE) & (lane < STAT_LANE + 3), 1.0, roped).astype(BF16)
        v_ref[...] = jnp.dot(a, wv_ref[...], preferred_element_type=F32).astype(BF16)

    tab = pl.BlockSpec((tm, HEAD_LANES), lambda i: (i, 0))
    return pl.pallas_call(
        body, out_shape=[jax.ShapeDtypeStruct((T, N), BF16), jax.ShapeDtypeStruct((T, NV), BF16),
                         jax.ShapeDtypeStruct((T, K), BF16)], grid=(T // tm,),
        in_specs=[ckv.spec(tm), pl.BlockSpec((1, 1, K), lambda i: (0, 0, 0)), pl.BlockSpec((K, N), lambda i: (0, 0)),
                  pl.BlockSpec((K, NV), lambda i: (0, 0)), kr.spec(tm),
                  pl.BlockSpec((kr.width, N), lambda i: (0, 0)), tab, tab, tab],
        out_specs=[pl.BlockSpec((tm, N), lambda i: (i, 0)), pl.BlockSpec((tm, NV), lambda i: (i, 0)),
                   pl.BlockSpec((tm, K), lambda i: (i, 0))],
        compiler_params=_params(("parallel",), VMEM_LIMIT), name=name)(
            ckv.arr, gain, w_kn_p, w_v, kr.arr, spread, *tabs)


def heads_unrope(d, tabs, spread=None, name="unrope"):
    T, N = d.shape
    tm = math.gcd(ROW_TILE, T)

    def body(*refs):
        if spread is None:
            d_ref, c_ref, sn_ref, sp_ref, o_ref = refs
        else:
            d_ref, c_ref, sn_ref, sp_ref, e_ref, o_ref, kr_ref = refs
        g = _rope_tiles(d_ref[...], c_ref[...], sn_ref[...], sp_ref[...], True)
        o_ref[...] = g.astype(BF16)
        if spread is not None:
            hi, lo = _split_bf16(g)
            kr_ref[...] = (lax.dot_general(hi, e_ref[...], NT_DIMS, preferred_element_type=F32)
                           + lax.dot_general(lo, e_ref[...], NT_DIMS, preferred_element_type=F32))

    tab = pl.BlockSpec((tm, HEAD_LANES), lambda i: (i, 0))
    row = pl.BlockSpec((tm, N), lambda i: (i, 0))
    ins, in_specs = [d, *tabs], [row, tab, tab, tab]
    out_shape, out_specs = [jax.ShapeDtypeStruct((T, N), BF16)], [row]
    if spread is not None:
        ins.append(spread)
        in_specs.append(pl.BlockSpec(spread.shape, lambda i: (0, 0)))
        out_shape.append(jax.ShapeDtypeStruct((T, spread.shape[0]), F32))
        out_specs.append(pl.BlockSpec((tm, spread.shape[0]), lambda i: (i, 0)))
    return pl.pallas_call(
        body, out_shape=out_shape, grid=(T // tm,), in_specs=in_specs, out_specs=out_specs,
        compiler_params=_params(("parallel",), VMEM_LIMIT), name=name)(*ins)


def _cmul(ar, ai, br, bi):
    return ar * br - ai * bi, ar * bi + ai * br


def s5_chain(finals, s0, a, n_steps, reverse, name):
    W = finals.shape[-1]
    first = N_SEG - 1 if reverse else 0

    def body(f_ref, s0_ref, a_ref, c_ref):
        pr, pi = jnp.ones((1, W), F32), jnp.zeros((1, W), F32)
        br, bi = a_ref[0], a_ref[1]
        n = n_steps
        while n:
            if n & 1:
                pr, pi = _cmul(pr, pi, br, bi)
            br, bi = _cmul(br, bi, br, bi)
            n >>= 1
        fr, fi = f_ref[0], f_ref[1]
        row = lax.broadcasted_iota(jnp.int32, (N_SEG, W), 0)
        s0r = jnp.broadcast_to(s0_ref[0], (N_SEG, W))
        s0i = jnp.broadcast_to(s0_ref[1], (N_SEG, W))
        cr = jnp.where(row == first, s0r, 0.0)
        ci = jnp.where(row == first, s0i, 0.0)
        shift = N_SEG - 1 if reverse else 1
        for _ in range(N_SEG - 1):
            mr, mi = _cmul(pr, pi, cr, ci)
            tr = pltpu.roll(fr + mr, shift, 0)
            ti = pltpu.roll(fi + mi, shift, 0)
            cr = jnp.where(row == first, s0r, tr)
            ci = jnp.where(row == first, s0i, ti)
        c_ref[0] = cr
        c_ref[1] = ci

    return pl.pallas_call(body, out_shape=jax.ShapeDtypeStruct((2, N_SEG, W), F32), name=name)(finals, s0, a)


def _scan_chunk(bur, bui, st_ref, a_ref, n_steps, reverse):
    for lc in range(S5_LANES // BLK_ST):
        sl = slice(lc * BLK_ST, (lc + 1) * BLK_ST)
        lr = jnp.broadcast_to(a_ref[0, :, sl], (N_SEG, BLK_ST))
        li = jnp.broadcast_to(a_ref[1, :, sl], (N_SEG, BLK_ST))

        def step(jj, carry, sl=sl, lr=lr, li=li):
            sr, si = carry
            j = (n_steps - 1 - jj) if reverse else jj
            r0 = pl.multiple_of(j * N_SEG, N_SEG)
            nr = lr * sr - li * si + bur[pl.ds(r0, N_SEG), sl]
            ni = lr * si + li * sr + bui[pl.ds(r0, N_SEG), sl]
            bur[pl.ds(r0, N_SEG), sl] = nr
            bui[pl.ds(r0, N_SEG), sl] = ni
            return nr, ni

        sr, si = lax.fori_loop(0, n_steps, step, (st_ref[0, :, sl], st_ref[1, :, sl]))
        st_ref[0, :, sl] = sr
        st_ref[1, :, sl] = si


def _project_in(x16, w_re, w_im, bur, bui, adjoint):
    for gb in range(N_BLOCKS):
        xb = x16[:, gb * BLK_CH:(gb + 1) * BLK_CH]
        sl = slice(gb * BLK_ST, (gb + 1) * BLK_ST)
        if adjoint:
            dn = (((1,), (1,)), ((), ()))
            bur[:, sl] = lax.dot_general(xb, w_re[gb], dn, preferred_element_type=F32)
            bui[:, sl] = -lax.dot_general(xb, w_im[gb], dn, preferred_element_type=F32)
        else:
            bur[:, sl] = jnp.dot(xb, w_re[gb], preferred_element_type=F32)
            bui[:, sl] = jnp.dot(xb, w_im[gb], preferred_element_type=F32)


def s5_scan(act, w_re, w_im, a, init, *, reverse, adjoint=False, c_re=None, c_im=None, add=None,
            want_ckpt=False, rows=None, name):
    act_off, N = rows if rows is not None else (0, act.shape[0])
    R = math.gcd(ROW_TILE, N, act_off)
    nch, jc = N // R, R // N_SEG
    with_out = c_re is not None

    def chunk(i):
        return (nch - 1 - i) if reverse else i

    def body(*refs):
        act_ref, wre_ref, wim_ref, a_ref, init_ref = refs[:5]
        k = 5
        if with_out:
            cre_ref, cim_ref = refs[k:k + 2]
            k += 2
        if add is not None:
            add_ref = refs[k]
            k += 1
        if with_out:
            out_ref = refs[k]
            k += 1
        if want_ckpt:
            ck_ref = refs[k]
            k += 1
        fin_ref, bur, bui = refs[k:k + 3]

        @pl.when(pl.program_id(0) == 0)
        def _():
            fin_ref[...] = init_ref[...]

        if want_ckpt:
            ck_ref[0] = fin_ref[...]
        _project_in(act_ref[...].astype(BF16), wre_ref, wim_ref, bur, bui, adjoint)
        _scan_chunk(bur, bui, fin_ref, a_ref, jc, reverse)
        if with_out:
            for gb in range(N_BLOCKS):
                sl = slice(gb * BLK_ST, (gb + 1) * BLK_ST)
                y = (jnp.dot(bur[:, sl].astype(BF16), cre_ref[gb], preferred_element_type=F32)
                     - jnp.dot(bui[:, sl].astype(BF16), cim_ref[gb], preferred_element_type=F32))
                cs = slice(gb * BLK_CH, (gb + 1) * BLK_CH)
                if add is not None:
                    y = y + add_ref[:, cs]
                out_ref[:, cs] = y

    row_spec = pl.BlockSpec((R, D_MODEL), lambda i: (chunk(i), 0))
    act_spec = pl.BlockSpec((R, D_MODEL), lambda i: (chunk(i) + act_off // R, 0))
    w_spec = pl.BlockSpec(w_re.shape, lambda i: (0, 0, 0))
    st_spec = pl.BlockSpec((2, N_SEG, S5_LANES), lambda i: (0, 0, 0))
    ins = [act, w_re, w_im, a, init]
    in_specs = [act_spec, w_spec, w_spec, pl.BlockSpec((2, 1, S5_LANES), lambda i: (0, 0, 0)), st_spec]
    if with_out:
        ins += [c_re, c_im]
        in_specs += [pl.BlockSpec(c_re.shape, lambda i: (0, 0, 0))] * 2
    if add is not None:
        ins.append(add)
        in_specs.append(row_spec)
    out_shape, out_specs = [], []
    if with_out:
        out_shape.append(jax.ShapeDtypeStruct((N, D_MODEL), F32))
        out_specs.append(row_spec)
    if want_ckpt:
        out_shape.append(jax.ShapeDtypeStruct((nch, 2, N_SEG, S5_LANES), F32))
        out_specs.append(pl.BlockSpec((1, 2, N_SEG, S5_LANES), lambda i: (chunk(i), 0, 0, 0)))
    out_shape.append(jax.ShapeDtypeStruct((2, N_SEG, S5_LANES), F32))
    out_specs.append(st_spec)
    res = pl.pallas_call(
        body, out_shape=out_shape, grid=(nch,), in_specs=in_specs, out_specs=out_specs,
        scratch_shapes=[pltpu.VMEM((R, S5_LANES), F32), pltpu.VMEM((R, S5_LANES), F32)],
        compiler_params=_params(("arbitrary",), VMEM_LIMIT), name=name)(*ins)
    res = list(res)
    out = res.pop(0) if with_out else None
    ckpt = res.pop(0) if want_ckpt else None
    return out, ckpt, res[0]


def s5_grads(dy, u, ckpt, b_re, b_im, c_re, c_im, lam, init_adj, *, reverse, add=None, u_off=0, du_rows=None,
             du_into=None, n_rows=None, name):
    N = n_rows if dy is None else dy.shape[0]
    du_total, du_first = du_rows if du_rows is not None else (N, 0)
    R = math.gcd(ROW_TILE, N, u_off, du_first)
    nch, jc = N // R, R // N_SEG
    W = S5_LANES

    def chunk(i):
        return i if reverse else (nch - 1 - i)

    def body(*refs):
        if dy is None:
            refs = (None,) + refs[:4] + (None, None) + refs[4:]
        dy_ref, u_ref, ck_ref, bre_ref, bim_ref, cre_ref, cim_ref, lam_ref, init_ref = refs[:9]
        k = 9
        if add is not None:
            add_ref = refs[k]
            k += 1
        k += du_into is not None
        outs = refs[k:]
        if dy is None:
            outs = outs[:4] + (None, None) + outs[4:]
        du_ref, dlam_ref, dbre_ref, dbim_ref, dcre_ref, dcim_ref, fin_ref = outs[:7]
        sr_buf, si_buf, er_buf, ei_buf, st_buf = outs[7:12]

        @pl.when(pl.program_id(0) == 0)
        def _():
            fin_ref[...] = init_ref[...]
            dlam_ref[...] = jnp.zeros_like(dlam_ref)
            dbre_ref[...] = jnp.zeros_like(dbre_ref)
            dbim_ref[...] = jnp.zeros_like(dbim_ref)
            if dy is not None:
                dcre_ref[...] = jnp.zeros_like(dcre_ref)
                dcim_ref[...] = jnp.zeros_like(dcim_ref)

        u16 = u_ref[...].astype(BF16)
        st_buf[...] = ck_ref[0]
        _project_in(u16, bre_ref, bim_ref, sr_buf, si_buf, False)
        _scan_chunk(sr_buf, si_buf, st_buf, lam_ref, jc, reverse)
        if dy is None:
            er_buf[...] = jnp.zeros_like(er_buf)
            ei_buf[...] = jnp.zeros_like(ei_buf)
        else:
            dy16 = dy_ref[...].astype(BF16)
            _project_in(dy16, cre_ref, cim_ref, er_buf, ei_buf, True)
        for lc in range(W // BLK_ST):
            sl = slice(lc * BLK_ST, (lc + 1) * BLK_ST)
            lr = jnp.broadcast_to(lam_ref[0, :, sl], (N_SEG, BLK_ST))
            li = jnp.broadcast_to(lam_ref[1, :, sl], (N_SEG, BLK_ST))

            def one(r0, spr, spi, carry, sl=sl, lr=lr, li=li):
                gr, gi, ar, ai = carry
                nr = er_buf[pl.ds(r0, N_SEG), sl] + lr * gr + li * gi
                ni = ei_buf[pl.ds(r0, N_SEG), sl] + lr * gi - li * gr
                er_buf[pl.ds(r0, N_SEG), sl] = nr
                ei_buf[pl.ds(r0, N_SEG), sl] = ni
                return nr, ni, ar + spr * nr + spi * ni, ai + spr * ni - spi * nr

            def step(ff, carry, sl=sl, one=one):
                f = jc - 1 - ff
                j = (jc - 1 - f) if reverse else f
                jp = (j + 1) if reverse else (j - 1)
                r0 = pl.multiple_of(j * N_SEG, N_SEG)
                p0 = pl.multiple_of(jp * N_SEG, N_SEG)
                return one(r0, sr_buf[pl.ds(p0, N_SEG), sl], si_buf[pl.ds(p0, N_SEG), sl], carry)

            carry = (fin_ref[0, :, sl], fin_ref[1, :, sl], dlam_ref[0, :, sl], dlam_ref[1, :, sl])
            carry = lax.fori_loop(0, jc - 1, step, carry)
            r_first = (jc - 1) * N_SEG if reverse else 0
            gr, gi, ar, ai = one(r_first, ck_ref[0, 0, :, sl], ck_ref[0, 1, :, sl], carry)
            fin_ref[0, :, sl] = gr
            fin_ref[1, :, sl] = gi
            dlam_ref[0, :, sl] = ar
            dlam_ref[1, :, sl] = ai
        tn = (((0,), (0,)), ((), ()))
        nt = (((1,), (1,)), ((), ()))
        for gb in range(N_BLOCKS):
            sl = slice(gb * BLK_ST, (gb + 1) * BLK_ST)
            cs = slice(gb * BLK_CH, (gb + 1) * BLK_CH)
            gr16 = er_buf[:, sl].astype(BF16)
            gi16 = ei_buf[:, sl].astype(BF16)
            du = (lax.dot_general(gr16, bre_ref[gb], nt, preferred_element_type=F32)
                  + lax.dot_general(gi16, bim_ref[gb], nt, preferred_element_type=F32))
            if add is not None:
                du = du + add_ref[:, cs]
            du_ref[:, cs] = du
            ub = u16[:, cs]
            dbre_ref[gb] += lax.dot_general(ub, gr16, tn, preferred_element_type=F32)
            dbim_ref[gb] += lax.dot_general(ub, gi16, tn, preferred_element_type=F32)
            if dy is not None:
                dyb = dy16[:, cs]
                dcre_ref[gb] += lax.dot_general(sr_buf[:, sl].astype(BF16), dyb, tn, preferred_element_type=F32)
                dcim_ref[gb] -= lax.dot_general(si_buf[:, sl].astype(BF16), dyb, tn, preferred_element_type=F32)

    row_spec = pl.BlockSpec((R, D_MODEL), lambda i: (chunk(i), 0))
    st_spec = pl.BlockSpec((2, N_SEG, W), lambda i: (0, 0, 0))
    wb_spec = pl.BlockSpec(b_re.shape, lambda i: (0, 0, 0))
    wc_spec = pl.BlockSpec(c_re.shape, lambda i: (0, 0, 0))
    ins = [dy, u, ckpt, b_re, b_im, c_re, c_im, lam, init_adj]
    u_spec = pl.BlockSpec((R, D_MODEL), lambda i: (chunk(i) + u_off // R, 0))
    in_specs = [row_spec, u_spec, pl.BlockSpec((1, 2, N_SEG, W), lambda i: (chunk(i), 0, 0, 0)),
                wb_spec, wb_spec, wc_spec, wc_spec, pl.BlockSpec((2, 1, W), lambda i: (0, 0, 0)), st_spec]
    if dy is None:
        ins, in_specs = ins[1:5] + ins[7:], in_specs[1:5] + in_specs[7:]
    if add is not None:
        ins.append(add)
        in_specs.append(row_spec)
    aliases = {}
    if du_into is not None:
        aliases[len(ins)] = 0
        ins.append(du_into)
        in_specs.append(pl.BlockSpec(memory_space=pl.ANY))
    du_spec = pl.BlockSpec((R, D_MODEL), lambda i: (chunk(i) + du_first // R, 0))
    out_shape = [jax.ShapeDtypeStruct((du_total, D_MODEL), F32), jax.ShapeDtypeStruct((2, N_SEG, W), F32),
                 jax.ShapeDtypeStruct(b_re.shape, F32), jax.ShapeDtypeStruct(b_re.shape, F32),
                 jax.ShapeDtypeStruct(c_re.shape, F32), jax.ShapeDtypeStruct(c_re.shape, F32),
                 jax.ShapeDtypeStruct((2, N_SEG, W), F32)]
    out_specs = [du_spec, st_spec, wb_spec, wb_spec, wc_spec, wc_spec, st_spec]
    if dy is None:
        out_shape, out_specs = out_shape[:4] + out_shape[6:], out_specs[:4] + out_specs[6:]
    res = pl.pallas_call(
        body, out_shape=out_shape, grid=(nch,), in_specs=in_specs, out_specs=out_specs, input_output_aliases=aliases,
        scratch_shapes=[pltpu.VMEM((R, W), F32) for _ in range(4)] + [pltpu.VMEM((2, N_SEG, W), F32)],
        compiler_params=_params(("arbitrary",), VMEM_LIMIT), name=name)(*ins)
    return res if dy is not None else [*res[:4], None, None, res[4]]


def adamw(w, g, m, v, name="adamw", after=None):
    n, d = w.shape
    lanes = -(-d // 128) * 128
    tm = n
    while tm * lanes * 4 > (1 << 20) and tm % 16 == 0:
        tm //= 2
    c1 = 1.0 - ADAM_B1 ** ADAM_STEP
    c2 = 1.0 - ADAM_B2 ** ADAM_STEP

    def body(w_ref, g_ref, m_ref, v_ref, *rest):
        d_ref, nm_ref, nv_ref = rest[-3:]
        g_ = g_ref[...]
        m_ = ADAM_B1 * m_ref[...] + (1.0 - ADAM_B1) * g_
        v_ = ADAM_B2 * v_ref[...] + (1.0 - ADAM_B2) * (g_ * g_)
        d_ref[...] = -ADAM_LR * ((m_ / c1) / (jnp.sqrt(v_ / c2) + ADAM_EPS) + ADAM_WD * w_ref[...])
        nm_ref[...] = m_
        nv_ref[...] = v_

    spec = pl.BlockSpec((tm, d), lambda i: (i, 0))
    extra = [] if after is None else [after]
    return pl.pallas_call(
        body, out_shape=[jax.ShapeDtypeStruct((n, d), F32)] * 3, grid=(n // tm,),
        in_specs=[spec] * 4 + [pl.BlockSpec(memory_space=pl.ANY)] * len(extra), out_specs=[spec] * 3,
        compiler_params=_params(("parallel",), VMEM_LIMIT), name=name)(w, g, m, v, *extra)


def _coords():
    return lax.axis_index("x"), lax.axis_index("y"), lax.axis_index("c")


def exchange(arrays, out_shapes, remote, local, name, aliases=None):
    n_in, n_out, n_rem, n_loc = len(arrays), len(out_shapes), len(remote), len(local)

    def at(ref, idx):
        return ref if idx is None else ref.at[idx]

    def body(*refs):
        ins, outs = refs[:n_in], refs[n_in:n_in + n_out]
        send_sems, recv_sems, local_sems = refs[n_in + n_out:]
        me = _coords()
        sends, recvs = [], []
        for k, (flip, ii, src_at, oi, dst_at) in enumerate(remote):
            peer = (me[0] ^ flip[0], me[1] ^ flip[1], me[2] ^ flip[2])
            src = at(ins[ii], src_at(me, peer))
            sends.append(pltpu.make_async_remote_copy(
                src_ref=src, dst_ref=at(outs[oi], dst_at(me)), send_sem=send_sems.at[k], recv_sem=recv_sems.at[k],
                device_id=peer, device_id_type=MESH))
            recvs.append(pltpu.make_async_remote_copy(
                src_ref=src, dst_ref=at(outs[oi], dst_at(peer)), send_sem=send_sems.at[k], recv_sem=recv_sems.at[k],
                device_id=peer, device_id_type=MESH))
        locs = [pltpu.make_async_copy(at(ins[ii], src_at(me)), at(outs[oi], dst_at(me)), local_sems.at[k])
                for k, (ii, src_at, oi, dst_at) in enumerate(local)]
        for cp in locs + sends:
            cp.start()
        for cp in recvs:
            cp.wait_recv()
        for cp in sends:
            cp.wait_send()
        for cp in locs:
            cp.wait()

    hbm = pl.BlockSpec(memory_space=pl.ANY)
    return pl.pallas_call(
        body, out_shape=list(out_shapes), in_specs=[hbm] * n_in, out_specs=[hbm] * n_out,
        scratch_shapes=[pltpu.SemaphoreType.DMA((n_rem,)), pltpu.SemaphoreType.DMA((n_rem,)),
                        pltpu.SemaphoreType.DMA((max(n_loc, 1),))],
        input_output_aliases=aliases or {}, name=name)(*arrays)


ALL_FLIPS = [(dx, dy, dc) for dx in (0, 1) for dy in (0, 1) for dc in (0, 1)][1:]
CHIP_FLIPS = [(1, 0, 0), (0, 1, 0), (1, 1, 0)]
CORE_FLIP = (0, 0, 1)


def _dev_index(p):
    return 4 * p[0] + 2 * p[1] + p[2]


def _chip_index(p):
    return 2 * p[0] + p[1]


def _gather(xs, flips, index, n, name):
    arrays = [x[None] for x in xs]
    outs = [jax.ShapeDtypeStruct((n,) + x.shape, x.dtype) for x in xs]
    remote = [(f, a, lambda me, peer: (0,), a, lambda s: (index(s),)) for a in range(len(xs)) for f in flips]
    local = [(a, lambda me: (0,), a, lambda me: (index(me),)) for a in range(len(xs))]
    return exchange(arrays, outs, remote, local, name)


def allgather_devices(x, name):
    return _gather([x], ALL_FLIPS, _dev_index, N_DEV, name)[0]


def allgather_chips(xs, name):
    return _gather(xs, CHIP_FLIPS, _chip_index, N_CHIP, name)


def gather_halves(xs, name):
    n = len(xs)
    nk = n * len(CHIP_FLIPS)

    def body(*refs):
        ins, outs = refs[:n], refs[n:2 * n]
        ici_send, ici_recv, d2d_send, d2d_recv = refs[2 * n:]
        me = _coords()
        sibling = (me[0], me[1], 1 - me[2])
        first, passed, landed = [], [], []
        for a in range(n):
            half = ins[a].shape[0] // 2
            mine = ins[a].at[pl.ds(pl.multiple_of(me[2] * half, 16), half)]
            for j, flip in enumerate(CHIP_FLIPS):
                k = a * len(CHIP_FLIPS) + j
                peer = (me[0] ^ flip[0], me[1] ^ flip[1], me[2])
                first.append(pltpu.make_async_remote_copy(
                    src_ref=mine, dst_ref=outs[a].at[_chip_index(me), me[2]], send_sem=ici_send.at[k],
                    recv_sem=ici_recv.at[k], device_id=peer, device_id_type=MESH))
                arrived = outs[a].at[_chip_index(peer), me[2]]
                landed.append(pltpu.make_async_remote_copy(
                    src_ref=mine, dst_ref=arrived, send_sem=ici_send.at[k], recv_sem=ici_recv.at[k],
                    device_id=peer, device_id_type=MESH))
                passed.append(pltpu.make_async_remote_copy(
                    src_ref=arrived, dst_ref=arrived, send_sem=d2d_send.at[k], recv_sem=d2d_recv.at[k],
                    device_id=sibling, device_id_type=MESH))
        for cp in first:
            cp.start()
        for k in range(nk):
            landed[k].wait_recv()
            passed[k].start()
        for a in range(n):
            for j, flip in enumerate(CHIP_FLIPS):
                k = a * len(CHIP_FLIPS) + j
                peer_chip = _chip_index((me[0] ^ flip[0], me[1] ^ flip[1]))
                from_sibling = outs[a].at[peer_chip, 1 - me[2]]
                pltpu.make_async_remote_copy(
                    src_ref=from_sibling, dst_ref=from_sibling, send_sem=d2d_send.at[k], recv_sem=d2d_recv.at[k],
                    device_id=sibling, device_id_type=MESH).wait_recv()
        for cp in first + passed:
            cp.wait_send()

    hbm = pl.BlockSpec(memory_space=pl.ANY)
    return pl.pallas_call(
        body, out_shape=[jax.ShapeDtypeStruct((N_CHIP, 2, x.shape[0] // 2, x.shape[1]), x.dtype) for x in xs],
        in_specs=[hbm] * n, out_specs=[hbm] * n,
        scratch_shapes=[pltpu.SemaphoreType.DMA((nk,)) for _ in range(4)], name=name)(*xs)


HBM_SPEC = pl.BlockSpec(memory_space=pltpu.HBM)
SEM_SPEC = pl.BlockSpec(memory_space=pltpu.SEMAPHORE)
DATAFLOW = pltpu.SideEffectType.DATAFLOW_SIDE_EFFECTING


def _at(ref, idx):
    return ref if idx is None else ref.at[idx]


def _peer(me, flip):
    return (me[0] ^ flip[0], me[1] ^ flip[1], me[2] ^ flip[2])


def exchange_start(arrays, land_shapes, remote, name, after=None):
    n_in, n_out, nk = len(arrays), len(land_shapes), len(remote)
    after = list(after or [])
    n_after = len(after)

    def body(*refs):
        srcs, lands = refs[:n_in], refs[n_in:n_in + n_out]
        first_out = n_in + n_out + n_after
        send_sems, recv_sems, token = refs[first_out], refs[first_out + 1], refs[-1]
        me = _coords()
        for k, (flip, ii, src_at, oi, dst_at) in enumerate(remote):
            peer = _peer(me, flip)
            pltpu.make_async_remote_copy(
                src_ref=_at(srcs[ii], src_at(me, peer)), dst_ref=_at(lands[oi], dst_at(me)), send_sem=send_sems.at[k],
                recv_sem=recv_sems.at[k], device_id=peer, device_id_type=MESH).start()
        token[...] = jnp.zeros_like(token)

    lands = [lax.empty(s.shape, s.dtype) for s in land_shapes]
    bufs = list(arrays) + lands
    out = pl.pallas_call(
        body, name=name,
        out_shape=(pltpu.SemaphoreType.DMA((nk,)), pltpu.SemaphoreType.DMA((nk,)),
                   *[pltpu.HBM(b.shape, b.dtype) for b in bufs], jax.ShapeDtypeStruct((8, 128), F32)),
        in_specs=[HBM_SPEC] * len(bufs) + [pl.BlockSpec(memory_space=pl.ANY)] * n_after,
        out_specs=(SEM_SPEC, SEM_SPEC, *[HBM_SPEC] * len(bufs), pl.BlockSpec(memory_space=pltpu.VMEM)),
        input_output_aliases={a: 2 + a for a in range(len(bufs))},
        compiler_params=pltpu.CompilerParams(has_side_effects=DATAFLOW),
    )(*[pltpu.with_memory_space_constraint(b, pltpu.HBM) for b in bufs], *after)
    flight = (out[0], out[1], list(out[2:2 + n_in]), list(out[2 + n_in:2 + n_in + n_out]), remote)
    return flight, out[-1]


def exchange_wait(flight, after, name):
    send_sems, recv_sems, arrays, lands, remote = flight
    n_in, n_out = len(arrays), len(lands)
    after = list(after) if isinstance(after, (list, tuple)) else [after]

    def body(*refs):
        srcs, lnds = refs[:n_in], refs[n_in:n_in + n_out]
        s_sems, r_sems = refs[n_in + n_out], refs[n_in + n_out + 1]
        me = _coords()
        for k, (flip, ii, src_at, oi, dst_at) in enumerate(remote):
            peer = _peer(me, flip)
            copy = pltpu.make_async_remote_copy(
                src_ref=_at(srcs[ii], src_at(me, peer)), dst_ref=_at(lnds[oi], dst_at(peer)), send_sem=s_sems.at[k],
                recv_sem=r_sems.at[k], device_id=peer, device_id_type=MESH)
            copy.wait_send()
            copy.wait_recv()

    bufs = list(arrays) + list(lands)
    out = pl.pallas_call(
        body, name=name,
        out_shape=tuple(pltpu.HBM(b.shape, b.dtype) for b in bufs),
        in_specs=[HBM_SPEC] * len(bufs) + [SEM_SPEC, SEM_SPEC] + [pl.BlockSpec(memory_space=pl.ANY)] * len(after),
        out_specs=tuple([HBM_SPEC] * len(bufs)),
        input_output_aliases={a: a for a in range(len(bufs))},
        compiler_params=pltpu.CompilerParams(has_side_effects=DATAFLOW),
    )(*bufs, send_sems, recv_sems, *after)
    return list(out[:n_in]), list(out[n_in:])


def _half_tile(h, cd):
    return h if h * cd * 4 <= (1 << 20) else math.gcd(512, h)


def pair_add(g, got, core, out_dtype, name):
    _, _, h, cd = g.shape
    th = _half_tile(h, cd)

    def body(c_ref, g_ref, got_ref, o_ref):
        o_ref[0] = (g_ref[0, 0] + got_ref[0]).astype(o_ref.dtype)

    return pl.pallas_call(
        body, out_shape=jax.ShapeDtypeStruct((N_CHIP, h, cd), out_dtype),
        grid_spec=pltpu.PrefetchScalarGridSpec(
            num_scalar_prefetch=1, grid=(N_CHIP, h // th),
            in_specs=[pl.BlockSpec((1, 1, th, cd), lambda q, i, c: (q, c[0], i, 0)),
                      pl.BlockSpec((1, th, cd), lambda q, i, c: (q, i, 0))],
            out_specs=pl.BlockSpec((1, th, cd), lambda q, i, c: (q, i, 0))),
        compiler_params=_params(("parallel", "parallel"), VMEM_LIMIT), name=name)(core, g, got)


def sum_chips(parts, sums, place, name):
    _, h, cd = parts.shape
    th = _half_tile(h, cd)

    def body(pc_ref, p_ref, own_ref, o_ref):
        acc = None
        for q in range(N_CHIP):
            term = jnp.where(pc_ref[1] == q, own_ref[0], p_ref[q]).astype(F32)
            acc = term if acc is None else acc + term
        o_ref[0] = acc

    return pl.pallas_call(
        body, out_shape=jax.ShapeDtypeStruct((2, h, cd), F32),
        grid_spec=pltpu.PrefetchScalarGridSpec(
            num_scalar_prefetch=1, grid=(h // th,),
            in_specs=[pl.BlockSpec((N_CHIP, th, cd), lambda i, pc: (0, i, 0)),
                      pl.BlockSpec((1, th, cd), lambda i, pc: (pc[1], i, 0))],
            out_specs=pl.BlockSpec((1, th, cd), lambda i, pc: (pc[0], i, 0))),
        compiler_params=_params(("parallel",), VMEM_LIMIT), name=name)(place, parts, sums)


def to_segments(a, n_ctx):
    def one(p):
        n = p.shape[0]
        return p.reshape(N_SEG, n // N_SEG, -1).transpose(1, 0, 2).reshape(n, -1)
    return jnp.concatenate([one(a[:n_ctx]), one(a[n_ctx:])], axis=0) if n_ctx else one(a)


def rows_to_segments(a, n_ctx, name):
    n, d = a.shape
    tj = n_ctx // N_SEG
    per_seg = (n - n_ctx) // N_SEG // tj
    assert n_ctx % N_SEG == 0 and (n - n_ctx) % (N_SEG * tj) == 0

    def body(*refs):
        out_ref, slabs = refs[N_SEG:]
        for c in range(d // HEAD_LANES):
            cols = slice(c * HEAD_LANES, (c + 1) * HEAD_LANES)
            for seg in range(N_SEG):
                slabs[c, pl.ds(seg, tj, stride=N_SEG), :] = refs[seg][:, cols]
            out_ref[:, cols] = slabs[c]

    def seg_spec(seg):
        return pl.BlockSpec((tj, d), lambda i: (jnp.where(i == 0, seg, N_SEG + seg * per_seg + i - 1), 0))

    return pl.pallas_call(
        body, out_shape=jax.ShapeDtypeStruct((n, d), a.dtype), grid=(1 + per_seg,),
        in_specs=[seg_spec(seg) for seg in range(N_SEG)], out_specs=pl.BlockSpec((N_SEG * tj, d), lambda i: (i, 0)),
        scratch_shapes=[pltpu.VMEM((d // HEAD_LANES, N_SEG * tj, HEAD_LANES), a.dtype)],
        compiler_params=_params(("parallel",), VMEM_LIMIT), name=name)(*[a] * N_SEG)


def rope_tables(n_ctx, n_lat):
    f32 = np.float32
    rows = n_lat // GRID_W
    row = np.repeat(np.arange(rows), GRID_W).astype(f32)
    col = np.tile(np.arange(GRID_W), rows).astype(f32)
    d = QK_ROPE_DIM // 2
    inv = (f32(1.0) / np.power(f32(ROPE_THETA), np.arange(0, d, 2, dtype=f32) / f32(d))).astype(f32)
    ang = np.concatenate([row[:, None] * inv[None, :], col[:, None] * inv[None, :]], axis=1).astype(f32)
    cos = np.concatenate([np.ones((n_ctx, d), f32), np.cos(ang)], axis=0)
    sin = np.concatenate([np.zeros((n_ctx, d), f32), np.sin(ang)], axis=0)
    q = QK_ROPE_DIM // 4
    T = n_ctx + n_lat
    ones, zeros = np.ones((T, QK_NOPE_DIM), f32), np.zeros((T, QK_NOPE_DIM), f32)
    tail, z8 = np.zeros((T, HEAD_LANES - QK_DIM), f32), np.zeros((T, q), f32)
    cr, cc, sr, sc = cos[:, :q], cos[:, q:], sin[:, :q], sin[:, q:]
    cos_t = np.concatenate([ones, cr, cr, cc, cc, tail], axis=1)
    sin_next = np.concatenate([zeros, -sr, z8, -sc, z8, tail], axis=1)
    sin_prev = np.concatenate([zeros, z8, sr, z8, sc, tail], axis=1)
    return tuple(jnp.asarray(t, F32) for t in (cos_t, sin_next, sin_prev))


def pad_heads(w, used):
    k = w.shape[0]
    return jnp.pad(w.reshape(k, MLA_HEADS, used), ((0, 0), (0, 0), (0, HEAD_LANES - used))).reshape(k, -1)


def unpad_heads(w, used):
    k = w.shape[0]
    return w.reshape(k, MLA_HEADS, HEAD_LANES)[:, :, :used].reshape(k, MLA_HEADS * used)


def rotary_spread():
    lane = np.arange(MLA_HEADS * HEAD_LANES) % HEAD_LANES
    return jnp.asarray(lane[None, :] == (QK_NOPE_DIM + np.arange(QK_ROPE_DIM))[:, None], BF16)


def s5_discretise(a_re, a_im, log_step, b_re, b_im):
    dt = jnp.exp(log_step)[:, None]
    mag = jnp.exp(a_re * dt)
    lb_re = mag * jnp.cos(a_im * dt)
    lb_im = mag * jnp.sin(a_im * dt)
    den = a_re * a_re + a_im * a_im
    nr = lb_re - 1.0
    f_re = ((nr * a_re + lb_im * a_im) / den)[:, None, :]
    f_im = ((lb_im * a_re - nr * a_im) / den)[:, None, :]
    return lb_re, lb_im, f_re * b_re - f_im * b_im, f_re * b_im + f_im * b_re


def s5_block_weights(lb_re, lb_im, bb_re, bb_im, c_re, c_im):
    eye = jnp.eye(GROUPS_PER_BLOCK, dtype=F32)
    lam = jnp.stack([lb_re.reshape(1, S5_LANES), lb_im.reshape(1, S5_LANES)])

    def b_blocks(bb):
        t = bb.reshape(N_BLOCKS, GROUPS_PER_BLOCK, S5_GROUP, S5_STATE)
        return jnp.einsum("bgcp,gh->bgchp", t, eye).reshape(N_BLOCKS, BLK_CH, BLK_ST).astype(BF16)

    def c_blocks(cc):
        t = cc.reshape(N_BLOCKS, GROUPS_PER_BLOCK, S5_GROUP, S5_STATE)
        return jnp.einsum("bgcp,gh->bgphc", t, eye).reshape(N_BLOCKS, BLK_ST, BLK_CH).astype(BF16)

    return lam, b_blocks(bb_re), b_blocks(bb_im), c_blocks(c_re), c_blocks(c_im)


def b_block_diag(db):
    t = db.reshape(N_BLOCKS, GROUPS_PER_BLOCK, S5_GROUP, GROUPS_PER_BLOCK, S5_STATE)
    return jnp.einsum("bgchp,gh->bgcp", t, jnp.eye(GROUPS_PER_BLOCK, dtype=F32)).reshape(S5_GROUPS, S5_GROUP, S5_STATE)


def c_block_diag(dc):
    t = dc.reshape(N_BLOCKS, GROUPS_PER_BLOCK, S5_STATE, GROUPS_PER_BLOCK, S5_GROUP)
    return jnp.einsum("bgphc,gh->bgcp", t, jnp.eye(GROUPS_PER_BLOCK, dtype=F32)).reshape(S5_GROUPS, S5_GROUP, S5_STATE)


def conj(a):
    return jnp.stack([a[0], -a[1]])


def _narrow(shape):
    return len(shape) >= 2 and shape[-1] < min(HEAD_LANES, shape[-2])


def _stored(a):
    return jnp.swapaxes(a, -1, -2) if _narrow(a.shape) else a


def _stored_shape(shape):
    return tuple(shape[:-2]) + (shape[-1], shape[-2]) if _narrow(shape) else tuple(shape)


def _from_stored(a, shape):
    return jnp.swapaxes(a, -1, -2) if _narrow(shape) else a


PACK_TILE = 16 * 128


def pack_flat(parts, dtype, multiple=PACK_TILE):
    flat = [p.reshape(-1).astype(dtype) for p in parts]
    sizes = [f.shape[0] for f in flat]
    total = sum(sizes)
    pad = (-total) % multiple
    if pad:
        flat.append(jnp.zeros((pad,), dtype))
    offs = np.cumsum([0] + sizes)[:-1].tolist()
    return jnp.concatenate(flat).reshape(-1, 128), offs


def unpack_flat(buf, offs, shapes):
    flat = buf.reshape(-1)
    return [flat[o:o + int(np.prod(s))].reshape(s) for o, s in zip(offs, shapes)]


def s5_forward(p1, n_ctx, dirs):
    saved = []
    y = None
    ctx_rows, lat_rows = (0, n_ctx), (n_ctx, p1.shape[0] - n_ctx)
    zeros_tile = jnp.zeros((2, N_SEG, S5_LANES), F32)
    zeros_row = jnp.zeros((2, 1, S5_LANES), F32)
    for k, (lam, b_re, b_im, c_re, c_im) in enumerate(dirs):
        rev = k == 1
        last = 0 if rev else N_SEG - 1
        _, _, fin = s5_scan(p1, b_re, b_im, lam, zeros_tile, reverse=rev, rows=ctx_rows, name=f"s5_ctx_finals{k}")
        carry_c = s5_chain(fin, zeros_row, lam, n_ctx // N_SEG, rev, name=f"s5_ctx_chain{k}")
        _, ck_c, fin_c = s5_scan(p1, b_re, b_im, lam, carry_c, reverse=rev, want_ckpt=True, rows=ctx_rows,
                                 name=f"s5_ctx_scan{k}")
        s0 = fin_c[:, last:last + 1, :]
        _, _, fin = s5_scan(p1, b_re, b_im, lam, zeros_tile, reverse=rev, rows=lat_rows, name=f"s5_lat_finals{k}")
        carry_l = s5_chain(fin, s0, lam, lat_rows[1] // N_SEG, rev, name=f"s5_lat_chain{k}")
        y, ck_l, _ = s5_scan(p1, b_re, b_im, lam, carry_l, reverse=rev, c_re=c_re, c_im=c_im, add=y,
                             want_ckpt=True, rows=lat_rows, name=f"s5_lat_scan{k}")
        saved.append((ck_c, ck_l))
    return y, saved


def s5_backward(dy_l, du_extra_l, p1, n_ctx, dirs, saved):
    n_lat = p1.shape[0] - n_ctx
    zeros_tile = jnp.zeros((2, N_SEG, S5_LANES), F32)
    zeros_row = jnp.zeros((2, 1, S5_LANES), F32)
    du_l, du_c = du_extra_l, None
    grads = []
    for k, (lam, b_re, b_im, c_re, c_im) in enumerate(dirs):
        rev = k == 1
        lam_c = conj(lam)
        ck_c, ck_l = saved[k]
        first = N_SEG - 1 if rev else 0
        _, _, fin = s5_scan(dy_l, c_re, c_im, lam_c, zeros_tile, reverse=not rev, adjoint=True,
                            name=f"s5_lat_adj_finals{k}")
        carry = s5_chain(fin, zeros_row, lam_c, n_lat // N_SEG, not rev, name=f"s5_lat_adj_chain{k}")
        whole = k == len(dirs) - 1
        du_l, dlam_l, dbr_l, dbi_l, dcr_l, dci_l, fin_a = s5_grads(
            dy_l, p1, ck_l, b_re, b_im, c_re, c_im, lam, carry, reverse=rev, add=du_l, u_off=n_ctx,
            du_rows=(p1.shape[0], n_ctx) if whole else None, name=f"s5_lat_grads{k}")
        g0 = fin_a[:, first:first + 1, :]
        carry = s5_chain(zeros_tile, g0, lam_c, n_ctx // N_SEG, not rev, name=f"s5_ctx_adj_chain{k}")
        du_c, dlam_c, dbr_c, dbi_c, _, _, _ = s5_grads(
            None, p1, ck_c, b_re, b_im, c_re, c_im, lam, carry, reverse=rev, add=du_c, n_rows=n_ctx,
            du_rows=(p1.shape[0], 0) if whole else None, du_into=du_l if whole else None, name=f"s5_ctx_grads{k}")
        dlam = jnp.sum(dlam_l + dlam_c, axis=1)
        grads.append((dlam, b_block_diag(dbr_l + dbr_c), b_block_diag(dbi_l + dbi_c),
                      c_block_diag(dcr_l), c_block_diag(dci_l)))
    return du_c, grads


def local_step(x, ctx, target, mod, w, late=None, reducer=None):
    L, Lc = x.shape[0], ctx.shape[0]
    T = L + Lc
    assert L % Lc == 0 and Lc % (2 * N_SEG) == 0 and L % GRID_W == 0
    D = D_MODEL
    X0 = (ctx, x)

    def mod_of(i, j):
        return mod[i, :, j, :][:, None, :]

    def vec(v):
        return v.reshape(1, 1, -1).astype(F32)

    g0 = vec(w["norm_g"][0])
    H0, p0 = norm_proj(X0, g0, mod_of(0, 1), mod_of(0, 0), w["mla_w_in"], Lc, "l0_norm_in")
    cq = Rows(p0, Q_LORA_RANK, col_blk=D // Q_LORA_RANK)
    ckv = Rows(p0, KV_LORA_RANK, col_blk=(D + Q_LORA_RANK) // KV_LORA_RANK)
    kr = Rows(p0, HEAD_LANES, col_blk=(D + Q_LORA_RANK + KV_LORA_RANK) // HEAD_LANES)
    qng, kvng = vec(w["mla_q_norm"]), vec(w["mla_kv_norm"])
    tabs = rope_tables(Lc, L)
    spread = jnp.pad(rotary_spread(), ((0, HEAD_LANES - QK_ROPE_DIM), (0, 0)))
    if late is not None:
        w = {**w, **late["qkv"](p0)}
    w_uq_p = pad_heads(w["mla_w_uq"], QK_DIM)
    w_ukv3 = w["mla_w_ukv"].reshape(KV_LORA_RANK, MLA_HEADS, QK_NOPE_DIM + V_HEAD_DIM)
    w_kn_p = pad_heads(w_ukv3[:, :, :QK_NOPE_DIM].reshape(KV_LORA_RANK, -1), QK_NOPE_DIM)
    w_v = w_ukv3[:, :, QK_NOPE_DIM:].reshape(KV_LORA_RANK, -1)
    qb, qn = q_heads(cq, qng, w_uq_p, tabs)
    kb, vb, kvn = kv_heads(ckv, kvng, w_kn_p, w_v, kr, spread, tabs)
    o, lse = attn_fwd(qb, kb, vb, Lc)
    if late is not None:
        w = {**w, **late["out"](o)}
    X1, og, out0 = mla_post_fwd(o, p0, X0, mod_of(0, 2), w["mla_w_out"], Lc)

    if late is not None:
        w = {**w, **late["l1"](X1)}
    X1p = rows_to_segments(X1, Lc, "l1_to_segments")
    tgt_p = to_segments(target, 0)
    g1 = vec(w["norm_g"][1])
    H1, p1 = norm_proj(X1p, g1, mod_of(1, 1), mod_of(1, 0), w["s5_w_in"], Lc, "l1_norm_in")
    disc_fn = lambda *a: tuple(zip(*[s5_discretise(a[0][k], a[1][k], a[2][k], a[3][k], a[4][k]) for k in range(2)]))
    disc, disc_vjp = jax.vjp(disc_fn, w["s5_a_re"], w["s5_a_im"], w["s5_log_step"], _stored(w["s5_b_re"]),
                             _stored(w["s5_b_im"]))
    dirs = [s5_block_weights(disc[0][k], disc[1][k], disc[2][k], disc[3][k], w["s5_c_re"][k], w["s5_c_im"][k])
            for k in range(2)]
    y_ssm, s5_saved = s5_forward(p1, Lc, dirs)

    row = lambda v: v.reshape(1, D).astype(F32)
    (lvec, dX2, d_yssm, d_u_act, d_z1, d_fg, d_gt1, d_bg, d_d, gw_glu, gw_out) = s5_tail(
        y_ssm, p1, X1p, tgt_p, Lc, row(w["s5_d"]), row(w["s5_b_glu"]), mod[1, 1:2, 2, :], row(w["final_g"]),
        w["s5_w_glu"], w["s5_w_out"])
    loss = jnp.sum(lvec)
    gw = {"final_g": d_fg.reshape(D), "s5_b_glu": d_bg.reshape(D), "s5_d": d_d.reshape(D),
          "s5_w_glu": gw_glu, "s5_w_out": gw_out}
    dmod = {}

    du_p, s5_g = s5_backward(d_yssm, d_u_act, p1, Lc, dirs, s5_saved)
    d_disc = tuple(tuple(s5_g[k][j - 1].reshape(disc[j][k].shape) if j >= 2 else
                         s5_g[k][0][j].reshape(disc[j][k].shape) for k in range(2)) for j in range(4))
    gw["s5_a_re"], gw["s5_a_im"], gw["s5_log_step"], d_bt_re, d_bt_im = disc_vjp(d_disc)
    gw["s5_b_re"], gw["s5_b_im"] = jnp.swapaxes(d_bt_re, -1, -2), jnp.swapaxes(d_bt_im, -1, -2)
    gw["s5_c_re"] = jnp.stack([s5_g[0][3], s5_g[1][3]])
    gw["s5_c_im"] = jnp.stack([s5_g[0][4], s5_g[1][4]])
    gw["s5_w_in"] = mm_tn(H1, du_p, name="l1_in_dw", b_more=d_z1)
    if reducer is not None:
        g1 = g1 + reducer["l1"][0]({n: gw.pop(n) for n in LAYER1_MATS})[0, 0]
    d_X1, d_g1, d_sc1, d_sh1 = norm_proj_bwd([(du_p, 0, 0), (d_z1, D, Lc)], w["s5_w_in"], X1p, g1, mod_of(1, 1),
                                             mod_of(1, 0), dX2, Lc, "l1_norm_in_bwd", dx_from_segments=True)
    d_gt1_full = jnp.concatenate([jnp.zeros((1, 1, D), F32), d_gt1[None]], axis=0)
    dmod[1] = (d_sh1, d_sc1, d_gt1_full)

    d_o, d_z0, d_gt0, gw["mla_w_out"] = mla_post_bwd(d_X1, out0, og, o, p0, mod_of(0, 2), w["mla_w_out"], Lc)
    if reducer is not None:
        started = reducer["l1"][1](d_o)[0, 0] + reducer["out"][0]({"mla_w_out": gw.pop("mla_w_out")})[0, 0]
        tabs = (tabs[0] + started,) + tabs[1:]
    d_q, dk_p, d_v = attn_bwd(qb, kb, vb, o, d_o, lse, tabs, Lc)
    if reducer is not None:
        tabs = (tabs[0] + reducer["out"][1](d_q)[0, 0],) + tabs[1:]
    d_k, d_kr = heads_unrope(dk_p, tabs, spread,
                             name="l0_k_unrope")
    d_qn = mm_nt(d_q, w_uq_p, name="l0_uq_dx")
    gw["mla_w_uq"] = unpad_heads(mm_tn(qn, d_q, name="l0_uq_dw"), QK_DIM)
    d_kvn = mm_nt(d_k, w_kn_p, name="l0_ukv_dx", more=(d_v, w_v))
    dw_kn = unpad_heads(mm_tn(kvn, d_k, name="l0_ukn_dw"), QK_NOPE_DIM).reshape(KV_LORA_RANK, MLA_HEADS, QK_NOPE_DIM)
    dw_v = mm_tn(kvn, d_v, name="l0_uv_dw").reshape(KV_LORA_RANK, MLA_HEADS, V_HEAD_DIM)
    gw["mla_w_ukv"] = jnp.concatenate([dw_kn, dw_v], axis=-1).reshape(KV_LORA_RANK, -1)
    d_cq, d_qng = rowwise_bwd(f_rms, [cq], [qng], [d_qn], [0], [0], T, 0, "l0_qnorm_bwd")
    d_ckv, d_kvng = rowwise_bwd(f_rms, [ckv], [kvng], [d_kvn], [0], [0], T, 0, "l0_kvnorm_bwd")
    gw["mla_q_norm"] = d_qng.reshape(-1)
    gw["mla_kv_norm"] = d_kvng.reshape(-1)
    o_cq, o_ckv = D, D + Q_LORA_RANK
    o_kr = o_ckv + KV_LORA_RANK
    d_head = jnp.concatenate([d_cq, d_ckv, d_kr], axis=1)
    gw["mla_w_in"] = jnp.concatenate([mm_tn(H0, d_head, name="l0_in_dw_head")[:, :P0_HEAD],
                                      mm_tn(H0, d_z0, name="l0_in_dw_z")], axis=1)
    dx, d_g0, d_sc0, d_sh0 = norm_proj_bwd(
        [(d_z0, 0, 0), (d_cq, o_cq, 0), (d_ckv, o_ckv, 0), (d_kr, o_kr, 0)], w["mla_w_in"], X0, g0, mod_of(0, 1),
        mod_of(0, 0), d_X1, Lc, "l0_norm_in_bwd", latent_dx_only=True)
    dmod[0] = (d_sh0, d_sc0, d_gt0)
    gw["norm_g"] = jnp.stack([d_g0.reshape(D), d_g1.reshape(D)])
    dmod_arr = jnp.stack([jnp.stack([dmod[i][j][:, 0, :] for j in range(3)], axis=1) for i in range(2)])
    ready = {**reducer["l1"][2](dx), **reducer["out"][2](dx)} if reducer is not None else {}
    return loss, dx, dmod_arr, gw, ready


SHARDED = {
    "mla_w_in": 1, "mla_w_uq": 1, "mla_w_ukv": 1, "mla_w_out": 0,
    "s5_w_in": 1, "s5_w_glu": 0, "s5_w_out": 0, "s5_d": 0, "s5_b_glu": 0,
}
SHARDED_MATS = ["mla_w_in", "mla_w_uq", "mla_w_ukv", "mla_w_out", "s5_w_in", "s5_w_glu", "s5_w_out"]
SHARDED_VECS = ["s5_d", "s5_b_glu"]
REPLICATED = ["norm_g", "mla_q_norm", "mla_kv_norm", "s5_a_re", "s5_a_im", "s5_log_step", "s5_b_re", "s5_b_im",
              "s5_c_re", "s5_c_im", "final_g"]
WEIGHT_ORDER = ["c_ctx", "ada_w", "ada_b", "norm_g", "mla_w_in", "mla_q_norm", "mla_w_uq", "mla_kv_norm", "mla_w_ukv",
                "mla_w_out", "s5_w_in", "s5_a_re", "s5_a_im", "s5_log_step", "s5_b_re", "s5_b_im", "s5_c_re", "s5_c_im",
                "s5_d", "s5_w_glu", "s5_b_glu", "s5_w_out", "final_g"]


P0_HEAD = Q_LORA_RANK + KV_LORA_RANK + QK_ROPE_DIM


P0_WIDTH = 1536


def w_in_to_kernel_order(w):
    pad = jnp.zeros((w.shape[0], P0_WIDTH - w.shape[1]), w.dtype)
    return jnp.concatenate([w[:, P0_HEAD:], w[:, :P0_HEAD], pad], axis=1)


LAYER0_MATS = ["mla_w_in", "mla_w_uq", "mla_w_ukv", "mla_w_out"]
LAYER1_MATS = ["s5_w_in", "s5_w_glu", "s5_w_out"]


def _whole_matrices(names, own_blocks, gathered):
    chip = _chip_index(_coords())
    full = {}
    for n, own, o in zip(names, own_blocks, gathered):
        slot = lax.broadcasted_iota(jnp.int32, (N_CHIP, 1, 1), 0)
        o = jnp.where(slot == chip, own[None], o.reshape((N_CHIP,) + own.shape))
        full[n] = o.reshape(-1, o.shape[-1]) if SHARDED[n] == 0 else o.transpose(1, 0, 2).reshape(o.shape[1], -1)
    return full


FIRST_MATS = ["mla_w_in"]
LATER_GROUPS = {"qkv": ["mla_w_uq", "mla_w_ukv"], "out": ["mla_w_out"], "l1": LAYER1_MATS}


def gather_weights(ws):
    mats = [ws[n].astype(BF16) for n in FIRST_MATS]
    full = _whole_matrices(FIRST_MATS, mats, gather_halves(mats, "gather_weights"))
    full["mla_w_in"] = w_in_to_kernel_order(full["mla_w_in"])
    return full


def gather_weights_behind(ws, after):
    token, finish = 0.0, {}
    for group, names in LATER_GROUPS.items():
        mats = [ws[n].astype(BF16) for n in names]
        flight, tok = exchange_start(
            mats, [jax.ShapeDtypeStruct((N_CHIP,) + m.shape, m.dtype) for m in mats],
            [(f, a, lambda me, peer: None, a, lambda s: (_chip_index(s),))
             for a in range(len(mats)) for f in CHIP_FLIPS], f"gather_{group}_start", after=after)
        after = [tok]
        token = token + tok[0, 0]

        def finish_group(after_work, group=group, names=names, flight=flight):
            own, got = exchange_wait(flight, after_work, f"gather_{group}_wait")
            return _whole_matrices(names, own, got)

        finish[group] = finish_group
    return token, finish


def _grad_slots(gw, names):
    slots = []
    for n in names:
        g = gw[n]
        if SHARDED[n] == 0:
            slots.append(g.reshape(N_CHIP, 2, g.shape[0] // (2 * N_CHIP), g.shape[1]))
        else:
            k, n4 = g.shape
            slots.append(g.reshape(k, N_CHIP, n4 // N_CHIP).transpose(1, 0, 2)
                         .reshape(N_CHIP, 2, k // 2, n4 // N_CHIP))
    return slots


def _to_sibling_half(count):
    return [(CORE_FLIP, i, lambda me, peer: (slice(None), 1 - me[2]), i, lambda s: None) for i in range(count)]


def _to_chips(count):
    return [(f, i, lambda me, peer: (_chip_index(peer),), i, lambda s: (_chip_index(s),))
            for i in range(count) for f in CHIP_FLIPS]


def _place():
    me = _coords()
    return jnp.stack([me[2], _chip_index(me)]).astype(jnp.int32)


def reduce_behind(names, tag):
    state = {}
    count = len(names)

    def begin(gw):
        slots = _grad_slots(gw, names)
        lands = [jax.ShapeDtypeStruct((N_CHIP,) + s.shape[2:], F32) for s in slots]
        state["in"], token = exchange_start(slots, lands, _to_sibling_half(count), f"grads_{tag}_swap_in_start")
        return token

    def middle(after):
        slots, got = exchange_wait(state["in"], after, f"grads_{tag}_swap_in_wait")
        place = _place()
        sums = [pair_add(s, g, place[:1], BF16, f"grads_pair_{n}") for n, s, g in zip(names, slots, got)]
        lands = [jax.ShapeDtypeStruct(s.shape, s.dtype) for s in sums]
        state["out"], token = exchange_start(sums, lands, _to_chips(count), f"grads_{tag}_scatter_start")
        return token

    def end(after):
        sums, parts = exchange_wait(state["out"], after, f"grads_{tag}_scatter_wait")
        place = _place()
        return {n: sum_chips(p, s, place, f"grads_sum_{n}") for n, p, s in zip(names, parts, sums)}

    return begin, middle, end


def reduce_gradients(gw, ready_halves, also=None):
    me = _coords()
    place = _place()
    mat_names = [n for n in SHARDED_MATS if n not in ready_halves]
    slots = dict(zip(mat_names, _grad_slots(gw, mat_names)))
    gw = {**gw, **(also or {})}
    small_names = REPLICATED + SHARDED_VECS + list(also or {})
    small, small_offs = pack_flat([_stored(gw[n]).astype(F32) for n in small_names], F32, N_CHIP * 32 * 128)
    slots["small"] = small.reshape(N_CHIP, 2, -1, 128)
    names = list(slots)
    count = len(names)
    got = exchange([slots[n] for n in names],
                   [jax.ShapeDtypeStruct((N_CHIP,) + slots[n].shape[2:], F32) for n in names],
                   _to_sibling_half(count), [], "grads_swap_in")
    sums = [pair_add(slots[n], g, place[:1], F32 if n == "small" else BF16, f"grads_pair_{n}")
            for n, g in zip(names, got)]
    parts = exchange(sums, [jax.ShapeDtypeStruct(s.shape, s.dtype) for s in sums], _to_chips(count), [],
                     "grads_scatter")
    halves = {n: sum_chips(p, s, place, f"grads_sum_{n}") for n, p, s in zip(names, parts, sums)}
    halves.update(ready_halves)
    all_names = list(halves)
    fulls = exchange(
        [halves[n] for n in all_names], [jax.ShapeDtypeStruct(halves[n].shape, F32) for n in all_names],
        [(CORE_FLIP, i, lambda me, peer: (me[2],), i, lambda s: (s[2],)) for i in range(len(all_names))], [],
        "grads_swap_out", aliases={i: i for i in range(len(all_names))})
    out = {n: f.reshape(-1, f.shape[-1]) for n, f in zip(all_names, fulls)}
    quarter = out.pop("small")
    gather, token = exchange_start(
        [quarter], [jax.ShapeDtypeStruct((N_CHIP,) + quarter.shape, F32)],
        [(f, 0, lambda me, peer: None, 0, lambda s: (_chip_index(s),)) for f in CHIP_FLIPS],
        "grads_gather_small_start")

    def finish_small(after):
        (own,), (got_small,) = exchange_wait(gather, after, "grads_gather_small_wait")
        slot = lax.broadcasted_iota(jnp.int32, (N_CHIP, 1, 1), 0)
        small_all = jnp.where(slot == _chip_index(me), own[None], got_small)
        vals = unpack_flat(small_all, small_offs, [_stored_shape(gw[n].shape) for n in small_names])
        res = {}
        for n, v in zip(small_names, vals):
            v = _from_stored(v, gw[n].shape)
            if n in SHARDED_VECS:
                size = v.shape[0] // N_CHIP
                v = lax.dynamic_slice_in_dim(v, _chip_index(me) * size, size)
            res[n] = v
        return res

    return out, finish_small, token


def kernel(x, c, ctx, c_ctx, ada_w, ada_b, norm_g, mla_w_in, mla_q_norm, mla_w_uq, mla_kv_norm, mla_w_ukv, mla_w_out, s5_w_in, s5_a_re, s5_a_im, s5_log_step, s5_b_re, s5_b_im, s5_c_re, s5_c_im, s5_d, s5_w_glu, s5_b_glu, s5_w_out, final_g, loss_target, m_c_ctx, m_ada_w, m_ada_b, m_norm_g, m_mla_w_in, m_mla_q_norm, m_mla_w_uq, m_mla_kv_norm, m_mla_w_ukv, m_mla_w_out, m_s5_w_in, m_s5_a_re, m_s5_a_im, m_s5_log_step, m_s5_b_re, m_s5_b_im, m_s5_c_re, m_s5_c_im, m_s5_d, m_s5_w_glu, m_s5_b_glu, m_s5_w_out, m_final_g, v_c_ctx, v_ada_w, v_ada_b, v_norm_g, v_mla_w_in, v_mla_q_norm, v_mla_w_uq, v_mla_kv_norm, v_mla_w_ukv, v_mla_w_out, v_s5_w_in, v_s5_a_re, v_s5_a_im, v_s5_log_step, v_s5_b_re, v_s5_b_im, v_s5_c_re, v_s5_c_im, v_s5_d, v_s5_w_glu, v_s5_b_glu, v_s5_w_out, v_final_g):
    args = dict(locals())
    weights = {n: args[n] for n in WEIGHT_ORDER}
    D = D_MODEL
    xi, yi, ci = _coords()
    chip = 2 * xi + yi
    me = 4 * xi + 2 * yi + ci
    n_col = ada_w.shape[2]

    c_all = allgather_devices(jnp.pad(c, ((0, 7), (0, 0))), "gather_c")[:, 0, :]
    cond = jnp.concatenate([c_all, jnp.broadcast_to(c_ctx[None], (8, D))], axis=0)
    (s_cond,) = rowwise_fwd(lambda v: (_silu(v),), [cond], [], [D], [F32], 16, 0, "cond_silu")
    ada_rows = ada_w.reshape(2 * D, n_col)
    mod_cols = jnp.stack([mm_nn(s_cond, ada_rows, name=f"mod_proj{i}", b_blk=i) for i in range(2)])
    vec_tiles = [jnp.pad(weights[n][0].reshape(-1, 128), ((0, 6), (0, 0))) for n in SHARDED_VECS]
    mod_all, *vec_all = allgather_chips([mod_cols] + vec_tiles, "gather_mod")
    mod_all = mod_all.transpose(1, 2, 0, 3).reshape(2, 16, 3 * D) + ada_b[:, None, :]
    mine = lax.broadcasted_iota(jnp.int32, (1, 16, 1), 1) == me
    mod_l = jnp.sum(jnp.where(mine, mod_all, 0.0), axis=1)
    mod_c = mod_all[:, 8, :]
    mod = jnp.stack([mod_c.reshape(2, 3, D), mod_l.reshape(2, 3, D)], axis=1)

    w = gather_weights({n: weights[n][0] for n in FIRST_MATS})
    token, late = gather_weights_behind({n: weights[n][0] for n in SHARDED_MATS}, [w["mla_w_in"], mod])
    for n, v in zip(SHARDED_VECS, vec_all):
        w[n] = v[:, :2, :].reshape(-1)
    for n in ["norm_g", "final_g"]:
        w[n] = weights[n]
    for n in ["mla_q_norm", "mla_kv_norm", "s5_a_re", "s5_a_im", "s5_log_step", "s5_b_re", "s5_b_im",
              "s5_c_re", "s5_c_im"]:
        w[n] = weights[n][0]

    reducer = {"l1": reduce_behind(LAYER1_MATS, "l1"), "out": reduce_behind(["mla_w_out"], "out")}
    loss_me, dx, dmod, gw, ready = local_step(x[0], ctx[0], loss_target[0], mod + token, w, late,
                                              reducer)

    dmod_rows, loss_all = _gather([dmod.reshape(2, 2, 3 * D), jnp.broadcast_to(loss_me, (8, 128))],
                                  ALL_FLIPS, _dev_index, N_DEV, "gather_dmod")
    loss = functools.reduce(lambda s, d: s + loss_all[d, 0, 0], range(1, N_DEV), loss_all[0, 0, 0])
    dm = jnp.concatenate([dmod_rows[:, :, 1, :], dmod_rows[:, :, 0, :]], axis=0).transpose(1, 0, 2)
    g_ada_b = jnp.sum(dm, axis=1)
    dm_cols = lax.dynamic_slice_in_dim(dm, chip * n_col, n_col, axis=2)
    g_ada_w = jnp.stack([mm_tn(s_cond, dm_cols[i], name=f"mod_proj_dw{i}") for i in range(2)])
    dmc = jnp.sum(dm_cols[:, 8:, :], axis=1)
    dmc8 = jnp.broadcast_to(dmc[:, None, :], (2, 8, n_col))
    g_sc = (mm_nt(dmc8[0], ada_rows, name="mod_proj_dx0", b_rows=D, b_blk=0)[0]
            + mm_nt(dmc8[1], ada_rows, name="mod_proj_dx1", b_rows=D, b_blk=1)[0])
    g_silu_part = jnp.where(ci == 0, g_sc, 0.0)

    grads = {"ada_w": g_ada_w, "ada_b": g_ada_b}
    deltas, new_m, new_v = {}, {}, {}
    small = [n for n in WEIGHT_ORDER if weights[n].size < 50000]

    def update(n, after=None):
        shp = weights[n].shape
        rows = lambda a: _stored(a.reshape(shp)).reshape(-1, _stored_shape(shp)[-1])
        back = lambda a: _from_stored(a.reshape(_stored_shape(shp)), shp)
        d_, m_, v_ = adamw(rows(weights[n]), rows(grads[n]), rows(args["m_" + n]), rows(args["v_" + n]),
                           name=f"adamw_{n}", after=after)
        deltas[n], new_m[n], new_v[n] = back(d_), back(m_), back(v_)

    red, finish_small, small_started = reduce_gradients(gw, ready, {"silu_c_ctx": g_silu_part})
    update("ada_w", small_started)
    for n in SHARDED_MATS:
        grads[n] = red[n].reshape(weights[n].shape)
        update(n, small_started)
    red_small = finish_small([deltas[n] for n in ["ada_w"] + SHARDED_MATS])
    for n in REPLICATED + SHARDED_VECS:
        grads[n] = red_small[n].reshape(weights[n].shape)
    (g_c_ctx,) = rowwise_bwd(lambda v: (_silu(v),), [jnp.broadcast_to(c_ctx[None], (8, D))], [],
                             [jnp.broadcast_to(red_small["silu_c_ctx"][None], (8, D))], [0], [], 8, 0, "cond_silu_bwd")
    grads["c_ctx"] = g_c_ctx[0]
    for n in WEIGHT_ORDER:
        if n not in small and n not in deltas:
            update(n)
    packs = []
    offs = None
    for src in (weights, grads, {n: args["m_" + n] for n in small}, {n: args["v_" + n] for n in small}):
        buf, offs = pack_flat([src[n] for n in small], F32)
        packs.append(buf)
    outs = adamw(*packs, name="adamw_small")
    for res, dst in zip(outs, (deltas, new_m, new_v)):
        for n, val in zip(small, unpack_flat(res, offs, [weights[n].shape for n in small])):
            dst[n] = val

    return (loss, dx[None], *[grads[n] for n in WEIGHT_ORDER], *[deltas[n] for n in WEIGHT_ORDER],
            *[new_m[n] for n in WEIGHT_ORDER], *[new_v[n] for n in WEIGHT_ORDER])
```
